```python
import math
import jax, jax.numpy as jnp
from jax import lax
import numpy as np

D_MODEL = 1024
BATCH = 8
SEQ = 8192
DEPTH = 1

SSM_GROUP = 16
SSM_GROUPS = D_MODEL // 32
SSM_WIDTH = SSM_GROUP * SSM_GROUPS
SSM_STATE = 64
DT_MIN = 1e-3
DT_MAX = 1e-1
FOX_HEAD_DIM = 64
FOX_HEADS = D_MODEL // 128
FOX_WIDTH = FOX_HEADS * FOX_HEAD_DIM
Q_BLOCK = 128
MEM_LEN = 256
MEM_HEADS = 4
MEM_HEAD_DIM = 128
MEM_WIDTH = MEM_HEADS * MEM_HEAD_DIM
FFN_HIDDEN = -(-8 * D_MODEL // (3 * 256)) * 256
N_BRANCHES = 2
RMS_EPS = 1e-6
SPLIT_Q = SSM_WIDTH
SPLIT_K = SPLIT_Q + FOX_WIDTH
SPLIT_V = SPLIT_K + FOX_WIDTH
SPLIT_F = SPLIT_V + FOX_WIDTH
SPLIT_G = SPLIT_F + FOX_HEADS
IN_WIDTH = SPLIT_G + N_BRANCHES * D_MODEL

kernel_name = "hybrid_s5_fox_gated_block"


def rms_norm(x, gain):
    xf = x.astype(jnp.float32)
    y = xf * lax.rsqrt(jnp.mean(xf * xf, axis=-1, keepdims=True) + RMS_EPS)
    return (y * gain.astype(jnp.float32)).astype(x.dtype)


def _linear_recurrence_op(left, right):
    a1, b1 = left
    a2, b2 = right
    return a1 * a2, a2 * b1 + b2


def s5_ssm(u, lam_re, lam_im, log_dt, b_re, b_im, c_re, c_im, d_skip):
    bsz, seq, _ = u.shape
    f32 = jnp.float32
    uf = u.astype(f32).reshape(bsz, seq, SSM_GROUPS, SSM_GROUP)
    lam = lax.complex(lam_re.astype(f32), lam_im.astype(f32))
    dt = jnp.exp(log_dt.astype(f32))[:, None]
    lam_bar = jnp.exp(lam * dt)
    b = lax.complex(b_re.astype(f32), b_im.astype(f32))
    b_bar = ((lam_bar - 1.0) / lam)[..., None] * b
    c = lax.complex(c_re.astype(f32), c_im.astype(f32))
    bu = jnp.einsum('gpn,blgn->blgp', b_bar, uf.astype(jnp.complex64))
    a = jnp.broadcast_to(lam_bar, bu.shape)
    _, states = lax.associative_scan(_linear_recurrence_op, (a, bu), axis=1)
    y = jnp.einsum('gnp,blgp->blgn', c, states).real
    y = y + d_skip.astype(f32).reshape(SSM_GROUPS, SSM_GROUP) * uf
    return y.reshape(bsz, seq, SSM_WIDTH).astype(u.dtype)


def forgetting_attention(q, k, v, f_logit):
    bsz, seq, n_heads, head_dim = q.shape
    log_f = jax.nn.log_sigmoid(f_logit.astype(jnp.float32))
    cum = jnp.cumsum(log_f, axis=1).transpose(0, 2, 1)
    kh = k.transpose(0, 2, 1, 3)
    vh = v.transpose(0, 2, 1, 3)
    n_blocks = seq // Q_BLOCK
    q_blocks = q.transpose(0, 2, 1, 3).reshape(bsz, n_heads, n_blocks, Q_BLOCK, head_dim).transpose(2, 0, 1, 3, 4)
    c_blocks = cum.reshape(bsz, n_heads, n_blocks, Q_BLOCK).transpose(2, 0, 1, 3)
    starts = jnp.arange(n_blocks, dtype=jnp.int32) * Q_BLOCK
    key_pos = jnp.arange(seq, dtype=jnp.int32)
    scale = head_dim ** -0.5

    def one_block(args):
        qb, cb, start = args
        s = jnp.einsum('bhqd,bhkd->bhqk', qb, kh).astype(jnp.float32) * scale
        s = s + cb[..., :, None] - cum[..., None, :]
        q_pos = start + jnp.arange(Q_BLOCK, dtype=jnp.int32)
        causal = key_pos[None, :] <= q_pos[:, None]
        s = jnp.where(causal, s, -jnp.inf)
        p = jax.nn.softmax(s, axis=-1)
        return jnp.einsum('bhqk,bhkd->bhqd', p.astype(vh.dtype), vh)

    out = lax.map(one_block, (q_blocks, c_blocks, starts))
    return out.transpose(1, 0, 3, 2, 4).reshape(bsz, seq, n_heads * head_dim)


def memory_cross_attention(n, m, w_q, w_kv, w_o):
    bsz, seq, _ = n.shape
    mem_len = m.shape[1]
    q = (n @ w_q).reshape(bsz, seq, MEM_HEADS, MEM_HEAD_DIM)
    k, v = jnp.split(m @ w_kv, 2, axis=-1)
    k = k.reshape(bsz, mem_len, MEM_HEADS, MEM_HEAD_DIM)
    v = v.reshape(bsz, mem_len, MEM_HEADS, MEM_HEAD_DIM)
    s = jnp.einsum('blhd,bmhd->bhlm', q, k).astype(jnp.float32) * (MEM_HEAD_DIM ** -0.5)
    p = jax.nn.softmax(s, axis=-1)
    o = jnp.einsum('bhlm,bmhd->blhd', p.astype(v.dtype), v).reshape(bsz, seq, MEM_WIDTH)
    return o @ w_o


def _fwd_setup_inputs(seed: int = 0) -> dict:
    key = jax.random.key(seed)
    ks = jax.random.split(key, 32)
    f32 = jnp.float32
    L = DEPTH

    def nrm(k, shape, fan_in):
        return jax.random.normal(k, shape, f32) * (fan_in ** -0.5)

    def gain(k, shape):
        return 1.0 + 0.01 * jax.random.normal(k, shape, f32)

    n_idx = jnp.arange(SSM_STATE, dtype=f32)
    lam_re = -0.5 + 0.01 * jax.random.normal(ks[4], (L, SSM_GROUPS, SSM_STATE), f32)
    lam_im = math.pi * n_idx + 0.01 * jax.random.normal(ks[5], (L, SSM_GROUPS, SSM_STATE), f32)
    log_dt = jax.random.uniform(ks[6], (L, SSM_GROUPS), f32, math.log(DT_MIN), math.log(DT_MAX))
    return {
        "x": jax.random.normal(ks[0], (BATCH, SEQ, D_MODEL), f32),
        "mem": jax.random.normal(ks[1], (BATCH, MEM_LEN, D_MODEL), f32),
        "norm_mix": gain(ks[2], (L, D_MODEL)),
        "w_in": nrm(ks[3], (L, D_MODEL, IN_WIDTH), D_MODEL),
        "b_forget": jax.random.uniform(ks[7], (L, FOX_HEADS), f32, 1.0, 5.0),
        "lam_re": lam_re,
        "lam_im": lam_im,
        "log_dt": log_dt,
        "b_re": nrm(ks[8], (L, SSM_GROUPS, SSM_STATE, SSM_GROUP), 2 * SSM_GROUP),
        "b_im": nrm(ks[9], (L, SSM_GROUPS, SSM_STATE, SSM_GROUP), 2 * SSM_GROUP),
        "c_re": nrm(ks[10], (L, SSM_GROUPS, SSM_GROUP, SSM_STATE), SSM_STATE),
        "c_im": nrm(ks[11], (L, SSM_GROUPS, SSM_GROUP, SSM_STATE), SSM_STATE),
        "d_skip": jax.random.normal(ks[12], (L, SSM_WIDTH), f32),
        "w_glu": nrm(ks[13], (L, SSM_WIDTH, 2 * D_MODEL), SSM_WIDTH),
        "w_fox_o": nrm(ks[14], (L, FOX_WIDTH, D_MODEL), FOX_WIDTH),
        "w_mix_out": nrm(ks[15], (L, D_MODEL, D_MODEL), D_MODEL),
        "norm_mem_q": gain(ks[16], (L, D_MODEL)),
        "norm_mem_kv": gain(ks[17], (L, D_MODEL)),
        "w_mem_q": nrm(ks[18], (L, D_MODEL, MEM_WIDTH), D_MODEL),
        "w_mem_kv": nrm(ks[19], (L, D_MODEL, 2 * MEM_WIDTH), D_MODEL),
        "w_mem_o": nrm(ks[20], (L, MEM_WIDTH, D_MODEL), MEM_WIDTH),
        "norm_ffn": gain(ks[21], (L, D_MODEL)),
        "w_ffn_in": nrm(ks[22], (L, D_MODEL, 2 * FFN_HIDDEN), D_MODEL),
        "w_ffn_out": nrm(ks[23], (L, FFN_HIDDEN, D_MODEL), FFN_HIDDEN),
        "norm_final": gain(ks[24], (D_MODEL,)),
    }


def _fwd_reference(x, mem, norm_mix, w_in, b_forget, lam_re, lam_im, log_dt, b_re, b_im, c_re, c_im,
              d_skip, w_glu, w_fox_o, w_mix_out, norm_mem_q, norm_mem_kv, w_mem_q, w_mem_kv,
              w_mem_o, norm_ffn, w_ffn_in, w_ffn_out, norm_final):
    bsz, seq, _ = x.shape
    h = x
    for l in range(DEPTH):
        u = rms_norm(h, norm_mix[l])
        proj = u @ w_in[l]
        u_ssm, q, k, v, f_logit, gate_logits = jnp.split(
            proj, [SPLIT_Q, SPLIT_K, SPLIT_V, SPLIT_F, SPLIT_G], axis=-1)
        y_ssm = jax.nn.gelu(s5_ssm(u_ssm, lam_re[l], lam_im[l], log_dt[l], b_re[l], b_im[l],
                                   c_re[l], c_im[l], d_skip[l]))
        glu_a, glu_b = jnp.split(y_ssm @ w_glu[l], 2, axis=-1)
        out_a = glu_a * jax.nn.sigmoid(glu_b)
        att = forgetting_attention(
            q.reshape(bsz, seq, FOX_HEADS, FOX_HEAD_DIM),
            k.reshape(bsz, seq, FOX_HEADS, FOX_HEAD_DIM),
            v.reshape(bsz, seq, FOX_HEADS, FOX_HEAD_DIM),
            f_logit + b_forget[l])
        out_b = att @ w_fox_o[l]
        gate_a, gate_b = jnp.split(jax.nn.sigmoid(gate_logits), 2, axis=-1)
        h = h + (gate_a * out_a + gate_b * out_b) @ w_mix_out[l]
        h = h + memory_cross_attention(rms_norm(h, norm_mem_q[l]), rms_norm(mem, norm_mem_kv[l]),
                                       w_mem_q[l], w_mem_kv[l], w_mem_o[l])
        f_a, f_b = jnp.split(rms_norm(h, norm_ffn[l]) @ w_ffn_in[l], 2, axis=-1)
        h = h + (jax.nn.silu(f_a) * f_b) @ w_ffn_out[l]
    return rms_norm(h, norm_final)


import jax as _jax
import jax.numpy as _jnp

TWIN_FORMAT = 'train_step'
FWD_PARAMS = ['x', 'mem', 'norm_mix', 'w_in', 'b_forget', 'lam_re', 'lam_im', 'log_dt', 'b_re', 'b_im', 'c_re', 'c_im', 'd_skip', 'w_glu', 'w_fox_o', 'w_mix_out', 'norm_mem_q', 'norm_mem_kv', 'w_mem_q', 'w_mem_kv', 'w_mem_o', 'norm_ffn', 'w_ffn_in', 'w_ffn_out', 'norm_final']
TWIN_WEIGHTS = ['norm_mix', 'w_in', 'b_forget', 'lam_re', 'lam_im', 'log_dt', 'b_re', 'b_im', 'c_re', 'c_im', 'd_skip', 'w_glu', 'w_fox_o', 'w_mix_out', 'norm_mem_q', 'norm_mem_kv', 'w_mem_q', 'w_mem_kv', 'w_mem_o', 'norm_ffn', 'w_ffn_in', 'w_ffn_out', 'norm_final']
TWIN_DIFF_INPUT = 'x'
TWIN_INPUTS = ['x', 'mem', 'norm_mix', 'w_in', 'b_forget', 'lam_re', 'lam_im', 'log_dt', 'b_re', 'b_im', 'c_re', 'c_im', 'd_skip', 'w_glu', 'w_fox_o', 'w_mix_out', 'norm_mem_q', 'norm_mem_kv', 'w_mem_q', 'w_mem_kv', 'w_mem_o', 'norm_ffn', 'w_ffn_in', 'w_ffn_out', 'norm_final', 'loss_target', 'm_norm_mix', 'm_w_in', 'm_b_forget', 'm_lam_re', 'm_lam_im', 'm_log_dt', 'm_b_re', 'm_b_im', 'm_c_re', 'm_c_im', 'm_d_skip', 'm_w_glu', 'm_w_fox_o', 'm_w_mix_out', 'm_norm_mem_q', 'm_norm_mem_kv', 'm_w_mem_q', 'm_w_mem_kv', 'm_w_mem_o', 'm_norm_ffn', 'm_w_ffn_in', 'm_w_ffn_out', 'm_norm_final', 'v_norm_mix', 'v_w_in', 'v_b_forget', 'v_lam_re', 'v_lam_im', 'v_log_dt', 'v_b_re', 'v_b_im', 'v_c_re', 'v_c_im', 'v_d_skip', 'v_w_glu', 'v_w_fox_o', 'v_w_mix_out', 'v_norm_mem_q', 'v_norm_mem_kv', 'v_w_mem_q', 'v_w_mem_kv', 'v_w_mem_o', 'v_norm_ffn', 'v_w_ffn_in', 'v_w_ffn_out', 'v_norm_final']
TWIN_OUTPUTS = ['loss', 'grad_x', 'grad_norm_mix', 'grad_w_in', 'grad_b_forget', 'grad_lam_re', 'grad_lam_im', 'grad_log_dt', 'grad_b_re', 'grad_b_im', 'grad_c_re', 'grad_c_im', 'grad_d_skip', 'grad_w_glu', 'grad_w_fox_o', 'grad_w_mix_out', 'grad_norm_mem_q', 'grad_norm_mem_kv', 'grad_w_mem_q', 'grad_w_mem_kv', 'grad_w_mem_o', 'grad_norm_ffn', 'grad_w_ffn_in', 'grad_w_ffn_out', 'grad_norm_final', 'delta_norm_mix', 'delta_w_in', 'delta_b_forget', 'delta_lam_re', 'delta_lam_im', 'delta_log_dt', 'delta_b_re', 'delta_b_im', 'delta_c_re', 'delta_c_im', 'delta_d_skip', 'delta_w_glu', 'delta_w_fox_o', 'delta_w_mix_out', 'delta_norm_mem_q', 'delta_norm_mem_kv', 'delta_w_mem_q', 'delta_w_mem_kv', 'delta_w_mem_o', 'delta_norm_ffn', 'delta_w_ffn_in', 'delta_w_ffn_out', 'delta_norm_final', 'new_m_norm_mix', 'new_m_w_in', 'new_m_b_forget', 'new_m_lam_re', 'new_m_lam_im', 'new_m_log_dt', 'new_m_b_re', 'new_m_b_im', 'new_m_c_re', 'new_m_c_im', 'new_m_d_skip', 'new_m_w_glu', 'new_m_w_fox_o', 'new_m_w_mix_out', 'new_m_norm_mem_q', 'new_m_norm_mem_kv', 'new_m_w_mem_q', 'new_m_w_mem_kv', 'new_m_w_mem_o', 'new_m_norm_ffn', 'new_m_w_ffn_in', 'new_m_w_ffn_out', 'new_m_norm_final', 'new_v_norm_mix', 'new_v_w_in', 'new_v_b_forget', 'new_v_lam_re', 'new_v_lam_im', 'new_v_log_dt', 'new_v_b_re', 'new_v_b_im', 'new_v_c_re', 'new_v_c_im', 'new_v_d_skip', 'new_v_w_glu', 'new_v_w_fox_o', 'new_v_w_mix_out', 'new_v_norm_mem_q', 'new_v_norm_mem_kv', 'new_v_w_mem_q', 'new_v_w_mem_kv', 'new_v_w_mem_o', 'new_v_norm_ffn', 'new_v_w_ffn_in', 'new_v_w_ffn_out', 'new_v_norm_final']
TWIN_LEAF_KINDS = {'loss': 'loss', 'grad_x': 'grad_x', 'grad_norm_mix': 'grad_w', 'grad_w_in': 'grad_w', 'grad_b_forget': 'grad_w', 'grad_lam_re': 'grad_w', 'grad_lam_im': 'grad_w', 'grad_log_dt': 'grad_w', 'grad_b_re': 'grad_w', 'grad_b_im': 'grad_w', 'grad_c_re': 'grad_w', 'grad_c_im': 'grad_w', 'grad_d_skip': 'grad_w', 'grad_w_glu': 'grad_w', 'grad_w_fox_o': 'grad_w', 'grad_w_mix_out': 'grad_w', 'grad_norm_mem_q': 'grad_w', 'grad_norm_mem_kv': 'grad_w', 'grad_w_mem_q': 'grad_w', 'grad_w_mem_kv': 'grad_w', 'grad_w_mem_o': 'grad_w', 'grad_norm_ffn': 'grad_w', 'grad_w_ffn_in': 'grad_w', 'grad_w_ffn_out': 'grad_w', 'grad_norm_final': 'grad_w', 'delta_norm_mix': 'delta_w', 'delta_w_in': 'delta_w', 'delta_b_forget': 'delta_w', 'delta_lam_re': 'delta_w', 'delta_lam_im': 'delta_w', 'delta_log_dt': 'delta_w', 'delta_b_re': 'delta_w', 'delta_b_im': 'delta_w', 'delta_c_re': 'delta_w', 'delta_c_im': 'delta_w', 'delta_d_skip': 'delta_w', 'delta_w_glu': 'delta_w', 'delta_w_fox_o': 'delta_w', 'delta_w_mix_out': 'delta_w', 'delta_norm_mem_q': 'delta_w', 'delta_norm_mem_kv': 'delta_w', 'delta_w_mem_q': 'delta_w', 'delta_w_mem_kv': 'delta_w', 'delta_w_mem_o': 'delta_w', 'delta_norm_ffn': 'delta_w', 'delta_w_ffn_in': 'delta_w', 'delta_w_ffn_out': 'delta_w', 'delta_norm_final': 'delta_w', 'new_m_norm_mix': 'new_m', 'new_m_w_in': 'new_m', 'new_m_b_forget': 'new_m', 'new_m_lam_re': 'new_m', 'new_m_lam_im': 'new_m', 'new_m_log_dt': 'new_m', 'new_m_b_re': 'new_m', 'new_m_b_im': 'new_m', 'new_m_c_re': 'new_m', 'new_m_c_im': 'new_m', 'new_m_d_skip': 'new_m', 'new_m_w_glu': 'new_m', 'new_m_w_fox_o': 'new_m', 'new_m_w_mix_out': 'new_m', 'new_m_norm_mem_q': 'new_m', 'new_m_norm_mem_kv': 'new_m', 'new_m_w_mem_q': 'new_m', 'new_m_w_mem_kv': 'new_m', 'new_m_w_mem_o': 'new_m', 'new_m_norm_ffn': 'new_m', 'new_m_w_ffn_in': 'new_m', 'new_m_w_ffn_out': 'new_m', 'new_m_norm_final': 'new_m', 'new_v_norm_mix': 'new_v', 'new_v_w_in': 'new_v', 'new_v_b_forget': 'new_v', 'new_v_lam_re': 'new_v', 'new_v_lam_im': 'new_v', 'new_v_log_dt': 'new_v', 'new_v_b_re': 'new_v', 'new_v_b_im': 'new_v', 'new_v_c_re': 'new_v', 'new_v_c_im': 'new_v', 'new_v_d_skip': 'new_v', 'new_v_w_glu': 'new_v', 'new_v_w_fox_o': 'new_v', 'new_v_w_mix_out': 'new_v', 'new_v_norm_mem_q': 'new_v', 'new_v_norm_mem_kv': 'new_v', 'new_v_w_mem_q': 'new_v', 'new_v_w_mem_kv': 'new_v', 'new_v_w_mem_o': 'new_v', 'new_v_norm_ffn': 'new_v', 'new_v_w_ffn_in': 'new_v', 'new_v_w_ffn_out': 'new_v', 'new_v_norm_final': 'new_v'}


def _forward(args):
    return _fwd_reference(*[args[k] for k in FWD_PARAMS])


def _output_shape():
    def fwd():
        inp = _fwd_setup_inputs(0)
        return _fwd_reference(*[inp[k] for k in FWD_PARAMS])
    out = _jax.eval_shape(fwd)
    return out.shape, out.dtype

N_MICROBATCH = 1
ADAM_LR = 0.001
ADAM_B1 = 0.9
ADAM_B2 = 0.999
ADAM_EPS = 1e-08
ADAM_WD = 0.01
ADAM_STEP = 10
PER_EXAMPLE_BATCH_AXIS = {'x': 0, 'mem': 0, 'loss_target': 0}
SHARED_INPUTS = []
_WEIGHT_DTYPES = {'norm_mix': _jnp.float32, 'w_in': _jnp.float32, 'b_forget': _jnp.float32, 'lam_re': _jnp.float32, 'lam_im': _jnp.float32, 'log_dt': _jnp.float32, 'b_re': _jnp.float32, 'b_im': _jnp.float32, 'c_re': _jnp.float32, 'c_im': _jnp.float32, 'd_skip': _jnp.float32, 'w_glu': _jnp.float32, 'w_fox_o': _jnp.float32, 'w_mix_out': _jnp.float32, 'norm_mem_q': _jnp.float32, 'norm_mem_kv': _jnp.float32, 'w_mem_q': _jnp.float32, 'w_mem_kv': _jnp.float32, 'w_mem_o': _jnp.float32, 'norm_ffn': _jnp.float32, 'w_ffn_in': _jnp.float32, 'w_ffn_out': _jnp.float32, 'norm_final': _jnp.float32}
MOMENT_SCALE = {'norm_mix': 1.063502e-01, 'w_in': 5.269139e-02, 'b_forget': 3.833170e-01, 'lam_re': 5.745645e-03, 'lam_im': 5.960299e-03, 'log_dt': 5.543690e+00, 'b_re': 3.759675e-03, 'b_im': 3.725557e-03, 'c_re': 5.343863e-03, 'c_im': 5.210536e-03, 'd_skip': 7.734704e-02, 'w_glu': 3.656546e-02, 'w_fox_o': 5.842180e-02, 'w_mix_out': 7.684905e-02, 'norm_mem_q': 2.816397e-02, 'norm_mem_kv': 4.219047e-02, 'w_mem_q': 4.018356e-02, 'w_mem_kv': 4.038503e-02, 'w_mem_o': 2.866275e-02, 'norm_ffn': 1.827298e-01, 'w_ffn_in': 7.603902e-02, 'w_ffn_out': 1.244986e-01, 'norm_final': 6.399695e+01}


def _to_microbatches(a, axis):
    t = _jnp.moveaxis(a, axis, 0)
    t = t.reshape((N_MICROBATCH, t.shape[0] // N_MICROBATCH) + t.shape[1:])
    return _jnp.moveaxis(t, 1, axis + 1)


def setup_inputs(seed: int = 0) -> dict:
    inp = _fwd_setup_inputs(seed)
    key = _jax.random.fold_in(_jax.random.key(seed), 7919)
    shape, _ = _output_shape()
    out = dict(inp)
    out["loss_target"] = _jax.random.normal(_jax.random.fold_in(key, 0), shape, _jnp.float32)
    for i, name in enumerate(TWIN_WEIGHTS):
        w = inp[name].astype(_jnp.float32)
        if MOMENT_SCALE is None:
            s = _jnp.sqrt(_jnp.mean(_jnp.square(w)) + 1e-30)
        else:
            s = MOMENT_SCALE[name]
        km, kv = _jax.random.split(_jax.random.fold_in(key, i + 1))
        out[name] = w
        out["m_" + name] = s * _jax.random.normal(km, w.shape, _jnp.float32)
        out["v_" + name] = (s * s) * _jax.random.uniform(kv, w.shape, _jnp.float32, 0.5, 1.5)
    if N_MICROBATCH > 1:
        for name, axis in PER_EXAMPLE_BATCH_AXIS.items():
            out[name] = _to_microbatches(out[name], axis)
    return {'x': out['x'], 'mem': out['mem'], 'norm_mix': out['norm_mix'], 'w_in': out['w_in'], 'b_forget': out['b_forget'], 'lam_re': out['lam_re'], 'lam_im': out['lam_im'], 'log_dt': out['log_dt'], 'b_re': out['b_re'], 'b_im': out['b_im'], 'c_re': out['c_re'], 'c_im': out['c_im'], 'd_skip': out['d_skip'], 'w_glu': out['w_glu'], 'w_fox_o': out['w_fox_o'], 'w_mix_out': out['w_mix_out'], 'norm_mem_q': out['norm_mem_q'], 'norm_mem_kv': out['norm_mem_kv'], 'w_mem_q': out['w_mem_q'], 'w_mem_kv': out['w_mem_kv'], 'w_mem_o': out['w_mem_o'], 'norm_ffn': out['norm_ffn'], 'w_ffn_in': out['w_ffn_in'], 'w_ffn_out': out['w_ffn_out'], 'norm_final': out['norm_final'], 'loss_target': out['loss_target'], 'm_norm_mix': out['m_norm_mix'], 'm_w_in': out['m_w_in'], 'm_b_forget': out['m_b_forget'], 'm_lam_re': out['m_lam_re'], 'm_lam_im': out['m_lam_im'], 'm_log_dt': out['m_log_dt'], 'm_b_re': out['m_b_re'], 'm_b_im': out['m_b_im'], 'm_c_re': out['m_c_re'], 'm_c_im': out['m_c_im'], 'm_d_skip': out['m_d_skip'], 'm_w_glu': out['m_w_glu'], 'm_w_fox_o': out['m_w_fox_o'], 'm_w_mix_out': out['m_w_mix_out'], 'm_norm_mem_q': out['m_norm_mem_q'], 'm_norm_mem_kv': out['m_norm_mem_kv'], 'm_w_mem_q': out['m_w_mem_q'], 'm_w_mem_kv': out['m_w_mem_kv'], 'm_w_mem_o': out['m_w_mem_o'], 'm_norm_ffn': out['m_norm_ffn'], 'm_w_ffn_in': out['m_w_ffn_in'], 'm_w_ffn_out': out['m_w_ffn_out'], 'm_norm_final': out['m_norm_final'], 'v_norm_mix': out['v_norm_mix'], 'v_w_in': out['v_w_in'], 'v_b_forget': out['v_b_forget'], 'v_lam_re': out['v_lam_re'], 'v_lam_im': out['v_lam_im'], 'v_log_dt': out['v_log_dt'], 'v_b_re': out['v_b_re'], 'v_b_im': out['v_b_im'], 'v_c_re': out['v_c_re'], 'v_c_im': out['v_c_im'], 'v_d_skip': out['v_d_skip'], 'v_w_glu': out['v_w_glu'], 'v_w_fox_o': out['v_w_fox_o'], 'v_w_mix_out': out['v_w_mix_out'], 'v_norm_mem_q': out['v_norm_mem_q'], 'v_norm_mem_kv': out['v_norm_mem_kv'], 'v_w_mem_q': out['v_w_mem_q'], 'v_w_mem_kv': out['v_w_mem_kv'], 'v_w_mem_o': out['v_w_mem_o'], 'v_norm_ffn': out['v_norm_ffn'], 'v_w_ffn_in': out['v_w_ffn_in'], 'v_w_ffn_out': out['v_w_ffn_out'], 'v_norm_final': out['v_norm_final']}


def _loss(weights, diff, rest, loss_target):
    with _jax.named_scope("forward"):
        args = {**rest, TWIN_DIFF_INPUT: diff, **{k: w.astype(_WEIGHT_DTYPES[k]) for k, w in weights.items()}}
        y = _forward(args)
    with _jax.named_scope("loss_head"):
        err = _jnp.square(y.astype(_jnp.float32) - loss_target)
        return 0.5 * _jnp.sum(_jnp.mean(err, axis=-1)) if err.ndim else 0.5 * err


def _adamw(w, g, m, v):
    m = ADAM_B1 * m + (1.0 - ADAM_B1) * g
    v = ADAM_B2 * v + (1.0 - ADAM_B2) * _jnp.square(g)
    m_hat = m / (1.0 - ADAM_B1 ** ADAM_STEP)
    v_hat = v / (1.0 - ADAM_B2 ** ADAM_STEP)
    delta = -ADAM_LR * (m_hat / (_jnp.sqrt(v_hat) + ADAM_EPS) + ADAM_WD * w)
    return delta, m, v


def reference(x, mem, norm_mix, w_in, b_forget, lam_re, lam_im, log_dt, b_re, b_im, c_re, c_im, d_skip, w_glu, w_fox_o, w_mix_out, norm_mem_q, norm_mem_kv, w_mem_q, w_mem_kv, w_mem_o, norm_ffn, w_ffn_in, w_ffn_out, norm_final, loss_target, m_norm_mix, m_w_in, m_b_forget, m_lam_re, m_lam_im, m_log_dt, m_b_re, m_b_im, m_c_re, m_c_im, m_d_skip, m_w_glu, m_w_fox_o, m_w_mix_out, m_norm_mem_q, m_norm_mem_kv, m_w_mem_q, m_w_mem_kv, m_w_mem_o, m_norm_ffn, m_w_ffn_in, m_w_ffn_out, m_norm_final, v_norm_mix, v_w_in, v_b_forget, v_lam_re, v_lam_im, v_log_dt, v_b_re, v_b_im, v_c_re, v_c_im, v_d_skip, v_w_glu, v_w_fox_o, v_w_mix_out, v_norm_mem_q, v_norm_mem_kv, v_w_mem_q, v_w_mem_kv, v_w_mem_o, v_norm_ffn, v_w_ffn_in, v_w_ffn_out, v_norm_final):
    given = dict(x=x, mem=mem, norm_mix=norm_mix, w_in=w_in, b_forget=b_forget, lam_re=lam_re, lam_im=lam_im, log_dt=log_dt, b_re=b_re, b_im=b_im, c_re=c_re, c_im=c_im, d_skip=d_skip, w_glu=w_glu, w_fox_o=w_fox_o, w_mix_out=w_mix_out, norm_mem_q=norm_mem_q, norm_mem_kv=norm_mem_kv, w_mem_q=w_mem_q, w_mem_kv=w_mem_kv, w_mem_o=w_mem_o, norm_ffn=norm_ffn, w_ffn_in=w_ffn_in, w_ffn_out=w_ffn_out, norm_final=norm_final, loss_target=loss_target, m_norm_mix=m_norm_mix, m_w_in=m_w_in, m_b_forget=m_b_forget, m_lam_re=m_lam_re, m_lam_im=m_lam_im, m_log_dt=m_log_dt, m_b_re=m_b_re, m_b_im=m_b_im, m_c_re=m_c_re, m_c_im=m_c_im, m_d_skip=m_d_skip, m_w_glu=m_w_glu, m_w_fox_o=m_w_fox_o, m_w_mix_out=m_w_mix_out, m_norm_mem_q=m_norm_mem_q, m_norm_mem_kv=m_norm_mem_kv, m_w_mem_q=m_w_mem_q, m_w_mem_kv=m_w_mem_kv, m_w_mem_o=m_w_mem_o, m_norm_ffn=m_norm_ffn, m_w_ffn_in=m_w_ffn_in, m_w_ffn_out=m_w_ffn_out, m_norm_final=m_norm_final, v_norm_mix=v_norm_mix, v_w_in=v_w_in, v_b_forget=v_b_forget, v_lam_re=v_lam_re, v_lam_im=v_lam_im, v_log_dt=v_log_dt, v_b_re=v_b_re, v_b_im=v_b_im, v_c_re=v_c_re, v_c_im=v_c_im, v_d_skip=v_d_skip, v_w_glu=v_w_glu, v_w_fox_o=v_w_fox_o, v_w_mix_out=v_w_mix_out, v_norm_mem_q=v_norm_mem_q, v_norm_mem_kv=v_norm_mem_kv, v_w_mem_q=v_w_mem_q, v_w_mem_kv=v_w_mem_kv, v_w_mem_o=v_w_mem_o, v_norm_ffn=v_norm_ffn, v_w_ffn_in=v_w_ffn_in, v_w_ffn_out=v_w_ffn_out, v_norm_final=v_norm_final)
    weights = {n: given[n] for n in TWIN_WEIGHTS}
    shared = {n: given[n] for n in SHARED_INPUTS}
    per_example = {n: given[n] for n in ['x', 'mem']}
    grad_fn = _jax.value_and_grad(_loss, argnums=(0, 1))

    def one_microbatch(ex, loss_target):
        ex = dict(ex)
        diff = ex.pop(TWIN_DIFF_INPUT)
        return grad_fn(weights, diff, {**shared, **ex}, loss_target)

    if N_MICROBATCH == 1:
        loss, (grad_w, grad_x) = one_microbatch(per_example, given["loss_target"])
    else:
        def body(carry, xs):
            loss_sum, grad_sum = carry
            l_k, (gw_k, gx_k) = one_microbatch(xs[0], xs[1])
            with _jax.named_scope("update"):
                return (loss_sum + l_k, _jax.tree.map(_jnp.add, grad_sum, gw_k)), gx_k

        init = (_jnp.zeros((), _jnp.float32), _jax.tree.map(_jnp.zeros_like, weights))
        (loss, grad_w), grad_x = _jax.lax.scan(body, init, (per_example, given["loss_target"]))
    with _jax.named_scope("update"):
        delta_w, new_m, new_v = {}, {}, {}
        for n in TWIN_WEIGHTS:
            delta_w[n], new_m[n], new_v[n] = _adamw(weights[n], grad_w[n], given["m_" + n], given["v_" + n])
    return (loss, grad_x, *[grad_w[n] for n in TWIN_WEIGHTS], *[delta_w[n] for n in TWIN_WEIGHTS],
            *[new_m[n] for n in TWIN_WEIGHTS], *[new_v[n] for n in TWIN_WEIGHTS])
```

```python
import math

import jax
import jax.numpy as jnp
import numpy as np
from jax import lax
from jax.experimental import pallas as pl
from jax.experimental.pallas import tpu as pltpu

F32 = jnp.float32
BF16 = jnp.bfloat16

N_DEV = 8
LANE = 128
VMEM_LIMIT = 56 * 1024 * 1024

D_MODEL = 1024
SSM_GROUP = 16
SSM_GROUPS = 32
SSM_WIDTH = 512
SSM_STATE = 64
SSM_CHUNK = 8
FOX_HEADS = 8
FOX_HEAD_DIM = 64
FOX_WIDTH = 512
MEM_HEADS = 4
MEM_HEAD_DIM = 128
MEM_WIDTH = 512
FFN_HIDDEN = 2816
RMS_EPS = 1e-6
IN_WIDTH = 4104
SHARD_IN = IN_WIDTH // N_DEV
SHARD_IN_PAD = 640
SHARD_FFN = 2 * FFN_HIDDEN // N_DEV
SHARD_FFN_PAD = 768
PROJ_GATE0 = 2048
PROJ_F0 = 4096
PROJ_WIDTH = 4224

ADAM_LR = 0.001
ADAM_B1 = 0.9
ADAM_B2 = 0.999
ADAM_EPS = 1e-08
ADAM_WD = 0.01
ADAM_STEP = 10


def _cparams(sem=None):
    return pltpu.CompilerParams(dimension_semantics=sem, vmem_limit_bytes=VMEM_LIMIT)


def _sigmoid(x):
    return 1.0 / (1.0 + jnp.exp(-x))


def _bdot(a, b, dims):
    return lax.dot_general(a.astype(BF16), b.astype(BF16), ((dims[0], dims[1]), ((), ())),
                           preferred_element_type=F32)


_DIMS = {"nn": ((1,), (0,)), "nt": ((1,), (1,)), "tn": ((0,), (0,))}


def _matmul(name, a, b, mode, m, n, k, *, out_dtype, tm, tn, tk, a_off=(0, 0), b_off=(0, 0),
            b_stack=None, out_stack=None, add=None):
    tm, tn, tk = min(tm, m), min(tn, n), min(tk, k)
    assert m % tm == 0 and n % tn == 0 and k % tk == 0, (name, m, n, k, tm, tn, tk)
    nk = k // tk
    grid = (m // tm, n // tn, nk)

    def blk(off, t):
        assert off % t == 0, (name, off, t)
        return off // t

    if mode in ("nn", "nt"):
        ar, ac = blk(a_off[0], tm), blk(a_off[1], tk)
        a_spec = pl.BlockSpec((tm, tk), lambda i, j, kk: (i + ar, kk + ac))
    else:
        ar, ac = blk(a_off[0], tk), blk(a_off[1], tm)
        a_spec = pl.BlockSpec((tk, tm), lambda i, j, kk: (kk + ar, i + ac))

    if b_stack is None:
        if mode in ("nn", "tn"):
            br, bc = blk(b_off[0], tk), blk(b_off[1], tn)
            b_spec = pl.BlockSpec((tk, tn), lambda i, j, kk: (kk + br, j + bc))
        else:
            br, bc = blk(b_off[0], tn), blk(b_off[1], tk)
            b_spec = pl.BlockSpec((tn, tk), lambda i, j, kk: (j + br, kk + bc))
    elif b_stack == "n":
        assert mode == "nn"
        per = (n // N_DEV) // tn
        assert per >= 1 and (n // N_DEV) % tn == 0, (name, n, tn)
        b_spec = pl.BlockSpec((None, tk, tn), lambda i, j, kk: (j // per, kk, j % per))
    else:
        assert mode == "nt" and b_stack == "k"
        per = (k // N_DEV) // tk
        assert per >= 1 and (k // N_DEV) % tk == 0, (name, k, tk)
        b_spec = pl.BlockSpec((None, tn, tk), lambda i, j, kk: (kk // per, j, kk % per))

    if out_stack:
        per_o = (n // N_DEV) // tn
        assert per_o >= 1 and (n // N_DEV) % tn == 0, (name, n, tn)
        o_spec = pl.BlockSpec((None, tm, tn), lambda i, j, kk: (j // per_o, i, j % per_o))
        out_shape = jax.ShapeDtypeStruct((N_DEV, m, n // N_DEV), out_dtype)
    else:
        o_spec = pl.BlockSpec((tm, tn), lambda i, j, kk: (i, j))
        out_shape = jax.ShapeDtypeStruct((m, n), out_dtype)

    in_specs = [a_spec, b_spec]
    operands = [a, b]
    if add is not None:
        in_specs.append(pl.BlockSpec((tm, tn), lambda i, j, kk: (i, j)))
        operands.append(add)
    dims = _DIMS[mode]
    has_add = add is not None

    def body(*refs):
        a_ref, b_ref = refs[0], refs[1]
        add_ref = refs[2] if has_add else None
        o_ref = refs[3] if has_add else refs[2]
        acc_ref = refs[-1] if nk > 1 else None
        prod = _bdot(a_ref[...], b_ref[...], dims)

        def finish(total):
            if has_add:
                total = total + add_ref[...].astype(F32)
            o_ref[...] = total.astype(o_ref.dtype)

        if nk == 1:
            finish(prod)
        else:
            kk = pl.program_id(2)

            @pl.when(kk == 0)
            def _():
                acc_ref[...] = prod

            @pl.when(jnp.logical_and(kk > 0, kk < nk - 1))
            def _():
                acc_ref[...] += prod

            @pl.when(kk == nk - 1)
            def _():
                finish(acc_ref[...] + prod)

    scratch = [pltpu.VMEM((tm, tn), F32)] if nk > 1 else []
    return pl.pallas_call(
        body, name=name, grid=grid, in_specs=in_specs, out_specs=o_spec, out_shape=out_shape,
        scratch_shapes=scratch,
        compiler_params=_cparams(("parallel", "parallel", "arbitrary")),
    )(*operands)


def _rms_fwd(name, x, gain, *, tr=512):
    r, d = x.shape
    tr = min(tr, r)

    def body(x_ref, g_ref, o_ref):
        xv = x_ref[...]
        rstd = lax.rsqrt(jnp.mean(xv * xv, axis=-1, keepdims=True) + RMS_EPS)
        o_ref[...] = (xv * rstd * g_ref[...]).astype(o_ref.dtype)

    return pl.pallas_call(
        body, name=name, grid=(r // tr,),
        in_specs=[pl.BlockSpec((tr, d), lambda i: (i, 0)), pl.BlockSpec((1, d), lambda i: (0, 0))],
        out_specs=pl.BlockSpec((tr, d), lambda i: (i, 0)),
        out_shape=jax.ShapeDtypeStruct((r, d), BF16),
        compiler_params=_cparams(("parallel",)),
    )(x, gain)


def _rms_bwd(name, dy, x, gain, res=None, *, tr=512):
    r, d = x.shape
    tr = min(tr, r)
    n = r // tr
    has_res = res is not None

    def body(*refs):
        dy_ref, x_ref, g_ref = refs[:3]
        res_ref = refs[3] if has_res else None
        dx_ref, dg_ref, acc_ref = refs[-3:]
        i = pl.program_id(0)
        xv = x_ref[...]
        rstd = lax.rsqrt(jnp.mean(xv * xv, axis=-1, keepdims=True) + RMS_EPS)
        xh = xv * rstd
        dyv = dy_ref[...].astype(F32)
        dxh = dyv * g_ref[...]
        dx = rstd * (dxh - xh * jnp.mean(dxh * xh, axis=-1, keepdims=True))
        if has_res:
            dx = dx + res_ref[...]
        dx_ref[...] = dx
        part = (dyv * xh).reshape(tr // 8, 8, d).sum(axis=0)

        @pl.when(i == 0)
        def _():
            acc_ref[...] = part

        @pl.when(i > 0)
        def _():
            acc_ref[...] += part

        @pl.when(i == n - 1)
        def _():
            dg_ref[...] = jnp.sum(acc_ref[...], axis=0, keepdims=True)

    row = pl.BlockSpec((tr, d), lambda i: (i, 0))
    in_specs = [row, row, pl.BlockSpec((1, d), lambda i: (0, 0))] + ([row] if has_res else [])
    ops = [dy, x, gain] + ([res] if has_res else [])
    return pl.pallas_call(
        body, name=name, grid=(n,), in_specs=in_specs,
        out_specs=(row, pl.BlockSpec((1, d), lambda i: (0, 0))),
        out_shape=(jax.ShapeDtypeStruct((r, d), F32), jax.ShapeDtypeStruct((1, d), F32)),
        scratch_shapes=[pltpu.VMEM((8, d), F32)],
        compiler_params=_cparams(("arbitrary",)),
    )(*ops)


def _final_loss(name, h, target, gain, *, tr=512):
    r, d = h.shape
    tr = min(tr, r)
    n = r // tr

    def body(h_ref, t_ref, g_ref, loss_ref, dh_ref, dg_ref, accl_ref, accg_ref):
        i = pl.program_id(0)
        xv = h_ref[...]
        rstd = lax.rsqrt(jnp.mean(xv * xv, axis=-1, keepdims=True) + RMS_EPS)
        xh = xv * rstd
        e = xh * g_ref[...] - t_ref[...]
        dyv = e * (1.0 / d)
        dxh = dyv * g_ref[...]
        dh_ref[...] = rstd * (dxh - xh * jnp.mean(dxh * xh, axis=-1, keepdims=True))
        lpart = (e * e).reshape(tr // 8, 8, d).sum(axis=0)
        gpart = (dyv * xh).reshape(tr // 8, 8, d).sum(axis=0)

        @pl.when(i == 0)
        def _():
            accl_ref[...] = lpart
            accg_ref[...] = gpart

        @pl.when(i > 0)
        def _():
            accl_ref[...] += lpart
            accg_ref[...] += gpart

        @pl.when(i == n - 1)
        def _():
            tot = jnp.sum(jnp.sum(accl_ref[...], axis=0, keepdims=True), axis=1, keepdims=True)
            loss_ref[...] = jnp.broadcast_to(tot * (0.5 / d), (1, LANE))
            dg_ref[...] = jnp.sum(accg_ref[...], axis=0, keepdims=True)

    row = pl.BlockSpec((tr, d), lambda i: (i, 0))
    one = pl.BlockSpec((1, d), lambda i: (0, 0))
    return pl.pallas_call(
        body, name=name, grid=(n,), in_specs=[row, row, one],
        out_specs=(pl.BlockSpec((1, LANE), lambda i: (0, 0)), row, one),
        out_shape=(jax.ShapeDtypeStruct((1, LANE), F32), jax.ShapeDtypeStruct((r, d), F32),
                   jax.ShapeDtypeStruct((1, d), F32)),
        scratch_shapes=[pltpu.VMEM((8, d), F32), pltpu.VMEM((8, d), F32)],
        compiler_params=_cparams(("arbitrary",)),
    )(h, target, gain)


_GELU_C = math.sqrt(2.0 / math.pi)


def _gelu_parts(z):
    inner = _GELU_C * (z + 0.044715 * z * z * z)
    t = jnp.tanh(inner)
    val = 0.5 * z * (1.0 + t)
    dinner = _GELU_C * (1.0 + 3.0 * 0.044715 * z * z)
    grad = 0.5 * (1.0 + t) + 0.5 * z * (1.0 - t * t) * dinner
    return val, grad


def _ssm_post_fwd(name, y8, u8, d8, *, tr=256):
    r, c = y8.shape
    tr = min(tr, r)

    def body(y_ref, u_ref, d_ref, o_ref):
        z = y_ref[...] + d_ref[...] * u_ref[...]
        o_ref[...] = _gelu_parts(z)[0].astype(o_ref.dtype)

    row = pl.BlockSpec((tr, c), lambda i: (i, 0))
    return pl.pallas_call(
        body, name=name, grid=(r // tr,), in_specs=[row, row, pl.BlockSpec((1, c), lambda i: (0, 0))],
        out_specs=row, out_shape=jax.ShapeDtypeStruct((r, c), BF16),
        compiler_params=_cparams(("parallel",)),
    )(y8, u8, d8)


def _ssm_post_bwd(name, dact8, y8, u8, d8, *, tr=256):
    r, c = y8.shape
    tr = min(tr, r)
    n = r // tr

    def body(da_ref, y_ref, u_ref, d_ref, dz_ref, dd_ref, acc_ref):
        i = pl.program_id(0)
        uv = u_ref[...]
        z = y_ref[...] + d_ref[...] * uv
        dz = da_ref[...].astype(F32) * _gelu_parts(z)[1]
        dz_ref[...] = dz
        part = (dz * uv).reshape(tr // 8, 8, c).sum(axis=0)

        @pl.when(i == 0)
        def _():
            acc_ref[...] = part

        @pl.when(i > 0)
        def _():
            acc_ref[...] += part

        @pl.when(i == n - 1)
        def _():
            tot = jnp.sum(acc_ref[...], axis=0, keepdims=True)
            out = tot[:, 0:SSM_WIDTH]
            for j in range(1, c // SSM_WIDTH):
                out = out + tot[:, j * SSM_WIDTH:(j + 1) * SSM_WIDTH]
            dd_ref[...] = out

    row = pl.BlockSpec((tr, c), lambda i: (i, 0))
    return pl.pallas_call(
        body, name=name, grid=(n,), in_specs=[row, row, row, pl.BlockSpec((1, c), lambda i: (0, 0))],
        out_specs=(row, pl.BlockSpec((1, SSM_WIDTH), lambda i: (0, 0))),
        out_shape=(jax.ShapeDtypeStruct((r, c), F32), jax.ShapeDtypeStruct((1, SSM_WIDTH), F32)),
        scratch_shapes=[pltpu.VMEM((8, c), F32)],
        compiler_params=_cparams(("arbitrary",)),
    )(dact8, y8, u8, d8)


def _mix_fwd(name, glu, gates, out_b, *, tr=256):
    r = glu.shape[0]
    d = D_MODEL
    tr = min(tr, r)

    def body(glu_ref, gate_ref, ob_ref, o_ref):
        out_a = glu_ref[:, 0:d] * _sigmoid(glu_ref[:, d:2 * d])
        mix = _sigmoid(gate_ref[:, 0:d]) * out_a + _sigmoid(gate_ref[:, d:2 * d]) * ob_ref[...]
        o_ref[...] = mix.astype(o_ref.dtype)

    wide = pl.BlockSpec((tr, 2 * d), lambda i: (i, 0))
    row = pl.BlockSpec((tr, d), lambda i: (i, 0))
    return pl.pallas_call(
        body, name=name, grid=(r // tr,), in_specs=[wide, wide, row], out_specs=row,
        out_shape=jax.ShapeDtypeStruct((r, d), BF16), compiler_params=_cparams(("parallel",)),
    )(glu, gates, out_b)


def _mix_bwd(name, dmix, glu, gates, out_b, *, tr=256):
    r = glu.shape[0]
    d = D_MODEL
    tr = min(tr, r)

    def body(dm_ref, glu_ref, gate_ref, ob_ref, dglu_ref, dgate_ref, dob_ref):
        dm = dm_ref[...]
        glu_a = glu_ref[:, 0:d]
        sb = _sigmoid(glu_ref[:, d:2 * d])
        ga = _sigmoid(gate_ref[:, 0:d])
        gb = _sigmoid(gate_ref[:, d:2 * d])
        out_a = glu_a * sb
        dout_a = dm * ga
        dglu_ref[:, 0:d] = (dout_a * sb).astype(dglu_ref.dtype)
        dglu_ref[:, d:2 * d] = (dout_a * glu_a * sb * (1.0 - sb)).astype(dglu_ref.dtype)
        dgate_ref[:, 0:d] = (dm * out_a * ga * (1.0 - ga)).astype(dgate_ref.dtype)
        dgate_ref[:, d:2 * d] = (dm * ob_ref[...] * gb * (1.0 - gb)).astype(dgate_ref.dtype)
        dob_ref[...] = (dm * gb).astype(dob_ref.dtype)

    wide = pl.BlockSpec((tr, 2 * d), lambda i: (i, 0))
    row = pl.BlockSpec((tr, d), lambda i: (i, 0))
    return pl.pallas_call(
        body, name=name, grid=(r // tr,), in_specs=[row, wide, wide, row], out_specs=(wide, wide, row),
        out_shape=(jax.ShapeDtypeStruct((r, 2 * d), BF16), jax.ShapeDtypeStruct((r, 2 * d), BF16),
                   jax.ShapeDtypeStruct((r, d), BF16)),
        compiler_params=_cparams(("parallel",)),
    )(dmix, glu, gates, out_b)


def _swiglu_fwd(name, f, *, tr=256):
    r = f.shape[0]
    hdn = FFN_HIDDEN
    tr = min(tr, r)

    def body(f_ref, o_ref):
        fa = f_ref[:, 0:hdn]
        o_ref[...] = (fa * _sigmoid(fa) * f_ref[:, hdn:2 * hdn]).astype(o_ref.dtype)

    return pl.pallas_call(
        body, name=name, grid=(r // tr,), in_specs=[pl.BlockSpec((tr, 2 * hdn), lambda i: (i, 0))],
        out_specs=pl.BlockSpec((tr, hdn), lambda i: (i, 0)),
        out_shape=jax.ShapeDtypeStruct((r, hdn), BF16), compiler_params=_cparams(("parallel",)),
    )(f)


def _swiglu_bwd(name, dg, f, *, tr=256):
    r = f.shape[0]
    hdn = FFN_HIDDEN
    tr = min(tr, r)

    def body(dg_ref, f_ref, o_ref):
        dgv = dg_ref[...].astype(F32)
        fa = f_ref[:, 0:hdn]
        fb = f_ref[:, hdn:2 * hdn]
        s = _sigmoid(fa)
        o_ref[:, 0:hdn] = (dgv * fb * s * (1.0 + fa * (1.0 - s))).astype(o_ref.dtype)
        o_ref[:, hdn:2 * hdn] = (dgv * fa * s).astype(o_ref.dtype)

    return pl.pallas_call(
        body, name=name, grid=(r // tr,),
        in_specs=[pl.BlockSpec((tr, hdn), lambda i: (i, 0)), pl.BlockSpec((tr, 2 * hdn), lambda i: (i, 0))],
        out_specs=pl.BlockSpec((tr, 2 * hdn), lambda i: (i, 0)),
        out_shape=jax.ShapeDtypeStruct((r, 2 * hdn), BF16), compiler_params=_cparams(("parallel",)),
    )(dg, f)


def _ssm_mats(lam_re, lam_im, log_dt, b_re, b_im, c_re, c_im, nc):
    hp = lax.Precision.HIGHEST
    t = SSM_CHUNK
    nq = SSM_GROUPS // 8
    lam = lax.complex(lam_re, lam_im)
    z = lam * jnp.exp(log_dt)[:, None]
    ks = jnp.arange(t + 1, dtype=F32)
    apow = jnp.exp(ks[:, None, None] * z[None])
    bbar = ((apow[1] - 1.0) / lam)[..., None] * lax.complex(b_re, b_im)
    c = lax.complex(c_re, c_im)
    eye = jnp.eye(8, dtype=F32)

    ca = c[None] * apow[:, :, None, :]
    kmat = jnp.einsum("kgnp,gpm->kgnm", ca, bbar, precision=hp).real
    ii = np.arange(t)
    lag = ii[None, :] - ii[:, None]
    kt = kmat[np.clip(lag, 0, t)] * jnp.asarray(lag >= 0, F32)[:, :, None, None, None]
    kt = kt.reshape(t, t, nq, 8, SSM_GROUP, SSM_GROUP)
    m_mat = jnp.einsum("ijqgnm,gh->qigmjhn", kt, eye).reshape(nq, 1024, 1024)

    arev = jnp.exp((float(t - 1) - ks[:t])[:, None, None] * z[None])
    w = arev[:, :, :, None] * bbar[None]
    wr = jnp.stack([w.real, w.imag]).reshape(2, t, nq, 8, SSM_STATE, SSM_GROUP)
    bw_mat = jnp.einsum("riqgpm,gh->qigmrhp", wr, eye).reshape(nq, 1024, 1024)

    ca1 = ca[1:]
    cr = jnp.stack([ca1.real, -ca1.imag]).reshape(2, t, nq, 8, SSM_GROUP, SSM_STATE)
    cm_mat = jnp.einsum("rjqgnp,gh->qrgpjhn", cr, eye).reshape(nq, 1024, 1024)

    def tiles(v):
        vq = jnp.concatenate([v.real.reshape(nq, 512), v.imag.reshape(nq, 512)], axis=1)
        return jnp.broadcast_to(vq.reshape(nq, 8, 1, LANE), (nq, 8, 8, LANE))

    return m_mat, bw_mat, cm_mat, tiles(apow[t]), tiles(jnp.exp(float(nc) * z))


def _x_tile_specs(nc, nq):
    return [pl.BlockSpec((nc, LANE), lambda q, t, i=i: (0, i * nq + q)) for i in range(SSM_CHUNK)]


def _cat_tiles(refs):
    return jnp.concatenate([r[...] for r in refs], axis=1)


def _ssm_w(name, x8, bw):
    nc = x8.shape[0]
    nq = bw.shape[0]

    def body(*refs):
        xq = _cat_tiles(refs[:8])
        refs[9][...] = _bdot(xq, refs[8][...], _DIMS["nn"])

    return pl.pallas_call(
        body, name=name, grid=(nq, 8),
        in_specs=_x_tile_specs(nc, nq) + [pl.BlockSpec((None, 1024, LANE), lambda q, t: (q, 0, t))],
        out_specs=pl.BlockSpec((None, None, nc, LANE), lambda q, t: (q, t, 0, 0)),
        out_shape=jax.ShapeDtypeStruct((nq, 8, nc, LANE), F32),
        compiler_params=_cparams(("parallel", "arbitrary")),
    )(*([x8] * 8), bw)


def _ssm_scan(name, w4, a_t, aseg_t, *, reverse, sprev4=None):
    nq, _, nc, _ = w4.shape
    ns = nc // 8
    with_da = sprev4 is not None

    def body(*refs):
        w_ref, a_ref, aseg_ref = refs[:3]
        s_ref = refs[3] if with_da else None
        o_ref = refs[4] if with_da else refs[3]
        da_ref = refs[5] if with_da else None
        sgn = -1.0 if reverse else 1.0
        ar = [a_ref[j] for j in range(4)]
        ai = [sgn * a_ref[j + 4] for j in range(4)]
        gr = [aseg_ref[j] for j in range(4)]
        gi = [sgn * aseg_ref[j + 4] for j in range(4)]
        zero = tuple(jnp.zeros((8, LANE), F32) for _ in range(8))

        def rows(tt):
            return pl.ds((ns - 1 - tt) if reverse else tt, 8, stride=ns)

        def step(carry, w):
            new_r = [ar[j] * carry[j] - ai[j] * carry[j + 4] + w[j] for j in range(4)]
            new_i = [ar[j] * carry[j + 4] + ai[j] * carry[j] + w[j + 4] for j in range(4)]
            return tuple(new_r + new_i)

        def pass1(tt, carry):
            return step(carry, [w_ref[j, rows(tt), :] for j in range(8)])

        ends = lax.fori_loop(0, ns, pass1, zero)
        sub = lax.broadcasted_iota(jnp.int32, (8, LANE), 0)
        init = list(zero)
        order = range(7, 0, -1) if reverse else range(0, 7)
        for s in order:
            nxt = s - 1 if reverse else s + 1
            cand_r = [gr[j] * init[j] - gi[j] * init[j + 4] + ends[j] for j in range(4)]
            cand_i = [gr[j] * init[j + 4] + gi[j] * init[j] + ends[j + 4] for j in range(4)]
            cand = cand_r + cand_i
            shift = 7 if reverse else 1
            init = [jnp.where(sub == nxt, pltpu.roll(cand[j], shift, axis=0), init[j]) for j in range(8)]

        def pass2(tt, state):
            carry, acc = state
            r = rows(tt)
            for j in range(8):
                o_ref[j, r, :] = carry[j]
            if with_da:
                sp = [s_ref[j, r, :] for j in range(8)]
                acc_r = [acc[j] + carry[j] * sp[j] + carry[j + 4] * sp[j + 4] for j in range(4)]
                acc_i = [acc[j + 4] + carry[j + 4] * sp[j] - carry[j] * sp[j + 4] for j in range(4)]
                acc = tuple(acc_r + acc_i)
            return step(carry, [w_ref[j, r, :] for j in range(8)]), acc

        _, acc = lax.fori_loop(0, ns, pass2, (tuple(init), zero))
        if with_da:
            for j in range(8):
                da_ref[j] = acc[j]

    big = pl.BlockSpec((None, 8, nc, LANE), lambda q: (q, 0, 0, 0))
    small = pl.BlockSpec((None, 8, 8, LANE), lambda q: (q, 0, 0, 0))
    in_specs = [big, small, small] + ([big] if with_da else [])
    ops = [w4, a_t, aseg_t] + ([sprev4] if with_da else [])
    out_specs = (big, small) if with_da else big
    big_s = jax.ShapeDtypeStruct((nq, 8, nc, LANE), F32)
    out_shape = (big_s, jax.ShapeDtypeStruct((nq, 8, 8, LANE), F32)) if with_da else big_s
    return pl.pallas_call(
        body, name=name, grid=(nq,), in_specs=in_specs, out_specs=out_specs, out_shape=out_shape,
        compiler_params=_cparams(("parallel",)),
    )(*ops)


def _ssm_y(name, x8, sprev4, m_mat, cm_mat):
    nc = x8.shape[0]
    nq = m_mat.shape[0]

    def body(*refs):
        xq = _cat_tiles(refs[:8])
        s_ref, m_ref, cm_ref, o_ref = refs[8:12]
        sq = jnp.concatenate([s_ref[t] for t in range(8)], axis=1)
        o_ref[...] = _bdot(xq, m_ref[...], _DIMS["nn"]) + _bdot(sq, cm_ref[...], _DIMS["nn"])

    col = pl.BlockSpec((None, 1024, LANE), lambda q, j: (q, 0, j))
    return pl.pallas_call(
        body, name=name, grid=(nq, 8),
        in_specs=_x_tile_specs(nc, nq) + [pl.BlockSpec((None, 8, nc, LANE), lambda q, j: (q, 0, 0, 0)), col, col],
        out_specs=pl.BlockSpec((nc, LANE), lambda q, j: (0, j * nq + q)),
        out_shape=jax.ShapeDtypeStruct((nc, 8 * SSM_WIDTH), F32),
        compiler_params=_cparams(("parallel", "arbitrary")),
    )(*([x8] * 8), sprev4, m_mat, cm_mat)


def _ssm_ds(name, dz8, sprev4, cm_mat):
    nc = dz8.shape[0]
    nq = cm_mat.shape[0]

    def body(*refs):
        dyq = _cat_tiles(refs[:8]).astype(BF16)
        s_ref, cm_ref, ds_ref, dcm_ref = refs[8:12]
        ds_ref[...] = _bdot(dyq, cm_ref[...], _DIMS["nt"])
        dcm_ref[...] = _bdot(s_ref[...], dyq, _DIMS["tn"])

    tile = pl.BlockSpec((None, None, nc, LANE), lambda q, t: (q, t, 0, 0))
    rowblk = pl.BlockSpec((None, LANE, 1024), lambda q, t: (q, t, 0))
    return pl.pallas_call(
        body, name=name, grid=(nq, 8),
        in_specs=_x_tile_specs(nc, nq) + [tile, rowblk],
        out_specs=(tile, rowblk),
        out_shape=(jax.ShapeDtypeStruct((nq, 8, nc, LANE), F32), jax.ShapeDtypeStruct((nq, 1024, 1024), F32)),
        compiler_params=_cparams(("parallel", "arbitrary")),
    )(*([dz8] * 8), sprev4, cm_mat)


def _ssm_dx(name, dz8, g4, x8, m_mat, bw_mat, d8):
    nc = dz8.shape[0]
    nq = m_mat.shape[0]

    def body(*refs):
        dyq = _cat_tiles(refs[:8]).astype(BF16)
        g_ref, x_ref, m_ref, bw_ref, d_ref, dzi_ref, dx_ref, dm_ref, dbw_ref = refs[8:17]
        gq = jnp.concatenate([g_ref[t] for t in range(8)], axis=1).astype(BF16)
        dx = _bdot(dyq, m_ref[...], _DIMS["nt"]) + _bdot(gq, bw_ref[...], _DIMS["nt"])
        dx_ref[...] = (dx + d_ref[...] * dzi_ref[...]).astype(dx_ref.dtype)
        xi = x_ref[...]
        dm_ref[...] = _bdot(xi, dyq, _DIMS["tn"])
        dbw_ref[...] = _bdot(xi, gq, _DIMS["tn"])

    xtile = pl.BlockSpec((nc, LANE), lambda q, i: (0, i * nq + q))
    rowblk = pl.BlockSpec((None, LANE, 1024), lambda q, i: (q, i, 0))
    return pl.pallas_call(
        body, name=name, grid=(nq, 8),
        in_specs=_x_tile_specs(nc, nq) + [pl.BlockSpec((None, 8, nc, LANE), lambda q, i: (q, 0, 0, 0)), xtile, rowblk, rowblk,
                                          pl.BlockSpec((1, LANE), lambda q, i: (0, q)), xtile],
        out_specs=(xtile, rowblk, rowblk),
        out_shape=(jax.ShapeDtypeStruct((nc, 8 * SSM_WIDTH), BF16), jax.ShapeDtypeStruct((nq, 1024, 1024), F32),
                   jax.ShapeDtypeStruct((nq, 1024, 1024), F32)),
        compiler_params=_cparams(("parallel", "arbitrary")),
    )(*([dz8] * 8), g4, x8, m_mat, bw_mat, d8, dz8)


CUM_BLK = 256


def _split3(x):
    hi = x.astype(BF16)
    r1 = x - hi.astype(F32)
    mid = r1.astype(BF16)
    lo = (r1 - mid.astype(F32)).astype(BF16)
    return hi, mid, lo


def _tri_dot(x, tri):
    hi, mid, lo = _split3(x)
    d = _DIMS["nn"]
    return _bdot(hi, tri, d) + _bdot(mid, tri, d) + _bdot(lo, tri, d)


def _tri(n, lower):
    r = lax.broadcasted_iota(jnp.int32, (n, n), 0)
    c = lax.broadcasted_iota(jnp.int32, (n, n), 1)
    return jnp.where((r >= c) if lower else (r <= c), 1.0, 0.0).astype(BF16)


def _fox_cum(name, fproj, bcol):
    seq = fproj.shape[0]
    blk = min(CUM_BLK, seq)

    def body(f_ref, b_ref, o_ref, carry_ref):
        i = pl.program_id(0)

        @pl.when(i == 0)
        def _():
            carry_ref[...] = jnp.zeros_like(carry_ref)

        z = f_ref[...].T + b_ref[...]
        logf = jnp.minimum(z, 0.0) - jnp.log(1.0 + jnp.exp(-jnp.abs(z)))
        carry = carry_ref[...]
        cum = _tri_dot(logf, _tri(blk, lower=False)) + jnp.tile(carry, (1, blk // LANE))
        o_ref[...] = cum[0:8, :]
        carry_ref[...] = carry + jnp.sum(logf, axis=1, keepdims=True)

    return pl.pallas_call(
        body, name=name, grid=(seq // blk,),
        in_specs=[pl.BlockSpec((blk, LANE), lambda i: (i, 0)), pl.BlockSpec((LANE, 1), lambda i: (0, 0))],
        out_specs=pl.BlockSpec((8, blk), lambda i: (0, i)),
        out_shape=jax.ShapeDtypeStruct((8, seq), F32),
        scratch_shapes=[pltpu.VMEM((LANE, LANE), F32)],
        compiler_params=_cparams(("arbitrary",)),
    )(fproj, bcol)


def _fox_cum_bwd(name, dcum_t, fproj, bcol):
    seq = fproj.shape[0]
    blk = min(CUM_BLK, seq)
    n = seq // blk

    def body(dc_ref, f_ref, b_ref, df_ref, db_ref, carry_ref, acc_ref):
        i = pl.program_id(0)

        @pl.when(i == 0)
        def _():
            carry_ref[...] = jnp.zeros_like(carry_ref)
            acc_ref[...] = jnp.zeros_like(acc_ref)

        dc = jnp.concatenate([dc_ref[...], jnp.zeros((LANE - 8, blk), F32)], axis=0)
        carry = carry_ref[...]
        dlogf = _tri_dot(dc, _tri(blk, lower=True)) + jnp.tile(carry, (1, blk // LANE))
        carry_ref[...] = carry + jnp.sum(dc, axis=1, keepdims=True)
        z = f_ref[...].T + b_ref[...]
        dft = dlogf / (1.0 + jnp.exp(z))
        df_ref[...] = dft.T.astype(df_ref.dtype)
        acc_ref[...] += jnp.sum(dft, axis=1, keepdims=True)

        @pl.when(i == n - 1)
        def _():
            db_ref[...] = acc_ref[...]

    return pl.pallas_call(
        body, name=name, grid=(n,),
        in_specs=[pl.BlockSpec((8, blk), lambda i: (0, n - 1 - i)), pl.BlockSpec((blk, LANE), lambda i: (n - 1 - i, 0)),
                  pl.BlockSpec((LANE, 1), lambda i: (0, 0))],
        out_specs=(pl.BlockSpec((blk, LANE), lambda i: (n - 1 - i, 0)), pl.BlockSpec((LANE, LANE), lambda i: (0, 0))),
        out_shape=(jax.ShapeDtypeStruct((seq, LANE), BF16), jax.ShapeDtypeStruct((LANE, LANE), F32)),
        scratch_shapes=[pltpu.VMEM((LANE, LANE), F32), pltpu.VMEM((LANE, LANE), F32)],
        compiler_params=_cparams(("arbitrary",)),
    )(dcum_t, fproj, bcol)


FOX_BLK = 512
FOX_SCALE = FOX_HEAD_DIM ** -0.5


def _fox_head_mask(shape, hh):
    lane = lax.broadcasted_iota(jnp.int32, shape, 1)
    return (lane < FOX_HEAD_DIM) if hh == 0 else (lane >= FOX_HEAD_DIM)


def _fox_bias(cum_ref, hh, q0, k0, blk):
    c0 = jnp.max(cum_ref[hh:hh + 1, pl.ds(q0, LANE)], axis=1, keepdims=True)
    return c0 - cum_ref[hh:hh + 1, pl.ds(k0, blk)]


def _fox_fwd(name, qkv, cum_t):
    seq = qkv.shape[0]
    blk = min(FOX_BLK, seq)
    nb = seq // blk
    npair = FOX_HEADS // 2

    def body(q_ref, k_ref, v_ref, cum_ref, o_ref, lse_ref):
        iq = pl.program_id(1)
        q0 = pl.multiple_of(iq * blk, blk)
        qv = q_ref[...]
        row = lax.broadcasted_iota(jnp.int32, (blk, blk), 0)
        col = lax.broadcasted_iota(jnp.int32, (blk, blk), 1)
        outs = []
        for hh in range(2):
            qh = jnp.where(_fox_head_mask(qv.shape, hh), qv, jnp.zeros_like(qv)) * FOX_SCALE

            def scores(kb, hh=hh, qh=qh):
                k0 = pl.multiple_of(kb * blk, blk)
                s = _bdot(qh, k_ref[pl.ds(k0, blk), :], _DIMS["nt"])
                return s + _fox_bias(cum_ref, hh, q0, k0, blk), k0

            def update(state, s, k0):
                m, l, acc = state
                m_new = jnp.maximum(m, jnp.max(s, axis=1, keepdims=True))
                alpha = jnp.exp(m - m_new)
                p = jnp.exp(s - m_new)
                l = alpha * l + jnp.sum(p, axis=1, keepdims=True)
                acc = alpha * acc + _bdot(p, v_ref[pl.ds(k0, blk), :], _DIMS["nn"])
                return m_new, l, acc

            def step(kb, state):
                s, k0 = scores(kb)
                return update(state, s, k0)

            init = (jnp.full((blk, 1), -jnp.inf, F32), jnp.zeros((blk, 1), F32), jnp.zeros((blk, LANE), F32))
            state = lax.fori_loop(0, iq, step, init)
            s, k0 = scores(iq)
            m, l, acc = update(state, jnp.where(row >= col, s, -jnp.inf), k0)
            outs.append(acc / l)
            lse_ref[hh] = jnp.broadcast_to(m + jnp.log(l), (blk, LANE))
        o_ref[...] = jnp.where(_fox_head_mask(outs[0].shape, 0), outs[0], outs[1]).astype(o_ref.dtype)

    return pl.pallas_call(
        body, name=name, grid=(npair, nb),
        in_specs=[pl.BlockSpec((blk, LANE), lambda p, i: (i, p)),
                  pl.BlockSpec((seq, LANE), lambda p, i: (0, npair + p)),
                  pl.BlockSpec((seq, LANE), lambda p, i: (0, 2 * npair + p)),
                  pl.BlockSpec((None, 2, seq), lambda p, i: (p, 0, 0))],
        out_specs=(pl.BlockSpec((blk, LANE), lambda p, i: (i, p)),
                   pl.BlockSpec((2, blk, LANE), lambda p, i: (p, i, 0))),
        out_shape=(jax.ShapeDtypeStruct((seq, FOX_WIDTH), BF16), jax.ShapeDtypeStruct((FOX_HEADS, seq, LANE), F32)),
        compiler_params=_cparams(("parallel", "arbitrary")),
    )(qkv, qkv, qkv, cum_t)


def _fox_bwd(name, qkv, cum_t, att, datt, lse):
    seq = qkv.shape[0]
    blk = min(FOX_BLK, seq)
    nb = seq // blk
    npair = FOX_HEADS // 2

    def body(q_ref, k_ref, v_ref, cum_ref, o_ref, do_ref, lse_ref, dq_ref, dk_ref, dv_ref, dcum_ref):
        iq = pl.program_id(1)
        q0 = pl.multiple_of(iq * blk, blk)

        @pl.when(iq == 0)
        def _():
            dk_ref[...] = jnp.zeros_like(dk_ref)
            dv_ref[...] = jnp.zeros_like(dv_ref)
            dcum_ref[...] = jnp.zeros_like(dcum_ref)

        qv = q_ref[...]
        dov = do_ref[...].astype(F32)
        ov = o_ref[...].astype(F32)
        row = lax.broadcasted_iota(jnp.int32, (blk, blk), 0)
        col = lax.broadcasted_iota(jnp.int32, (blk, blk), 1)
        dqs = []
        for hh in range(2):
            hm = _fox_head_mask(qv.shape, hh)
            qh = jnp.where(hm, qv, jnp.zeros_like(qv)) * FOX_SCALE
            doh = jnp.where(hm, dov, 0.0)
            dohb = doh.astype(BF16)
            delta = jnp.sum(doh * ov, axis=1, keepdims=True)
            lse_t = jnp.tile(lse_ref[hh], (1, blk // LANE))

            def block(kb, accs, masked):
                dq_acc, rs_acc = accs
                k0 = pl.multiple_of(kb * blk, blk)
                kv = k_ref[pl.ds(k0, blk), :]
                s = _bdot(qh, kv, _DIMS["nt"]) + _fox_bias(cum_ref, hh, q0, k0, blk)
                p = jnp.exp(s - lse_t)
                if masked:
                    p = jnp.where(row >= col, p, 0.0)
                dp = _bdot(dohb, v_ref[pl.ds(k0, blk), :], _DIMS["nt"])
                ds = p * (dp - delta)
                dsb = ds.astype(BF16)
                dk_ref[pl.ds(k0, blk), :] += _bdot(dsb, qh, _DIMS["tn"])
                dv_ref[pl.ds(k0, blk), :] += _bdot(p, dohb, _DIMS["tn"])
                dcum_ref[hh:hh + 1, pl.ds(k0, blk)] -= jnp.sum(ds, axis=0, keepdims=True)
                return dq_acc + _bdot(dsb, kv, _DIMS["nn"]), rs_acc + jnp.sum(ds, axis=1, keepdims=True)

            accs = lax.fori_loop(0, iq, lambda kb, a: block(kb, a, False),
                                 (jnp.zeros((blk, LANE), F32), jnp.zeros((blk, 1), F32)))
            dq_acc, rs = block(iq, accs, True)
            dqs.append(dq_acc)
            dcum_ref[hh:hh + 1, pl.ds(q0, blk)] += jnp.broadcast_to(rs, (blk, LANE)).T[0:1, :]
        dq = jnp.where(_fox_head_mask(dqs[0].shape, 0), dqs[0], dqs[1]) * FOX_SCALE
        dq_ref[...] = dq.astype(dq_ref.dtype)

    qblk = pl.BlockSpec((blk, LANE), lambda p, i: (i, p))
    full = pl.BlockSpec((seq, LANE), lambda p, i: (0, p))
    return pl.pallas_call(
        body, name=name, grid=(npair, nb),
        in_specs=[qblk,
                  pl.BlockSpec((seq, LANE), lambda p, i: (0, npair + p)),
                  pl.BlockSpec((seq, LANE), lambda p, i: (0, 2 * npair + p)),
                  pl.BlockSpec((None, 2, seq), lambda p, i: (p, 0, 0)),
                  qblk, qblk,
                  pl.BlockSpec((2, blk, LANE), lambda p, i: (p, i, 0))],
        out_specs=(qblk, full, full, pl.BlockSpec((None, 2, seq), lambda p, i: (p, 0, 0))),
        out_shape=(jax.ShapeDtypeStruct((seq, FOX_WIDTH), BF16), jax.ShapeDtypeStruct((seq, FOX_WIDTH), F32),
                   jax.ShapeDtypeStruct((seq, FOX_WIDTH), F32), jax.ShapeDtypeStruct((npair, 2, seq), F32)),
        compiler_params=_cparams(("arbitrary", "arbitrary")),
    )(qkv, qkv, qkv, cum_t, att, datt, lse)


MEM_SCALE = MEM_HEAD_DIM ** -0.5


def _mem_probs(qh, kh):
    s = _bdot(qh, kh, _DIMS["nt"]) * MEM_SCALE
    p = jnp.exp(s - jnp.max(s, axis=1, keepdims=True))
    return p / jnp.sum(p, axis=1, keepdims=True)


def _mem_fwd(name, q2, kv, *, tr=512):
    seq = q2.shape[0]
    mlen = kv.shape[0]
    tr = min(tr, seq)

    def body(q_ref, kv_ref, o_ref):
        for h in range(MEM_HEADS):
            sl = slice(h * MEM_HEAD_DIM, (h + 1) * MEM_HEAD_DIM)
            sv = slice(MEM_WIDTH + h * MEM_HEAD_DIM, MEM_WIDTH + (h + 1) * MEM_HEAD_DIM)
            p = _mem_probs(q_ref[:, sl], kv_ref[:, sl])
            o_ref[:, sl] = _bdot(p, kv_ref[:, sv], _DIMS["nn"]).astype(o_ref.dtype)

    return pl.pallas_call(
        body, name=name, grid=(seq // tr,),
        in_specs=[pl.BlockSpec((tr, MEM_WIDTH), lambda i: (i, 0)), pl.BlockSpec((mlen, 2 * MEM_WIDTH), lambda i: (0, 0))],
        out_specs=pl.BlockSpec((tr, MEM_WIDTH), lambda i: (i, 0)),
        out_shape=jax.ShapeDtypeStruct((seq, MEM_WIDTH), BF16),
        compiler_params=_cparams(("parallel",)),
    )(q2, kv)


def _mem_bwd(name, q2, kv, do2, *, tr=512):
    seq = q2.shape[0]
    mlen = kv.shape[0]
    tr = min(tr, seq)

    def body(q_ref, kv_ref, do_ref, dq_ref, dkv_ref):
        i = pl.program_id(0)

        @pl.when(i == 0)
        def _():
            dkv_ref[...] = jnp.zeros_like(dkv_ref)

        for h in range(MEM_HEADS):
            sl = slice(h * MEM_HEAD_DIM, (h + 1) * MEM_HEAD_DIM)
            sv = slice(MEM_WIDTH + h * MEM_HEAD_DIM, MEM_WIDTH + (h + 1) * MEM_HEAD_DIM)
            qh = q_ref[:, sl]
            kh = kv_ref[:, sl]
            doh = do_ref[:, sl].astype(BF16)
            p = _mem_probs(qh, kh)
            dp = _bdot(doh, kv_ref[:, sv], _DIMS["nt"])
            ds = (p * (dp - jnp.sum(p * dp, axis=1, keepdims=True)) * MEM_SCALE).astype(BF16)
            dq_ref[:, sl] = _bdot(ds, kh, _DIMS["nn"]).astype(dq_ref.dtype)
            dkv_ref[:, sl] += _bdot(ds, qh, _DIMS["tn"])
            dkv_ref[:, sv] += _bdot(p, doh, _DIMS["tn"])

    row = pl.BlockSpec((tr, MEM_WIDTH), lambda i: (i, 0))
    kvs = pl.BlockSpec((mlen, 2 * MEM_WIDTH), lambda i: (0, 0))
    return pl.pallas_call(
        body, name=name, grid=(seq // tr,), in_specs=[row, kvs, row], out_specs=(row, kvs),
        out_shape=(jax.ShapeDtypeStruct((seq, MEM_WIDTH), BF16), jax.ShapeDtypeStruct((mlen, 2 * MEM_WIDTH), F32)),
        compiler_params=_cparams(("arbitrary",)),
    )(q2, kv, do2)


def _exchange(name, arrays, *, scatter):
    n = len(arrays)

    def body(*refs):
        ins, outs = refs[:n], refs[n:2 * n]
        send_sems, recv_sems, local_sems = refs[2 * n:]
        x, y, c = lax.axis_index("x"), lax.axis_index("y"), lax.axis_index("c")
        me = 4 * x + 2 * y + c
        copies = []
        for a in range(n):
            src = ins[a].at[me] if scatter else ins[a]
            local = pltpu.make_async_copy(src, outs[a].at[me], local_sems.at[a])
            local.start()
            copies.append(local)
        for k in range(N_DEV - 1):
            flip = k + 1
            px = 1 - x if flip & 4 else x
            py = 1 - y if flip & 2 else y
            pc = 1 - c if flip & 1 else c
            peer = 4 * px + 2 * py + pc
            for a in range(n):
                src = ins[a].at[peer] if scatter else ins[a]
                rdma = pltpu.make_async_remote_copy(
                    src_ref=src, dst_ref=outs[a].at[me], send_sem=send_sems.at[a, k], recv_sem=recv_sems.at[a, k],
                    device_id=(px, py, pc), device_id_type=pl.DeviceIdType.MESH)
                rdma.start()
                copies.append(rdma)
        for cp in copies:
            cp.wait()

    hbm = pl.BlockSpec(memory_space=pl.ANY)
    out_shape = tuple(
        jax.ShapeDtypeStruct(arr.shape if scatter else (N_DEV,) + arr.shape, arr.dtype) for arr in arrays)
    return pl.pallas_call(
        body, name=name, in_specs=[hbm] * n, out_specs=tuple([hbm] * n), out_shape=out_shape,
        scratch_shapes=[pltpu.SemaphoreType.DMA((n, N_DEV - 1)), pltpu.SemaphoreType.DMA((n, N_DEV - 1)),
                        pltpu.SemaphoreType.DMA((n,))],
    )(*arrays)


def _remap_pieces(runs):
    plan = {}
    for du, dc, su, sc, ln in runs:
        while ln > 0:
            lane = dc % LANE
            take = min(ln, LANE - lane)
            plan.setdefault((du, dc // LANE), []).append((su, sc, take, lane))
            dc, sc, ln = dc + take, sc + take, ln - take
    return plan


def _remap(name, srcs, src_units, runs, *, out_units, out_cols, out_dtype, tr=256):
    rows = srcs[0].shape[-2]
    tr = min(tr, rows)
    plan = _remap_pieces(runs)
    n_src = len(srcs)
    stacked_out = out_units is not None
    n_tiles = out_cols // LANE

    def body(*refs):
        o_ref = refs[n_src]

        def src_tile(unit, t):
            ai, lead = src_units[unit]
            ref = refs[ai]
            sl = slice(t * LANE, (t + 1) * LANE)
            return (ref[:, sl] if lead is None else ref[lead, :, sl]).astype(F32)

        lane = lax.broadcasted_iota(jnp.int32, (tr, LANE), 1)
        for du in range(out_units if stacked_out else 1):
            for t in range(n_tiles):
                acc = jnp.zeros((tr, LANE), F32)
                for su, sc, ln, dl in plan.get((du if stacked_out else None, t), []):
                    st, so = sc // LANE, sc % LANE
                    first = src_tile(su, st)
                    if so == dl and so + ln <= LANE:
                        piece = first
                    else:
                        second = src_tile(su, st + 1) if so + ln > LANE else first
                        both = jnp.concatenate([first, second], axis=1)
                        piece = pltpu.roll(both, (dl - so) % (2 * LANE), axis=1)[:, 0:LANE]
                    acc = piece if (dl == 0 and ln == LANE) else jnp.where(
                        jnp.logical_and(lane >= dl, lane < dl + ln), piece, acc)
                if stacked_out:
                    o_ref[du, :, t * LANE:(t + 1) * LANE] = acc.astype(o_ref.dtype)
                else:
                    o_ref[:, t * LANE:(t + 1) * LANE] = acc.astype(o_ref.dtype)

    in_specs = []
    for arr in srcs:
        if arr.ndim == 2:
            in_specs.append(pl.BlockSpec((tr, arr.shape[1]), lambda i: (i, 0)))
        else:
            in_specs.append(pl.BlockSpec((arr.shape[0], tr, arr.shape[2]), lambda i: (0, i, 0)))
    if stacked_out:
        out_spec = pl.BlockSpec((out_units, tr, out_cols), lambda i: (0, i, 0))
        out_shape = jax.ShapeDtypeStruct((out_units, rows, out_cols), out_dtype)
    else:
        out_spec = pl.BlockSpec((tr, out_cols), lambda i: (i, 0))
        out_shape = jax.ShapeDtypeStruct((rows, out_cols), out_dtype)
    return pl.pallas_call(
        body, name=name, grid=(rows // tr,), in_specs=in_specs, out_specs=out_spec, out_shape=out_shape,
        compiler_params=_cparams(("parallel",)),
    )(*srcs)


def _proj_col(c):
    if c < PROJ_GATE0:
        return c
    if c < PROJ_GATE0 + FOX_HEADS:
        return PROJ_F0 + (c - PROJ_GATE0)
    return c - FOX_HEADS


def _win_runs():
    cuts = sorted(set([0, PROJ_GATE0, PROJ_GATE0 + FOX_HEADS, IN_WIDTH] + [SHARD_IN * k for k in range(N_DEV + 1)]))
    return [(lo // SHARD_IN, lo % SHARD_IN, _proj_col(lo), hi - lo) for lo, hi in zip(cuts[:-1], cuts[1:])]


def _assemble_win(name, stacked):
    runs = [(None, pc, k, sc, ln) for k, sc, pc, ln in _win_runs()]
    return _remap(name, [stacked], [(0, k) for k in range(N_DEV)], runs,
                  out_units=None, out_cols=PROJ_WIDTH, out_dtype=BF16)


_DPROJ_SEGS = ((0, 512), (512, 1024), (1024, 1536), (1536, 2048), (2048, 4096), (4096, 4224))


def _disassemble_dwin(name, parts):
    runs = []
    for k, sc, pc, ln in _win_runs():
        while ln > 0:
            seg = next(i for i, (lo, hi) in enumerate(_DPROJ_SEGS) if lo <= pc < hi)
            take = min(ln, _DPROJ_SEGS[seg][1] - pc)
            runs.append((k, sc, seg, pc - _DPROJ_SEGS[seg][0], take))
            sc, pc, ln = sc + take, pc + take, ln - take
    return _remap(name, list(parts), [(i, None) for i in range(len(parts))], runs,
                  out_units=N_DEV, out_cols=SHARD_IN_PAD, out_dtype=BF16)


def _assemble_wffn(name, stacked):
    runs = [(None, SHARD_FFN * k, k, 0, SHARD_FFN) for k in range(N_DEV)]
    return _remap(name, [stacked], [(0, k) for k in range(N_DEV)], runs,
                  out_units=None, out_cols=2 * FFN_HIDDEN, out_dtype=BF16)


def _disassemble_dwffn(name, dw):
    runs = [(k, 0, 0, SHARD_FFN * k, SHARD_FFN) for k in range(N_DEV)]
    return _remap(name, [dw], [(0, None)], runs, out_units=N_DEV, out_cols=SHARD_FFN_PAD, out_dtype=BF16)


def _adamw(name, parts, w, m, v, *, tr=128):
    rows, cols = w.shape
    tr = min(tr, rows)
    assert rows % tr == 0, (name, rows, tr)
    c1 = 1.0 - ADAM_B1 ** ADAM_STEP
    c2 = 1.0 - ADAM_B2 ** ADAM_STEP

    def body(p_ref, w_ref, m_ref, v_ref, g_ref, d_ref, nm_ref, nv_ref):
        g = p_ref[0].astype(F32)
        for s in range(1, N_DEV):
            g = g + p_ref[s].astype(F32)
        m_new = ADAM_B1 * m_ref[...] + (1.0 - ADAM_B1) * g
        v_new = ADAM_B2 * v_ref[...] + (1.0 - ADAM_B2) * (g * g)
        upd = (m_new / c1) / (jnp.sqrt(v_new / c2) + ADAM_EPS) + ADAM_WD * w_ref[...]
        g_ref[...] = g
        d_ref[...] = -ADAM_LR * upd
        nm_ref[...] = m_new
        nv_ref[...] = v_new

    row = pl.BlockSpec((tr, cols), lambda i: (i, 0))
    out = jax.ShapeDtypeStruct((rows, cols), F32)
    return pl.pallas_call(
        body, name=name, grid=(rows // tr,),
        in_specs=[pl.BlockSpec((N_DEV, tr, cols), lambda i: (0, i, 0)), row, row, row],
        out_specs=(row, row, row, row), out_shape=(out, out, out, out),
        compiler_params=_cparams(("parallel",)),
    )(parts, w, m, v)


_WEIGHTS = ("norm_mix", "w_in", "b_forget", "lam_re", "lam_im", "log_dt", "b_re", "b_im", "c_re", "c_im",
            "d_skip", "w_glu", "w_fox_o", "w_mix_out", "norm_mem_q", "norm_mem_kv", "w_mem_q", "w_mem_kv",
            "w_mem_o", "norm_ffn", "w_ffn_in", "w_ffn_out", "norm_final")
_SHARDED = ("w_in", "w_glu", "w_fox_o", "w_mix_out", "w_mem_q", "w_mem_kv", "w_mem_o", "w_ffn_in", "w_ffn_out")
_SMALL = tuple(n for n in _WEIGHTS if n not in _SHARDED)
_PACK_COLS = 1024


def _pack(arrays):
    flat = jnp.concatenate([a.reshape(-1).astype(F32) for a in arrays])
    rows = -(-flat.shape[0] // _PACK_COLS)
    return jnp.pad(flat, (0, rows * _PACK_COLS - flat.shape[0])).reshape(rows, _PACK_COLS)


def _unpack(buf, like):
    flat = buf.reshape(-1)
    out, pos = [], 0
    for a in like:
        out.append(flat[pos:pos + a.size].reshape(a.shape))
        pos += a.size
    return out


def _mm(name, a, b, mode, m, n, k, out_dtype, tm=1024, tn=512, tk=1024, **kw):
    return _matmul(name, a, b, mode, m, n, k, out_dtype=out_dtype, tm=tm, tn=tn, tk=tk, **kw)


def kernel(x, mem, norm_mix, w_in, b_forget, lam_re, lam_im, log_dt, b_re, b_im, c_re, c_im, d_skip, w_glu, w_fox_o, w_mix_out, norm_mem_q, norm_mem_kv, w_mem_q, w_mem_kv, w_mem_o, norm_ffn, w_ffn_in, w_ffn_out, norm_final, loss_target, m_norm_mix, m_w_in, m_b_forget, m_lam_re, m_lam_im, m_log_dt, m_b_re, m_b_im, m_c_re, m_c_im, m_d_skip, m_w_glu, m_w_fox_o, m_w_mix_out, m_norm_mem_q, m_norm_mem_kv, m_w_mem_q, m_w_mem_kv, m_w_mem_o, m_norm_ffn, m_w_ffn_in, m_w_ffn_out, m_norm_final, v_norm_mix, v_w_in, v_b_forget, v_lam_re, v_lam_im, v_log_dt, v_b_re, v_b_im, v_c_re, v_c_im, v_d_skip, v_w_glu, v_w_fox_o, v_w_mix_out, v_norm_mem_q, v_norm_mem_kv, v_w_mem_q, v_w_mem_kv, v_w_mem_o, v_norm_ffn, v_w_ffn_in, v_w_ffn_out, v_norm_final):
    given = dict(locals())
    weights = {n: given[n] for n in _WEIGHTS}
    mom_m = {n: given["m_" + n] for n in _WEIGHTS}
    mom_v = {n: given["v_" + n] for n in _WEIGHTS}
    seq = x.shape[1]
    nc = seq // SSM_CHUNK
    d = D_MODEL
    xs, mems, tgt = x[0], mem[0], loss_target[0]

    def padcols(a, width):
        return jnp.pad(a, ((0, 0), (0, width - a.shape[1])))

    shards = [padcols(w_in[0].astype(BF16), SHARD_IN_PAD), w_glu[0].astype(BF16), w_fox_o[0].astype(BF16),
              w_mix_out[0].astype(BF16), w_mem_q[0].astype(BF16), w_mem_kv[0].astype(BF16),
              w_mem_o[0].astype(BF16), padcols(w_ffn_in[0].astype(BF16), SHARD_FFN_PAD), w_ffn_out[0].astype(BF16)]
    gathered = _exchange("gather_weights", shards, scatter=False)
    win = _assemble_win("assemble_w_in", gathered[0])
    wglu_s, wfoxo_s, wmo_s = gathered[1], gathered[2], gathered[6]
    wmix = gathered[3].reshape(d, d)
    wmq = gathered[4].reshape(d, MEM_WIDTH)
    wmkv = gathered[5].reshape(d, 2 * MEM_WIDTH)
    wffn_in = _assemble_wffn("assemble_w_ffn_in", gathered[7])
    wffn_out = gathered[8].reshape(FFN_HIDDEN, d)

    u = _rms_fwd("rms_mix", xs, norm_mix)
    ussm = _mm("proj_ssm", u, win, "nn", seq, SSM_WIDTH, d, F32)
    qkv = _mm("proj_qkv", u, win, "nn", seq, 3 * FOX_WIDTH, d, BF16, b_off=(0, SSM_WIDTH))
    gates = _mm("proj_gates", u, win, "nn", seq, 2 * d, d, F32, b_off=(0, PROJ_GATE0))
    fproj = _mm("proj_forget", u, win, "nn", seq, LANE, d, F32, tn=LANE, b_off=(0, PROJ_F0))

    ssm_params = (lam_re[0], lam_im[0], log_dt[0], b_re[0], b_im[0], c_re[0], c_im[0])
    (m_mat, bw_mat, cm_mat, a8, aseg), mats_vjp = jax.vjp(lambda *p: _ssm_mats(*p, nc), *ssm_params)
    m_b, bw_b, cm_b = m_mat.astype(BF16), bw_mat.astype(BF16), cm_mat.astype(BF16)
    u8 = ussm.reshape(nc, SSM_CHUNK * SSM_WIDTH)
    d8 = jnp.tile(d_skip, (1, SSM_CHUNK))
    w4 = _ssm_w("ssm_w", u8, bw_b)
    sp4 = _ssm_scan("ssm_scan", w4, a8, aseg, reverse=False)
    y8 = _ssm_y("ssm_y", u8, sp4, m_b, cm_b)
    act = _ssm_post_fwd("ssm_act", y8, u8, d8).reshape(seq, SSM_WIDTH)
    glu = _mm("glu", act, wglu_s, "nn", seq, 2 * d, SSM_WIDTH, F32, tn=256, b_stack="n")

    bcol = jnp.pad(b_forget[0], (0, LANE - FOX_HEADS)).reshape(LANE, 1)
    cum_t = _fox_cum("fox_cum", fproj, bcol).reshape(FOX_HEADS // 2, 2, seq)
    att, lse = _fox_fwd("fox_fwd", qkv, cum_t)
    out_b = _mm("fox_out", att, wfoxo_s, "nn", seq, d, FOX_WIDTH, F32, tn=LANE, b_stack="n")

    mixin = _mix_fwd("mix", glu, gates, out_b)
    h1 = _mm("mix_out", mixin, wmix, "nn", seq, d, d, F32, add=xs)

    n1 = _rms_fwd("rms_mem_q", h1, norm_mem_q)
    q2 = _mm("mem_q", n1, wmq, "nn", seq, MEM_WIDTH, d, BF16)
    mn = _rms_fwd("rms_mem_kv", mems, norm_mem_kv)
    mlen = mems.shape[0]
    kv = _mm("mem_kv", mn, wmkv, "nn", mlen, 2 * MEM_WIDTH, d, BF16)
    o2 = _mem_fwd("mem_attn", q2, kv)
    h2 = _mm("mem_out", o2, wmo_s, "nn", seq, d, MEM_WIDTH, F32, tn=LANE, b_stack="n", add=h1)

    n2 = _rms_fwd("rms_ffn", h2, norm_ffn)
    f = _mm("ffn_in", n2, wffn_in, "nn", seq, 2 * FFN_HIDDEN, d, F32)
    g_act = _swiglu_fwd("swiglu", f)
    h3 = _mm("ffn_out", g_act, wffn_out, "nn", seq, d, FFN_HIDDEN, F32, tk=FFN_HIDDEN, add=h2)
    loss_part, dh3, dg_final = _final_loss("final_loss", h3, tgt, norm_final.reshape(1, d))

    dg_act = _mm("d_ffn_out_x", dh3, wffn_out, "nt", seq, FFN_HIDDEN, d, F32, tn=256)
    dwffn_out = _mm("d_ffn_out_w", g_act, dh3, "tn", FFN_HIDDEN, d, seq, BF16, tm=256, tn=1024)
    df = _swiglu_bwd("d_swiglu", dg_act, f)
    dn2 = _mm("d_ffn_in_x", df, wffn_in, "nt", seq, d, 2 * FFN_HIDDEN, F32, tk=FFN_HIDDEN)
    dwffn_in = _mm("d_ffn_in_w", n2, df, "tn", d, 2 * FFN_HIDDEN, seq, BF16, tm=512)
    dh2, dg_ffn = _rms_bwd("d_rms_ffn", dn2, h2, norm_ffn, res=dh3)

    do2 = _mm("d_mem_out_x", dh2, wmo_s, "nt", seq, MEM_WIDTH, d, F32, tk=LANE, b_stack="k")
    dwmo = _mm("d_mem_out_w", o2, dh2, "tn", MEM_WIDTH, d, seq, BF16, tm=512, tn=LANE, out_stack=True)
    dq2, dkv = _mem_bwd("d_mem_attn", q2, kv, do2)
    dwmq = _mm("d_mem_q_w", n1, dq2, "tn", d, MEM_WIDTH, seq, BF16, tm=512)
    dn1 = _mm("d_mem_q_x", dq2, wmq, "nt", seq, d, MEM_WIDTH, F32)
    dwmkv = _mm("d_mem_kv_w", mn, dkv, "tn", d, 2 * MEM_WIDTH, mlen, BF16, tm=512)
    dmn = _mm("d_mem_kv_x", dkv, wmkv, "nt", mlen, d, 2 * MEM_WIDTH, F32)
    _, dg_memkv = _rms_bwd("d_rms_mem_kv", dmn, mems, norm_mem_kv)
    dh1, dg_memq = _rms_bwd("d_rms_mem_q", dn1, h1, norm_mem_q, res=dh2)

    dmixin = _mm("d_mix_out_x", dh1, wmix, "nt", seq, d, d, F32)
    dwmix = _mm("d_mix_out_w", mixin, dh1, "tn", d, d, seq, BF16, tm=512)
    dglu, dgates, dout_b = _mix_bwd("d_mix", dmixin, glu, gates, out_b)
    datt = _mm("d_fox_out_x", dout_b, wfoxo_s, "nt", seq, FOX_WIDTH, d, F32, tk=LANE, b_stack="k")
    dwfoxo = _mm("d_fox_out_w", att, dout_b, "tn", FOX_WIDTH, d, seq, BF16, tm=512, tn=LANE, out_stack=True)
    dact = _mm("d_glu_x", dglu, wglu_s, "nt", seq, SSM_WIDTH, 2 * d, F32, tk=256, b_stack="k")
    dwglu = _mm("d_glu_w", act, dglu, "tn", SSM_WIDTH, 2 * d, seq, BF16, tm=512, tn=256, out_stack=True)

    dz8, dg_dskip = _ssm_post_bwd("d_ssm_act", dact.reshape(nc, SSM_CHUNK * SSM_WIDTH), y8, u8, d8)
    ds4, dcm = _ssm_ds("d_ssm_y_state", dz8, sp4, cm_b)
    g4, da8 = _ssm_scan("d_ssm_scan", ds4, a8, aseg, reverse=True, sprev4=sp4)
    dx8, dm, dbw = _ssm_dx("d_ssm_x", dz8, g4, u8, m_b, bw_b, d8)
    dussm = dx8.reshape(seq, SSM_WIDTH)
    g_ssm = mats_vjp((dm, dbw, dcm, da8, jnp.zeros_like(aseg)))

    dq, dk, dv, dcum = _fox_bwd("d_fox", qkv, cum_t, att, datt, lse)
    dfproj, dbf = _fox_cum_bwd("d_fox_cum", dcum.reshape(FOX_HEADS, seq), fproj, bcol)
    dg_bforget = dbf[0:FOX_HEADS, 0].reshape(1, FOX_HEADS)

    dparts = (dussm, dq, dk, dv, dgates, dfproj)
    du = None
    dw_parts = []
    for i, (part, (lo, hi)) in enumerate(zip(dparts, _DPROJ_SEGS)):
        width = hi - lo
        tk = min(width, 1024)
        du = _mm(f"d_proj_x{i}", part, win, "nt", seq, d, width, F32, tk=tk, b_off=(0, lo), add=du)
        dw_parts.append(_mm(f"d_proj_w{i}", u, part, "tn", d, width, seq, BF16, tm=512, tn=min(width, 512)))
    dx, dg_mix = _rms_bwd("d_rms_mix", du, xs, norm_mix, res=dh1)

    pieces = [_disassemble_dwin("split_d_w_in", dw_parts), dwglu, dwfoxo,
              dwmix.reshape(N_DEV, d // N_DEV, d), dwmq.reshape(N_DEV, d // N_DEV, MEM_WIDTH),
              dwmkv.reshape(N_DEV, d // N_DEV, 2 * MEM_WIDTH), dwmo,
              _disassemble_dwffn("split_d_w_ffn_in", dwffn_in),
              dwffn_out.reshape(N_DEV, FFN_HIDDEN // N_DEV, d)]
    received = _exchange("scatter_grads", pieces, scatter=True)

    small_grads = dict(zip(
        _SMALL, (dg_mix, dg_bforget, g_ssm[0][None], g_ssm[1][None], g_ssm[2][None], g_ssm[3][None], g_ssm[4][None],
                 g_ssm[5][None], g_ssm[6][None], dg_dskip, dg_memq, dg_memkv, dg_ffn, dg_final.reshape(d))))
    small_like = [weights[n] for n in _SMALL]
    small_all = _exchange("gather_small_grads", [_pack([small_grads[n] for n in _SMALL])], scatter=False)[0]
    pk = [_pack([src[n] for n in _SMALL]) for src in (weights, mom_m, mom_v)]
    small_out = _adamw("adamw_small", small_all, pk[0], pk[1], pk[2], tr=small_all.shape[1])
    small_res = [dict(zip(_SMALL, _unpack(buf, small_like))) for buf in small_out]

    results = [dict(r) for r in small_res]
    tiles = {"w_in": 128, "w_glu": 128, "w_fox_o": 128, "w_mix_out": 128, "w_mem_q": 128, "w_mem_kv": 128,
             "w_mem_o": 128, "w_ffn_in": 128, "w_ffn_out": 176}
    pads = {"w_in": SHARD_IN_PAD, "w_ffn_in": SHARD_FFN_PAD}
    for name, parts in zip(_SHARDED, received):
        w2, m2, v2 = weights[name][0], mom_m[name][0], mom_v[name][0]
        cols = w2.shape[1]
        if name in pads:
            w2, m2, v2 = (padcols(t, pads[name]) for t in (w2, m2, v2))
        outs = _adamw("adamw_" + name, parts, w2, m2, v2, tr=tiles[name])
        for res, o in zip(results, outs):
            res[name] = o[:, :cols][None]

    loss = lax.psum(loss_part[0, 0], ("x", "y", "c"))
    out = [loss, dx[None]]
    for res in results:
        out.extend(res[n] for n in _WEIGHTS)
    return tuple(out)
```

```python
import math

import jax
import jax.numpy as jnp
import numpy as np
from jax import lax
from jax.experimental import pallas as pl
from jax.experimental.pallas import tpu as pltpu

F32 = jnp.float32
BF16 = jnp.bfloat16

N_DEV = 8
LANE = 128
VMEM_LIMIT = 56 * 1024 * 1024

D_MODEL = 1024
SSM_GROUP = 16
SSM_GROUPS = 32
SSM_WIDTH = 512
SSM_STATE = 64
SSM_CHUNK = 8
FOX_HEADS = 8
FOX_HEAD_DIM = 64
FOX_WIDTH = 512
MEM_HEADS = 4
MEM_HEAD_DIM = 128
MEM_WIDTH = 512
FFN_HIDDEN = 2816
RMS_EPS = 1e-6
IN_WIDTH = 4104
SHARD_IN = IN_WIDTH // N_DEV
SHARD_IN_PAD = 640
SHARD_FFN = 2 * FFN_HIDDEN // N_DEV
SHARD_FFN_PAD = 768
PROJ_GATE0 = 2048
PROJ_F0 = 4096
PROJ_WIDTH = 4224

ADAM_LR = 0.001
ADAM_B1 = 0.9
ADAM_B2 = 0.999
ADAM_EPS = 1e-08
ADAM_WD = 0.01
ADAM_STEP = 10


def _cparams(sem=None):
    return pltpu.CompilerParams(dimension_semantics=sem, vmem_limit_bytes=VMEM_LIMIT)


def _sigmoid(x):
    return 1.0 / (1.0 + jnp.exp(-x))


def _bdot(a, b, dims):
    return lax.dot_general(a.astype(BF16), b.astype(BF16), ((dims[0], dims[1]), ((), ())),
                           preferred_element_type=F32)


_DIMS = {"nn": ((1,), (0,)), "nt": ((1,), (1,)), "tn": ((0,), (0,))}


def _matmul(name, a, b, mode, m, n, k, *, out_dtype, tm, tn, tk, a_off=(0, 0), b_off=(0, 0),
            b_stack=None, out_stack=None, add=None):
    tm, tn, tk = min(tm, m), min(tn, n), min(tk, k)
    assert m % tm == 0 and n % tn == 0 and k % tk == 0, (name, m, n, k, tm, tn, tk)
    nk = k // tk
    grid = (m // tm, n // tn, nk)

    def blk(off, t):
        assert off % t == 0, (name, off, t)
        return off // t

    if mode in ("nn", "nt"):
        ar, ac = blk(a_off[0], tm), blk(a_off[1], tk)
        a_spec = pl.BlockSpec((tm, tk), lambda i, j, kk: (i + ar, kk + ac))
    else:
        ar, ac = blk(a_off[0], tk), blk(a_off[1], tm)
        a_spec = pl.BlockSpec((tk, tm), lambda i, j, kk: (kk + ar, i + ac))

    if b_stack is None:
        if mode in ("nn", "tn"):
            br, bc = blk(b_off[0], tk), blk(b_off[1], tn)
            b_spec = pl.BlockSpec((tk, tn), lambda i, j, kk: (kk + br, j + bc))
        else:
            br, bc = blk(b_off[0], tn), blk(b_off[1], tk)
            b_spec = pl.BlockSpec((tn, tk), lambda i, j, kk: (j + br, kk + bc))
    elif b_stack == "n":
        assert mode == "nn"
        per = (n // N_DEV) // tn
        assert per >= 1 and (n // N_DEV) % tn == 0, (name, n, tn)
        b_spec = pl.BlockSpec((None, tk, tn), lambda i, j, kk: (j // per, kk, j % per))
    else:
        assert mode == "nt" and b_stack == "k"
        per = (k // N_DEV) // tk
        assert per >= 1 and (k // N_DEV) % tk == 0, (name, k, tk)
        b_spec = pl.BlockSpec((None, tn, tk), lambda i, j, kk: (kk // per, j, kk % per))

    if out_stack:
        per_o = (n // N_DEV) // tn
        assert per_o >= 1 and (n // N_DEV) % tn == 0, (name, n, tn)
        o_spec = pl.BlockSpec((None, tm, tn), lambda i, j, kk: (j // per_o, i, j % per_o))
        out_shape = jax.ShapeDtypeStruct((N_DEV, m, n // N_DEV), out_dtype)
    else:
        o_spec = pl.BlockSpec((tm, tn), lambda i, j, kk: (i, j))
        out_shape = jax.ShapeDtypeStruct((m, n), out_dtype)

    in_specs = [a_spec, b_spec]
    operands = [a, b]
    if add is not None:
        in_specs.append(pl.BlockSpec((tm, tn), lambda i, j, kk: (i, j)))
        operands.append(add)
    dims = _DIMS[mode]
    has_add = add is not None

    def body(*refs):
        a_ref, b_ref = refs[0], refs[1]
        add_ref = refs[2] if has_add else None
        o_ref = refs[3] if has_add else refs[2]
        acc_ref = refs[-1] if nk > 1 else None
        prod = _bdot(a_ref[...], b_ref[...], dims)

        def finish(total):
            if has_add:
                total = total + add_ref[...].astype(F32)
            o_ref[...] = total.astype(o_ref.dtype)

        if nk == 1:
            finish(prod)
        else:
            kk = pl.program_id(2)

            @pl.when(kk == 0)
            def _():
                acc_ref[...] = prod

            @pl.when(jnp.logical_and(kk > 0, kk < nk - 1))
            def _():
                acc_ref[...] += prod

            @pl.when(kk == nk - 1)
            def _():
                finish(acc_ref[...] + prod)

    scratch = [pltpu.VMEM((tm, tn), F32)] if nk > 1 else []
    return pl.pallas_call(
        body, name=name, grid=grid, in_specs=in_specs, out_specs=o_spec, out_shape=out_shape,
        scratch_shapes=scratch,
        compiler_params=_cparams(("parallel", "parallel", "arbitrary")),
    )(*operands)


def _rms_fwd(name, x, gain, *, tr=512):
    r, d = x.shape
    tr = min(tr, r)

    def body(x_ref, g_ref, o_ref):
        xv = x_ref[...]
        rstd = lax.rsqrt(jnp.mean(xv * xv, axis=-1, keepdims=True) + RMS_EPS)
        o_ref[...] = (xv * rstd * g_ref[...]).astype(o_ref.dtype)

    return pl.pallas_call(
        body, name=name, grid=(r // tr,),
        in_specs=[pl.BlockSpec((tr, d), lambda i: (i, 0)), pl.BlockSpec((1, d), lambda i: (0, 0))],
        out_specs=pl.BlockSpec((tr, d), lambda i: (i, 0)),
        out_shape=jax.ShapeDtypeStruct((r, d), BF16),
        compiler_params=_cparams(("parallel",)),
    )(x, gain)


def _rms_bwd(name, dy, x, gain, res=None, *, tr=512):
    r, d = x.shape
    tr = min(tr, r)
    n = r // tr
    has_res = res is not None

    def body(*refs):
        dy_ref, x_ref, g_ref = refs[:3]
        res_ref = refs[3] if has_res else None
        dx_ref, dg_ref, acc_ref = refs[-3:]
        i = pl.program_id(0)
        xv = x_ref[...]
        rstd = lax.rsqrt(jnp.mean(xv * xv, axis=-1, keepdims=True) + RMS_EPS)
        xh = xv * rstd
        dyv = dy_ref[...].astype(F32)
        dxh = dyv * g_ref[...]
        dx = rstd * (dxh - xh * jnp.mean(dxh * xh, axis=-1, keepdims=True))
        if has_res:
            dx = dx + res_ref[...]
        dx_ref[...] = dx
        part = (dyv * xh).reshape(tr // 8, 8, d).sum(axis=0)

        @pl.when(i == 0)
        def _():
            acc_ref[...] = part

        @pl.when(i > 0)
        def _():
            acc_ref[...] += part

        @pl.when(i == n - 1)
        def _():
            dg_ref[...] = jnp.sum(acc_ref[...], axis=0, keepdims=True)

    row = pl.BlockSpec((tr, d), lambda i: (i, 0))
    in_specs = [row, row, pl.BlockSpec((1, d), lambda i: (0, 0))] + ([row] if has_res else [])
    ops = [dy, x, gain] + ([res] if has_res else [])
    return pl.pallas_call(
        body, name=name, grid=(n,), in_specs=in_specs,
        out_specs=(row, pl.BlockSpec((1, d), lambda i: (0, 0))),
        out_shape=(jax.ShapeDtypeStruct((r, d), F32), jax.ShapeDtypeStruct((1, d), F32)),
        scratch_shapes=[pltpu.VMEM((8, d), F32)],
        compiler_params=_cparams(("arbitrary",)),
    )(*ops)


def _final_loss(name, h, target, gain, *, tr=512):
    r, d = h.shape
    tr = min(tr, r)
    n = r // tr

    def body(h_ref, t_ref, g_ref, loss_ref, dh_ref, dg_ref, accl_ref, accg_ref):
        i = pl.program_id(0)
        xv = h_ref[...]
        rstd = lax.rsqrt(jnp.mean(xv * xv, axis=-1, keepdims=True) + RMS_EPS)
        xh = xv * rstd
        e = xh * g_ref[...] - t_ref[...]
        dyv = e * (1.0 / d)
        dxh = dyv * g_ref[...]
        dh_ref[...] = rstd * (dxh - xh * jnp.mean(dxh * xh, axis=-1, keepdims=True))
        lpart = (e * e).reshape(tr // 8, 8, d).sum(axis=0)
        gpart = (dyv * xh).reshape(tr // 8, 8, d).sum(axis=0)

        @pl.when(i == 0)
        def _():
            accl_ref[...] = lpart
            accg_ref[...] = gpart

        @pl.when(i > 0)
        def _():
            accl_ref[...] += lpart
            accg_ref[...] += gpart

        @pl.when(i == n - 1)
        def _():
            tot = jnp.sum(jnp.sum(accl_ref[...], axis=0, keepdims=True), axis=1, keepdims=True)
            loss_ref[...] = jnp.broadcast_to(tot * (0.5 / d), (1, LANE))
            dg_ref[...] = jnp.sum(accg_ref[...], axis=0, keepdims=True)

    row = pl.BlockSpec((tr, d), lambda i: (i, 0))
    one = pl.BlockSpec((1, d), lambda i: (0, 0))
    return pl.pallas_call(
        body, name=name, grid=(n,), in_specs=[row, row, one],
        out_specs=(pl.BlockSpec((1, LANE), lambda i: (0, 0)), row, one),
        out_shape=(jax.ShapeDtypeStruct((1, LANE), F32), jax.ShapeDtypeStruct((r, d), F32),
                   jax.ShapeDtypeStruct((1, d), F32)),
        scratch_shapes=[pltpu.VMEM((8, d), F32), pltpu.VMEM((8, d), F32)],
        compiler_params=_cparams(("arbitrary",)),
    )(h, target, gain)


_GELU_C = math.sqrt(2.0 / math.pi)


def _gelu_parts(z):
    inner = _GELU_C * (z + 0.044715 * z * z * z)
    t = jnp.tanh(inner)
    val = 0.5 * z * (1.0 + t)
    dinner = _GELU_C * (1.0 + 3.0 * 0.044715 * z * z)
    grad = 0.5 * (1.0 + t) + 0.5 * z * (1.0 - t * t) * dinner
    return val, grad


def _ssm_post_fwd(name, y8, u8, d8, *, tr=256):
    r, c = y8.shape
    tr = min(tr, r)

    def body(y_ref, u_ref, d_ref, o_ref):
        z = y_ref[...] + d_ref[...] * u_ref[...]
        o_ref[...] = _gelu_parts(z)[0].astype(o_ref.dtype)

    row = pl.BlockSpec((tr, c), lambda i: (i, 0))
    return pl.pallas_call(
        body, name=name, grid=(r // tr,), in_specs=[row, row, pl.BlockSpec((1, c), lambda i: (0, 0))],
        out_specs=row, out_shape=jax.ShapeDtypeStruct((r, c), BF16),
        compiler_params=_cparams(("parallel",)),
    )(y8, u8, d8)


def _ssm_post_bwd(name, dact8, y8, u8, d8, *, tr=256):
    r, c = y8.shape
    tr = min(tr, r)
    n = r // tr

    def body(da_ref, y_ref, u_ref, d_ref, dz_ref, dd_ref, acc_ref):
        i = pl.program_id(0)
        uv = u_ref[...]
        z = y_ref[...] + d_ref[...] * uv
        dz = da_ref[...].astype(F32) * _gelu_parts(z)[1]
        dz_ref[...] = dz
        part = (dz * uv).reshape(tr // 8, 8, c).sum(axis=0)

        @pl.when(i == 0)
        def _():
            acc_ref[...] = part

        @pl.when(i > 0)
        def _():
            acc_ref[...] += part

        @pl.when(i == n - 1)
        def _():
            tot = jnp.sum(acc_ref[...], axis=0, keepdims=True)
            out = tot[:, 0:SSM_WIDTH]
            for j in range(1, c // SSM_WIDTH):
                out = out + tot[:, j * SSM_WIDTH:(j + 1) * SSM_WIDTH]
            dd_ref[...] = out

    row = pl.BlockSpec((tr, c), lambda i: (i, 0))
    return pl.pallas_call(
        body, name=name, grid=(n,), in_specs=[row, row, row, pl.BlockSpec((1, c), lambda i: (0, 0))],
        out_specs=(row, pl.BlockSpec((1, SSM_WIDTH), lambda i: (0, 0))),
        out_shape=(jax.ShapeDtypeStruct((r, c), F32), jax.ShapeDtypeStruct((1, SSM_WIDTH), F32)),
        scratch_shapes=[pltpu.VMEM((8, c), F32)],
        compiler_params=_cparams(("arbitrary",)),
    )(dact8, y8, u8, d8)


def _mix_fwd(name, glu, gates, out_b, *, tr=256):
    r = glu.shape[0]
    d = D_MODEL
    tr = min(tr, r)

    def body(glu_ref, gate_ref, ob_ref, o_ref):
        out_a = glu_ref[:, 0:d] * _sigmoid(glu_ref[:, d:2 * d])
        mix = _sigmoid(gate_ref[:, 0:d]) * out_a + _sigmoid(gate_ref[:, d:2 * d]) * ob_ref[...]
        o_ref[...] = mix.astype(o_ref.dtype)

    wide = pl.BlockSpec((tr, 2 * d), lambda i: (i, 0))
    row = pl.BlockSpec((tr, d), lambda i: (i, 0))
    return pl.pallas_call(
        body, name=name, grid=(r // tr,), in_specs=[wide, wide, row], out_specs=row,
        out_shape=jax.ShapeDtypeStruct((r, d), BF16), compiler_params=_cparams(("parallel",)),
    )(glu, gates, out_b)


def _mix_bwd(name, dmix, glu, gates, out_b, *, tr=256):
    r = glu.shape[0]
    d = D_MODEL
    tr = min(tr, r)

    def body(dm_ref, glu_ref, gate_ref, ob_ref, dglu_ref, dgate_ref, dob_ref):
        dm = dm_ref[...]
        glu_a = glu_ref[:, 0:d]
        sb = _sigmoid(glu_ref[:, d:2 * d])
        ga = _sigmoid(gate_ref[:, 0:d])
        gb = _sigmoid(gate_ref[:, d:2 * d])
        out_a = glu_a * sb
        dout_a = dm * ga
        dglu_ref[:, 0:d] = (dout_a * sb).astype(dglu_ref.dtype)
        dglu_ref[:, d:2 * d] = (dout_a * glu_a * sb * (1.0 - sb)).astype(dglu_ref.dtype)
        dgate_ref[:, 0:d] = (dm * out_a * ga * (1.0 - ga)).astype(dgate_ref.dtype)
        dgate_ref[:, d:2 * d] = (dm * ob_ref[...] * gb * (1.0 - gb)).astype(dgate_ref.dtype)
        dob_ref[...] = (dm * gb).astype(dob_ref.dtype)

    wide = pl.BlockSpec((tr, 2 * d), lambda i: (i, 0))
    row = pl.BlockSpec((tr, d), lambda i: (i, 0))
    return pl.pallas_call(
        body, name=name, grid=(r // tr,), in_specs=[row, wide, wide, row], out_specs=(wide, wide, row),
        out_shape=(jax.ShapeDtypeStruct((r, 2 * d), BF16), jax.ShapeDtypeStruct((r, 2 * d), BF16),
                   jax.ShapeDtypeStruct((r, d), BF16)),
        compiler_params=_cparams(("parallel",)),
    )(dmix, glu, gates, out_b)


def _swiglu_fwd(name, f, *, tr=256):
    r = f.shape[0]
    hdn = FFN_HIDDEN
    tr = min(tr, r)

    def body(f_ref, o_ref):
        fa = f_ref[:, 0:hdn]
        o_ref[...] = (fa * _sigmoid(fa) * f_ref[:, hdn:2 * hdn]).astype(o_ref.dtype)

    return pl.pallas_call(
        body, name=name, grid=(r // tr,), in_specs=[pl.BlockSpec((tr, 2 * hdn), lambda i: (i, 0))],
        out_specs=pl.BlockSpec((tr, hdn), lambda i: (i, 0)),
        out_shape=jax.ShapeDtypeStruct((r, hdn), BF16), compiler_params=_cparams(("parallel",)),
    )(f)


def _swiglu_bwd(name, dg, f, *, tr=256):
    r = f.shape[0]
    hdn = FFN_HIDDEN
    tr = min(tr, r)

    def body(dg_ref, f_ref, o_ref):
        dgv = dg_ref[...].astype(F32)
        fa = f_ref[:, 0:hdn]
        fb = f_ref[:, hdn:2 * hdn]
        s = _sigmoid(fa)
        o_ref[:, 0:hdn] = (dgv * fb * s * (1.0 + fa * (1.0 - s))).astype(o_ref.dtype)
        o_ref[:, hdn:2 * hdn] = (dgv * fa * s).astype(o_ref.dtype)

    return pl.pallas_call(
        body, name=name, grid=(r // tr,),
        in_specs=[pl.BlockSpec((tr, hdn), lambda i: (i, 0)), pl.BlockSpec((tr, 2 * hdn), lambda i: (i, 0))],
        out_specs=pl.BlockSpec((tr, 2 * hdn), lambda i: (i, 0)),
        out_shape=jax.ShapeDtypeStruct((r, 2 * hdn), BF16), compiler_params=_cparams(("parallel",)),
    )(dg, f)


def _ssm_mats(lam_re, lam_im, log_dt, b_re, b_im, c_re, c_im, nc):
    hp = lax.Precision.HIGHEST
    t = SSM_CHUNK
    nq = SSM_GROUPS // 8
    lam = lax.complex(lam_re, lam_im)
    z = lam * jnp.exp(log_dt)[:, None]
    ks = jnp.arange(t + 1, dtype=F32)
    apow = jnp.exp(ks[:, None, None] * z[None])
    bbar = ((apow[1] - 1.0) / lam)[..., None] * lax.complex(b_re, b_im)
    c = lax.complex(c_re, c_im)
    eye = jnp.eye(8, dtype=F32)

    ca = c[None] * apow[:, :, None, :]
    kmat = jnp.einsum("kgnp,gpm->kgnm", ca, bbar, precision=hp).real
    ii = np.arange(t)
    lag = ii[None, :] - ii[:, None]
    kt = kmat[np.clip(lag, 0, t)] * jnp.asarray(lag >= 0, F32)[:, :, None, None, None]
    kt = kt.reshape(t, t, nq, 8, SSM_GROUP, SSM_GROUP)
    m_mat = jnp.einsum("ijqgnm,gh->qigmjhn", kt, eye).reshape(nq, 1024, 1024)

    arev = jnp.exp((float(t - 1) - ks[:t])[:, None, None] * z[None])
    w = arev[:, :, :, None] * bbar[None]
    wr = jnp.stack([w.real, w.imag]).reshape(2, t, nq, 8, SSM_STATE, SSM_GROUP)
    bw_mat = jnp.einsum("riqgpm,gh->qigmrhp", wr, eye).reshape(nq, 1024, 1024)

    ca1 = ca[1:]
    cr = jnp.stack([ca1.real, -ca1.imag]).reshape(2, t, nq, 8, SSM_GROUP, SSM_STATE)
    cm_mat = jnp.einsum("rjqgnp,gh->qrgpjhn", cr, eye).reshape(nq, 1024, 1024)

    def tiles(v):
        vq = jnp.concatenate([v.real.reshape(nq, 512), v.imag.reshape(nq, 512)], axis=1)
        return jnp.broadcast_to(vq.reshape(nq, 8, 1, LANE), (nq, 8, 8, LANE))

    return m_mat, bw_mat, cm_mat, tiles(apow[t]), tiles(jnp.exp(float(nc) * z))


def _x_tile_specs(nc, nq):
    return [pl.BlockSpec((nc, LANE), lambda q, t, i=i: (0, i * nq + q)) for i in range(SSM_CHUNK)]


def _cat_tiles(refs):
    return jnp.concatenate([r[...] for r in refs], axis=1)


def _ssm_w(name, x8, bw):
    nc = x8.shape[0]
    nq = bw.shape[0]

    def body(*refs):
        xq = _cat_tiles(refs[:8])
        refs[9][...] = _bdot(xq, refs[8][...], _DIMS["nn"])

    return pl.pallas_call(
        body, name=name, grid=(nq, 8),
        in_specs=_x_tile_specs(nc, nq) + [pl.BlockSpec((None, 1024, LANE), lambda q, t: (q, 0, t))],
        out_specs=pl.BlockSpec((None, None, nc, LANE), lambda q, t: (q, t, 0, 0)),
        out_shape=jax.ShapeDtypeStruct((nq, 8, nc, LANE), F32),
        compiler_params=_cparams(("parallel", "arbitrary")),
    )(*([x8] * 8), bw)


def _ssm_scan(name, w4, a_t, aseg_t, *, reverse, sprev4=None):
    nq, _, nc, _ = w4.shape
    ns = nc // 8
    with_da = sprev4 is not None

    def body(*refs):
        w_ref, a_ref, aseg_ref = refs[:3]
        s_ref = refs[3] if with_da else None
        o_ref = refs[4] if with_da else refs[3]
        da_ref = refs[5] if with_da else None
        sgn = -1.0 if reverse else 1.0
        ar = [a_ref[j] for j in range(4)]
        ai = [sgn * a_ref[j + 4] for j in range(4)]
        gr = [aseg_ref[j] for j in range(4)]
        gi = [sgn * aseg_ref[j + 4] for j in range(4)]
        zero = tuple(jnp.zeros((8, LANE), F32) for _ in range(8))

        def rows(tt):
            return pl.ds((ns - 1 - tt) if reverse else tt, 8, stride=ns)

        def step(carry, w):
            new_r = [ar[j] * carry[j] - ai[j] * carry[j + 4] + w[j] for j in range(4)]
            new_i = [ar[j] * carry[j + 4] + ai[j] * carry[j] + w[j + 4] for j in range(4)]
            return tuple(new_r + new_i)

        def pass1(tt, carry):
            return step(carry, [w_ref[j, rows(tt), :] for j in range(8)])

        ends = lax.fori_loop(0, ns, pass1, zero)
        sub = lax.broadcasted_iota(jnp.int32, (8, LANE), 0)
        init = list(zero)
        order = range(7, 0, -1) if reverse else range(0, 7)
        for s in order:
            nxt = s - 1 if reverse else s + 1
            cand_r = [gr[j] * init[j] - gi[j] * init[j + 4] + ends[j] for j in range(4)]
            cand_i = [gr[j] * init[j + 4] + gi[j] * init[j] + ends[j + 4] for j in range(4)]
            cand = cand_r + cand_i
            shift = 7 if reverse else 1
            init = [jnp.where(sub == nxt, pltpu.roll(cand[j], shift, axis=0), init[j]) for j in range(8)]

        def pass2(tt, state):
            carry, acc = state
            r = rows(tt)
            for j in range(8):
                o_ref[j, r, :] = carry[j]
            if with_da:
                sp = [s_ref[j, r, :] for j in range(8)]
                acc_r = [acc[j] + carry[j] * sp[j] + carry[j + 4] * sp[j + 4] for j in range(4)]
                acc_i = [acc[j + 4] + carry[j + 4] * sp[j] - carry[j] * sp[j + 4] for j in range(4)]
                acc = tuple(acc_r + acc_i)
            return step(carry, [w_ref[j, r, :] for j in range(8)]), acc

        _, acc = lax.fori_loop(0, ns, pass2, (tuple(init), zero))
        if with_da:
            for j in range(8):
                da_ref[j] = acc[j]

    big = pl.BlockSpec((None, 8, nc, LANE), lambda q: (q, 0, 0, 0))
    small = pl.BlockSpec((None, 8, 8, LANE), lambda q: (q, 0, 0, 0))
    in_specs = [big, small, small] + ([big] if with_da else [])
    ops = [w4, a_t, aseg_t] + ([sprev4] if with_da else [])
    out_specs = (big, small) if with_da else big
    big_s = jax.ShapeDtypeStruct((nq, 8, nc, LANE), F32)
    out_shape = (big_s, jax.ShapeDtypeStruct((nq, 8, 8, LANE), F32)) if with_da else big_s
    return pl.pallas_call(
        body, name=name, grid=(nq,), in_specs=in_specs, out_specs=out_specs, out_shape=out_shape,
        compiler_params=_cparams(("parallel",)),
    )(*ops)


def _ssm_y(name, x8, sprev4, m_mat, cm_mat):
    nc = x8.shape[0]
    nq = m_mat.shape[0]

    def body(*refs):
        xq = _cat_tiles(refs[:8])
        s_ref, m_ref, cm_ref, o_ref = refs[8:12]
        sq = jnp.concatenate([s_ref[t] for t in range(8)], axis=1)
        o_ref[...] = _bdot(xq, m_ref[...], _DIMS["nn"]) + _bdot(sq, cm_ref[...], _DIMS["nn"])

    col = pl.BlockSpec((None, 1024, LANE), lambda q, j: (q, 0, j))
    return pl.pallas_call(
        body, name=name, grid=(nq, 8),
        in_specs=_x_tile_specs(nc, nq) + [pl.BlockSpec((None, 8, nc, LANE), lambda q, j: (q, 0, 0, 0)), col, col],
        out_specs=pl.BlockSpec((nc, LANE), lambda q, j: (0, j * nq + q)),
        out_shape=jax.ShapeDtypeStruct((nc, 8 * SSM_WIDTH), F32),
        compiler_params=_cparams(("parallel", "arbitrary")),
    )(*([x8] * 8), sprev4, m_mat, cm_mat)


def _ssm_ds(name, dz8, sprev4, cm_mat):
    nc = dz8.shape[0]
    nq = cm_mat.shape[0]

    def body(*refs):
        dyq = _cat_tiles(refs[:8]).astype(BF16)
        s_ref, cm_ref, ds_ref, dcm_ref = refs[8:12]
        ds_ref[...] = _bdot(dyq, cm_ref[...], _DIMS["nt"])
        dcm_ref[...] = _bdot(s_ref[...], dyq, _DIMS["tn"])

    tile = pl.BlockSpec((None, None, nc, LANE), lambda q, t: (q, t, 0, 0))
    rowblk = pl.BlockSpec((None, LANE, 1024), lambda q, t: (q, t, 0))
    return pl.pallas_call(
        body, name=name, grid=(nq, 8),
        in_specs=_x_tile_specs(nc, nq) + [tile, rowblk],
        out_specs=(tile, rowblk),
        out_shape=(jax.ShapeDtypeStruct((nq, 8, nc, LANE), F32), jax.ShapeDtypeStruct((nq, 1024, 1024), F32)),
        compiler_params=_cparams(("parallel", "arbitrary")),
    )(*([dz8] * 8), sprev4, cm_mat)


def _ssm_dx(name, dz8, g4, x8, m_mat, bw_mat, d8):
    nc = dz8.shape[0]
    nq = m_mat.shape[0]

    def body(*refs):
        dyq = _cat_tiles(refs[:8]).astype(BF16)
        g_ref, x_ref, m_ref, bw_ref, d_ref, dzi_ref, dx_ref, dm_ref, dbw_ref = refs[8:17]
        gq = jnp.concatenate([g_ref[t] for t in range(8)], axis=1).astype(BF16)
        dx = _bdot(dyq, m_ref[...], _DIMS["nt"]) + _bdot(gq, bw_ref[...], _DIMS["nt"])
        dx_ref[...] = (dx + d_ref[...] * dzi_ref[...]).astype(dx_ref.dtype)
        xi = x_ref[...]
        dm_ref[...] = _bdot(xi, dyq, _DIMS["tn"])
        dbw_ref[...] = _bdot(xi, gq, _DIMS["tn"])

    xtile = pl.BlockSpec((nc, LANE), lambda q, i: (0, i * nq + q))
    rowblk = pl.BlockSpec((None, LANE, 1024), lambda q, i: (q, i, 0))
    return pl.pallas_call(
        body, name=name, grid=(nq, 8),
        in_specs=_x_tile_specs(nc, nq) + [pl.BlockSpec((None, 8, nc, LANE), lambda q, i: (q, 0, 0, 0)), xtile, rowblk, rowblk,
                                          pl.BlockSpec((1, LANE), lambda q, i: (0, q)), xtile],
        out_specs=(xtile, rowblk, rowblk),
        out_shape=(jax.ShapeDtypeStruct((nc, 8 * SSM_WIDTH), BF16), jax.ShapeDtypeStruct((nq, 1024, 1024), F32),
                   jax.ShapeDtypeStruct((nq, 1024, 1024), F32)),
        compiler_params=_cparams(("parallel", "arbitrary")),
    )(*([dz8] * 8), g4, x8, m_mat, bw_mat, d8, dz8)


CUM_BLK = 256


def _split3(x):
    hi = x.astype(BF16)
    r1 = x - hi.astype(F32)
    mid = r1.astype(BF16)
    lo = (r1 - mid.astype(F32)).astype(BF16)
    return hi, mid, lo


def _tri_dot(x, tri):
    hi, mid, lo = _split3(x)
    d = _DIMS["nn"]
    return _bdot(hi, tri, d) + _bdot(mid, tri, d) + _bdot(lo, tri, d)


def _tri(n, lower):
    r = lax.broadcasted_iota(jnp.int32, (n, n), 0)
    c = lax.broadcasted_iota(jnp.int32, (n, n), 1)
    return jnp.where((r >= c) if lower else (r <= c), 1.0, 0.0).astype(BF16)


def _fox_cum(name, fproj, bcol):
    seq = fproj.shape[0]
    blk = min(CUM_BLK, seq)

    def body(f_ref, b_ref, o_ref, carry_ref):
        i = pl.program_id(0)

        @pl.when(i == 0)
        def _():
            carry_ref[...] = jnp.zeros_like(carry_ref)

        z = f_ref[...].T + b_ref[...]
        logf = jnp.minimum(z, 0.0) - jnp.log(1.0 + jnp.exp(-jnp.abs(z)))
        carry = carry_ref[...]
        cum = _tri_dot(logf, _tri(blk, lower=False)) + jnp.tile(carry, (1, blk // LANE))
        o_ref[...] = cum[0:8, :]
        carry_ref[...] = carry + jnp.sum(logf, axis=1, keepdims=True)

    return pl.pallas_call(
        body, name=name, grid=(seq // blk,),
        in_specs=[pl.BlockSpec((blk, LANE), lambda i: (i, 0)), pl.BlockSpec((LANE, 1), lambda i: (0, 0))],
        out_specs=pl.BlockSpec((8, blk), lambda i: (0, i)),
        out_shape=jax.ShapeDtypeStruct((8, seq), F32),
        scratch_shapes=[pltpu.VMEM((LANE, LANE), F32)],
        compiler_params=_cparams(("arbitrary",)),
    )(fproj, bcol)


def _fox_cum_bwd(name, dcum_t, fproj, bcol):
    seq = fproj.shape[0]
    blk = min(CUM_BLK, seq)
    n = seq // blk

    def body(dc_ref, f_ref, b_ref, df_ref, db_ref, carry_ref, acc_ref):
        i = pl.program_id(0)

        @pl.when(i == 0)
        def _():
            carry_ref[...] = jnp.zeros_like(carry_ref)
            acc_ref[...] = jnp.zeros_like(acc_ref)

        dc = jnp.concatenate([dc_ref[...], jnp.zeros((LANE - 8, blk), F32)], axis=0)
        carry = carry_ref[...]
        dlogf = _tri_dot(dc, _tri(blk, lower=True)) + jnp.tile(carry, (1, blk // LANE))
        carry_ref[...] = carry + jnp.sum(dc, axis=1, keepdims=True)
        z = f_ref[...].T + b_ref[...]
        dft = dlogf / (1.0 + jnp.exp(z))
        df_ref[...] = dft.T.astype(df_ref.dtype)
        acc_ref[...] += jnp.sum(dft, axis=1, keepdims=True)

        @pl.when(i == n - 1)
        def _():
            db_ref[...] = acc_ref[...]

    return pl.pallas_call(
        body, name=name, grid=(n,),
        in_specs=[pl.BlockSpec((8, blk), lambda i: (0, n - 1 - i)), pl.BlockSpec((blk, LANE), lambda i: (n - 1 - i, 0)),
                  pl.BlockSpec((LANE, 1), lambda i: (0, 0))],
        out_specs=(pl.BlockSpec((blk, LANE), lambda i: (n - 1 - i, 0)), pl.BlockSpec((LANE, LANE), lambda i: (0, 0))),
        out_shape=(jax.ShapeDtypeStruct((seq, LANE), BF16), jax.ShapeDtypeStruct((LANE, LANE), F32)),
        scratch_shapes=[pltpu.VMEM((LANE, LANE), F32), pltpu.VMEM((LANE, LANE), F32)],
        compiler_params=_cparams(("arbitrary",)),
    )(dcum_t, fproj, bcol)


FOX_BLK = 512
FOX_SCALE = FOX_HEAD_DIM ** -0.5


def _fox_head_mask(shape, hh):
    lane = lax.broadcasted_iota(jnp.int32, shape, 1)
    return (lane < FOX_HEAD_DIM) if hh == 0 else (lane >= FOX_HEAD_DIM)


def _fox_bias(cum_ref, hh, q0, k0, blk):
    c0 = jnp.max(cum_ref[hh:hh + 1, pl.ds(q0, LANE)], axis=1, keepdims=True)
    return c0 - cum_ref[hh:hh + 1, pl.ds(k0, blk)]


def _fox_fwd(name, qkv, cum_t):
    seq = qkv.shape[0]
    blk = min(FOX_BLK, seq)
    nb = seq // blk
    npair = FOX_HEADS // 2

    def body(q_ref, k_ref, v_ref, cum_ref, o_ref, lse_ref):
        iq = pl.program_id(1)
        q0 = pl.multiple_of(iq * blk, blk)
        qv = q_ref[...]
        row = lax.broadcasted_iota(jnp.int32, (blk, blk), 0)
        col = lax.broadcasted_iota(jnp.int32, (blk, blk), 1)
        qhs = [jnp.where(_fox_head_mask(qv.shape, hh), qv, jnp.zeros_like(qv)) * FOX_SCALE for hh in range(2)]

        def block(kb, states, masked):
            k0 = pl.multiple_of(kb * blk, blk)
            kv = k_ref[pl.ds(k0, blk), :]
            vv = v_ref[pl.ds(k0, blk), :]
            new = []
            for hh in range(2):
                m, l, acc = states[hh]
                s = _bdot(qhs[hh], kv, _DIMS["nt"]) + _fox_bias(cum_ref, hh, q0, k0, blk)
                if masked:
                    s = jnp.where(row >= col, s, -jnp.inf)
                m_new = jnp.maximum(m, jnp.max(s, axis=1, keepdims=True))
                alpha = jnp.exp(m - m_new)
                p = jnp.exp(s - m_new)
                l = alpha * l + jnp.sum(p, axis=1, keepdims=True)
                acc = alpha * acc + _bdot(p, vv, _DIMS["nn"])
                new.append((m_new, l, acc))
            return tuple(new)

        init = (jnp.full((blk, 1), -jnp.inf, F32), jnp.zeros((blk, 1), F32), jnp.zeros((blk, LANE), F32))
        states = lax.fori_loop(0, iq, lambda kb, st: block(kb, st, False), (init, init))
        states = block(iq, states, True)
        outs = []
        for hh in range(2):
            m, l, acc = states[hh]
            outs.append(acc / l)
            lse_ref[hh] = jnp.broadcast_to(m + jnp.log(l), (blk, LANE))
        o_ref[...] = jnp.where(_fox_head_mask(outs[0].shape, 0), outs[0], outs[1]).astype(o_ref.dtype)

    return pl.pallas_call(
        body, name=name, grid=(npair, nb),
        in_specs=[pl.BlockSpec((blk, LANE), lambda p, i: (i, p)),
                  pl.BlockSpec((seq, LANE), lambda p, i: (0, npair + p)),
                  pl.BlockSpec((seq, LANE), lambda p, i: (0, 2 * npair + p)),
                  pl.BlockSpec((None, 2, seq), lambda p, i: (p, 0, 0))],
        out_specs=(pl.BlockSpec((blk, LANE), lambda p, i: (i, p)),
                   pl.BlockSpec((2, blk, LANE), lambda p, i: (p, i, 0))),
        out_shape=(jax.ShapeDtypeStruct((seq, FOX_WIDTH), BF16), jax.ShapeDtypeStruct((FOX_HEADS, seq, LANE), F32)),
        compiler_params=_cparams(("parallel", "arbitrary")),
    )(qkv, qkv, qkv, cum_t)


def _fox_bwd(name, qkv, cum_t, att, datt, lse):
    seq = qkv.shape[0]
    blk = min(FOX_BLK, seq)
    nb = seq // blk
    npair = FOX_HEADS // 2

    def body(q_ref, k_ref, v_ref, cum_ref, o_ref, do_ref, lse_ref, dq_ref, dk_ref, dv_ref, dcum_ref):
        iq = pl.program_id(1)
        q0 = pl.multiple_of(iq * blk, blk)

        @pl.when(iq == 0)
        def _():
            dk_ref[...] = jnp.zeros_like(dk_ref)
            dv_ref[...] = jnp.zeros_like(dv_ref)
            dcum_ref[...] = jnp.zeros_like(dcum_ref)

        qv = q_ref[...]
        dov = do_ref[...].astype(F32)
        ov = o_ref[...].astype(F32)
        row = lax.broadcasted_iota(jnp.int32, (blk, blk), 0)
        col = lax.broadcasted_iota(jnp.int32, (blk, blk), 1)
        qhs, dohbs, deltas, lses = [], [], [], []
        for hh in range(2):
            hm = _fox_head_mask(qv.shape, hh)
            qhs.append(jnp.where(hm, qv, jnp.zeros_like(qv)) * FOX_SCALE)
            doh = jnp.where(hm, dov, 0.0)
            dohbs.append(doh.astype(BF16))
            deltas.append(jnp.sum(doh * ov, axis=1, keepdims=True))
            lses.append(jnp.tile(lse_ref[hh], (1, blk // LANE)))

        def block(kb, accs, masked):
            k0 = pl.multiple_of(kb * blk, blk)
            kv = k_ref[pl.ds(k0, blk), :]
            vv = v_ref[pl.ds(k0, blk), :]
            new = []
            dk_blk = None
            dv_blk = None
            for hh in range(2):
                dq_acc, rs_acc = accs[hh]
                s = _bdot(qhs[hh], kv, _DIMS["nt"]) + _fox_bias(cum_ref, hh, q0, k0, blk)
                p = jnp.exp(s - lses[hh])
                if masked:
                    p = jnp.where(row >= col, p, 0.0)
                dp = _bdot(dohbs[hh], vv, _DIMS["nt"])
                ds = p * (dp - deltas[hh])
                dsb = ds.astype(BF16)
                dk_h = _bdot(dsb, qhs[hh], _DIMS["tn"])
                dv_h = _bdot(p, dohbs[hh], _DIMS["tn"])
                dk_blk = dk_h if dk_blk is None else dk_blk + dk_h
                dv_blk = dv_h if dv_blk is None else dv_blk + dv_h
                dcum_ref[hh:hh + 1, pl.ds(k0, blk)] -= jnp.sum(ds, axis=0, keepdims=True)
                new.append((dq_acc + _bdot(dsb, kv, _DIMS["nn"]), rs_acc + jnp.sum(ds, axis=1, keepdims=True)))
            dk_ref[pl.ds(k0, blk), :] += dk_blk
            dv_ref[pl.ds(k0, blk), :] += dv_blk
            return tuple(new)

        init = (jnp.zeros((blk, LANE), F32), jnp.zeros((blk, 1), F32))
        accs = lax.fori_loop(0, iq, lambda kb, a: block(kb, a, False), (init, init))
        accs = block(iq, accs, True)
        for hh in range(2):
            dcum_ref[hh:hh + 1, pl.ds(q0, blk)] += jnp.broadcast_to(accs[hh][1], (blk, LANE)).T[0:1, :]
        dq = jnp.where(_fox_head_mask(qv.shape, 0), accs[0][0], accs[1][0]) * FOX_SCALE
        dq_ref[...] = dq.astype(dq_ref.dtype)

    qblk = pl.BlockSpec((blk, LANE), lambda p, i: (i, p))
    full = pl.BlockSpec((seq, LANE), lambda p, i: (0, p))
    return pl.pallas_call(
        body, name=name, grid=(npair, nb),
        in_specs=[qblk,
                  pl.BlockSpec((seq, LANE), lambda p, i: (0, npair + p)),
                  pl.BlockSpec((seq, LANE), lambda p, i: (0, 2 * npair + p)),
                  pl.BlockSpec((None, 2, seq), lambda p, i: (p, 0, 0)),
                  qblk, qblk,
                  pl.BlockSpec((2, blk, LANE), lambda p, i: (p, i, 0))],
        out_specs=(qblk, full, full, pl.BlockSpec((None, 2, seq), lambda p, i: (p, 0, 0))),
        out_shape=(jax.ShapeDtypeStruct((seq, FOX_WIDTH), BF16), jax.ShapeDtypeStruct((seq, FOX_WIDTH), F32),
                   jax.ShapeDtypeStruct((seq, FOX_WIDTH), F32), jax.ShapeDtypeStruct((npair, 2, seq), F32)),
        compiler_params=_cparams(("arbitrary", "arbitrary")),
    )(qkv, qkv, qkv, cum_t, att, datt, lse)


MEM_SCALE = MEM_HEAD_DIM ** -0.5


def _mem_probs(qh, kh):
    s = _bdot(qh, kh, _DIMS["nt"]) * MEM_SCALE
    p = jnp.exp(s - jnp.max(s, axis=1, keepdims=True))
    return p / jnp.sum(p, axis=1, keepdims=True)


def _mem_fwd(name, q2, kv, *, tr=512):
    seq = q2.shape[0]
    mlen = kv.shape[0]
    tr = min(tr, seq)

    def body(q_ref, kv_ref, o_ref):
        for h in range(MEM_HEADS):
            sl = slice(h * MEM_HEAD_DIM, (h + 1) * MEM_HEAD_DIM)
            sv = slice(MEM_WIDTH + h * MEM_HEAD_DIM, MEM_WIDTH + (h + 1) * MEM_HEAD_DIM)
            p = _mem_probs(q_ref[:, sl], kv_ref[:, sl])
            o_ref[:, sl] = _bdot(p, kv_ref[:, sv], _DIMS["nn"]).astype(o_ref.dtype)

    return pl.pallas_call(
        body, name=name, grid=(seq // tr,),
        in_specs=[pl.BlockSpec((tr, MEM_WIDTH), lambda i: (i, 0)), pl.BlockSpec((mlen, 2 * MEM_WIDTH), lambda i: (0, 0))],
        out_specs=pl.BlockSpec((tr, MEM_WIDTH), lambda i: (i, 0)),
        out_shape=jax.ShapeDtypeStruct((seq, MEM_WIDTH), BF16),
        compiler_params=_cparams(("parallel",)),
    )(q2, kv)


def _mem_bwd(name, q2, kv, do2, *, tr=512):
    seq = q2.shape[0]
    mlen = kv.shape[0]
    tr = min(tr, seq)

    def body(q_ref, kv_ref, do_ref, dq_ref, dkv_ref):
        i = pl.program_id(0)

        @pl.when(i == 0)
        def _():
            dkv_ref[...] = jnp.zeros_like(dkv_ref)

        for h in range(MEM_HEADS):
            sl = slice(h * MEM_HEAD_DIM, (h + 1) * MEM_HEAD_DIM)
            sv = slice(MEM_WIDTH + h * MEM_HEAD_DIM, MEM_WIDTH + (h + 1) * MEM_HEAD_DIM)
            qh = q_ref[:, sl]
            kh = kv_ref[:, sl]
            doh = do_ref[:, sl].astype(BF16)
            p = _mem_probs(qh, kh)
            dp = _bdot(doh, kv_ref[:, sv], _DIMS["nt"])
            ds = (p * (dp - jnp.sum(p * dp, axis=1, keepdims=True)) * MEM_SCALE).astype(BF16)
            dq_ref[:, sl] = _bdot(ds, kh, _DIMS["nn"]).astype(dq_ref.dtype)
            dkv_ref[:, sl] += _bdot(ds, qh, _DIMS["tn"])
            dkv_ref[:, sv] += _bdot(p, doh, _DIMS["tn"])

    row = pl.BlockSpec((tr, MEM_WIDTH), lambda i: (i, 0))
    kvs = pl.BlockSpec((mlen, 2 * MEM_WIDTH), lambda i: (0, 0))
    return pl.pallas_call(
        body, name=name, grid=(seq // tr,), in_specs=[row, kvs, row], out_specs=(row, kvs),
        out_shape=(jax.ShapeDtypeStruct((seq, MEM_WIDTH), BF16), jax.ShapeDtypeStruct((mlen, 2 * MEM_WIDTH), F32)),
        compiler_params=_cparams(("arbitrary",)),
    )(q2, kv, do2)


def _exchange(name, arrays, *, scatter):
    n = len(arrays)

    def body(*refs):
        ins, outs = refs[:n], refs[n:2 * n]
        send_sems, recv_sems, local_sems = refs[2 * n:]
        x, y, c = lax.axis_index("x"), lax.axis_index("y"), lax.axis_index("c")
        me = 4 * x + 2 * y + c
        copies = []
        for a in range(n):
            src = ins[a].at[me] if scatter else ins[a]
            local = pltpu.make_async_copy(src, outs[a].at[me], local_sems.at[a])
            local.start()
            copies.append(local)
        for k in range(N_DEV - 1):
            flip = k + 1
            px = 1 - x if flip & 4 else x
            py = 1 - y if flip & 2 else y
            pc = 1 - c if flip & 1 else c
            peer = 4 * px + 2 * py + pc
            for a in range(n):
                src = ins[a].at[peer] if scatter else ins[a]
                rdma = pltpu.make_async_remote_copy(
                    src_ref=src, dst_ref=outs[a].at[me], send_sem=send_sems.at[a, k], recv_sem=recv_sems.at[a, k],
                    device_id=(px, py, pc), device_id_type=pl.DeviceIdType.MESH)
                rdma.start()
                copies.append(rdma)
        for cp in copies:
            cp.wait()

    hbm = pl.BlockSpec(memory_space=pl.ANY)
    out_shape = tuple(
        jax.ShapeDtypeStruct(arr.shape if scatter else (N_DEV,) + arr.shape, arr.dtype) for arr in arrays)
    return pl.pallas_call(
        body, name=name, in_specs=[hbm] * n, out_specs=tuple([hbm] * n), out_shape=out_shape,
        scratch_shapes=[pltpu.SemaphoreType.DMA((n, N_DEV - 1)), pltpu.SemaphoreType.DMA((n, N_DEV - 1)),
                        pltpu.SemaphoreType.DMA((n,))],
    )(*arrays)


def _unstack_cols(name, stacked):
    n, rows, cols = stacked.shape

    def body(i_ref, o_ref):
        o_ref[...] = i_ref[...]

    return pl.pallas_call(
        body, name=name, grid=(n,), in_specs=[pl.BlockSpec((None, rows, cols), lambda k: (k, 0, 0))],
        out_specs=pl.BlockSpec((rows, cols), lambda k: (0, k)),
        out_shape=jax.ShapeDtypeStruct((rows, n * cols), stacked.dtype),
        compiler_params=_cparams(("parallel",)),
    )(stacked)


def _restack_cols(name, mat):
    rows, width = mat.shape
    cols = width // N_DEV

    def body(i_ref, o_ref):
        o_ref[...] = i_ref[...]

    return pl.pallas_call(
        body, name=name, grid=(N_DEV,), in_specs=[pl.BlockSpec((rows, cols), lambda k: (0, k))],
        out_specs=pl.BlockSpec((None, rows, cols), lambda k: (k, 0, 0)),
        out_shape=jax.ShapeDtypeStruct((N_DEV, rows, cols), mat.dtype),
        compiler_params=_cparams(("parallel",)),
    )(mat)


def _remap_pieces(runs):
    plan = {}
    for du, dc, su, sc, ln in runs:
        while ln > 0:
            lane = dc % LANE
            take = min(ln, LANE - lane)
            plan.setdefault((du, dc // LANE), []).append((su, sc, take, lane))
            dc, sc, ln = dc + take, sc + take, ln - take
    return plan


def _remap(name, srcs, src_units, runs, *, out_units, out_cols, out_dtype, tr=256):
    rows = srcs[0].shape[-2]
    tr = min(tr, rows)
    plan = _remap_pieces(runs)
    n_src = len(srcs)
    stacked_out = out_units is not None
    n_tiles = out_cols // LANE

    def body(*refs):
        o_ref = refs[n_src]

        def src_tile(unit, t):
            ai, lead = src_units[unit]
            ref = refs[ai]
            sl = slice(t * LANE, (t + 1) * LANE)
            return (ref[:, sl] if lead is None else ref[lead, :, sl]).astype(F32)

        lane = lax.broadcasted_iota(jnp.int32, (tr, LANE), 1)
        for du in range(out_units if stacked_out else 1):
            for t in range(n_tiles):
                acc = jnp.zeros((tr, LANE), F32)
                for su, sc, ln, dl in plan.get((du if stacked_out else None, t), []):
                    st, so = sc // LANE, sc % LANE
                    first = src_tile(su, st)
                    if so == dl and so + ln <= LANE:
                        piece = first
                    else:
                        second = src_tile(su, st + 1) if so + ln > LANE else first
                        both = jnp.concatenate([first, second], axis=1)
                        piece = pltpu.roll(both, (dl - so) % (2 * LANE), axis=1)[:, 0:LANE]
                    acc = piece if (dl == 0 and ln == LANE) else jnp.where(
                        jnp.logical_and(lane >= dl, lane < dl + ln), piece, acc)
                if stacked_out:
                    o_ref[du, :, t * LANE:(t + 1) * LANE] = acc.astype(o_ref.dtype)
                else:
                    o_ref[:, t * LANE:(t + 1) * LANE] = acc.astype(o_ref.dtype)

    in_specs = []
    for arr in srcs:
        if arr.ndim == 2:
            in_specs.append(pl.BlockSpec((tr, arr.shape[1]), lambda i: (i, 0)))
        else:
            in_specs.append(pl.BlockSpec((arr.shape[0], tr, arr.shape[2]), lambda i: (0, i, 0)))
    if stacked_out:
        out_spec = pl.BlockSpec((out_units, tr, out_cols), lambda i: (0, i, 0))
        out_shape = jax.ShapeDtypeStruct((out_units, rows, out_cols), out_dtype)
    else:
        out_spec = pl.BlockSpec((tr, out_cols), lambda i: (i, 0))
        out_shape = jax.ShapeDtypeStruct((rows, out_cols), out_dtype)
    return pl.pallas_call(
        body, name=name, grid=(rows // tr,), in_specs=in_specs, out_specs=out_spec, out_shape=out_shape,
        compiler_params=_cparams(("parallel",)),
    )(*srcs)


def _proj_col(c):
    if c < PROJ_GATE0:
        return c
    if c < PROJ_GATE0 + FOX_HEADS:
        return PROJ_F0 + (c - PROJ_GATE0)
    return c - FOX_HEADS


def _win_runs():
    cuts = sorted(set([0, PROJ_GATE0, PROJ_GATE0 + FOX_HEADS, IN_WIDTH] + [SHARD_IN * k for k in range(N_DEV + 1)]))
    return [(lo // SHARD_IN, lo % SHARD_IN, _proj_col(lo), hi - lo) for lo, hi in zip(cuts[:-1], cuts[1:])]


def _assemble_win(name, stacked):
    runs = [(None, pc, k, sc, ln) for k, sc, pc, ln in _win_runs()]
    return _remap(name, [stacked], [(0, k) for k in range(N_DEV)], runs,
                  out_units=None, out_cols=PROJ_WIDTH, out_dtype=BF16)


_DPROJ_SEGS = ((0, 512), (512, 1024), (1024, 1536), (1536, 2048), (2048, 4096), (4096, 4224))


def _disassemble_dwin(name, parts):
    runs = []
    for k, sc, pc, ln in _win_runs():
        while ln > 0:
            seg = next(i for i, (lo, hi) in enumerate(_DPROJ_SEGS) if lo <= pc < hi)
            take = min(ln, _DPROJ_SEGS[seg][1] - pc)
            runs.append((k, sc, seg, pc - _DPROJ_SEGS[seg][0], take))
            sc, pc, ln = sc + take, pc + take, ln - take
    return _remap(name, list(parts), [(i, None) for i in range(len(parts))], runs,
                  out_units=N_DEV, out_cols=SHARD_IN_PAD, out_dtype=BF16)


def _assemble_wffn(name, stacked):
    runs = [(None, SHARD_FFN * k, k, 0, SHARD_FFN) for k in range(N_DEV)]
    return _remap(name, [stacked], [(0, k) for k in range(N_DEV)], runs,
                  out_units=None, out_cols=2 * FFN_HIDDEN, out_dtype=BF16)


def _disassemble_dwffn(name, dw):
    runs = [(k, 0, 0, SHARD_FFN * k, SHARD_FFN) for k in range(N_DEV)]
    return _remap(name, [dw], [(0, None)], runs, out_units=N_DEV, out_cols=SHARD_FFN_PAD, out_dtype=BF16)


def _adamw(name, parts, w, m, v, *, tr=128):
    rows, cols = w.shape
    tr = min(tr, rows)
    assert rows % tr == 0, (name, rows, tr)
    c1 = 1.0 - ADAM_B1 ** ADAM_STEP
    c2 = 1.0 - ADAM_B2 ** ADAM_STEP

    def body(p_ref, w_ref, m_ref, v_ref, g_ref, d_ref, nm_ref, nv_ref):
        g = p_ref[0].astype(F32)
        for s in range(1, N_DEV):
            g = g + p_ref[s].astype(F32)
        m_new = ADAM_B1 * m_ref[...] + (1.0 - ADAM_B1) * g
        v_new = ADAM_B2 * v_ref[...] + (1.0 - ADAM_B2) * (g * g)
        upd = (m_new / c1) / (jnp.sqrt(v_new / c2) + ADAM_EPS) + ADAM_WD * w_ref[...]
        g_ref[...] = g
        d_ref[...] = -ADAM_LR * upd
        nm_ref[...] = m_new
        nv_ref[...] = v_new

    row = pl.BlockSpec((tr, cols), lambda i: (i, 0))
    out = jax.ShapeDtypeStruct((rows, cols), F32)
    return pl.pallas_call(
        body, name=name, grid=(rows // tr,),
        in_specs=[pl.BlockSpec((N_DEV, tr, cols), lambda i: (0, i, 0)), row, row, row],
        out_specs=(row, row, row, row), out_shape=(out, out, out, out),
        compiler_params=_cparams(("parallel",)),
    )(parts, w, m, v)


_WEIGHTS = ("norm_mix", "w_in", "b_forget", "lam_re", "lam_im", "log_dt", "b_re", "b_im", "c_re", "c_im",
            "d_skip", "w_glu", "w_fox_o", "w_mix_out", "norm_mem_q", "norm_mem_kv", "w_mem_q", "w_mem_kv",
            "w_mem_o", "norm_ffn", "w_ffn_in", "w_ffn_out", "norm_final")
_SHARDED = ("w_in", "w_glu", "w_fox_o", "w_mix_out", "w_mem_q", "w_mem_kv", "w_mem_o", "w_ffn_in", "w_ffn_out")
_SMALL = tuple(n for n in _WEIGHTS if n not in _SHARDED)
_PACK_COLS = 1024


def _pack(arrays):
    flat = jnp.concatenate([a.reshape(-1).astype(F32) for a in arrays])
    rows = -(-flat.shape[0] // _PACK_COLS)
    return jnp.pad(flat, (0, rows * _PACK_COLS - flat.shape[0])).reshape(rows, _PACK_COLS)


def _unpack(buf, like):
    flat = buf.reshape(-1)
    out, pos = [], 0
    for a in like:
        out.append(flat[pos:pos + a.size].reshape(a.shape))
        pos += a.size
    return out


def _mm(name, a, b, mode, m, n, k, out_dtype, tm=1024, tn=512, tk=1024, **kw):
    return _matmul(name, a, b, mode, m, n, k, out_dtype=out_dtype, tm=tm, tn=tn, tk=tk, **kw)


def kernel(x, mem, norm_mix, w_in, b_forget, lam_re, lam_im, log_dt, b_re, b_im, c_re, c_im, d_skip, w_glu, w_fox_o, w_mix_out, norm_mem_q, norm_mem_kv, w_mem_q, w_mem_kv, w_mem_o, norm_ffn, w_ffn_in, w_ffn_out, norm_final, loss_target, m_norm_mix, m_w_in, m_b_forget, m_lam_re, m_lam_im, m_log_dt, m_b_re, m_b_im, m_c_re, m_c_im, m_d_skip, m_w_glu, m_w_fox_o, m_w_mix_out, m_norm_mem_q, m_norm_mem_kv, m_w_mem_q, m_w_mem_kv, m_w_mem_o, m_norm_ffn, m_w_ffn_in, m_w_ffn_out, m_norm_final, v_norm_mix, v_w_in, v_b_forget, v_lam_re, v_lam_im, v_log_dt, v_b_re, v_b_im, v_c_re, v_c_im, v_d_skip, v_w_glu, v_w_fox_o, v_w_mix_out, v_norm_mem_q, v_norm_mem_kv, v_w_mem_q, v_w_mem_kv, v_w_mem_o, v_norm_ffn, v_w_ffn_in, v_w_ffn_out, v_norm_final):
    given = dict(locals())
    weights = {n: given[n] for n in _WEIGHTS}
    mom_m = {n: given["m_" + n] for n in _WEIGHTS}
    mom_v = {n: given["v_" + n] for n in _WEIGHTS}
    seq = x.shape[1]
    nc = seq // SSM_CHUNK
    d = D_MODEL
    xs, mems, tgt = x[0], mem[0], loss_target[0]

    def padcols(a, width):
        return jnp.pad(a, ((0, 0), (0, width - a.shape[1])))

    shards = [padcols(w_in[0].astype(BF16), SHARD_IN_PAD), w_glu[0].astype(BF16), w_fox_o[0].astype(BF16),
              w_mix_out[0].astype(BF16), w_mem_q[0].astype(BF16), w_mem_kv[0].astype(BF16),
              w_mem_o[0].astype(BF16), padcols(w_ffn_in[0].astype(BF16), SHARD_FFN_PAD), w_ffn_out[0].astype(BF16)]
    gathered = _exchange("gather_weights", shards, scatter=False)
    win = _assemble_win("assemble_w_in", gathered[0])
    wglu = _unstack_cols("unstack_w_glu", gathered[1])
    wfoxo = _unstack_cols("unstack_w_fox_o", gathered[2])
    wmo = _unstack_cols("unstack_w_mem_o", gathered[6])
    wmix = gathered[3].reshape(d, d)
    wmq = gathered[4].reshape(d, MEM_WIDTH)
    wmkv = gathered[5].reshape(d, 2 * MEM_WIDTH)
    wffn_in = _assemble_wffn("assemble_w_ffn_in", gathered[7])
    wffn_out = gathered[8].reshape(FFN_HIDDEN, d)

    u = _rms_fwd("rms_mix", xs, norm_mix)
    ussm = _mm("proj_ssm", u, win, "nn", seq, SSM_WIDTH, d, F32)
    qkv = _mm("proj_qkv", u, win, "nn", seq, 3 * FOX_WIDTH, d, BF16, tn=512, b_off=(0, SSM_WIDTH))
    gates = _mm("proj_gates", u, win, "nn", seq, 2 * d, d, F32, tn=1024, b_off=(0, PROJ_GATE0))
    fproj = _mm("proj_forget", u, win, "nn", seq, LANE, d, F32, tn=LANE, b_off=(0, PROJ_F0))

    ssm_params = (lam_re[0], lam_im[0], log_dt[0], b_re[0], b_im[0], c_re[0], c_im[0])
    (m_mat, bw_mat, cm_mat, a8, aseg), mats_vjp = jax.vjp(lambda *p: _ssm_mats(*p, nc), *ssm_params)
    m_b, bw_b, cm_b = m_mat.astype(BF16), bw_mat.astype(BF16), cm_mat.astype(BF16)
    u8 = ussm.reshape(nc, SSM_CHUNK * SSM_WIDTH)
    d8 = jnp.tile(d_skip, (1, SSM_CHUNK))
    w4 = _ssm_w("ssm_w", u8, bw_b)
    sp4 = _ssm_scan("ssm_scan", w4, a8, aseg, reverse=False)
    y8 = _ssm_y("ssm_y", u8, sp4, m_b, cm_b)
    act = _ssm_post_fwd("ssm_act", y8, u8, d8).reshape(seq, SSM_WIDTH)
    glu = _mm("glu", act, wglu, "nn", seq, 2 * d, SSM_WIDTH, F32, tn=1024)

    bcol = jnp.pad(b_forget[0], (0, LANE - FOX_HEADS)).reshape(LANE, 1)
    cum_t = _fox_cum("fox_cum", fproj, bcol).reshape(FOX_HEADS // 2, 2, seq)
    att, lse = _fox_fwd("fox_fwd", qkv, cum_t)
    out_b = _mm("fox_out", att, wfoxo, "nn", seq, d, FOX_WIDTH, F32, tn=1024)

    mixin = _mix_fwd("mix", glu, gates, out_b)
    h1 = _mm("mix_out", mixin, wmix, "nn", seq, d, d, F32, tn=1024, add=xs)

    n1 = _rms_fwd("rms_mem_q", h1, norm_mem_q)
    q2 = _mm("mem_q", n1, wmq, "nn", seq, MEM_WIDTH, d, BF16)
    mn = _rms_fwd("rms_mem_kv", mems, norm_mem_kv)
    mlen = mems.shape[0]
    kv = _mm("mem_kv", mn, wmkv, "nn", mlen, 2 * MEM_WIDTH, d, BF16)
    o2 = _mem_fwd("mem_attn", q2, kv)
    h2 = _mm("mem_out", o2, wmo, "nn", seq, d, MEM_WIDTH, F32, tn=1024, add=h1)

    n2 = _rms_fwd("rms_ffn", h2, norm_ffn)
    f = _mm("ffn_in", n2, wffn_in, "nn", seq, 2 * FFN_HIDDEN, d, F32, tn=1408)
    g_act = _swiglu_fwd("swiglu", f)
    h3 = _mm("ffn_out", g_act, wffn_out, "nn", seq, d, FFN_HIDDEN, F32, tk=FFN_HIDDEN, add=h2)
    loss_part, dh3, dg_final = _final_loss("final_loss", h3, tgt, norm_final.reshape(1, d))

    dg_act = _mm("d_ffn_out_x", dh3, wffn_out, "nt", seq, FFN_HIDDEN, d, F32, tn=1408)
    dwffn_out = _mm("d_ffn_out_w", g_act, dh3, "tn", FFN_HIDDEN, d, seq, BF16, tm=1408, tn=1024)
    df = _swiglu_bwd("d_swiglu", dg_act, f)
    dn2 = _mm("d_ffn_in_x", df, wffn_in, "nt", seq, d, 2 * FFN_HIDDEN, F32, tn=1024, tk=FFN_HIDDEN)
    dwffn_in = _mm("d_ffn_in_w", n2, df, "tn", d, 2 * FFN_HIDDEN, seq, BF16, tn=1408)
    dh2, dg_ffn = _rms_bwd("d_rms_ffn", dn2, h2, norm_ffn, res=dh3)

    do2 = _mm("d_mem_out_x", dh2, wmo, "nt", seq, MEM_WIDTH, d, F32)
    dwmo = _restack_cols("restack_d_w_mem_o", _mm("d_mem_out_w", o2, dh2, "tn", MEM_WIDTH, d, seq, BF16, tn=1024))
    dq2, dkv = _mem_bwd("d_mem_attn", q2, kv, do2)
    dwmq = _mm("d_mem_q_w", n1, dq2, "tn", d, MEM_WIDTH, seq, BF16)
    dn1 = _mm("d_mem_q_x", dq2, wmq, "nt", seq, d, MEM_WIDTH, F32)
    dwmkv = _mm("d_mem_kv_w", mn, dkv, "tn", d, 2 * MEM_WIDTH, mlen, BF16, tn=1024)
    dmn = _mm("d_mem_kv_x", dkv, wmkv, "nt", mlen, d, 2 * MEM_WIDTH, F32)
    _, dg_memkv = _rms_bwd("d_rms_mem_kv", dmn, mems, norm_mem_kv)
    dh1, dg_memq = _rms_bwd("d_rms_mem_q", dn1, h1, norm_mem_q, res=dh2)

    dmixin = _mm("d_mix_out_x", dh1, wmix, "nt", seq, d, d, F32, tn=1024)
    dwmix = _mm("d_mix_out_w", mixin, dh1, "tn", d, d, seq, BF16, tn=1024)
    dglu, dgates, dout_b = _mix_bwd("d_mix", dmixin, glu, gates, out_b)
    datt = _mm("d_fox_out_x", dout_b, wfoxo, "nt", seq, FOX_WIDTH, d, F32)
    dwfoxo = _restack_cols("restack_d_w_fox_o", _mm("d_fox_out_w", att, dout_b, "tn", FOX_WIDTH, d, seq, BF16, tn=1024))
    dact = _mm("d_glu_x", dglu, wglu, "nt", seq, SSM_WIDTH, 2 * d, F32, tk=2 * d)
    dwglu = _restack_cols("restack_d_w_glu", _mm("d_glu_w", act, dglu, "tn", SSM_WIDTH, 2 * d, seq, BF16, tn=2 * d))

    dz8, dg_dskip = _ssm_post_bwd("d_ssm_act", dact.reshape(nc, SSM_CHUNK * SSM_WIDTH), y8, u8, d8)
    ds4, dcm = _ssm_ds("d_ssm_y_state", dz8, sp4, cm_b)
    g4, da8 = _ssm_scan("d_ssm_scan", ds4, a8, aseg, reverse=True, sprev4=sp4)
    dx8, dm, dbw = _ssm_dx("d_ssm_x", dz8, g4, u8, m_b, bw_b, d8)
    dussm = dx8.reshape(seq, SSM_WIDTH)
    g_ssm = mats_vjp((dm, dbw, dcm, da8, jnp.zeros_like(aseg)))

    dq, dk, dv, dcum = _fox_bwd("d_fox", qkv, cum_t, att, datt, lse)
    dfproj, dbf = _fox_cum_bwd("d_fox_cum", dcum.reshape(FOX_HEADS, seq), fproj, bcol)
    dg_bforget = dbf[0:FOX_HEADS, 0].reshape(1, FOX_HEADS)

    dparts = (dussm, dq, dk, dv, dgates, dfproj)
    du = None
    dw_parts = []
    for i, (part, (lo, hi)) in enumerate(zip(dparts, _DPROJ_SEGS)):
        width = hi - lo
        tk = min(width, 1024)
        du = _mm(f"d_proj_x{i}", part, win, "nt", seq, d, width, F32, tk=tk, b_off=(0, lo), add=du)
        dw_parts.append(_mm(f"d_proj_w{i}", u, part, "tn", d, width, seq, BF16, tn=min(width, 1024)))
    dx, dg_mix = _rms_bwd("d_rms_mix", du, xs, norm_mix, res=dh1)

    pieces = [_disassemble_dwin("split_d_w_in", dw_parts), dwglu, dwfoxo,
              dwmix.reshape(N_DEV, d // N_DEV, d), dwmq.reshape(N_DEV, d // N_DEV, MEM_WIDTH),
              dwmkv.reshape(N_DEV, d // N_DEV, 2 * MEM_WIDTH), dwmo,
              _disassemble_dwffn("split_d_w_ffn_in", dwffn_in),
              dwffn_out.reshape(N_DEV, FFN_HIDDEN // N_DEV, d)]
    received = _exchange("scatter_grads", pieces, scatter=True)

    small_grads = dict(zip(
        _SMALL, (dg_mix, dg_bforget, g_ssm[0][None], g_ssm[1][None], g_ssm[2][None], g_ssm[3][None], g_ssm[4][None],
                 g_ssm[5][None], g_ssm[6][None], dg_dskip, dg_memq, dg_memkv, dg_ffn, dg_final.reshape(d))))
    small_like = [weights[n] for n in _SMALL]
    small_all = _exchange("gather_small_grads", [_pack([small_grads[n] for n in _SMALL])], scatter=False)[0]
    pk = [_pack([src[n] for n in _SMALL]) for src in (weights, mom_m, mom_v)]
    small_out = _adamw("adamw_small", small_all, pk[0], pk[1], pk[2], tr=small_all.shape[1])
    small_res = [dict(zip(_SMALL, _unpack(buf, small_like))) for buf in small_out]

    results = [dict(r) for r in small_res]
    tiles = {"w_in": 128, "w_glu": 128, "w_fox_o": 128, "w_mix_out": 128, "w_mem_q": 128, "w_mem_kv": 128,
             "w_mem_o": 128, "w_ffn_in": 128, "w_ffn_out": 176}
    pads = {"w_in": SHARD_IN_PAD, "w_ffn_in": SHARD_FFN_PAD}
    for name, parts in zip(_SHARDED, received):
        w2, m2, v2 = weights[name][0], mom_m[name][0], mom_v[name][0]
        cols = w2.shape[1]
        if name in pads:
            w2, m2, v2 = (padcols(t, pads[name]) for t in (w2, m2, v2))
        outs = _adamw("adamw_" + name, parts, w2, m2, v2, tr=tiles[name])
        for res, o in zip(results, outs):
            res[name] = o[:, :cols][None]

    loss = lax.psum(loss_part[0, 0], ("x", "y", "c"))
    out = [loss, dx[None]]
    for res in results:
        out.extend(res[n] for n in _WEIGHTS)
    return tuple(out)
```

```python
import math

import jax
import jax.numpy as jnp
import numpy as np
from jax import lax
from jax.experimental import pallas as pl
from jax.experimental.pallas import tpu as pltpu

F32 = jnp.float32
BF16 = jnp.bfloat16

N_DEV = 8
LANE = 128
VMEM_LIMIT = 56 * 1024 * 1024

D_MODEL = 1024
SSM_GROUP = 16
SSM_GROUPS = 32
SSM_WIDTH = 512
SSM_STATE = 64
SSM_CHUNK = 8
FOX_HEADS = 8
FOX_HEAD_DIM = 64
FOX_WIDTH = 512
MEM_HEADS = 4
MEM_HEAD_DIM = 128
MEM_WIDTH = 512
FFN_HIDDEN = 2816
RMS_EPS = 1e-6
IN_WIDTH = 4104
SHARD_IN = IN_WIDTH // N_DEV
SHARD_IN_PAD = 640
SHARD_FFN = 2 * FFN_HIDDEN // N_DEV
SHARD_FFN_PAD = 768
PROJ_GATE0 = 2048
PROJ_F0 = 4096
PROJ_WIDTH = 4224

ADAM_LR = 0.001
ADAM_B1 = 0.9
ADAM_B2 = 0.999
ADAM_EPS = 1e-08
ADAM_WD = 0.01
ADAM_STEP = 10


def _cparams(sem=None):
    return pltpu.CompilerParams(dimension_semantics=sem, vmem_limit_bytes=VMEM_LIMIT)


def _sigmoid(x):
    return 1.0 / (1.0 + jnp.exp(-x))


def _bdot(a, b, dims):
    return lax.dot_general(a.astype(BF16), b.astype(BF16), ((dims[0], dims[1]), ((), ())),
                           preferred_element_type=F32)


_DIMS = {"nn": ((1,), (0,)), "nt": ((1,), (1,)), "tn": ((0,), (0,))}


def _matmul(name, a, b, mode, m, n, k, *, out_dtype, tm, tn, tk, a_off=(0, 0), b_off=(0, 0), add=None):
    tm, tn, tk = min(tm, m), min(tn, n), min(tk, k)
    assert m % tm == 0 and n % tn == 0 and k % tk == 0, (name, m, n, k, tm, tn, tk)
    nk = k // tk
    grid = (m // tm, n // tn, nk)

    def blk(off, t):
        assert off % t == 0, (name, off, t)
        return off // t

    if mode in ("nn", "nt"):
        ar, ac = blk(a_off[0], tm), blk(a_off[1], tk)
        a_spec = pl.BlockSpec((tm, tk), lambda i, j, kk: (i + ar, kk + ac))
    else:
        ar, ac = blk(a_off[0], tk), blk(a_off[1], tm)
        a_spec = pl.BlockSpec((tk, tm), lambda i, j, kk: (kk + ar, i + ac))

    if mode in ("nn", "tn"):
        br, bc = blk(b_off[0], tk), blk(b_off[1], tn)
        b_spec = pl.BlockSpec((tk, tn), lambda i, j, kk: (kk + br, j + bc))
    else:
        br, bc = blk(b_off[0], tn), blk(b_off[1], tk)
        b_spec = pl.BlockSpec((tn, tk), lambda i, j, kk: (j + br, kk + bc))
    o_spec = pl.BlockSpec((tm, tn), lambda i, j, kk: (i, j))
    out_shape = jax.ShapeDtypeStruct((m, n), out_dtype)

    in_specs = [a_spec, b_spec]
    operands = [a, b]
    if add is not None:
        in_specs.append(pl.BlockSpec((tm, tn), lambda i, j, kk: (i, j)))
        operands.append(add)
    dims = _DIMS[mode]
    has_add = add is not None

    def body(*refs):
        a_ref, b_ref = refs[0], refs[1]
        add_ref = refs[2] if has_add else None
        o_ref = refs[3] if has_add else refs[2]
        acc_ref = refs[-1] if nk > 1 else None
        prod = _bdot(a_ref[...], b_ref[...], dims)

        def finish(total):
            if has_add:
                total = total + add_ref[...].astype(F32)
            o_ref[...] = total.astype(o_ref.dtype)

        if nk == 1:
            finish(prod)
        else:
            kk = pl.program_id(2)

            @pl.when(kk == 0)
            def _():
                acc_ref[...] = prod

            @pl.when(jnp.logical_and(kk > 0, kk < nk - 1))
            def _():
                acc_ref[...] += prod

            @pl.when(kk == nk - 1)
            def _():
                finish(acc_ref[...] + prod)

    scratch = [pltpu.VMEM((tm, tn), F32)] if nk > 1 else []
    return pl.pallas_call(
        body, name=name, grid=grid, in_specs=in_specs, out_specs=o_spec, out_shape=out_shape,
        scratch_shapes=scratch,
        compiler_params=_cparams(("parallel", "parallel", "arbitrary")),
    )(*operands)


def _rms_fwd(name, x, gain, *, tr=512):
    r, d = x.shape
    tr = min(tr, r)

    def body(x_ref, g_ref, o_ref):
        xv = x_ref[...]
        rstd = lax.rsqrt(jnp.mean(xv * xv, axis=-1, keepdims=True) + RMS_EPS)
        o_ref[...] = (xv * rstd * g_ref[...]).astype(o_ref.dtype)

    return pl.pallas_call(
        body, name=name, grid=(r // tr,),
        in_specs=[pl.BlockSpec((tr, d), lambda i: (i, 0)), pl.BlockSpec((1, d), lambda i: (0, 0))],
        out_specs=pl.BlockSpec((tr, d), lambda i: (i, 0)),
        out_shape=jax.ShapeDtypeStruct((r, d), BF16),
        compiler_params=_cparams(("parallel",)),
    )(x, gain)


def _rms_bwd(name, dy, x, gain, res=None, *, tr=512):
    r, d = x.shape
    tr = min(tr, r)
    n = r // tr
    has_res = res is not None

    def body(*refs):
        dy_ref, x_ref, g_ref = refs[:3]
        res_ref = refs[3] if has_res else None
        dx_ref, dg_ref, acc_ref = refs[-3:]
        i = pl.program_id(0)
        xv = x_ref[...]
        rstd = lax.rsqrt(jnp.mean(xv * xv, axis=-1, keepdims=True) + RMS_EPS)
        xh = xv * rstd
        dyv = dy_ref[...].astype(F32)
        dxh = dyv * g_ref[...]
        dx = rstd * (dxh - xh * jnp.mean(dxh * xh, axis=-1, keepdims=True))
        if has_res:
            dx = dx + res_ref[...]
        dx_ref[...] = dx
        part = (dyv * xh).reshape(tr // 8, 8, d).sum(axis=0)

        @pl.when(i == 0)
        def _():
            acc_ref[...] = part

        @pl.when(i > 0)
        def _():
            acc_ref[...] += part

        @pl.when(i == n - 1)
        def _():
            dg_ref[...] = jnp.sum(acc_ref[...], axis=0, keepdims=True)

    row = pl.BlockSpec((tr, d), lambda i: (i, 0))
    in_specs = [row, row, pl.BlockSpec((1, d), lambda i: (0, 0))] + ([row] if has_res else [])
    ops = [dy, x, gain] + ([res] if has_res else [])
    return pl.pallas_call(
        body, name=name, grid=(n,), in_specs=in_specs,
        out_specs=(row, pl.BlockSpec((1, d), lambda i: (0, 0))),
        out_shape=(jax.ShapeDtypeStruct((r, d), F32), jax.ShapeDtypeStruct((1, d), F32)),
        scratch_shapes=[pltpu.VMEM((8, d), F32)],
        compiler_params=_cparams(("arbitrary",)),
    )(*ops)


def _final_loss(name, h, target, gain, *, tr=512):
    r, d = h.shape
    tr = min(tr, r)
    n = r // tr

    def body(h_ref, t_ref, g_ref, loss_ref, dh_ref, dg_ref, accl_ref, accg_ref):
        i = pl.program_id(0)
        xv = h_ref[...]
        rstd = lax.rsqrt(jnp.mean(xv * xv, axis=-1, keepdims=True) + RMS_EPS)
        xh = xv * rstd
        e = xh * g_ref[...] - t_ref[...]
        dyv = e * (1.0 / d)
        dxh = dyv * g_ref[...]
        dh_ref[...] = rstd * (dxh - xh * jnp.mean(dxh * xh, axis=-1, keepdims=True))
        lpart = (e * e).reshape(tr // 8, 8, d).sum(axis=0)
        gpart = (dyv * xh).reshape(tr // 8, 8, d).sum(axis=0)

        @pl.when(i == 0)
        def _():
            accl_ref[...] = lpart
            accg_ref[...] = gpart

        @pl.when(i > 0)
        def _():
            accl_ref[...] += lpart
            accg_ref[...] += gpart

        @pl.when(i == n - 1)
        def _():
            tot = jnp.sum(jnp.sum(accl_ref[...], axis=0, keepdims=True), axis=1, keepdims=True)
            loss_ref[...] = jnp.broadcast_to(tot * (0.5 / d), (1, LANE))
            dg_ref[...] = jnp.sum(accg_ref[...], axis=0, keepdims=True)

    row = pl.BlockSpec((tr, d), lambda i: (i, 0))
    one = pl.BlockSpec((1, d), lambda i: (0, 0))
    return pl.pallas_call(
        body, name=name, grid=(n,), in_specs=[row, row, one],
        out_specs=(pl.BlockSpec((1, LANE), lambda i: (0, 0)), row, one),
        out_shape=(jax.ShapeDtypeStruct((1, LANE), F32), jax.ShapeDtypeStruct((r, d), F32),
                   jax.ShapeDtypeStruct((1, d), F32)),
        scratch_shapes=[pltpu.VMEM((8, d), F32), pltpu.VMEM((8, d), F32)],
        compiler_params=_cparams(("arbitrary",)),
    )(h, target, gain)


_GELU_C = math.sqrt(2.0 / math.pi)


def _gelu_parts(z):
    inner = _GELU_C * (z + 0.044715 * z * z * z)
    t = jnp.tanh(inner)
    val = 0.5 * z * (1.0 + t)
    dinner = _GELU_C * (1.0 + 3.0 * 0.044715 * z * z)
    grad = 0.5 * (1.0 + t) + 0.5 * z * (1.0 - t * t) * dinner
    return val, grad


def _ssm_post_fwd(name, y8, u8, d8, *, tr=256):
    r, c = y8.shape
    tr = min(tr, r)

    def body(y_ref, u_ref, d_ref, o_ref):
        z = y_ref[...] + d_ref[...] * u_ref[...]
        o_ref[...] = _gelu_parts(z)[0].astype(o_ref.dtype)

    row = pl.BlockSpec((tr, c), lambda i: (i, 0))
    return pl.pallas_call(
        body, name=name, grid=(r // tr,), in_specs=[row, row, pl.BlockSpec((1, c), lambda i: (0, 0))],
        out_specs=row, out_shape=jax.ShapeDtypeStruct((r, c), BF16),
        compiler_params=_cparams(("parallel",)),
    )(y8, u8, d8)


def _ssm_post_bwd(name, dact8, y8, u8, d8, *, tr=256):
    r, c = y8.shape
    tr = min(tr, r)
    n = r // tr

    def body(da_ref, y_ref, u_ref, d_ref, dz_ref, dd_ref, acc_ref):
        i = pl.program_id(0)
        uv = u_ref[...]
        z = y_ref[...] + d_ref[...] * uv
        dz = da_ref[...].astype(F32) * _gelu_parts(z)[1]
        dz_ref[...] = dz
        part = (dz * uv).reshape(tr // 8, 8, c).sum(axis=0)

        @pl.when(i == 0)
        def _():
            acc_ref[...] = part

        @pl.when(i > 0)
        def _():
            acc_ref[...] += part

        @pl.when(i == n - 1)
        def _():
            tot = jnp.sum(acc_ref[...], axis=0, keepdims=True)
            out = tot[:, 0:SSM_WIDTH]
            for j in range(1, c // SSM_WIDTH):
                out = out + tot[:, j * SSM_WIDTH:(j + 1) * SSM_WIDTH]
            dd_ref[...] = out

    row = pl.BlockSpec((tr, c), lambda i: (i, 0))
    return pl.pallas_call(
        body, name=name, grid=(n,), in_specs=[row, row, row, pl.BlockSpec((1, c), lambda i: (0, 0))],
        out_specs=(row, pl.BlockSpec((1, SSM_WIDTH), lambda i: (0, 0))),
        out_shape=(jax.ShapeDtypeStruct((r, c), F32), jax.ShapeDtypeStruct((1, SSM_WIDTH), F32)),
        scratch_shapes=[pltpu.VMEM((8, c), F32)],
        compiler_params=_cparams(("arbitrary",)),
    )(dact8, y8, u8, d8)


def _mix_fwd(name, glu, gates, out_b, *, tr=256):
    r = glu.shape[0]
    d = D_MODEL
    tr = min(tr, r)

    def body(glu_ref, gate_ref, ob_ref, o_ref):
        out_a = glu_ref[:, 0:d] * _sigmoid(glu_ref[:, d:2 * d])
        mix = _sigmoid(gate_ref[:, 0:d]) * out_a + _sigmoid(gate_ref[:, d:2 * d]) * ob_ref[...]
        o_ref[...] = mix.astype(o_ref.dtype)

    wide = pl.BlockSpec((tr, 2 * d), lambda i: (i, 0))
    row = pl.BlockSpec((tr, d), lambda i: (i, 0))
    return pl.pallas_call(
        body, name=name, grid=(r // tr,), in_specs=[wide, wide, row], out_specs=row,
        out_shape=jax.ShapeDtypeStruct((r, d), BF16), compiler_params=_cparams(("parallel",)),
    )(glu, gates, out_b)


def _mix_bwd(name, dmix, glu, gates, out_b, *, tr=256):
    r = glu.shape[0]
    d = D_MODEL
    tr = min(tr, r)

    def body(dm_ref, glu_ref, gate_ref, ob_ref, dglu_ref, dgate_ref, dob_ref):
        dm = dm_ref[...]
        glu_a = glu_ref[:, 0:d]
        sb = _sigmoid(glu_ref[:, d:2 * d])
        ga = _sigmoid(gate_ref[:, 0:d])
        gb = _sigmoid(gate_ref[:, d:2 * d])
        out_a = glu_a * sb
        dout_a = dm * ga
        dglu_ref[:, 0:d] = (dout_a * sb).astype(dglu_ref.dtype)
        dglu_ref[:, d:2 * d] = (dout_a * glu_a * sb * (1.0 - sb)).astype(dglu_ref.dtype)
        dgate_ref[:, 0:d] = (dm * out_a * ga * (1.0 - ga)).astype(dgate_ref.dtype)
        dgate_ref[:, d:2 * d] = (dm * ob_ref[...] * gb * (1.0 - gb)).astype(dgate_ref.dtype)
        dob_ref[...] = (dm * gb).astype(dob_ref.dtype)

    wide = pl.BlockSpec((tr, 2 * d), lambda i: (i, 0))
    row = pl.BlockSpec((tr, d), lambda i: (i, 0))
    return pl.pallas_call(
        body, name=name, grid=(r // tr,), in_specs=[row, wide, wide, row], out_specs=(wide, wide, row),
        out_shape=(jax.ShapeDtypeStruct((r, 2 * d), BF16), jax.ShapeDtypeStruct((r, 2 * d), BF16),
                   jax.ShapeDtypeStruct((r, d), BF16)),
        compiler_params=_cparams(("parallel",)),
    )(dmix, glu, gates, out_b)


def _swiglu_fwd(name, f, *, tr=256):
    r = f.shape[0]
    hdn = FFN_HIDDEN
    tr = min(tr, r)

    def body(f_ref, o_ref):
        fa = f_ref[:, 0:hdn]
        o_ref[...] = (fa * _sigmoid(fa) * f_ref[:, hdn:2 * hdn]).astype(o_ref.dtype)

    return pl.pallas_call(
        body, name=name, grid=(r // tr,), in_specs=[pl.BlockSpec((tr, 2 * hdn), lambda i: (i, 0))],
        out_specs=pl.BlockSpec((tr, hdn), lambda i: (i, 0)),
        out_shape=jax.ShapeDtypeStruct((r, hdn), BF16), compiler_params=_cparams(("parallel",)),
    )(f)


def _swiglu_bwd(name, dg, f, *, tr=256):
    r = f.shape[0]
    hdn = FFN_HIDDEN
    tr = min(tr, r)

    def body(dg_ref, f_ref, o_ref):
        dgv = dg_ref[...].astype(F32)
        fa = f_ref[:, 0:hdn]
        fb = f_ref[:, hdn:2 * hdn]
        s = _sigmoid(fa)
        o_ref[:, 0:hdn] = (dgv * fb * s * (1.0 + fa * (1.0 - s))).astype(o_ref.dtype)
        o_ref[:, hdn:2 * hdn] = (dgv * fa * s).astype(o_ref.dtype)

    return pl.pallas_call(
        body, name=name, grid=(r // tr,),
        in_specs=[pl.BlockSpec((tr, hdn), lambda i: (i, 0)), pl.BlockSpec((tr, 2 * hdn), lambda i: (i, 0))],
        out_specs=pl.BlockSpec((tr, 2 * hdn), lambda i: (i, 0)),
        out_shape=jax.ShapeDtypeStruct((r, 2 * hdn), BF16), compiler_params=_cparams(("parallel",)),
    )(dg, f)


def _ssm_mats(lam_re, lam_im, log_dt, b_re, b_im, c_re, c_im, nc):
    hp = lax.Precision.HIGHEST
    t = SSM_CHUNK
    nq = SSM_GROUPS // 8
    lam = lax.complex(lam_re, lam_im)
    z = lam * jnp.exp(log_dt)[:, None]
    ks = jnp.arange(t + 1, dtype=F32)
    apow = jnp.exp(ks[:, None, None] * z[None])
    bbar = ((apow[1] - 1.0) / lam)[..., None] * lax.complex(b_re, b_im)
    c = lax.complex(c_re, c_im)

    ca = c[None] * apow[:, :, None, :]
    kmat = jnp.einsum("kgnp,gpm->kgnm", ca, bbar, precision=hp).real
    ii = np.arange(t)
    lag = ii[None, :] - ii[:, None]
    kt = kmat[np.clip(lag, 0, t)] * jnp.asarray(lag >= 0, F32)[:, :, None, None, None]
    kt = kt.reshape(t, t, nq, 8, SSM_GROUP, SSM_GROUP)
    m_c = kt.transpose(2, 0, 3, 5, 1, 4).reshape(nq, 1024, LANE)

    arev = jnp.exp((float(t - 1) - ks[:t])[:, None, None] * z[None])
    w = arev[:, :, :, None] * bbar[None]
    wr = jnp.stack([w.real, w.imag]).reshape(2, t, nq, 8, SSM_STATE, SSM_GROUP)
    bw_c = wr.transpose(2, 1, 3, 5, 0, 4).reshape(nq, 1024, LANE)

    ca1 = ca[1:]
    cr = jnp.stack([ca1.real, -ca1.imag]).reshape(2, t, nq, 8, SSM_GROUP, SSM_STATE)
    cm_c = cr.transpose(2, 0, 3, 5, 1, 4).reshape(nq, 1024, LANE)

    def tiles(v):
        vq = jnp.concatenate([v.real.reshape(nq, 512), v.imag.reshape(nq, 512)], axis=1)
        return jnp.broadcast_to(vq.reshape(nq, 8, 1, LANE), (nq, 8, 8, LANE))

    return m_c, bw_c, cm_c, tiles(apow[t]), tiles(jnp.exp(float(nc) * z))


_BD_M = (LANE, SSM_GROUP)
_BD_BW = (LANE, SSM_STATE)
_BD_CM = (512, SSM_GROUP)


def _bd_perm(cn):
    rr = lax.broadcasted_iota(jnp.int32, (1024, 1024), 0)
    cc = lax.broadcasted_iota(jnp.int32, (1024, 1024), 1)
    sh = cn.bit_length() - 1
    src = ((rr >> 7) << sh) + (((rr & (LANE - 1)) >> sh) << (3 + sh)) + (rr & (cn - 1))
    return jnp.where(src == cc, 1.0, 0.0).astype(BF16)


def _bd_rowgroup(span):
    r = lax.broadcasted_iota(jnp.int32, (1024, LANE), 0)
    return (r & (span - 1)) >> ((span // 8).bit_length() - 1)


def _bd_expand(name, kind, compact):
    span, cn = kind
    nq = compact.shape[0]

    def body(c_ref, o_ref):
        x = c_ref[...]
        grp = _bd_rowgroup(span)
        xcat = jnp.concatenate([jnp.where(grp == h, x, 0.0) for h in range(8)], axis=1)
        o_ref[...] = _bdot(xcat, _bd_perm(cn), _DIMS["nn"]).astype(o_ref.dtype)

    return pl.pallas_call(
        body, name=name, grid=(nq,), in_specs=[pl.BlockSpec((None, 1024, LANE), lambda q: (q, 0, 0))],
        out_specs=pl.BlockSpec((None, 1024, 1024), lambda q: (q, 0, 0)),
        out_shape=jax.ShapeDtypeStruct((nq, 1024, 1024), BF16),
        compiler_params=_cparams(("parallel",)),
    )(compact)


def _bd_reduce(name, kind, dbig):
    span, cn = kind
    nq = dbig.shape[0]

    def body(g_ref, o_ref):
        perm = _bd_perm(cn)
        hi, mid, lo = _split3(g_ref[...])
        d = _DIMS["nt"]
        back = _bdot(hi, perm, d) + _bdot(mid, perm, d) + _bdot(lo, perm, d)
        grp = _bd_rowgroup(span)
        out = jnp.zeros((1024, LANE), F32)
        for h in range(8):
            out = jnp.where(grp == h, back[:, h * LANE:(h + 1) * LANE], out)
        o_ref[...] = out

    return pl.pallas_call(
        body, name=name, grid=(nq,), in_specs=[pl.BlockSpec((None, 1024, 1024), lambda q: (q, 0, 0))],
        out_specs=pl.BlockSpec((None, 1024, LANE), lambda q: (q, 0, 0)),
        out_shape=jax.ShapeDtypeStruct((nq, 1024, LANE), F32),
        compiler_params=_cparams(("parallel",)),
    )(dbig)


def _x_tile_specs(nc, nq):
    return [pl.BlockSpec((nc, LANE), lambda q, t, i=i: (0, i * nq + q)) for i in range(SSM_CHUNK)]


def _cat_tiles(refs):
    return jnp.concatenate([r[...] for r in refs], axis=1)


def _ssm_w(name, x8, bw):
    nc = x8.shape[0]
    nq = bw.shape[0]

    def body(*refs):
        xq = _cat_tiles(refs[:8])
        refs[9][...] = _bdot(xq, refs[8][...], _DIMS["nn"])

    return pl.pallas_call(
        body, name=name, grid=(nq, 8),
        in_specs=_x_tile_specs(nc, nq) + [pl.BlockSpec((None, 1024, LANE), lambda q, t: (q, 0, t))],
        out_specs=pl.BlockSpec((None, None, nc, LANE), lambda q, t: (q, t, 0, 0)),
        out_shape=jax.ShapeDtypeStruct((nq, 8, nc, LANE), F32),
        compiler_params=_cparams(("parallel", "arbitrary")),
    )(*([x8] * 8), bw)


def _ssm_scan(name, w4, a_t, aseg_t, *, reverse, sprev4=None):
    nq, _, nc, _ = w4.shape
    ns = nc // 8
    with_da = sprev4 is not None

    def body(*refs):
        w_ref, a_ref, aseg_ref = refs[:3]
        s_ref = refs[3] if with_da else None
        o_ref = refs[4] if with_da else refs[3]
        da_ref = refs[5] if with_da else None
        sgn = -1.0 if reverse else 1.0
        ar = [a_ref[j] for j in range(4)]
        ai = [sgn * a_ref[j + 4] for j in range(4)]
        gr = [aseg_ref[j] for j in range(4)]
        gi = [sgn * aseg_ref[j + 4] for j in range(4)]
        zero = tuple(jnp.zeros((8, LANE), F32) for _ in range(8))

        def rows(tt):
            return pl.ds((ns - 1 - tt) if reverse else tt, 8, stride=ns)

        def step(carry, w):
            new_r = [ar[j] * carry[j] - ai[j] * carry[j + 4] + w[j] for j in range(4)]
            new_i = [ar[j] * carry[j + 4] + ai[j] * carry[j] + w[j + 4] for j in range(4)]
            return tuple(new_r + new_i)

        def pass1(tt, carry):
            return step(carry, [w_ref[j, rows(tt), :] for j in range(8)])

        ends = lax.fori_loop(0, ns, pass1, zero)
        sub = lax.broadcasted_iota(jnp.int32, (8, LANE), 0)
        init = list(zero)
        order = range(7, 0, -1) if reverse else range(0, 7)
        for s in order:
            nxt = s - 1 if reverse else s + 1
            cand_r = [gr[j] * init[j] - gi[j] * init[j + 4] + ends[j] for j in range(4)]
            cand_i = [gr[j] * init[j + 4] + gi[j] * init[j] + ends[j + 4] for j in range(4)]
            cand = cand_r + cand_i
            shift = 7 if reverse else 1
            init = [jnp.where(sub == nxt, pltpu.roll(cand[j], shift, axis=0), init[j]) for j in range(8)]

        def pass2(tt, state):
            carry, acc = state
            r = rows(tt)
            for j in range(8):
                o_ref[j, r, :] = carry[j]
            if with_da:
                sp = [s_ref[j, r, :] for j in range(8)]
                acc_r = [acc[j] + carry[j] * sp[j] + carry[j + 4] * sp[j + 4] for j in range(4)]
                acc_i = [acc[j + 4] + carry[j + 4] * sp[j] - carry[j] * sp[j + 4] for j in range(4)]
                acc = tuple(acc_r + acc_i)
            return step(carry, [w_ref[j, r, :] for j in range(8)]), acc

        _, acc = lax.fori_loop(0, ns, pass2, (tuple(init), zero))
        if with_da:
            for j in range(8):
                da_ref[j] = acc[j]

    big = pl.BlockSpec((None, 8, nc, LANE), lambda q: (q, 0, 0, 0))
    small = pl.BlockSpec((None, 8, 8, LANE), lambda q: (q, 0, 0, 0))
    in_specs = [big, small, small] + ([big] if with_da else [])
    ops = [w4, a_t, aseg_t] + ([sprev4] if with_da else [])
    out_specs = (big, small) if with_da else big
    big_s = jax.ShapeDtypeStruct((nq, 8, nc, LANE), F32)
    out_shape = (big_s, jax.ShapeDtypeStruct((nq, 8, 8, LANE), F32)) if with_da else big_s
    return pl.pallas_call(
        body, name=name, grid=(nq,), in_specs=in_specs, out_specs=out_specs, out_shape=out_shape,
        compiler_params=_cparams(("parallel",)),
    )(*ops)


def _ssm_y(name, x8, sprev4, m_mat, cm_mat):
    nc = x8.shape[0]
    nq = m_mat.shape[0]

    def body(*refs):
        xq = _cat_tiles(refs[:8])
        s_ref, m_ref, cm_ref, o_ref = refs[8:12]
        sq = jnp.concatenate([s_ref[t] for t in range(8)], axis=1)
        o_ref[...] = _bdot(xq, m_ref[...], _DIMS["nn"]) + _bdot(sq, cm_ref[...], _DIMS["nn"])

    col = pl.BlockSpec((None, 1024, LANE), lambda q, j: (q, 0, j))
    return pl.pallas_call(
        body, name=name, grid=(nq, 8),
        in_specs=_x_tile_specs(nc, nq) + [pl.BlockSpec((None, 8, nc, LANE), lambda q, j: (q, 0, 0, 0)), col, col],
        out_specs=pl.BlockSpec((nc, LANE), lambda q, j: (0, j * nq + q)),
        out_shape=jax.ShapeDtypeStruct((nc, 8 * SSM_WIDTH), F32),
        compiler_params=_cparams(("parallel", "arbitrary")),
    )(*([x8] * 8), sprev4, m_mat, cm_mat)


def _ssm_ds(name, dz8, sprev4, cm_mat):
    nc = dz8.shape[0]
    nq = cm_mat.shape[0]

    def body(*refs):
        dyq = _cat_tiles(refs[:8]).astype(BF16)
        s_ref, cm_ref, ds_ref, dcm_ref = refs[8:12]
        ds_ref[...] = _bdot(dyq, cm_ref[...], _DIMS["nt"])
        dcm_ref[...] = _bdot(s_ref[...], dyq, _DIMS["tn"])

    tile = pl.BlockSpec((None, None, nc, LANE), lambda q, t: (q, t, 0, 0))
    rowblk = pl.BlockSpec((None, LANE, 1024), lambda q, t: (q, t, 0))
    return pl.pallas_call(
        body, name=name, grid=(nq, 8),
        in_specs=_x_tile_specs(nc, nq) + [tile, rowblk],
        out_specs=(tile, rowblk),
        out_shape=(jax.ShapeDtypeStruct((nq, 8, nc, LANE), F32), jax.ShapeDtypeStruct((nq, 1024, 1024), F32)),
        compiler_params=_cparams(("parallel", "arbitrary")),
    )(*([dz8] * 8), sprev4, cm_mat)


def _ssm_dx(name, dz8, g4, x8, m_mat, bw_mat, d8):
    nc = dz8.shape[0]
    nq = m_mat.shape[0]

    def body(*refs):
        dyq = _cat_tiles(refs[:8]).astype(BF16)
        g_ref, x_ref, m_ref, bw_ref, d_ref, dzi_ref, dx_ref, dm_ref, dbw_ref = refs[8:17]
        gq = jnp.concatenate([g_ref[t] for t in range(8)], axis=1).astype(BF16)
        dx = _bdot(dyq, m_ref[...], _DIMS["nt"]) + _bdot(gq, bw_ref[...], _DIMS["nt"])
        dx_ref[...] = (dx + d_ref[...] * dzi_ref[...]).astype(dx_ref.dtype)
        xi = x_ref[...]
        dm_ref[...] = _bdot(xi, dyq, _DIMS["tn"])
        dbw_ref[...] = _bdot(xi, gq, _DIMS["tn"])

    xtile = pl.BlockSpec((nc, LANE), lambda q, i: (0, i * nq + q))
    rowblk = pl.BlockSpec((None, LANE, 1024), lambda q, i: (q, i, 0))
    return pl.pallas_call(
        body, name=name, grid=(nq, 8),
        in_specs=_x_tile_specs(nc, nq) + [pl.BlockSpec((None, 8, nc, LANE), lambda q, i: (q, 0, 0, 0)), xtile, rowblk, rowblk,
                                          pl.BlockSpec((1, LANE), lambda q, i: (0, q)), xtile],
        out_specs=(xtile, rowblk, rowblk),
        out_shape=(jax.ShapeDtypeStruct((nc, 8 * SSM_WIDTH), BF16), jax.ShapeDtypeStruct((nq, 1024, 1024), F32),
                   jax.ShapeDtypeStruct((nq, 1024, 1024), F32)),
        compiler_params=_cparams(("parallel", "arbitrary")),
    )(*([dz8] * 8), g4, x8, m_mat, bw_mat, d8, dz8)


CUM_BLK = 256


def _split3(x):
    hi = x.astype(BF16)
    r1 = x - hi.astype(F32)
    mid = r1.astype(BF16)
    lo = (r1 - mid.astype(F32)).astype(BF16)
    return hi, mid, lo


def _tri_dot(x, tri):
    hi, mid, lo = _split3(x)
    d = _DIMS["nn"]
    return _bdot(hi, tri, d) + _bdot(mid, tri, d) + _bdot(lo, tri, d)


def _tri(n, lower):
    r = lax.broadcasted_iota(jnp.int32, (n, n), 0)
    c = lax.broadcasted_iota(jnp.int32, (n, n), 1)
    return jnp.where((r >= c) if lower else (r <= c), 1.0, 0.0).astype(BF16)


def _fox_cum(name, fproj, bcol):
    seq = fproj.shape[0]
    blk = min(CUM_BLK, seq)

    def body(f_ref, b_ref, o_ref, carry_ref):
        i = pl.program_id(0)

        @pl.when(i == 0)
        def _():
            carry_ref[...] = jnp.zeros_like(carry_ref)

        z = f_ref[...].T + b_ref[...]
        logf = jnp.minimum(z, 0.0) - jnp.log(1.0 + jnp.exp(-jnp.abs(z)))
        carry = carry_ref[...]
        cum = _tri_dot(logf, _tri(blk, lower=False)) + jnp.tile(carry, (1, blk // LANE))
        o_ref[...] = cum[0:8, :]
        carry_ref[...] = carry + jnp.sum(logf, axis=1, keepdims=True)

    return pl.pallas_call(
        body, name=name, grid=(seq // blk,),
        in_specs=[pl.BlockSpec((blk, LANE), lambda i: (i, 0)), pl.BlockSpec((LANE, 1), lambda i: (0, 0))],
        out_specs=pl.BlockSpec((8, blk), lambda i: (0, i)),
        out_shape=jax.ShapeDtypeStruct((8, seq), F32),
        scratch_shapes=[pltpu.VMEM((LANE, LANE), F32)],
        compiler_params=_cparams(("arbitrary",)),
    )(fproj, bcol)


def _fox_cum_bwd(name, dcum_t, fproj, bcol):
    seq = fproj.shape[0]
    blk = min(CUM_BLK, seq)
    n = seq // blk

    def body(dc_ref, f_ref, b_ref, df_ref, db_ref, carry_ref, acc_ref):
        i = pl.program_id(0)

        @pl.when(i == 0)
        def _():
            carry_ref[...] = jnp.zeros_like(carry_ref)
            acc_ref[...] = jnp.zeros_like(acc_ref)

        dc = jnp.concatenate([dc_ref[...], jnp.zeros((LANE - 8, blk), F32)], axis=0)
        carry = carry_ref[...]
        dlogf = _tri_dot(dc, _tri(blk, lower=True)) + jnp.tile(carry, (1, blk // LANE))
        carry_ref[...] = carry + jnp.sum(dc, axis=1, keepdims=True)
        z = f_ref[...].T + b_ref[...]
        dft = dlogf / (1.0 + jnp.exp(z))
        df_ref[...] = dft.T.astype(df_ref.dtype)
        acc_ref[...] += jnp.sum(dft, axis=1, keepdims=True)

        @pl.when(i == n - 1)
        def _():
            db_ref[...] = acc_ref[...]

    return pl.pallas_call(
        body, name=name, grid=(n,),
        in_specs=[pl.BlockSpec((8, blk), lambda i: (0, n - 1 - i)), pl.BlockSpec((blk, LANE), lambda i: (n - 1 - i, 0)),
                  pl.BlockSpec((LANE, 1), lambda i: (0, 0))],
        out_specs=(pl.BlockSpec((blk, LANE), lambda i: (n - 1 - i, 0)), pl.BlockSpec((LANE, LANE), lambda i: (0, 0))),
        out_shape=(jax.ShapeDtypeStruct((seq, LANE), BF16), jax.ShapeDtypeStruct((LANE, LANE), F32)),
        scratch_shapes=[pltpu.VMEM((LANE, LANE), F32), pltpu.VMEM((LANE, LANE), F32)],
        compiler_params=_cparams(("arbitrary",)),
    )(dcum_t, fproj, bcol)


FOX_BLK = 512
FOX_SCALE = FOX_HEAD_DIM ** -0.5


def _fox_head_mask(shape, hh):
    lane = lax.broadcasted_iota(jnp.int32, shape, 1)
    return (lane < FOX_HEAD_DIM) if hh == 0 else (lane >= FOX_HEAD_DIM)


def _fox_bias(cum_ref, hh, q0, k0, blk):
    c0 = jnp.max(cum_ref[hh:hh + 1, pl.ds(q0, LANE)], axis=1, keepdims=True)
    return c0 - cum_ref[hh:hh + 1, pl.ds(k0, blk)]


def _fox_fwd(name, qkv, cum_t):
    seq = qkv.shape[0]
    blk = min(FOX_BLK, seq)
    nb = seq // blk
    npair = FOX_HEADS // 2

    def body(q_ref, k_ref, v_ref, cum_ref, o_ref, lse_ref):
        iq = pl.program_id(1)
        q0 = pl.multiple_of(iq * blk, blk)
        qv = q_ref[...]
        row = lax.broadcasted_iota(jnp.int32, (blk, blk), 0)
        col = lax.broadcasted_iota(jnp.int32, (blk, blk), 1)
        qhs = [jnp.where(_fox_head_mask(qv.shape, hh), qv, jnp.zeros_like(qv)) * FOX_SCALE for hh in range(2)]

        def block(kb, states, masked):
            k0 = pl.multiple_of(kb * blk, blk)
            kv = k_ref[pl.ds(k0, blk), :]
            vv = v_ref[pl.ds(k0, blk), :]
            new = []
            for hh in range(2):
                m, l, acc = states[hh]
                s = _bdot(qhs[hh], kv, _DIMS["nt"]) + _fox_bias(cum_ref, hh, q0, k0, blk)
                if masked:
                    s = jnp.where(row >= col, s, -jnp.inf)
                m_new = jnp.maximum(m, jnp.max(s, axis=1, keepdims=True))
                alpha = jnp.exp(m - m_new)
                p = jnp.exp(s - m_new)
                l = alpha * l + jnp.sum(p, axis=1, keepdims=True)
                acc = alpha * acc + _bdot(p, vv, _DIMS["nn"])
                new.append((m_new, l, acc))
            return tuple(new)

        init = (jnp.full((blk, 1), -jnp.inf, F32), jnp.zeros((blk, 1), F32), jnp.zeros((blk, LANE), F32))
        states = lax.fori_loop(0, iq, lambda kb, st: block(kb, st, False), (init, init))
        states = block(iq, states, True)
        outs = []
        for hh in range(2):
            m, l, acc = states[hh]
            outs.append(acc / l)
            lse_ref[hh] = jnp.broadcast_to(m + jnp.log(l), (blk, LANE))
        o_ref[...] = jnp.where(_fox_head_mask(outs[0].shape, 0), outs[0], outs[1]).astype(o_ref.dtype)

    return pl.pallas_call(
        body, name=name, grid=(npair, nb),
        in_specs=[pl.BlockSpec((blk, LANE), lambda p, i: (i, p)),
                  pl.BlockSpec((seq, LANE), lambda p, i: (0, npair + p)),
                  pl.BlockSpec((seq, LANE), lambda p, i: (0, 2 * npair + p)),
                  pl.BlockSpec((None, 2, seq), lambda p, i: (p, 0, 0))],
        out_specs=(pl.BlockSpec((blk, LANE), lambda p, i: (i, p)),
                   pl.BlockSpec((2, blk, LANE), lambda p, i: (p, i, 0))),
        out_shape=(jax.ShapeDtypeStruct((seq, FOX_WIDTH), BF16), jax.ShapeDtypeStruct((FOX_HEADS, seq, LANE), F32)),
        compiler_params=_cparams(("parallel", "arbitrary")),
    )(qkv, qkv, qkv, cum_t)


def _fox_bwd(name, qkv, cum_t, att, datt, lse):
    seq = qkv.shape[0]
    blk = min(FOX_BLK, seq)
    nb = seq // blk
    npair = FOX_HEADS // 2

    def body(q_ref, k_ref, v_ref, cum_ref, o_ref, do_ref, lse_ref, dq_ref, dk_ref, dv_ref, dcum_ref):
        iq = pl.program_id(1)
        q0 = pl.multiple_of(iq * blk, blk)

        @pl.when(iq == 0)
        def _():
            dk_ref[...] = jnp.zeros_like(dk_ref)
            dv_ref[...] = jnp.zeros_like(dv_ref)
            dcum_ref[...] = jnp.zeros_like(dcum_ref)

        qv = q_ref[...]
        dov = do_ref[...].astype(F32)
        ov = o_ref[...].astype(F32)
        row = lax.broadcasted_iota(jnp.int32, (blk, blk), 0)
        col = lax.broadcasted_iota(jnp.int32, (blk, blk), 1)
        qhs, dohbs, deltas, lses = [], [], [], []
        for hh in range(2):
            hm = _fox_head_mask(qv.shape, hh)
            qhs.append(jnp.where(hm, qv, jnp.zeros_like(qv)) * FOX_SCALE)
            doh = jnp.where(hm, dov, 0.0)
            dohbs.append(doh.astype(BF16))
            deltas.append(jnp.sum(doh * ov, axis=1, keepdims=True))
            lses.append(jnp.tile(lse_ref[hh], (1, blk // LANE)))

        def block(kb, accs, masked):
            k0 = pl.multiple_of(kb * blk, blk)
            kv = k_ref[pl.ds(k0, blk), :]
            vv = v_ref[pl.ds(k0, blk), :]
            new = []
            dk_blk = None
            dv_blk = None
            for hh in range(2):
                dq_acc, rs_acc = accs[hh]
                s = _bdot(qhs[hh], kv, _DIMS["nt"]) + _fox_bias(cum_ref, hh, q0, k0, blk)
                p = jnp.exp(s - lses[hh])
                if masked:
                    p = jnp.where(row >= col, p, 0.0)
                dp = _bdot(dohbs[hh], vv, _DIMS["nt"])
                ds = p * (dp - deltas[hh])
                dsb = ds.astype(BF16)
                dk_h = _bdot(dsb, qhs[hh], _DIMS["tn"])
                dv_h = _bdot(p, dohbs[hh], _DIMS["tn"])
                dk_blk = dk_h if dk_blk is None else dk_blk + dk_h
                dv_blk = dv_h if dv_blk is None else dv_blk + dv_h
                dcum_ref[hh:hh + 1, pl.ds(k0, blk)] -= jnp.sum(ds, axis=0, keepdims=True)
                new.append((dq_acc + _bdot(dsb, kv, _DIMS["nn"]), rs_acc + jnp.sum(ds, axis=1, keepdims=True)))
            dk_ref[pl.ds(k0, blk), :] += dk_blk
            dv_ref[pl.ds(k0, blk), :] += dv_blk
            return tuple(new)

        init = (jnp.zeros((blk, LANE), F32), jnp.zeros((blk, 1), F32))
        accs = lax.fori_loop(0, iq, lambda kb, a: block(kb, a, False), (init, init))
        accs = block(iq, accs, True)
        for hh in range(2):
            dcum_ref[hh:hh + 1, pl.ds(q0, blk)] += jnp.broadcast_to(accs[hh][1], (blk, LANE)).T[0:1, :]
        dq = jnp.where(_fox_head_mask(qv.shape, 0), accs[0][0], accs[1][0]) * FOX_SCALE
        dq_ref[...] = dq.astype(dq_ref.dtype)

    qblk = pl.BlockSpec((blk, LANE), lambda p, i: (i, p))
    full = pl.BlockSpec((seq, LANE), lambda p, i: (0, p))
    return pl.pallas_call(
        body, name=name, grid=(npair, nb),
        in_specs=[qblk,
                  pl.BlockSpec((seq, LANE), lambda p, i: (0, npair + p)),
                  pl.BlockSpec((seq, LANE), lambda p, i: (0, 2 * npair + p)),
                  pl.BlockSpec((None, 2, seq), lambda p, i: (p, 0, 0)),
                  qblk, qblk,
                  pl.BlockSpec((2, blk, LANE), lambda p, i: (p, i, 0))],
        out_specs=(qblk, full, full, pl.BlockSpec((None, 2, seq), lambda p, i: (p, 0, 0))),
        out_shape=(jax.ShapeDtypeStruct((seq, FOX_WIDTH), BF16), jax.ShapeDtypeStruct((seq, FOX_WIDTH), F32),
                   jax.ShapeDtypeStruct((seq, FOX_WIDTH), F32), jax.ShapeDtypeStruct((npair, 2, seq), F32)),
        compiler_params=_cparams(("arbitrary", "arbitrary")),
    )(qkv, qkv, qkv, cum_t, att, datt, lse)


MEM_SCALE = MEM_HEAD_DIM ** -0.5


def _mem_probs(qh, kh):
    s = _bdot(qh, kh, _DIMS["nt"]) * MEM_SCALE
    p = jnp.exp(s - jnp.max(s, axis=1, keepdims=True))
    return p / jnp.sum(p, axis=1, keepdims=True)


def _mem_fwd(name, q2, kv, *, tr=512):
    seq = q2.shape[0]
    mlen = kv.shape[0]
    tr = min(tr, seq)

    def body(q_ref, kv_ref, o_ref):
        for h in range(MEM_HEADS):
            sl = slice(h * MEM_HEAD_DIM, (h + 1) * MEM_HEAD_DIM)
            sv = slice(MEM_WIDTH + h * MEM_HEAD_DIM, MEM_WIDTH + (h + 1) * MEM_HEAD_DIM)
            p = _mem_probs(q_ref[:, sl], kv_ref[:, sl])
            o_ref[:, sl] = _bdot(p, kv_ref[:, sv], _DIMS["nn"]).astype(o_ref.dtype)

    return pl.pallas_call(
        body, name=name, grid=(seq // tr,),
        in_specs=[pl.BlockSpec((tr, MEM_WIDTH), lambda i: (i, 0)), pl.BlockSpec((mlen, 2 * MEM_WIDTH), lambda i: (0, 0))],
        out_specs=pl.BlockSpec((tr, MEM_WIDTH), lambda i: (i, 0)),
        out_shape=jax.ShapeDtypeStruct((seq, MEM_WIDTH), BF16),
        compiler_params=_cparams(("parallel",)),
    )(q2, kv)


def _mem_bwd(name, q2, kv, do2, *, tr=512):
    seq = q2.shape[0]
    mlen = kv.shape[0]
    tr = min(tr, seq)

    def body(q_ref, kv_ref, do_ref, dq_ref, dkv_ref):
        i = pl.program_id(0)

        @pl.when(i == 0)
        def _():
            dkv_ref[...] = jnp.zeros_like(dkv_ref)

        for h in range(MEM_HEADS):
            sl = slice(h * MEM_HEAD_DIM, (h + 1) * MEM_HEAD_DIM)
            sv = slice(MEM_WIDTH + h * MEM_HEAD_DIM, MEM_WIDTH + (h + 1) * MEM_HEAD_DIM)
            qh = q_ref[:, sl]
            kh = kv_ref[:, sl]
            doh = do_ref[:, sl].astype(BF16)
            p = _mem_probs(qh, kh)
            dp = _bdot(doh, kv_ref[:, sv], _DIMS["nt"])
            ds = (p * (dp - jnp.sum(p * dp, axis=1, keepdims=True)) * MEM_SCALE).astype(BF16)
            dq_ref[:, sl] = _bdot(ds, kh, _DIMS["nn"]).astype(dq_ref.dtype)
            dkv_ref[:, sl] += _bdot(ds, qh, _DIMS["tn"])
            dkv_ref[:, sv] += _bdot(p, doh, _DIMS["tn"])

    row = pl.BlockSpec((tr, MEM_WIDTH), lambda i: (i, 0))
    kvs = pl.BlockSpec((mlen, 2 * MEM_WIDTH), lambda i: (0, 0))
    return pl.pallas_call(
        body, name=name, grid=(seq // tr,), in_specs=[row, kvs, row], out_specs=(row, kvs),
        out_shape=(jax.ShapeDtypeStruct((seq, MEM_WIDTH), BF16), jax.ShapeDtypeStruct((mlen, 2 * MEM_WIDTH), F32)),
        compiler_params=_cparams(("arbitrary",)),
    )(q2, kv, do2)


_HBM = pl.BlockSpec(memory_space=pl.ANY)
_MESH = pl.DeviceIdType.MESH


def _mesh_place():
    x, y, c = lax.axis_index("x"), lax.axis_index("y"), lax.axis_index("c")
    other_chips = [(1 - x, y), (x, 1 - y), (1 - x, 1 - y)]
    return x, y, c, other_chips


def _gather_all(name, arrays):
    n = len(arrays)

    def body(*refs):
        ins, outs = refs[:n], refs[n:2 * n]
        send_sems, recv_sems, local_sems = refs[2 * n:]
        x, y, c, chips = _mesh_place()
        me, sibling = (x, y, c), (x, y, 1 - c)

        def slot(a, place):
            px, py, pc = place
            return outs[a].at[4 * px + 2 * py + pc]

        def copy(a, k, block, to, src=None):
            return pltpu.make_async_remote_copy(
                src_ref=slot(a, block) if src is None else src, dst_ref=slot(a, block),
                send_sem=send_sems.at[a, k], recv_sem=recv_sems.at[a, k], device_id=to, device_id_type=_MESH)

        mine = [pltpu.make_async_copy(ins[a], slot(a, me), local_sems.at[a]) for a in range(n)]
        for cp in mine:
            cp.start()
        first = []
        for a in range(n):
            first.append(copy(a, 0, me, sibling, src=ins[a]))
            first += [copy(a, 1 + j, me, (*chip, c), src=ins[a]) for j, chip in enumerate(chips)]
        for cp in first:
            cp.start()
        passed = []
        for j, chip in enumerate(chips):
            for a in range(n):
                copy(a, 1 + j, (*chip, c), me).wait_recv()
                fwd = copy(a, 4 + j, (*chip, c), sibling)
                fwd.start()
                passed.append(fwd)
        for a in range(n):
            copy(a, 0, sibling, me).wait_recv()
            for j, chip in enumerate(chips):
                copy(a, 4 + j, (*chip, 1 - c), me).wait_recv()
        for cp in first + passed:
            cp.wait_send()
        for cp in mine:
            cp.wait()

    out_shape = tuple(jax.ShapeDtypeStruct((N_DEV,) + arr.shape, arr.dtype) for arr in arrays)
    return pl.pallas_call(
        body, name=name, in_specs=[_HBM] * n, out_specs=tuple([_HBM] * n), out_shape=out_shape,
        scratch_shapes=[pltpu.SemaphoreType.DMA((n, N_DEV - 1)), pltpu.SemaphoreType.DMA((n, N_DEV - 1)),
                        pltpu.SemaphoreType.DMA((n,))],
    )(*arrays)


def _scatter_sibling(name, arrays):
    n = len(arrays)

    def body(*refs):
        ins, owns, sibs = refs[:n], refs[n:2 * n], refs[2 * n:3 * n]
        send_sems, recv_sems, local_sems = refs[3 * n:]
        x, y, c, _ = _mesh_place()
        copies = []
        for a in range(n):
            for j in range(4):
                local = pltpu.make_async_copy(ins[a].at[2 * j + c], owns[a].at[j], local_sems.at[a, j])
                rdma = pltpu.make_async_remote_copy(
                    src_ref=ins[a].at[2 * j + (1 - c)], dst_ref=sibs[a].at[j], send_sem=send_sems.at[a, j],
                    recv_sem=recv_sems.at[a, j], device_id=(x, y, 1 - c), device_id_type=_MESH)
                local.start()
                rdma.start()
                copies += [local, rdma]
        for cp in copies:
            cp.wait()

    four = tuple(jax.ShapeDtypeStruct((4,) + arr.shape[1:], arr.dtype) for arr in arrays)
    outs = pl.pallas_call(
        body, name=name, in_specs=[_HBM] * n, out_specs=tuple([_HBM] * (2 * n)), out_shape=four + four,
        scratch_shapes=[pltpu.SemaphoreType.DMA((n, 4)), pltpu.SemaphoreType.DMA((n, 4)),
                        pltpu.SemaphoreType.DMA((n, 4))],
    )(*arrays)
    return outs[:n], outs[n:]


def _add_pairs(name, lhs, rhs):
    n = len(lhs)

    def body(*refs):
        for a in range(n):
            refs[2 * n + a][...] = (refs[a][...].astype(F32) + refs[n + a][...].astype(F32)).astype(refs[2 * n + a].dtype)

    specs = [pl.BlockSpec((None,) + arr.shape[1:], lambda j: (j, 0, 0)) for arr in lhs]
    return pl.pallas_call(
        body, name=name, grid=(4,), in_specs=specs + specs, out_specs=tuple(specs),
        out_shape=tuple(jax.ShapeDtypeStruct(arr.shape, arr.dtype) for arr in lhs),
        compiler_params=_cparams(("parallel",)),
    )(*lhs, *rhs)


def _scatter_chips(name, arrays):
    n = len(arrays)

    def body(*refs):
        ins, outs = refs[:n], refs[n:2 * n]
        send_sems, recv_sems, local_sems = refs[2 * n:]
        x, y, c, chips = _mesh_place()
        my_chip = 2 * x + y
        copies = []
        for a in range(n):
            local = pltpu.make_async_copy(ins[a].at[my_chip], outs[a].at[my_chip], local_sems.at[a])
            local.start()
            copies.append(local)
            for k, (px, py) in enumerate(chips):
                rdma = pltpu.make_async_remote_copy(
                    src_ref=ins[a].at[2 * px + py], dst_ref=outs[a].at[my_chip], send_sem=send_sems.at[a, k],
                    recv_sem=recv_sems.at[a, k], device_id=(px, py, c), device_id_type=_MESH)
                rdma.start()
                copies.append(rdma)
        for cp in copies:
            cp.wait()

    return pl.pallas_call(
        body, name=name, in_specs=[_HBM] * n, out_specs=tuple([_HBM] * n),
        out_shape=tuple(jax.ShapeDtypeStruct(arr.shape, arr.dtype) for arr in arrays),
        scratch_shapes=[pltpu.SemaphoreType.DMA((n, 3)), pltpu.SemaphoreType.DMA((n, 3)),
                        pltpu.SemaphoreType.DMA((n,))],
    )(*arrays)


def _unstack_cols(name, stacked):
    n, rows, cols = stacked.shape

    def body(i_ref, o_ref):
        o_ref[...] = i_ref[...]

    return pl.pallas_call(
        body, name=name, grid=(n,), in_specs=[pl.BlockSpec((None, rows, cols), lambda k: (k, 0, 0))],
        out_specs=pl.BlockSpec((rows, cols), lambda k: (0, k)),
        out_shape=jax.ShapeDtypeStruct((rows, n * cols), stacked.dtype),
        compiler_params=_cparams(("parallel",)),
    )(stacked)


def _restack_cols(name, mat):
    rows, width = mat.shape
    cols = width // N_DEV

    def body(i_ref, o_ref):
        o_ref[...] = i_ref[...]

    return pl.pallas_call(
        body, name=name, grid=(N_DEV,), in_specs=[pl.BlockSpec((rows, cols), lambda k: (0, k))],
        out_specs=pl.BlockSpec((None, rows, cols), lambda k: (k, 0, 0)),
        out_shape=jax.ShapeDtypeStruct((N_DEV, rows, cols), mat.dtype),
        compiler_params=_cparams(("parallel",)),
    )(mat)


def _remap_pieces(runs):
    plan = {}
    for du, dc, su, sc, ln in runs:
        while ln > 0:
            lane = dc % LANE
            take = min(ln, LANE - lane)
            plan.setdefault((du, dc // LANE), []).append((su, sc, take, lane))
            dc, sc, ln = dc + take, sc + take, ln - take
    return plan


def _remap(name, srcs, src_units, runs, *, out_units, out_cols, out_dtype, tr=256):
    rows = srcs[0].shape[-2]
    tr = min(tr, rows)
    plan = _remap_pieces(runs)
    n_src = len(srcs)
    stacked_out = out_units is not None
    n_tiles = out_cols // LANE

    def body(*refs):
        o_ref = refs[n_src]

        def src_tile(unit, t):
            ai, lead = src_units[unit]
            ref = refs[ai]
            sl = slice(t * LANE, (t + 1) * LANE)
            return (ref[:, sl] if lead is None else ref[lead, :, sl]).astype(F32)

        lane = lax.broadcasted_iota(jnp.int32, (tr, LANE), 1)
        for du in range(out_units if stacked_out else 1):
            for t in range(n_tiles):
                acc = jnp.zeros((tr, LANE), F32)
                for su, sc, ln, dl in plan.get((du if stacked_out else None, t), []):
                    st, so = sc // LANE, sc % LANE
                    first = src_tile(su, st)
                    if so == dl and so + ln <= LANE:
                        piece = first
                    else:
                        second = src_tile(su, st + 1) if so + ln > LANE else first
                        both = jnp.concatenate([first, second], axis=1)
                        piece = pltpu.roll(both, (dl - so) % (2 * LANE), axis=1)[:, 0:LANE]
                    acc = piece if (dl == 0 and ln == LANE) else jnp.where(
                        jnp.logical_and(lane >= dl, lane < dl + ln), piece, acc)
                if stacked_out:
                    o_ref[du, :, t * LANE:(t + 1) * LANE] = acc.astype(o_ref.dtype)
                else:
                    o_ref[:, t * LANE:(t + 1) * LANE] = acc.astype(o_ref.dtype)

    in_specs = []
    for arr in srcs:
        if arr.ndim == 2:
            in_specs.append(pl.BlockSpec((tr, arr.shape[1]), lambda i: (i, 0)))
        else:
            in_specs.append(pl.BlockSpec((arr.shape[0], tr, arr.shape[2]), lambda i: (0, i, 0)))
    if stacked_out:
        out_spec = pl.BlockSpec((out_units, tr, out_cols), lambda i: (0, i, 0))
        out_shape = jax.ShapeDtypeStruct((out_units, rows, out_cols), out_dtype)
    else:
        out_spec = pl.BlockSpec((tr, out_cols), lambda i: (i, 0))
        out_shape = jax.ShapeDtypeStruct((rows, out_cols), out_dtype)
    return pl.pallas_call(
        body, name=name, grid=(rows // tr,), in_specs=in_specs, out_specs=out_spec, out_shape=out_shape,
        compiler_params=_cparams(("parallel",)),
    )(*srcs)


def _proj_col(c):
    if c < PROJ_GATE0:
        return c
    if c < PROJ_GATE0 + FOX_HEADS:
        return PROJ_F0 + (c - PROJ_GATE0)
    return c - FOX_HEADS


def _win_runs():
    cuts = sorted(set([0, PROJ_GATE0, PROJ_GATE0 + FOX_HEADS, IN_WIDTH] + [SHARD_IN * k for k in range(N_DEV + 1)]))
    return [(lo // SHARD_IN, lo % SHARD_IN, _proj_col(lo), hi - lo) for lo, hi in zip(cuts[:-1], cuts[1:])]


def _assemble_win(name, stacked):
    runs = [(None, pc, k, sc, ln) for k, sc, pc, ln in _win_runs()]
    return _remap(name, [stacked], [(0, k) for k in range(N_DEV)], runs,
                  out_units=None, out_cols=PROJ_WIDTH, out_dtype=BF16)


def _disassemble_dwin(name, dw):
    runs = [(k, sc, 0, pc, ln) for k, sc, pc, ln in _win_runs()]
    return _remap(name, [dw], [(0, None)], runs, out_units=N_DEV, out_cols=SHARD_IN_PAD, out_dtype=BF16)


def _concat_cols(name, parts, *, tr=512):
    rows = parts[0].shape[0]
    tr = min(tr, rows)
    widths = [p.shape[1] for p in parts]
    total = sum(widths)

    def body(*refs):
        o_ref = refs[len(parts)]
        lo = 0
        for r, w in zip(refs[:len(parts)], widths):
            o_ref[:, lo:lo + w] = r[...].astype(o_ref.dtype)
            lo += w

    return pl.pallas_call(
        body, name=name, grid=(rows // tr,),
        in_specs=[pl.BlockSpec((tr, w), lambda i: (i, 0)) for w in widths],
        out_specs=pl.BlockSpec((tr, total), lambda i: (i, 0)),
        out_shape=jax.ShapeDtypeStruct((rows, total), BF16),
        compiler_params=_cparams(("parallel",)),
    )(*parts)


def _assemble_wffn(name, stacked):
    runs = [(None, SHARD_FFN * k, k, 0, SHARD_FFN) for k in range(N_DEV)]
    return _remap(name, [stacked], [(0, k) for k in range(N_DEV)], runs,
                  out_units=None, out_cols=2 * FFN_HIDDEN, out_dtype=BF16)


def _disassemble_dwffn(name, dw):
    runs = [(k, 0, 0, SHARD_FFN * k, SHARD_FFN) for k in range(N_DEV)]
    return _remap(name, [dw], [(0, None)], runs, out_units=N_DEV, out_cols=SHARD_FFN_PAD, out_dtype=BF16)


def _adamw(name, parts, w, m, v, *, tr=128):
    rows, cols = w.shape
    n_parts = parts.shape[0]
    tr = min(tr, rows)
    assert rows % tr == 0, (name, rows, tr)
    c1 = 1.0 - ADAM_B1 ** ADAM_STEP
    c2 = 1.0 - ADAM_B2 ** ADAM_STEP

    def body(p_ref, w_ref, m_ref, v_ref, g_ref, d_ref, nm_ref, nv_ref):
        g = p_ref[0].astype(F32)
        for s in range(1, n_parts):
            g = g + p_ref[s].astype(F32)
        m_new = ADAM_B1 * m_ref[...] + (1.0 - ADAM_B1) * g
        v_new = ADAM_B2 * v_ref[...] + (1.0 - ADAM_B2) * (g * g)
        upd = (m_new / c1) / (jnp.sqrt(v_new / c2) + ADAM_EPS) + ADAM_WD * w_ref[...]
        g_ref[...] = g
        d_ref[...] = -ADAM_LR * upd
        nm_ref[...] = m_new
        nv_ref[...] = v_new

    row = pl.BlockSpec((tr, cols), lambda i: (i, 0))
    out = jax.ShapeDtypeStruct((rows, cols), F32)
    return pl.pallas_call(
        body, name=name, grid=(rows // tr,),
        in_specs=[pl.BlockSpec((n_parts, tr, cols), lambda i: (0, i, 0)), row, row, row],
        out_specs=(row, row, row, row), out_shape=(out, out, out, out),
        compiler_params=_cparams(("parallel",)),
    )(parts, w, m, v)


_WEIGHTS = ("norm_mix", "w_in", "b_forget", "lam_re", "lam_im", "log_dt", "b_re", "b_im", "c_re", "c_im",
            "d_skip", "w_glu", "w_fox_o", "w_mix_out", "norm_mem_q", "norm_mem_kv", "w_mem_q", "w_mem_kv",
            "w_mem_o", "norm_ffn", "w_ffn_in", "w_ffn_out", "norm_final")
_SHARDED = ("w_in", "w_glu", "w_fox_o", "w_mix_out", "w_mem_q", "w_mem_kv", "w_mem_o", "w_ffn_in", "w_ffn_out")
_SMALL = tuple(n for n in _WEIGHTS if n not in _SHARDED)
_PACK_COLS = 1024


def _pack(arrays):
    flat = jnp.concatenate([a.reshape(-1).astype(F32) for a in arrays])
    rows = -(-flat.shape[0] // _PACK_COLS)
    return jnp.pad(flat, (0, rows * _PACK_COLS - flat.shape[0])).reshape(rows, _PACK_COLS)


def _unpack(buf, like):
    flat = buf.reshape(-1)
    out, pos = [], 0
    for a in like:
        out.append(flat[pos:pos + a.size].reshape(a.shape))
        pos += a.size
    return out


def _mm(name, a, b, mode, m, n, k, out_dtype, tm=1024, tn=512, tk=1024, **kw):
    return _matmul(name, a, b, mode, m, n, k, out_dtype=out_dtype, tm=tm, tn=tn, tk=tk, **kw)


def kernel(x, mem, norm_mix, w_in, b_forget, lam_re, lam_im, log_dt, b_re, b_im, c_re, c_im, d_skip, w_glu, w_fox_o, w_mix_out, norm_mem_q, norm_mem_kv, w_mem_q, w_mem_kv, w_mem_o, norm_ffn, w_ffn_in, w_ffn_out, norm_final, loss_target, m_norm_mix, m_w_in, m_b_forget, m_lam_re, m_lam_im, m_log_dt, m_b_re, m_b_im, m_c_re, m_c_im, m_d_skip, m_w_glu, m_w_fox_o, m_w_mix_out, m_norm_mem_q, m_norm_mem_kv, m_w_mem_q, m_w_mem_kv, m_w_mem_o, m_norm_ffn, m_w_ffn_in, m_w_ffn_out, m_norm_final, v_norm_mix, v_w_in, v_b_forget, v_lam_re, v_lam_im, v_log_dt, v_b_re, v_b_im, v_c_re, v_c_im, v_d_skip, v_w_glu, v_w_fox_o, v_w_mix_out, v_norm_mem_q, v_norm_mem_kv, v_w_mem_q, v_w_mem_kv, v_w_mem_o, v_norm_ffn, v_w_ffn_in, v_w_ffn_out, v_norm_final):
    given = dict(locals())
    weights = {n: given[n] for n in _WEIGHTS}
    mom_m = {n: given["m_" + n] for n in _WEIGHTS}
    mom_v = {n: given["v_" + n] for n in _WEIGHTS}
    seq = x.shape[1]
    nc = seq // SSM_CHUNK
    d = D_MODEL
    xs, mems, tgt = x[0], mem[0], loss_target[0]

    def padcols(a, width):
        return jnp.pad(a, ((0, 0), (0, width - a.shape[1])))

    shards = [padcols(w_in[0].astype(BF16), SHARD_IN_PAD), w_glu[0].astype(BF16), w_fox_o[0].astype(BF16),
              w_mix_out[0].astype(BF16), w_mem_q[0].astype(BF16), w_mem_kv[0].astype(BF16),
              w_mem_o[0].astype(BF16), padcols(w_ffn_in[0].astype(BF16), SHARD_FFN_PAD), w_ffn_out[0].astype(BF16)]
    gathered = _gather_all("gather_weights", shards)
    win = _assemble_win("assemble_w_in", gathered[0])
    wglu = _unstack_cols("unstack_w_glu", gathered[1])
    wfoxo = _unstack_cols("unstack_w_fox_o", gathered[2])
    wmo = _unstack_cols("unstack_w_mem_o", gathered[6])
    wmix = gathered[3].reshape(d, d)
    wmq = gathered[4].reshape(d, MEM_WIDTH)
    wmkv = gathered[5].reshape(d, 2 * MEM_WIDTH)
    wffn_in = _assemble_wffn("assemble_w_ffn_in", gathered[7])
    wffn_out = gathered[8].reshape(FFN_HIDDEN, d)

    u = _rms_fwd("rms_mix", xs, norm_mix)
    ussm = _mm("proj_ssm", u, win, "nn", seq, SSM_WIDTH, d, F32)
    qkv = _mm("proj_qkv", u, win, "nn", seq, 3 * FOX_WIDTH, d, BF16, tn=512, b_off=(0, SSM_WIDTH))
    gates = _mm("proj_gates", u, win, "nn", seq, 2 * d, d, F32, tn=1024, b_off=(0, PROJ_GATE0))
    fproj = _mm("proj_forget", u, win, "nn", seq, LANE, d, F32, tn=LANE, b_off=(0, PROJ_F0))

    ssm_params = (lam_re[0], lam_im[0], log_dt[0], b_re[0], b_im[0], c_re[0], c_im[0])
    (m_c, bw_c, cm_c, a8, aseg), mats_vjp = jax.vjp(lambda *p: _ssm_mats(*p, nc), *ssm_params)
    m_b = _bd_expand("ssm_expand_m", _BD_M, m_c)
    bw_b = _bd_expand("ssm_expand_bw", _BD_BW, bw_c)
    cm_b = _bd_expand("ssm_expand_cm", _BD_CM, cm_c)
    u8 = ussm.reshape(nc, SSM_CHUNK * SSM_WIDTH)
    d8 = jnp.tile(d_skip, (1, SSM_CHUNK))
    w4 = _ssm_w("ssm_w", u8, bw_b)
    sp4 = _ssm_scan("ssm_scan", w4, a8, aseg, reverse=False)
    y8 = _ssm_y("ssm_y", u8, sp4, m_b, cm_b)
    act = _ssm_post_fwd("ssm_act", y8, u8, d8).reshape(seq, SSM_WIDTH)
    glu = _mm("glu", act, wglu, "nn", seq, 2 * d, SSM_WIDTH, F32, tn=1024)

    bcol = jnp.pad(b_forget[0], (0, LANE - FOX_HEADS)).reshape(LANE, 1)
    cum_t = _fox_cum("fox_cum", fproj, bcol).reshape(FOX_HEADS // 2, 2, seq)
    att, lse = _fox_fwd("fox_fwd", qkv, cum_t)
    out_b = _mm("fox_out", att, wfoxo, "nn", seq, d, FOX_WIDTH, F32, tn=1024)

    mixin = _mix_fwd("mix", glu, gates, out_b)
    h1 = _mm("mix_out", mixin, wmix, "nn", seq, d, d, F32, tn=1024, add=xs)

    n1 = _rms_fwd("rms_mem_q", h1, norm_mem_q)
    q2 = _mm("mem_q", n1, wmq, "nn", seq, MEM_WIDTH, d, BF16)
    mn = _rms_fwd("rms_mem_kv", mems, norm_mem_kv)
    mlen = mems.shape[0]
    kv = _mm("mem_kv", mn, wmkv, "nn", mlen, 2 * MEM_WIDTH, d, BF16)
    o2 = _mem_fwd("mem_attn", q2, kv)
    h2 = _mm("mem_out", o2, wmo, "nn", seq, d, MEM_WIDTH, F32, tn=1024, add=h1)

    n2 = _rms_fwd("rms_ffn", h2, norm_ffn)
    f = _mm("ffn_in", n2, wffn_in, "nn", seq, 2 * FFN_HIDDEN, d, F32, tn=1408)
    g_act = _swiglu_fwd("swiglu", f)
    h3 = _mm("ffn_out", g_act, wffn_out, "nn", seq, d, FFN_HIDDEN, F32, tk=FFN_HIDDEN, add=h2)
    loss_part, dh3, dg_final = _final_loss("final_loss", h3, tgt, norm_final.reshape(1, d))

    dg_act = _mm("d_ffn_out_x", dh3, wffn_out, "nt", seq, FFN_HIDDEN, d, F32, tn=1408)
    dwffn_out = _mm("d_ffn_out_w", g_act, dh3, "tn", FFN_HIDDEN, d, seq, BF16, tm=1408, tn=1024)
    df = _swiglu_bwd("d_swiglu", dg_act, f)
    dn2 = _mm("d_ffn_in_x", df, wffn_in, "nt", seq, d, 2 * FFN_HIDDEN, F32, tn=1024, tk=FFN_HIDDEN)
    dwffn_in = _mm("d_ffn_in_w", n2, df, "tn", d, 2 * FFN_HIDDEN, seq, BF16, tn=1408)
    dh2, dg_ffn = _rms_bwd("d_rms_ffn", dn2, h2, norm_ffn, res=dh3)

    do2 = _mm("d_mem_out_x", dh2, wmo, "nt", seq, MEM_WIDTH, d, F32)
    dwmo = _restack_cols("restack_d_w_mem_o", _mm("d_mem_out_w", o2, dh2, "tn", MEM_WIDTH, d, seq, BF16, tn=1024))
    dq2, dkv = _mem_bwd("d_mem_attn", q2, kv, do2)
    dwmq = _mm("d_mem_q_w", n1, dq2, "tn", d, MEM_WIDTH, seq, BF16)
    dn1 = _mm("d_mem_q_x", dq2, wmq, "nt", seq, d, MEM_WIDTH, F32)
    dwmkv = _mm("d_mem_kv_w", mn, dkv, "tn", d, 2 * MEM_WIDTH, mlen, BF16, tn=1024)
    dmn = _mm("d_mem_kv_x", dkv, wmkv, "nt", mlen, d, 2 * MEM_WIDTH, F32)
    _, dg_memkv = _rms_bwd("d_rms_mem_kv", dmn, mems, norm_mem_kv)
    dh1, dg_memq = _rms_bwd("d_rms_mem_q", dn1, h1, norm_mem_q, res=dh2)

    dmixin = _mm("d_mix_out_x", dh1, wmix, "nt", seq, d, d, F32, tn=1024)
    dwmix = _mm("d_mix_out_w", mixin, dh1, "tn", d, d, seq, BF16, tn=1024)
    dglu, dgates, dout_b = _mix_bwd("d_mix", dmixin, glu, gates, out_b)
    datt = _mm("d_fox_out_x", dout_b, wfoxo, "nt", seq, FOX_WIDTH, d, F32)
    dwfoxo = _restack_cols("restack_d_w_fox_o", _mm("d_fox_out_w", att, dout_b, "tn", FOX_WIDTH, d, seq, BF16, tn=1024))
    dact = _mm("d_glu_x", dglu, wglu, "nt", seq, SSM_WIDTH, 2 * d, F32, tk=2 * d)
    dwglu = _restack_cols("restack_d_w_glu", _mm("d_glu_w", act, dglu, "tn", SSM_WIDTH, 2 * d, seq, BF16, tn=2 * d))

    dz8, dg_dskip = _ssm_post_bwd("d_ssm_act", dact.reshape(nc, SSM_CHUNK * SSM_WIDTH), y8, u8, d8)
    ds4, dcm = _ssm_ds("d_ssm_y_state", dz8, sp4, cm_b)
    g4, da8 = _ssm_scan("d_ssm_scan", ds4, a8, aseg, reverse=True, sprev4=sp4)
    dx8, dm, dbw = _ssm_dx("d_ssm_x", dz8, g4, u8, m_b, bw_b, d8)
    dussm = dx8.reshape(seq, SSM_WIDTH)
    g_ssm = mats_vjp((_bd_reduce("ssm_reduce_dm", _BD_M, dm), _bd_reduce("ssm_reduce_dbw", _BD_BW, dbw),
                      _bd_reduce("ssm_reduce_dcm", _BD_CM, dcm), da8, jnp.zeros_like(aseg)))

    dq, dk, dv, dcum = _fox_bwd("d_fox", qkv, cum_t, att, datt, lse)
    dfproj, dbf = _fox_cum_bwd("d_fox_cum", dcum.reshape(FOX_HEADS, seq), fproj, bcol)
    dg_bforget = dbf[0:FOX_HEADS, 0].reshape(1, FOX_HEADS)

    dproj = _concat_cols("d_proj_concat", (dussm, dq, dk, dv, dgates, dfproj))
    du = _mm("d_proj_x", dproj, win, "nt", seq, d, PROJ_WIDTH, F32, tn=1024, tk=1408)
    dwin = _mm("d_proj_w", u, dproj, "tn", d, PROJ_WIDTH, seq, BF16, tn=1408)
    dx, dg_mix = _rms_bwd("d_rms_mix", du, xs, norm_mix, res=dh1)

    pieces = [_disassemble_dwin("split_d_w_in", dwin), dwglu, dwfoxo,
              dwmix.reshape(N_DEV, d // N_DEV, d), dwmq.reshape(N_DEV, d // N_DEV, MEM_WIDTH),
              dwmkv.reshape(N_DEV, d // N_DEV, 2 * MEM_WIDTH), dwmo,
              _disassemble_dwffn("split_d_w_ffn_in", dwffn_in),
              dwffn_out.reshape(N_DEV, FFN_HIDDEN // N_DEV, d)]
    own, sib = _scatter_sibling("scatter_grads_sibling", pieces)
    received = _scatter_chips("scatter_grads_chips", _add_pairs("sum_grads_chip", own, sib))

    small_grads = dict(zip(
        _SMALL, (dg_mix, dg_bforget, g_ssm[0][None], g_ssm[1][None], g_ssm[2][None], g_ssm[3][None], g_ssm[4][None],
                 g_ssm[5][None], g_ssm[6][None], dg_dskip, dg_memq, dg_memkv, dg_ffn, dg_final.reshape(d))))
    small_like = [weights[n] for n in _SMALL]
    small_all = _gather_all("gather_small_grads", [_pack([small_grads[n] for n in _SMALL])])[0]
    pk = [_pack([src[n] for n in _SMALL]) for src in (weights, mom_m, mom_v)]
    small_out = _adamw("adamw_small", small_all, pk[0], pk[1], pk[2], tr=small_all.shape[1])
    small_res = [dict(zip(_SMALL, _unpack(buf, small_like))) for buf in small_out]

    results = [dict(r) for r in small_res]
    tiles = {"w_in": 128, "w_glu": 128, "w_fox_o": 128, "w_mix_out": 128, "w_mem_q": 128, "w_mem_kv": 128,
             "w_mem_o": 128, "w_ffn_in": 128, "w_ffn_out": 176}
    pads = {"w_in": SHARD_IN_PAD, "w_ffn_in": SHARD_FFN_PAD}
    for name, parts in zip(_SHARDED, received):
        w2, m2, v2 = weights[name][0], mom_m[name][0], mom_v[name][0]
        cols = w2.shape[1]
        if name in pads:
            w2, m2, v2 = (padcols(t, pads[name]) for t in (w2, m2, v2))
        outs = _adamw("adamw_" + name, parts, w2, m2, v2, tr=tiles[name])
        for res, o in zip(results, outs):
            res[name] = o[:, :cols][None]

    loss = lax.psum(loss_part[0, 0], ("x", "y", "c"))
    out = [loss, dx[None]]
    for res in results:
        out.extend(res[n] for n in _WEIGHTS)
    return tuple(out)
```

```python
import math

import jax
import jax.numpy as jnp
import numpy as np
from jax import lax
from jax.experimental import pallas as pl
from jax.experimental.pallas import tpu as pltpu

F32 = jnp.float32
BF16 = jnp.bfloat16

N_DEV = 8
LANE = 128
VMEM_LIMIT = 56 * 1024 * 1024

D_MODEL = 1024
SSM_GROUP = 16
SSM_GROUPS = 32
SSM_WIDTH = 512
SSM_STATE = 64
SSM_CHUNK = 8
FOX_HEADS = 8
FOX_HEAD_DIM = 64
FOX_WIDTH = 512
MEM_HEADS = 4
MEM_HEAD_DIM = 128
MEM_WIDTH = 512
FFN_HIDDEN = 2816
RMS_EPS = 1e-6
IN_WIDTH = 4104
SHARD_IN = IN_WIDTH // N_DEV
SHARD_IN_PAD = 640
SHARD_FFN = 2 * FFN_HIDDEN // N_DEV
SHARD_FFN_PAD = 768
PROJ_GATE0 = 2048
PROJ_F0 = 4096
PROJ_WIDTH = 4224

ADAM_LR = 0.001
ADAM_B1 = 0.9
ADAM_B2 = 0.999
ADAM_EPS = 1e-08
ADAM_WD = 0.01
ADAM_STEP = 10


def _cparams(sem=None):
    return pltpu.CompilerParams(dimension_semantics=sem, vmem_limit_bytes=VMEM_LIMIT)


def _sigmoid(x):
    return 1.0 / (1.0 + jnp.exp(-x))


def _bdot(a, b, dims):
    return lax.dot_general(a.astype(BF16), b.astype(BF16), ((dims[0], dims[1]), ((), ())),
                           preferred_element_type=F32)


_DIMS = {"nn": ((1,), (0,)), "nt": ((1,), (1,)), "tn": ((0,), (0,))}


def _matmul(name, a, b, mode, m, n, k, *, out_dtype, tm, tn, tk, a_off=(0, 0), b_off=(0, 0), add=None):
    tm, tn, tk = min(tm, m), min(tn, n), min(tk, k)
    assert m % tm == 0 and n % tn == 0 and k % tk == 0, (name, m, n, k, tm, tn, tk)
    nk = k // tk
    grid = (m // tm, n // tn, nk)

    def blk(off, t):
        assert off % t == 0, (name, off, t)
        return off // t

    if mode in ("nn", "nt"):
        ar, ac = blk(a_off[0], tm), blk(a_off[1], tk)
        a_spec = pl.BlockSpec((tm, tk), lambda i, j, kk: (i + ar, kk + ac))
    else:
        ar, ac = blk(a_off[0], tk), blk(a_off[1], tm)
        a_spec = pl.BlockSpec((tk, tm), lambda i, j, kk: (kk + ar, i + ac))

    if mode in ("nn", "tn"):
        br, bc = blk(b_off[0], tk), blk(b_off[1], tn)
        b_spec = pl.BlockSpec((tk, tn), lambda i, j, kk: (kk + br, j + bc))
    else:
        br, bc = blk(b_off[0], tn), blk(b_off[1], tk)
        b_spec = pl.BlockSpec((tn, tk), lambda i, j, kk: (j + br, kk + bc))
    o_spec = pl.BlockSpec((tm, tn), lambda i, j, kk: (i, j))
    out_shape = jax.ShapeDtypeStruct((m, n), out_dtype)

    in_specs = [a_spec, b_spec]
    operands = [a, b]
    if add is not None:
        in_specs.append(pl.BlockSpec((tm, tn), lambda i, j, kk: (i, j)))
        operands.append(add)
    dims = _DIMS[mode]
    has_add = add is not None

    def body(*refs):
        a_ref, b_ref = refs[0], refs[1]
        add_ref = refs[2] if has_add else None
        o_ref = refs[3] if has_add else refs[2]
        acc_ref = refs[-1] if nk > 1 else None
        prod = _bdot(a_ref[...], b_ref[...], dims)

        def finish(total):
            if has_add:
                total = total + add_ref[...].astype(F32)
            o_ref[...] = total.astype(o_ref.dtype)

        if nk == 1:
            finish(prod)
        else:
            kk = pl.program_id(2)

            @pl.when(kk == 0)
            def _():
                acc_ref[...] = prod

            @pl.when(jnp.logical_and(kk > 0, kk < nk - 1))
            def _():
                acc_ref[...] += prod

            @pl.when(kk == nk - 1)
            def _():
                finish(acc_ref[...] + prod)

    scratch = [pltpu.VMEM((tm, tn), F32)] if nk > 1 else []
    return pl.pallas_call(
        body, name=name, grid=grid, in_specs=in_specs, out_specs=o_spec, out_shape=out_shape,
        scratch_shapes=scratch,
        compiler_params=_cparams(("parallel", "parallel", "arbitrary")),
    )(*operands)


def _rms_fwd(name, x, gain, *, tr=512):
    r, d = x.shape
    tr = min(tr, r)

    def body(x_ref, g_ref, o_ref):
        xv = x_ref[...]
        rstd = lax.rsqrt(jnp.mean(xv * xv, axis=-1, keepdims=True) + RMS_EPS)
        o_ref[...] = (xv * rstd * g_ref[...]).astype(o_ref.dtype)

    return pl.pallas_call(
        body, name=name, grid=(r // tr,),
        in_specs=[pl.BlockSpec((tr, d), lambda i: (i, 0)), pl.BlockSpec((1, d), lambda i: (0, 0))],
        out_specs=pl.BlockSpec((tr, d), lambda i: (i, 0)),
        out_shape=jax.ShapeDtypeStruct((r, d), BF16),
        compiler_params=_cparams(("parallel",)),
    )(x, gain)


def _rms_bwd(name, dy, x, gain, res=None, *, tr=512):
    r, d = x.shape
    tr = min(tr, r)
    n = r // tr
    has_res = res is not None

    def body(*refs):
        dy_ref, x_ref, g_ref = refs[:3]
        res_ref = refs[3] if has_res else None
        dx_ref, dg_ref, acc_ref = refs[-3:]
        i = pl.program_id(0)
        xv = x_ref[...]
        rstd = lax.rsqrt(jnp.mean(xv * xv, axis=-1, keepdims=True) + RMS_EPS)
        xh = xv * rstd
        dyv = dy_ref[...].astype(F32)
        dxh = dyv * g_ref[...]
        dx = rstd * (dxh - xh * jnp.mean(dxh * xh, axis=-1, keepdims=True))
        if has_res:
            dx = dx + res_ref[...]
        dx_ref[...] = dx
        part = (dyv * xh).reshape(tr // 8, 8, d).sum(axis=0)

        @pl.when(i == 0)
        def _():
            acc_ref[...] = part

        @pl.when(i > 0)
        def _():
            acc_ref[...] += part

        @pl.when(i == n - 1)
        def _():
            dg_ref[...] = jnp.sum(acc_ref[...], axis=0, keepdims=True)

    row = pl.BlockSpec((tr, d), lambda i: (i, 0))
    in_specs = [row, row, pl.BlockSpec((1, d), lambda i: (0, 0))] + ([row] if has_res else [])
    ops = [dy, x, gain] + ([res] if has_res else [])
    return pl.pallas_call(
        body, name=name, grid=(n,), in_specs=in_specs,
        out_specs=(row, pl.BlockSpec((1, d), lambda i: (0, 0))),
        out_shape=(jax.ShapeDtypeStruct((r, d), F32), jax.ShapeDtypeStruct((1, d), F32)),
        scratch_shapes=[pltpu.VMEM((8, d), F32)],
        compiler_params=_cparams(("arbitrary",)),
    )(*ops)


def _final_loss(name, h, target, gain, *, tr=512):
    r, d = h.shape
    tr = min(tr, r)
    n = r // tr

    def body(h_ref, t_ref, g_ref, loss_ref, dh_ref, dg_ref, accl_ref, accg_ref):
        i = pl.program_id(0)
        xv = h_ref[...]
        rstd = lax.rsqrt(jnp.mean(xv * xv, axis=-1, keepdims=True) + RMS_EPS)
        xh = xv * rstd
        e = xh * g_ref[...] - t_ref[...]
        dyv = e * (1.0 / d)
        dxh = dyv * g_ref[...]
        dh_ref[...] = rstd * (dxh - xh * jnp.mean(dxh * xh, axis=-1, keepdims=True))
        lpart = (e * e).reshape(tr // 8, 8, d).sum(axis=0)
        gpart = (dyv * xh).reshape(tr // 8, 8, d).sum(axis=0)

        @pl.when(i == 0)
        def _():
            accl_ref[...] = lpart
            accg_ref[...] = gpart

        @pl.when(i > 0)
        def _():
            accl_ref[...] += lpart
            accg_ref[...] += gpart

        @pl.when(i == n - 1)
        def _():
            tot = jnp.sum(jnp.sum(accl_ref[...], axis=0, keepdims=True), axis=1, keepdims=True)
            loss_ref[...] = jnp.broadcast_to(tot * (0.5 / d), (1, LANE))
            dg_ref[...] = jnp.sum(accg_ref[...], axis=0, keepdims=True)

    row = pl.BlockSpec((tr, d), lambda i: (i, 0))
    one = pl.BlockSpec((1, d), lambda i: (0, 0))
    return pl.pallas_call(
        body, name=name, grid=(n,), in_specs=[row, row, one],
        out_specs=(pl.BlockSpec((1, LANE), lambda i: (0, 0)), row, one),
        out_shape=(jax.ShapeDtypeStruct((1, LANE), F32), jax.ShapeDtypeStruct((r, d), F32),
                   jax.ShapeDtypeStruct((1, d), F32)),
        scratch_shapes=[pltpu.VMEM((8, d), F32), pltpu.VMEM((8, d), F32)],
        compiler_params=_cparams(("arbitrary",)),
    )(h, target, gain)


_GELU_C = math.sqrt(2.0 / math.pi)


def _gelu_parts(z):
    inner = _GELU_C * (z + 0.044715 * z * z * z)
    t = jnp.tanh(inner)
    val = 0.5 * z * (1.0 + t)
    dinner = _GELU_C * (1.0 + 3.0 * 0.044715 * z * z)
    grad = 0.5 * (1.0 + t) + 0.5 * z * (1.0 - t * t) * dinner
    return val, grad


def _ssm_post_fwd(name, y8, u8, d8, *, tr=256):
    r, c = y8.shape
    tr = min(tr, r)

    def body(y_ref, u_ref, d_ref, o_ref):
        z = y_ref[...] + d_ref[...] * u_ref[...]
        o_ref[...] = _gelu_parts(z)[0].astype(o_ref.dtype)

    row = pl.BlockSpec((tr, c), lambda i: (i, 0))
    return pl.pallas_call(
        body, name=name, grid=(r // tr,), in_specs=[row, row, pl.BlockSpec((1, c), lambda i: (0, 0))],
        out_specs=row, out_shape=jax.ShapeDtypeStruct((r, c), BF16),
        compiler_params=_cparams(("parallel",)),
    )(y8, u8, d8)


def _ssm_post_bwd(name, dact8, y8, u8, d8, *, tr=256):
    r, c = y8.shape
    tr = min(tr, r)
    n = r // tr

    def body(da_ref, y_ref, u_ref, d_ref, dz_ref, dd_ref, acc_ref):
        i = pl.program_id(0)
        uv = u_ref[...]
        z = y_ref[...] + d_ref[...] * uv
        dz = da_ref[...].astype(F32) * _gelu_parts(z)[1]
        dz_ref[...] = dz
        part = (dz * uv).reshape(tr // 8, 8, c).sum(axis=0)

        @pl.when(i == 0)
        def _():
            acc_ref[...] = part

        @pl.when(i > 0)
        def _():
            acc_ref[...] += part

        @pl.when(i == n - 1)
        def _():
            tot = jnp.sum(acc_ref[...], axis=0, keepdims=True)
            out = tot[:, 0:SSM_WIDTH]
            for j in range(1, c // SSM_WIDTH):
                out = out + tot[:, j * SSM_WIDTH:(j + 1) * SSM_WIDTH]
            dd_ref[...] = out

    row = pl.BlockSpec((tr, c), lambda i: (i, 0))
    return pl.pallas_call(
        body, name=name, grid=(n,), in_specs=[row, row, row, pl.BlockSpec((1, c), lambda i: (0, 0))],
        out_specs=(row, pl.BlockSpec((1, SSM_WIDTH), lambda i: (0, 0))),
        out_shape=(jax.ShapeDtypeStruct((r, c), F32), jax.ShapeDtypeStruct((1, SSM_WIDTH), F32)),
        scratch_shapes=[pltpu.VMEM((8, c), F32)],
        compiler_params=_cparams(("arbitrary",)),
    )(dact8, y8, u8, d8)


def _mix_fwd(name, glu, gates, out_b, *, tr=256):
    r = glu.shape[0]
    d = D_MODEL
    tr = min(tr, r)

    def body(glu_ref, gate_ref, ob_ref, o_ref):
        out_a = glu_ref[:, 0:d] * _sigmoid(glu_ref[:, d:2 * d])
        mix = _sigmoid(gate_ref[:, 0:d]) * out_a + _sigmoid(gate_ref[:, d:2 * d]) * ob_ref[...]
        o_ref[...] = mix.astype(o_ref.dtype)

    wide = pl.BlockSpec((tr, 2 * d), lambda i: (i, 0))
    row = pl.BlockSpec((tr, d), lambda i: (i, 0))
    return pl.pallas_call(
        body, name=name, grid=(r // tr,), in_specs=[wide, wide, row], out_specs=row,
        out_shape=jax.ShapeDtypeStruct((r, d), BF16), compiler_params=_cparams(("parallel",)),
    )(glu, gates, out_b)


def _mix_bwd(name, dmix, glu, gates, out_b, *, tr=256):
    r = glu.shape[0]
    d = D_MODEL
    tr = min(tr, r)

    def body(dm_ref, glu_ref, gate_ref, ob_ref, dglu_ref, dgate_ref, dob_ref):
        dm = dm_ref[...]
        glu_a = glu_ref[:, 0:d]
        sb = _sigmoid(glu_ref[:, d:2 * d])
        ga = _sigmoid(gate_ref[:, 0:d])
        gb = _sigmoid(gate_ref[:, d:2 * d])
        out_a = glu_a * sb
        dout_a = dm * ga
        dglu_ref[:, 0:d] = (dout_a * sb).astype(dglu_ref.dtype)
        dglu_ref[:, d:2 * d] = (dout_a * glu_a * sb * (1.0 - sb)).astype(dglu_ref.dtype)
        dgate_ref[:, 0:d] = (dm * out_a * ga * (1.0 - ga)).astype(dgate_ref.dtype)
        dgate_ref[:, d:2 * d] = (dm * ob_ref[...] * gb * (1.0 - gb)).astype(dgate_ref.dtype)
        dob_ref[...] = (dm * gb).astype(dob_ref.dtype)

    wide = pl.BlockSpec((tr, 2 * d), lambda i: (i, 0))
    row = pl.BlockSpec((tr, d), lambda i: (i, 0))
    return pl.pallas_call(
        body, name=name, grid=(r // tr,), in_specs=[row, wide, wide, row], out_specs=(wide, wide, row),
        out_shape=(jax.ShapeDtypeStruct((r, 2 * d), BF16), jax.ShapeDtypeStruct((r, 2 * d), BF16),
                   jax.ShapeDtypeStruct((r, d), BF16)),
        compiler_params=_cparams(("parallel",)),
    )(dmix, glu, gates, out_b)


def _swiglu_fwd(name, f, *, tr=256):
    r = f.shape[0]
    hdn = FFN_HIDDEN
    tr = min(tr, r)

    def body(f_ref, o_ref):
        fa = f_ref[:, 0:hdn]
        o_ref[...] = (fa * _sigmoid(fa) * f_ref[:, hdn:2 * hdn]).astype(o_ref.dtype)

    return pl.pallas_call(
        body, name=name, grid=(r // tr,), in_specs=[pl.BlockSpec((tr, 2 * hdn), lambda i: (i, 0))],
        out_specs=pl.BlockSpec((tr, hdn), lambda i: (i, 0)),
        out_shape=jax.ShapeDtypeStruct((r, hdn), BF16), compiler_params=_cparams(("parallel",)),
    )(f)


def _swiglu_bwd(name, dg, f, *, tr=256):
    r = f.shape[0]
    hdn = FFN_HIDDEN
    tr = min(tr, r)

    def body(dg_ref, f_ref, o_ref):
        dgv = dg_ref[...].astype(F32)
        fa = f_ref[:, 0:hdn]
        fb = f_ref[:, hdn:2 * hdn]
        s = _sigmoid(fa)
        o_ref[:, 0:hdn] = (dgv * fb * s * (1.0 + fa * (1.0 - s))).astype(o_ref.dtype)
        o_ref[:, hdn:2 * hdn] = (dgv * fa * s).astype(o_ref.dtype)

    return pl.pallas_call(
        body, name=name, grid=(r // tr,),
        in_specs=[pl.BlockSpec((tr, hdn), lambda i: (i, 0)), pl.BlockSpec((tr, 2 * hdn), lambda i: (i, 0))],
        out_specs=pl.BlockSpec((tr, 2 * hdn), lambda i: (i, 0)),
        out_shape=jax.ShapeDtypeStruct((r, 2 * hdn), BF16), compiler_params=_cparams(("parallel",)),
    )(dg, f)


def _ssm_mats(lam_re, lam_im, log_dt, b_re, b_im, c_re, c_im, nc):
    hp = lax.Precision.HIGHEST
    t = SSM_CHUNK
    nq = SSM_GROUPS // 8
    lam = lax.complex(lam_re, lam_im)
    z = lam * jnp.exp(log_dt)[:, None]
    ks = jnp.arange(t + 1, dtype=F32)
    apow = jnp.exp(ks[:, None, None] * z[None])
    bbar = ((apow[1] - 1.0) / lam)[..., None] * lax.complex(b_re, b_im)
    c = lax.complex(c_re, c_im)

    ca = c[None] * apow[:, :, None, :]
    kmat = jnp.einsum("kgnp,gpm->kgnm", ca, bbar, precision=hp).real
    ii = np.arange(t)
    lag = ii[None, :] - ii[:, None]
    kt = kmat[np.clip(lag, 0, t)] * jnp.asarray(lag >= 0, F32)[:, :, None, None, None]
    kt = kt.reshape(t, t, nq, 8, SSM_GROUP, SSM_GROUP)
    m_c = kt.transpose(2, 0, 3, 5, 1, 4).reshape(nq, 1024, LANE)

    arev = jnp.exp((float(t - 1) - ks[:t])[:, None, None] * z[None])
    w = arev[:, :, :, None] * bbar[None]
    wr = jnp.stack([w.real, w.imag]).reshape(2, t, nq, 8, SSM_STATE, SSM_GROUP)
    bw_c = wr.transpose(2, 1, 3, 5, 0, 4).reshape(nq, 1024, LANE)

    ca1 = ca[1:]
    cr = jnp.stack([ca1.real, -ca1.imag]).reshape(2, t, nq, 8, SSM_GROUP, SSM_STATE)
    cm_c = cr.transpose(2, 0, 3, 5, 1, 4).reshape(nq, 1024, LANE)

    def tiles(v):
        vq = jnp.concatenate([v.real.reshape(nq, 512), v.imag.reshape(nq, 512)], axis=1)
        return jnp.broadcast_to(vq.reshape(nq, 8, 1, LANE), (nq, 8, 8, LANE))

    return m_c, bw_c, cm_c, tiles(apow[t]), tiles(jnp.exp(float(nc) * z))


_BD_M = (LANE, SSM_GROUP)
_BD_BW = (LANE, SSM_STATE)
_BD_CM = (512, SSM_GROUP)


def _bd_perm(cn):
    rr = lax.broadcasted_iota(jnp.int32, (1024, 1024), 0)
    cc = lax.broadcasted_iota(jnp.int32, (1024, 1024), 1)
    sh = cn.bit_length() - 1
    src = ((rr >> 7) << sh) + (((rr & (LANE - 1)) >> sh) << (3 + sh)) + (rr & (cn - 1))
    return jnp.where(src == cc, 1.0, 0.0).astype(BF16)


def _bd_rowgroup(span):
    r = lax.broadcasted_iota(jnp.int32, (1024, LANE), 0)
    return (r & (span - 1)) >> ((span // 8).bit_length() - 1)


def _bd_expand(name, kind, compact):
    span, cn = kind
    nq = compact.shape[0]

    def body(c_ref, o_ref):
        x = c_ref[...]
        grp = _bd_rowgroup(span)
        xcat = jnp.concatenate([jnp.where(grp == h, x, 0.0) for h in range(8)], axis=1)
        o_ref[...] = _bdot(xcat, _bd_perm(cn), _DIMS["nn"]).astype(o_ref.dtype)

    return pl.pallas_call(
        body, name=name, grid=(nq,), in_specs=[pl.BlockSpec((None, 1024, LANE), lambda q: (q, 0, 0))],
        out_specs=pl.BlockSpec((None, 1024, 1024), lambda q: (q, 0, 0)),
        out_shape=jax.ShapeDtypeStruct((nq, 1024, 1024), BF16),
        compiler_params=_cparams(("parallel",)),
    )(compact)


def _bd_reduce(name, kind, dbig):
    span, cn = kind
    nq = dbig.shape[0]

    def body(g_ref, o_ref):
        perm = _bd_perm(cn)
        hi, mid, lo = _split3(g_ref[...])
        d = _DIMS["nt"]
        back = _bdot(hi, perm, d) + _bdot(mid, perm, d) + _bdot(lo, perm, d)
        grp = _bd_rowgroup(span)
        out = jnp.zeros((1024, LANE), F32)
        for h in range(8):
            out = jnp.where(grp == h, back[:, h * LANE:(h + 1) * LANE], out)
        o_ref[...] = out

    return pl.pallas_call(
        body, name=name, grid=(nq,), in_specs=[pl.BlockSpec((None, 1024, 1024), lambda q: (q, 0, 0))],
        out_specs=pl.BlockSpec((None, 1024, LANE), lambda q: (q, 0, 0)),
        out_shape=jax.ShapeDtypeStruct((nq, 1024, LANE), F32),
        compiler_params=_cparams(("parallel",)),
    )(dbig)


def _x_tile_specs(nc, nq):
    return [pl.BlockSpec((nc, LANE), lambda q, t, i=i: (0, i * nq + q)) for i in range(SSM_CHUNK)]


def _cat_tiles(refs):
    return jnp.concatenate([r[...] for r in refs], axis=1)


def _ssm_w(name, x8, bw):
    nc = x8.shape[0]
    nq = bw.shape[0]

    def body(*refs):
        xq = _cat_tiles(refs[:8])
        refs[9][...] = _bdot(xq, refs[8][...], _DIMS["nn"])

    return pl.pallas_call(
        body, name=name, grid=(nq, 8),
        in_specs=_x_tile_specs(nc, nq) + [pl.BlockSpec((None, 1024, LANE), lambda q, t: (q, 0, t))],
        out_specs=pl.BlockSpec((None, None, nc, LANE), lambda q, t: (q, t, 0, 0)),
        out_shape=jax.ShapeDtypeStruct((nq, 8, nc, LANE), F32),
        compiler_params=_cparams(("parallel", "arbitrary")),
    )(*([x8] * 8), bw)


def _ssm_scan(name, w4, a_t, aseg_t, *, reverse, sprev4=None):
    nq, _, nc, _ = w4.shape
    ns = nc // 8
    with_da = sprev4 is not None

    def body(*refs):
        w_ref, a_ref, aseg_ref = refs[:3]
        s_ref = refs[3] if with_da else None
        o_ref = refs[4] if with_da else refs[3]
        da_ref = refs[5] if with_da else None
        sgn = -1.0 if reverse else 1.0
        ar = [a_ref[j] for j in range(4)]
        ai = [sgn * a_ref[j + 4] for j in range(4)]
        gr = [aseg_ref[j] for j in range(4)]
        gi = [sgn * aseg_ref[j + 4] for j in range(4)]
        zero = tuple(jnp.zeros((8, LANE), F32) for _ in range(8))

        def rows(tt):
            return pl.ds((ns - 1 - tt) if reverse else tt, 8, stride=ns)

        def step(carry, w):
            new_r = [ar[j] * carry[j] - ai[j] * carry[j + 4] + w[j] for j in range(4)]
            new_i = [ar[j] * carry[j + 4] + ai[j] * carry[j] + w[j + 4] for j in range(4)]
            return tuple(new_r + new_i)

        def pass1(tt, carry):
            return step(carry, [w_ref[j, rows(tt), :] for j in range(8)])

        ends = lax.fori_loop(0, ns, pass1, zero)
        sub = lax.broadcasted_iota(jnp.int32, (8, LANE), 0)
        init = list(zero)
        order = range(7, 0, -1) if reverse else range(0, 7)
        for s in order:
            nxt = s - 1 if reverse else s + 1
            cand_r = [gr[j] * init[j] - gi[j] * init[j + 4] + ends[j] for j in range(4)]
            cand_i = [gr[j] * init[j + 4] + gi[j] * init[j] + ends[j + 4] for j in range(4)]
            cand = cand_r + cand_i
            shift = 7 if reverse else 1
            init = [jnp.where(sub == nxt, pltpu.roll(cand[j], shift, axis=0), init[j]) for j in range(8)]

        def pass2(tt, state):
            carry, acc = state
            r = rows(tt)
            for j in range(8):
                o_ref[j, r, :] = carry[j]
            if with_da:
                sp = [s_ref[j, r, :] for j in range(8)]
                acc_r = [acc[j] + carry[j] * sp[j] + carry[j + 4] * sp[j + 4] for j in range(4)]
                acc_i = [acc[j + 4] + carry[j + 4] * sp[j] - carry[j] * sp[j + 4] for j in range(4)]
                acc = tuple(acc_r + acc_i)
            return step(carry, [w_ref[j, r, :] for j in range(8)]), acc

        _, acc = lax.fori_loop(0, ns, pass2, (tuple(init), zero))
        if with_da:
            for j in range(8):
                da_ref[j] = acc[j]

    big = pl.BlockSpec((None, 8, nc, LANE), lambda q: (q, 0, 0, 0))
    small = pl.BlockSpec((None, 8, 8, LANE), lambda q: (q, 0, 0, 0))
    in_specs = [big, small, small] + ([big] if with_da else [])
    ops = [w4, a_t, aseg_t] + ([sprev4] if with_da else [])
    out_specs = (big, small) if with_da else big
    big_s = jax.ShapeDtypeStruct((nq, 8, nc, LANE), F32)
    out_shape = (big_s, jax.ShapeDtypeStruct((nq, 8, 8, LANE), F32)) if with_da else big_s
    return pl.pallas_call(
        body, name=name, grid=(nq,), in_specs=in_specs, out_specs=out_specs, out_shape=out_shape,
        compiler_params=_cparams(("parallel",)),
    )(*ops)


def _ssm_y(name, x8, sprev4, m_mat, cm_mat):
    nc = x8.shape[0]
    nq = m_mat.shape[0]

    def body(*refs):
        xq = _cat_tiles(refs[:8])
        s_ref, m_ref, cm_ref, o_ref = refs[8:12]
        sq = jnp.concatenate([s_ref[t] for t in range(8)], axis=1)
        o_ref[...] = _bdot(xq, m_ref[...], _DIMS["nn"]) + _bdot(sq, cm_ref[...], _DIMS["nn"])

    col = pl.BlockSpec((None, 1024, LANE), lambda q, j: (q, 0, j))
    return pl.pallas_call(
        body, name=name, grid=(nq, 8),
        in_specs=_x_tile_specs(nc, nq) + [pl.BlockSpec((None, 8, nc, LANE), lambda q, j: (q, 0, 0, 0)), col, col],
        out_specs=pl.BlockSpec((nc, LANE), lambda q, j: (0, j * nq + q)),
        out_shape=jax.ShapeDtypeStruct((nc, 8 * SSM_WIDTH), F32),
        compiler_params=_cparams(("parallel", "arbitrary")),
    )(*([x8] * 8), sprev4, m_mat, cm_mat)


def _ssm_ds(name, dz8, sprev4, cm_mat):
    nc = dz8.shape[0]
    nq = cm_mat.shape[0]

    def body(*refs):
        dyq = _cat_tiles(refs[:8]).astype(BF16)
        s_ref, cm_ref, ds_ref, dcm_ref = refs[8:12]
        ds_ref[...] = _bdot(dyq, cm_ref[...], _DIMS["nt"])
        dcm_ref[...] = _bdot(s_ref[...], dyq, _DIMS["tn"])

    tile = pl.BlockSpec((None, None, nc, LANE), lambda q, t: (q, t, 0, 0))
    rowblk = pl.BlockSpec((None, LANE, 1024), lambda q, t: (q, t, 0))
    return pl.pallas_call(
        body, name=name, grid=(nq, 8),
        in_specs=_x_tile_specs(nc, nq) + [tile, rowblk],
        out_specs=(tile, rowblk),
        out_shape=(jax.ShapeDtypeStruct((nq, 8, nc, LANE), F32), jax.ShapeDtypeStruct((nq, 1024, 1024), F32)),
        compiler_params=_cparams(("parallel", "arbitrary")),
    )(*([dz8] * 8), sprev4, cm_mat)


def _ssm_dx(name, dz8, g4, x8, m_mat, bw_mat, d8):
    nc = dz8.shape[0]
    nq = m_mat.shape[0]

    def body(*refs):
        dyq = _cat_tiles(refs[:8]).astype(BF16)
        g_ref, x_ref, m_ref, bw_ref, d_ref, dzi_ref, dx_ref, dm_ref, dbw_ref = refs[8:17]
        gq = jnp.concatenate([g_ref[t] for t in range(8)], axis=1).astype(BF16)
        dx = _bdot(dyq, m_ref[...], _DIMS["nt"]) + _bdot(gq, bw_ref[...], _DIMS["nt"])
        dx_ref[...] = (dx + d_ref[...] * dzi_ref[...]).astype(dx_ref.dtype)
        xi = x_ref[...]
        dm_ref[...] = _bdot(xi, dyq, _DIMS["tn"])
        dbw_ref[...] = _bdot(xi, gq, _DIMS["tn"])

    xtile = pl.BlockSpec((nc, LANE), lambda q, i: (0, i * nq + q))
    rowblk = pl.BlockSpec((None, LANE, 1024), lambda q, i: (q, i, 0))
    return pl.pallas_call(
        body, name=name, grid=(nq, 8),
        in_specs=_x_tile_specs(nc, nq) + [pl.BlockSpec((None, 8, nc, LANE), lambda q, i: (q, 0, 0, 0)), xtile, rowblk, rowblk,
                                          pl.BlockSpec((1, LANE), lambda q, i: (0, q)), xtile],
        out_specs=(xtile, rowblk, rowblk),
        out_shape=(jax.ShapeDtypeStruct((nc, 8 * SSM_WIDTH), BF16), jax.ShapeDtypeStruct((nq, 1024, 1024), F32),
                   jax.ShapeDtypeStruct((nq, 1024, 1024), F32)),
        compiler_params=_cparams(("parallel", "arbitrary")),
    )(*([dz8] * 8), g4, x8, m_mat, bw_mat, d8, dz8)


CUM_BLK = 256


def _split3(x):
    hi = x.astype(BF16)
    r1 = x - hi.astype(F32)
    mid = r1.astype(BF16)
    lo = (r1 - mid.astype(F32)).astype(BF16)
    return hi, mid, lo


def _tri_dot(x, tri):
    hi, mid, lo = _split3(x)
    d = _DIMS["nn"]
    return _bdot(hi, tri, d) + _bdot(mid, tri, d) + _bdot(lo, tri, d)


def _tri(n, lower):
    r = lax.broadcasted_iota(jnp.int32, (n, n), 0)
    c = lax.broadcasted_iota(jnp.int32, (n, n), 1)
    return jnp.where((r >= c) if lower else (r <= c), 1.0, 0.0).astype(BF16)


def _fox_cum(name, fproj, bcol):
    seq = fproj.shape[0]
    blk = min(CUM_BLK, seq)

    def body(f_ref, b_ref, o_ref, carry_ref):
        i = pl.program_id(0)

        @pl.when(i == 0)
        def _():
            carry_ref[...] = jnp.zeros_like(carry_ref)

        z = f_ref[...].T + b_ref[...]
        logf = jnp.minimum(z, 0.0) - jnp.log(1.0 + jnp.exp(-jnp.abs(z)))
        carry = carry_ref[...]
        cum = _tri_dot(logf, _tri(blk, lower=False)) + jnp.tile(carry, (1, blk // LANE))
        o_ref[...] = cum[0:8, :]
        carry_ref[...] = carry + jnp.sum(logf, axis=1, keepdims=True)

    return pl.pallas_call(
        body, name=name, grid=(seq // blk,),
        in_specs=[pl.BlockSpec((blk, LANE), lambda i: (i, 0)), pl.BlockSpec((LANE, 1), lambda i: (0, 0))],
        out_specs=pl.BlockSpec((8, blk), lambda i: (0, i)),
        out_shape=jax.ShapeDtypeStruct((8, seq), F32),
        scratch_shapes=[pltpu.VMEM((LANE, LANE), F32)],
        compiler_params=_cparams(("arbitrary",)),
    )(fproj, bcol)


def _fox_cum_bwd(name, dcum_t, fproj, bcol):
    seq = fproj.shape[0]
    blk = min(CUM_BLK, seq)
    n = seq // blk

    def body(dc_ref, f_ref, b_ref, df_ref, db_ref, carry_ref, acc_ref):
        i = pl.program_id(0)

        @pl.when(i == 0)
        def _():
            carry_ref[...] = jnp.zeros_like(carry_ref)
            acc_ref[...] = jnp.zeros_like(acc_ref)

        dc = jnp.concatenate([dc_ref[...], jnp.zeros((LANE - 8, blk), F32)], axis=0)
        carry = carry_ref[...]
        dlogf = _tri_dot(dc, _tri(blk, lower=True)) + jnp.tile(carry, (1, blk // LANE))
        carry_ref[...] = carry + jnp.sum(dc, axis=1, keepdims=True)
        z = f_ref[...].T + b_ref[...]
        dft = dlogf / (1.0 + jnp.exp(z))
        df_ref[...] = dft.T.astype(df_ref.dtype)
        acc_ref[...] += jnp.sum(dft, axis=1, keepdims=True)

        @pl.when(i == n - 1)
        def _():
            db_ref[...] = acc_ref[...]

    return pl.pallas_call(
        body, name=name, grid=(n,),
        in_specs=[pl.BlockSpec((8, blk), lambda i: (0, n - 1 - i)), pl.BlockSpec((blk, LANE), lambda i: (n - 1 - i, 0)),
                  pl.BlockSpec((LANE, 1), lambda i: (0, 0))],
        out_specs=(pl.BlockSpec((blk, LANE), lambda i: (n - 1 - i, 0)), pl.BlockSpec((LANE, LANE), lambda i: (0, 0))),
        out_shape=(jax.ShapeDtypeStruct((seq, LANE), BF16), jax.ShapeDtypeStruct((LANE, LANE), F32)),
        scratch_shapes=[pltpu.VMEM((LANE, LANE), F32), pltpu.VMEM((LANE, LANE), F32)],
        compiler_params=_cparams(("arbitrary",)),
    )(dcum_t, fproj, bcol)


FOX_BLK = 512
FOX_SCALE = FOX_HEAD_DIM ** -0.5


def _fox_head_mask(shape, hh):
    lane = lax.broadcasted_iota(jnp.int32, shape, 1)
    return (lane < FOX_HEAD_DIM) if hh == 0 else (lane >= FOX_HEAD_DIM)


def _fox_bias(cum_ref, hh, q0, k0, blk):
    c0 = jnp.max(cum_ref[hh:hh + 1, pl.ds(q0, LANE)], axis=1, keepdims=True)
    return c0 - cum_ref[hh:hh + 1, pl.ds(k0, blk)]


def _fox_fwd(name, qkv, cum_t):
    seq = qkv.shape[0]
    blk = min(FOX_BLK, seq)
    nb = seq // blk
    npair = FOX_HEADS // 2

    def body(q_ref, k_ref, v_ref, cum_ref, o_ref, lse_ref):
        iq = pl.program_id(1)
        q0 = pl.multiple_of(iq * blk, blk)
        qv = q_ref[...]
        row = lax.broadcasted_iota(jnp.int32, (blk, blk), 0)
        col = lax.broadcasted_iota(jnp.int32, (blk, blk), 1)
        qhs = [jnp.where(_fox_head_mask(qv.shape, hh), qv, jnp.zeros_like(qv)) * FOX_SCALE for hh in range(2)]

        def block(kb, states, masked):
            k0 = pl.multiple_of(kb * blk, blk)
            kv = k_ref[pl.ds(k0, blk), :]
            vv = v_ref[pl.ds(k0, blk), :]
            new = []
            for hh in range(2):
                m, l, acc = states[hh]
                s = _bdot(qhs[hh], kv, _DIMS["nt"]) + _fox_bias(cum_ref, hh, q0, k0, blk)
                if masked:
                    s = jnp.where(row >= col, s, -jnp.inf)
                m_new = jnp.maximum(m, jnp.max(s, axis=1, keepdims=True))
                alpha = jnp.exp(m - m_new)
                p = jnp.exp(s - m_new)
                l = alpha * l + jnp.sum(p, axis=1, keepdims=True)
                acc = alpha * acc + _bdot(p, vv, _DIMS["nn"])
                new.append((m_new, l, acc))
            return tuple(new)

        init = (jnp.full((blk, 1), -jnp.inf, F32), jnp.zeros((blk, 1), F32), jnp.zeros((blk, LANE), F32))
        states = lax.fori_loop(0, iq, lambda kb, st: block(kb, st, False), (init, init))
        states = block(iq, states, True)
        outs = []
        for hh in range(2):
            m, l, acc = states[hh]
            outs.append(acc / l)
            lse_ref[hh] = jnp.broadcast_to(m + jnp.log(l), (blk, LANE))
        o_ref[...] = jnp.where(_fox_head_mask(outs[0].shape, 0), outs[0], outs[1]).astype(o_ref.dtype)

    return pl.pallas_call(
        body, name=name, grid=(npair, nb),
        in_specs=[pl.BlockSpec((blk, LANE), lambda p, i: (i, p)),
                  pl.BlockSpec((seq, LANE), lambda p, i: (0, npair + p)),
                  pl.BlockSpec((seq, LANE), lambda p, i: (0, 2 * npair + p)),
                  pl.BlockSpec((None, 2, seq), lambda p, i: (p, 0, 0))],
        out_specs=(pl.BlockSpec((blk, LANE), lambda p, i: (i, p)),
                   pl.BlockSpec((2, blk, LANE), lambda p, i: (p, i, 0))),
        out_shape=(jax.ShapeDtypeStruct((seq, FOX_WIDTH), BF16), jax.ShapeDtypeStruct((FOX_HEADS, seq, LANE), F32)),
        compiler_params=_cparams(("parallel", "arbitrary")),
    )(qkv, qkv, qkv, cum_t)


def _fox_bwd(name, qkv, cum_t, att, datt, lse):
    seq = qkv.shape[0]
    blk = min(FOX_BLK, seq)
    nb = seq // blk
    npair = FOX_HEADS // 2

    def body(q_ref, k_ref, v_ref, cum_ref, o_ref, do_ref, lse_ref, dq_ref, dk_ref, dv_ref, dcum_ref):
        iq = pl.program_id(1)
        q0 = pl.multiple_of(iq * blk, blk)

        @pl.when(iq == 0)
        def _():
            dk_ref[...] = jnp.zeros_like(dk_ref)
            dv_ref[...] = jnp.zeros_like(dv_ref)
            dcum_ref[...] = jnp.zeros_like(dcum_ref)

        qv = q_ref[...]
        dov = do_ref[...].astype(F32)
        ov = o_ref[...].astype(F32)
        row = lax.broadcasted_iota(jnp.int32, (blk, blk), 0)
        col = lax.broadcasted_iota(jnp.int32, (blk, blk), 1)
        qhs, dohbs, deltas, lses = [], [], [], []
        for hh in range(2):
            hm = _fox_head_mask(qv.shape, hh)
            qhs.append(jnp.where(hm, qv, jnp.zeros_like(qv)) * FOX_SCALE)
            doh = jnp.where(hm, dov, 0.0)
            dohbs.append(doh.astype(BF16))
            deltas.append(jnp.sum(doh * ov, axis=1, keepdims=True))
            lses.append(jnp.tile(lse_ref[hh], (1, blk // LANE)))

        def block(kb, accs, masked):
            k0 = pl.multiple_of(kb * blk, blk)
            kv = k_ref[pl.ds(k0, blk), :]
            vv = v_ref[pl.ds(k0, blk), :]
            new = []
            dk_blk = None
            dv_blk = None
            for hh in range(2):
                dq_acc, rs_acc = accs[hh]
                s = _bdot(qhs[hh], kv, _DIMS["nt"]) + _fox_bias(cum_ref, hh, q0, k0, blk)
                p = jnp.exp(s - lses[hh])
                if masked:
                    p = jnp.where(row >= col, p, 0.0)
                dp = _bdot(dohbs[hh], vv, _DIMS["nt"])
                ds = p * (dp - deltas[hh])
                dsb = ds.astype(BF16)
                dk_h = _bdot(dsb, qhs[hh], _DIMS["tn"])
                dv_h = _bdot(p, dohbs[hh], _DIMS["tn"])
                dk_blk = dk_h if dk_blk is None else dk_blk + dk_h
                dv_blk = dv_h if dv_blk is None else dv_blk + dv_h
                dcum_ref[hh:hh + 1, pl.ds(k0, blk)] -= jnp.sum(ds, axis=0, keepdims=True)
                new.append((dq_acc + _bdot(dsb, kv, _DIMS["nn"]), rs_acc + jnp.sum(ds, axis=1, keepdims=True)))
            dk_ref[pl.ds(k0, blk), :] += dk_blk
            dv_ref[pl.ds(k0, blk), :] += dv_blk
            return tuple(new)

        init = (jnp.zeros((blk, LANE), F32), jnp.zeros((blk, 1), F32))
        accs = lax.fori_loop(0, iq, lambda kb, a: block(kb, a, False), (init, init))
        accs = block(iq, accs, True)
        for hh in range(2):
            dcum_ref[hh:hh + 1, pl.ds(q0, blk)] += jnp.broadcast_to(accs[hh][1], (blk, LANE)).T[0:1, :]
        dq = jnp.where(_fox_head_mask(qv.shape, 0), accs[0][0], accs[1][0]) * FOX_SCALE
        dq_ref[...] = dq.astype(dq_ref.dtype)

    qblk = pl.BlockSpec((blk, LANE), lambda p, i: (i, p))
    full = pl.BlockSpec((seq, LANE), lambda p, i: (0, p))
    return pl.pallas_call(
        body, name=name, grid=(npair, nb),
        in_specs=[qblk,
                  pl.BlockSpec((seq, LANE), lambda p, i: (0, npair + p)),
                  pl.BlockSpec((seq, LANE), lambda p, i: (0, 2 * npair + p)),
                  pl.BlockSpec((None, 2, seq), lambda p, i: (p, 0, 0)),
                  qblk, qblk,
                  pl.BlockSpec((2, blk, LANE), lambda p, i: (p, i, 0))],
        out_specs=(qblk, full, full, pl.BlockSpec((None, 2, seq), lambda p, i: (p, 0, 0))),
        out_shape=(jax.ShapeDtypeStruct((seq, FOX_WIDTH), BF16), jax.ShapeDtypeStruct((seq, FOX_WIDTH), F32),
                   jax.ShapeDtypeStruct((seq, FOX_WIDTH), F32), jax.ShapeDtypeStruct((npair, 2, seq), F32)),
        compiler_params=_cparams(("arbitrary", "arbitrary")),
    )(qkv, qkv, qkv, cum_t, att, datt, lse)


MEM_SCALE = MEM_HEAD_DIM ** -0.5


def _mem_probs(qh, kh):
    s = _bdot(qh, kh, _DIMS["nt"]) * MEM_SCALE
    p = jnp.exp(s - jnp.max(s, axis=1, keepdims=True))
    return p / jnp.sum(p, axis=1, keepdims=True)


def _mem_fwd(name, q2, kv, *, tr=512):
    seq = q2.shape[0]
    mlen = kv.shape[0]
    tr = min(tr, seq)

    def body(q_ref, kv_ref, o_ref):
        for h in range(MEM_HEADS):
            sl = slice(h * MEM_HEAD_DIM, (h + 1) * MEM_HEAD_DIM)
            sv = slice(MEM_WIDTH + h * MEM_HEAD_DIM, MEM_WIDTH + (h + 1) * MEM_HEAD_DIM)
            p = _mem_probs(q_ref[:, sl], kv_ref[:, sl])
            o_ref[:, sl] = _bdot(p, kv_ref[:, sv], _DIMS["nn"]).astype(o_ref.dtype)

    return pl.pallas_call(
        body, name=name, grid=(seq // tr,),
        in_specs=[pl.BlockSpec((tr, MEM_WIDTH), lambda i: (i, 0)), pl.BlockSpec((mlen, 2 * MEM_WIDTH), lambda i: (0, 0))],
        out_specs=pl.BlockSpec((tr, MEM_WIDTH), lambda i: (i, 0)),
        out_shape=jax.ShapeDtypeStruct((seq, MEM_WIDTH), BF16),
        compiler_params=_cparams(("parallel",)),
    )(q2, kv)


def _mem_bwd(name, q2, kv, do2, *, tr=512):
    seq = q2.shape[0]
    mlen = kv.shape[0]
    tr = min(tr, seq)

    def body(q_ref, kv_ref, do_ref, dq_ref, dkv_ref):
        i = pl.program_id(0)

        @pl.when(i == 0)
        def _():
            dkv_ref[...] = jnp.zeros_like(dkv_ref)

        for h in range(MEM_HEADS):
            sl = slice(h * MEM_HEAD_DIM, (h + 1) * MEM_HEAD_DIM)
            sv = slice(MEM_WIDTH + h * MEM_HEAD_DIM, MEM_WIDTH + (h + 1) * MEM_HEAD_DIM)
            qh = q_ref[:, sl]
            kh = kv_ref[:, sl]
            doh = do_ref[:, sl].astype(BF16)
            p = _mem_probs(qh, kh)
            dp = _bdot(doh, kv_ref[:, sv], _DIMS["nt"])
            ds = (p * (dp - jnp.sum(p * dp, axis=1, keepdims=True)) * MEM_SCALE).astype(BF16)
            dq_ref[:, sl] = _bdot(ds, kh, _DIMS["nn"]).astype(dq_ref.dtype)
            dkv_ref[:, sl] += _bdot(ds, qh, _DIMS["tn"])
            dkv_ref[:, sv] += _bdot(p, doh, _DIMS["tn"])

    row = pl.BlockSpec((tr, MEM_WIDTH), lambda i: (i, 0))
    kvs = pl.BlockSpec((mlen, 2 * MEM_WIDTH), lambda i: (0, 0))
    return pl.pallas_call(
        body, name=name, grid=(seq // tr,), in_specs=[row, kvs, row], out_specs=(row, kvs),
        out_shape=(jax.ShapeDtypeStruct((seq, MEM_WIDTH), BF16), jax.ShapeDtypeStruct((mlen, 2 * MEM_WIDTH), F32)),
        compiler_params=_cparams(("arbitrary",)),
    )(q2, kv, do2)


_HBM = pl.BlockSpec(memory_space=pl.ANY)
_MESH = pl.DeviceIdType.MESH


def _mesh_place():
    x, y, c = lax.axis_index("x"), lax.axis_index("y"), lax.axis_index("c")
    other_chips = [(1 - x, y), (x, 1 - y), (1 - x, 1 - y)]
    return x, y, c, other_chips


def _gather_all(name, arrays):
    n = len(arrays)

    def body(*refs):
        ins, outs = refs[:n], refs[n:2 * n]
        send_sems, recv_sems, local_sems = refs[2 * n:]
        x, y, c, chips = _mesh_place()
        me, sibling = (x, y, c), (x, y, 1 - c)

        def slot(a, place):
            px, py, pc = place
            return outs[a].at[4 * px + 2 * py + pc]

        def copy(a, k, block, to, src=None):
            return pltpu.make_async_remote_copy(
                src_ref=slot(a, block) if src is None else src, dst_ref=slot(a, block),
                send_sem=send_sems.at[a, k], recv_sem=recv_sems.at[a, k], device_id=to, device_id_type=_MESH)

        mine = [pltpu.make_async_copy(ins[a], slot(a, me), local_sems.at[a]) for a in range(n)]
        for cp in mine:
            cp.start()
        first = []
        for a in range(n):
            first.append(copy(a, 0, me, sibling, src=ins[a]))
            first += [copy(a, 1 + j, me, (*chip, c), src=ins[a]) for j, chip in enumerate(chips)]
        for cp in first:
            cp.start()
        passed = []
        for j, chip in enumerate(chips):
            for a in range(n):
                copy(a, 1 + j, (*chip, c), me).wait_recv()
                fwd = copy(a, 4 + j, (*chip, c), sibling)
                fwd.start()
                passed.append(fwd)
        for a in range(n):
            copy(a, 0, sibling, me).wait_recv()
            for j, chip in enumerate(chips):
                copy(a, 4 + j, (*chip, 1 - c), me).wait_recv()
        for cp in first + passed:
            cp.wait_send()
        for cp in mine:
            cp.wait()

    out_shape = tuple(jax.ShapeDtypeStruct((N_DEV,) + arr.shape, arr.dtype) for arr in arrays)
    return pl.pallas_call(
        body, name=name, in_specs=[_HBM] * n, out_specs=tuple([_HBM] * n), out_shape=out_shape,
        scratch_shapes=[pltpu.SemaphoreType.DMA((n, N_DEV - 1)), pltpu.SemaphoreType.DMA((n, N_DEV - 1)),
                        pltpu.SemaphoreType.DMA((n,))],
    )(*arrays)


def _scatter_sibling(name, arrays):
    n = len(arrays)

    def body(*refs):
        ins, sibs = refs[:n], refs[n:2 * n]
        send_sems, recv_sems = refs[2 * n:]
        x, y, c, _ = _mesh_place()
        copies = []
        for a in range(n):
            for j in range(4):
                rdma = pltpu.make_async_remote_copy(
                    src_ref=ins[a].at[2 * j + (1 - c)], dst_ref=sibs[a].at[j], send_sem=send_sems.at[a, j],
                    recv_sem=recv_sems.at[a, j], device_id=(x, y, 1 - c), device_id_type=_MESH)
                rdma.start()
                copies.append(rdma)
        for cp in copies:
            cp.wait()

    four = tuple(jax.ShapeDtypeStruct((4,) + arr.shape[1:], arr.dtype) for arr in arrays)
    return pl.pallas_call(
        body, name=name, in_specs=[_HBM] * n, out_specs=tuple([_HBM] * n), out_shape=four,
        scratch_shapes=[pltpu.SemaphoreType.DMA((n, 4)), pltpu.SemaphoreType.DMA((n, 4))],
    )(*arrays)


def _add_chip_partials(name, pieces, sibs):
    n = len(pieces)
    core = lax.axis_index("c").astype(jnp.int32).reshape(1)

    def body(c_ref, *refs):
        for a in range(n):
            out = refs[2 * n + a]
            out[...] = (refs[a][...].astype(F32) + refs[n + a][...].astype(F32)).astype(out.dtype)

    own_specs = [pl.BlockSpec((None,) + arr.shape[1:], lambda j, c_ref: (2 * j + c_ref[0], 0, 0)) for arr in pieces]
    four_specs = [pl.BlockSpec((None,) + arr.shape[1:], lambda j, c_ref: (j, 0, 0)) for arr in sibs]
    return pl.pallas_call(
        body, name=name,
        grid_spec=pltpu.PrefetchScalarGridSpec(num_scalar_prefetch=1, grid=(4,), in_specs=own_specs + four_specs,
                                               out_specs=tuple(four_specs)),
        out_shape=tuple(jax.ShapeDtypeStruct(arr.shape, arr.dtype) for arr in sibs),
        compiler_params=_cparams(("parallel",)),
    )(core, *pieces, *sibs)


def _scatter_chips(name, arrays):
    n = len(arrays)

    def body(*refs):
        ins, outs = refs[:n], refs[n:2 * n]
        send_sems, recv_sems, local_sems = refs[2 * n:]
        x, y, c, chips = _mesh_place()
        my_chip = 2 * x + y
        copies = []
        for a in range(n):
            local = pltpu.make_async_copy(ins[a].at[my_chip], outs[a].at[my_chip], local_sems.at[a])
            local.start()
            copies.append(local)
            for k, (px, py) in enumerate(chips):
                rdma = pltpu.make_async_remote_copy(
                    src_ref=ins[a].at[2 * px + py], dst_ref=outs[a].at[my_chip], send_sem=send_sems.at[a, k],
                    recv_sem=recv_sems.at[a, k], device_id=(px, py, c), device_id_type=_MESH)
                rdma.start()
                copies.append(rdma)
        for cp in copies:
            cp.wait()

    return pl.pallas_call(
        body, name=name, in_specs=[_HBM] * n, out_specs=tuple([_HBM] * n),
        out_shape=tuple(jax.ShapeDtypeStruct(arr.shape, arr.dtype) for arr in arrays),
        scratch_shapes=[pltpu.SemaphoreType.DMA((n, 3)), pltpu.SemaphoreType.DMA((n, 3)),
                        pltpu.SemaphoreType.DMA((n,))],
    )(*arrays)


def _unstack_cols(name, stacked):
    n, rows, cols = stacked.shape

    def body(i_ref, o_ref):
        o_ref[...] = i_ref[...]

    return pl.pallas_call(
        body, name=name, grid=(n,), in_specs=[pl.BlockSpec((None, rows, cols), lambda k: (k, 0, 0))],
        out_specs=pl.BlockSpec((rows, cols), lambda k: (0, k)),
        out_shape=jax.ShapeDtypeStruct((rows, n * cols), stacked.dtype),
        compiler_params=_cparams(("parallel",)),
    )(stacked)


def _restack_cols(name, mat):
    rows, width = mat.shape
    cols = width // N_DEV

    def body(i_ref, o_ref):
        o_ref[...] = i_ref[...]

    return pl.pallas_call(
        body, name=name, grid=(N_DEV,), in_specs=[pl.BlockSpec((rows, cols), lambda k: (0, k))],
        out_specs=pl.BlockSpec((None, rows, cols), lambda k: (k, 0, 0)),
        out_shape=jax.ShapeDtypeStruct((N_DEV, rows, cols), mat.dtype),
        compiler_params=_cparams(("parallel",)),
    )(mat)


def _remap_pieces(runs):
    plan = {}
    for du, dc, su, sc, ln in runs:
        while ln > 0:
            lane = dc % LANE
            take = min(ln, LANE - lane)
            plan.setdefault((du, dc // LANE), []).append((su, sc, take, lane))
            dc, sc, ln = dc + take, sc + take, ln - take
    return plan


def _remap(name, srcs, src_units, runs, *, out_units, out_cols, out_dtype, tr=256):
    rows = srcs[0].shape[-2]
    tr = min(tr, rows)
    plan = _remap_pieces(runs)
    n_src = len(srcs)
    stacked_out = out_units is not None
    n_tiles = out_cols // LANE

    def body(*refs):
        o_ref = refs[n_src]

        def src_tile(unit, t):
            ai, lead = src_units[unit]
            ref = refs[ai]
            sl = slice(t * LANE, (t + 1) * LANE)
            return (ref[:, sl] if lead is None else ref[lead, :, sl]).astype(F32)

        lane = lax.broadcasted_iota(jnp.int32, (tr, LANE), 1)
        for du in range(out_units if stacked_out else 1):
            for t in range(n_tiles):
                acc = jnp.zeros((tr, LANE), F32)
                for su, sc, ln, dl in plan.get((du if stacked_out else None, t), []):
                    st, so = sc // LANE, sc % LANE
                    first = src_tile(su, st)
                    if so == dl and so + ln <= LANE:
                        piece = first
                    else:
                        second = src_tile(su, st + 1) if so + ln > LANE else first
                        both = jnp.concatenate([first, second], axis=1)
                        piece = pltpu.roll(both, (dl - so) % (2 * LANE), axis=1)[:, 0:LANE]
                    acc = piece if (dl == 0 and ln == LANE) else jnp.where(
                        jnp.logical_and(lane >= dl, lane < dl + ln), piece, acc)
                if stacked_out:
                    o_ref[du, :, t * LANE:(t + 1) * LANE] = acc.astype(o_ref.dtype)
                else:
                    o_ref[:, t * LANE:(t + 1) * LANE] = acc.astype(o_ref.dtype)

    in_specs = []
    for arr in srcs:
        if arr.ndim == 2:
            in_specs.append(pl.BlockSpec((tr, arr.shape[1]), lambda i: (i, 0)))
        else:
            in_specs.append(pl.BlockSpec((arr.shape[0], tr, arr.shape[2]), lambda i: (0, i, 0)))
    if stacked_out:
        out_spec = pl.BlockSpec((out_units, tr, out_cols), lambda i: (0, i, 0))
        out_shape = jax.ShapeDtypeStruct((out_units, rows, out_cols), out_dtype)
    else:
        out_spec = pl.BlockSpec((tr, out_cols), lambda i: (i, 0))
        out_shape = jax.ShapeDtypeStruct((rows, out_cols), out_dtype)
    return pl.pallas_call(
        body, name=name, grid=(rows // tr,), in_specs=in_specs, out_specs=out_spec, out_shape=out_shape,
        compiler_params=_cparams(("parallel",)),
    )(*srcs)


def _proj_col(c):
    if c < PROJ_GATE0:
        return c
    if c < PROJ_GATE0 + FOX_HEADS:
        return PROJ_F0 + (c - PROJ_GATE0)
    return c - FOX_HEADS


def _win_runs():
    cuts = sorted(set([0, PROJ_GATE0, PROJ_GATE0 + FOX_HEADS, IN_WIDTH] + [SHARD_IN * k for k in range(N_DEV + 1)]))
    return [(lo // SHARD_IN, lo % SHARD_IN, _proj_col(lo), hi - lo) for lo, hi in zip(cuts[:-1], cuts[1:])]


def _assemble_win(name, stacked):
    runs = [(None, pc, k, sc, ln) for k, sc, pc, ln in _win_runs()]
    return _remap(name, [stacked], [(0, k) for k in range(N_DEV)], runs,
                  out_units=None, out_cols=PROJ_WIDTH, out_dtype=BF16)


def _disassemble_dwin(name, dw):
    runs = [(k, sc, 0, pc, ln) for k, sc, pc, ln in _win_runs()]
    return _remap(name, [dw], [(0, None)], runs, out_units=N_DEV, out_cols=SHARD_IN_PAD, out_dtype=BF16)


def _concat_cols(name, parts, *, tr=512):
    rows = parts[0].shape[0]
    tr = min(tr, rows)
    widths = [p.shape[1] for p in parts]
    total = sum(widths)

    def body(*refs):
        o_ref = refs[len(parts)]
        lo = 0
        for r, w in zip(refs[:len(parts)], widths):
            o_ref[:, lo:lo + w] = r[...].astype(o_ref.dtype)
            lo += w

    return pl.pallas_call(
        body, name=name, grid=(rows // tr,),
        in_specs=[pl.BlockSpec((tr, w), lambda i: (i, 0)) for w in widths],
        out_specs=pl.BlockSpec((tr, total), lambda i: (i, 0)),
        out_shape=jax.ShapeDtypeStruct((rows, total), BF16),
        compiler_params=_cparams(("parallel",)),
    )(*parts)


def _assemble_wffn(name, stacked):
    runs = [(None, SHARD_FFN * k, k, 0, SHARD_FFN) for k in range(N_DEV)]
    return _remap(name, [stacked], [(0, k) for k in range(N_DEV)], runs,
                  out_units=None, out_cols=2 * FFN_HIDDEN, out_dtype=BF16)


def _disassemble_dwffn(name, dw):
    runs = [(k, 0, 0, SHARD_FFN * k, SHARD_FFN) for k in range(N_DEV)]
    return _remap(name, [dw], [(0, None)], runs, out_units=N_DEV, out_cols=SHARD_FFN_PAD, out_dtype=BF16)


def _adamw(name, parts, w, m, v, *, tr=128):
    rows, cols = w.shape
    n_parts = parts.shape[0]
    tr = min(tr, rows)
    assert rows % tr == 0, (name, rows, tr)
    c1 = 1.0 - ADAM_B1 ** ADAM_STEP
    c2 = 1.0 - ADAM_B2 ** ADAM_STEP

    def body(p_ref, w_ref, m_ref, v_ref, g_ref, d_ref, nm_ref, nv_ref):
        g = p_ref[0].astype(F32)
        for s in range(1, n_parts):
            g = g + p_ref[s].astype(F32)
        m_new = ADAM_B1 * m_ref[...] + (1.0 - ADAM_B1) * g
        v_new = ADAM_B2 * v_ref[...] + (1.0 - ADAM_B2) * (g * g)
        upd = (m_new / c1) / (jnp.sqrt(v_new / c2) + ADAM_EPS) + ADAM_WD * w_ref[...]
        g_ref[...] = g
        d_ref[...] = -ADAM_LR * upd
        nm_ref[...] = m_new
        nv_ref[...] = v_new

    row = pl.BlockSpec((tr, cols), lambda i: (i, 0))
    out = jax.ShapeDtypeStruct((rows, cols), F32)
    return pl.pallas_call(
        body, name=name, grid=(rows // tr,),
        in_specs=[pl.BlockSpec((n_parts, tr, cols), lambda i: (0, i, 0)), row, row, row],
        out_specs=(row, row, row, row), out_shape=(out, out, out, out),
        compiler_params=_cparams(("parallel",)),
    )(parts, w, m, v)


_WEIGHTS = ("norm_mix", "w_in", "b_forget", "lam_re", "lam_im", "log_dt", "b_re", "b_im", "c_re", "c_im",
            "d_skip", "w_glu", "w_fox_o", "w_mix_out", "norm_mem_q", "norm_mem_kv", "w_mem_q", "w_mem_kv",
            "w_mem_o", "norm_ffn", "w_ffn_in", "w_ffn_out", "norm_final")
_SHARDED = ("w_in", "w_glu", "w_fox_o", "w_mix_out", "w_mem_q", "w_mem_kv", "w_mem_o", "w_ffn_in", "w_ffn_out")
_SMALL = tuple(n for n in _WEIGHTS if n not in _SHARDED)
_PACK_COLS = 1024


def _pack(arrays):
    flat = jnp.concatenate([a.reshape(-1).astype(F32) for a in arrays])
    rows = -(-flat.shape[0] // _PACK_COLS)
    return jnp.pad(flat, (0, rows * _PACK_COLS - flat.shape[0])).reshape(rows, _PACK_COLS)


def _unpack(buf, like):
    flat = buf.reshape(-1)
    out, pos = [], 0
    for a in like:
        out.append(flat[pos:pos + a.size].reshape(a.shape))
        pos += a.size
    return out


def _mm(name, a, b, mode, m, n, k, out_dtype, tm=1024, tn=512, tk=1024, **kw):
    return _matmul(name, a, b, mode, m, n, k, out_dtype=out_dtype, tm=tm, tn=tn, tk=tk, **kw)


def kernel(x, mem, norm_mix, w_in, b_forget, lam_re, lam_im, log_dt, b_re, b_im, c_re, c_im, d_skip, w_glu, w_fox_o, w_mix_out, norm_mem_q, norm_mem_kv, w_mem_q, w_mem_kv, w_mem_o, norm_ffn, w_ffn_in, w_ffn_out, norm_final, loss_target, m_norm_mix, m_w_in, m_b_forget, m_lam_re, m_lam_im, m_log_dt, m_b_re, m_b_im, m_c_re, m_c_im, m_d_skip, m_w_glu, m_w_fox_o, m_w_mix_out, m_norm_mem_q, m_norm_mem_kv, m_w_mem_q, m_w_mem_kv, m_w_mem_o, m_norm_ffn, m_w_ffn_in, m_w_ffn_out, m_norm_final, v_norm_mix, v_w_in, v_b_forget, v_lam_re, v_lam_im, v_log_dt, v_b_re, v_b_im, v_c_re, v_c_im, v_d_skip, v_w_glu, v_w_fox_o, v_w_mix_out, v_norm_mem_q, v_norm_mem_kv, v_w_mem_q, v_w_mem_kv, v_w_mem_o, v_norm_ffn, v_w_ffn_in, v_w_ffn_out, v_norm_final):
    given = dict(locals())
    weights = {n: given[n] for n in _WEIGHTS}
    mom_m = {n: given["m_" + n] for n in _WEIGHTS}
    mom_v = {n: given["v_" + n] for n in _WEIGHTS}
    seq = x.shape[1]
    nc = seq // SSM_CHUNK
    d = D_MODEL
    xs, mems, tgt = x[0], mem[0], loss_target[0]

    def padcols(a, width):
        return jnp.pad(a, ((0, 0), (0, width - a.shape[1])))

    shards = [padcols(w_in[0].astype(BF16), SHARD_IN_PAD), w_glu[0].astype(BF16), w_fox_o[0].astype(BF16),
              w_mix_out[0].astype(BF16), w_mem_q[0].astype(BF16), w_mem_kv[0].astype(BF16),
              w_mem_o[0].astype(BF16), padcols(w_ffn_in[0].astype(BF16), SHARD_FFN_PAD), w_ffn_out[0].astype(BF16)]
    gathered = _gather_all("gather_weights", shards)
    win = _assemble_win("assemble_w_in", gathered[0])
    wglu = _unstack_cols("unstack_w_glu", gathered[1])
    wfoxo = _unstack_cols("unstack_w_fox_o", gathered[2])
    wmo = _unstack_cols("unstack_w_mem_o", gathered[6])
    wmix = gathered[3].reshape(d, d)
    wmq = gathered[4].reshape(d, MEM_WIDTH)
    wmkv = gathered[5].reshape(d, 2 * MEM_WIDTH)
    wffn_in = _assemble_wffn("assemble_w_ffn_in", gathered[7])
    wffn_out = gathered[8].reshape(FFN_HIDDEN, d)

    u = _rms_fwd("rms_mix", xs, norm_mix)
    ussm = _mm("proj_ssm", u, win, "nn", seq, SSM_WIDTH, d, F32)
    qkv = _mm("proj_qkv", u, win, "nn", seq, 3 * FOX_WIDTH, d, BF16, tn=512, b_off=(0, SSM_WIDTH))
    gates = _mm("proj_gates", u, win, "nn", seq, 2 * d, d, F32, tn=1024, b_off=(0, PROJ_GATE0))
    fproj = _mm("proj_forget", u, win, "nn", seq, LANE, d, F32, tn=LANE, b_off=(0, PROJ_F0))

    ssm_params = (lam_re[0], lam_im[0], log_dt[0], b_re[0], b_im[0], c_re[0], c_im[0])
    (m_c, bw_c, cm_c, a8, aseg), mats_vjp = jax.vjp(lambda *p: _ssm_mats(*p, nc), *ssm_params)
    m_b = _bd_expand("ssm_expand_m", _BD_M, m_c)
    bw_b = _bd_expand("ssm_expand_bw", _BD_BW, bw_c)
    cm_b = _bd_expand("ssm_expand_cm", _BD_CM, cm_c)
    u8 = ussm.reshape(nc, SSM_CHUNK * SSM_WIDTH)
    d8 = jnp.tile(d_skip, (1, SSM_CHUNK))
    w4 = _ssm_w("ssm_w", u8, bw_b)
    sp4 = _ssm_scan("ssm_scan", w4, a8, aseg, reverse=False)
    y8 = _ssm_y("ssm_y", u8, sp4, m_b, cm_b)
    act = _ssm_post_fwd("ssm_act", y8, u8, d8).reshape(seq, SSM_WIDTH)
    glu = _mm("glu", act, wglu, "nn", seq, 2 * d, SSM_WIDTH, F32, tn=1024)

    bcol = jnp.pad(b_forget[0], (0, LANE - FOX_HEADS)).reshape(LANE, 1)
    cum_t = _fox_cum("fox_cum", fproj, bcol).reshape(FOX_HEADS // 2, 2, seq)
    att, lse = _fox_fwd("fox_fwd", qkv, cum_t)
    out_b = _mm("fox_out", att, wfoxo, "nn", seq, d, FOX_WIDTH, F32, tn=1024)

    mixin = _mix_fwd("mix", glu, gates, out_b)
    h1 = _mm("mix_out", mixin, wmix, "nn", seq, d, d, F32, tn=1024, add=xs)

    n1 = _rms_fwd("rms_mem_q", h1, norm_mem_q)
    q2 = _mm("mem_q", n1, wmq, "nn", seq, MEM_WIDTH, d, BF16)
    mn = _rms_fwd("rms_mem_kv", mems, norm_mem_kv)
    mlen = mems.shape[0]
    kv = _mm("mem_kv", mn, wmkv, "nn", mlen, 2 * MEM_WIDTH, d, BF16)
    o2 = _mem_fwd("mem_attn", q2, kv)
    h2 = _mm("mem_out", o2, wmo, "nn", seq, d, MEM_WIDTH, F32, tn=1024, add=h1)

    n2 = _rms_fwd("rms_ffn", h2, norm_ffn)
    f = _mm("ffn_in", n2, wffn_in, "nn", seq, 2 * FFN_HIDDEN, d, F32, tn=1408)
    g_act = _swiglu_fwd("swiglu", f)
    h3 = _mm("ffn_out", g_act, wffn_out, "nn", seq, d, FFN_HIDDEN, F32, tk=FFN_HIDDEN, add=h2)
    loss_part, dh3, dg_final = _final_loss("final_loss", h3, tgt, norm_final.reshape(1, d))

    dg_act = _mm("d_ffn_out_x", dh3, wffn_out, "nt", seq, FFN_HIDDEN, d, F32, tn=1408)
    dwffn_out = _mm("d_ffn_out_w", g_act, dh3, "tn", FFN_HIDDEN, d, seq, BF16, tm=1408, tn=1024)
    df = _swiglu_bwd("d_swiglu", dg_act, f)
    dn2 = _mm("d_ffn_in_x", df, wffn_in, "nt", seq, d, 2 * FFN_HIDDEN, F32, tn=1024, tk=FFN_HIDDEN)
    dwffn_in = _mm("d_ffn_in_w", n2, df, "tn", d, 2 * FFN_HIDDEN, seq, BF16, tn=1408)
    dh2, dg_ffn = _rms_bwd("d_rms_ffn", dn2, h2, norm_ffn, res=dh3)

    do2 = _mm("d_mem_out_x", dh2, wmo, "nt", seq, MEM_WIDTH, d, F32)
    dwmo = _restack_cols("restack_d_w_mem_o", _mm("d_mem_out_w", o2, dh2, "tn", MEM_WIDTH, d, seq, BF16, tn=1024))
    dq2, dkv = _mem_bwd("d_mem_attn", q2, kv, do2)
    dwmq = _mm("d_mem_q_w", n1, dq2, "tn", d, MEM_WIDTH, seq, BF16)
    dn1 = _mm("d_mem_q_x", dq2, wmq, "nt", seq, d, MEM_WIDTH, F32)
    dwmkv = _mm("d_mem_kv_w", mn, dkv, "tn", d, 2 * MEM_WIDTH, mlen, BF16, tn=1024)
    dmn = _mm("d_mem_kv_x", dkv, wmkv, "nt", mlen, d, 2 * MEM_WIDTH, F32)
    _, dg_memkv = _rms_bwd("d_rms_mem_kv", dmn, mems, norm_mem_kv)
    dh1, dg_memq = _rms_bwd("d_rms_mem_q", dn1, h1, norm_mem_q, res=dh2)

    dmixin = _mm("d_mix_out_x", dh1, wmix, "nt", seq, d, d, F32, tn=1024)
    dwmix = _mm("d_mix_out_w", mixin, dh1, "tn", d, d, seq, BF16, tn=1024)
    dglu, dgates, dout_b = _mix_bwd("d_mix", dmixin, glu, gates, out_b)
    datt = _mm("d_fox_out_x", dout_b, wfoxo, "nt", seq, FOX_WIDTH, d, F32)
    dwfoxo = _restack_cols("restack_d_w_fox_o", _mm("d_fox_out_w", att, dout_b, "tn", FOX_WIDTH, d, seq, BF16, tn=1024))
    dact = _mm("d_glu_x", dglu, wglu, "nt", seq, SSM_WIDTH, 2 * d, F32, tk=2 * d)
    dwglu = _restack_cols("restack_d_w_glu", _mm("d_glu_w", act, dglu, "tn", SSM_WIDTH, 2 * d, seq, BF16, tn=2 * d))

    dz8, dg_dskip = _ssm_post_bwd("d_ssm_act", dact.reshape(nc, SSM_CHUNK * SSM_WIDTH), y8, u8, d8)
    ds4, dcm = _ssm_ds("d_ssm_y_state", dz8, sp4, cm_b)
    g4, da8 = _ssm_scan("d_ssm_scan", ds4, a8, aseg, reverse=True, sprev4=sp4)
    dx8, dm, dbw = _ssm_dx("d_ssm_x", dz8, g4, u8, m_b, bw_b, d8)
    dussm = dx8.reshape(seq, SSM_WIDTH)
    g_ssm = mats_vjp((_bd_reduce("ssm_reduce_dm", _BD_M, dm), _bd_reduce("ssm_reduce_dbw", _BD_BW, dbw),
                      _bd_reduce("ssm_reduce_dcm", _BD_CM, dcm), da8, jnp.zeros_like(aseg)))

    dq, dk, dv, dcum = _fox_bwd("d_fox", qkv, cum_t, att, datt, lse)
    dfproj, dbf = _fox_cum_bwd("d_fox_cum", dcum.reshape(FOX_HEADS, seq), fproj, bcol)
    dg_bforget = dbf[0:FOX_HEADS, 0].reshape(1, FOX_HEADS)

    dproj = _concat_cols("d_proj_concat", (dussm, dq, dk, dv, dgates, dfproj))
    du = _mm("d_proj_x", dproj, win, "nt", seq, d, PROJ_WIDTH, F32, tn=1024, tk=1408)
    dwin = _mm("d_proj_w", u, dproj, "tn", d, PROJ_WIDTH, seq, BF16, tn=1408)
    dx, dg_mix = _rms_bwd("d_rms_mix", du, xs, norm_mix, res=dh1)

    pieces = [_disassemble_dwin("split_d_w_in", dwin), dwglu, dwfoxo,
              dwmix.reshape(N_DEV, d // N_DEV, d), dwmq.reshape(N_DEV, d // N_DEV, MEM_WIDTH),
              dwmkv.reshape(N_DEV, d // N_DEV, 2 * MEM_WIDTH), dwmo,
              _disassemble_dwffn("split_d_w_ffn_in", dwffn_in),
              dwffn_out.reshape(N_DEV, FFN_HIDDEN // N_DEV, d)]
    sib = _scatter_sibling("scatter_grads_sibling", pieces)
    received = _scatter_chips("scatter_grads_chips", _add_chip_partials("sum_grads_chip", pieces, sib))

    small_grads = dict(zip(
        _SMALL, (dg_mix, dg_bforget, g_ssm[0][None], g_ssm[1][None], g_ssm[2][None], g_ssm[3][None], g_ssm[4][None],
                 g_ssm[5][None], g_ssm[6][None], dg_dskip, dg_memq, dg_memkv, dg_ffn, dg_final.reshape(d))))
    small_like = [weights[n] for n in _SMALL]
    small_all = _gather_all("gather_small_grads", [_pack([small_grads[n] for n in _SMALL])])[0]
    pk = [_pack([src[n] for n in _SMALL]) for src in (weights, mom_m, mom_v)]
    small_out = _adamw("adamw_small", small_all, pk[0], pk[1], pk[2], tr=small_all.shape[1])
    small_res = [dict(zip(_SMALL, _unpack(buf, small_like))) for buf in small_out]

    results = [dict(r) for r in small_res]
    tiles = {"w_in": 128, "w_glu": 128, "w_fox_o": 128, "w_mix_out": 128, "w_mem_q": 128, "w_mem_kv": 128,
             "w_mem_o": 128, "w_ffn_in": 128, "w_ffn_out": 176}
    pads = {"w_in": SHARD_IN_PAD, "w_ffn_in": SHARD_FFN_PAD}
    for name, parts in zip(_SHARDED, received):
        w2, m2, v2 = weights[name][0], mom_m[name][0], mom_v[name][0]
        cols = w2.shape[1]
        if name in pads:
            w2, m2, v2 = (padcols(t, pads[name]) for t in (w2, m2, v2))
        outs = _adamw("adamw_" + name, parts, w2, m2, v2, tr=tiles[name])
        for res, o in zip(results, outs):
            res[name] = o[:, :cols][None]

    loss = lax.psum(loss_part[0, 0], ("x", "y", "c"))
    out = [loss, dx[None]]
    for res in results:
        out.extend(res[n] for n in _WEIGHTS)
    return tuple(out)
```

```python
import math

import jax
import jax.numpy as jnp
import numpy as np
from jax import lax
from jax.experimental import pallas as pl
from jax.experimental.pallas import tpu as pltpu

F32 = jnp.float32
BF16 = jnp.bfloat16

N_DEV = 8
LANE = 128
VMEM_LIMIT = 56 * 1024 * 1024

D_MODEL = 1024
SSM_GROUP = 16
SSM_GROUPS = 32
SSM_WIDTH = 512
SSM_STATE = 64
SSM_CHUNK = 8
FOX_HEADS = 8
FOX_HEAD_DIM = 64
FOX_WIDTH = 512
MEM_HEADS = 4
MEM_HEAD_DIM = 128
MEM_WIDTH = 512
FFN_HIDDEN = 2816
RMS_EPS = 1e-6
IN_WIDTH = 4104
SHARD_IN = IN_WIDTH // N_DEV
SHARD_IN_PAD = 640
SHARD_FFN = 2 * FFN_HIDDEN // N_DEV
SHARD_FFN_PAD = 768
PROJ_GATE0 = 2048
PROJ_F0 = 4096
PROJ_WIDTH = 4224

ADAM_LR = 0.001
ADAM_B1 = 0.9
ADAM_B2 = 0.999
ADAM_EPS = 1e-08
ADAM_WD = 0.01
ADAM_STEP = 10


def _cparams(sem=None):
    return pltpu.CompilerParams(dimension_semantics=sem, vmem_limit_bytes=VMEM_LIMIT)


def _sigmoid(x):
    return 1.0 / (1.0 + jnp.exp(-x))


def _bdot(a, b, dims):
    return lax.dot_general(a.astype(BF16), b.astype(BF16), ((dims[0], dims[1]), ((), ())),
                           preferred_element_type=F32)


_DIMS = {"nn": ((1,), (0,)), "nt": ((1,), (1,)), "tn": ((0,), (0,))}


def _matmul(name, a, b, mode, m, n, k, *, out_dtype, tm, tn, tk, a_off=(0, 0), b_off=(0, 0), add=None):
    tm, tn, tk = min(tm, m), min(tn, n), min(tk, k)
    assert m % tm == 0 and n % tn == 0 and k % tk == 0, (name, m, n, k, tm, tn, tk)
    nk = k // tk
    grid = (m // tm, n // tn, nk)

    def blk(off, t):
        assert off % t == 0, (name, off, t)
        return off // t

    if mode in ("nn", "nt"):
        ar, ac = blk(a_off[0], tm), blk(a_off[1], tk)
        a_spec = pl.BlockSpec((tm, tk), lambda i, j, kk: (i + ar, kk + ac))
    else:
        ar, ac = blk(a_off[0], tk), blk(a_off[1], tm)
        a_spec = pl.BlockSpec((tk, tm), lambda i, j, kk: (kk + ar, i + ac))

    if mode in ("nn", "tn"):
        br, bc = blk(b_off[0], tk), blk(b_off[1], tn)
        b_spec = pl.BlockSpec((tk, tn), lambda i, j, kk: (kk + br, j + bc))
    else:
        br, bc = blk(b_off[0], tn), blk(b_off[1], tk)
        b_spec = pl.BlockSpec((tn, tk), lambda i, j, kk: (j + br, kk + bc))
    o_spec = pl.BlockSpec((tm, tn), lambda i, j, kk: (i, j))
    out_shape = jax.ShapeDtypeStruct((m, n), out_dtype)

    in_specs = [a_spec, b_spec]
    operands = [a, b]
    if add is not None:
        in_specs.append(pl.BlockSpec((tm, tn), lambda i, j, kk: (i, j)))
        operands.append(add)
    dims = _DIMS[mode]
    has_add = add is not None

    def body(*refs):
        a_ref, b_ref = refs[0], refs[1]
        add_ref = refs[2] if has_add else None
        o_ref = refs[3] if has_add else refs[2]
        acc_ref = refs[-1] if nk > 1 else None
        prod = _bdot(a_ref[...], b_ref[...], dims)

        def finish(total):
            if has_add:
                total = total + add_ref[...].astype(F32)
            o_ref[...] = total.astype(o_ref.dtype)

        if nk == 1:
            finish(prod)
        else:
            kk = pl.program_id(2)

            @pl.when(kk == 0)
            def _():
                acc_ref[...] = prod

            @pl.when(jnp.logical_and(kk > 0, kk < nk - 1))
            def _():
                acc_ref[...] += prod

            @pl.when(kk == nk - 1)
            def _():
                finish(acc_ref[...] + prod)

    scratch = [pltpu.VMEM((tm, tn), F32)] if nk > 1 else []
    return pl.pallas_call(
        body, name=name, grid=grid, in_specs=in_specs, out_specs=o_spec, out_shape=out_shape,
        scratch_shapes=scratch,
        compiler_params=_cparams(("parallel", "parallel", "arbitrary")),
    )(*operands)


def _rms_fwd(name, x, gain, *, tr=512):
    r, d = x.shape
    tr = min(tr, r)

    def body(x_ref, g_ref, o_ref):
        xv = x_ref[...]
        rstd = lax.rsqrt(jnp.mean(xv * xv, axis=-1, keepdims=True) + RMS_EPS)
        o_ref[...] = (xv * rstd * g_ref[...]).astype(o_ref.dtype)

    return pl.pallas_call(
        body, name=name, grid=(r // tr,),
        in_specs=[pl.BlockSpec((tr, d), lambda i: (i, 0)), pl.BlockSpec((1, d), lambda i: (0, 0))],
        out_specs=pl.BlockSpec((tr, d), lambda i: (i, 0)),
        out_shape=jax.ShapeDtypeStruct((r, d), BF16),
        compiler_params=_cparams(("parallel",)),
    )(x, gain)


def _rms_bwd(name, dy, x, gain, res=None, *, tr=512):
    r, d = x.shape
    tr = min(tr, r)
    n = r // tr
    has_res = res is not None

    def body(*refs):
        dy_ref, x_ref, g_ref = refs[:3]
        res_ref = refs[3] if has_res else None
        dx_ref, dg_ref, acc_ref = refs[-3:]
        i = pl.program_id(0)
        xv = x_ref[...]
        rstd = lax.rsqrt(jnp.mean(xv * xv, axis=-1, keepdims=True) + RMS_EPS)
        xh = xv * rstd
        dyv = dy_ref[...].astype(F32)
        dxh = dyv * g_ref[...]
        dx = rstd * (dxh - xh * jnp.mean(dxh * xh, axis=-1, keepdims=True))
        if has_res:
            dx = dx + res_ref[...]
        dx_ref[...] = dx
        part = (dyv * xh).reshape(tr // 8, 8, d).sum(axis=0)

        @pl.when(i == 0)
        def _():
            acc_ref[...] = part

        @pl.when(i > 0)
        def _():
            acc_ref[...] += part

        @pl.when(i == n - 1)
        def _():
            dg_ref[...] = jnp.sum(acc_ref[...], axis=0, keepdims=True)

    row = pl.BlockSpec((tr, d), lambda i: (i, 0))
    in_specs = [row, row, pl.BlockSpec((1, d), lambda i: (0, 0))] + ([row] if has_res else [])
    ops = [dy, x, gain] + ([res] if has_res else [])
    return pl.pallas_call(
        body, name=name, grid=(n,), in_specs=in_specs,
        out_specs=(row, pl.BlockSpec((1, d), lambda i: (0, 0))),
        out_shape=(jax.ShapeDtypeStruct((r, d), F32), jax.ShapeDtypeStruct((1, d), F32)),
        scratch_shapes=[pltpu.VMEM((8, d), F32)],
        compiler_params=_cparams(("arbitrary",)),
    )(*ops)


def _final_loss(name, h, target, gain, *, tr=512):
    r, d = h.shape
    tr = min(tr, r)
    n = r // tr

    def body(h_ref, t_ref, g_ref, loss_ref, dh_ref, dg_ref, accl_ref, accg_ref):
        i = pl.program_id(0)
        xv = h_ref[...]
        rstd = lax.rsqrt(jnp.mean(xv * xv, axis=-1, keepdims=True) + RMS_EPS)
        xh = xv * rstd
        e = xh * g_ref[...] - t_ref[...]
        dyv = e * (1.0 / d)
        dxh = dyv * g_ref[...]
        dh_ref[...] = rstd * (dxh - xh * jnp.mean(dxh * xh, axis=-1, keepdims=True))
        lpart = (e * e).reshape(tr // 8, 8, d).sum(axis=0)
        gpart = (dyv * xh).reshape(tr // 8, 8, d).sum(axis=0)

        @pl.when(i == 0)
        def _():
            accl_ref[...] = lpart
            accg_ref[...] = gpart

        @pl.when(i > 0)
        def _():
            accl_ref[...] += lpart
            accg_ref[...] += gpart

        @pl.when(i == n - 1)
        def _():
            tot = jnp.sum(jnp.sum(accl_ref[...], axis=0, keepdims=True), axis=1, keepdims=True)
            loss_ref[...] = jnp.broadcast_to(tot * (0.5 / d), (1, LANE))
            dg_ref[...] = jnp.sum(accg_ref[...], axis=0, keepdims=True)

    row = pl.BlockSpec((tr, d), lambda i: (i, 0))
    one = pl.BlockSpec((1, d), lambda i: (0, 0))
    return pl.pallas_call(
        body, name=name, grid=(n,), in_specs=[row, row, one],
        out_specs=(pl.BlockSpec((1, LANE), lambda i: (0, 0)), row, one),
        out_shape=(jax.ShapeDtypeStruct((1, LANE), F32), jax.ShapeDtypeStruct((r, d), F32),
                   jax.ShapeDtypeStruct((1, d), F32)),
        scratch_shapes=[pltpu.VMEM((8, d), F32), pltpu.VMEM((8, d), F32)],
        compiler_params=_cparams(("arbitrary",)),
    )(h, target, gain)


_GELU_C = math.sqrt(2.0 / math.pi)


def _gelu_parts(z):
    inner = _GELU_C * (z + 0.044715 * z * z * z)
    t = jnp.tanh(inner)
    val = 0.5 * z * (1.0 + t)
    dinner = _GELU_C * (1.0 + 3.0 * 0.044715 * z * z)
    grad = 0.5 * (1.0 + t) + 0.5 * z * (1.0 - t * t) * dinner
    return val, grad


def _ssm_post_fwd(name, y8, u8, d8, *, tr=256):
    r, c = y8.shape
    tr = min(tr, r)

    def body(y_ref, u_ref, d_ref, o_ref):
        z = y_ref[...] + d_ref[...] * u_ref[...]
        o_ref[...] = _gelu_parts(z)[0].astype(o_ref.dtype)

    row = pl.BlockSpec((tr, c), lambda i: (i, 0))
    return pl.pallas_call(
        body, name=name, grid=(r // tr,), in_specs=[row, row, pl.BlockSpec((1, c), lambda i: (0, 0))],
        out_specs=row, out_shape=jax.ShapeDtypeStruct((r, c), BF16),
        compiler_params=_cparams(("parallel",)),
    )(y8, u8, d8)


def _ssm_post_bwd(name, dact8, y8, u8, d8, *, tr=256):
    r, c = y8.shape
    tr = min(tr, r)
    n = r // tr

    def body(da_ref, y_ref, u_ref, d_ref, dz_ref, dd_ref, acc_ref):
        i = pl.program_id(0)
        uv = u_ref[...]
        z = y_ref[...] + d_ref[...] * uv
        dz = da_ref[...].astype(F32) * _gelu_parts(z)[1]
        dz_ref[...] = dz
        part = (dz * uv).reshape(tr // 8, 8, c).sum(axis=0)

        @pl.when(i == 0)
        def _():
            acc_ref[...] = part

        @pl.when(i > 0)
        def _():
            acc_ref[...] += part

        @pl.when(i == n - 1)
        def _():
            tot = jnp.sum(acc_ref[...], axis=0, keepdims=True)
            out = tot[:, 0:SSM_WIDTH]
            for j in range(1, c // SSM_WIDTH):
                out = out + tot[:, j * SSM_WIDTH:(j + 1) * SSM_WIDTH]
            dd_ref[...] = out

    row = pl.BlockSpec((tr, c), lambda i: (i, 0))
    return pl.pallas_call(
        body, name=name, grid=(n,), in_specs=[row, row, row, pl.BlockSpec((1, c), lambda i: (0, 0))],
        out_specs=(row, pl.BlockSpec((1, SSM_WIDTH), lambda i: (0, 0))),
        out_shape=(jax.ShapeDtypeStruct((r, c), F32), jax.ShapeDtypeStruct((1, SSM_WIDTH), F32)),
        scratch_shapes=[pltpu.VMEM((8, c), F32)],
        compiler_params=_cparams(("arbitrary",)),
    )(dact8, y8, u8, d8)


def _mix_fwd(name, glu, gates, out_b, *, tr=256):
    r = glu.shape[0]
    d = D_MODEL
    tr = min(tr, r)

    def body(glu_ref, gate_ref, ob_ref, o_ref):
        out_a = glu_ref[:, 0:d] * _sigmoid(glu_ref[:, d:2 * d])
        mix = _sigmoid(gate_ref[:, 0:d]) * out_a + _sigmoid(gate_ref[:, d:2 * d]) * ob_ref[...]
        o_ref[...] = mix.astype(o_ref.dtype)

    wide = pl.BlockSpec((tr, 2 * d), lambda i: (i, 0))
    row = pl.BlockSpec((tr, d), lambda i: (i, 0))
    return pl.pallas_call(
        body, name=name, grid=(r // tr,), in_specs=[wide, wide, row], out_specs=row,
        out_shape=jax.ShapeDtypeStruct((r, d), BF16), compiler_params=_cparams(("parallel",)),
    )(glu, gates, out_b)


def _mix_bwd(name, dmix, glu, gates, out_b, *, tr=256):
    r = glu.shape[0]
    d = D_MODEL
    tr = min(tr, r)

    def body(dm_ref, glu_ref, gate_ref, ob_ref, dglu_ref, dgate_ref, dob_ref):
        dm = dm_ref[...]
        glu_a = glu_ref[:, 0:d]
        sb = _sigmoid(glu_ref[:, d:2 * d])
        ga = _sigmoid(gate_ref[:, 0:d])
        gb = _sigmoid(gate_ref[:, d:2 * d])
        out_a = glu_a * sb
        dout_a = dm * ga
        dglu_ref[:, 0:d] = (dout_a * sb).astype(dglu_ref.dtype)
        dglu_ref[:, d:2 * d] = (dout_a * glu_a * sb * (1.0 - sb)).astype(dglu_ref.dtype)
        dgate_ref[:, 0:d] = (dm * out_a * ga * (1.0 - ga)).astype(dgate_ref.dtype)
        dgate_ref[:, d:2 * d] = (dm * ob_ref[...] * gb * (1.0 - gb)).astype(dgate_ref.dtype)
        dob_ref[...] = (dm * gb).astype(dob_ref.dtype)

    wide = pl.BlockSpec((tr, 2 * d), lambda i: (i, 0))
    row = pl.BlockSpec((tr, d), lambda i: (i, 0))
    return pl.pallas_call(
        body, name=name, grid=(r // tr,), in_specs=[row, wide, wide, row], out_specs=(wide, wide, row),
        out_shape=(jax.ShapeDtypeStruct((r, 2 * d), BF16), jax.ShapeDtypeStruct((r, 2 * d), BF16),
                   jax.ShapeDtypeStruct((r, d), BF16)),
        compiler_params=_cparams(("parallel",)),
    )(dmix, glu, gates, out_b)


def _ssm_mats(lam_re, lam_im, log_dt, b_re, b_im, c_re, c_im, nc):
    hp = lax.Precision.HIGHEST
    t = SSM_CHUNK
    nq = SSM_GROUPS // 8
    lam = lax.complex(lam_re, lam_im)
    z = lam * jnp.exp(log_dt)[:, None]
    ks = jnp.arange(t + 1, dtype=F32)
    apow = jnp.exp(ks[:, None, None] * z[None])
    bbar = ((apow[1] - 1.0) / lam)[..., None] * lax.complex(b_re, b_im)
    c = lax.complex(c_re, c_im)

    ca = c[None] * apow[:, :, None, :]
    kmat = jnp.einsum("kgnp,gpm->kgnm", ca, bbar, precision=hp).real
    ii = np.arange(t)
    lag = ii[None, :] - ii[:, None]
    kt = kmat[np.clip(lag, 0, t)] * jnp.asarray(lag >= 0, F32)[:, :, None, None, None]
    kt = kt.reshape(t, t, nq, 8, SSM_GROUP, SSM_GROUP)
    m_c = kt.transpose(2, 0, 3, 5, 1, 4).reshape(nq, 1024, LANE)

    arev = jnp.exp((float(t - 1) - ks[:t])[:, None, None] * z[None])
    w = arev[:, :, :, None] * bbar[None]
    wr = jnp.stack([w.real, w.imag]).reshape(2, t, nq, 8, SSM_STATE, SSM_GROUP)
    bw_c = wr.transpose(2, 1, 3, 5, 0, 4).reshape(nq, 1024, LANE)

    ca1 = ca[1:]
    cr = jnp.stack([ca1.real, -ca1.imag]).reshape(2, t, nq, 8, SSM_GROUP, SSM_STATE)
    cm_c = cr.transpose(2, 0, 3, 5, 1, 4).reshape(nq, 1024, LANE)

    def tiles(v):
        vq = jnp.concatenate([v.real.reshape(nq, 512), v.imag.reshape(nq, 512)], axis=1)
        return jnp.broadcast_to(vq.reshape(nq, 8, 1, LANE), (nq, 8, 8, LANE))

    return m_c, bw_c, cm_c, tiles(apow[t]), tiles(jnp.exp(float(nc) * z))


_BD_M = (LANE, SSM_GROUP)
_BD_BW = (LANE, SSM_STATE)
_BD_CM = (512, SSM_GROUP)


def _bd_perm(cn):
    rr = lax.broadcasted_iota(jnp.int32, (1024, 1024), 0)
    cc = lax.broadcasted_iota(jnp.int32, (1024, 1024), 1)
    sh = cn.bit_length() - 1
    src = ((rr >> 7) << sh) + (((rr & (LANE - 1)) >> sh) << (3 + sh)) + (rr & (cn - 1))
    return jnp.where(src == cc, 1.0, 0.0).astype(BF16)


def _bd_rowgroup(span):
    r = lax.broadcasted_iota(jnp.int32, (1024, LANE), 0)
    return (r & (span - 1)) >> ((span // 8).bit_length() - 1)


def _bd_expand(name, kind, compact):
    span, cn = kind
    nq = compact.shape[0]

    def body(c_ref, o_ref):
        x = c_ref[...]
        grp = _bd_rowgroup(span)
        xcat = jnp.concatenate([jnp.where(grp == h, x, 0.0) for h in range(8)], axis=1)
        o_ref[...] = _bdot(xcat, _bd_perm(cn), _DIMS["nn"]).astype(o_ref.dtype)

    return pl.pallas_call(
        body, name=name, grid=(nq,), in_specs=[pl.BlockSpec((None, 1024, LANE), lambda q: (q, 0, 0))],
        out_specs=pl.BlockSpec((None, 1024, 1024), lambda q: (q, 0, 0)),
        out_shape=jax.ShapeDtypeStruct((nq, 1024, 1024), BF16),
        compiler_params=_cparams(("parallel",)),
    )(compact)


def _bd_reduce(name, kind, dbig):
    span, cn = kind
    nq = dbig.shape[0]

    def body(g_ref, o_ref):
        perm = _bd_perm(cn)
        hi, mid, lo = _split3(g_ref[...])
        d = _DIMS["nt"]
        back = _bdot(hi, perm, d) + _bdot(mid, perm, d) + _bdot(lo, perm, d)
        grp = _bd_rowgroup(span)
        out = jnp.zeros((1024, LANE), F32)
        for h in range(8):
            out = jnp.where(grp == h, back[:, h * LANE:(h + 1) * LANE], out)
        o_ref[...] = out

    return pl.pallas_call(
        body, name=name, grid=(nq,), in_specs=[pl.BlockSpec((None, 1024, 1024), lambda q: (q, 0, 0))],
        out_specs=pl.BlockSpec((None, 1024, LANE), lambda q: (q, 0, 0)),
        out_shape=jax.ShapeDtypeStruct((nq, 1024, LANE), F32),
        compiler_params=_cparams(("parallel",)),
    )(dbig)


def _x_tile_specs(nc, nq):
    return [pl.BlockSpec((nc, LANE), lambda q, t, i=i: (0, i * nq + q)) for i in range(SSM_CHUNK)]


def _cat_tiles(refs):
    return jnp.concatenate([r[...] for r in refs], axis=1)


def _ssm_w(name, x8, bw):
    nc = x8.shape[0]
    nq = bw.shape[0]

    def body(*refs):
        xq = _cat_tiles(refs[:8])
        refs[9][...] = _bdot(xq, refs[8][...], _DIMS["nn"])

    return pl.pallas_call(
        body, name=name, grid=(nq, 8),
        in_specs=_x_tile_specs(nc, nq) + [pl.BlockSpec((None, 1024, LANE), lambda q, t: (q, 0, t))],
        out_specs=pl.BlockSpec((None, None, nc, LANE), lambda q, t: (q, t, 0, 0)),
        out_shape=jax.ShapeDtypeStruct((nq, 8, nc, LANE), F32),
        compiler_params=_cparams(("parallel", "arbitrary")),
    )(*([x8] * 8), bw)


def _ssm_scan(name, w4, a_t, aseg_t, *, reverse, sprev4=None):
    nq, _, nc, _ = w4.shape
    ns = nc // 8
    with_da = sprev4 is not None

    def body(*refs):
        w_ref, a_ref, aseg_ref = refs[:3]
        s_ref = refs[3] if with_da else None
        o_ref = refs[4] if with_da else refs[3]
        da_ref = refs[5] if with_da else None
        sgn = -1.0 if reverse else 1.0
        ar = [a_ref[j] for j in range(4)]
        ai = [sgn * a_ref[j + 4] for j in range(4)]
        gr = [aseg_ref[j] for j in range(4)]
        gi = [sgn * aseg_ref[j + 4] for j in range(4)]
        zero = tuple(jnp.zeros((8, LANE), F32) for _ in range(8))

        def rows(tt):
            return pl.ds((ns - 1 - tt) if reverse else tt, 8, stride=ns)

        def step(carry, w):
            new_r = [ar[j] * carry[j] - ai[j] * carry[j + 4] + w[j] for j in range(4)]
            new_i = [ar[j] * carry[j + 4] + ai[j] * carry[j] + w[j + 4] for j in range(4)]
            return tuple(new_r + new_i)

        def pass1(tt, carry):
            return step(carry, [w_ref[j, rows(tt), :] for j in range(8)])

        ends = lax.fori_loop(0, ns, pass1, zero)
        sub = lax.broadcasted_iota(jnp.int32, (8, LANE), 0)
        init = list(zero)
        order = range(7, 0, -1) if reverse else range(0, 7)
        for s in order:
            nxt = s - 1 if reverse else s + 1
            cand_r = [gr[j] * init[j] - gi[j] * init[j + 4] + ends[j] for j in range(4)]
            cand_i = [gr[j] * init[j + 4] + gi[j] * init[j] + ends[j + 4] for j in range(4)]
            cand = cand_r + cand_i
            shift = 7 if reverse else 1
            init = [jnp.where(sub == nxt, pltpu.roll(cand[j], shift, axis=0), init[j]) for j in range(8)]

        def pass2(tt, state):
            carry, acc = state
            r = rows(tt)
            for j in range(8):
                o_ref[j, r, :] = carry[j]
            if with_da:
                sp = [s_ref[j, r, :] for j in range(8)]
                acc_r = [acc[j] + carry[j] * sp[j] + carry[j + 4] * sp[j + 4] for j in range(4)]
                acc_i = [acc[j + 4] + carry[j + 4] * sp[j] - carry[j] * sp[j + 4] for j in range(4)]
                acc = tuple(acc_r + acc_i)
            return step(carry, [w_ref[j, r, :] for j in range(8)]), acc

        _, acc = lax.fori_loop(0, ns, pass2, (tuple(init), zero))
        if with_da:
            for j in range(8):
                da_ref[j] = acc[j]

    big = pl.BlockSpec((None, 8, nc, LANE), lambda q: (q, 0, 0, 0))
    small = pl.BlockSpec((None, 8, 8, LANE), lambda q: (q, 0, 0, 0))
    in_specs = [big, small, small] + ([big] if with_da else [])
    ops = [w4, a_t, aseg_t] + ([sprev4] if with_da else [])
    out_specs = (big, small) if with_da else big
    big_s = jax.ShapeDtypeStruct((nq, 8, nc, LANE), F32)
    out_shape = (big_s, jax.ShapeDtypeStruct((nq, 8, 8, LANE), F32)) if with_da else big_s
    return pl.pallas_call(
        body, name=name, grid=(nq,), in_specs=in_specs, out_specs=out_specs, out_shape=out_shape,
        compiler_params=_cparams(("parallel",)),
    )(*ops)


def _ssm_y(name, x8, sprev4, m_mat, cm_mat):
    nc = x8.shape[0]
    nq = m_mat.shape[0]

    def body(*refs):
        xq = _cat_tiles(refs[:8])
        s_ref, m_ref, cm_ref, o_ref = refs[8:12]
        sq = jnp.concatenate([s_ref[t] for t in range(8)], axis=1)
        o_ref[...] = _bdot(xq, m_ref[...], _DIMS["nn"]) + _bdot(sq, cm_ref[...], _DIMS["nn"])

    col = pl.BlockSpec((None, 1024, LANE), lambda q, j: (q, 0, j))
    return pl.pallas_call(
        body, name=name, grid=(nq, 8),
        in_specs=_x_tile_specs(nc, nq) + [pl.BlockSpec((None, 8, nc, LANE), lambda q, j: (q, 0, 0, 0)), col, col],
        out_specs=pl.BlockSpec((nc, LANE), lambda q, j: (0, j * nq + q)),
        out_shape=jax.ShapeDtypeStruct((nc, 8 * SSM_WIDTH), F32),
        compiler_params=_cparams(("parallel", "arbitrary")),
    )(*([x8] * 8), sprev4, m_mat, cm_mat)


def _ssm_ds(name, dz8, sprev4, cm_mat):
    nc = dz8.shape[0]
    nq = cm_mat.shape[0]

    def body(*refs):
        dyq = _cat_tiles(refs[:8]).astype(BF16)
        s_ref, cm_ref, ds_ref, dcm_ref = refs[8:12]
        ds_ref[...] = _bdot(dyq, cm_ref[...], _DIMS["nt"])
        dcm_ref[...] = _bdot(s_ref[...], dyq, _DIMS["tn"])

    tile = pl.BlockSpec((None, None, nc, LANE), lambda q, t: (q, t, 0, 0))
    rowblk = pl.BlockSpec((None, LANE, 1024), lambda q, t: (q, t, 0))
    return pl.pallas_call(
        body, name=name, grid=(nq, 8),
        in_specs=_x_tile_specs(nc, nq) + [tile, rowblk],
        out_specs=(tile, rowblk),
        out_shape=(jax.ShapeDtypeStruct((nq, 8, nc, LANE), F32), jax.ShapeDtypeStruct((nq, 1024, 1024), F32)),
        compiler_params=_cparams(("parallel", "arbitrary")),
    )(*([dz8] * 8), sprev4, cm_mat)


def _ssm_dx(name, dz8, g4, x8, m_mat, bw_mat, d8):
    nc = dz8.shape[0]
    nq = m_mat.shape[0]

    def body(*refs):
        dyq = _cat_tiles(refs[:8]).astype(BF16)
        g_ref, x_ref, m_ref, bw_ref, d_ref, dzi_ref, dx_ref, dm_ref, dbw_ref = refs[8:17]
        gq = jnp.concatenate([g_ref[t] for t in range(8)], axis=1).astype(BF16)
        dx = _bdot(dyq, m_ref[...], _DIMS["nt"]) + _bdot(gq, bw_ref[...], _DIMS["nt"])
        dx_ref[...] = (dx + d_ref[...] * dzi_ref[...]).astype(dx_ref.dtype)
        xi = x_ref[...]
        dm_ref[...] = _bdot(xi, dyq, _DIMS["tn"])
        dbw_ref[...] = _bdot(xi, gq, _DIMS["tn"])

    xtile = pl.BlockSpec((nc, LANE), lambda q, i: (0, i * nq + q))
    rowblk = pl.BlockSpec((None, LANE, 1024), lambda q, i: (q, i, 0))
    return pl.pallas_call(
        body, name=name, grid=(nq, 8),
        in_specs=_x_tile_specs(nc, nq) + [pl.BlockSpec((None, 8, nc, LANE), lambda q, i: (q, 0, 0, 0)), xtile, rowblk, rowblk,
                                          pl.BlockSpec((1, LANE), lambda q, i: (0, q)), xtile],
        out_specs=(xtile, rowblk, rowblk),
        out_shape=(jax.ShapeDtypeStruct((nc, 8 * SSM_WIDTH), BF16), jax.ShapeDtypeStruct((nq, 1024, 1024), F32),
                   jax.ShapeDtypeStruct((nq, 1024, 1024), F32)),
        compiler_params=_cparams(("parallel", "arbitrary")),
    )(*([dz8] * 8), g4, x8, m_mat, bw_mat, d8, dz8)


CUM_BLK = 256


def _split3(x):
    hi = x.astype(BF16)
    r1 = x - hi.astype(F32)
    mid = r1.astype(BF16)
    lo = (r1 - mid.astype(F32)).astype(BF16)
    return hi, mid, lo


def _tri_dot(x, tri):
    hi, mid, lo = _split3(x)
    d = _DIMS["nn"]
    return _bdot(hi, tri, d) + _bdot(mid, tri, d) + _bdot(lo, tri, d)


def _tri(n, lower):
    r = lax.broadcasted_iota(jnp.int32, (n, n), 0)
    c = lax.broadcasted_iota(jnp.int32, (n, n), 1)
    return jnp.where((r >= c) if lower else (r <= c), 1.0, 0.0).astype(BF16)


def _fox_cum(name, fproj, bcol):
    seq = fproj.shape[0]
    blk = min(CUM_BLK, seq)

    def body(f_ref, b_ref, o_ref, carry_ref):
        i = pl.program_id(0)

        @pl.when(i == 0)
        def _():
            carry_ref[...] = jnp.zeros_like(carry_ref)

        z = f_ref[...].T + b_ref[...]
        logf = jnp.minimum(z, 0.0) - jnp.log(1.0 + jnp.exp(-jnp.abs(z)))
        carry = carry_ref[...]
        cum = _tri_dot(logf, _tri(blk, lower=False)) + jnp.tile(carry, (1, blk // LANE))
        o_ref[...] = cum[0:8, :]
        carry_ref[...] = carry + jnp.sum(logf, axis=1, keepdims=True)

    return pl.pallas_call(
        body, name=name, grid=(seq // blk,),
        in_specs=[pl.BlockSpec((blk, LANE), lambda i: (i, 0)), pl.BlockSpec((LANE, 1), lambda i: (0, 0))],
        out_specs=pl.BlockSpec((8, blk), lambda i: (0, i)),
        out_shape=jax.ShapeDtypeStruct((8, seq), F32),
        scratch_shapes=[pltpu.VMEM((LANE, LANE), F32)],
        compiler_params=_cparams(("arbitrary",)),
    )(fproj, bcol)


def _fox_cum_bwd(name, dcum_t, fproj, bcol):
    seq = fproj.shape[0]
    blk = min(CUM_BLK, seq)
    n = seq // blk

    def body(dc_ref, f_ref, b_ref, df_ref, db_ref, carry_ref, acc_ref):
        i = pl.program_id(0)

        @pl.when(i == 0)
        def _():
            carry_ref[...] = jnp.zeros_like(carry_ref)
            acc_ref[...] = jnp.zeros_like(acc_ref)

        dc = jnp.concatenate([dc_ref[...], jnp.zeros((LANE - 8, blk), F32)], axis=0)
        carry = carry_ref[...]
        dlogf = _tri_dot(dc, _tri(blk, lower=True)) + jnp.tile(carry, (1, blk // LANE))
        carry_ref[...] = carry + jnp.sum(dc, axis=1, keepdims=True)
        z = f_ref[...].T + b_ref[...]
        dft = dlogf / (1.0 + jnp.exp(z))
        df_ref[...] = dft.T.astype(df_ref.dtype)
        acc_ref[...] += jnp.sum(dft, axis=1, keepdims=True)

        @pl.when(i == n - 1)
        def _():
            db_ref[...] = acc_ref[...]

    return pl.pallas_call(
        body, name=name, grid=(n,),
        in_specs=[pl.BlockSpec((8, blk), lambda i: (0, n - 1 - i)), pl.BlockSpec((blk, LANE), lambda i: (n - 1 - i, 0)),
                  pl.BlockSpec((LANE, 1), lambda i: (0, 0))],
        out_specs=(pl.BlockSpec((blk, LANE), lambda i: (n - 1 - i, 0)), pl.BlockSpec((LANE, LANE), lambda i: (0, 0))),
        out_shape=(jax.ShapeDtypeStruct((seq, LANE), BF16), jax.ShapeDtypeStruct((LANE, LANE), F32)),
        scratch_shapes=[pltpu.VMEM((LANE, LANE), F32), pltpu.VMEM((LANE, LANE), F32)],
        compiler_params=_cparams(("arbitrary",)),
    )(dcum_t, fproj, bcol)


FOX_BLK = 512
FOX_SCALE = FOX_HEAD_DIM ** -0.5


def _fox_head_mask(shape, hh):
    lane = lax.broadcasted_iota(jnp.int32, shape, 1)
    return (lane < FOX_HEAD_DIM) if hh == 0 else (lane >= FOX_HEAD_DIM)


def _fox_bias(cum_ref, hh, q0, k0, blk):
    c0 = jnp.max(cum_ref[hh:hh + 1, pl.ds(q0, LANE)], axis=1, keepdims=True)
    return c0 - cum_ref[hh:hh + 1, pl.ds(k0, blk)]


def _fox_fwd(name, qkv, cum_t):
    seq = qkv.shape[0]
    blk = min(FOX_BLK, seq)
    nb = seq // blk
    npair = FOX_HEADS // 2

    def body(q_ref, k_ref, v_ref, cum_ref, o_ref, lse_ref):
        iq = pl.program_id(1)
        q0 = pl.multiple_of(iq * blk, blk)
        qv = q_ref[...]
        row = lax.broadcasted_iota(jnp.int32, (blk, blk), 0)
        col = lax.broadcasted_iota(jnp.int32, (blk, blk), 1)
        qhs = [jnp.where(_fox_head_mask(qv.shape, hh), qv, jnp.zeros_like(qv)) * FOX_SCALE for hh in range(2)]

        def block(kb, states, masked):
            k0 = pl.multiple_of(kb * blk, blk)
            kv = k_ref[pl.ds(k0, blk), :]
            vv = v_ref[pl.ds(k0, blk), :]
            new = []
            for hh in range(2):
                m, l, acc = states[hh]
                s = _bdot(qhs[hh], kv, _DIMS["nt"]) + _fox_bias(cum_ref, hh, q0, k0, blk)
                if masked:
                    s = jnp.where(row >= col, s, -jnp.inf)
                m_new = jnp.maximum(m, jnp.max(s, axis=1, keepdims=True))
                alpha = jnp.exp(m - m_new)
                p = jnp.exp(s - m_new)
                l = alpha * l + jnp.sum(p, axis=1, keepdims=True)
                acc = alpha * acc + _bdot(p, vv, _DIMS["nn"])
                new.append((m_new, l, acc))
            return tuple(new)

        init = (jnp.full((blk, 1), -jnp.inf, F32), jnp.zeros((blk, 1), F32), jnp.zeros((blk, LANE), F32))
        states = lax.fori_loop(0, iq, lambda kb, st: block(kb, st, False), (init, init))
        states = block(iq, states, True)
        outs = []
        for hh in range(2):
            m, l, acc = states[hh]
            outs.append(acc / l)
            lse_ref[hh] = jnp.broadcast_to(m + jnp.log(l), (blk, LANE))
        o_ref[...] = jnp.where(_fox_head_mask(outs[0].shape, 0), outs[0], outs[1]).astype(o_ref.dtype)

    return pl.pallas_call(
        body, name=name, grid=(npair, nb),
        in_specs=[pl.BlockSpec((blk, LANE), lambda p, i: (i, p)),
                  pl.BlockSpec((seq, LANE), lambda p, i: (0, npair + p)),
                  pl.BlockSpec((seq, LANE), lambda p, i: (0, 2 * npair + p)),
                  pl.BlockSpec((None, 2, seq), lambda p, i: (p, 0, 0))],
        out_specs=(pl.BlockSpec((blk, LANE), lambda p, i: (i, p)),
                   pl.BlockSpec((2, blk, LANE), lambda p, i: (p, i, 0))),
        out_shape=(jax.ShapeDtypeStruct((seq, FOX_WIDTH), BF16), jax.ShapeDtypeStruct((FOX_HEADS, seq, LANE), F32)),
        compiler_params=_cparams(("parallel", "arbitrary")),
    )(qkv, qkv, qkv, cum_t)


def _fox_bwd(name, qkv, cum_t, att, datt, lse):
    seq = qkv.shape[0]
    blk = min(FOX_BLK, seq)
    nb = seq // blk
    npair = FOX_HEADS // 2

    def body(q_ref, k_ref, v_ref, cum_ref, o_ref, do_ref, lse_ref, dq_ref, dk_ref, dv_ref, dcum_ref):
        iq = pl.program_id(1)
        q0 = pl.multiple_of(iq * blk, blk)

        @pl.when(iq == 0)
        def _():
            dk_ref[...] = jnp.zeros_like(dk_ref)
            dv_ref[...] = jnp.zeros_like(dv_ref)
            dcum_ref[...] = jnp.zeros_like(dcum_ref)

        qv = q_ref[...]
        dov = do_ref[...].astype(F32)
        ov = o_ref[...].astype(F32)
        row = lax.broadcasted_iota(jnp.int32, (blk, blk), 0)
        col = lax.broadcasted_iota(jnp.int32, (blk, blk), 1)
        qhs, dohbs, deltas, lses = [], [], [], []
        for hh in range(2):
            hm = _fox_head_mask(qv.shape, hh)
            qhs.append(jnp.where(hm, qv, jnp.zeros_like(qv)) * FOX_SCALE)
            doh = jnp.where(hm, dov, 0.0)
            dohbs.append(doh.astype(BF16))
            deltas.append(jnp.sum(doh * ov, axis=1, keepdims=True))
            lses.append(jnp.tile(lse_ref[hh], (1, blk // LANE)))

        def block(kb, accs, masked):
            k0 = pl.multiple_of(kb * blk, blk)
            kv = k_ref[pl.ds(k0, blk), :]
            vv = v_ref[pl.ds(k0, blk), :]
            new = []
            dk_blk = None
            dv_blk = None
            for hh in range(2):
                dq_acc, rs_acc = accs[hh]
                s = _bdot(qhs[hh], kv, _DIMS["nt"]) + _fox_bias(cum_ref, hh, q0, k0, blk)
                p = jnp.exp(s - lses[hh])
                if masked:
                    p = jnp.where(row >= col, p, 0.0)
                dp = _bdot(dohbs[hh], vv, _DIMS["nt"])
                ds = p * (dp - deltas[hh])
                dsb = ds.astype(BF16)
                dk_h = _bdot(dsb, qhs[hh], _DIMS["tn"])
                dv_h = _bdot(p, dohbs[hh], _DIMS["tn"])
                dk_blk = dk_h if dk_blk is None else dk_blk + dk_h
                dv_blk = dv_h if dv_blk is None else dv_blk + dv_h
                dcum_ref[hh:hh + 1, pl.ds(k0, blk)] -= jnp.sum(ds, axis=0, keepdims=True)
                new.append((dq_acc + _bdot(dsb, kv, _DIMS["nn"]), rs_acc + jnp.sum(ds, axis=1, keepdims=True)))
            dk_ref[pl.ds(k0, blk), :] += dk_blk
            dv_ref[pl.ds(k0, blk), :] += dv_blk
            return tuple(new)

        init = (jnp.zeros((blk, LANE), F32), jnp.zeros((blk, 1), F32))
        accs = lax.fori_loop(0, iq, lambda kb, a: block(kb, a, False), (init, init))
        accs = block(iq, accs, True)
        for hh in range(2):
            dcum_ref[hh:hh + 1, pl.ds(q0, blk)] += jnp.broadcast_to(accs[hh][1], (blk, LANE)).T[0:1, :]
        dq = jnp.where(_fox_head_mask(qv.shape, 0), accs[0][0], accs[1][0]) * FOX_SCALE
        dq_ref[...] = dq.astype(dq_ref.dtype)

    qblk = pl.BlockSpec((blk, LANE), lambda p, i: (i, p))
    full = pl.BlockSpec((seq, LANE), lambda p, i: (0, p))
    return pl.pallas_call(
        body, name=name, grid=(npair, nb),
        in_specs=[qblk,
                  pl.BlockSpec((seq, LANE), lambda p, i: (0, npair + p)),
                  pl.BlockSpec((seq, LANE), lambda p, i: (0, 2 * npair + p)),
                  pl.BlockSpec((None, 2, seq), lambda p, i: (p, 0, 0)),
                  qblk, qblk,
                  pl.BlockSpec((2, blk, LANE), lambda p, i: (p, i, 0))],
        out_specs=(qblk, full, full, pl.BlockSpec((None, 2, seq), lambda p, i: (p, 0, 0))),
        out_shape=(jax.ShapeDtypeStruct((seq, FOX_WIDTH), BF16), jax.ShapeDtypeStruct((seq, FOX_WIDTH), F32),
                   jax.ShapeDtypeStruct((seq, FOX_WIDTH), F32), jax.ShapeDtypeStruct((npair, 2, seq), F32)),
        compiler_params=_cparams(("arbitrary", "arbitrary")),
    )(qkv, qkv, qkv, cum_t, att, datt, lse)


MEM_SCALE = MEM_HEAD_DIM ** -0.5


def _mem_probs(qh, kh):
    s = _bdot(qh, kh, _DIMS["nt"]) * MEM_SCALE
    p = jnp.exp(s - jnp.max(s, axis=1, keepdims=True))
    return p / jnp.sum(p, axis=1, keepdims=True)


def _mem_fwd(name, q2, kv, *, tr=512):
    seq = q2.shape[0]
    mlen = kv.shape[0]
    tr = min(tr, seq)

    def body(q_ref, kv_ref, o_ref):
        for h in range(MEM_HEADS):
            sl = slice(h * MEM_HEAD_DIM, (h + 1) * MEM_HEAD_DIM)
            sv = slice(MEM_WIDTH + h * MEM_HEAD_DIM, MEM_WIDTH + (h + 1) * MEM_HEAD_DIM)
            p = _mem_probs(q_ref[:, sl], kv_ref[:, sl])
            o_ref[:, sl] = _bdot(p, kv_ref[:, sv], _DIMS["nn"]).astype(o_ref.dtype)

    return pl.pallas_call(
        body, name=name, grid=(seq // tr,),
        in_specs=[pl.BlockSpec((tr, MEM_WIDTH), lambda i: (i, 0)), pl.BlockSpec((mlen, 2 * MEM_WIDTH), lambda i: (0, 0))],
        out_specs=pl.BlockSpec((tr, MEM_WIDTH), lambda i: (i, 0)),
        out_shape=jax.ShapeDtypeStruct((seq, MEM_WIDTH), BF16),
        compiler_params=_cparams(("parallel",)),
    )(q2, kv)


def _mem_bwd(name, q2, kv, do2, *, tr=512):
    seq = q2.shape[0]
    mlen = kv.shape[0]
    tr = min(tr, seq)

    def body(q_ref, kv_ref, do_ref, dq_ref, dkv_ref):
        i = pl.program_id(0)

        @pl.when(i == 0)
        def _():
            dkv_ref[...] = jnp.zeros_like(dkv_ref)

        for h in range(MEM_HEADS):
            sl = slice(h * MEM_HEAD_DIM, (h + 1) * MEM_HEAD_DIM)
            sv = slice(MEM_WIDTH + h * MEM_HEAD_DIM, MEM_WIDTH + (h + 1) * MEM_HEAD_DIM)
            qh = q_ref[:, sl]
            kh = kv_ref[:, sl]
            doh = do_ref[:, sl].astype(BF16)
            p = _mem_probs(qh, kh)
            dp = _bdot(doh, kv_ref[:, sv], _DIMS["nt"])
            ds = (p * (dp - jnp.sum(p * dp, axis=1, keepdims=True)) * MEM_SCALE).astype(BF16)
            dq_ref[:, sl] = _bdot(ds, kh, _DIMS["nn"]).astype(dq_ref.dtype)
            dkv_ref[:, sl] += _bdot(ds, qh, _DIMS["tn"])
            dkv_ref[:, sv] += _bdot(p, doh, _DIMS["tn"])

    row = pl.BlockSpec((tr, MEM_WIDTH), lambda i: (i, 0))
    kvs = pl.BlockSpec((mlen, 2 * MEM_WIDTH), lambda i: (0, 0))
    return pl.pallas_call(
        body, name=name, grid=(seq // tr,), in_specs=[row, kvs, row], out_specs=(row, kvs),
        out_shape=(jax.ShapeDtypeStruct((seq, MEM_WIDTH), BF16), jax.ShapeDtypeStruct((mlen, 2 * MEM_WIDTH), F32)),
        compiler_params=_cparams(("arbitrary",)),
    )(q2, kv, do2)


_HBM = pl.BlockSpec(memory_space=pl.ANY)
_MESH = pl.DeviceIdType.MESH


def _mesh_place():
    x, y, c = lax.axis_index("x"), lax.axis_index("y"), lax.axis_index("c")
    other_chips = [(1 - x, y), (x, 1 - y), (1 - x, 1 - y)]
    return x, y, c, other_chips


def _gather_all(name, arrays):
    n = len(arrays)

    def body(*refs):
        ins, outs = refs[:n], refs[n:2 * n]
        send_sems, recv_sems, local_sems = refs[2 * n:]
        x, y, c, chips = _mesh_place()
        me, sibling = (x, y, c), (x, y, 1 - c)

        def slot(a, place):
            px, py, pc = place
            return outs[a].at[4 * px + 2 * py + pc]

        def copy(a, k, block, to, src=None):
            return pltpu.make_async_remote_copy(
                src_ref=slot(a, block) if src is None else src, dst_ref=slot(a, block),
                send_sem=send_sems.at[a, k], recv_sem=recv_sems.at[a, k], device_id=to, device_id_type=_MESH)

        mine = [pltpu.make_async_copy(ins[a], slot(a, me), local_sems.at[a]) for a in range(n)]
        for cp in mine:
            cp.start()
        first = []
        for a in range(n):
            first.append(copy(a, 0, me, sibling, src=ins[a]))
            first += [copy(a, 1 + j, me, (*chip, c), src=ins[a]) for j, chip in enumerate(chips)]
        for cp in first:
            cp.start()
        passed = []
        for j, chip in enumerate(chips):
            for a in range(n):
                copy(a, 1 + j, (*chip, c), me).wait_recv()
                fwd = copy(a, 4 + j, (*chip, c), sibling)
                fwd.start()
                passed.append(fwd)
        for a in range(n):
            copy(a, 0, sibling, me).wait_recv()
            for j, chip in enumerate(chips):
                copy(a, 4 + j, (*chip, 1 - c), me).wait_recv()
        for cp in first + passed:
            cp.wait_send()
        for cp in mine:
            cp.wait()

    out_shape = tuple(jax.ShapeDtypeStruct((N_DEV,) + arr.shape, arr.dtype) for arr in arrays)
    return pl.pallas_call(
        body, name=name, in_specs=[_HBM] * n, out_specs=tuple([_HBM] * n), out_shape=out_shape,
        scratch_shapes=[pltpu.SemaphoreType.DMA((n, N_DEV - 1)), pltpu.SemaphoreType.DMA((n, N_DEV - 1)),
                        pltpu.SemaphoreType.DMA((n,))],
    )(*arrays)


def _scatter_sibling(name, arrays):
    n = len(arrays)

    def body(*refs):
        ins, sibs = refs[:n], refs[n:2 * n]
        send_sems, recv_sems = refs[2 * n:]
        x, y, c, _ = _mesh_place()
        copies = []
        for a in range(n):
            for j in range(4):
                rdma = pltpu.make_async_remote_copy(
                    src_ref=ins[a].at[2 * j + (1 - c)], dst_ref=sibs[a].at[j], send_sem=send_sems.at[a, j],
                    recv_sem=recv_sems.at[a, j], device_id=(x, y, 1 - c), device_id_type=_MESH)
                rdma.start()
                copies.append(rdma)
        for cp in copies:
            cp.wait()

    four = tuple(jax.ShapeDtypeStruct((4,) + arr.shape[1:], arr.dtype) for arr in arrays)
    return pl.pallas_call(
        body, name=name, in_specs=[_HBM] * n, out_specs=tuple([_HBM] * n), out_shape=four,
        scratch_shapes=[pltpu.SemaphoreType.DMA((n, 4)), pltpu.SemaphoreType.DMA((n, 4))],
    )(*arrays)


def _add_chip_partials(name, pieces, sibs):
    n = len(pieces)
    core = lax.axis_index("c").astype(jnp.int32).reshape(1)

    def body(c_ref, *refs):
        for a in range(n):
            out = refs[2 * n + a]
            out[...] = (refs[a][...].astype(F32) + refs[n + a][...].astype(F32)).astype(out.dtype)

    own_specs = [pl.BlockSpec((None,) + arr.shape[1:], lambda j, c_ref: (2 * j + c_ref[0], 0, 0)) for arr in pieces]
    four_specs = [pl.BlockSpec((None,) + arr.shape[1:], lambda j, c_ref: (j, 0, 0)) for arr in sibs]
    return pl.pallas_call(
        body, name=name,
        grid_spec=pltpu.PrefetchScalarGridSpec(num_scalar_prefetch=1, grid=(4,), in_specs=own_specs + four_specs,
                                               out_specs=tuple(four_specs)),
        out_shape=tuple(jax.ShapeDtypeStruct(arr.shape, arr.dtype) for arr in sibs),
        compiler_params=_cparams(("parallel",)),
    )(core, *pieces, *sibs)


def _scatter_chips(name, arrays):
    n = len(arrays)

    def body(*refs):
        ins, outs = refs[:n], refs[n:2 * n]
        send_sems, recv_sems, local_sems = refs[2 * n:]
        x, y, c, chips = _mesh_place()
        my_chip = 2 * x + y
        copies = []
        for a in range(n):
            local = pltpu.make_async_copy(ins[a].at[my_chip], outs[a].at[my_chip], local_sems.at[a])
            local.start()
            copies.append(local)
            for k, (px, py) in enumerate(chips):
                rdma = pltpu.make_async_remote_copy(
                    src_ref=ins[a].at[2 * px + py], dst_ref=outs[a].at[my_chip], send_sem=send_sems.at[a, k],
                    recv_sem=recv_sems.at[a, k], device_id=(px, py, c), device_id_type=_MESH)
                rdma.start()
                copies.append(rdma)
        for cp in copies:
            cp.wait()

    return pl.pallas_call(
        body, name=name, in_specs=[_HBM] * n, out_specs=tuple([_HBM] * n),
        out_shape=tuple(jax.ShapeDtypeStruct(arr.shape, arr.dtype) for arr in arrays),
        scratch_shapes=[pltpu.SemaphoreType.DMA((n, 3)), pltpu.SemaphoreType.DMA((n, 3)),
                        pltpu.SemaphoreType.DMA((n,))],
    )(*arrays)


def _unstack_cols(name, stacked):
    n, rows, cols = stacked.shape

    def body(i_ref, o_ref):
        o_ref[...] = i_ref[...]

    return pl.pallas_call(
        body, name=name, grid=(n,), in_specs=[pl.BlockSpec((None, rows, cols), lambda k: (k, 0, 0))],
        out_specs=pl.BlockSpec((rows, cols), lambda k: (0, k)),
        out_shape=jax.ShapeDtypeStruct((rows, n * cols), stacked.dtype),
        compiler_params=_cparams(("parallel",)),
    )(stacked)


def _restack_cols(name, mat):
    rows, width = mat.shape
    cols = width // N_DEV

    def body(i_ref, o_ref):
        o_ref[...] = i_ref[...]

    return pl.pallas_call(
        body, name=name, grid=(N_DEV,), in_specs=[pl.BlockSpec((rows, cols), lambda k: (0, k))],
        out_specs=pl.BlockSpec((None, rows, cols), lambda k: (k, 0, 0)),
        out_shape=jax.ShapeDtypeStruct((N_DEV, rows, cols), mat.dtype),
        compiler_params=_cparams(("parallel",)),
    )(mat)


def _remap_pieces(runs):
    plan = {}
    for du, dc, su, sc, ln in runs:
        while ln > 0:
            lane = dc % LANE
            take = min(ln, LANE - lane)
            plan.setdefault((du, dc // LANE), []).append((su, sc, take, lane))
            dc, sc, ln = dc + take, sc + take, ln - take
    return plan


def _remap(name, srcs, src_units, runs, *, out_units, out_cols, out_dtype, tr=256):
    rows = srcs[0].shape[-2]
    tr = min(tr, rows)
    plan = _remap_pieces(runs)
    n_src = len(srcs)
    stacked_out = out_units is not None
    n_tiles = out_cols // LANE

    def body(*refs):
        o_ref = refs[n_src]

        def src_tile(unit, t):
            ai, lead = src_units[unit]
            ref = refs[ai]
            sl = slice(t * LANE, (t + 1) * LANE)
            return (ref[:, sl] if lead is None else ref[lead, :, sl]).astype(F32)

        lane = lax.broadcasted_iota(jnp.int32, (tr, LANE), 1)
        for du in range(out_units if stacked_out else 1):
            for t in range(n_tiles):
                acc = jnp.zeros((tr, LANE), F32)
                for su, sc, ln, dl in plan.get((du if stacked_out else None, t), []):
                    st, so = sc // LANE, sc % LANE
                    first = src_tile(su, st)
                    if so == dl and so + ln <= LANE:
                        piece = first
                    else:
                        second = src_tile(su, st + 1) if so + ln > LANE else first
                        both = jnp.concatenate([first, second], axis=1)
                        piece = pltpu.roll(both, (dl - so) % (2 * LANE), axis=1)[:, 0:LANE]
                    acc = piece if (dl == 0 and ln == LANE) else jnp.where(
                        jnp.logical_and(lane >= dl, lane < dl + ln), piece, acc)
                if stacked_out:
                    o_ref[du, :, t * LANE:(t + 1) * LANE] = acc.astype(o_ref.dtype)
                else:
                    o_ref[:, t * LANE:(t + 1) * LANE] = acc.astype(o_ref.dtype)

    in_specs = []
    for arr in srcs:
        if arr.ndim == 2:
            in_specs.append(pl.BlockSpec((tr, arr.shape[1]), lambda i: (i, 0)))
        else:
            in_specs.append(pl.BlockSpec((arr.shape[0], tr, arr.shape[2]), lambda i: (0, i, 0)))
    if stacked_out:
        out_spec = pl.BlockSpec((out_units, tr, out_cols), lambda i: (0, i, 0))
        out_shape = jax.ShapeDtypeStruct((out_units, rows, out_cols), out_dtype)
    else:
        out_spec = pl.BlockSpec((tr, out_cols), lambda i: (i, 0))
        out_shape = jax.ShapeDtypeStruct((rows, out_cols), out_dtype)
    return pl.pallas_call(
        body, name=name, grid=(rows // tr,), in_specs=in_specs, out_specs=out_spec, out_shape=out_shape,
        compiler_params=_cparams(("parallel",)),
    )(*srcs)


def _proj_col(c):
    if c < PROJ_GATE0:
        return c
    if c < PROJ_GATE0 + FOX_HEADS:
        return PROJ_F0 + (c - PROJ_GATE0)
    return c - FOX_HEADS


def _win_runs():
    cuts = sorted(set([0, PROJ_GATE0, PROJ_GATE0 + FOX_HEADS, IN_WIDTH] + [SHARD_IN * k for k in range(N_DEV + 1)]))
    return [(lo // SHARD_IN, lo % SHARD_IN, _proj_col(lo), hi - lo) for lo, hi in zip(cuts[:-1], cuts[1:])]


def _assemble_win(name, stacked):
    runs = [(None, pc, k, sc, ln) for k, sc, pc, ln in _win_runs()]
    return _remap(name, [stacked], [(0, k) for k in range(N_DEV)], runs,
                  out_units=None, out_cols=PROJ_WIDTH, out_dtype=BF16)


def _disassemble_dwin(name, dw):
    runs = [(k, sc, 0, pc, ln) for k, sc, pc, ln in _win_runs()]
    return _remap(name, [dw], [(0, None)], runs, out_units=N_DEV, out_cols=SHARD_IN_PAD, out_dtype=BF16)


def _concat_cols(name, parts, *, tr=512):
    rows = parts[0].shape[0]
    tr = min(tr, rows)
    widths = [p.shape[1] for p in parts]
    total = sum(widths)

    def body(*refs):
        o_ref = refs[len(parts)]
        lo = 0
        for r, w in zip(refs[:len(parts)], widths):
            o_ref[:, lo:lo + w] = r[...].astype(o_ref.dtype)
            lo += w

    return pl.pallas_call(
        body, name=name, grid=(rows // tr,),
        in_specs=[pl.BlockSpec((tr, w), lambda i: (i, 0)) for w in widths],
        out_specs=pl.BlockSpec((tr, total), lambda i: (i, 0)),
        out_shape=jax.ShapeDtypeStruct((rows, total), BF16),
        compiler_params=_cparams(("parallel",)),
    )(*parts)


FFN_BLK = FFN_HIDDEN // 2


def _ffn_col(c):
    half, r = divmod(c, FFN_HIDDEN)
    blk, r = divmod(r, FFN_BLK)
    return blk * 2 * FFN_BLK + half * FFN_BLK + r


def _assemble_wffn(name, stacked):
    runs = [(None, _ffn_col(SHARD_FFN * k), k, 0, SHARD_FFN) for k in range(N_DEV)]
    return _remap(name, [stacked], [(0, k) for k in range(N_DEV)], runs,
                  out_units=None, out_cols=2 * FFN_HIDDEN, out_dtype=BF16)


def _disassemble_dwffn(name, dw):
    runs = [(k, 0, 0, _ffn_col(SHARD_FFN * k), SHARD_FFN) for k in range(N_DEV)]
    return _remap(name, [dw], [(0, None)], runs, out_units=N_DEV, out_cols=SHARD_FFN_PAD, out_dtype=BF16)


def _ffn_in_swiglu(name, xn, w, *, tm=512):
    rows, k = xn.shape
    tm = min(tm, rows)
    nblk = FFN_HIDDEN // FFN_BLK

    def body(x_ref, w_ref, f_ref, g_ref):
        f = _bdot(x_ref[...], w_ref[...], _DIMS["nn"])
        f_ref[...] = f
        fa = f[:, 0:FFN_BLK]
        g_ref[...] = (fa * _sigmoid(fa) * f[:, FFN_BLK:2 * FFN_BLK]).astype(g_ref.dtype)

    return pl.pallas_call(
        body, name=name, grid=(nblk, rows // tm),
        in_specs=[pl.BlockSpec((tm, k), lambda j, i: (i, 0)), pl.BlockSpec((k, 2 * FFN_BLK), lambda j, i: (0, j))],
        out_specs=(pl.BlockSpec((tm, 2 * FFN_BLK), lambda j, i: (i, j)), pl.BlockSpec((tm, FFN_BLK), lambda j, i: (i, j))),
        out_shape=(jax.ShapeDtypeStruct((rows, 2 * FFN_HIDDEN), F32), jax.ShapeDtypeStruct((rows, FFN_HIDDEN), BF16)),
        compiler_params=_cparams(("parallel", "arbitrary")),
    )(xn, w)


def _d_ffn_out_swiglu(name, dh, w_out, f, *, tm=512):
    rows, d = dh.shape
    tm = min(tm, rows)
    nblk = FFN_HIDDEN // FFN_BLK

    def body(dh_ref, w_ref, f_ref, df_ref):
        dg = _bdot(dh_ref[...], w_ref[...], _DIMS["nt"])
        fa = f_ref[:, 0:FFN_BLK]
        fb = f_ref[:, FFN_BLK:2 * FFN_BLK]
        s = _sigmoid(fa)
        df_ref[:, 0:FFN_BLK] = (dg * fb * s * (1.0 + fa * (1.0 - s))).astype(df_ref.dtype)
        df_ref[:, FFN_BLK:2 * FFN_BLK] = (dg * fa * s).astype(df_ref.dtype)

    wide = pl.BlockSpec((tm, 2 * FFN_BLK), lambda j, i: (i, j))
    return pl.pallas_call(
        body, name=name, grid=(nblk, rows // tm),
        in_specs=[pl.BlockSpec((tm, d), lambda j, i: (i, 0)), pl.BlockSpec((FFN_BLK, d), lambda j, i: (j, 0)), wide],
        out_specs=wide, out_shape=jax.ShapeDtypeStruct((rows, 2 * FFN_HIDDEN), BF16),
        compiler_params=_cparams(("parallel", "arbitrary")),
    )(dh, w_out, f)


def _adamw(name, parts, w, m, v, *, tr=128):
    rows, cols = w.shape
    n_parts = parts.shape[0]
    tr = min(tr, rows)
    assert rows % tr == 0, (name, rows, tr)
    c1 = 1.0 - ADAM_B1 ** ADAM_STEP
    c2 = 1.0 - ADAM_B2 ** ADAM_STEP

    def body(p_ref, w_ref, m_ref, v_ref, g_ref, d_ref, nm_ref, nv_ref):
        g = p_ref[0].astype(F32)
        for s in range(1, n_parts):
            g = g + p_ref[s].astype(F32)
        m_new = ADAM_B1 * m_ref[...] + (1.0 - ADAM_B1) * g
        v_new = ADAM_B2 * v_ref[...] + (1.0 - ADAM_B2) * (g * g)
        upd = (m_new / c1) / (jnp.sqrt(v_new / c2) + ADAM_EPS) + ADAM_WD * w_ref[...]
        g_ref[...] = g
        d_ref[...] = -ADAM_LR * upd
        nm_ref[...] = m_new
        nv_ref[...] = v_new

    row = pl.BlockSpec((tr, cols), lambda i: (i, 0))
    out = jax.ShapeDtypeStruct((rows, cols), F32)
    return pl.pallas_call(
        body, name=name, grid=(rows // tr,),
        in_specs=[pl.BlockSpec((n_parts, tr, cols), lambda i: (0, i, 0)), row, row, row],
        out_specs=(row, row, row, row), out_shape=(out, out, out, out),
        compiler_params=_cparams(("parallel",)),
    )(parts, w, m, v)


_WEIGHTS = ("norm_mix", "w_in", "b_forget", "lam_re", "lam_im", "log_dt", "b_re", "b_im", "c_re", "c_im",
            "d_skip", "w_glu", "w_fox_o", "w_mix_out", "norm_mem_q", "norm_mem_kv", "w_mem_q", "w_mem_kv",
            "w_mem_o", "norm_ffn", "w_ffn_in", "w_ffn_out", "norm_final")
_SHARDED = ("w_in", "w_glu", "w_fox_o", "w_mix_out", "w_mem_q", "w_mem_kv", "w_mem_o", "w_ffn_in", "w_ffn_out")
_SMALL = tuple(n for n in _WEIGHTS if n not in _SHARDED)
_PACK_COLS = 1024


def _pack(arrays):
    flat = jnp.concatenate([a.reshape(-1).astype(F32) for a in arrays])
    rows = -(-flat.shape[0] // _PACK_COLS)
    return jnp.pad(flat, (0, rows * _PACK_COLS - flat.shape[0])).reshape(rows, _PACK_COLS)


def _unpack(buf, like):
    flat = buf.reshape(-1)
    out, pos = [], 0
    for a in like:
        out.append(flat[pos:pos + a.size].reshape(a.shape))
        pos += a.size
    return out


def _mm(name, a, b, mode, m, n, k, out_dtype, tm=1024, tn=512, tk=1024, **kw):
    return _matmul(name, a, b, mode, m, n, k, out_dtype=out_dtype, tm=tm, tn=tn, tk=tk, **kw)


def kernel(x, mem, norm_mix, w_in, b_forget, lam_re, lam_im, log_dt, b_re, b_im, c_re, c_im, d_skip, w_glu, w_fox_o, w_mix_out, norm_mem_q, norm_mem_kv, w_mem_q, w_mem_kv, w_mem_o, norm_ffn, w_ffn_in, w_ffn_out, norm_final, loss_target, m_norm_mix, m_w_in, m_b_forget, m_lam_re, m_lam_im, m_log_dt, m_b_re, m_b_im, m_c_re, m_c_im, m_d_skip, m_w_glu, m_w_fox_o, m_w_mix_out, m_norm_mem_q, m_norm_mem_kv, m_w_mem_q, m_w_mem_kv, m_w_mem_o, m_norm_ffn, m_w_ffn_in, m_w_ffn_out, m_norm_final, v_norm_mix, v_w_in, v_b_forget, v_lam_re, v_lam_im, v_log_dt, v_b_re, v_b_im, v_c_re, v_c_im, v_d_skip, v_w_glu, v_w_fox_o, v_w_mix_out, v_norm_mem_q, v_norm_mem_kv, v_w_mem_q, v_w_mem_kv, v_w_mem_o, v_norm_ffn, v_w_ffn_in, v_w_ffn_out, v_norm_final):
    given = dict(locals())
    weights = {n: given[n] for n in _WEIGHTS}
    mom_m = {n: given["m_" + n] for n in _WEIGHTS}
    mom_v = {n: given["v_" + n] for n in _WEIGHTS}
    seq = x.shape[1]
    nc = seq // SSM_CHUNK
    d = D_MODEL
    xs, mems, tgt = x[0], mem[0], loss_target[0]

    def padcols(a, width):
        return jnp.pad(a, ((0, 0), (0, width - a.shape[1])))

    shards = [padcols(w_in[0].astype(BF16), SHARD_IN_PAD), w_glu[0].astype(BF16), w_fox_o[0].astype(BF16),
              w_mix_out[0].astype(BF16), w_mem_q[0].astype(BF16), w_mem_kv[0].astype(BF16),
              w_mem_o[0].astype(BF16), padcols(w_ffn_in[0].astype(BF16), SHARD_FFN_PAD), w_ffn_out[0].astype(BF16)]
    gathered = _gather_all("gather_weights", shards)
    win = _assemble_win("assemble_w_in", gathered[0])
    wglu = _unstack_cols("unstack_w_glu", gathered[1])
    wfoxo = _unstack_cols("unstack_w_fox_o", gathered[2])
    wmo = _unstack_cols("unstack_w_mem_o", gathered[6])
    wmix = gathered[3].reshape(d, d)
    wmq = gathered[4].reshape(d, MEM_WIDTH)
    wmkv = gathered[5].reshape(d, 2 * MEM_WIDTH)
    wffn_in = _assemble_wffn("assemble_w_ffn_in", gathered[7])
    wffn_out = gathered[8].reshape(FFN_HIDDEN, d)

    u = _rms_fwd("rms_mix", xs, norm_mix)
    ussm = _mm("proj_ssm", u, win, "nn", seq, SSM_WIDTH, d, F32)
    qkv = _mm("proj_qkv", u, win, "nn", seq, 3 * FOX_WIDTH, d, BF16, tn=512, b_off=(0, SSM_WIDTH))
    gates = _mm("proj_gates", u, win, "nn", seq, 2 * d, d, F32, tn=1024, b_off=(0, PROJ_GATE0))
    fproj = _mm("proj_forget", u, win, "nn", seq, LANE, d, F32, tn=LANE, b_off=(0, PROJ_F0))

    ssm_params = (lam_re[0], lam_im[0], log_dt[0], b_re[0], b_im[0], c_re[0], c_im[0])
    (m_c, bw_c, cm_c, a8, aseg), mats_vjp = jax.vjp(lambda *p: _ssm_mats(*p, nc), *ssm_params)
    m_b = _bd_expand("ssm_expand_m", _BD_M, m_c)
    bw_b = _bd_expand("ssm_expand_bw", _BD_BW, bw_c)
    cm_b = _bd_expand("ssm_expand_cm", _BD_CM, cm_c)
    u8 = ussm.reshape(nc, SSM_CHUNK * SSM_WIDTH)
    d8 = jnp.tile(d_skip, (1, SSM_CHUNK))
    w4 = _ssm_w("ssm_w", u8, bw_b)
    sp4 = _ssm_scan("ssm_scan", w4, a8, aseg, reverse=False)
    y8 = _ssm_y("ssm_y", u8, sp4, m_b, cm_b)
    act = _ssm_post_fwd("ssm_act", y8, u8, d8).reshape(seq, SSM_WIDTH)
    glu = _mm("glu", act, wglu, "nn", seq, 2 * d, SSM_WIDTH, F32, tn=1024)

    bcol = jnp.pad(b_forget[0], (0, LANE - FOX_HEADS)).reshape(LANE, 1)
    cum_t = _fox_cum("fox_cum", fproj, bcol).reshape(FOX_HEADS // 2, 2, seq)
    att, lse = _fox_fwd("fox_fwd", qkv, cum_t)
    out_b = _mm("fox_out", att, wfoxo, "nn", seq, d, FOX_WIDTH, F32, tn=1024)

    mixin = _mix_fwd("mix", glu, gates, out_b)
    h1 = _mm("mix_out", mixin, wmix, "nn", seq, d, d, F32, tn=1024, add=xs)

    n1 = _rms_fwd("rms_mem_q", h1, norm_mem_q)
    q2 = _mm("mem_q", n1, wmq, "nn", seq, MEM_WIDTH, d, BF16)
    mn = _rms_fwd("rms_mem_kv", mems, norm_mem_kv)
    mlen = mems.shape[0]
    kv = _mm("mem_kv", mn, wmkv, "nn", mlen, 2 * MEM_WIDTH, d, BF16)
    o2 = _mem_fwd("mem_attn", q2, kv)
    h2 = _mm("mem_out", o2, wmo, "nn", seq, d, MEM_WIDTH, F32, tn=1024, add=h1)

    n2 = _rms_fwd("rms_ffn", h2, norm_ffn)
    f, g_act = _ffn_in_swiglu("ffn_in_swiglu", n2, wffn_in)
    h3 = _mm("ffn_out", g_act, wffn_out, "nn", seq, d, FFN_HIDDEN, F32, tk=FFN_HIDDEN, add=h2)
    loss_part, dh3, dg_final = _final_loss("final_loss", h3, tgt, norm_final.reshape(1, d))

    df = _d_ffn_out_swiglu("d_ffn_out_swiglu", dh3, wffn_out, f)
    dwffn_out = _mm("d_ffn_out_w", g_act, dh3, "tn", FFN_HIDDEN, d, seq, BF16, tm=1408, tn=1024)
    dn2 = _mm("d_ffn_in_x", df, wffn_in, "nt", seq, d, 2 * FFN_HIDDEN, F32, tn=1024, tk=FFN_HIDDEN)
    dwffn_in = _mm("d_ffn_in_w", n2, df, "tn", d, 2 * FFN_HIDDEN, seq, BF16, tn=1408)
    dh2, dg_ffn = _rms_bwd("d_rms_ffn", dn2, h2, norm_ffn, res=dh3)

    do2 = _mm("d_mem_out_x", dh2, wmo, "nt", seq, MEM_WIDTH, d, F32)
    dwmo = _restack_cols("restack_d_w_mem_o", _mm("d_mem_out_w", o2, dh2, "tn", MEM_WIDTH, d, seq, BF16, tn=1024))
    dq2, dkv = _mem_bwd("d_mem_attn", q2, kv, do2)
    dwmq = _mm("d_mem_q_w", n1, dq2, "tn", d, MEM_WIDTH, seq, BF16)
    dn1 = _mm("d_mem_q_x", dq2, wmq, "nt", seq, d, MEM_WIDTH, F32)
    dwmkv = _mm("d_mem_kv_w", mn, dkv, "tn", d, 2 * MEM_WIDTH, mlen, BF16, tn=1024)
    dmn = _mm("d_mem_kv_x", dkv, wmkv, "nt", mlen, d, 2 * MEM_WIDTH, F32)
    _, dg_memkv = _rms_bwd("d_rms_mem_kv", dmn, mems, norm_mem_kv)
    dh1, dg_memq = _rms_bwd("d_rms_mem_q", dn1, h1, norm_mem_q, res=dh2)

    dmixin = _mm("d_mix_out_x", dh1, wmix, "nt", seq, d, d, F32, tn=1024)
    dwmix = _mm("d_mix_out_w", mixin, dh1, "tn", d, d, seq, BF16, tn=1024)
    dglu, dgates, dout_b = _mix_bwd("d_mix", dmixin, glu, gates, out_b)
    datt = _mm("d_fox_out_x", dout_b, wfoxo, "nt", seq, FOX_WIDTH, d, F32)
    dwfoxo = _restack_cols("restack_d_w_fox_o", _mm("d_fox_out_w", att, dout_b, "tn", FOX_WIDTH, d, seq, BF16, tn=1024))
    dact = _mm("d_glu_x", dglu, wglu, "nt", seq, SSM_WIDTH, 2 * d, F32, tk=2 * d)
    dwglu = _restack_cols("restack_d_w_glu", _mm("d_glu_w", act, dglu, "tn", SSM_WIDTH, 2 * d, seq, BF16, tn=2 * d))

    dz8, dg_dskip = _ssm_post_bwd("d_ssm_act", dact.reshape(nc, SSM_CHUNK * SSM_WIDTH), y8, u8, d8)
    ds4, dcm = _ssm_ds("d_ssm_y_state", dz8, sp4, cm_b)
    g4, da8 = _ssm_scan("d_ssm_scan", ds4, a8, aseg, reverse=True, sprev4=sp4)
    dx8, dm, dbw = _ssm_dx("d_ssm_x", dz8, g4, u8, m_b, bw_b, d8)
    dussm = dx8.reshape(seq, SSM_WIDTH)
    g_ssm = mats_vjp((_bd_reduce("ssm_reduce_dm", _BD_M, dm), _bd_reduce("ssm_reduce_dbw", _BD_BW, dbw),
                      _bd_reduce("ssm_reduce_dcm", _BD_CM, dcm), da8, jnp.zeros_like(aseg)))

    dq, dk, dv, dcum = _fox_bwd("d_fox", qkv, cum_t, att, datt, lse)
    dfproj, dbf = _fox_cum_bwd("d_fox_cum", dcum.reshape(FOX_HEADS, seq), fproj, bcol)
    dg_bforget = dbf[0:FOX_HEADS, 0].reshape(1, FOX_HEADS)

    dproj = _concat_cols("d_proj_concat", (dussm, dq, dk, dv, dgates, dfproj))
    du = _mm("d_proj_x", dproj, win, "nt", seq, d, PROJ_WIDTH, F32, tn=1024, tk=1408)
    dwin = _mm("d_proj_w", u, dproj, "tn", d, PROJ_WIDTH, seq, BF16, tn=1408)
    dx, dg_mix = _rms_bwd("d_rms_mix", du, xs, norm_mix, res=dh1)

    pieces = [_disassemble_dwin("split_d_w_in", dwin), dwglu, dwfoxo,
              dwmix.reshape(N_DEV, d // N_DEV, d), dwmq.reshape(N_DEV, d // N_DEV, MEM_WIDTH),
              dwmkv.reshape(N_DEV, d // N_DEV, 2 * MEM_WIDTH), dwmo,
              _disassemble_dwffn("split_d_w_ffn_in", dwffn_in),
              dwffn_out.reshape(N_DEV, FFN_HIDDEN // N_DEV, d)]
    sib = _scatter_sibling("scatter_grads_sibling", pieces)
    received = _scatter_chips("scatter_grads_chips", _add_chip_partials("sum_grads_chip", pieces, sib))

    small_grads = dict(zip(
        _SMALL, (dg_mix, dg_bforget, g_ssm[0][None], g_ssm[1][None], g_ssm[2][None], g_ssm[3][None], g_ssm[4][None],
                 g_ssm[5][None], g_ssm[6][None], dg_dskip, dg_memq, dg_memkv, dg_ffn, dg_final.reshape(d))))
    small_like = [weights[n] for n in _SMALL]
    small_all = _gather_all("gather_small_grads", [_pack([small_grads[n] for n in _SMALL])])[0]
    pk = [_pack([src[n] for n in _SMALL]) for src in (weights, mom_m, mom_v)]
    small_out = _adamw("adamw_small", small_all, pk[0], pk[1], pk[2], tr=small_all.shape[1])
    small_res = [dict(zip(_SMALL, _unpack(buf, small_like))) for buf in small_out]

    results = [dict(r) for r in small_res]
    tiles = {"w_in": 128, "w_glu": 128, "w_fox_o": 128, "w_mix_out": 128, "w_mem_q": 128, "w_mem_kv": 128,
             "w_mem_o": 128, "w_ffn_in": 128, "w_ffn_out": 176}
    pads = {"w_in": SHARD_IN_PAD, "w_ffn_in": SHARD_FFN_PAD}
    for name, parts in zip(_SHARDED, received):
        w2, m2, v2 = weights[name][0], mom_m[name][0], mom_v[name][0]
        cols = w2.shape[1]
        if name in pads:
            w2, m2, v2 = (padcols(t, pads[name]) for t in (w2, m2, v2))
        outs = _adamw("adamw_" + name, parts, w2, m2, v2, tr=tiles[name])
        for res, o in zip(results, outs):
            res[name] = o[:, :cols][None]

    loss = lax.psum(loss_part[0, 0], ("x", "y", "c"))
    out = [loss, dx[None]]
    for res in results:
        out.extend(res[n] for n in _WEIGHTS)
    return tuple(out)
```

```python
import math

import jax
import jax.numpy as jnp
import numpy as np
from jax import lax
from jax.experimental import pallas as pl
from jax.experimental.pallas import tpu as pltpu

F32 = jnp.float32
BF16 = jnp.bfloat16

N_DEV = 8
LANE = 128
VMEM_LIMIT = 56 * 1024 * 1024

D_MODEL = 1024
SSM_GROUP = 16
SSM_GROUPS = 32
SSM_WIDTH = 512
SSM_STATE = 64
SSM_CHUNK = 8
FOX_HEADS = 8
FOX_HEAD_DIM = 64
FOX_WIDTH = 512
MEM_HEADS = 4
MEM_HEAD_DIM = 128
MEM_WIDTH = 512
FFN_HIDDEN = 2816
RMS_EPS = 1e-6
IN_WIDTH = 4104
SHARD_IN = IN_WIDTH // N_DEV
SHARD_IN_PAD = 640
SHARD_FFN = 2 * FFN_HIDDEN // N_DEV
SHARD_FFN_PAD = 768
PROJ_GATE0 = 2048
PROJ_F0 = 4096
PROJ_WIDTH = 4224

ADAM_LR = 0.001
ADAM_B1 = 0.9
ADAM_B2 = 0.999
ADAM_EPS = 1e-08
ADAM_WD = 0.01
ADAM_STEP = 10


def _cparams(sem=None):
    return pltpu.CompilerParams(dimension_semantics=sem, vmem_limit_bytes=VMEM_LIMIT)


def _sigmoid(x):
    return 1.0 / (1.0 + jnp.exp(-x))


def _bdot(a, b, dims):
    return lax.dot_general(a.astype(BF16), b.astype(BF16), ((dims[0], dims[1]), ((), ())),
                           preferred_element_type=F32)


_DIMS = {"nn": ((1,), (0,)), "nt": ((1,), (1,)), "tn": ((0,), (0,))}


def _matmul(name, a, b, mode, m, n, k, *, out_dtype, tm, tn, tk, a_off=(0, 0), b_off=(0, 0), add=None):
    tm, tn, tk = min(tm, m), min(tn, n), min(tk, k)
    assert m % tm == 0 and n % tn == 0 and k % tk == 0, (name, m, n, k, tm, tn, tk)
    nk = k // tk
    grid = (m // tm, n // tn, nk)

    def blk(off, t):
        assert off % t == 0, (name, off, t)
        return off // t

    if mode in ("nn", "nt"):
        ar, ac = blk(a_off[0], tm), blk(a_off[1], tk)
        a_spec = pl.BlockSpec((tm, tk), lambda i, j, kk: (i + ar, kk + ac))
    else:
        ar, ac = blk(a_off[0], tk), blk(a_off[1], tm)
        a_spec = pl.BlockSpec((tk, tm), lambda i, j, kk: (kk + ar, i + ac))

    if mode in ("nn", "tn"):
        br, bc = blk(b_off[0], tk), blk(b_off[1], tn)
        b_spec = pl.BlockSpec((tk, tn), lambda i, j, kk: (kk + br, j + bc))
    else:
        br, bc = blk(b_off[0], tn), blk(b_off[1], tk)
        b_spec = pl.BlockSpec((tn, tk), lambda i, j, kk: (j + br, kk + bc))
    o_spec = pl.BlockSpec((tm, tn), lambda i, j, kk: (i, j))
    out_shape = jax.ShapeDtypeStruct((m, n), out_dtype)

    in_specs = [a_spec, b_spec]
    operands = [a, b]
    if add is not None:
        in_specs.append(pl.BlockSpec((tm, tn), lambda i, j, kk: (i, j)))
        operands.append(add)
    dims = _DIMS[mode]
    has_add = add is not None

    def body(*refs):
        a_ref, b_ref = refs[0], refs[1]
        add_ref = refs[2] if has_add else None
        o_ref = refs[3] if has_add else refs[2]
        acc_ref = refs[-1] if nk > 1 else None
        prod = _bdot(a_ref[...], b_ref[...], dims)

        def finish(total):
            if has_add:
                total = total + add_ref[...].astype(F32)
            o_ref[...] = total.astype(o_ref.dtype)

        if nk == 1:
            finish(prod)
        else:
            kk = pl.program_id(2)

            @pl.when(kk == 0)
            def _():
                acc_ref[...] = prod

            @pl.when(jnp.logical_and(kk > 0, kk < nk - 1))
            def _():
                acc_ref[...] += prod

            @pl.when(kk == nk - 1)
            def _():
                finish(acc_ref[...] + prod)

    scratch = [pltpu.VMEM((tm, tn), F32)] if nk > 1 else []
    return pl.pallas_call(
        body, name=name, grid=grid, in_specs=in_specs, out_specs=o_spec, out_shape=out_shape,
        scratch_shapes=scratch,
        compiler_params=_cparams(("parallel", "parallel", "arbitrary")),
    )(*operands)


def _rms_fwd(name, x, gain, *, tr=512, after=None):
    r, d = x.shape
    tr = min(tr, r)

    def body(x_ref, g_ref, *rest):
        o_ref = rest[-1]
        xv = x_ref[...]
        rstd = lax.rsqrt(jnp.mean(xv * xv, axis=-1, keepdims=True) + RMS_EPS)
        o_ref[...] = (xv * rstd * g_ref[...]).astype(o_ref.dtype)

    in_specs = [pl.BlockSpec((tr, d), lambda i: (i, 0)), pl.BlockSpec((1, d), lambda i: (0, 0))]
    ops = [x, gain]
    if after is not None:
        in_specs.append(pl.BlockSpec(after.shape, lambda i: (0, 0)))
        ops.append(after)
    return pl.pallas_call(
        body, name=name, grid=(r // tr,), in_specs=in_specs,
        out_specs=pl.BlockSpec((tr, d), lambda i: (i, 0)),
        out_shape=jax.ShapeDtypeStruct((r, d), BF16),
        compiler_params=_cparams(("parallel",)),
    )(*ops)


def _rms_bwd(name, dy, x, gain, res=None, *, tr=512, after=None):
    r, d = x.shape
    tr = min(tr, r)
    n = r // tr
    has_res = res is not None

    def body(*refs):
        dy_ref, x_ref, g_ref = refs[:3]
        res_ref = refs[3] if has_res else None
        dx_ref, dg_ref, acc_ref = refs[-3:]
        i = pl.program_id(0)
        xv = x_ref[...]
        rstd = lax.rsqrt(jnp.mean(xv * xv, axis=-1, keepdims=True) + RMS_EPS)
        xh = xv * rstd
        dyv = dy_ref[...].astype(F32)
        dxh = dyv * g_ref[...]
        dx = rstd * (dxh - xh * jnp.mean(dxh * xh, axis=-1, keepdims=True))
        if has_res:
            dx = dx + res_ref[...]
        dx_ref[...] = dx
        part = (dyv * xh).reshape(tr // 8, 8, d).sum(axis=0)

        @pl.when(i == 0)
        def _():
            acc_ref[...] = part

        @pl.when(i > 0)
        def _():
            acc_ref[...] += part

        @pl.when(i == n - 1)
        def _():
            dg_ref[...] = jnp.sum(acc_ref[...], axis=0, keepdims=True)

    row = pl.BlockSpec((tr, d), lambda i: (i, 0))
    in_specs = [row, row, pl.BlockSpec((1, d), lambda i: (0, 0))] + ([row] if has_res else [])
    ops = [dy, x, gain] + ([res] if has_res else [])
    if after is not None:
        in_specs.append(pl.BlockSpec(after.shape, lambda i: (0, 0)))
        ops.append(after)
    return pl.pallas_call(
        body, name=name, grid=(n,), in_specs=in_specs,
        out_specs=(row, pl.BlockSpec((1, d), lambda i: (0, 0))),
        out_shape=(jax.ShapeDtypeStruct((r, d), F32), jax.ShapeDtypeStruct((1, d), F32)),
        scratch_shapes=[pltpu.VMEM((8, d), F32)],
        compiler_params=_cparams(("arbitrary",)),
    )(*ops)


def _final_loss(name, h, target, gain, *, tr=512):
    r, d = h.shape
    tr = min(tr, r)
    n = r // tr

    def body(h_ref, t_ref, g_ref, loss_ref, dh_ref, dg_ref, accl_ref, accg_ref):
        i = pl.program_id(0)
        xv = h_ref[...]
        rstd = lax.rsqrt(jnp.mean(xv * xv, axis=-1, keepdims=True) + RMS_EPS)
        xh = xv * rstd
        e = xh * g_ref[...] - t_ref[...]
        dyv = e * (1.0 / d)
        dxh = dyv * g_ref[...]
        dh_ref[...] = rstd * (dxh - xh * jnp.mean(dxh * xh, axis=-1, keepdims=True))
        lpart = (e * e).reshape(tr // 8, 8, d).sum(axis=0)
        gpart = (dyv * xh).reshape(tr // 8, 8, d).sum(axis=0)

        @pl.when(i == 0)
        def _():
            accl_ref[...] = lpart
            accg_ref[...] = gpart

        @pl.when(i > 0)
        def _():
            accl_ref[...] += lpart
            accg_ref[...] += gpart

        @pl.when(i == n - 1)
        def _():
            tot = jnp.sum(jnp.sum(accl_ref[...], axis=0, keepdims=True), axis=1, keepdims=True)
            loss_ref[...] = jnp.broadcast_to(tot * (0.5 / d), (1, LANE))
            dg_ref[...] = jnp.sum(accg_ref[...], axis=0, keepdims=True)

    row = pl.BlockSpec((tr, d), lambda i: (i, 0))
    one = pl.BlockSpec((1, d), lambda i: (0, 0))
    return pl.pallas_call(
        body, name=name, grid=(n,), in_specs=[row, row, one],
        out_specs=(pl.BlockSpec((1, LANE), lambda i: (0, 0)), row, one),
        out_shape=(jax.ShapeDtypeStruct((1, LANE), F32), jax.ShapeDtypeStruct((r, d), F32),
                   jax.ShapeDtypeStruct((1, d), F32)),
        scratch_shapes=[pltpu.VMEM((8, d), F32), pltpu.VMEM((8, d), F32)],
        compiler_params=_cparams(("arbitrary",)),
    )(h, target, gain)


_GELU_C = math.sqrt(2.0 / math.pi)


def _gelu_parts(z):
    inner = _GELU_C * (z + 0.044715 * z * z * z)
    t = jnp.tanh(inner)
    val = 0.5 * z * (1.0 + t)
    dinner = _GELU_C * (1.0 + 3.0 * 0.044715 * z * z)
    grad = 0.5 * (1.0 + t) + 0.5 * z * (1.0 - t * t) * dinner
    return val, grad


def _ssm_post_fwd(name, y8, u8, d8, *, tr=256):
    r, c = y8.shape
    tr = min(tr, r)

    def body(y_ref, u_ref, d_ref, o_ref):
        z = y_ref[...] + d_ref[...] * u_ref[...]
        o_ref[...] = _gelu_parts(z)[0].astype(o_ref.dtype)

    row = pl.BlockSpec((tr, c), lambda i: (i, 0))
    return pl.pallas_call(
        body, name=name, grid=(r // tr,), in_specs=[row, row, pl.BlockSpec((1, c), lambda i: (0, 0))],
        out_specs=row, out_shape=jax.ShapeDtypeStruct((r, c), BF16),
        compiler_params=_cparams(("parallel",)),
    )(y8, u8, d8)


def _ssm_post_bwd(name, dact8, y8, u8, d8, *, tr=256):
    r, c = y8.shape
    tr = min(tr, r)
    n = r // tr

    def body(da_ref, y_ref, u_ref, d_ref, dz_ref, dd_ref, acc_ref):
        i = pl.program_id(0)
        uv = u_ref[...]
        z = y_ref[...] + d_ref[...] * uv
        dz = da_ref[...].astype(F32) * _gelu_parts(z)[1]
        dz_ref[...] = dz
        part = (dz * uv).reshape(tr // 8, 8, c).sum(axis=0)

        @pl.when(i == 0)
        def _():
            acc_ref[...] = part

        @pl.when(i > 0)
        def _():
            acc_ref[...] += part

        @pl.when(i == n - 1)
        def _():
            tot = jnp.sum(acc_ref[...], axis=0, keepdims=True)
            out = tot[:, 0:SSM_WIDTH]
            for j in range(1, c // SSM_WIDTH):
                out = out + tot[:, j * SSM_WIDTH:(j + 1) * SSM_WIDTH]
            dd_ref[...] = out

    row = pl.BlockSpec((tr, c), lambda i: (i, 0))
    return pl.pallas_call(
        body, name=name, grid=(n,), in_specs=[row, row, row, pl.BlockSpec((1, c), lambda i: (0, 0))],
        out_specs=(row, pl.BlockSpec((1, SSM_WIDTH), lambda i: (0, 0))),
        out_shape=(jax.ShapeDtypeStruct((r, c), F32), jax.ShapeDtypeStruct((1, SSM_WIDTH), F32)),
        scratch_shapes=[pltpu.VMEM((8, c), F32)],
        compiler_params=_cparams(("arbitrary",)),
    )(dact8, y8, u8, d8)


def _mix_fwd(name, glu, gates, out_b, *, tr=256):
    r = glu.shape[0]
    d = D_MODEL
    tr = min(tr, r)

    def body(glu_ref, gate_ref, ob_ref, o_ref):
        out_a = glu_ref[:, 0:d] * _sigmoid(glu_ref[:, d:2 * d])
        mix = _sigmoid(gate_ref[:, 0:d]) * out_a + _sigmoid(gate_ref[:, d:2 * d]) * ob_ref[...]
        o_ref[...] = mix.astype(o_ref.dtype)

    wide = pl.BlockSpec((tr, 2 * d), lambda i: (i, 0))
    row = pl.BlockSpec((tr, d), lambda i: (i, 0))
    return pl.pallas_call(
        body, name=name, grid=(r // tr,), in_specs=[wide, wide, row], out_specs=row,
        out_shape=jax.ShapeDtypeStruct((r, d), BF16), compiler_params=_cparams(("parallel",)),
    )(glu, gates, out_b)


def _mix_bwd(name, dmix, glu, gates, out_b, *, tr=256):
    r = glu.shape[0]
    d = D_MODEL
    tr = min(tr, r)

    def body(dm_ref, glu_ref, gate_ref, ob_ref, dglu_ref, dgate_ref, dob_ref):
        dm = dm_ref[...]
        glu_a = glu_ref[:, 0:d]
        sb = _sigmoid(glu_ref[:, d:2 * d])
        ga = _sigmoid(gate_ref[:, 0:d])
        gb = _sigmoid(gate_ref[:, d:2 * d])
        out_a = glu_a * sb
        dout_a = dm * ga
        dglu_ref[:, 0:d] = (dout_a * sb).astype(dglu_ref.dtype)
        dglu_ref[:, d:2 * d] = (dout_a * glu_a * sb * (1.0 - sb)).astype(dglu_ref.dtype)
        dgate_ref[:, 0:d] = (dm * out_a * ga * (1.0 - ga)).astype(dgate_ref.dtype)
        dgate_ref[:, d:2 * d] = (dm * ob_ref[...] * gb * (1.0 - gb)).astype(dgate_ref.dtype)
        dob_ref[...] = (dm * gb).astype(dob_ref.dtype)

    wide = pl.BlockSpec((tr, 2 * d), lambda i: (i, 0))
    row = pl.BlockSpec((tr, d), lambda i: (i, 0))
    return pl.pallas_call(
        body, name=name, grid=(r // tr,), in_specs=[row, wide, wide, row], out_specs=(wide, wide, row),
        out_shape=(jax.ShapeDtypeStruct((r, 2 * d), BF16), jax.ShapeDtypeStruct((r, 2 * d), BF16),
                   jax.ShapeDtypeStruct((r, d), BF16)),
        compiler_params=_cparams(("parallel",)),
    )(dmix, glu, gates, out_b)


def _ssm_mats(lam_re, lam_im, log_dt, b_re, b_im, c_re, c_im, nc):
    hp = lax.Precision.HIGHEST
    t = SSM_CHUNK
    nq = SSM_GROUPS // 8
    lam = lax.complex(lam_re, lam_im)
    z = lam * jnp.exp(log_dt)[:, None]
    ks = jnp.arange(t + 1, dtype=F32)
    apow = jnp.exp(ks[:, None, None] * z[None])
    bbar = ((apow[1] - 1.0) / lam)[..., None] * lax.complex(b_re, b_im)
    c = lax.complex(c_re, c_im)

    ca = c[None] * apow[:, :, None, :]
    kmat = jnp.einsum("kgnp,gpm->kgnm", ca, bbar, precision=hp).real
    ii = np.arange(t)
    lag = ii[None, :] - ii[:, None]
    kt = kmat[np.clip(lag, 0, t)] * jnp.asarray(lag >= 0, F32)[:, :, None, None, None]
    kt = kt.reshape(t, t, nq, 8, SSM_GROUP, SSM_GROUP)
    m_c = kt.transpose(2, 0, 3, 5, 1, 4).reshape(nq, 1024, LANE)

    arev = jnp.exp((float(t - 1) - ks[:t])[:, None, None] * z[None])
    w = arev[:, :, :, None] * bbar[None]
    wr = jnp.stack([w.real, w.imag]).reshape(2, t, nq, 8, SSM_STATE, SSM_GROUP)
    bw_c = wr.transpose(2, 1, 3, 5, 0, 4).reshape(nq, 1024, LANE)

    ca1 = ca[1:]
    cr = jnp.stack([ca1.real, -ca1.imag]).reshape(2, t, nq, 8, SSM_GROUP, SSM_STATE)
    cm_c = cr.transpose(2, 0, 3, 5, 1, 4).reshape(nq, 1024, LANE)

    def tiles(v):
        vq = jnp.concatenate([v.real.reshape(nq, 512), v.imag.reshape(nq, 512)], axis=1)
        return jnp.broadcast_to(vq.reshape(nq, 8, 1, LANE), (nq, 8, 8, LANE))

    return m_c, bw_c, cm_c, tiles(apow[t]), tiles(jnp.exp(float(nc) * z))


_BD_M = (LANE, SSM_GROUP)
_BD_BW = (LANE, SSM_STATE)
_BD_CM = (512, SSM_GROUP)


def _bd_perm(cn):
    rr = lax.broadcasted_iota(jnp.int32, (1024, 1024), 0)
    cc = lax.broadcasted_iota(jnp.int32, (1024, 1024), 1)
    sh = cn.bit_length() - 1
    src = ((rr >> 7) << sh) + (((rr & (LANE - 1)) >> sh) << (3 + sh)) + (rr & (cn - 1))
    return jnp.where(src == cc, 1.0, 0.0).astype(BF16)


def _bd_rowgroup(span):
    r = lax.broadcasted_iota(jnp.int32, (1024, LANE), 0)
    return (r & (span - 1)) >> ((span // 8).bit_length() - 1)


def _bd_expand(name, kind, compact):
    span, cn = kind
    nq = compact.shape[0]

    def body(c_ref, o_ref):
        x = c_ref[...]
        grp = _bd_rowgroup(span)
        xcat = jnp.concatenate([jnp.where(grp == h, x, 0.0) for h in range(8)], axis=1)
        o_ref[...] = _bdot(xcat, _bd_perm(cn), _DIMS["nn"]).astype(o_ref.dtype)

    return pl.pallas_call(
        body, name=name, grid=(nq,), in_specs=[pl.BlockSpec((None, 1024, LANE), lambda q: (q, 0, 0))],
        out_specs=pl.BlockSpec((None, 1024, 1024), lambda q: (q, 0, 0)),
        out_shape=jax.ShapeDtypeStruct((nq, 1024, 1024), BF16),
        compiler_params=_cparams(("parallel",)),
    )(compact)


def _bd_reduce(name, kind, dbig):
    span, cn = kind
    nq = dbig.shape[0]

    def body(g_ref, o_ref):
        perm = _bd_perm(cn)
        hi, mid, lo = _split3(g_ref[...])
        d = _DIMS["nt"]
        back = _bdot(hi, perm, d) + _bdot(mid, perm, d) + _bdot(lo, perm, d)
        grp = _bd_rowgroup(span)
        out = jnp.zeros((1024, LANE), F32)
        for h in range(8):
            out = jnp.where(grp == h, back[:, h * LANE:(h + 1) * LANE], out)
        o_ref[...] = out

    return pl.pallas_call(
        body, name=name, grid=(nq,), in_specs=[pl.BlockSpec((None, 1024, 1024), lambda q: (q, 0, 0))],
        out_specs=pl.BlockSpec((None, 1024, LANE), lambda q: (q, 0, 0)),
        out_shape=jax.ShapeDtypeStruct((nq, 1024, LANE), F32),
        compiler_params=_cparams(("parallel",)),
    )(dbig)


def _x_tile_specs(nc, nq):
    return [pl.BlockSpec((nc, LANE), lambda q, t, i=i: (0, i * nq + q)) for i in range(SSM_CHUNK)]


def _cat_tiles(refs):
    return jnp.concatenate([r[...] for r in refs], axis=1)


def _ssm_w(name, x8, bw):
    nc = x8.shape[0]
    nq = bw.shape[0]

    def body(*refs):
        xq = _cat_tiles(refs[:8])
        refs[9][...] = _bdot(xq, refs[8][...], _DIMS["nn"])

    return pl.pallas_call(
        body, name=name, grid=(nq, 8),
        in_specs=_x_tile_specs(nc, nq) + [pl.BlockSpec((None, 1024, LANE), lambda q, t: (q, 0, t))],
        out_specs=pl.BlockSpec((None, None, nc, LANE), lambda q, t: (q, t, 0, 0)),
        out_shape=jax.ShapeDtypeStruct((nq, 8, nc, LANE), F32),
        compiler_params=_cparams(("parallel", "arbitrary")),
    )(*([x8] * 8), bw)


def _ssm_scan(name, w4, a_t, aseg_t, *, reverse, sprev4=None):
    nq, _, nc, _ = w4.shape
    ns = nc // 8
    with_da = sprev4 is not None

    def body(*refs):
        w_ref, a_ref, aseg_ref = refs[:3]
        s_ref = refs[3] if with_da else None
        o_ref = refs[4] if with_da else refs[3]
        da_ref = refs[5] if with_da else None
        sgn = -1.0 if reverse else 1.0
        ar = [a_ref[j] for j in range(4)]
        ai = [sgn * a_ref[j + 4] for j in range(4)]
        gr = [aseg_ref[j] for j in range(4)]
        gi = [sgn * aseg_ref[j + 4] for j in range(4)]
        zero = tuple(jnp.zeros((8, LANE), F32) for _ in range(8))

        def rows(tt):
            return pl.ds((ns - 1 - tt) if reverse else tt, 8, stride=ns)

        def step(carry, w):
            new_r = [ar[j] * carry[j] - ai[j] * carry[j + 4] + w[j] for j in range(4)]
            new_i = [ar[j] * carry[j + 4] + ai[j] * carry[j] + w[j + 4] for j in range(4)]
            return tuple(new_r + new_i)

        def pass1(tt, carry):
            return step(carry, [w_ref[j, rows(tt), :] for j in range(8)])

        ends = lax.fori_loop(0, ns, pass1, zero)
        sub = lax.broadcasted_iota(jnp.int32, (8, LANE), 0)
        init = list(zero)
        order = range(7, 0, -1) if reverse else range(0, 7)
        for s in order:
            nxt = s - 1 if reverse else s + 1
            cand_r = [gr[j] * init[j] - gi[j] * init[j + 4] + ends[j] for j in range(4)]
            cand_i = [gr[j] * init[j + 4] + gi[j] * init[j] + ends[j + 4] for j in range(4)]
            cand = cand_r + cand_i
            shift = 7 if reverse else 1
            init = [jnp.where(sub == nxt, pltpu.roll(cand[j], shift, axis=0), init[j]) for j in range(8)]

        def pass2(tt, state):
            carry, acc = state
            r = rows(tt)
            for j in range(8):
                o_ref[j, r, :] = carry[j]
            if with_da:
                sp = [s_ref[j, r, :] for j in range(8)]
                acc_r = [acc[j] + carry[j] * sp[j] + carry[j + 4] * sp[j + 4] for j in range(4)]
                acc_i = [acc[j + 4] + carry[j + 4] * sp[j] - carry[j] * sp[j + 4] for j in range(4)]
                acc = tuple(acc_r + acc_i)
            return step(carry, [w_ref[j, r, :] for j in range(8)]), acc

        _, acc = lax.fori_loop(0, ns, pass2, (tuple(init), zero))
        if with_da:
            for j in range(8):
                da_ref[j] = acc[j]

    big = pl.BlockSpec((None, 8, nc, LANE), lambda q: (q, 0, 0, 0))
    small = pl.BlockSpec((None, 8, 8, LANE), lambda q: (q, 0, 0, 0))
    in_specs = [big, small, small] + ([big] if with_da else [])
    ops = [w4, a_t, aseg_t] + ([sprev4] if with_da else [])
    out_specs = (big, small) if with_da else big
    big_s = jax.ShapeDtypeStruct((nq, 8, nc, LANE), F32)
    out_shape = (big_s, jax.ShapeDtypeStruct((nq, 8, 8, LANE), F32)) if with_da else big_s
    return pl.pallas_call(
        body, name=name, grid=(nq,), in_specs=in_specs, out_specs=out_specs, out_shape=out_shape,
        compiler_params=_cparams(("parallel",)),
    )(*ops)


def _ssm_y(name, x8, sprev4, m_mat, cm_mat):
    nc = x8.shape[0]
    nq = m_mat.shape[0]

    def body(*refs):
        xq = _cat_tiles(refs[:8])
        s_ref, m_ref, cm_ref, o_ref = refs[8:12]
        sq = jnp.concatenate([s_ref[t] for t in range(8)], axis=1)
        o_ref[...] = _bdot(xq, m_ref[...], _DIMS["nn"]) + _bdot(sq, cm_ref[...], _DIMS["nn"])

    col = pl.BlockSpec((None, 1024, LANE), lambda q, j: (q, 0, j))
    return pl.pallas_call(
        body, name=name, grid=(nq, 8),
        in_specs=_x_tile_specs(nc, nq) + [pl.BlockSpec((None, 8, nc, LANE), lambda q, j: (q, 0, 0, 0)), col, col],
        out_specs=pl.BlockSpec((nc, LANE), lambda q, j: (0, j * nq + q)),
        out_shape=jax.ShapeDtypeStruct((nc, 8 * SSM_WIDTH), F32),
        compiler_params=_cparams(("parallel", "arbitrary")),
    )(*([x8] * 8), sprev4, m_mat, cm_mat)


def _ssm_ds(name, dz8, sprev4, cm_mat):
    nc = dz8.shape[0]
    nq = cm_mat.shape[0]

    def body(*refs):
        dyq = _cat_tiles(refs[:8]).astype(BF16)
        s_ref, cm_ref, ds_ref, dcm_ref = refs[8:12]
        ds_ref[...] = _bdot(dyq, cm_ref[...], _DIMS["nt"])
        dcm_ref[...] = _bdot(s_ref[...], dyq, _DIMS["tn"])

    tile = pl.BlockSpec((None, None, nc, LANE), lambda q, t: (q, t, 0, 0))
    rowblk = pl.BlockSpec((None, LANE, 1024), lambda q, t: (q, t, 0))
    return pl.pallas_call(
        body, name=name, grid=(nq, 8),
        in_specs=_x_tile_specs(nc, nq) + [tile, rowblk],
        out_specs=(tile, rowblk),
        out_shape=(jax.ShapeDtypeStruct((nq, 8, nc, LANE), F32), jax.ShapeDtypeStruct((nq, 1024, 1024), F32)),
        compiler_params=_cparams(("parallel", "arbitrary")),
    )(*([dz8] * 8), sprev4, cm_mat)


def _ssm_dx(name, dz8, g4, x8, m_mat, bw_mat, d8):
    nc = dz8.shape[0]
    nq = m_mat.shape[0]

    def body(*refs):
        dyq = _cat_tiles(refs[:8]).astype(BF16)
        g_ref, x_ref, m_ref, bw_ref, d_ref, dzi_ref, dx_ref, dm_ref, dbw_ref = refs[8:17]
        gq = jnp.concatenate([g_ref[t] for t in range(8)], axis=1).astype(BF16)
        dx = _bdot(dyq, m_ref[...], _DIMS["nt"]) + _bdot(gq, bw_ref[...], _DIMS["nt"])
        dx_ref[...] = (dx + d_ref[...] * dzi_ref[...]).astype(dx_ref.dtype)
        xi = x_ref[...]
        dm_ref[...] = _bdot(xi, dyq, _DIMS["tn"])
        dbw_ref[...] = _bdot(xi, gq, _DIMS["tn"])

    xtile = pl.BlockSpec((nc, LANE), lambda q, i: (0, i * nq + q))
    rowblk = pl.BlockSpec((None, LANE, 1024), lambda q, i: (q, i, 0))
    return pl.pallas_call(
        body, name=name, grid=(nq, 8),
        in_specs=_x_tile_specs(nc, nq) + [pl.BlockSpec((None, 8, nc, LANE), lambda q, i: (q, 0, 0, 0)), xtile, rowblk, rowblk,
                                          pl.BlockSpec((1, LANE), lambda q, i: (0, q)), xtile],
        out_specs=(xtile, rowblk, rowblk),
        out_shape=(jax.ShapeDtypeStruct((nc, 8 * SSM_WIDTH), BF16), jax.ShapeDtypeStruct((nq, 1024, 1024), F32),
                   jax.ShapeDtypeStruct((nq, 1024, 1024), F32)),
        compiler_params=_cparams(("parallel", "arbitrary")),
    )(*([dz8] * 8), g4, x8, m_mat, bw_mat, d8, dz8)


CUM_BLK = 256


def _split3(x):
    hi = x.astype(BF16)
    r1 = x - hi.astype(F32)
    mid = r1.astype(BF16)
    lo = (r1 - mid.astype(F32)).astype(BF16)
    return hi, mid, lo


def _tri_dot(x, tri):
    hi, mid, lo = _split3(x)
    d = _DIMS["nn"]
    return _bdot(hi, tri, d) + _bdot(mid, tri, d) + _bdot(lo, tri, d)


def _tri(n, lower):
    r = lax.broadcasted_iota(jnp.int32, (n, n), 0)
    c = lax.broadcasted_iota(jnp.int32, (n, n), 1)
    return jnp.where((r >= c) if lower else (r <= c), 1.0, 0.0).astype(BF16)


def _fox_cum(name, fproj, bcol):
    seq = fproj.shape[0]
    blk = min(CUM_BLK, seq)

    def body(f_ref, b_ref, o_ref, carry_ref):
        i = pl.program_id(0)

        @pl.when(i == 0)
        def _():
            carry_ref[...] = jnp.zeros_like(carry_ref)

        z = f_ref[...].T + b_ref[...]
        logf = jnp.minimum(z, 0.0) - jnp.log(1.0 + jnp.exp(-jnp.abs(z)))
        carry = carry_ref[...]
        cum = _tri_dot(logf, _tri(blk, lower=False)) + jnp.tile(carry, (1, blk // LANE))
        o_ref[...] = cum[0:8, :]
        carry_ref[...] = carry + jnp.sum(logf, axis=1, keepdims=True)

    return pl.pallas_call(
        body, name=name, grid=(seq // blk,),
        in_specs=[pl.BlockSpec((blk, LANE), lambda i: (i, 0)), pl.BlockSpec((LANE, 1), lambda i: (0, 0))],
        out_specs=pl.BlockSpec((8, blk), lambda i: (0, i)),
        out_shape=jax.ShapeDtypeStruct((8, seq), F32),
        scratch_shapes=[pltpu.VMEM((LANE, LANE), F32)],
        compiler_params=_cparams(("arbitrary",)),
    )(fproj, bcol)


def _fox_cum_bwd(name, dcum_t, fproj, bcol):
    seq = fproj.shape[0]
    blk = min(CUM_BLK, seq)
    n = seq // blk

    def body(dc_ref, f_ref, b_ref, df_ref, db_ref, carry_ref, acc_ref):
        i = pl.program_id(0)

        @pl.when(i == 0)
        def _():
            carry_ref[...] = jnp.zeros_like(carry_ref)
            acc_ref[...] = jnp.zeros_like(acc_ref)

        dc = jnp.concatenate([dc_ref[...], jnp.zeros((LANE - 8, blk), F32)], axis=0)
        carry = carry_ref[...]
        dlogf = _tri_dot(dc, _tri(blk, lower=True)) + jnp.tile(carry, (1, blk // LANE))
        carry_ref[...] = carry + jnp.sum(dc, axis=1, keepdims=True)
        z = f_ref[...].T + b_ref[...]
        dft = dlogf / (1.0 + jnp.exp(z))
        df_ref[...] = dft.T.astype(df_ref.dtype)
        acc_ref[...] += jnp.sum(dft, axis=1, keepdims=True)

        @pl.when(i == n - 1)
        def _():
            db_ref[...] = acc_ref[...]

    return pl.pallas_call(
        body, name=name, grid=(n,),
        in_specs=[pl.BlockSpec((8, blk), lambda i: (0, n - 1 - i)), pl.BlockSpec((blk, LANE), lambda i: (n - 1 - i, 0)),
                  pl.BlockSpec((LANE, 1), lambda i: (0, 0))],
        out_specs=(pl.BlockSpec((blk, LANE), lambda i: (n - 1 - i, 0)), pl.BlockSpec((LANE, LANE), lambda i: (0, 0))),
        out_shape=(jax.ShapeDtypeStruct((seq, LANE), BF16), jax.ShapeDtypeStruct((LANE, LANE), F32)),
        scratch_shapes=[pltpu.VMEM((LANE, LANE), F32), pltpu.VMEM((LANE, LANE), F32)],
        compiler_params=_cparams(("arbitrary",)),
    )(dcum_t, fproj, bcol)


FOX_BLK = 512
FOX_SCALE = FOX_HEAD_DIM ** -0.5


def _fox_head_mask(shape, hh):
    lane = lax.broadcasted_iota(jnp.int32, shape, 1)
    return (lane < FOX_HEAD_DIM) if hh == 0 else (lane >= FOX_HEAD_DIM)


def _fox_bias(cum_ref, hh, q0, k0, blk):
    c0 = jnp.max(cum_ref[hh:hh + 1, pl.ds(q0, LANE)], axis=1, keepdims=True)
    return c0 - cum_ref[hh:hh + 1, pl.ds(k0, blk)]


def _fox_fwd(name, qkv, cum_t):
    seq = qkv.shape[0]
    blk = min(FOX_BLK, seq)
    nb = seq // blk
    npair = FOX_HEADS // 2

    def body(q_ref, k_ref, v_ref, cum_ref, o_ref, lse_ref):
        iq = pl.program_id(1)
        q0 = pl.multiple_of(iq * blk, blk)
        qv = q_ref[...]
        row = lax.broadcasted_iota(jnp.int32, (blk, blk), 0)
        col = lax.broadcasted_iota(jnp.int32, (blk, blk), 1)
        qhs = [jnp.where(_fox_head_mask(qv.shape, hh), qv, jnp.zeros_like(qv)) * FOX_SCALE for hh in range(2)]

        def block(kb, states, masked):
            k0 = pl.multiple_of(kb * blk, blk)
            kv = k_ref[pl.ds(k0, blk), :]
            vv = v_ref[pl.ds(k0, blk), :]
            new = []
            for hh in range(2):
                m, l, acc = states[hh]
                s = _bdot(qhs[hh], kv, _DIMS["nt"]) + _fox_bias(cum_ref, hh, q0, k0, blk)
                if masked:
                    s = jnp.where(row >= col, s, -jnp.inf)
                m_new = jnp.maximum(m, jnp.max(s, axis=1, keepdims=True))
                alpha = jnp.exp(m - m_new)
                p = jnp.exp(s - m_new)
                l = alpha * l + jnp.sum(p, axis=1, keepdims=True)
                acc = alpha * acc + _bdot(p, vv, _DIMS["nn"])
                new.append((m_new, l, acc))
            return tuple(new)

        init = (jnp.full((blk, 1), -jnp.inf, F32), jnp.zeros((blk, 1), F32), jnp.zeros((blk, LANE), F32))
        states = lax.fori_loop(0, iq, lambda kb, st: block(kb, st, False), (init, init))
        states = block(iq, states, True)
        outs = []
        for hh in range(2):
            m, l, acc = states[hh]
            outs.append(acc / l)
            lse_ref[hh] = jnp.broadcast_to(m + jnp.log(l), (blk, LANE))
        o_ref[...] = jnp.where(_fox_head_mask(outs[0].shape, 0), outs[0], outs[1]).astype(o_ref.dtype)

    return pl.pallas_call(
        body, name=name, grid=(npair, nb),
        in_specs=[pl.BlockSpec((blk, LANE), lambda p, i: (i, p)),
                  pl.BlockSpec((seq, LANE), lambda p, i: (0, npair + p)),
                  pl.BlockSpec((seq, LANE), lambda p, i: (0, 2 * npair + p)),
                  pl.BlockSpec((None, 2, seq), lambda p, i: (p, 0, 0))],
        out_specs=(pl.BlockSpec((blk, LANE), lambda p, i: (i, p)),
                   pl.BlockSpec((2, blk, LANE), lambda p, i: (p, i, 0))),
        out_shape=(jax.ShapeDtypeStruct((seq, FOX_WIDTH), BF16), jax.ShapeDtypeStruct((FOX_HEADS, seq, LANE), F32)),
        compiler_params=_cparams(("parallel", "arbitrary")),
    )(qkv, qkv, qkv, cum_t)


def _fox_bwd(name, qkv, cum_t, att, datt, lse):
    seq = qkv.shape[0]
    blk = min(FOX_BLK, seq)
    nb = seq // blk
    npair = FOX_HEADS // 2

    def body(q_ref, k_ref, v_ref, cum_ref, o_ref, do_ref, lse_ref, dq_ref, dk_ref, dv_ref, dcum_ref):
        iq = pl.program_id(1)
        q0 = pl.multiple_of(iq * blk, blk)

        @pl.when(iq == 0)
        def _():
            dk_ref[...] = jnp.zeros_like(dk_ref)
            dv_ref[...] = jnp.zeros_like(dv_ref)
            dcum_ref[...] = jnp.zeros_like(dcum_ref)

        qv = q_ref[...]
        dov = do_ref[...].astype(F32)
        ov = o_ref[...].astype(F32)
        row = lax.broadcasted_iota(jnp.int32, (blk, blk), 0)
        col = lax.broadcasted_iota(jnp.int32, (blk, blk), 1)
        qhs, dohbs, deltas, lses = [], [], [], []
        for hh in range(2):
            hm = _fox_head_mask(qv.shape, hh)
            qhs.append(jnp.where(hm, qv, jnp.zeros_like(qv)) * FOX_SCALE)
            doh = jnp.where(hm, dov, 0.0)
            dohbs.append(doh.astype(BF16))
            deltas.append(jnp.sum(doh * ov, axis=1, keepdims=True))
            lses.append(jnp.tile(lse_ref[hh], (1, blk // LANE)))

        def block(kb, accs, masked):
            k0 = pl.multiple_of(kb * blk, blk)
            kv = k_ref[pl.ds(k0, blk), :]
            vv = v_ref[pl.ds(k0, blk), :]
            new = []
            dk_blk = None
            dv_blk = None
            for hh in range(2):
                dq_acc, rs_acc = accs[hh]
                s = _bdot(qhs[hh], kv, _DIMS["nt"]) + _fox_bias(cum_ref, hh, q0, k0, blk)
                p = jnp.exp(s - lses[hh])
                if masked:
                    p = jnp.where(row >= col, p, 0.0)
                dp = _bdot(dohbs[hh], vv, _DIMS["nt"])
                ds = p * (dp - deltas[hh])
                dsb = ds.astype(BF16)
                dk_h = _bdot(dsb, qhs[hh], _DIMS["tn"])
                dv_h = _bdot(p, dohbs[hh], _DIMS["tn"])
                dk_blk = dk_h if dk_blk is None else dk_blk + dk_h
                dv_blk = dv_h if dv_blk is None else dv_blk + dv_h
                dcum_ref[hh:hh + 1, pl.ds(k0, blk)] -= jnp.sum(ds, axis=0, keepdims=True)
                new.append((dq_acc + _bdot(dsb, kv, _DIMS["nn"]), rs_acc + jnp.sum(ds, axis=1, keepdims=True)))
            dk_ref[pl.ds(k0, blk), :] += dk_blk
            dv_ref[pl.ds(k0, blk), :] += dv_blk
            return tuple(new)

        init = (jnp.zeros((blk, LANE), F32), jnp.zeros((blk, 1), F32))
        accs = lax.fori_loop(0, iq, lambda kb, a: block(kb, a, False), (init, init))
        accs = block(iq, accs, True)
        for hh in range(2):
            dcum_ref[hh:hh + 1, pl.ds(q0, blk)] += jnp.broadcast_to(accs[hh][1], (blk, LANE)).T[0:1, :]
        dq = jnp.where(_fox_head_mask(qv.shape, 0), accs[0][0], accs[1][0]) * FOX_SCALE
        dq_ref[...] = dq.astype(dq_ref.dtype)

    qblk = pl.BlockSpec((blk, LANE), lambda p, i: (i, p))
    full = pl.BlockSpec((seq, LANE), lambda p, i: (0, p))
    return pl.pallas_call(
        body, name=name, grid=(npair, nb),
        in_specs=[qblk,
                  pl.BlockSpec((seq, LANE), lambda p, i: (0, npair + p)),
                  pl.BlockSpec((seq, LANE), lambda p, i: (0, 2 * npair + p)),
                  pl.BlockSpec((None, 2, seq), lambda p, i: (p, 0, 0)),
                  qblk, qblk,
                  pl.BlockSpec((2, blk, LANE), lambda p, i: (p, i, 0))],
        out_specs=(qblk, full, full, pl.BlockSpec((None, 2, seq), lambda p, i: (p, 0, 0))),
        out_shape=(jax.ShapeDtypeStruct((seq, FOX_WIDTH), BF16), jax.ShapeDtypeStruct((seq, FOX_WIDTH), F32),
                   jax.ShapeDtypeStruct((seq, FOX_WIDTH), F32), jax.ShapeDtypeStruct((npair, 2, seq), F32)),
        compiler_params=_cparams(("arbitrary", "arbitrary")),
    )(qkv, qkv, qkv, cum_t, att, datt, lse)


MEM_SCALE = MEM_HEAD_DIM ** -0.5


def _mem_probs(qh, kh):
    s = _bdot(qh, kh, _DIMS["nt"]) * MEM_SCALE
    p = jnp.exp(s - jnp.max(s, axis=1, keepdims=True))
    return p / jnp.sum(p, axis=1, keepdims=True)


def _mem_fwd(name, q2, kv, *, tr=512):
    seq = q2.shape[0]
    mlen = kv.shape[0]
    tr = min(tr, seq)

    def body(q_ref, kv_ref, o_ref):
        for h in range(MEM_HEADS):
            sl = slice(h * MEM_HEAD_DIM, (h + 1) * MEM_HEAD_DIM)
            sv = slice(MEM_WIDTH + h * MEM_HEAD_DIM, MEM_WIDTH + (h + 1) * MEM_HEAD_DIM)
            p = _mem_probs(q_ref[:, sl], kv_ref[:, sl])
            o_ref[:, sl] = _bdot(p, kv_ref[:, sv], _DIMS["nn"]).astype(o_ref.dtype)

    return pl.pallas_call(
        body, name=name, grid=(seq // tr,),
        in_specs=[pl.BlockSpec((tr, MEM_WIDTH), lambda i: (i, 0)), pl.BlockSpec((mlen, 2 * MEM_WIDTH), lambda i: (0, 0))],
        out_specs=pl.BlockSpec((tr, MEM_WIDTH), lambda i: (i, 0)),
        out_shape=jax.ShapeDtypeStruct((seq, MEM_WIDTH), BF16),
        compiler_params=_cparams(("parallel",)),
    )(q2, kv)


def _mem_bwd(name, q2, kv, do2, *, tr=512):
    seq = q2.shape[0]
    mlen = kv.shape[0]
    tr = min(tr, seq)

    def body(q_ref, kv_ref, do_ref, dq_ref, dkv_ref):
        i = pl.program_id(0)

        @pl.when(i == 0)
        def _():
            dkv_ref[...] = jnp.zeros_like(dkv_ref)

        for h in range(MEM_HEADS):
            sl = slice(h * MEM_HEAD_DIM, (h + 1) * MEM_HEAD_DIM)
            sv = slice(MEM_WIDTH + h * MEM_HEAD_DIM, MEM_WIDTH + (h + 1) * MEM_HEAD_DIM)
            qh = q_ref[:, sl]
            kh = kv_ref[:, sl]
            doh = do_ref[:, sl].astype(BF16)
            p = _mem_probs(qh, kh)
            dp = _bdot(doh, kv_ref[:, sv], _DIMS["nt"])
            ds = (p * (dp - jnp.sum(p * dp, axis=1, keepdims=True)) * MEM_SCALE).astype(BF16)
            dq_ref[:, sl] = _bdot(ds, kh, _DIMS["nn"]).astype(dq_ref.dtype)
            dkv_ref[:, sl] += _bdot(ds, qh, _DIMS["tn"])
            dkv_ref[:, sv] += _bdot(p, doh, _DIMS["tn"])

    row = pl.BlockSpec((tr, MEM_WIDTH), lambda i: (i, 0))
    kvs = pl.BlockSpec((mlen, 2 * MEM_WIDTH), lambda i: (0, 0))
    return pl.pallas_call(
        body, name=name, grid=(seq // tr,), in_specs=[row, kvs, row], out_specs=(row, kvs),
        out_shape=(jax.ShapeDtypeStruct((seq, MEM_WIDTH), BF16), jax.ShapeDtypeStruct((mlen, 2 * MEM_WIDTH), F32)),
        compiler_params=_cparams(("arbitrary",)),
    )(q2, kv, do2)


_HBM = pl.BlockSpec(memory_space=pl.ANY)
_HBM_ONLY = pl.BlockSpec(memory_space=pltpu.HBM)
_MESH = pl.DeviceIdType.MESH


def _mesh_place():
    x, y, c = lax.axis_index("x"), lax.axis_index("y"), lax.axis_index("c")
    other_chips = [(1 - x, y), (x, 1 - y), (1 - x, 1 - y)]
    return x, y, c, other_chips


def _gather_all(name, arrays):
    n = len(arrays)

    def body(*refs):
        ins, outs = refs[:n], refs[n:2 * n]
        send_sems, recv_sems, local_sems = refs[2 * n:]
        x, y, c, chips = _mesh_place()
        me, sibling = (x, y, c), (x, y, 1 - c)

        def slot(a, place):
            px, py, pc = place
            return outs[a].at[4 * px + 2 * py + pc]

        def copy(a, k, block, to, src=None):
            return pltpu.make_async_remote_copy(
                src_ref=slot(a, block) if src is None else src, dst_ref=slot(a, block),
                send_sem=send_sems.at[a, k], recv_sem=recv_sems.at[a, k], device_id=to, device_id_type=_MESH)

        mine = [pltpu.make_async_copy(ins[a], slot(a, me), local_sems.at[a]) for a in range(n)]
        for cp in mine:
            cp.start()
        first = []
        for a in range(n):
            first.append(copy(a, 0, me, sibling, src=ins[a]))
            first += [copy(a, 1 + j, me, (*chip, c), src=ins[a]) for j, chip in enumerate(chips)]
        for cp in first:
            cp.start()
        passed = []
        for j, chip in enumerate(chips):
            for a in range(n):
                copy(a, 1 + j, (*chip, c), me).wait_recv()
                fwd = copy(a, 4 + j, (*chip, c), sibling)
                fwd.start()
                passed.append(fwd)
        for a in range(n):
            copy(a, 0, sibling, me).wait_recv()
            for j, chip in enumerate(chips):
                copy(a, 4 + j, (*chip, 1 - c), me).wait_recv()
        for cp in first + passed:
            cp.wait_send()
        for cp in mine:
            cp.wait()

    out_shape = tuple(jax.ShapeDtypeStruct((N_DEV,) + arr.shape, arr.dtype) for arr in arrays)
    return pl.pallas_call(
        body, name=name, in_specs=[_HBM] * n, out_specs=tuple([_HBM] * n), out_shape=out_shape,
        scratch_shapes=[pltpu.SemaphoreType.DMA((n, N_DEV - 1)), pltpu.SemaphoreType.DMA((n, N_DEV - 1)),
                        pltpu.SemaphoreType.DMA((n,))],
    )(*arrays)


_SEM = pl.BlockSpec(memory_space=pltpu.SEMAPHORE)
_DATAFLOW = pltpu.SideEffectType.DATAFLOW_SIDE_EFFECTING


def _device_index():
    return (4 * lax.axis_index("x") + 2 * lax.axis_index("y") + lax.axis_index("c")).astype(jnp.int32).reshape(1)


def _place_own(name, pieces, *, stacked_src):
    n = len(pieces)

    def body(me_ref, *refs):
        for a in range(n):
            refs[n + a][...] = refs[a][...]

    def spec(shape):
        return pl.BlockSpec((None,) + tuple(shape), lambda i, me_ref: (me_ref[0],) + (0,) * len(shape))

    shapes = [p.shape[1:] if stacked_src else p.shape for p in pieces]
    if stacked_src:
        in_specs = [spec(s) for s in shapes]
    else:
        in_specs = [pl.BlockSpec(tuple(s), lambda i, me_ref, nd=len(s): (0,) * nd) for s in shapes]
    return pl.pallas_call(
        body, name=name,
        grid_spec=pltpu.PrefetchScalarGridSpec(num_scalar_prefetch=1, grid=(1,), in_specs=in_specs,
                                               out_specs=tuple(spec(s) for s in shapes)),
        out_shape=tuple(jax.ShapeDtypeStruct((N_DEV,) + tuple(s), p.dtype) for s, p in zip(shapes, pieces)),
        compiler_params=_cparams(("arbitrary",)),
    )(_device_index(), *pieces)


def _peer_places():
    x, y, c = lax.axis_index("x"), lax.axis_index("y"), lax.axis_index("c")
    peers = []
    for k in range(N_DEV - 1):
        flip = k + 1
        px = 1 - x if flip & 4 else x
        py = 1 - y if flip & 2 else y
        pc = 1 - c if flip & 1 else c
        peers.append((px, py, pc, 4 * px + 2 * py + pc))
    return 4 * x + 2 * y + c, peers


def _direct_copy(srcs, lands, send_sems, recv_sems, a, k, me, peer, scatter):
    px, py, pc, pidx = peer
    return pltpu.make_async_remote_copy(
        src_ref=srcs[a].at[pidx] if scatter else srcs[a], dst_ref=lands[a].at[me],
        send_sem=send_sems.at[a * (N_DEV - 1) + k], recv_sem=recv_sems.at[a * (N_DEV - 1) + k],
        device_id=(px, py, pc), device_id_type=_MESH)


def _send_start(name, srcs, lands, *, scatter):
    n = len(srcs)

    def body(*refs):
        src_refs, land_refs = refs[:n], refs[n:2 * n]
        send_sems, recv_sems = refs[2 * n], refs[2 * n + 1]
        token = refs[-1]
        me, peers = _peer_places()
        for k, peer in enumerate(peers):
            for a in range(n):
                _direct_copy(src_refs, land_refs, send_sems, recv_sems, a, k, me, peer, scatter).start()
        token[...] = jnp.zeros_like(token)

    hbm_shapes = [pltpu.HBM(t.shape, t.dtype) for t in list(srcs) + list(lands)]
    outs = pl.pallas_call(
        body, name=name,
        out_shape=(pltpu.SemaphoreType.DMA((n * (N_DEV - 1),)), pltpu.SemaphoreType.DMA((n * (N_DEV - 1),)), *hbm_shapes,
                   jax.ShapeDtypeStruct((8, LANE), F32)),
        in_specs=[_HBM_ONLY] * (2 * n),
        out_specs=(_SEM, _SEM, *([_HBM_ONLY] * (2 * n)), pl.BlockSpec(memory_space=pltpu.VMEM)),
        input_output_aliases={i: 2 + i for i in range(2 * n)},
        compiler_params=pltpu.CompilerParams(has_side_effects=_DATAFLOW),
    )(*[pltpu.with_memory_space_constraint(t, pltpu.HBM) for t in list(srcs) + list(lands)])
    return outs[0], outs[1], outs[2:2 + n], outs[2 + n:2 + 2 * n], outs[-1]


def _send_wait(name, send_sems, recv_sems, srcs, lands, after, *, scatter):
    n = len(srcs)

    def body(*refs):
        src_refs, land_refs = refs[:n], refs[n:2 * n]
        send_sems, recv_sems = refs[2 * n], refs[2 * n + 1]
        me, peers = _peer_places()
        for k, peer in enumerate(peers):
            for a in range(n):
                cp = _direct_copy(src_refs, land_refs, send_sems, recv_sems, a, k, me, peer, scatter)
                cp.wait_send()
                cp.wait_recv()

    hbm_shapes = [pltpu.HBM(t.shape, t.dtype) for t in list(srcs) + list(lands)]
    outs = pl.pallas_call(
        body, name=name, out_shape=tuple(hbm_shapes),
        in_specs=[_HBM_ONLY] * (2 * n) + [_SEM, _SEM, _HBM],
        out_specs=tuple([_HBM_ONLY] * (2 * n)),
        input_output_aliases={i: i for i in range(2 * n)},
        compiler_params=pltpu.CompilerParams(has_side_effects=_DATAFLOW),
    )(*srcs, *lands, send_sems, recv_sems, after)
    return outs[n:]


def _scatter_sibling(name, arrays):
    n = len(arrays)

    def body(*refs):
        ins, sibs = refs[:n], refs[n:2 * n]
        send_sems, recv_sems = refs[2 * n:]
        x, y, c, _ = _mesh_place()
        copies = []
        for a in range(n):
            for j in range(4):
                rdma = pltpu.make_async_remote_copy(
                    src_ref=ins[a].at[2 * j + (1 - c)], dst_ref=sibs[a].at[j], send_sem=send_sems.at[a, j],
                    recv_sem=recv_sems.at[a, j], device_id=(x, y, 1 - c), device_id_type=_MESH)
                rdma.start()
                copies.append(rdma)
        for cp in copies:
            cp.wait()

    four = tuple(jax.ShapeDtypeStruct((4,) + arr.shape[1:], arr.dtype) for arr in arrays)
    return pl.pallas_call(
        body, name=name, in_specs=[_HBM] * n, out_specs=tuple([_HBM] * n), out_shape=four,
        scratch_shapes=[pltpu.SemaphoreType.DMA((n, 4)), pltpu.SemaphoreType.DMA((n, 4))],
    )(*arrays)


def _add_chip_partials(name, pieces, sibs):
    n = len(pieces)
    core = lax.axis_index("c").astype(jnp.int32).reshape(1)

    def body(c_ref, *refs):
        for a in range(n):
            out = refs[2 * n + a]
            out[...] = (refs[a][...].astype(F32) + refs[n + a][...].astype(F32)).astype(out.dtype)

    own_specs = [pl.BlockSpec((None,) + arr.shape[1:], lambda j, c_ref: (2 * j + c_ref[0], 0, 0)) for arr in pieces]
    four_specs = [pl.BlockSpec((None,) + arr.shape[1:], lambda j, c_ref: (j, 0, 0)) for arr in sibs]
    return pl.pallas_call(
        body, name=name,
        grid_spec=pltpu.PrefetchScalarGridSpec(num_scalar_prefetch=1, grid=(4,), in_specs=own_specs + four_specs,
                                               out_specs=tuple(four_specs)),
        out_shape=tuple(jax.ShapeDtypeStruct(arr.shape, arr.dtype) for arr in sibs),
        compiler_params=_cparams(("parallel",)),
    )(core, *pieces, *sibs)


def _scatter_chips(name, arrays):
    n = len(arrays)

    def body(*refs):
        ins, outs = refs[:n], refs[n:2 * n]
        send_sems, recv_sems, local_sems = refs[2 * n:]
        x, y, c, chips = _mesh_place()
        my_chip = 2 * x + y
        copies = []
        for a in range(n):
            local = pltpu.make_async_copy(ins[a].at[my_chip], outs[a].at[my_chip], local_sems.at[a])
            local.start()
            copies.append(local)
            for k, (px, py) in enumerate(chips):
                rdma = pltpu.make_async_remote_copy(
                    src_ref=ins[a].at[2 * px + py], dst_ref=outs[a].at[my_chip], send_sem=send_sems.at[a, k],
                    recv_sem=recv_sems.at[a, k], device_id=(px, py, c), device_id_type=_MESH)
                rdma.start()
                copies.append(rdma)
        for cp in copies:
            cp.wait()

    return pl.pallas_call(
        body, name=name, in_specs=[_HBM] * n, out_specs=tuple([_HBM] * n),
        out_shape=tuple(jax.ShapeDtypeStruct(arr.shape, arr.dtype) for arr in arrays),
        scratch_shapes=[pltpu.SemaphoreType.DMA((n, 3)), pltpu.SemaphoreType.DMA((n, 3)),
                        pltpu.SemaphoreType.DMA((n,))],
    )(*arrays)


def _unstack_cols(name, stacked):
    n, rows, cols = stacked.shape

    def body(i_ref, o_ref):
        o_ref[...] = i_ref[...]

    return pl.pallas_call(
        body, name=name, grid=(n,), in_specs=[pl.BlockSpec((None, rows, cols), lambda k: (k, 0, 0))],
        out_specs=pl.BlockSpec((rows, cols), lambda k: (0, k)),
        out_shape=jax.ShapeDtypeStruct((rows, n * cols), stacked.dtype),
        compiler_params=_cparams(("parallel",)),
    )(stacked)


def _restack_cols(name, mat):
    rows, width = mat.shape
    cols = width // N_DEV

    def body(i_ref, o_ref):
        o_ref[...] = i_ref[...]

    return pl.pallas_call(
        body, name=name, grid=(N_DEV,), in_specs=[pl.BlockSpec((rows, cols), lambda k: (0, k))],
        out_specs=pl.BlockSpec((None, rows, cols), lambda k: (k, 0, 0)),
        out_shape=jax.ShapeDtypeStruct((N_DEV, rows, cols), mat.dtype),
        compiler_params=_cparams(("parallel",)),
    )(mat)


def _remap_pieces(runs):
    plan = {}
    for du, dc, su, sc, ln in runs:
        while ln > 0:
            lane = dc % LANE
            take = min(ln, LANE - lane)
            plan.setdefault((du, dc // LANE), []).append((su, sc, take, lane))
            dc, sc, ln = dc + take, sc + take, ln - take
    return plan


def _remap(name, srcs, src_units, runs, *, out_units, out_cols, out_dtype, tr=256):
    rows = srcs[0].shape[-2]
    tr = min(tr, rows)
    plan = _remap_pieces(runs)
    n_src = len(srcs)
    stacked_out = out_units is not None
    n_tiles = out_cols // LANE

    def body(*refs):
        o_ref = refs[n_src]

        def src_tile(unit, t):
            ai, lead = src_units[unit]
            ref = refs[ai]
            sl = slice(t * LANE, (t + 1) * LANE)
            return (ref[:, sl] if lead is None else ref[lead, :, sl]).astype(F32)

        lane = lax.broadcasted_iota(jnp.int32, (tr, LANE), 1)
        for du in range(out_units if stacked_out else 1):
            for t in range(n_tiles):
                acc = jnp.zeros((tr, LANE), F32)
                for su, sc, ln, dl in plan.get((du if stacked_out else None, t), []):
                    st, so = sc // LANE, sc % LANE
                    first = src_tile(su, st)
                    if so == dl and so + ln <= LANE:
                        piece = first
                    else:
                        second = src_tile(su, st + 1) if so + ln > LANE else first
                        both = jnp.concatenate([first, second], axis=1)
                        piece = pltpu.roll(both, (dl - so) % (2 * LANE), axis=1)[:, 0:LANE]
                    acc = piece if (dl == 0 and ln == LANE) else jnp.where(
                        jnp.logical_and(lane >= dl, lane < dl + ln), piece, acc)
                if stacked_out:
                    o_ref[du, :, t * LANE:(t + 1) * LANE] = acc.astype(o_ref.dtype)
                else:
                    o_ref[:, t * LANE:(t + 1) * LANE] = acc.astype(o_ref.dtype)

    in_specs = []
    for arr in srcs:
        if arr.ndim == 2:
            in_specs.append(pl.BlockSpec((tr, arr.shape[1]), lambda i: (i, 0)))
        else:
            in_specs.append(pl.BlockSpec((arr.shape[0], tr, arr.shape[2]), lambda i: (0, i, 0)))
    if stacked_out:
        out_spec = pl.BlockSpec((out_units, tr, out_cols), lambda i: (0, i, 0))
        out_shape = jax.ShapeDtypeStruct((out_units, rows, out_cols), out_dtype)
    else:
        out_spec = pl.BlockSpec((tr, out_cols), lambda i: (i, 0))
        out_shape = jax.ShapeDtypeStruct((rows, out_cols), out_dtype)
    return pl.pallas_call(
        body, name=name, grid=(rows // tr,), in_specs=in_specs, out_specs=out_spec, out_shape=out_shape,
        compiler_params=_cparams(("parallel",)),
    )(*srcs)


def _proj_col(c):
    if c < PROJ_GATE0:
        return c
    if c < PROJ_GATE0 + FOX_HEADS:
        return PROJ_F0 + (c - PROJ_GATE0)
    return c - FOX_HEADS


def _win_runs():
    cuts = sorted(set([0, PROJ_GATE0, PROJ_GATE0 + FOX_HEADS, IN_WIDTH] + [SHARD_IN * k for k in range(N_DEV + 1)]))
    return [(lo // SHARD_IN, lo % SHARD_IN, _proj_col(lo), hi - lo) for lo, hi in zip(cuts[:-1], cuts[1:])]


def _assemble_win(name, stacked):
    runs = [(None, pc, k, sc, ln) for k, sc, pc, ln in _win_runs()]
    return _remap(name, [stacked], [(0, k) for k in range(N_DEV)], runs,
                  out_units=None, out_cols=PROJ_WIDTH, out_dtype=BF16)


def _disassemble_dwin(name, dw):
    runs = [(k, sc, 0, pc, ln) for k, sc, pc, ln in _win_runs()]
    return _remap(name, [dw], [(0, None)], runs, out_units=N_DEV, out_cols=SHARD_IN_PAD, out_dtype=BF16)


def _concat_cols(name, parts, *, tr=512):
    rows = parts[0].shape[0]
    tr = min(tr, rows)
    widths = [p.shape[1] for p in parts]
    total = sum(widths)

    def body(*refs):
        o_ref = refs[len(parts)]
        lo = 0
        for r, w in zip(refs[:len(parts)], widths):
            o_ref[:, lo:lo + w] = r[...].astype(o_ref.dtype)
            lo += w

    return pl.pallas_call(
        body, name=name, grid=(rows // tr,),
        in_specs=[pl.BlockSpec((tr, w), lambda i: (i, 0)) for w in widths],
        out_specs=pl.BlockSpec((tr, total), lambda i: (i, 0)),
        out_shape=jax.ShapeDtypeStruct((rows, total), BF16),
        compiler_params=_cparams(("parallel",)),
    )(*parts)


FFN_BLK = FFN_HIDDEN // 2


def _ffn_col(c):
    half, r = divmod(c, FFN_HIDDEN)
    blk, r = divmod(r, FFN_BLK)
    return blk * 2 * FFN_BLK + half * FFN_BLK + r


def _assemble_wffn(name, stacked):
    runs = [(None, _ffn_col(SHARD_FFN * k), k, 0, SHARD_FFN) for k in range(N_DEV)]
    return _remap(name, [stacked], [(0, k) for k in range(N_DEV)], runs,
                  out_units=None, out_cols=2 * FFN_HIDDEN, out_dtype=BF16)


def _disassemble_dwffn(name, dw):
    runs = [(k, 0, 0, _ffn_col(SHARD_FFN * k), SHARD_FFN) for k in range(N_DEV)]
    return _remap(name, [dw], [(0, None)], runs, out_units=N_DEV, out_cols=SHARD_FFN_PAD, out_dtype=BF16)


def _ffn_in_swiglu(name, xn, w, *, tm=512):
    rows, k = xn.shape
    tm = min(tm, rows)
    nblk = FFN_HIDDEN // FFN_BLK

    def body(x_ref, w_ref, f_ref, g_ref):
        f = _bdot(x_ref[...], w_ref[...], _DIMS["nn"])
        f_ref[...] = f
        fa = f[:, 0:FFN_BLK]
        g_ref[...] = (fa * _sigmoid(fa) * f[:, FFN_BLK:2 * FFN_BLK]).astype(g_ref.dtype)

    return pl.pallas_call(
        body, name=name, grid=(nblk, rows // tm),
        in_specs=[pl.BlockSpec((tm, k), lambda j, i: (i, 0)), pl.BlockSpec((k, 2 * FFN_BLK), lambda j, i: (0, j))],
        out_specs=(pl.BlockSpec((tm, 2 * FFN_BLK), lambda j, i: (i, j)), pl.BlockSpec((tm, FFN_BLK), lambda j, i: (i, j))),
        out_shape=(jax.ShapeDtypeStruct((rows, 2 * FFN_HIDDEN), F32), jax.ShapeDtypeStruct((rows, FFN_HIDDEN), BF16)),
        compiler_params=_cparams(("parallel", "arbitrary")),
    )(xn, w)


def _d_ffn_out_swiglu(name, dh, w_out, f, *, tm=512):
    rows, d = dh.shape
    tm = min(tm, rows)
    nblk = FFN_HIDDEN // FFN_BLK

    def body(dh_ref, w_ref, f_ref, df_ref):
        dg = _bdot(dh_ref[...], w_ref[...], _DIMS["nt"])
        fa = f_ref[:, 0:FFN_BLK]
        fb = f_ref[:, FFN_BLK:2 * FFN_BLK]
        s = _sigmoid(fa)
        df_ref[:, 0:FFN_BLK] = (dg * fb * s * (1.0 + fa * (1.0 - s))).astype(df_ref.dtype)
        df_ref[:, FFN_BLK:2 * FFN_BLK] = (dg * fa * s).astype(df_ref.dtype)

    wide = pl.BlockSpec((tm, 2 * FFN_BLK), lambda j, i: (i, j))
    return pl.pallas_call(
        body, name=name, grid=(nblk, rows // tm),
        in_specs=[pl.BlockSpec((tm, d), lambda j, i: (i, 0)), pl.BlockSpec((FFN_BLK, d), lambda j, i: (j, 0)), wide],
        out_specs=wide, out_shape=jax.ShapeDtypeStruct((rows, 2 * FFN_HIDDEN), BF16),
        compiler_params=_cparams(("parallel", "arbitrary")),
    )(dh, w_out, f)


def _adamw(name, parts, w, m, v, *, tr=128):
    rows, cols = w.shape
    n_parts = parts.shape[0]
    tr = min(tr, rows)
    assert rows % tr == 0, (name, rows, tr)
    c1 = 1.0 - ADAM_B1 ** ADAM_STEP
    c2 = 1.0 - ADAM_B2 ** ADAM_STEP

    def body(p_ref, w_ref, m_ref, v_ref, g_ref, d_ref, nm_ref, nv_ref):
        g = p_ref[0].astype(F32)
        for s in range(1, n_parts):
            g = g + p_ref[s].astype(F32)
        m_new = ADAM_B1 * m_ref[...] + (1.0 - ADAM_B1) * g
        v_new = ADAM_B2 * v_ref[...] + (1.0 - ADAM_B2) * (g * g)
        upd = (m_new / c1) / (jnp.sqrt(v_new / c2) + ADAM_EPS) + ADAM_WD * w_ref[...]
        g_ref[...] = g
        d_ref[...] = -ADAM_LR * upd
        nm_ref[...] = m_new
        nv_ref[...] = v_new

    row = pl.BlockSpec((tr, cols), lambda i: (i, 0))
    out = jax.ShapeDtypeStruct((rows, cols), F32)
    return pl.pallas_call(
        body, name=name, grid=(rows // tr,),
        in_specs=[pl.BlockSpec((n_parts, tr, cols), lambda i: (0, i, 0)), row, row, row],
        out_specs=(row, row, row, row), out_shape=(out, out, out, out),
        compiler_params=_cparams(("parallel",)),
    )(parts, w, m, v)


_WEIGHTS = ("norm_mix", "w_in", "b_forget", "lam_re", "lam_im", "log_dt", "b_re", "b_im", "c_re", "c_im",
            "d_skip", "w_glu", "w_fox_o", "w_mix_out", "norm_mem_q", "norm_mem_kv", "w_mem_q", "w_mem_kv",
            "w_mem_o", "norm_ffn", "w_ffn_in", "w_ffn_out", "norm_final")
_SHARDED = ("w_in", "w_glu", "w_fox_o", "w_mix_out", "w_mem_q", "w_mem_kv", "w_mem_o", "w_ffn_in", "w_ffn_out")
_SMALL = tuple(n for n in _WEIGHTS if n not in _SHARDED)
_PACK_COLS = 1024


def _pack(arrays):
    flat = jnp.concatenate([a.reshape(-1).astype(F32) for a in arrays])
    rows = -(-flat.shape[0] // _PACK_COLS)
    return jnp.pad(flat, (0, rows * _PACK_COLS - flat.shape[0])).reshape(rows, _PACK_COLS)


def _unpack(buf, like):
    flat = buf.reshape(-1)
    out, pos = [], 0
    for a in like:
        out.append(flat[pos:pos + a.size].reshape(a.shape))
        pos += a.size
    return out


def _mm(name, a, b, mode, m, n, k, out_dtype, tm=1024, tn=512, tk=1024, **kw):
    return _matmul(name, a, b, mode, m, n, k, out_dtype=out_dtype, tm=tm, tn=tn, tk=tk, **kw)


def kernel(x, mem, norm_mix, w_in, b_forget, lam_re, lam_im, log_dt, b_re, b_im, c_re, c_im, d_skip, w_glu, w_fox_o, w_mix_out, norm_mem_q, norm_mem_kv, w_mem_q, w_mem_kv, w_mem_o, norm_ffn, w_ffn_in, w_ffn_out, norm_final, loss_target, m_norm_mix, m_w_in, m_b_forget, m_lam_re, m_lam_im, m_log_dt, m_b_re, m_b_im, m_c_re, m_c_im, m_d_skip, m_w_glu, m_w_fox_o, m_w_mix_out, m_norm_mem_q, m_norm_mem_kv, m_w_mem_q, m_w_mem_kv, m_w_mem_o, m_norm_ffn, m_w_ffn_in, m_w_ffn_out, m_norm_final, v_norm_mix, v_w_in, v_b_forget, v_lam_re, v_lam_im, v_log_dt, v_b_re, v_b_im, v_c_re, v_c_im, v_d_skip, v_w_glu, v_w_fox_o, v_w_mix_out, v_norm_mem_q, v_norm_mem_kv, v_w_mem_q, v_w_mem_kv, v_w_mem_o, v_norm_ffn, v_w_ffn_in, v_w_ffn_out, v_norm_final):
    given = dict(locals())
    weights = {n: given[n] for n in _WEIGHTS}
    mom_m = {n: given["m_" + n] for n in _WEIGHTS}
    mom_v = {n: given["v_" + n] for n in _WEIGHTS}
    seq = x.shape[1]
    nc = seq // SSM_CHUNK
    d = D_MODEL
    xs, mems, tgt = x[0], mem[0], loss_target[0]

    def padcols(a, width):
        return jnp.pad(a, ((0, 0), (0, width - a.shape[1])))

    shards = [padcols(w_in[0].astype(BF16), SHARD_IN_PAD), w_glu[0].astype(BF16), w_fox_o[0].astype(BF16),
              w_mix_out[0].astype(BF16), w_mem_q[0].astype(BF16), w_mem_kv[0].astype(BF16),
              w_mem_o[0].astype(BF16), padcols(w_ffn_in[0].astype(BF16), SHARD_FFN_PAD), w_ffn_out[0].astype(BF16)]
    win = _assemble_win("assemble_w_in", _gather_all("gather_w_in", shards[:1])[0])
    rest = shards[1:]
    gsend, grecv, rest_thru, lands, gtoken = _send_start(
        "gather_rest_start", rest, _place_own("place_weight_shards", rest, stacked_src=False), scatter=False)

    u = _rms_fwd("rms_mix", xs, norm_mix, after=gtoken)
    ussm = _mm("proj_ssm", u, win, "nn", seq, SSM_WIDTH, d, F32)
    qkv = _mm("proj_qkv", u, win, "nn", seq, 3 * FOX_WIDTH, d, BF16, tn=512, b_off=(0, SSM_WIDTH))
    gates = _mm("proj_gates", u, win, "nn", seq, 2 * d, d, F32, tn=1024, b_off=(0, PROJ_GATE0))
    fproj = _mm("proj_forget", u, win, "nn", seq, LANE, d, F32, tn=LANE, b_off=(0, PROJ_F0))

    ssm_params = (lam_re[0], lam_im[0], log_dt[0], b_re[0], b_im[0], c_re[0], c_im[0])
    (m_c, bw_c, cm_c, a8, aseg), mats_vjp = jax.vjp(lambda *p: _ssm_mats(*p, nc), *ssm_params)
    m_b = _bd_expand("ssm_expand_m", _BD_M, m_c)
    bw_b = _bd_expand("ssm_expand_bw", _BD_BW, bw_c)
    cm_b = _bd_expand("ssm_expand_cm", _BD_CM, cm_c)
    u8 = ussm.reshape(nc, SSM_CHUNK * SSM_WIDTH)
    d8 = jnp.tile(d_skip, (1, SSM_CHUNK))
    w4 = _ssm_w("ssm_w", u8, bw_b)
    sp4 = _ssm_scan("ssm_scan", w4, a8, aseg, reverse=False)
    y8 = _ssm_y("ssm_y", u8, sp4, m_b, cm_b)
    act = _ssm_post_fwd("ssm_act", y8, u8, d8).reshape(seq, SSM_WIDTH)

    bcol = jnp.pad(b_forget[0], (0, LANE - FOX_HEADS)).reshape(LANE, 1)
    cum_t = _fox_cum("fox_cum", fproj, bcol).reshape(FOX_HEADS // 2, 2, seq)
    att, lse = _fox_fwd("fox_fwd", qkv, cum_t)

    gathered = _send_wait("gather_rest_wait", gsend, grecv, rest_thru, lands, att, scatter=False)
    wglu = _unstack_cols("unstack_w_glu", gathered[0])
    wfoxo = _unstack_cols("unstack_w_fox_o", gathered[1])
    wmix = gathered[2].reshape(d, d)
    wmq = gathered[3].reshape(d, MEM_WIDTH)
    wmkv = gathered[4].reshape(d, 2 * MEM_WIDTH)
    wmo = _unstack_cols("unstack_w_mem_o", gathered[5])
    wffn_in = _assemble_wffn("assemble_w_ffn_in", gathered[6])
    wffn_out = gathered[7].reshape(FFN_HIDDEN, d)

    glu = _mm("glu", act, wglu, "nn", seq, 2 * d, SSM_WIDTH, F32, tn=1024)
    out_b = _mm("fox_out", att, wfoxo, "nn", seq, d, FOX_WIDTH, F32, tn=1024)

    mixin = _mix_fwd("mix", glu, gates, out_b)
    h1 = _mm("mix_out", mixin, wmix, "nn", seq, d, d, F32, tn=1024, add=xs)

    n1 = _rms_fwd("rms_mem_q", h1, norm_mem_q)
    q2 = _mm("mem_q", n1, wmq, "nn", seq, MEM_WIDTH, d, BF16)
    mn = _rms_fwd("rms_mem_kv", mems, norm_mem_kv)
    mlen = mems.shape[0]
    kv = _mm("mem_kv", mn, wmkv, "nn", mlen, 2 * MEM_WIDTH, d, BF16)
    o2 = _mem_fwd("mem_attn", q2, kv)
    h2 = _mm("mem_out", o2, wmo, "nn", seq, d, MEM_WIDTH, F32, tn=1024, add=h1)

    n2 = _rms_fwd("rms_ffn", h2, norm_ffn)
    f, g_act = _ffn_in_swiglu("ffn_in_swiglu", n2, wffn_in)
    h3 = _mm("ffn_out", g_act, wffn_out, "nn", seq, d, FFN_HIDDEN, F32, tk=FFN_HIDDEN, add=h2)
    loss_part, dh3, dg_final = _final_loss("final_loss", h3, tgt, norm_final.reshape(1, d))

    df = _d_ffn_out_swiglu("d_ffn_out_swiglu", dh3, wffn_out, f)
    dwffn_out = _mm("d_ffn_out_w", g_act, dh3, "tn", FFN_HIDDEN, d, seq, BF16, tm=1408, tn=1024)
    dn2 = _mm("d_ffn_in_x", df, wffn_in, "nt", seq, d, 2 * FFN_HIDDEN, F32, tn=1024, tk=FFN_HIDDEN)
    dwffn_in = _mm("d_ffn_in_w", n2, df, "tn", d, 2 * FFN_HIDDEN, seq, BF16, tn=1408)
    dh2, dg_ffn = _rms_bwd("d_rms_ffn", dn2, h2, norm_ffn, res=dh3)

    do2 = _mm("d_mem_out_x", dh2, wmo, "nt", seq, MEM_WIDTH, d, F32)
    dwmo = _restack_cols("restack_d_w_mem_o", _mm("d_mem_out_w", o2, dh2, "tn", MEM_WIDTH, d, seq, BF16, tn=1024))
    dq2, dkv = _mem_bwd("d_mem_attn", q2, kv, do2)
    dwmq = _mm("d_mem_q_w", n1, dq2, "tn", d, MEM_WIDTH, seq, BF16)
    dn1 = _mm("d_mem_q_x", dq2, wmq, "nt", seq, d, MEM_WIDTH, F32)
    dwmkv = _mm("d_mem_kv_w", mn, dkv, "tn", d, 2 * MEM_WIDTH, mlen, BF16, tn=1024)
    dmn = _mm("d_mem_kv_x", dkv, wmkv, "nt", mlen, d, 2 * MEM_WIDTH, F32)
    _, dg_memkv = _rms_bwd("d_rms_mem_kv", dmn, mems, norm_mem_kv)

    early = [dwmq.reshape(N_DEV, d // N_DEV, MEM_WIDTH), dwmkv.reshape(N_DEV, d // N_DEV, 2 * MEM_WIDTH), dwmo,
             _disassemble_dwffn("split_d_w_ffn_in", dwffn_in), dwffn_out.reshape(N_DEV, FFN_HIDDEN // N_DEV, d)]
    ssend, srecv, early_thru, early_lands, stoken = _send_start(
        "scatter_early_start", early, _place_own("place_early_grads", early, stacked_src=True), scatter=True)
    dh1, dg_memq = _rms_bwd("d_rms_mem_q", dn1, h1, norm_mem_q, res=dh2, after=stoken)

    dmixin = _mm("d_mix_out_x", dh1, wmix, "nt", seq, d, d, F32, tn=1024)
    dwmix = _mm("d_mix_out_w", mixin, dh1, "tn", d, d, seq, BF16, tn=1024)
    dglu, dgates, dout_b = _mix_bwd("d_mix", dmixin, glu, gates, out_b)
    datt = _mm("d_fox_out_x", dout_b, wfoxo, "nt", seq, FOX_WIDTH, d, F32)
    dwfoxo = _restack_cols("restack_d_w_fox_o", _mm("d_fox_out_w", att, dout_b, "tn", FOX_WIDTH, d, seq, BF16, tn=1024))
    dact = _mm("d_glu_x", dglu, wglu, "nt", seq, SSM_WIDTH, 2 * d, F32, tk=2 * d)
    dwglu = _restack_cols("restack_d_w_glu", _mm("d_glu_w", act, dglu, "tn", SSM_WIDTH, 2 * d, seq, BF16, tn=2 * d))

    dz8, dg_dskip = _ssm_post_bwd("d_ssm_act", dact.reshape(nc, SSM_CHUNK * SSM_WIDTH), y8, u8, d8)
    ds4, dcm = _ssm_ds("d_ssm_y_state", dz8, sp4, cm_b)
    g4, da8 = _ssm_scan("d_ssm_scan", ds4, a8, aseg, reverse=True, sprev4=sp4)
    dx8, dm, dbw = _ssm_dx("d_ssm_x", dz8, g4, u8, m_b, bw_b, d8)
    dussm = dx8.reshape(seq, SSM_WIDTH)
    g_ssm = mats_vjp((_bd_reduce("ssm_reduce_dm", _BD_M, dm), _bd_reduce("ssm_reduce_dbw", _BD_BW, dbw),
                      _bd_reduce("ssm_reduce_dcm", _BD_CM, dcm), da8, jnp.zeros_like(aseg)))

    dq, dk, dv, dcum = _fox_bwd("d_fox", qkv, cum_t, att, datt, lse)
    dfproj, dbf = _fox_cum_bwd("d_fox_cum", dcum.reshape(FOX_HEADS, seq), fproj, bcol)
    dg_bforget = dbf[0:FOX_HEADS, 0].reshape(1, FOX_HEADS)

    dproj = _concat_cols("d_proj_concat", (dussm, dq, dk, dv, dgates, dfproj))
    du = _mm("d_proj_x", dproj, win, "nt", seq, d, PROJ_WIDTH, F32, tn=1024, tk=1408)
    dwin = _mm("d_proj_w", u, dproj, "tn", d, PROJ_WIDTH, seq, BF16, tn=1408)
    dx, dg_mix = _rms_bwd("d_rms_mix", du, xs, norm_mix, res=dh1)

    late = [_disassemble_dwin("split_d_w_in", dwin), dwglu, dwfoxo, dwmix.reshape(N_DEV, d // N_DEV, d)]
    sib = _scatter_sibling("scatter_late_sibling", late)
    late_parts = _scatter_chips("scatter_late_chips", _add_chip_partials("sum_late_chip", late, sib))
    early_parts = _send_wait("scatter_early_wait", ssend, srecv, early_thru, early_lands, dx, scatter=True)
    received = dict(zip(("w_in", "w_glu", "w_fox_o", "w_mix_out"), late_parts))
    received.update(zip(("w_mem_q", "w_mem_kv", "w_mem_o", "w_ffn_in", "w_ffn_out"), early_parts))

    small_grads = dict(zip(
        _SMALL, (dg_mix, dg_bforget, g_ssm[0][None], g_ssm[1][None], g_ssm[2][None], g_ssm[3][None], g_ssm[4][None],
                 g_ssm[5][None], g_ssm[6][None], dg_dskip, dg_memq, dg_memkv, dg_ffn, dg_final.reshape(d))))
    small_like = [weights[n] for n in _SMALL]
    small_all = _gather_all("gather_small_grads", [_pack([small_grads[n] for n in _SMALL])])[0]
    pk = [_pack([src[n] for n in _SMALL]) for src in (weights, mom_m, mom_v)]
    small_out = _adamw("adamw_small", small_all, pk[0], pk[1], pk[2], tr=small_all.shape[1])
    small_res = [dict(zip(_SMALL, _unpack(buf, small_like))) for buf in small_out]

    results = [dict(r) for r in small_res]
    tiles = {"w_in": 128, "w_glu": 128, "w_fox_o": 128, "w_mix_out": 128, "w_mem_q": 128, "w_mem_kv": 128,
             "w_mem_o": 128, "w_ffn_in": 128, "w_ffn_out": 176}
    pads = {"w_in": SHARD_IN_PAD, "w_ffn_in": SHARD_FFN_PAD}
    for name in _SHARDED:
        parts = received[name]
        w2, m2, v2 = weights[name][0], mom_m[name][0], mom_v[name][0]
        cols = w2.shape[1]
        if name in pads:
            w2, m2, v2 = (padcols(t, pads[name]) for t in (w2, m2, v2))
        outs = _adamw("adamw_" + name, parts, w2, m2, v2, tr=tiles[name])
        for res, o in zip(results, outs):
            res[name] = o[:, :cols][None]

    loss = lax.psum(loss_part[0, 0], ("x", "y", "c"))
    out = [loss, dx[None]]
    for res in results:
        out.extend(res[n] for n in _WEIGHTS)
    return tuple(out)
```

```python
import math

import jax
import jax.numpy as jnp
import numpy as np
from jax import lax
from jax.experimental import pallas as pl
from jax.experimental.pallas import tpu as pltpu

F32 = jnp.float32
BF16 = jnp.bfloat16

N_DEV = 8
LANE = 128
VMEM_LIMIT = 56 * 1024 * 1024

D_MODEL = 1024
SSM_GROUP = 16
SSM_GROUPS = 32
SSM_WIDTH = 512
SSM_STATE = 64
SSM_CHUNK = 8
FOX_HEADS = 8
FOX_HEAD_DIM = 64
FOX_WIDTH = 512
MEM_HEADS = 4
MEM_HEAD_DIM = 128
MEM_WIDTH = 512
FFN_HIDDEN = 2816
RMS_EPS = 1e-6
IN_WIDTH = 4104
SHARD_IN = IN_WIDTH // N_DEV
SHARD_IN_PAD = 640
SHARD_FFN = 2 * FFN_HIDDEN // N_DEV
SHARD_FFN_PAD = 768
PROJ_GATE0 = 2048
PROJ_F0 = 4096
PROJ_WIDTH = 4224

ADAM_LR = 0.001
ADAM_B1 = 0.9
ADAM_B2 = 0.999
ADAM_EPS = 1e-08
ADAM_WD = 0.01
ADAM_STEP = 10


def _cparams(sem=None):
    return pltpu.CompilerParams(dimension_semantics=sem, vmem_limit_bytes=VMEM_LIMIT)


def _sigmoid(x):
    return 1.0 / (1.0 + jnp.exp(-x))


def _bdot(a, b, dims):
    return lax.dot_general(a.astype(BF16), b.astype(BF16), ((dims[0], dims[1]), ((), ())),
                           preferred_element_type=F32)


_DIMS = {"nn": ((1,), (0,)), "nt": ((1,), (1,)), "tn": ((0,), (0,))}


def _matmul(name, a, b, mode, m, n, k, *, out_dtype, tm, tn, tk, a_off=(0, 0), b_off=(0, 0), add=None):
    tm, tn, tk = min(tm, m), min(tn, n), min(tk, k)
    assert m % tm == 0 and n % tn == 0 and k % tk == 0, (name, m, n, k, tm, tn, tk)
    nk = k // tk
    grid = (m // tm, n // tn, nk)

    def blk(off, t):
        assert off % t == 0, (name, off, t)
        return off // t

    if mode in ("nn", "nt"):
        ar, ac = blk(a_off[0], tm), blk(a_off[1], tk)
        a_spec = pl.BlockSpec((tm, tk), lambda i, j, kk: (i + ar, kk + ac))
    else:
        ar, ac = blk(a_off[0], tk), blk(a_off[1], tm)
        a_spec = pl.BlockSpec((tk, tm), lambda i, j, kk: (kk + ar, i + ac))

    if mode in ("nn", "tn"):
        br, bc = blk(b_off[0], tk), blk(b_off[1], tn)
        b_spec = pl.BlockSpec((tk, tn), lambda i, j, kk: (kk + br, j + bc))
    else:
        br, bc = blk(b_off[0], tn), blk(b_off[1], tk)
        b_spec = pl.BlockSpec((tn, tk), lambda i, j, kk: (j + br, kk + bc))
    o_spec = pl.BlockSpec((tm, tn), lambda i, j, kk: (i, j))
    out_shape = jax.ShapeDtypeStruct((m, n), out_dtype)

    in_specs = [a_spec, b_spec]
    operands = [a, b]
    if add is not None:
        in_specs.append(pl.BlockSpec((tm, tn), lambda i, j, kk: (i, j)))
        operands.append(add)
    dims = _DIMS[mode]
    has_add = add is not None

    def body(*refs):
        a_ref, b_ref = refs[0], refs[1]
        add_ref = refs[2] if has_add else None
        o_ref = refs[3] if has_add else refs[2]
        acc_ref = refs[-1] if nk > 1 else None
        prod = _bdot(a_ref[...], b_ref[...], dims)

        def finish(total):
            if has_add:
                total = total + add_ref[...].astype(F32)
            o_ref[...] = total.astype(o_ref.dtype)

        if nk == 1:
            finish(prod)
        else:
            kk = pl.program_id(2)

            @pl.when(kk == 0)
            def _():
                acc_ref[...] = prod

            @pl.when(jnp.logical_and(kk > 0, kk < nk - 1))
            def _():
                acc_ref[...] += prod

            @pl.when(kk == nk - 1)
            def _():
                finish(acc_ref[...] + prod)

    scratch = [pltpu.VMEM((tm, tn), F32)] if nk > 1 else []
    return pl.pallas_call(
        body, name=name, grid=grid, in_specs=in_specs, out_specs=o_spec, out_shape=out_shape,
        scratch_shapes=scratch,
        compiler_params=_cparams(("parallel", "parallel", "arbitrary")),
    )(*operands)


def _rms_fwd(name, x, gain, *, tr=512, after=None):
    r, d = x.shape
    tr = min(tr, r)

    def body(x_ref, g_ref, *rest):
        o_ref = rest[-1]
        xv = x_ref[...]
        rstd = lax.rsqrt(jnp.mean(xv * xv, axis=-1, keepdims=True) + RMS_EPS)
        o_ref[...] = (xv * rstd * g_ref[...]).astype(o_ref.dtype)

    in_specs = [pl.BlockSpec((tr, d), lambda i: (i, 0)), pl.BlockSpec((1, d), lambda i: (0, 0))]
    ops = [x, gain]
    if after is not None:
        in_specs.append(pl.BlockSpec(after.shape, lambda i: (0, 0)))
        ops.append(after)
    return pl.pallas_call(
        body, name=name, grid=(r // tr,), in_specs=in_specs,
        out_specs=pl.BlockSpec((tr, d), lambda i: (i, 0)),
        out_shape=jax.ShapeDtypeStruct((r, d), BF16),
        compiler_params=_cparams(("parallel",)),
    )(*ops)


def _rms_bwd(name, dy, x, gain, res=None, *, tr=512, after=None):
    r, d = x.shape
    tr = min(tr, r)
    n = r // tr
    has_res = res is not None

    def body(*refs):
        dy_ref, x_ref, g_ref = refs[:3]
        res_ref = refs[3] if has_res else None
        dx_ref, dg_ref, acc_ref = refs[-3:]
        i = pl.program_id(0)
        xv = x_ref[...]
        rstd = lax.rsqrt(jnp.mean(xv * xv, axis=-1, keepdims=True) + RMS_EPS)
        xh = xv * rstd
        dyv = dy_ref[...].astype(F32)
        dxh = dyv * g_ref[...]
        dx = rstd * (dxh - xh * jnp.mean(dxh * xh, axis=-1, keepdims=True))
        if has_res:
            dx = dx + res_ref[...]
        dx_ref[...] = dx
        part = (dyv * xh).reshape(tr // 8, 8, d).sum(axis=0)

        @pl.when(i == 0)
        def _():
            acc_ref[...] = part

        @pl.when(i > 0)
        def _():
            acc_ref[...] += part

        @pl.when(i == n - 1)
        def _():
            dg_ref[...] = jnp.sum(acc_ref[...], axis=0, keepdims=True)

    row = pl.BlockSpec((tr, d), lambda i: (i, 0))
    in_specs = [row, row, pl.BlockSpec((1, d), lambda i: (0, 0))] + ([row] if has_res else [])
    ops = [dy, x, gain] + ([res] if has_res else [])
    if after is not None:
        in_specs.append(pl.BlockSpec(after.shape, lambda i: (0, 0)))
        ops.append(after)
    return pl.pallas_call(
        body, name=name, grid=(n,), in_specs=in_specs,
        out_specs=(row, pl.BlockSpec((1, d), lambda i: (0, 0))),
        out_shape=(jax.ShapeDtypeStruct((r, d), F32), jax.ShapeDtypeStruct((1, d), F32)),
        scratch_shapes=[pltpu.VMEM((8, d), F32)],
        compiler_params=_cparams(("arbitrary",)),
    )(*ops)


def _final_loss(name, h, target, gain, *, tr=512):
    r, d = h.shape
    tr = min(tr, r)
    n = r // tr

    def body(h_ref, t_ref, g_ref, loss_ref, dh_ref, dg_ref, accl_ref, accg_ref):
        i = pl.program_id(0)
        xv = h_ref[...]
        rstd = lax.rsqrt(jnp.mean(xv * xv, axis=-1, keepdims=True) + RMS_EPS)
        xh = xv * rstd
        e = xh * g_ref[...] - t_ref[...]
        dyv = e * (1.0 / d)
        dxh = dyv * g_ref[...]
        dh_ref[...] = rstd * (dxh - xh * jnp.mean(dxh * xh, axis=-1, keepdims=True))
        lpart = (e * e).reshape(tr // 8, 8, d).sum(axis=0)
        gpart = (dyv * xh).reshape(tr // 8, 8, d).sum(axis=0)

        @pl.when(i == 0)
        def _():
            accl_ref[...] = lpart
            accg_ref[...] = gpart

        @pl.when(i > 0)
        def _():
            accl_ref[...] += lpart
            accg_ref[...] += gpart

        @pl.when(i == n - 1)
        def _():
            tot = jnp.sum(jnp.sum(accl_ref[...], axis=0, keepdims=True), axis=1, keepdims=True)
            loss_ref[...] = jnp.broadcast_to(tot * (0.5 / d), (1, LANE))
            dg_ref[...] = jnp.sum(accg_ref[...], axis=0, keepdims=True)

    row = pl.BlockSpec((tr, d), lambda i: (i, 0))
    one = pl.BlockSpec((1, d), lambda i: (0, 0))
    return pl.pallas_call(
        body, name=name, grid=(n,), in_specs=[row, row, one],
        out_specs=(pl.BlockSpec((1, LANE), lambda i: (0, 0)), row, one),
        out_shape=(jax.ShapeDtypeStruct((1, LANE), F32), jax.ShapeDtypeStruct((r, d), F32),
                   jax.ShapeDtypeStruct((1, d), F32)),
        scratch_shapes=[pltpu.VMEM((8, d), F32), pltpu.VMEM((8, d), F32)],
        compiler_params=_cparams(("arbitrary",)),
    )(h, target, gain)


_GELU_C = math.sqrt(2.0 / math.pi)


def _gelu_parts(z):
    inner = _GELU_C * (z + 0.044715 * z * z * z)
    t = jnp.tanh(inner)
    val = 0.5 * z * (1.0 + t)
    dinner = _GELU_C * (1.0 + 3.0 * 0.044715 * z * z)
    grad = 0.5 * (1.0 + t) + 0.5 * z * (1.0 - t * t) * dinner
    return val, grad


def _ssm_post_fwd(name, y8, u8, d8, *, tr=256):
    r, c = y8.shape
    tr = min(tr, r)

    def body(y_ref, u_ref, d_ref, o_ref):
        z = y_ref[...] + d_ref[...] * u_ref[...]
        o_ref[...] = _gelu_parts(z)[0].astype(o_ref.dtype)

    row = pl.BlockSpec((tr, c), lambda i: (i, 0))
    return pl.pallas_call(
        body, name=name, grid=(r // tr,), in_specs=[row, row, pl.BlockSpec((1, c), lambda i: (0, 0))],
        out_specs=row, out_shape=jax.ShapeDtypeStruct((r, c), BF16),
        compiler_params=_cparams(("parallel",)),
    )(y8, u8, d8)


def _ssm_post_bwd(name, dact8, y8, u8, d8, *, tr=256, after=None):
    r, c = y8.shape
    tr = min(tr, r)
    n = r // tr

    def body(*refs):
        da_ref, y_ref, u_ref, d_ref = refs[:4]
        dz_ref, dd_ref, acc_ref = refs[-3:]
        i = pl.program_id(0)
        uv = u_ref[...]
        z = y_ref[...] + d_ref[...] * uv
        dz = da_ref[...].astype(F32) * _gelu_parts(z)[1]
        dz_ref[...] = dz
        part = (dz * uv).reshape(tr // 8, 8, c).sum(axis=0)

        @pl.when(i == 0)
        def _():
            acc_ref[...] = part

        @pl.when(i > 0)
        def _():
            acc_ref[...] += part

        @pl.when(i == n - 1)
        def _():
            tot = jnp.sum(acc_ref[...], axis=0, keepdims=True)
            out = tot[:, 0:SSM_WIDTH]
            for j in range(1, c // SSM_WIDTH):
                out = out + tot[:, j * SSM_WIDTH:(j + 1) * SSM_WIDTH]
            dd_ref[...] = out

    row = pl.BlockSpec((tr, c), lambda i: (i, 0))
    in_specs = [row, row, row, pl.BlockSpec((1, c), lambda i: (0, 0))]
    ops = [dact8, y8, u8, d8]
    if after is not None:
        in_specs.append(pl.BlockSpec(memory_space=pl.ANY))
        ops.append(after)
    return pl.pallas_call(
        body, name=name, grid=(n,), in_specs=in_specs,
        out_specs=(row, pl.BlockSpec((1, SSM_WIDTH), lambda i: (0, 0))),
        out_shape=(jax.ShapeDtypeStruct((r, c), F32), jax.ShapeDtypeStruct((1, SSM_WIDTH), F32)),
        scratch_shapes=[pltpu.VMEM((8, c), F32)],
        compiler_params=_cparams(("arbitrary",)),
    )(*ops)


def _mix_fwd(name, glu, gates, out_b, *, tr=256):
    r = glu.shape[0]
    d = D_MODEL
    tr = min(tr, r)

    def body(glu_ref, gate_ref, ob_ref, o_ref):
        out_a = glu_ref[:, 0:d] * _sigmoid(glu_ref[:, d:2 * d])
        mix = _sigmoid(gate_ref[:, 0:d]) * out_a + _sigmoid(gate_ref[:, d:2 * d]) * ob_ref[...]
        o_ref[...] = mix.astype(o_ref.dtype)

    wide = pl.BlockSpec((tr, 2 * d), lambda i: (i, 0))
    row = pl.BlockSpec((tr, d), lambda i: (i, 0))
    return pl.pallas_call(
        body, name=name, grid=(r // tr,), in_specs=[wide, wide, row], out_specs=row,
        out_shape=jax.ShapeDtypeStruct((r, d), BF16), compiler_params=_cparams(("parallel",)),
    )(glu, gates, out_b)


def _mix_bwd(name, dmix, glu, gates, out_b, *, tr=256):
    r = glu.shape[0]
    d = D_MODEL
    tr = min(tr, r)

    def body(dm_ref, glu_ref, gate_ref, ob_ref, dglu_ref, dgate_ref, dob_ref):
        dm = dm_ref[...]
        glu_a = glu_ref[:, 0:d]
        sb = _sigmoid(glu_ref[:, d:2 * d])
        ga = _sigmoid(gate_ref[:, 0:d])
        gb = _sigmoid(gate_ref[:, d:2 * d])
        out_a = glu_a * sb
        dout_a = dm * ga
        dglu_ref[:, 0:d] = (dout_a * sb).astype(dglu_ref.dtype)
        dglu_ref[:, d:2 * d] = (dout_a * glu_a * sb * (1.0 - sb)).astype(dglu_ref.dtype)
        dgate_ref[:, 0:d] = (dm * out_a * ga * (1.0 - ga)).astype(dgate_ref.dtype)
        dgate_ref[:, d:2 * d] = (dm * ob_ref[...] * gb * (1.0 - gb)).astype(dgate_ref.dtype)
        dob_ref[...] = (dm * gb).astype(dob_ref.dtype)

    wide = pl.BlockSpec((tr, 2 * d), lambda i: (i, 0))
    row = pl.BlockSpec((tr, d), lambda i: (i, 0))
    return pl.pallas_call(
        body, name=name, grid=(r // tr,), in_specs=[row, wide, wide, row], out_specs=(wide, wide, row),
        out_shape=(jax.ShapeDtypeStruct((r, 2 * d), BF16), jax.ShapeDtypeStruct((r, 2 * d), BF16),
                   jax.ShapeDtypeStruct((r, d), BF16)),
        compiler_params=_cparams(("parallel",)),
    )(dmix, glu, gates, out_b)


def _ssm_mats(lam_re, lam_im, log_dt, b_re, b_im, c_re, c_im, nc):
    hp = lax.Precision.HIGHEST
    t = SSM_CHUNK
    nq = SSM_GROUPS // 8
    lam = lax.complex(lam_re, lam_im)
    z = lam * jnp.exp(log_dt)[:, None]
    ks = jnp.arange(t + 1, dtype=F32)
    apow = jnp.exp(ks[:, None, None] * z[None])
    bbar = ((apow[1] - 1.0) / lam)[..., None] * lax.complex(b_re, b_im)
    c = lax.complex(c_re, c_im)

    ca = c[None] * apow[:, :, None, :]
    kmat = jnp.einsum("kgnp,gpm->kgnm", ca, bbar, precision=hp).real
    ii = np.arange(t)
    lag = ii[None, :] - ii[:, None]
    kt = kmat[np.clip(lag, 0, t)] * jnp.asarray(lag >= 0, F32)[:, :, None, None, None]
    kt = kt.reshape(t, t, nq, 8, SSM_GROUP, SSM_GROUP)
    m_c = kt.transpose(2, 0, 3, 5, 1, 4).reshape(nq, 1024, LANE)

    arev = jnp.exp((float(t - 1) - ks[:t])[:, None, None] * z[None])
    w = arev[:, :, :, None] * bbar[None]
    wr = jnp.stack([w.real, w.imag]).reshape(2, t, nq, 8, SSM_STATE, SSM_GROUP)
    bw_c = wr.transpose(2, 1, 3, 5, 0, 4).reshape(nq, 1024, LANE)

    ca1 = ca[1:]
    cr = jnp.stack([ca1.real, -ca1.imag]).reshape(2, t, nq, 8, SSM_GROUP, SSM_STATE)
    cm_c = cr.transpose(2, 0, 3, 5, 1, 4).reshape(nq, 1024, LANE)

    def tiles(v):
        vq = jnp.concatenate([v.real.reshape(nq, 512), v.imag.reshape(nq, 512)], axis=1)
        return jnp.broadcast_to(vq.reshape(nq, 8, 1, LANE), (nq, 8, 8, LANE))

    return m_c, bw_c, cm_c, tiles(apow[t]), tiles(jnp.exp(float(nc) * z))


_BD_M = (LANE, SSM_GROUP)
_BD_BW = (LANE, SSM_STATE)
_BD_CM = (512, SSM_GROUP)


def _bd_perm(cn):
    rr = lax.broadcasted_iota(jnp.int32, (1024, 1024), 0)
    cc = lax.broadcasted_iota(jnp.int32, (1024, 1024), 1)
    sh = cn.bit_length() - 1
    src = ((rr >> 7) << sh) + (((rr & (LANE - 1)) >> sh) << (3 + sh)) + (rr & (cn - 1))
    return jnp.where(src == cc, 1.0, 0.0).astype(BF16)


def _bd_rowgroup(span):
    r = lax.broadcasted_iota(jnp.int32, (1024, LANE), 0)
    return (r & (span - 1)) >> ((span // 8).bit_length() - 1)


def _bd_expand(name, kind, compact):
    span, cn = kind
    nq = compact.shape[0]

    def body(c_ref, o_ref):
        x = c_ref[...]
        grp = _bd_rowgroup(span)
        xcat = jnp.concatenate([jnp.where(grp == h, x, 0.0) for h in range(8)], axis=1)
        o_ref[...] = _bdot(xcat, _bd_perm(cn), _DIMS["nn"]).astype(o_ref.dtype)

    return pl.pallas_call(
        body, name=name, grid=(nq,), in_specs=[pl.BlockSpec((None, 1024, LANE), lambda q: (q, 0, 0))],
        out_specs=pl.BlockSpec((None, 1024, 1024), lambda q: (q, 0, 0)),
        out_shape=jax.ShapeDtypeStruct((nq, 1024, 1024), BF16),
        compiler_params=_cparams(("parallel",)),
    )(compact)


def _bd_reduce(name, kind, dbig):
    span, cn = kind
    nq = dbig.shape[0]

    def body(g_ref, o_ref):
        perm = _bd_perm(cn)
        hi, mid, lo = _split3(g_ref[...])
        d = _DIMS["nt"]
        back = _bdot(hi, perm, d) + _bdot(mid, perm, d) + _bdot(lo, perm, d)
        grp = _bd_rowgroup(span)
        out = jnp.zeros((1024, LANE), F32)
        for h in range(8):
            out = jnp.where(grp == h, back[:, h * LANE:(h + 1) * LANE], out)
        o_ref[...] = out

    return pl.pallas_call(
        body, name=name, grid=(nq,), in_specs=[pl.BlockSpec((None, 1024, 1024), lambda q: (q, 0, 0))],
        out_specs=pl.BlockSpec((None, 1024, LANE), lambda q: (q, 0, 0)),
        out_shape=jax.ShapeDtypeStruct((nq, 1024, LANE), F32),
        compiler_params=_cparams(("parallel",)),
    )(dbig)


def _x_tile_specs(nc, nq):
    return [pl.BlockSpec((nc, LANE), lambda q, t, i=i: (0, i * nq + q)) for i in range(SSM_CHUNK)]


def _cat_tiles(refs):
    return jnp.concatenate([r[...] for r in refs], axis=1)


def _ssm_w(name, x8, bw):
    nc = x8.shape[0]
    nq = bw.shape[0]

    def body(*refs):
        xq = _cat_tiles(refs[:8])
        refs[9][...] = _bdot(xq, refs[8][...], _DIMS["nn"])

    return pl.pallas_call(
        body, name=name, grid=(nq, 8),
        in_specs=_x_tile_specs(nc, nq) + [pl.BlockSpec((None, 1024, LANE), lambda q, t: (q, 0, t))],
        out_specs=pl.BlockSpec((None, None, nc, LANE), lambda q, t: (q, t, 0, 0)),
        out_shape=jax.ShapeDtypeStruct((nq, 8, nc, LANE), F32),
        compiler_params=_cparams(("parallel", "arbitrary")),
    )(*([x8] * 8), bw)


def _ssm_scan(name, w4, a_t, aseg_t, *, reverse, sprev4=None):
    nq, _, nc, _ = w4.shape
    ns = nc // 8
    with_da = sprev4 is not None

    def body(*refs):
        w_ref, a_ref, aseg_ref = refs[:3]
        s_ref = refs[3] if with_da else None
        o_ref = refs[4] if with_da else refs[3]
        da_ref = refs[5] if with_da else None
        sgn = -1.0 if reverse else 1.0
        ar = [a_ref[j] for j in range(4)]
        ai = [sgn * a_ref[j + 4] for j in range(4)]
        gr = [aseg_ref[j] for j in range(4)]
        gi = [sgn * aseg_ref[j + 4] for j in range(4)]
        zero = tuple(jnp.zeros((8, LANE), F32) for _ in range(8))

        def rows(tt):
            return pl.ds((ns - 1 - tt) if reverse else tt, 8, stride=ns)

        def step(carry, w):
            new_r = [ar[j] * carry[j] - ai[j] * carry[j + 4] + w[j] for j in range(4)]
            new_i = [ar[j] * carry[j + 4] + ai[j] * carry[j] + w[j + 4] for j in range(4)]
            return tuple(new_r + new_i)

        def pass1(tt, carry):
            return step(carry, [w_ref[j, rows(tt), :] for j in range(8)])

        ends = lax.fori_loop(0, ns, pass1, zero)
        sub = lax.broadcasted_iota(jnp.int32, (8, LANE), 0)
        init = list(zero)
        order = range(7, 0, -1) if reverse else range(0, 7)
        for s in order:
            nxt = s - 1 if reverse else s + 1
            cand_r = [gr[j] * init[j] - gi[j] * init[j + 4] + ends[j] for j in range(4)]
            cand_i = [gr[j] * init[j + 4] + gi[j] * init[j] + ends[j + 4] for j in range(4)]
            cand = cand_r + cand_i
            shift = 7 if reverse else 1
            init = [jnp.where(sub == nxt, pltpu.roll(cand[j], shift, axis=0), init[j]) for j in range(8)]

        def pass2(tt, state):
            carry, acc = state
            r = rows(tt)
            for j in range(8):
                o_ref[j, r, :] = carry[j]
            if with_da:
                sp = [s_ref[j, r, :] for j in range(8)]
                acc_r = [acc[j] + carry[j] * sp[j] + carry[j + 4] * sp[j + 4] for j in range(4)]
                acc_i = [acc[j + 4] + carry[j + 4] * sp[j] - carry[j] * sp[j + 4] for j in range(4)]
                acc = tuple(acc_r + acc_i)
            return step(carry, [w_ref[j, r, :] for j in range(8)]), acc

        _, acc = lax.fori_loop(0, ns, pass2, (tuple(init), zero))
        if with_da:
            for j in range(8):
                da_ref[j] = acc[j]

    big = pl.BlockSpec((None, 8, nc, LANE), lambda q: (q, 0, 0, 0))
    small = pl.BlockSpec((None, 8, 8, LANE), lambda q: (q, 0, 0, 0))
    in_specs = [big, small, small] + ([big] if with_da else [])
    ops = [w4, a_t, aseg_t] + ([sprev4] if with_da else [])
    out_specs = (big, small) if with_da else big
    big_s = jax.ShapeDtypeStruct((nq, 8, nc, LANE), F32)
    out_shape = (big_s, jax.ShapeDtypeStruct((nq, 8, 8, LANE), F32)) if with_da else big_s
    return pl.pallas_call(
        body, name=name, grid=(nq,), in_specs=in_specs, out_specs=out_specs, out_shape=out_shape,
        compiler_params=_cparams(("parallel",)),
    )(*ops)


def _ssm_y(name, x8, sprev4, m_mat, cm_mat):
    nc = x8.shape[0]
    nq = m_mat.shape[0]

    def body(*refs):
        xq = _cat_tiles(refs[:8])
        s_ref, m_ref, cm_ref, o_ref = refs[8:12]
        sq = jnp.concatenate([s_ref[t] for t in range(8)], axis=1)
        o_ref[...] = _bdot(xq, m_ref[...], _DIMS["nn"]) + _bdot(sq, cm_ref[...], _DIMS["nn"])

    col = pl.BlockSpec((None, 1024, LANE), lambda q, j: (q, 0, j))
    return pl.pallas_call(
        body, name=name, grid=(nq, 8),
        in_specs=_x_tile_specs(nc, nq) + [pl.BlockSpec((None, 8, nc, LANE), lambda q, j: (q, 0, 0, 0)), col, col],
        out_specs=pl.BlockSpec((nc, LANE), lambda q, j: (0, j * nq + q)),
        out_shape=jax.ShapeDtypeStruct((nc, 8 * SSM_WIDTH), F32),
        compiler_params=_cparams(("parallel", "arbitrary")),
    )(*([x8] * 8), sprev4, m_mat, cm_mat)


def _ssm_ds(name, dz8, sprev4, cm_mat):
    nc = dz8.shape[0]
    nq = cm_mat.shape[0]

    def body(*refs):
        dyq = _cat_tiles(refs[:8]).astype(BF16)
        s_ref, cm_ref, ds_ref, dcm_ref = refs[8:12]
        ds_ref[...] = _bdot(dyq, cm_ref[...], _DIMS["nt"])
        dcm_ref[...] = _bdot(s_ref[...], dyq, _DIMS["tn"])

    tile = pl.BlockSpec((None, None, nc, LANE), lambda q, t: (q, t, 0, 0))
    rowblk = pl.BlockSpec((None, LANE, 1024), lambda q, t: (q, t, 0))
    return pl.pallas_call(
        body, name=name, grid=(nq, 8),
        in_specs=_x_tile_specs(nc, nq) + [tile, rowblk],
        out_specs=(tile, rowblk),
        out_shape=(jax.ShapeDtypeStruct((nq, 8, nc, LANE), F32), jax.ShapeDtypeStruct((nq, 1024, 1024), F32)),
        compiler_params=_cparams(("parallel", "arbitrary")),
    )(*([dz8] * 8), sprev4, cm_mat)


def _ssm_dx(name, dz8, g4, x8, m_mat, bw_mat, d8):
    nc = dz8.shape[0]
    nq = m_mat.shape[0]

    def body(*refs):
        dyq = _cat_tiles(refs[:8]).astype(BF16)
        g_ref, x_ref, m_ref, bw_ref, d_ref, dzi_ref, dx_ref, dm_ref, dbw_ref = refs[8:17]
        gq = jnp.concatenate([g_ref[t] for t in range(8)], axis=1).astype(BF16)
        dx = _bdot(dyq, m_ref[...], _DIMS["nt"]) + _bdot(gq, bw_ref[...], _DIMS["nt"])
        dx_ref[...] = (dx + d_ref[...] * dzi_ref[...]).astype(dx_ref.dtype)
        xi = x_ref[...]
        dm_ref[...] = _bdot(xi, dyq, _DIMS["tn"])
        dbw_ref[...] = _bdot(xi, gq, _DIMS["tn"])

    xtile = pl.BlockSpec((nc, LANE), lambda q, i: (0, i * nq + q))
    rowblk = pl.BlockSpec((None, LANE, 1024), lambda q, i: (q, i, 0))
    return pl.pallas_call(
        body, name=name, grid=(nq, 8),
        in_specs=_x_tile_specs(nc, nq) + [pl.BlockSpec((None, 8, nc, LANE), lambda q, i: (q, 0, 0, 0)), xtile, rowblk, rowblk,
                                          pl.BlockSpec((1, LANE), lambda q, i: (0, q)), xtile],
        out_specs=(xtile, rowblk, rowblk),
        out_shape=(jax.ShapeDtypeStruct((nc, 8 * SSM_WIDTH), BF16), jax.ShapeDtypeStruct((nq, 1024, 1024), F32),
                   jax.ShapeDtypeStruct((nq, 1024, 1024), F32)),
        compiler_params=_cparams(("parallel", "arbitrary")),
    )(*([dz8] * 8), g4, x8, m_mat, bw_mat, d8, dz8)


CUM_BLK = 256


def _split3(x):
    hi = x.astype(BF16)
    r1 = x - hi.astype(F32)
    mid = r1.astype(BF16)
    lo = (r1 - mid.astype(F32)).astype(BF16)
    return hi, mid, lo


def _tri_dot(x, tri):
    hi, mid, lo = _split3(x)
    d = _DIMS["nn"]
    return _bdot(hi, tri, d) + _bdot(mid, tri, d) + _bdot(lo, tri, d)


def _tri(n, lower):
    r = lax.broadcasted_iota(jnp.int32, (n, n), 0)
    c = lax.broadcasted_iota(jnp.int32, (n, n), 1)
    return jnp.where((r >= c) if lower else (r <= c), 1.0, 0.0).astype(BF16)


def _fox_cum(name, fproj, bcol):
    seq = fproj.shape[0]
    blk = min(CUM_BLK, seq)

    def body(f_ref, b_ref, o_ref, carry_ref):
        i = pl.program_id(0)

        @pl.when(i == 0)
        def _():
            carry_ref[...] = jnp.zeros_like(carry_ref)

        z = f_ref[...].T + b_ref[...]
        logf = jnp.minimum(z, 0.0) - jnp.log(1.0 + jnp.exp(-jnp.abs(z)))
        carry = carry_ref[...]
        cum = _tri_dot(logf, _tri(blk, lower=False)) + jnp.tile(carry, (1, blk // LANE))
        o_ref[...] = cum[0:8, :]
        carry_ref[...] = carry + jnp.sum(logf, axis=1, keepdims=True)

    return pl.pallas_call(
        body, name=name, grid=(seq // blk,),
        in_specs=[pl.BlockSpec((blk, LANE), lambda i: (i, 0)), pl.BlockSpec((LANE, 1), lambda i: (0, 0))],
        out_specs=pl.BlockSpec((8, blk), lambda i: (0, i)),
        out_shape=jax.ShapeDtypeStruct((8, seq), F32),
        scratch_shapes=[pltpu.VMEM((LANE, LANE), F32)],
        compiler_params=_cparams(("arbitrary",)),
    )(fproj, bcol)


def _fox_cum_bwd(name, dcum_t, fproj, bcol):
    seq = fproj.shape[0]
    blk = min(CUM_BLK, seq)
    n = seq // blk

    def body(dc_ref, f_ref, b_ref, df_ref, db_ref, carry_ref, acc_ref):
        i = pl.program_id(0)

        @pl.when(i == 0)
        def _():
            carry_ref[...] = jnp.zeros_like(carry_ref)
            acc_ref[...] = jnp.zeros_like(acc_ref)

        dc = jnp.concatenate([dc_ref[...], jnp.zeros((LANE - 8, blk), F32)], axis=0)
        carry = carry_ref[...]
        dlogf = _tri_dot(dc, _tri(blk, lower=True)) + jnp.tile(carry, (1, blk // LANE))
        carry_ref[...] = carry + jnp.sum(dc, axis=1, keepdims=True)
        z = f_ref[...].T + b_ref[...]
        dft = dlogf / (1.0 + jnp.exp(z))
        df_ref[...] = dft.T.astype(df_ref.dtype)
        acc_ref[...] += jnp.sum(dft, axis=1, keepdims=True)

        @pl.when(i == n - 1)
        def _():
            db_ref[...] = acc_ref[...]

    return pl.pallas_call(
        body, name=name, grid=(n,),
        in_specs=[pl.BlockSpec((8, blk), lambda i: (0, n - 1 - i)), pl.BlockSpec((blk, LANE), lambda i: (n - 1 - i, 0)),
                  pl.BlockSpec((LANE, 1), lambda i: (0, 0))],
        out_specs=(pl.BlockSpec((blk, LANE), lambda i: (n - 1 - i, 0)), pl.BlockSpec((LANE, LANE), lambda i: (0, 0))),
        out_shape=(jax.ShapeDtypeStruct((seq, LANE), BF16), jax.ShapeDtypeStruct((LANE, LANE), F32)),
        scratch_shapes=[pltpu.VMEM((LANE, LANE), F32), pltpu.VMEM((LANE, LANE), F32)],
        compiler_params=_cparams(("arbitrary",)),
    )(dcum_t, fproj, bcol)


FOX_BLK = 512
FOX_SCALE = FOX_HEAD_DIM ** -0.5


def _fox_head_mask(shape, hh):
    lane = lax.broadcasted_iota(jnp.int32, shape, 1)
    return (lane < FOX_HEAD_DIM) if hh == 0 else (lane >= FOX_HEAD_DIM)


def _fox_bias(cum_ref, hh, q0, k0, blk):
    c0 = jnp.max(cum_ref[hh:hh + 1, pl.ds(q0, LANE)], axis=1, keepdims=True)
    return c0 - cum_ref[hh:hh + 1, pl.ds(k0, blk)]


def _fox_fwd(name, qkv, cum_t):
    seq = qkv.shape[0]
    blk = min(FOX_BLK, seq)
    nb = seq // blk
    npair = FOX_HEADS // 2

    def body(q_ref, k_ref, v_ref, cum_ref, o_ref, lse_ref):
        iq = pl.program_id(1)
        q0 = pl.multiple_of(iq * blk, blk)
        qv = q_ref[...]
        row = lax.broadcasted_iota(jnp.int32, (blk, blk), 0)
        col = lax.broadcasted_iota(jnp.int32, (blk, blk), 1)
        qhs = [jnp.where(_fox_head_mask(qv.shape, hh), qv, jnp.zeros_like(qv)) * FOX_SCALE for hh in range(2)]

        def block(kb, states, masked):
            k0 = pl.multiple_of(kb * blk, blk)
            kv = k_ref[pl.ds(k0, blk), :]
            vv = v_ref[pl.ds(k0, blk), :]
            new = []
            for hh in range(2):
                m, l, acc = states[hh]
                s = _bdot(qhs[hh], kv, _DIMS["nt"]) + _fox_bias(cum_ref, hh, q0, k0, blk)
                if masked:
                    s = jnp.where(row >= col, s, -jnp.inf)
                m_new = jnp.maximum(m, jnp.max(s, axis=1, keepdims=True))
                alpha = jnp.exp(m - m_new)
                p = jnp.exp(s - m_new)
                l = alpha * l + jnp.sum(p, axis=1, keepdims=True)
                acc = alpha * acc + _bdot(p, vv, _DIMS["nn"])
                new.append((m_new, l, acc))
            return tuple(new)

        init = (jnp.full((blk, 1), -jnp.inf, F32), jnp.zeros((blk, 1), F32), jnp.zeros((blk, LANE), F32))
        states = lax.fori_loop(0, iq, lambda kb, st: block(kb, st, False), (init, init))
        states = block(iq, states, True)
        outs = []
        for hh in range(2):
            m, l, acc = states[hh]
            outs.append(acc / l)
            lse_ref[hh] = jnp.broadcast_to(m + jnp.log(l), (blk, LANE))
        o_ref[...] = jnp.where(_fox_head_mask(outs[0].shape, 0), outs[0], outs[1]).astype(o_ref.dtype)

    return pl.pallas_call(
        body, name=name, grid=(npair, nb),
        in_specs=[pl.BlockSpec((blk, LANE), lambda p, i: (i, p)),
                  pl.BlockSpec((seq, LANE), lambda p, i: (0, npair + p)),
                  pl.BlockSpec((seq, LANE), lambda p, i: (0, 2 * npair + p)),
                  pl.BlockSpec((None, 2, seq), lambda p, i: (p, 0, 0))],
        out_specs=(pl.BlockSpec((blk, LANE), lambda p, i: (i, p)),
                   pl.BlockSpec((2, blk, LANE), lambda p, i: (p, i, 0))),
        out_shape=(jax.ShapeDtypeStruct((seq, FOX_WIDTH), BF16), jax.ShapeDtypeStruct((FOX_HEADS, seq, LANE), F32)),
        compiler_params=_cparams(("parallel", "arbitrary")),
    )(qkv, qkv, qkv, cum_t)


def _fox_bwd(name, qkv, cum_t, att, datt, lse):
    seq = qkv.shape[0]
    blk = min(FOX_BLK, seq)
    nb = seq // blk
    npair = FOX_HEADS // 2

    def body(q_ref, k_ref, v_ref, cum_ref, o_ref, do_ref, lse_ref, dq_ref, dk_ref, dv_ref, dcum_ref):
        iq = pl.program_id(1)
        q0 = pl.multiple_of(iq * blk, blk)

        @pl.when(iq == 0)
        def _():
            dk_ref[...] = jnp.zeros_like(dk_ref)
            dv_ref[...] = jnp.zeros_like(dv_ref)
            dcum_ref[...] = jnp.zeros_like(dcum_ref)

        qv = q_ref[...]
        dov = do_ref[...].astype(F32)
        ov = o_ref[...].astype(F32)
        row = lax.broadcasted_iota(jnp.int32, (blk, blk), 0)
        col = lax.broadcasted_iota(jnp.int32, (blk, blk), 1)
        qhs, dohbs, deltas, lses = [], [], [], []
        for hh in range(2):
            hm = _fox_head_mask(qv.shape, hh)
            qhs.append(jnp.where(hm, qv, jnp.zeros_like(qv)) * FOX_SCALE)
            doh = jnp.where(hm, dov, 0.0)
            dohbs.append(doh.astype(BF16))
            deltas.append(jnp.sum(doh * ov, axis=1, keepdims=True))
            lses.append(jnp.tile(lse_ref[hh], (1, blk // LANE)))

        def block(kb, accs, masked):
            k0 = pl.multiple_of(kb * blk, blk)
            kv = k_ref[pl.ds(k0, blk), :]
            vv = v_ref[pl.ds(k0, blk), :]
            new = []
            dk_blk = None
            dv_blk = None
            for hh in range(2):
                dq_acc, rs_acc = accs[hh]
                s = _bdot(qhs[hh], kv, _DIMS["nt"]) + _fox_bias(cum_ref, hh, q0, k0, blk)
                p = jnp.exp(s - lses[hh])
                if masked:
                    p = jnp.where(row >= col, p, 0.0)
                dp = _bdot(dohbs[hh], vv, _DIMS["nt"])
                ds = p * (dp - deltas[hh])
                dsb = ds.astype(BF16)
                dk_h = _bdot(dsb, qhs[hh], _DIMS["tn"])
                dv_h = _bdot(p, dohbs[hh], _DIMS["tn"])
                dk_blk = dk_h if dk_blk is None else dk_blk + dk_h
                dv_blk = dv_h if dv_blk is None else dv_blk + dv_h
                dcum_ref[hh:hh + 1, pl.ds(k0, blk)] -= jnp.sum(ds, axis=0, keepdims=True)
                new.append((dq_acc + _bdot(dsb, kv, _DIMS["nn"]), rs_acc + jnp.sum(ds, axis=1, keepdims=True)))
            dk_ref[pl.ds(k0, blk), :] += dk_blk
            dv_ref[pl.ds(k0, blk), :] += dv_blk
            return tuple(new)

        init = (jnp.zeros((blk, LANE), F32), jnp.zeros((blk, 1), F32))
        accs = lax.fori_loop(0, iq, lambda kb, a: block(kb, a, False), (init, init))
        accs = block(iq, accs, True)
        for hh in range(2):
            dcum_ref[hh:hh + 1, pl.ds(q0, blk)] += jnp.broadcast_to(accs[hh][1], (blk, LANE)).T[0:1, :]
        dq = jnp.where(_fox_head_mask(qv.shape, 0), accs[0][0], accs[1][0]) * FOX_SCALE
        dq_ref[...] = dq.astype(dq_ref.dtype)

    qblk = pl.BlockSpec((blk, LANE), lambda p, i: (i, p))
    full = pl.BlockSpec((seq, LANE), lambda p, i: (0, p))
    return pl.pallas_call(
        body, name=name, grid=(npair, nb),
        in_specs=[qblk,
                  pl.BlockSpec((seq, LANE), lambda p, i: (0, npair + p)),
                  pl.BlockSpec((seq, LANE), lambda p, i: (0, 2 * npair + p)),
                  pl.BlockSpec((None, 2, seq), lambda p, i: (p, 0, 0)),
                  qblk, qblk,
                  pl.BlockSpec((2, blk, LANE), lambda p, i: (p, i, 0))],
        out_specs=(qblk, full, full, pl.BlockSpec((None, 2, seq), lambda p, i: (p, 0, 0))),
        out_shape=(jax.ShapeDtypeStruct((seq, FOX_WIDTH), BF16), jax.ShapeDtypeStruct((seq, FOX_WIDTH), F32),
                   jax.ShapeDtypeStruct((seq, FOX_WIDTH), F32), jax.ShapeDtypeStruct((npair, 2, seq), F32)),
        compiler_params=_cparams(("arbitrary", "arbitrary")),
    )(qkv, qkv, qkv, cum_t, att, datt, lse)


MEM_SCALE = MEM_HEAD_DIM ** -0.5


def _mem_probs(qh, kh):
    s = _bdot(qh, kh, _DIMS["nt"]) * MEM_SCALE
    p = jnp.exp(s - jnp.max(s, axis=1, keepdims=True))
    return p / jnp.sum(p, axis=1, keepdims=True)


def _mem_fwd(name, q2, kv, *, tr=512):
    seq = q2.shape[0]
    mlen = kv.shape[0]
    tr = min(tr, seq)

    def body(q_ref, kv_ref, o_ref):
        for h in range(MEM_HEADS):
            sl = slice(h * MEM_HEAD_DIM, (h + 1) * MEM_HEAD_DIM)
            sv = slice(MEM_WIDTH + h * MEM_HEAD_DIM, MEM_WIDTH + (h + 1) * MEM_HEAD_DIM)
            p = _mem_probs(q_ref[:, sl], kv_ref[:, sl])
            o_ref[:, sl] = _bdot(p, kv_ref[:, sv], _DIMS["nn"]).astype(o_ref.dtype)

    return pl.pallas_call(
        body, name=name, grid=(seq // tr,),
        in_specs=[pl.BlockSpec((tr, MEM_WIDTH), lambda i: (i, 0)), pl.BlockSpec((mlen, 2 * MEM_WIDTH), lambda i: (0, 0))],
        out_specs=pl.BlockSpec((tr, MEM_WIDTH), lambda i: (i, 0)),
        out_shape=jax.ShapeDtypeStruct((seq, MEM_WIDTH), BF16),
        compiler_params=_cparams(("parallel",)),
    )(q2, kv)


def _mem_bwd(name, q2, kv, do2, *, tr=512):
    seq = q2.shape[0]
    mlen = kv.shape[0]
    tr = min(tr, seq)

    def body(q_ref, kv_ref, do_ref, dq_ref, dkv_ref):
        i = pl.program_id(0)

        @pl.when(i == 0)
        def _():
            dkv_ref[...] = jnp.zeros_like(dkv_ref)

        for h in range(MEM_HEADS):
            sl = slice(h * MEM_HEAD_DIM, (h + 1) * MEM_HEAD_DIM)
            sv = slice(MEM_WIDTH + h * MEM_HEAD_DIM, MEM_WIDTH + (h + 1) * MEM_HEAD_DIM)
            qh = q_ref[:, sl]
            kh = kv_ref[:, sl]
            doh = do_ref[:, sl].astype(BF16)
            p = _mem_probs(qh, kh)
            dp = _bdot(doh, kv_ref[:, sv], _DIMS["nt"])
            ds = (p * (dp - jnp.sum(p * dp, axis=1, keepdims=True)) * MEM_SCALE).astype(BF16)
            dq_ref[:, sl] = _bdot(ds, kh, _DIMS["nn"]).astype(dq_ref.dtype)
            dkv_ref[:, sl] += _bdot(ds, qh, _DIMS["tn"])
            dkv_ref[:, sv] += _bdot(p, doh, _DIMS["tn"])

    row = pl.BlockSpec((tr, MEM_WIDTH), lambda i: (i, 0))
    kvs = pl.BlockSpec((mlen, 2 * MEM_WIDTH), lambda i: (0, 0))
    return pl.pallas_call(
        body, name=name, grid=(seq // tr,), in_specs=[row, kvs, row], out_specs=(row, kvs),
        out_shape=(jax.ShapeDtypeStruct((seq, MEM_WIDTH), BF16), jax.ShapeDtypeStruct((mlen, 2 * MEM_WIDTH), F32)),
        compiler_params=_cparams(("arbitrary",)),
    )(q2, kv, do2)


_HBM = pl.BlockSpec(memory_space=pl.ANY)
_HBM_ONLY = pl.BlockSpec(memory_space=pltpu.HBM)
_MESH = pl.DeviceIdType.MESH


def _mesh_place():
    x, y, c = lax.axis_index("x"), lax.axis_index("y"), lax.axis_index("c")
    other_chips = [(1 - x, y), (x, 1 - y), (1 - x, 1 - y)]
    return x, y, c, other_chips


def _gather_all(name, arrays):
    n = len(arrays)

    def body(*refs):
        ins, outs = refs[:n], refs[n:2 * n]
        send_sems, recv_sems, local_sems = refs[2 * n:]
        x, y, c, chips = _mesh_place()
        me, sibling = (x, y, c), (x, y, 1 - c)

        def slot(a, place):
            px, py, pc = place
            return outs[a].at[4 * px + 2 * py + pc]

        def copy(a, k, block, to, src=None):
            return pltpu.make_async_remote_copy(
                src_ref=slot(a, block) if src is None else src, dst_ref=slot(a, block),
                send_sem=send_sems.at[a, k], recv_sem=recv_sems.at[a, k], device_id=to, device_id_type=_MESH)

        mine = [pltpu.make_async_copy(ins[a], slot(a, me), local_sems.at[a]) for a in range(n)]
        for cp in mine:
            cp.start()
        first = []
        for a in range(n):
            first.append(copy(a, 0, me, sibling, src=ins[a]))
            first += [copy(a, 1 + j, me, (*chip, c), src=ins[a]) for j, chip in enumerate(chips)]
        for cp in first:
            cp.start()
        passed = []
        for j, chip in enumerate(chips):
            for a in range(n):
                copy(a, 1 + j, (*chip, c), me).wait_recv()
                fwd = copy(a, 4 + j, (*chip, c), sibling)
                fwd.start()
                passed.append(fwd)
        for a in range(n):
            copy(a, 0, sibling, me).wait_recv()
            for j, chip in enumerate(chips):
                copy(a, 4 + j, (*chip, 1 - c), me).wait_recv()
        for cp in first + passed:
            cp.wait_send()
        for cp in mine:
            cp.wait()

    out_shape = tuple(jax.ShapeDtypeStruct((N_DEV,) + arr.shape, arr.dtype) for arr in arrays)
    return pl.pallas_call(
        body, name=name, in_specs=[_HBM] * n, out_specs=tuple([_HBM] * n), out_shape=out_shape,
        scratch_shapes=[pltpu.SemaphoreType.DMA((n, N_DEV - 1)), pltpu.SemaphoreType.DMA((n, N_DEV - 1)),
                        pltpu.SemaphoreType.DMA((n,))],
    )(*arrays)


_SEM = pl.BlockSpec(memory_space=pltpu.SEMAPHORE)
_DATAFLOW = pltpu.SideEffectType.DATAFLOW_SIDE_EFFECTING


def _device_index():
    return (4 * lax.axis_index("x") + 2 * lax.axis_index("y") + lax.axis_index("c")).astype(jnp.int32).reshape(1)


def _place_own(name, pieces, *, stacked_src, after=None):
    n = len(pieces)
    n_in = n + (after is not None)

    def body(me_ref, *refs):
        for a in range(n):
            refs[n_in + a][...] = refs[a][...]

    def spec(shape):
        return pl.BlockSpec((None,) + tuple(shape), lambda i, me_ref: (me_ref[0],) + (0,) * len(shape))

    shapes = [p.shape[1:] if stacked_src else p.shape for p in pieces]
    if stacked_src:
        in_specs = [spec(s) for s in shapes]
    else:
        in_specs = [pl.BlockSpec(tuple(s), lambda i, me_ref, nd=len(s): (0,) * nd) for s in shapes]
    operands = list(pieces)
    if after is not None:
        in_specs.append(_HBM)
        operands.append(after)
    return pl.pallas_call(
        body, name=name,
        grid_spec=pltpu.PrefetchScalarGridSpec(num_scalar_prefetch=1, grid=(1,), in_specs=in_specs,
                                               out_specs=tuple(spec(s) for s in shapes)),
        out_shape=tuple(jax.ShapeDtypeStruct((N_DEV,) + tuple(s), p.dtype) for s, p in zip(shapes, pieces)),
        compiler_params=_cparams(("arbitrary",)),
    )(_device_index(), *operands)


def _peer_places():
    x, y, c = lax.axis_index("x"), lax.axis_index("y"), lax.axis_index("c")
    peers = []
    for k in range(N_DEV - 1):
        flip = k + 1
        px = 1 - x if flip & 4 else x
        py = 1 - y if flip & 2 else y
        pc = 1 - c if flip & 1 else c
        peers.append((px, py, pc, 4 * px + 2 * py + pc))
    return 4 * x + 2 * y + c, peers


def _direct_copy(srcs, lands, send_sems, recv_sems, a, k, me, peer, scatter):
    px, py, pc, pidx = peer
    return pltpu.make_async_remote_copy(
        src_ref=srcs[a].at[pidx] if scatter else srcs[a], dst_ref=lands[a].at[me],
        send_sem=send_sems.at[a * (N_DEV - 1) + k], recv_sem=recv_sems.at[a * (N_DEV - 1) + k],
        device_id=(px, py, pc), device_id_type=_MESH)


def _send_start(name, srcs, lands, *, scatter):
    n = len(srcs)

    def body(*refs):
        src_refs, land_refs = refs[:n], refs[n:2 * n]
        send_sems, recv_sems = refs[2 * n], refs[2 * n + 1]
        token = refs[-1]
        me, peers = _peer_places()
        for k, peer in enumerate(peers):
            for a in range(n):
                _direct_copy(src_refs, land_refs, send_sems, recv_sems, a, k, me, peer, scatter).start()
        token[...] = jnp.zeros_like(token)

    hbm_shapes = [pltpu.HBM(t.shape, t.dtype) for t in list(srcs) + list(lands)]
    outs = pl.pallas_call(
        body, name=name,
        out_shape=(pltpu.SemaphoreType.DMA((n * (N_DEV - 1),)), pltpu.SemaphoreType.DMA((n * (N_DEV - 1),)), *hbm_shapes,
                   jax.ShapeDtypeStruct((8, LANE), F32)),
        in_specs=[_HBM_ONLY] * (2 * n),
        out_specs=(_SEM, _SEM, *([_HBM_ONLY] * (2 * n)), pl.BlockSpec(memory_space=pltpu.VMEM)),
        input_output_aliases={i: 2 + i for i in range(2 * n)},
        compiler_params=pltpu.CompilerParams(has_side_effects=_DATAFLOW),
    )(*[pltpu.with_memory_space_constraint(t, pltpu.HBM) for t in list(srcs) + list(lands)])
    return outs[0], outs[1], outs[2:2 + n], outs[2 + n:2 + 2 * n], outs[-1]


def _send_wait(name, send_sems, recv_sems, srcs, lands, after, *, scatter):
    n = len(srcs)

    def body(*refs):
        src_refs, land_refs = refs[:n], refs[n:2 * n]
        send_sems, recv_sems = refs[2 * n], refs[2 * n + 1]
        me, peers = _peer_places()
        for k, peer in enumerate(peers):
            for a in range(n):
                cp = _direct_copy(src_refs, land_refs, send_sems, recv_sems, a, k, me, peer, scatter)
                cp.wait_send()
                cp.wait_recv()

    hbm_shapes = [pltpu.HBM(t.shape, t.dtype) for t in list(srcs) + list(lands)]
    outs = pl.pallas_call(
        body, name=name, out_shape=tuple(hbm_shapes),
        in_specs=[_HBM_ONLY] * (2 * n) + [_SEM, _SEM, _HBM],
        out_specs=tuple([_HBM_ONLY] * (2 * n)),
        input_output_aliases={i: i for i in range(2 * n)},
        compiler_params=pltpu.CompilerParams(has_side_effects=_DATAFLOW),
    )(*srcs, *lands, send_sems, recv_sems, after)
    return outs[n:]


def _scatter_sibling(name, arrays):
    n = len(arrays)

    def body(*refs):
        ins, sibs = refs[:n], refs[n:2 * n]
        send_sems, recv_sems = refs[2 * n:]
        x, y, c, _ = _mesh_place()
        copies = []
        for a in range(n):
            for j in range(4):
                rdma = pltpu.make_async_remote_copy(
                    src_ref=ins[a].at[2 * j + (1 - c)], dst_ref=sibs[a].at[j], send_sem=send_sems.at[a, j],
                    recv_sem=recv_sems.at[a, j], device_id=(x, y, 1 - c), device_id_type=_MESH)
                rdma.start()
                copies.append(rdma)
        for cp in copies:
            cp.wait()

    four = tuple(jax.ShapeDtypeStruct((4,) + arr.shape[1:], arr.dtype) for arr in arrays)
    return pl.pallas_call(
        body, name=name, in_specs=[_HBM] * n, out_specs=tuple([_HBM] * n), out_shape=four,
        scratch_shapes=[pltpu.SemaphoreType.DMA((n, 4)), pltpu.SemaphoreType.DMA((n, 4))],
    )(*arrays)


def _add_chip_partials(name, pieces, sibs):
    n = len(pieces)
    core = lax.axis_index("c").astype(jnp.int32).reshape(1)

    def body(c_ref, *refs):
        for a in range(n):
            out = refs[2 * n + a]
            out[...] = (refs[a][...].astype(F32) + refs[n + a][...].astype(F32)).astype(out.dtype)

    own_specs = [pl.BlockSpec((None,) + arr.shape[1:], lambda j, c_ref: (2 * j + c_ref[0], 0, 0)) for arr in pieces]
    four_specs = [pl.BlockSpec((None,) + arr.shape[1:], lambda j, c_ref: (j, 0, 0)) for arr in sibs]
    return pl.pallas_call(
        body, name=name,
        grid_spec=pltpu.PrefetchScalarGridSpec(num_scalar_prefetch=1, grid=(4,), in_specs=own_specs + four_specs,
                                               out_specs=tuple(four_specs)),
        out_shape=tuple(jax.ShapeDtypeStruct(arr.shape, arr.dtype) for arr in sibs),
        compiler_params=_cparams(("parallel",)),
    )(core, *pieces, *sibs)


def _scatter_chips(name, arrays):
    n = len(arrays)

    def body(*refs):
        ins, outs = refs[:n], refs[n:2 * n]
        send_sems, recv_sems, local_sems = refs[2 * n:]
        x, y, c, chips = _mesh_place()
        my_chip = 2 * x + y
        copies = []
        for a in range(n):
            local = pltpu.make_async_copy(ins[a].at[my_chip], outs[a].at[my_chip], local_sems.at[a])
            local.start()
            copies.append(local)
            for k, (px, py) in enumerate(chips):
                rdma = pltpu.make_async_remote_copy(
                    src_ref=ins[a].at[2 * px + py], dst_ref=outs[a].at[my_chip], send_sem=send_sems.at[a, k],
                    recv_sem=recv_sems.at[a, k], device_id=(px, py, c), device_id_type=_MESH)
                rdma.start()
                copies.append(rdma)
        for cp in copies:
            cp.wait()

    return pl.pallas_call(
        body, name=name, in_specs=[_HBM] * n, out_specs=tuple([_HBM] * n),
        out_shape=tuple(jax.ShapeDtypeStruct(arr.shape, arr.dtype) for arr in arrays),
        scratch_shapes=[pltpu.SemaphoreType.DMA((n, 3)), pltpu.SemaphoreType.DMA((n, 3)),
                        pltpu.SemaphoreType.DMA((n,))],
    )(*arrays)


def _unstack_cols(name, stacked):
    n, rows, cols = stacked.shape

    def body(i_ref, o_ref):
        o_ref[...] = i_ref[...]

    return pl.pallas_call(
        body, name=name, grid=(n,), in_specs=[pl.BlockSpec((None, rows, cols), lambda k: (k, 0, 0))],
        out_specs=pl.BlockSpec((rows, cols), lambda k: (0, k)),
        out_shape=jax.ShapeDtypeStruct((rows, n * cols), stacked.dtype),
        compiler_params=_cparams(("parallel",)),
    )(stacked)


def _restack_cols(name, mat):
    rows, width = mat.shape
    cols = width // N_DEV

    def body(i_ref, o_ref):
        o_ref[...] = i_ref[...]

    return pl.pallas_call(
        body, name=name, grid=(N_DEV,), in_specs=[pl.BlockSpec((rows, cols), lambda k: (0, k))],
        out_specs=pl.BlockSpec((None, rows, cols), lambda k: (k, 0, 0)),
        out_shape=jax.ShapeDtypeStruct((N_DEV, rows, cols), mat.dtype),
        compiler_params=_cparams(("parallel",)),
    )(mat)


def _remap_pieces(runs):
    plan = {}
    for du, dc, su, sc, ln in runs:
        while ln > 0:
            lane = dc % LANE
            take = min(ln, LANE - lane)
            plan.setdefault((du, dc // LANE), []).append((su, sc, take, lane))
            dc, sc, ln = dc + take, sc + take, ln - take
    return plan


def _remap(name, srcs, src_units, runs, *, out_units, out_cols, out_dtype, tr=256):
    rows = srcs[0].shape[-2]
    tr = min(tr, rows)
    plan = _remap_pieces(runs)
    n_src = len(srcs)
    stacked_out = out_units is not None
    n_tiles = out_cols // LANE

    def body(*refs):
        o_ref = refs[n_src]

        def src_tile(unit, t):
            ai, lead = src_units[unit]
            ref = refs[ai]
            sl = slice(t * LANE, (t + 1) * LANE)
            return (ref[:, sl] if lead is None else ref[lead, :, sl]).astype(F32)

        lane = lax.broadcasted_iota(jnp.int32, (tr, LANE), 1)
        for du in range(out_units if stacked_out else 1):
            for t in range(n_tiles):
                acc = jnp.zeros((tr, LANE), F32)
                for su, sc, ln, dl in plan.get((du if stacked_out else None, t), []):
                    st, so = sc // LANE, sc % LANE
                    first = src_tile(su, st)
                    if so == dl and so + ln <= LANE:
                        piece = first
                    else:
                        second = src_tile(su, st + 1) if so + ln > LANE else first
                        both = jnp.concatenate([first, second], axis=1)
                        piece = pltpu.roll(both, (dl - so) % (2 * LANE), axis=1)[:, 0:LANE]
                    acc = piece if (dl == 0 and ln == LANE) else jnp.where(
                        jnp.logical_and(lane >= dl, lane < dl + ln), piece, acc)
                if stacked_out:
                    o_ref[du, :, t * LANE:(t + 1) * LANE] = acc.astype(o_ref.dtype)
                else:
                    o_ref[:, t * LANE:(t + 1) * LANE] = acc.astype(o_ref.dtype)

    in_specs = []
    for arr in srcs:
        if arr.ndim == 2:
            in_specs.append(pl.BlockSpec((tr, arr.shape[1]), lambda i: (i, 0)))
        else:
            in_specs.append(pl.BlockSpec((arr.shape[0], tr, arr.shape[2]), lambda i: (0, i, 0)))
    if stacked_out:
        out_spec = pl.BlockSpec((out_units, tr, out_cols), lambda i: (0, i, 0))
        out_shape = jax.ShapeDtypeStruct((out_units, rows, out_cols), out_dtype)
    else:
        out_spec = pl.BlockSpec((tr, out_cols), lambda i: (i, 0))
        out_shape = jax.ShapeDtypeStruct((rows, out_cols), out_dtype)
    return pl.pallas_call(
        body, name=name, grid=(rows // tr,), in_specs=in_specs, out_specs=out_spec, out_shape=out_shape,
        compiler_params=_cparams(("parallel",)),
    )(*srcs)


def _proj_col(c):
    if c < PROJ_GATE0:
        return c
    if c < PROJ_GATE0 + FOX_HEADS:
        return PROJ_F0 + (c - PROJ_GATE0)
    return c - FOX_HEADS


def _win_runs():
    cuts = sorted(set([0, PROJ_GATE0, PROJ_GATE0 + FOX_HEADS, IN_WIDTH] + [SHARD_IN * k for k in range(N_DEV + 1)]))
    return [(lo // SHARD_IN, lo % SHARD_IN, _proj_col(lo), hi - lo) for lo, hi in zip(cuts[:-1], cuts[1:])]


def _assemble_win(name, stacked):
    runs = [(None, pc, k, sc, ln) for k, sc, pc, ln in _win_runs()]
    return _remap(name, [stacked], [(0, k) for k in range(N_DEV)], runs,
                  out_units=None, out_cols=PROJ_WIDTH, out_dtype=BF16)


def _disassemble_dwin(name, dw):
    runs = [(k, sc, 0, pc, ln) for k, sc, pc, ln in _win_runs()]
    return _remap(name, [dw], [(0, None)], runs, out_units=N_DEV, out_cols=SHARD_IN_PAD, out_dtype=BF16)


def _concat_cols(name, parts, *, tr=512):
    rows = parts[0].shape[0]
    tr = min(tr, rows)
    widths = [p.shape[1] for p in parts]
    total = sum(widths)

    def body(*refs):
        o_ref = refs[len(parts)]
        lo = 0
        for r, w in zip(refs[:len(parts)], widths):
            o_ref[:, lo:lo + w] = r[...].astype(o_ref.dtype)
            lo += w

    return pl.pallas_call(
        body, name=name, grid=(rows // tr,),
        in_specs=[pl.BlockSpec((tr, w), lambda i: (i, 0)) for w in widths],
        out_specs=pl.BlockSpec((tr, total), lambda i: (i, 0)),
        out_shape=jax.ShapeDtypeStruct((rows, total), BF16),
        compiler_params=_cparams(("parallel",)),
    )(*parts)


FFN_BLK = FFN_HIDDEN // 2


def _ffn_col(c):
    half, r = divmod(c, FFN_HIDDEN)
    blk, r = divmod(r, FFN_BLK)
    return blk * 2 * FFN_BLK + half * FFN_BLK + r


def _assemble_wffn(name, stacked):
    runs = [(None, _ffn_col(SHARD_FFN * k), k, 0, SHARD_FFN) for k in range(N_DEV)]
    return _remap(name, [stacked], [(0, k) for k in range(N_DEV)], runs,
                  out_units=None, out_cols=2 * FFN_HIDDEN, out_dtype=BF16)


def _disassemble_dwffn(name, dw):
    runs = [(k, 0, 0, _ffn_col(SHARD_FFN * k), SHARD_FFN) for k in range(N_DEV)]
    return _remap(name, [dw], [(0, None)], runs, out_units=N_DEV, out_cols=SHARD_FFN_PAD, out_dtype=BF16)


def _ffn_in_swiglu(name, xn, w, *, tm=512):
    rows, k = xn.shape
    tm = min(tm, rows)
    nblk = FFN_HIDDEN // FFN_BLK

    def body(x_ref, w_ref, f_ref, g_ref):
        f = _bdot(x_ref[...], w_ref[...], _DIMS["nn"])
        f_ref[...] = f
        fa = f[:, 0:FFN_BLK]
        g_ref[...] = (fa * _sigmoid(fa) * f[:, FFN_BLK:2 * FFN_BLK]).astype(g_ref.dtype)

    return pl.pallas_call(
        body, name=name, grid=(nblk, rows // tm),
        in_specs=[pl.BlockSpec((tm, k), lambda j, i: (i, 0)), pl.BlockSpec((k, 2 * FFN_BLK), lambda j, i: (0, j))],
        out_specs=(pl.BlockSpec((tm, 2 * FFN_BLK), lambda j, i: (i, j)), pl.BlockSpec((tm, FFN_BLK), lambda j, i: (i, j))),
        out_shape=(jax.ShapeDtypeStruct((rows, 2 * FFN_HIDDEN), F32), jax.ShapeDtypeStruct((rows, FFN_HIDDEN), BF16)),
        compiler_params=_cparams(("parallel", "arbitrary")),
    )(xn, w)


def _d_ffn_out_swiglu(name, dh, w_out, f, *, tm=512):
    rows, d = dh.shape
    tm = min(tm, rows)
    nblk = FFN_HIDDEN // FFN_BLK

    def body(dh_ref, w_ref, f_ref, df_ref):
        dg = _bdot(dh_ref[...], w_ref[...], _DIMS["nt"])
        fa = f_ref[:, 0:FFN_BLK]
        fb = f_ref[:, FFN_BLK:2 * FFN_BLK]
        s = _sigmoid(fa)
        df_ref[:, 0:FFN_BLK] = (dg * fb * s * (1.0 + fa * (1.0 - s))).astype(df_ref.dtype)
        df_ref[:, FFN_BLK:2 * FFN_BLK] = (dg * fa * s).astype(df_ref.dtype)

    wide = pl.BlockSpec((tm, 2 * FFN_BLK), lambda j, i: (i, j))
    return pl.pallas_call(
        body, name=name, grid=(nblk, rows // tm),
        in_specs=[pl.BlockSpec((tm, d), lambda j, i: (i, 0)), pl.BlockSpec((FFN_BLK, d), lambda j, i: (j, 0)), wide],
        out_specs=wide, out_shape=jax.ShapeDtypeStruct((rows, 2 * FFN_HIDDEN), BF16),
        compiler_params=_cparams(("parallel", "arbitrary")),
    )(dh, w_out, f)


def _adamw(name, parts, w, m, v, *, tr=128):
    rows, cols = w.shape
    n_parts = parts.shape[0]
    tr = min(tr, rows)
    assert rows % tr == 0, (name, rows, tr)
    c1 = 1.0 - ADAM_B1 ** ADAM_STEP
    c2 = 1.0 - ADAM_B2 ** ADAM_STEP

    def body(p_ref, w_ref, m_ref, v_ref, g_ref, d_ref, nm_ref, nv_ref):
        g = p_ref[0].astype(F32)
        for s in range(1, n_parts):
            g = g + p_ref[s].astype(F32)
        m_new = ADAM_B1 * m_ref[...] + (1.0 - ADAM_B1) * g
        v_new = ADAM_B2 * v_ref[...] + (1.0 - ADAM_B2) * (g * g)
        upd = (m_new / c1) / (jnp.sqrt(v_new / c2) + ADAM_EPS) + ADAM_WD * w_ref[...]
        g_ref[...] = g
        d_ref[...] = -ADAM_LR * upd
        nm_ref[...] = m_new
        nv_ref[...] = v_new

    row = pl.BlockSpec((tr, cols), lambda i: (i, 0))
    out = jax.ShapeDtypeStruct((rows, cols), F32)
    return pl.pallas_call(
        body, name=name, grid=(rows // tr,),
        in_specs=[pl.BlockSpec((n_parts, tr, cols), lambda i: (0, i, 0)), row, row, row],
        out_specs=(row, row, row, row), out_shape=(out, out, out, out),
        compiler_params=_cparams(("parallel",)),
    )(parts, w, m, v)


_WEIGHTS = ("norm_mix", "w_in", "b_forget", "lam_re", "lam_im", "log_dt", "b_re", "b_im", "c_re", "c_im",
            "d_skip", "w_glu", "w_fox_o", "w_mix_out", "norm_mem_q", "norm_mem_kv", "w_mem_q", "w_mem_kv",
            "w_mem_o", "norm_ffn", "w_ffn_in", "w_ffn_out", "norm_final")
_SHARDED = ("w_in", "w_glu", "w_fox_o", "w_mix_out", "w_mem_q", "w_mem_kv", "w_mem_o", "w_ffn_in", "w_ffn_out")
_SMALL = tuple(n for n in _WEIGHTS if n not in _SHARDED)
_PACK_COLS = 1024


def _pack(arrays):
    flat = jnp.concatenate([a.reshape(-1).astype(F32) for a in arrays])
    rows = -(-flat.shape[0] // _PACK_COLS)
    return jnp.pad(flat, (0, rows * _PACK_COLS - flat.shape[0])).reshape(rows, _PACK_COLS)


def _unpack(buf, like):
    flat = buf.reshape(-1)
    out, pos = [], 0
    for a in like:
        out.append(flat[pos:pos + a.size].reshape(a.shape))
        pos += a.size
    return out


def _mm(name, a, b, mode, m, n, k, out_dtype, tm=1024, tn=512, tk=1024, **kw):
    return _matmul(name, a, b, mode, m, n, k, out_dtype=out_dtype, tm=tm, tn=tn, tk=tk, **kw)


def kernel(x, mem, norm_mix, w_in, b_forget, lam_re, lam_im, log_dt, b_re, b_im, c_re, c_im, d_skip, w_glu, w_fox_o, w_mix_out, norm_mem_q, norm_mem_kv, w_mem_q, w_mem_kv, w_mem_o, norm_ffn, w_ffn_in, w_ffn_out, norm_final, loss_target, m_norm_mix, m_w_in, m_b_forget, m_lam_re, m_lam_im, m_log_dt, m_b_re, m_b_im, m_c_re, m_c_im, m_d_skip, m_w_glu, m_w_fox_o, m_w_mix_out, m_norm_mem_q, m_norm_mem_kv, m_w_mem_q, m_w_mem_kv, m_w_mem_o, m_norm_ffn, m_w_ffn_in, m_w_ffn_out, m_norm_final, v_norm_mix, v_w_in, v_b_forget, v_lam_re, v_lam_im, v_log_dt, v_b_re, v_b_im, v_c_re, v_c_im, v_d_skip, v_w_glu, v_w_fox_o, v_w_mix_out, v_norm_mem_q, v_norm_mem_kv, v_w_mem_q, v_w_mem_kv, v_w_mem_o, v_norm_ffn, v_w_ffn_in, v_w_ffn_out, v_norm_final):
    given = dict(locals())
    weights = {n: given[n] for n in _WEIGHTS}
    mom_m = {n: given["m_" + n] for n in _WEIGHTS}
    mom_v = {n: given["v_" + n] for n in _WEIGHTS}
    seq = x.shape[1]
    nc = seq // SSM_CHUNK
    d = D_MODEL
    xs, mems, tgt = x[0], mem[0], loss_target[0]

    def padcols(a, width):
        return jnp.pad(a, ((0, 0), (0, width - a.shape[1])))

    shards = [padcols(w_in[0].astype(BF16), SHARD_IN_PAD), w_glu[0].astype(BF16), w_fox_o[0].astype(BF16),
              w_mix_out[0].astype(BF16), w_mem_q[0].astype(BF16), w_mem_kv[0].astype(BF16),
              w_mem_o[0].astype(BF16), padcols(w_ffn_in[0].astype(BF16), SHARD_FFN_PAD), w_ffn_out[0].astype(BF16)]
    win = _assemble_win("assemble_w_in", _gather_all("gather_w_in", shards[:1])[0])
    rest = shards[1:]
    gsend, grecv, rest_thru, lands, gtoken = _send_start(
        "gather_rest_start", rest, _place_own("place_weight_shards", rest, stacked_src=False, after=win), scatter=False)

    u = _rms_fwd("rms_mix", xs, norm_mix, after=gtoken)
    ussm = _mm("proj_ssm", u, win, "nn", seq, SSM_WIDTH, d, F32)
    qkv = _mm("proj_qkv", u, win, "nn", seq, 3 * FOX_WIDTH, d, BF16, tn=512, b_off=(0, SSM_WIDTH))
    gates = _mm("proj_gates", u, win, "nn", seq, 2 * d, d, F32, tn=1024, b_off=(0, PROJ_GATE0))
    fproj = _mm("proj_forget", u, win, "nn", seq, LANE, d, F32, tn=LANE, b_off=(0, PROJ_F0))

    ssm_params = (lam_re[0], lam_im[0], log_dt[0], b_re[0], b_im[0], c_re[0], c_im[0])
    (m_c, bw_c, cm_c, a8, aseg), mats_vjp = jax.vjp(lambda *p: _ssm_mats(*p, nc), *ssm_params)
    m_b = _bd_expand("ssm_expand_m", _BD_M, m_c)
    bw_b = _bd_expand("ssm_expand_bw", _BD_BW, bw_c)
    cm_b = _bd_expand("ssm_expand_cm", _BD_CM, cm_c)
    u8 = ussm.reshape(nc, SSM_CHUNK * SSM_WIDTH)
    d8 = jnp.tile(d_skip, (1, SSM_CHUNK))
    w4 = _ssm_w("ssm_w", u8, bw_b)
    sp4 = _ssm_scan("ssm_scan", w4, a8, aseg, reverse=False)
    y8 = _ssm_y("ssm_y", u8, sp4, m_b, cm_b)
    act = _ssm_post_fwd("ssm_act", y8, u8, d8).reshape(seq, SSM_WIDTH)

    bcol = jnp.pad(b_forget[0], (0, LANE - FOX_HEADS)).reshape(LANE, 1)
    cum_t = _fox_cum("fox_cum", fproj, bcol).reshape(FOX_HEADS // 2, 2, seq)
    att, lse = _fox_fwd("fox_fwd", qkv, cum_t)

    gathered = _send_wait("gather_rest_wait", gsend, grecv, rest_thru, lands, att, scatter=False)
    wglu = _unstack_cols("unstack_w_glu", gathered[0])
    wfoxo = _unstack_cols("unstack_w_fox_o", gathered[1])
    wmix = gathered[2].reshape(d, d)
    wmq = gathered[3].reshape(d, MEM_WIDTH)
    wmkv = gathered[4].reshape(d, 2 * MEM_WIDTH)
    wmo = _unstack_cols("unstack_w_mem_o", gathered[5])
    wffn_in = _assemble_wffn("assemble_w_ffn_in", gathered[6])
    wffn_out = gathered[7].reshape(FFN_HIDDEN, d)

    glu = _mm("glu", act, wglu, "nn", seq, 2 * d, SSM_WIDTH, F32, tn=1024)
    out_b = _mm("fox_out", att, wfoxo, "nn", seq, d, FOX_WIDTH, F32, tn=1024)

    mixin = _mix_fwd("mix", glu, gates, out_b)
    h1 = _mm("mix_out", mixin, wmix, "nn", seq, d, d, F32, tn=1024, add=xs)

    n1 = _rms_fwd("rms_mem_q", h1, norm_mem_q)
    q2 = _mm("mem_q", n1, wmq, "nn", seq, MEM_WIDTH, d, BF16)
    mn = _rms_fwd("rms_mem_kv", mems, norm_mem_kv)
    mlen = mems.shape[0]
    kv = _mm("mem_kv", mn, wmkv, "nn", mlen, 2 * MEM_WIDTH, d, BF16)
    o2 = _mem_fwd("mem_attn", q2, kv)
    h2 = _mm("mem_out", o2, wmo, "nn", seq, d, MEM_WIDTH, F32, tn=1024, add=h1)

    n2 = _rms_fwd("rms_ffn", h2, norm_ffn)
    f, g_act = _ffn_in_swiglu("ffn_in_swiglu", n2, wffn_in)
    h3 = _mm("ffn_out", g_act, wffn_out, "nn", seq, d, FFN_HIDDEN, F32, tk=FFN_HIDDEN, add=h2)
    loss_part, dh3, dg_final = _final_loss("final_loss", h3, tgt, norm_final.reshape(1, d))

    df = _d_ffn_out_swiglu("d_ffn_out_swiglu", dh3, wffn_out, f)
    dwffn_out = _mm("d_ffn_out_w", g_act, dh3, "tn", FFN_HIDDEN, d, seq, BF16, tm=1408, tn=1024)
    dn2 = _mm("d_ffn_in_x", df, wffn_in, "nt", seq, d, 2 * FFN_HIDDEN, F32, tn=1024, tk=FFN_HIDDEN)
    dwffn_in = _mm("d_ffn_in_w", n2, df, "tn", d, 2 * FFN_HIDDEN, seq, BF16, tn=1408)
    dh2, dg_ffn = _rms_bwd("d_rms_ffn", dn2, h2, norm_ffn, res=dh3)

    do2 = _mm("d_mem_out_x", dh2, wmo, "nt", seq, MEM_WIDTH, d, F32)
    dwmo = _restack_cols("restack_d_w_mem_o", _mm("d_mem_out_w", o2, dh2, "tn", MEM_WIDTH, d, seq, BF16, tn=1024))
    dq2, dkv = _mem_bwd("d_mem_attn", q2, kv, do2)
    dwmq = _mm("d_mem_q_w", n1, dq2, "tn", d, MEM_WIDTH, seq, BF16)
    dn1 = _mm("d_mem_q_x", dq2, wmq, "nt", seq, d, MEM_WIDTH, F32)
    dwmkv = _mm("d_mem_kv_w", mn, dkv, "tn", d, 2 * MEM_WIDTH, mlen, BF16, tn=1024)
    dmn = _mm("d_mem_kv_x", dkv, wmkv, "nt", mlen, d, 2 * MEM_WIDTH, F32)
    _, dg_memkv = _rms_bwd("d_rms_mem_kv", dmn, mems, norm_mem_kv)

    early = [dwmq.reshape(N_DEV, d // N_DEV, MEM_WIDTH), dwmkv.reshape(N_DEV, d // N_DEV, 2 * MEM_WIDTH), dwmo,
             _disassemble_dwffn("split_d_w_ffn_in", dwffn_in), dwffn_out.reshape(N_DEV, FFN_HIDDEN // N_DEV, d)]
    ssend, srecv, early_thru, early_lands, stoken = _send_start(
        "scatter_early_start", early, _place_own("place_early_grads", early, stacked_src=True), scatter=True)
    dh1, dg_memq = _rms_bwd("d_rms_mem_q", dn1, h1, norm_mem_q, res=dh2, after=stoken)

    dmixin = _mm("d_mix_out_x", dh1, wmix, "nt", seq, d, d, F32, tn=1024)
    dwmix = _mm("d_mix_out_w", mixin, dh1, "tn", d, d, seq, BF16, tn=1024)
    dglu, dgates, dout_b = _mix_bwd("d_mix", dmixin, glu, gates, out_b)
    datt = _mm("d_fox_out_x", dout_b, wfoxo, "nt", seq, FOX_WIDTH, d, F32)
    dwfoxo = _restack_cols("restack_d_w_fox_o", _mm("d_fox_out_w", att, dout_b, "tn", FOX_WIDTH, d, seq, BF16, tn=1024))
    dact = _mm("d_glu_x", dglu, wglu, "nt", seq, SSM_WIDTH, 2 * d, F32, tk=2 * d)
    dwglu = _restack_cols("restack_d_w_glu", _mm("d_glu_w", act, dglu, "tn", SSM_WIDTH, 2 * d, seq, BF16, tn=2 * d))

    mid = [dwglu, dwfoxo, dwmix.reshape(N_DEV, d // N_DEV, d)]
    msend, mrecv, mid_thru, mid_lands, mtoken = _send_start(
        "scatter_mid_start", mid, _place_own("place_mid_grads", mid, stacked_src=True), scatter=True)

    dz8, dg_dskip = _ssm_post_bwd("d_ssm_act", dact.reshape(nc, SSM_CHUNK * SSM_WIDTH), y8, u8, d8, after=mtoken)
    ds4, dcm = _ssm_ds("d_ssm_y_state", dz8, sp4, cm_b)
    g4, da8 = _ssm_scan("d_ssm_scan", ds4, a8, aseg, reverse=True, sprev4=sp4)
    dx8, dm, dbw = _ssm_dx("d_ssm_x", dz8, g4, u8, m_b, bw_b, d8)
    dussm = dx8.reshape(seq, SSM_WIDTH)
    g_ssm = mats_vjp((_bd_reduce("ssm_reduce_dm", _BD_M, dm), _bd_reduce("ssm_reduce_dbw", _BD_BW, dbw),
                      _bd_reduce("ssm_reduce_dcm", _BD_CM, dcm), da8, jnp.zeros_like(aseg)))

    dq, dk, dv, dcum = _fox_bwd("d_fox", qkv, cum_t, att, datt, lse)
    dfproj, dbf = _fox_cum_bwd("d_fox_cum", dcum.reshape(FOX_HEADS, seq), fproj, bcol)
    dg_bforget = dbf[0:FOX_HEADS, 0].reshape(1, FOX_HEADS)

    dproj = _concat_cols("d_proj_concat", (dussm, dq, dk, dv, dgates, dfproj))
    du = _mm("d_proj_x", dproj, win, "nt", seq, d, PROJ_WIDTH, F32, tn=1024, tk=1408)
    dwin = _mm("d_proj_w", u, dproj, "tn", d, PROJ_WIDTH, seq, BF16, tn=1408)
    dx, dg_mix = _rms_bwd("d_rms_mix", du, xs, norm_mix, res=dh1)

    late = [_disassemble_dwin("split_d_w_in", dwin)]
    sib = _scatter_sibling("scatter_late_sibling", late)
    late_parts = _scatter_chips("scatter_late_chips", _add_chip_partials("sum_late_chip", late, sib))
    early_parts = _send_wait("scatter_early_wait", ssend, srecv, early_thru, early_lands, dx, scatter=True)
    mid_parts = _send_wait("scatter_mid_wait", msend, mrecv, mid_thru, mid_lands, dx, scatter=True)
    received = {"w_in": late_parts[0]}
    received.update(zip(("w_glu", "w_fox_o", "w_mix_out"), mid_parts))
    received.update(zip(("w_mem_q", "w_mem_kv", "w_mem_o", "w_ffn_in", "w_ffn_out"), early_parts))

    small_grads = dict(zip(
        _SMALL, (dg_mix, dg_bforget, g_ssm[0][None], g_ssm[1][None], g_ssm[2][None], g_ssm[3][None], g_ssm[4][None],
                 g_ssm[5][None], g_ssm[6][None], dg_dskip, dg_memq, dg_memkv, dg_ffn, dg_final.reshape(d))))
    small_like = [weights[n] for n in _SMALL]
    small_all = _gather_all("gather_small_grads", [_pack([small_grads[n] for n in _SMALL])])[0]
    pk = [_pack([src[n] for n in _SMALL]) for src in (weights, mom_m, mom_v)]
    small_out = _adamw("adamw_small", small_all, pk[0], pk[1], pk[2], tr=small_all.shape[1])
    small_res = [dict(zip(_SMALL, _unpack(buf, small_like))) for buf in small_out]

    results = [dict(r) for r in small_res]
    tiles = {"w_in": 128, "w_glu": 128, "w_fox_o": 128, "w_mix_out": 128, "w_mem_q": 128, "w_mem_kv": 128,
             "w_mem_o": 128, "w_ffn_in": 128, "w_ffn_out": 176}
    pads = {"w_in": SHARD_IN_PAD, "w_ffn_in": SHARD_FFN_PAD}
    for name in _SHARDED:
        parts = received[name]
        w2, m2, v2 = weights[name][0], mom_m[name][0], mom_v[name][0]
        cols = w2.shape[1]
        if name in pads:
            w2, m2, v2 = (padcols(t, pads[name]) for t in (w2, m2, v2))
        outs = _adamw("adamw_" + name, parts, w2, m2, v2, tr=tiles[name])
        for res, o in zip(results, outs):
            res[name] = o[:, :cols][None]

    loss = lax.psum(loss_part[0, 0], ("x", "y", "c"))
    out = [loss, dx[None]]
    for res in results:
        out.extend(res[n] for n in _WEIGHTS)
    return tuple(out)
```

```python
import math

import jax
import jax.numpy as jnp
import numpy as np
from jax import lax
from jax.experimental import pallas as pl
from jax.experimental.pallas import tpu as pltpu

F32 = jnp.float32
BF16 = jnp.bfloat16

N_DEV = 8
LANE = 128
VMEM_LIMIT = 56 * 1024 * 1024

D_MODEL = 1024
SSM_GROUP = 16
SSM_GROUPS = 32
SSM_WIDTH = 512
SSM_STATE = 64
SSM_CHUNK = 8
FOX_HEADS = 8
FOX_HEAD_DIM = 64
FOX_WIDTH = 512
MEM_HEADS = 4
MEM_HEAD_DIM = 128
MEM_WIDTH = 512
FFN_HIDDEN = 2816
RMS_EPS = 1e-6
IN_WIDTH = 4104
SHARD_IN = IN_WIDTH // N_DEV
SHARD_IN_PAD = 640
SHARD_FFN = 2 * FFN_HIDDEN // N_DEV
SHARD_FFN_PAD = 768
PROJ_GATE0 = 2048
PROJ_F0 = 4096
PROJ_WIDTH = 4224

ADAM_LR = 0.001
ADAM_B1 = 0.9
ADAM_B2 = 0.999
ADAM_EPS = 1e-08
ADAM_WD = 0.01
ADAM_STEP = 10


def _cparams(sem=None):
    return pltpu.CompilerParams(dimension_semantics=sem, vmem_limit_bytes=VMEM_LIMIT)


def _sigmoid(x):
    return 1.0 / (1.0 + jnp.exp(-x))


def _bdot(a, b, dims):
    return lax.dot_general(a.astype(BF16), b.astype(BF16), ((dims[0], dims[1]), ((), ())),
                           preferred_element_type=F32)


_DIMS = {"nn": ((1,), (0,)), "nt": ((1,), (1,)), "tn": ((0,), (0,))}


def _matmul(name, a, b, mode, m, n, k, *, out_dtype, tm, tn, tk, a_off=(0, 0), b_off=(0, 0), add=None):
    tm, tn, tk = min(tm, m), min(tn, n), min(tk, k)
    assert m % tm == 0 and n % tn == 0 and k % tk == 0, (name, m, n, k, tm, tn, tk)
    nk = k // tk
    grid = (m // tm, n // tn, nk)

    def blk(off, t):
        assert off % t == 0, (name, off, t)
        return off // t

    if mode in ("nn", "nt"):
        ar, ac = blk(a_off[0], tm), blk(a_off[1], tk)
        a_spec = pl.BlockSpec((tm, tk), lambda i, j, kk: (i + ar, kk + ac))
    else:
        ar, ac = blk(a_off[0], tk), blk(a_off[1], tm)
        a_spec = pl.BlockSpec((tk, tm), lambda i, j, kk: (kk + ar, i + ac))

    if mode in ("nn", "tn"):
        br, bc = blk(b_off[0], tk), blk(b_off[1], tn)
        b_spec = pl.BlockSpec((tk, tn), lambda i, j, kk: (kk + br, j + bc))
    else:
        br, bc = blk(b_off[0], tn), blk(b_off[1], tk)
        b_spec = pl.BlockSpec((tn, tk), lambda i, j, kk: (j + br, kk + bc))
    o_spec = pl.BlockSpec((tm, tn), lambda i, j, kk: (i, j))
    out_shape = jax.ShapeDtypeStruct((m, n), out_dtype)

    in_specs = [a_spec, b_spec]
    operands = [a, b]
    if add is not None:
        in_specs.append(pl.BlockSpec((tm, tn), lambda i, j, kk: (i, j)))
        operands.append(add)
    dims = _DIMS[mode]
    has_add = add is not None

    def body(*refs):
        a_ref, b_ref = refs[0], refs[1]
        add_ref = refs[2] if has_add else None
        o_ref = refs[3] if has_add else refs[2]
        acc_ref = refs[-1] if nk > 1 else None
        prod = _bdot(a_ref[...], b_ref[...], dims)

        def finish(total):
            if has_add:
                total = total + add_ref[...].astype(F32)
            o_ref[...] = total.astype(o_ref.dtype)

        if nk == 1:
            finish(prod)
        else:
            kk = pl.program_id(2)

            @pl.when(kk == 0)
            def _():
                acc_ref[...] = prod

            @pl.when(jnp.logical_and(kk > 0, kk < nk - 1))
            def _():
                acc_ref[...] += prod

            @pl.when(kk == nk - 1)
            def _():
                finish(acc_ref[...] + prod)

    scratch = [pltpu.VMEM((tm, tn), F32)] if nk > 1 else []
    return pl.pallas_call(
        body, name=name, grid=grid, in_specs=in_specs, out_specs=o_spec, out_shape=out_shape,
        scratch_shapes=scratch,
        compiler_params=_cparams(("parallel", "parallel", "arbitrary")),
    )(*operands)


def _rms_fwd(name, x, gain, *, tr=512, after=None):
    r, d = x.shape
    tr = min(tr, r)

    def body(x_ref, g_ref, *rest):
        o_ref = rest[-1]
        xv = x_ref[...]
        rstd = lax.rsqrt(jnp.mean(xv * xv, axis=-1, keepdims=True) + RMS_EPS)
        o_ref[...] = (xv * rstd * g_ref[...]).astype(o_ref.dtype)

    in_specs = [pl.BlockSpec((tr, d), lambda i: (i, 0)), pl.BlockSpec((1, d), lambda i: (0, 0))]
    ops = [x, gain]
    if after is not None:
        in_specs.append(pl.BlockSpec(after.shape, lambda i: (0, 0)))
        ops.append(after)
    return pl.pallas_call(
        body, name=name, grid=(r // tr,), in_specs=in_specs,
        out_specs=pl.BlockSpec((tr, d), lambda i: (i, 0)),
        out_shape=jax.ShapeDtypeStruct((r, d), BF16),
        compiler_params=_cparams(("parallel",)),
    )(*ops)


def _rms_bwd(name, dy, x, gain, res=None, *, tr=512, after=None):
    r, d = x.shape
    tr = min(tr, r)
    n = r // tr
    has_res = res is not None

    def body(*refs):
        dy_ref, x_ref, g_ref = refs[:3]
        res_ref = refs[3] if has_res else None
        dx_ref, dg_ref, acc_ref = refs[-3:]
        i = pl.program_id(0)
        xv = x_ref[...]
        rstd = lax.rsqrt(jnp.mean(xv * xv, axis=-1, keepdims=True) + RMS_EPS)
        xh = xv * rstd
        dyv = dy_ref[...].astype(F32)
        dxh = dyv * g_ref[...]
        dx = rstd * (dxh - xh * jnp.mean(dxh * xh, axis=-1, keepdims=True))
        if has_res:
            dx = dx + res_ref[...]
        dx_ref[...] = dx
        part = (dyv * xh).reshape(tr // 8, 8, d).sum(axis=0)

        @pl.when(i == 0)
        def _():
            acc_ref[...] = part

        @pl.when(i > 0)
        def _():
            acc_ref[...] += part

        @pl.when(i == n - 1)
        def _():
            dg_ref[...] = jnp.sum(acc_ref[...], axis=0, keepdims=True)

    row = pl.BlockSpec((tr, d), lambda i: (i, 0))
    in_specs = [row, row, pl.BlockSpec((1, d), lambda i: (0, 0))] + ([row] if has_res else [])
    ops = [dy, x, gain] + ([res] if has_res else [])
    if after is not None:
        in_specs.append(pl.BlockSpec(after.shape, lambda i: (0, 0)))
        ops.append(after)
    return pl.pallas_call(
        body, name=name, grid=(n,), in_specs=in_specs,
        out_specs=(row, pl.BlockSpec((1, d), lambda i: (0, 0))),
        out_shape=(jax.ShapeDtypeStruct((r, d), F32), jax.ShapeDtypeStruct((1, d), F32)),
        scratch_shapes=[pltpu.VMEM((8, d), F32)],
        compiler_params=_cparams(("arbitrary",)),
    )(*ops)


def _final_loss(name, h, target, gain, *, tr=512):
    r, d = h.shape
    tr = min(tr, r)
    n = r // tr

    def body(h_ref, t_ref, g_ref, loss_ref, dh_ref, dg_ref, accl_ref, accg_ref):
        i = pl.program_id(0)
        xv = h_ref[...]
        rstd = lax.rsqrt(jnp.mean(xv * xv, axis=-1, keepdims=True) + RMS_EPS)
        xh = xv * rstd
        e = xh * g_ref[...] - t_ref[...]
        dyv = e * (1.0 / d)
        dxh = dyv * g_ref[...]
        dh_ref[...] = rstd * (dxh - xh * jnp.mean(dxh * xh, axis=-1, keepdims=True))
        lpart = (e * e).reshape(tr // 8, 8, d).sum(axis=0)
        gpart = (dyv * xh).reshape(tr // 8, 8, d).sum(axis=0)

        @pl.when(i == 0)
        def _():
            accl_ref[...] = lpart
            accg_ref[...] = gpart

        @pl.when(i > 0)
        def _():
            accl_ref[...] += lpart
            accg_ref[...] += gpart

        @pl.when(i == n - 1)
        def _():
            tot = jnp.sum(jnp.sum(accl_ref[...], axis=0, keepdims=True), axis=1, keepdims=True)
            loss_ref[...] = jnp.broadcast_to(tot * (0.5 / d), (1, LANE))
            dg_ref[...] = jnp.sum(accg_ref[...], axis=0, keepdims=True)

    row = pl.BlockSpec((tr, d), lambda i: (i, 0))
    one = pl.BlockSpec((1, d), lambda i: (0, 0))
    return pl.pallas_call(
        body, name=name, grid=(n,), in_specs=[row, row, one],
        out_specs=(pl.BlockSpec((1, LANE), lambda i: (0, 0)), row, one),
        out_shape=(jax.ShapeDtypeStruct((1, LANE), F32), jax.ShapeDtypeStruct((r, d), F32),
                   jax.ShapeDtypeStruct((1, d), F32)),
        scratch_shapes=[pltpu.VMEM((8, d), F32), pltpu.VMEM((8, d), F32)],
        compiler_params=_cparams(("arbitrary",)),
    )(h, target, gain)


_GELU_C = math.sqrt(2.0 / math.pi)


def _gelu_parts(z):
    inner = _GELU_C * (z + 0.044715 * z * z * z)
    t = jnp.tanh(inner)
    val = 0.5 * z * (1.0 + t)
    dinner = _GELU_C * (1.0 + 3.0 * 0.044715 * z * z)
    grad = 0.5 * (1.0 + t) + 0.5 * z * (1.0 - t * t) * dinner
    return val, grad


def _ssm_post_fwd(name, y8, u8, d8, *, tr=256):
    r, c = y8.shape
    tr = min(tr, r)

    def body(y_ref, u_ref, d_ref, o_ref):
        z = y_ref[...] + d_ref[...] * u_ref[...]
        o_ref[...] = _gelu_parts(z)[0].astype(o_ref.dtype)

    row = pl.BlockSpec((tr, c), lambda i: (i, 0))
    return pl.pallas_call(
        body, name=name, grid=(r // tr,), in_specs=[row, row, pl.BlockSpec((1, c), lambda i: (0, 0))],
        out_specs=row, out_shape=jax.ShapeDtypeStruct((r, c), BF16),
        compiler_params=_cparams(("parallel",)),
    )(y8, u8, d8)


def _ssm_post_bwd(name, dact8, y8, u8, d8, *, tr=256, after=None):
    r, c = y8.shape
    tr = min(tr, r)
    n = r // tr

    def body(*refs):
        da_ref, y_ref, u_ref, d_ref = refs[:4]
        dz_ref, dd_ref, acc_ref = refs[-3:]
        i = pl.program_id(0)
        uv = u_ref[...]
        z = y_ref[...] + d_ref[...] * uv
        dz = da_ref[...].astype(F32) * _gelu_parts(z)[1]
        dz_ref[...] = dz
        part = (dz * uv).reshape(tr // 8, 8, c).sum(axis=0)

        @pl.when(i == 0)
        def _():
            acc_ref[...] = part

        @pl.when(i > 0)
        def _():
            acc_ref[...] += part

        @pl.when(i == n - 1)
        def _():
            tot = jnp.sum(acc_ref[...], axis=0, keepdims=True)
            out = tot[:, 0:SSM_WIDTH]
            for j in range(1, c // SSM_WIDTH):
                out = out + tot[:, j * SSM_WIDTH:(j + 1) * SSM_WIDTH]
            dd_ref[...] = out

    row = pl.BlockSpec((tr, c), lambda i: (i, 0))
    in_specs = [row, row, row, pl.BlockSpec((1, c), lambda i: (0, 0))]
    ops = [dact8, y8, u8, d8]
    if after is not None:
        in_specs.append(pl.BlockSpec(memory_space=pl.ANY))
        ops.append(after)
    return pl.pallas_call(
        body, name=name, grid=(n,), in_specs=in_specs,
        out_specs=(row, pl.BlockSpec((1, SSM_WIDTH), lambda i: (0, 0))),
        out_shape=(jax.ShapeDtypeStruct((r, c), F32), jax.ShapeDtypeStruct((1, SSM_WIDTH), F32)),
        scratch_shapes=[pltpu.VMEM((8, c), F32)],
        compiler_params=_cparams(("arbitrary",)),
    )(*ops)


def _mix_fwd(name, glu, gates, out_b, *, tr=256):
    r = glu.shape[0]
    d = D_MODEL
    tr = min(tr, r)

    def body(glu_ref, gate_ref, ob_ref, o_ref):
        out_a = glu_ref[:, 0:d].astype(F32) * _sigmoid(glu_ref[:, d:2 * d].astype(F32))
        mix = (_sigmoid(gate_ref[:, 0:d].astype(F32)) * out_a
               + _sigmoid(gate_ref[:, d:2 * d].astype(F32)) * ob_ref[...].astype(F32))
        o_ref[...] = mix.astype(o_ref.dtype)

    wide = pl.BlockSpec((tr, 2 * d), lambda i: (i, 0))
    row = pl.BlockSpec((tr, d), lambda i: (i, 0))
    return pl.pallas_call(
        body, name=name, grid=(r // tr,), in_specs=[wide, wide, row], out_specs=row,
        out_shape=jax.ShapeDtypeStruct((r, d), BF16), compiler_params=_cparams(("parallel",)),
    )(glu, gates, out_b)


def _mix_bwd(name, dmix, glu, gates, out_b, *, tr=256):
    r = glu.shape[0]
    d = D_MODEL
    tr = min(tr, r)

    def body(dm_ref, glu_ref, gate_ref, ob_ref, dglu_ref, dgate_ref, dob_ref):
        dm = dm_ref[...]
        glu_a = glu_ref[:, 0:d].astype(F32)
        sb = _sigmoid(glu_ref[:, d:2 * d].astype(F32))
        ga = _sigmoid(gate_ref[:, 0:d].astype(F32))
        gb = _sigmoid(gate_ref[:, d:2 * d].astype(F32))
        out_a = glu_a * sb
        dout_a = dm * ga
        dglu_ref[:, 0:d] = (dout_a * sb).astype(dglu_ref.dtype)
        dglu_ref[:, d:2 * d] = (dout_a * glu_a * sb * (1.0 - sb)).astype(dglu_ref.dtype)
        dgate_ref[:, 0:d] = (dm * out_a * ga * (1.0 - ga)).astype(dgate_ref.dtype)
        dgate_ref[:, d:2 * d] = (dm * ob_ref[...].astype(F32) * gb * (1.0 - gb)).astype(dgate_ref.dtype)
        dob_ref[...] = (dm * gb).astype(dob_ref.dtype)

    wide = pl.BlockSpec((tr, 2 * d), lambda i: (i, 0))
    row = pl.BlockSpec((tr, d), lambda i: (i, 0))
    return pl.pallas_call(
        body, name=name, grid=(r // tr,), in_specs=[row, wide, wide, row], out_specs=(wide, wide, row),
        out_shape=(jax.ShapeDtypeStruct((r, 2 * d), BF16), jax.ShapeDtypeStruct((r, 2 * d), BF16),
                   jax.ShapeDtypeStruct((r, d), BF16)),
        compiler_params=_cparams(("parallel",)),
    )(dmix, glu, gates, out_b)


def _ssm_mats(lam_re, lam_im, log_dt, b_re, b_im, c_re, c_im, nc):
    hp = lax.Precision.HIGHEST
    t = SSM_CHUNK
    nq = SSM_GROUPS // 8
    lam = lax.complex(lam_re, lam_im)
    z = lam * jnp.exp(log_dt)[:, None]
    ks = jnp.arange(t + 1, dtype=F32)
    apow = jnp.exp(ks[:, None, None] * z[None])
    bbar = ((apow[1] - 1.0) / lam)[..., None] * lax.complex(b_re, b_im)
    c = lax.complex(c_re, c_im)

    ca = c[None] * apow[:, :, None, :]
    kmat = jnp.einsum("kgnp,gpm->kgnm", ca, bbar, precision=hp).real
    ii = np.arange(t)
    lag = ii[None, :] - ii[:, None]
    kt = kmat[np.clip(lag, 0, t)] * jnp.asarray(lag >= 0, F32)[:, :, None, None, None]
    kt = kt.reshape(t, t, nq, 8, SSM_GROUP, SSM_GROUP)
    m_c = kt.transpose(2, 0, 3, 5, 1, 4).reshape(nq, 1024, LANE)

    arev = jnp.exp((float(t - 1) - ks[:t])[:, None, None] * z[None])
    w = arev[:, :, :, None] * bbar[None]
    wr = jnp.stack([w.real, w.imag]).reshape(2, t, nq, 8, SSM_STATE, SSM_GROUP)
    bw_c = wr.transpose(2, 1, 3, 5, 0, 4).reshape(nq, 1024, LANE)

    ca1 = ca[1:]
    cr = jnp.stack([ca1.real, -ca1.imag]).reshape(2, t, nq, 8, SSM_GROUP, SSM_STATE)
    cm_c = cr.transpose(2, 0, 3, 5, 1, 4).reshape(nq, 1024, LANE)

    def tiles(v):
        vq = jnp.concatenate([v.real.reshape(nq, 512), v.imag.reshape(nq, 512)], axis=1)
        return jnp.broadcast_to(vq.reshape(nq, 8, 1, LANE), (nq, 8, 8, LANE))

    return m_c, bw_c, cm_c, tiles(apow[t]), tiles(jnp.exp(float(nc) * z))


_BD_M = (LANE, SSM_GROUP)
_BD_BW = (LANE, SSM_STATE)
_BD_CM = (512, SSM_GROUP)


def _bd_perm(cn):
    rr = lax.broadcasted_iota(jnp.int32, (1024, 1024), 0)
    cc = lax.broadcasted_iota(jnp.int32, (1024, 1024), 1)
    sh = cn.bit_length() - 1
    src = ((rr >> 7) << sh) + (((rr & (LANE - 1)) >> sh) << (3 + sh)) + (rr & (cn - 1))
    return jnp.where(src == cc, 1.0, 0.0).astype(BF16)


def _bd_rowgroup(span):
    r = lax.broadcasted_iota(jnp.int32, (1024, LANE), 0)
    return (r & (span - 1)) >> ((span // 8).bit_length() - 1)


def _bd_expand(name, kind, compact):
    span, cn = kind
    nq = compact.shape[0]

    def body(c_ref, o_ref):
        x = c_ref[...]
        grp = _bd_rowgroup(span)
        xcat = jnp.concatenate([jnp.where(grp == h, x, 0.0) for h in range(8)], axis=1)
        o_ref[...] = _bdot(xcat, _bd_perm(cn), _DIMS["nn"]).astype(o_ref.dtype)

    return pl.pallas_call(
        body, name=name, grid=(nq,), in_specs=[pl.BlockSpec((None, 1024, LANE), lambda q: (q, 0, 0))],
        out_specs=pl.BlockSpec((None, 1024, 1024), lambda q: (q, 0, 0)),
        out_shape=jax.ShapeDtypeStruct((nq, 1024, 1024), BF16),
        compiler_params=_cparams(("parallel",)),
    )(compact)


def _bd_reduce(name, kind, dbig):
    span, cn = kind
    nq = dbig.shape[0]

    def body(g_ref, o_ref):
        perm = _bd_perm(cn)
        hi, mid, lo = _split3(g_ref[...])
        d = _DIMS["nt"]
        back = _bdot(hi, perm, d) + _bdot(mid, perm, d) + _bdot(lo, perm, d)
        grp = _bd_rowgroup(span)
        out = jnp.zeros((1024, LANE), F32)
        for h in range(8):
            out = jnp.where(grp == h, back[:, h * LANE:(h + 1) * LANE], out)
        o_ref[...] = out

    return pl.pallas_call(
        body, name=name, grid=(nq,), in_specs=[pl.BlockSpec((None, 1024, 1024), lambda q: (q, 0, 0))],
        out_specs=pl.BlockSpec((None, 1024, LANE), lambda q: (q, 0, 0)),
        out_shape=jax.ShapeDtypeStruct((nq, 1024, LANE), F32),
        compiler_params=_cparams(("parallel",)),
    )(dbig)


def _x_tile_specs(nc, nq):
    return [pl.BlockSpec((nc, LANE), lambda q, t, i=i: (0, i * nq + q)) for i in range(SSM_CHUNK)]


def _cat_tiles(refs):
    return jnp.concatenate([r[...] for r in refs], axis=1)


def _ssm_w(name, x8, bw):
    nc = x8.shape[0]
    nq = bw.shape[0]

    def body(*refs):
        xq = _cat_tiles(refs[:8])
        refs[9][...] = _bdot(xq, refs[8][...], _DIMS["nn"])

    return pl.pallas_call(
        body, name=name, grid=(nq, 8),
        in_specs=_x_tile_specs(nc, nq) + [pl.BlockSpec((None, 1024, LANE), lambda q, t: (q, 0, t))],
        out_specs=pl.BlockSpec((None, None, nc, LANE), lambda q, t: (q, t, 0, 0)),
        out_shape=jax.ShapeDtypeStruct((nq, 8, nc, LANE), F32),
        compiler_params=_cparams(("parallel", "arbitrary")),
    )(*([x8] * 8), bw)


def _ssm_scan(name, w4, a_t, aseg_t, *, reverse, sprev4=None):
    nq, _, nc, _ = w4.shape
    ns = nc // 8
    with_da = sprev4 is not None

    def body(*refs):
        w_ref, a_ref, aseg_ref = refs[:3]
        s_ref = refs[3] if with_da else None
        o_ref = refs[4] if with_da else refs[3]
        da_ref = refs[5] if with_da else None
        sgn = -1.0 if reverse else 1.0
        ar = [a_ref[j] for j in range(4)]
        ai = [sgn * a_ref[j + 4] for j in range(4)]
        gr = [aseg_ref[j] for j in range(4)]
        gi = [sgn * aseg_ref[j + 4] for j in range(4)]
        zero = tuple(jnp.zeros((8, LANE), F32) for _ in range(8))

        def rows(tt):
            return pl.ds((ns - 1 - tt) if reverse else tt, 8, stride=ns)

        def step(carry, w):
            new_r = [ar[j] * carry[j] - ai[j] * carry[j + 4] + w[j] for j in range(4)]
            new_i = [ar[j] * carry[j + 4] + ai[j] * carry[j] + w[j + 4] for j in range(4)]
            return tuple(new_r + new_i)

        def pass1(tt, carry):
            return step(carry, [w_ref[j, rows(tt), :] for j in range(8)])

        ends = lax.fori_loop(0, ns, pass1, zero)
        sub = lax.broadcasted_iota(jnp.int32, (8, LANE), 0)
        init = list(zero)
        order = range(7, 0, -1) if reverse else range(0, 7)
        for s in order:
            nxt = s - 1 if reverse else s + 1
            cand_r = [gr[j] * init[j] - gi[j] * init[j + 4] + ends[j] for j in range(4)]
            cand_i = [gr[j] * init[j + 4] + gi[j] * init[j] + ends[j + 4] for j in range(4)]
            cand = cand_r + cand_i
            shift = 7 if reverse else 1
            init = [jnp.where(sub == nxt, pltpu.roll(cand[j], shift, axis=0), init[j]) for j in range(8)]

        def pass2(tt, state):
            carry, acc = state
            r = rows(tt)
            for j in range(8):
                o_ref[j, r, :] = carry[j]
            if with_da:
                sp = [s_ref[j, r, :] for j in range(8)]
                acc_r = [acc[j] + carry[j] * sp[j] + carry[j + 4] * sp[j + 4] for j in range(4)]
                acc_i = [acc[j + 4] + carry[j + 4] * sp[j] - carry[j] * sp[j + 4] for j in range(4)]
                acc = tuple(acc_r + acc_i)
            return step(carry, [w_ref[j, r, :] for j in range(8)]), acc

        _, acc = lax.fori_loop(0, ns, pass2, (tuple(init), zero))
        if with_da:
            for j in range(8):
                da_ref[j] = acc[j]

    big = pl.BlockSpec((None, 8, nc, LANE), lambda q: (q, 0, 0, 0))
    small = pl.BlockSpec((None, 8, 8, LANE), lambda q: (q, 0, 0, 0))
    in_specs = [big, small, small] + ([big] if with_da else [])
    ops = [w4, a_t, aseg_t] + ([sprev4] if with_da else [])
    out_specs = (big, small) if with_da else big
    big_s = jax.ShapeDtypeStruct((nq, 8, nc, LANE), F32)
    out_shape = (big_s, jax.ShapeDtypeStruct((nq, 8, 8, LANE), F32)) if with_da else big_s
    return pl.pallas_call(
        body, name=name, grid=(nq,), in_specs=in_specs, out_specs=out_specs, out_shape=out_shape,
        compiler_params=_cparams(("parallel",)),
    )(*ops)


def _ssm_y(name, x8, sprev4, m_mat, cm_mat):
    nc = x8.shape[0]
    nq = m_mat.shape[0]

    def body(*refs):
        xq = _cat_tiles(refs[:8])
        s_ref, m_ref, cm_ref, o_ref = refs[8:12]
        sq = jnp.concatenate([s_ref[t] for t in range(8)], axis=1)
        o_ref[...] = _bdot(xq, m_ref[...], _DIMS["nn"]) + _bdot(sq, cm_ref[...], _DIMS["nn"])

    col = pl.BlockSpec((None, 1024, LANE), lambda q, j: (q, 0, j))
    return pl.pallas_call(
        body, name=name, grid=(nq, 8),
        in_specs=_x_tile_specs(nc, nq) + [pl.BlockSpec((None, 8, nc, LANE), lambda q, j: (q, 0, 0, 0)), col, col],
        out_specs=pl.BlockSpec((nc, LANE), lambda q, j: (0, j * nq + q)),
        out_shape=jax.ShapeDtypeStruct((nc, 8 * SSM_WIDTH), F32),
        compiler_params=_cparams(("parallel", "arbitrary")),
    )(*([x8] * 8), sprev4, m_mat, cm_mat)


def _ssm_ds(name, dz8, sprev4, cm_mat):
    nc = dz8.shape[0]
    nq = cm_mat.shape[0]

    def body(*refs):
        dyq = _cat_tiles(refs[:8]).astype(BF16)
        s_ref, cm_ref, ds_ref, dcm_ref = refs[8:12]
        ds_ref[...] = _bdot(dyq, cm_ref[...], _DIMS["nt"])
        dcm_ref[...] = _bdot(s_ref[...], dyq, _DIMS["tn"])

    tile = pl.BlockSpec((None, None, nc, LANE), lambda q, t: (q, t, 0, 0))
    rowblk = pl.BlockSpec((None, LANE, 1024), lambda q, t: (q, t, 0))
    return pl.pallas_call(
        body, name=name, grid=(nq, 8),
        in_specs=_x_tile_specs(nc, nq) + [tile, rowblk],
        out_specs=(tile, rowblk),
        out_shape=(jax.ShapeDtypeStruct((nq, 8, nc, LANE), F32), jax.ShapeDtypeStruct((nq, 1024, 1024), F32)),
        compiler_params=_cparams(("parallel", "arbitrary")),
    )(*([dz8] * 8), sprev4, cm_mat)


def _ssm_dx(name, dz8, g4, x8, m_mat, bw_mat, d8):
    nc = dz8.shape[0]
    nq = m_mat.shape[0]

    def body(*refs):
        dyq = _cat_tiles(refs[:8]).astype(BF16)
        g_ref, x_ref, m_ref, bw_ref, d_ref, dzi_ref, dx_ref, dm_ref, dbw_ref = refs[8:17]
        gq = jnp.concatenate([g_ref[t] for t in range(8)], axis=1).astype(BF16)
        dx = _bdot(dyq, m_ref[...], _DIMS["nt"]) + _bdot(gq, bw_ref[...], _DIMS["nt"])
        dx_ref[...] = (dx + d_ref[...] * dzi_ref[...]).astype(dx_ref.dtype)
        xi = x_ref[...]
        dm_ref[...] = _bdot(xi, dyq, _DIMS["tn"])
        dbw_ref[...] = _bdot(xi, gq, _DIMS["tn"])

    xtile = pl.BlockSpec((nc, LANE), lambda q, i: (0, i * nq + q))
    rowblk = pl.BlockSpec((None, LANE, 1024), lambda q, i: (q, i, 0))
    return pl.pallas_call(
        body, name=name, grid=(nq, 8),
        in_specs=_x_tile_specs(nc, nq) + [pl.BlockSpec((None, 8, nc, LANE), lambda q, i: (q, 0, 0, 0)), xtile, rowblk, rowblk,
                                          pl.BlockSpec((1, LANE), lambda q, i: (0, q)), xtile],
        out_specs=(xtile, rowblk, rowblk),
        out_shape=(jax.ShapeDtypeStruct((nc, 8 * SSM_WIDTH), BF16), jax.ShapeDtypeStruct((nq, 1024, 1024), F32),
                   jax.ShapeDtypeStruct((nq, 1024, 1024), F32)),
        compiler_params=_cparams(("parallel", "arbitrary")),
    )(*([dz8] * 8), g4, x8, m_mat, bw_mat, d8, dz8)


CUM_BLK = 256


def _split3(x):
    hi = x.astype(BF16)
    r1 = x - hi.astype(F32)
    mid = r1.astype(BF16)
    lo = (r1 - mid.astype(F32)).astype(BF16)
    return hi, mid, lo


def _tri_dot(x, tri):
    hi, mid, lo = _split3(x)
    d = _DIMS["nn"]
    return _bdot(hi, tri, d) + _bdot(mid, tri, d) + _bdot(lo, tri, d)


def _tri(n, lower):
    r = lax.broadcasted_iota(jnp.int32, (n, n), 0)
    c = lax.broadcasted_iota(jnp.int32, (n, n), 1)
    return jnp.where((r >= c) if lower else (r <= c), 1.0, 0.0).astype(BF16)


def _fox_cum(name, fproj, bcol):
    seq = fproj.shape[0]
    blk = min(CUM_BLK, seq)

    def body(f_ref, b_ref, o_ref, carry_ref):
        i = pl.program_id(0)

        @pl.when(i == 0)
        def _():
            carry_ref[...] = jnp.zeros_like(carry_ref)

        z = f_ref[...].T + b_ref[...]
        logf = jnp.minimum(z, 0.0) - jnp.log(1.0 + jnp.exp(-jnp.abs(z)))
        carry = carry_ref[...]
        cum = _tri_dot(logf, _tri(blk, lower=False)) + jnp.tile(carry, (1, blk // LANE))
        o_ref[...] = cum[0:8, :]
        carry_ref[...] = carry + jnp.sum(logf, axis=1, keepdims=True)

    return pl.pallas_call(
        body, name=name, grid=(seq // blk,),
        in_specs=[pl.BlockSpec((blk, LANE), lambda i: (i, 0)), pl.BlockSpec((LANE, 1), lambda i: (0, 0))],
        out_specs=pl.BlockSpec((8, blk), lambda i: (0, i)),
        out_shape=jax.ShapeDtypeStruct((8, seq), F32),
        scratch_shapes=[pltpu.VMEM((LANE, LANE), F32)],
        compiler_params=_cparams(("arbitrary",)),
    )(fproj, bcol)


def _fox_cum_bwd(name, dcum_t, fproj, bcol):
    seq = fproj.shape[0]
    blk = min(CUM_BLK, seq)
    n = seq // blk

    def body(dc_ref, f_ref, b_ref, df_ref, db_ref, carry_ref, acc_ref):
        i = pl.program_id(0)

        @pl.when(i == 0)
        def _():
            carry_ref[...] = jnp.zeros_like(carry_ref)
            acc_ref[...] = jnp.zeros_like(acc_ref)

        dc = jnp.concatenate([dc_ref[...], jnp.zeros((LANE - 8, blk), F32)], axis=0)
        carry = carry_ref[...]
        dlogf = _tri_dot(dc, _tri(blk, lower=True)) + jnp.tile(carry, (1, blk // LANE))
        carry_ref[...] = carry + jnp.sum(dc, axis=1, keepdims=True)
        z = f_ref[...].T + b_ref[...]
        dft = dlogf / (1.0 + jnp.exp(z))
        df_ref[...] = dft.T.astype(df_ref.dtype)
        acc_ref[...] += jnp.sum(dft, axis=1, keepdims=True)

        @pl.when(i == n - 1)
        def _():
            db_ref[...] = acc_ref[...]

    return pl.pallas_call(
        body, name=name, grid=(n,),
        in_specs=[pl.BlockSpec((8, blk), lambda i: (0, n - 1 - i)), pl.BlockSpec((blk, LANE), lambda i: (n - 1 - i, 0)),
                  pl.BlockSpec((LANE, 1), lambda i: (0, 0))],
        out_specs=(pl.BlockSpec((blk, LANE), lambda i: (n - 1 - i, 0)), pl.BlockSpec((LANE, LANE), lambda i: (0, 0))),
        out_shape=(jax.ShapeDtypeStruct((seq, LANE), BF16), jax.ShapeDtypeStruct((LANE, LANE), F32)),
        scratch_shapes=[pltpu.VMEM((LANE, LANE), F32), pltpu.VMEM((LANE, LANE), F32)],
        compiler_params=_cparams(("arbitrary",)),
    )(dcum_t, fproj, bcol)


FOX_BLK = 512
FOX_SCALE = FOX_HEAD_DIM ** -0.5


def _fox_head_mask(shape, hh):
    lane = lax.broadcasted_iota(jnp.int32, shape, 1)
    return (lane < FOX_HEAD_DIM) if hh == 0 else (lane >= FOX_HEAD_DIM)


def _fox_bias(cum_ref, hh, q0, k0, blk):
    c0 = jnp.max(cum_ref[hh:hh + 1, pl.ds(q0, LANE)], axis=1, keepdims=True)
    return c0 - cum_ref[hh:hh + 1, pl.ds(k0, blk)]


def _fox_fwd(name, qkv, cum_t):
    seq = qkv.shape[0]
    blk = min(FOX_BLK, seq)
    nb = seq // blk
    npair = FOX_HEADS // 2

    def body(q_ref, k_ref, v_ref, cum_ref, o_ref, lse_ref):
        iq = pl.program_id(1)
        q0 = pl.multiple_of(iq * blk, blk)
        qv = q_ref[...]
        row = lax.broadcasted_iota(jnp.int32, (blk, blk), 0)
        col = lax.broadcasted_iota(jnp.int32, (blk, blk), 1)
        qhs = [jnp.where(_fox_head_mask(qv.shape, hh), qv, jnp.zeros_like(qv)) * FOX_SCALE for hh in range(2)]

        def block(kb, states, masked):
            k0 = pl.multiple_of(kb * blk, blk)
            kv = k_ref[pl.ds(k0, blk), :]
            vv = v_ref[pl.ds(k0, blk), :]
            new = []
            for hh in range(2):
                m, l, acc = states[hh]
                s = _bdot(qhs[hh], kv, _DIMS["nt"]) + _fox_bias(cum_ref, hh, q0, k0, blk)
                if masked:
                    s = jnp.where(row >= col, s, -jnp.inf)
                m_new = jnp.maximum(m, jnp.max(s, axis=1, keepdims=True))
                alpha = jnp.exp(m - m_new)
                p = jnp.exp(s - m_new)
                l = alpha * l + jnp.sum(p, axis=1, keepdims=True)
                acc = alpha * acc + _bdot(p, vv, _DIMS["nn"])
                new.append((m_new, l, acc))
            return tuple(new)

        init = (jnp.full((blk, 1), -jnp.inf, F32), jnp.zeros((blk, 1), F32), jnp.zeros((blk, LANE), F32))
        states = lax.fori_loop(0, iq, lambda kb, st: block(kb, st, False), (init, init))
        states = block(iq, states, True)
        outs = []
        for hh in range(2):
            m, l, acc = states[hh]
            outs.append(acc / l)
            lse_ref[hh] = jnp.broadcast_to(m + jnp.log(l), (blk, LANE))
        o_ref[...] = jnp.where(_fox_head_mask(outs[0].shape, 0), outs[0], outs[1]).astype(o_ref.dtype)

    return pl.pallas_call(
        body, name=name, grid=(npair, nb),
        in_specs=[pl.BlockSpec((blk, LANE), lambda p, i: (i, p)),
                  pl.BlockSpec((seq, LANE), lambda p, i: (0, npair + p)),
                  pl.BlockSpec((seq, LANE), lambda p, i: (0, 2 * npair + p)),
                  pl.BlockSpec((None, 2, seq), lambda p, i: (p, 0, 0))],
        out_specs=(pl.BlockSpec((blk, LANE), lambda p, i: (i, p)),
                   pl.BlockSpec((2, blk, LANE), lambda p, i: (p, i, 0))),
        out_shape=(jax.ShapeDtypeStruct((seq, FOX_WIDTH), BF16), jax.ShapeDtypeStruct((FOX_HEADS, seq, LANE), F32)),
        compiler_params=_cparams(("parallel", "arbitrary")),
    )(qkv, qkv, qkv, cum_t)


def _fox_bwd(name, qkv, cum_t, att, datt, lse):
    seq = qkv.shape[0]
    blk = min(FOX_BLK, seq)
    nb = seq // blk
    npair = FOX_HEADS // 2

    def body(q_ref, k_ref, v_ref, cum_ref, o_ref, do_ref, lse_ref, dq_ref, dk_ref, dv_ref, dcum_ref):
        iq = pl.program_id(1)
        q0 = pl.multiple_of(iq * blk, blk)

        @pl.when(iq == 0)
        def _():
            dk_ref[...] = jnp.zeros_like(dk_ref)
            dv_ref[...] = jnp.zeros_like(dv_ref)
            dcum_ref[...] = jnp.zeros_like(dcum_ref)

        qv = q_ref[...]
        dov = do_ref[...].astype(F32)
        ov = o_ref[...].astype(F32)
        row = lax.broadcasted_iota(jnp.int32, (blk, blk), 0)
        col = lax.broadcasted_iota(jnp.int32, (blk, blk), 1)
        qhs, dohbs, deltas, lses = [], [], [], []
        for hh in range(2):
            hm = _fox_head_mask(qv.shape, hh)
            qhs.append(jnp.where(hm, qv, jnp.zeros_like(qv)) * FOX_SCALE)
            doh = jnp.where(hm, dov, 0.0)
            dohbs.append(doh.astype(BF16))
            deltas.append(jnp.sum(doh * ov, axis=1, keepdims=True))
            lses.append(jnp.tile(lse_ref[hh], (1, blk // LANE)))

        def block(kb, accs, masked):
            k0 = pl.multiple_of(kb * blk, blk)
            kv = k_ref[pl.ds(k0, blk), :]
            vv = v_ref[pl.ds(k0, blk), :]
            new = []
            dk_blk = None
            dv_blk = None
            for hh in range(2):
                dq_acc, rs_acc = accs[hh]
                s = _bdot(qhs[hh], kv, _DIMS["nt"]) + _fox_bias(cum_ref, hh, q0, k0, blk)
                p = jnp.exp(s - lses[hh])
                if masked:
                    p = jnp.where(row >= col, p, 0.0)
                dp = _bdot(dohbs[hh], vv, _DIMS["nt"])
                ds = p * (dp - deltas[hh])
                dsb = ds.astype(BF16)
                dk_h = _bdot(dsb, qhs[hh], _DIMS["tn"])
                dv_h = _bdot(p, dohbs[hh], _DIMS["tn"])
                dk_blk = dk_h if dk_blk is None else dk_blk + dk_h
                dv_blk = dv_h if dv_blk is None else dv_blk + dv_h
                dcum_ref[hh:hh + 1, pl.ds(k0, blk)] -= jnp.sum(ds, axis=0, keepdims=True)
                new.append((dq_acc + _bdot(dsb, kv, _DIMS["nn"]), rs_acc + jnp.sum(ds, axis=1, keepdims=True)))
            dk_ref[pl.ds(k0, blk), :] += dk_blk
            dv_ref[pl.ds(k0, blk), :] += dv_blk
            return tuple(new)

        init = (jnp.zeros((blk, LANE), F32), jnp.zeros((blk, 1), F32))
        accs = lax.fori_loop(0, iq, lambda kb, a: block(kb, a, False), (init, init))
        accs = block(iq, accs, True)
        for hh in range(2):
            dcum_ref[hh:hh + 1, pl.ds(q0, blk)] += jnp.broadcast_to(accs[hh][1], (blk, LANE)).T[0:1, :]
        dq = jnp.where(_fox_head_mask(qv.shape, 0), accs[0][0], accs[1][0]) * FOX_SCALE
        dq_ref[...] = dq.astype(dq_ref.dtype)

    qblk = pl.BlockSpec((blk, LANE), lambda p, i: (i, p))
    full = pl.BlockSpec((seq, LANE), lambda p, i: (0, p))
    return pl.pallas_call(
        body, name=name, grid=(npair, nb),
        in_specs=[qblk,
                  pl.BlockSpec((seq, LANE), lambda p, i: (0, npair + p)),
                  pl.BlockSpec((seq, LANE), lambda p, i: (0, 2 * npair + p)),
                  pl.BlockSpec((None, 2, seq), lambda p, i: (p, 0, 0)),
                  qblk, qblk,
                  pl.BlockSpec((2, blk, LANE), lambda p, i: (p, i, 0))],
        out_specs=(qblk, full, full, pl.BlockSpec((None, 2, seq), lambda p, i: (p, 0, 0))),
        out_shape=(jax.ShapeDtypeStruct((seq, FOX_WIDTH), BF16), jax.ShapeDtypeStruct((seq, FOX_WIDTH), F32),
                   jax.ShapeDtypeStruct((seq, FOX_WIDTH), F32), jax.ShapeDtypeStruct((npair, 2, seq), F32)),
        compiler_params=_cparams(("arbitrary", "arbitrary")),
    )(qkv, qkv, qkv, cum_t, att, datt, lse)


MEM_SCALE = MEM_HEAD_DIM ** -0.5


def _mem_probs(qh, kh):
    s = _bdot(qh, kh, _DIMS["nt"]) * MEM_SCALE
    p = jnp.exp(s - jnp.max(s, axis=1, keepdims=True))
    return p / jnp.sum(p, axis=1, keepdims=True)


def _mem_fwd(name, q2, kv, *, tr=512):
    seq = q2.shape[0]
    mlen = kv.shape[0]
    tr = min(tr, seq)

    def body(q_ref, kv_ref, o_ref):
        for h in range(MEM_HEADS):
            sl = slice(h * MEM_HEAD_DIM, (h + 1) * MEM_HEAD_DIM)
            sv = slice(MEM_WIDTH + h * MEM_HEAD_DIM, MEM_WIDTH + (h + 1) * MEM_HEAD_DIM)
            p = _mem_probs(q_ref[:, sl], kv_ref[:, sl])
            o_ref[:, sl] = _bdot(p, kv_ref[:, sv], _DIMS["nn"]).astype(o_ref.dtype)

    return pl.pallas_call(
        body, name=name, grid=(seq // tr,),
        in_specs=[pl.BlockSpec((tr, MEM_WIDTH), lambda i: (i, 0)), pl.BlockSpec((mlen, 2 * MEM_WIDTH), lambda i: (0, 0))],
        out_specs=pl.BlockSpec((tr, MEM_WIDTH), lambda i: (i, 0)),
        out_shape=jax.ShapeDtypeStruct((seq, MEM_WIDTH), BF16),
        compiler_params=_cparams(("parallel",)),
    )(q2, kv)


def _mem_bwd(name, q2, kv, do2, *, tr=512):
    seq = q2.shape[0]
    mlen = kv.shape[0]
    tr = min(tr, seq)

    def body(q_ref, kv_ref, do_ref, dq_ref, dkv_ref):
        i = pl.program_id(0)

        @pl.when(i == 0)
        def _():
            dkv_ref[...] = jnp.zeros_like(dkv_ref)

        for h in range(MEM_HEADS):
            sl = slice(h * MEM_HEAD_DIM, (h + 1) * MEM_HEAD_DIM)
            sv = slice(MEM_WIDTH + h * MEM_HEAD_DIM, MEM_WIDTH + (h + 1) * MEM_HEAD_DIM)
            qh = q_ref[:, sl]
            kh = kv_ref[:, sl]
            doh = do_ref[:, sl].astype(BF16)
            p = _mem_probs(qh, kh)
            dp = _bdot(doh, kv_ref[:, sv], _DIMS["nt"])
            ds = (p * (dp - jnp.sum(p * dp, axis=1, keepdims=True)) * MEM_SCALE).astype(BF16)
            dq_ref[:, sl] = _bdot(ds, kh, _DIMS["nn"]).astype(dq_ref.dtype)
            dkv_ref[:, sl] += _bdot(ds, qh, _DIMS["tn"])
            dkv_ref[:, sv] += _bdot(p, doh, _DIMS["tn"])

    row = pl.BlockSpec((tr, MEM_WIDTH), lambda i: (i, 0))
    kvs = pl.BlockSpec((mlen, 2 * MEM_WIDTH), lambda i: (0, 0))
    return pl.pallas_call(
        body, name=name, grid=(seq // tr,), in_specs=[row, kvs, row], out_specs=(row, kvs),
        out_shape=(jax.ShapeDtypeStruct((seq, MEM_WIDTH), BF16), jax.ShapeDtypeStruct((mlen, 2 * MEM_WIDTH), F32)),
        compiler_params=_cparams(("arbitrary",)),
    )(q2, kv, do2)


_HBM = pl.BlockSpec(memory_space=pl.ANY)
_HBM_ONLY = pl.BlockSpec(memory_space=pltpu.HBM)
_MESH = pl.DeviceIdType.MESH


def _mesh_place():
    x, y, c = lax.axis_index("x"), lax.axis_index("y"), lax.axis_index("c")
    other_chips = [(1 - x, y), (x, 1 - y), (1 - x, 1 - y)]
    return x, y, c, other_chips


def _gather_all(name, arrays):
    n = len(arrays)

    def body(*refs):
        ins, outs = refs[:n], refs[n:2 * n]
        send_sems, recv_sems, local_sems = refs[2 * n:]
        x, y, c, chips = _mesh_place()
        me, sibling = (x, y, c), (x, y, 1 - c)

        def slot(a, place):
            px, py, pc = place
            return outs[a].at[4 * px + 2 * py + pc]

        def copy(a, k, block, to, src=None):
            return pltpu.make_async_remote_copy(
                src_ref=slot(a, block) if src is None else src, dst_ref=slot(a, block),
                send_sem=send_sems.at[a, k], recv_sem=recv_sems.at[a, k], device_id=to, device_id_type=_MESH)

        mine = [pltpu.make_async_copy(ins[a], slot(a, me), local_sems.at[a]) for a in range(n)]
        for cp in mine:
            cp.start()
        first = []
        for a in range(n):
            first.append(copy(a, 0, me, sibling, src=ins[a]))
            first += [copy(a, 1 + j, me, (*chip, c), src=ins[a]) for j, chip in enumerate(chips)]
        for cp in first:
            cp.start()
        passed = []
        for j, chip in enumerate(chips):
            for a in range(n):
                copy(a, 1 + j, (*chip, c), me).wait_recv()
                fwd = copy(a, 4 + j, (*chip, c), sibling)
                fwd.start()
                passed.append(fwd)
        for a in range(n):
            copy(a, 0, sibling, me).wait_recv()
            for j, chip in enumerate(chips):
                copy(a, 4 + j, (*chip, 1 - c), me).wait_recv()
        for cp in first + passed:
            cp.wait_send()
        for cp in mine:
            cp.wait()

    out_shape = tuple(jax.ShapeDtypeStruct((N_DEV,) + arr.shape, arr.dtype) for arr in arrays)
    return pl.pallas_call(
        body, name=name, in_specs=[_HBM] * n, out_specs=tuple([_HBM] * n), out_shape=out_shape,
        scratch_shapes=[pltpu.SemaphoreType.DMA((n, N_DEV - 1)), pltpu.SemaphoreType.DMA((n, N_DEV - 1)),
                        pltpu.SemaphoreType.DMA((n,))],
    )(*arrays)


_SEM = pl.BlockSpec(memory_space=pltpu.SEMAPHORE)
_DATAFLOW = pltpu.SideEffectType.DATAFLOW_SIDE_EFFECTING


def _device_index():
    return (4 * lax.axis_index("x") + 2 * lax.axis_index("y") + lax.axis_index("c")).astype(jnp.int32).reshape(1)


def _place_own(name, pieces, *, stacked_src, after=None):
    n = len(pieces)
    n_in = n + (after is not None)

    def body(me_ref, *refs):
        for a in range(n):
            refs[n_in + a][...] = refs[a][...]

    def spec(shape):
        return pl.BlockSpec((None,) + tuple(shape), lambda i, me_ref: (me_ref[0],) + (0,) * len(shape))

    shapes = [p.shape[1:] if stacked_src else p.shape for p in pieces]
    if stacked_src:
        in_specs = [spec(s) for s in shapes]
    else:
        in_specs = [pl.BlockSpec(tuple(s), lambda i, me_ref, nd=len(s): (0,) * nd) for s in shapes]
    operands = list(pieces)
    if after is not None:
        in_specs.append(_HBM)
        operands.append(after)
    return pl.pallas_call(
        body, name=name,
        grid_spec=pltpu.PrefetchScalarGridSpec(num_scalar_prefetch=1, grid=(1,), in_specs=in_specs,
                                               out_specs=tuple(spec(s) for s in shapes)),
        out_shape=tuple(jax.ShapeDtypeStruct((N_DEV,) + tuple(s), p.dtype) for s, p in zip(shapes, pieces)),
        compiler_params=_cparams(("arbitrary",)),
    )(_device_index(), *operands)


def _peer_places():
    x, y, c = lax.axis_index("x"), lax.axis_index("y"), lax.axis_index("c")
    peers = []
    for k in range(N_DEV - 1):
        flip = k + 1
        px = 1 - x if flip & 4 else x
        py = 1 - y if flip & 2 else y
        pc = 1 - c if flip & 1 else c
        peers.append((px, py, pc, 4 * px + 2 * py + pc))
    return 4 * x + 2 * y + c, peers


def _direct_copy(srcs, lands, send_sems, recv_sems, a, k, me, peer, scatter):
    px, py, pc, pidx = peer
    return pltpu.make_async_remote_copy(
        src_ref=srcs[a].at[pidx] if scatter else srcs[a], dst_ref=lands[a].at[me],
        send_sem=send_sems.at[a * (N_DEV - 1) + k], recv_sem=recv_sems.at[a * (N_DEV - 1) + k],
        device_id=(px, py, pc), device_id_type=_MESH)


def _send_start(name, srcs, lands, *, scatter):
    n = len(srcs)

    def body(*refs):
        src_refs, land_refs = refs[:n], refs[n:2 * n]
        send_sems, recv_sems = refs[2 * n], refs[2 * n + 1]
        token = refs[-1]
        me, peers = _peer_places()
        for k, peer in enumerate(peers):
            for a in range(n):
                _direct_copy(src_refs, land_refs, send_sems, recv_sems, a, k, me, peer, scatter).start()
        token[...] = jnp.zeros_like(token)

    hbm_shapes = [pltpu.HBM(t.shape, t.dtype) for t in list(srcs) + list(lands)]
    outs = pl.pallas_call(
        body, name=name,
        out_shape=(pltpu.SemaphoreType.DMA((n * (N_DEV - 1),)), pltpu.SemaphoreType.DMA((n * (N_DEV - 1),)), *hbm_shapes,
                   jax.ShapeDtypeStruct((8, LANE), F32)),
        in_specs=[_HBM_ONLY] * (2 * n),
        out_specs=(_SEM, _SEM, *([_HBM_ONLY] * (2 * n)), pl.BlockSpec(memory_space=pltpu.VMEM)),
        input_output_aliases={i: 2 + i for i in range(2 * n)},
        compiler_params=pltpu.CompilerParams(has_side_effects=_DATAFLOW),
    )(*[pltpu.with_memory_space_constraint(t, pltpu.HBM) for t in list(srcs) + list(lands)])
    return outs[0], outs[1], outs[2:2 + n], outs[2 + n:2 + 2 * n], outs[-1]


def _send_wait(name, send_sems, recv_sems, srcs, lands, after, *, scatter):
    n = len(srcs)

    def body(*refs):
        src_refs, land_refs = refs[:n], refs[n:2 * n]
        send_sems, recv_sems = refs[2 * n], refs[2 * n + 1]
        me, peers = _peer_places()
        for k, peer in enumerate(peers):
            for a in range(n):
                cp = _direct_copy(src_refs, land_refs, send_sems, recv_sems, a, k, me, peer, scatter)
                cp.wait_send()
                cp.wait_recv()

    hbm_shapes = [pltpu.HBM(t.shape, t.dtype) for t in list(srcs) + list(lands)]
    outs = pl.pallas_call(
        body, name=name, out_shape=tuple(hbm_shapes),
        in_specs=[_HBM_ONLY] * (2 * n) + [_SEM, _SEM, _HBM],
        out_specs=tuple([_HBM_ONLY] * (2 * n)),
        input_output_aliases={i: i for i in range(2 * n)},
        compiler_params=pltpu.CompilerParams(has_side_effects=_DATAFLOW),
    )(*srcs, *lands, send_sems, recv_sems, after)
    return outs[n:]


def _scatter_sibling(name, arrays):
    n = len(arrays)

    def body(*refs):
        ins, sibs = refs[:n], refs[n:2 * n]
        send_sems, recv_sems = refs[2 * n:]
        x, y, c, _ = _mesh_place()
        copies = []
        for a in range(n):
            for j in range(4):
                rdma = pltpu.make_async_remote_copy(
                    src_ref=ins[a].at[2 * j + (1 - c)], dst_ref=sibs[a].at[j], send_sem=send_sems.at[a, j],
                    recv_sem=recv_sems.at[a, j], device_id=(x, y, 1 - c), device_id_type=_MESH)
                rdma.start()
                copies.append(rdma)
        for cp in copies:
            cp.wait()

    four = tuple(jax.ShapeDtypeStruct((4,) + arr.shape[1:], arr.dtype) for arr in arrays)
    return pl.pallas_call(
        body, name=name, in_specs=[_HBM] * n, out_specs=tuple([_HBM] * n), out_shape=four,
        scratch_shapes=[pltpu.SemaphoreType.DMA((n, 4)), pltpu.SemaphoreType.DMA((n, 4))],
    )(*arrays)


def _add_chip_partials(name, pieces, sibs):
    n = len(pieces)
    core = lax.axis_index("c").astype(jnp.int32).reshape(1)

    def body(c_ref, *refs):
        for a in range(n):
            out = refs[2 * n + a]
            out[...] = (refs[a][...].astype(F32) + refs[n + a][...].astype(F32)).astype(out.dtype)

    own_specs = [pl.BlockSpec((None,) + arr.shape[1:], lambda j, c_ref: (2 * j + c_ref[0], 0, 0)) for arr in pieces]
    four_specs = [pl.BlockSpec((None,) + arr.shape[1:], lambda j, c_ref: (j, 0, 0)) for arr in sibs]
    return pl.pallas_call(
        body, name=name,
        grid_spec=pltpu.PrefetchScalarGridSpec(num_scalar_prefetch=1, grid=(4,), in_specs=own_specs + four_specs,
                                               out_specs=tuple(four_specs)),
        out_shape=tuple(jax.ShapeDtypeStruct(arr.shape, arr.dtype) for arr in sibs),
        compiler_params=_cparams(("parallel",)),
    )(core, *pieces, *sibs)


def _scatter_chips(name, arrays):
    n = len(arrays)

    def body(*refs):
        ins, outs = refs[:n], refs[n:2 * n]
        send_sems, recv_sems, local_sems = refs[2 * n:]
        x, y, c, chips = _mesh_place()
        my_chip = 2 * x + y
        copies = []
        for a in range(n):
            local = pltpu.make_async_copy(ins[a].at[my_chip], outs[a].at[my_chip], local_sems.at[a])
            local.start()
            copies.append(local)
            for k, (px, py) in enumerate(chips):
                rdma = pltpu.make_async_remote_copy(
                    src_ref=ins[a].at[2 * px + py], dst_ref=outs[a].at[my_chip], send_sem=send_sems.at[a, k],
                    recv_sem=recv_sems.at[a, k], device_id=(px, py, c), device_id_type=_MESH)
                rdma.start()
                copies.append(rdma)
        for cp in copies:
            cp.wait()

    return pl.pallas_call(
        body, name=name, in_specs=[_HBM] * n, out_specs=tuple([_HBM] * n),
        out_shape=tuple(jax.ShapeDtypeStruct(arr.shape, arr.dtype) for arr in arrays),
        scratch_shapes=[pltpu.SemaphoreType.DMA((n, 3)), pltpu.SemaphoreType.DMA((n, 3)),
                        pltpu.SemaphoreType.DMA((n,))],
    )(*arrays)


def _unstack_cols(name, stacked):
    n, rows, cols = stacked.shape

    def body(i_ref, o_ref):
        o_ref[...] = i_ref[...]

    return pl.pallas_call(
        body, name=name, grid=(n,), in_specs=[pl.BlockSpec((None, rows, cols), lambda k: (k, 0, 0))],
        out_specs=pl.BlockSpec((rows, cols), lambda k: (0, k)),
        out_shape=jax.ShapeDtypeStruct((rows, n * cols), stacked.dtype),
        compiler_params=_cparams(("parallel",)),
    )(stacked)


def _restack_cols(name, mat):
    rows, width = mat.shape
    cols = width // N_DEV

    def body(i_ref, o_ref):
        o_ref[...] = i_ref[...]

    return pl.pallas_call(
        body, name=name, grid=(N_DEV,), in_specs=[pl.BlockSpec((rows, cols), lambda k: (0, k))],
        out_specs=pl.BlockSpec((None, rows, cols), lambda k: (k, 0, 0)),
        out_shape=jax.ShapeDtypeStruct((N_DEV, rows, cols), mat.dtype),
        compiler_params=_cparams(("parallel",)),
    )(mat)


def _remap_pieces(runs):
    plan = {}
    for du, dc, su, sc, ln in runs:
        while ln > 0:
            lane = dc % LANE
            take = min(ln, LANE - lane)
            plan.setdefault((du, dc // LANE), []).append((su, sc, take, lane))
            dc, sc, ln = dc + take, sc + take, ln - take
    return plan


def _remap(name, srcs, src_units, runs, *, out_units, out_cols, out_dtype, tr=256):
    rows = srcs[0].shape[-2]
    tr = min(tr, rows)
    plan = _remap_pieces(runs)
    n_src = len(srcs)
    stacked_out = out_units is not None
    n_tiles = out_cols // LANE

    def body(*refs):
        o_ref = refs[n_src]

        def src_tile(unit, t):
            ai, lead = src_units[unit]
            ref = refs[ai]
            sl = slice(t * LANE, (t + 1) * LANE)
            return (ref[:, sl] if lead is None else ref[lead, :, sl]).astype(F32)

        lane = lax.broadcasted_iota(jnp.int32, (tr, LANE), 1)
        for du in range(out_units if stacked_out else 1):
            for t in range(n_tiles):
                acc = jnp.zeros((tr, LANE), F32)
                for su, sc, ln, dl in plan.get((du if stacked_out else None, t), []):
                    st, so = sc // LANE, sc % LANE
                    first = src_tile(su, st)
                    if so == dl and so + ln <= LANE:
                        piece = first
                    else:
                        second = src_tile(su, st + 1) if so + ln > LANE else first
                        both = jnp.concatenate([first, second], axis=1)
                        piece = pltpu.roll(both, (dl - so) % (2 * LANE), axis=1)[:, 0:LANE]
                    acc = piece if (dl == 0 and ln == LANE) else jnp.where(
                        jnp.logical_and(lane >= dl, lane < dl + ln), piece, acc)
                if stacked_out:
                    o_ref[du, :, t * LANE:(t + 1) * LANE] = acc.astype(o_ref.dtype)
                else:
                    o_ref[:, t * LANE:(t + 1) * LANE] = acc.astype(o_ref.dtype)

    in_specs = []
    for arr in srcs:
        if arr.ndim == 2:
            in_specs.append(pl.BlockSpec((tr, arr.shape[1]), lambda i: (i, 0)))
        else:
            in_specs.append(pl.BlockSpec((arr.shape[0], tr, arr.shape[2]), lambda i: (0, i, 0)))
    if stacked_out:
        out_spec = pl.BlockSpec((out_units, tr, out_cols), lambda i: (0, i, 0))
        out_shape = jax.ShapeDtypeStruct((out_units, rows, out_cols), out_dtype)
    else:
        out_spec = pl.BlockSpec((tr, out_cols), lambda i: (i, 0))
        out_shape = jax.ShapeDtypeStruct((rows, out_cols), out_dtype)
    return pl.pallas_call(
        body, name=name, grid=(rows // tr,), in_specs=in_specs, out_specs=out_spec, out_shape=out_shape,
        compiler_params=_cparams(("parallel",)),
    )(*srcs)


def _proj_col(c):
    if c < PROJ_GATE0:
        return c
    if c < PROJ_GATE0 + FOX_HEADS:
        return PROJ_F0 + (c - PROJ_GATE0)
    return c - FOX_HEADS


def _win_runs():
    cuts = sorted(set([0, PROJ_GATE0, PROJ_GATE0 + FOX_HEADS, IN_WIDTH] + [SHARD_IN * k for k in range(N_DEV + 1)]))
    return [(lo // SHARD_IN, lo % SHARD_IN, _proj_col(lo), hi - lo) for lo, hi in zip(cuts[:-1], cuts[1:])]


def _assemble_win(name, stacked):
    runs = [(None, pc, k, sc, ln) for k, sc, pc, ln in _win_runs()]
    return _remap(name, [stacked], [(0, k) for k in range(N_DEV)], runs,
                  out_units=None, out_cols=PROJ_WIDTH, out_dtype=BF16)


def _disassemble_dwin(name, dw):
    runs = [(k, sc, 0, pc, ln) for k, sc, pc, ln in _win_runs()]
    return _remap(name, [dw], [(0, None)], runs, out_units=N_DEV, out_cols=SHARD_IN_PAD, out_dtype=BF16)


def _concat_cols(name, parts, *, tr=512):
    rows = parts[0].shape[0]
    tr = min(tr, rows)
    widths = [p.shape[1] for p in parts]
    total = sum(widths)

    def body(*refs):
        o_ref = refs[len(parts)]
        lo = 0
        for r, w in zip(refs[:len(parts)], widths):
            o_ref[:, lo:lo + w] = r[...].astype(o_ref.dtype)
            lo += w

    return pl.pallas_call(
        body, name=name, grid=(rows // tr,),
        in_specs=[pl.BlockSpec((tr, w), lambda i: (i, 0)) for w in widths],
        out_specs=pl.BlockSpec((tr, total), lambda i: (i, 0)),
        out_shape=jax.ShapeDtypeStruct((rows, total), BF16),
        compiler_params=_cparams(("parallel",)),
    )(*parts)


FFN_BLK = FFN_HIDDEN // 2


def _ffn_col(c):
    half, r = divmod(c, FFN_HIDDEN)
    blk, r = divmod(r, FFN_BLK)
    return blk * 2 * FFN_BLK + half * FFN_BLK + r


def _assemble_wffn(name, stacked):
    runs = [(None, _ffn_col(SHARD_FFN * k), k, 0, SHARD_FFN) for k in range(N_DEV)]
    return _remap(name, [stacked], [(0, k) for k in range(N_DEV)], runs,
                  out_units=None, out_cols=2 * FFN_HIDDEN, out_dtype=BF16)


def _disassemble_dwffn(name, dw):
    runs = [(k, 0, 0, _ffn_col(SHARD_FFN * k), SHARD_FFN) for k in range(N_DEV)]
    return _remap(name, [dw], [(0, None)], runs, out_units=N_DEV, out_cols=SHARD_FFN_PAD, out_dtype=BF16)


def _ffn_in_swiglu(name, xn, w, *, tm=512):
    rows, k = xn.shape
    tm = min(tm, rows)
    nblk = FFN_HIDDEN // FFN_BLK

    def body(x_ref, w_ref, f_ref, g_ref):
        f = _bdot(x_ref[...], w_ref[...], _DIMS["nn"])
        f_ref[...] = f.astype(f_ref.dtype)
        fa = f[:, 0:FFN_BLK]
        g_ref[...] = (fa * _sigmoid(fa) * f[:, FFN_BLK:2 * FFN_BLK]).astype(g_ref.dtype)

    return pl.pallas_call(
        body, name=name, grid=(nblk, rows // tm),
        in_specs=[pl.BlockSpec((tm, k), lambda j, i: (i, 0)), pl.BlockSpec((k, 2 * FFN_BLK), lambda j, i: (0, j))],
        out_specs=(pl.BlockSpec((tm, 2 * FFN_BLK), lambda j, i: (i, j)), pl.BlockSpec((tm, FFN_BLK), lambda j, i: (i, j))),
        out_shape=(jax.ShapeDtypeStruct((rows, 2 * FFN_HIDDEN), BF16), jax.ShapeDtypeStruct((rows, FFN_HIDDEN), BF16)),
        compiler_params=_cparams(("parallel", "arbitrary")),
    )(xn, w)


def _d_ffn_out_swiglu(name, dh, w_out, f, *, tm=512):
    rows, d = dh.shape
    tm = min(tm, rows)
    nblk = FFN_HIDDEN // FFN_BLK

    def body(dh_ref, w_ref, f_ref, df_ref):
        dg = _bdot(dh_ref[...], w_ref[...], _DIMS["nt"])
        fa = f_ref[:, 0:FFN_BLK].astype(F32)
        fb = f_ref[:, FFN_BLK:2 * FFN_BLK].astype(F32)
        s = _sigmoid(fa)
        df_ref[:, 0:FFN_BLK] = (dg * fb * s * (1.0 + fa * (1.0 - s))).astype(df_ref.dtype)
        df_ref[:, FFN_BLK:2 * FFN_BLK] = (dg * fa * s).astype(df_ref.dtype)

    wide = pl.BlockSpec((tm, 2 * FFN_BLK), lambda j, i: (i, j))
    return pl.pallas_call(
        body, name=name, grid=(nblk, rows // tm),
        in_specs=[pl.BlockSpec((tm, d), lambda j, i: (i, 0)), pl.BlockSpec((FFN_BLK, d), lambda j, i: (j, 0)), wide],
        out_specs=wide, out_shape=jax.ShapeDtypeStruct((rows, 2 * FFN_HIDDEN), BF16),
        compiler_params=_cparams(("parallel", "arbitrary")),
    )(dh, w_out, f)


def _adamw(name, parts, w, m, v, *, tr=128):
    rows, cols = w.shape
    n_parts = parts.shape[0]
    tr = min(tr, rows)
    assert rows % tr == 0, (name, rows, tr)
    c1 = 1.0 - ADAM_B1 ** ADAM_STEP
    c2 = 1.0 - ADAM_B2 ** ADAM_STEP

    def body(p_ref, w_ref, m_ref, v_ref, g_ref, d_ref, nm_ref, nv_ref):
        g = p_ref[0].astype(F32)
        for s in range(1, n_parts):
            g = g + p_ref[s].astype(F32)
        m_new = ADAM_B1 * m_ref[...] + (1.0 - ADAM_B1) * g
        v_new = ADAM_B2 * v_ref[...] + (1.0 - ADAM_B2) * (g * g)
        upd = (m_new / c1) / (jnp.sqrt(v_new / c2) + ADAM_EPS) + ADAM_WD * w_ref[...]
        g_ref[...] = g
        d_ref[...] = -ADAM_LR * upd
        nm_ref[...] = m_new
        nv_ref[...] = v_new

    row = pl.BlockSpec((tr, cols), lambda i: (i, 0))
    out = jax.ShapeDtypeStruct((rows, cols), F32)
    return pl.pallas_call(
        body, name=name, grid=(rows // tr,),
        in_specs=[pl.BlockSpec((n_parts, tr, cols), lambda i: (0, i, 0)), row, row, row],
        out_specs=(row, row, row, row), out_shape=(out, out, out, out),
        compiler_params=_cparams(("parallel",)),
    )(parts, w, m, v)


_WEIGHTS = ("norm_mix", "w_in", "b_forget", "lam_re", "lam_im", "log_dt", "b_re", "b_im", "c_re", "c_im",
            "d_skip", "w_glu", "w_fox_o", "w_mix_out", "norm_mem_q", "norm_mem_kv", "w_mem_q", "w_mem_kv",
            "w_mem_o", "norm_ffn", "w_ffn_in", "w_ffn_out", "norm_final")
_SHARDED = ("w_in", "w_glu", "w_fox_o", "w_mix_out", "w_mem_q", "w_mem_kv", "w_mem_o", "w_ffn_in", "w_ffn_out")
_SMALL = tuple(n for n in _WEIGHTS if n not in _SHARDED)
_PACK_COLS = 1024


def _pack(arrays):
    flat = jnp.concatenate([a.reshape(-1).astype(F32) for a in arrays])
    rows = -(-flat.shape[0] // _PACK_COLS)
    return jnp.pad(flat, (0, rows * _PACK_COLS - flat.shape[0])).reshape(rows, _PACK_COLS)


def _unpack(buf, like):
    flat = buf.reshape(-1)
    out, pos = [], 0
    for a in like:
        out.append(flat[pos:pos + a.size].reshape(a.shape))
        pos += a.size
    return out


def _mm(name, a, b, mode, m, n, k, out_dtype, tm=1024, tn=512, tk=1024, **kw):
    return _matmul(name, a, b, mode, m, n, k, out_dtype=out_dtype, tm=tm, tn=tn, tk=tk, **kw)


def kernel(x, mem, norm_mix, w_in, b_forget, lam_re, lam_im, log_dt, b_re, b_im, c_re, c_im, d_skip, w_glu, w_fox_o, w_mix_out, norm_mem_q, norm_mem_kv, w_mem_q, w_mem_kv, w_mem_o, norm_ffn, w_ffn_in, w_ffn_out, norm_final, loss_target, m_norm_mix, m_w_in, m_b_forget, m_lam_re, m_lam_im, m_log_dt, m_b_re, m_b_im, m_c_re, m_c_im, m_d_skip, m_w_glu, m_w_fox_o, m_w_mix_out, m_norm_mem_q, m_norm_mem_kv, m_w_mem_q, m_w_mem_kv, m_w_mem_o, m_norm_ffn, m_w_ffn_in, m_w_ffn_out, m_norm_final, v_norm_mix, v_w_in, v_b_forget, v_lam_re, v_lam_im, v_log_dt, v_b_re, v_b_im, v_c_re, v_c_im, v_d_skip, v_w_glu, v_w_fox_o, v_w_mix_out, v_norm_mem_q, v_norm_mem_kv, v_w_mem_q, v_w_mem_kv, v_w_mem_o, v_norm_ffn, v_w_ffn_in, v_w_ffn_out, v_norm_final):
    given = dict(locals())
    weights = {n: given[n] for n in _WEIGHTS}
    mom_m = {n: given["m_" + n] for n in _WEIGHTS}
    mom_v = {n: given["v_" + n] for n in _WEIGHTS}
    seq = x.shape[1]
    nc = seq // SSM_CHUNK
    d = D_MODEL
    xs, mems, tgt = x[0], mem[0], loss_target[0]

    def padcols(a, width):
        return jnp.pad(a, ((0, 0), (0, width - a.shape[1])))

    shards = [padcols(w_in[0].astype(BF16), SHARD_IN_PAD), w_glu[0].astype(BF16), w_fox_o[0].astype(BF16),
              w_mix_out[0].astype(BF16), w_mem_q[0].astype(BF16), w_mem_kv[0].astype(BF16),
              w_mem_o[0].astype(BF16), padcols(w_ffn_in[0].astype(BF16), SHARD_FFN_PAD), w_ffn_out[0].astype(BF16)]
    win = _assemble_win("assemble_w_in", _gather_all("gather_w_in", shards[:1])[0])
    rest = shards[1:]
    gsend, grecv, rest_thru, lands, gtoken = _send_start(
        "gather_rest_start", rest, _place_own("place_weight_shards", rest, stacked_src=False, after=win), scatter=False)

    u = _rms_fwd("rms_mix", xs, norm_mix, after=gtoken)
    ussm = _mm("proj_ssm", u, win, "nn", seq, SSM_WIDTH, d, F32)
    qkv = _mm("proj_qkv", u, win, "nn", seq, 3 * FOX_WIDTH, d, BF16, tn=512, b_off=(0, SSM_WIDTH))
    gates = _mm("proj_gates", u, win, "nn", seq, 2 * d, d, BF16, tn=1024, b_off=(0, PROJ_GATE0))
    fproj = _mm("proj_forget", u, win, "nn", seq, LANE, d, F32, tn=LANE, b_off=(0, PROJ_F0))

    ssm_params = (lam_re[0], lam_im[0], log_dt[0], b_re[0], b_im[0], c_re[0], c_im[0])
    (m_c, bw_c, cm_c, a8, aseg), mats_vjp = jax.vjp(lambda *p: _ssm_mats(*p, nc), *ssm_params)
    m_b = _bd_expand("ssm_expand_m", _BD_M, m_c)
    bw_b = _bd_expand("ssm_expand_bw", _BD_BW, bw_c)
    cm_b = _bd_expand("ssm_expand_cm", _BD_CM, cm_c)
    u8 = ussm.reshape(nc, SSM_CHUNK * SSM_WIDTH)
    d8 = jnp.tile(d_skip, (1, SSM_CHUNK))
    w4 = _ssm_w("ssm_w", u8, bw_b)
    sp4 = _ssm_scan("ssm_scan", w4, a8, aseg, reverse=False)
    y8 = _ssm_y("ssm_y", u8, sp4, m_b, cm_b)
    act = _ssm_post_fwd("ssm_act", y8, u8, d8).reshape(seq, SSM_WIDTH)

    bcol = jnp.pad(b_forget[0], (0, LANE - FOX_HEADS)).reshape(LANE, 1)
    cum_t = _fox_cum("fox_cum", fproj, bcol).reshape(FOX_HEADS // 2, 2, seq)
    att, lse = _fox_fwd("fox_fwd", qkv, cum_t)

    gathered = _send_wait("gather_rest_wait", gsend, grecv, rest_thru, lands, att, scatter=False)
    wglu = _unstack_cols("unstack_w_glu", gathered[0])
    wfoxo = _unstack_cols("unstack_w_fox_o", gathered[1])
    wmix = gathered[2].reshape(d, d)
    wmq = gathered[3].reshape(d, MEM_WIDTH)
    wmkv = gathered[4].reshape(d, 2 * MEM_WIDTH)
    wmo = _unstack_cols("unstack_w_mem_o", gathered[5])
    wffn_in = _assemble_wffn("assemble_w_ffn_in", gathered[6])
    wffn_out = gathered[7].reshape(FFN_HIDDEN, d)

    glu = _mm("glu", act, wglu, "nn", seq, 2 * d, SSM_WIDTH, BF16, tn=1024)
    out_b = _mm("fox_out", att, wfoxo, "nn", seq, d, FOX_WIDTH, BF16, tn=1024)

    mixin = _mix_fwd("mix", glu, gates, out_b)
    h1 = _mm("mix_out", mixin, wmix, "nn", seq, d, d, F32, tn=1024, add=xs)

    n1 = _rms_fwd("rms_mem_q", h1, norm_mem_q)
    q2 = _mm("mem_q", n1, wmq, "nn", seq, MEM_WIDTH, d, BF16)
    mn = _rms_fwd("rms_mem_kv", mems, norm_mem_kv)
    mlen = mems.shape[0]
    kv = _mm("mem_kv", mn, wmkv, "nn", mlen, 2 * MEM_WIDTH, d, BF16)
    o2 = _mem_fwd("mem_attn", q2, kv)
    h2 = _mm("mem_out", o2, wmo, "nn", seq, d, MEM_WIDTH, F32, tn=1024, add=h1)

    n2 = _rms_fwd("rms_ffn", h2, norm_ffn)
    f, g_act = _ffn_in_swiglu("ffn_in_swiglu", n2, wffn_in)
    h3 = _mm("ffn_out", g_act, wffn_out, "nn", seq, d, FFN_HIDDEN, F32, tk=FFN_HIDDEN, add=h2)
    loss_part, dh3, dg_final = _final_loss("final_loss", h3, tgt, norm_final.reshape(1, d))

    df = _d_ffn_out_swiglu("d_ffn_out_swiglu", dh3, wffn_out, f)
    dwffn_out = _mm("d_ffn_out_w", g_act, dh3, "tn", FFN_HIDDEN, d, seq, BF16, tm=1408, tn=1024)
    dn2 = _mm("d_ffn_in_x", df, wffn_in, "nt", seq, d, 2 * FFN_HIDDEN, F32, tn=1024, tk=FFN_HIDDEN)
    dwffn_in = _mm("d_ffn_in_w", n2, df, "tn", d, 2 * FFN_HIDDEN, seq, BF16, tn=1408)
    dh2, dg_ffn = _rms_bwd("d_rms_ffn", dn2, h2, norm_ffn, res=dh3)

    do2 = _mm("d_mem_out_x", dh2, wmo, "nt", seq, MEM_WIDTH, d, F32)
    dwmo = _restack_cols("restack_d_w_mem_o", _mm("d_mem_out_w", o2, dh2, "tn", MEM_WIDTH, d, seq, BF16, tn=1024))
    dq2, dkv = _mem_bwd("d_mem_attn", q2, kv, do2)
    dwmq = _mm("d_mem_q_w", n1, dq2, "tn", d, MEM_WIDTH, seq, BF16)
    dn1 = _mm("d_mem_q_x", dq2, wmq, "nt", seq, d, MEM_WIDTH, F32)
    dwmkv = _mm("d_mem_kv_w", mn, dkv, "tn", d, 2 * MEM_WIDTH, mlen, BF16, tn=1024)
    dmn = _mm("d_mem_kv_x", dkv, wmkv, "nt", mlen, d, 2 * MEM_WIDTH, F32)
    _, dg_memkv = _rms_bwd("d_rms_mem_kv", dmn, mems, norm_mem_kv)

    early = [dwmq.reshape(N_DEV, d // N_DEV, MEM_WIDTH), dwmkv.reshape(N_DEV, d // N_DEV, 2 * MEM_WIDTH), dwmo,
             _disassemble_dwffn("split_d_w_ffn_in", dwffn_in), dwffn_out.reshape(N_DEV, FFN_HIDDEN // N_DEV, d)]
    ssend, srecv, early_thru, early_lands, stoken = _send_start(
        "scatter_early_start", early, _place_own("place_early_grads", early, stacked_src=True), scatter=True)
    dh1, dg_memq = _rms_bwd("d_rms_mem_q", dn1, h1, norm_mem_q, res=dh2, after=stoken)

    dmixin = _mm("d_mix_out_x", dh1, wmix, "nt", seq, d, d, F32, tn=1024)
    dwmix = _mm("d_mix_out_w", mixin, dh1, "tn", d, d, seq, BF16, tn=1024)
    dglu, dgates, dout_b = _mix_bwd("d_mix", dmixin, glu, gates, out_b)
    datt = _mm("d_fox_out_x", dout_b, wfoxo, "nt", seq, FOX_WIDTH, d, F32)
    dwfoxo = _restack_cols("restack_d_w_fox_o", _mm("d_fox_out_w", att, dout_b, "tn", FOX_WIDTH, d, seq, BF16, tn=1024))
    dact = _mm("d_glu_x", dglu, wglu, "nt", seq, SSM_WIDTH, 2 * d, F32, tk=2 * d)
    dwglu = _restack_cols("restack_d_w_glu", _mm("d_glu_w", act, dglu, "tn", SSM_WIDTH, 2 * d, seq, BF16, tn=2 * d))

    mid = [dwglu, dwfoxo, dwmix.reshape(N_DEV, d // N_DEV, d)]
    msend, mrecv, mid_thru, mid_lands, mtoken = _send_start(
        "scatter_mid_start", mid, _place_own("place_mid_grads", mid, stacked_src=True), scatter=True)

    dz8, dg_dskip = _ssm_post_bwd("d_ssm_act", dact.reshape(nc, SSM_CHUNK * SSM_WIDTH), y8, u8, d8, after=mtoken)
    ds4, dcm = _ssm_ds("d_ssm_y_state", dz8, sp4, cm_b)
    g4, da8 = _ssm_scan("d_ssm_scan", ds4, a8, aseg, reverse=True, sprev4=sp4)
    dx8, dm, dbw = _ssm_dx("d_ssm_x", dz8, g4, u8, m_b, bw_b, d8)
    dussm = dx8.reshape(seq, SSM_WIDTH)
    g_ssm = mats_vjp((_bd_reduce("ssm_reduce_dm", _BD_M, dm), _bd_reduce("ssm_reduce_dbw", _BD_BW, dbw),
                      _bd_reduce("ssm_reduce_dcm", _BD_CM, dcm), da8, jnp.zeros_like(aseg)))

    dq, dk, dv, dcum = _fox_bwd("d_fox", qkv, cum_t, att, datt, lse)
    dfproj, dbf = _fox_cum_bwd("d_fox_cum", dcum.reshape(FOX_HEADS, seq), fproj, bcol)
    dg_bforget = dbf[0:FOX_HEADS, 0].reshape(1, FOX_HEADS)

    dproj = _concat_cols("d_proj_concat", (dussm, dq, dk, dv, dgates, dfproj))
    du = _mm("d_proj_x", dproj, win, "nt", seq, d, PROJ_WIDTH, F32, tn=1024, tk=1408)
    dwin = _mm("d_proj_w", u, dproj, "tn", d, PROJ_WIDTH, seq, BF16, tn=1408)
    dx, dg_mix = _rms_bwd("d_rms_mix", du, xs, norm_mix, res=dh1)

    late = [_disassemble_dwin("split_d_w_in", dwin)]
    sib = _scatter_sibling("scatter_late_sibling", late)
    late_parts = _scatter_chips("scatter_late_chips", _add_chip_partials("sum_late_chip", late, sib))
    early_parts = _send_wait("scatter_early_wait", ssend, srecv, early_thru, early_lands, dx, scatter=True)
    mid_parts = _send_wait("scatter_mid_wait", msend, mrecv, mid_thru, mid_lands, dx, scatter=True)
    received = {"w_in": late_parts[0]}
    received.update(zip(("w_glu", "w_fox_o", "w_mix_out"), mid_parts))
    received.update(zip(("w_mem_q", "w_mem_kv", "w_mem_o", "w_ffn_in", "w_ffn_out"), early_parts))

    small_grads = dict(zip(
        _SMALL, (dg_mix, dg_bforget, g_ssm[0][None], g_ssm[1][None], g_ssm[2][None], g_ssm[3][None], g_ssm[4][None],
                 g_ssm[5][None], g_ssm[6][None], dg_dskip, dg_memq, dg_memkv, dg_ffn, dg_final.reshape(d))))
    small_like = [weights[n] for n in _SMALL]
    small_all = _gather_all("gather_small_grads", [_pack([small_grads[n] for n in _SMALL])])[0]
    pk = [_pack([src[n] for n in _SMALL]) for src in (weights, mom_m, mom_v)]
    small_out = _adamw("adamw_small", small_all, pk[0], pk[1], pk[2], tr=small_all.shape[1])
    small_res = [dict(zip(_SMALL, _unpack(buf, small_like))) for buf in small_out]

    results = [dict(r) for r in small_res]
    tiles = {"w_in": 128, "w_glu": 128, "w_fox_o": 128, "w_mix_out": 128, "w_mem_q": 128, "w_mem_kv": 128,
             "w_mem_o": 128, "w_ffn_in": 128, "w_ffn_out": 176}
    pads = {"w_in": SHARD_IN_PAD, "w_ffn_in": SHARD_FFN_PAD}
    for name in _SHARDED:
        parts = received[name]
        w2, m2, v2 = weights[name][0], mom_m[name][0], mom_v[name][0]
        cols = w2.shape[1]
        if name in pads:
            w2, m2, v2 = (padcols(t, pads[name]) for t in (w2, m2, v2))
        outs = _adamw("adamw_" + name, parts, w2, m2, v2, tr=tiles[name])
        for res, o in zip(results, outs):
            res[name] = o[:, :cols][None]

    loss = lax.psum(loss_part[0, 0], ("x", "y", "c"))
    out = [loss, dx[None]]
    for res in results:
        out.extend(res[n] for n in _WEIGHTS)
    return tuple(out)
```

```python
import math

import jax
import jax.numpy as jnp
import numpy as np
from jax import lax
from jax.experimental import pallas as pl
from jax.experimental.pallas import tpu as pltpu

F32 = jnp.float32
BF16 = jnp.bfloat16

N_DEV = 8
LANE = 128
VMEM_LIMIT = 56 * 1024 * 1024

D_MODEL = 1024
SSM_GROUP = 16
SSM_GROUPS = 32
SSM_WIDTH = 512
SSM_STATE = 64
SSM_CHUNK = 8
FOX_HEADS = 8
FOX_HEAD_DIM = 64
FOX_WIDTH = 512
MEM_HEADS = 4
MEM_HEAD_DIM = 128
MEM_WIDTH = 512
FFN_HIDDEN = 2816
RMS_EPS = 1e-6
IN_WIDTH = 4104
SHARD_IN = IN_WIDTH // N_DEV
SHARD_IN_PAD = 640
SHARD_FFN = 2 * FFN_HIDDEN // N_DEV
SHARD_FFN_PAD = 768
PROJ_GATE0 = 2048
PROJ_F0 = 4096
PROJ_WIDTH = 4224

ADAM_LR = 0.001
ADAM_B1 = 0.9
ADAM_B2 = 0.999
ADAM_EPS = 1e-08
ADAM_WD = 0.01
ADAM_STEP = 10


def _cparams(sem=None):
    return pltpu.CompilerParams(dimension_semantics=sem, vmem_limit_bytes=VMEM_LIMIT)


def _sigmoid(x):
    return 1.0 / (1.0 + jnp.exp(-x))


def _bdot(a, b, dims):
    return lax.dot_general(a.astype(BF16), b.astype(BF16), ((dims[0], dims[1]), ((), ())),
                           preferred_element_type=F32)


_DIMS = {"nn": ((1,), (0,)), "nt": ((1,), (1,)), "tn": ((0,), (0,))}


def _matmul(name, a, b, mode, m, n, k, *, out_dtype, tm, tn, tk, a_off=(0, 0), b_off=(0, 0), add=None):
    tm, tn, tk = min(tm, m), min(tn, n), min(tk, k)
    assert m % tm == 0 and n % tn == 0 and k % tk == 0, (name, m, n, k, tm, tn, tk)
    nk = k // tk
    grid = (m // tm, n // tn, nk)

    def blk(off, t):
        assert off % t == 0, (name, off, t)
        return off // t

    if mode in ("nn", "nt"):
        ar, ac = blk(a_off[0], tm), blk(a_off[1], tk)
        a_spec = pl.BlockSpec((tm, tk), lambda i, j, kk: (i + ar, kk + ac))
    else:
        ar, ac = blk(a_off[0], tk), blk(a_off[1], tm)
        a_spec = pl.BlockSpec((tk, tm), lambda i, j, kk: (kk + ar, i + ac))

    if mode in ("nn", "tn"):
        br, bc = blk(b_off[0], tk), blk(b_off[1], tn)
        b_spec = pl.BlockSpec((tk, tn), lambda i, j, kk: (kk + br, j + bc))
    else:
        br, bc = blk(b_off[0], tn), blk(b_off[1], tk)
        b_spec = pl.BlockSpec((tn, tk), lambda i, j, kk: (j + br, kk + bc))
    o_spec = pl.BlockSpec((tm, tn), lambda i, j, kk: (i, j))
    out_shape = jax.ShapeDtypeStruct((m, n), out_dtype)

    in_specs = [a_spec, b_spec]
    operands = [a, b]
    if add is not None:
        in_specs.append(pl.BlockSpec((tm, tn), lambda i, j, kk: (i, j)))
        operands.append(add)
    dims = _DIMS[mode]
    has_add = add is not None

    def body(*refs):
        a_ref, b_ref = refs[0], refs[1]
        add_ref = refs[2] if has_add else None
        o_ref = refs[3] if has_add else refs[2]
        acc_ref = refs[-1] if nk > 1 else None
        prod = _bdot(a_ref[...], b_ref[...], dims)

        def finish(total):
            if has_add:
                total = total + add_ref[...].astype(F32)
            o_ref[...] = total.astype(o_ref.dtype)

        if nk == 1:
            finish(prod)
        else:
            kk = pl.program_id(2)

            @pl.when(kk == 0)
            def _():
                acc_ref[...] = prod

            @pl.when(jnp.logical_and(kk > 0, kk < nk - 1))
            def _():
                acc_ref[...] += prod

            @pl.when(kk == nk - 1)
            def _():
                finish(acc_ref[...] + prod)

    scratch = [pltpu.VMEM((tm, tn), F32)] if nk > 1 else []
    return pl.pallas_call(
        body, name=name, grid=grid, in_specs=in_specs, out_specs=o_spec, out_shape=out_shape,
        scratch_shapes=scratch,
        compiler_params=_cparams(("parallel", "parallel", "arbitrary")),
    )(*operands)


def _rms_fwd(name, x, gain, *, tr=512, after=None):
    r, d = x.shape
    tr = min(tr, r)

    def body(x_ref, g_ref, *rest):
        o_ref = rest[-1]
        xv = x_ref[...]
        rstd = lax.rsqrt(jnp.mean(xv * xv, axis=-1, keepdims=True) + RMS_EPS)
        o_ref[...] = (xv * rstd * g_ref[...]).astype(o_ref.dtype)

    in_specs = [pl.BlockSpec((tr, d), lambda i: (i, 0)), pl.BlockSpec((1, d), lambda i: (0, 0))]
    ops = [x, gain]
    if after is not None:
        in_specs.append(pl.BlockSpec(after.shape, lambda i: (0, 0)))
        ops.append(after)
    return pl.pallas_call(
        body, name=name, grid=(r // tr,), in_specs=in_specs,
        out_specs=pl.BlockSpec((tr, d), lambda i: (i, 0)),
        out_shape=jax.ShapeDtypeStruct((r, d), BF16),
        compiler_params=_cparams(("parallel",)),
    )(*ops)


def _rms_bwd(name, dy, x, gain, res=None, *, tr=512, after=None):
    r, d = x.shape
    tr = min(tr, r)
    n = r // tr
    has_res = res is not None

    def body(*refs):
        dy_ref, x_ref, g_ref = refs[:3]
        res_ref = refs[3] if has_res else None
        dx_ref, dg_ref, acc_ref = refs[-3:]
        i = pl.program_id(0)
        xv = x_ref[...]
        rstd = lax.rsqrt(jnp.mean(xv * xv, axis=-1, keepdims=True) + RMS_EPS)
        xh = xv * rstd
        dyv = dy_ref[...].astype(F32)
        dxh = dyv * g_ref[...]
        dx = rstd * (dxh - xh * jnp.mean(dxh * xh, axis=-1, keepdims=True))
        if has_res:
            dx = dx + res_ref[...]
        dx_ref[...] = dx
        part = (dyv * xh).reshape(tr // 8, 8, d).sum(axis=0)

        @pl.when(i == 0)
        def _():
            acc_ref[...] = part

        @pl.when(i > 0)
        def _():
            acc_ref[...] += part

        @pl.when(i == n - 1)
        def _():
            dg_ref[...] = jnp.sum(acc_ref[...], axis=0, keepdims=True)

    row = pl.BlockSpec((tr, d), lambda i: (i, 0))
    in_specs = [row, row, pl.BlockSpec((1, d), lambda i: (0, 0))] + ([row] if has_res else [])
    ops = [dy, x, gain] + ([res] if has_res else [])
    if after is not None:
        in_specs.append(pl.BlockSpec(after.shape, lambda i: (0, 0)))
        ops.append(after)
    return pl.pallas_call(
        body, name=name, grid=(n,), in_specs=in_specs,
        out_specs=(row, pl.BlockSpec((1, d), lambda i: (0, 0))),
        out_shape=(jax.ShapeDtypeStruct((r, d), F32), jax.ShapeDtypeStruct((1, d), F32)),
        scratch_shapes=[pltpu.VMEM((8, d), F32)],
        compiler_params=_cparams(("arbitrary",)),
    )(*ops)


def _final_loss(name, h, target, gain, *, tr=512):
    r, d = h.shape
    tr = min(tr, r)
    n = r // tr

    def body(h_ref, t_ref, g_ref, loss_ref, dh_ref, dg_ref, accl_ref, accg_ref):
        i = pl.program_id(0)
        xv = h_ref[...]
        rstd = lax.rsqrt(jnp.mean(xv * xv, axis=-1, keepdims=True) + RMS_EPS)
        xh = xv * rstd
        e = xh * g_ref[...] - t_ref[...]
        dyv = e * (1.0 / d)
        dxh = dyv * g_ref[...]
        dh_ref[...] = rstd * (dxh - xh * jnp.mean(dxh * xh, axis=-1, keepdims=True))
        lpart = (e * e).reshape(tr // 8, 8, d).sum(axis=0)
        gpart = (dyv * xh).reshape(tr // 8, 8, d).sum(axis=0)

        @pl.when(i == 0)
        def _():
            accl_ref[...] = lpart
            accg_ref[...] = gpart

        @pl.when(i > 0)
        def _():
            accl_ref[...] += lpart
            accg_ref[...] += gpart

        @pl.when(i == n - 1)
        def _():
            tot = jnp.sum(jnp.sum(accl_ref[...], axis=0, keepdims=True), axis=1, keepdims=True)
            loss_ref[...] = jnp.broadcast_to(tot * (0.5 / d), (1, LANE))
            dg_ref[...] = jnp.sum(accg_ref[...], axis=0, keepdims=True)

    row = pl.BlockSpec((tr, d), lambda i: (i, 0))
    one = pl.BlockSpec((1, d), lambda i: (0, 0))
    return pl.pallas_call(
        body, name=name, grid=(n,), in_specs=[row, row, one],
        out_specs=(pl.BlockSpec((1, LANE), lambda i: (0, 0)), row, one),
        out_shape=(jax.ShapeDtypeStruct((1, LANE), F32), jax.ShapeDtypeStruct((r, d), F32),
                   jax.ShapeDtypeStruct((1, d), F32)),
        scratch_shapes=[pltpu.VMEM((8, d), F32), pltpu.VMEM((8, d), F32)],
        compiler_params=_cparams(("arbitrary",)),
    )(h, target, gain)


_GELU_C = math.sqrt(2.0 / math.pi)


def _gelu_parts(z):
    inner = _GELU_C * (z + 0.044715 * z * z * z)
    t = jnp.tanh(inner)
    val = 0.5 * z * (1.0 + t)
    dinner = _GELU_C * (1.0 + 3.0 * 0.044715 * z * z)
    grad = 0.5 * (1.0 + t) + 0.5 * z * (1.0 - t * t) * dinner
    return val, grad


def _ssm_post_fwd(name, y8, u8, d8, *, tr=256):
    r, c = y8.shape
    tr = min(tr, r)

    def body(y_ref, u_ref, d_ref, o_ref):
        z = y_ref[...] + d_ref[...] * u_ref[...]
        o_ref[...] = _gelu_parts(z)[0].astype(o_ref.dtype)

    row = pl.BlockSpec((tr, c), lambda i: (i, 0))
    return pl.pallas_call(
        body, name=name, grid=(r // tr,), in_specs=[row, row, pl.BlockSpec((1, c), lambda i: (0, 0))],
        out_specs=row, out_shape=jax.ShapeDtypeStruct((r, c), BF16),
        compiler_params=_cparams(("parallel",)),
    )(y8, u8, d8)


def _ssm_post_bwd(name, dact8, y8, u8, d8, *, tr=256, after=None):
    r, c = y8.shape
    tr = min(tr, r)
    n = r // tr

    def body(*refs):
        da_ref, y_ref, u_ref, d_ref = refs[:4]
        dz_ref, dd_ref, acc_ref = refs[-3:]
        i = pl.program_id(0)
        uv = u_ref[...]
        z = y_ref[...] + d_ref[...] * uv
        dz = da_ref[...].astype(F32) * _gelu_parts(z)[1]
        dz_ref[...] = dz
        part = (dz * uv).reshape(tr // 8, 8, c).sum(axis=0)

        @pl.when(i == 0)
        def _():
            acc_ref[...] = part

        @pl.when(i > 0)
        def _():
            acc_ref[...] += part

        @pl.when(i == n - 1)
        def _():
            tot = jnp.sum(acc_ref[...], axis=0, keepdims=True)
            out = tot[:, 0:SSM_WIDTH]
            for j in range(1, c // SSM_WIDTH):
                out = out + tot[:, j * SSM_WIDTH:(j + 1) * SSM_WIDTH]
            dd_ref[...] = out

    row = pl.BlockSpec((tr, c), lambda i: (i, 0))
    in_specs = [row, row, row, pl.BlockSpec((1, c), lambda i: (0, 0))]
    ops = [dact8, y8, u8, d8]
    if after is not None:
        in_specs.append(pl.BlockSpec(memory_space=pl.ANY))
        ops.append(after)
    return pl.pallas_call(
        body, name=name, grid=(n,), in_specs=in_specs,
        out_specs=(row, pl.BlockSpec((1, SSM_WIDTH), lambda i: (0, 0))),
        out_shape=(jax.ShapeDtypeStruct((r, c), F32), jax.ShapeDtypeStruct((1, SSM_WIDTH), F32)),
        scratch_shapes=[pltpu.VMEM((8, c), F32)],
        compiler_params=_cparams(("arbitrary",)),
    )(*ops)


def _mix_fwd(name, glu, gates, out_b, *, tr=256):
    r = glu.shape[0]
    d = D_MODEL
    tr = min(tr, r)

    def body(glu_ref, gate_ref, ob_ref, o_ref):
        out_a = glu_ref[:, 0:d].astype(F32) * _sigmoid(glu_ref[:, d:2 * d].astype(F32))
        mix = (_sigmoid(gate_ref[:, 0:d].astype(F32)) * out_a
               + _sigmoid(gate_ref[:, d:2 * d].astype(F32)) * ob_ref[...].astype(F32))
        o_ref[...] = mix.astype(o_ref.dtype)

    wide = pl.BlockSpec((tr, 2 * d), lambda i: (i, 0))
    row = pl.BlockSpec((tr, d), lambda i: (i, 0))
    return pl.pallas_call(
        body, name=name, grid=(r // tr,), in_specs=[wide, wide, row], out_specs=row,
        out_shape=jax.ShapeDtypeStruct((r, d), BF16), compiler_params=_cparams(("parallel",)),
    )(glu, gates, out_b)


def _mix_bwd(name, dmix, glu, gates, out_b, *, tr=256):
    r = glu.shape[0]
    d = D_MODEL
    tr = min(tr, r)

    def body(dm_ref, glu_ref, gate_ref, ob_ref, dglu_ref, dgate_ref, dob_ref):
        dm = dm_ref[...]
        glu_a = glu_ref[:, 0:d].astype(F32)
        sb = _sigmoid(glu_ref[:, d:2 * d].astype(F32))
        ga = _sigmoid(gate_ref[:, 0:d].astype(F32))
        gb = _sigmoid(gate_ref[:, d:2 * d].astype(F32))
        out_a = glu_a * sb
        dout_a = dm * ga
        dglu_ref[:, 0:d] = (dout_a * sb).astype(dglu_ref.dtype)
        dglu_ref[:, d:2 * d] = (dout_a * glu_a * sb * (1.0 - sb)).astype(dglu_ref.dtype)
        dgate_ref[:, 0:d] = (dm * out_a * ga * (1.0 - ga)).astype(dgate_ref.dtype)
        dgate_ref[:, d:2 * d] = (dm * ob_ref[...].astype(F32) * gb * (1.0 - gb)).astype(dgate_ref.dtype)
        dob_ref[...] = (dm * gb).astype(dob_ref.dtype)

    wide = pl.BlockSpec((tr, 2 * d), lambda i: (i, 0))
    row = pl.BlockSpec((tr, d), lambda i: (i, 0))
    return pl.pallas_call(
        body, name=name, grid=(r // tr,), in_specs=[row, wide, wide, row], out_specs=(wide, wide, row),
        out_shape=(jax.ShapeDtypeStruct((r, 2 * d), BF16), jax.ShapeDtypeStruct((r, 2 * d), BF16),
                   jax.ShapeDtypeStruct((r, d), BF16)),
        compiler_params=_cparams(("parallel",)),
    )(dmix, glu, gates, out_b)


def _ssm_mats(lam_re, lam_im, log_dt, b_re, b_im, c_re, c_im, nc):
    hp = lax.Precision.HIGHEST
    t = SSM_CHUNK
    nq = SSM_GROUPS // 8
    lam = lax.complex(lam_re, lam_im)
    z = lam * jnp.exp(log_dt)[:, None]
    ks = jnp.arange(t + 1, dtype=F32)
    apow = jnp.exp(ks[:, None, None] * z[None])
    bbar = ((apow[1] - 1.0) / lam)[..., None] * lax.complex(b_re, b_im)
    c = lax.complex(c_re, c_im)

    ca = c[None] * apow[:, :, None, :]
    kmat = jnp.einsum("kgnp,gpm->kgnm", ca, bbar, precision=hp).real
    ii = np.arange(t)
    lag = ii[None, :] - ii[:, None]
    kt = kmat[np.clip(lag, 0, t)] * jnp.asarray(lag >= 0, F32)[:, :, None, None, None]
    kt = kt.reshape(t, t, nq, 8, SSM_GROUP, SSM_GROUP)
    m_c = kt.transpose(2, 0, 3, 5, 1, 4).reshape(nq, 1024, LANE)

    arev = jnp.exp((float(t - 1) - ks[:t])[:, None, None] * z[None])
    w = arev[:, :, :, None] * bbar[None]
    wr = jnp.stack([w.real, w.imag]).reshape(2, t, nq, 8, SSM_STATE, SSM_GROUP)
    bw_c = wr.transpose(2, 1, 3, 5, 0, 4).reshape(nq, 1024, LANE)

    ca1 = ca[1:]
    cr = jnp.stack([ca1.real, -ca1.imag]).reshape(2, t, nq, 8, SSM_GROUP, SSM_STATE)
    cm_c = cr.transpose(2, 0, 3, 5, 1, 4).reshape(nq, 1024, LANE)

    def tiles(v):
        vq = jnp.concatenate([v.real.reshape(nq, 512), v.imag.reshape(nq, 512)], axis=1)
        return jnp.broadcast_to(vq.reshape(nq, 8, 1, LANE), (nq, 8, 8, LANE))

    return m_c, bw_c, cm_c, tiles(apow[t]), tiles(jnp.exp(float(nc) * z))


_BD_M = (LANE, SSM_GROUP)
_BD_BW = (LANE, SSM_STATE)
_BD_CM = (512, SSM_GROUP)


def _bd_perm(cn):
    rr = lax.broadcasted_iota(jnp.int32, (1024, 1024), 0)
    cc = lax.broadcasted_iota(jnp.int32, (1024, 1024), 1)
    sh = cn.bit_length() - 1
    src = ((rr >> 7) << sh) + (((rr & (LANE - 1)) >> sh) << (3 + sh)) + (rr & (cn - 1))
    return jnp.where(src == cc, 1.0, 0.0).astype(BF16)


def _bd_rowgroup(span):
    r = lax.broadcasted_iota(jnp.int32, (1024, LANE), 0)
    return (r & (span - 1)) >> ((span // 8).bit_length() - 1)


def _bd_expand(name, kind, compact):
    span, cn = kind
    nq = compact.shape[0]

    def body(c_ref, o_ref):
        x = c_ref[...]
        grp = _bd_rowgroup(span)
        xcat = jnp.concatenate([jnp.where(grp == h, x, 0.0) for h in range(8)], axis=1)
        o_ref[...] = _bdot(xcat, _bd_perm(cn), _DIMS["nn"]).astype(o_ref.dtype)

    return pl.pallas_call(
        body, name=name, grid=(nq,), in_specs=[pl.BlockSpec((None, 1024, LANE), lambda q: (q, 0, 0))],
        out_specs=pl.BlockSpec((None, 1024, 1024), lambda q: (q, 0, 0)),
        out_shape=jax.ShapeDtypeStruct((nq, 1024, 1024), BF16),
        compiler_params=_cparams(("parallel",)),
    )(compact)


def _bd_reduce(name, kind, dbig):
    span, cn = kind
    nq = dbig.shape[0]

    def body(g_ref, o_ref):
        perm = _bd_perm(cn)
        hi, mid, lo = _split3(g_ref[...])
        d = _DIMS["nt"]
        back = _bdot(hi, perm, d) + _bdot(mid, perm, d) + _bdot(lo, perm, d)
        grp = _bd_rowgroup(span)
        out = jnp.zeros((1024, LANE), F32)
        for h in range(8):
            out = jnp.where(grp == h, back[:, h * LANE:(h + 1) * LANE], out)
        o_ref[...] = out

    return pl.pallas_call(
        body, name=name, grid=(nq,), in_specs=[pl.BlockSpec((None, 1024, 1024), lambda q: (q, 0, 0))],
        out_specs=pl.BlockSpec((None, 1024, LANE), lambda q: (q, 0, 0)),
        out_shape=jax.ShapeDtypeStruct((nq, 1024, LANE), F32),
        compiler_params=_cparams(("parallel",)),
    )(dbig)


def _x_tile_specs(nc, nq):
    return [pl.BlockSpec((nc, LANE), lambda q, t, i=i: (0, i * nq + q)) for i in range(SSM_CHUNK)]


def _cat_tiles(refs):
    return jnp.concatenate([r[...] for r in refs], axis=1)


def _ssm_w(name, x8, bw):
    nc = x8.shape[0]
    nq = bw.shape[0]

    def body(*refs):
        xq = _cat_tiles(refs[:8])
        refs[9][...] = _bdot(xq, refs[8][...], _DIMS["nn"])

    return pl.pallas_call(
        body, name=name, grid=(nq, 8),
        in_specs=_x_tile_specs(nc, nq) + [pl.BlockSpec((None, 1024, LANE), lambda q, t: (q, 0, t))],
        out_specs=pl.BlockSpec((None, None, nc, LANE), lambda q, t: (q, t, 0, 0)),
        out_shape=jax.ShapeDtypeStruct((nq, 8, nc, LANE), F32),
        compiler_params=_cparams(("parallel", "arbitrary")),
    )(*([x8] * 8), bw)


def _ssm_scan(name, w4, a_t, aseg_t, *, reverse, sprev4=None):
    nq, _, nc, _ = w4.shape
    ns = nc // 8
    with_da = sprev4 is not None

    def body(*refs):
        w_ref, a_ref, aseg_ref = refs[:3]
        s_ref = refs[3] if with_da else None
        o_ref = refs[4] if with_da else refs[3]
        da_ref = refs[5] if with_da else None
        sgn = -1.0 if reverse else 1.0
        ar = [a_ref[j] for j in range(4)]
        ai = [sgn * a_ref[j + 4] for j in range(4)]
        gr = [aseg_ref[j] for j in range(4)]
        gi = [sgn * aseg_ref[j + 4] for j in range(4)]
        zero = tuple(jnp.zeros((8, LANE), F32) for _ in range(8))

        def rows(tt):
            return pl.ds((ns - 1 - tt) if reverse else tt, 8, stride=ns)

        def step(carry, w):
            new_r = [ar[j] * carry[j] - ai[j] * carry[j + 4] + w[j] for j in range(4)]
            new_i = [ar[j] * carry[j + 4] + ai[j] * carry[j] + w[j + 4] for j in range(4)]
            return tuple(new_r + new_i)

        def pass1(tt, carry):
            return step(carry, [w_ref[j, rows(tt), :] for j in range(8)])

        ends = lax.fori_loop(0, ns, pass1, zero)
        sub = lax.broadcasted_iota(jnp.int32, (8, LANE), 0)
        init = list(zero)
        order = range(7, 0, -1) if reverse else range(0, 7)
        for s in order:
            nxt = s - 1 if reverse else s + 1
            cand_r = [gr[j] * init[j] - gi[j] * init[j + 4] + ends[j] for j in range(4)]
            cand_i = [gr[j] * init[j + 4] + gi[j] * init[j] + ends[j + 4] for j in range(4)]
            cand = cand_r + cand_i
            shift = 7 if reverse else 1
            init = [jnp.where(sub == nxt, pltpu.roll(cand[j], shift, axis=0), init[j]) for j in range(8)]

        def pass2(tt, state):
            carry, acc = state
            r = rows(tt)
            for j in range(8):
                o_ref[j, r, :] = carry[j]
            if with_da:
                sp = [s_ref[j, r, :] for j in range(8)]
                acc_r = [acc[j] + carry[j] * sp[j] + carry[j + 4] * sp[j + 4] for j in range(4)]
                acc_i = [acc[j + 4] + carry[j + 4] * sp[j] - carry[j] * sp[j + 4] for j in range(4)]
                acc = tuple(acc_r + acc_i)
            return step(carry, [w_ref[j, r, :] for j in range(8)]), acc

        _, acc = lax.fori_loop(0, ns, pass2, (tuple(init), zero))
        if with_da:
            for j in range(8):
                da_ref[j] = acc[j]

    big = pl.BlockSpec((None, 8, nc, LANE), lambda q: (q, 0, 0, 0))
    small = pl.BlockSpec((None, 8, 8, LANE), lambda q: (q, 0, 0, 0))
    in_specs = [big, small, small] + ([big] if with_da else [])
    ops = [w4, a_t, aseg_t] + ([sprev4] if with_da else [])
    out_specs = (big, small) if with_da else big
    big_s = jax.ShapeDtypeStruct((nq, 8, nc, LANE), F32)
    out_shape = (big_s, jax.ShapeDtypeStruct((nq, 8, 8, LANE), F32)) if with_da else big_s
    return pl.pallas_call(
        body, name=name, grid=(nq,), in_specs=in_specs, out_specs=out_specs, out_shape=out_shape,
        compiler_params=_cparams(("parallel",)),
    )(*ops)


def _ssm_y(name, x8, sprev4, m_mat, cm_mat):
    nc = x8.shape[0]
    nq = m_mat.shape[0]

    def body(*refs):
        xq = _cat_tiles(refs[:8])
        s_ref, m_ref, cm_ref, o_ref = refs[8:12]
        sq = jnp.concatenate([s_ref[t] for t in range(8)], axis=1)
        o_ref[...] = _bdot(xq, m_ref[...], _DIMS["nn"]) + _bdot(sq, cm_ref[...], _DIMS["nn"])

    col = pl.BlockSpec((None, 1024, LANE), lambda q, j: (q, 0, j))
    return pl.pallas_call(
        body, name=name, grid=(nq, 8),
        in_specs=_x_tile_specs(nc, nq) + [pl.BlockSpec((None, 8, nc, LANE), lambda q, j: (q, 0, 0, 0)), col, col],
        out_specs=pl.BlockSpec((nc, LANE), lambda q, j: (0, j * nq + q)),
        out_shape=jax.ShapeDtypeStruct((nc, 8 * SSM_WIDTH), F32),
        compiler_params=_cparams(("parallel", "arbitrary")),
    )(*([x8] * 8), sprev4, m_mat, cm_mat)


def _ssm_ds(name, dz8, sprev4, cm_mat):
    nc = dz8.shape[0]
    nq = cm_mat.shape[0]

    def body(*refs):
        dyq = _cat_tiles(refs[:8]).astype(BF16)
        s_ref, cm_ref, ds_ref, dcm_ref = refs[8:12]
        ds_ref[...] = _bdot(dyq, cm_ref[...], _DIMS["nt"])
        dcm_ref[...] = _bdot(s_ref[...], dyq, _DIMS["tn"])

    tile = pl.BlockSpec((None, None, nc, LANE), lambda q, t: (q, t, 0, 0))
    rowblk = pl.BlockSpec((None, LANE, 1024), lambda q, t: (q, t, 0))
    return pl.pallas_call(
        body, name=name, grid=(nq, 8),
        in_specs=_x_tile_specs(nc, nq) + [tile, rowblk],
        out_specs=(tile, rowblk),
        out_shape=(jax.ShapeDtypeStruct((nq, 8, nc, LANE), F32), jax.ShapeDtypeStruct((nq, 1024, 1024), F32)),
        compiler_params=_cparams(("parallel", "arbitrary")),
    )(*([dz8] * 8), sprev4, cm_mat)


def _ssm_dx(name, dz8, g4, x8, m_mat, bw_mat, d8):
    nc = dz8.shape[0]
    nq = m_mat.shape[0]

    def body(*refs):
        dyq = _cat_tiles(refs[:8]).astype(BF16)
        g_ref, x_ref, m_ref, bw_ref, d_ref, dzi_ref, dx_ref, dm_ref, dbw_ref = refs[8:17]
        gq = jnp.concatenate([g_ref[t] for t in range(8)], axis=1).astype(BF16)
        dx = _bdot(dyq, m_ref[...], _DIMS["nt"]) + _bdot(gq, bw_ref[...], _DIMS["nt"])
        dx_ref[...] = (dx + d_ref[...] * dzi_ref[...]).astype(dx_ref.dtype)
        xi = x_ref[...]
        dm_ref[...] = _bdot(xi, dyq, _DIMS["tn"])
        dbw_ref[...] = _bdot(xi, gq, _DIMS["tn"])

    xtile = pl.BlockSpec((nc, LANE), lambda q, i: (0, i * nq + q))
    rowblk = pl.BlockSpec((None, LANE, 1024), lambda q, i: (q, i, 0))
    return pl.pallas_call(
        body, name=name, grid=(nq, 8),
        in_specs=_x_tile_specs(nc, nq) + [pl.BlockSpec((None, 8, nc, LANE), lambda q, i: (q, 0, 0, 0)), xtile, rowblk, rowblk,
                                          pl.BlockSpec((1, LANE), lambda q, i: (0, q)), xtile],
        out_specs=(xtile, rowblk, rowblk),
        out_shape=(jax.ShapeDtypeStruct((nc, 8 * SSM_WIDTH), BF16), jax.ShapeDtypeStruct((nq, 1024, 1024), F32),
                   jax.ShapeDtypeStruct((nq, 1024, 1024), F32)),
        compiler_params=_cparams(("parallel", "arbitrary")),
    )(*([dz8] * 8), g4, x8, m_mat, bw_mat, d8, dz8)


CUM_BLK = 256


def _split3(x):
    hi = x.astype(BF16)
    r1 = x - hi.astype(F32)
    mid = r1.astype(BF16)
    lo = (r1 - mid.astype(F32)).astype(BF16)
    return hi, mid, lo


def _tri_dot(x, tri):
    hi, mid, lo = _split3(x)
    d = _DIMS["nn"]
    return _bdot(hi, tri, d) + _bdot(mid, tri, d) + _bdot(lo, tri, d)


def _tri(n, lower):
    r = lax.broadcasted_iota(jnp.int32, (n, n), 0)
    c = lax.broadcasted_iota(jnp.int32, (n, n), 1)
    return jnp.where((r >= c) if lower else (r <= c), 1.0, 0.0).astype(BF16)


def _fox_cum(name, fproj, bcol):
    seq = fproj.shape[0]
    blk = min(CUM_BLK, seq)

    def body(f_ref, b_ref, o_ref, carry_ref):
        i = pl.program_id(0)

        @pl.when(i == 0)
        def _():
            carry_ref[...] = jnp.zeros_like(carry_ref)

        z = f_ref[...].T + b_ref[...]
        logf = jnp.minimum(z, 0.0) - jnp.log(1.0 + jnp.exp(-jnp.abs(z)))
        carry = carry_ref[...]
        cum = _tri_dot(logf, _tri(blk, lower=False)) + jnp.tile(carry, (1, blk // LANE))
        o_ref[...] = cum[0:8, :]
        carry_ref[...] = carry + jnp.sum(logf, axis=1, keepdims=True)

    return pl.pallas_call(
        body, name=name, grid=(seq // blk,),
        in_specs=[pl.BlockSpec((blk, LANE), lambda i: (i, 0)), pl.BlockSpec((LANE, 1), lambda i: (0, 0))],
        out_specs=pl.BlockSpec((8, blk), lambda i: (0, i)),
        out_shape=jax.ShapeDtypeStruct((8, seq), F32),
        scratch_shapes=[pltpu.VMEM((LANE, LANE), F32)],
        compiler_params=_cparams(("arbitrary",)),
    )(fproj, bcol)


def _fox_cum_bwd(name, dcum_t, fproj, bcol):
    seq = fproj.shape[0]
    blk = min(CUM_BLK, seq)
    n = seq // blk

    def body(dc_ref, f_ref, b_ref, df_ref, db_ref, carry_ref, acc_ref):
        i = pl.program_id(0)

        @pl.when(i == 0)
        def _():
            carry_ref[...] = jnp.zeros_like(carry_ref)
            acc_ref[...] = jnp.zeros_like(acc_ref)

        dc = jnp.concatenate([dc_ref[...], jnp.zeros((LANE - 8, blk), F32)], axis=0)
        carry = carry_ref[...]
        dlogf = _tri_dot(dc, _tri(blk, lower=True)) + jnp.tile(carry, (1, blk // LANE))
        carry_ref[...] = carry + jnp.sum(dc, axis=1, keepdims=True)
        z = f_ref[...].T + b_ref[...]
        dft = dlogf / (1.0 + jnp.exp(z))
        df_ref[...] = dft.T.astype(df_ref.dtype)
        acc_ref[...] += jnp.sum(dft, axis=1, keepdims=True)

        @pl.when(i == n - 1)
        def _():
            db_ref[...] = acc_ref[...]

    return pl.pallas_call(
        body, name=name, grid=(n,),
        in_specs=[pl.BlockSpec((8, blk), lambda i: (0, n - 1 - i)), pl.BlockSpec((blk, LANE), lambda i: (n - 1 - i, 0)),
                  pl.BlockSpec((LANE, 1), lambda i: (0, 0))],
        out_specs=(pl.BlockSpec((blk, LANE), lambda i: (n - 1 - i, 0)), pl.BlockSpec((LANE, LANE), lambda i: (0, 0))),
        out_shape=(jax.ShapeDtypeStruct((seq, LANE), BF16), jax.ShapeDtypeStruct((LANE, LANE), F32)),
        scratch_shapes=[pltpu.VMEM((LANE, LANE), F32), pltpu.VMEM((LANE, LANE), F32)],
        compiler_params=_cparams(("arbitrary",)),
    )(dcum_t, fproj, bcol)


FOX_BLK = 512
FOX_SCALE = FOX_HEAD_DIM ** -0.5


def _fox_head_mask(shape, hh):
    lane = lax.broadcasted_iota(jnp.int32, shape, 1)
    return (lane < FOX_HEAD_DIM) if hh == 0 else (lane >= FOX_HEAD_DIM)


def _fox_bias(cum_ref, hh, q0, k0, blk):
    c0 = jnp.max(cum_ref[hh:hh + 1, pl.ds(q0, LANE)], axis=1, keepdims=True)
    return c0 - cum_ref[hh:hh + 1, pl.ds(k0, blk)]


def _fox_fwd(name, qkv, cum_t):
    seq = qkv.shape[0]
    blk = min(FOX_BLK, seq)
    nb = seq // blk
    npair = FOX_HEADS // 2

    def body(q_ref, k_ref, v_ref, cum_ref, o_ref, lse_ref):
        iq = pl.program_id(1)
        q0 = pl.multiple_of(iq * blk, blk)
        qv = q_ref[...]
        row = lax.broadcasted_iota(jnp.int32, (blk, blk), 0)
        col = lax.broadcasted_iota(jnp.int32, (blk, blk), 1)
        qhs = [jnp.where(_fox_head_mask(qv.shape, hh), qv, jnp.zeros_like(qv)) * FOX_SCALE for hh in range(2)]

        def block(kb, states, masked):
            k0 = pl.multiple_of(kb * blk, blk)
            kv = k_ref[pl.ds(k0, blk), :]
            vv = v_ref[pl.ds(k0, blk), :]
            new = []
            for hh in range(2):
                m, l, acc = states[hh]
                s = _bdot(qhs[hh], kv, _DIMS["nt"]) + _fox_bias(cum_ref, hh, q0, k0, blk)
                if masked:
                    s = jnp.where(row >= col, s, -jnp.inf)
                m_new = jnp.maximum(m, jnp.max(s, axis=1, keepdims=True))
                alpha = jnp.exp(m - m_new)
                p = jnp.exp(s - m_new)
                l = alpha * l + jnp.sum(p, axis=1, keepdims=True)
                acc = alpha * acc + _bdot(p, vv, _DIMS["nn"])
                new.append((m_new, l, acc))
            return tuple(new)

        init = (jnp.full((blk, 1), -jnp.inf, F32), jnp.zeros((blk, 1), F32), jnp.zeros((blk, LANE), F32))
        states = lax.fori_loop(0, iq, lambda kb, st: block(kb, st, False), (init, init))
        states = block(iq, states, True)
        outs = []
        for hh in range(2):
            m, l, acc = states[hh]
            outs.append(acc / l)
            lse_ref[hh] = jnp.broadcast_to(m + jnp.log(l), (blk, LANE))
        o_ref[...] = jnp.where(_fox_head_mask(outs[0].shape, 0), outs[0], outs[1]).astype(o_ref.dtype)

    return pl.pallas_call(
        body, name=name, grid=(npair, nb),
        in_specs=[pl.BlockSpec((blk, LANE), lambda p, i: (i, p)),
                  pl.BlockSpec((seq, LANE), lambda p, i: (0, npair + p)),
                  pl.BlockSpec((seq, LANE), lambda p, i: (0, 2 * npair + p)),
                  pl.BlockSpec((None, 2, seq), lambda p, i: (p, 0, 0))],
        out_specs=(pl.BlockSpec((blk, LANE), lambda p, i: (i, p)),
                   pl.BlockSpec((2, blk, LANE), lambda p, i: (p, i, 0))),
        out_shape=(jax.ShapeDtypeStruct((seq, FOX_WIDTH), BF16), jax.ShapeDtypeStruct((FOX_HEADS, seq, LANE), F32)),
        compiler_params=_cparams(("parallel", "arbitrary")),
    )(qkv, qkv, qkv, cum_t)


def _fox_bwd(name, qkv, cum_t, att, datt, lse):
    seq = qkv.shape[0]
    blk = min(FOX_BLK, seq)
    nb = seq // blk
    npair = FOX_HEADS // 2

    def body(q_ref, k_ref, v_ref, cum_ref, o_ref, do_ref, lse_ref, dq_ref, dk_ref, dv_ref, dcum_ref):
        iq = pl.program_id(1)
        q0 = pl.multiple_of(iq * blk, blk)

        @pl.when(iq == 0)
        def _():
            dk_ref[...] = jnp.zeros_like(dk_ref)
            dv_ref[...] = jnp.zeros_like(dv_ref)
            dcum_ref[...] = jnp.zeros_like(dcum_ref)

        qv = q_ref[...]
        dov = do_ref[...].astype(F32)
        ov = o_ref[...].astype(F32)
        row = lax.broadcasted_iota(jnp.int32, (blk, blk), 0)
        col = lax.broadcasted_iota(jnp.int32, (blk, blk), 1)
        qhs, dohbs, deltas, lses = [], [], [], []
        for hh in range(2):
            hm = _fox_head_mask(qv.shape, hh)
            qhs.append(jnp.where(hm, qv, jnp.zeros_like(qv)) * FOX_SCALE)
            doh = jnp.where(hm, dov, 0.0)
            dohbs.append(doh.astype(BF16))
            deltas.append(jnp.sum(doh * ov, axis=1, keepdims=True))
            lses.append(jnp.tile(lse_ref[hh], (1, blk // LANE)))

        def block(kb, accs, masked):
            k0 = pl.multiple_of(kb * blk, blk)
            kv = k_ref[pl.ds(k0, blk), :]
            vv = v_ref[pl.ds(k0, blk), :]
            new = []
            dk_blk = None
            dv_blk = None
            for hh in range(2):
                dq_acc, rs_acc = accs[hh]
                s = _bdot(qhs[hh], kv, _DIMS["nt"]) + _fox_bias(cum_ref, hh, q0, k0, blk)
                p = jnp.exp(s - lses[hh])
                if masked:
                    p = jnp.where(row >= col, p, 0.0)
                dp = _bdot(dohbs[hh], vv, _DIMS["nt"])
                ds = p * (dp - deltas[hh])
                dsb = ds.astype(BF16)
                dk_h = _bdot(dsb, qhs[hh], _DIMS["tn"])
                dv_h = _bdot(p, dohbs[hh], _DIMS["tn"])
                dk_blk = dk_h if dk_blk is None else dk_blk + dk_h
                dv_blk = dv_h if dv_blk is None else dv_blk + dv_h
                dcum_ref[hh:hh + 1, pl.ds(k0, blk)] -= jnp.sum(ds, axis=0, keepdims=True)
                new.append((dq_acc + _bdot(dsb, kv, _DIMS["nn"]), rs_acc + jnp.sum(ds, axis=1, keepdims=True)))
            dk_ref[pl.ds(k0, blk), :] += dk_blk
            dv_ref[pl.ds(k0, blk), :] += dv_blk
            return tuple(new)

        init = (jnp.zeros((blk, LANE), F32), jnp.zeros((blk, 1), F32))
        accs = lax.fori_loop(0, iq, lambda kb, a: block(kb, a, False), (init, init))
        accs = block(iq, accs, True)
        for hh in range(2):
            dcum_ref[hh:hh + 1, pl.ds(q0, blk)] += jnp.broadcast_to(accs[hh][1], (blk, LANE)).T[0:1, :]
        dq = jnp.where(_fox_head_mask(qv.shape, 0), accs[0][0], accs[1][0]) * FOX_SCALE
        dq_ref[...] = dq.astype(dq_ref.dtype)

    qblk = pl.BlockSpec((blk, LANE), lambda p, i: (i, p))
    full = pl.BlockSpec((seq, LANE), lambda p, i: (0, p))
    return pl.pallas_call(
        body, name=name, grid=(npair, nb),
        in_specs=[qblk,
                  pl.BlockSpec((seq, LANE), lambda p, i: (0, npair + p)),
                  pl.BlockSpec((seq, LANE), lambda p, i: (0, 2 * npair + p)),
                  pl.BlockSpec((None, 2, seq), lambda p, i: (p, 0, 0)),
                  qblk, qblk,
                  pl.BlockSpec((2, blk, LANE), lambda p, i: (p, i, 0))],
        out_specs=(qblk, full, full, pl.BlockSpec((None, 2, seq), lambda p, i: (p, 0, 0))),
        out_shape=(jax.ShapeDtypeStruct((seq, FOX_WIDTH), BF16), jax.ShapeDtypeStruct((seq, FOX_WIDTH), F32),
                   jax.ShapeDtypeStruct((seq, FOX_WIDTH), F32), jax.ShapeDtypeStruct((npair, 2, seq), F32)),
        compiler_params=_cparams(("arbitrary", "arbitrary")),
    )(qkv, qkv, qkv, cum_t, att, datt, lse)


MEM_SCALE = MEM_HEAD_DIM ** -0.5


def _mem_probs(qh, kh):
    s = _bdot(qh, kh, _DIMS["nt"]) * MEM_SCALE
    p = jnp.exp(s - jnp.max(s, axis=1, keepdims=True))
    return p / jnp.sum(p, axis=1, keepdims=True)


def _mem_fwd(name, q2, kv, *, tr=512):
    seq = q2.shape[0]
    mlen = kv.shape[0]
    tr = min(tr, seq)

    def body(q_ref, kv_ref, o_ref):
        for h in range(MEM_HEADS):
            sl = slice(h * MEM_HEAD_DIM, (h + 1) * MEM_HEAD_DIM)
            sv = slice(MEM_WIDTH + h * MEM_HEAD_DIM, MEM_WIDTH + (h + 1) * MEM_HEAD_DIM)
            p = _mem_probs(q_ref[:, sl], kv_ref[:, sl])
            o_ref[:, sl] = _bdot(p, kv_ref[:, sv], _DIMS["nn"]).astype(o_ref.dtype)

    return pl.pallas_call(
        body, name=name, grid=(seq // tr,),
        in_specs=[pl.BlockSpec((tr, MEM_WIDTH), lambda i: (i, 0)), pl.BlockSpec((mlen, 2 * MEM_WIDTH), lambda i: (0, 0))],
        out_specs=pl.BlockSpec((tr, MEM_WIDTH), lambda i: (i, 0)),
        out_shape=jax.ShapeDtypeStruct((seq, MEM_WIDTH), BF16),
        compiler_params=_cparams(("parallel",)),
    )(q2, kv)


def _mem_bwd(name, q2, kv, do2, *, tr=512):
    seq = q2.shape[0]
    mlen = kv.shape[0]
    tr = min(tr, seq)

    def body(q_ref, kv_ref, do_ref, dq_ref, dkv_ref):
        i = pl.program_id(0)

        @pl.when(i == 0)
        def _():
            dkv_ref[...] = jnp.zeros_like(dkv_ref)

        for h in range(MEM_HEADS):
            sl = slice(h * MEM_HEAD_DIM, (h + 1) * MEM_HEAD_DIM)
            sv = slice(MEM_WIDTH + h * MEM_HEAD_DIM, MEM_WIDTH + (h + 1) * MEM_HEAD_DIM)
            qh = q_ref[:, sl]
            kh = kv_ref[:, sl]
            doh = do_ref[:, sl].astype(BF16)
            p = _mem_probs(qh, kh)
            dp = _bdot(doh, kv_ref[:, sv], _DIMS["nt"])
            ds = (p * (dp - jnp.sum(p * dp, axis=1, keepdims=True)) * MEM_SCALE).astype(BF16)
            dq_ref[:, sl] = _bdot(ds, kh, _DIMS["nn"]).astype(dq_ref.dtype)
            dkv_ref[:, sl] += _bdot(ds, qh, _DIMS["tn"])
            dkv_ref[:, sv] += _bdot(p, doh, _DIMS["tn"])

    row = pl.BlockSpec((tr, MEM_WIDTH), lambda i: (i, 0))
    kvs = pl.BlockSpec((mlen, 2 * MEM_WIDTH), lambda i: (0, 0))
    return pl.pallas_call(
        body, name=name, grid=(seq // tr,), in_specs=[row, kvs, row], out_specs=(row, kvs),
        out_shape=(jax.ShapeDtypeStruct((seq, MEM_WIDTH), BF16), jax.ShapeDtypeStruct((mlen, 2 * MEM_WIDTH), F32)),
        compiler_params=_cparams(("arbitrary",)),
    )(q2, kv, do2)


_HBM = pl.BlockSpec(memory_space=pl.ANY)
_HBM_ONLY = pl.BlockSpec(memory_space=pltpu.HBM)
_MESH = pl.DeviceIdType.MESH


def _mesh_place():
    x, y, c = lax.axis_index("x"), lax.axis_index("y"), lax.axis_index("c")
    other_chips = [(1 - x, y), (x, 1 - y), (1 - x, 1 - y)]
    return x, y, c, other_chips


def _gather_all(name, arrays):
    n = len(arrays)

    def body(*refs):
        ins, outs = refs[:n], refs[n:2 * n]
        send_sems, recv_sems, local_sems = refs[2 * n:]
        x, y, c, chips = _mesh_place()
        me, sibling = (x, y, c), (x, y, 1 - c)

        def slot(a, place):
            px, py, pc = place
            return outs[a].at[4 * px + 2 * py + pc]

        def copy(a, k, block, to, src=None):
            return pltpu.make_async_remote_copy(
                src_ref=slot(a, block) if src is None else src, dst_ref=slot(a, block),
                send_sem=send_sems.at[a, k], recv_sem=recv_sems.at[a, k], device_id=to, device_id_type=_MESH)

        mine = [pltpu.make_async_copy(ins[a], slot(a, me), local_sems.at[a]) for a in range(n)]
        for cp in mine:
            cp.start()
        first = []
        for a in range(n):
            first.append(copy(a, 0, me, sibling, src=ins[a]))
            first += [copy(a, 1 + j, me, (*chip, c), src=ins[a]) for j, chip in enumerate(chips)]
        for cp in first:
            cp.start()
        passed = []
        for j, chip in enumerate(chips):
            for a in range(n):
                copy(a, 1 + j, (*chip, c), me).wait_recv()
                fwd = copy(a, 4 + j, (*chip, c), sibling)
                fwd.start()
                passed.append(fwd)
        for a in range(n):
            copy(a, 0, sibling, me).wait_recv()
            for j, chip in enumerate(chips):
                copy(a, 4 + j, (*chip, 1 - c), me).wait_recv()
        for cp in first + passed:
            cp.wait_send()
        for cp in mine:
            cp.wait()

    out_shape = tuple(jax.ShapeDtypeStruct((N_DEV,) + arr.shape, arr.dtype) for arr in arrays)
    return pl.pallas_call(
        body, name=name, in_specs=[_HBM] * n, out_specs=tuple([_HBM] * n), out_shape=out_shape,
        scratch_shapes=[pltpu.SemaphoreType.DMA((n, N_DEV - 1)), pltpu.SemaphoreType.DMA((n, N_DEV - 1)),
                        pltpu.SemaphoreType.DMA((n,))],
    )(*arrays)


_SEM = pl.BlockSpec(memory_space=pltpu.SEMAPHORE)
_DATAFLOW = pltpu.SideEffectType.DATAFLOW_SIDE_EFFECTING


def _device_index():
    return (4 * lax.axis_index("x") + 2 * lax.axis_index("y") + lax.axis_index("c")).astype(jnp.int32).reshape(1)


def _place_own(name, pieces, *, stacked_src, after=None):
    n = len(pieces)
    n_in = n + (after is not None)

    def body(me_ref, *refs):
        for a in range(n):
            refs[n_in + a][...] = refs[a][...]

    def spec(shape):
        return pl.BlockSpec((None,) + tuple(shape), lambda i, me_ref: (me_ref[0],) + (0,) * len(shape))

    shapes = [p.shape[1:] if stacked_src else p.shape for p in pieces]
    if stacked_src:
        in_specs = [spec(s) for s in shapes]
    else:
        in_specs = [pl.BlockSpec(tuple(s), lambda i, me_ref, nd=len(s): (0,) * nd) for s in shapes]
    operands = list(pieces)
    if after is not None:
        in_specs.append(_HBM)
        operands.append(after)
    return pl.pallas_call(
        body, name=name,
        grid_spec=pltpu.PrefetchScalarGridSpec(num_scalar_prefetch=1, grid=(1,), in_specs=in_specs,
                                               out_specs=tuple(spec(s) for s in shapes)),
        out_shape=tuple(jax.ShapeDtypeStruct((N_DEV,) + tuple(s), p.dtype) for s, p in zip(shapes, pieces)),
        compiler_params=_cparams(("arbitrary",)),
    )(_device_index(), *operands)


def _peer_places():
    x, y, c = lax.axis_index("x"), lax.axis_index("y"), lax.axis_index("c")
    peers = []
    for k in range(N_DEV - 1):
        flip = k + 1
        px = 1 - x if flip & 4 else x
        py = 1 - y if flip & 2 else y
        pc = 1 - c if flip & 1 else c
        peers.append((px, py, pc, 4 * px + 2 * py + pc))
    return 4 * x + 2 * y + c, peers


def _direct_copy(srcs, lands, send_sems, recv_sems, a, k, me, peer, scatter):
    px, py, pc, pidx = peer
    return pltpu.make_async_remote_copy(
        src_ref=srcs[a].at[pidx] if scatter else srcs[a], dst_ref=lands[a].at[me],
        send_sem=send_sems.at[a * (N_DEV - 1) + k], recv_sem=recv_sems.at[a * (N_DEV - 1) + k],
        device_id=(px, py, pc), device_id_type=_MESH)


def _send_start(name, srcs, lands, *, scatter):
    n = len(srcs)

    def body(*refs):
        src_refs, land_refs = refs[:n], refs[n:2 * n]
        send_sems, recv_sems = refs[2 * n], refs[2 * n + 1]
        token = refs[-1]
        me, peers = _peer_places()
        for k, peer in enumerate(peers):
            for a in range(n):
                _direct_copy(src_refs, land_refs, send_sems, recv_sems, a, k, me, peer, scatter).start()
        token[...] = jnp.zeros_like(token)

    hbm_shapes = [pltpu.HBM(t.shape, t.dtype) for t in list(srcs) + list(lands)]
    outs = pl.pallas_call(
        body, name=name,
        out_shape=(pltpu.SemaphoreType.DMA((n * (N_DEV - 1),)), pltpu.SemaphoreType.DMA((n * (N_DEV - 1),)), *hbm_shapes,
                   jax.ShapeDtypeStruct((8, LANE), F32)),
        in_specs=[_HBM_ONLY] * (2 * n),
        out_specs=(_SEM, _SEM, *([_HBM_ONLY] * (2 * n)), pl.BlockSpec(memory_space=pltpu.VMEM)),
        input_output_aliases={i: 2 + i for i in range(2 * n)},
        compiler_params=pltpu.CompilerParams(has_side_effects=_DATAFLOW),
    )(*[pltpu.with_memory_space_constraint(t, pltpu.HBM) for t in list(srcs) + list(lands)])
    return outs[0], outs[1], outs[2:2 + n], outs[2 + n:2 + 2 * n], outs[-1]


def _send_wait(name, send_sems, recv_sems, srcs, lands, after, *, scatter):
    n = len(srcs)

    def body(*refs):
        src_refs, land_refs = refs[:n], refs[n:2 * n]
        send_sems, recv_sems = refs[2 * n], refs[2 * n + 1]
        me, peers = _peer_places()
        for k, peer in enumerate(peers):
            for a in range(n):
                cp = _direct_copy(src_refs, land_refs, send_sems, recv_sems, a, k, me, peer, scatter)
                cp.wait_send()
                cp.wait_recv()

    hbm_shapes = [pltpu.HBM(t.shape, t.dtype) for t in list(srcs) + list(lands)]
    outs = pl.pallas_call(
        body, name=name, out_shape=tuple(hbm_shapes),
        in_specs=[_HBM_ONLY] * (2 * n) + [_SEM, _SEM, _HBM],
        out_specs=tuple([_HBM_ONLY] * (2 * n)),
        input_output_aliases={i: i for i in range(2 * n)},
        compiler_params=pltpu.CompilerParams(has_side_effects=_DATAFLOW),
    )(*srcs, *lands, send_sems, recv_sems, after)
    return outs[n:]


def _unstack_cols(name, stacked):
    n, rows, cols = stacked.shape

    def body(i_ref, o_ref):
        o_ref[...] = i_ref[...]

    return pl.pallas_call(
        body, name=name, grid=(n,), in_specs=[pl.BlockSpec((None, rows, cols), lambda k: (k, 0, 0))],
        out_specs=pl.BlockSpec((rows, cols), lambda k: (0, k)),
        out_shape=jax.ShapeDtypeStruct((rows, n * cols), stacked.dtype),
        compiler_params=_cparams(("parallel",)),
    )(stacked)


def _restack_cols(name, mat):
    rows, width = mat.shape
    cols = width // N_DEV

    def body(i_ref, o_ref):
        o_ref[...] = i_ref[...]

    return pl.pallas_call(
        body, name=name, grid=(N_DEV,), in_specs=[pl.BlockSpec((rows, cols), lambda k: (0, k))],
        out_specs=pl.BlockSpec((None, rows, cols), lambda k: (k, 0, 0)),
        out_shape=jax.ShapeDtypeStruct((N_DEV, rows, cols), mat.dtype),
        compiler_params=_cparams(("parallel",)),
    )(mat)


def _remap_pieces(runs):
    plan = {}
    for du, dc, su, sc, ln in runs:
        while ln > 0:
            lane = dc % LANE
            take = min(ln, LANE - lane)
            plan.setdefault((du, dc // LANE), []).append((su, sc, take, lane))
            dc, sc, ln = dc + take, sc + take, ln - take
    return plan


def _remap(name, srcs, src_units, runs, *, out_units, out_cols, out_dtype, tr=256):
    rows = srcs[0].shape[-2]
    tr = min(tr, rows)
    plan = _remap_pieces(runs)
    n_src = len(srcs)
    stacked_out = out_units is not None
    n_tiles = out_cols // LANE

    def body(*refs):
        o_ref = refs[n_src]

        def src_tile(unit, t):
            ai, lead = src_units[unit]
            ref = refs[ai]
            sl = slice(t * LANE, (t + 1) * LANE)
            return (ref[:, sl] if lead is None else ref[lead, :, sl]).astype(F32)

        lane = lax.broadcasted_iota(jnp.int32, (tr, LANE), 1)
        for du in range(out_units if stacked_out else 1):
            for t in range(n_tiles):
                acc = jnp.zeros((tr, LANE), F32)
                for su, sc, ln, dl in plan.get((du if stacked_out else None, t), []):
                    st, so = sc // LANE, sc % LANE
                    first = src_tile(su, st)
                    if so == dl and so + ln <= LANE:
                        piece = first
                    else:
                        second = src_tile(su, st + 1) if so + ln > LANE else first
                        both = jnp.concatenate([first, second], axis=1)
                        piece = pltpu.roll(both, (dl - so) % (2 * LANE), axis=1)[:, 0:LANE]
                    acc = piece if (dl == 0 and ln == LANE) else jnp.where(
                        jnp.logical_and(lane >= dl, lane < dl + ln), piece, acc)
                if stacked_out:
                    o_ref[du, :, t * LANE:(t + 1) * LANE] = acc.astype(o_ref.dtype)
                else:
                    o_ref[:, t * LANE:(t + 1) * LANE] = acc.astype(o_ref.dtype)

    in_specs = []
    for arr in srcs:
        if arr.ndim == 2:
            in_specs.append(pl.BlockSpec((tr, arr.shape[1]), lambda i: (i, 0)))
        else:
            in_specs.append(pl.BlockSpec((arr.shape[0], tr, arr.shape[2]), lambda i: (0, i, 0)))
    if stacked_out:
        out_spec = pl.BlockSpec((out_units, tr, out_cols), lambda i: (0, i, 0))
        out_shape = jax.ShapeDtypeStruct((out_units, rows, out_cols), out_dtype)
    else:
        out_spec = pl.BlockSpec((tr, out_cols), lambda i: (i, 0))
        out_shape = jax.ShapeDtypeStruct((rows, out_cols), out_dtype)
    return pl.pallas_call(
        body, name=name, grid=(rows // tr,), in_specs=in_specs, out_specs=out_spec, out_shape=out_shape,
        compiler_params=_cparams(("parallel",)),
    )(*srcs)


def _proj_col(c):
    if c < PROJ_GATE0:
        return c
    if c < PROJ_GATE0 + FOX_HEADS:
        return PROJ_F0 + (c - PROJ_GATE0)
    return c - FOX_HEADS


def _win_runs():
    cuts = sorted(set([0, PROJ_GATE0, PROJ_GATE0 + FOX_HEADS, IN_WIDTH] + [SHARD_IN * k for k in range(N_DEV + 1)]))
    return [(lo // SHARD_IN, lo % SHARD_IN, _proj_col(lo), hi - lo) for lo, hi in zip(cuts[:-1], cuts[1:])]


def _assemble_win(name, stacked):
    runs = [(None, pc, k, sc, ln) for k, sc, pc, ln in _win_runs()]
    return _remap(name, [stacked], [(0, k) for k in range(N_DEV)], runs,
                  out_units=None, out_cols=PROJ_WIDTH, out_dtype=BF16)


def _disassemble_dwin(name, dw):
    runs = [(k, sc, 0, pc, ln) for k, sc, pc, ln in _win_runs()]
    return _remap(name, [dw], [(0, None)], runs, out_units=N_DEV, out_cols=SHARD_IN_PAD, out_dtype=BF16)


def _concat_cols(name, parts, *, tr=512):
    rows = parts[0].shape[0]
    tr = min(tr, rows)
    widths = [p.shape[1] for p in parts]
    total = sum(widths)

    def body(*refs):
        o_ref = refs[len(parts)]
        lo = 0
        for r, w in zip(refs[:len(parts)], widths):
            o_ref[:, lo:lo + w] = r[...].astype(o_ref.dtype)
            lo += w

    return pl.pallas_call(
        body, name=name, grid=(rows // tr,),
        in_specs=[pl.BlockSpec((tr, w), lambda i: (i, 0)) for w in widths],
        out_specs=pl.BlockSpec((tr, total), lambda i: (i, 0)),
        out_shape=jax.ShapeDtypeStruct((rows, total), BF16),
        compiler_params=_cparams(("parallel",)),
    )(*parts)


FFN_BLK = FFN_HIDDEN // 2


def _ffn_col(c):
    half, r = divmod(c, FFN_HIDDEN)
    blk, r = divmod(r, FFN_BLK)
    return blk * 2 * FFN_BLK + half * FFN_BLK + r


def _assemble_wffn(name, stacked):
    runs = [(None, _ffn_col(SHARD_FFN * k), k, 0, SHARD_FFN) for k in range(N_DEV)]
    return _remap(name, [stacked], [(0, k) for k in range(N_DEV)], runs,
                  out_units=None, out_cols=2 * FFN_HIDDEN, out_dtype=BF16)


def _disassemble_dwffn(name, dw):
    runs = [(k, 0, 0, _ffn_col(SHARD_FFN * k), SHARD_FFN) for k in range(N_DEV)]
    return _remap(name, [dw], [(0, None)], runs, out_units=N_DEV, out_cols=SHARD_FFN_PAD, out_dtype=BF16)


def _ffn_in_swiglu(name, xn, w, *, tm=512):
    rows, k = xn.shape
    tm = min(tm, rows)
    nblk = FFN_HIDDEN // FFN_BLK

    def body(x_ref, w_ref, f_ref, g_ref):
        f = _bdot(x_ref[...], w_ref[...], _DIMS["nn"])
        f_ref[...] = f.astype(f_ref.dtype)
        fa = f[:, 0:FFN_BLK]
        g_ref[...] = (fa * _sigmoid(fa) * f[:, FFN_BLK:2 * FFN_BLK]).astype(g_ref.dtype)

    return pl.pallas_call(
        body, name=name, grid=(nblk, rows // tm),
        in_specs=[pl.BlockSpec((tm, k), lambda j, i: (i, 0)), pl.BlockSpec((k, 2 * FFN_BLK), lambda j, i: (0, j))],
        out_specs=(pl.BlockSpec((tm, 2 * FFN_BLK), lambda j, i: (i, j)), pl.BlockSpec((tm, FFN_BLK), lambda j, i: (i, j))),
        out_shape=(jax.ShapeDtypeStruct((rows, 2 * FFN_HIDDEN), BF16), jax.ShapeDtypeStruct((rows, FFN_HIDDEN), BF16)),
        compiler_params=_cparams(("parallel", "arbitrary")),
    )(xn, w)


def _d_ffn_out_swiglu(name, dh, w_out, f, *, tm=512):
    rows, d = dh.shape
    tm = min(tm, rows)
    nblk = FFN_HIDDEN // FFN_BLK

    def body(dh_ref, w_ref, f_ref, df_ref):
        dg = _bdot(dh_ref[...], w_ref[...], _DIMS["nt"])
        fa = f_ref[:, 0:FFN_BLK].astype(F32)
        fb = f_ref[:, FFN_BLK:2 * FFN_BLK].astype(F32)
        s = _sigmoid(fa)
        df_ref[:, 0:FFN_BLK] = (dg * fb * s * (1.0 + fa * (1.0 - s))).astype(df_ref.dtype)
        df_ref[:, FFN_BLK:2 * FFN_BLK] = (dg * fa * s).astype(df_ref.dtype)

    wide = pl.BlockSpec((tm, 2 * FFN_BLK), lambda j, i: (i, j))
    return pl.pallas_call(
        body, name=name, grid=(nblk, rows // tm),
        in_specs=[pl.BlockSpec((tm, d), lambda j, i: (i, 0)), pl.BlockSpec((FFN_BLK, d), lambda j, i: (j, 0)), wide],
        out_specs=wide, out_shape=jax.ShapeDtypeStruct((rows, 2 * FFN_HIDDEN), BF16),
        compiler_params=_cparams(("parallel", "arbitrary")),
    )(dh, w_out, f)


def _adamw(name, parts, w, m, v, *, tr=128):
    rows, cols = w.shape
    n_parts = parts.shape[0]
    tr = min(tr, rows)
    assert rows % tr == 0, (name, rows, tr)
    c1 = 1.0 - ADAM_B1 ** ADAM_STEP
    c2 = 1.0 - ADAM_B2 ** ADAM_STEP

    def body(p_ref, w_ref, m_ref, v_ref, g_ref, d_ref, nm_ref, nv_ref):
        g = p_ref[0].astype(F32)
        for s in range(1, n_parts):
            g = g + p_ref[s].astype(F32)
        m_new = ADAM_B1 * m_ref[...] + (1.0 - ADAM_B1) * g
        v_new = ADAM_B2 * v_ref[...] + (1.0 - ADAM_B2) * (g * g)
        upd = (m_new / c1) / (jnp.sqrt(v_new / c2) + ADAM_EPS) + ADAM_WD * w_ref[...]
        g_ref[...] = g
        d_ref[...] = -ADAM_LR * upd
        nm_ref[...] = m_new
        nv_ref[...] = v_new

    row = pl.BlockSpec((tr, cols), lambda i: (i, 0))
    out = jax.ShapeDtypeStruct((rows, cols), F32)
    return pl.pallas_call(
        body, name=name, grid=(rows // tr,),
        in_specs=[pl.BlockSpec((n_parts, tr, cols), lambda i: (0, i, 0)), row, row, row],
        out_specs=(row, row, row, row), out_shape=(out, out, out, out),
        compiler_params=_cparams(("parallel",)),
    )(parts, w, m, v)


_WEIGHTS = ("norm_mix", "w_in", "b_forget", "lam_re", "lam_im", "log_dt", "b_re", "b_im", "c_re", "c_im",
            "d_skip", "w_glu", "w_fox_o", "w_mix_out", "norm_mem_q", "norm_mem_kv", "w_mem_q", "w_mem_kv",
            "w_mem_o", "norm_ffn", "w_ffn_in", "w_ffn_out", "norm_final")
_SHARDED = ("w_in", "w_glu", "w_fox_o", "w_mix_out", "w_mem_q", "w_mem_kv", "w_mem_o", "w_ffn_in", "w_ffn_out")
_SMALL = tuple(n for n in _WEIGHTS if n not in _SHARDED)
_PACK_COLS = 1024


def _pack(arrays):
    flat = jnp.concatenate([a.reshape(-1).astype(F32) for a in arrays])
    rows = -(-flat.shape[0] // _PACK_COLS)
    return jnp.pad(flat, (0, rows * _PACK_COLS - flat.shape[0])).reshape(rows, _PACK_COLS)


def _unpack(buf, like):
    flat = buf.reshape(-1)
    out, pos = [], 0
    for a in like:
        out.append(flat[pos:pos + a.size].reshape(a.shape))
        pos += a.size
    return out


def _mm(name, a, b, mode, m, n, k, out_dtype, tm=1024, tn=512, tk=1024, **kw):
    return _matmul(name, a, b, mode, m, n, k, out_dtype=out_dtype, tm=tm, tn=tn, tk=tk, **kw)


def kernel(x, mem, norm_mix, w_in, b_forget, lam_re, lam_im, log_dt, b_re, b_im, c_re, c_im, d_skip, w_glu, w_fox_o, w_mix_out, norm_mem_q, norm_mem_kv, w_mem_q, w_mem_kv, w_mem_o, norm_ffn, w_ffn_in, w_ffn_out, norm_final, loss_target, m_norm_mix, m_w_in, m_b_forget, m_lam_re, m_lam_im, m_log_dt, m_b_re, m_b_im, m_c_re, m_c_im, m_d_skip, m_w_glu, m_w_fox_o, m_w_mix_out, m_norm_mem_q, m_norm_mem_kv, m_w_mem_q, m_w_mem_kv, m_w_mem_o, m_norm_ffn, m_w_ffn_in, m_w_ffn_out, m_norm_final, v_norm_mix, v_w_in, v_b_forget, v_lam_re, v_lam_im, v_log_dt, v_b_re, v_b_im, v_c_re, v_c_im, v_d_skip, v_w_glu, v_w_fox_o, v_w_mix_out, v_norm_mem_q, v_norm_mem_kv, v_w_mem_q, v_w_mem_kv, v_w_mem_o, v_norm_ffn, v_w_ffn_in, v_w_ffn_out, v_norm_final):
    given = dict(locals())
    weights = {n: given[n] for n in _WEIGHTS}
    mom_m = {n: given["m_" + n] for n in _WEIGHTS}
    mom_v = {n: given["v_" + n] for n in _WEIGHTS}
    seq = x.shape[1]
    nc = seq // SSM_CHUNK
    d = D_MODEL
    xs, mems, tgt = x[0], mem[0], loss_target[0]

    def padcols(a, width):
        return jnp.pad(a, ((0, 0), (0, width - a.shape[1])))

    shards = [padcols(w_in[0].astype(BF16), SHARD_IN_PAD), w_glu[0].astype(BF16), w_fox_o[0].astype(BF16),
              w_mix_out[0].astype(BF16), w_mem_q[0].astype(BF16), w_mem_kv[0].astype(BF16),
              w_mem_o[0].astype(BF16), padcols(w_ffn_in[0].astype(BF16), SHARD_FFN_PAD), w_ffn_out[0].astype(BF16)]
    win = _assemble_win("assemble_w_in", _gather_all("gather_w_in", shards[:1])[0])
    rest = shards[1:]
    gsend, grecv, rest_thru, lands, gtoken = _send_start(
        "gather_rest_start", rest, _place_own("place_weight_shards", rest, stacked_src=False, after=win), scatter=False)

    u = _rms_fwd("rms_mix", xs, norm_mix, after=gtoken)
    ussm = _mm("proj_ssm", u, win, "nn", seq, SSM_WIDTH, d, F32)
    qkv = _mm("proj_qkv", u, win, "nn", seq, 3 * FOX_WIDTH, d, BF16, tn=512, b_off=(0, SSM_WIDTH))
    gates = _mm("proj_gates", u, win, "nn", seq, 2 * d, d, BF16, tn=1024, b_off=(0, PROJ_GATE0))
    fproj = _mm("proj_forget", u, win, "nn", seq, LANE, d, F32, tn=LANE, b_off=(0, PROJ_F0))

    ssm_params = (lam_re[0], lam_im[0], log_dt[0], b_re[0], b_im[0], c_re[0], c_im[0])
    (m_c, bw_c, cm_c, a8, aseg), mats_vjp = jax.vjp(lambda *p: _ssm_mats(*p, nc), *ssm_params)
    m_b = _bd_expand("ssm_expand_m", _BD_M, m_c)
    bw_b = _bd_expand("ssm_expand_bw", _BD_BW, bw_c)
    cm_b = _bd_expand("ssm_expand_cm", _BD_CM, cm_c)
    u8 = ussm.reshape(nc, SSM_CHUNK * SSM_WIDTH)
    d8 = jnp.tile(d_skip, (1, SSM_CHUNK))
    w4 = _ssm_w("ssm_w", u8, bw_b)
    sp4 = _ssm_scan("ssm_scan", w4, a8, aseg, reverse=False)
    y8 = _ssm_y("ssm_y", u8, sp4, m_b, cm_b)
    act = _ssm_post_fwd("ssm_act", y8, u8, d8).reshape(seq, SSM_WIDTH)

    bcol = jnp.pad(b_forget[0], (0, LANE - FOX_HEADS)).reshape(LANE, 1)
    cum_t = _fox_cum("fox_cum", fproj, bcol).reshape(FOX_HEADS // 2, 2, seq)
    att, lse = _fox_fwd("fox_fwd", qkv, cum_t)

    gathered = _send_wait("gather_rest_wait", gsend, grecv, rest_thru, lands, att, scatter=False)
    wglu = _unstack_cols("unstack_w_glu", gathered[0])
    wfoxo = _unstack_cols("unstack_w_fox_o", gathered[1])
    wmix = gathered[2].reshape(d, d)
    wmq = gathered[3].reshape(d, MEM_WIDTH)
    wmkv = gathered[4].reshape(d, 2 * MEM_WIDTH)
    wmo = _unstack_cols("unstack_w_mem_o", gathered[5])
    wffn_in = _assemble_wffn("assemble_w_ffn_in", gathered[6])
    wffn_out = gathered[7].reshape(FFN_HIDDEN, d)

    glu = _mm("glu", act, wglu, "nn", seq, 2 * d, SSM_WIDTH, BF16, tn=1024)
    out_b = _mm("fox_out", att, wfoxo, "nn", seq, d, FOX_WIDTH, BF16, tn=1024)

    mixin = _mix_fwd("mix", glu, gates, out_b)
    h1 = _mm("mix_out", mixin, wmix, "nn", seq, d, d, F32, tn=1024, add=xs)

    n1 = _rms_fwd("rms_mem_q", h1, norm_mem_q)
    q2 = _mm("mem_q", n1, wmq, "nn", seq, MEM_WIDTH, d, BF16)
    mn = _rms_fwd("rms_mem_kv", mems, norm_mem_kv)
    mlen = mems.shape[0]
    kv = _mm("mem_kv", mn, wmkv, "nn", mlen, 2 * MEM_WIDTH, d, BF16)
    o2 = _mem_fwd("mem_attn", q2, kv)
    h2 = _mm("mem_out", o2, wmo, "nn", seq, d, MEM_WIDTH, F32, tn=1024, add=h1)

    n2 = _rms_fwd("rms_ffn", h2, norm_ffn)
    f, g_act = _ffn_in_swiglu("ffn_in_swiglu", n2, wffn_in)
    h3 = _mm("ffn_out", g_act, wffn_out, "nn", seq, d, FFN_HIDDEN, F32, tk=FFN_HIDDEN, add=h2)
    loss_part, dh3, dg_final = _final_loss("final_loss", h3, tgt, norm_final.reshape(1, d))

    df = _d_ffn_out_swiglu("d_ffn_out_swiglu", dh3, wffn_out, f)
    dwffn_out = _mm("d_ffn_out_w", g_act, dh3, "tn", FFN_HIDDEN, d, seq, BF16, tm=1408, tn=1024)
    dn2 = _mm("d_ffn_in_x", df, wffn_in, "nt", seq, d, 2 * FFN_HIDDEN, F32, tn=1024, tk=FFN_HIDDEN)
    dwffn_in = _mm("d_ffn_in_w", n2, df, "tn", d, 2 * FFN_HIDDEN, seq, BF16, tn=1408)
    dh2, dg_ffn = _rms_bwd("d_rms_ffn", dn2, h2, norm_ffn, res=dh3)

    do2 = _mm("d_mem_out_x", dh2, wmo, "nt", seq, MEM_WIDTH, d, F32)
    dwmo = _restack_cols("restack_d_w_mem_o", _mm("d_mem_out_w", o2, dh2, "tn", MEM_WIDTH, d, seq, BF16, tn=1024))
    dq2, dkv = _mem_bwd("d_mem_attn", q2, kv, do2)
    dwmq = _mm("d_mem_q_w", n1, dq2, "tn", d, MEM_WIDTH, seq, BF16)
    dn1 = _mm("d_mem_q_x", dq2, wmq, "nt", seq, d, MEM_WIDTH, F32)
    dwmkv = _mm("d_mem_kv_w", mn, dkv, "tn", d, 2 * MEM_WIDTH, mlen, BF16, tn=1024)
    dmn = _mm("d_mem_kv_x", dkv, wmkv, "nt", mlen, d, 2 * MEM_WIDTH, F32)
    _, dg_memkv = _rms_bwd("d_rms_mem_kv", dmn, mems, norm_mem_kv)

    early = [dwmq.reshape(N_DEV, d // N_DEV, MEM_WIDTH), dwmkv.reshape(N_DEV, d // N_DEV, 2 * MEM_WIDTH), dwmo,
             _disassemble_dwffn("split_d_w_ffn_in", dwffn_in), dwffn_out.reshape(N_DEV, FFN_HIDDEN // N_DEV, d)]
    ssend, srecv, early_thru, early_lands, stoken = _send_start(
        "scatter_early_start", early, _place_own("place_early_grads", early, stacked_src=True), scatter=True)
    dh1, dg_memq = _rms_bwd("d_rms_mem_q", dn1, h1, norm_mem_q, res=dh2, after=stoken)

    dmixin = _mm("d_mix_out_x", dh1, wmix, "nt", seq, d, d, F32, tn=1024)
    dwmix = _mm("d_mix_out_w", mixin, dh1, "tn", d, d, seq, BF16, tn=1024)
    dglu, dgates, dout_b = _mix_bwd("d_mix", dmixin, glu, gates, out_b)
    datt = _mm("d_fox_out_x", dout_b, wfoxo, "nt", seq, FOX_WIDTH, d, F32)
    dwfoxo = _restack_cols("restack_d_w_fox_o", _mm("d_fox_out_w", att, dout_b, "tn", FOX_WIDTH, d, seq, BF16, tn=1024))
    dact = _mm("d_glu_x", dglu, wglu, "nt", seq, SSM_WIDTH, 2 * d, F32, tk=2 * d)
    dwglu = _restack_cols("restack_d_w_glu", _mm("d_glu_w", act, dglu, "tn", SSM_WIDTH, 2 * d, seq, BF16, tn=2 * d))

    mid = [dwglu, dwfoxo, dwmix.reshape(N_DEV, d // N_DEV, d)]
    msend, mrecv, mid_thru, mid_lands, mtoken = _send_start(
        "scatter_mid_start", mid, _place_own("place_mid_grads", mid, stacked_src=True), scatter=True)

    dz8, dg_dskip = _ssm_post_bwd("d_ssm_act", dact.reshape(nc, SSM_CHUNK * SSM_WIDTH), y8, u8, d8, after=mtoken)
    ds4, dcm = _ssm_ds("d_ssm_y_state", dz8, sp4, cm_b)
    g4, da8 = _ssm_scan("d_ssm_scan", ds4, a8, aseg, reverse=True, sprev4=sp4)
    dx8, dm, dbw = _ssm_dx("d_ssm_x", dz8, g4, u8, m_b, bw_b, d8)
    dussm = dx8.reshape(seq, SSM_WIDTH)
    g_ssm = mats_vjp((_bd_reduce("ssm_reduce_dm", _BD_M, dm), _bd_reduce("ssm_reduce_dbw", _BD_BW, dbw),
                      _bd_reduce("ssm_reduce_dcm", _BD_CM, dcm), da8, jnp.zeros_like(aseg)))

    dq, dk, dv, dcum = _fox_bwd("d_fox", qkv, cum_t, att, datt, lse)
    dfproj, dbf = _fox_cum_bwd("d_fox_cum", dcum.reshape(FOX_HEADS, seq), fproj, bcol)
    dg_bforget = dbf[0:FOX_HEADS, 0].reshape(1, FOX_HEADS)

    dproj = _concat_cols("d_proj_concat", (dussm, dq, dk, dv, dgates, dfproj))
    dwin = _mm("d_proj_w", u, dproj, "tn", d, PROJ_WIDTH, seq, BF16, tn=1408)
    late = [_disassemble_dwin("split_d_w_in", dwin)]
    lsend, lrecv, late_thru, late_lands, ltoken = _send_start(
        "scatter_late_start", late, _place_own("place_late_grads", late, stacked_src=True), scatter=True)
    du = _mm("d_proj_x", dproj, win, "nt", seq, d, PROJ_WIDTH, F32, tn=1024, tk=1408)
    dx, dg_mix = _rms_bwd("d_rms_mix", du, xs, norm_mix, res=dh1, after=ltoken)

    early_parts = _send_wait("scatter_early_wait", ssend, srecv, early_thru, early_lands, dx, scatter=True)
    mid_parts = _send_wait("scatter_mid_wait", msend, mrecv, mid_thru, mid_lands, dx, scatter=True)
    received = dict(zip(("w_glu", "w_fox_o", "w_mix_out"), mid_parts))
    received.update(zip(("w_mem_q", "w_mem_kv", "w_mem_o", "w_ffn_in", "w_ffn_out"), early_parts))

    small_grads = dict(zip(
        _SMALL, (dg_mix, dg_bforget, g_ssm[0][None], g_ssm[1][None], g_ssm[2][None], g_ssm[3][None], g_ssm[4][None],
                 g_ssm[5][None], g_ssm[6][None], dg_dskip, dg_memq, dg_memkv, dg_ffn, dg_final.reshape(d))))
    small_like = [weights[n] for n in _SMALL]
    small_all = _gather_all("gather_small_grads", [_pack([small_grads[n] for n in _SMALL])])[0]
    pk = [_pack([src[n] for n in _SMALL]) for src in (weights, mom_m, mom_v)]
    small_out = _adamw("adamw_small", small_all, pk[0], pk[1], pk[2], tr=small_all.shape[1])
    small_res = [dict(zip(_SMALL, _unpack(buf, small_like))) for buf in small_out]

    results = [dict(r) for r in small_res]
    tiles = {"w_in": 128, "w_glu": 128, "w_fox_o": 128, "w_mix_out": 128, "w_mem_q": 128, "w_mem_kv": 128,
             "w_mem_o": 128, "w_ffn_in": 128, "w_ffn_out": 176}
    pads = {"w_in": SHARD_IN_PAD, "w_ffn_in": SHARD_FFN_PAD}
    outs = small_out
    for name in _SHARDED[1:] + _SHARDED[:1]:
        if name == "w_in":
            received[name] = _send_wait("scatter_late_wait", lsend, lrecv, late_thru, late_lands, outs[0],
                                        scatter=True)[0]
        parts = received[name]
        w2, m2, v2 = weights[name][0], mom_m[name][0], mom_v[name][0]
        cols = w2.shape[1]
        if name in pads:
            w2, m2, v2 = (padcols(t, pads[name]) for t in (w2, m2, v2))
        outs = _adamw("adamw_" + name, parts, w2, m2, v2, tr=tiles[name])
        for res, o in zip(results, outs):
            res[name] = o[:, :cols][None]

    loss = lax.psum(loss_part[0, 0], ("x", "y", "c"))
    out = [loss, dx[None]]
    for res in results:
        out.extend(res[n] for n in _WEIGHTS)
    return tuple(out)
```

```python
import math

import jax
import jax.numpy as jnp
import numpy as np
from jax import lax
from jax.experimental import pallas as pl
from jax.experimental.pallas import tpu as pltpu

F32 = jnp.float32
BF16 = jnp.bfloat16

N_DEV = 8
LANE = 128
VMEM_LIMIT = 56 * 1024 * 1024

D_MODEL = 1024
SSM_GROUP = 16
SSM_GROUPS = 32
SSM_WIDTH = 512
SSM_STATE = 64
SSM_CHUNK = 8
FOX_HEADS = 8
FOX_HEAD_DIM = 64
FOX_WIDTH = 512
MEM_HEADS = 4
MEM_HEAD_DIM = 128
MEM_WIDTH = 512
FFN_HIDDEN = 2816
RMS_EPS = 1e-6
IN_WIDTH = 4104
SHARD_IN = IN_WIDTH // N_DEV
SHARD_IN_PAD = 640
SHARD_FFN = 2 * FFN_HIDDEN // N_DEV
SHARD_FFN_PAD = 768
PROJ_GATE0 = 2048
PROJ_F0 = 4096
PROJ_WIDTH = 4224

ADAM_LR = 0.001
ADAM_B1 = 0.9
ADAM_B2 = 0.999
ADAM_EPS = 1e-08
ADAM_WD = 0.01
ADAM_STEP = 10


def _cparams(sem=None):
    return pltpu.CompilerParams(dimension_semantics=sem, vmem_limit_bytes=VMEM_LIMIT)


def _sigmoid(x):
    return 1.0 / (1.0 + jnp.exp(-x))


def _bdot(a, b, dims):
    return lax.dot_general(a.astype(BF16), b.astype(BF16), ((dims[0], dims[1]), ((), ())),
                           preferred_element_type=F32)


_DIMS = {"nn": ((1,), (0,)), "nt": ((1,), (1,)), "tn": ((0,), (0,))}


def _matmul(name, a, b, mode, m, n, k, *, out_dtype, tm, tn, tk, a_off=(0, 0), b_off=(0, 0), add=None):
    tm, tn, tk = min(tm, m), min(tn, n), min(tk, k)
    assert m % tm == 0 and n % tn == 0 and k % tk == 0, (name, m, n, k, tm, tn, tk)
    nk = k // tk
    grid = (m // tm, n // tn, nk)

    def blk(off, t):
        assert off % t == 0, (name, off, t)
        return off // t

    if mode in ("nn", "nt"):
        ar, ac = blk(a_off[0], tm), blk(a_off[1], tk)
        a_spec = pl.BlockSpec((tm, tk), lambda i, j, kk: (i + ar, kk + ac))
    else:
        ar, ac = blk(a_off[0], tk), blk(a_off[1], tm)
        a_spec = pl.BlockSpec((tk, tm), lambda i, j, kk: (kk + ar, i + ac))

    if mode in ("nn", "tn"):
        br, bc = blk(b_off[0], tk), blk(b_off[1], tn)
        b_spec = pl.BlockSpec((tk, tn), lambda i, j, kk: (kk + br, j + bc))
    else:
        br, bc = blk(b_off[0], tn), blk(b_off[1], tk)
        b_spec = pl.BlockSpec((tn, tk), lambda i, j, kk: (j + br, kk + bc))
    o_spec = pl.BlockSpec((tm, tn), lambda i, j, kk: (i, j))
    out_shape = jax.ShapeDtypeStruct((m, n), out_dtype)

    in_specs = [a_spec, b_spec]
    operands = [a, b]
    if add is not None:
        in_specs.append(pl.BlockSpec((tm, tn), lambda i, j, kk: (i, j)))
        operands.append(add)
    dims = _DIMS[mode]
    has_add = add is not None

    def body(*refs):
        a_ref, b_ref = refs[0], refs[1]
        add_ref = refs[2] if has_add else None
        o_ref = refs[3] if has_add else refs[2]
        acc_ref = refs[-1] if nk > 1 else None
        prod = _bdot(a_ref[...], b_ref[...], dims)

        def finish(total):
            if has_add:
                total = total + add_ref[...].astype(F32)
            o_ref[...] = total.astype(o_ref.dtype)

        if nk == 1:
            finish(prod)
        else:
            kk = pl.program_id(2)

            @pl.when(kk == 0)
            def _():
                acc_ref[...] = prod

            @pl.when(jnp.logical_and(kk > 0, kk < nk - 1))
            def _():
                acc_ref[...] += prod

            @pl.when(kk == nk - 1)
            def _():
                finish(acc_ref[...] + prod)

    scratch = [pltpu.VMEM((tm, tn), F32)] if nk > 1 else []
    return pl.pallas_call(
        body, name=name, grid=grid, in_specs=in_specs, out_specs=o_spec, out_shape=out_shape,
        scratch_shapes=scratch,
        compiler_params=_cparams(("parallel", "parallel", "arbitrary")),
    )(*operands)


def _rms_fwd(name, x, gain, *, tr=512, after=None):
    r, d = x.shape
    tr = min(tr, r)

    def body(x_ref, g_ref, *rest):
        o_ref = rest[-1]
        xv = x_ref[...]
        rstd = lax.rsqrt(jnp.mean(xv * xv, axis=-1, keepdims=True) + RMS_EPS)
        o_ref[...] = (xv * rstd * g_ref[...]).astype(o_ref.dtype)

    in_specs = [pl.BlockSpec((tr, d), lambda i: (i, 0)), pl.BlockSpec((1, d), lambda i: (0, 0))]
    ops = [x, gain]
    if after is not None:
        in_specs.append(pl.BlockSpec(after.shape, lambda i: (0, 0)))
        ops.append(after)
    return pl.pallas_call(
        body, name=name, grid=(r // tr,), in_specs=in_specs,
        out_specs=pl.BlockSpec((tr, d), lambda i: (i, 0)),
        out_shape=jax.ShapeDtypeStruct((r, d), BF16),
        compiler_params=_cparams(("parallel",)),
    )(*ops)


def _rms_bwd(name, dy, x, gain, res=None, *, tr=512, after=None):
    r, d = x.shape
    tr = min(tr, r)
    n = r // tr
    has_res = res is not None

    def body(*refs):
        dy_ref, x_ref, g_ref = refs[:3]
        res_ref = refs[3] if has_res else None
        dx_ref, dg_ref, acc_ref = refs[-3:]
        i = pl.program_id(0)
        xv = x_ref[...]
        rstd = lax.rsqrt(jnp.mean(xv * xv, axis=-1, keepdims=True) + RMS_EPS)
        xh = xv * rstd
        dyv = dy_ref[...].astype(F32)
        dxh = dyv * g_ref[...]
        dx = rstd * (dxh - xh * jnp.mean(dxh * xh, axis=-1, keepdims=True))
        if has_res:
            dx = dx + res_ref[...]
        dx_ref[...] = dx
        part = (dyv * xh).reshape(tr // 8, 8, d).sum(axis=0)

        @pl.when(i == 0)
        def _():
            acc_ref[...] = part

        @pl.when(i > 0)
        def _():
            acc_ref[...] += part

        @pl.when(i == n - 1)
        def _():
            dg_ref[...] = jnp.sum(acc_ref[...], axis=0, keepdims=True)

    row = pl.BlockSpec((tr, d), lambda i: (i, 0))
    in_specs = [row, row, pl.BlockSpec((1, d), lambda i: (0, 0))] + ([row] if has_res else [])
    ops = [dy, x, gain] + ([res] if has_res else [])
    if after is not None:
        in_specs.append(pl.BlockSpec(after.shape, lambda i: (0, 0)))
        ops.append(after)
    return pl.pallas_call(
        body, name=name, grid=(n,), in_specs=in_specs,
        out_specs=(row, pl.BlockSpec((1, d), lambda i: (0, 0))),
        out_shape=(jax.ShapeDtypeStruct((r, d), F32), jax.ShapeDtypeStruct((1, d), F32)),
        scratch_shapes=[pltpu.VMEM((8, d), F32)],
        compiler_params=_cparams(("arbitrary",)),
    )(*ops)


def _final_loss(name, h, target, gain, *, tr=512):
    r, d = h.shape
    tr = min(tr, r)
    n = r // tr

    def body(h_ref, t_ref, g_ref, loss_ref, dh_ref, dg_ref, accl_ref, accg_ref):
        i = pl.program_id(0)
        xv = h_ref[...]
        rstd = lax.rsqrt(jnp.mean(xv * xv, axis=-1, keepdims=True) + RMS_EPS)
        xh = xv * rstd
        e = xh * g_ref[...] - t_ref[...]
        dyv = e * (1.0 / d)
        dxh = dyv * g_ref[...]
        dh_ref[...] = rstd * (dxh - xh * jnp.mean(dxh * xh, axis=-1, keepdims=True))
        lpart = (e * e).reshape(tr // 8, 8, d).sum(axis=0)
        gpart = (dyv * xh).reshape(tr // 8, 8, d).sum(axis=0)

        @pl.when(i == 0)
        def _():
            accl_ref[...] = lpart
            accg_ref[...] = gpart

        @pl.when(i > 0)
        def _():
            accl_ref[...] += lpart
            accg_ref[...] += gpart

        @pl.when(i == n - 1)
        def _():
            tot = jnp.sum(jnp.sum(accl_ref[...], axis=0, keepdims=True), axis=1, keepdims=True)
            loss_ref[...] = jnp.broadcast_to(tot * (0.5 / d), (1, LANE))
            dg_ref[...] = jnp.sum(accg_ref[...], axis=0, keepdims=True)

    row = pl.BlockSpec((tr, d), lambda i: (i, 0))
    one = pl.BlockSpec((1, d), lambda i: (0, 0))
    return pl.pallas_call(
        body, name=name, grid=(n,), in_specs=[row, row, one],
        out_specs=(pl.BlockSpec((1, LANE), lambda i: (0, 0)), row, one),
        out_shape=(jax.ShapeDtypeStruct((1, LANE), F32), jax.ShapeDtypeStruct((r, d), F32),
                   jax.ShapeDtypeStruct((1, d), F32)),
        scratch_shapes=[pltpu.VMEM((8, d), F32), pltpu.VMEM((8, d), F32)],
        compiler_params=_cparams(("arbitrary",)),
    )(h, target, gain)


_GELU_C = math.sqrt(2.0 / math.pi)


def _gelu_parts(z):
    inner = _GELU_C * (z + 0.044715 * z * z * z)
    t = jnp.tanh(inner)
    val = 0.5 * z * (1.0 + t)
    dinner = _GELU_C * (1.0 + 3.0 * 0.044715 * z * z)
    grad = 0.5 * (1.0 + t) + 0.5 * z * (1.0 - t * t) * dinner
    return val, grad


def _ssm_post_fwd(name, y8, u8, d8, *, tr=256):
    r, c = y8.shape
    tr = min(tr, r)

    def body(y_ref, u_ref, d_ref, o_ref):
        z = y_ref[...] + d_ref[...] * u_ref[...]
        o_ref[...] = _gelu_parts(z)[0].astype(o_ref.dtype)

    row = pl.BlockSpec((tr, c), lambda i: (i, 0))
    return pl.pallas_call(
        body, name=name, grid=(r // tr,), in_specs=[row, row, pl.BlockSpec((1, c), lambda i: (0, 0))],
        out_specs=row, out_shape=jax.ShapeDtypeStruct((r, c), BF16),
        compiler_params=_cparams(("parallel",)),
    )(y8, u8, d8)


def _ssm_post_bwd(name, dact8, y8, u8, d8, *, tr=256, after=None):
    r, c = y8.shape
    tr = min(tr, r)
    n = r // tr

    def body(*refs):
        da_ref, y_ref, u_ref, d_ref = refs[:4]
        dz_ref, dd_ref, acc_ref = refs[-3:]
        i = pl.program_id(0)
        uv = u_ref[...]
        z = y_ref[...] + d_ref[...] * uv
        dz = da_ref[...].astype(F32) * _gelu_parts(z)[1]
        dz_ref[...] = dz
        part = (dz * uv).reshape(tr // 8, 8, c).sum(axis=0)

        @pl.when(i == 0)
        def _():
            acc_ref[...] = part

        @pl.when(i > 0)
        def _():
            acc_ref[...] += part

        @pl.when(i == n - 1)
        def _():
            tot = jnp.sum(acc_ref[...], axis=0, keepdims=True)
            out = tot[:, 0:SSM_WIDTH]
            for j in range(1, c // SSM_WIDTH):
                out = out + tot[:, j * SSM_WIDTH:(j + 1) * SSM_WIDTH]
            dd_ref[...] = out

    row = pl.BlockSpec((tr, c), lambda i: (i, 0))
    in_specs = [row, row, row, pl.BlockSpec((1, c), lambda i: (0, 0))]
    ops = [dact8, y8, u8, d8]
    if after is not None:
        in_specs.append(pl.BlockSpec(memory_space=pl.ANY))
        ops.append(after)
    return pl.pallas_call(
        body, name=name, grid=(n,), in_specs=in_specs,
        out_specs=(row, pl.BlockSpec((1, SSM_WIDTH), lambda i: (0, 0))),
        out_shape=(jax.ShapeDtypeStruct((r, c), F32), jax.ShapeDtypeStruct((1, SSM_WIDTH), F32)),
        scratch_shapes=[pltpu.VMEM((8, c), F32)],
        compiler_params=_cparams(("arbitrary",)),
    )(*ops)


def _mix_fwd(name, glu, gates, out_b, *, tr=256):
    r = glu.shape[0]
    d = D_MODEL
    tr = min(tr, r)

    def body(glu_ref, gate_ref, ob_ref, o_ref):
        out_a = glu_ref[:, 0:d].astype(F32) * _sigmoid(glu_ref[:, d:2 * d].astype(F32))
        mix = (_sigmoid(gate_ref[:, 0:d].astype(F32)) * out_a
               + _sigmoid(gate_ref[:, d:2 * d].astype(F32)) * ob_ref[...].astype(F32))
        o_ref[...] = mix.astype(o_ref.dtype)

    wide = pl.BlockSpec((tr, 2 * d), lambda i: (i, 0))
    row = pl.BlockSpec((tr, d), lambda i: (i, 0))
    return pl.pallas_call(
        body, name=name, grid=(r // tr,), in_specs=[wide, wide, row], out_specs=row,
        out_shape=jax.ShapeDtypeStruct((r, d), BF16), compiler_params=_cparams(("parallel",)),
    )(glu, gates, out_b)


def _mix_bwd(name, dmix, glu, gates, out_b, *, tr=256):
    r = glu.shape[0]
    d = D_MODEL
    tr = min(tr, r)

    def body(dm_ref, glu_ref, gate_ref, ob_ref, dglu_ref, dgate_ref, dob_ref):
        dm = dm_ref[...]
        glu_a = glu_ref[:, 0:d].astype(F32)
        sb = _sigmoid(glu_ref[:, d:2 * d].astype(F32))
        ga = _sigmoid(gate_ref[:, 0:d].astype(F32))
        gb = _sigmoid(gate_ref[:, d:2 * d].astype(F32))
        out_a = glu_a * sb
        dout_a = dm * ga
        dglu_ref[:, 0:d] = (dout_a * sb).astype(dglu_ref.dtype)
        dglu_ref[:, d:2 * d] = (dout_a * glu_a * sb * (1.0 - sb)).astype(dglu_ref.dtype)
        dgate_ref[:, 0:d] = (dm * out_a * ga * (1.0 - ga)).astype(dgate_ref.dtype)
        dgate_ref[:, d:2 * d] = (dm * ob_ref[...].astype(F32) * gb * (1.0 - gb)).astype(dgate_ref.dtype)
        dob_ref[...] = (dm * gb).astype(dob_ref.dtype)

    wide = pl.BlockSpec((tr, 2 * d), lambda i: (i, 0))
    row = pl.BlockSpec((tr, d), lambda i: (i, 0))
    return pl.pallas_call(
        body, name=name, grid=(r // tr,), in_specs=[row, wide, wide, row], out_specs=(wide, wide, row),
        out_shape=(jax.ShapeDtypeStruct((r, 2 * d), BF16), jax.ShapeDtypeStruct((r, 2 * d), BF16),
                   jax.ShapeDtypeStruct((r, d), BF16)),
        compiler_params=_cparams(("parallel",)),
    )(dmix, glu, gates, out_b)


def _ssm_mats(lam_re, lam_im, log_dt, b_re, b_im, c_re, c_im, nc):
    hp = lax.Precision.HIGHEST
    t = SSM_CHUNK
    nq = SSM_GROUPS // 8
    lam = lax.complex(lam_re, lam_im)
    z = lam * jnp.exp(log_dt)[:, None]
    ks = jnp.arange(t + 1, dtype=F32)
    apow = jnp.exp(ks[:, None, None] * z[None])
    bbar = ((apow[1] - 1.0) / lam)[..., None] * lax.complex(b_re, b_im)
    c = lax.complex(c_re, c_im)

    ca = c[None] * apow[:, :, None, :]
    kmat = jnp.einsum("kgnp,gpm->kgnm", ca, bbar, precision=hp).real
    ii = np.arange(t)
    lag = ii[None, :] - ii[:, None]
    kt = kmat[np.clip(lag, 0, t)] * jnp.asarray(lag >= 0, F32)[:, :, None, None, None]
    kt = kt.reshape(t, t, nq, 8, SSM_GROUP, SSM_GROUP)
    m_c = kt.transpose(2, 0, 3, 5, 1, 4).reshape(nq, 1024, LANE)

    arev = jnp.exp((float(t - 1) - ks[:t])[:, None, None] * z[None])
    w = arev[:, :, :, None] * bbar[None]
    wr = jnp.stack([w.real, w.imag]).reshape(2, t, nq, 8, SSM_STATE, SSM_GROUP)
    bw_c = wr.transpose(2, 1, 3, 5, 0, 4).reshape(nq, 1024, LANE)

    ca1 = ca[1:]
    cr = jnp.stack([ca1.real, -ca1.imag]).reshape(2, t, nq, 8, SSM_GROUP, SSM_STATE)
    cm_c = cr.transpose(2, 0, 3, 5, 1, 4).reshape(nq, 1024, LANE)

    def tiles(v):
        vq = jnp.concatenate([v.real.reshape(nq, 512), v.imag.reshape(nq, 512)], axis=1)
        return jnp.broadcast_to(vq.reshape(nq, 8, 1, LANE), (nq, 8, 8, LANE))

    return m_c, bw_c, cm_c, tiles(apow[t]), tiles(jnp.exp(float(nc) * z))


_BD_M = (LANE, SSM_GROUP)
_BD_BW = (LANE, SSM_STATE)
_BD_CM = (512, SSM_GROUP)


def _bd_perm(cn):
    rr = lax.broadcasted_iota(jnp.int32, (1024, 1024), 0)
    cc = lax.broadcasted_iota(jnp.int32, (1024, 1024), 1)
    sh = cn.bit_length() - 1
    src = ((rr >> 7) << sh) + (((rr & (LANE - 1)) >> sh) << (3 + sh)) + (rr & (cn - 1))
    return jnp.where(src == cc, 1.0, 0.0).astype(BF16)


def _bd_rowgroup(span):
    r = lax.broadcasted_iota(jnp.int32, (1024, LANE), 0)
    return (r & (span - 1)) >> ((span // 8).bit_length() - 1)


def _bd_expand(name, kind, compact):
    span, cn = kind
    nq = compact.shape[0]

    def body(c_ref, o_ref, perm_scr):
        @pl.when(pl.program_id(0) == 0)
        def _():
            perm_scr[...] = _bd_perm(cn)

        x = c_ref[...]
        grp = _bd_rowgroup(span)
        xcat = jnp.concatenate([jnp.where(grp == h, x, 0.0) for h in range(8)], axis=1)
        o_ref[...] = _bdot(xcat, perm_scr[...], _DIMS["nn"]).astype(o_ref.dtype)

    return pl.pallas_call(
        body, name=name, grid=(nq,), in_specs=[pl.BlockSpec((None, 1024, LANE), lambda q: (q, 0, 0))],
        out_specs=pl.BlockSpec((None, 1024, 1024), lambda q: (q, 0, 0)),
        out_shape=jax.ShapeDtypeStruct((nq, 1024, 1024), BF16),
        scratch_shapes=[pltpu.VMEM((1024, 1024), BF16)],
        compiler_params=_cparams(("arbitrary",)),
    )(compact)


def _bd_reduce(name, kind, dbig):
    span, cn = kind
    nq = dbig.shape[0]

    def body(g_ref, o_ref, perm_scr):
        @pl.when(pl.program_id(0) == 0)
        def _():
            perm_scr[...] = _bd_perm(cn)

        back = _bdot(g_ref[...], perm_scr[...], _DIMS["nt"])
        grp = _bd_rowgroup(span)
        out = jnp.zeros((1024, LANE), F32)
        for h in range(8):
            out = jnp.where(grp == h, back[:, h * LANE:(h + 1) * LANE], out)
        o_ref[...] = out

    return pl.pallas_call(
        body, name=name, grid=(nq,), in_specs=[pl.BlockSpec((None, 1024, 1024), lambda q: (q, 0, 0))],
        out_specs=pl.BlockSpec((None, 1024, LANE), lambda q: (q, 0, 0)),
        out_shape=jax.ShapeDtypeStruct((nq, 1024, LANE), F32),
        scratch_shapes=[pltpu.VMEM((1024, 1024), BF16)],
        compiler_params=_cparams(("arbitrary",)),
    )(dbig)


def _x_tile_specs(nc, nq):
    return [pl.BlockSpec((nc, LANE), lambda q, t, i=i: (0, i * nq + q)) for i in range(SSM_CHUNK)]


def _cat_tiles(refs):
    return jnp.concatenate([r[...] for r in refs], axis=1)


def _ssm_w(name, x8, bw):
    nc = x8.shape[0]
    nq = bw.shape[0]

    def body(*refs):
        xq = _cat_tiles(refs[:8])
        refs[9][...] = _bdot(xq, refs[8][...], _DIMS["nn"])

    return pl.pallas_call(
        body, name=name, grid=(nq, 8),
        in_specs=_x_tile_specs(nc, nq) + [pl.BlockSpec((None, 1024, LANE), lambda q, t: (q, 0, t))],
        out_specs=pl.BlockSpec((None, None, nc, LANE), lambda q, t: (q, t, 0, 0)),
        out_shape=jax.ShapeDtypeStruct((nq, 8, nc, LANE), F32),
        compiler_params=_cparams(("parallel", "arbitrary")),
    )(*([x8] * 8), bw)


def _ssm_scan(name, w4, a_t, aseg_t, *, reverse, sprev4=None):
    nq, _, nc, _ = w4.shape
    ns = nc // 8
    with_da = sprev4 is not None

    def body(*refs):
        w_ref, a_ref, aseg_ref = refs[:3]
        s_ref = refs[3] if with_da else None
        o_ref = refs[4] if with_da else refs[3]
        da_ref = refs[5] if with_da else None
        sgn = -1.0 if reverse else 1.0
        ar = [a_ref[j] for j in range(4)]
        ai = [sgn * a_ref[j + 4] for j in range(4)]
        gr = [aseg_ref[j] for j in range(4)]
        gi = [sgn * aseg_ref[j + 4] for j in range(4)]
        zero = tuple(jnp.zeros((8, LANE), F32) for _ in range(8))

        def rows(tt):
            return pl.ds((ns - 1 - tt) if reverse else tt, 8, stride=ns)

        def step(carry, w):
            new_r = [ar[j] * carry[j] - ai[j] * carry[j + 4] + w[j] for j in range(4)]
            new_i = [ar[j] * carry[j + 4] + ai[j] * carry[j] + w[j + 4] for j in range(4)]
            return tuple(new_r + new_i)

        def pass1(tt, carry):
            return step(carry, [w_ref[j, rows(tt), :] for j in range(8)])

        ends = lax.fori_loop(0, ns, pass1, zero)
        sub = lax.broadcasted_iota(jnp.int32, (8, LANE), 0)
        init = list(zero)
        order = range(7, 0, -1) if reverse else range(0, 7)
        for s in order:
            nxt = s - 1 if reverse else s + 1
            cand_r = [gr[j] * init[j] - gi[j] * init[j + 4] + ends[j] for j in range(4)]
            cand_i = [gr[j] * init[j + 4] + gi[j] * init[j] + ends[j + 4] for j in range(4)]
            cand = cand_r + cand_i
            shift = 7 if reverse else 1
            init = [jnp.where(sub == nxt, pltpu.roll(cand[j], shift, axis=0), init[j]) for j in range(8)]

        def pass2(tt, state):
            carry, acc = state
            r = rows(tt)
            for j in range(8):
                o_ref[j, r, :] = carry[j]
            if with_da:
                sp = [s_ref[j, r, :] for j in range(8)]
                acc_r = [acc[j] + carry[j] * sp[j] + carry[j + 4] * sp[j + 4] for j in range(4)]
                acc_i = [acc[j + 4] + carry[j + 4] * sp[j] - carry[j] * sp[j + 4] for j in range(4)]
                acc = tuple(acc_r + acc_i)
            return step(carry, [w_ref[j, r, :] for j in range(8)]), acc

        _, acc = lax.fori_loop(0, ns, pass2, (tuple(init), zero))
        if with_da:
            for j in range(8):
                da_ref[j] = acc[j]

    big = pl.BlockSpec((None, 8, nc, LANE), lambda q: (q, 0, 0, 0))
    small = pl.BlockSpec((None, 8, 8, LANE), lambda q: (q, 0, 0, 0))
    in_specs = [big, small, small] + ([big] if with_da else [])
    ops = [w4, a_t, aseg_t] + ([sprev4] if with_da else [])
    out_specs = (big, small) if with_da else big
    big_s = jax.ShapeDtypeStruct((nq, 8, nc, LANE), F32)
    out_shape = (big_s, jax.ShapeDtypeStruct((nq, 8, 8, LANE), F32)) if with_da else big_s
    return pl.pallas_call(
        body, name=name, grid=(nq,), in_specs=in_specs, out_specs=out_specs, out_shape=out_shape,
        compiler_params=_cparams(("parallel",)),
    )(*ops)


def _ssm_y(name, x8, sprev4, m_mat, cm_mat):
    nc = x8.shape[0]
    nq = m_mat.shape[0]

    def body(*refs):
        xq = _cat_tiles(refs[:8])
        s_ref, m_ref, cm_ref, o_ref = refs[8:12]
        sq = jnp.concatenate([s_ref[t] for t in range(8)], axis=1)
        o_ref[...] = _bdot(xq, m_ref[...], _DIMS["nn"]) + _bdot(sq, cm_ref[...], _DIMS["nn"])

    col = pl.BlockSpec((None, 1024, LANE), lambda q, j: (q, 0, j))
    return pl.pallas_call(
        body, name=name, grid=(nq, 8),
        in_specs=_x_tile_specs(nc, nq) + [pl.BlockSpec((None, 8, nc, LANE), lambda q, j: (q, 0, 0, 0)), col, col],
        out_specs=pl.BlockSpec((nc, LANE), lambda q, j: (0, j * nq + q)),
        out_shape=jax.ShapeDtypeStruct((nc, 8 * SSM_WIDTH), F32),
        compiler_params=_cparams(("parallel", "arbitrary")),
    )(*([x8] * 8), sprev4, m_mat, cm_mat)


def _ssm_ds(name, dz8, sprev4, cm_mat):
    nc = dz8.shape[0]
    nq = cm_mat.shape[0]

    def body(*refs):
        dyq = _cat_tiles(refs[:8]).astype(BF16)
        s_ref, cm_ref, ds_ref, dcm_ref = refs[8:12]
        ds_ref[...] = _bdot(dyq, cm_ref[...], _DIMS["nt"])
        dcm_ref[...] = _bdot(s_ref[...], dyq, _DIMS["tn"])

    tile = pl.BlockSpec((None, None, nc, LANE), lambda q, t: (q, t, 0, 0))
    rowblk = pl.BlockSpec((None, LANE, 1024), lambda q, t: (q, t, 0))
    return pl.pallas_call(
        body, name=name, grid=(nq, 8),
        in_specs=_x_tile_specs(nc, nq) + [tile, rowblk],
        out_specs=(tile, rowblk),
        out_shape=(jax.ShapeDtypeStruct((nq, 8, nc, LANE), F32), jax.ShapeDtypeStruct((nq, 1024, 1024), F32)),
        compiler_params=_cparams(("parallel", "arbitrary")),
    )(*([dz8] * 8), sprev4, cm_mat)


def _ssm_dx(name, dz8, g4, x8, m_mat, bw_mat, d8):
    nc = dz8.shape[0]
    nq = m_mat.shape[0]

    def body(*refs):
        dyq = _cat_tiles(refs[:8]).astype(BF16)
        g_ref, x_ref, m_ref, bw_ref, d_ref, dzi_ref, dx_ref, dm_ref, dbw_ref = refs[8:17]
        gq = jnp.concatenate([g_ref[t] for t in range(8)], axis=1).astype(BF16)
        dx = _bdot(dyq, m_ref[...], _DIMS["nt"]) + _bdot(gq, bw_ref[...], _DIMS["nt"])
        dx_ref[...] = (dx + d_ref[...] * dzi_ref[...]).astype(dx_ref.dtype)
        xi = x_ref[...]
        dm_ref[...] = _bdot(xi, dyq, _DIMS["tn"])
        dbw_ref[...] = _bdot(xi, gq, _DIMS["tn"])

    xtile = pl.BlockSpec((nc, LANE), lambda q, i: (0, i * nq + q))
    rowblk = pl.BlockSpec((None, LANE, 1024), lambda q, i: (q, i, 0))
    return pl.pallas_call(
        body, name=name, grid=(nq, 8),
        in_specs=_x_tile_specs(nc, nq) + [pl.BlockSpec((None, 8, nc, LANE), lambda q, i: (q, 0, 0, 0)), xtile, rowblk, rowblk,
                                          pl.BlockSpec((1, LANE), lambda q, i: (0, q)), xtile],
        out_specs=(xtile, rowblk, rowblk),
        out_shape=(jax.ShapeDtypeStruct((nc, 8 * SSM_WIDTH), BF16), jax.ShapeDtypeStruct((nq, 1024, 1024), F32),
                   jax.ShapeDtypeStruct((nq, 1024, 1024), F32)),
        compiler_params=_cparams(("parallel", "arbitrary")),
    )(*([dz8] * 8), g4, x8, m_mat, bw_mat, d8, dz8)


CUM_BLK = 256


def _split3(x):
    hi = x.astype(BF16)
    r1 = x - hi.astype(F32)
    mid = r1.astype(BF16)
    lo = (r1 - mid.astype(F32)).astype(BF16)
    return hi, mid, lo


def _tri_dot(x, tri):
    hi, mid, lo = _split3(x)
    d = _DIMS["nn"]
    return _bdot(hi, tri, d) + _bdot(mid, tri, d) + _bdot(lo, tri, d)


def _tri(n, lower):
    r = lax.broadcasted_iota(jnp.int32, (n, n), 0)
    c = lax.broadcasted_iota(jnp.int32, (n, n), 1)
    return jnp.where((r >= c) if lower else (r <= c), 1.0, 0.0).astype(BF16)


def _fox_cum(name, fproj, bcol):
    seq = fproj.shape[0]
    blk = min(CUM_BLK, seq)

    def body(f_ref, b_ref, o_ref, carry_ref):
        i = pl.program_id(0)

        @pl.when(i == 0)
        def _():
            carry_ref[...] = jnp.zeros_like(carry_ref)

        z = f_ref[...].T + b_ref[...]
        logf = jnp.minimum(z, 0.0) - jnp.log(1.0 + jnp.exp(-jnp.abs(z)))
        carry = carry_ref[...]
        cum = _tri_dot(logf, _tri(blk, lower=False)) + jnp.tile(carry, (1, blk // LANE))
        o_ref[...] = cum[0:8, :]
        carry_ref[...] = carry + jnp.sum(logf, axis=1, keepdims=True)

    return pl.pallas_call(
        body, name=name, grid=(seq // blk,),
        in_specs=[pl.BlockSpec((blk, LANE), lambda i: (i, 0)), pl.BlockSpec((LANE, 1), lambda i: (0, 0))],
        out_specs=pl.BlockSpec((8, blk), lambda i: (0, i)),
        out_shape=jax.ShapeDtypeStruct((8, seq), F32),
        scratch_shapes=[pltpu.VMEM((LANE, LANE), F32)],
        compiler_params=_cparams(("arbitrary",)),
    )(fproj, bcol)


def _fox_cum_bwd(name, dcum_t, fproj, bcol):
    seq = fproj.shape[0]
    blk = min(CUM_BLK, seq)
    n = seq // blk

    def body(dc_ref, f_ref, b_ref, df_ref, db_ref, carry_ref, acc_ref):
        i = pl.program_id(0)

        @pl.when(i == 0)
        def _():
            carry_ref[...] = jnp.zeros_like(carry_ref)
            acc_ref[...] = jnp.zeros_like(acc_ref)

        dc = jnp.concatenate([dc_ref[...], jnp.zeros((LANE - 8, blk), F32)], axis=0)
        carry = carry_ref[...]
        dlogf = _tri_dot(dc, _tri(blk, lower=True)) + jnp.tile(carry, (1, blk // LANE))
        carry_ref[...] = carry + jnp.sum(dc, axis=1, keepdims=True)
        z = f_ref[...].T + b_ref[...]
        dft = dlogf / (1.0 + jnp.exp(z))
        df_ref[...] = dft.T.astype(df_ref.dtype)
        acc_ref[...] += jnp.sum(dft, axis=1, keepdims=True)

        @pl.when(i == n - 1)
        def _():
            db_ref[...] = acc_ref[...]

    return pl.pallas_call(
        body, name=name, grid=(n,),
        in_specs=[pl.BlockSpec((8, blk), lambda i: (0, n - 1 - i)), pl.BlockSpec((blk, LANE), lambda i: (n - 1 - i, 0)),
                  pl.BlockSpec((LANE, 1), lambda i: (0, 0))],
        out_specs=(pl.BlockSpec((blk, LANE), lambda i: (n - 1 - i, 0)), pl.BlockSpec((LANE, LANE), lambda i: (0, 0))),
        out_shape=(jax.ShapeDtypeStruct((seq, LANE), BF16), jax.ShapeDtypeStruct((LANE, LANE), F32)),
        scratch_shapes=[pltpu.VMEM((LANE, LANE), F32), pltpu.VMEM((LANE, LANE), F32)],
        compiler_params=_cparams(("arbitrary",)),
    )(dcum_t, fproj, bcol)


FOX_BLK = 512
FOX_SCALE = FOX_HEAD_DIM ** -0.5


def _fox_head_mask(shape, hh):
    lane = lax.broadcasted_iota(jnp.int32, shape, 1)
    return (lane < FOX_HEAD_DIM) if hh == 0 else (lane >= FOX_HEAD_DIM)


def _fox_bias(cum_ref, hh, q0, k0, blk):
    c0 = jnp.max(cum_ref[hh:hh + 1, pl.ds(q0, LANE)], axis=1, keepdims=True)
    return c0 - cum_ref[hh:hh + 1, pl.ds(k0, blk)]


def _fox_fwd(name, qkv, cum_t):
    seq = qkv.shape[0]
    blk = min(FOX_BLK, seq)
    nb = seq // blk
    npair = FOX_HEADS // 2

    def body(q_ref, k_ref, v_ref, cum_ref, o_ref, lse_ref):
        iq = pl.program_id(1)
        q0 = pl.multiple_of(iq * blk, blk)
        qv = q_ref[...]
        row = lax.broadcasted_iota(jnp.int32, (blk, blk), 0)
        col = lax.broadcasted_iota(jnp.int32, (blk, blk), 1)
        qhs = [jnp.where(_fox_head_mask(qv.shape, hh), qv, jnp.zeros_like(qv)) * FOX_SCALE for hh in range(2)]

        def block(kb, states, masked):
            k0 = pl.multiple_of(kb * blk, blk)
            kv = k_ref[pl.ds(k0, blk), :]
            vv = v_ref[pl.ds(k0, blk), :]
            new = []
            for hh in range(2):
                m, l, acc = states[hh]
                s = _bdot(qhs[hh], kv, _DIMS["nt"]) + _fox_bias(cum_ref, hh, q0, k0, blk)
                if masked:
                    s = jnp.where(row >= col, s, -jnp.inf)
                m_new = jnp.maximum(m, jnp.max(s, axis=1, keepdims=True))
                alpha = jnp.exp(m - m_new)
                p = jnp.exp(s - m_new)
                l = alpha * l + jnp.sum(p, axis=1, keepdims=True)
                acc = alpha * acc + _bdot(p, vv, _DIMS["nn"])
                new.append((m_new, l, acc))
            return tuple(new)

        init = (jnp.full((blk, 1), -jnp.inf, F32), jnp.zeros((blk, 1), F32), jnp.zeros((blk, LANE), F32))
        states = lax.fori_loop(0, iq, lambda kb, st: block(kb, st, False), (init, init))
        states = block(iq, states, True)
        outs = []
        for hh in range(2):
            m, l, acc = states[hh]
            outs.append(acc / l)
            lse_ref[hh] = jnp.broadcast_to(m + jnp.log(l), (blk, LANE))
        o_ref[...] = jnp.where(_fox_head_mask(outs[0].shape, 0), outs[0], outs[1]).astype(o_ref.dtype)

    return pl.pallas_call(
        body, name=name, grid=(npair, nb),
        in_specs=[pl.BlockSpec((blk, LANE), lambda p, i: (i, p)),
                  pl.BlockSpec((seq, LANE), lambda p, i: (0, npair + p)),
                  pl.BlockSpec((seq, LANE), lambda p, i: (0, 2 * npair + p)),
                  pl.BlockSpec((None, 2, seq), lambda p, i: (p, 0, 0))],
        out_specs=(pl.BlockSpec((blk, LANE), lambda p, i: (i, p)),
                   pl.BlockSpec((2, blk, LANE), lambda p, i: (p, i, 0))),
        out_shape=(jax.ShapeDtypeStruct((seq, FOX_WIDTH), BF16), jax.ShapeDtypeStruct((FOX_HEADS, seq, LANE), F32)),
        compiler_params=_cparams(("parallel", "arbitrary")),
    )(qkv, qkv, qkv, cum_t)


def _fox_bwd(name, qkv, cum_t, att, datt, lse):
    seq = qkv.shape[0]
    blk = min(FOX_BLK, seq)
    nb = seq // blk
    npair = FOX_HEADS // 2

    def body(q_ref, k_ref, v_ref, cum_ref, o_ref, do_ref, lse_ref, dq_ref, dk_ref, dv_ref, dcum_ref):
        iq = pl.program_id(1)
        q0 = pl.multiple_of(iq * blk, blk)

        @pl.when(iq == 0)
        def _():
            dk_ref[...] = jnp.zeros_like(dk_ref)
            dv_ref[...] = jnp.zeros_like(dv_ref)
            dcum_ref[...] = jnp.zeros_like(dcum_ref)

        qv = q_ref[...]
        dov = do_ref[...].astype(F32)
        ov = o_ref[...].astype(F32)
        row = lax.broadcasted_iota(jnp.int32, (blk, blk), 0)
        col = lax.broadcasted_iota(jnp.int32, (blk, blk), 1)
        qhs, dohbs, deltas, lses = [], [], [], []
        for hh in range(2):
            hm = _fox_head_mask(qv.shape, hh)
            qhs.append(jnp.where(hm, qv, jnp.zeros_like(qv)) * FOX_SCALE)
            doh = jnp.where(hm, dov, 0.0)
            dohbs.append(doh.astype(BF16))
            deltas.append(jnp.sum(doh * ov, axis=1, keepdims=True))
            lses.append(jnp.tile(lse_ref[hh], (1, blk // LANE)))

        def block(kb, accs, masked):
            k0 = pl.multiple_of(kb * blk, blk)
            kv = k_ref[pl.ds(k0, blk), :]
            vv = v_ref[pl.ds(k0, blk), :]
            new = []
            dk_blk = None
            dv_blk = None
            for hh in range(2):
                dq_acc, rs_acc = accs[hh]
                s = _bdot(qhs[hh], kv, _DIMS["nt"]) + _fox_bias(cum_ref, hh, q0, k0, blk)
                p = jnp.exp(s - lses[hh])
                if masked:
                    p = jnp.where(row >= col, p, 0.0)
                dp = _bdot(dohbs[hh], vv, _DIMS["nt"])
                ds = p * (dp - deltas[hh])
                dsb = ds.astype(BF16)
                dk_h = _bdot(dsb, qhs[hh], _DIMS["tn"])
                dv_h = _bdot(p, dohbs[hh], _DIMS["tn"])
                dk_blk = dk_h if dk_blk is None else dk_blk + dk_h
                dv_blk = dv_h if dv_blk is None else dv_blk + dv_h
                dcum_ref[hh:hh + 1, pl.ds(k0, blk)] -= jnp.sum(ds, axis=0, keepdims=True)
                new.append((dq_acc + _bdot(dsb, kv, _DIMS["nn"]), rs_acc + jnp.sum(ds, axis=1, keepdims=True)))
            dk_ref[pl.ds(k0, blk), :] += dk_blk
            dv_ref[pl.ds(k0, blk), :] += dv_blk
            return tuple(new)

        init = (jnp.zeros((blk, LANE), F32), jnp.zeros((blk, 1), F32))
        accs = lax.fori_loop(0, iq, lambda kb, a: block(kb, a, False), (init, init))
        accs = block(iq, accs, True)
        for hh in range(2):
            dcum_ref[hh:hh + 1, pl.ds(q0, blk)] += jnp.broadcast_to(accs[hh][1], (blk, LANE)).T[0:1, :]
        dq = jnp.where(_fox_head_mask(qv.shape, 0), accs[0][0], accs[1][0]) * FOX_SCALE
        dq_ref[...] = dq.astype(dq_ref.dtype)

    qblk = pl.BlockSpec((blk, LANE), lambda p, i: (i, p))
    full = pl.BlockSpec((seq, LANE), lambda p, i: (0, p))
    return pl.pallas_call(
        body, name=name, grid=(npair, nb),
        in_specs=[qblk,
                  pl.BlockSpec((seq, LANE), lambda p, i: (0, npair + p)),
                  pl.BlockSpec((seq, LANE), lambda p, i: (0, 2 * npair + p)),
                  pl.BlockSpec((None, 2, seq), lambda p, i: (p, 0, 0)),
                  qblk, qblk,
                  pl.BlockSpec((2, blk, LANE), lambda p, i: (p, i, 0))],
        out_specs=(qblk, full, full, pl.BlockSpec((None, 2, seq), lambda p, i: (p, 0, 0))),
        out_shape=(jax.ShapeDtypeStruct((seq, FOX_WIDTH), BF16), jax.ShapeDtypeStruct((seq, FOX_WIDTH), F32),
                   jax.ShapeDtypeStruct((seq, FOX_WIDTH), F32), jax.ShapeDtypeStruct((npair, 2, seq), F32)),
        compiler_params=_cparams(("arbitrary", "arbitrary")),
    )(qkv, qkv, qkv, cum_t, att, datt, lse)


MEM_SCALE = MEM_HEAD_DIM ** -0.5


def _mem_probs(qh, kh):
    s = _bdot(qh, kh, _DIMS["nt"]) * MEM_SCALE
    p = jnp.exp(s - jnp.max(s, axis=1, keepdims=True))
    return p / jnp.sum(p, axis=1, keepdims=True)


def _mem_fwd(name, q2, kv, *, tr=512):
    seq = q2.shape[0]
    mlen = kv.shape[0]
    tr = min(tr, seq)

    def body(q_ref, kv_ref, o_ref):
        for h in range(MEM_HEADS):
            sl = slice(h * MEM_HEAD_DIM, (h + 1) * MEM_HEAD_DIM)
            sv = slice(MEM_WIDTH + h * MEM_HEAD_DIM, MEM_WIDTH + (h + 1) * MEM_HEAD_DIM)
            p = _mem_probs(q_ref[:, sl], kv_ref[:, sl])
            o_ref[:, sl] = _bdot(p, kv_ref[:, sv], _DIMS["nn"]).astype(o_ref.dtype)

    return pl.pallas_call(
        body, name=name, grid=(seq // tr,),
        in_specs=[pl.BlockSpec((tr, MEM_WIDTH), lambda i: (i, 0)), pl.BlockSpec((mlen, 2 * MEM_WIDTH), lambda i: (0, 0))],
        out_specs=pl.BlockSpec((tr, MEM_WIDTH), lambda i: (i, 0)),
        out_shape=jax.ShapeDtypeStruct((seq, MEM_WIDTH), BF16),
        compiler_params=_cparams(("parallel",)),
    )(q2, kv)


def _mem_bwd(name, q2, kv, do2, *, tr=512):
    seq = q2.shape[0]
    mlen = kv.shape[0]
    tr = min(tr, seq)

    def body(q_ref, kv_ref, do_ref, dq_ref, dkv_ref):
        i = pl.program_id(0)

        @pl.when(i == 0)
        def _():
            dkv_ref[...] = jnp.zeros_like(dkv_ref)

        for h in range(MEM_HEADS):
            sl = slice(h * MEM_HEAD_DIM, (h + 1) * MEM_HEAD_DIM)
            sv = slice(MEM_WIDTH + h * MEM_HEAD_DIM, MEM_WIDTH + (h + 1) * MEM_HEAD_DIM)
            qh = q_ref[:, sl]
            kh = kv_ref[:, sl]
            doh = do_ref[:, sl].astype(BF16)
            p = _mem_probs(qh, kh)
            dp = _bdot(doh, kv_ref[:, sv], _DIMS["nt"])
            ds = (p * (dp - jnp.sum(p * dp, axis=1, keepdims=True)) * MEM_SCALE).astype(BF16)
            dq_ref[:, sl] = _bdot(ds, kh, _DIMS["nn"]).astype(dq_ref.dtype)
            dkv_ref[:, sl] += _bdot(ds, qh, _DIMS["tn"])
            dkv_ref[:, sv] += _bdot(p, doh, _DIMS["tn"])

    row = pl.BlockSpec((tr, MEM_WIDTH), lambda i: (i, 0))
    kvs = pl.BlockSpec((mlen, 2 * MEM_WIDTH), lambda i: (0, 0))
    return pl.pallas_call(
        body, name=name, grid=(seq // tr,), in_specs=[row, kvs, row], out_specs=(row, kvs),
        out_shape=(jax.ShapeDtypeStruct((seq, MEM_WIDTH), BF16), jax.ShapeDtypeStruct((mlen, 2 * MEM_WIDTH), F32)),
        compiler_params=_cparams(("arbitrary",)),
    )(q2, kv, do2)


_HBM = pl.BlockSpec(memory_space=pl.ANY)
_HBM_ONLY = pl.BlockSpec(memory_space=pltpu.HBM)
_MESH = pl.DeviceIdType.MESH


def _mesh_place():
    x, y, c = lax.axis_index("x"), lax.axis_index("y"), lax.axis_index("c")
    other_chips = [(1 - x, y), (x, 1 - y), (1 - x, 1 - y)]
    return x, y, c, other_chips


def _gather_all(name, arrays):
    n = len(arrays)

    def body(*refs):
        ins, outs = refs[:n], refs[n:2 * n]
        send_sems, recv_sems, local_sems = refs[2 * n:]
        x, y, c, chips = _mesh_place()
        me, sibling = (x, y, c), (x, y, 1 - c)

        def slot(a, place):
            px, py, pc = place
            return outs[a].at[4 * px + 2 * py + pc]

        def copy(a, k, block, to, src=None):
            return pltpu.make_async_remote_copy(
                src_ref=slot(a, block) if src is None else src, dst_ref=slot(a, block),
                send_sem=send_sems.at[a, k], recv_sem=recv_sems.at[a, k], device_id=to, device_id_type=_MESH)

        mine = [pltpu.make_async_copy(ins[a], slot(a, me), local_sems.at[a]) for a in range(n)]
        for cp in mine:
            cp.start()
        first = []
        for a in range(n):
            first.append(copy(a, 0, me, sibling, src=ins[a]))
            first += [copy(a, 1 + j, me, (*chip, c), src=ins[a]) for j, chip in enumerate(chips)]
        for cp in first:
            cp.start()
        passed = []
        for j, chip in enumerate(chips):
            for a in range(n):
                copy(a, 1 + j, (*chip, c), me).wait_recv()
                fwd = copy(a, 4 + j, (*chip, c), sibling)
                fwd.start()
                passed.append(fwd)
        for a in range(n):
            copy(a, 0, sibling, me).wait_recv()
            for j, chip in enumerate(chips):
                copy(a, 4 + j, (*chip, 1 - c), me).wait_recv()
        for cp in first + passed:
            cp.wait_send()
        for cp in mine:
            cp.wait()

    out_shape = tuple(jax.ShapeDtypeStruct((N_DEV,) + arr.shape, arr.dtype) for arr in arrays)
    return pl.pallas_call(
        body, name=name, in_specs=[_HBM] * n, out_specs=tuple([_HBM] * n), out_shape=out_shape,
        scratch_shapes=[pltpu.SemaphoreType.DMA((n, N_DEV - 1)), pltpu.SemaphoreType.DMA((n, N_DEV - 1)),
                        pltpu.SemaphoreType.DMA((n,))],
    )(*arrays)


_SEM = pl.BlockSpec(memory_space=pltpu.SEMAPHORE)
_DATAFLOW = pltpu.SideEffectType.DATAFLOW_SIDE_EFFECTING


def _device_index():
    return (4 * lax.axis_index("x") + 2 * lax.axis_index("y") + lax.axis_index("c")).astype(jnp.int32).reshape(1)


def _place_own(name, pieces, *, stacked_src, after=None):
    n = len(pieces)
    n_in = n + (after is not None)

    def body(me_ref, *refs):
        for a in range(n):
            refs[n_in + a][...] = refs[a][...]

    def spec(shape):
        return pl.BlockSpec((None,) + tuple(shape), lambda i, me_ref: (me_ref[0],) + (0,) * len(shape))

    shapes = [p.shape[1:] if stacked_src else p.shape for p in pieces]
    if stacked_src:
        in_specs = [spec(s) for s in shapes]
    else:
        in_specs = [pl.BlockSpec(tuple(s), lambda i, me_ref, nd=len(s): (0,) * nd) for s in shapes]
    operands = list(pieces)
    if after is not None:
        in_specs.append(_HBM)
        operands.append(after)
    return pl.pallas_call(
        body, name=name,
        grid_spec=pltpu.PrefetchScalarGridSpec(num_scalar_prefetch=1, grid=(1,), in_specs=in_specs,
                                               out_specs=tuple(spec(s) for s in shapes)),
        out_shape=tuple(jax.ShapeDtypeStruct((N_DEV,) + tuple(s), p.dtype) for s, p in zip(shapes, pieces)),
        compiler_params=_cparams(("arbitrary",)),
    )(_device_index(), *operands)


def _peer_places():
    x, y, c = lax.axis_index("x"), lax.axis_index("y"), lax.axis_index("c")
    peers = []
    for k in range(N_DEV - 1):
        flip = k + 1
        px = 1 - x if flip & 4 else x
        py = 1 - y if flip & 2 else y
        pc = 1 - c if flip & 1 else c
        peers.append((px, py, pc, 4 * px + 2 * py + pc))
    return 4 * x + 2 * y + c, peers


def _direct_copy(srcs, lands, send_sems, recv_sems, a, k, me, peer, scatter):
    px, py, pc, pidx = peer
    return pltpu.make_async_remote_copy(
        src_ref=srcs[a].at[pidx] if scatter else srcs[a], dst_ref=lands[a].at[me],
        send_sem=send_sems.at[a * (N_DEV - 1) + k], recv_sem=recv_sems.at[a * (N_DEV - 1) + k],
        device_id=(px, py, pc), device_id_type=_MESH)


def _send_start(name, srcs, lands, *, scatter):
    n = len(srcs)

    def body(*refs):
        src_refs, land_refs = refs[:n], refs[n:2 * n]
        send_sems, recv_sems = refs[2 * n], refs[2 * n + 1]
        token = refs[-1]
        me, peers = _peer_places()
        for k, peer in enumerate(peers):
            for a in range(n):
                _direct_copy(src_refs, land_refs, send_sems, recv_sems, a, k, me, peer, scatter).start()
        token[...] = jnp.zeros_like(token)

    hbm_shapes = [pltpu.HBM(t.shape, t.dtype) for t in list(srcs) + list(lands)]
    outs = pl.pallas_call(
        body, name=name,
        out_shape=(pltpu.SemaphoreType.DMA((n * (N_DEV - 1),)), pltpu.SemaphoreType.DMA((n * (N_DEV - 1),)), *hbm_shapes,
                   jax.ShapeDtypeStruct((8, LANE), F32)),
        in_specs=[_HBM_ONLY] * (2 * n),
        out_specs=(_SEM, _SEM, *([_HBM_ONLY] * (2 * n)), pl.BlockSpec(memory_space=pltpu.VMEM)),
        input_output_aliases={i: 2 + i for i in range(2 * n)},
        compiler_params=pltpu.CompilerParams(has_side_effects=_DATAFLOW),
    )(*[pltpu.with_memory_space_constraint(t, pltpu.HBM) for t in list(srcs) + list(lands)])
    return outs[0], outs[1], outs[2:2 + n], outs[2 + n:2 + 2 * n], outs[-1]


def _send_wait(name, send_sems, recv_sems, srcs, lands, after, *, scatter):
    n = len(srcs)

    def body(*refs):
        src_refs, land_refs = refs[:n], refs[n:2 * n]
        send_sems, recv_sems = refs[2 * n], refs[2 * n + 1]
        me, peers = _peer_places()
        for k, peer in enumerate(peers):
            for a in range(n):
                cp = _direct_copy(src_refs, land_refs, send_sems, recv_sems, a, k, me, peer, scatter)
                cp.wait_send()
                cp.wait_recv()

    hbm_shapes = [pltpu.HBM(t.shape, t.dtype) for t in list(srcs) + list(lands)]
    outs = pl.pallas_call(
        body, name=name, out_shape=tuple(hbm_shapes),
        in_specs=[_HBM_ONLY] * (2 * n) + [_SEM, _SEM, _HBM],
        out_specs=tuple([_HBM_ONLY] * (2 * n)),
        input_output_aliases={i: i for i in range(2 * n)},
        compiler_params=pltpu.CompilerParams(has_side_effects=_DATAFLOW),
    )(*srcs, *lands, send_sems, recv_sems, after)
    return outs[n:]


def _unstack_cols(name, stacked):
    n, rows, cols = stacked.shape

    def body(i_ref, o_ref):
        o_ref[...] = i_ref[...]

    return pl.pallas_call(
        body, name=name, grid=(n,), in_specs=[pl.BlockSpec((None, rows, cols), lambda k: (k, 0, 0))],
        out_specs=pl.BlockSpec((rows, cols), lambda k: (0, k)),
        out_shape=jax.ShapeDtypeStruct((rows, n * cols), stacked.dtype),
        compiler_params=_cparams(("parallel",)),
    )(stacked)


def _restack_cols(name, mat):
    rows, width = mat.shape
    cols = width // N_DEV

    def body(i_ref, o_ref):
        o_ref[...] = i_ref[...]

    return pl.pallas_call(
        body, name=name, grid=(N_DEV,), in_specs=[pl.BlockSpec((rows, cols), lambda k: (0, k))],
        out_specs=pl.BlockSpec((None, rows, cols), lambda k: (k, 0, 0)),
        out_shape=jax.ShapeDtypeStruct((N_DEV, rows, cols), mat.dtype),
        compiler_params=_cparams(("parallel",)),
    )(mat)


def _remap_pieces(runs):
    plan = {}
    for du, dc, su, sc, ln in runs:
        while ln > 0:
            lane = dc % LANE
            take = min(ln, LANE - lane)
            plan.setdefault((du, dc // LANE), []).append((su, sc, take, lane))
            dc, sc, ln = dc + take, sc + take, ln - take
    return plan


def _remap(name, srcs, src_units, runs, *, out_units, out_cols, out_dtype, tr=256):
    rows = srcs[0].shape[-2]
    tr = min(tr, rows)
    plan = _remap_pieces(runs)
    n_src = len(srcs)
    stacked_out = out_units is not None
    n_tiles = out_cols // LANE

    def body(*refs):
        o_ref = refs[n_src]

        def src_tile(unit, t):
            ai, lead = src_units[unit]
            ref = refs[ai]
            sl = slice(t * LANE, (t + 1) * LANE)
            return (ref[:, sl] if lead is None else ref[lead, :, sl]).astype(F32)

        lane = lax.broadcasted_iota(jnp.int32, (tr, LANE), 1)
        for du in range(out_units if stacked_out else 1):
            for t in range(n_tiles):
                acc = jnp.zeros((tr, LANE), F32)
                for su, sc, ln, dl in plan.get((du if stacked_out else None, t), []):
                    st, so = sc // LANE, sc % LANE
                    first = src_tile(su, st)
                    if so == dl and so + ln <= LANE:
                        piece = first
                    else:
                        second = src_tile(su, st + 1) if so + ln > LANE else first
                        both = jnp.concatenate([first, second], axis=1)
                        piece = pltpu.roll(both, (dl - so) % (2 * LANE), axis=1)[:, 0:LANE]
                    acc = piece if (dl == 0 and ln == LANE) else jnp.where(
                        jnp.logical_and(lane >= dl, lane < dl + ln), piece, acc)
                if stacked_out:
                    o_ref[du, :, t * LANE:(t + 1) * LANE] = acc.astype(o_ref.dtype)
                else:
                    o_ref[:, t * LANE:(t + 1) * LANE] = acc.astype(o_ref.dtype)

    in_specs = []
    for arr in srcs:
        if arr.ndim == 2:
            in_specs.append(pl.BlockSpec((tr, arr.shape[1]), lambda i: (i, 0)))
        else:
            in_specs.append(pl.BlockSpec((arr.shape[0], tr, arr.shape[2]), lambda i: (0, i, 0)))
    if stacked_out:
        out_spec = pl.BlockSpec((out_units, tr, out_cols), lambda i: (0, i, 0))
        out_shape = jax.ShapeDtypeStruct((out_units, rows, out_cols), out_dtype)
    else:
        out_spec = pl.BlockSpec((tr, out_cols), lambda i: (i, 0))
        out_shape = jax.ShapeDtypeStruct((rows, out_cols), out_dtype)
    return pl.pallas_call(
        body, name=name, grid=(rows // tr,), in_specs=in_specs, out_specs=out_spec, out_shape=out_shape,
        compiler_params=_cparams(("parallel",)),
    )(*srcs)


def _proj_col(c):
    if c < PROJ_GATE0:
        return c
    if c < PROJ_GATE0 + FOX_HEADS:
        return PROJ_F0 + (c - PROJ_GATE0)
    return c - FOX_HEADS


def _win_runs():
    cuts = sorted(set([0, PROJ_GATE0, PROJ_GATE0 + FOX_HEADS, IN_WIDTH] + [SHARD_IN * k for k in range(N_DEV + 1)]))
    return [(lo // SHARD_IN, lo % SHARD_IN, _proj_col(lo), hi - lo) for lo, hi in zip(cuts[:-1], cuts[1:])]


def _assemble_win(name, stacked):
    runs = [(None, pc, k, sc, ln) for k, sc, pc, ln in _win_runs()]
    return _remap(name, [stacked], [(0, k) for k in range(N_DEV)], runs,
                  out_units=None, out_cols=PROJ_WIDTH, out_dtype=BF16)


def _disassemble_dwin(name, dw):
    runs = [(k, sc, 0, pc, ln) for k, sc, pc, ln in _win_runs()]
    return _remap(name, [dw], [(0, None)], runs, out_units=N_DEV, out_cols=SHARD_IN_PAD, out_dtype=BF16)


def _concat_cols(name, parts, *, tr=512):
    rows = parts[0].shape[0]
    tr = min(tr, rows)
    widths = [p.shape[1] for p in parts]
    total = sum(widths)

    def body(*refs):
        o_ref = refs[len(parts)]
        lo = 0
        for r, w in zip(refs[:len(parts)], widths):
            o_ref[:, lo:lo + w] = r[...].astype(o_ref.dtype)
            lo += w

    return pl.pallas_call(
        body, name=name, grid=(rows // tr,),
        in_specs=[pl.BlockSpec((tr, w), lambda i: (i, 0)) for w in widths],
        out_specs=pl.BlockSpec((tr, total), lambda i: (i, 0)),
        out_shape=jax.ShapeDtypeStruct((rows, total), BF16),
        compiler_params=_cparams(("parallel",)),
    )(*parts)


FFN_BLK = FFN_HIDDEN // 2


def _ffn_col(c):
    half, r = divmod(c, FFN_HIDDEN)
    blk, r = divmod(r, FFN_BLK)
    return blk * 2 * FFN_BLK + half * FFN_BLK + r


def _assemble_wffn(name, stacked):
    runs = [(None, _ffn_col(SHARD_FFN * k), k, 0, SHARD_FFN) for k in range(N_DEV)]
    return _remap(name, [stacked], [(0, k) for k in range(N_DEV)], runs,
                  out_units=None, out_cols=2 * FFN_HIDDEN, out_dtype=BF16)


def _disassemble_dwffn(name, dw):
    runs = [(k, 0, 0, _ffn_col(SHARD_FFN * k), SHARD_FFN) for k in range(N_DEV)]
    return _remap(name, [dw], [(0, None)], runs, out_units=N_DEV, out_cols=SHARD_FFN_PAD, out_dtype=BF16)


def _ffn_in_swiglu(name, xn, w, *, tm=512):
    rows, k = xn.shape
    tm = min(tm, rows)
    nblk = FFN_HIDDEN // FFN_BLK

    def body(x_ref, w_ref, f_ref, g_ref):
        f = _bdot(x_ref[...], w_ref[...], _DIMS["nn"])
        f_ref[...] = f.astype(f_ref.dtype)
        fa = f[:, 0:FFN_BLK]
        g_ref[...] = (fa * _sigmoid(fa) * f[:, FFN_BLK:2 * FFN_BLK]).astype(g_ref.dtype)

    return pl.pallas_call(
        body, name=name, grid=(nblk, rows // tm),
        in_specs=[pl.BlockSpec((tm, k), lambda j, i: (i, 0)), pl.BlockSpec((k, 2 * FFN_BLK), lambda j, i: (0, j))],
        out_specs=(pl.BlockSpec((tm, 2 * FFN_BLK), lambda j, i: (i, j)), pl.BlockSpec((tm, FFN_BLK), lambda j, i: (i, j))),
        out_shape=(jax.ShapeDtypeStruct((rows, 2 * FFN_HIDDEN), BF16), jax.ShapeDtypeStruct((rows, FFN_HIDDEN), BF16)),
        compiler_params=_cparams(("parallel", "arbitrary")),
    )(xn, w)


def _d_ffn_out_swiglu(name, dh, w_out, f, *, tm=512):
    rows, d = dh.shape
    tm = min(tm, rows)
    nblk = FFN_HIDDEN // FFN_BLK

    def body(dh_ref, w_ref, f_ref, df_ref):
        dg = _bdot(dh_ref[...], w_ref[...], _DIMS["nt"])
        fa = f_ref[:, 0:FFN_BLK].astype(F32)
        fb = f_ref[:, FFN_BLK:2 * FFN_BLK].astype(F32)
        s = _sigmoid(fa)
        df_ref[:, 0:FFN_BLK] = (dg * fb * s * (1.0 + fa * (1.0 - s))).astype(df_ref.dtype)
        df_ref[:, FFN_BLK:2 * FFN_BLK] = (dg * fa * s).astype(df_ref.dtype)

    wide = pl.BlockSpec((tm, 2 * FFN_BLK), lambda j, i: (i, j))
    return pl.pallas_call(
        body, name=name, grid=(nblk, rows // tm),
        in_specs=[pl.BlockSpec((tm, d), lambda j, i: (i, 0)), pl.BlockSpec((FFN_BLK, d), lambda j, i: (j, 0)), wide],
        out_specs=wide, out_shape=jax.ShapeDtypeStruct((rows, 2 * FFN_HIDDEN), BF16),
        compiler_params=_cparams(("parallel", "arbitrary")),
    )(dh, w_out, f)


def _adamw(name, parts, w, m, v, *, tr=128):
    rows, cols = w.shape
    n_parts = parts.shape[0]
    tr = min(tr, rows)
    assert rows % tr == 0, (name, rows, tr)
    c1 = 1.0 - ADAM_B1 ** ADAM_STEP
    c2 = 1.0 - ADAM_B2 ** ADAM_STEP

    def body(p_ref, w_ref, m_ref, v_ref, g_ref, d_ref, nm_ref, nv_ref):
        g = p_ref[0].astype(F32)
        for s in range(1, n_parts):
            g = g + p_ref[s].astype(F32)
        m_new = ADAM_B1 * m_ref[...] + (1.0 - ADAM_B1) * g
        v_new = ADAM_B2 * v_ref[...] + (1.0 - ADAM_B2) * (g * g)
        upd = (m_new / c1) / (jnp.sqrt(v_new / c2) + ADAM_EPS) + ADAM_WD * w_ref[...]
        g_ref[...] = g
        d_ref[...] = -ADAM_LR * upd
        nm_ref[...] = m_new
        nv_ref[...] = v_new

    row = pl.BlockSpec((tr, cols), lambda i: (i, 0))
    out = jax.ShapeDtypeStruct((rows, cols), F32)
    return pl.pallas_call(
        body, name=name, grid=(rows // tr,),
        in_specs=[pl.BlockSpec((n_parts, tr, cols), lambda i: (0, i, 0)), row, row, row],
        out_specs=(row, row, row, row), out_shape=(out, out, out, out),
        compiler_params=_cparams(("parallel",)),
    )(parts, w, m, v)


_WEIGHTS = ("norm_mix", "w_in", "b_forget", "lam_re", "lam_im", "log_dt", "b_re", "b_im", "c_re", "c_im",
            "d_skip", "w_glu", "w_fox_o", "w_mix_out", "norm_mem_q", "norm_mem_kv", "w_mem_q", "w_mem_kv",
            "w_mem_o", "norm_ffn", "w_ffn_in", "w_ffn_out", "norm_final")
_SHARDED = ("w_in", "w_glu", "w_fox_o", "w_mix_out", "w_mem_q", "w_mem_kv", "w_mem_o", "w_ffn_in", "w_ffn_out")
_SMALL = tuple(n for n in _WEIGHTS if n not in _SHARDED)
_PACK_COLS = 1024


def _pack(arrays):
    flat = jnp.concatenate([a.reshape(-1).astype(F32) for a in arrays])
    rows = -(-flat.shape[0] // _PACK_COLS)
    return jnp.pad(flat, (0, rows * _PACK_COLS - flat.shape[0])).reshape(rows, _PACK_COLS)


def _unpack(buf, like):
    flat = buf.reshape(-1)
    out, pos = [], 0
    for a in like:
        out.append(flat[pos:pos + a.size].reshape(a.shape))
        pos += a.size
    return out


def _mm(name, a, b, mode, m, n, k, out_dtype, tm=1024, tn=512, tk=1024, **kw):
    return _matmul(name, a, b, mode, m, n, k, out_dtype=out_dtype, tm=tm, tn=tn, tk=tk, **kw)


def kernel(x, mem, norm_mix, w_in, b_forget, lam_re, lam_im, log_dt, b_re, b_im, c_re, c_im, d_skip, w_glu, w_fox_o, w_mix_out, norm_mem_q, norm_mem_kv, w_mem_q, w_mem_kv, w_mem_o, norm_ffn, w_ffn_in, w_ffn_out, norm_final, loss_target, m_norm_mix, m_w_in, m_b_forget, m_lam_re, m_lam_im, m_log_dt, m_b_re, m_b_im, m_c_re, m_c_im, m_d_skip, m_w_glu, m_w_fox_o, m_w_mix_out, m_norm_mem_q, m_norm_mem_kv, m_w_mem_q, m_w_mem_kv, m_w_mem_o, m_norm_ffn, m_w_ffn_in, m_w_ffn_out, m_norm_final, v_norm_mix, v_w_in, v_b_forget, v_lam_re, v_lam_im, v_log_dt, v_b_re, v_b_im, v_c_re, v_c_im, v_d_skip, v_w_glu, v_w_fox_o, v_w_mix_out, v_norm_mem_q, v_norm_mem_kv, v_w_mem_q, v_w_mem_kv, v_w_mem_o, v_norm_ffn, v_w_ffn_in, v_w_ffn_out, v_norm_final):
    given = dict(locals())
    weights = {n: given[n] for n in _WEIGHTS}
    mom_m = {n: given["m_" + n] for n in _WEIGHTS}
    mom_v = {n: given["v_" + n] for n in _WEIGHTS}
    seq = x.shape[1]
    nc = seq // SSM_CHUNK
    d = D_MODEL
    xs, mems, tgt = x[0], mem[0], loss_target[0]

    def padcols(a, width):
        return jnp.pad(a, ((0, 0), (0, width - a.shape[1])))

    shards = [padcols(w_in[0].astype(BF16), SHARD_IN_PAD), w_glu[0].astype(BF16), w_fox_o[0].astype(BF16),
              w_mix_out[0].astype(BF16), w_mem_q[0].astype(BF16), w_mem_kv[0].astype(BF16),
              w_mem_o[0].astype(BF16), padcols(w_ffn_in[0].astype(BF16), SHARD_FFN_PAD), w_ffn_out[0].astype(BF16)]
    win = _assemble_win("assemble_w_in", _gather_all("gather_w_in", shards[:1])[0])
    rest = shards[1:]
    gsend, grecv, rest_thru, lands, gtoken = _send_start(
        "gather_rest_start", rest, _place_own("place_weight_shards", rest, stacked_src=False, after=win), scatter=False)

    u = _rms_fwd("rms_mix", xs, norm_mix, after=gtoken)
    ussm = _mm("proj_ssm", u, win, "nn", seq, SSM_WIDTH, d, F32)
    qkv = _mm("proj_qkv", u, win, "nn", seq, 3 * FOX_WIDTH, d, BF16, tn=512, b_off=(0, SSM_WIDTH))
    gates = _mm("proj_gates", u, win, "nn", seq, 2 * d, d, BF16, tn=1024, b_off=(0, PROJ_GATE0))
    fproj = _mm("proj_forget", u, win, "nn", seq, LANE, d, F32, tn=LANE, b_off=(0, PROJ_F0))

    ssm_params = (lam_re[0], lam_im[0], log_dt[0], b_re[0], b_im[0], c_re[0], c_im[0])
    (m_c, bw_c, cm_c, a8, aseg), mats_vjp = jax.vjp(lambda *p: _ssm_mats(*p, nc), *ssm_params)
    m_b = _bd_expand("ssm_expand_m", _BD_M, m_c)
    bw_b = _bd_expand("ssm_expand_bw", _BD_BW, bw_c)
    cm_b = _bd_expand("ssm_expand_cm", _BD_CM, cm_c)
    u8 = ussm.reshape(nc, SSM_CHUNK * SSM_WIDTH)
    d8 = jnp.tile(d_skip, (1, SSM_CHUNK))
    w4 = _ssm_w("ssm_w", u8, bw_b)
    sp4 = _ssm_scan("ssm_scan", w4, a8, aseg, reverse=False)
    y8 = _ssm_y("ssm_y", u8, sp4, m_b, cm_b)
    act = _ssm_post_fwd("ssm_act", y8, u8, d8).reshape(seq, SSM_WIDTH)

    bcol = jnp.pad(b_forget[0], (0, LANE - FOX_HEADS)).reshape(LANE, 1)
    cum_t = _fox_cum("fox_cum", fproj, bcol).reshape(FOX_HEADS // 2, 2, seq)
    att, lse = _fox_fwd("fox_fwd", qkv, cum_t)

    gathered = _send_wait("gather_rest_wait", gsend, grecv, rest_thru, lands, att, scatter=False)
    wglu = _unstack_cols("unstack_w_glu", gathered[0])
    wfoxo = _unstack_cols("unstack_w_fox_o", gathered[1])
    wmix = gathered[2].reshape(d, d)
    wmq = gathered[3].reshape(d, MEM_WIDTH)
    wmkv = gathered[4].reshape(d, 2 * MEM_WIDTH)
    wmo = _unstack_cols("unstack_w_mem_o", gathered[5])
    wffn_in = _assemble_wffn("assemble_w_ffn_in", gathered[6])
    wffn_out = gathered[7].reshape(FFN_HIDDEN, d)

    glu = _mm("glu", act, wglu, "nn", seq, 2 * d, SSM_WIDTH, BF16, tn=1024)
    out_b = _mm("fox_out", att, wfoxo, "nn", seq, d, FOX_WIDTH, BF16, tn=1024)

    mixin = _mix_fwd("mix", glu, gates, out_b)
    h1 = _mm("mix_out", mixin, wmix, "nn", seq, d, d, F32, tn=1024, add=xs)

    n1 = _rms_fwd("rms_mem_q", h1, norm_mem_q)
    q2 = _mm("mem_q", n1, wmq, "nn", seq, MEM_WIDTH, d, BF16)
    mn = _rms_fwd("rms_mem_kv", mems, norm_mem_kv)
    mlen = mems.shape[0]
    kv = _mm("mem_kv", mn, wmkv, "nn", mlen, 2 * MEM_WIDTH, d, BF16)
    o2 = _mem_fwd("mem_attn", q2, kv)
    h2 = _mm("mem_out", o2, wmo, "nn", seq, d, MEM_WIDTH, F32, tn=1024, add=h1)

    n2 = _rms_fwd("rms_ffn", h2, norm_ffn)
    f, g_act = _ffn_in_swiglu("ffn_in_swiglu", n2, wffn_in)
    h3 = _mm("ffn_out", g_act, wffn_out, "nn", seq, d, FFN_HIDDEN, F32, tk=FFN_HIDDEN, add=h2)
    loss_part, dh3, dg_final = _final_loss("final_loss", h3, tgt, norm_final.reshape(1, d))

    df = _d_ffn_out_swiglu("d_ffn_out_swiglu", dh3, wffn_out, f)
    dwffn_out = _mm("d_ffn_out_w", g_act, dh3, "tn", FFN_HIDDEN, d, seq, BF16, tm=1408, tn=1024)
    dn2 = _mm("d_ffn_in_x", df, wffn_in, "nt", seq, d, 2 * FFN_HIDDEN, F32, tn=1024, tk=FFN_HIDDEN)
    dwffn_in = _mm("d_ffn_in_w", n2, df, "tn", d, 2 * FFN_HIDDEN, seq, BF16, tn=1408)
    dh2, dg_ffn = _rms_bwd("d_rms_ffn", dn2, h2, norm_ffn, res=dh3)

    do2 = _mm("d_mem_out_x", dh2, wmo, "nt", seq, MEM_WIDTH, d, F32)
    dwmo = _restack_cols("restack_d_w_mem_o", _mm("d_mem_out_w", o2, dh2, "tn", MEM_WIDTH, d, seq, BF16, tn=1024))
    dq2, dkv = _mem_bwd("d_mem_attn", q2, kv, do2)
    dwmq = _mm("d_mem_q_w", n1, dq2, "tn", d, MEM_WIDTH, seq, BF16)
    dn1 = _mm("d_mem_q_x", dq2, wmq, "nt", seq, d, MEM_WIDTH, F32)
    dwmkv = _mm("d_mem_kv_w", mn, dkv, "tn", d, 2 * MEM_WIDTH, mlen, BF16, tn=1024)
    dmn = _mm("d_mem_kv_x", dkv, wmkv, "nt", mlen, d, 2 * MEM_WIDTH, F32)
    _, dg_memkv = _rms_bwd("d_rms_mem_kv", dmn, mems, norm_mem_kv)

    early = [dwmq.reshape(N_DEV, d // N_DEV, MEM_WIDTH), dwmkv.reshape(N_DEV, d // N_DEV, 2 * MEM_WIDTH), dwmo,
             _disassemble_dwffn("split_d_w_ffn_in", dwffn_in), dwffn_out.reshape(N_DEV, FFN_HIDDEN // N_DEV, d)]
    ssend, srecv, early_thru, early_lands, stoken = _send_start(
        "scatter_early_start", early, _place_own("place_early_grads", early, stacked_src=True), scatter=True)
    dh1, dg_memq = _rms_bwd("d_rms_mem_q", dn1, h1, norm_mem_q, res=dh2, after=stoken)

    dmixin = _mm("d_mix_out_x", dh1, wmix, "nt", seq, d, d, F32, tn=1024)
    dwmix = _mm("d_mix_out_w", mixin, dh1, "tn", d, d, seq, BF16, tn=1024)
    dglu, dgates, dout_b = _mix_bwd("d_mix", dmixin, glu, gates, out_b)
    datt = _mm("d_fox_out_x", dout_b, wfoxo, "nt", seq, FOX_WIDTH, d, F32)
    dwfoxo = _restack_cols("restack_d_w_fox_o", _mm("d_fox_out_w", att, dout_b, "tn", FOX_WIDTH, d, seq, BF16, tn=1024))
    dact = _mm("d_glu_x", dglu, wglu, "nt", seq, SSM_WIDTH, 2 * d, F32, tk=2 * d)
    dwglu = _restack_cols("restack_d_w_glu", _mm("d_glu_w", act, dglu, "tn", SSM_WIDTH, 2 * d, seq, BF16, tn=2 * d))

    mid = [dwglu, dwfoxo, dwmix.reshape(N_DEV, d // N_DEV, d)]
    msend, mrecv, mid_thru, mid_lands, mtoken = _send_start(
        "scatter_mid_start", mid, _place_own("place_mid_grads", mid, stacked_src=True), scatter=True)

    dz8, dg_dskip = _ssm_post_bwd("d_ssm_act", dact.reshape(nc, SSM_CHUNK * SSM_WIDTH), y8, u8, d8, after=mtoken)
    ds4, dcm = _ssm_ds("d_ssm_y_state", dz8, sp4, cm_b)
    g4, da8 = _ssm_scan("d_ssm_scan", ds4, a8, aseg, reverse=True, sprev4=sp4)
    dx8, dm, dbw = _ssm_dx("d_ssm_x", dz8, g4, u8, m_b, bw_b, d8)
    dussm = dx8.reshape(seq, SSM_WIDTH)
    g_ssm = mats_vjp((_bd_reduce("ssm_reduce_dm", _BD_M, dm), _bd_reduce("ssm_reduce_dbw", _BD_BW, dbw),
                      _bd_reduce("ssm_reduce_dcm", _BD_CM, dcm), da8, jnp.zeros_like(aseg)))

    dq, dk, dv, dcum = _fox_bwd("d_fox", qkv, cum_t, att, datt, lse)
    dfproj, dbf = _fox_cum_bwd("d_fox_cum", dcum.reshape(FOX_HEADS, seq), fproj, bcol)
    dg_bforget = dbf[0:FOX_HEADS, 0].reshape(1, FOX_HEADS)

    dproj = _concat_cols("d_proj_concat", (dussm, dq, dk, dv, dgates, dfproj))
    dwin = _mm("d_proj_w", u, dproj, "tn", d, PROJ_WIDTH, seq, BF16, tn=1408)
    late = [_disassemble_dwin("split_d_w_in", dwin)]
    lsend, lrecv, late_thru, late_lands, ltoken = _send_start(
        "scatter_late_start", late, _place_own("place_late_grads", late, stacked_src=True), scatter=True)
    du = _mm("d_proj_x", dproj, win, "nt", seq, d, PROJ_WIDTH, F32, tn=1024, tk=1408)
    dx, dg_mix = _rms_bwd("d_rms_mix", du, xs, norm_mix, res=dh1, after=ltoken)

    early_parts = _send_wait("scatter_early_wait", ssend, srecv, early_thru, early_lands, dx, scatter=True)
    mid_parts = _send_wait("scatter_mid_wait", msend, mrecv, mid_thru, mid_lands, dx, scatter=True)
    received = dict(zip(("w_glu", "w_fox_o", "w_mix_out"), mid_parts))
    received.update(zip(("w_mem_q", "w_mem_kv", "w_mem_o", "w_ffn_in", "w_ffn_out"), early_parts))

    small_grads = dict(zip(
        _SMALL, (dg_mix, dg_bforget, g_ssm[0][None], g_ssm[1][None], g_ssm[2][None], g_ssm[3][None], g_ssm[4][None],
                 g_ssm[5][None], g_ssm[6][None], dg_dskip, dg_memq, dg_memkv, dg_ffn, dg_final.reshape(d))))
    small_like = [weights[n] for n in _SMALL]
    packed = [_pack([small_grads[n] for n in _SMALL])]
    psend, precv, packed_thru, packed_lands, ptoken = _send_start(
        "gather_small_start", packed, _place_own("place_small_grads", packed, stacked_src=False), scatter=False)

    results = [{}, {}, {}, {}]
    tiles = {"w_in": 128, "w_glu": 128, "w_fox_o": 128, "w_mix_out": 128, "w_mem_q": 128, "w_mem_kv": 128,
             "w_mem_o": 128, "w_ffn_in": 128, "w_ffn_out": 176}
    pads = {"w_in": SHARD_IN_PAD, "w_ffn_in": SHARD_FFN_PAD}
    outs = (ptoken,)
    for name in _SHARDED[1:] + _SHARDED[:1]:
        if name == "w_in":
            received[name] = _send_wait("scatter_late_wait", lsend, lrecv, late_thru, late_lands, outs[0],
                                        scatter=True)[0]
        parts = received[name]
        w2, m2, v2 = weights[name][0], mom_m[name][0], mom_v[name][0]
        cols = w2.shape[1]
        if name in pads:
            w2, m2, v2 = (padcols(t, pads[name]) for t in (w2, m2, v2))
        outs = _adamw("adamw_" + name, parts, w2, m2, v2, tr=tiles[name])
        for res, o in zip(results, outs):
            res[name] = o[:, :cols][None]

    small_all = _send_wait("gather_small_wait", psend, precv, packed_thru, packed_lands, outs[0], scatter=False)[0]
    pk = [_pack([src[n] for n in _SMALL]) for src in (weights, mom_m, mom_v)]
    small_out = _adamw("adamw_small", small_all, pk[0], pk[1], pk[2], tr=small_all.shape[1])
    for res, buf in zip(results, small_out):
        res.update(zip(_SMALL, _unpack(buf, small_like)))

    loss = lax.psum(loss_part[0, 0], ("x", "y", "c"))
    out = [loss, dx[None]]
    for res in results:
        out.extend(res[n] for n in _WEIGHTS)
    return tuple(out)
```

```python
import math

import jax
import jax.numpy as jnp
import numpy as np
from jax import lax
from jax.experimental import pallas as pl
from jax.experimental.pallas import tpu as pltpu

F32 = jnp.float32
BF16 = jnp.bfloat16

N_DEV = 8
LANE = 128
VMEM_LIMIT = 56 * 1024 * 1024

D_MODEL = 1024
SSM_GROUP = 16
SSM_GROUPS = 32
SSM_WIDTH = 512
SSM_STATE = 64
SSM_CHUNK = 8
FOX_HEADS = 8
FOX_HEAD_DIM = 64
FOX_WIDTH = 512
MEM_HEADS = 4
MEM_HEAD_DIM = 128
MEM_WIDTH = 512
FFN_HIDDEN = 2816
RMS_EPS = 1e-6
IN_WIDTH = 4104
SHARD_IN = IN_WIDTH // N_DEV
SHARD_IN_PAD = 640
SHARD_FFN = 2 * FFN_HIDDEN // N_DEV
SHARD_FFN_PAD = 768
PROJ_GATE0 = 2048
PROJ_F0 = 4096
PROJ_WIDTH = 4224

ADAM_LR = 0.001
ADAM_B1 = 0.9
ADAM_B2 = 0.999
ADAM_EPS = 1e-08
ADAM_WD = 0.01
ADAM_STEP = 10


def _cparams(sem=None):
    return pltpu.CompilerParams(dimension_semantics=sem, vmem_limit_bytes=VMEM_LIMIT)


def _sigmoid(x):
    return 1.0 / (1.0 + jnp.exp(-x))


def _bdot(a, b, dims):
    return lax.dot_general(a.astype(BF16), b.astype(BF16), ((dims[0], dims[1]), ((), ())),
                           preferred_element_type=F32)


_DIMS = {"nn": ((1,), (0,)), "nt": ((1,), (1,)), "tn": ((0,), (0,))}


def _matmul(name, a, b, mode, m, n, k, *, out_dtype, tm, tn, tk, a_off=(0, 0), b_off=(0, 0), add=None):
    tm, tn, tk = min(tm, m), min(tn, n), min(tk, k)
    assert m % tm == 0 and n % tn == 0 and k % tk == 0, (name, m, n, k, tm, tn, tk)
    nk = k // tk
    grid = (m // tm, n // tn, nk)

    def blk(off, t):
        assert off % t == 0, (name, off, t)
        return off // t

    if mode in ("nn", "nt"):
        ar, ac = blk(a_off[0], tm), blk(a_off[1], tk)
        a_spec = pl.BlockSpec((tm, tk), lambda i, j, kk: (i + ar, kk + ac))
    else:
        ar, ac = blk(a_off[0], tk), blk(a_off[1], tm)
        a_spec = pl.BlockSpec((tk, tm), lambda i, j, kk: (kk + ar, i + ac))

    if mode in ("nn", "tn"):
        br, bc = blk(b_off[0], tk), blk(b_off[1], tn)
        b_spec = pl.BlockSpec((tk, tn), lambda i, j, kk: (kk + br, j + bc))
    else:
        br, bc = blk(b_off[0], tn), blk(b_off[1], tk)
        b_spec = pl.BlockSpec((tn, tk), lambda i, j, kk: (j + br, kk + bc))
    o_spec = pl.BlockSpec((tm, tn), lambda i, j, kk: (i, j))
    out_shape = jax.ShapeDtypeStruct((m, n), out_dtype)

    in_specs = [a_spec, b_spec]
    operands = [a, b]
    if add is not None:
        in_specs.append(pl.BlockSpec((tm, tn), lambda i, j, kk: (i, j)))
        operands.append(add)
    dims = _DIMS[mode]
    has_add = add is not None

    def body(*refs):
        a_ref, b_ref = refs[0], refs[1]
        add_ref = refs[2] if has_add else None
        o_ref = refs[3] if has_add else refs[2]
        acc_ref = refs[-1] if nk > 1 else None
        prod = _bdot(a_ref[...], b_ref[...], dims)

        def finish(total):
            if has_add:
                total = total + add_ref[...].astype(F32)
            o_ref[...] = total.astype(o_ref.dtype)

        if nk == 1:
            finish(prod)
        else:
            kk = pl.program_id(2)

            @pl.when(kk == 0)
            def _():
                acc_ref[...] = prod

            @pl.when(jnp.logical_and(kk > 0, kk < nk - 1))
            def _():
                acc_ref[...] += prod

            @pl.when(kk == nk - 1)
            def _():
                finish(acc_ref[...] + prod)

    scratch = [pltpu.VMEM((tm, tn), F32)] if nk > 1 else []
    return pl.pallas_call(
        body, name=name, grid=grid, in_specs=in_specs, out_specs=o_spec, out_shape=out_shape,
        scratch_shapes=scratch,
        compiler_params=_cparams(("parallel", "parallel", "arbitrary")),
    )(*operands)


def _rms_fwd(name, x, gain, *, tr=512, after=None):
    r, d = x.shape
    tr = min(tr, r)

    def body(x_ref, g_ref, *rest):
        o_ref = rest[-1]
        xv = x_ref[...]
        rstd = lax.rsqrt(jnp.mean(xv * xv, axis=-1, keepdims=True) + RMS_EPS)
        o_ref[...] = (xv * rstd * g_ref[...]).astype(o_ref.dtype)

    in_specs = [pl.BlockSpec((tr, d), lambda i: (i, 0)), pl.BlockSpec((1, d), lambda i: (0, 0))]
    ops = [x, gain]
    if after is not None:
        in_specs.append(pl.BlockSpec(after.shape, lambda i: (0, 0)))
        ops.append(after)
    return pl.pallas_call(
        body, name=name, grid=(r // tr,), in_specs=in_specs,
        out_specs=pl.BlockSpec((tr, d), lambda i: (i, 0)),
        out_shape=jax.ShapeDtypeStruct((r, d), BF16),
        compiler_params=_cparams(("parallel",)),
    )(*ops)


def _rms_bwd(name, dy, x, gain, res=None, *, tr=512, after=None):
    r, d = x.shape
    tr = min(tr, r)
    n = r // tr
    has_res = res is not None

    def body(*refs):
        dy_ref, x_ref, g_ref = refs[:3]
        res_ref = refs[3] if has_res else None
        dx_ref, dg_ref, acc_ref = refs[-3:]
        i = pl.program_id(0)
        xv = x_ref[...]
        rstd = lax.rsqrt(jnp.mean(xv * xv, axis=-1, keepdims=True) + RMS_EPS)
        xh = xv * rstd
        dyv = dy_ref[...].astype(F32)
        dxh = dyv * g_ref[...]
        dx = rstd * (dxh - xh * jnp.mean(dxh * xh, axis=-1, keepdims=True))
        if has_res:
            dx = dx + res_ref[...]
        dx_ref[...] = dx
        part = (dyv * xh).reshape(tr // 8, 8, d).sum(axis=0)

        @pl.when(i == 0)
        def _():
            acc_ref[...] = part

        @pl.when(i > 0)
        def _():
            acc_ref[...] += part

        @pl.when(i == n - 1)
        def _():
            dg_ref[...] = jnp.sum(acc_ref[...], axis=0, keepdims=True)

    row = pl.BlockSpec((tr, d), lambda i: (i, 0))
    in_specs = [row, row, pl.BlockSpec((1, d), lambda i: (0, 0))] + ([row] if has_res else [])
    ops = [dy, x, gain] + ([res] if has_res else [])
    if after is not None:
        in_specs.append(pl.BlockSpec(after.shape, lambda i: (0, 0)))
        ops.append(after)
    return pl.pallas_call(
        body, name=name, grid=(n,), in_specs=in_specs,
        out_specs=(row, pl.BlockSpec((1, d), lambda i: (0, 0))),
        out_shape=(jax.ShapeDtypeStruct((r, d), F32), jax.ShapeDtypeStruct((1, d), F32)),
        scratch_shapes=[pltpu.VMEM((8, d), F32)],
        compiler_params=_cparams(("arbitrary",)),
    )(*ops)


def _final_loss(name, h, target, gain, *, tr=512):
    r, d = h.shape
    tr = min(tr, r)
    n = r // tr

    def body(h_ref, t_ref, g_ref, loss_ref, dh_ref, dg_ref, accl_ref, accg_ref):
        i = pl.program_id(0)
        xv = h_ref[...]
        rstd = lax.rsqrt(jnp.mean(xv * xv, axis=-1, keepdims=True) + RMS_EPS)
        xh = xv * rstd
        e = xh * g_ref[...] - t_ref[...]
        dyv = e * (1.0 / d)
        dxh = dyv * g_ref[...]
        dh_ref[...] = rstd * (dxh - xh * jnp.mean(dxh * xh, axis=-1, keepdims=True))
        lpart = (e * e).reshape(tr // 8, 8, d).sum(axis=0)
        gpart = (dyv * xh).reshape(tr // 8, 8, d).sum(axis=0)

        @pl.when(i == 0)
        def _():
            accl_ref[...] = lpart
            accg_ref[...] = gpart

        @pl.when(i > 0)
        def _():
            accl_ref[...] += lpart
            accg_ref[...] += gpart

        @pl.when(i == n - 1)
        def _():
            tot = jnp.sum(jnp.sum(accl_ref[...], axis=0, keepdims=True), axis=1, keepdims=True)
            loss_ref[...] = jnp.broadcast_to(tot * (0.5 / d), (1, LANE))
            dg_ref[...] = jnp.sum(accg_ref[...], axis=0, keepdims=True)

    row = pl.BlockSpec((tr, d), lambda i: (i, 0))
    one = pl.BlockSpec((1, d), lambda i: (0, 0))
    return pl.pallas_call(
        body, name=name, grid=(n,), in_specs=[row, row, one],
        out_specs=(pl.BlockSpec((1, LANE), lambda i: (0, 0)), row, one),
        out_shape=(jax.ShapeDtypeStruct((1, LANE), F32), jax.ShapeDtypeStruct((r, d), F32),
                   jax.ShapeDtypeStruct((1, d), F32)),
        scratch_shapes=[pltpu.VMEM((8, d), F32), pltpu.VMEM((8, d), F32)],
        compiler_params=_cparams(("arbitrary",)),
    )(h, target, gain)


_GELU_C = math.sqrt(2.0 / math.pi)


def _gelu_parts(z):
    inner = _GELU_C * (z + 0.044715 * z * z * z)
    t = jnp.tanh(inner)
    val = 0.5 * z * (1.0 + t)
    dinner = _GELU_C * (1.0 + 3.0 * 0.044715 * z * z)
    grad = 0.5 * (1.0 + t) + 0.5 * z * (1.0 - t * t) * dinner
    return val, grad


def _ssm_post_fwd(name, y8, u8, d8, *, tr=256):
    r, c = y8.shape
    tr = min(tr, r)

    def body(y_ref, u_ref, d_ref, o_ref):
        z = y_ref[...] + d_ref[...] * u_ref[...]
        o_ref[...] = _gelu_parts(z)[0].astype(o_ref.dtype)

    row = pl.BlockSpec((tr, c), lambda i: (i, 0))
    return pl.pallas_call(
        body, name=name, grid=(r // tr,), in_specs=[row, row, pl.BlockSpec((1, c), lambda i: (0, 0))],
        out_specs=row, out_shape=jax.ShapeDtypeStruct((r, c), BF16),
        compiler_params=_cparams(("parallel",)),
    )(y8, u8, d8)


def _ssm_post_bwd(name, dact8, y8, u8, d8, *, tr=256, after=None):
    r, c = y8.shape
    tr = min(tr, r)
    n = r // tr

    def body(*refs):
        da_ref, y_ref, u_ref, d_ref = refs[:4]
        dz_ref, dd_ref, acc_ref = refs[-3:]
        i = pl.program_id(0)
        uv = u_ref[...]
        z = y_ref[...] + d_ref[...] * uv
        dz = da_ref[...].astype(F32) * _gelu_parts(z)[1]
        dz_ref[...] = dz
        part = (dz * uv).reshape(tr // 8, 8, c).sum(axis=0)

        @pl.when(i == 0)
        def _():
            acc_ref[...] = part

        @pl.when(i > 0)
        def _():
            acc_ref[...] += part

        @pl.when(i == n - 1)
        def _():
            tot = jnp.sum(acc_ref[...], axis=0, keepdims=True)
            out = tot[:, 0:SSM_WIDTH]
            for j in range(1, c // SSM_WIDTH):
                out = out + tot[:, j * SSM_WIDTH:(j + 1) * SSM_WIDTH]
            dd_ref[...] = out

    row = pl.BlockSpec((tr, c), lambda i: (i, 0))
    in_specs = [row, row, row, pl.BlockSpec((1, c), lambda i: (0, 0))]
    ops = [dact8, y8, u8, d8]
    if after is not None:
        in_specs.append(pl.BlockSpec(memory_space=pl.ANY))
        ops.append(after)
    return pl.pallas_call(
        body, name=name, grid=(n,), in_specs=in_specs,
        out_specs=(row, pl.BlockSpec((1, SSM_WIDTH), lambda i: (0, 0))),
        out_shape=(jax.ShapeDtypeStruct((r, c), F32), jax.ShapeDtypeStruct((1, SSM_WIDTH), F32)),
        scratch_shapes=[pltpu.VMEM((8, c), F32)],
        compiler_params=_cparams(("arbitrary",)),
    )(*ops)


def _mix_fwd(name, glu, gates, out_b, *, tr=256):
    r = glu.shape[0]
    d = D_MODEL
    tr = min(tr, r)

    def body(glu_ref, gate_ref, ob_ref, o_ref):
        out_a = glu_ref[:, 0:d].astype(F32) * _sigmoid(glu_ref[:, d:2 * d].astype(F32))
        mix = (_sigmoid(gate_ref[:, 0:d].astype(F32)) * out_a
               + _sigmoid(gate_ref[:, d:2 * d].astype(F32)) * ob_ref[...].astype(F32))
        o_ref[...] = mix.astype(o_ref.dtype)

    wide = pl.BlockSpec((tr, 2 * d), lambda i: (i, 0))
    row = pl.BlockSpec((tr, d), lambda i: (i, 0))
    return pl.pallas_call(
        body, name=name, grid=(r // tr,), in_specs=[wide, wide, row], out_specs=row,
        out_shape=jax.ShapeDtypeStruct((r, d), BF16), compiler_params=_cparams(("parallel",)),
    )(glu, gates, out_b)


def _mix_bwd(name, dmix, glu, gates, out_b, *, tr=256):
    r = glu.shape[0]
    d = D_MODEL
    tr = min(tr, r)

    def body(dm_ref, glu_ref, gate_ref, ob_ref, dglu_ref, dgate_ref, dob_ref):
        dm = dm_ref[...]
        glu_a = glu_ref[:, 0:d].astype(F32)
        sb = _sigmoid(glu_ref[:, d:2 * d].astype(F32))
        ga = _sigmoid(gate_ref[:, 0:d].astype(F32))
        gb = _sigmoid(gate_ref[:, d:2 * d].astype(F32))
        out_a = glu_a * sb
        dout_a = dm * ga
        dglu_ref[:, 0:d] = (dout_a * sb).astype(dglu_ref.dtype)
        dglu_ref[:, d:2 * d] = (dout_a * glu_a * sb * (1.0 - sb)).astype(dglu_ref.dtype)
        dgate_ref[:, 0:d] = (dm * out_a * ga * (1.0 - ga)).astype(dgate_ref.dtype)
        dgate_ref[:, d:2 * d] = (dm * ob_ref[...].astype(F32) * gb * (1.0 - gb)).astype(dgate_ref.dtype)
        dob_ref[...] = (dm * gb).astype(dob_ref.dtype)

    wide = pl.BlockSpec((tr, 2 * d), lambda i: (i, 0))
    row = pl.BlockSpec((tr, d), lambda i: (i, 0))
    return pl.pallas_call(
        body, name=name, grid=(r // tr,), in_specs=[row, wide, wide, row], out_specs=(wide, wide, row),
        out_shape=(jax.ShapeDtypeStruct((r, 2 * d), BF16), jax.ShapeDtypeStruct((r, 2 * d), BF16),
                   jax.ShapeDtypeStruct((r, d), BF16)),
        compiler_params=_cparams(("parallel",)),
    )(dmix, glu, gates, out_b)


def _ssm_mats(lam_re, lam_im, log_dt, b_re, b_im, c_re, c_im, nc):
    hp = lax.Precision.HIGHEST
    t = SSM_CHUNK
    nq = SSM_GROUPS // 8
    lam = lax.complex(lam_re, lam_im)
    z = lam * jnp.exp(log_dt)[:, None]
    ks = jnp.arange(t + 1, dtype=F32)
    apow = jnp.exp(ks[:, None, None] * z[None])
    bbar = ((apow[1] - 1.0) / lam)[..., None] * lax.complex(b_re, b_im)
    c = lax.complex(c_re, c_im)

    ca = c[None] * apow[:, :, None, :]
    kmat = jnp.einsum("kgnp,gpm->kgnm", ca, bbar, precision=hp).real
    ii = np.arange(t)
    lag = ii[None, :] - ii[:, None]
    kt = kmat[np.clip(lag, 0, t)] * jnp.asarray(lag >= 0, F32)[:, :, None, None, None]
    kt = kt.reshape(t, t, nq, 8, SSM_GROUP, SSM_GROUP)
    m_c = kt.transpose(2, 0, 3, 5, 1, 4).reshape(nq, 1024, LANE)

    arev = jnp.exp((float(t - 1) - ks[:t])[:, None, None] * z[None])
    w = arev[:, :, :, None] * bbar[None]
    wr = jnp.stack([w.real, w.imag]).reshape(2, t, nq, 8, SSM_STATE, SSM_GROUP)
    bw_c = wr.transpose(2, 1, 3, 5, 0, 4).reshape(nq, 1024, LANE)

    ca1 = ca[1:]
    cr = jnp.stack([ca1.real, -ca1.imag]).reshape(2, t, nq, 8, SSM_GROUP, SSM_STATE)
    cm_c = cr.transpose(2, 0, 3, 5, 1, 4).reshape(nq, 1024, LANE)

    def tiles(v):
        vq = jnp.concatenate([v.real.reshape(nq, 512), v.imag.reshape(nq, 512)], axis=1)
        return jnp.broadcast_to(vq.reshape(nq, 8, 1, LANE), (nq, 8, 8, LANE))

    return m_c, bw_c, cm_c, tiles(apow[t]), tiles(jnp.exp(float(nc) * z))


_BD_M = (LANE, SSM_GROUP)
_BD_BW = (LANE, SSM_STATE)
_BD_CM = (512, SSM_GROUP)


def _bd_perm(cn):
    rr = lax.broadcasted_iota(jnp.int32, (1024, 1024), 0)
    cc = lax.broadcasted_iota(jnp.int32, (1024, 1024), 1)
    sh = cn.bit_length() - 1
    src = ((rr >> 7) << sh) + (((rr & (LANE - 1)) >> sh) << (3 + sh)) + (rr & (cn - 1))
    return jnp.where(src == cc, 1.0, 0.0).astype(BF16)


def _bd_rowgroup(span):
    r = lax.broadcasted_iota(jnp.int32, (1024, LANE), 0)
    return (r & (span - 1)) >> ((span // 8).bit_length() - 1)


def _bd_expand(name, kind, compact):
    span, cn = kind
    nq = compact.shape[0]

    def body(c_ref, o_ref, perm_scr):
        @pl.when(pl.program_id(0) == 0)
        def _():
            perm_scr[...] = _bd_perm(cn)

        x = c_ref[...]
        grp = _bd_rowgroup(span)
        xcat = jnp.concatenate([jnp.where(grp == h, x, 0.0) for h in range(8)], axis=1)
        o_ref[...] = _bdot(xcat, perm_scr[...], _DIMS["nn"]).astype(o_ref.dtype)

    return pl.pallas_call(
        body, name=name, grid=(nq,), in_specs=[pl.BlockSpec((None, 1024, LANE), lambda q: (q, 0, 0))],
        out_specs=pl.BlockSpec((None, 1024, 1024), lambda q: (q, 0, 0)),
        out_shape=jax.ShapeDtypeStruct((nq, 1024, 1024), BF16),
        scratch_shapes=[pltpu.VMEM((1024, 1024), BF16)],
        compiler_params=_cparams(("arbitrary",)),
    )(compact)


def _bd_reduce(name, kind, dbig):
    span, cn = kind
    nq = dbig.shape[0]

    def body(g_ref, o_ref, perm_scr):
        @pl.when(pl.program_id(0) == 0)
        def _():
            perm_scr[...] = _bd_perm(cn)

        back = _bdot(g_ref[...], perm_scr[...], _DIMS["nt"])
        grp = _bd_rowgroup(span)
        out = jnp.zeros((1024, LANE), F32)
        for h in range(8):
            out = jnp.where(grp == h, back[:, h * LANE:(h + 1) * LANE], out)
        o_ref[...] = out

    return pl.pallas_call(
        body, name=name, grid=(nq,), in_specs=[pl.BlockSpec((None, 1024, 1024), lambda q: (q, 0, 0))],
        out_specs=pl.BlockSpec((None, 1024, LANE), lambda q: (q, 0, 0)),
        out_shape=jax.ShapeDtypeStruct((nq, 1024, LANE), F32),
        scratch_shapes=[pltpu.VMEM((1024, 1024), BF16)],
        compiler_params=_cparams(("arbitrary",)),
    )(dbig)


def _x_tile_specs(nc, nq):
    return [pl.BlockSpec((nc, LANE), lambda q, t, i=i: (0, i * nq + q)) for i in range(SSM_CHUNK)]


def _cat_tiles(refs):
    return jnp.concatenate([r[...] for r in refs], axis=1)


def _ssm_w(name, x8, bw):
    nc = x8.shape[0]
    nq = bw.shape[0]

    def body(*refs):
        xq = _cat_tiles(refs[:8])
        refs[9][...] = _bdot(xq, refs[8][...], _DIMS["nn"])

    return pl.pallas_call(
        body, name=name, grid=(nq, 8),
        in_specs=_x_tile_specs(nc, nq) + [pl.BlockSpec((None, 1024, LANE), lambda q, t: (q, 0, t))],
        out_specs=pl.BlockSpec((None, None, nc, LANE), lambda q, t: (q, t, 0, 0)),
        out_shape=jax.ShapeDtypeStruct((nq, 8, nc, LANE), F32),
        compiler_params=_cparams(("parallel", "arbitrary")),
    )(*([x8] * 8), bw)


def _ssm_scan(name, w4, a_t, aseg_t, *, reverse, sprev4=None):
    nq, _, nc, _ = w4.shape
    ns = nc // 8
    with_da = sprev4 is not None

    def body(*refs):
        w_ref, a_ref, aseg_ref = refs[:3]
        s_ref = refs[3] if with_da else None
        o_ref = refs[4] if with_da else refs[3]
        da_ref = refs[5] if with_da else None
        sgn = -1.0 if reverse else 1.0
        ar = [a_ref[j] for j in range(4)]
        ai = [sgn * a_ref[j + 4] for j in range(4)]
        gr = [aseg_ref[j] for j in range(4)]
        gi = [sgn * aseg_ref[j + 4] for j in range(4)]
        zero = tuple(jnp.zeros((8, LANE), F32) for _ in range(8))

        def rows(tt):
            return pl.ds((ns - 1 - tt) if reverse else tt, 8, stride=ns)

        def step(carry, w):
            new_r = [ar[j] * carry[j] - ai[j] * carry[j + 4] + w[j] for j in range(4)]
            new_i = [ar[j] * carry[j + 4] + ai[j] * carry[j] + w[j + 4] for j in range(4)]
            return tuple(new_r + new_i)

        def pass1(tt, carry):
            return step(carry, [w_ref[j, rows(tt), :] for j in range(8)])

        ends = lax.fori_loop(0, ns, pass1, zero)
        sub = lax.broadcasted_iota(jnp.int32, (8, LANE), 0)
        init = list(zero)
        order = range(7, 0, -1) if reverse else range(0, 7)
        for s in order:
            nxt = s - 1 if reverse else s + 1
            cand_r = [gr[j] * init[j] - gi[j] * init[j + 4] + ends[j] for j in range(4)]
            cand_i = [gr[j] * init[j + 4] + gi[j] * init[j] + ends[j + 4] for j in range(4)]
            cand = cand_r + cand_i
            shift = 7 if reverse else 1
            init = [jnp.where(sub == nxt, pltpu.roll(cand[j], shift, axis=0), init[j]) for j in range(8)]

        def pass2(tt, state):
            carry, acc = state
            r = rows(tt)
            for j in range(8):
                o_ref[j, r, :] = carry[j]
            if with_da:
                sp = [s_ref[j, r, :] for j in range(8)]
                acc_r = [acc[j] + carry[j] * sp[j] + carry[j + 4] * sp[j + 4] for j in range(4)]
                acc_i = [acc[j + 4] + carry[j + 4] * sp[j] - carry[j] * sp[j + 4] for j in range(4)]
                acc = tuple(acc_r + acc_i)
            return step(carry, [w_ref[j, r, :] for j in range(8)]), acc

        _, acc = lax.fori_loop(0, ns, pass2, (tuple(init), zero))
        if with_da:
            for j in range(8):
                da_ref[j] = acc[j]

    big = pl.BlockSpec((None, 8, nc, LANE), lambda q: (q, 0, 0, 0))
    small = pl.BlockSpec((None, 8, 8, LANE), lambda q: (q, 0, 0, 0))
    in_specs = [big, small, small] + ([big] if with_da else [])
    ops = [w4, a_t, aseg_t] + ([sprev4] if with_da else [])
    out_specs = (big, small) if with_da else big
    big_s = jax.ShapeDtypeStruct((nq, 8, nc, LANE), F32)
    out_shape = (big_s, jax.ShapeDtypeStruct((nq, 8, 8, LANE), F32)) if with_da else big_s
    return pl.pallas_call(
        body, name=name, grid=(nq,), in_specs=in_specs, out_specs=out_specs, out_shape=out_shape,
        compiler_params=_cparams(("parallel",)),
    )(*ops)


def _ssm_y(name, x8, sprev4, m_mat, cm_mat):
    nc = x8.shape[0]
    nq = m_mat.shape[0]

    def body(*refs):
        xq = _cat_tiles(refs[:8])
        s_ref, m_ref, cm_ref, o_ref = refs[8:12]
        sq = jnp.concatenate([s_ref[t] for t in range(8)], axis=1)
        o_ref[...] = _bdot(xq, m_ref[...], _DIMS["nn"]) + _bdot(sq, cm_ref[...], _DIMS["nn"])

    col = pl.BlockSpec((None, 1024, LANE), lambda q, j: (q, 0, j))
    return pl.pallas_call(
        body, name=name, grid=(nq, 8),
        in_specs=_x_tile_specs(nc, nq) + [pl.BlockSpec((None, 8, nc, LANE), lambda q, j: (q, 0, 0, 0)), col, col],
        out_specs=pl.BlockSpec((nc, LANE), lambda q, j: (0, j * nq + q)),
        out_shape=jax.ShapeDtypeStruct((nc, 8 * SSM_WIDTH), F32),
        compiler_params=_cparams(("parallel", "arbitrary")),
    )(*([x8] * 8), sprev4, m_mat, cm_mat)


def _ssm_ds(name, dz8, sprev4, cm_mat):
    nc = dz8.shape[0]
    nq = cm_mat.shape[0]

    def body(*refs):
        dyq = _cat_tiles(refs[:8]).astype(BF16)
        s_ref, cm_ref, ds_ref, dcm_ref = refs[8:12]
        ds_ref[...] = _bdot(dyq, cm_ref[...], _DIMS["nt"])
        dcm_ref[...] = _bdot(s_ref[...], dyq, _DIMS["tn"])

    tile = pl.BlockSpec((None, None, nc, LANE), lambda q, t: (q, t, 0, 0))
    rowblk = pl.BlockSpec((None, LANE, 1024), lambda q, t: (q, t, 0))
    return pl.pallas_call(
        body, name=name, grid=(nq, 8),
        in_specs=_x_tile_specs(nc, nq) + [tile, rowblk],
        out_specs=(tile, rowblk),
        out_shape=(jax.ShapeDtypeStruct((nq, 8, nc, LANE), F32), jax.ShapeDtypeStruct((nq, 1024, 1024), F32)),
        compiler_params=_cparams(("parallel", "arbitrary")),
    )(*([dz8] * 8), sprev4, cm_mat)


def _ssm_dx(name, dz8, g4, x8, m_mat, bw_mat, d8):
    nc = dz8.shape[0]
    nq = m_mat.shape[0]

    def body(*refs):
        dyq = _cat_tiles(refs[:8]).astype(BF16)
        g_ref, x_ref, m_ref, bw_ref, d_ref, dzi_ref, dx_ref, dm_ref, dbw_ref = refs[8:17]
        gq = jnp.concatenate([g_ref[t] for t in range(8)], axis=1).astype(BF16)
        dx = _bdot(dyq, m_ref[...], _DIMS["nt"]) + _bdot(gq, bw_ref[...], _DIMS["nt"])
        dx_ref[...] = (dx + d_ref[...] * dzi_ref[...]).astype(dx_ref.dtype)
        xi = x_ref[...]
        dm_ref[...] = _bdot(xi, dyq, _DIMS["tn"])
        dbw_ref[...] = _bdot(xi, gq, _DIMS["tn"])

    xtile = pl.BlockSpec((nc, LANE), lambda q, i: (0, i * nq + q))
    rowblk = pl.BlockSpec((None, LANE, 1024), lambda q, i: (q, i, 0))
    return pl.pallas_call(
        body, name=name, grid=(nq, 8),
        in_specs=_x_tile_specs(nc, nq) + [pl.BlockSpec((None, 8, nc, LANE), lambda q, i: (q, 0, 0, 0)), xtile, rowblk, rowblk,
                                          pl.BlockSpec((1, LANE), lambda q, i: (0, q)), xtile],
        out_specs=(xtile, rowblk, rowblk),
        out_shape=(jax.ShapeDtypeStruct((nc, 8 * SSM_WIDTH), BF16), jax.ShapeDtypeStruct((nq, 1024, 1024), F32),
                   jax.ShapeDtypeStruct((nq, 1024, 1024), F32)),
        compiler_params=_cparams(("parallel", "arbitrary")),
    )(*([dz8] * 8), g4, x8, m_mat, bw_mat, d8, dz8)


CUM_BLK = 256


def _split3(x):
    hi = x.astype(BF16)
    r1 = x - hi.astype(F32)
    mid = r1.astype(BF16)
    lo = (r1 - mid.astype(F32)).astype(BF16)
    return hi, mid, lo


def _tri_dot(x, tri):
    hi, mid, lo = _split3(x)
    d = _DIMS["nn"]
    return _bdot(hi, tri, d) + _bdot(mid, tri, d) + _bdot(lo, tri, d)


def _tri(n, lower):
    r = lax.broadcasted_iota(jnp.int32, (n, n), 0)
    c = lax.broadcasted_iota(jnp.int32, (n, n), 1)
    return jnp.where((r >= c) if lower else (r <= c), 1.0, 0.0).astype(BF16)


def _fox_cum(name, fproj, bcol):
    seq = fproj.shape[0]
    blk = min(CUM_BLK, seq)

    def body(f_ref, b_ref, o_ref, carry_ref):
        i = pl.program_id(0)

        @pl.when(i == 0)
        def _():
            carry_ref[...] = jnp.zeros_like(carry_ref)

        z = f_ref[...].T + b_ref[...]
        logf = jnp.minimum(z, 0.0) - jnp.log(1.0 + jnp.exp(-jnp.abs(z)))
        carry = carry_ref[...]
        cum = _tri_dot(logf, _tri(blk, lower=False)) + jnp.tile(carry, (1, blk // LANE))
        o_ref[...] = cum[0:8, :]
        carry_ref[...] = carry + jnp.sum(logf, axis=1, keepdims=True)

    return pl.pallas_call(
        body, name=name, grid=(seq // blk,),
        in_specs=[pl.BlockSpec((blk, LANE), lambda i: (i, 0)), pl.BlockSpec((LANE, 1), lambda i: (0, 0))],
        out_specs=pl.BlockSpec((8, blk), lambda i: (0, i)),
        out_shape=jax.ShapeDtypeStruct((8, seq), F32),
        scratch_shapes=[pltpu.VMEM((LANE, LANE), F32)],
        compiler_params=_cparams(("arbitrary",)),
    )(fproj, bcol)


def _fox_cum_bwd(name, dcs, fproj, bcol):
    seq = fproj.shape[0]
    blk = min(CUM_BLK, seq)
    n = seq // blk

    def body(dc_ref, f_ref, b_ref, df_ref, db_ref, carry_ref, acc_ref):
        i = pl.program_id(0)

        @pl.when(i == 0)
        def _():
            carry_ref[...] = jnp.zeros_like(carry_ref)
            acc_ref[...] = jnp.zeros_like(acc_ref)

        r = lax.broadcasted_iota(jnp.int32, (LANE, FOX_WIDTH), 0)
        c = lax.broadcasted_iota(jnp.int32, (LANE, FOX_WIDTH), 1)
        want = (r >> 1) * LANE + jnp.where((r & 1) == 0, FOX_HEAD_DIM, 0)
        sel = jnp.where(jnp.logical_and(r < FOX_HEADS, c == want), 1.0, 0.0).astype(BF16)
        hi, mid, lo = _split3(dc_ref[...])
        nt = _DIMS["nt"]
        dc = _bdot(sel, hi, nt) + _bdot(sel, mid, nt) + _bdot(sel, lo, nt)
        carry = carry_ref[...]
        dlogf = _tri_dot(dc, _tri(blk, lower=True)) + jnp.tile(carry, (1, blk // LANE))
        carry_ref[...] = carry + jnp.sum(dc, axis=1, keepdims=True)
        z = f_ref[...].T + b_ref[...]
        dft = dlogf / (1.0 + jnp.exp(z))
        df_ref[...] = dft.T.astype(df_ref.dtype)
        acc_ref[...] += jnp.sum(dft, axis=1, keepdims=True)

        @pl.when(i == n - 1)
        def _():
            db_ref[...] = acc_ref[...]

    return pl.pallas_call(
        body, name=name, grid=(n,),
        in_specs=[pl.BlockSpec((blk, FOX_WIDTH), lambda i: (n - 1 - i, 0)), pl.BlockSpec((blk, LANE), lambda i: (n - 1 - i, 0)),
                  pl.BlockSpec((LANE, 1), lambda i: (0, 0))],
        out_specs=(pl.BlockSpec((blk, LANE), lambda i: (n - 1 - i, 0)), pl.BlockSpec((LANE, LANE), lambda i: (0, 0))),
        out_shape=(jax.ShapeDtypeStruct((seq, LANE), BF16), jax.ShapeDtypeStruct((LANE, LANE), F32)),
        scratch_shapes=[pltpu.VMEM((LANE, LANE), F32), pltpu.VMEM((LANE, LANE), F32)],
        compiler_params=_cparams(("arbitrary",)),
    )(dcs, fproj, bcol)


FOX_BLK = 512
FOX_SCALE = FOX_HEAD_DIM ** -0.5


def _fox_head_mask(shape, hh):
    lane = lax.broadcasted_iota(jnp.int32, shape, 1)
    return (lane < FOX_HEAD_DIM) if hh == 0 else (lane >= FOX_HEAD_DIM)


def _fox_bias(cum_ref, hh, q0, k0, blk):
    c0 = jnp.max(cum_ref[hh:hh + 1, pl.ds(q0, LANE)], axis=1, keepdims=True)
    return c0 - cum_ref[hh:hh + 1, pl.ds(k0, blk)]


def _fox_fwd(name, qkv, cum_t):
    seq = qkv.shape[0]
    blk = min(FOX_BLK, seq)
    nb = seq // blk
    npair = FOX_HEADS // 2

    def body(q_ref, k_ref, v_ref, cum_ref, o_ref, lse_ref):
        iq = pl.program_id(1)
        q0 = pl.multiple_of(iq * blk, blk)
        qv = q_ref[...]
        row = lax.broadcasted_iota(jnp.int32, (blk, blk), 0)
        col = lax.broadcasted_iota(jnp.int32, (blk, blk), 1)
        qhs = [jnp.where(_fox_head_mask(qv.shape, hh), qv, jnp.zeros_like(qv)) * FOX_SCALE for hh in range(2)]

        def block(kb, states, masked):
            k0 = pl.multiple_of(kb * blk, blk)
            kv = k_ref[pl.ds(k0, blk), :]
            vv = v_ref[pl.ds(k0, blk), :]
            new = []
            for hh in range(2):
                m, acc = states[hh]
                s = _bdot(qhs[hh], kv, _DIMS["nt"]) + _fox_bias(cum_ref, hh, q0, k0, blk)
                if masked:
                    s = jnp.where(row >= col, s, -jnp.inf)
                m_new = jnp.maximum(m, jnp.max(s, axis=1, keepdims=True))
                p = jnp.exp(s - m_new)
                vh = jnp.where(_fox_head_mask(vv.shape, hh), vv, jnp.ones_like(vv))
                acc = jnp.exp(m - m_new) * acc + _bdot(p, vh, _DIMS["nn"])
                new.append((m_new, acc))
            return tuple(new)

        init = (jnp.full((blk, 1), -jnp.inf, F32), jnp.zeros((blk, LANE), F32))
        states = lax.fori_loop(0, iq, lambda kb, st: block(kb, st, False), (init, init))
        states = block(iq, states, True)
        outs = []
        for hh in range(2):
            m, acc = states[hh]
            other = pltpu.roll(acc, FOX_HEAD_DIM, axis=1)
            outs.append(acc / other)
            lse_ref[hh] = m + jnp.log(jnp.where(_fox_head_mask(acc.shape, hh), other, acc))
        o_ref[...] = jnp.where(_fox_head_mask(outs[0].shape, 0), outs[0], outs[1]).astype(o_ref.dtype)

    return pl.pallas_call(
        body, name=name, grid=(npair, nb),
        in_specs=[pl.BlockSpec((blk, LANE), lambda p, i: (i, p)),
                  pl.BlockSpec((seq, LANE), lambda p, i: (0, npair + p)),
                  pl.BlockSpec((seq, LANE), lambda p, i: (0, 2 * npair + p)),
                  pl.BlockSpec((None, 2, seq), lambda p, i: (p, 0, 0))],
        out_specs=(pl.BlockSpec((blk, LANE), lambda p, i: (i, p)),
                   pl.BlockSpec((2, blk, LANE), lambda p, i: (p, i, 0))),
        out_shape=(jax.ShapeDtypeStruct((seq, FOX_WIDTH), BF16), jax.ShapeDtypeStruct((FOX_HEADS, seq, LANE), F32)),
        compiler_params=_cparams(("parallel", "arbitrary")),
    )(qkv, qkv, qkv, cum_t)


def _fox_bwd(name, qkv, cum_t, att, datt, lse):
    seq = qkv.shape[0]
    blk = min(FOX_BLK, seq)
    nb = seq // blk
    npair = FOX_HEADS // 2

    def body(q_ref, k_ref, v_ref, cum_ref, o_ref, do_ref, lse_ref, dq_ref, dk_ref, dv_ref, dcs_ref):
        iq = pl.program_id(1)
        q0 = pl.multiple_of(iq * blk, blk)

        @pl.when(iq == 0)
        def _():
            dk_ref[...] = jnp.zeros_like(dk_ref)
            dv_ref[...] = jnp.zeros_like(dv_ref)
            dcs_ref[...] = jnp.zeros_like(dcs_ref)

        qv = q_ref[...]
        dov = do_ref[...].astype(F32)
        ov = o_ref[...].astype(F32)
        row = lax.broadcasted_iota(jnp.int32, (blk, blk), 0)
        col = lax.broadcasted_iota(jnp.int32, (blk, blk), 1)
        low = _fox_head_mask((blk, LANE), 0)
        qhs, qones, dohbs, deltas, lses = [], [], [], [], []
        for hh in range(2):
            hm = _fox_head_mask(qv.shape, hh)
            qh = jnp.where(hm, qv, jnp.zeros_like(qv)) * FOX_SCALE
            qhs.append(qh)
            qones.append(jnp.where(hm, qh, jnp.ones_like(qh)))
            doh = jnp.where(hm, dov, 0.0)
            dohbs.append(doh.astype(BF16))
            deltas.append(jnp.sum(doh * ov, axis=1, keepdims=True))
            lses.append(jnp.tile(lse_ref[hh], (1, blk // LANE)))

        def block(kb, dqs, masked):
            k0 = pl.multiple_of(kb * blk, blk)
            kv = k_ref[pl.ds(k0, blk), :]
            vv = v_ref[pl.ds(k0, blk), :]
            new, dks, dvs = [], [], []
            for hh in range(2):
                s = _bdot(qhs[hh], kv, _DIMS["nt"]) + _fox_bias(cum_ref, hh, q0, k0, blk)
                p = jnp.exp(s - lses[hh])
                if masked:
                    p = jnp.where(row >= col, p, 0.0)
                dp = _bdot(dohbs[hh], vv, _DIMS["nt"])
                dsb = (p * (dp - deltas[hh])).astype(BF16)
                dks.append(_bdot(dsb, qones[hh], _DIMS["tn"]))
                dvs.append(_bdot(p, dohbs[hh], _DIMS["tn"]))
                kones = jnp.where(_fox_head_mask(kv.shape, hh), kv, jnp.ones_like(kv))
                new.append(dqs[hh] + _bdot(dsb, kones, _DIMS["nn"]))
            dk_ref[pl.ds(k0, blk), :] += jnp.where(low, dks[0], dks[1])
            dv_ref[pl.ds(k0, blk), :] += dvs[0] + dvs[1]
            dcs_ref[pl.ds(k0, blk), :] -= jnp.where(low, dks[1], dks[0])
            return tuple(new)

        init = jnp.zeros((blk, LANE), F32)
        dqs = lax.fori_loop(0, iq, lambda kb, a: block(kb, a, False), (init, init))
        dqs = block(iq, dqs, True)
        dcs_ref[pl.ds(q0, blk), :] += jnp.where(low, dqs[1], dqs[0])
        dq_ref[...] = (jnp.where(low, dqs[0], dqs[1]) * FOX_SCALE).astype(dq_ref.dtype)

    qblk = pl.BlockSpec((blk, LANE), lambda p, i: (i, p))
    full = pl.BlockSpec((seq, LANE), lambda p, i: (0, p))
    return pl.pallas_call(
        body, name=name, grid=(npair, nb),
        in_specs=[qblk,
                  pl.BlockSpec((seq, LANE), lambda p, i: (0, npair + p)),
                  pl.BlockSpec((seq, LANE), lambda p, i: (0, 2 * npair + p)),
                  pl.BlockSpec((None, 2, seq), lambda p, i: (p, 0, 0)),
                  qblk, qblk,
                  pl.BlockSpec((2, blk, LANE), lambda p, i: (p, i, 0))],
        out_specs=(qblk, full, full, full),
        out_shape=(jax.ShapeDtypeStruct((seq, FOX_WIDTH), BF16), jax.ShapeDtypeStruct((seq, FOX_WIDTH), F32),
                   jax.ShapeDtypeStruct((seq, FOX_WIDTH), F32), jax.ShapeDtypeStruct((seq, FOX_WIDTH), F32)),
        compiler_params=_cparams(("arbitrary", "arbitrary")),
    )(qkv, qkv, qkv, cum_t, att, datt, lse)


MEM_SCALE = MEM_HEAD_DIM ** -0.5


def _mem_probs(qh, kh):
    s = _bdot(qh, kh, _DIMS["nt"]) * MEM_SCALE
    p = jnp.exp(s - jnp.max(s, axis=1, keepdims=True))
    return p / jnp.sum(p, axis=1, keepdims=True)


def _mem_fwd(name, q2, kv, *, tr=512):
    seq = q2.shape[0]
    mlen = kv.shape[0]
    tr = min(tr, seq)

    def body(q_ref, kv_ref, o_ref):
        for h in range(MEM_HEADS):
            sl = slice(h * MEM_HEAD_DIM, (h + 1) * MEM_HEAD_DIM)
            sv = slice(MEM_WIDTH + h * MEM_HEAD_DIM, MEM_WIDTH + (h + 1) * MEM_HEAD_DIM)
            p = _mem_probs(q_ref[:, sl], kv_ref[:, sl])
            o_ref[:, sl] = _bdot(p, kv_ref[:, sv], _DIMS["nn"]).astype(o_ref.dtype)

    return pl.pallas_call(
        body, name=name, grid=(seq // tr,),
        in_specs=[pl.BlockSpec((tr, MEM_WIDTH), lambda i: (i, 0)), pl.BlockSpec((mlen, 2 * MEM_WIDTH), lambda i: (0, 0))],
        out_specs=pl.BlockSpec((tr, MEM_WIDTH), lambda i: (i, 0)),
        out_shape=jax.ShapeDtypeStruct((seq, MEM_WIDTH), BF16),
        compiler_params=_cparams(("parallel",)),
    )(q2, kv)


def _mem_bwd(name, q2, kv, do2, *, tr=512):
    seq = q2.shape[0]
    mlen = kv.shape[0]
    tr = min(tr, seq)

    def body(q_ref, kv_ref, do_ref, dq_ref, dkv_ref):
        i = pl.program_id(0)

        @pl.when(i == 0)
        def _():
            dkv_ref[...] = jnp.zeros_like(dkv_ref)

        for h in range(MEM_HEADS):
            sl = slice(h * MEM_HEAD_DIM, (h + 1) * MEM_HEAD_DIM)
            sv = slice(MEM_WIDTH + h * MEM_HEAD_DIM, MEM_WIDTH + (h + 1) * MEM_HEAD_DIM)
            qh = q_ref[:, sl]
            kh = kv_ref[:, sl]
            doh = do_ref[:, sl].astype(BF16)
            p = _mem_probs(qh, kh)
            dp = _bdot(doh, kv_ref[:, sv], _DIMS["nt"])
            ds = (p * (dp - jnp.sum(p * dp, axis=1, keepdims=True)) * MEM_SCALE).astype(BF16)
            dq_ref[:, sl] = _bdot(ds, kh, _DIMS["nn"]).astype(dq_ref.dtype)
            dkv_ref[:, sl] += _bdot(ds, qh, _DIMS["tn"])
            dkv_ref[:, sv] += _bdot(p, doh, _DIMS["tn"])

    row = pl.BlockSpec((tr, MEM_WIDTH), lambda i: (i, 0))
    kvs = pl.BlockSpec((mlen, 2 * MEM_WIDTH), lambda i: (0, 0))
    return pl.pallas_call(
        body, name=name, grid=(seq // tr,), in_specs=[row, kvs, row], out_specs=(row, kvs),
        out_shape=(jax.ShapeDtypeStruct((seq, MEM_WIDTH), BF16), jax.ShapeDtypeStruct((mlen, 2 * MEM_WIDTH), F32)),
        compiler_params=_cparams(("arbitrary",)),
    )(q2, kv, do2)


_HBM = pl.BlockSpec(memory_space=pl.ANY)
_HBM_ONLY = pl.BlockSpec(memory_space=pltpu.HBM)
_MESH = pl.DeviceIdType.MESH


def _mesh_place():
    x, y, c = lax.axis_index("x"), lax.axis_index("y"), lax.axis_index("c")
    other_chips = [(1 - x, y), (x, 1 - y), (1 - x, 1 - y)]
    return x, y, c, other_chips


def _gather_all(name, arrays):
    n = len(arrays)

    def body(*refs):
        ins, outs = refs[:n], refs[n:2 * n]
        send_sems, recv_sems, local_sems = refs[2 * n:]
        x, y, c, chips = _mesh_place()
        me, sibling = (x, y, c), (x, y, 1 - c)

        def slot(a, place):
            px, py, pc = place
            return outs[a].at[4 * px + 2 * py + pc]

        def copy(a, k, block, to, src=None):
            return pltpu.make_async_remote_copy(
                src_ref=slot(a, block) if src is None else src, dst_ref=slot(a, block),
                send_sem=send_sems.at[a, k], recv_sem=recv_sems.at[a, k], device_id=to, device_id_type=_MESH)

        mine = [pltpu.make_async_copy(ins[a], slot(a, me), local_sems.at[a]) for a in range(n)]
        for cp in mine:
            cp.start()
        first = []
        for a in range(n):
            first.append(copy(a, 0, me, sibling, src=ins[a]))
            first += [copy(a, 1 + j, me, (*chip, c), src=ins[a]) for j, chip in enumerate(chips)]
        for cp in first:
            cp.start()
        passed = []
        for j, chip in enumerate(chips):
            for a in range(n):
                copy(a, 1 + j, (*chip, c), me).wait_recv()
                fwd = copy(a, 4 + j, (*chip, c), sibling)
                fwd.start()
                passed.append(fwd)
        for a in range(n):
            copy(a, 0, sibling, me).wait_recv()
            for j, chip in enumerate(chips):
                copy(a, 4 + j, (*chip, 1 - c), me).wait_recv()
        for cp in first + passed:
            cp.wait_send()
        for cp in mine:
            cp.wait()

    out_shape = tuple(jax.ShapeDtypeStruct((N_DEV,) + arr.shape, arr.dtype) for arr in arrays)
    return pl.pallas_call(
        body, name=name, in_specs=[_HBM] * n, out_specs=tuple([_HBM] * n), out_shape=out_shape,
        scratch_shapes=[pltpu.SemaphoreType.DMA((n, N_DEV - 1)), pltpu.SemaphoreType.DMA((n, N_DEV - 1)),
                        pltpu.SemaphoreType.DMA((n,))],
    )(*arrays)


_SEM = pl.BlockSpec(memory_space=pltpu.SEMAPHORE)
_DATAFLOW = pltpu.SideEffectType.DATAFLOW_SIDE_EFFECTING


def _device_index():
    return (4 * lax.axis_index("x") + 2 * lax.axis_index("y") + lax.axis_index("c")).astype(jnp.int32).reshape(1)


def _place_own(name, pieces, *, stacked_src, after=None):
    n = len(pieces)
    n_in = n + (after is not None)

    def body(me_ref, *refs):
        for a in range(n):
            refs[n_in + a][...] = refs[a][...]

    def spec(shape):
        return pl.BlockSpec((None,) + tuple(shape), lambda i, me_ref: (me_ref[0],) + (0,) * len(shape))

    shapes = [p.shape[1:] if stacked_src else p.shape for p in pieces]
    if stacked_src:
        in_specs = [spec(s) for s in shapes]
    else:
        in_specs = [pl.BlockSpec(tuple(s), lambda i, me_ref, nd=len(s): (0,) * nd) for s in shapes]
    operands = list(pieces)
    if after is not None:
        in_specs.append(_HBM)
        operands.append(after)
    return pl.pallas_call(
        body, name=name,
        grid_spec=pltpu.PrefetchScalarGridSpec(num_scalar_prefetch=1, grid=(1,), in_specs=in_specs,
                                               out_specs=tuple(spec(s) for s in shapes)),
        out_shape=tuple(jax.ShapeDtypeStruct((N_DEV,) + tuple(s), p.dtype) for s, p in zip(shapes, pieces)),
        compiler_params=_cparams(("arbitrary",)),
    )(_device_index(), *operands)


def _peer_places():
    x, y, c = lax.axis_index("x"), lax.axis_index("y"), lax.axis_index("c")
    peers = []
    for k in range(N_DEV - 1):
        flip = k + 1
        px = 1 - x if flip & 4 else x
        py = 1 - y if flip & 2 else y
        pc = 1 - c if flip & 1 else c
        peers.append((px, py, pc, 4 * px + 2 * py + pc))
    return 4 * x + 2 * y + c, peers


def _direct_copy(srcs, lands, send_sems, recv_sems, a, k, me, peer, scatter):
    px, py, pc, pidx = peer
    return pltpu.make_async_remote_copy(
        src_ref=srcs[a].at[pidx] if scatter else srcs[a], dst_ref=lands[a].at[me],
        send_sem=send_sems.at[a * (N_DEV - 1) + k], recv_sem=recv_sems.at[a * (N_DEV - 1) + k],
        device_id=(px, py, pc), device_id_type=_MESH)


def _send_start(name, srcs, lands, *, scatter):
    n = len(srcs)

    def body(*refs):
        src_refs, land_refs = refs[:n], refs[n:2 * n]
        send_sems, recv_sems = refs[2 * n], refs[2 * n + 1]
        token = refs[-1]
        me, peers = _peer_places()
        for k, peer in enumerate(peers):
            for a in range(n):
                _direct_copy(src_refs, land_refs, send_sems, recv_sems, a, k, me, peer, scatter).start()
        token[...] = jnp.zeros_like(token)

    hbm_shapes = [pltpu.HBM(t.shape, t.dtype) for t in list(srcs) + list(lands)]
    outs = pl.pallas_call(
        body, name=name,
        out_shape=(pltpu.SemaphoreType.DMA((n * (N_DEV - 1),)), pltpu.SemaphoreType.DMA((n * (N_DEV - 1),)), *hbm_shapes,
                   jax.ShapeDtypeStruct((8, LANE), F32)),
        in_specs=[_HBM_ONLY] * (2 * n),
        out_specs=(_SEM, _SEM, *([_HBM_ONLY] * (2 * n)), pl.BlockSpec(memory_space=pltpu.VMEM)),
        input_output_aliases={i: 2 + i for i in range(2 * n)},
        compiler_params=pltpu.CompilerParams(has_side_effects=_DATAFLOW),
    )(*[pltpu.with_memory_space_constraint(t, pltpu.HBM) for t in list(srcs) + list(lands)])
    return outs[0], outs[1], outs[2:2 + n], outs[2 + n:2 + 2 * n], outs[-1]


def _send_wait(name, send_sems, recv_sems, srcs, lands, after, *, scatter):
    n = len(srcs)

    def body(*refs):
        src_refs, land_refs = refs[:n], refs[n:2 * n]
        send_sems, recv_sems = refs[2 * n], refs[2 * n + 1]
        me, peers = _peer_places()
        for k, peer in enumerate(peers):
            for a in range(n):
                cp = _direct_copy(src_refs, land_refs, send_sems, recv_sems, a, k, me, peer, scatter)
                cp.wait_send()
                cp.wait_recv()

    hbm_shapes = [pltpu.HBM(t.shape, t.dtype) for t in list(srcs) + list(lands)]
    outs = pl.pallas_call(
        body, name=name, out_shape=tuple(hbm_shapes),
        in_specs=[_HBM_ONLY] * (2 * n) + [_SEM, _SEM, _HBM],
        out_specs=tuple([_HBM_ONLY] * (2 * n)),
        input_output_aliases={i: i for i in range(2 * n)},
        compiler_params=pltpu.CompilerParams(has_side_effects=_DATAFLOW),
    )(*srcs, *lands, send_sems, recv_sems, after)
    return outs[n:]


def _unstack_cols(name, stacked):
    n, rows, cols = stacked.shape

    def body(i_ref, o_ref):
        o_ref[...] = i_ref[...]

    return pl.pallas_call(
        body, name=name, grid=(n,), in_specs=[pl.BlockSpec((None, rows, cols), lambda k: (k, 0, 0))],
        out_specs=pl.BlockSpec((rows, cols), lambda k: (0, k)),
        out_shape=jax.ShapeDtypeStruct((rows, n * cols), stacked.dtype),
        compiler_params=_cparams(("parallel",)),
    )(stacked)


def _restack_cols(name, mat):
    rows, width = mat.shape
    cols = width // N_DEV

    def body(i_ref, o_ref):
        o_ref[...] = i_ref[...]

    return pl.pallas_call(
        body, name=name, grid=(N_DEV,), in_specs=[pl.BlockSpec((rows, cols), lambda k: (0, k))],
        out_specs=pl.BlockSpec((None, rows, cols), lambda k: (k, 0, 0)),
        out_shape=jax.ShapeDtypeStruct((N_DEV, rows, cols), mat.dtype),
        compiler_params=_cparams(("parallel",)),
    )(mat)


def _remap_pieces(runs):
    plan = {}
    for du, dc, su, sc, ln in runs:
        while ln > 0:
            lane = dc % LANE
            take = min(ln, LANE - lane)
            plan.setdefault((du, dc // LANE), []).append((su, sc, take, lane))
            dc, sc, ln = dc + take, sc + take, ln - take
    return plan


def _remap(name, srcs, src_units, runs, *, out_units, out_cols, out_dtype, tr=256):
    rows = srcs[0].shape[-2]
    tr = min(tr, rows)
    plan = _remap_pieces(runs)
    n_src = len(srcs)
    stacked_out = out_units is not None
    n_tiles = out_cols // LANE

    def body(*refs):
        o_ref = refs[n_src]

        def src_tile(unit, t):
            ai, lead = src_units[unit]
            ref = refs[ai]
            sl = slice(t * LANE, (t + 1) * LANE)
            return (ref[:, sl] if lead is None else ref[lead, :, sl]).astype(F32)

        lane = lax.broadcasted_iota(jnp.int32, (tr, LANE), 1)
        for du in range(out_units if stacked_out else 1):
            for t in range(n_tiles):
                acc = jnp.zeros((tr, LANE), F32)
                for su, sc, ln, dl in plan.get((du if stacked_out else None, t), []):
                    st, so = sc // LANE, sc % LANE
                    first = src_tile(su, st)
                    if so == dl and so + ln <= LANE:
                        piece = first
                    else:
                        second = src_tile(su, st + 1) if so + ln > LANE else first
                        both = jnp.concatenate([first, second], axis=1)
                        piece = pltpu.roll(both, (dl - so) % (2 * LANE), axis=1)[:, 0:LANE]
                    acc = piece if (dl == 0 and ln == LANE) else jnp.where(
                        jnp.logical_and(lane >= dl, lane < dl + ln), piece, acc)
                if stacked_out:
                    o_ref[du, :, t * LANE:(t + 1) * LANE] = acc.astype(o_ref.dtype)
                else:
                    o_ref[:, t * LANE:(t + 1) * LANE] = acc.astype(o_ref.dtype)

    in_specs = []
    for arr in srcs:
        if arr.ndim == 2:
            in_specs.append(pl.BlockSpec((tr, arr.shape[1]), lambda i: (i, 0)))
        else:
            in_specs.append(pl.BlockSpec((arr.shape[0], tr, arr.shape[2]), lambda i: (0, i, 0)))
    if stacked_out:
        out_spec = pl.BlockSpec((out_units, tr, out_cols), lambda i: (0, i, 0))
        out_shape = jax.ShapeDtypeStruct((out_units, rows, out_cols), out_dtype)
    else:
        out_spec = pl.BlockSpec((tr, out_cols), lambda i: (i, 0))
        out_shape = jax.ShapeDtypeStruct((rows, out_cols), out_dtype)
    return pl.pallas_call(
        body, name=name, grid=(rows // tr,), in_specs=in_specs, out_specs=out_spec, out_shape=out_shape,
        compiler_params=_cparams(("parallel",)),
    )(*srcs)


def _proj_col(c):
    if c < PROJ_GATE0:
        return c
    if c < PROJ_GATE0 + FOX_HEADS:
        return PROJ_F0 + (c - PROJ_GATE0)
    return c - FOX_HEADS


def _win_runs():
    cuts = sorted(set([0, PROJ_GATE0, PROJ_GATE0 + FOX_HEADS, IN_WIDTH] + [SHARD_IN * k for k in range(N_DEV + 1)]))
    return [(lo // SHARD_IN, lo % SHARD_IN, _proj_col(lo), hi - lo) for lo, hi in zip(cuts[:-1], cuts[1:])]


def _assemble_win(name, stacked):
    runs = [(None, pc, k, sc, ln) for k, sc, pc, ln in _win_runs()]
    return _remap(name, [stacked], [(0, k) for k in range(N_DEV)], runs,
                  out_units=None, out_cols=PROJ_WIDTH, out_dtype=BF16)


def _disassemble_dwin(name, dw):
    runs = [(k, sc, 0, pc, ln) for k, sc, pc, ln in _win_runs()]
    return _remap(name, [dw], [(0, None)], runs, out_units=N_DEV, out_cols=SHARD_IN_PAD, out_dtype=BF16)


def _concat_cols(name, parts, *, tr=512):
    rows = parts[0].shape[0]
    tr = min(tr, rows)
    widths = [p.shape[1] for p in parts]
    total = sum(widths)

    def body(*refs):
        o_ref = refs[len(parts)]
        lo = 0
        for r, w in zip(refs[:len(parts)], widths):
            o_ref[:, lo:lo + w] = r[...].astype(o_ref.dtype)
            lo += w

    return pl.pallas_call(
        body, name=name, grid=(rows // tr,),
        in_specs=[pl.BlockSpec((tr, w), lambda i: (i, 0)) for w in widths],
        out_specs=pl.BlockSpec((tr, total), lambda i: (i, 0)),
        out_shape=jax.ShapeDtypeStruct((rows, total), BF16),
        compiler_params=_cparams(("parallel",)),
    )(*parts)


FFN_BLK = FFN_HIDDEN // 2


def _ffn_col(c):
    half, r = divmod(c, FFN_HIDDEN)
    blk, r = divmod(r, FFN_BLK)
    return blk * 2 * FFN_BLK + half * FFN_BLK + r


def _assemble_wffn(name, stacked):
    runs = [(None, _ffn_col(SHARD_FFN * k), k, 0, SHARD_FFN) for k in range(N_DEV)]
    return _remap(name, [stacked], [(0, k) for k in range(N_DEV)], runs,
                  out_units=None, out_cols=2 * FFN_HIDDEN, out_dtype=BF16)


def _disassemble_dwffn(name, dw):
    runs = [(k, 0, 0, _ffn_col(SHARD_FFN * k), SHARD_FFN) for k in range(N_DEV)]
    return _remap(name, [dw], [(0, None)], runs, out_units=N_DEV, out_cols=SHARD_FFN_PAD, out_dtype=BF16)


def _ffn_in_swiglu(name, xn, w, *, tm=512):
    rows, k = xn.shape
    tm = min(tm, rows)
    nblk = FFN_HIDDEN // FFN_BLK

    def body(x_ref, w_ref, f_ref, g_ref):
        f = _bdot(x_ref[...], w_ref[...], _DIMS["nn"])
        f_ref[...] = f.astype(f_ref.dtype)
        fa = f[:, 0:FFN_BLK]
        g_ref[...] = (fa * _sigmoid(fa) * f[:, FFN_BLK:2 * FFN_BLK]).astype(g_ref.dtype)

    return pl.pallas_call(
        body, name=name, grid=(nblk, rows // tm),
        in_specs=[pl.BlockSpec((tm, k), lambda j, i: (i, 0)), pl.BlockSpec((k, 2 * FFN_BLK), lambda j, i: (0, j))],
        out_specs=(pl.BlockSpec((tm, 2 * FFN_BLK), lambda j, i: (i, j)), pl.BlockSpec((tm, FFN_BLK), lambda j, i: (i, j))),
        out_shape=(jax.ShapeDtypeStruct((rows, 2 * FFN_HIDDEN), BF16), jax.ShapeDtypeStruct((rows, FFN_HIDDEN), BF16)),
        compiler_params=_cparams(("parallel", "arbitrary")),
    )(xn, w)


def _d_ffn_out_swiglu(name, dh, w_out, f, *, tm=512):
    rows, d = dh.shape
    tm = min(tm, rows)
    nblk = FFN_HIDDEN // FFN_BLK

    def body(dh_ref, w_ref, f_ref, df_ref):
        dg = _bdot(dh_ref[...], w_ref[...], _DIMS["nt"])
        fa = f_ref[:, 0:FFN_BLK].astype(F32)
        fb = f_ref[:, FFN_BLK:2 * FFN_BLK].astype(F32)
        s = _sigmoid(fa)
        df_ref[:, 0:FFN_BLK] = (dg * fb * s * (1.0 + fa * (1.0 - s))).astype(df_ref.dtype)
        df_ref[:, FFN_BLK:2 * FFN_BLK] = (dg * fa * s).astype(df_ref.dtype)

    wide = pl.BlockSpec((tm, 2 * FFN_BLK), lambda j, i: (i, j))
    return pl.pallas_call(
        body, name=name, grid=(nblk, rows // tm),
        in_specs=[pl.BlockSpec((tm, d), lambda j, i: (i, 0)), pl.BlockSpec((FFN_BLK, d), lambda j, i: (j, 0)), wide],
        out_specs=wide, out_shape=jax.ShapeDtypeStruct((rows, 2 * FFN_HIDDEN), BF16),
        compiler_params=_cparams(("parallel", "arbitrary")),
    )(dh, w_out, f)


def _adamw(name, parts, w, m, v, *, tr=128):
    rows, cols = w.shape
    n_parts = parts.shape[0]
    tr = min(tr, rows)
    assert rows % tr == 0, (name, rows, tr)
    c1 = 1.0 - ADAM_B1 ** ADAM_STEP
    c2 = 1.0 - ADAM_B2 ** ADAM_STEP

    def body(p_ref, w_ref, m_ref, v_ref, g_ref, d_ref, nm_ref, nv_ref):
        g = p_ref[0].astype(F32)
        for s in range(1, n_parts):
            g = g + p_ref[s].astype(F32)
        m_new = ADAM_B1 * m_ref[...] + (1.0 - ADAM_B1) * g
        v_new = ADAM_B2 * v_ref[...] + (1.0 - ADAM_B2) * (g * g)
        upd = (m_new / c1) / (jnp.sqrt(v_new / c2) + ADAM_EPS) + ADAM_WD * w_ref[...]
        g_ref[...] = g
        d_ref[...] = -ADAM_LR * upd
        nm_ref[...] = m_new
        nv_ref[...] = v_new

    row = pl.BlockSpec((tr, cols), lambda i: (i, 0))
    out = jax.ShapeDtypeStruct((rows, cols), F32)
    return pl.pallas_call(
        body, name=name, grid=(rows // tr,),
        in_specs=[pl.BlockSpec((n_parts, tr, cols), lambda i: (0, i, 0)), row, row, row],
        out_specs=(row, row, row, row), out_shape=(out, out, out, out),
        compiler_params=_cparams(("parallel",)),
    )(parts, w, m, v)


_WEIGHTS = ("norm_mix", "w_in", "b_forget", "lam_re", "lam_im", "log_dt", "b_re", "b_im", "c_re", "c_im",
            "d_skip", "w_glu", "w_fox_o", "w_mix_out", "norm_mem_q", "norm_mem_kv", "w_mem_q", "w_mem_kv",
            "w_mem_o", "norm_ffn", "w_ffn_in", "w_ffn_out", "norm_final")
_SHARDED = ("w_in", "w_glu", "w_fox_o", "w_mix_out", "w_mem_q", "w_mem_kv", "w_mem_o", "w_ffn_in", "w_ffn_out")
_SMALL = tuple(n for n in _WEIGHTS if n not in _SHARDED)
_PACK_COLS = 1024


def _pack(arrays):
    flat = jnp.concatenate([a.reshape(-1).astype(F32) for a in arrays])
    rows = -(-flat.shape[0] // _PACK_COLS)
    return jnp.pad(flat, (0, rows * _PACK_COLS - flat.shape[0])).reshape(rows, _PACK_COLS)


def _unpack(buf, like):
    flat = buf.reshape(-1)
    out, pos = [], 0
    for a in like:
        out.append(flat[pos:pos + a.size].reshape(a.shape))
        pos += a.size
    return out


def _mm(name, a, b, mode, m, n, k, out_dtype, tm=1024, tn=512, tk=1024, **kw):
    return _matmul(name, a, b, mode, m, n, k, out_dtype=out_dtype, tm=tm, tn=tn, tk=tk, **kw)


def kernel(x, mem, norm_mix, w_in, b_forget, lam_re, lam_im, log_dt, b_re, b_im, c_re, c_im, d_skip, w_glu, w_fox_o, w_mix_out, norm_mem_q, norm_mem_kv, w_mem_q, w_mem_kv, w_mem_o, norm_ffn, w_ffn_in, w_ffn_out, norm_final, loss_target, m_norm_mix, m_w_in, m_b_forget, m_lam_re, m_lam_im, m_log_dt, m_b_re, m_b_im, m_c_re, m_c_im, m_d_skip, m_w_glu, m_w_fox_o, m_w_mix_out, m_norm_mem_q, m_norm_mem_kv, m_w_mem_q, m_w_mem_kv, m_w_mem_o, m_norm_ffn, m_w_ffn_in, m_w_ffn_out, m_norm_final, v_norm_mix, v_w_in, v_b_forget, v_lam_re, v_lam_im, v_log_dt, v_b_re, v_b_im, v_c_re, v_c_im, v_d_skip, v_w_glu, v_w_fox_o, v_w_mix_out, v_norm_mem_q, v_norm_mem_kv, v_w_mem_q, v_w_mem_kv, v_w_mem_o, v_norm_ffn, v_w_ffn_in, v_w_ffn_out, v_norm_final):
    given = dict(locals())
    weights = {n: given[n] for n in _WEIGHTS}
    mom_m = {n: given["m_" + n] for n in _WEIGHTS}
    mom_v = {n: given["v_" + n] for n in _WEIGHTS}
    seq = x.shape[1]
    nc = seq // SSM_CHUNK
    d = D_MODEL
    xs, mems, tgt = x[0], mem[0], loss_target[0]

    def padcols(a, width):
        return jnp.pad(a, ((0, 0), (0, width - a.shape[1])))

    shards = [padcols(w_in[0].astype(BF16), SHARD_IN_PAD), w_glu[0].astype(BF16), w_fox_o[0].astype(BF16),
              w_mix_out[0].astype(BF16), w_mem_q[0].astype(BF16), w_mem_kv[0].astype(BF16),
              w_mem_o[0].astype(BF16), padcols(w_ffn_in[0].astype(BF16), SHARD_FFN_PAD), w_ffn_out[0].astype(BF16)]
    win = _assemble_win("assemble_w_in", _gather_all("gather_w_in", shards[:1])[0])
    rest = shards[1:]
    gsend, grecv, rest_thru, lands, gtoken = _send_start(
        "gather_rest_start", rest, _place_own("place_weight_shards", rest, stacked_src=False, after=win), scatter=False)

    u = _rms_fwd("rms_mix", xs, norm_mix, after=gtoken)
    ussm = _mm("proj_ssm", u, win, "nn", seq, SSM_WIDTH, d, F32)
    qkv = _mm("proj_qkv", u, win, "nn", seq, 3 * FOX_WIDTH, d, BF16, tn=512, b_off=(0, SSM_WIDTH))
    gates = _mm("proj_gates", u, win, "nn", seq, 2 * d, d, BF16, tn=1024, b_off=(0, PROJ_GATE0))
    fproj = _mm("proj_forget", u, win, "nn", seq, LANE, d, F32, tn=LANE, b_off=(0, PROJ_F0))

    ssm_params = (lam_re[0], lam_im[0], log_dt[0], b_re[0], b_im[0], c_re[0], c_im[0])
    (m_c, bw_c, cm_c, a8, aseg), mats_vjp = jax.vjp(lambda *p: _ssm_mats(*p, nc), *ssm_params)
    m_b = _bd_expand("ssm_expand_m", _BD_M, m_c)
    bw_b = _bd_expand("ssm_expand_bw", _BD_BW, bw_c)
    cm_b = _bd_expand("ssm_expand_cm", _BD_CM, cm_c)
    u8 = ussm.reshape(nc, SSM_CHUNK * SSM_WIDTH)
    d8 = jnp.tile(d_skip, (1, SSM_CHUNK))
    w4 = _ssm_w("ssm_w", u8, bw_b)
    sp4 = _ssm_scan("ssm_scan", w4, a8, aseg, reverse=False)
    y8 = _ssm_y("ssm_y", u8, sp4, m_b, cm_b)
    act = _ssm_post_fwd("ssm_act", y8, u8, d8).reshape(seq, SSM_WIDTH)

    bcol = jnp.pad(b_forget[0], (0, LANE - FOX_HEADS)).reshape(LANE, 1)
    cum_t = _fox_cum("fox_cum", fproj, bcol).reshape(FOX_HEADS // 2, 2, seq)
    att, lse = _fox_fwd("fox_fwd", qkv, cum_t)

    gathered = _send_wait("gather_rest_wait", gsend, grecv, rest_thru, lands, att, scatter=False)
    wglu = _unstack_cols("unstack_w_glu", gathered[0])
    wfoxo = _unstack_cols("unstack_w_fox_o", gathered[1])
    wmix = gathered[2].reshape(d, d)
    wmq = gathered[3].reshape(d, MEM_WIDTH)
    wmkv = gathered[4].reshape(d, 2 * MEM_WIDTH)
    wmo = _unstack_cols("unstack_w_mem_o", gathered[5])
    wffn_in = _assemble_wffn("assemble_w_ffn_in", gathered[6])
    wffn_out = gathered[7].reshape(FFN_HIDDEN, d)

    glu = _mm("glu", act, wglu, "nn", seq, 2 * d, SSM_WIDTH, BF16, tn=1024)
    out_b = _mm("fox_out", att, wfoxo, "nn", seq, d, FOX_WIDTH, BF16, tn=1024)

    mixin = _mix_fwd("mix", glu, gates, out_b)
    h1 = _mm("mix_out", mixin, wmix, "nn", seq, d, d, F32, tn=1024, add=xs)

    n1 = _rms_fwd("rms_mem_q", h1, norm_mem_q)
    q2 = _mm("mem_q", n1, wmq, "nn", seq, MEM_WIDTH, d, BF16)
    mn = _rms_fwd("rms_mem_kv", mems, norm_mem_kv)
    mlen = mems.shape[0]
    kv = _mm("mem_kv", mn, wmkv, "nn", mlen, 2 * MEM_WIDTH, d, BF16)
    o2 = _mem_fwd("mem_attn", q2, kv)
    h2 = _mm("mem_out", o2, wmo, "nn", seq, d, MEM_WIDTH, F32, tn=1024, add=h1)

    n2 = _rms_fwd("rms_ffn", h2, norm_ffn)
    f, g_act = _ffn_in_swiglu("ffn_in_swiglu", n2, wffn_in)
    h3 = _mm("ffn_out", g_act, wffn_out, "nn", seq, d, FFN_HIDDEN, F32, tk=FFN_HIDDEN, add=h2)
    loss_part, dh3, dg_final = _final_loss("final_loss", h3, tgt, norm_final.reshape(1, d))

    df = _d_ffn_out_swiglu("d_ffn_out_swiglu", dh3, wffn_out, f)
    dwffn_out = _mm("d_ffn_out_w", g_act, dh3, "tn", FFN_HIDDEN, d, seq, BF16, tm=1408, tn=1024)
    dn2 = _mm("d_ffn_in_x", df, wffn_in, "nt", seq, d, 2 * FFN_HIDDEN, F32, tn=1024, tk=FFN_HIDDEN)
    dwffn_in = _mm("d_ffn_in_w", n2, df, "tn", d, 2 * FFN_HIDDEN, seq, BF16, tn=1408)
    dh2, dg_ffn = _rms_bwd("d_rms_ffn", dn2, h2, norm_ffn, res=dh3)

    do2 = _mm("d_mem_out_x", dh2, wmo, "nt", seq, MEM_WIDTH, d, F32)
    dwmo = _restack_cols("restack_d_w_mem_o", _mm("d_mem_out_w", o2, dh2, "tn", MEM_WIDTH, d, seq, BF16, tn=1024))
    dq2, dkv = _mem_bwd("d_mem_attn", q2, kv, do2)
    dwmq = _mm("d_mem_q_w", n1, dq2, "tn", d, MEM_WIDTH, seq, BF16)
    dn1 = _mm("d_mem_q_x", dq2, wmq, "nt", seq, d, MEM_WIDTH, F32)
    dwmkv = _mm("d_mem_kv_w", mn, dkv, "tn", d, 2 * MEM_WIDTH, mlen, BF16, tn=1024)
    dmn = _mm("d_mem_kv_x", dkv, wmkv, "nt", mlen, d, 2 * MEM_WIDTH, F32)
    _, dg_memkv = _rms_bwd("d_rms_mem_kv", dmn, mems, norm_mem_kv)

    early = [dwmq.reshape(N_DEV, d // N_DEV, MEM_WIDTH), dwmkv.reshape(N_DEV, d // N_DEV, 2 * MEM_WIDTH), dwmo,
             _disassemble_dwffn("split_d_w_ffn_in", dwffn_in), dwffn_out.reshape(N_DEV, FFN_HIDDEN // N_DEV, d)]
    ssend, srecv, early_thru, early_lands, stoken = _send_start(
        "scatter_early_start", early, _place_own("place_early_grads", early, stacked_src=True), scatter=True)
    dh1, dg_memq = _rms_bwd("d_rms_mem_q", dn1, h1, norm_mem_q, res=dh2, after=stoken)

    dmixin = _mm("d_mix_out_x", dh1, wmix, "nt", seq, d, d, F32, tn=1024)
    dwmix = _mm("d_mix_out_w", mixin, dh1, "tn", d, d, seq, BF16, tn=1024)
    dglu, dgates, dout_b = _mix_bwd("d_mix", dmixin, glu, gates, out_b)
    datt = _mm("d_fox_out_x", dout_b, wfoxo, "nt", seq, FOX_WIDTH, d, F32)
    dwfoxo = _restack_cols("restack_d_w_fox_o", _mm("d_fox_out_w", att, dout_b, "tn", FOX_WIDTH, d, seq, BF16, tn=1024))
    dact = _mm("d_glu_x", dglu, wglu, "nt", seq, SSM_WIDTH, 2 * d, F32, tk=2 * d)
    dwglu = _restack_cols("restack_d_w_glu", _mm("d_glu_w", act, dglu, "tn", SSM_WIDTH, 2 * d, seq, BF16, tn=2 * d))

    mid = [dwglu, dwfoxo, dwmix.reshape(N_DEV, d // N_DEV, d)]
    msend, mrecv, mid_thru, mid_lands, mtoken = _send_start(
        "scatter_mid_start", mid, _place_own("place_mid_grads", mid, stacked_src=True), scatter=True)

    dz8, dg_dskip = _ssm_post_bwd("d_ssm_act", dact.reshape(nc, SSM_CHUNK * SSM_WIDTH), y8, u8, d8, after=mtoken)
    ds4, dcm = _ssm_ds("d_ssm_y_state", dz8, sp4, cm_b)
    g4, da8 = _ssm_scan("d_ssm_scan", ds4, a8, aseg, reverse=True, sprev4=sp4)
    dx8, dm, dbw = _ssm_dx("d_ssm_x", dz8, g4, u8, m_b, bw_b, d8)
    dussm = dx8.reshape(seq, SSM_WIDTH)
    g_ssm = mats_vjp((_bd_reduce("ssm_reduce_dm", _BD_M, dm), _bd_reduce("ssm_reduce_dbw", _BD_BW, dbw),
                      _bd_reduce("ssm_reduce_dcm", _BD_CM, dcm), da8, jnp.zeros_like(aseg)))

    dq, dk, dv, dcs = _fox_bwd("d_fox", qkv, cum_t, att, datt, lse)
    dfproj, dbf = _fox_cum_bwd("d_fox_cum", dcs, fproj, bcol)
    dg_bforget = dbf[0:FOX_HEADS, 0].reshape(1, FOX_HEADS)

    dproj = _concat_cols("d_proj_concat", (dussm, dq, dk, dv, dgates, dfproj))
    dwin = _mm("d_proj_w", u, dproj, "tn", d, PROJ_WIDTH, seq, BF16, tn=1408)
    late = [_disassemble_dwin("split_d_w_in", dwin)]
    lsend, lrecv, late_thru, late_lands, ltoken = _send_start(
        "scatter_late_start", late, _place_own("place_late_grads", late, stacked_src=True), scatter=True)
    du = _mm("d_proj_x", dproj, win, "nt", seq, d, PROJ_WIDTH, F32, tn=1024, tk=1408)
    dx, dg_mix = _rms_bwd("d_rms_mix", du, xs, norm_mix, res=dh1, after=ltoken)

    early_parts = _send_wait("scatter_early_wait", ssend, srecv, early_thru, early_lands, dx, scatter=True)
    mid_parts = _send_wait("scatter_mid_wait", msend, mrecv, mid_thru, mid_lands, dx, scatter=True)
    received = dict(zip(("w_glu", "w_fox_o", "w_mix_out"), mid_parts))
    received.update(zip(("w_mem_q", "w_mem_kv", "w_mem_o", "w_ffn_in", "w_ffn_out"), early_parts))

    small_grads = dict(zip(
        _SMALL, (dg_mix, dg_bforget, g_ssm[0][None], g_ssm[1][None], g_ssm[2][None], g_ssm[3][None], g_ssm[4][None],
                 g_ssm[5][None], g_ssm[6][None], dg_dskip, dg_memq, dg_memkv, dg_ffn, dg_final.reshape(d))))
    small_like = [weights[n] for n in _SMALL]
    small_all = _gather_all("gather_small_grads", [_pack([small_grads[n] for n in _SMALL])])[0]
    pk = [_pack([src[n] for n in _SMALL]) for src in (weights, mom_m, mom_v)]
    small_out = _adamw("adamw_small", small_all, pk[0], pk[1], pk[2], tr=small_all.shape[1])
    results = [dict(zip(_SMALL, _unpack(buf, small_like))) for buf in small_out]
    tiles = {"w_in": 128, "w_glu": 128, "w_fox_o": 128, "w_mix_out": 128, "w_mem_q": 128, "w_mem_kv": 128,
             "w_mem_o": 128, "w_ffn_in": 128, "w_ffn_out": 176}
    pads = {"w_in": SHARD_IN_PAD, "w_ffn_in": SHARD_FFN_PAD}
    outs = small_out
    for name in _SHARDED[1:] + _SHARDED[:1]:
        if name == "w_in":
            received[name] = _send_wait("scatter_late_wait", lsend, lrecv, late_thru, late_lands, outs[0],
                                        scatter=True)[0]
        parts = received[name]
        w2, m2, v2 = weights[name][0], mom_m[name][0], mom_v[name][0]
        cols = w2.shape[1]
        if name in pads:
            w2, m2, v2 = (padcols(t, pads[name]) for t in (w2, m2, v2))
        outs = _adamw("adamw_" + name, parts, w2, m2, v2, tr=tiles[name])
        for res, o in zip(results, outs):
            res[name] = o[:, :cols][None]

    loss = lax.psum(loss_part[0, 0], ("x", "y", "c"))
    out = [loss, dx[None]]
    for res in results:
        out.extend(res[n] for n in _WEIGHTS)
    return tuple(out)
```

```python
import math

import jax
import jax.numpy as jnp
import numpy as np
from jax import lax
from jax.experimental import pallas as pl
from jax.experimental.pallas import tpu as pltpu

F32 = jnp.float32
BF16 = jnp.bfloat16

N_DEV = 8
LANE = 128
VMEM_LIMIT = 56 * 1024 * 1024

D_MODEL = 1024
SSM_GROUP = 16
SSM_GROUPS = 32
SSM_WIDTH = 512
SSM_STATE = 64
SSM_CHUNK = 8
FOX_HEADS = 8
FOX_HEAD_DIM = 64
FOX_WIDTH = 512
MEM_HEADS = 4
MEM_HEAD_DIM = 128
MEM_WIDTH = 512
FFN_HIDDEN = 2816
RMS_EPS = 1e-6
IN_WIDTH = 4104
SHARD_IN = IN_WIDTH // N_DEV
SHARD_IN_PAD = 640
SHARD_FFN = 2 * FFN_HIDDEN // N_DEV
SHARD_FFN_PAD = 768
PROJ_GATE0 = 2048
PROJ_F0 = 4096
PROJ_WIDTH = 4224

ADAM_LR = 0.001
ADAM_B1 = 0.9
ADAM_B2 = 0.999
ADAM_EPS = 1e-08
ADAM_WD = 0.01
ADAM_STEP = 10


def _cparams(sem=None):
    return pltpu.CompilerParams(dimension_semantics=sem, vmem_limit_bytes=VMEM_LIMIT)


def _sigmoid(x):
    return 1.0 / (1.0 + jnp.exp(-x))


def _bdot(a, b, dims):
    return lax.dot_general(a.astype(BF16), b.astype(BF16), ((dims[0], dims[1]), ((), ())),
                           preferred_element_type=F32)


_DIMS = {"nn": ((1,), (0,)), "nt": ((1,), (1,)), "tn": ((0,), (0,))}


def _matmul(name, a, b, mode, m, n, k, *, out_dtype, tm, tn, tk, a_off=(0, 0), b_off=(0, 0), add=None):
    tm, tn, tk = min(tm, m), min(tn, n), min(tk, k)
    assert m % tm == 0 and n % tn == 0 and k % tk == 0, (name, m, n, k, tm, tn, tk)
    nk = k // tk
    grid = (m // tm, n // tn, nk)

    def blk(off, t):
        assert off % t == 0, (name, off, t)
        return off // t

    if mode in ("nn", "nt"):
        ar, ac = blk(a_off[0], tm), blk(a_off[1], tk)
        a_spec = pl.BlockSpec((tm, tk), lambda i, j, kk: (i + ar, kk + ac))
    else:
        ar, ac = blk(a_off[0], tk), blk(a_off[1], tm)
        a_spec = pl.BlockSpec((tk, tm), lambda i, j, kk: (kk + ar, i + ac))

    if mode in ("nn", "tn"):
        br, bc = blk(b_off[0], tk), blk(b_off[1], tn)
        b_spec = pl.BlockSpec((tk, tn), lambda i, j, kk: (kk + br, j + bc))
    else:
        br, bc = blk(b_off[0], tn), blk(b_off[1], tk)
        b_spec = pl.BlockSpec((tn, tk), lambda i, j, kk: (j + br, kk + bc))
    o_spec = pl.BlockSpec((tm, tn), lambda i, j, kk: (i, j))
    out_shape = jax.ShapeDtypeStruct((m, n), out_dtype)

    in_specs = [a_spec, b_spec]
    operands = [a, b]
    if add is not None:
        in_specs.append(pl.BlockSpec((tm, tn), lambda i, j, kk: (i, j)))
        operands.append(add)
    dims = _DIMS[mode]
    has_add = add is not None

    def body(*refs):
        a_ref, b_ref = refs[0], refs[1]
        add_ref = refs[2] if has_add else None
        o_ref = refs[3] if has_add else refs[2]
        acc_ref = refs[-1] if nk > 1 else None
        prod = _bdot(a_ref[...], b_ref[...], dims)

        def finish(total):
            if has_add:
                total = total + add_ref[...].astype(F32)
            o_ref[...] = total.astype(o_ref.dtype)

        if nk == 1:
            finish(prod)
        else:
            kk = pl.program_id(2)

            @pl.when(kk == 0)
            def _():
                acc_ref[...] = prod

            @pl.when(jnp.logical_and(kk > 0, kk < nk - 1))
            def _():
                acc_ref[...] += prod

            @pl.when(kk == nk - 1)
            def _():
                finish(acc_ref[...] + prod)

    scratch = [pltpu.VMEM((tm, tn), F32)] if nk > 1 else []
    return pl.pallas_call(
        body, name=name, grid=grid, in_specs=in_specs, out_specs=o_spec, out_shape=out_shape,
        scratch_shapes=scratch,
        compiler_params=_cparams(("parallel", "parallel", "arbitrary")),
    )(*operands)


def _rms_fwd(name, x, gain, *, tr=512, after=None):
    r, d = x.shape
    tr = min(tr, r)

    def body(x_ref, g_ref, *rest):
        o_ref = rest[-1]
        xv = x_ref[...]
        rstd = lax.rsqrt(jnp.mean(xv * xv, axis=-1, keepdims=True) + RMS_EPS)
        o_ref[...] = (xv * rstd * g_ref[...]).astype(o_ref.dtype)

    in_specs = [pl.BlockSpec((tr, d), lambda i: (i, 0)), pl.BlockSpec((1, d), lambda i: (0, 0))]
    ops = [x, gain]
    if after is not None:
        in_specs.append(pl.BlockSpec(after.shape, lambda i: (0, 0)))
        ops.append(after)
    return pl.pallas_call(
        body, name=name, grid=(r // tr,), in_specs=in_specs,
        out_specs=pl.BlockSpec((tr, d), lambda i: (i, 0)),
        out_shape=jax.ShapeDtypeStruct((r, d), BF16),
        compiler_params=_cparams(("parallel",)),
    )(*ops)


def _rms_gain_grad(name, dy, x, *, tr=512):
    r, d = x.shape
    tr = min(tr, r)
    n = r // tr

    def body(dy_ref, x_ref, dg_ref, acc_ref):
        i = pl.program_id(0)
        xv = x_ref[...]
        xh = xv * lax.rsqrt(jnp.mean(xv * xv, axis=-1, keepdims=True) + RMS_EPS)
        part = (dy_ref[...].astype(F32) * xh).reshape(tr // 8, 8, d).sum(axis=0)

        @pl.when(i == 0)
        def _():
            acc_ref[...] = part

        @pl.when(i > 0)
        def _():
            acc_ref[...] += part

        @pl.when(i == n - 1)
        def _():
            dg_ref[...] = jnp.sum(acc_ref[...], axis=0, keepdims=True)

    row = pl.BlockSpec((tr, d), lambda i: (i, 0))
    return pl.pallas_call(
        body, name=name, grid=(n,), in_specs=[row, row],
        out_specs=pl.BlockSpec((1, d), lambda i: (0, 0)),
        out_shape=jax.ShapeDtypeStruct((1, d), F32),
        scratch_shapes=[pltpu.VMEM((8, d), F32)],
        compiler_params=_cparams(("arbitrary",)),
    )(dy, x)


def _matmul_rms_bwd(name, a, b, k, x, gain, res, *, tm=512, tk=1024, after=None):
    m, d = x.shape
    tm, tk = min(tm, m), min(tk, k)
    assert m % tm == 0 and k % tk == 0, (name, m, k, tm, tk)
    ni, nk = m // tm, k // tk

    def body(a_ref, b_ref, x_ref, g_ref, res_ref, *rest):
        dx_ref, dg_ref, acc_ref, accg_ref = rest[-4:]
        i, kk = pl.program_id(0), pl.program_id(1)
        prod = _bdot(a_ref[...], b_ref[...], _DIMS["nt"])

        @pl.when(kk == 0)
        def _():
            acc_ref[...] = prod

        @pl.when(kk > 0)
        def _():
            acc_ref[...] += prod

        @pl.when(kk == nk - 1)
        def _():
            dyv = acc_ref[...]
            xv = x_ref[...]
            rstd = lax.rsqrt(jnp.mean(xv * xv, axis=-1, keepdims=True) + RMS_EPS)
            xh = xv * rstd
            dxh = dyv * g_ref[...]
            dx_ref[...] = rstd * (dxh - xh * jnp.mean(dxh * xh, axis=-1, keepdims=True)) + res_ref[...]
            part = (dyv * xh).reshape(tm // 8, 8, d).sum(axis=0)

            @pl.when(i == 0)
            def _():
                accg_ref[...] = part

            @pl.when(i > 0)
            def _():
                accg_ref[...] += part

            @pl.when(i == ni - 1)
            def _():
                dg_ref[...] = jnp.sum(accg_ref[...], axis=0, keepdims=True)

    row = pl.BlockSpec((tm, d), lambda i, kk: (i, 0))
    one = pl.BlockSpec((1, d), lambda i, kk: (0, 0))
    in_specs = [pl.BlockSpec((tm, tk), lambda i, kk: (i, kk)), pl.BlockSpec((d, tk), lambda i, kk: (0, kk)), row, one, row]
    ops = [a, b, x, gain, res]
    if after is not None:
        in_specs.append(pl.BlockSpec(after.shape, lambda i, kk: (0, 0)))
        ops.append(after)
    return pl.pallas_call(
        body, name=name, grid=(ni, nk), in_specs=in_specs, out_specs=(row, one),
        out_shape=(jax.ShapeDtypeStruct((m, d), F32), jax.ShapeDtypeStruct((1, d), F32)),
        scratch_shapes=[pltpu.VMEM((tm, d), F32), pltpu.VMEM((8, d), F32)],
        compiler_params=_cparams(("arbitrary", "arbitrary")),
    )(*ops)


def _final_loss(name, h, target, gain, *, tr=512):
    r, d = h.shape
    tr = min(tr, r)
    n = r // tr

    def body(h_ref, t_ref, g_ref, loss_ref, dh_ref, dg_ref, accl_ref, accg_ref):
        i = pl.program_id(0)
        xv = h_ref[...]
        rstd = lax.rsqrt(jnp.mean(xv * xv, axis=-1, keepdims=True) + RMS_EPS)
        xh = xv * rstd
        e = xh * g_ref[...] - t_ref[...]
        dyv = e * (1.0 / d)
        dxh = dyv * g_ref[...]
        dh_ref[...] = rstd * (dxh - xh * jnp.mean(dxh * xh, axis=-1, keepdims=True))
        lpart = (e * e).reshape(tr // 8, 8, d).sum(axis=0)
        gpart = (dyv * xh).reshape(tr // 8, 8, d).sum(axis=0)

        @pl.when(i == 0)
        def _():
            accl_ref[...] = lpart
            accg_ref[...] = gpart

        @pl.when(i > 0)
        def _():
            accl_ref[...] += lpart
            accg_ref[...] += gpart

        @pl.when(i == n - 1)
        def _():
            tot = jnp.sum(jnp.sum(accl_ref[...], axis=0, keepdims=True), axis=1, keepdims=True)
            loss_ref[...] = jnp.broadcast_to(tot * (0.5 / d), (1, LANE))
            dg_ref[...] = jnp.sum(accg_ref[...], axis=0, keepdims=True)

    row = pl.BlockSpec((tr, d), lambda i: (i, 0))
    one = pl.BlockSpec((1, d), lambda i: (0, 0))
    return pl.pallas_call(
        body, name=name, grid=(n,), in_specs=[row, row, one],
        out_specs=(pl.BlockSpec((1, LANE), lambda i: (0, 0)), row, one),
        out_shape=(jax.ShapeDtypeStruct((1, LANE), F32), jax.ShapeDtypeStruct((r, d), F32),
                   jax.ShapeDtypeStruct((1, d), F32)),
        scratch_shapes=[pltpu.VMEM((8, d), F32), pltpu.VMEM((8, d), F32)],
        compiler_params=_cparams(("arbitrary",)),
    )(h, target, gain)


_GELU_C = math.sqrt(2.0 / math.pi)


def _gelu_parts(z):
    inner = _GELU_C * (z + 0.044715 * z * z * z)
    t = jnp.tanh(inner)
    val = 0.5 * z * (1.0 + t)
    dinner = _GELU_C * (1.0 + 3.0 * 0.044715 * z * z)
    grad = 0.5 * (1.0 + t) + 0.5 * z * (1.0 - t * t) * dinner
    return val, grad


def _ssm_post_fwd(name, y8, u8, d8, *, tr=256):
    r, c = y8.shape
    tr = min(tr, r)

    def body(y_ref, u_ref, d_ref, o_ref):
        z = y_ref[...] + d_ref[...] * u_ref[...]
        o_ref[...] = _gelu_parts(z)[0].astype(o_ref.dtype)

    row = pl.BlockSpec((tr, c), lambda i: (i, 0))
    return pl.pallas_call(
        body, name=name, grid=(r // tr,), in_specs=[row, row, pl.BlockSpec((1, c), lambda i: (0, 0))],
        out_specs=row, out_shape=jax.ShapeDtypeStruct((r, c), BF16),
        compiler_params=_cparams(("parallel",)),
    )(y8, u8, d8)


def _ssm_post_bwd(name, dact8, y8, u8, d8, *, tr=256, after=None):
    r, c = y8.shape
    tr = min(tr, r)
    n = r // tr

    def body(*refs):
        da_ref, y_ref, u_ref, d_ref = refs[:4]
        dz_ref, dd_ref, acc_ref = refs[-3:]
        i = pl.program_id(0)
        uv = u_ref[...]
        z = y_ref[...] + d_ref[...] * uv
        dz = da_ref[...].astype(F32) * _gelu_parts(z)[1]
        dz_ref[...] = dz
        part = (dz * uv).reshape(tr // 8, 8, c).sum(axis=0)

        @pl.when(i == 0)
        def _():
            acc_ref[...] = part

        @pl.when(i > 0)
        def _():
            acc_ref[...] += part

        @pl.when(i == n - 1)
        def _():
            tot = jnp.sum(acc_ref[...], axis=0, keepdims=True)
            out = tot[:, 0:SSM_WIDTH]
            for j in range(1, c // SSM_WIDTH):
                out = out + tot[:, j * SSM_WIDTH:(j + 1) * SSM_WIDTH]
            dd_ref[...] = out

    row = pl.BlockSpec((tr, c), lambda i: (i, 0))
    in_specs = [row, row, row, pl.BlockSpec((1, c), lambda i: (0, 0))]
    ops = [dact8, y8, u8, d8]
    if after is not None:
        in_specs.append(pl.BlockSpec(memory_space=pl.ANY))
        ops.append(after)
    return pl.pallas_call(
        body, name=name, grid=(n,), in_specs=in_specs,
        out_specs=(row, pl.BlockSpec((1, SSM_WIDTH), lambda i: (0, 0))),
        out_shape=(jax.ShapeDtypeStruct((r, c), F32), jax.ShapeDtypeStruct((1, SSM_WIDTH), F32)),
        scratch_shapes=[pltpu.VMEM((8, c), F32)],
        compiler_params=_cparams(("arbitrary",)),
    )(*ops)


def _mix_fwd(name, glu, gates, out_b, *, tr=256):
    r = glu.shape[0]
    d = D_MODEL
    tr = min(tr, r)

    def body(glu_ref, gate_ref, ob_ref, o_ref):
        out_a = glu_ref[:, 0:d].astype(F32) * _sigmoid(glu_ref[:, d:2 * d].astype(F32))
        mix = (_sigmoid(gate_ref[:, 0:d].astype(F32)) * out_a
               + _sigmoid(gate_ref[:, d:2 * d].astype(F32)) * ob_ref[...].astype(F32))
        o_ref[...] = mix.astype(o_ref.dtype)

    wide = pl.BlockSpec((tr, 2 * d), lambda i: (i, 0))
    row = pl.BlockSpec((tr, d), lambda i: (i, 0))
    return pl.pallas_call(
        body, name=name, grid=(r // tr,), in_specs=[wide, wide, row], out_specs=row,
        out_shape=jax.ShapeDtypeStruct((r, d), BF16), compiler_params=_cparams(("parallel",)),
    )(glu, gates, out_b)


def _mix_bwd(name, dmix, glu, gates, out_b, *, tr=256):
    r = glu.shape[0]
    d = D_MODEL
    tr = min(tr, r)

    def body(dm_ref, glu_ref, gate_ref, ob_ref, dglu_ref, dgate_ref, dob_ref):
        dm = dm_ref[...]
        glu_a = glu_ref[:, 0:d].astype(F32)
        sb = _sigmoid(glu_ref[:, d:2 * d].astype(F32))
        ga = _sigmoid(gate_ref[:, 0:d].astype(F32))
        gb = _sigmoid(gate_ref[:, d:2 * d].astype(F32))
        out_a = glu_a * sb
        dout_a = dm * ga
        dglu_ref[:, 0:d] = (dout_a * sb).astype(dglu_ref.dtype)
        dglu_ref[:, d:2 * d] = (dout_a * glu_a * sb * (1.0 - sb)).astype(dglu_ref.dtype)
        dgate_ref[:, 0:d] = (dm * out_a * ga * (1.0 - ga)).astype(dgate_ref.dtype)
        dgate_ref[:, d:2 * d] = (dm * ob_ref[...].astype(F32) * gb * (1.0 - gb)).astype(dgate_ref.dtype)
        dob_ref[...] = (dm * gb).astype(dob_ref.dtype)

    wide = pl.BlockSpec((tr, 2 * d), lambda i: (i, 0))
    row = pl.BlockSpec((tr, d), lambda i: (i, 0))
    return pl.pallas_call(
        body, name=name, grid=(r // tr,), in_specs=[row, wide, wide, row], out_specs=(wide, wide, row),
        out_shape=(jax.ShapeDtypeStruct((r, 2 * d), BF16), jax.ShapeDtypeStruct((r, 2 * d), BF16),
                   jax.ShapeDtypeStruct((r, d), BF16)),
        compiler_params=_cparams(("parallel",)),
    )(dmix, glu, gates, out_b)


def _ssm_mats(lam_re, lam_im, log_dt, b_re, b_im, c_re, c_im, nc):
    hp = lax.Precision.HIGHEST
    t = SSM_CHUNK
    nq = SSM_GROUPS // 8
    lam = lax.complex(lam_re, lam_im)
    z = lam * jnp.exp(log_dt)[:, None]
    ks = jnp.arange(t + 1, dtype=F32)
    apow = jnp.exp(ks[:, None, None] * z[None])
    bbar = ((apow[1] - 1.0) / lam)[..., None] * lax.complex(b_re, b_im)
    c = lax.complex(c_re, c_im)

    ca = c[None] * apow[:, :, None, :]
    kmat = jnp.einsum("kgnp,gpm->kgnm", ca, bbar, precision=hp).real
    ii = np.arange(t)
    lag = ii[None, :] - ii[:, None]
    kt = kmat[np.clip(lag, 0, t)] * jnp.asarray(lag >= 0, F32)[:, :, None, None, None]
    kt = kt.reshape(t, t, nq, 8, SSM_GROUP, SSM_GROUP)
    m_c = kt.transpose(2, 0, 3, 5, 1, 4).reshape(nq, 1024, LANE)

    arev = jnp.exp((float(t - 1) - ks[:t])[:, None, None] * z[None])
    w = arev[:, :, :, None] * bbar[None]
    wr = jnp.stack([w.real, w.imag]).reshape(2, t, nq, 8, SSM_STATE, SSM_GROUP)
    bw_c = wr.transpose(2, 1, 3, 5, 0, 4).reshape(nq, 1024, LANE)

    ca1 = ca[1:]
    cr = jnp.stack([ca1.real, -ca1.imag]).reshape(2, t, nq, 8, SSM_GROUP, SSM_STATE)
    cm_c = cr.transpose(2, 0, 3, 5, 1, 4).reshape(nq, 1024, LANE)

    def tiles(v):
        vq = jnp.concatenate([v.real.reshape(nq, 512), v.imag.reshape(nq, 512)], axis=1)
        return jnp.broadcast_to(vq.reshape(nq, 8, 1, LANE), (nq, 8, 8, LANE))

    return m_c, bw_c, cm_c, tiles(apow[t]), tiles(jnp.exp(float(nc) * z))


_BD_M = (LANE, SSM_GROUP)
_BD_BW = (LANE, SSM_STATE)
_BD_CM = (512, SSM_GROUP)


def _bd_perm(cn):
    rr = lax.broadcasted_iota(jnp.int32, (1024, 1024), 0)
    cc = lax.broadcasted_iota(jnp.int32, (1024, 1024), 1)
    sh = cn.bit_length() - 1
    src = ((rr >> 7) << sh) + (((rr & (LANE - 1)) >> sh) << (3 + sh)) + (rr & (cn - 1))
    return jnp.where(src == cc, 1.0, 0.0).astype(BF16)


def _bd_rowgroup(span):
    r = lax.broadcasted_iota(jnp.int32, (1024, LANE), 0)
    return (r & (span - 1)) >> ((span // 8).bit_length() - 1)


def _bd_expand(name, kind, compact):
    span, cn = kind
    nq = compact.shape[0]

    def body(c_ref, o_ref, perm_scr):
        @pl.when(pl.program_id(0) == 0)
        def _():
            perm_scr[...] = _bd_perm(cn)

        x = c_ref[...]
        grp = _bd_rowgroup(span)
        xcat = jnp.concatenate([jnp.where(grp == h, x, 0.0) for h in range(8)], axis=1)
        o_ref[...] = _bdot(xcat, perm_scr[...], _DIMS["nn"]).astype(o_ref.dtype)

    return pl.pallas_call(
        body, name=name, grid=(nq,), in_specs=[pl.BlockSpec((None, 1024, LANE), lambda q: (q, 0, 0))],
        out_specs=pl.BlockSpec((None, 1024, 1024), lambda q: (q, 0, 0)),
        out_shape=jax.ShapeDtypeStruct((nq, 1024, 1024), BF16),
        scratch_shapes=[pltpu.VMEM((1024, 1024), BF16)],
        compiler_params=_cparams(("arbitrary",)),
    )(compact)


def _bd_reduce(name, kind, dbig):
    span, cn = kind
    nq = dbig.shape[0]

    def body(g_ref, o_ref, perm_scr):
        @pl.when(pl.program_id(0) == 0)
        def _():
            perm_scr[...] = _bd_perm(cn)

        back = _bdot(g_ref[...], perm_scr[...], _DIMS["nt"])
        grp = _bd_rowgroup(span)
        out = jnp.zeros((1024, LANE), F32)
        for h in range(8):
            out = jnp.where(grp == h, back[:, h * LANE:(h + 1) * LANE], out)
        o_ref[...] = out

    return pl.pallas_call(
        body, name=name, grid=(nq,), in_specs=[pl.BlockSpec((None, 1024, 1024), lambda q: (q, 0, 0))],
        out_specs=pl.BlockSpec((None, 1024, LANE), lambda q: (q, 0, 0)),
        out_shape=jax.ShapeDtypeStruct((nq, 1024, LANE), F32),
        scratch_shapes=[pltpu.VMEM((1024, 1024), BF16)],
        compiler_params=_cparams(("arbitrary",)),
    )(dbig)


def _x_tile_specs(nc, nq):
    return [pl.BlockSpec((nc, LANE), lambda q, t, i=i: (0, i * nq + q)) for i in range(SSM_CHUNK)]


def _cat_tiles(refs):
    return jnp.concatenate([r[...] for r in refs], axis=1)


def _ssm_w(name, x8, bw):
    nc = x8.shape[0]
    nq = bw.shape[0]

    def body(*refs):
        xq = _cat_tiles(refs[:8])
        refs[9][...] = _bdot(xq, refs[8][...], _DIMS["nn"])

    return pl.pallas_call(
        body, name=name, grid=(nq, 8),
        in_specs=_x_tile_specs(nc, nq) + [pl.BlockSpec((None, 1024, LANE), lambda q, t: (q, 0, t))],
        out_specs=pl.BlockSpec((None, None, nc, LANE), lambda q, t: (q, t, 0, 0)),
        out_shape=jax.ShapeDtypeStruct((nq, 8, nc, LANE), F32),
        compiler_params=_cparams(("parallel", "arbitrary")),
    )(*([x8] * 8), bw)


def _ssm_scan(name, w4, a_t, aseg_t, *, reverse, sprev4=None):
    nq, _, nc, _ = w4.shape
    ns = nc // 8
    with_da = sprev4 is not None

    def body(*refs):
        w_ref, a_ref, aseg_ref = refs[:3]
        s_ref = refs[3] if with_da else None
        o_ref = refs[4] if with_da else refs[3]
        da_ref = refs[5] if with_da else None
        sgn = -1.0 if reverse else 1.0
        ar = [a_ref[j] for j in range(4)]
        ai = [sgn * a_ref[j + 4] for j in range(4)]
        gr = [aseg_ref[j] for j in range(4)]
        gi = [sgn * aseg_ref[j + 4] for j in range(4)]
        zero = tuple(jnp.zeros((8, LANE), F32) for _ in range(8))

        def rows(tt):
            return pl.ds((ns - 1 - tt) if reverse else tt, 8, stride=ns)

        def step(carry, w):
            new_r = [ar[j] * carry[j] - ai[j] * carry[j + 4] + w[j] for j in range(4)]
            new_i = [ar[j] * carry[j + 4] + ai[j] * carry[j] + w[j + 4] for j in range(4)]
            return tuple(new_r + new_i)

        def pass1(tt, carry):
            return step(carry, [w_ref[j, rows(tt), :] for j in range(8)])

        ends = lax.fori_loop(0, ns, pass1, zero)
        sub = lax.broadcasted_iota(jnp.int32, (8, LANE), 0)
        init = list(zero)
        order = range(7, 0, -1) if reverse else range(0, 7)
        for s in order:
            nxt = s - 1 if reverse else s + 1
            cand_r = [gr[j] * init[j] - gi[j] * init[j + 4] + ends[j] for j in range(4)]
            cand_i = [gr[j] * init[j + 4] + gi[j] * init[j] + ends[j + 4] for j in range(4)]
            cand = cand_r + cand_i
            shift = 7 if reverse else 1
            init = [jnp.where(sub == nxt, pltpu.roll(cand[j], shift, axis=0), init[j]) for j in range(8)]

        def pass2(tt, state):
            carry, acc = state
            r = rows(tt)
            for j in range(8):
                o_ref[j, r, :] = carry[j]
            if with_da:
                sp = [s_ref[j, r, :] for j in range(8)]
                acc_r = [acc[j] + carry[j] * sp[j] + carry[j + 4] * sp[j + 4] for j in range(4)]
                acc_i = [acc[j + 4] + carry[j + 4] * sp[j] - carry[j] * sp[j + 4] for j in range(4)]
                acc = tuple(acc_r + acc_i)
            return step(carry, [w_ref[j, r, :] for j in range(8)]), acc

        _, acc = lax.fori_loop(0, ns, pass2, (tuple(init), zero))
        if with_da:
            for j in range(8):
                da_ref[j] = acc[j]

    big = pl.BlockSpec((None, 8, nc, LANE), lambda q: (q, 0, 0, 0))
    small = pl.BlockSpec((None, 8, 8, LANE), lambda q: (q, 0, 0, 0))
    in_specs = [big, small, small] + ([big] if with_da else [])
    ops = [w4, a_t, aseg_t] + ([sprev4] if with_da else [])
    out_specs = (big, small) if with_da else big
    big_s = jax.ShapeDtypeStruct((nq, 8, nc, LANE), F32)
    out_shape = (big_s, jax.ShapeDtypeStruct((nq, 8, 8, LANE), F32)) if with_da else big_s
    return pl.pallas_call(
        body, name=name, grid=(nq,), in_specs=in_specs, out_specs=out_specs, out_shape=out_shape,
        compiler_params=_cparams(("parallel",)),
    )(*ops)


def _ssm_y(name, x8, sprev4, m_mat, cm_mat):
    nc = x8.shape[0]
    nq = m_mat.shape[0]

    def body(*refs):
        xq = _cat_tiles(refs[:8])
        s_ref, m_ref, cm_ref, o_ref = refs[8:12]
        sq = jnp.concatenate([s_ref[t] for t in range(8)], axis=1)
        o_ref[...] = _bdot(xq, m_ref[...], _DIMS["nn"]) + _bdot(sq, cm_ref[...], _DIMS["nn"])

    col = pl.BlockSpec((None, 1024, LANE), lambda q, j: (q, 0, j))
    return pl.pallas_call(
        body, name=name, grid=(nq, 8),
        in_specs=_x_tile_specs(nc, nq) + [pl.BlockSpec((None, 8, nc, LANE), lambda q, j: (q, 0, 0, 0)), col, col],
        out_specs=pl.BlockSpec((nc, LANE), lambda q, j: (0, j * nq + q)),
        out_shape=jax.ShapeDtypeStruct((nc, 8 * SSM_WIDTH), F32),
        compiler_params=_cparams(("parallel", "arbitrary")),
    )(*([x8] * 8), sprev4, m_mat, cm_mat)


def _ssm_ds(name, dz8, sprev4, cm_mat):
    nc = dz8.shape[0]
    nq = cm_mat.shape[0]

    def body(*refs):
        dyq = _cat_tiles(refs[:8]).astype(BF16)
        s_ref, cm_ref, ds_ref, dcm_ref = refs[8:12]
        ds_ref[...] = _bdot(dyq, cm_ref[...], _DIMS["nt"])
        dcm_ref[...] = _bdot(s_ref[...], dyq, _DIMS["tn"])

    tile = pl.BlockSpec((None, None, nc, LANE), lambda q, t: (q, t, 0, 0))
    rowblk = pl.BlockSpec((None, LANE, 1024), lambda q, t: (q, t, 0))
    return pl.pallas_call(
        body, name=name, grid=(nq, 8),
        in_specs=_x_tile_specs(nc, nq) + [tile, rowblk],
        out_specs=(tile, rowblk),
        out_shape=(jax.ShapeDtypeStruct((nq, 8, nc, LANE), F32), jax.ShapeDtypeStruct((nq, 1024, 1024), F32)),
        compiler_params=_cparams(("parallel", "arbitrary")),
    )(*([dz8] * 8), sprev4, cm_mat)


def _ssm_dx(name, dz8, g4, x8, m_mat, bw_mat, d8):
    nc = dz8.shape[0]
    nq = m_mat.shape[0]

    def body(*refs):
        dyq = _cat_tiles(refs[:8]).astype(BF16)
        g_ref, x_ref, m_ref, bw_ref, d_ref, dzi_ref, dx_ref, dm_ref, dbw_ref = refs[8:17]
        gq = jnp.concatenate([g_ref[t] for t in range(8)], axis=1).astype(BF16)
        dx = _bdot(dyq, m_ref[...], _DIMS["nt"]) + _bdot(gq, bw_ref[...], _DIMS["nt"])
        dx_ref[...] = (dx + d_ref[...] * dzi_ref[...]).astype(dx_ref.dtype)
        xi = x_ref[...]
        dm_ref[...] = _bdot(xi, dyq, _DIMS["tn"])
        dbw_ref[...] = _bdot(xi, gq, _DIMS["tn"])

    xtile = pl.BlockSpec((nc, LANE), lambda q, i: (0, i * nq + q))
    rowblk = pl.BlockSpec((None, LANE, 1024), lambda q, i: (q, i, 0))
    return pl.pallas_call(
        body, name=name, grid=(nq, 8),
        in_specs=_x_tile_specs(nc, nq) + [pl.BlockSpec((None, 8, nc, LANE), lambda q, i: (q, 0, 0, 0)), xtile, rowblk, rowblk,
                                          pl.BlockSpec((1, LANE), lambda q, i: (0, q)), xtile],
        out_specs=(xtile, rowblk, rowblk),
        out_shape=(jax.ShapeDtypeStruct((nc, 8 * SSM_WIDTH), BF16), jax.ShapeDtypeStruct((nq, 1024, 1024), F32),
                   jax.ShapeDtypeStruct((nq, 1024, 1024), F32)),
        compiler_params=_cparams(("parallel", "arbitrary")),
    )(*([dz8] * 8), g4, x8, m_mat, bw_mat, d8, dz8)


CUM_BLK = 256


def _split3(x):
    hi = x.astype(BF16)
    r1 = x - hi.astype(F32)
    mid = r1.astype(BF16)
    lo = (r1 - mid.astype(F32)).astype(BF16)
    return hi, mid, lo


def _tri_dot(x, tri):
    hi, mid, lo = _split3(x)
    d = _DIMS["nn"]
    return _bdot(hi, tri, d) + _bdot(mid, tri, d) + _bdot(lo, tri, d)


def _tri(n, lower):
    r = lax.broadcasted_iota(jnp.int32, (n, n), 0)
    c = lax.broadcasted_iota(jnp.int32, (n, n), 1)
    return jnp.where((r >= c) if lower else (r <= c), 1.0, 0.0).astype(BF16)


def _fox_cum(name, fproj, bcol):
    seq = fproj.shape[0]
    blk = min(CUM_BLK, seq)

    def body(f_ref, b_ref, o_ref, carry_ref):
        i = pl.program_id(0)

        @pl.when(i == 0)
        def _():
            carry_ref[...] = jnp.zeros_like(carry_ref)

        z = f_ref[...].T + b_ref[...]
        logf = jnp.minimum(z, 0.0) - jnp.log(1.0 + jnp.exp(-jnp.abs(z)))
        carry = carry_ref[...]
        cum = _tri_dot(logf, _tri(blk, lower=False)) + jnp.tile(carry, (1, blk // LANE))
        o_ref[...] = cum[0:8, :]
        carry_ref[...] = carry + jnp.sum(logf, axis=1, keepdims=True)

    return pl.pallas_call(
        body, name=name, grid=(seq // blk,),
        in_specs=[pl.BlockSpec((blk, LANE), lambda i: (i, 0)), pl.BlockSpec((LANE, 1), lambda i: (0, 0))],
        out_specs=pl.BlockSpec((8, blk), lambda i: (0, i)),
        out_shape=jax.ShapeDtypeStruct((8, seq), F32),
        scratch_shapes=[pltpu.VMEM((LANE, LANE), F32)],
        compiler_params=_cparams(("arbitrary",)),
    )(fproj, bcol)


def _fox_cum_bwd(name, dcs, fproj, bcol):
    seq = fproj.shape[0]
    blk = min(CUM_BLK, seq)
    n = seq // blk

    def body(dc_ref, f_ref, b_ref, df_ref, db_ref, carry_ref, acc_ref):
        i = pl.program_id(0)

        @pl.when(i == 0)
        def _():
            carry_ref[...] = jnp.zeros_like(carry_ref)
            acc_ref[...] = jnp.zeros_like(acc_ref)

        r = lax.broadcasted_iota(jnp.int32, (LANE, FOX_WIDTH), 0)
        c = lax.broadcasted_iota(jnp.int32, (LANE, FOX_WIDTH), 1)
        want = (r >> 1) * LANE + jnp.where((r & 1) == 0, FOX_HEAD_DIM, 0)
        sel = jnp.where(jnp.logical_and(r < FOX_HEADS, c == want), 1.0, 0.0).astype(BF16)
        hi, mid, lo = _split3(dc_ref[...])
        nt = _DIMS["nt"]
        dc = _bdot(sel, hi, nt) + _bdot(sel, mid, nt) + _bdot(sel, lo, nt)
        carry = carry_ref[...]
        dlogf = _tri_dot(dc, _tri(blk, lower=True)) + jnp.tile(carry, (1, blk // LANE))
        carry_ref[...] = carry + jnp.sum(dc, axis=1, keepdims=True)
        z = f_ref[...].T + b_ref[...]
        dft = dlogf / (1.0 + jnp.exp(z))
        df_ref[...] = dft.T.astype(df_ref.dtype)
        acc_ref[...] += jnp.sum(dft, axis=1, keepdims=True)

        @pl.when(i == n - 1)
        def _():
            db_ref[...] = acc_ref[...]

    return pl.pallas_call(
        body, name=name, grid=(n,),
        in_specs=[pl.BlockSpec((blk, FOX_WIDTH), lambda i: (n - 1 - i, 0)), pl.BlockSpec((blk, LANE), lambda i: (n - 1 - i, 0)),
                  pl.BlockSpec((LANE, 1), lambda i: (0, 0))],
        out_specs=(pl.BlockSpec((blk, LANE), lambda i: (n - 1 - i, 0)), pl.BlockSpec((LANE, LANE), lambda i: (0, 0))),
        out_shape=(jax.ShapeDtypeStruct((seq, LANE), BF16), jax.ShapeDtypeStruct((LANE, LANE), F32)),
        scratch_shapes=[pltpu.VMEM((LANE, LANE), F32), pltpu.VMEM((LANE, LANE), F32)],
        compiler_params=_cparams(("arbitrary",)),
    )(dcs, fproj, bcol)


FOX_BLK = 512
FOX_SCALE = FOX_HEAD_DIM ** -0.5


def _fox_head_mask(shape, hh):
    lane = lax.broadcasted_iota(jnp.int32, shape, 1)
    return (lane < FOX_HEAD_DIM) if hh == 0 else (lane >= FOX_HEAD_DIM)


def _fox_bias(cum_ref, hh, q0, k0, blk):
    c0 = jnp.max(cum_ref[hh:hh + 1, pl.ds(q0, LANE)], axis=1, keepdims=True)
    return c0 - cum_ref[hh:hh + 1, pl.ds(k0, blk)]


def _fox_fwd(name, qkv, cum_t):
    seq = qkv.shape[0]
    blk = min(FOX_BLK, seq)
    nb = seq // blk
    npair = FOX_HEADS // 2

    def body(q_ref, k_ref, v_ref, cum_ref, o_ref, lse_ref):
        iq = pl.program_id(1)
        q0 = pl.multiple_of(iq * blk, blk)
        qv = q_ref[...]
        row = lax.broadcasted_iota(jnp.int32, (blk, blk), 0)
        col = lax.broadcasted_iota(jnp.int32, (blk, blk), 1)
        qhs = [jnp.where(_fox_head_mask(qv.shape, hh), qv, jnp.zeros_like(qv)) * FOX_SCALE for hh in range(2)]

        def block(kb, states, masked):
            k0 = pl.multiple_of(kb * blk, blk)
            kv = k_ref[pl.ds(k0, blk), :]
            vv = v_ref[pl.ds(k0, blk), :]
            new = []
            for hh in range(2):
                m, acc = states[hh]
                s = _bdot(qhs[hh], kv, _DIMS["nt"]) + _fox_bias(cum_ref, hh, q0, k0, blk)
                if masked:
                    s = jnp.where(row >= col, s, -jnp.inf)
                m_new = jnp.maximum(m, jnp.max(s, axis=1, keepdims=True))
                p = jnp.exp(s - m_new)
                vh = jnp.where(_fox_head_mask(vv.shape, hh), vv, jnp.ones_like(vv))
                acc = jnp.exp(m - m_new) * acc + _bdot(p, vh, _DIMS["nn"])
                new.append((m_new, acc))
            return tuple(new)

        init = (jnp.full((blk, 1), -jnp.inf, F32), jnp.zeros((blk, LANE), F32))
        states = lax.fori_loop(0, iq, lambda kb, st: block(kb, st, False), (init, init))
        states = block(iq, states, True)
        outs = []
        for hh in range(2):
            m, acc = states[hh]
            other = pltpu.roll(acc, FOX_HEAD_DIM, axis=1)
            outs.append(acc / other)
            lse_ref[hh] = m + jnp.log(jnp.where(_fox_head_mask(acc.shape, hh), other, acc))
        o_ref[...] = jnp.where(_fox_head_mask(outs[0].shape, 0), outs[0], outs[1]).astype(o_ref.dtype)

    return pl.pallas_call(
        body, name=name, grid=(npair, nb),
        in_specs=[pl.BlockSpec((blk, LANE), lambda p, i: (i, p)),
                  pl.BlockSpec((seq, LANE), lambda p, i: (0, npair + p)),
                  pl.BlockSpec((seq, LANE), lambda p, i: (0, 2 * npair + p)),
                  pl.BlockSpec((None, 2, seq), lambda p, i: (p, 0, 0))],
        out_specs=(pl.BlockSpec((blk, LANE), lambda p, i: (i, p)),
                   pl.BlockSpec((2, blk, LANE), lambda p, i: (p, i, 0))),
        out_shape=(jax.ShapeDtypeStruct((seq, FOX_WIDTH), BF16), jax.ShapeDtypeStruct((FOX_HEADS, seq, LANE), F32)),
        compiler_params=_cparams(("parallel", "arbitrary")),
    )(qkv, qkv, qkv, cum_t)


def _fox_bwd(name, qkv, cum_t, att, datt, lse):
    seq = qkv.shape[0]
    blk = min(FOX_BLK, seq)
    nb = seq // blk
    npair = FOX_HEADS // 2

    def body(q_ref, k_ref, v_ref, cum_ref, o_ref, do_ref, lse_ref, dq_ref, dk_ref, dv_ref, dcs_ref):
        iq = pl.program_id(1)
        q0 = pl.multiple_of(iq * blk, blk)

        @pl.when(iq == 0)
        def _():
            dk_ref[...] = jnp.zeros_like(dk_ref)
            dv_ref[...] = jnp.zeros_like(dv_ref)
            dcs_ref[...] = jnp.zeros_like(dcs_ref)

        qv = q_ref[...]
        dov = do_ref[...].astype(F32)
        ov = o_ref[...].astype(F32)
        row = lax.broadcasted_iota(jnp.int32, (blk, blk), 0)
        col = lax.broadcasted_iota(jnp.int32, (blk, blk), 1)
        low = _fox_head_mask((blk, LANE), 0)
        qhs, qones, dohbs, deltas, lses = [], [], [], [], []
        for hh in range(2):
            hm = _fox_head_mask(qv.shape, hh)
            qh = jnp.where(hm, qv, jnp.zeros_like(qv)) * FOX_SCALE
            qhs.append(qh)
            qones.append(jnp.where(hm, qh, jnp.ones_like(qh)))
            doh = jnp.where(hm, dov, 0.0)
            dohbs.append(doh.astype(BF16))
            deltas.append(jnp.sum(doh * ov, axis=1, keepdims=True))
            lses.append(jnp.tile(lse_ref[hh], (1, blk // LANE)))

        def block(kb, dqs, masked):
            k0 = pl.multiple_of(kb * blk, blk)
            kv = k_ref[pl.ds(k0, blk), :]
            vv = v_ref[pl.ds(k0, blk), :]
            new, dks, dvs = [], [], []
            for hh in range(2):
                s = _bdot(qhs[hh], kv, _DIMS["nt"]) + _fox_bias(cum_ref, hh, q0, k0, blk)
                p = jnp.exp(s - lses[hh])
                if masked:
                    p = jnp.where(row >= col, p, 0.0)
                dp = _bdot(dohbs[hh], vv, _DIMS["nt"])
                dsb = (p * (dp - deltas[hh])).astype(BF16)
                dks.append(_bdot(dsb, qones[hh], _DIMS["tn"]))
                dvs.append(_bdot(p, dohbs[hh], _DIMS["tn"]))
                kones = jnp.where(_fox_head_mask(kv.shape, hh), kv, jnp.ones_like(kv))
                new.append(dqs[hh] + _bdot(dsb, kones, _DIMS["nn"]))
            dk_ref[pl.ds(k0, blk), :] += jnp.where(low, dks[0], dks[1])
            dv_ref[pl.ds(k0, blk), :] += dvs[0] + dvs[1]
            dcs_ref[pl.ds(k0, blk), :] -= jnp.where(low, dks[1], dks[0])
            return tuple(new)

        init = jnp.zeros((blk, LANE), F32)
        dqs = lax.fori_loop(0, iq, lambda kb, a: block(kb, a, False), (init, init))
        dqs = block(iq, dqs, True)
        dcs_ref[pl.ds(q0, blk), :] += jnp.where(low, dqs[1], dqs[0])
        dq_ref[...] = (jnp.where(low, dqs[0], dqs[1]) * FOX_SCALE).astype(dq_ref.dtype)

    qblk = pl.BlockSpec((blk, LANE), lambda p, i: (i, p))
    full = pl.BlockSpec((seq, LANE), lambda p, i: (0, p))
    return pl.pallas_call(
        body, name=name, grid=(npair, nb),
        in_specs=[qblk,
                  pl.BlockSpec((seq, LANE), lambda p, i: (0, npair + p)),
                  pl.BlockSpec((seq, LANE), lambda p, i: (0, 2 * npair + p)),
                  pl.BlockSpec((None, 2, seq), lambda p, i: (p, 0, 0)),
                  qblk, qblk,
                  pl.BlockSpec((2, blk, LANE), lambda p, i: (p, i, 0))],
        out_specs=(qblk, full, full, full),
        out_shape=(jax.ShapeDtypeStruct((seq, FOX_WIDTH), BF16), jax.ShapeDtypeStruct((seq, FOX_WIDTH), F32),
                   jax.ShapeDtypeStruct((seq, FOX_WIDTH), F32), jax.ShapeDtypeStruct((seq, FOX_WIDTH), F32)),
        compiler_params=_cparams(("arbitrary", "arbitrary")),
    )(qkv, qkv, qkv, cum_t, att, datt, lse)


MEM_SCALE = MEM_HEAD_DIM ** -0.5


def _mem_probs(qh, kh):
    s = _bdot(qh, kh, _DIMS["nt"]) * MEM_SCALE
    p = jnp.exp(s - jnp.max(s, axis=1, keepdims=True))
    return p / jnp.sum(p, axis=1, keepdims=True)


def _mem_fwd(name, q2, kv, *, tr=512):
    seq = q2.shape[0]
    mlen = kv.shape[0]
    tr = min(tr, seq)

    def body(q_ref, kv_ref, o_ref):
        for h in range(MEM_HEADS):
            sl = slice(h * MEM_HEAD_DIM, (h + 1) * MEM_HEAD_DIM)
            sv = slice(MEM_WIDTH + h * MEM_HEAD_DIM, MEM_WIDTH + (h + 1) * MEM_HEAD_DIM)
            p = _mem_probs(q_ref[:, sl], kv_ref[:, sl])
            o_ref[:, sl] = _bdot(p, kv_ref[:, sv], _DIMS["nn"]).astype(o_ref.dtype)

    return pl.pallas_call(
        body, name=name, grid=(seq // tr,),
        in_specs=[pl.BlockSpec((tr, MEM_WIDTH), lambda i: (i, 0)), pl.BlockSpec((mlen, 2 * MEM_WIDTH), lambda i: (0, 0))],
        out_specs=pl.BlockSpec((tr, MEM_WIDTH), lambda i: (i, 0)),
        out_shape=jax.ShapeDtypeStruct((seq, MEM_WIDTH), BF16),
        compiler_params=_cparams(("parallel",)),
    )(q2, kv)


def _mem_bwd(name, q2, kv, do2, *, tr=512):
    seq = q2.shape[0]
    mlen = kv.shape[0]
    tr = min(tr, seq)

    def body(q_ref, kv_ref, do_ref, dq_ref, dkv_ref):
        i = pl.program_id(0)

        @pl.when(i == 0)
        def _():
            dkv_ref[...] = jnp.zeros_like(dkv_ref)

        for h in range(MEM_HEADS):
            sl = slice(h * MEM_HEAD_DIM, (h + 1) * MEM_HEAD_DIM)
            sv = slice(MEM_WIDTH + h * MEM_HEAD_DIM, MEM_WIDTH + (h + 1) * MEM_HEAD_DIM)
            qh = q_ref[:, sl]
            kh = kv_ref[:, sl]
            doh = do_ref[:, sl].astype(BF16)
            p = _mem_probs(qh, kh)
            dp = _bdot(doh, kv_ref[:, sv], _DIMS["nt"])
            ds = (p * (dp - jnp.sum(p * dp, axis=1, keepdims=True)) * MEM_SCALE).astype(BF16)
            dq_ref[:, sl] = _bdot(ds, kh, _DIMS["nn"]).astype(dq_ref.dtype)
            dkv_ref[:, sl] += _bdot(ds, qh, _DIMS["tn"])
            dkv_ref[:, sv] += _bdot(p, doh, _DIMS["tn"])

    row = pl.BlockSpec((tr, MEM_WIDTH), lambda i: (i, 0))
    kvs = pl.BlockSpec((mlen, 2 * MEM_WIDTH), lambda i: (0, 0))
    return pl.pallas_call(
        body, name=name, grid=(seq // tr,), in_specs=[row, kvs, row], out_specs=(row, kvs),
        out_shape=(jax.ShapeDtypeStruct((seq, MEM_WIDTH), BF16), jax.ShapeDtypeStruct((mlen, 2 * MEM_WIDTH), F32)),
        compiler_params=_cparams(("arbitrary",)),
    )(q2, kv, do2)


_HBM = pl.BlockSpec(memory_space=pl.ANY)
_HBM_ONLY = pl.BlockSpec(memory_space=pltpu.HBM)
_MESH = pl.DeviceIdType.MESH


def _mesh_place():
    x, y, c = lax.axis_index("x"), lax.axis_index("y"), lax.axis_index("c")
    other_chips = [(1 - x, y), (x, 1 - y), (1 - x, 1 - y)]
    return x, y, c, other_chips


def _gather_all(name, arrays):
    n = len(arrays)

    def body(*refs):
        ins, outs = refs[:n], refs[n:2 * n]
        send_sems, recv_sems, local_sems = refs[2 * n:]
        x, y, c, chips = _mesh_place()
        me, sibling = (x, y, c), (x, y, 1 - c)

        def slot(a, place):
            px, py, pc = place
            return outs[a].at[4 * px + 2 * py + pc]

        def copy(a, k, block, to, src=None):
            return pltpu.make_async_remote_copy(
                src_ref=slot(a, block) if src is None else src, dst_ref=slot(a, block),
                send_sem=send_sems.at[a, k], recv_sem=recv_sems.at[a, k], device_id=to, device_id_type=_MESH)

        mine = [pltpu.make_async_copy(ins[a], slot(a, me), local_sems.at[a]) for a in range(n)]
        for cp in mine:
            cp.start()
        first = []
        for a in range(n):
            first.append(copy(a, 0, me, sibling, src=ins[a]))
            first += [copy(a, 1 + j, me, (*chip, c), src=ins[a]) for j, chip in enumerate(chips)]
        for cp in first:
            cp.start()
        passed = []
        for j, chip in enumerate(chips):
            for a in range(n):
                copy(a, 1 + j, (*chip, c), me).wait_recv()
                fwd = copy(a, 4 + j, (*chip, c), sibling)
                fwd.start()
                passed.append(fwd)
        for a in range(n):
            copy(a, 0, sibling, me).wait_recv()
            for j, chip in enumerate(chips):
                copy(a, 4 + j, (*chip, 1 - c), me).wait_recv()
        for cp in first + passed:
            cp.wait_send()
        for cp in mine:
            cp.wait()

    out_shape = tuple(jax.ShapeDtypeStruct((N_DEV,) + arr.shape, arr.dtype) for arr in arrays)
    return pl.pallas_call(
        body, name=name, in_specs=[_HBM] * n, out_specs=tuple([_HBM] * n), out_shape=out_shape,
        scratch_shapes=[pltpu.SemaphoreType.DMA((n, N_DEV - 1)), pltpu.SemaphoreType.DMA((n, N_DEV - 1)),
                        pltpu.SemaphoreType.DMA((n,))],
    )(*arrays)


_SEM = pl.BlockSpec(memory_space=pltpu.SEMAPHORE)
_DATAFLOW = pltpu.SideEffectType.DATAFLOW_SIDE_EFFECTING


def _device_index():
    return (4 * lax.axis_index("x") + 2 * lax.axis_index("y") + lax.axis_index("c")).astype(jnp.int32).reshape(1)


def _place_own(name, pieces, *, stacked_src, after=None):
    n = len(pieces)
    n_in = n + (after is not None)

    def body(me_ref, *refs):
        for a in range(n):
            refs[n_in + a][...] = refs[a][...]

    def spec(shape):
        return pl.BlockSpec((None,) + tuple(shape), lambda i, me_ref: (me_ref[0],) + (0,) * len(shape))

    shapes = [p.shape[1:] if stacked_src else p.shape for p in pieces]
    if stacked_src:
        in_specs = [spec(s) for s in shapes]
    else:
        in_specs = [pl.BlockSpec(tuple(s), lambda i, me_ref, nd=len(s): (0,) * nd) for s in shapes]
    operands = list(pieces)
    if after is not None:
        in_specs.append(_HBM)
        operands.append(after)
    return pl.pallas_call(
        body, name=name,
        grid_spec=pltpu.PrefetchScalarGridSpec(num_scalar_prefetch=1, grid=(1,), in_specs=in_specs,
                                               out_specs=tuple(spec(s) for s in shapes)),
        out_shape=tuple(jax.ShapeDtypeStruct((N_DEV,) + tuple(s), p.dtype) for s, p in zip(shapes, pieces)),
        compiler_params=_cparams(("arbitrary",)),
    )(_device_index(), *operands)


def _peer_places():
    x, y, c = lax.axis_index("x"), lax.axis_index("y"), lax.axis_index("c")
    peers = []
    for k in range(N_DEV - 1):
        flip = k + 1
        px = 1 - x if flip & 4 else x
        py = 1 - y if flip & 2 else y
        pc = 1 - c if flip & 1 else c
        peers.append((px, py, pc, 4 * px + 2 * py + pc))
    return 4 * x + 2 * y + c, peers


def _direct_copy(srcs, lands, send_sems, recv_sems, a, k, me, peer, scatter):
    px, py, pc, pidx = peer
    return pltpu.make_async_remote_copy(
        src_ref=srcs[a].at[pidx] if scatter else srcs[a], dst_ref=lands[a].at[me],
        send_sem=send_sems.at[a * (N_DEV - 1) + k], recv_sem=recv_sems.at[a * (N_DEV - 1) + k],
        device_id=(px, py, pc), device_id_type=_MESH)


def _send_start(name, srcs, lands, *, scatter):
    n = len(srcs)

    def body(*refs):
        src_refs, land_refs = refs[:n], refs[n:2 * n]
        send_sems, recv_sems = refs[2 * n], refs[2 * n + 1]
        token = refs[-1]
        me, peers = _peer_places()
        for k, peer in enumerate(peers):
            for a in range(n):
                _direct_copy(src_refs, land_refs, send_sems, recv_sems, a, k, me, peer, scatter).start()
        token[...] = jnp.zeros_like(token)

    hbm_shapes = [pltpu.HBM(t.shape, t.dtype) for t in list(srcs) + list(lands)]
    outs = pl.pallas_call(
        body, name=name,
        out_shape=(pltpu.SemaphoreType.DMA((n * (N_DEV - 1),)), pltpu.SemaphoreType.DMA((n * (N_DEV - 1),)), *hbm_shapes,
                   jax.ShapeDtypeStruct((8, LANE), F32)),
        in_specs=[_HBM_ONLY] * (2 * n),
        out_specs=(_SEM, _SEM, *([_HBM_ONLY] * (2 * n)), pl.BlockSpec(memory_space=pltpu.VMEM)),
        input_output_aliases={i: 2 + i for i in range(2 * n)},
        compiler_params=pltpu.CompilerParams(has_side_effects=_DATAFLOW),
    )(*[pltpu.with_memory_space_constraint(t, pltpu.HBM) for t in list(srcs) + list(lands)])
    return outs[0], outs[1], outs[2:2 + n], outs[2 + n:2 + 2 * n], outs[-1]


def _send_wait(name, send_sems, recv_sems, srcs, lands, after, *, scatter):
    n = len(srcs)

    def body(*refs):
        src_refs, land_refs = refs[:n], refs[n:2 * n]
        send_sems, recv_sems = refs[2 * n], refs[2 * n + 1]
        me, peers = _peer_places()
        for k, peer in enumerate(peers):
            for a in range(n):
                cp = _direct_copy(src_refs, land_refs, send_sems, recv_sems, a, k, me, peer, scatter)
                cp.wait_send()
                cp.wait_recv()

    hbm_shapes = [pltpu.HBM(t.shape, t.dtype) for t in list(srcs) + list(lands)]
    outs = pl.pallas_call(
        body, name=name, out_shape=tuple(hbm_shapes),
        in_specs=[_HBM_ONLY] * (2 * n) + [_SEM, _SEM, _HBM],
        out_specs=tuple([_HBM_ONLY] * (2 * n)),
        input_output_aliases={i: i for i in range(2 * n)},
        compiler_params=pltpu.CompilerParams(has_side_effects=_DATAFLOW),
    )(*srcs, *lands, send_sems, recv_sems, after)
    return outs[n:]


def _unstack_cols(name, stacked):
    n, rows, cols = stacked.shape

    def body(i_ref, o_ref):
        o_ref[...] = i_ref[...]

    return pl.pallas_call(
        body, name=name, grid=(n,), in_specs=[pl.BlockSpec((None, rows, cols), lambda k: (k, 0, 0))],
        out_specs=pl.BlockSpec((rows, cols), lambda k: (0, k)),
        out_shape=jax.ShapeDtypeStruct((rows, n * cols), stacked.dtype),
        compiler_params=_cparams(("parallel",)),
    )(stacked)


def _restack_cols(name, mat):
    rows, width = mat.shape
    cols = width // N_DEV

    def body(i_ref, o_ref):
        o_ref[...] = i_ref[...]

    return pl.pallas_call(
        body, name=name, grid=(N_DEV,), in_specs=[pl.BlockSpec((rows, cols), lambda k: (0, k))],
        out_specs=pl.BlockSpec((None, rows, cols), lambda k: (k, 0, 0)),
        out_shape=jax.ShapeDtypeStruct((N_DEV, rows, cols), mat.dtype),
        compiler_params=_cparams(("parallel",)),
    )(mat)


def _remap_pieces(runs):
    plan = {}
    for du, dc, su, sc, ln in runs:
        while ln > 0:
            lane = dc % LANE
            take = min(ln, LANE - lane)
            plan.setdefault((du, dc // LANE), []).append((su, sc, take, lane))
            dc, sc, ln = dc + take, sc + take, ln - take
    return plan


def _remap(name, srcs, src_units, runs, *, out_units, out_cols, out_dtype, tr=256):
    rows = srcs[0].shape[-2]
    tr = min(tr, rows)
    plan = _remap_pieces(runs)
    n_src = len(srcs)
    stacked_out = out_units is not None
    n_tiles = out_cols // LANE

    def body(*refs):
        o_ref = refs[n_src]

        def src_tile(unit, t):
            ai, lead = src_units[unit]
            ref = refs[ai]
            sl = slice(t * LANE, (t + 1) * LANE)
            return (ref[:, sl] if lead is None else ref[lead, :, sl]).astype(F32)

        lane = lax.broadcasted_iota(jnp.int32, (tr, LANE), 1)
        for du in range(out_units if stacked_out else 1):
            for t in range(n_tiles):
                acc = jnp.zeros((tr, LANE), F32)
                for su, sc, ln, dl in plan.get((du if stacked_out else None, t), []):
                    st, so = sc // LANE, sc % LANE
                    first = src_tile(su, st)
                    if so == dl and so + ln <= LANE:
                        piece = first
                    else:
                        second = src_tile(su, st + 1) if so + ln > LANE else first
                        both = jnp.concatenate([first, second], axis=1)
                        piece = pltpu.roll(both, (dl - so) % (2 * LANE), axis=1)[:, 0:LANE]
                    acc = piece if (dl == 0 and ln == LANE) else jnp.where(
                        jnp.logical_and(lane >= dl, lane < dl + ln), piece, acc)
                if stacked_out:
                    o_ref[du, :, t * LANE:(t + 1) * LANE] = acc.astype(o_ref.dtype)
                else:
                    o_ref[:, t * LANE:(t + 1) * LANE] = acc.astype(o_ref.dtype)

    in_specs = []
    for arr in srcs:
        if arr.ndim == 2:
            in_specs.append(pl.BlockSpec((tr, arr.shape[1]), lambda i: (i, 0)))
        else:
            in_specs.append(pl.BlockSpec((arr.shape[0], tr, arr.shape[2]), lambda i: (0, i, 0)))
    if stacked_out:
        out_spec = pl.BlockSpec((out_units, tr, out_cols), lambda i: (0, i, 0))
        out_shape = jax.ShapeDtypeStruct((out_units, rows, out_cols), out_dtype)
    else:
        out_spec = pl.BlockSpec((tr, out_cols), lambda i: (i, 0))
        out_shape = jax.ShapeDtypeStruct((rows, out_cols), out_dtype)
    return pl.pallas_call(
        body, name=name, grid=(rows // tr,), in_specs=in_specs, out_specs=out_spec, out_shape=out_shape,
        compiler_params=_cparams(("parallel",)),
    )(*srcs)


def _proj_col(c):
    if c < PROJ_GATE0:
        return c
    if c < PROJ_GATE0 + FOX_HEADS:
        return PROJ_F0 + (c - PROJ_GATE0)
    return c - FOX_HEADS


def _win_runs():
    cuts = sorted(set([0, PROJ_GATE0, PROJ_GATE0 + FOX_HEADS, IN_WIDTH] + [SHARD_IN * k for k in range(N_DEV + 1)]))
    return [(lo // SHARD_IN, lo % SHARD_IN, _proj_col(lo), hi - lo) for lo, hi in zip(cuts[:-1], cuts[1:])]


def _assemble_win(name, stacked):
    runs = [(None, pc, k, sc, ln) for k, sc, pc, ln in _win_runs()]
    return _remap(name, [stacked], [(0, k) for k in range(N_DEV)], runs,
                  out_units=None, out_cols=PROJ_WIDTH, out_dtype=BF16)


def _disassemble_dwin(name, dw):
    runs = [(k, sc, 0, pc, ln) for k, sc, pc, ln in _win_runs()]
    return _remap(name, [dw], [(0, None)], runs, out_units=N_DEV, out_cols=SHARD_IN_PAD, out_dtype=BF16)


def _concat_cols(name, parts, *, tr=512):
    rows = parts[0].shape[0]
    tr = min(tr, rows)
    widths = [p.shape[1] for p in parts]
    total = sum(widths)

    def body(*refs):
        o_ref = refs[len(parts)]
        lo = 0
        for r, w in zip(refs[:len(parts)], widths):
            o_ref[:, lo:lo + w] = r[...].astype(o_ref.dtype)
            lo += w

    return pl.pallas_call(
        body, name=name, grid=(rows // tr,),
        in_specs=[pl.BlockSpec((tr, w), lambda i: (i, 0)) for w in widths],
        out_specs=pl.BlockSpec((tr, total), lambda i: (i, 0)),
        out_shape=jax.ShapeDtypeStruct((rows, total), BF16),
        compiler_params=_cparams(("parallel",)),
    )(*parts)


FFN_BLK = FFN_HIDDEN // 2


def _ffn_col(c):
    half, r = divmod(c, FFN_HIDDEN)
    blk, r = divmod(r, FFN_BLK)
    return blk * 2 * FFN_BLK + half * FFN_BLK + r


def _assemble_wffn(name, stacked):
    runs = [(None, _ffn_col(SHARD_FFN * k), k, 0, SHARD_FFN) for k in range(N_DEV)]
    return _remap(name, [stacked], [(0, k) for k in range(N_DEV)], runs,
                  out_units=None, out_cols=2 * FFN_HIDDEN, out_dtype=BF16)


def _disassemble_dwffn(name, dw):
    runs = [(k, 0, 0, _ffn_col(SHARD_FFN * k), SHARD_FFN) for k in range(N_DEV)]
    return _remap(name, [dw], [(0, None)], runs, out_units=N_DEV, out_cols=SHARD_FFN_PAD, out_dtype=BF16)


def _ffn_in_swiglu(name, xn, w, *, tm=512):
    rows, k = xn.shape
    tm = min(tm, rows)
    nblk = FFN_HIDDEN // FFN_BLK

    def body(x_ref, w_ref, f_ref, g_ref):
        f = _bdot(x_ref[...], w_ref[...], _DIMS["nn"])
        f_ref[...] = f.astype(f_ref.dtype)
        fa = f[:, 0:FFN_BLK]
        g_ref[...] = (fa * _sigmoid(fa) * f[:, FFN_BLK:2 * FFN_BLK]).astype(g_ref.dtype)

    return pl.pallas_call(
        body, name=name, grid=(nblk, rows // tm),
        in_specs=[pl.BlockSpec((tm, k), lambda j, i: (i, 0)), pl.BlockSpec((k, 2 * FFN_BLK), lambda j, i: (0, j))],
        out_specs=(pl.BlockSpec((tm, 2 * FFN_BLK), lambda j, i: (i, j)), pl.BlockSpec((tm, FFN_BLK), lambda j, i: (i, j))),
        out_shape=(jax.ShapeDtypeStruct((rows, 2 * FFN_HIDDEN), BF16), jax.ShapeDtypeStruct((rows, FFN_HIDDEN), BF16)),
        compiler_params=_cparams(("parallel", "arbitrary")),
    )(xn, w)


def _d_ffn_out_swiglu(name, dh, w_out, f, *, tm=512):
    rows, d = dh.shape
    tm = min(tm, rows)
    nblk = FFN_HIDDEN // FFN_BLK

    def body(dh_ref, w_ref, f_ref, df_ref):
        dg = _bdot(dh_ref[...], w_ref[...], _DIMS["nt"])
        fa = f_ref[:, 0:FFN_BLK].astype(F32)
        fb = f_ref[:, FFN_BLK:2 * FFN_BLK].astype(F32)
        s = _sigmoid(fa)
        df_ref[:, 0:FFN_BLK] = (dg * fb * s * (1.0 + fa * (1.0 - s))).astype(df_ref.dtype)
        df_ref[:, FFN_BLK:2 * FFN_BLK] = (dg * fa * s).astype(df_ref.dtype)

    wide = pl.BlockSpec((tm, 2 * FFN_BLK), lambda j, i: (i, j))
    return pl.pallas_call(
        body, name=name, grid=(nblk, rows // tm),
        in_specs=[pl.BlockSpec((tm, d), lambda j, i: (i, 0)), pl.BlockSpec((FFN_BLK, d), lambda j, i: (j, 0)), wide],
        out_specs=wide, out_shape=jax.ShapeDtypeStruct((rows, 2 * FFN_HIDDEN), BF16),
        compiler_params=_cparams(("parallel", "arbitrary")),
    )(dh, w_out, f)


def _adamw(name, parts, w, m, v, *, tr=128):
    rows, cols = w.shape
    n_parts = parts.shape[0]
    tr = min(tr, rows)
    assert rows % tr == 0, (name, rows, tr)
    c1 = 1.0 - ADAM_B1 ** ADAM_STEP
    c2 = 1.0 - ADAM_B2 ** ADAM_STEP

    def body(p_ref, w_ref, m_ref, v_ref, g_ref, d_ref, nm_ref, nv_ref):
        g = p_ref[0].astype(F32)
        for s in range(1, n_parts):
            g = g + p_ref[s].astype(F32)
        m_new = ADAM_B1 * m_ref[...] + (1.0 - ADAM_B1) * g
        v_new = ADAM_B2 * v_ref[...] + (1.0 - ADAM_B2) * (g * g)
        upd = (m_new / c1) / (jnp.sqrt(v_new / c2) + ADAM_EPS) + ADAM_WD * w_ref[...]
        g_ref[...] = g
        d_ref[...] = -ADAM_LR * upd
        nm_ref[...] = m_new
        nv_ref[...] = v_new

    row = pl.BlockSpec((tr, cols), lambda i: (i, 0))
    out = jax.ShapeDtypeStruct((rows, cols), F32)
    return pl.pallas_call(
        body, name=name, grid=(rows // tr,),
        in_specs=[pl.BlockSpec((n_parts, tr, cols), lambda i: (0, i, 0)), row, row, row],
        out_specs=(row, row, row, row), out_shape=(out, out, out, out),
        compiler_params=_cparams(("parallel",)),
    )(parts, w, m, v)


_WEIGHTS = ("norm_mix", "w_in", "b_forget", "lam_re", "lam_im", "log_dt", "b_re", "b_im", "c_re", "c_im",
            "d_skip", "w_glu", "w_fox_o", "w_mix_out", "norm_mem_q", "norm_mem_kv", "w_mem_q", "w_mem_kv",
            "w_mem_o", "norm_ffn", "w_ffn_in", "w_ffn_out", "norm_final")
_SHARDED = ("w_in", "w_glu", "w_fox_o", "w_mix_out", "w_mem_q", "w_mem_kv", "w_mem_o", "w_ffn_in", "w_ffn_out")
_SMALL = tuple(n for n in _WEIGHTS if n not in _SHARDED)
_PACK_COLS = 1024


def _pack(arrays):
    flat = jnp.concatenate([a.reshape(-1).astype(F32) for a in arrays])
    rows = -(-flat.shape[0] // _PACK_COLS)
    return jnp.pad(flat, (0, rows * _PACK_COLS - flat.shape[0])).reshape(rows, _PACK_COLS)


def _unpack(buf, like):
    flat = buf.reshape(-1)
    out, pos = [], 0
    for a in like:
        out.append(flat[pos:pos + a.size].reshape(a.shape))
        pos += a.size
    return out


def _mm(name, a, b, mode, m, n, k, out_dtype, tm=1024, tn=512, tk=1024, **kw):
    return _matmul(name, a, b, mode, m, n, k, out_dtype=out_dtype, tm=tm, tn=tn, tk=tk, **kw)


def kernel(x, mem, norm_mix, w_in, b_forget, lam_re, lam_im, log_dt, b_re, b_im, c_re, c_im, d_skip, w_glu, w_fox_o, w_mix_out, norm_mem_q, norm_mem_kv, w_mem_q, w_mem_kv, w_mem_o, norm_ffn, w_ffn_in, w_ffn_out, norm_final, loss_target, m_norm_mix, m_w_in, m_b_forget, m_lam_re, m_lam_im, m_log_dt, m_b_re, m_b_im, m_c_re, m_c_im, m_d_skip, m_w_glu, m_w_fox_o, m_w_mix_out, m_norm_mem_q, m_norm_mem_kv, m_w_mem_q, m_w_mem_kv, m_w_mem_o, m_norm_ffn, m_w_ffn_in, m_w_ffn_out, m_norm_final, v_norm_mix, v_w_in, v_b_forget, v_lam_re, v_lam_im, v_log_dt, v_b_re, v_b_im, v_c_re, v_c_im, v_d_skip, v_w_glu, v_w_fox_o, v_w_mix_out, v_norm_mem_q, v_norm_mem_kv, v_w_mem_q, v_w_mem_kv, v_w_mem_o, v_norm_ffn, v_w_ffn_in, v_w_ffn_out, v_norm_final):
    given = dict(locals())
    weights = {n: given[n] for n in _WEIGHTS}
    mom_m = {n: given["m_" + n] for n in _WEIGHTS}
    mom_v = {n: given["v_" + n] for n in _WEIGHTS}
    seq = x.shape[1]
    nc = seq // SSM_CHUNK
    d = D_MODEL
    xs, mems, tgt = x[0], mem[0], loss_target[0]

    def padcols(a, width):
        return jnp.pad(a, ((0, 0), (0, width - a.shape[1])))

    shards = [padcols(w_in[0].astype(BF16), SHARD_IN_PAD), w_glu[0].astype(BF16), w_fox_o[0].astype(BF16),
              w_mix_out[0].astype(BF16), w_mem_q[0].astype(BF16), w_mem_kv[0].astype(BF16),
              w_mem_o[0].astype(BF16), padcols(w_ffn_in[0].astype(BF16), SHARD_FFN_PAD), w_ffn_out[0].astype(BF16)]
    win = _assemble_win("assemble_w_in", _gather_all("gather_w_in", shards[:1])[0])
    rest = shards[1:]
    gsend, grecv, rest_thru, lands, gtoken = _send_start(
        "gather_rest_start", rest, _place_own("place_weight_shards", rest, stacked_src=False, after=win), scatter=False)

    u = _rms_fwd("rms_mix", xs, norm_mix, after=gtoken)
    ussm = _mm("proj_ssm", u, win, "nn", seq, SSM_WIDTH, d, F32)
    qkv = _mm("proj_qkv", u, win, "nn", seq, 3 * FOX_WIDTH, d, BF16, tn=512, b_off=(0, SSM_WIDTH))
    gates = _mm("proj_gates", u, win, "nn", seq, 2 * d, d, BF16, tn=1024, b_off=(0, PROJ_GATE0))
    fproj = _mm("proj_forget", u, win, "nn", seq, LANE, d, F32, tn=LANE, b_off=(0, PROJ_F0))

    ssm_params = (lam_re[0], lam_im[0], log_dt[0], b_re[0], b_im[0], c_re[0], c_im[0])
    (m_c, bw_c, cm_c, a8, aseg), mats_vjp = jax.vjp(lambda *p: _ssm_mats(*p, nc), *ssm_params)
    m_b = _bd_expand("ssm_expand_m", _BD_M, m_c)
    bw_b = _bd_expand("ssm_expand_bw", _BD_BW, bw_c)
    cm_b = _bd_expand("ssm_expand_cm", _BD_CM, cm_c)
    u8 = ussm.reshape(nc, SSM_CHUNK * SSM_WIDTH)
    d8 = jnp.tile(d_skip, (1, SSM_CHUNK))
    w4 = _ssm_w("ssm_w", u8, bw_b)
    sp4 = _ssm_scan("ssm_scan", w4, a8, aseg, reverse=False)
    y8 = _ssm_y("ssm_y", u8, sp4, m_b, cm_b)
    act = _ssm_post_fwd("ssm_act", y8, u8, d8).reshape(seq, SSM_WIDTH)

    bcol = jnp.pad(b_forget[0], (0, LANE - FOX_HEADS)).reshape(LANE, 1)
    cum_t = _fox_cum("fox_cum", fproj, bcol).reshape(FOX_HEADS // 2, 2, seq)
    att, lse = _fox_fwd("fox_fwd", qkv, cum_t)

    gathered = _send_wait("gather_rest_wait", gsend, grecv, rest_thru, lands, att, scatter=False)
    wglu = _unstack_cols("unstack_w_glu", gathered[0])
    wfoxo = _unstack_cols("unstack_w_fox_o", gathered[1])
    wmix = gathered[2].reshape(d, d)
    wmq = gathered[3].reshape(d, MEM_WIDTH)
    wmkv = gathered[4].reshape(d, 2 * MEM_WIDTH)
    wmo = _unstack_cols("unstack_w_mem_o", gathered[5])
    wffn_in = _assemble_wffn("assemble_w_ffn_in", gathered[6])
    wffn_out = gathered[7].reshape(FFN_HIDDEN, d)

    glu = _mm("glu", act, wglu, "nn", seq, 2 * d, SSM_WIDTH, BF16, tn=1024)
    out_b = _mm("fox_out", att, wfoxo, "nn", seq, d, FOX_WIDTH, BF16, tn=1024)

    mixin = _mix_fwd("mix", glu, gates, out_b)
    h1 = _mm("mix_out", mixin, wmix, "nn", seq, d, d, F32, tn=1024, add=xs)

    n1 = _rms_fwd("rms_mem_q", h1, norm_mem_q)
    q2 = _mm("mem_q", n1, wmq, "nn", seq, MEM_WIDTH, d, BF16)
    mn = _rms_fwd("rms_mem_kv", mems, norm_mem_kv)
    mlen = mems.shape[0]
    kv = _mm("mem_kv", mn, wmkv, "nn", mlen, 2 * MEM_WIDTH, d, BF16)
    o2 = _mem_fwd("mem_attn", q2, kv)
    h2 = _mm("mem_out", o2, wmo, "nn", seq, d, MEM_WIDTH, F32, tn=1024, add=h1)

    n2 = _rms_fwd("rms_ffn", h2, norm_ffn)
    f, g_act = _ffn_in_swiglu("ffn_in_swiglu", n2, wffn_in)
    h3 = _mm("ffn_out", g_act, wffn_out, "nn", seq, d, FFN_HIDDEN, F32, tk=FFN_HIDDEN, add=h2)
    loss_part, dh3, dg_final = _final_loss("final_loss", h3, tgt, norm_final.reshape(1, d))

    df = _d_ffn_out_swiglu("d_ffn_out_swiglu", dh3, wffn_out, f)
    dwffn_out = _mm("d_ffn_out_w", g_act, dh3, "tn", FFN_HIDDEN, d, seq, BF16, tm=1408, tn=1024)
    dh2, dg_ffn = _matmul_rms_bwd("d_ffn_in_x_rms", df, wffn_in, 2 * FFN_HIDDEN, h2, norm_ffn, dh3, tk=FFN_HIDDEN)
    dwffn_in = _mm("d_ffn_in_w", n2, df, "tn", d, 2 * FFN_HIDDEN, seq, BF16, tn=1408)

    do2 = _mm("d_mem_out_x", dh2, wmo, "nt", seq, MEM_WIDTH, d, F32)
    dwmo = _restack_cols("restack_d_w_mem_o", _mm("d_mem_out_w", o2, dh2, "tn", MEM_WIDTH, d, seq, BF16, tn=1024))
    dq2, dkv = _mem_bwd("d_mem_attn", q2, kv, do2)
    dwmq = _mm("d_mem_q_w", n1, dq2, "tn", d, MEM_WIDTH, seq, BF16)
    dwmkv = _mm("d_mem_kv_w", mn, dkv, "tn", d, 2 * MEM_WIDTH, mlen, BF16, tn=1024)
    dmn = _mm("d_mem_kv_x", dkv, wmkv, "nt", mlen, d, 2 * MEM_WIDTH, F32)
    dg_memkv = _rms_gain_grad("d_rms_mem_kv", dmn, mems)

    early = [dwmq.reshape(N_DEV, d // N_DEV, MEM_WIDTH), dwmkv.reshape(N_DEV, d // N_DEV, 2 * MEM_WIDTH), dwmo,
             _disassemble_dwffn("split_d_w_ffn_in", dwffn_in), dwffn_out.reshape(N_DEV, FFN_HIDDEN // N_DEV, d)]
    ssend, srecv, early_thru, early_lands, stoken = _send_start(
        "scatter_early_start", early, _place_own("place_early_grads", early, stacked_src=True), scatter=True)
    dh1, dg_memq = _matmul_rms_bwd("d_mem_q_x_rms", dq2, wmq, MEM_WIDTH, h1, norm_mem_q, dh2, tm=1024, after=stoken)

    dmixin = _mm("d_mix_out_x", dh1, wmix, "nt", seq, d, d, F32, tn=1024)
    dwmix = _mm("d_mix_out_w", mixin, dh1, "tn", d, d, seq, BF16, tn=1024)
    dglu, dgates, dout_b = _mix_bwd("d_mix", dmixin, glu, gates, out_b)
    datt = _mm("d_fox_out_x", dout_b, wfoxo, "nt", seq, FOX_WIDTH, d, F32)
    dwfoxo = _restack_cols("restack_d_w_fox_o", _mm("d_fox_out_w", att, dout_b, "tn", FOX_WIDTH, d, seq, BF16, tn=1024))
    dact = _mm("d_glu_x", dglu, wglu, "nt", seq, SSM_WIDTH, 2 * d, F32, tk=2 * d)
    dwglu = _restack_cols("restack_d_w_glu", _mm("d_glu_w", act, dglu, "tn", SSM_WIDTH, 2 * d, seq, BF16, tn=2 * d))

    mid = [dwglu, dwfoxo, dwmix.reshape(N_DEV, d // N_DEV, d)]
    msend, mrecv, mid_thru, mid_lands, mtoken = _send_start(
        "scatter_mid_start", mid, _place_own("place_mid_grads", mid, stacked_src=True), scatter=True)

    dz8, dg_dskip = _ssm_post_bwd("d_ssm_act", dact.reshape(nc, SSM_CHUNK * SSM_WIDTH), y8, u8, d8, after=mtoken)
    ds4, dcm = _ssm_ds("d_ssm_y_state", dz8, sp4, cm_b)
    g4, da8 = _ssm_scan("d_ssm_scan", ds4, a8, aseg, reverse=True, sprev4=sp4)
    dx8, dm, dbw = _ssm_dx("d_ssm_x", dz8, g4, u8, m_b, bw_b, d8)
    dussm = dx8.reshape(seq, SSM_WIDTH)
    g_ssm = mats_vjp((_bd_reduce("ssm_reduce_dm", _BD_M, dm), _bd_reduce("ssm_reduce_dbw", _BD_BW, dbw),
                      _bd_reduce("ssm_reduce_dcm", _BD_CM, dcm), da8, jnp.zeros_like(aseg)))

    dq, dk, dv, dcs = _fox_bwd("d_fox", qkv, cum_t, att, datt, lse)
    dfproj, dbf = _fox_cum_bwd("d_fox_cum", dcs, fproj, bcol)
    dg_bforget = dbf[0:FOX_HEADS, 0].reshape(1, FOX_HEADS)

    dproj = _concat_cols("d_proj_concat", (dussm, dq, dk, dv, dgates, dfproj))
    dwin = _mm("d_proj_w", u, dproj, "tn", d, PROJ_WIDTH, seq, BF16, tn=1408)
    late = [_disassemble_dwin("split_d_w_in", dwin)]
    lsend, lrecv, late_thru, late_lands, ltoken = _send_start(
        "scatter_late_start", late, _place_own("place_late_grads", late, stacked_src=True), scatter=True)
    dx, dg_mix = _matmul_rms_bwd("d_proj_x_rms", dproj, win, PROJ_WIDTH, xs, norm_mix, dh1, tk=1408, after=ltoken)

    early_parts = _send_wait("scatter_early_wait", ssend, srecv, early_thru, early_lands, dx, scatter=True)
    mid_parts = _send_wait("scatter_mid_wait", msend, mrecv, mid_thru, mid_lands, dx, scatter=True)
    received = dict(zip(("w_glu", "w_fox_o", "w_mix_out"), mid_parts))
    received.update(zip(("w_mem_q", "w_mem_kv", "w_mem_o", "w_ffn_in", "w_ffn_out"), early_parts))

    small_grads = dict(zip(
        _SMALL, (dg_mix, dg_bforget, g_ssm[0][None], g_ssm[1][None], g_ssm[2][None], g_ssm[3][None], g_ssm[4][None],
                 g_ssm[5][None], g_ssm[6][None], dg_dskip, dg_memq, dg_memkv, dg_ffn, dg_final.reshape(d))))
    small_like = [weights[n] for n in _SMALL]
    small_all = _gather_all("gather_small_grads", [_pack([small_grads[n] for n in _SMALL])])[0]
    pk = [_pack([src[n] for n in _SMALL]) for src in (weights, mom_m, mom_v)]
    small_out = _adamw("adamw_small", small_all, pk[0], pk[1], pk[2], tr=small_all.shape[1])
    results = [dict(zip(_SMALL, _unpack(buf, small_like))) for buf in small_out]
    tiles = {"w_in": 128, "w_glu": 128, "w_fox_o": 128, "w_mix_out": 128, "w_mem_q": 128, "w_mem_kv": 128,
             "w_mem_o": 128, "w_ffn_in": 128, "w_ffn_out": 176}
    pads = {"w_in": SHARD_IN_PAD, "w_ffn_in": SHARD_FFN_PAD}
    outs = small_out
    for name in _SHARDED[1:] + _SHARDED[:1]:
        if name == "w_in":
            received[name] = _send_wait("scatter_late_wait", lsend, lrecv, late_thru, late_lands, outs[0],
                                        scatter=True)[0]
        parts = received[name]
        w2, m2, v2 = weights[name][0], mom_m[name][0], mom_v[name][0]
        cols = w2.shape[1]
        if name in pads:
            w2, m2, v2 = (padcols(t, pads[name]) for t in (w2, m2, v2))
        outs = _adamw("adamw_" + name, parts, w2, m2, v2, tr=tiles[name])
        for res, o in zip(results, outs):
            res[name] = o[:, :cols][None]

    loss = lax.psum(loss_part[0, 0], ("x", "y", "c"))
    out = [loss, dx[None]]
    for res in results:
        out.extend(res[n] for n in _WEIGHTS)
    return tuple(out)
```

```python
import math

import jax
import jax.numpy as jnp
import numpy as np
from jax import lax
from jax.experimental import pallas as pl
from jax.experimental.pallas import tpu as pltpu

F32 = jnp.float32
BF16 = jnp.bfloat16

N_DEV = 8
LANE = 128
VMEM_LIMIT = 56 * 1024 * 1024

D_MODEL = 1024
SSM_GROUP = 16
SSM_GROUPS = 32
SSM_WIDTH = 512
SSM_STATE = 64
SSM_CHUNK = 8
FOX_HEADS = 8
FOX_HEAD_DIM = 64
FOX_WIDTH = 512
MEM_HEADS = 4
MEM_HEAD_DIM = 128
MEM_WIDTH = 512
FFN_HIDDEN = 2816
RMS_EPS = 1e-6
IN_WIDTH = 4104
SHARD_IN = IN_WIDTH // N_DEV
SHARD_IN_PAD = 640
SHARD_FFN = 2 * FFN_HIDDEN // N_DEV
SHARD_FFN_PAD = 768
PROJ_GATE0 = 2048
PROJ_F0 = 4096
PROJ_WIDTH = 4224

ADAM_LR = 0.001
ADAM_B1 = 0.9
ADAM_B2 = 0.999
ADAM_EPS = 1e-08
ADAM_WD = 0.01
ADAM_STEP = 10


def _cparams(sem=None):
    return pltpu.CompilerParams(dimension_semantics=sem, vmem_limit_bytes=VMEM_LIMIT)


def _sigmoid(x):
    return 1.0 / (1.0 + jnp.exp(-x))


def _bdot(a, b, dims):
    return lax.dot_general(a.astype(BF16), b.astype(BF16), ((dims[0], dims[1]), ((), ())),
                           preferred_element_type=F32)


_DIMS = {"nn": ((1,), (0,)), "nt": ((1,), (1,)), "tn": ((0,), (0,))}


def _matmul(name, a, b, mode, m, n, k, *, out_dtype, tm, tn, tk, a_off=(0, 0), b_off=(0, 0), add=None):
    tm, tn, tk = min(tm, m), min(tn, n), min(tk, k)
    assert m % tm == 0 and n % tn == 0 and k % tk == 0, (name, m, n, k, tm, tn, tk)
    nk = k // tk
    grid = (m // tm, n // tn, nk)

    def blk(off, t):
        assert off % t == 0, (name, off, t)
        return off // t

    if mode in ("nn", "nt"):
        ar, ac = blk(a_off[0], tm), blk(a_off[1], tk)
        a_spec = pl.BlockSpec((tm, tk), lambda i, j, kk: (i + ar, kk + ac))
    else:
        ar, ac = blk(a_off[0], tk), blk(a_off[1], tm)
        a_spec = pl.BlockSpec((tk, tm), lambda i, j, kk: (kk + ar, i + ac))

    if mode in ("nn", "tn"):
        br, bc = blk(b_off[0], tk), blk(b_off[1], tn)
        b_spec = pl.BlockSpec((tk, tn), lambda i, j, kk: (kk + br, j + bc))
    else:
        br, bc = blk(b_off[0], tn), blk(b_off[1], tk)
        b_spec = pl.BlockSpec((tn, tk), lambda i, j, kk: (j + br, kk + bc))
    o_spec = pl.BlockSpec((tm, tn), lambda i, j, kk: (i, j))
    out_shape = jax.ShapeDtypeStruct((m, n), out_dtype)

    in_specs = [a_spec, b_spec]
    operands = [a, b]
    if add is not None:
        in_specs.append(pl.BlockSpec((tm, tn), lambda i, j, kk: (i, j)))
        operands.append(add)
    dims = _DIMS[mode]
    has_add = add is not None

    def body(*refs):
        a_ref, b_ref = refs[0], refs[1]
        add_ref = refs[2] if has_add else None
        o_ref = refs[3] if has_add else refs[2]
        acc_ref = refs[-1] if nk > 1 else None
        prod = _bdot(a_ref[...], b_ref[...], dims)

        def finish(total):
            if has_add:
                total = total + add_ref[...].astype(F32)
            o_ref[...] = total.astype(o_ref.dtype)

        if nk == 1:
            finish(prod)
        else:
            kk = pl.program_id(2)

            @pl.when(kk == 0)
            def _():
                acc_ref[...] = prod

            @pl.when(jnp.logical_and(kk > 0, kk < nk - 1))
            def _():
                acc_ref[...] += prod

            @pl.when(kk == nk - 1)
            def _():
                finish(acc_ref[...] + prod)

    scratch = [pltpu.VMEM((tm, tn), F32)] if nk > 1 else []
    return pl.pallas_call(
        body, name=name, grid=grid, in_specs=in_specs, out_specs=o_spec, out_shape=out_shape,
        scratch_shapes=scratch,
        compiler_params=_cparams(("parallel", "parallel", "arbitrary")),
    )(*operands)


def _rms_fwd(name, x, gain, *, tr=512, after=None):
    r, d = x.shape
    tr = min(tr, r)

    def body(x_ref, g_ref, *rest):
        o_ref = rest[-1]
        xv = x_ref[...]
        rstd = lax.rsqrt(jnp.mean(xv * xv, axis=-1, keepdims=True) + RMS_EPS)
        o_ref[...] = (xv * rstd * g_ref[...]).astype(o_ref.dtype)

    in_specs = [pl.BlockSpec((tr, d), lambda i: (i, 0)), pl.BlockSpec((1, d), lambda i: (0, 0))]
    ops = [x, gain]
    if after is not None:
        in_specs.append(pl.BlockSpec(after.shape, lambda i: (0, 0)))
        ops.append(after)
    return pl.pallas_call(
        body, name=name, grid=(r // tr,), in_specs=in_specs,
        out_specs=pl.BlockSpec((tr, d), lambda i: (i, 0)),
        out_shape=jax.ShapeDtypeStruct((r, d), BF16),
        compiler_params=_cparams(("parallel",)),
    )(*ops)


def _rms_gain_grad(name, dy, x, *, tr=512):
    r, d = x.shape
    tr = min(tr, r)
    n = r // tr

    def body(dy_ref, x_ref, dg_ref, acc_ref):
        i = pl.program_id(0)
        xv = x_ref[...]
        xh = xv * lax.rsqrt(jnp.mean(xv * xv, axis=-1, keepdims=True) + RMS_EPS)
        part = (dy_ref[...].astype(F32) * xh).reshape(tr // 8, 8, d).sum(axis=0)

        @pl.when(i == 0)
        def _():
            acc_ref[...] = part

        @pl.when(i > 0)
        def _():
            acc_ref[...] += part

        @pl.when(i == n - 1)
        def _():
            dg_ref[...] = jnp.sum(acc_ref[...], axis=0, keepdims=True)

    row = pl.BlockSpec((tr, d), lambda i: (i, 0))
    return pl.pallas_call(
        body, name=name, grid=(n,), in_specs=[row, row],
        out_specs=pl.BlockSpec((1, d), lambda i: (0, 0)),
        out_shape=jax.ShapeDtypeStruct((1, d), F32),
        scratch_shapes=[pltpu.VMEM((8, d), F32)],
        compiler_params=_cparams(("arbitrary",)),
    )(dy, x)


def _matmul_rms_bwd(name, a, b, k, x, gain, res, *, tm=512, tk=1024, after=None):
    m, d = x.shape
    tm, tk = min(tm, m), min(tk, k)
    assert m % tm == 0 and k % tk == 0, (name, m, k, tm, tk)
    ni, nk = m // tm, k // tk

    def body(a_ref, b_ref, x_ref, g_ref, res_ref, *rest):
        dx_ref, dg_ref, acc_ref, accg_ref = rest[-4:]
        i, kk = pl.program_id(0), pl.program_id(1)
        prod = _bdot(a_ref[...], b_ref[...], _DIMS["nt"])

        @pl.when(kk == 0)
        def _():
            acc_ref[...] = prod

        @pl.when(kk > 0)
        def _():
            acc_ref[...] += prod

        @pl.when(kk == nk - 1)
        def _():
            dyv = acc_ref[...]
            xv = x_ref[...]
            rstd = lax.rsqrt(jnp.mean(xv * xv, axis=-1, keepdims=True) + RMS_EPS)
            xh = xv * rstd
            dxh = dyv * g_ref[...]
            dx_ref[...] = rstd * (dxh - xh * jnp.mean(dxh * xh, axis=-1, keepdims=True)) + res_ref[...]
            part = (dyv * xh).reshape(tm // 8, 8, d).sum(axis=0)

            @pl.when(i == 0)
            def _():
                accg_ref[...] = part

            @pl.when(i > 0)
            def _():
                accg_ref[...] += part

            @pl.when(i == ni - 1)
            def _():
                dg_ref[...] = jnp.sum(accg_ref[...], axis=0, keepdims=True)

    row = pl.BlockSpec((tm, d), lambda i, kk: (i, 0))
    one = pl.BlockSpec((1, d), lambda i, kk: (0, 0))
    in_specs = [pl.BlockSpec((tm, tk), lambda i, kk: (i, kk)), pl.BlockSpec((d, tk), lambda i, kk: (0, kk)), row, one, row]
    ops = [a, b, x, gain, res]
    if after is not None:
        in_specs.append(pl.BlockSpec(after.shape, lambda i, kk: (0, 0)))
        ops.append(after)
    return pl.pallas_call(
        body, name=name, grid=(ni, nk), in_specs=in_specs, out_specs=(row, one),
        out_shape=(jax.ShapeDtypeStruct((m, d), F32), jax.ShapeDtypeStruct((1, d), F32)),
        scratch_shapes=[pltpu.VMEM((tm, d), F32), pltpu.VMEM((8, d), F32)],
        compiler_params=_cparams(("arbitrary", "arbitrary")),
    )(*ops)


def _matmul_final_loss(name, a, b, res, target, gain, *, tr=512):
    r, d = res.shape
    k = a.shape[1]
    tr = min(tr, r)
    n = r // tr

    def body(a_ref, b_ref, res_ref, t_ref, g_ref, loss_ref, dh_ref, dg_ref, accl_ref, accg_ref):
        i = pl.program_id(0)
        xv = _bdot(a_ref[...], b_ref[...], _DIMS["nn"]) + res_ref[...]
        rstd = lax.rsqrt(jnp.mean(xv * xv, axis=-1, keepdims=True) + RMS_EPS)
        xh = xv * rstd
        e = xh * g_ref[...] - t_ref[...]
        dyv = e * (1.0 / d)
        dxh = dyv * g_ref[...]
        dh_ref[...] = rstd * (dxh - xh * jnp.mean(dxh * xh, axis=-1, keepdims=True))
        lpart = (e * e).reshape(tr // 8, 8, d).sum(axis=0)
        gpart = (dyv * xh).reshape(tr // 8, 8, d).sum(axis=0)

        @pl.when(i == 0)
        def _():
            accl_ref[...] = lpart
            accg_ref[...] = gpart

        @pl.when(i > 0)
        def _():
            accl_ref[...] += lpart
            accg_ref[...] += gpart

        @pl.when(i == n - 1)
        def _():
            tot = jnp.sum(jnp.sum(accl_ref[...], axis=0, keepdims=True), axis=1, keepdims=True)
            loss_ref[...] = jnp.broadcast_to(tot * (0.5 / d), (1, LANE))
            dg_ref[...] = jnp.sum(accg_ref[...], axis=0, keepdims=True)

    row = pl.BlockSpec((tr, d), lambda i: (i, 0))
    one = pl.BlockSpec((1, d), lambda i: (0, 0))
    return pl.pallas_call(
        body, name=name, grid=(n,),
        in_specs=[pl.BlockSpec((tr, k), lambda i: (i, 0)), pl.BlockSpec((k, d), lambda i: (0, 0)), row, row, one],
        out_specs=(pl.BlockSpec((1, LANE), lambda i: (0, 0)), row, one),
        out_shape=(jax.ShapeDtypeStruct((1, LANE), F32), jax.ShapeDtypeStruct((r, d), F32),
                   jax.ShapeDtypeStruct((1, d), F32)),
        scratch_shapes=[pltpu.VMEM((8, d), F32), pltpu.VMEM((8, d), F32)],
        compiler_params=_cparams(("arbitrary",)),
    )(a, b, res, target, gain)


_GELU_C = math.sqrt(2.0 / math.pi)


def _gelu_parts(z):
    inner = _GELU_C * (z + 0.044715 * z * z * z)
    t = jnp.tanh(inner)
    val = 0.5 * z * (1.0 + t)
    dinner = _GELU_C * (1.0 + 3.0 * 0.044715 * z * z)
    grad = 0.5 * (1.0 + t) + 0.5 * z * (1.0 - t * t) * dinner
    return val, grad


def _ssm_post_fwd(name, y8, u8, d8, *, tr=256):
    r, c = y8.shape
    tr = min(tr, r)

    def body(y_ref, u_ref, d_ref, o_ref):
        z = y_ref[...] + d_ref[...] * u_ref[...]
        o_ref[...] = _gelu_parts(z)[0].astype(o_ref.dtype)

    row = pl.BlockSpec((tr, c), lambda i: (i, 0))
    return pl.pallas_call(
        body, name=name, grid=(r // tr,), in_specs=[row, row, pl.BlockSpec((1, c), lambda i: (0, 0))],
        out_specs=row, out_shape=jax.ShapeDtypeStruct((r, c), BF16),
        compiler_params=_cparams(("parallel",)),
    )(y8, u8, d8)


def _ssm_post_bwd(name, dact8, y8, u8, d8, *, tr=256, after=None):
    r, c = y8.shape
    tr = min(tr, r)
    n = r // tr

    def body(*refs):
        da_ref, y_ref, u_ref, d_ref = refs[:4]
        dz_ref, dd_ref, acc_ref = refs[-3:]
        i = pl.program_id(0)
        uv = u_ref[...]
        z = y_ref[...] + d_ref[...] * uv
        dz = da_ref[...].astype(F32) * _gelu_parts(z)[1]
        dz_ref[...] = dz
        part = (dz * uv).reshape(tr // 8, 8, c).sum(axis=0)

        @pl.when(i == 0)
        def _():
            acc_ref[...] = part

        @pl.when(i > 0)
        def _():
            acc_ref[...] += part

        @pl.when(i == n - 1)
        def _():
            tot = jnp.sum(acc_ref[...], axis=0, keepdims=True)
            out = tot[:, 0:SSM_WIDTH]
            for j in range(1, c // SSM_WIDTH):
                out = out + tot[:, j * SSM_WIDTH:(j + 1) * SSM_WIDTH]
            dd_ref[...] = out

    row = pl.BlockSpec((tr, c), lambda i: (i, 0))
    in_specs = [row, row, row, pl.BlockSpec((1, c), lambda i: (0, 0))]
    ops = [dact8, y8, u8, d8]
    if after is not None:
        in_specs.append(pl.BlockSpec(memory_space=pl.ANY))
        ops.append(after)
    return pl.pallas_call(
        body, name=name, grid=(n,), in_specs=in_specs,
        out_specs=(row, pl.BlockSpec((1, SSM_WIDTH), lambda i: (0, 0))),
        out_shape=(jax.ShapeDtypeStruct((r, c), F32), jax.ShapeDtypeStruct((1, SSM_WIDTH), F32)),
        scratch_shapes=[pltpu.VMEM((8, c), F32)],
        compiler_params=_cparams(("arbitrary",)),
    )(*ops)


def _mix_fwd(name, glu, gates, out_b, *, tr=256):
    r = glu.shape[0]
    d = D_MODEL
    tr = min(tr, r)

    def body(glu_ref, gate_ref, ob_ref, o_ref):
        out_a = glu_ref[:, 0:d].astype(F32) * _sigmoid(glu_ref[:, d:2 * d].astype(F32))
        mix = (_sigmoid(gate_ref[:, 0:d].astype(F32)) * out_a
               + _sigmoid(gate_ref[:, d:2 * d].astype(F32)) * ob_ref[...].astype(F32))
        o_ref[...] = mix.astype(o_ref.dtype)

    wide = pl.BlockSpec((tr, 2 * d), lambda i: (i, 0))
    row = pl.BlockSpec((tr, d), lambda i: (i, 0))
    return pl.pallas_call(
        body, name=name, grid=(r // tr,), in_specs=[wide, wide, row], out_specs=row,
        out_shape=jax.ShapeDtypeStruct((r, d), BF16), compiler_params=_cparams(("parallel",)),
    )(glu, gates, out_b)


def _mix_bwd(name, dmix, glu, gates, out_b, *, tr=256):
    r = glu.shape[0]
    d = D_MODEL
    tr = min(tr, r)

    def body(dm_ref, glu_ref, gate_ref, ob_ref, dglu_ref, dgate_ref, dob_ref):
        dm = dm_ref[...]
        glu_a = glu_ref[:, 0:d].astype(F32)
        sb = _sigmoid(glu_ref[:, d:2 * d].astype(F32))
        ga = _sigmoid(gate_ref[:, 0:d].astype(F32))
        gb = _sigmoid(gate_ref[:, d:2 * d].astype(F32))
        out_a = glu_a * sb
        dout_a = dm * ga
        dglu_ref[:, 0:d] = (dout_a * sb).astype(dglu_ref.dtype)
        dglu_ref[:, d:2 * d] = (dout_a * glu_a * sb * (1.0 - sb)).astype(dglu_ref.dtype)
        dgate_ref[:, 0:d] = (dm * out_a * ga * (1.0 - ga)).astype(dgate_ref.dtype)
        dgate_ref[:, d:2 * d] = (dm * ob_ref[...].astype(F32) * gb * (1.0 - gb)).astype(dgate_ref.dtype)
        dob_ref[...] = (dm * gb).astype(dob_ref.dtype)

    wide = pl.BlockSpec((tr, 2 * d), lambda i: (i, 0))
    row = pl.BlockSpec((tr, d), lambda i: (i, 0))
    return pl.pallas_call(
        body, name=name, grid=(r // tr,), in_specs=[row, wide, wide, row], out_specs=(wide, wide, row),
        out_shape=(jax.ShapeDtypeStruct((r, 2 * d), BF16), jax.ShapeDtypeStruct((r, 2 * d), BF16),
                   jax.ShapeDtypeStruct((r, d), BF16)),
        compiler_params=_cparams(("parallel",)),
    )(dmix, glu, gates, out_b)


def _ssm_mats(lam_re, lam_im, log_dt, b_re, b_im, c_re, c_im, nc):
    hp = lax.Precision.HIGHEST
    t = SSM_CHUNK
    nq = SSM_GROUPS // 8
    lam = lax.complex(lam_re, lam_im)
    z = lam * jnp.exp(log_dt)[:, None]
    ks = jnp.arange(t + 1, dtype=F32)
    apow = jnp.exp(ks[:, None, None] * z[None])
    bbar = ((apow[1] - 1.0) / lam)[..., None] * lax.complex(b_re, b_im)
    c = lax.complex(c_re, c_im)

    ca = c[None] * apow[:, :, None, :]
    kmat = jnp.einsum("kgnp,gpm->kgnm", ca, bbar, precision=hp).real
    ii = np.arange(t)
    lag = ii[None, :] - ii[:, None]
    kt = kmat[np.clip(lag, 0, t)] * jnp.asarray(lag >= 0, F32)[:, :, None, None, None]
    kt = kt.reshape(t, t, nq, 8, SSM_GROUP, SSM_GROUP)
    m_c = kt.transpose(2, 0, 3, 5, 1, 4).reshape(nq, 1024, LANE)

    arev = jnp.exp((float(t - 1) - ks[:t])[:, None, None] * z[None])
    w = arev[:, :, :, None] * bbar[None]
    wr = jnp.stack([w.real, w.imag]).reshape(2, t, nq, 8, SSM_STATE, SSM_GROUP)
    bw_c = wr.transpose(2, 1, 3, 5, 0, 4).reshape(nq, 1024, LANE)

    ca1 = ca[1:]
    cr = jnp.stack([ca1.real, -ca1.imag]).reshape(2, t, nq, 8, SSM_GROUP, SSM_STATE)
    cm_c = cr.transpose(2, 0, 3, 5, 1, 4).reshape(nq, 1024, LANE)

    def tiles(v):
        vq = jnp.concatenate([v.real.reshape(nq, 512), v.imag.reshape(nq, 512)], axis=1)
        return jnp.broadcast_to(vq.reshape(nq, 8, 1, LANE), (nq, 8, 8, LANE))

    return m_c, bw_c, cm_c, tiles(apow[t]), tiles(jnp.exp(float(nc) * z))


_BD_M = (LANE, SSM_GROUP)
_BD_BW = (LANE, SSM_STATE)
_BD_CM = (512, SSM_GROUP)


def _bd_perm(cn):
    rr = lax.broadcasted_iota(jnp.int32, (1024, 1024), 0)
    cc = lax.broadcasted_iota(jnp.int32, (1024, 1024), 1)
    sh = cn.bit_length() - 1
    src = ((rr >> 7) << sh) + (((rr & (LANE - 1)) >> sh) << (3 + sh)) + (rr & (cn - 1))
    return jnp.where(src == cc, 1.0, 0.0).astype(BF16)


def _bd_rowgroup(span):
    r = lax.broadcasted_iota(jnp.int32, (1024, LANE), 0)
    return (r & (span - 1)) >> ((span // 8).bit_length() - 1)


def _bd_expand(name, kind, compact):
    span, cn = kind
    nq = compact.shape[0]

    def body(c_ref, o_ref, perm_scr):
        @pl.when(pl.program_id(0) == 0)
        def _():
            perm_scr[...] = _bd_perm(cn)

        x = c_ref[...]
        grp = _bd_rowgroup(span)
        xcat = jnp.concatenate([jnp.where(grp == h, x, 0.0) for h in range(8)], axis=1)
        o_ref[...] = _bdot(xcat, perm_scr[...], _DIMS["nn"]).astype(o_ref.dtype)

    return pl.pallas_call(
        body, name=name, grid=(nq,), in_specs=[pl.BlockSpec((None, 1024, LANE), lambda q: (q, 0, 0))],
        out_specs=pl.BlockSpec((None, 1024, 1024), lambda q: (q, 0, 0)),
        out_shape=jax.ShapeDtypeStruct((nq, 1024, 1024), BF16),
        scratch_shapes=[pltpu.VMEM((1024, 1024), BF16)],
        compiler_params=_cparams(("arbitrary",)),
    )(compact)


def _bd_reduce(name, kind, dbig):
    span, cn = kind
    nq = dbig.shape[0]

    def body(g_ref, o_ref, perm_scr):
        @pl.when(pl.program_id(0) == 0)
        def _():
            perm_scr[...] = _bd_perm(cn)

        back = _bdot(g_ref[...], perm_scr[...], _DIMS["nt"])
        grp = _bd_rowgroup(span)
        out = jnp.zeros((1024, LANE), F32)
        for h in range(8):
            out = jnp.where(grp == h, back[:, h * LANE:(h + 1) * LANE], out)
        o_ref[...] = out

    return pl.pallas_call(
        body, name=name, grid=(nq,), in_specs=[pl.BlockSpec((None, 1024, 1024), lambda q: (q, 0, 0))],
        out_specs=pl.BlockSpec((None, 1024, LANE), lambda q: (q, 0, 0)),
        out_shape=jax.ShapeDtypeStruct((nq, 1024, LANE), F32),
        scratch_shapes=[pltpu.VMEM((1024, 1024), BF16)],
        compiler_params=_cparams(("arbitrary",)),
    )(dbig)


def _x_tile_specs(nc, nq):
    return [pl.BlockSpec((nc, LANE), lambda q, t, i=i: (0, i * nq + q)) for i in range(SSM_CHUNK)]


def _cat_tiles(refs):
    return jnp.concatenate([r[...] for r in refs], axis=1)


def _ssm_w(name, x8, bw):
    nc = x8.shape[0]
    nq = bw.shape[0]

    def body(*refs):
        xq = _cat_tiles(refs[:8])
        refs[9][...] = _bdot(xq, refs[8][...], _DIMS["nn"])

    return pl.pallas_call(
        body, name=name, grid=(nq, 8),
        in_specs=_x_tile_specs(nc, nq) + [pl.BlockSpec((None, 1024, LANE), lambda q, t: (q, 0, t))],
        out_specs=pl.BlockSpec((None, None, nc, LANE), lambda q, t: (q, t, 0, 0)),
        out_shape=jax.ShapeDtypeStruct((nq, 8, nc, LANE), F32),
        compiler_params=_cparams(("parallel", "arbitrary")),
    )(*([x8] * 8), bw)


def _ssm_scan(name, w4, a_t, aseg_t, *, reverse, sprev4=None):
    nq, _, nc, _ = w4.shape
    ns = nc // 8
    with_da = sprev4 is not None

    def body(*refs):
        w_ref, a_ref, aseg_ref = refs[:3]
        s_ref = refs[3] if with_da else None
        o_ref = refs[4] if with_da else refs[3]
        da_ref = refs[5] if with_da else None
        sgn = -1.0 if reverse else 1.0
        ar = [a_ref[j] for j in range(4)]
        ai = [sgn * a_ref[j + 4] for j in range(4)]
        gr = [aseg_ref[j] for j in range(4)]
        gi = [sgn * aseg_ref[j + 4] for j in range(4)]
        zero = tuple(jnp.zeros((8, LANE), F32) for _ in range(8))

        def rows(tt):
            return pl.ds((ns - 1 - tt) if reverse else tt, 8, stride=ns)

        def step(carry, w):
            new_r = [ar[j] * carry[j] - ai[j] * carry[j + 4] + w[j] for j in range(4)]
            new_i = [ar[j] * carry[j + 4] + ai[j] * carry[j] + w[j + 4] for j in range(4)]
            return tuple(new_r + new_i)

        def pass1(tt, carry):
            return step(carry, [w_ref[j, rows(tt), :] for j in range(8)])

        ends = lax.fori_loop(0, ns, pass1, zero)
        sub = lax.broadcasted_iota(jnp.int32, (8, LANE), 0)
        init = list(zero)
        order = range(7, 0, -1) if reverse else range(0, 7)
        for s in order:
            nxt = s - 1 if reverse else s + 1
            cand_r = [gr[j] * init[j] - gi[j] * init[j + 4] + ends[j] for j in range(4)]
            cand_i = [gr[j] * init[j + 4] + gi[j] * init[j] + ends[j + 4] for j in range(4)]
            cand = cand_r + cand_i
            shift = 7 if reverse else 1
            init = [jnp.where(sub == nxt, pltpu.roll(cand[j], shift, axis=0), init[j]) for j in range(8)]

        def pass2(tt, state):
            carry, acc = state
            r = rows(tt)
            for j in range(8):
                o_ref[j, r, :] = carry[j]
            if with_da:
                sp = [s_ref[j, r, :] for j in range(8)]
                acc_r = [acc[j] + carry[j] * sp[j] + carry[j + 4] * sp[j + 4] for j in range(4)]
                acc_i = [acc[j + 4] + carry[j + 4] * sp[j] - carry[j] * sp[j + 4] for j in range(4)]
                acc = tuple(acc_r + acc_i)
            return step(carry, [w_ref[j, r, :] for j in range(8)]), acc

        _, acc = lax.fori_loop(0, ns, pass2, (tuple(init), zero))
        if with_da:
            for j in range(8):
                da_ref[j] = acc[j]

    big = pl.BlockSpec((None, 8, nc, LANE), lambda q: (q, 0, 0, 0))
    small = pl.BlockSpec((None, 8, 8, LANE), lambda q: (q, 0, 0, 0))
    in_specs = [big, small, small] + ([big] if with_da else [])
    ops = [w4, a_t, aseg_t] + ([sprev4] if with_da else [])
    out_specs = (big, small) if with_da else big
    big_s = jax.ShapeDtypeStruct((nq, 8, nc, LANE), F32)
    out_shape = (big_s, jax.ShapeDtypeStruct((nq, 8, 8, LANE), F32)) if with_da else big_s
    return pl.pallas_call(
        body, name=name, grid=(nq,), in_specs=in_specs, out_specs=out_specs, out_shape=out_shape,
        compiler_params=_cparams(("parallel",)),
    )(*ops)


def _ssm_y(name, x8, sprev4, m_mat, cm_mat):
    nc = x8.shape[0]
    nq = m_mat.shape[0]

    def body(*refs):
        xq = _cat_tiles(refs[:8])
        s_ref, m_ref, cm_ref, o_ref = refs[8:12]
        sq = jnp.concatenate([s_ref[t] for t in range(8)], axis=1)
        o_ref[...] = _bdot(xq, m_ref[...], _DIMS["nn"]) + _bdot(sq, cm_ref[...], _DIMS["nn"])

    col = pl.BlockSpec((None, 1024, LANE), lambda q, j: (q, 0, j))
    return pl.pallas_call(
        body, name=name, grid=(nq, 8),
        in_specs=_x_tile_specs(nc, nq) + [pl.BlockSpec((None, 8, nc, LANE), lambda q, j: (q, 0, 0, 0)), col, col],
        out_specs=pl.BlockSpec((nc, LANE), lambda q, j: (0, j * nq + q)),
        out_shape=jax.ShapeDtypeStruct((nc, 8 * SSM_WIDTH), F32),
        compiler_params=_cparams(("parallel", "arbitrary")),
    )(*([x8] * 8), sprev4, m_mat, cm_mat)


def _ssm_ds(name, dz8, sprev4, cm_mat):
    nc = dz8.shape[0]
    nq = cm_mat.shape[0]

    def body(*refs):
        dyq = _cat_tiles(refs[:8]).astype(BF16)
        s_ref, cm_ref, ds_ref, dcm_ref = refs[8:12]
        ds_ref[...] = _bdot(dyq, cm_ref[...], _DIMS["nt"])
        dcm_ref[...] = _bdot(s_ref[...], dyq, _DIMS["tn"])

    tile = pl.BlockSpec((None, None, nc, LANE), lambda q, t: (q, t, 0, 0))
    rowblk = pl.BlockSpec((None, LANE, 1024), lambda q, t: (q, t, 0))
    return pl.pallas_call(
        body, name=name, grid=(nq, 8),
        in_specs=_x_tile_specs(nc, nq) + [tile, rowblk],
        out_specs=(tile, rowblk),
        out_shape=(jax.ShapeDtypeStruct((nq, 8, nc, LANE), F32), jax.ShapeDtypeStruct((nq, 1024, 1024), F32)),
        compiler_params=_cparams(("parallel", "arbitrary")),
    )(*([dz8] * 8), sprev4, cm_mat)


def _ssm_dx(name, dz8, g4, x8, m_mat, bw_mat, d8):
    nc = dz8.shape[0]
    nq = m_mat.shape[0]

    def body(*refs):
        dyq = _cat_tiles(refs[:8]).astype(BF16)
        g_ref, x_ref, m_ref, bw_ref, d_ref, dzi_ref, dx_ref, dm_ref, dbw_ref = refs[8:17]
        gq = jnp.concatenate([g_ref[t] for t in range(8)], axis=1).astype(BF16)
        dx = _bdot(dyq, m_ref[...], _DIMS["nt"]) + _bdot(gq, bw_ref[...], _DIMS["nt"])
        dx_ref[...] = (dx + d_ref[...] * dzi_ref[...]).astype(dx_ref.dtype)
        xi = x_ref[...]
        dm_ref[...] = _bdot(xi, dyq, _DIMS["tn"])
        dbw_ref[...] = _bdot(xi, gq, _DIMS["tn"])

    xtile = pl.BlockSpec((nc, LANE), lambda q, i: (0, i * nq + q))
    rowblk = pl.BlockSpec((None, LANE, 1024), lambda q, i: (q, i, 0))
    return pl.pallas_call(
        body, name=name, grid=(nq, 8),
        in_specs=_x_tile_specs(nc, nq) + [pl.BlockSpec((None, 8, nc, LANE), lambda q, i: (q, 0, 0, 0)), xtile, rowblk, rowblk,
                                          pl.BlockSpec((1, LANE), lambda q, i: (0, q)), xtile],
        out_specs=(xtile, rowblk, rowblk),
        out_shape=(jax.ShapeDtypeStruct((nc, 8 * SSM_WIDTH), BF16), jax.ShapeDtypeStruct((nq, 1024, 1024), F32),
                   jax.ShapeDtypeStruct((nq, 1024, 1024), F32)),
        compiler_params=_cparams(("parallel", "arbitrary")),
    )(*([dz8] * 8), g4, x8, m_mat, bw_mat, d8, dz8)


CUM_BLK = 256


def _split3(x):
    hi = x.astype(BF16)
    r1 = x - hi.astype(F32)
    mid = r1.astype(BF16)
    lo = (r1 - mid.astype(F32)).astype(BF16)
    return hi, mid, lo


def _tri_dot(x, tri):
    hi, mid, lo = _split3(x)
    d = _DIMS["nn"]
    return _bdot(hi, tri, d) + _bdot(mid, tri, d) + _bdot(lo, tri, d)


def _tri(n, lower):
    r = lax.broadcasted_iota(jnp.int32, (n, n), 0)
    c = lax.broadcasted_iota(jnp.int32, (n, n), 1)
    return jnp.where((r >= c) if lower else (r <= c), 1.0, 0.0).astype(BF16)


def _fox_cum(name, fproj, bcol):
    seq = fproj.shape[0]
    blk = min(CUM_BLK, seq)

    def body(f_ref, b_ref, o_ref, carry_ref):
        i = pl.program_id(0)

        @pl.when(i == 0)
        def _():
            carry_ref[...] = jnp.zeros_like(carry_ref)

        z = f_ref[...].T + b_ref[...]
        logf = jnp.minimum(z, 0.0) - jnp.log(1.0 + jnp.exp(-jnp.abs(z)))
        carry = carry_ref[...]
        cum = _tri_dot(logf, _tri(blk, lower=False)) + jnp.tile(carry, (1, blk // LANE))
        o_ref[...] = cum[0:8, :]
        carry_ref[...] = carry + jnp.sum(logf, axis=1, keepdims=True)

    return pl.pallas_call(
        body, name=name, grid=(seq // blk,),
        in_specs=[pl.BlockSpec((blk, LANE), lambda i: (i, 0)), pl.BlockSpec((LANE, 1), lambda i: (0, 0))],
        out_specs=pl.BlockSpec((8, blk), lambda i: (0, i)),
        out_shape=jax.ShapeDtypeStruct((8, seq), F32),
        scratch_shapes=[pltpu.VMEM((LANE, LANE), F32)],
        compiler_params=_cparams(("arbitrary",)),
    )(fproj, bcol)


def _fox_cum_bwd(name, dcs, fproj, bcol):
    seq = fproj.shape[0]
    blk = min(CUM_BLK, seq)
    n = seq // blk

    def body(dc_ref, f_ref, b_ref, df_ref, db_ref, carry_ref, acc_ref):
        i = pl.program_id(0)

        @pl.when(i == 0)
        def _():
            carry_ref[...] = jnp.zeros_like(carry_ref)
            acc_ref[...] = jnp.zeros_like(acc_ref)

        r = lax.broadcasted_iota(jnp.int32, (LANE, FOX_WIDTH), 0)
        c = lax.broadcasted_iota(jnp.int32, (LANE, FOX_WIDTH), 1)
        want = (r >> 1) * LANE + jnp.where((r & 1) == 0, FOX_HEAD_DIM, 0)
        sel = jnp.where(jnp.logical_and(r < FOX_HEADS, c == want), 1.0, 0.0).astype(BF16)
        hi, mid, lo = _split3(dc_ref[...])
        nt = _DIMS["nt"]
        dc = _bdot(sel, hi, nt) + _bdot(sel, mid, nt) + _bdot(sel, lo, nt)
        carry = carry_ref[...]
        dlogf = _tri_dot(dc, _tri(blk, lower=True)) + jnp.tile(carry, (1, blk // LANE))
        carry_ref[...] = carry + jnp.sum(dc, axis=1, keepdims=True)
        z = f_ref[...].T + b_ref[...]
        dft = dlogf / (1.0 + jnp.exp(z))
        df_ref[...] = dft.T.astype(df_ref.dtype)
        acc_ref[...] += jnp.sum(dft, axis=1, keepdims=True)

        @pl.when(i == n - 1)
        def _():
            db_ref[...] = acc_ref[...]

    return pl.pallas_call(
        body, name=name, grid=(n,),
        in_specs=[pl.BlockSpec((blk, FOX_WIDTH), lambda i: (n - 1 - i, 0)), pl.BlockSpec((blk, LANE), lambda i: (n - 1 - i, 0)),
                  pl.BlockSpec((LANE, 1), lambda i: (0, 0))],
        out_specs=(pl.BlockSpec((blk, LANE), lambda i: (n - 1 - i, 0)), pl.BlockSpec((LANE, LANE), lambda i: (0, 0))),
        out_shape=(jax.ShapeDtypeStruct((seq, LANE), BF16), jax.ShapeDtypeStruct((LANE, LANE), F32)),
        scratch_shapes=[pltpu.VMEM((LANE, LANE), F32), pltpu.VMEM((LANE, LANE), F32)],
        compiler_params=_cparams(("arbitrary",)),
    )(dcs, fproj, bcol)


FOX_BLK = 512
FOX_SCALE = FOX_HEAD_DIM ** -0.5


def _fox_head_mask(shape, hh):
    lane = lax.broadcasted_iota(jnp.int32, shape, 1)
    return (lane < FOX_HEAD_DIM) if hh == 0 else (lane >= FOX_HEAD_DIM)


def _fox_bias(cum_ref, hh, q0, k0, blk):
    c0 = jnp.max(cum_ref[hh:hh + 1, pl.ds(q0, LANE)], axis=1, keepdims=True)
    return c0 - cum_ref[hh:hh + 1, pl.ds(k0, blk)]


def _fox_fwd(name, qkv, cum_t):
    seq = qkv.shape[0]
    blk = min(FOX_BLK, seq)
    nb = seq // blk
    npair = FOX_HEADS // 2

    def body(q_ref, k_ref, v_ref, cum_ref, o_ref, lse_ref):
        iq = pl.program_id(1)
        q0 = pl.multiple_of(iq * blk, blk)
        qv = q_ref[...]
        row = lax.broadcasted_iota(jnp.int32, (blk, blk), 0)
        col = lax.broadcasted_iota(jnp.int32, (blk, blk), 1)
        qhs = [jnp.where(_fox_head_mask(qv.shape, hh), qv, jnp.zeros_like(qv)) * FOX_SCALE for hh in range(2)]

        def block(kb, states, masked):
            k0 = pl.multiple_of(kb * blk, blk)
            kv = k_ref[pl.ds(k0, blk), :]
            vv = v_ref[pl.ds(k0, blk), :]
            new = []
            for hh in range(2):
                m, acc = states[hh]
                s = _bdot(qhs[hh], kv, _DIMS["nt"]) + _fox_bias(cum_ref, hh, q0, k0, blk)
                if masked:
                    s = jnp.where(row >= col, s, -jnp.inf)
                m_new = jnp.maximum(m, jnp.max(s, axis=1, keepdims=True))
                p = jnp.exp(s - m_new)
                vh = jnp.where(_fox_head_mask(vv.shape, hh), vv, jnp.ones_like(vv))
                acc = jnp.exp(m - m_new) * acc + _bdot(p, vh, _DIMS["nn"])
                new.append((m_new, acc))
            return tuple(new)

        init = (jnp.full((blk, 1), -jnp.inf, F32), jnp.zeros((blk, LANE), F32))
        states = lax.fori_loop(0, iq, lambda kb, st: block(kb, st, False), (init, init))
        states = block(iq, states, True)
        outs = []
        for hh in range(2):
            m, acc = states[hh]
            other = pltpu.roll(acc, FOX_HEAD_DIM, axis=1)
            outs.append(acc / other)
            lse_ref[hh] = m + jnp.log(jnp.where(_fox_head_mask(acc.shape, hh), other, acc))
        o_ref[...] = jnp.where(_fox_head_mask(outs[0].shape, 0), outs[0], outs[1]).astype(o_ref.dtype)

    return pl.pallas_call(
        body, name=name, grid=(npair, nb),
        in_specs=[pl.BlockSpec((blk, LANE), lambda p, i: (i, p)),
                  pl.BlockSpec((seq, LANE), lambda p, i: (0, npair + p)),
                  pl.BlockSpec((seq, LANE), lambda p, i: (0, 2 * npair + p)),
                  pl.BlockSpec((None, 2, seq), lambda p, i: (p, 0, 0))],
        out_specs=(pl.BlockSpec((blk, LANE), lambda p, i: (i, p)),
                   pl.BlockSpec((2, blk, LANE), lambda p, i: (p, i, 0))),
        out_shape=(jax.ShapeDtypeStruct((seq, FOX_WIDTH), BF16), jax.ShapeDtypeStruct((FOX_HEADS, seq, LANE), F32)),
        compiler_params=_cparams(("parallel", "arbitrary")),
    )(qkv, qkv, qkv, cum_t)


def _fox_bwd(name, qkv, cum_t, att, datt, lse):
    seq = qkv.shape[0]
    blk = min(FOX_BLK, seq)
    nb = seq // blk
    npair = FOX_HEADS // 2

    def body(q_ref, k_ref, v_ref, cum_ref, o_ref, do_ref, lse_ref, dq_ref, dk_ref, dv_ref, dcs_ref):
        iq = pl.program_id(1)
        q0 = pl.multiple_of(iq * blk, blk)

        @pl.when(iq == 0)
        def _():
            dk_ref[...] = jnp.zeros_like(dk_ref)
            dv_ref[...] = jnp.zeros_like(dv_ref)
            dcs_ref[...] = jnp.zeros_like(dcs_ref)

        qv = q_ref[...]
        dov = do_ref[...].astype(F32)
        ov = o_ref[...].astype(F32)
        row = lax.broadcasted_iota(jnp.int32, (blk, blk), 0)
        col = lax.broadcasted_iota(jnp.int32, (blk, blk), 1)
        low = _fox_head_mask((blk, LANE), 0)
        qhs, qones, dohbs, deltas, lses = [], [], [], [], []
        for hh in range(2):
            hm = _fox_head_mask(qv.shape, hh)
            qh = jnp.where(hm, qv, jnp.zeros_like(qv)) * FOX_SCALE
            qhs.append(qh)
            qones.append(jnp.where(hm, qh, jnp.ones_like(qh)))
            doh = jnp.where(hm, dov, 0.0)
            dohbs.append(doh.astype(BF16))
            deltas.append(jnp.sum(doh * ov, axis=1, keepdims=True))
            lses.append(jnp.tile(lse_ref[hh], (1, blk // LANE)))

        def block(kb, dqs, masked):
            k0 = pl.multiple_of(kb * blk, blk)
            kv = k_ref[pl.ds(k0, blk), :]
            vv = v_ref[pl.ds(k0, blk), :]
            new, dks, dvs = [], [], []
            for hh in range(2):
                s = _bdot(qhs[hh], kv, _DIMS["nt"]) + _fox_bias(cum_ref, hh, q0, k0, blk)
                p = jnp.exp(s - lses[hh])
                if masked:
                    p = jnp.where(row >= col, p, 0.0)
                dp = _bdot(dohbs[hh], vv, _DIMS["nt"])
                dsb = (p * (dp - deltas[hh])).astype(BF16)
                dks.append(_bdot(dsb, qones[hh], _DIMS["tn"]))
                dvs.append(_bdot(p, dohbs[hh], _DIMS["tn"]))
                kones = jnp.where(_fox_head_mask(kv.shape, hh), kv, jnp.ones_like(kv))
                new.append(dqs[hh] + _bdot(dsb, kones, _DIMS["nn"]))
            dk_ref[pl.ds(k0, blk), :] += jnp.where(low, dks[0], dks[1])
            dv_ref[pl.ds(k0, blk), :] += dvs[0] + dvs[1]
            dcs_ref[pl.ds(k0, blk), :] -= jnp.where(low, dks[1], dks[0])
            return tuple(new)

        init = jnp.zeros((blk, LANE), F32)
        dqs = lax.fori_loop(0, iq, lambda kb, a: block(kb, a, False), (init, init))
        dqs = block(iq, dqs, True)
        dcs_ref[pl.ds(q0, blk), :] += jnp.where(low, dqs[1], dqs[0])
        dq_ref[...] = (jnp.where(low, dqs[0], dqs[1]) * FOX_SCALE).astype(dq_ref.dtype)

    qblk = pl.BlockSpec((blk, LANE), lambda p, i: (i, p))
    full = pl.BlockSpec((seq, LANE), lambda p, i: (0, p))
    return pl.pallas_call(
        body, name=name, grid=(npair, nb),
        in_specs=[qblk,
                  pl.BlockSpec((seq, LANE), lambda p, i: (0, npair + p)),
                  pl.BlockSpec((seq, LANE), lambda p, i: (0, 2 * npair + p)),
                  pl.BlockSpec((None, 2, seq), lambda p, i: (p, 0, 0)),
                  qblk, qblk,
                  pl.BlockSpec((2, blk, LANE), lambda p, i: (p, i, 0))],
        out_specs=(qblk, full, full, full),
        out_shape=(jax.ShapeDtypeStruct((seq, FOX_WIDTH), BF16), jax.ShapeDtypeStruct((seq, FOX_WIDTH), F32),
                   jax.ShapeDtypeStruct((seq, FOX_WIDTH), F32), jax.ShapeDtypeStruct((seq, FOX_WIDTH), F32)),
        compiler_params=_cparams(("arbitrary", "arbitrary")),
    )(qkv, qkv, qkv, cum_t, att, datt, lse)


MEM_SCALE = MEM_HEAD_DIM ** -0.5


def _mem_probs(qh, kh):
    s = _bdot(qh, kh, _DIMS["nt"]) * MEM_SCALE
    p = jnp.exp(s - jnp.max(s, axis=1, keepdims=True))
    return p / jnp.sum(p, axis=1, keepdims=True)


def _mem_fwd(name, q2, kv, *, tr=512):
    seq = q2.shape[0]
    mlen = kv.shape[0]
    tr = min(tr, seq)

    def body(q_ref, kv_ref, o_ref):
        for h in range(MEM_HEADS):
            sl = slice(h * MEM_HEAD_DIM, (h + 1) * MEM_HEAD_DIM)
            sv = slice(MEM_WIDTH + h * MEM_HEAD_DIM, MEM_WIDTH + (h + 1) * MEM_HEAD_DIM)
            p = _mem_probs(q_ref[:, sl], kv_ref[:, sl])
            o_ref[:, sl] = _bdot(p, kv_ref[:, sv], _DIMS["nn"]).astype(o_ref.dtype)

    return pl.pallas_call(
        body, name=name, grid=(seq // tr,),
        in_specs=[pl.BlockSpec((tr, MEM_WIDTH), lambda i: (i, 0)), pl.BlockSpec((mlen, 2 * MEM_WIDTH), lambda i: (0, 0))],
        out_specs=pl.BlockSpec((tr, MEM_WIDTH), lambda i: (i, 0)),
        out_shape=jax.ShapeDtypeStruct((seq, MEM_WIDTH), BF16),
        compiler_params=_cparams(("parallel",)),
    )(q2, kv)


def _mem_bwd(name, q2, kv, do2, *, tr=512):
    seq = q2.shape[0]
    mlen = kv.shape[0]
    tr = min(tr, seq)

    def body(q_ref, kv_ref, do_ref, dq_ref, dkv_ref):
        i = pl.program_id(0)

        @pl.when(i == 0)
        def _():
            dkv_ref[...] = jnp.zeros_like(dkv_ref)

        for h in range(MEM_HEADS):
            sl = slice(h * MEM_HEAD_DIM, (h + 1) * MEM_HEAD_DIM)
            sv = slice(MEM_WIDTH + h * MEM_HEAD_DIM, MEM_WIDTH + (h + 1) * MEM_HEAD_DIM)
            qh = q_ref[:, sl]
            kh = kv_ref[:, sl]
            doh = do_ref[:, sl].astype(BF16)
            p = _mem_probs(qh, kh)
            dp = _bdot(doh, kv_ref[:, sv], _DIMS["nt"])
            ds = (p * (dp - jnp.sum(p * dp, axis=1, keepdims=True)) * MEM_SCALE).astype(BF16)
            dq_ref[:, sl] = _bdot(ds, kh, _DIMS["nn"]).astype(dq_ref.dtype)
            dkv_ref[:, sl] += _bdot(ds, qh, _DIMS["tn"])
            dkv_ref[:, sv] += _bdot(p, doh, _DIMS["tn"])

    row = pl.BlockSpec((tr, MEM_WIDTH), lambda i: (i, 0))
    kvs = pl.BlockSpec((mlen, 2 * MEM_WIDTH), lambda i: (0, 0))
    return pl.pallas_call(
        body, name=name, grid=(seq // tr,), in_specs=[row, kvs, row], out_specs=(row, kvs),
        out_shape=(jax.ShapeDtypeStruct((seq, MEM_WIDTH), BF16), jax.ShapeDtypeStruct((mlen, 2 * MEM_WIDTH), F32)),
        compiler_params=_cparams(("arbitrary",)),
    )(q2, kv, do2)


_HBM = pl.BlockSpec(memory_space=pl.ANY)
_HBM_ONLY = pl.BlockSpec(memory_space=pltpu.HBM)
_MESH = pl.DeviceIdType.MESH


def _mesh_place():
    x, y, c = lax.axis_index("x"), lax.axis_index("y"), lax.axis_index("c")
    other_chips = [(1 - x, y), (x, 1 - y), (1 - x, 1 - y)]
    return x, y, c, other_chips


def _gather_all(name, arrays):
    n = len(arrays)

    def body(*refs):
        ins, outs = refs[:n], refs[n:2 * n]
        send_sems, recv_sems, local_sems = refs[2 * n:]
        x, y, c, chips = _mesh_place()
        me, sibling = (x, y, c), (x, y, 1 - c)

        def slot(a, place):
            px, py, pc = place
            return outs[a].at[4 * px + 2 * py + pc]

        def copy(a, k, block, to, src=None):
            return pltpu.make_async_remote_copy(
                src_ref=slot(a, block) if src is None else src, dst_ref=slot(a, block),
                send_sem=send_sems.at[a, k], recv_sem=recv_sems.at[a, k], device_id=to, device_id_type=_MESH)

        mine = [pltpu.make_async_copy(ins[a], slot(a, me), local_sems.at[a]) for a in range(n)]
        for cp in mine:
            cp.start()
        first = []
        for a in range(n):
            first.append(copy(a, 0, me, sibling, src=ins[a]))
            first += [copy(a, 1 + j, me, (*chip, c), src=ins[a]) for j, chip in enumerate(chips)]
        for cp in first:
            cp.start()
        passed = []
        for j, chip in enumerate(chips):
            for a in range(n):
                copy(a, 1 + j, (*chip, c), me).wait_recv()
                fwd = copy(a, 4 + j, (*chip, c), sibling)
                fwd.start()
                passed.append(fwd)
        for a in range(n):
            copy(a, 0, sibling, me).wait_recv()
            for j, chip in enumerate(chips):
                copy(a, 4 + j, (*chip, 1 - c), me).wait_recv()
        for cp in first + passed:
            cp.wait_send()
        for cp in mine:
            cp.wait()

    out_shape = tuple(jax.ShapeDtypeStruct((N_DEV,) + arr.shape, arr.dtype) for arr in arrays)
    return pl.pallas_call(
        body, name=name, in_specs=[_HBM] * n, out_specs=tuple([_HBM] * n), out_shape=out_shape,
        scratch_shapes=[pltpu.SemaphoreType.DMA((n, N_DEV - 1)), pltpu.SemaphoreType.DMA((n, N_DEV - 1)),
                        pltpu.SemaphoreType.DMA((n,))],
    )(*arrays)


_SEM = pl.BlockSpec(memory_space=pltpu.SEMAPHORE)
_DATAFLOW = pltpu.SideEffectType.DATAFLOW_SIDE_EFFECTING


def _device_index():
    return (4 * lax.axis_index("x") + 2 * lax.axis_index("y") + lax.axis_index("c")).astype(jnp.int32).reshape(1)


def _place_own(name, pieces, *, stacked_src, after=None):
    n = len(pieces)
    n_in = n + (after is not None)

    def body(me_ref, *refs):
        for a in range(n):
            refs[n_in + a][...] = refs[a][...]

    def spec(shape):
        return pl.BlockSpec((None,) + tuple(shape), lambda i, me_ref: (me_ref[0],) + (0,) * len(shape))

    shapes = [p.shape[1:] if stacked_src else p.shape for p in pieces]
    if stacked_src:
        in_specs = [spec(s) for s in shapes]
    else:
        in_specs = [pl.BlockSpec(tuple(s), lambda i, me_ref, nd=len(s): (0,) * nd) for s in shapes]
    operands = list(pieces)
    if after is not None:
        in_specs.append(_HBM)
        operands.append(after)
    return pl.pallas_call(
        body, name=name,
        grid_spec=pltpu.PrefetchScalarGridSpec(num_scalar_prefetch=1, grid=(1,), in_specs=in_specs,
                                               out_specs=tuple(spec(s) for s in shapes)),
        out_shape=tuple(jax.ShapeDtypeStruct((N_DEV,) + tuple(s), p.dtype) for s, p in zip(shapes, pieces)),
        compiler_params=_cparams(("arbitrary",)),
    )(_device_index(), *operands)


def _peer_places():
    x, y, c = lax.axis_index("x"), lax.axis_index("y"), lax.axis_index("c")
    peers = []
    for k in range(N_DEV - 1):
        flip = k + 1
        px = 1 - x if flip & 4 else x
        py = 1 - y if flip & 2 else y
        pc = 1 - c if flip & 1 else c
        peers.append((px, py, pc, 4 * px + 2 * py + pc))
    return 4 * x + 2 * y + c, peers


def _direct_copy(srcs, lands, send_sems, recv_sems, a, k, me, peer, scatter):
    px, py, pc, pidx = peer
    return pltpu.make_async_remote_copy(
        src_ref=srcs[a].at[pidx] if scatter else srcs[a], dst_ref=lands[a].at[me],
        send_sem=send_sems.at[a * (N_DEV - 1) + k], recv_sem=recv_sems.at[a * (N_DEV - 1) + k],
        device_id=(px, py, pc), device_id_type=_MESH)


def _send_start(name, srcs, lands, *, scatter):
    n = len(srcs)

    def body(*refs):
        src_refs, land_refs = refs[:n], refs[n:2 * n]
        send_sems, recv_sems = refs[2 * n], refs[2 * n + 1]
        token = refs[-1]
        me, peers = _peer_places()
        for k, peer in enumerate(peers):
            for a in range(n):
                _direct_copy(src_refs, land_refs, send_sems, recv_sems, a, k, me, peer, scatter).start()
        token[...] = jnp.zeros_like(token)

    hbm_shapes = [pltpu.HBM(t.shape, t.dtype) for t in list(srcs) + list(lands)]
    outs = pl.pallas_call(
        body, name=name,
        out_shape=(pltpu.SemaphoreType.DMA((n * (N_DEV - 1),)), pltpu.SemaphoreType.DMA((n * (N_DEV - 1),)), *hbm_shapes,
                   jax.ShapeDtypeStruct((8, LANE), F32)),
        in_specs=[_HBM_ONLY] * (2 * n),
        out_specs=(_SEM, _SEM, *([_HBM_ONLY] * (2 * n)), pl.BlockSpec(memory_space=pltpu.VMEM)),
        input_output_aliases={i: 2 + i for i in range(2 * n)},
        compiler_params=pltpu.CompilerParams(has_side_effects=_DATAFLOW),
    )(*[pltpu.with_memory_space_constraint(t, pltpu.HBM) for t in list(srcs) + list(lands)])
    return outs[0], outs[1], outs[2:2 + n], outs[2 + n:2 + 2 * n], outs[-1]


def _send_wait(name, send_sems, recv_sems, srcs, lands, after, *, scatter):
    n = len(srcs)

    def body(*refs):
        src_refs, land_refs = refs[:n], refs[n:2 * n]
        send_sems, recv_sems = refs[2 * n], refs[2 * n + 1]
        me, peers = _peer_places()
        for k, peer in enumerate(peers):
            for a in range(n):
                cp = _direct_copy(src_refs, land_refs, send_sems, recv_sems, a, k, me, peer, scatter)
                cp.wait_send()
                cp.wait_recv()

    hbm_shapes = [pltpu.HBM(t.shape, t.dtype) for t in list(srcs) + list(lands)]
    outs = pl.pallas_call(
        body, name=name, out_shape=tuple(hbm_shapes),
        in_specs=[_HBM_ONLY] * (2 * n) + [_SEM, _SEM, _HBM],
        out_specs=tuple([_HBM_ONLY] * (2 * n)),
        input_output_aliases={i: i for i in range(2 * n)},
        compiler_params=pltpu.CompilerParams(has_side_effects=_DATAFLOW),
    )(*srcs, *lands, send_sems, recv_sems, after)
    return outs[n:]


def _unstack_cols(name, stacked):
    n, rows, cols = stacked.shape

    def body(i_ref, o_ref):
        o_ref[...] = i_ref[...]

    return pl.pallas_call(
        body, name=name, grid=(n,), in_specs=[pl.BlockSpec((None, rows, cols), lambda k: (k, 0, 0))],
        out_specs=pl.BlockSpec((rows, cols), lambda k: (0, k)),
        out_shape=jax.ShapeDtypeStruct((rows, n * cols), stacked.dtype),
        compiler_params=_cparams(("parallel",)),
    )(stacked)


def _restack_cols(name, mat):
    rows, width = mat.shape
    cols = width // N_DEV

    def body(i_ref, o_ref):
        o_ref[...] = i_ref[...]

    return pl.pallas_call(
        body, name=name, grid=(N_DEV,), in_specs=[pl.BlockSpec((rows, cols), lambda k: (0, k))],
        out_specs=pl.BlockSpec((None, rows, cols), lambda k: (k, 0, 0)),
        out_shape=jax.ShapeDtypeStruct((N_DEV, rows, cols), mat.dtype),
        compiler_params=_cparams(("parallel",)),
    )(mat)


def _remap_pieces(runs):
    plan = {}
    for du, dc, su, sc, ln in runs:
        while ln > 0:
            lane = dc % LANE
            take = min(ln, LANE - lane)
            plan.setdefault((du, dc // LANE), []).append((su, sc, take, lane))
            dc, sc, ln = dc + take, sc + take, ln - take
    return plan


def _remap(name, srcs, src_units, runs, *, out_units, out_cols, out_dtype, tr=256):
    rows = srcs[0].shape[-2]
    tr = min(tr, rows)
    plan = _remap_pieces(runs)
    n_src = len(srcs)
    stacked_out = out_units is not None
    n_tiles = out_cols // LANE

    def body(*refs):
        o_ref = refs[n_src]

        def src_tile(unit, t):
            ai, lead = src_units[unit]
            ref = refs[ai]
            sl = slice(t * LANE, (t + 1) * LANE)
            return (ref[:, sl] if lead is None else ref[lead, :, sl]).astype(F32)

        lane = lax.broadcasted_iota(jnp.int32, (tr, LANE), 1)
        for du in range(out_units if stacked_out else 1):
            for t in range(n_tiles):
                acc = jnp.zeros((tr, LANE), F32)
                for su, sc, ln, dl in plan.get((du if stacked_out else None, t), []):
                    st, so = sc // LANE, sc % LANE
                    first = src_tile(su, st)
                    if so == dl and so + ln <= LANE:
                        piece = first
                    else:
                        second = src_tile(su, st + 1) if so + ln > LANE else first
                        both = jnp.concatenate([first, second], axis=1)
                        piece = pltpu.roll(both, (dl - so) % (2 * LANE), axis=1)[:, 0:LANE]
                    acc = piece if (dl == 0 and ln == LANE) else jnp.where(
                        jnp.logical_and(lane >= dl, lane < dl + ln), piece, acc)
                if stacked_out:
                    o_ref[du, :, t * LANE:(t + 1) * LANE] = acc.astype(o_ref.dtype)
                else:
                    o_ref[:, t * LANE:(t + 1) * LANE] = acc.astype(o_ref.dtype)

    in_specs = []
    for arr in srcs:
        if arr.ndim == 2:
            in_specs.append(pl.BlockSpec((tr, arr.shape[1]), lambda i: (i, 0)))
        else:
            in_specs.append(pl.BlockSpec((arr.shape[0], tr, arr.shape[2]), lambda i: (0, i, 0)))
    if stacked_out:
        out_spec = pl.BlockSpec((out_units, tr, out_cols), lambda i: (0, i, 0))
        out_shape = jax.ShapeDtypeStruct((out_units, rows, out_cols), out_dtype)
    else:
        out_spec = pl.BlockSpec((tr, out_cols), lambda i: (i, 0))
        out_shape = jax.ShapeDtypeStruct((rows, out_cols), out_dtype)
    return pl.pallas_call(
        body, name=name, grid=(rows // tr,), in_specs=in_specs, out_specs=out_spec, out_shape=out_shape,
        compiler_params=_cparams(("parallel",)),
    )(*srcs)


def _proj_col(c):
    if c < PROJ_GATE0:
        return c
    if c < PROJ_GATE0 + FOX_HEADS:
        return PROJ_F0 + (c - PROJ_GATE0)
    return c - FOX_HEADS


def _win_runs():
    cuts = sorted(set([0, PROJ_GATE0, PROJ_GATE0 + FOX_HEADS, IN_WIDTH] + [SHARD_IN * k for k in range(N_DEV + 1)]))
    return [(lo // SHARD_IN, lo % SHARD_IN, _proj_col(lo), hi - lo) for lo, hi in zip(cuts[:-1], cuts[1:])]


def _assemble_win(name, stacked):
    runs = [(None, pc, k, sc, ln) for k, sc, pc, ln in _win_runs()]
    return _remap(name, [stacked], [(0, k) for k in range(N_DEV)], runs,
                  out_units=None, out_cols=PROJ_WIDTH, out_dtype=BF16)


def _disassemble_dwin(name, dw):
    runs = [(k, sc, 0, pc, ln) for k, sc, pc, ln in _win_runs()]
    return _remap(name, [dw], [(0, None)], runs, out_units=N_DEV, out_cols=SHARD_IN_PAD, out_dtype=BF16)


def _concat_cols(name, parts, *, tr=512):
    rows = parts[0].shape[0]
    tr = min(tr, rows)
    widths = [p.shape[1] for p in parts]
    total = sum(widths)

    def body(*refs):
        o_ref = refs[len(parts)]
        lo = 0
        for r, w in zip(refs[:len(parts)], widths):
            o_ref[:, lo:lo + w] = r[...].astype(o_ref.dtype)
            lo += w

    return pl.pallas_call(
        body, name=name, grid=(rows // tr,),
        in_specs=[pl.BlockSpec((tr, w), lambda i: (i, 0)) for w in widths],
        out_specs=pl.BlockSpec((tr, total), lambda i: (i, 0)),
        out_shape=jax.ShapeDtypeStruct((rows, total), BF16),
        compiler_params=_cparams(("parallel",)),
    )(*parts)


FFN_BLK = FFN_HIDDEN // 2


def _ffn_col(c):
    half, r = divmod(c, FFN_HIDDEN)
    blk, r = divmod(r, FFN_BLK)
    return blk * 2 * FFN_BLK + half * FFN_BLK + r


def _assemble_wffn(name, stacked):
    runs = [(None, _ffn_col(SHARD_FFN * k), k, 0, SHARD_FFN) for k in range(N_DEV)]
    return _remap(name, [stacked], [(0, k) for k in range(N_DEV)], runs,
                  out_units=None, out_cols=2 * FFN_HIDDEN, out_dtype=BF16)


def _disassemble_dwffn(name, dw):
    runs = [(k, 0, 0, _ffn_col(SHARD_FFN * k), SHARD_FFN) for k in range(N_DEV)]
    return _remap(name, [dw], [(0, None)], runs, out_units=N_DEV, out_cols=SHARD_FFN_PAD, out_dtype=BF16)


def _ffn_in_swiglu(name, xn, w, *, tm=512):
    rows, k = xn.shape
    tm = min(tm, rows)
    nblk = FFN_HIDDEN // FFN_BLK

    def body(x_ref, w_ref, f_ref, g_ref):
        f = _bdot(x_ref[...], w_ref[...], _DIMS["nn"])
        f_ref[...] = f.astype(f_ref.dtype)
        fa = f[:, 0:FFN_BLK]
        g_ref[...] = (fa * _sigmoid(fa) * f[:, FFN_BLK:2 * FFN_BLK]).astype(g_ref.dtype)

    return pl.pallas_call(
        body, name=name, grid=(nblk, rows // tm),
        in_specs=[pl.BlockSpec((tm, k), lambda j, i: (i, 0)), pl.BlockSpec((k, 2 * FFN_BLK), lambda j, i: (0, j))],
        out_specs=(pl.BlockSpec((tm, 2 * FFN_BLK), lambda j, i: (i, j)), pl.BlockSpec((tm, FFN_BLK), lambda j, i: (i, j))),
        out_shape=(jax.ShapeDtypeStruct((rows, 2 * FFN_HIDDEN), BF16), jax.ShapeDtypeStruct((rows, FFN_HIDDEN), BF16)),
        compiler_params=_cparams(("parallel", "arbitrary")),
    )(xn, w)


def _d_ffn_out_swiglu(name, dh, w_out, f, *, tm=512):
    rows, d = dh.shape
    tm = min(tm, rows)
    nblk = FFN_HIDDEN // FFN_BLK

    def body(dh_ref, w_ref, f_ref, df_ref):
        dg = _bdot(dh_ref[...], w_ref[...], _DIMS["nt"])
        fa = f_ref[:, 0:FFN_BLK].astype(F32)
        fb = f_ref[:, FFN_BLK:2 * FFN_BLK].astype(F32)
        s = _sigmoid(fa)
        df_ref[:, 0:FFN_BLK] = (dg * fb * s * (1.0 + fa * (1.0 - s))).astype(df_ref.dtype)
        df_ref[:, FFN_BLK:2 * FFN_BLK] = (dg * fa * s).astype(df_ref.dtype)

    wide = pl.BlockSpec((tm, 2 * FFN_BLK), lambda j, i: (i, j))
    return pl.pallas_call(
        body, name=name, grid=(nblk, rows // tm),
        in_specs=[pl.BlockSpec((tm, d), lambda j, i: (i, 0)), pl.BlockSpec((FFN_BLK, d), lambda j, i: (j, 0)), wide],
        out_specs=wide, out_shape=jax.ShapeDtypeStruct((rows, 2 * FFN_HIDDEN), BF16),
        compiler_params=_cparams(("parallel", "arbitrary")),
    )(dh, w_out, f)


def _adamw(name, parts, w, m, v, *, tr=128):
    rows, cols = w.shape
    n_parts = parts.shape[0]
    tr = min(tr, rows)
    assert rows % tr == 0, (name, rows, tr)
    c1 = 1.0 - ADAM_B1 ** ADAM_STEP
    c2 = 1.0 - ADAM_B2 ** ADAM_STEP

    def body(p_ref, w_ref, m_ref, v_ref, g_ref, d_ref, nm_ref, nv_ref):
        g = p_ref[0].astype(F32)
        for s in range(1, n_parts):
            g = g + p_ref[s].astype(F32)
        m_new = ADAM_B1 * m_ref[...] + (1.0 - ADAM_B1) * g
        v_new = ADAM_B2 * v_ref[...] + (1.0 - ADAM_B2) * (g * g)
        upd = (m_new / c1) / (jnp.sqrt(v_new / c2) + ADAM_EPS) + ADAM_WD * w_ref[...]
        g_ref[...] = g
        d_ref[...] = -ADAM_LR * upd
        nm_ref[...] = m_new
        nv_ref[...] = v_new

    row = pl.BlockSpec((tr, cols), lambda i: (i, 0))
    out = jax.ShapeDtypeStruct((rows, cols), F32)
    return pl.pallas_call(
        body, name=name, grid=(rows // tr,),
        in_specs=[pl.BlockSpec((n_parts, tr, cols), lambda i: (0, i, 0)), row, row, row],
        out_specs=(row, row, row, row), out_shape=(out, out, out, out),
        compiler_params=_cparams(("parallel",)),
    )(parts, w, m, v)


_WEIGHTS = ("norm_mix", "w_in", "b_forget", "lam_re", "lam_im", "log_dt", "b_re", "b_im", "c_re", "c_im",
            "d_skip", "w_glu", "w_fox_o", "w_mix_out", "norm_mem_q", "norm_mem_kv", "w_mem_q", "w_mem_kv",
            "w_mem_o", "norm_ffn", "w_ffn_in", "w_ffn_out", "norm_final")
_SHARDED = ("w_in", "w_glu", "w_fox_o", "w_mix_out", "w_mem_q", "w_mem_kv", "w_mem_o", "w_ffn_in", "w_ffn_out")
_SMALL = tuple(n for n in _WEIGHTS if n not in _SHARDED)
_PACK_COLS = 1024


def _pack(arrays):
    flat = jnp.concatenate([a.reshape(-1).astype(F32) for a in arrays])
    rows = -(-flat.shape[0] // _PACK_COLS)
    return jnp.pad(flat, (0, rows * _PACK_COLS - flat.shape[0])).reshape(rows, _PACK_COLS)


def _unpack(buf, like):
    flat = buf.reshape(-1)
    out, pos = [], 0
    for a in like:
        out.append(flat[pos:pos + a.size].reshape(a.shape))
        pos += a.size
    return out


def _mm(name, a, b, mode, m, n, k, out_dtype, tm=1024, tn=512, tk=1024, **kw):
    return _matmul(name, a, b, mode, m, n, k, out_dtype=out_dtype, tm=tm, tn=tn, tk=tk, **kw)


def kernel(x, mem, norm_mix, w_in, b_forget, lam_re, lam_im, log_dt, b_re, b_im, c_re, c_im, d_skip, w_glu, w_fox_o, w_mix_out, norm_mem_q, norm_mem_kv, w_mem_q, w_mem_kv, w_mem_o, norm_ffn, w_ffn_in, w_ffn_out, norm_final, loss_target, m_norm_mix, m_w_in, m_b_forget, m_lam_re, m_lam_im, m_log_dt, m_b_re, m_b_im, m_c_re, m_c_im, m_d_skip, m_w_glu, m_w_fox_o, m_w_mix_out, m_norm_mem_q, m_norm_mem_kv, m_w_mem_q, m_w_mem_kv, m_w_mem_o, m_norm_ffn, m_w_ffn_in, m_w_ffn_out, m_norm_final, v_norm_mix, v_w_in, v_b_forget, v_lam_re, v_lam_im, v_log_dt, v_b_re, v_b_im, v_c_re, v_c_im, v_d_skip, v_w_glu, v_w_fox_o, v_w_mix_out, v_norm_mem_q, v_norm_mem_kv, v_w_mem_q, v_w_mem_kv, v_w_mem_o, v_norm_ffn, v_w_ffn_in, v_w_ffn_out, v_norm_final):
    given = dict(locals())
    weights = {n: given[n] for n in _WEIGHTS}
    mom_m = {n: given["m_" + n] for n in _WEIGHTS}
    mom_v = {n: given["v_" + n] for n in _WEIGHTS}
    seq = x.shape[1]
    nc = seq // SSM_CHUNK
    d = D_MODEL
    xs, mems, tgt = x[0], mem[0], loss_target[0]

    def padcols(a, width):
        return jnp.pad(a, ((0, 0), (0, width - a.shape[1])))

    shards = [padcols(w_in[0].astype(BF16), SHARD_IN_PAD), w_glu[0].astype(BF16), w_fox_o[0].astype(BF16),
              w_mix_out[0].astype(BF16), w_mem_q[0].astype(BF16), w_mem_kv[0].astype(BF16),
              w_mem_o[0].astype(BF16), padcols(w_ffn_in[0].astype(BF16), SHARD_FFN_PAD), w_ffn_out[0].astype(BF16)]
    win = _assemble_win("assemble_w_in", _gather_all("gather_w_in", shards[:1])[0])
    rest = shards[1:]
    gsend, grecv, rest_thru, lands, gtoken = _send_start(
        "gather_rest_start", rest, _place_own("place_weight_shards", rest, stacked_src=False, after=win), scatter=False)

    u = _rms_fwd("rms_mix", xs, norm_mix, after=gtoken)
    ussm = _mm("proj_ssm", u, win, "nn", seq, SSM_WIDTH, d, F32)
    qkv = _mm("proj_qkv", u, win, "nn", seq, 3 * FOX_WIDTH, d, BF16, tn=512, b_off=(0, SSM_WIDTH))
    gates = _mm("proj_gates", u, win, "nn", seq, 2 * d, d, BF16, tn=1024, b_off=(0, PROJ_GATE0))
    fproj = _mm("proj_forget", u, win, "nn", seq, LANE, d, F32, tn=LANE, b_off=(0, PROJ_F0))

    ssm_params = (lam_re[0], lam_im[0], log_dt[0], b_re[0], b_im[0], c_re[0], c_im[0])
    (m_c, bw_c, cm_c, a8, aseg), mats_vjp = jax.vjp(lambda *p: _ssm_mats(*p, nc), *ssm_params)
    m_b = _bd_expand("ssm_expand_m", _BD_M, m_c)
    bw_b = _bd_expand("ssm_expand_bw", _BD_BW, bw_c)
    cm_b = _bd_expand("ssm_expand_cm", _BD_CM, cm_c)
    u8 = ussm.reshape(nc, SSM_CHUNK * SSM_WIDTH)
    d8 = jnp.tile(d_skip, (1, SSM_CHUNK))
    w4 = _ssm_w("ssm_w", u8, bw_b)
    sp4 = _ssm_scan("ssm_scan", w4, a8, aseg, reverse=False)
    y8 = _ssm_y("ssm_y", u8, sp4, m_b, cm_b)
    act = _ssm_post_fwd("ssm_act", y8, u8, d8).reshape(seq, SSM_WIDTH)

    bcol = jnp.pad(b_forget[0], (0, LANE - FOX_HEADS)).reshape(LANE, 1)
    cum_t = _fox_cum("fox_cum", fproj, bcol).reshape(FOX_HEADS // 2, 2, seq)
    att, lse = _fox_fwd("fox_fwd", qkv, cum_t)

    gathered = _send_wait("gather_rest_wait", gsend, grecv, rest_thru, lands, att, scatter=False)
    wglu = _unstack_cols("unstack_w_glu", gathered[0])
    wfoxo = _unstack_cols("unstack_w_fox_o", gathered[1])
    wmix = gathered[2].reshape(d, d)
    wmq = gathered[3].reshape(d, MEM_WIDTH)
    wmkv = gathered[4].reshape(d, 2 * MEM_WIDTH)
    wmo = _unstack_cols("unstack_w_mem_o", gathered[5])
    wffn_in = _assemble_wffn("assemble_w_ffn_in", gathered[6])
    wffn_out = gathered[7].reshape(FFN_HIDDEN, d)

    glu = _mm("glu", act, wglu, "nn", seq, 2 * d, SSM_WIDTH, BF16, tn=1024)
    out_b = _mm("fox_out", att, wfoxo, "nn", seq, d, FOX_WIDTH, BF16, tn=1024)

    mixin = _mix_fwd("mix", glu, gates, out_b)
    h1 = _mm("mix_out", mixin, wmix, "nn", seq, d, d, F32, tn=1024, add=xs)

    n1 = _rms_fwd("rms_mem_q", h1, norm_mem_q)
    q2 = _mm("mem_q", n1, wmq, "nn", seq, MEM_WIDTH, d, BF16)
    mn = _rms_fwd("rms_mem_kv", mems, norm_mem_kv)
    mlen = mems.shape[0]
    kv = _mm("mem_kv", mn, wmkv, "nn", mlen, 2 * MEM_WIDTH, d, BF16)
    o2 = _mem_fwd("mem_attn", q2, kv)
    h2 = _mm("mem_out", o2, wmo, "nn", seq, d, MEM_WIDTH, F32, tn=1024, add=h1)

    n2 = _rms_fwd("rms_ffn", h2, norm_ffn)
    f, g_act = _ffn_in_swiglu("ffn_in_swiglu", n2, wffn_in)
    loss_part, dh3, dg_final = _matmul_final_loss("ffn_out_final_loss", g_act, wffn_out, h2, tgt,
                                                  norm_final.reshape(1, d))

    df = _d_ffn_out_swiglu("d_ffn_out_swiglu", dh3, wffn_out, f)
    dwffn_out = _mm("d_ffn_out_w", g_act, dh3, "tn", FFN_HIDDEN, d, seq, BF16, tm=1408, tn=1024)
    dh2, dg_ffn = _matmul_rms_bwd("d_ffn_in_x_rms", df, wffn_in, 2 * FFN_HIDDEN, h2, norm_ffn, dh3, tm=1024, tk=1408)
    dwffn_in = _mm("d_ffn_in_w", n2, df, "tn", d, 2 * FFN_HIDDEN, seq, BF16, tn=1408)

    do2 = _mm("d_mem_out_x", dh2, wmo, "nt", seq, MEM_WIDTH, d, F32)
    dwmo = _restack_cols("restack_d_w_mem_o", _mm("d_mem_out_w", o2, dh2, "tn", MEM_WIDTH, d, seq, BF16, tn=1024))
    dq2, dkv = _mem_bwd("d_mem_attn", q2, kv, do2)
    dwmq = _mm("d_mem_q_w", n1, dq2, "tn", d, MEM_WIDTH, seq, BF16)
    dwmkv = _mm("d_mem_kv_w", mn, dkv, "tn", d, 2 * MEM_WIDTH, mlen, BF16, tn=1024)
    dmn = _mm("d_mem_kv_x", dkv, wmkv, "nt", mlen, d, 2 * MEM_WIDTH, F32)
    dg_memkv = _rms_gain_grad("d_rms_mem_kv", dmn, mems)

    early = [dwmq.reshape(N_DEV, d // N_DEV, MEM_WIDTH), dwmkv.reshape(N_DEV, d // N_DEV, 2 * MEM_WIDTH), dwmo,
             _disassemble_dwffn("split_d_w_ffn_in", dwffn_in), dwffn_out.reshape(N_DEV, FFN_HIDDEN // N_DEV, d)]
    ssend, srecv, early_thru, early_lands, stoken = _send_start(
        "scatter_early_start", early, _place_own("place_early_grads", early, stacked_src=True), scatter=True)
    dh1, dg_memq = _matmul_rms_bwd("d_mem_q_x_rms", dq2, wmq, MEM_WIDTH, h1, norm_mem_q, dh2, tm=1024, after=stoken)

    dmixin = _mm("d_mix_out_x", dh1, wmix, "nt", seq, d, d, F32, tn=1024)
    dwmix = _mm("d_mix_out_w", mixin, dh1, "tn", d, d, seq, BF16, tn=1024)
    dglu, dgates, dout_b = _mix_bwd("d_mix", dmixin, glu, gates, out_b)
    datt = _mm("d_fox_out_x", dout_b, wfoxo, "nt", seq, FOX_WIDTH, d, F32)
    dwfoxo = _restack_cols("restack_d_w_fox_o", _mm("d_fox_out_w", att, dout_b, "tn", FOX_WIDTH, d, seq, BF16, tn=1024))
    dact = _mm("d_glu_x", dglu, wglu, "nt", seq, SSM_WIDTH, 2 * d, F32, tk=2 * d)
    dwglu = _restack_cols("restack_d_w_glu", _mm("d_glu_w", act, dglu, "tn", SSM_WIDTH, 2 * d, seq, BF16, tn=2 * d))

    mid = [dwglu, dwfoxo, dwmix.reshape(N_DEV, d // N_DEV, d)]
    msend, mrecv, mid_thru, mid_lands, mtoken = _send_start(
        "scatter_mid_start", mid, _place_own("place_mid_grads", mid, stacked_src=True), scatter=True)

    dz8, dg_dskip = _ssm_post_bwd("d_ssm_act", dact.reshape(nc, SSM_CHUNK * SSM_WIDTH), y8, u8, d8, after=mtoken)
    ds4, dcm = _ssm_ds("d_ssm_y_state", dz8, sp4, cm_b)
    g4, da8 = _ssm_scan("d_ssm_scan", ds4, a8, aseg, reverse=True, sprev4=sp4)
    dx8, dm, dbw = _ssm_dx("d_ssm_x", dz8, g4, u8, m_b, bw_b, d8)
    dussm = dx8.reshape(seq, SSM_WIDTH)
    g_ssm = mats_vjp((_bd_reduce("ssm_reduce_dm", _BD_M, dm), _bd_reduce("ssm_reduce_dbw", _BD_BW, dbw),
                      _bd_reduce("ssm_reduce_dcm", _BD_CM, dcm), da8, jnp.zeros_like(aseg)))

    dq, dk, dv, dcs = _fox_bwd("d_fox", qkv, cum_t, att, datt, lse)
    dfproj, dbf = _fox_cum_bwd("d_fox_cum", dcs, fproj, bcol)
    dg_bforget = dbf[0:FOX_HEADS, 0].reshape(1, FOX_HEADS)

    dproj = _concat_cols("d_proj_concat", (dussm, dq, dk, dv, dgates, dfproj))
    dwin = _mm("d_proj_w", u, dproj, "tn", d, PROJ_WIDTH, seq, BF16, tn=1408)
    late = [_disassemble_dwin("split_d_w_in", dwin)]
    lsend, lrecv, late_thru, late_lands, ltoken = _send_start(
        "scatter_late_start", late, _place_own("place_late_grads", late, stacked_src=True), scatter=True)
    dx, dg_mix = _matmul_rms_bwd("d_proj_x_rms", dproj, win, PROJ_WIDTH, xs, norm_mix, dh1, tm=1024, tk=1408,
                                 after=ltoken)

    early_parts = _send_wait("scatter_early_wait", ssend, srecv, early_thru, early_lands, dx, scatter=True)
    mid_parts = _send_wait("scatter_mid_wait", msend, mrecv, mid_thru, mid_lands, dx, scatter=True)
    received = dict(zip(("w_glu", "w_fox_o", "w_mix_out"), mid_parts))
    received.update(zip(("w_mem_q", "w_mem_kv", "w_mem_o", "w_ffn_in", "w_ffn_out"), early_parts))

    small_grads = dict(zip(
        _SMALL, (dg_mix, dg_bforget, g_ssm[0][None], g_ssm[1][None], g_ssm[2][None], g_ssm[3][None], g_ssm[4][None],
                 g_ssm[5][None], g_ssm[6][None], dg_dskip, dg_memq, dg_memkv, dg_ffn, dg_final.reshape(d))))
    small_like = [weights[n] for n in _SMALL]
    small_all = _gather_all("gather_small_grads", [_pack([small_grads[n] for n in _SMALL])])[0]
    pk = [_pack([src[n] for n in _SMALL]) for src in (weights, mom_m, mom_v)]
    small_out = _adamw("adamw_small", small_all, pk[0], pk[1], pk[2], tr=small_all.shape[1])
    results = [dict(zip(_SMALL, _unpack(buf, small_like))) for buf in small_out]
    tiles = {"w_in": 128, "w_glu": 128, "w_fox_o": 128, "w_mix_out": 128, "w_mem_q": 128, "w_mem_kv": 128,
             "w_mem_o": 128, "w_ffn_in": 128, "w_ffn_out": 176}
    pads = {"w_in": SHARD_IN_PAD, "w_ffn_in": SHARD_FFN_PAD}
    outs = small_out
    for name in _SHARDED[1:] + _SHARDED[:1]:
        if name == "w_in":
            received[name] = _send_wait("scatter_late_wait", lsend, lrecv, late_thru, late_lands, outs[0],
                                        scatter=True)[0]
        parts = received[name]
        w2, m2, v2 = weights[name][0], mom_m[name][0], mom_v[name][0]
        cols = w2.shape[1]
        if name in pads:
            w2, m2, v2 = (padcols(t, pads[name]) for t in (w2, m2, v2))
        outs = _adamw("adamw_" + name, parts, w2, m2, v2, tr=tiles[name])
        for res, o in zip(results, outs):
            res[name] = o[:, :cols][None]

    loss = lax.psum(loss_part[0, 0], ("x", "y", "c"))
    out = [loss, dx[None]]
    for res in results:
        out.extend(res[n] for n in _WEIGHTS)
    return tuple(out)
```

```python
import math

import jax
import jax.numpy as jnp
import numpy as np
from jax import lax
from jax.experimental import pallas as pl
from jax.experimental.pallas import tpu as pltpu

F32 = jnp.float32
BF16 = jnp.bfloat16

N_DEV = 8
LANE = 128
VMEM_LIMIT = 56 * 1024 * 1024

D_MODEL = 1024
SSM_GROUP = 16
SSM_GROUPS = 32
SSM_WIDTH = 512
SSM_STATE = 64
SSM_CHUNK = 8
FOX_HEADS = 8
FOX_HEAD_DIM = 64
FOX_WIDTH = 512
MEM_HEADS = 4
MEM_HEAD_DIM = 128
MEM_WIDTH = 512
FFN_HIDDEN = 2816
RMS_EPS = 1e-6
IN_WIDTH = 4104
SHARD_IN = IN_WIDTH // N_DEV
SHARD_IN_PAD = 640
SHARD_FFN = 2 * FFN_HIDDEN // N_DEV
SHARD_FFN_PAD = 768
PROJ_GATE0 = 2048
PROJ_F0 = 4096
PROJ_WIDTH = 4224

ADAM_LR = 0.001
ADAM_B1 = 0.9
ADAM_B2 = 0.999
ADAM_EPS = 1e-08
ADAM_WD = 0.01
ADAM_STEP = 10


def _cparams(sem=None):
    return pltpu.CompilerParams(dimension_semantics=sem, vmem_limit_bytes=VMEM_LIMIT)


def _sigmoid(x):
    return 1.0 / (1.0 + jnp.exp(-x))


def _bdot(a, b, dims):
    return lax.dot_general(a.astype(BF16), b.astype(BF16), ((dims[0], dims[1]), ((), ())),
                           preferred_element_type=F32)


_DIMS = {"nn": ((1,), (0,)), "nt": ((1,), (1,)), "tn": ((0,), (0,))}


def _matmul(name, a, b, mode, m, n, k, *, out_dtype, tm, tn, tk, a_off=(0, 0), b_off=(0, 0), add=None):
    tm, tn, tk = min(tm, m), min(tn, n), min(tk, k)
    assert m % tm == 0 and n % tn == 0 and k % tk == 0, (name, m, n, k, tm, tn, tk)
    nk = k // tk
    grid = (m // tm, n // tn, nk)

    def blk(off, t):
        assert off % t == 0, (name, off, t)
        return off // t

    if mode in ("nn", "nt"):
        ar, ac = blk(a_off[0], tm), blk(a_off[1], tk)
        a_spec = pl.BlockSpec((tm, tk), lambda i, j, kk: (i + ar, kk + ac))
    else:
        ar, ac = blk(a_off[0], tk), blk(a_off[1], tm)
        a_spec = pl.BlockSpec((tk, tm), lambda i, j, kk: (kk + ar, i + ac))

    if mode in ("nn", "tn"):
        br, bc = blk(b_off[0], tk), blk(b_off[1], tn)
        b_spec = pl.BlockSpec((tk, tn), lambda i, j, kk: (kk + br, j + bc))
    else:
        br, bc = blk(b_off[0], tn), blk(b_off[1], tk)
        b_spec = pl.BlockSpec((tn, tk), lambda i, j, kk: (j + br, kk + bc))
    o_spec = pl.BlockSpec((tm, tn), lambda i, j, kk: (i, j))
    out_shape = jax.ShapeDtypeStruct((m, n), out_dtype)

    in_specs = [a_spec, b_spec]
    operands = [a, b]
    if add is not None:
        in_specs.append(pl.BlockSpec((tm, tn), lambda i, j, kk: (i, j)))
        operands.append(add)
    dims = _DIMS[mode]
    has_add = add is not None

    def body(*refs):
        a_ref, b_ref = refs[0], refs[1]
        add_ref = refs[2] if has_add else None
        o_ref = refs[3] if has_add else refs[2]
        acc_ref = refs[-1] if nk > 1 else None
        prod = _bdot(a_ref[...], b_ref[...], dims)

        def finish(total):
            if has_add:
                total = total + add_ref[...].astype(F32)
            o_ref[...] = total.astype(o_ref.dtype)

        if nk == 1:
            finish(prod)
        else:
            kk = pl.program_id(2)

            @pl.when(kk == 0)
            def _():
                acc_ref[...] = prod

            @pl.when(jnp.logical_and(kk > 0, kk < nk - 1))
            def _():
                acc_ref[...] += prod

            @pl.when(kk == nk - 1)
            def _():
                finish(acc_ref[...] + prod)

    scratch = [pltpu.VMEM((tm, tn), F32)] if nk > 1 else []
    return pl.pallas_call(
        body, name=name, grid=grid, in_specs=in_specs, out_specs=o_spec, out_shape=out_shape,
        scratch_shapes=scratch,
        compiler_params=_cparams(("parallel", "parallel", "arbitrary")),
    )(*operands)


def _rms_fwd(name, x, gain, *, tr=512, after=None):
    r, d = x.shape
    tr = min(tr, r)

    def body(x_ref, g_ref, *rest):
        o_ref = rest[-1]
        xv = x_ref[...]
        rstd = lax.rsqrt(jnp.mean(xv * xv, axis=-1, keepdims=True) + RMS_EPS)
        o_ref[...] = (xv * rstd * g_ref[...]).astype(o_ref.dtype)

    in_specs = [pl.BlockSpec((tr, d), lambda i: (i, 0)), pl.BlockSpec((1, d), lambda i: (0, 0))]
    ops = [x, gain]
    if after is not None:
        in_specs.append(pl.BlockSpec(after.shape, lambda i: (0, 0)))
        ops.append(after)
    return pl.pallas_call(
        body, name=name, grid=(r // tr,), in_specs=in_specs,
        out_specs=pl.BlockSpec((tr, d), lambda i: (i, 0)),
        out_shape=jax.ShapeDtypeStruct((r, d), BF16),
        compiler_params=_cparams(("parallel",)),
    )(*ops)


def _rms_gain_grad(name, dy, x, *, tr=512):
    r, d = x.shape
    tr = min(tr, r)
    n = r // tr

    def body(dy_ref, x_ref, dg_ref, acc_ref):
        i = pl.program_id(0)
        xv = x_ref[...]
        xh = xv * lax.rsqrt(jnp.mean(xv * xv, axis=-1, keepdims=True) + RMS_EPS)
        part = (dy_ref[...].astype(F32) * xh).reshape(tr // 8, 8, d).sum(axis=0)

        @pl.when(i == 0)
        def _():
            acc_ref[...] = part

        @pl.when(i > 0)
        def _():
            acc_ref[...] += part

        @pl.when(i == n - 1)
        def _():
            dg_ref[...] = jnp.sum(acc_ref[...], axis=0, keepdims=True)

    row = pl.BlockSpec((tr, d), lambda i: (i, 0))
    return pl.pallas_call(
        body, name=name, grid=(n,), in_specs=[row, row],
        out_specs=pl.BlockSpec((1, d), lambda i: (0, 0)),
        out_shape=jax.ShapeDtypeStruct((1, d), F32),
        scratch_shapes=[pltpu.VMEM((8, d), F32)],
        compiler_params=_cparams(("arbitrary",)),
    )(dy, x)


def _matmul_rms_bwd(name, a, b, k, x, gain, res, *, tm=512, tk=1024, after=None):
    m, d = x.shape
    tm, tk = min(tm, m), min(tk, k)
    assert m % tm == 0 and k % tk == 0, (name, m, k, tm, tk)
    ni, nk = m // tm, k // tk

    def body(a_ref, b_ref, x_ref, g_ref, res_ref, *rest):
        dx_ref, dg_ref, acc_ref, accg_ref = rest[-4:]
        i, kk = pl.program_id(0), pl.program_id(1)
        prod = _bdot(a_ref[...], b_ref[...], _DIMS["nt"])

        @pl.when(kk == 0)
        def _():
            acc_ref[...] = prod

        @pl.when(kk > 0)
        def _():
            acc_ref[...] += prod

        @pl.when(kk == nk - 1)
        def _():
            dyv = acc_ref[...]
            xv = x_ref[...]
            rstd = lax.rsqrt(jnp.mean(xv * xv, axis=-1, keepdims=True) + RMS_EPS)
            xh = xv * rstd
            dxh = dyv * g_ref[...]
            dx_ref[...] = rstd * (dxh - xh * jnp.mean(dxh * xh, axis=-1, keepdims=True)) + res_ref[...]
            part = (dyv * xh).reshape(tm // 8, 8, d).sum(axis=0)

            @pl.when(i == 0)
            def _():
                accg_ref[...] = part

            @pl.when(i > 0)
            def _():
                accg_ref[...] += part

            @pl.when(i == ni - 1)
            def _():
                dg_ref[...] = jnp.sum(accg_ref[...], axis=0, keepdims=True)

    row = pl.BlockSpec((tm, d), lambda i, kk: (i, 0))
    one = pl.BlockSpec((1, d), lambda i, kk: (0, 0))
    in_specs = [pl.BlockSpec((tm, tk), lambda i, kk: (i, kk)), pl.BlockSpec((d, tk), lambda i, kk: (0, kk)), row, one, row]
    ops = [a, b, x, gain, res]
    if after is not None:
        in_specs.append(pl.BlockSpec(after.shape, lambda i, kk: (0, 0)))
        ops.append(after)
    return pl.pallas_call(
        body, name=name, grid=(ni, nk), in_specs=in_specs, out_specs=(row, one),
        out_shape=(jax.ShapeDtypeStruct((m, d), F32), jax.ShapeDtypeStruct((1, d), F32)),
        scratch_shapes=[pltpu.VMEM((tm, d), F32), pltpu.VMEM((8, d), F32)],
        compiler_params=_cparams(("arbitrary", "arbitrary")),
    )(*ops)


def _matmul_final_loss(name, a, b, res, target, gain, *, tr=512):
    r, d = res.shape
    k = a.shape[1]
    tr = min(tr, r)
    n = r // tr

    def body(a_ref, b_ref, res_ref, t_ref, g_ref, loss_ref, dh_ref, dg_ref, accl_ref, accg_ref):
        i = pl.program_id(0)
        xv = _bdot(a_ref[...], b_ref[...], _DIMS["nn"]) + res_ref[...]
        rstd = lax.rsqrt(jnp.mean(xv * xv, axis=-1, keepdims=True) + RMS_EPS)
        xh = xv * rstd
        e = xh * g_ref[...] - t_ref[...]
        dyv = e * (1.0 / d)
        dxh = dyv * g_ref[...]
        dh_ref[...] = rstd * (dxh - xh * jnp.mean(dxh * xh, axis=-1, keepdims=True))
        lpart = (e * e).reshape(tr // 8, 8, d).sum(axis=0)
        gpart = (dyv * xh).reshape(tr // 8, 8, d).sum(axis=0)

        @pl.when(i == 0)
        def _():
            accl_ref[...] = lpart
            accg_ref[...] = gpart

        @pl.when(i > 0)
        def _():
            accl_ref[...] += lpart
            accg_ref[...] += gpart

        @pl.when(i == n - 1)
        def _():
            tot = jnp.sum(jnp.sum(accl_ref[...], axis=0, keepdims=True), axis=1, keepdims=True)
            loss_ref[...] = jnp.broadcast_to(tot * (0.5 / d), (1, LANE))
            dg_ref[...] = jnp.sum(accg_ref[...], axis=0, keepdims=True)

    row = pl.BlockSpec((tr, d), lambda i: (i, 0))
    one = pl.BlockSpec((1, d), lambda i: (0, 0))
    return pl.pallas_call(
        body, name=name, grid=(n,),
        in_specs=[pl.BlockSpec((tr, k), lambda i: (i, 0)), pl.BlockSpec((k, d), lambda i: (0, 0)), row, row, one],
        out_specs=(pl.BlockSpec((1, LANE), lambda i: (0, 0)), row, one),
        out_shape=(jax.ShapeDtypeStruct((1, LANE), F32), jax.ShapeDtypeStruct((r, d), F32),
                   jax.ShapeDtypeStruct((1, d), F32)),
        scratch_shapes=[pltpu.VMEM((8, d), F32), pltpu.VMEM((8, d), F32)],
        compiler_params=_cparams(("arbitrary",)),
    )(a, b, res, target, gain)


_GELU_C = math.sqrt(2.0 / math.pi)


def _gelu_parts(z):
    inner = _GELU_C * (z + 0.044715 * z * z * z)
    t = jnp.tanh(inner)
    val = 0.5 * z * (1.0 + t)
    dinner = _GELU_C * (1.0 + 3.0 * 0.044715 * z * z)
    grad = 0.5 * (1.0 + t) + 0.5 * z * (1.0 - t * t) * dinner
    return val, grad


def _ssm_post_fwd(name, y8, u8, d8, *, tr=256):
    r, c = y8.shape
    tr = min(tr, r)

    def body(y_ref, u_ref, d_ref, o_ref):
        z = y_ref[...] + d_ref[...] * u_ref[...]
        o_ref[...] = _gelu_parts(z)[0].astype(o_ref.dtype)

    row = pl.BlockSpec((tr, c), lambda i: (i, 0))
    return pl.pallas_call(
        body, name=name, grid=(r // tr,), in_specs=[row, row, pl.BlockSpec((1, c), lambda i: (0, 0))],
        out_specs=row, out_shape=jax.ShapeDtypeStruct((r, c), BF16),
        compiler_params=_cparams(("parallel",)),
    )(y8, u8, d8)


def _ssm_post_bwd(name, dact8, y8, u8, d8, *, tr=256, after=None):
    r, c = y8.shape
    tr = min(tr, r)
    n = r // tr

    def body(*refs):
        da_ref, y_ref, u_ref, d_ref = refs[:4]
        dz_ref, dd_ref, acc_ref = refs[-3:]
        i = pl.program_id(0)
        uv = u_ref[...]
        z = y_ref[...] + d_ref[...] * uv
        dz = da_ref[...].astype(F32) * _gelu_parts(z)[1]
        dz_ref[...] = dz
        part = (dz * uv).reshape(tr // 8, 8, c).sum(axis=0)

        @pl.when(i == 0)
        def _():
            acc_ref[...] = part

        @pl.when(i > 0)
        def _():
            acc_ref[...] += part

        @pl.when(i == n - 1)
        def _():
            tot = jnp.sum(acc_ref[...], axis=0, keepdims=True)
            out = tot[:, 0:SSM_WIDTH]
            for j in range(1, c // SSM_WIDTH):
                out = out + tot[:, j * SSM_WIDTH:(j + 1) * SSM_WIDTH]
            dd_ref[...] = out

    row = pl.BlockSpec((tr, c), lambda i: (i, 0))
    in_specs = [row, row, row, pl.BlockSpec((1, c), lambda i: (0, 0))]
    ops = [dact8, y8, u8, d8]
    if after is not None:
        in_specs.append(pl.BlockSpec(memory_space=pl.ANY))
        ops.append(after)
    return pl.pallas_call(
        body, name=name, grid=(n,), in_specs=in_specs,
        out_specs=(row, pl.BlockSpec((1, SSM_WIDTH), lambda i: (0, 0))),
        out_shape=(jax.ShapeDtypeStruct((r, c), F32), jax.ShapeDtypeStruct((1, SSM_WIDTH), F32)),
        scratch_shapes=[pltpu.VMEM((8, c), F32)],
        compiler_params=_cparams(("arbitrary",)),
    )(*ops)


def _mix_fwd(name, glu, gates, out_b, *, tr=256):
    r = glu.shape[0]
    d = D_MODEL
    tr = min(tr, r)

    def body(glu_ref, gate_ref, ob_ref, o_ref):
        out_a = glu_ref[:, 0:d].astype(F32) * _sigmoid(glu_ref[:, d:2 * d].astype(F32))
        mix = (_sigmoid(gate_ref[:, 0:d].astype(F32)) * out_a
               + _sigmoid(gate_ref[:, d:2 * d].astype(F32)) * ob_ref[...].astype(F32))
        o_ref[...] = mix.astype(o_ref.dtype)

    wide = pl.BlockSpec((tr, 2 * d), lambda i: (i, 0))
    row = pl.BlockSpec((tr, d), lambda i: (i, 0))
    return pl.pallas_call(
        body, name=name, grid=(r // tr,), in_specs=[wide, wide, row], out_specs=row,
        out_shape=jax.ShapeDtypeStruct((r, d), BF16), compiler_params=_cparams(("parallel",)),
    )(glu, gates, out_b)


def _mix_bwd(name, dmix, glu, gates, out_b, *, tr=256):
    r = glu.shape[0]
    d = D_MODEL
    tr = min(tr, r)

    def body(dm_ref, glu_ref, gate_ref, ob_ref, dglu_ref, dgate_ref, dob_ref):
        dm = dm_ref[...]
        glu_a = glu_ref[:, 0:d].astype(F32)
        sb = _sigmoid(glu_ref[:, d:2 * d].astype(F32))
        ga = _sigmoid(gate_ref[:, 0:d].astype(F32))
        gb = _sigmoid(gate_ref[:, d:2 * d].astype(F32))
        out_a = glu_a * sb
        dout_a = dm * ga
        dglu_ref[:, 0:d] = (dout_a * sb).astype(dglu_ref.dtype)
        dglu_ref[:, d:2 * d] = (dout_a * glu_a * sb * (1.0 - sb)).astype(dglu_ref.dtype)
        dgate_ref[:, 0:d] = (dm * out_a * ga * (1.0 - ga)).astype(dgate_ref.dtype)
        dgate_ref[:, d:2 * d] = (dm * ob_ref[...].astype(F32) * gb * (1.0 - gb)).astype(dgate_ref.dtype)
        dob_ref[...] = (dm * gb).astype(dob_ref.dtype)

    wide = pl.BlockSpec((tr, 2 * d), lambda i: (i, 0))
    row = pl.BlockSpec((tr, d), lambda i: (i, 0))
    return pl.pallas_call(
        body, name=name, grid=(r // tr,), in_specs=[row, wide, wide, row], out_specs=(wide, wide, row),
        out_shape=(jax.ShapeDtypeStruct((r, 2 * d), BF16), jax.ShapeDtypeStruct((r, 2 * d), BF16),
                   jax.ShapeDtypeStruct((r, d), BF16)),
        compiler_params=_cparams(("parallel",)),
    )(dmix, glu, gates, out_b)


def _ssm_mats(lam_re, lam_im, log_dt, b_re, b_im, c_re, c_im, nc):
    hp = lax.Precision.HIGHEST
    t = SSM_CHUNK
    nq = SSM_GROUPS // 8
    lam = lax.complex(lam_re, lam_im)
    z = lam * jnp.exp(log_dt)[:, None]
    ks = jnp.arange(t + 1, dtype=F32)
    apow = jnp.exp(ks[:, None, None] * z[None])
    bbar = ((apow[1] - 1.0) / lam)[..., None] * lax.complex(b_re, b_im)
    c = lax.complex(c_re, c_im)

    ca = c[None] * apow[:, :, None, :]
    kmat = jnp.einsum("kgnp,gpm->kgnm", ca, bbar, precision=hp).real
    ii = np.arange(t)
    lag = ii[None, :] - ii[:, None]
    kt = kmat[np.clip(lag, 0, t)] * jnp.asarray(lag >= 0, F32)[:, :, None, None, None]
    kt = kt.reshape(t, t, nq, 8, SSM_GROUP, SSM_GROUP)
    m_c = kt.transpose(2, 0, 3, 5, 1, 4).reshape(nq, 1024, LANE)

    arev = jnp.exp((float(t - 1) - ks[:t])[:, None, None] * z[None])
    w = arev[:, :, :, None] * bbar[None]
    wr = jnp.stack([w.real, w.imag]).reshape(2, t, nq, 8, SSM_STATE, SSM_GROUP)
    bw_c = wr.transpose(2, 1, 3, 5, 0, 4).reshape(nq, 1024, LANE)

    ca1 = ca[1:]
    cr = jnp.stack([ca1.real, -ca1.imag]).reshape(2, t, nq, 8, SSM_GROUP, SSM_STATE)
    cm_c = cr.transpose(2, 0, 3, 5, 1, 4).reshape(nq, 1024, LANE)

    def tiles(v):
        vq = jnp.concatenate([v.real.reshape(nq, 512), v.imag.reshape(nq, 512)], axis=1)
        return jnp.broadcast_to(vq.reshape(nq, 8, 1, LANE), (nq, 8, 8, LANE))

    return m_c, bw_c, cm_c, tiles(apow[t]), tiles(jnp.exp(float(nc) * z))


_BD_M = (LANE, SSM_GROUP)
_BD_BW = (LANE, SSM_STATE)
_BD_CM = (512, SSM_GROUP)


def _bd_perm(cn):
    rr = lax.broadcasted_iota(jnp.int32, (1024, 1024), 0)
    cc = lax.broadcasted_iota(jnp.int32, (1024, 1024), 1)
    sh = cn.bit_length() - 1
    src = ((rr >> 7) << sh) + (((rr & (LANE - 1)) >> sh) << (3 + sh)) + (rr & (cn - 1))
    return jnp.where(src == cc, 1.0, 0.0).astype(BF16)


def _bd_rowgroup(span):
    r = lax.broadcasted_iota(jnp.int32, (1024, LANE), 0)
    return (r & (span - 1)) >> ((span // 8).bit_length() - 1)


def _bd_expand(name, kind, compact):
    span, cn = kind
    nq = compact.shape[0]

    def body(c_ref, o_ref, perm_scr):
        @pl.when(pl.program_id(0) == 0)
        def _():
            perm_scr[...] = _bd_perm(cn)

        x = c_ref[...]
        grp = _bd_rowgroup(span)
        xcat = jnp.concatenate([jnp.where(grp == h, x, 0.0) for h in range(8)], axis=1)
        o_ref[...] = _bdot(xcat, perm_scr[...], _DIMS["nn"]).astype(o_ref.dtype)

    return pl.pallas_call(
        body, name=name, grid=(nq,), in_specs=[pl.BlockSpec((None, 1024, LANE), lambda q: (q, 0, 0))],
        out_specs=pl.BlockSpec((None, 1024, 1024), lambda q: (q, 0, 0)),
        out_shape=jax.ShapeDtypeStruct((nq, 1024, 1024), BF16),
        scratch_shapes=[pltpu.VMEM((1024, 1024), BF16)],
        compiler_params=_cparams(("arbitrary",)),
    )(compact)


def _bd_reduce(name, kind, dbig):
    span, cn = kind
    nq = dbig.shape[0]

    def body(g_ref, o_ref, perm_scr):
        @pl.when(pl.program_id(0) == 0)
        def _():
            perm_scr[...] = _bd_perm(cn)

        back = _bdot(g_ref[...], perm_scr[...], _DIMS["nt"])
        grp = _bd_rowgroup(span)
        out = jnp.zeros((1024, LANE), F32)
        for h in range(8):
            out = jnp.where(grp == h, back[:, h * LANE:(h + 1) * LANE], out)
        o_ref[...] = out

    return pl.pallas_call(
        body, name=name, grid=(nq,), in_specs=[pl.BlockSpec((None, 1024, 1024), lambda q: (q, 0, 0))],
        out_specs=pl.BlockSpec((None, 1024, LANE), lambda q: (q, 0, 0)),
        out_shape=jax.ShapeDtypeStruct((nq, 1024, LANE), F32),
        scratch_shapes=[pltpu.VMEM((1024, 1024), BF16)],
        compiler_params=_cparams(("arbitrary",)),
    )(dbig)


def _x_tile_specs(nc, nq):
    return [pl.BlockSpec((nc, LANE), lambda q, t, i=i: (0, i * nq + q)) for i in range(SSM_CHUNK)]


def _cat_tiles(refs):
    return jnp.concatenate([r[...] for r in refs], axis=1)


def _ssm_w(name, x8, bw):
    nc = x8.shape[0]
    nq = bw.shape[0]

    def body(*refs):
        xq = _cat_tiles(refs[:8])
        refs[9][...] = _bdot(xq, refs[8][...], _DIMS["nn"])

    return pl.pallas_call(
        body, name=name, grid=(nq, 8),
        in_specs=_x_tile_specs(nc, nq) + [pl.BlockSpec((None, 1024, LANE), lambda q, t: (q, 0, t))],
        out_specs=pl.BlockSpec((None, None, nc, LANE), lambda q, t: (q, t, 0, 0)),
        out_shape=jax.ShapeDtypeStruct((nq, 8, nc, LANE), F32),
        compiler_params=_cparams(("parallel", "arbitrary")),
    )(*([x8] * 8), bw)


def _ssm_scan(name, w4, a_t, aseg_t, *, reverse, sprev4=None):
    nq, _, nc, _ = w4.shape
    ns = nc // 8
    with_da = sprev4 is not None

    def body(*refs):
        w_ref, a_ref, aseg_ref = refs[:3]
        s_ref = refs[3] if with_da else None
        o_ref = refs[4] if with_da else refs[3]
        da_ref = refs[5] if with_da else None
        sgn = -1.0 if reverse else 1.0
        ar = [a_ref[j] for j in range(4)]
        ai = [sgn * a_ref[j + 4] for j in range(4)]
        gr = [aseg_ref[j] for j in range(4)]
        gi = [sgn * aseg_ref[j + 4] for j in range(4)]
        zero = tuple(jnp.zeros((8, LANE), F32) for _ in range(8))

        def rows(tt):
            return pl.ds((ns - 1 - tt) if reverse else tt, 8, stride=ns)

        def step(carry, w):
            new_r = [ar[j] * carry[j] - ai[j] * carry[j + 4] + w[j] for j in range(4)]
            new_i = [ar[j] * carry[j + 4] + ai[j] * carry[j] + w[j + 4] for j in range(4)]
            return tuple(new_r + new_i)

        def pass1(tt, carry):
            return step(carry, [w_ref[j, rows(tt), :] for j in range(8)])

        ends = lax.fori_loop(0, ns, pass1, zero)
        sub = lax.broadcasted_iota(jnp.int32, (8, LANE), 0)
        init = list(zero)
        order = range(7, 0, -1) if reverse else range(0, 7)
        for s in order:
            nxt = s - 1 if reverse else s + 1
            cand_r = [gr[j] * init[j] - gi[j] * init[j + 4] + ends[j] for j in range(4)]
            cand_i = [gr[j] * init[j + 4] + gi[j] * init[j] + ends[j + 4] for j in range(4)]
            cand = cand_r + cand_i
            shift = 7 if reverse else 1
            init = [jnp.where(sub == nxt, pltpu.roll(cand[j], shift, axis=0), init[j]) for j in range(8)]

        def pass2(tt, state):
            carry, acc = state
            r = rows(tt)
            for j in range(8):
                o_ref[j, r, :] = carry[j]
            if with_da:
                sp = [s_ref[j, r, :] for j in range(8)]
                acc_r = [acc[j] + carry[j] * sp[j] + carry[j + 4] * sp[j + 4] for j in range(4)]
                acc_i = [acc[j + 4] + carry[j + 4] * sp[j] - carry[j] * sp[j + 4] for j in range(4)]
                acc = tuple(acc_r + acc_i)
            return step(carry, [w_ref[j, r, :] for j in range(8)]), acc

        _, acc = lax.fori_loop(0, ns, pass2, (tuple(init), zero))
        if with_da:
            for j in range(8):
                da_ref[j] = acc[j]

    big = pl.BlockSpec((None, 8, nc, LANE), lambda q: (q, 0, 0, 0))
    small = pl.BlockSpec((None, 8, 8, LANE), lambda q: (q, 0, 0, 0))
    in_specs = [big, small, small] + ([big] if with_da else [])
    ops = [w4, a_t, aseg_t] + ([sprev4] if with_da else [])
    out_specs = (big, small) if with_da else big
    big_s = jax.ShapeDtypeStruct((nq, 8, nc, LANE), F32)
    out_shape = (big_s, jax.ShapeDtypeStruct((nq, 8, 8, LANE), F32)) if with_da else big_s
    return pl.pallas_call(
        body, name=name, grid=(nq,), in_specs=in_specs, out_specs=out_specs, out_shape=out_shape,
        compiler_params=_cparams(("parallel",)),
    )(*ops)


def _ssm_y(name, x8, sprev4, m_mat, cm_mat):
    nc = x8.shape[0]
    nq = m_mat.shape[0]

    def body(*refs):
        xq = _cat_tiles(refs[:8])
        s_ref, m_ref, cm_ref, o_ref = refs[8:12]
        sq = jnp.concatenate([s_ref[t] for t in range(8)], axis=1)
        o_ref[...] = _bdot(xq, m_ref[...], _DIMS["nn"]) + _bdot(sq, cm_ref[...], _DIMS["nn"])

    col = pl.BlockSpec((None, 1024, LANE), lambda q, j: (q, 0, j))
    return pl.pallas_call(
        body, name=name, grid=(nq, 8),
        in_specs=_x_tile_specs(nc, nq) + [pl.BlockSpec((None, 8, nc, LANE), lambda q, j: (q, 0, 0, 0)), col, col],
        out_specs=pl.BlockSpec((nc, LANE), lambda q, j: (0, j * nq + q)),
        out_shape=jax.ShapeDtypeStruct((nc, 8 * SSM_WIDTH), F32),
        compiler_params=_cparams(("parallel", "arbitrary")),
    )(*([x8] * 8), sprev4, m_mat, cm_mat)


def _ssm_ds(name, dz8, sprev4, cm_mat):
    nc = dz8.shape[0]
    nq = cm_mat.shape[0]

    def body(*refs):
        dyq = _cat_tiles(refs[:8]).astype(BF16)
        s_ref, cm_ref, ds_ref, dcm_ref = refs[8:12]
        ds_ref[...] = _bdot(dyq, cm_ref[...], _DIMS["nt"])
        dcm_ref[...] = _bdot(s_ref[...], dyq, _DIMS["tn"])

    tile = pl.BlockSpec((None, None, nc, LANE), lambda q, t: (q, t, 0, 0))
    rowblk = pl.BlockSpec((None, LANE, 1024), lambda q, t: (q, t, 0))
    return pl.pallas_call(
        body, name=name, grid=(nq, 8),
        in_specs=_x_tile_specs(nc, nq) + [tile, rowblk],
        out_specs=(tile, rowblk),
        out_shape=(jax.ShapeDtypeStruct((nq, 8, nc, LANE), F32), jax.ShapeDtypeStruct((nq, 1024, 1024), F32)),
        compiler_params=_cparams(("parallel", "arbitrary")),
    )(*([dz8] * 8), sprev4, cm_mat)


def _ssm_dx(name, dz8, g4, x8, m_mat, bw_mat, d8):
    nc = dz8.shape[0]
    nq = m_mat.shape[0]

    def body(*refs):
        dyq = _cat_tiles(refs[:8]).astype(BF16)
        g_ref, x_ref, m_ref, bw_ref, d_ref, dzi_ref, dx_ref, dm_ref, dbw_ref = refs[8:17]
        gq = jnp.concatenate([g_ref[t] for t in range(8)], axis=1).astype(BF16)
        dx = _bdot(dyq, m_ref[...], _DIMS["nt"]) + _bdot(gq, bw_ref[...], _DIMS["nt"])
        dx_ref[...] = (dx + d_ref[...] * dzi_ref[...]).astype(dx_ref.dtype)
        xi = x_ref[...]
        dm_ref[...] = _bdot(xi, dyq, _DIMS["tn"])
        dbw_ref[...] = _bdot(xi, gq, _DIMS["tn"])

    xtile = pl.BlockSpec((nc, LANE), lambda q, i: (0, i * nq + q))
    rowblk = pl.BlockSpec((None, LANE, 1024), lambda q, i: (q, i, 0))
    return pl.pallas_call(
        body, name=name, grid=(nq, 8),
        in_specs=_x_tile_specs(nc, nq) + [pl.BlockSpec((None, 8, nc, LANE), lambda q, i: (q, 0, 0, 0)), xtile, rowblk, rowblk,
                                          pl.BlockSpec((1, LANE), lambda q, i: (0, q)), xtile],
        out_specs=(xtile, rowblk, rowblk),
        out_shape=(jax.ShapeDtypeStruct((nc, 8 * SSM_WIDTH), BF16), jax.ShapeDtypeStruct((nq, 1024, 1024), F32),
                   jax.ShapeDtypeStruct((nq, 1024, 1024), F32)),
        compiler_params=_cparams(("parallel", "arbitrary")),
    )(*([dz8] * 8), g4, x8, m_mat, bw_mat, d8, dz8)


CUM_BLK = 256


def _split3(x):
    hi = x.astype(BF16)
    r1 = x - hi.astype(F32)
    mid = r1.astype(BF16)
    lo = (r1 - mid.astype(F32)).astype(BF16)
    return hi, mid, lo


def _tri_dot(x, tri):
    hi, mid, lo = _split3(x)
    d = _DIMS["nn"]
    return _bdot(hi, tri, d) + _bdot(mid, tri, d) + _bdot(lo, tri, d)


def _tri(n, lower):
    r = lax.broadcasted_iota(jnp.int32, (n, n), 0)
    c = lax.broadcasted_iota(jnp.int32, (n, n), 1)
    return jnp.where((r >= c) if lower else (r <= c), 1.0, 0.0).astype(BF16)


def _fox_cum(name, fproj, bcol):
    seq = fproj.shape[0]
    blk = min(CUM_BLK, seq)

    def body(f_ref, b_ref, o_ref, carry_ref):
        i = pl.program_id(0)

        @pl.when(i == 0)
        def _():
            carry_ref[...] = jnp.zeros_like(carry_ref)

        z = f_ref[...].T + b_ref[...]
        logf = jnp.minimum(z, 0.0) - jnp.log(1.0 + jnp.exp(-jnp.abs(z)))
        carry = carry_ref[...]
        cum = _tri_dot(logf, _tri(blk, lower=False)) + jnp.tile(carry, (1, blk // LANE))
        o_ref[...] = cum[0:8, :]
        carry_ref[...] = carry + jnp.sum(logf, axis=1, keepdims=True)

    return pl.pallas_call(
        body, name=name, grid=(seq // blk,),
        in_specs=[pl.BlockSpec((blk, LANE), lambda i: (i, 0)), pl.BlockSpec((LANE, 1), lambda i: (0, 0))],
        out_specs=pl.BlockSpec((8, blk), lambda i: (0, i)),
        out_shape=jax.ShapeDtypeStruct((8, seq), F32),
        scratch_shapes=[pltpu.VMEM((LANE, LANE), F32)],
        compiler_params=_cparams(("arbitrary",)),
    )(fproj, bcol)


def _fox_cum_bwd(name, dcs, fproj, bcol):
    seq = fproj.shape[0]
    blk = min(CUM_BLK, seq)
    n = seq // blk

    def body(dc_ref, f_ref, b_ref, df_ref, db_ref, carry_ref, acc_ref):
        i = pl.program_id(0)

        @pl.when(i == 0)
        def _():
            carry_ref[...] = jnp.zeros_like(carry_ref)
            acc_ref[...] = jnp.zeros_like(acc_ref)

        r = lax.broadcasted_iota(jnp.int32, (LANE, FOX_WIDTH), 0)
        c = lax.broadcasted_iota(jnp.int32, (LANE, FOX_WIDTH), 1)
        want = (r >> 1) * LANE + jnp.where((r & 1) == 0, FOX_HEAD_DIM, 0)
        sel = jnp.where(jnp.logical_and(r < FOX_HEADS, c == want), 1.0, 0.0).astype(BF16)
        hi, mid, lo = _split3(dc_ref[...])
        nt = _DIMS["nt"]
        dc = _bdot(sel, hi, nt) + _bdot(sel, mid, nt) + _bdot(sel, lo, nt)
        carry = carry_ref[...]
        dlogf = _tri_dot(dc, _tri(blk, lower=True)) + jnp.tile(carry, (1, blk // LANE))
        carry_ref[...] = carry + jnp.sum(dc, axis=1, keepdims=True)
        z = f_ref[...].T + b_ref[...]
        dft = dlogf / (1.0 + jnp.exp(z))
        df_ref[...] = dft.T.astype(df_ref.dtype)
        acc_ref[...] += jnp.sum(dft, axis=1, keepdims=True)

        @pl.when(i == n - 1)
        def _():
            db_ref[...] = acc_ref[...]

    return pl.pallas_call(
        body, name=name, grid=(n,),
        in_specs=[pl.BlockSpec((blk, FOX_WIDTH), lambda i: (n - 1 - i, 0)), pl.BlockSpec((blk, LANE), lambda i: (n - 1 - i, 0)),
                  pl.BlockSpec((LANE, 1), lambda i: (0, 0))],
        out_specs=(pl.BlockSpec((blk, LANE), lambda i: (n - 1 - i, 0)), pl.BlockSpec((LANE, LANE), lambda i: (0, 0))),
        out_shape=(jax.ShapeDtypeStruct((seq, LANE), BF16), jax.ShapeDtypeStruct((LANE, LANE), F32)),
        scratch_shapes=[pltpu.VMEM((LANE, LANE), F32), pltpu.VMEM((LANE, LANE), F32)],
        compiler_params=_cparams(("arbitrary",)),
    )(dcs, fproj, bcol)


FOX_BLK = 512
FOX_SCALE = FOX_HEAD_DIM ** -0.5


def _fox_head_mask(shape, hh):
    lane = lax.broadcasted_iota(jnp.int32, shape, 1)
    return (lane < FOX_HEAD_DIM) if hh == 0 else (lane >= FOX_HEAD_DIM)


def _fox_bias(cum_ref, hh, q0, k0, blk):
    c0 = jnp.max(cum_ref[hh:hh + 1, pl.ds(q0, LANE)], axis=1, keepdims=True)
    return c0 - cum_ref[hh:hh + 1, pl.ds(k0, blk)]


def _fox_fwd(name, qkv, cum_t):
    seq = qkv.shape[0]
    blk = min(FOX_BLK, seq)
    nb = seq // blk
    npair = FOX_HEADS // 2

    def body(q_ref, k_ref, v_ref, cum_ref, o_ref, lse_ref):
        iq = pl.program_id(1)
        q0 = pl.multiple_of(iq * blk, blk)
        qv = q_ref[...]
        row = lax.broadcasted_iota(jnp.int32, (blk, blk), 0)
        col = lax.broadcasted_iota(jnp.int32, (blk, blk), 1)
        qhs = [jnp.where(_fox_head_mask(qv.shape, hh), qv, jnp.zeros_like(qv)) * FOX_SCALE for hh in range(2)]

        def block(kb, states, masked):
            k0 = pl.multiple_of(kb * blk, blk)
            kv = k_ref[pl.ds(k0, blk), :]
            vv = v_ref[pl.ds(k0, blk), :]
            new = []
            for hh in range(2):
                m, acc = states[hh]
                s = _bdot(qhs[hh], kv, _DIMS["nt"]) + _fox_bias(cum_ref, hh, q0, k0, blk)
                if masked:
                    s = jnp.where(row >= col, s, -jnp.inf)
                m_new = jnp.maximum(m, jnp.max(s, axis=1, keepdims=True))
                p = jnp.exp(s - m_new)
                vh = jnp.where(_fox_head_mask(vv.shape, hh), vv, jnp.ones_like(vv))
                acc = jnp.exp(m - m_new) * acc + _bdot(p, vh, _DIMS["nn"])
                new.append((m_new, acc))
            return tuple(new)

        init = (jnp.full((blk, 1), -jnp.inf, F32), jnp.zeros((blk, LANE), F32))
        states = lax.fori_loop(0, iq, lambda kb, st: block(kb, st, False), (init, init))
        states = block(iq, states, True)
        outs = []
        for hh in range(2):
            m, acc = states[hh]
            other = pltpu.roll(acc, FOX_HEAD_DIM, axis=1)
            outs.append(acc / other)
            lse_ref[hh] = m + jnp.log(jnp.where(_fox_head_mask(acc.shape, hh), other, acc))
        o_ref[...] = jnp.where(_fox_head_mask(outs[0].shape, 0), outs[0], outs[1]).astype(o_ref.dtype)

    return pl.pallas_call(
        body, name=name, grid=(npair, nb),
        in_specs=[pl.BlockSpec((blk, LANE), lambda p, i: (i, p)),
                  pl.BlockSpec((seq, LANE), lambda p, i: (0, npair + p)),
                  pl.BlockSpec((seq, LANE), lambda p, i: (0, 2 * npair + p)),
                  pl.BlockSpec((None, 2, seq), lambda p, i: (p, 0, 0))],
        out_specs=(pl.BlockSpec((blk, LANE), lambda p, i: (i, p)),
                   pl.BlockSpec((2, blk, LANE), lambda p, i: (p, i, 0))),
        out_shape=(jax.ShapeDtypeStruct((seq, FOX_WIDTH), BF16), jax.ShapeDtypeStruct((FOX_HEADS, seq, LANE), F32)),
        compiler_params=_cparams(("parallel", "arbitrary")),
    )(qkv, qkv, qkv, cum_t)


def _fox_bwd(name, qkv, cum_t, att, datt, lse):
    seq = qkv.shape[0]
    blk = min(FOX_BLK, seq)
    nb = seq // blk
    npair = FOX_HEADS // 2

    def body(q_ref, k_ref, v_ref, cum_ref, o_ref, do_ref, lse_ref, dq_ref, dk_ref, dv_ref, dcs_ref):
        iq = pl.program_id(1)
        q0 = pl.multiple_of(iq * blk, blk)

        @pl.when(iq == 0)
        def _():
            dk_ref[...] = jnp.zeros_like(dk_ref)
            dv_ref[...] = jnp.zeros_like(dv_ref)
            dcs_ref[...] = jnp.zeros_like(dcs_ref)

        qv = q_ref[...]
        dov = do_ref[...].astype(F32)
        ov = o_ref[...].astype(F32)
        row = lax.broadcasted_iota(jnp.int32, (blk, blk), 0)
        col = lax.broadcasted_iota(jnp.int32, (blk, blk), 1)
        low = _fox_head_mask((blk, LANE), 0)
        qhs, qones, dohbs, deltas, lses = [], [], [], [], []
        for hh in range(2):
            hm = _fox_head_mask(qv.shape, hh)
            qh = jnp.where(hm, qv, jnp.zeros_like(qv)) * FOX_SCALE
            qhs.append(qh)
            qones.append(jnp.where(hm, qh, jnp.ones_like(qh)))
            doh = jnp.where(hm, dov, 0.0)
            dohbs.append(doh.astype(BF16))
            deltas.append(jnp.sum(doh * ov, axis=1, keepdims=True))
            lses.append(jnp.tile(lse_ref[hh], (1, blk // LANE)))

        def block(kb, dqs, masked):
            k0 = pl.multiple_of(kb * blk, blk)
            kv = k_ref[pl.ds(k0, blk), :]
            vv = v_ref[pl.ds(k0, blk), :]
            new, dks, dvs = [], [], []
            for hh in range(2):
                s = _bdot(qhs[hh], kv, _DIMS["nt"]) + _fox_bias(cum_ref, hh, q0, k0, blk)
                p = jnp.exp(s - lses[hh])
                if masked:
                    p = jnp.where(row >= col, p, 0.0)
                dp = _bdot(dohbs[hh], vv, _DIMS["nt"])
                dsb = (p * (dp - deltas[hh])).astype(BF16)
                dks.append(_bdot(dsb, qones[hh], _DIMS["tn"]))
                dvs.append(_bdot(p, dohbs[hh], _DIMS["tn"]))
                kones = jnp.where(_fox_head_mask(kv.shape, hh), kv, jnp.ones_like(kv))
                new.append(dqs[hh] + _bdot(dsb, kones, _DIMS["nn"]))
            dk_ref[pl.ds(k0, blk), :] += jnp.where(low, dks[0], dks[1])
            dv_ref[pl.ds(k0, blk), :] += dvs[0] + dvs[1]
            dcs_ref[pl.ds(k0, blk), :] -= jnp.where(low, dks[1], dks[0])
            return tuple(new)

        init = jnp.zeros((blk, LANE), F32)
        dqs = lax.fori_loop(0, iq, lambda kb, a: block(kb, a, False), (init, init))
        dqs = block(iq, dqs, True)
        dcs_ref[pl.ds(q0, blk), :] += jnp.where(low, dqs[1], dqs[0])
        dq_ref[...] = (jnp.where(low, dqs[0], dqs[1]) * FOX_SCALE).astype(dq_ref.dtype)

    qblk = pl.BlockSpec((blk, LANE), lambda p, i: (i, p))
    full = pl.BlockSpec((seq, LANE), lambda p, i: (0, p))
    return pl.pallas_call(
        body, name=name, grid=(npair, nb),
        in_specs=[qblk,
                  pl.BlockSpec((seq, LANE), lambda p, i: (0, npair + p)),
                  pl.BlockSpec((seq, LANE), lambda p, i: (0, 2 * npair + p)),
                  pl.BlockSpec((None, 2, seq), lambda p, i: (p, 0, 0)),
                  qblk, qblk,
                  pl.BlockSpec((2, blk, LANE), lambda p, i: (p, i, 0))],
        out_specs=(qblk, full, full, full),
        out_shape=(jax.ShapeDtypeStruct((seq, FOX_WIDTH), BF16), jax.ShapeDtypeStruct((seq, FOX_WIDTH), F32),
                   jax.ShapeDtypeStruct((seq, FOX_WIDTH), F32), jax.ShapeDtypeStruct((seq, FOX_WIDTH), F32)),
        compiler_params=_cparams(("arbitrary", "arbitrary")),
    )(qkv, qkv, qkv, cum_t, att, datt, lse)


MEM_SCALE = MEM_HEAD_DIM ** -0.5


def _mem_probs(qh, kh):
    s = _bdot(qh, kh, _DIMS["nt"]) * MEM_SCALE
    p = jnp.exp(s - jnp.max(s, axis=1, keepdims=True))
    return p / jnp.sum(p, axis=1, keepdims=True)


def _mem_fwd(name, q2, kv, *, tr=512):
    seq = q2.shape[0]
    mlen = kv.shape[0]
    tr = min(tr, seq)

    def body(q_ref, kv_ref, o_ref):
        for h in range(MEM_HEADS):
            sl = slice(h * MEM_HEAD_DIM, (h + 1) * MEM_HEAD_DIM)
            sv = slice(MEM_WIDTH + h * MEM_HEAD_DIM, MEM_WIDTH + (h + 1) * MEM_HEAD_DIM)
            p = _mem_probs(q_ref[:, sl], kv_ref[:, sl])
            o_ref[:, sl] = _bdot(p, kv_ref[:, sv], _DIMS["nn"]).astype(o_ref.dtype)

    return pl.pallas_call(
        body, name=name, grid=(seq // tr,),
        in_specs=[pl.BlockSpec((tr, MEM_WIDTH), lambda i: (i, 0)), pl.BlockSpec((mlen, 2 * MEM_WIDTH), lambda i: (0, 0))],
        out_specs=pl.BlockSpec((tr, MEM_WIDTH), lambda i: (i, 0)),
        out_shape=jax.ShapeDtypeStruct((seq, MEM_WIDTH), BF16),
        compiler_params=_cparams(("parallel",)),
    )(q2, kv)


def _mem_bwd(name, q2, kv, do2, *, tr=512):
    seq = q2.shape[0]
    mlen = kv.shape[0]
    tr = min(tr, seq)

    def body(q_ref, kv_ref, do_ref, dq_ref, dkv_ref):
        i = pl.program_id(0)

        @pl.when(i == 0)
        def _():
            dkv_ref[...] = jnp.zeros_like(dkv_ref)

        for h in range(MEM_HEADS):
            sl = slice(h * MEM_HEAD_DIM, (h + 1) * MEM_HEAD_DIM)
            sv = slice(MEM_WIDTH + h * MEM_HEAD_DIM, MEM_WIDTH + (h + 1) * MEM_HEAD_DIM)
            qh = q_ref[:, sl]
            kh = kv_ref[:, sl]
            doh = do_ref[:, sl].astype(BF16)
            p = _mem_probs(qh, kh)
            dp = _bdot(doh, kv_ref[:, sv], _DIMS["nt"])
            ds = (p * (dp - jnp.sum(p * dp, axis=1, keepdims=True)) * MEM_SCALE).astype(BF16)
            dq_ref[:, sl] = _bdot(ds, kh, _DIMS["nn"]).astype(dq_ref.dtype)
            dkv_ref[:, sl] += _bdot(ds, qh, _DIMS["tn"])
            dkv_ref[:, sv] += _bdot(p, doh, _DIMS["tn"])

    row = pl.BlockSpec((tr, MEM_WIDTH), lambda i: (i, 0))
    kvs = pl.BlockSpec((mlen, 2 * MEM_WIDTH), lambda i: (0, 0))
    return pl.pallas_call(
        body, name=name, grid=(seq // tr,), in_specs=[row, kvs, row], out_specs=(row, kvs),
        out_shape=(jax.ShapeDtypeStruct((seq, MEM_WIDTH), BF16), jax.ShapeDtypeStruct((mlen, 2 * MEM_WIDTH), F32)),
        compiler_params=_cparams(("arbitrary",)),
    )(q2, kv, do2)


_HBM = pl.BlockSpec(memory_space=pl.ANY)
_HBM_ONLY = pl.BlockSpec(memory_space=pltpu.HBM)
_MESH = pl.DeviceIdType.MESH


def _mesh_place():
    x, y, c = lax.axis_index("x"), lax.axis_index("y"), lax.axis_index("c")
    other_chips = [(1 - x, y), (x, 1 - y), (1 - x, 1 - y)]
    return x, y, c, other_chips


def _gather_all(name, arrays):
    n = len(arrays)

    def body(*refs):
        ins, outs = refs[:n], refs[n:2 * n]
        send_sems, recv_sems, local_sems = refs[2 * n:]
        x, y, c, chips = _mesh_place()
        me, sibling = (x, y, c), (x, y, 1 - c)

        def slot(a, place):
            px, py, pc = place
            return outs[a].at[4 * px + 2 * py + pc]

        def copy(a, k, block, to, src=None):
            return pltpu.make_async_remote_copy(
                src_ref=slot(a, block) if src is None else src, dst_ref=slot(a, block),
                send_sem=send_sems.at[a, k], recv_sem=recv_sems.at[a, k], device_id=to, device_id_type=_MESH)

        mine = [pltpu.make_async_copy(ins[a], slot(a, me), local_sems.at[a]) for a in range(n)]
        for cp in mine:
            cp.start()
        first = []
        for a in range(n):
            first.append(copy(a, 0, me, sibling, src=ins[a]))
            first += [copy(a, 1 + j, me, (*chip, c), src=ins[a]) for j, chip in enumerate(chips)]
        for cp in first:
            cp.start()
        passed = []
        for j, chip in enumerate(chips):
            for a in range(n):
                copy(a, 1 + j, (*chip, c), me).wait_recv()
                fwd = copy(a, 4 + j, (*chip, c), sibling)
                fwd.start()
                passed.append(fwd)
        for a in range(n):
            copy(a, 0, sibling, me).wait_recv()
            for j, chip in enumerate(chips):
                copy(a, 4 + j, (*chip, 1 - c), me).wait_recv()
        for cp in first + passed:
            cp.wait_send()
        for cp in mine:
            cp.wait()

    out_shape = tuple(jax.ShapeDtypeStruct((N_DEV,) + arr.shape, arr.dtype) for arr in arrays)
    return pl.pallas_call(
        body, name=name, in_specs=[_HBM] * n, out_specs=tuple([_HBM] * n), out_shape=out_shape,
        scratch_shapes=[pltpu.SemaphoreType.DMA((n, N_DEV - 1)), pltpu.SemaphoreType.DMA((n, N_DEV - 1)),
                        pltpu.SemaphoreType.DMA((n,))],
    )(*arrays)


_SEM = pl.BlockSpec(memory_space=pltpu.SEMAPHORE)
_DATAFLOW = pltpu.SideEffectType.DATAFLOW_SIDE_EFFECTING


def _device_index():
    return (4 * lax.axis_index("x") + 2 * lax.axis_index("y") + lax.axis_index("c")).astype(jnp.int32).reshape(1)


def _place_own(name, pieces, *, stacked_src, after=None):
    n = len(pieces)
    n_in = n + (after is not None)

    def body(me_ref, *refs):
        for a in range(n):
            refs[n_in + a][...] = refs[a][...]

    def spec(shape):
        return pl.BlockSpec((None,) + tuple(shape), lambda i, me_ref: (me_ref[0],) + (0,) * len(shape))

    shapes = [p.shape[1:] if stacked_src else p.shape for p in pieces]
    if stacked_src:
        in_specs = [spec(s) for s in shapes]
    else:
        in_specs = [pl.BlockSpec(tuple(s), lambda i, me_ref, nd=len(s): (0,) * nd) for s in shapes]
    operands = list(pieces)
    if after is not None:
        in_specs.append(_HBM)
        operands.append(after)
    return pl.pallas_call(
        body, name=name,
        grid_spec=pltpu.PrefetchScalarGridSpec(num_scalar_prefetch=1, grid=(1,), in_specs=in_specs,
                                               out_specs=tuple(spec(s) for s in shapes)),
        out_shape=tuple(jax.ShapeDtypeStruct((N_DEV,) + tuple(s), p.dtype) for s, p in zip(shapes, pieces)),
        compiler_params=_cparams(("arbitrary",)),
    )(_device_index(), *operands)


def _peer_places():
    x, y, c = lax.axis_index("x"), lax.axis_index("y"), lax.axis_index("c")
    peers = []
    for k in range(N_DEV - 1):
        flip = k + 1
        px = 1 - x if flip & 4 else x
        py = 1 - y if flip & 2 else y
        pc = 1 - c if flip & 1 else c
        peers.append((px, py, pc, 4 * px + 2 * py + pc))
    return 4 * x + 2 * y + c, peers


def _direct_copy(srcs, lands, send_sems, recv_sems, a, k, me, peer, scatter):
    px, py, pc, pidx = peer
    return pltpu.make_async_remote_copy(
        src_ref=srcs[a].at[pidx] if scatter else srcs[a], dst_ref=lands[a].at[me],
        send_sem=send_sems.at[a * (N_DEV - 1) + k], recv_sem=recv_sems.at[a * (N_DEV - 1) + k],
        device_id=(px, py, pc), device_id_type=_MESH)


def _send_start(name, srcs, lands, *, scatter):
    n = len(srcs)

    def body(*refs):
        src_refs, land_refs = refs[:n], refs[n:2 * n]
        send_sems, recv_sems = refs[2 * n], refs[2 * n + 1]
        token = refs[-1]
        me, peers = _peer_places()
        for k, peer in enumerate(peers):
            for a in range(n):
                _direct_copy(src_refs, land_refs, send_sems, recv_sems, a, k, me, peer, scatter).start()
        token[...] = jnp.zeros_like(token)

    hbm_shapes = [pltpu.HBM(t.shape, t.dtype) for t in list(srcs) + list(lands)]
    outs = pl.pallas_call(
        body, name=name,
        out_shape=(pltpu.SemaphoreType.DMA((n * (N_DEV - 1),)), pltpu.SemaphoreType.DMA((n * (N_DEV - 1),)), *hbm_shapes,
                   jax.ShapeDtypeStruct((8, LANE), F32)),
        in_specs=[_HBM_ONLY] * (2 * n),
        out_specs=(_SEM, _SEM, *([_HBM_ONLY] * (2 * n)), pl.BlockSpec(memory_space=pltpu.VMEM)),
        input_output_aliases={i: 2 + i for i in range(2 * n)},
        compiler_params=pltpu.CompilerParams(has_side_effects=_DATAFLOW),
    )(*[pltpu.with_memory_space_constraint(t, pltpu.HBM) for t in list(srcs) + list(lands)])
    return outs[0], outs[1], outs[2:2 + n], outs[2 + n:2 + 2 * n], outs[-1]


def _send_wait(name, send_sems, recv_sems, srcs, lands, after, *, scatter):
    n = len(srcs)
    afters = list(after) if isinstance(after, (tuple, list)) else [after]

    def body(*refs):
        src_refs, land_refs = refs[:n], refs[n:2 * n]
        send_sems, recv_sems = refs[2 * n], refs[2 * n + 1]
        me, peers = _peer_places()
        for k, peer in enumerate(peers):
            for a in range(n):
                cp = _direct_copy(src_refs, land_refs, send_sems, recv_sems, a, k, me, peer, scatter)
                cp.wait_send()
                cp.wait_recv()

    hbm_shapes = [pltpu.HBM(t.shape, t.dtype) for t in list(srcs) + list(lands)]
    outs = pl.pallas_call(
        body, name=name, out_shape=tuple(hbm_shapes),
        in_specs=[_HBM_ONLY] * (2 * n) + [_SEM, _SEM] + [_HBM] * len(afters),
        out_specs=tuple([_HBM_ONLY] * (2 * n)),
        input_output_aliases={i: i for i in range(2 * n)},
        compiler_params=pltpu.CompilerParams(has_side_effects=_DATAFLOW),
    )(*srcs, *lands, send_sems, recv_sems, *afters)
    return outs[n:]


def _unstack_cols(name, stacked):
    n, rows, cols = stacked.shape

    def body(i_ref, o_ref):
        o_ref[...] = i_ref[...]

    return pl.pallas_call(
        body, name=name, grid=(n,), in_specs=[pl.BlockSpec((None, rows, cols), lambda k: (k, 0, 0))],
        out_specs=pl.BlockSpec((rows, cols), lambda k: (0, k)),
        out_shape=jax.ShapeDtypeStruct((rows, n * cols), stacked.dtype),
        compiler_params=_cparams(("parallel",)),
    )(stacked)


def _restack_cols(name, mat):
    rows, width = mat.shape
    cols = width // N_DEV

    def body(i_ref, o_ref):
        o_ref[...] = i_ref[...]

    return pl.pallas_call(
        body, name=name, grid=(N_DEV,), in_specs=[pl.BlockSpec((rows, cols), lambda k: (0, k))],
        out_specs=pl.BlockSpec((None, rows, cols), lambda k: (k, 0, 0)),
        out_shape=jax.ShapeDtypeStruct((N_DEV, rows, cols), mat.dtype),
        compiler_params=_cparams(("parallel",)),
    )(mat)


def _remap_pieces(runs):
    plan = {}
    for du, dc, su, sc, ln in runs:
        while ln > 0:
            lane = dc % LANE
            take = min(ln, LANE - lane)
            plan.setdefault((du, dc // LANE), []).append((su, sc, take, lane))
            dc, sc, ln = dc + take, sc + take, ln - take
    return plan


def _remap(name, srcs, src_units, runs, *, out_units, out_cols, out_dtype, tr=256):
    rows = srcs[0].shape[-2]
    tr = min(tr, rows)
    plan = _remap_pieces(runs)
    n_src = len(srcs)
    stacked_out = out_units is not None
    n_tiles = out_cols // LANE

    def body(*refs):
        o_ref = refs[n_src]

        def src_tile(unit, t):
            ai, lead = src_units[unit]
            ref = refs[ai]
            sl = slice(t * LANE, (t + 1) * LANE)
            return (ref[:, sl] if lead is None else ref[lead, :, sl]).astype(F32)

        lane = lax.broadcasted_iota(jnp.int32, (tr, LANE), 1)
        for du in range(out_units if stacked_out else 1):
            for t in range(n_tiles):
                acc = jnp.zeros((tr, LANE), F32)
                for su, sc, ln, dl in plan.get((du if stacked_out else None, t), []):
                    st, so = sc // LANE, sc % LANE
                    first = src_tile(su, st)
                    if so == dl and so + ln <= LANE:
                        piece = first
                    else:
                        second = src_tile(su, st + 1) if so + ln > LANE else first
                        both = jnp.concatenate([first, second], axis=1)
                        piece = pltpu.roll(both, (dl - so) % (2 * LANE), axis=1)[:, 0:LANE]
                    acc = piece if (dl == 0 and ln == LANE) else jnp.where(
                        jnp.logical_and(lane >= dl, lane < dl + ln), piece, acc)
                if stacked_out:
                    o_ref[du, :, t * LANE:(t + 1) * LANE] = acc.astype(o_ref.dtype)
                else:
                    o_ref[:, t * LANE:(t + 1) * LANE] = acc.astype(o_ref.dtype)

    in_specs = []
    for arr in srcs:
        if arr.ndim == 2:
            in_specs.append(pl.BlockSpec((tr, arr.shape[1]), lambda i: (i, 0)))
        else:
            in_specs.append(pl.BlockSpec((arr.shape[0], tr, arr.shape[2]), lambda i: (0, i, 0)))
    if stacked_out:
        out_spec = pl.BlockSpec((out_units, tr, out_cols), lambda i: (0, i, 0))
        out_shape = jax.ShapeDtypeStruct((out_units, rows, out_cols), out_dtype)
    else:
        out_spec = pl.BlockSpec((tr, out_cols), lambda i: (i, 0))
        out_shape = jax.ShapeDtypeStruct((rows, out_cols), out_dtype)
    return pl.pallas_call(
        body, name=name, grid=(rows // tr,), in_specs=in_specs, out_specs=out_spec, out_shape=out_shape,
        compiler_params=_cparams(("parallel",)),
    )(*srcs)


def _proj_col(c):
    if c < PROJ_GATE0:
        return c
    if c < PROJ_GATE0 + FOX_HEADS:
        return PROJ_F0 + (c - PROJ_GATE0)
    return c - FOX_HEADS


def _win_runs():
    cuts = sorted(set([0, PROJ_GATE0, PROJ_GATE0 + FOX_HEADS, IN_WIDTH] + [SHARD_IN * k for k in range(N_DEV + 1)]))
    return [(lo // SHARD_IN, lo % SHARD_IN, _proj_col(lo), hi - lo) for lo, hi in zip(cuts[:-1], cuts[1:])]


def _assemble_win(name, stacked):
    runs = [(None, pc, k, sc, ln) for k, sc, pc, ln in _win_runs()]
    return _remap(name, [stacked], [(0, k) for k in range(N_DEV)], runs,
                  out_units=None, out_cols=PROJ_WIDTH, out_dtype=BF16)


def _disassemble_dwin(name, dw):
    runs = [(k, sc, 0, pc, ln) for k, sc, pc, ln in _win_runs()]
    return _remap(name, [dw], [(0, None)], runs, out_units=N_DEV, out_cols=SHARD_IN_PAD, out_dtype=BF16)


def _concat_cols(name, parts, *, tr=512):
    rows = parts[0].shape[0]
    tr = min(tr, rows)
    widths = [p.shape[1] for p in parts]
    total = sum(widths)

    def body(*refs):
        o_ref = refs[len(parts)]
        lo = 0
        for r, w in zip(refs[:len(parts)], widths):
            o_ref[:, lo:lo + w] = r[...].astype(o_ref.dtype)
            lo += w

    return pl.pallas_call(
        body, name=name, grid=(rows // tr,),
        in_specs=[pl.BlockSpec((tr, w), lambda i: (i, 0)) for w in widths],
        out_specs=pl.BlockSpec((tr, total), lambda i: (i, 0)),
        out_shape=jax.ShapeDtypeStruct((rows, total), BF16),
        compiler_params=_cparams(("parallel",)),
    )(*parts)


FFN_BLK = FFN_HIDDEN // 2


def _ffn_col(c):
    half, r = divmod(c, FFN_HIDDEN)
    blk, r = divmod(r, FFN_BLK)
    return blk * 2 * FFN_BLK + half * FFN_BLK + r


def _assemble_wffn(name, stacked):
    runs = [(None, _ffn_col(SHARD_FFN * k), k, 0, SHARD_FFN) for k in range(N_DEV)]
    return _remap(name, [stacked], [(0, k) for k in range(N_DEV)], runs,
                  out_units=None, out_cols=2 * FFN_HIDDEN, out_dtype=BF16)


def _disassemble_dwffn(name, dw):
    runs = [(k, 0, 0, _ffn_col(SHARD_FFN * k), SHARD_FFN) for k in range(N_DEV)]
    return _remap(name, [dw], [(0, None)], runs, out_units=N_DEV, out_cols=SHARD_FFN_PAD, out_dtype=BF16)


def _ffn_in_swiglu(name, xn, w, *, tm=512):
    rows, k = xn.shape
    tm = min(tm, rows)
    nblk = FFN_HIDDEN // FFN_BLK

    def body(x_ref, w_ref, f_ref, g_ref):
        f = _bdot(x_ref[...], w_ref[...], _DIMS["nn"])
        f_ref[...] = f.astype(f_ref.dtype)
        fa = f[:, 0:FFN_BLK]
        g_ref[...] = (fa * _sigmoid(fa) * f[:, FFN_BLK:2 * FFN_BLK]).astype(g_ref.dtype)

    return pl.pallas_call(
        body, name=name, grid=(nblk, rows // tm),
        in_specs=[pl.BlockSpec((tm, k), lambda j, i: (i, 0)), pl.BlockSpec((k, 2 * FFN_BLK), lambda j, i: (0, j))],
        out_specs=(pl.BlockSpec((tm, 2 * FFN_BLK), lambda j, i: (i, j)), pl.BlockSpec((tm, FFN_BLK), lambda j, i: (i, j))),
        out_shape=(jax.ShapeDtypeStruct((rows, 2 * FFN_HIDDEN), BF16), jax.ShapeDtypeStruct((rows, FFN_HIDDEN), BF16)),
        compiler_params=_cparams(("parallel", "arbitrary")),
    )(xn, w)


def _d_ffn_out_swiglu(name, dh, w_out, f, *, tm=512):
    rows, d = dh.shape
    tm = min(tm, rows)
    nblk = FFN_HIDDEN // FFN_BLK

    def body(dh_ref, w_ref, f_ref, df_ref):
        dg = _bdot(dh_ref[...], w_ref[...], _DIMS["nt"])
        fa = f_ref[:, 0:FFN_BLK].astype(F32)
        fb = f_ref[:, FFN_BLK:2 * FFN_BLK].astype(F32)
        s = _sigmoid(fa)
        df_ref[:, 0:FFN_BLK] = (dg * fb * s * (1.0 + fa * (1.0 - s))).astype(df_ref.dtype)
        df_ref[:, FFN_BLK:2 * FFN_BLK] = (dg * fa * s).astype(df_ref.dtype)

    wide = pl.BlockSpec((tm, 2 * FFN_BLK), lambda j, i: (i, j))
    return pl.pallas_call(
        body, name=name, grid=(nblk, rows // tm),
        in_specs=[pl.BlockSpec((tm, d), lambda j, i: (i, 0)), pl.BlockSpec((FFN_BLK, d), lambda j, i: (j, 0)), wide],
        out_specs=wide, out_shape=jax.ShapeDtypeStruct((rows, 2 * FFN_HIDDEN), BF16),
        compiler_params=_cparams(("parallel", "arbitrary")),
    )(dh, w_out, f)


def _adamw(name, parts, w, m, v, *, tr=128):
    rows, cols = w.shape
    n_parts = parts.shape[0]
    tr = min(tr, rows)
    assert rows % tr == 0, (name, rows, tr)
    c1 = 1.0 - ADAM_B1 ** ADAM_STEP
    c2 = 1.0 - ADAM_B2 ** ADAM_STEP

    def body(p_ref, w_ref, m_ref, v_ref, g_ref, d_ref, nm_ref, nv_ref):
        g = p_ref[0].astype(F32)
        for s in range(1, n_parts):
            g = g + p_ref[s].astype(F32)
        m_new = ADAM_B1 * m_ref[...] + (1.0 - ADAM_B1) * g
        v_new = ADAM_B2 * v_ref[...] + (1.0 - ADAM_B2) * (g * g)
        upd = (m_new / c1) / (jnp.sqrt(v_new / c2) + ADAM_EPS) + ADAM_WD * w_ref[...]
        g_ref[...] = g
        d_ref[...] = -ADAM_LR * upd
        nm_ref[...] = m_new
        nv_ref[...] = v_new

    row = pl.BlockSpec((tr, cols), lambda i: (i, 0))
    out = jax.ShapeDtypeStruct((rows, cols), F32)
    return pl.pallas_call(
        body, name=name, grid=(rows // tr,),
        in_specs=[pl.BlockSpec((n_parts, tr, cols), lambda i: (0, i, 0)), row, row, row],
        out_specs=(row, row, row, row), out_shape=(out, out, out, out),
        compiler_params=_cparams(("parallel",)),
    )(parts, w, m, v)


_WEIGHTS = ("norm_mix", "w_in", "b_forget", "lam_re", "lam_im", "log_dt", "b_re", "b_im", "c_re", "c_im",
            "d_skip", "w_glu", "w_fox_o", "w_mix_out", "norm_mem_q", "norm_mem_kv", "w_mem_q", "w_mem_kv",
            "w_mem_o", "norm_ffn", "w_ffn_in", "w_ffn_out", "norm_final")
_SHARDED = ("w_in", "w_glu", "w_fox_o", "w_mix_out", "w_mem_q", "w_mem_kv", "w_mem_o", "w_ffn_in", "w_ffn_out")
_SMALL = tuple(n for n in _WEIGHTS if n not in _SHARDED)
_PACK_COLS = 1024


def _pack(arrays):
    flat = jnp.concatenate([a.reshape(-1).astype(F32) for a in arrays])
    rows = -(-flat.shape[0] // _PACK_COLS)
    return jnp.pad(flat, (0, rows * _PACK_COLS - flat.shape[0])).reshape(rows, _PACK_COLS)


def _unpack(buf, like):
    flat = buf.reshape(-1)
    out, pos = [], 0
    for a in like:
        out.append(flat[pos:pos + a.size].reshape(a.shape))
        pos += a.size
    return out


def _mm(name, a, b, mode, m, n, k, out_dtype, tm=1024, tn=512, tk=1024, **kw):
    return _matmul(name, a, b, mode, m, n, k, out_dtype=out_dtype, tm=tm, tn=tn, tk=tk, **kw)


def kernel(x, mem, norm_mix, w_in, b_forget, lam_re, lam_im, log_dt, b_re, b_im, c_re, c_im, d_skip, w_glu, w_fox_o, w_mix_out, norm_mem_q, norm_mem_kv, w_mem_q, w_mem_kv, w_mem_o, norm_ffn, w_ffn_in, w_ffn_out, norm_final, loss_target, m_norm_mix, m_w_in, m_b_forget, m_lam_re, m_lam_im, m_log_dt, m_b_re, m_b_im, m_c_re, m_c_im, m_d_skip, m_w_glu, m_w_fox_o, m_w_mix_out, m_norm_mem_q, m_norm_mem_kv, m_w_mem_q, m_w_mem_kv, m_w_mem_o, m_norm_ffn, m_w_ffn_in, m_w_ffn_out, m_norm_final, v_norm_mix, v_w_in, v_b_forget, v_lam_re, v_lam_im, v_log_dt, v_b_re, v_b_im, v_c_re, v_c_im, v_d_skip, v_w_glu, v_w_fox_o, v_w_mix_out, v_norm_mem_q, v_norm_mem_kv, v_w_mem_q, v_w_mem_kv, v_w_mem_o, v_norm_ffn, v_w_ffn_in, v_w_ffn_out, v_norm_final):
    given = dict(locals())
    weights = {n: given[n] for n in _WEIGHTS}
    mom_m = {n: given["m_" + n] for n in _WEIGHTS}
    mom_v = {n: given["v_" + n] for n in _WEIGHTS}
    seq = x.shape[1]
    nc = seq // SSM_CHUNK
    d = D_MODEL
    xs, mems, tgt = x[0], mem[0], loss_target[0]

    def padcols(a, width):
        return jnp.pad(a, ((0, 0), (0, width - a.shape[1])))

    shards = [padcols(w_in[0].astype(BF16), SHARD_IN_PAD), w_glu[0].astype(BF16), w_fox_o[0].astype(BF16),
              w_mix_out[0].astype(BF16), w_mem_q[0].astype(BF16), w_mem_kv[0].astype(BF16),
              w_mem_o[0].astype(BF16), padcols(w_ffn_in[0].astype(BF16), SHARD_FFN_PAD), w_ffn_out[0].astype(BF16)]
    first = shards[:1]
    wsend, wrecv, first_thru, first_lands, wtoken = _send_start(
        "gather_w_in_start", first, _place_own("place_w_in_shard", first, stacked_src=False), scatter=False)
    rest = shards[1:]
    gsend, grecv, rest_thru, lands, gtoken = _send_start(
        "gather_rest_start", rest, _place_own("place_weight_shards", rest, stacked_src=False, after=wtoken),
        scatter=False)

    u = _rms_fwd("rms_mix", xs, norm_mix, after=gtoken)
    ssm_params = (lam_re[0], lam_im[0], log_dt[0], b_re[0], b_im[0], c_re[0], c_im[0])
    (m_c, bw_c, cm_c, a8, aseg), mats_vjp = jax.vjp(lambda *p: _ssm_mats(*p, nc), *ssm_params)
    m_b = _bd_expand("ssm_expand_m", _BD_M, m_c)
    bw_b = _bd_expand("ssm_expand_bw", _BD_BW, bw_c)
    cm_b = _bd_expand("ssm_expand_cm", _BD_CM, cm_c)
    win = _assemble_win("assemble_w_in", _send_wait(
        "gather_w_in_wait", wsend, wrecv, first_thru, first_lands, (u, m_b, bw_b, cm_b), scatter=False)[0])
    ussm = _mm("proj_ssm", u, win, "nn", seq, SSM_WIDTH, d, F32)
    qkv = _mm("proj_qkv", u, win, "nn", seq, 3 * FOX_WIDTH, d, BF16, tn=512, b_off=(0, SSM_WIDTH))
    gates = _mm("proj_gates", u, win, "nn", seq, 2 * d, d, BF16, tn=1024, b_off=(0, PROJ_GATE0))
    fproj = _mm("proj_forget", u, win, "nn", seq, LANE, d, F32, tn=LANE, b_off=(0, PROJ_F0))

    u8 = ussm.reshape(nc, SSM_CHUNK * SSM_WIDTH)
    d8 = jnp.tile(d_skip, (1, SSM_CHUNK))
    w4 = _ssm_w("ssm_w", u8, bw_b)
    sp4 = _ssm_scan("ssm_scan", w4, a8, aseg, reverse=False)
    y8 = _ssm_y("ssm_y", u8, sp4, m_b, cm_b)
    act = _ssm_post_fwd("ssm_act", y8, u8, d8).reshape(seq, SSM_WIDTH)

    bcol = jnp.pad(b_forget[0], (0, LANE - FOX_HEADS)).reshape(LANE, 1)
    cum_t = _fox_cum("fox_cum", fproj, bcol).reshape(FOX_HEADS // 2, 2, seq)
    att, lse = _fox_fwd("fox_fwd", qkv, cum_t)

    gathered = _send_wait("gather_rest_wait", gsend, grecv, rest_thru, lands, att, scatter=False)
    wglu = _unstack_cols("unstack_w_glu", gathered[0])
    wfoxo = _unstack_cols("unstack_w_fox_o", gathered[1])
    wmix = gathered[2].reshape(d, d)
    wmq = gathered[3].reshape(d, MEM_WIDTH)
    wmkv = gathered[4].reshape(d, 2 * MEM_WIDTH)
    wmo = _unstack_cols("unstack_w_mem_o", gathered[5])
    wffn_in = _assemble_wffn("assemble_w_ffn_in", gathered[6])
    wffn_out = gathered[7].reshape(FFN_HIDDEN, d)

    glu = _mm("glu", act, wglu, "nn", seq, 2 * d, SSM_WIDTH, BF16, tn=1024)
    out_b = _mm("fox_out", att, wfoxo, "nn", seq, d, FOX_WIDTH, BF16, tn=1024)

    mixin = _mix_fwd("mix", glu, gates, out_b)
    h1 = _mm("mix_out", mixin, wmix, "nn", seq, d, d, F32, tn=1024, add=xs)

    n1 = _rms_fwd("rms_mem_q", h1, norm_mem_q)
    q2 = _mm("mem_q", n1, wmq, "nn", seq, MEM_WIDTH, d, BF16)
    mn = _rms_fwd("rms_mem_kv", mems, norm_mem_kv)
    mlen = mems.shape[0]
    kv = _mm("mem_kv", mn, wmkv, "nn", mlen, 2 * MEM_WIDTH, d, BF16)
    o2 = _mem_fwd("mem_attn", q2, kv)
    h2 = _mm("mem_out", o2, wmo, "nn", seq, d, MEM_WIDTH, F32, tn=1024, add=h1)

    n2 = _rms_fwd("rms_ffn", h2, norm_ffn)
    f, g_act = _ffn_in_swiglu("ffn_in_swiglu", n2, wffn_in)
    loss_part, dh3, dg_final = _matmul_final_loss("ffn_out_final_loss", g_act, wffn_out, h2, tgt,
                                                  norm_final.reshape(1, d))

    df = _d_ffn_out_swiglu("d_ffn_out_swiglu", dh3, wffn_out, f)
    dwffn_out = _mm("d_ffn_out_w", g_act, dh3, "tn", FFN_HIDDEN, d, seq, BF16, tm=1408, tn=1024)
    dh2, dg_ffn = _matmul_rms_bwd("d_ffn_in_x_rms", df, wffn_in, 2 * FFN_HIDDEN, h2, norm_ffn, dh3, tm=1024, tk=1408)
    dwffn_in = _mm("d_ffn_in_w", n2, df, "tn", d, 2 * FFN_HIDDEN, seq, BF16, tn=1408)

    do2 = _mm("d_mem_out_x", dh2, wmo, "nt", seq, MEM_WIDTH, d, F32)
    dwmo = _restack_cols("restack_d_w_mem_o", _mm("d_mem_out_w", o2, dh2, "tn", MEM_WIDTH, d, seq, BF16, tn=1024))
    dq2, dkv = _mem_bwd("d_mem_attn", q2, kv, do2)
    dwmq = _mm("d_mem_q_w", n1, dq2, "tn", d, MEM_WIDTH, seq, BF16)
    dwmkv = _mm("d_mem_kv_w", mn, dkv, "tn", d, 2 * MEM_WIDTH, mlen, BF16, tn=1024)
    dmn = _mm("d_mem_kv_x", dkv, wmkv, "nt", mlen, d, 2 * MEM_WIDTH, F32)
    dg_memkv = _rms_gain_grad("d_rms_mem_kv", dmn, mems)

    early = [dwmq.reshape(N_DEV, d // N_DEV, MEM_WIDTH), dwmkv.reshape(N_DEV, d // N_DEV, 2 * MEM_WIDTH), dwmo,
             _disassemble_dwffn("split_d_w_ffn_in", dwffn_in), dwffn_out.reshape(N_DEV, FFN_HIDDEN // N_DEV, d)]
    ssend, srecv, early_thru, early_lands, stoken = _send_start(
        "scatter_early_start", early, _place_own("place_early_grads", early, stacked_src=True), scatter=True)
    dh1, dg_memq = _matmul_rms_bwd("d_mem_q_x_rms", dq2, wmq, MEM_WIDTH, h1, norm_mem_q, dh2, tm=1024, after=stoken)

    dmixin = _mm("d_mix_out_x", dh1, wmix, "nt", seq, d, d, F32, tn=1024)
    dwmix = _mm("d_mix_out_w", mixin, dh1, "tn", d, d, seq, BF16, tn=1024)
    dglu, dgates, dout_b = _mix_bwd("d_mix", dmixin, glu, gates, out_b)
    datt = _mm("d_fox_out_x", dout_b, wfoxo, "nt", seq, FOX_WIDTH, d, F32)
    dwfoxo = _restack_cols("restack_d_w_fox_o", _mm("d_fox_out_w", att, dout_b, "tn", FOX_WIDTH, d, seq, BF16, tn=1024))
    dact = _mm("d_glu_x", dglu, wglu, "nt", seq, SSM_WIDTH, 2 * d, F32, tk=2 * d)
    dwglu = _restack_cols("restack_d_w_glu", _mm("d_glu_w", act, dglu, "tn", SSM_WIDTH, 2 * d, seq, BF16, tn=2 * d))

    mid = [dwglu, dwfoxo, dwmix.reshape(N_DEV, d // N_DEV, d)]
    msend, mrecv, mid_thru, mid_lands, mtoken = _send_start(
        "scatter_mid_start", mid, _place_own("place_mid_grads", mid, stacked_src=True), scatter=True)

    dz8, dg_dskip = _ssm_post_bwd("d_ssm_act", dact.reshape(nc, SSM_CHUNK * SSM_WIDTH), y8, u8, d8, after=mtoken)
    ds4, dcm = _ssm_ds("d_ssm_y_state", dz8, sp4, cm_b)
    g4, da8 = _ssm_scan("d_ssm_scan", ds4, a8, aseg, reverse=True, sprev4=sp4)
    dx8, dm, dbw = _ssm_dx("d_ssm_x", dz8, g4, u8, m_b, bw_b, d8)
    dussm = dx8.reshape(seq, SSM_WIDTH)
    g_ssm = mats_vjp((_bd_reduce("ssm_reduce_dm", _BD_M, dm), _bd_reduce("ssm_reduce_dbw", _BD_BW, dbw),
                      _bd_reduce("ssm_reduce_dcm", _BD_CM, dcm), da8, jnp.zeros_like(aseg)))

    dq, dk, dv, dcs = _fox_bwd("d_fox", qkv, cum_t, att, datt, lse)
    dfproj, dbf = _fox_cum_bwd("d_fox_cum", dcs, fproj, bcol)
    dg_bforget = dbf[0:FOX_HEADS, 0].reshape(1, FOX_HEADS)

    dproj = _concat_cols("d_proj_concat", (dussm, dq, dk, dv, dgates, dfproj))
    dwin = _mm("d_proj_w", u, dproj, "tn", d, PROJ_WIDTH, seq, BF16, tn=1408)
    late = [_disassemble_dwin("split_d_w_in", dwin)]
    lsend, lrecv, late_thru, late_lands, ltoken = _send_start(
        "scatter_late_start", late, _place_own("place_late_grads", late, stacked_src=True), scatter=True)
    dx, dg_mix = _matmul_rms_bwd("d_proj_x_rms", dproj, win, PROJ_WIDTH, xs, norm_mix, dh1, tm=1024, tk=1408,
                                 after=ltoken)

    early_parts = _send_wait("scatter_early_wait", ssend, srecv, early_thru, early_lands, dx, scatter=True)
    mid_parts = _send_wait("scatter_mid_wait", msend, mrecv, mid_thru, mid_lands, dx, scatter=True)
    received = dict(zip(("w_glu", "w_fox_o", "w_mix_out"), mid_parts))
    received.update(zip(("w_mem_q", "w_mem_kv", "w_mem_o", "w_ffn_in", "w_ffn_out"), early_parts))

    small_grads = dict(zip(
        _SMALL, (dg_mix, dg_bforget, g_ssm[0][None], g_ssm[1][None], g_ssm[2][None], g_ssm[3][None], g_ssm[4][None],
                 g_ssm[5][None], g_ssm[6][None], dg_dskip, dg_memq, dg_memkv, dg_ffn, dg_final.reshape(d))))
    small_like = [weights[n] for n in _SMALL]
    small_all = _gather_all("gather_small_grads", [_pack([small_grads[n] for n in _SMALL])])[0]
    pk = [_pack([src[n] for n in _SMALL]) for src in (weights, mom_m, mom_v)]
    small_out = _adamw("adamw_small", small_all, pk[0], pk[1], pk[2], tr=small_all.shape[1])
    results = [dict(zip(_SMALL, _unpack(buf, small_like))) for buf in small_out]
    tiles = {"w_in": 128, "w_glu": 128, "w_fox_o": 128, "w_mix_out": 128, "w_mem_q": 128, "w_mem_kv": 128,
             "w_mem_o": 128, "w_ffn_in": 128, "w_ffn_out": 176}
    pads = {"w_in": SHARD_IN_PAD, "w_ffn_in": SHARD_FFN_PAD}
    outs = small_out
    for name in _SHARDED[1:] + _SHARDED[:1]:
        if name == "w_in":
            received[name] = _send_wait("scatter_late_wait", lsend, lrecv, late_thru, late_lands, outs[0],
                                        scatter=True)[0]
        parts = received[name]
        w2, m2, v2 = weights[name][0], mom_m[name][0], mom_v[name][0]
        cols = w2.shape[1]
        if name in pads:
            w2, m2, v2 = (padcols(t, pads[name]) for t in (w2, m2, v2))
        outs = _adamw("adamw_" + name, parts, w2, m2, v2, tr=tiles[name])
        for res, o in zip(results, outs):
            res[name] = o[:, :cols][None]

    loss = lax.psum(loss_part[0, 0], ("x", "y", "c"))
    out = [loss, dx[None]]
    for res in results:
        out.extend(res[n] for n in _WEIGHTS)
    return tuple(out)
```

```python
import math

import jax
import jax.numpy as jnp
import numpy as np
from jax import lax
from jax.experimental import pallas as pl
from jax.experimental.pallas import tpu as pltpu

F32 = jnp.float32
BF16 = jnp.bfloat16

N_DEV = 8
LANE = 128
VMEM_LIMIT = 56 * 1024 * 1024

D_MODEL = 1024
SSM_GROUP = 16
SSM_GROUPS = 32
SSM_WIDTH = 512
SSM_STATE = 64
SSM_CHUNK = 8
FOX_HEADS = 8
FOX_HEAD_DIM = 64
FOX_WIDTH = 512
MEM_HEADS = 4
MEM_HEAD_DIM = 128
MEM_WIDTH = 512
FFN_HIDDEN = 2816
RMS_EPS = 1e-6
IN_WIDTH = 4104
SHARD_IN = IN_WIDTH // N_DEV
SHARD_IN_PAD = 640
SHARD_FFN = 2 * FFN_HIDDEN // N_DEV
SHARD_FFN_PAD = 768
PROJ_GATE0 = 2048
PROJ_F0 = 4096
PROJ_WIDTH = 4224

ADAM_LR = 0.001
ADAM_B1 = 0.9
ADAM_B2 = 0.999
ADAM_EPS = 1e-08
ADAM_WD = 0.01
ADAM_STEP = 10


def _cparams(sem=None):
    return pltpu.CompilerParams(dimension_semantics=sem, vmem_limit_bytes=VMEM_LIMIT)


def _sigmoid(x):
    return 1.0 / (1.0 + jnp.exp(-x))


def _bdot(a, b, dims):
    return lax.dot_general(a.astype(BF16), b.astype(BF16), ((dims[0], dims[1]), ((), ())),
                           preferred_element_type=F32)


_DIMS = {"nn": ((1,), (0,)), "nt": ((1,), (1,)), "tn": ((0,), (0,))}


def _matmul(name, a, b, mode, m, n, k, *, out_dtype, tm, tn, tk, a_off=(0, 0), b_off=(0, 0), add=None):
    tm, tn, tk = min(tm, m), min(tn, n), min(tk, k)
    assert m % tm == 0 and n % tn == 0 and k % tk == 0, (name, m, n, k, tm, tn, tk)
    nk = k // tk
    grid = (m // tm, n // tn, nk)

    def blk(off, t):
        assert off % t == 0, (name, off, t)
        return off // t

    if mode in ("nn", "nt"):
        ar, ac = blk(a_off[0], tm), blk(a_off[1], tk)
        a_spec = pl.BlockSpec((tm, tk), lambda i, j, kk: (i + ar, kk + ac))
    else:
        ar, ac = blk(a_off[0], tk), blk(a_off[1], tm)
        a_spec = pl.BlockSpec((tk, tm), lambda i, j, kk: (kk + ar, i + ac))

    if mode in ("nn", "tn"):
        br, bc = blk(b_off[0], tk), blk(b_off[1], tn)
        b_spec = pl.BlockSpec((tk, tn), lambda i, j, kk: (kk + br, j + bc))
    else:
        br, bc = blk(b_off[0], tn), blk(b_off[1], tk)
        b_spec = pl.BlockSpec((tn, tk), lambda i, j, kk: (j + br, kk + bc))
    o_spec = pl.BlockSpec((tm, tn), lambda i, j, kk: (i, j))
    out_shape = jax.ShapeDtypeStruct((m, n), out_dtype)

    in_specs = [a_spec, b_spec]
    operands = [a, b]
    if add is not None:
        in_specs.append(pl.BlockSpec((tm, tn), lambda i, j, kk: (i, j)))
        operands.append(add)
    dims = _DIMS[mode]
    has_add = add is not None

    def body(*refs):
        a_ref, b_ref = refs[0], refs[1]
        add_ref = refs[2] if has_add else None
        o_ref = refs[3] if has_add else refs[2]
        acc_ref = refs[-1] if nk > 1 else None
        prod = _bdot(a_ref[...], b_ref[...], dims)

        def finish(total):
            if has_add:
                total = total + add_ref[...].astype(F32)
            o_ref[...] = total.astype(o_ref.dtype)

        if nk == 1:
            finish(prod)
        else:
            kk = pl.program_id(2)

            @pl.when(kk == 0)
            def _():
                acc_ref[...] = prod

            @pl.when(jnp.logical_and(kk > 0, kk < nk - 1))
            def _():
                acc_ref[...] += prod

            @pl.when(kk == nk - 1)
            def _():
                finish(acc_ref[...] + prod)

    scratch = [pltpu.VMEM((tm, tn), F32)] if nk > 1 else []
    return pl.pallas_call(
        body, name=name, grid=grid, in_specs=in_specs, out_specs=o_spec, out_shape=out_shape,
        scratch_shapes=scratch,
        compiler_params=_cparams(("parallel", "parallel", "arbitrary")),
    )(*operands)


def _rms_fwd(name, x, gain, *, tr=512, after=None):
    r, d = x.shape
    tr = min(tr, r)

    def body(x_ref, g_ref, *rest):
        o_ref = rest[-1]
        xv = x_ref[...]
        rstd = lax.rsqrt(jnp.mean(xv * xv, axis=-1, keepdims=True) + RMS_EPS)
        o_ref[...] = (xv * rstd * g_ref[...]).astype(o_ref.dtype)

    in_specs = [pl.BlockSpec((tr, d), lambda i: (i, 0)), pl.BlockSpec((1, d), lambda i: (0, 0))]
    ops = [x, gain]
    if after is not None:
        in_specs.append(pl.BlockSpec(after.shape, lambda i: (0, 0)))
        ops.append(after)
    return pl.pallas_call(
        body, name=name, grid=(r // tr,), in_specs=in_specs,
        out_specs=pl.BlockSpec((tr, d), lambda i: (i, 0)),
        out_shape=jax.ShapeDtypeStruct((r, d), BF16),
        compiler_params=_cparams(("parallel",)),
    )(*ops)


def _rms_gain_grad(name, dy, x, *, tr=512):
    r, d = x.shape
    tr = min(tr, r)
    n = r // tr

    def body(dy_ref, x_ref, dg_ref, acc_ref):
        i = pl.program_id(0)
        xv = x_ref[...]
        xh = xv * lax.rsqrt(jnp.mean(xv * xv, axis=-1, keepdims=True) + RMS_EPS)
        part = (dy_ref[...].astype(F32) * xh).reshape(tr // 8, 8, d).sum(axis=0)

        @pl.when(i == 0)
        def _():
            acc_ref[...] = part

        @pl.when(i > 0)
        def _():
            acc_ref[...] += part

        @pl.when(i == n - 1)
        def _():
            dg_ref[...] = jnp.sum(acc_ref[...], axis=0, keepdims=True)

    row = pl.BlockSpec((tr, d), lambda i: (i, 0))
    return pl.pallas_call(
        body, name=name, grid=(n,), in_specs=[row, row],
        out_specs=pl.BlockSpec((1, d), lambda i: (0, 0)),
        out_shape=jax.ShapeDtypeStruct((1, d), F32),
        scratch_shapes=[pltpu.VMEM((8, d), F32)],
        compiler_params=_cparams(("arbitrary",)),
    )(dy, x)


def _matmul_rms_bwd(name, a, b, k, x, gain, res, *, tm=512, tk=1024, after=None):
    m, d = x.shape
    tm, tk = min(tm, m), min(tk, k)
    assert m % tm == 0 and k % tk == 0, (name, m, k, tm, tk)
    ni, nk = m // tm, k // tk

    def body(a_ref, b_ref, x_ref, g_ref, res_ref, *rest):
        dx_ref, dg_ref, acc_ref, accg_ref = rest[-4:]
        i, kk = pl.program_id(0), pl.program_id(1)
        prod = _bdot(a_ref[...], b_ref[...], _DIMS["nt"])

        @pl.when(kk == 0)
        def _():
            acc_ref[...] = prod

        @pl.when(kk > 0)
        def _():
            acc_ref[...] += prod

        @pl.when(kk == nk - 1)
        def _():
            dyv = acc_ref[...]
            xv = x_ref[...]
            rstd = lax.rsqrt(jnp.mean(xv * xv, axis=-1, keepdims=True) + RMS_EPS)
            xh = xv * rstd
            dxh = dyv * g_ref[...]
            dx_ref[...] = rstd * (dxh - xh * jnp.mean(dxh * xh, axis=-1, keepdims=True)) + res_ref[...]
            part = (dyv * xh).reshape(tm // 8, 8, d).sum(axis=0)

            @pl.when(i == 0)
            def _():
                accg_ref[...] = part

            @pl.when(i > 0)
            def _():
                accg_ref[...] += part

            @pl.when(i == ni - 1)
            def _():
                dg_ref[...] = jnp.sum(accg_ref[...], axis=0, keepdims=True)

    row = pl.BlockSpec((tm, d), lambda i, kk: (i, 0))
    one = pl.BlockSpec((1, d), lambda i, kk: (0, 0))
    in_specs = [pl.BlockSpec((tm, tk), lambda i, kk: (i, kk)), pl.BlockSpec((d, tk), lambda i, kk: (0, kk)), row, one, row]
    ops = [a, b, x, gain, res]
    if after is not None:
        in_specs.append(pl.BlockSpec(after.shape, lambda i, kk: (0, 0)))
        ops.append(after)
    return pl.pallas_call(
        body, name=name, grid=(ni, nk), in_specs=in_specs, out_specs=(row, one),
        out_shape=(jax.ShapeDtypeStruct((m, d), F32), jax.ShapeDtypeStruct((1, d), F32)),
        scratch_shapes=[pltpu.VMEM((tm, d), F32), pltpu.VMEM((8, d), F32)],
        compiler_params=_cparams(("arbitrary", "arbitrary")),
    )(*ops)


def _matmul_final_loss(name, a, b, res, target, gain, *, tr=512):
    r, d = res.shape
    k = a.shape[1]
    tr = min(tr, r)
    n = r // tr

    def body(a_ref, b_ref, res_ref, t_ref, g_ref, loss_ref, dh_ref, dg_ref, accl_ref, accg_ref):
        i = pl.program_id(0)
        xv = _bdot(a_ref[...], b_ref[...], _DIMS["nn"]) + res_ref[...]
        rstd = lax.rsqrt(jnp.mean(xv * xv, axis=-1, keepdims=True) + RMS_EPS)
        xh = xv * rstd
        e = xh * g_ref[...] - t_ref[...]
        dyv = e * (1.0 / d)
        dxh = dyv * g_ref[...]
        dh_ref[...] = rstd * (dxh - xh * jnp.mean(dxh * xh, axis=-1, keepdims=True))
        lpart = (e * e).reshape(tr // 8, 8, d).sum(axis=0)
        gpart = (dyv * xh).reshape(tr // 8, 8, d).sum(axis=0)

        @pl.when(i == 0)
        def _():
            accl_ref[...] = lpart
            accg_ref[...] = gpart

        @pl.when(i > 0)
        def _():
            accl_ref[...] += lpart
            accg_ref[...] += gpart

        @pl.when(i == n - 1)
        def _():
            tot = jnp.sum(jnp.sum(accl_ref[...], axis=0, keepdims=True), axis=1, keepdims=True)
            loss_ref[...] = jnp.broadcast_to(tot * (0.5 / d), (1, LANE))
            dg_ref[...] = jnp.sum(accg_ref[...], axis=0, keepdims=True)

    row = pl.BlockSpec((tr, d), lambda i: (i, 0))
    one = pl.BlockSpec((1, d), lambda i: (0, 0))
    return pl.pallas_call(
        body, name=name, grid=(n,),
        in_specs=[pl.BlockSpec((tr, k), lambda i: (i, 0)), pl.BlockSpec((k, d), lambda i: (0, 0)), row, row, one],
        out_specs=(pl.BlockSpec((1, LANE), lambda i: (0, 0)), row, one),
        out_shape=(jax.ShapeDtypeStruct((1, LANE), F32), jax.ShapeDtypeStruct((r, d), F32),
                   jax.ShapeDtypeStruct((1, d), F32)),
        scratch_shapes=[pltpu.VMEM((8, d), F32), pltpu.VMEM((8, d), F32)],
        compiler_params=_cparams(("arbitrary",)),
    )(a, b, res, target, gain)


_GELU_C = math.sqrt(2.0 / math.pi)


def _gelu_parts(z):
    inner = _GELU_C * (z + 0.044715 * z * z * z)
    t = jnp.tanh(inner)
    val = 0.5 * z * (1.0 + t)
    dinner = _GELU_C * (1.0 + 3.0 * 0.044715 * z * z)
    grad = 0.5 * (1.0 + t) + 0.5 * z * (1.0 - t * t) * dinner
    return val, grad


def _ssm_post_fwd(name, y8, u8, d8, *, tr=256):
    r, c = y8.shape
    tr = min(tr, r)

    def body(y_ref, u_ref, d_ref, o_ref):
        z = y_ref[...] + d_ref[...] * u_ref[...]
        o_ref[...] = _gelu_parts(z)[0].astype(o_ref.dtype)

    row = pl.BlockSpec((tr, c), lambda i: (i, 0))
    return pl.pallas_call(
        body, name=name, grid=(r // tr,), in_specs=[row, row, pl.BlockSpec((1, c), lambda i: (0, 0))],
        out_specs=row, out_shape=jax.ShapeDtypeStruct((r, c), BF16),
        compiler_params=_cparams(("parallel",)),
    )(y8, u8, d8)


def _ssm_post_bwd(name, dact8, y8, u8, d8, *, tr=256, after=None):
    r, c = y8.shape
    tr = min(tr, r)
    n = r // tr

    def body(*refs):
        da_ref, y_ref, u_ref, d_ref = refs[:4]
        dz_ref, dd_ref, acc_ref = refs[-3:]
        i = pl.program_id(0)
        uv = u_ref[...]
        z = y_ref[...] + d_ref[...] * uv
        dz = da_ref[...].astype(F32) * _gelu_parts(z)[1]
        dz_ref[...] = dz
        part = (dz * uv).reshape(tr // 8, 8, c).sum(axis=0)

        @pl.when(i == 0)
        def _():
            acc_ref[...] = part

        @pl.when(i > 0)
        def _():
            acc_ref[...] += part

        @pl.when(i == n - 1)
        def _():
            tot = jnp.sum(acc_ref[...], axis=0, keepdims=True)
            out = tot[:, 0:SSM_WIDTH]
            for j in range(1, c // SSM_WIDTH):
                out = out + tot[:, j * SSM_WIDTH:(j + 1) * SSM_WIDTH]
            dd_ref[...] = out

    row = pl.BlockSpec((tr, c), lambda i: (i, 0))
    in_specs = [row, row, row, pl.BlockSpec((1, c), lambda i: (0, 0))]
    ops = [dact8, y8, u8, d8]
    if after is not None:
        in_specs.append(pl.BlockSpec(memory_space=pl.ANY))
        ops.append(after)
    return pl.pallas_call(
        body, name=name, grid=(n,), in_specs=in_specs,
        out_specs=(row, pl.BlockSpec((1, SSM_WIDTH), lambda i: (0, 0))),
        out_shape=(jax.ShapeDtypeStruct((r, c), F32), jax.ShapeDtypeStruct((1, SSM_WIDTH), F32)),
        scratch_shapes=[pltpu.VMEM((8, c), F32)],
        compiler_params=_cparams(("arbitrary",)),
    )(*ops)


def _mix_fwd(name, glu, gates, out_b, *, tr=256):
    r = glu.shape[0]
    d = D_MODEL
    tr = min(tr, r)

    def body(glu_ref, gate_ref, ob_ref, o_ref):
        out_a = glu_ref[:, 0:d].astype(F32) * _sigmoid(glu_ref[:, d:2 * d].astype(F32))
        mix = (_sigmoid(gate_ref[:, 0:d].astype(F32)) * out_a
               + _sigmoid(gate_ref[:, d:2 * d].astype(F32)) * ob_ref[...].astype(F32))
        o_ref[...] = mix.astype(o_ref.dtype)

    wide = pl.BlockSpec((tr, 2 * d), lambda i: (i, 0))
    row = pl.BlockSpec((tr, d), lambda i: (i, 0))
    return pl.pallas_call(
        body, name=name, grid=(r // tr,), in_specs=[wide, wide, row], out_specs=row,
        out_shape=jax.ShapeDtypeStruct((r, d), BF16), compiler_params=_cparams(("parallel",)),
    )(glu, gates, out_b)


def _mix_bwd(name, dmix, glu, gates, out_b, *, tr=256):
    r = glu.shape[0]
    d = D_MODEL
    tr = min(tr, r)

    def body(dm_ref, glu_ref, gate_ref, ob_ref, dglu_ref, dgate_ref, dob_ref):
        dm = dm_ref[...]
        glu_a = glu_ref[:, 0:d].astype(F32)
        sb = _sigmoid(glu_ref[:, d:2 * d].astype(F32))
        ga = _sigmoid(gate_ref[:, 0:d].astype(F32))
        gb = _sigmoid(gate_ref[:, d:2 * d].astype(F32))
        out_a = glu_a * sb
        dout_a = dm * ga
        dglu_ref[:, 0:d] = (dout_a * sb).astype(dglu_ref.dtype)
        dglu_ref[:, d:2 * d] = (dout_a * glu_a * sb * (1.0 - sb)).astype(dglu_ref.dtype)
        dgate_ref[:, 0:d] = (dm * out_a * ga * (1.0 - ga)).astype(dgate_ref.dtype)
        dgate_ref[:, d:2 * d] = (dm * ob_ref[...].astype(F32) * gb * (1.0 - gb)).astype(dgate_ref.dtype)
        dob_ref[...] = (dm * gb).astype(dob_ref.dtype)

    wide = pl.BlockSpec((tr, 2 * d), lambda i: (i, 0))
    row = pl.BlockSpec((tr, d), lambda i: (i, 0))
    return pl.pallas_call(
        body, name=name, grid=(r // tr,), in_specs=[row, wide, wide, row], out_specs=(wide, wide, row),
        out_shape=(jax.ShapeDtypeStruct((r, 2 * d), BF16), jax.ShapeDtypeStruct((r, 2 * d), BF16),
                   jax.ShapeDtypeStruct((r, d), BF16)),
        compiler_params=_cparams(("parallel",)),
    )(dmix, glu, gates, out_b)


def _ssm_mats(lam_re, lam_im, log_dt, b_re, b_im, c_re, c_im, nc):
    hp = lax.Precision.HIGHEST
    t = SSM_CHUNK
    nq = SSM_GROUPS // 8
    lam = lax.complex(lam_re, lam_im)
    z = lam * jnp.exp(log_dt)[:, None]
    ks = jnp.arange(t + 1, dtype=F32)
    apow = jnp.exp(ks[:, None, None] * z[None])
    bbar = ((apow[1] - 1.0) / lam)[..., None] * lax.complex(b_re, b_im)
    c = lax.complex(c_re, c_im)

    ca = c[None] * apow[:, :, None, :]
    kmat = jnp.einsum("kgnp,gpm->kgnm", ca, bbar, precision=hp).real
    ii = np.arange(t)
    lag = ii[None, :] - ii[:, None]
    kt = kmat[np.clip(lag, 0, t)] * jnp.asarray(lag >= 0, F32)[:, :, None, None, None]
    kt = kt.reshape(t, t, nq, 8, SSM_GROUP, SSM_GROUP)
    m_c = kt.transpose(2, 0, 3, 5, 1, 4).reshape(nq, 1024, LANE)

    arev = jnp.exp((float(t - 1) - ks[:t])[:, None, None] * z[None])
    w = arev[:, :, :, None] * bbar[None]
    wr = jnp.stack([w.real, w.imag]).reshape(2, t, nq, 8, SSM_STATE, SSM_GROUP)
    bw_c = wr.transpose(2, 1, 3, 5, 0, 4).reshape(nq, 1024, LANE)

    ca1 = ca[1:]
    cr = jnp.stack([ca1.real, -ca1.imag]).reshape(2, t, nq, 8, SSM_GROUP, SSM_STATE)
    cm_c = cr.transpose(2, 0, 3, 5, 1, 4).reshape(nq, 1024, LANE)

    def tiles(v):
        vq = jnp.concatenate([v.real.reshape(nq, 512), v.imag.reshape(nq, 512)], axis=1)
        return jnp.broadcast_to(vq.reshape(nq, 8, 1, LANE), (nq, 8, 8, LANE))

    return m_c, bw_c, cm_c, tiles(apow[t]), tiles(jnp.exp(float(nc) * z))


_BD_M = (LANE, SSM_GROUP)
_BD_BW = (LANE, SSM_STATE)
_BD_CM = (512, SSM_GROUP)


def _bd_perm(cn):
    rr = lax.broadcasted_iota(jnp.int32, (1024, 1024), 0)
    cc = lax.broadcasted_iota(jnp.int32, (1024, 1024), 1)
    sh = cn.bit_length() - 1
    src = ((rr >> 7) << sh) + (((rr & (LANE - 1)) >> sh) << (3 + sh)) + (rr & (cn - 1))
    return jnp.where(src == cc, 1.0, 0.0).astype(BF16)


def _bd_rowgroup(span):
    r = lax.broadcasted_iota(jnp.int32, (1024, LANE), 0)
    return (r & (span - 1)) >> ((span // 8).bit_length() - 1)


def _bd_expand(name, kind, compact):
    span, cn = kind
    nq = compact.shape[0]

    def body(c_ref, o_ref, perm_scr):
        @pl.when(pl.program_id(0) == 0)
        def _():
            perm_scr[...] = _bd_perm(cn)

        x = c_ref[...]
        grp = _bd_rowgroup(span)
        xcat = jnp.concatenate([jnp.where(grp == h, x, 0.0) for h in range(8)], axis=1)
        o_ref[...] = _bdot(xcat, perm_scr[...], _DIMS["nn"]).astype(o_ref.dtype)

    return pl.pallas_call(
        body, name=name, grid=(nq,), in_specs=[pl.BlockSpec((None, 1024, LANE), lambda q: (q, 0, 0))],
        out_specs=pl.BlockSpec((None, 1024, 1024), lambda q: (q, 0, 0)),
        out_shape=jax.ShapeDtypeStruct((nq, 1024, 1024), BF16),
        scratch_shapes=[pltpu.VMEM((1024, 1024), BF16)],
        compiler_params=_cparams(("arbitrary",)),
    )(compact)


def _bd_reduce(name, kind, dbig):
    span, cn = kind
    nq = dbig.shape[0]

    def body(g_ref, o_ref, perm_scr):
        @pl.when(pl.program_id(0) == 0)
        def _():
            perm_scr[...] = _bd_perm(cn)

        back = _bdot(g_ref[...], perm_scr[...], _DIMS["nt"])
        grp = _bd_rowgroup(span)
        out = jnp.zeros((1024, LANE), F32)
        for h in range(8):
            out = jnp.where(grp == h, back[:, h * LANE:(h + 1) * LANE], out)
        o_ref[...] = out

    return pl.pallas_call(
        body, name=name, grid=(nq,), in_specs=[pl.BlockSpec((None, 1024, 1024), lambda q: (q, 0, 0))],
        out_specs=pl.BlockSpec((None, 1024, LANE), lambda q: (q, 0, 0)),
        out_shape=jax.ShapeDtypeStruct((nq, 1024, LANE), F32),
        scratch_shapes=[pltpu.VMEM((1024, 1024), BF16)],
        compiler_params=_cparams(("arbitrary",)),
    )(dbig)


def _x_tile_specs(nc, nq):
    return [pl.BlockSpec((nc, LANE), lambda q, t, i=i: (0, i * nq + q)) for i in range(SSM_CHUNK)]


def _cat_tiles(refs):
    return jnp.concatenate([r[...] for r in refs], axis=1)


def _ssm_w(name, x8, bw):
    nc = x8.shape[0]
    nq = bw.shape[0]

    def body(*refs):
        xq = _cat_tiles(refs[:8])
        refs[9][...] = _bdot(xq, refs[8][...], _DIMS["nn"])

    return pl.pallas_call(
        body, name=name, grid=(nq, 8),
        in_specs=_x_tile_specs(nc, nq) + [pl.BlockSpec((None, 1024, LANE), lambda q, t: (q, 0, t))],
        out_specs=pl.BlockSpec((None, None, nc, LANE), lambda q, t: (q, t, 0, 0)),
        out_shape=jax.ShapeDtypeStruct((nq, 8, nc, LANE), F32),
        compiler_params=_cparams(("parallel", "arbitrary")),
    )(*([x8] * 8), bw)


def _ssm_scan(name, w4, a_t, aseg_t, *, reverse, sprev4=None):
    nq, _, nc, _ = w4.shape
    ns = nc // 8
    with_da = sprev4 is not None

    def body(*refs):
        w_ref, a_ref, aseg_ref = refs[:3]
        s_ref = refs[3] if with_da else None
        o_ref = refs[4] if with_da else refs[3]
        da_ref = refs[5] if with_da else None
        sgn = -1.0 if reverse else 1.0
        ar = [a_ref[j] for j in range(4)]
        ai = [sgn * a_ref[j + 4] for j in range(4)]
        gr = [aseg_ref[j] for j in range(4)]
        gi = [sgn * aseg_ref[j + 4] for j in range(4)]
        zero = tuple(jnp.zeros((8, LANE), F32) for _ in range(8))

        def rows(tt):
            return pl.ds((ns - 1 - tt) if reverse else tt, 8, stride=ns)

        def step(carry, w):
            new_r = [ar[j] * carry[j] - ai[j] * carry[j + 4] + w[j] for j in range(4)]
            new_i = [ar[j] * carry[j + 4] + ai[j] * carry[j] + w[j + 4] for j in range(4)]
            return tuple(new_r + new_i)

        def pass1(tt, carry):
            return step(carry, [w_ref[j, rows(tt), :] for j in range(8)])

        ends = lax.fori_loop(0, ns, pass1, zero)
        sub = lax.broadcasted_iota(jnp.int32, (8, LANE), 0)
        init = list(zero)
        order = range(7, 0, -1) if reverse else range(0, 7)
        for s in order:
            nxt = s - 1 if reverse else s + 1
            cand_r = [gr[j] * init[j] - gi[j] * init[j + 4] + ends[j] for j in range(4)]
            cand_i = [gr[j] * init[j + 4] + gi[j] * init[j] + ends[j + 4] for j in range(4)]
            cand = cand_r + cand_i
            shift = 7 if reverse else 1
            init = [jnp.where(sub == nxt, pltpu.roll(cand[j], shift, axis=0), init[j]) for j in range(8)]

        def pass2(tt, state):
            carry, acc = state
            r = rows(tt)
            for j in range(8):
                o_ref[j, r, :] = carry[j]
            if with_da:
                sp = [s_ref[j, r, :] for j in range(8)]
                acc_r = [acc[j] + carry[j] * sp[j] + carry[j + 4] * sp[j + 4] for j in range(4)]
                acc_i = [acc[j + 4] + carry[j + 4] * sp[j] - carry[j] * sp[j + 4] for j in range(4)]
                acc = tuple(acc_r + acc_i)
            return step(carry, [w_ref[j, r, :] for j in range(8)]), acc

        _, acc = lax.fori_loop(0, ns, pass2, (tuple(init), zero))
        if with_da:
            for j in range(8):
                da_ref[j] = acc[j]

    big = pl.BlockSpec((None, 8, nc, LANE), lambda q: (q, 0, 0, 0))
    small = pl.BlockSpec((None, 8, 8, LANE), lambda q: (q, 0, 0, 0))
    in_specs = [big, small, small] + ([big] if with_da else [])
    ops = [w4, a_t, aseg_t] + ([sprev4] if with_da else [])
    out_specs = (big, small) if with_da else big
    big_s = jax.ShapeDtypeStruct((nq, 8, nc, LANE), F32)
    out_shape = (big_s, jax.ShapeDtypeStruct((nq, 8, 8, LANE), F32)) if with_da else big_s
    return pl.pallas_call(
        body, name=name, grid=(nq,), in_specs=in_specs, out_specs=out_specs, out_shape=out_shape,
        compiler_params=_cparams(("parallel",)),
    )(*ops)


def _ssm_y(name, x8, sprev4, m_mat, cm_mat):
    nc = x8.shape[0]
    nq = m_mat.shape[0]

    def body(*refs):
        xq = _cat_tiles(refs[:8])
        s_ref, m_ref, cm_ref, o_ref = refs[8:12]
        sq = jnp.concatenate([s_ref[t] for t in range(8)], axis=1)
        o_ref[...] = _bdot(xq, m_ref[...], _DIMS["nn"]) + _bdot(sq, cm_ref[...], _DIMS["nn"])

    col = pl.BlockSpec((None, 1024, LANE), lambda q, j: (q, 0, j))
    return pl.pallas_call(
        body, name=name, grid=(nq, 8),
        in_specs=_x_tile_specs(nc, nq) + [pl.BlockSpec((None, 8, nc, LANE), lambda q, j: (q, 0, 0, 0)), col, col],
        out_specs=pl.BlockSpec((nc, LANE), lambda q, j: (0, j * nq + q)),
        out_shape=jax.ShapeDtypeStruct((nc, 8 * SSM_WIDTH), F32),
        compiler_params=_cparams(("parallel", "arbitrary")),
    )(*([x8] * 8), sprev4, m_mat, cm_mat)


def _ssm_ds(name, dz8, sprev4, cm_mat):
    nc = dz8.shape[0]
    nq = cm_mat.shape[0]

    def body(*refs):
        dyq = _cat_tiles(refs[:8]).astype(BF16)
        s_ref, cm_ref, ds_ref, dcm_ref = refs[8:12]
        ds_ref[...] = _bdot(dyq, cm_ref[...], _DIMS["nt"])
        dcm_ref[...] = _bdot(s_ref[...], dyq, _DIMS["tn"])

    tile = pl.BlockSpec((None, None, nc, LANE), lambda q, t: (q, t, 0, 0))
    rowblk = pl.BlockSpec((None, LANE, 1024), lambda q, t: (q, t, 0))
    return pl.pallas_call(
        body, name=name, grid=(nq, 8),
        in_specs=_x_tile_specs(nc, nq) + [tile, rowblk],
        out_specs=(tile, rowblk),
        out_shape=(jax.ShapeDtypeStruct((nq, 8, nc, LANE), F32), jax.ShapeDtypeStruct((nq, 1024, 1024), F32)),
        compiler_params=_cparams(("parallel", "arbitrary")),
    )(*([dz8] * 8), sprev4, cm_mat)


def _ssm_dx(name, dz8, g4, x8, m_mat, bw_mat, d8):
    nc = dz8.shape[0]
    nq = m_mat.shape[0]

    def body(*refs):
        dyq = _cat_tiles(refs[:8]).astype(BF16)
        g_ref, x_ref, m_ref, bw_ref, d_ref, dzi_ref, dx_ref, dm_ref, dbw_ref = refs[8:17]
        gq = jnp.concatenate([g_ref[t] for t in range(8)], axis=1).astype(BF16)
        dx = _bdot(dyq, m_ref[...], _DIMS["nt"]) + _bdot(gq, bw_ref[...], _DIMS["nt"])
        dx_ref[...] = (dx + d_ref[...] * dzi_ref[...]).astype(dx_ref.dtype)
        xi = x_ref[...]
        dm_ref[...] = _bdot(xi, dyq, _DIMS["tn"])
        dbw_ref[...] = _bdot(xi, gq, _DIMS["tn"])

    xtile = pl.BlockSpec((nc, LANE), lambda q, i: (0, i * nq + q))
    rowblk = pl.BlockSpec((None, LANE, 1024), lambda q, i: (q, i, 0))
    return pl.pallas_call(
        body, name=name, grid=(nq, 8),
        in_specs=_x_tile_specs(nc, nq) + [pl.BlockSpec((None, 8, nc, LANE), lambda q, i: (q, 0, 0, 0)), xtile, rowblk, rowblk,
                                          pl.BlockSpec((1, LANE), lambda q, i: (0, q)), xtile],
        out_specs=(xtile, rowblk, rowblk),
        out_shape=(jax.ShapeDtypeStruct((nc, 8 * SSM_WIDTH), BF16), jax.ShapeDtypeStruct((nq, 1024, 1024), F32),
                   jax.ShapeDtypeStruct((nq, 1024, 1024), F32)),
        compiler_params=_cparams(("parallel", "arbitrary")),
    )(*([dz8] * 8), g4, x8, m_mat, bw_mat, d8, dz8)


CUM_BLK = 256


def _split3(x):
    hi = x.astype(BF16)
    r1 = x - hi.astype(F32)
    mid = r1.astype(BF16)
    lo = (r1 - mid.astype(F32)).astype(BF16)
    return hi, mid, lo


def _tri_dot(x, tri):
    hi, mid, lo = _split3(x)
    d = _DIMS["nn"]
    return _bdot(hi, tri, d) + _bdot(mid, tri, d) + _bdot(lo, tri, d)


def _tri(n, lower):
    r = lax.broadcasted_iota(jnp.int32, (n, n), 0)
    c = lax.broadcasted_iota(jnp.int32, (n, n), 1)
    return jnp.where((r >= c) if lower else (r <= c), 1.0, 0.0).astype(BF16)


def _fox_cum(name, fproj, bcol):
    seq = fproj.shape[0]
    blk = min(CUM_BLK, seq)

    def body(f_ref, b_ref, o_ref, carry_ref):
        i = pl.program_id(0)

        @pl.when(i == 0)
        def _():
            carry_ref[...] = jnp.zeros_like(carry_ref)

        z = f_ref[...].T + b_ref[...]
        logf = jnp.minimum(z, 0.0) - jnp.log(1.0 + jnp.exp(-jnp.abs(z)))
        carry = carry_ref[...]
        cum = _tri_dot(logf, _tri(blk, lower=False)) + jnp.tile(carry, (1, blk // LANE))
        o_ref[...] = cum[0:8, :]
        carry_ref[...] = carry + jnp.sum(logf, axis=1, keepdims=True)

    return pl.pallas_call(
        body, name=name, grid=(seq // blk,),
        in_specs=[pl.BlockSpec((blk, LANE), lambda i: (i, 0)), pl.BlockSpec((LANE, 1), lambda i: (0, 0))],
        out_specs=pl.BlockSpec((8, blk), lambda i: (0, i)),
        out_shape=jax.ShapeDtypeStruct((8, seq), F32),
        scratch_shapes=[pltpu.VMEM((LANE, LANE), F32)],
        compiler_params=_cparams(("arbitrary",)),
    )(fproj, bcol)


def _fox_cum_bwd(name, dcs, fproj, bcol):
    seq = fproj.shape[0]
    blk = min(CUM_BLK, seq)
    n = seq // blk

    def body(dc_ref, f_ref, b_ref, df_ref, db_ref, carry_ref, acc_ref):
        i = pl.program_id(0)

        @pl.when(i == 0)
        def _():
            carry_ref[...] = jnp.zeros_like(carry_ref)
            acc_ref[...] = jnp.zeros_like(acc_ref)

        r = lax.broadcasted_iota(jnp.int32, (LANE, FOX_WIDTH), 0)
        c = lax.broadcasted_iota(jnp.int32, (LANE, FOX_WIDTH), 1)
        want = (r >> 1) * LANE + jnp.where((r & 1) == 0, FOX_HEAD_DIM, 0)
        sel = jnp.where(jnp.logical_and(r < FOX_HEADS, c == want), 1.0, 0.0).astype(BF16)
        hi, mid, lo = _split3(dc_ref[...])
        nt = _DIMS["nt"]
        dc = _bdot(sel, hi, nt) + _bdot(sel, mid, nt) + _bdot(sel, lo, nt)
        carry = carry_ref[...]
        dlogf = _tri_dot(dc, _tri(blk, lower=True)) + jnp.tile(carry, (1, blk // LANE))
        carry_ref[...] = carry + jnp.sum(dc, axis=1, keepdims=True)
        z = f_ref[...].T + b_ref[...]
        dft = dlogf / (1.0 + jnp.exp(z))
        df_ref[...] = dft.T.astype(df_ref.dtype)
        acc_ref[...] += jnp.sum(dft, axis=1, keepdims=True)

        @pl.when(i == n - 1)
        def _():
            db_ref[...] = acc_ref[...]

    return pl.pallas_call(
        body, name=name, grid=(n,),
        in_specs=[pl.BlockSpec((blk, FOX_WIDTH), lambda i: (n - 1 - i, 0)), pl.BlockSpec((blk, LANE), lambda i: (n - 1 - i, 0)),
                  pl.BlockSpec((LANE, 1), lambda i: (0, 0))],
        out_specs=(pl.BlockSpec((blk, LANE), lambda i: (n - 1 - i, 0)), pl.BlockSpec((LANE, LANE), lambda i: (0, 0))),
        out_shape=(jax.ShapeDtypeStruct((seq, LANE), BF16), jax.ShapeDtypeStruct((LANE, LANE), F32)),
        scratch_shapes=[pltpu.VMEM((LANE, LANE), F32), pltpu.VMEM((LANE, LANE), F32)],
        compiler_params=_cparams(("arbitrary",)),
    )(dcs, fproj, bcol)


FOX_BLK = 512
FOX_SCALE = FOX_HEAD_DIM ** -0.5


def _fox_head_mask(shape, hh):
    lane = lax.broadcasted_iota(jnp.int32, shape, 1)
    return (lane < FOX_HEAD_DIM) if hh == 0 else (lane >= FOX_HEAD_DIM)


def _fox_bias(cum_ref, hh, q0, k0, blk):
    c0 = jnp.max(cum_ref[hh:hh + 1, pl.ds(q0, LANE)], axis=1, keepdims=True)
    return c0 - cum_ref[hh:hh + 1, pl.ds(k0, blk)]


def _fox_fwd(name, qkv, cum_t):
    seq = qkv.shape[0]
    blk = min(FOX_BLK, seq)
    nb = seq // blk
    npair = FOX_HEADS // 2

    def body(q_ref, k_ref, v_ref, cum_ref, o_ref, lse_ref):
        iq = pl.program_id(1)
        q0 = pl.multiple_of(iq * blk, blk)
        qv = q_ref[...]
        row = lax.broadcasted_iota(jnp.int32, (blk, blk), 0)
        col = lax.broadcasted_iota(jnp.int32, (blk, blk), 1)
        qhs = [jnp.where(_fox_head_mask(qv.shape, hh), qv, jnp.zeros_like(qv)) * FOX_SCALE for hh in range(2)]

        def block(kb, states, masked):
            k0 = pl.multiple_of(kb * blk, blk)
            kv = k_ref[pl.ds(k0, blk), :]
            vv = v_ref[pl.ds(k0, blk), :]
            new = []
            for hh in range(2):
                m, acc = states[hh]
                s = _bdot(qhs[hh], kv, _DIMS["nt"]) + _fox_bias(cum_ref, hh, q0, k0, blk)
                if masked:
                    s = jnp.where(row >= col, s, -jnp.inf)
                m_new = jnp.maximum(m, jnp.max(s, axis=1, keepdims=True))
                p = jnp.exp(s - m_new)
                vh = jnp.where(_fox_head_mask(vv.shape, hh), vv, jnp.ones_like(vv))
                acc = jnp.exp(m - m_new) * acc + _bdot(p, vh, _DIMS["nn"])
                new.append((m_new, acc))
            return tuple(new)

        init = (jnp.full((blk, 1), -jnp.inf, F32), jnp.zeros((blk, LANE), F32))
        states = lax.fori_loop(0, iq, lambda kb, st: block(kb, st, False), (init, init))
        states = block(iq, states, True)
        outs = []
        for hh in range(2):
            m, acc = states[hh]
            other = pltpu.roll(acc, FOX_HEAD_DIM, axis=1)
            outs.append(acc / other)
            lse_ref[hh] = m + jnp.log(jnp.where(_fox_head_mask(acc.shape, hh), other, acc))
        o_ref[...] = jnp.where(_fox_head_mask(outs[0].shape, 0), outs[0], outs[1]).astype(o_ref.dtype)

    return pl.pallas_call(
        body, name=name, grid=(npair, nb),
        in_specs=[pl.BlockSpec((blk, LANE), lambda p, i: (i, p)),
                  pl.BlockSpec((seq, LANE), lambda p, i: (0, npair + p)),
                  pl.BlockSpec((seq, LANE), lambda p, i: (0, 2 * npair + p)),
                  pl.BlockSpec((None, 2, seq), lambda p, i: (p, 0, 0))],
        out_specs=(pl.BlockSpec((blk, LANE), lambda p, i: (i, p)),
                   pl.BlockSpec((2, blk, LANE), lambda p, i: (p, i, 0))),
        out_shape=(jax.ShapeDtypeStruct((seq, FOX_WIDTH), BF16), jax.ShapeDtypeStruct((FOX_HEADS, seq, LANE), F32)),
        compiler_params=_cparams(("parallel", "arbitrary")),
    )(qkv, qkv, qkv, cum_t)


def _fox_bwd(name, qkv, cum_t, att, datt, lse):
    seq = qkv.shape[0]
    blk = min(FOX_BLK, seq)
    nb = seq // blk
    npair = FOX_HEADS // 2

    def body(q_ref, k_ref, v_ref, cum_ref, o_ref, do_ref, lse_ref, dq_ref, dk_ref, dv_ref, dcs_ref):
        iq = pl.program_id(1)
        q0 = pl.multiple_of(iq * blk, blk)

        @pl.when(iq == 0)
        def _():
            dk_ref[...] = jnp.zeros_like(dk_ref)
            dv_ref[...] = jnp.zeros_like(dv_ref)
            dcs_ref[...] = jnp.zeros_like(dcs_ref)

        qv = q_ref[...]
        dov = do_ref[...].astype(F32)
        ov = o_ref[...].astype(F32)
        row = lax.broadcasted_iota(jnp.int32, (blk, blk), 0)
        col = lax.broadcasted_iota(jnp.int32, (blk, blk), 1)
        low = _fox_head_mask((blk, LANE), 0)
        qhs, qones, dohbs, deltas, lses = [], [], [], [], []
        for hh in range(2):
            hm = _fox_head_mask(qv.shape, hh)
            qh = jnp.where(hm, qv, jnp.zeros_like(qv)) * FOX_SCALE
            qhs.append(qh)
            qones.append(jnp.where(hm, qh, jnp.ones_like(qh)))
            doh = jnp.where(hm, dov, 0.0)
            dohbs.append(doh.astype(BF16))
            deltas.append(jnp.sum(doh * ov, axis=1, keepdims=True))
            lses.append(jnp.tile(lse_ref[hh], (1, blk // LANE)))

        def block(kb, dqs, masked):
            k0 = pl.multiple_of(kb * blk, blk)
            kv = k_ref[pl.ds(k0, blk), :]
            vv = v_ref[pl.ds(k0, blk), :]
            new, dks, dvs = [], [], []
            for hh in range(2):
                s = _bdot(qhs[hh], kv, _DIMS["nt"]) + _fox_bias(cum_ref, hh, q0, k0, blk)
                p = jnp.exp(s - lses[hh])
                if masked:
                    p = jnp.where(row >= col, p, 0.0)
                dp = _bdot(dohbs[hh], vv, _DIMS["nt"])
                dsb = (p * (dp - deltas[hh])).astype(BF16)
                dks.append(_bdot(dsb, qones[hh], _DIMS["tn"]))
                dvs.append(_bdot(p, dohbs[hh], _DIMS["tn"]))
                kones = jnp.where(_fox_head_mask(kv.shape, hh), kv, jnp.ones_like(kv))
                new.append(dqs[hh] + _bdot(dsb, kones, _DIMS["nn"]))
            dk_ref[pl.ds(k0, blk), :] += jnp.where(low, dks[0], dks[1])
            dv_ref[pl.ds(k0, blk), :] += dvs[0] + dvs[1]
            dcs_ref[pl.ds(k0, blk), :] -= jnp.where(low, dks[1], dks[0])
            return tuple(new)

        init = jnp.zeros((blk, LANE), F32)
        dqs = lax.fori_loop(0, iq, lambda kb, a: block(kb, a, False), (init, init))
        dqs = block(iq, dqs, True)
        dcs_ref[pl.ds(q0, blk), :] += jnp.where(low, dqs[1], dqs[0])
        dq_ref[...] = (jnp.where(low, dqs[0], dqs[1]) * FOX_SCALE).astype(dq_ref.dtype)

    qblk = pl.BlockSpec((blk, LANE), lambda p, i: (i, p))
    full = pl.BlockSpec((seq, LANE), lambda p, i: (0, p))
    return pl.pallas_call(
        body, name=name, grid=(npair, nb),
        in_specs=[qblk,
                  pl.BlockSpec((seq, LANE), lambda p, i: (0, npair + p)),
                  pl.BlockSpec((seq, LANE), lambda p, i: (0, 2 * npair + p)),
                  pl.BlockSpec((None, 2, seq), lambda p, i: (p, 0, 0)),
                  qblk, qblk,
                  pl.BlockSpec((2, blk, LANE), lambda p, i: (p, i, 0))],
        out_specs=(qblk, full, full, full),
        out_shape=(jax.ShapeDtypeStruct((seq, FOX_WIDTH), BF16), jax.ShapeDtypeStruct((seq, FOX_WIDTH), F32),
                   jax.ShapeDtypeStruct((seq, FOX_WIDTH), F32), jax.ShapeDtypeStruct((seq, FOX_WIDTH), F32)),
        compiler_params=_cparams(("arbitrary", "arbitrary")),
    )(qkv, qkv, qkv, cum_t, att, datt, lse)


MEM_SCALE = MEM_HEAD_DIM ** -0.5


def _mem_probs(qh, kh):
    s = _bdot(qh, kh, _DIMS["nt"]) * MEM_SCALE
    p = jnp.exp(s - jnp.max(s, axis=1, keepdims=True))
    return p / jnp.sum(p, axis=1, keepdims=True)


def _mem_fwd(name, q2, kv, *, tr=512):
    seq = q2.shape[0]
    mlen = kv.shape[0]
    tr = min(tr, seq)

    def body(q_ref, kv_ref, o_ref):
        for h in range(MEM_HEADS):
            sl = slice(h * MEM_HEAD_DIM, (h + 1) * MEM_HEAD_DIM)
            sv = slice(MEM_WIDTH + h * MEM_HEAD_DIM, MEM_WIDTH + (h + 1) * MEM_HEAD_DIM)
            p = _mem_probs(q_ref[:, sl], kv_ref[:, sl])
            o_ref[:, sl] = _bdot(p, kv_ref[:, sv], _DIMS["nn"]).astype(o_ref.dtype)

    return pl.pallas_call(
        body, name=name, grid=(seq // tr,),
        in_specs=[pl.BlockSpec((tr, MEM_WIDTH), lambda i: (i, 0)), pl.BlockSpec((mlen, 2 * MEM_WIDTH), lambda i: (0, 0))],
        out_specs=pl.BlockSpec((tr, MEM_WIDTH), lambda i: (i, 0)),
        out_shape=jax.ShapeDtypeStruct((seq, MEM_WIDTH), BF16),
        compiler_params=_cparams(("parallel",)),
    )(q2, kv)


def _mem_bwd(name, q2, kv, do2, *, tr=512):
    seq = q2.shape[0]
    mlen = kv.shape[0]
    tr = min(tr, seq)

    def body(q_ref, kv_ref, do_ref, dq_ref, dkv_ref):
        i = pl.program_id(0)

        @pl.when(i == 0)
        def _():
            dkv_ref[...] = jnp.zeros_like(dkv_ref)

        for h in range(MEM_HEADS):
            sl = slice(h * MEM_HEAD_DIM, (h + 1) * MEM_HEAD_DIM)
            sv = slice(MEM_WIDTH + h * MEM_HEAD_DIM, MEM_WIDTH + (h + 1) * MEM_HEAD_DIM)
            qh = q_ref[:, sl]
            kh = kv_ref[:, sl]
            doh = do_ref[:, sl].astype(BF16)
            p = _mem_probs(qh, kh)
            dp = _bdot(doh, kv_ref[:, sv], _DIMS["nt"])
            ds = (p * (dp - jnp.sum(p * dp, axis=1, keepdims=True)) * MEM_SCALE).astype(BF16)
            dq_ref[:, sl] = _bdot(ds, kh, _DIMS["nn"]).astype(dq_ref.dtype)
            dkv_ref[:, sl] += _bdot(ds, qh, _DIMS["tn"])
            dkv_ref[:, sv] += _bdot(p, doh, _DIMS["tn"])

    row = pl.BlockSpec((tr, MEM_WIDTH), lambda i: (i, 0))
    kvs = pl.BlockSpec((mlen, 2 * MEM_WIDTH), lambda i: (0, 0))
    return pl.pallas_call(
        body, name=name, grid=(seq // tr,), in_specs=[row, kvs, row], out_specs=(row, kvs),
        out_shape=(jax.ShapeDtypeStruct((seq, MEM_WIDTH), BF16), jax.ShapeDtypeStruct((mlen, 2 * MEM_WIDTH), F32)),
        compiler_params=_cparams(("arbitrary",)),
    )(q2, kv, do2)


_HBM = pl.BlockSpec(memory_space=pl.ANY)
_HBM_ONLY = pl.BlockSpec(memory_space=pltpu.HBM)
_MESH = pl.DeviceIdType.MESH


def _mesh_place():
    x, y, c = lax.axis_index("x"), lax.axis_index("y"), lax.axis_index("c")
    other_chips = [(1 - x, y), (x, 1 - y), (1 - x, 1 - y)]
    return x, y, c, other_chips


def _gather_all(name, arrays):
    n = len(arrays)

    def body(*refs):
        ins, outs = refs[:n], refs[n:2 * n]
        send_sems, recv_sems, local_sems = refs[2 * n:]
        x, y, c, chips = _mesh_place()
        me, sibling = (x, y, c), (x, y, 1 - c)

        def slot(a, place):
            px, py, pc = place
            return outs[a].at[4 * px + 2 * py + pc]

        def copy(a, k, block, to, src=None):
            return pltpu.make_async_remote_copy(
                src_ref=slot(a, block) if src is None else src, dst_ref=slot(a, block),
                send_sem=send_sems.at[a, k], recv_sem=recv_sems.at[a, k], device_id=to, device_id_type=_MESH)

        mine = [pltpu.make_async_copy(ins[a], slot(a, me), local_sems.at[a]) for a in range(n)]
        for cp in mine:
            cp.start()
        first = []
        for a in range(n):
            first.append(copy(a, 0, me, sibling, src=ins[a]))
            first += [copy(a, 1 + j, me, (*chip, c), src=ins[a]) for j, chip in enumerate(chips)]
        for cp in first:
            cp.start()
        passed = []
        for j, chip in enumerate(chips):
            for a in range(n):
                copy(a, 1 + j, (*chip, c), me).wait_recv()
                fwd = copy(a, 4 + j, (*chip, c), sibling)
                fwd.start()
                passed.append(fwd)
        for a in range(n):
            copy(a, 0, sibling, me).wait_recv()
            for j, chip in enumerate(chips):
                copy(a, 4 + j, (*chip, 1 - c), me).wait_recv()
        for cp in first + passed:
            cp.wait_send()
        for cp in mine:
            cp.wait()

    out_shape = tuple(jax.ShapeDtypeStruct((N_DEV,) + arr.shape, arr.dtype) for arr in arrays)
    return pl.pallas_call(
        body, name=name, in_specs=[_HBM] * n, out_specs=tuple([_HBM] * n), out_shape=out_shape,
        scratch_shapes=[pltpu.SemaphoreType.DMA((n, N_DEV - 1)), pltpu.SemaphoreType.DMA((n, N_DEV - 1)),
                        pltpu.SemaphoreType.DMA((n,))],
    )(*arrays)


_SEM = pl.BlockSpec(memory_space=pltpu.SEMAPHORE)
_DATAFLOW = pltpu.SideEffectType.DATAFLOW_SIDE_EFFECTING


def _device_index():
    return (4 * lax.axis_index("x") + 2 * lax.axis_index("y") + lax.axis_index("c")).astype(jnp.int32).reshape(1)


def _place_own(name, pieces, *, stacked_src, after=None):
    n = len(pieces)
    n_in = n + (after is not None)

    def body(me_ref, *refs):
        for a in range(n):
            refs[n_in + a][...] = refs[a][...]

    def spec(shape):
        return pl.BlockSpec((None,) + tuple(shape), lambda i, me_ref: (me_ref[0],) + (0,) * len(shape))

    shapes = [p.shape[1:] if stacked_src else p.shape for p in pieces]
    if stacked_src:
        in_specs = [spec(s) for s in shapes]
    else:
        in_specs = [pl.BlockSpec(tuple(s), lambda i, me_ref, nd=len(s): (0,) * nd) for s in shapes]
    operands = list(pieces)
    if after is not None:
        in_specs.append(_HBM)
        operands.append(after)
    return pl.pallas_call(
        body, name=name,
        grid_spec=pltpu.PrefetchScalarGridSpec(num_scalar_prefetch=1, grid=(1,), in_specs=in_specs,
                                               out_specs=tuple(spec(s) for s in shapes)),
        out_shape=tuple(jax.ShapeDtypeStruct((N_DEV,) + tuple(s), p.dtype) for s, p in zip(shapes, pieces)),
        compiler_params=_cparams(("arbitrary",)),
    )(_device_index(), *operands)


def _peer_places():
    x, y, c = lax.axis_index("x"), lax.axis_index("y"), lax.axis_index("c")
    peers = []
    for k in range(N_DEV - 1):
        flip = k + 1
        px = 1 - x if flip & 4 else x
        py = 1 - y if flip & 2 else y
        pc = 1 - c if flip & 1 else c
        peers.append((px, py, pc, 4 * px + 2 * py + pc))
    return 4 * x + 2 * y + c, peers


def _direct_copy(srcs, lands, send_sems, recv_sems, a, k, me, peer, scatter):
    px, py, pc, pidx = peer
    return pltpu.make_async_remote_copy(
        src_ref=srcs[a].at[pidx] if scatter else srcs[a], dst_ref=lands[a].at[me],
        send_sem=send_sems.at[a * (N_DEV - 1) + k], recv_sem=recv_sems.at[a * (N_DEV - 1) + k],
        device_id=(px, py, pc), device_id_type=_MESH)


def _send_start(name, srcs, lands, *, scatter):
    n = len(srcs)

    def body(*refs):
        src_refs, land_refs = refs[:n], refs[n:2 * n]
        send_sems, recv_sems = refs[2 * n], refs[2 * n + 1]
        token = refs[-1]
        me, peers = _peer_places()
        for k, peer in enumerate(peers):
            for a in range(n):
                _direct_copy(src_refs, land_refs, send_sems, recv_sems, a, k, me, peer, scatter).start()
        token[...] = jnp.zeros_like(token)

    hbm_shapes = [pltpu.HBM(t.shape, t.dtype) for t in list(srcs) + list(lands)]
    outs = pl.pallas_call(
        body, name=name,
        out_shape=(pltpu.SemaphoreType.DMA((n * (N_DEV - 1),)), pltpu.SemaphoreType.DMA((n * (N_DEV - 1),)), *hbm_shapes,
                   jax.ShapeDtypeStruct((8, LANE), F32)),
        in_specs=[_HBM_ONLY] * (2 * n),
        out_specs=(_SEM, _SEM, *([_HBM_ONLY] * (2 * n)), pl.BlockSpec(memory_space=pltpu.VMEM)),
        input_output_aliases={i: 2 + i for i in range(2 * n)},
        compiler_params=pltpu.CompilerParams(has_side_effects=_DATAFLOW),
    )(*[pltpu.with_memory_space_constraint(t, pltpu.HBM) for t in list(srcs) + list(lands)])
    return outs[0], outs[1], outs[2:2 + n], outs[2 + n:2 + 2 * n], outs[-1]


def _send_wait(name, send_sems, recv_sems, srcs, lands, after, *, scatter):
    n = len(srcs)
    afters = list(after) if isinstance(after, (tuple, list)) else [after]

    def body(*refs):
        src_refs, land_refs = refs[:n], refs[n:2 * n]
        send_sems, recv_sems = refs[2 * n], refs[2 * n + 1]
        me, peers = _peer_places()
        for k, peer in enumerate(peers):
            for a in range(n):
                cp = _direct_copy(src_refs, land_refs, send_sems, recv_sems, a, k, me, peer, scatter)
                cp.wait_send()
                cp.wait_recv()

    hbm_shapes = [pltpu.HBM(t.shape, t.dtype) for t in list(srcs) + list(lands)]
    outs = pl.pallas_call(
        body, name=name, out_shape=tuple(hbm_shapes),
        in_specs=[_HBM_ONLY] * (2 * n) + [_SEM, _SEM] + [_HBM] * len(afters),
        out_specs=tuple([_HBM_ONLY] * (2 * n)),
        input_output_aliases={i: i for i in range(2 * n)},
        compiler_params=pltpu.CompilerParams(has_side_effects=_DATAFLOW),
    )(*srcs, *lands, send_sems, recv_sems, *afters)
    return outs[n:]


def _unstack_cols(name, stacked):
    n, rows, cols = stacked.shape

    def body(i_ref, o_ref):
        o_ref[...] = i_ref[...]

    return pl.pallas_call(
        body, name=name, grid=(n,), in_specs=[pl.BlockSpec((None, rows, cols), lambda k: (k, 0, 0))],
        out_specs=pl.BlockSpec((rows, cols), lambda k: (0, k)),
        out_shape=jax.ShapeDtypeStruct((rows, n * cols), stacked.dtype),
        compiler_params=_cparams(("parallel",)),
    )(stacked)


def _restack_cols(name, mat):
    rows, width = mat.shape
    cols = width // N_DEV

    def body(i_ref, o_ref):
        o_ref[...] = i_ref[...]

    return pl.pallas_call(
        body, name=name, grid=(N_DEV,), in_specs=[pl.BlockSpec((rows, cols), lambda k: (0, k))],
        out_specs=pl.BlockSpec((None, rows, cols), lambda k: (k, 0, 0)),
        out_shape=jax.ShapeDtypeStruct((N_DEV, rows, cols), mat.dtype),
        compiler_params=_cparams(("parallel",)),
    )(mat)


def _remap_pieces(runs):
    plan = {}
    for du, dc, su, sc, ln in runs:
        while ln > 0:
            lane = dc % LANE
            take = min(ln, LANE - lane)
            plan.setdefault((du, dc // LANE), []).append((su, sc, take, lane))
            dc, sc, ln = dc + take, sc + take, ln - take
    return plan


def _remap(name, srcs, src_units, runs, *, out_units, out_cols, out_dtype, tr=256):
    rows = srcs[0].shape[-2]
    tr = min(tr, rows)
    plan = _remap_pieces(runs)
    n_src = len(srcs)
    stacked_out = out_units is not None
    n_tiles = out_cols // LANE

    def body(*refs):
        o_ref = refs[n_src]

        def src_tile(unit, t):
            ai, lead = src_units[unit]
            ref = refs[ai]
            sl = slice(t * LANE, (t + 1) * LANE)
            return (ref[:, sl] if lead is None else ref[lead, :, sl]).astype(F32)

        lane = lax.broadcasted_iota(jnp.int32, (tr, LANE), 1)
        for du in range(out_units if stacked_out else 1):
            for t in range(n_tiles):
                acc = jnp.zeros((tr, LANE), F32)
                for su, sc, ln, dl in plan.get((du if stacked_out else None, t), []):
                    st, so = sc // LANE, sc % LANE
                    first = src_tile(su, st)
                    if so == dl and so + ln <= LANE:
                        piece = first
                    else:
                        second = src_tile(su, st + 1) if so + ln > LANE else first
                        both = jnp.concatenate([first, second], axis=1)
                        piece = pltpu.roll(both, (dl - so) % (2 * LANE), axis=1)[:, 0:LANE]
                    acc = piece if (dl == 0 and ln == LANE) else jnp.where(
                        jnp.logical_and(lane >= dl, lane < dl + ln), piece, acc)
                if stacked_out:
                    o_ref[du, :, t * LANE:(t + 1) * LANE] = acc.astype(o_ref.dtype)
                else:
                    o_ref[:, t * LANE:(t + 1) * LANE] = acc.astype(o_ref.dtype)

    in_specs = []
    for arr in srcs:
        if arr.ndim == 2:
            in_specs.append(pl.BlockSpec((tr, arr.shape[1]), lambda i: (i, 0)))
        else:
            in_specs.append(pl.BlockSpec((arr.shape[0], tr, arr.shape[2]), lambda i: (0, i, 0)))
    if stacked_out:
        out_spec = pl.BlockSpec((out_units, tr, out_cols), lambda i: (0, i, 0))
        out_shape = jax.ShapeDtypeStruct((out_units, rows, out_cols), out_dtype)
    else:
        out_spec = pl.BlockSpec((tr, out_cols), lambda i: (i, 0))
        out_shape = jax.ShapeDtypeStruct((rows, out_cols), out_dtype)
    return pl.pallas_call(
        body, name=name, grid=(rows // tr,), in_specs=in_specs, out_specs=out_spec, out_shape=out_shape,
        compiler_params=_cparams(("parallel",)),
    )(*srcs)


def _proj_col(c):
    if c < PROJ_GATE0:
        return c
    if c < PROJ_GATE0 + FOX_HEADS:
        return PROJ_F0 + (c - PROJ_GATE0)
    return c - FOX_HEADS


def _win_runs():
    cuts = sorted(set([0, PROJ_GATE0, PROJ_GATE0 + FOX_HEADS, IN_WIDTH] + [SHARD_IN * k for k in range(N_DEV + 1)]))
    return [(lo // SHARD_IN, lo % SHARD_IN, _proj_col(lo), hi - lo) for lo, hi in zip(cuts[:-1], cuts[1:])]


def _assemble_win(name, stacked):
    runs = [(None, pc, k, sc, ln) for k, sc, pc, ln in _win_runs()]
    return _remap(name, [stacked], [(0, k) for k in range(N_DEV)], runs,
                  out_units=None, out_cols=PROJ_WIDTH, out_dtype=BF16)


def _disassemble_dwin(name, dw):
    runs = [(k, sc, 0, pc, ln) for k, sc, pc, ln in _win_runs()]
    return _remap(name, [dw], [(0, None)], runs, out_units=N_DEV, out_cols=SHARD_IN_PAD, out_dtype=BF16)


def _concat_cols(name, parts, *, tr=512):
    rows = parts[0].shape[0]
    tr = min(tr, rows)
    widths = [p.shape[1] for p in parts]
    total = sum(widths)

    def body(*refs):
        o_ref = refs[len(parts)]
        lo = 0
        for r, w in zip(refs[:len(parts)], widths):
            o_ref[:, lo:lo + w] = r[...].astype(o_ref.dtype)
            lo += w

    return pl.pallas_call(
        body, name=name, grid=(rows // tr,),
        in_specs=[pl.BlockSpec((tr, w), lambda i: (i, 0)) for w in widths],
        out_specs=pl.BlockSpec((tr, total), lambda i: (i, 0)),
        out_shape=jax.ShapeDtypeStruct((rows, total), BF16),
        compiler_params=_cparams(("parallel",)),
    )(*parts)


FFN_BLK = FFN_HIDDEN // 2


def _ffn_col(c):
    half, r = divmod(c, FFN_HIDDEN)
    blk, r = divmod(r, FFN_BLK)
    return blk * 2 * FFN_BLK + half * FFN_BLK + r


def _assemble_wffn(name, stacked):
    runs = [(None, _ffn_col(SHARD_FFN * k), k, 0, SHARD_FFN) for k in range(N_DEV)]
    return _remap(name, [stacked], [(0, k) for k in range(N_DEV)], runs,
                  out_units=None, out_cols=2 * FFN_HIDDEN, out_dtype=BF16)


def _disassemble_dwffn(name, dw):
    runs = [(k, 0, 0, _ffn_col(SHARD_FFN * k), SHARD_FFN) for k in range(N_DEV)]
    return _remap(name, [dw], [(0, None)], runs, out_units=N_DEV, out_cols=SHARD_FFN_PAD, out_dtype=BF16)


def _ffn_in_swiglu(name, xn, w, *, tm=512):
    rows, k = xn.shape
    tm = min(tm, rows)
    nblk = FFN_HIDDEN // FFN_BLK

    def body(x_ref, w_ref, f_ref, g_ref):
        f = _bdot(x_ref[...], w_ref[...], _DIMS["nn"])
        f_ref[...] = f.astype(f_ref.dtype)
        fa = f[:, 0:FFN_BLK]
        g_ref[...] = (fa * _sigmoid(fa) * f[:, FFN_BLK:2 * FFN_BLK]).astype(g_ref.dtype)

    return pl.pallas_call(
        body, name=name, grid=(nblk, rows // tm),
        in_specs=[pl.BlockSpec((tm, k), lambda j, i: (i, 0)), pl.BlockSpec((k, 2 * FFN_BLK), lambda j, i: (0, j))],
        out_specs=(pl.BlockSpec((tm, 2 * FFN_BLK), lambda j, i: (i, j)), pl.BlockSpec((tm, FFN_BLK), lambda j, i: (i, j))),
        out_shape=(jax.ShapeDtypeStruct((rows, 2 * FFN_HIDDEN), BF16), jax.ShapeDtypeStruct((rows, FFN_HIDDEN), BF16)),
        compiler_params=_cparams(("parallel", "arbitrary")),
    )(xn, w)


def _d_ffn_out_swiglu(name, dh, w_out, f, *, tm=512):
    rows, d = dh.shape
    tm = min(tm, rows)
    nblk = FFN_HIDDEN // FFN_BLK

    def body(dh_ref, w_ref, f_ref, df_ref):
        dg = _bdot(dh_ref[...], w_ref[...], _DIMS["nt"])
        fa = f_ref[:, 0:FFN_BLK].astype(F32)
        fb = f_ref[:, FFN_BLK:2 * FFN_BLK].astype(F32)
        s = _sigmoid(fa)
        df_ref[:, 0:FFN_BLK] = (dg * fb * s * (1.0 + fa * (1.0 - s))).astype(df_ref.dtype)
        df_ref[:, FFN_BLK:2 * FFN_BLK] = (dg * fa * s).astype(df_ref.dtype)

    wide = pl.BlockSpec((tm, 2 * FFN_BLK), lambda j, i: (i, j))
    return pl.pallas_call(
        body, name=name, grid=(nblk, rows // tm),
        in_specs=[pl.BlockSpec((tm, d), lambda j, i: (i, 0)), pl.BlockSpec((FFN_BLK, d), lambda j, i: (j, 0)), wide],
        out_specs=wide, out_shape=jax.ShapeDtypeStruct((rows, 2 * FFN_HIDDEN), BF16),
        compiler_params=_cparams(("parallel", "arbitrary")),
    )(dh, w_out, f)


def _adamw(name, parts, w, m, v, *, tr=128):
    rows, cols = w.shape
    n_parts = parts.shape[0]
    tr = min(tr, rows)
    assert rows % tr == 0, (name, rows, tr)
    c1 = 1.0 - ADAM_B1 ** ADAM_STEP
    c2 = 1.0 - ADAM_B2 ** ADAM_STEP

    def body(p_ref, w_ref, m_ref, v_ref, g_ref, d_ref, nm_ref, nv_ref):
        g = p_ref[0].astype(F32)
        for s in range(1, n_parts):
            g = g + p_ref[s].astype(F32)
        m_new = ADAM_B1 * m_ref[...] + (1.0 - ADAM_B1) * g
        v_new = ADAM_B2 * v_ref[...] + (1.0 - ADAM_B2) * (g * g)
        upd = (m_new / c1) / (jnp.sqrt(v_new / c2) + ADAM_EPS) + ADAM_WD * w_ref[...]
        g_ref[...] = g
        d_ref[...] = -ADAM_LR * upd
        nm_ref[...] = m_new
        nv_ref[...] = v_new

    row = pl.BlockSpec((tr, cols), lambda i: (i, 0))
    out = jax.ShapeDtypeStruct((rows, cols), F32)
    return pl.pallas_call(
        body, name=name, grid=(rows // tr,),
        in_specs=[pl.BlockSpec((n_parts, tr, cols), lambda i: (0, i, 0)), row, row, row],
        out_specs=(row, row, row, row), out_shape=(out, out, out, out),
        compiler_params=_cparams(("parallel",)),
    )(parts, w, m, v)


_WEIGHTS = ("norm_mix", "w_in", "b_forget", "lam_re", "lam_im", "log_dt", "b_re", "b_im", "c_re", "c_im",
            "d_skip", "w_glu", "w_fox_o", "w_mix_out", "norm_mem_q", "norm_mem_kv", "w_mem_q", "w_mem_kv",
            "w_mem_o", "norm_ffn", "w_ffn_in", "w_ffn_out", "norm_final")
_SHARDED = ("w_in", "w_glu", "w_fox_o", "w_mix_out", "w_mem_q", "w_mem_kv", "w_mem_o", "w_ffn_in", "w_ffn_out")
_SMALL = tuple(n for n in _WEIGHTS if n not in _SHARDED)
_PACK_COLS = 1024


def _pack(arrays):
    flat = jnp.concatenate([a.reshape(-1).astype(F32) for a in arrays])
    rows = -(-flat.shape[0] // _PACK_COLS)
    return jnp.pad(flat, (0, rows * _PACK_COLS - flat.shape[0])).reshape(rows, _PACK_COLS)


def _unpack(buf, like):
    flat = buf.reshape(-1)
    out, pos = [], 0
    for a in like:
        out.append(flat[pos:pos + a.size].reshape(a.shape))
        pos += a.size
    return out


def _mm(name, a, b, mode, m, n, k, out_dtype, tm=1024, tn=512, tk=1024, **kw):
    return _matmul(name, a, b, mode, m, n, k, out_dtype=out_dtype, tm=tm, tn=tn, tk=tk, **kw)


def kernel(x, mem, norm_mix, w_in, b_forget, lam_re, lam_im, log_dt, b_re, b_im, c_re, c_im, d_skip, w_glu, w_fox_o, w_mix_out, norm_mem_q, norm_mem_kv, w_mem_q, w_mem_kv, w_mem_o, norm_ffn, w_ffn_in, w_ffn_out, norm_final, loss_target, m_norm_mix, m_w_in, m_b_forget, m_lam_re, m_lam_im, m_log_dt, m_b_re, m_b_im, m_c_re, m_c_im, m_d_skip, m_w_glu, m_w_fox_o, m_w_mix_out, m_norm_mem_q, m_norm_mem_kv, m_w_mem_q, m_w_mem_kv, m_w_mem_o, m_norm_ffn, m_w_ffn_in, m_w_ffn_out, m_norm_final, v_norm_mix, v_w_in, v_b_forget, v_lam_re, v_lam_im, v_log_dt, v_b_re, v_b_im, v_c_re, v_c_im, v_d_skip, v_w_glu, v_w_fox_o, v_w_mix_out, v_norm_mem_q, v_norm_mem_kv, v_w_mem_q, v_w_mem_kv, v_w_mem_o, v_norm_ffn, v_w_ffn_in, v_w_ffn_out, v_norm_final):
    given = dict(locals())
    weights = {n: given[n] for n in _WEIGHTS}
    mom_m = {n: given["m_" + n] for n in _WEIGHTS}
    mom_v = {n: given["v_" + n] for n in _WEIGHTS}
    seq = x.shape[1]
    nc = seq // SSM_CHUNK
    d = D_MODEL
    xs, mems, tgt = x[0], mem[0], loss_target[0]

    def padcols(a, width):
        return jnp.pad(a, ((0, 0), (0, width - a.shape[1])))

    shards = [padcols(w_in[0].astype(BF16), SHARD_IN_PAD), w_glu[0].astype(BF16), w_fox_o[0].astype(BF16),
              w_mix_out[0].astype(BF16), w_mem_q[0].astype(BF16), w_mem_kv[0].astype(BF16),
              w_mem_o[0].astype(BF16), padcols(w_ffn_in[0].astype(BF16), SHARD_FFN_PAD), w_ffn_out[0].astype(BF16)]
    first = shards[:1]
    wsend, wrecv, first_thru, first_lands, wtoken = _send_start(
        "gather_w_in_start", first, _place_own("place_w_in_shard", first, stacked_src=False), scatter=False)
    rest = shards[1:]
    gsend, grecv, rest_thru, lands, gtoken = _send_start(
        "gather_rest_start", rest, _place_own("place_weight_shards", rest, stacked_src=False, after=wtoken),
        scatter=False)

    u = _rms_fwd("rms_mix", xs, norm_mix, after=gtoken)
    ssm_params = tuple(p[0] + wtoken[0, 0] for p in (lam_re, lam_im, log_dt, b_re, b_im, c_re, c_im))
    (m_c, bw_c, cm_c, a8, aseg), mats_vjp = jax.vjp(lambda *p: _ssm_mats(*p, nc), *ssm_params)
    m_b = _bd_expand("ssm_expand_m", _BD_M, m_c)
    bw_b = _bd_expand("ssm_expand_bw", _BD_BW, bw_c)
    cm_b = _bd_expand("ssm_expand_cm", _BD_CM, cm_c)
    win = _assemble_win("assemble_w_in", _send_wait(
        "gather_w_in_wait", wsend, wrecv, first_thru, first_lands, (u, m_b, bw_b, cm_b), scatter=False)[0])
    ussm = _mm("proj_ssm", u, win, "nn", seq, SSM_WIDTH, d, F32)
    qkv = _mm("proj_qkv", u, win, "nn", seq, 3 * FOX_WIDTH, d, BF16, tn=512, b_off=(0, SSM_WIDTH))
    gates = _mm("proj_gates", u, win, "nn", seq, 2 * d, d, BF16, tn=1024, b_off=(0, PROJ_GATE0))
    fproj = _mm("proj_forget", u, win, "nn", seq, LANE, d, F32, tn=LANE, b_off=(0, PROJ_F0))

    u8 = ussm.reshape(nc, SSM_CHUNK * SSM_WIDTH)
    d8 = jnp.tile(d_skip, (1, SSM_CHUNK))
    w4 = _ssm_w("ssm_w", u8, bw_b)
    sp4 = _ssm_scan("ssm_scan", w4, a8, aseg, reverse=False)
    y8 = _ssm_y("ssm_y", u8, sp4, m_b, cm_b)
    act = _ssm_post_fwd("ssm_act", y8, u8, d8).reshape(seq, SSM_WIDTH)

    bcol = jnp.pad(b_forget[0], (0, LANE - FOX_HEADS)).reshape(LANE, 1)
    cum_t = _fox_cum("fox_cum", fproj, bcol).reshape(FOX_HEADS // 2, 2, seq)
    att, lse = _fox_fwd("fox_fwd", qkv, cum_t)

    gathered = _send_wait("gather_rest_wait", gsend, grecv, rest_thru, lands, att, scatter=False)
    wglu = _unstack_cols("unstack_w_glu", gathered[0])
    wfoxo = _unstack_cols("unstack_w_fox_o", gathered[1])
    wmix = gathered[2].reshape(d, d)
    wmq = gathered[3].reshape(d, MEM_WIDTH)
    wmkv = gathered[4].reshape(d, 2 * MEM_WIDTH)
    wmo = _unstack_cols("unstack_w_mem_o", gathered[5])
    wffn_in = _assemble_wffn("assemble_w_ffn_in", gathered[6])
    wffn_out = gathered[7].reshape(FFN_HIDDEN, d)

    glu = _mm("glu", act, wglu, "nn", seq, 2 * d, SSM_WIDTH, BF16, tn=1024)
    out_b = _mm("fox_out", att, wfoxo, "nn", seq, d, FOX_WIDTH, BF16, tn=1024)

    mixin = _mix_fwd("mix", glu, gates, out_b)
    h1 = _mm("mix_out", mixin, wmix, "nn", seq, d, d, F32, tn=1024, add=xs)

    n1 = _rms_fwd("rms_mem_q", h1, norm_mem_q)
    q2 = _mm("mem_q", n1, wmq, "nn", seq, MEM_WIDTH, d, BF16)
    mn = _rms_fwd("rms_mem_kv", mems, norm_mem_kv)
    mlen = mems.shape[0]
    kv = _mm("mem_kv", mn, wmkv, "nn", mlen, 2 * MEM_WIDTH, d, BF16)
    o2 = _mem_fwd("mem_attn", q2, kv)
    h2 = _mm("mem_out", o2, wmo, "nn", seq, d, MEM_WIDTH, F32, tn=1024, add=h1)

    n2 = _rms_fwd("rms_ffn", h2, norm_ffn)
    f, g_act = _ffn_in_swiglu("ffn_in_swiglu", n2, wffn_in)
    loss_part, dh3, dg_final = _matmul_final_loss("ffn_out_final_loss", g_act, wffn_out, h2, tgt,
                                                  norm_final.reshape(1, d))

    df = _d_ffn_out_swiglu("d_ffn_out_swiglu", dh3, wffn_out, f)
    dwffn_out = _mm("d_ffn_out_w", g_act, dh3, "tn", FFN_HIDDEN, d, seq, BF16, tm=1408, tn=1024)
    dh2, dg_ffn = _matmul_rms_bwd("d_ffn_in_x_rms", df, wffn_in, 2 * FFN_HIDDEN, h2, norm_ffn, dh3, tm=1024, tk=1408)
    dwffn_in = _mm("d_ffn_in_w", n2, df, "tn", d, 2 * FFN_HIDDEN, seq, BF16, tn=1408)

    do2 = _mm("d_mem_out_x", dh2, wmo, "nt", seq, MEM_WIDTH, d, F32)
    dwmo = _restack_cols("restack_d_w_mem_o", _mm("d_mem_out_w", o2, dh2, "tn", MEM_WIDTH, d, seq, BF16, tn=1024))
    dq2, dkv = _mem_bwd("d_mem_attn", q2, kv, do2)
    dwmq = _mm("d_mem_q_w", n1, dq2, "tn", d, MEM_WIDTH, seq, BF16)
    dwmkv = _mm("d_mem_kv_w", mn, dkv, "tn", d, 2 * MEM_WIDTH, mlen, BF16, tn=1024)
    dmn = _mm("d_mem_kv_x", dkv, wmkv, "nt", mlen, d, 2 * MEM_WIDTH, F32)
    dg_memkv = _rms_gain_grad("d_rms_mem_kv", dmn, mems)

    early = [dwmq.reshape(N_DEV, d // N_DEV, MEM_WIDTH), dwmkv.reshape(N_DEV, d // N_DEV, 2 * MEM_WIDTH), dwmo,
             _disassemble_dwffn("split_d_w_ffn_in", dwffn_in), dwffn_out.reshape(N_DEV, FFN_HIDDEN // N_DEV, d)]
    ssend, srecv, early_thru, early_lands, stoken = _send_start(
        "scatter_early_start", early, _place_own("place_early_grads", early, stacked_src=True), scatter=True)
    dh1, dg_memq = _matmul_rms_bwd("d_mem_q_x_rms", dq2, wmq, MEM_WIDTH, h1, norm_mem_q, dh2, tm=1024, after=stoken)

    dmixin = _mm("d_mix_out_x", dh1, wmix, "nt", seq, d, d, F32, tn=1024)
    dwmix = _mm("d_mix_out_w", mixin, dh1, "tn", d, d, seq, BF16, tn=1024)
    dglu, dgates, dout_b = _mix_bwd("d_mix", dmixin, glu, gates, out_b)
    datt = _mm("d_fox_out_x", dout_b, wfoxo, "nt", seq, FOX_WIDTH, d, F32)
    dwfoxo = _restack_cols("restack_d_w_fox_o", _mm("d_fox_out_w", att, dout_b, "tn", FOX_WIDTH, d, seq, BF16, tn=1024))
    dact = _mm("d_glu_x", dglu, wglu, "nt", seq, SSM_WIDTH, 2 * d, F32, tk=2 * d)
    dwglu = _restack_cols("restack_d_w_glu", _mm("d_glu_w", act, dglu, "tn", SSM_WIDTH, 2 * d, seq, BF16, tn=2 * d))

    mid = [dwglu, dwfoxo, dwmix.reshape(N_DEV, d // N_DEV, d)]
    msend, mrecv, mid_thru, mid_lands, mtoken = _send_start(
        "scatter_mid_start", mid, _place_own("place_mid_grads", mid, stacked_src=True), scatter=True)

    dz8, dg_dskip = _ssm_post_bwd("d_ssm_act", dact.reshape(nc, SSM_CHUNK * SSM_WIDTH), y8, u8, d8, after=mtoken)
    ds4, dcm = _ssm_ds("d_ssm_y_state", dz8, sp4, cm_b)
    g4, da8 = _ssm_scan("d_ssm_scan", ds4, a8, aseg, reverse=True, sprev4=sp4)
    dx8, dm, dbw = _ssm_dx("d_ssm_x", dz8, g4, u8, m_b, bw_b, d8)
    dussm = dx8.reshape(seq, SSM_WIDTH)
    g_ssm = mats_vjp((_bd_reduce("ssm_reduce_dm", _BD_M, dm), _bd_reduce("ssm_reduce_dbw", _BD_BW, dbw),
                      _bd_reduce("ssm_reduce_dcm", _BD_CM, dcm), da8, jnp.zeros_like(aseg)))

    dq, dk, dv, dcs = _fox_bwd("d_fox", qkv, cum_t, att, datt, lse)
    dfproj, dbf = _fox_cum_bwd("d_fox_cum", dcs, fproj, bcol)
    dg_bforget = dbf[0:FOX_HEADS, 0].reshape(1, FOX_HEADS)

    dproj = _concat_cols("d_proj_concat", (dussm, dq, dk, dv, dgates, dfproj))
    dwin = _mm("d_proj_w", u, dproj, "tn", d, PROJ_WIDTH, seq, BF16, tn=1408)
    late = [_disassemble_dwin("split_d_w_in", dwin)]
    lsend, lrecv, late_thru, late_lands, ltoken = _send_start(
        "scatter_late_start", late, _place_own("place_late_grads", late, stacked_src=True), scatter=True)
    dx, dg_mix = _matmul_rms_bwd("d_proj_x_rms", dproj, win, PROJ_WIDTH, xs, norm_mix, dh1, tm=1024, tk=1408,
                                 after=ltoken)

    early_parts = _send_wait("scatter_early_wait", ssend, srecv, early_thru, early_lands, dx, scatter=True)
    mid_parts = _send_wait("scatter_mid_wait", msend, mrecv, mid_thru, mid_lands, dx, scatter=True)
    received = dict(zip(("w_glu", "w_fox_o", "w_mix_out"), mid_parts))
    received.update(zip(("w_mem_q", "w_mem_kv", "w_mem_o", "w_ffn_in", "w_ffn_out"), early_parts))

    small_grads = dict(zip(
        _SMALL, (dg_mix, dg_bforget, g_ssm[0][None], g_ssm[1][None], g_ssm[2][None], g_ssm[3][None], g_ssm[4][None],
                 g_ssm[5][None], g_ssm[6][None], dg_dskip, dg_memq, dg_memkv, dg_ffn, dg_final.reshape(d))))
    small_like = [weights[n] for n in _SMALL]
    small_all = _gather_all("gather_small_grads", [_pack([small_grads[n] for n in _SMALL])])[0]
    pk = [_pack([src[n] for n in _SMALL]) for src in (weights, mom_m, mom_v)]
    small_out = _adamw("adamw_small", small_all, pk[0], pk[1], pk[2], tr=small_all.shape[1])
    results = [dict(zip(_SMALL, _unpack(buf, small_like))) for buf in small_out]
    tiles = {"w_in": 128, "w_glu": 128, "w_fox_o": 128, "w_mix_out": 128, "w_mem_q": 128, "w_mem_kv": 128,
             "w_mem_o": 128, "w_ffn_in": 128, "w_ffn_out": 176}
    pads = {"w_in": SHARD_IN_PAD, "w_ffn_in": SHARD_FFN_PAD}
    outs = small_out
    for name in _SHARDED[1:] + _SHARDED[:1]:
        if name == "w_in":
            received[name] = _send_wait("scatter_late_wait", lsend, lrecv, late_thru, late_lands, outs[0],
                                        scatter=True)[0]
        parts = received[name]
        w2, m2, v2 = weights[name][0], mom_m[name][0], mom_v[name][0]
        cols = w2.shape[1]
        if name in pads:
            w2, m2, v2 = (padcols(t, pads[name]) for t in (w2, m2, v2))
        outs = _adamw("adamw_" + name, parts, w2, m2, v2, tr=tiles[name])
        for res, o in zip(results, outs):
            res[name] = o[:, :cols][None]

    loss = lax.psum(loss_part[0, 0], ("x", "y", "c"))
    out = [loss, dx[None]]
    for res in results:
        out.extend(res[n] for n in _WEIGHTS)
    return tuple(out)
```

```python
import math

import jax
import jax.numpy as jnp
import numpy as np
from jax import lax
from jax.experimental import pallas as pl
from jax.experimental.pallas import tpu as pltpu

F32 = jnp.float32
BF16 = jnp.bfloat16

N_DEV = 8
LANE = 128
VMEM_LIMIT = 56 * 1024 * 1024

D_MODEL = 1024
SSM_GROUP = 16
SSM_GROUPS = 32
SSM_WIDTH = 512
SSM_STATE = 64
SSM_CHUNK = 8
FOX_HEADS = 8
FOX_HEAD_DIM = 64
FOX_WIDTH = 512
MEM_HEADS = 4
MEM_HEAD_DIM = 128
MEM_WIDTH = 512
FFN_HIDDEN = 2816
RMS_EPS = 1e-6
IN_WIDTH = 4104
SHARD_IN = IN_WIDTH // N_DEV
SHARD_IN_PAD = 640
SHARD_FFN = 2 * FFN_HIDDEN // N_DEV
SHARD_FFN_PAD = 768
PROJ_GATE0 = 2048
PROJ_F0 = 4096
PROJ_WIDTH = 4224

ADAM_LR = 0.001
ADAM_B1 = 0.9
ADAM_B2 = 0.999
ADAM_EPS = 1e-08
ADAM_WD = 0.01
ADAM_STEP = 10


def _cparams(sem=None):
    return pltpu.CompilerParams(dimension_semantics=sem, vmem_limit_bytes=VMEM_LIMIT)


def _sigmoid(x):
    return 1.0 / (1.0 + jnp.exp(-x))


def _bdot(a, b, dims):
    return lax.dot_general(a.astype(BF16), b.astype(BF16), ((dims[0], dims[1]), ((), ())),
                           preferred_element_type=F32)


_DIMS = {"nn": ((1,), (0,)), "nt": ((1,), (1,)), "tn": ((0,), (0,))}


def _matmul(name, a, b, mode, m, n, k, *, out_dtype, tm, tn, tk, a_off=(0, 0), b_off=(0, 0), add=None):
    tm, tn, tk = min(tm, m), min(tn, n), min(tk, k)
    assert m % tm == 0 and n % tn == 0 and k % tk == 0, (name, m, n, k, tm, tn, tk)
    nk = k // tk
    grid = (m // tm, n // tn, nk)

    def blk(off, t):
        assert off % t == 0, (name, off, t)
        return off // t

    if mode in ("nn", "nt"):
        ar, ac = blk(a_off[0], tm), blk(a_off[1], tk)
        a_spec = pl.BlockSpec((tm, tk), lambda i, j, kk: (i + ar, kk + ac))
    else:
        ar, ac = blk(a_off[0], tk), blk(a_off[1], tm)
        a_spec = pl.BlockSpec((tk, tm), lambda i, j, kk: (kk + ar, i + ac))

    if mode in ("nn", "tn"):
        br, bc = blk(b_off[0], tk), blk(b_off[1], tn)
        b_spec = pl.BlockSpec((tk, tn), lambda i, j, kk: (kk + br, j + bc))
    else:
        br, bc = blk(b_off[0], tn), blk(b_off[1], tk)
        b_spec = pl.BlockSpec((tn, tk), lambda i, j, kk: (j + br, kk + bc))
    o_spec = pl.BlockSpec((tm, tn), lambda i, j, kk: (i, j))
    out_shape = jax.ShapeDtypeStruct((m, n), out_dtype)

    in_specs = [a_spec, b_spec]
    operands = [a, b]
    if add is not None:
        in_specs.append(pl.BlockSpec((tm, tn), lambda i, j, kk: (i, j)))
        operands.append(add)
    dims = _DIMS[mode]
    has_add = add is not None

    def body(*refs):
        a_ref, b_ref = refs[0], refs[1]
        add_ref = refs[2] if has_add else None
        o_ref = refs[3] if has_add else refs[2]
        acc_ref = refs[-1] if nk > 1 else None
        prod = _bdot(a_ref[...], b_ref[...], dims)

        def finish(total):
            if has_add:
                total = total + add_ref[...].astype(F32)
            o_ref[...] = total.astype(o_ref.dtype)

        if nk == 1:
            finish(prod)
        else:
            kk = pl.program_id(2)

            @pl.when(kk == 0)
            def _():
                acc_ref[...] = prod

            @pl.when(jnp.logical_and(kk > 0, kk < nk - 1))
            def _():
                acc_ref[...] += prod

            @pl.when(kk == nk - 1)
            def _():
                finish(acc_ref[...] + prod)

    scratch = [pltpu.VMEM((tm, tn), F32)] if nk > 1 else []
    return pl.pallas_call(
        body, name=name, grid=grid, in_specs=in_specs, out_specs=o_spec, out_shape=out_shape,
        scratch_shapes=scratch,
        compiler_params=_cparams(("parallel", "parallel", "arbitrary")),
    )(*operands)


def _rms_fwd(name, x, gain, *, tr=512, after=None):
    r, d = x.shape
    tr = min(tr, r)

    def body(x_ref, g_ref, *rest):
        o_ref = rest[-1]
        xv = x_ref[...]
        rstd = lax.rsqrt(jnp.mean(xv * xv, axis=-1, keepdims=True) + RMS_EPS)
        o_ref[...] = (xv * rstd * g_ref[...]).astype(o_ref.dtype)

    in_specs = [pl.BlockSpec((tr, d), lambda i: (i, 0)), pl.BlockSpec((1, d), lambda i: (0, 0))]
    ops = [x, gain]
    if after is not None:
        in_specs.append(pl.BlockSpec(after.shape, lambda i: (0, 0)))
        ops.append(after)
    return pl.pallas_call(
        body, name=name, grid=(r // tr,), in_specs=in_specs,
        out_specs=pl.BlockSpec((tr, d), lambda i: (i, 0)),
        out_shape=jax.ShapeDtypeStruct((r, d), BF16),
        compiler_params=_cparams(("parallel",)),
    )(*ops)


def _rms_gain_grad(name, dy, x, *, tr=512):
    r, d = x.shape
    tr = min(tr, r)
    n = r // tr

    def body(dy_ref, x_ref, dg_ref, acc_ref):
        i = pl.program_id(0)
        xv = x_ref[...]
        xh = xv * lax.rsqrt(jnp.mean(xv * xv, axis=-1, keepdims=True) + RMS_EPS)
        part = (dy_ref[...].astype(F32) * xh).reshape(tr // 8, 8, d).sum(axis=0)

        @pl.when(i == 0)
        def _():
            acc_ref[...] = part

        @pl.when(i > 0)
        def _():
            acc_ref[...] += part

        @pl.when(i == n - 1)
        def _():
            dg_ref[...] = jnp.sum(acc_ref[...], axis=0, keepdims=True)

    row = pl.BlockSpec((tr, d), lambda i: (i, 0))
    return pl.pallas_call(
        body, name=name, grid=(n,), in_specs=[row, row],
        out_specs=pl.BlockSpec((1, d), lambda i: (0, 0)),
        out_shape=jax.ShapeDtypeStruct((1, d), F32),
        scratch_shapes=[pltpu.VMEM((8, d), F32)],
        compiler_params=_cparams(("arbitrary",)),
    )(dy, x)


def _matmul_rms_bwd(name, a, b, k, x, gain, res, *, tm=512, tk=1024, after=None):
    m, d = x.shape
    tm, tk = min(tm, m), min(tk, k)
    assert m % tm == 0 and k % tk == 0, (name, m, k, tm, tk)
    ni, nk = m // tm, k // tk

    def body(a_ref, b_ref, x_ref, g_ref, res_ref, *rest):
        dx_ref, dg_ref, acc_ref, accg_ref = rest[-4:]
        i, kk = pl.program_id(0), pl.program_id(1)
        prod = _bdot(a_ref[...], b_ref[...], _DIMS["nt"])

        @pl.when(kk == 0)
        def _():
            acc_ref[...] = prod

        @pl.when(kk > 0)
        def _():
            acc_ref[...] += prod

        @pl.when(kk == nk - 1)
        def _():
            dyv = acc_ref[...]
            xv = x_ref[...]
            rstd = lax.rsqrt(jnp.mean(xv * xv, axis=-1, keepdims=True) + RMS_EPS)
            xh = xv * rstd
            dxh = dyv * g_ref[...]
            dx_ref[...] = rstd * (dxh - xh * jnp.mean(dxh * xh, axis=-1, keepdims=True)) + res_ref[...]
            part = (dyv * xh).reshape(tm // 8, 8, d).sum(axis=0)

            @pl.when(i == 0)
            def _():
                accg_ref[...] = part

            @pl.when(i > 0)
            def _():
                accg_ref[...] += part

            @pl.when(i == ni - 1)
            def _():
                dg_ref[...] = jnp.sum(accg_ref[...], axis=0, keepdims=True)

    row = pl.BlockSpec((tm, d), lambda i, kk: (i, 0))
    one = pl.BlockSpec((1, d), lambda i, kk: (0, 0))
    in_specs = [pl.BlockSpec((tm, tk), lambda i, kk: (i, kk)), pl.BlockSpec((d, tk), lambda i, kk: (0, kk)), row, one, row]
    ops = [a, b, x, gain, res]
    if after is not None:
        in_specs.append(pl.BlockSpec(after.shape, lambda i, kk: (0, 0)))
        ops.append(after)
    return pl.pallas_call(
        body, name=name, grid=(ni, nk), in_specs=in_specs, out_specs=(row, one),
        out_shape=(jax.ShapeDtypeStruct((m, d), F32), jax.ShapeDtypeStruct((1, d), F32)),
        scratch_shapes=[pltpu.VMEM((tm, d), F32), pltpu.VMEM((8, d), F32)],
        compiler_params=_cparams(("arbitrary", "arbitrary")),
    )(*ops)


def _matmul_final_loss(name, a, b, res, target, gain, *, tr=512):
    r, d = res.shape
    k = a.shape[1]
    tr = min(tr, r)
    n = r // tr

    def body(a_ref, b_ref, res_ref, t_ref, g_ref, loss_ref, dh_ref, dg_ref, accl_ref, accg_ref):
        i = pl.program_id(0)
        xv = _bdot(a_ref[...], b_ref[...], _DIMS["nn"]) + res_ref[...]
        rstd = lax.rsqrt(jnp.mean(xv * xv, axis=-1, keepdims=True) + RMS_EPS)
        xh = xv * rstd
        e = xh * g_ref[...] - t_ref[...]
        dyv = e * (1.0 / d)
        dxh = dyv * g_ref[...]
        dh_ref[...] = rstd * (dxh - xh * jnp.mean(dxh * xh, axis=-1, keepdims=True))
        lpart = (e * e).reshape(tr // 8, 8, d).sum(axis=0)
        gpart = (dyv * xh).reshape(tr // 8, 8, d).sum(axis=0)

        @pl.when(i == 0)
        def _():
            accl_ref[...] = lpart
            accg_ref[...] = gpart

        @pl.when(i > 0)
        def _():
            accl_ref[...] += lpart
            accg_ref[...] += gpart

        @pl.when(i == n - 1)
        def _():
            tot = jnp.sum(jnp.sum(accl_ref[...], axis=0, keepdims=True), axis=1, keepdims=True)
            loss_ref[...] = jnp.broadcast_to(tot * (0.5 / d), (1, LANE))
            dg_ref[...] = jnp.sum(accg_ref[...], axis=0, keepdims=True)

    row = pl.BlockSpec((tr, d), lambda i: (i, 0))
    one = pl.BlockSpec((1, d), lambda i: (0, 0))
    return pl.pallas_call(
        body, name=name, grid=(n,),
        in_specs=[pl.BlockSpec((tr, k), lambda i: (i, 0)), pl.BlockSpec((k, d), lambda i: (0, 0)), row, row, one],
        out_specs=(pl.BlockSpec((1, LANE), lambda i: (0, 0)), row, one),
        out_shape=(jax.ShapeDtypeStruct((1, LANE), F32), jax.ShapeDtypeStruct((r, d), F32),
                   jax.ShapeDtypeStruct((1, d), F32)),
        scratch_shapes=[pltpu.VMEM((8, d), F32), pltpu.VMEM((8, d), F32)],
        compiler_params=_cparams(("arbitrary",)),
    )(a, b, res, target, gain)


_GELU_C = math.sqrt(2.0 / math.pi)


def _gelu_parts(z):
    inner = _GELU_C * (z + 0.044715 * z * z * z)
    t = jnp.tanh(inner)
    val = 0.5 * z * (1.0 + t)
    dinner = _GELU_C * (1.0 + 3.0 * 0.044715 * z * z)
    grad = 0.5 * (1.0 + t) + 0.5 * z * (1.0 - t * t) * dinner
    return val, grad


def _ssm_post_fwd(name, y8, u8, d8, *, tr=256):
    r, c = y8.shape
    tr = min(tr, r)

    def body(y_ref, u_ref, d_ref, o_ref):
        z = y_ref[...] + d_ref[...] * u_ref[...]
        o_ref[...] = _gelu_parts(z)[0].astype(o_ref.dtype)

    row = pl.BlockSpec((tr, c), lambda i: (i, 0))
    return pl.pallas_call(
        body, name=name, grid=(r // tr,), in_specs=[row, row, pl.BlockSpec((1, c), lambda i: (0, 0))],
        out_specs=row, out_shape=jax.ShapeDtypeStruct((r, c), BF16),
        compiler_params=_cparams(("parallel",)),
    )(y8, u8, d8)


def _ssm_post_bwd(name, dact8, y8, u8, d8, *, tr=256, after=None):
    r, c = y8.shape
    tr = min(tr, r)
    n = r // tr

    def body(*refs):
        da_ref, y_ref, u_ref, d_ref = refs[:4]
        dz_ref, dd_ref, acc_ref = refs[-3:]
        i = pl.program_id(0)
        uv = u_ref[...]
        z = y_ref[...] + d_ref[...] * uv
        dz = da_ref[...].astype(F32) * _gelu_parts(z)[1]
        dz_ref[...] = dz
        part = (dz * uv).reshape(tr // 8, 8, c).sum(axis=0)

        @pl.when(i == 0)
        def _():
            acc_ref[...] = part

        @pl.when(i > 0)
        def _():
            acc_ref[...] += part

        @pl.when(i == n - 1)
        def _():
            tot = jnp.sum(acc_ref[...], axis=0, keepdims=True)
            out = tot[:, 0:SSM_WIDTH]
            for j in range(1, c // SSM_WIDTH):
                out = out + tot[:, j * SSM_WIDTH:(j + 1) * SSM_WIDTH]
            dd_ref[...] = out

    row = pl.BlockSpec((tr, c), lambda i: (i, 0))
    in_specs = [row, row, row, pl.BlockSpec((1, c), lambda i: (0, 0))]
    ops = [dact8, y8, u8, d8]
    if after is not None:
        in_specs.append(pl.BlockSpec(memory_space=pl.ANY))
        ops.append(after)
    return pl.pallas_call(
        body, name=name, grid=(n,), in_specs=in_specs,
        out_specs=(row, pl.BlockSpec((1, SSM_WIDTH), lambda i: (0, 0))),
        out_shape=(jax.ShapeDtypeStruct((r, c), F32), jax.ShapeDtypeStruct((1, SSM_WIDTH), F32)),
        scratch_shapes=[pltpu.VMEM((8, c), F32)],
        compiler_params=_cparams(("arbitrary",)),
    )(*ops)


def _mix_fwd(name, glu, gates, out_b, *, tr=256):
    r = glu.shape[0]
    d = D_MODEL
    tr = min(tr, r)

    def body(glu_ref, gate_ref, ob_ref, o_ref):
        out_a = glu_ref[:, 0:d].astype(F32) * _sigmoid(glu_ref[:, d:2 * d].astype(F32))
        mix = (_sigmoid(gate_ref[:, 0:d].astype(F32)) * out_a
               + _sigmoid(gate_ref[:, d:2 * d].astype(F32)) * ob_ref[...].astype(F32))
        o_ref[...] = mix.astype(o_ref.dtype)

    wide = pl.BlockSpec((tr, 2 * d), lambda i: (i, 0))
    row = pl.BlockSpec((tr, d), lambda i: (i, 0))
    return pl.pallas_call(
        body, name=name, grid=(r // tr,), in_specs=[wide, wide, row], out_specs=row,
        out_shape=jax.ShapeDtypeStruct((r, d), BF16), compiler_params=_cparams(("parallel",)),
    )(glu, gates, out_b)


def _mix_bwd(name, dh, w_mix, glu, gates, out_b, *, tr=512):
    r = glu.shape[0]
    d = D_MODEL
    tr = min(tr, r)

    def body(dh_ref, w_ref, glu_ref, gate_ref, ob_ref, dglu_ref, dgate_ref, dob_ref):
        dm = _bdot(dh_ref[...], w_ref[...], _DIMS["nt"])
        glu_a = glu_ref[:, 0:d].astype(F32)
        sb = _sigmoid(glu_ref[:, d:2 * d].astype(F32))
        ga = _sigmoid(gate_ref[:, 0:d].astype(F32))
        gb = _sigmoid(gate_ref[:, d:2 * d].astype(F32))
        out_a = glu_a * sb
        dout_a = dm * ga
        dglu_ref[:, 0:d] = (dout_a * sb).astype(dglu_ref.dtype)
        dglu_ref[:, d:2 * d] = (dout_a * glu_a * sb * (1.0 - sb)).astype(dglu_ref.dtype)
        dgate_ref[:, 0:d] = (dm * out_a * ga * (1.0 - ga)).astype(dgate_ref.dtype)
        dgate_ref[:, d:2 * d] = (dm * ob_ref[...].astype(F32) * gb * (1.0 - gb)).astype(dgate_ref.dtype)
        dob_ref[...] = (dm * gb).astype(dob_ref.dtype)

    wide = pl.BlockSpec((tr, 2 * d), lambda i: (i, 0))
    row = pl.BlockSpec((tr, d), lambda i: (i, 0))
    return pl.pallas_call(
        body, name=name, grid=(r // tr,),
        in_specs=[row, pl.BlockSpec((d, d), lambda i: (0, 0)), wide, wide, row], out_specs=(wide, wide, row),
        out_shape=(jax.ShapeDtypeStruct((r, 2 * d), BF16), jax.ShapeDtypeStruct((r, 2 * d), BF16),
                   jax.ShapeDtypeStruct((r, d), BF16)),
        compiler_params=_cparams(("parallel",)),
    )(dh, w_mix, glu, gates, out_b)


def _ssm_mats(lam_re, lam_im, log_dt, b_re, b_im, c_re, c_im, nc):
    hp = lax.Precision.HIGHEST
    t = SSM_CHUNK
    nq = SSM_GROUPS // 8
    lam = lax.complex(lam_re, lam_im)
    z = lam * jnp.exp(log_dt)[:, None]
    ks = jnp.arange(t + 1, dtype=F32)
    apow = jnp.exp(ks[:, None, None] * z[None])
    bbar = ((apow[1] - 1.0) / lam)[..., None] * lax.complex(b_re, b_im)
    c = lax.complex(c_re, c_im)

    ca = c[None] * apow[:, :, None, :]
    kmat = jnp.einsum("kgnp,gpm->kgnm", ca, bbar, precision=hp).real
    ii = np.arange(t)
    lag = ii[None, :] - ii[:, None]
    kt = kmat[np.clip(lag, 0, t)] * jnp.asarray(lag >= 0, F32)[:, :, None, None, None]
    kt = kt.reshape(t, t, nq, 8, SSM_GROUP, SSM_GROUP)
    m_c = kt.transpose(2, 0, 3, 5, 1, 4).reshape(nq, 1024, LANE)

    arev = jnp.exp((float(t - 1) - ks[:t])[:, None, None] * z[None])
    w = arev[:, :, :, None] * bbar[None]
    wr = jnp.stack([w.real, w.imag]).reshape(2, t, nq, 8, SSM_STATE, SSM_GROUP)
    bw_c = wr.transpose(2, 1, 3, 5, 0, 4).reshape(nq, 1024, LANE)

    ca1 = ca[1:]
    cr = jnp.stack([ca1.real, -ca1.imag]).reshape(2, t, nq, 8, SSM_GROUP, SSM_STATE)
    cm_c = cr.transpose(2, 0, 3, 5, 1, 4).reshape(nq, 1024, LANE)

    def tiles(v):
        vq = jnp.concatenate([v.real.reshape(nq, 512), v.imag.reshape(nq, 512)], axis=1)
        return jnp.broadcast_to(vq.reshape(nq, 8, 1, LANE), (nq, 8, 8, LANE))

    return m_c, bw_c, cm_c, tiles(apow[t]), tiles(jnp.exp(float(nc) * z))


_BD_M = (LANE, SSM_GROUP)
_BD_BW = (LANE, SSM_STATE)
_BD_CM = (512, SSM_GROUP)


def _bd_perm(cn):
    rr = lax.broadcasted_iota(jnp.int32, (1024, 1024), 0)
    cc = lax.broadcasted_iota(jnp.int32, (1024, 1024), 1)
    sh = cn.bit_length() - 1
    src = ((rr >> 7) << sh) + (((rr & (LANE - 1)) >> sh) << (3 + sh)) + (rr & (cn - 1))
    return jnp.where(src == cc, 1.0, 0.0).astype(BF16)


def _bd_rowgroup(span):
    r = lax.broadcasted_iota(jnp.int32, (1024, LANE), 0)
    return (r & (span - 1)) >> ((span // 8).bit_length() - 1)


def _bd_expand(name, kind, compact):
    span, cn = kind
    nq = compact.shape[0]

    def body(c_ref, o_ref, perm_scr):
        @pl.when(pl.program_id(0) == 0)
        def _():
            perm_scr[...] = _bd_perm(cn)

        x = c_ref[...]
        grp = _bd_rowgroup(span)
        xcat = jnp.concatenate([jnp.where(grp == h, x, 0.0) for h in range(8)], axis=1)
        o_ref[...] = _bdot(xcat, perm_scr[...], _DIMS["nn"]).astype(o_ref.dtype)

    return pl.pallas_call(
        body, name=name, grid=(nq,), in_specs=[pl.BlockSpec((None, 1024, LANE), lambda q: (q, 0, 0))],
        out_specs=pl.BlockSpec((None, 1024, 1024), lambda q: (q, 0, 0)),
        out_shape=jax.ShapeDtypeStruct((nq, 1024, 1024), BF16),
        scratch_shapes=[pltpu.VMEM((1024, 1024), BF16)],
        compiler_params=_cparams(("arbitrary",)),
    )(compact)


def _bd_reduce(name, kind, dbig):
    span, cn = kind
    nq = dbig.shape[0]

    def body(g_ref, o_ref, perm_scr):
        @pl.when(pl.program_id(0) == 0)
        def _():
            perm_scr[...] = _bd_perm(cn)

        back = _bdot(g_ref[...], perm_scr[...], _DIMS["nt"])
        grp = _bd_rowgroup(span)
        out = jnp.zeros((1024, LANE), F32)
        for h in range(8):
            out = jnp.where(grp == h, back[:, h * LANE:(h + 1) * LANE], out)
        o_ref[...] = out

    return pl.pallas_call(
        body, name=name, grid=(nq,), in_specs=[pl.BlockSpec((None, 1024, 1024), lambda q: (q, 0, 0))],
        out_specs=pl.BlockSpec((None, 1024, LANE), lambda q: (q, 0, 0)),
        out_shape=jax.ShapeDtypeStruct((nq, 1024, LANE), F32),
        scratch_shapes=[pltpu.VMEM((1024, 1024), BF16)],
        compiler_params=_cparams(("arbitrary",)),
    )(dbig)


def _x_tile_specs(nc, nq):
    return [pl.BlockSpec((nc, LANE), lambda q, t, i=i: (0, i * nq + q)) for i in range(SSM_CHUNK)]


def _cat_tiles(refs):
    return jnp.concatenate([r[...] for r in refs], axis=1)


def _ssm_w(name, x8, bw):
    nc = x8.shape[0]
    nq = bw.shape[0]

    def body(*refs):
        xq = _cat_tiles(refs[:8])
        refs[9][...] = _bdot(xq, refs[8][...], _DIMS["nn"])

    return pl.pallas_call(
        body, name=name, grid=(nq, 8),
        in_specs=_x_tile_specs(nc, nq) + [pl.BlockSpec((None, 1024, LANE), lambda q, t: (q, 0, t))],
        out_specs=pl.BlockSpec((None, None, nc, LANE), lambda q, t: (q, t, 0, 0)),
        out_shape=jax.ShapeDtypeStruct((nq, 8, nc, LANE), F32),
        compiler_params=_cparams(("parallel", "arbitrary")),
    )(*([x8] * 8), bw)


def _ssm_scan(name, w4, a_t, aseg_t, *, reverse, sprev4=None):
    nq, _, nc, _ = w4.shape
    ns = nc // 8
    with_da = sprev4 is not None

    def body(*refs):
        w_ref, a_ref, aseg_ref = refs[:3]
        s_ref = refs[3] if with_da else None
        o_ref = refs[4] if with_da else refs[3]
        da_ref = refs[5] if with_da else None
        sgn = -1.0 if reverse else 1.0
        ar = [a_ref[j] for j in range(4)]
        ai = [sgn * a_ref[j + 4] for j in range(4)]
        gr = [aseg_ref[j] for j in range(4)]
        gi = [sgn * aseg_ref[j + 4] for j in range(4)]
        zero = tuple(jnp.zeros((8, LANE), F32) for _ in range(8))

        def rows(tt):
            return pl.ds((ns - 1 - tt) if reverse else tt, 8, stride=ns)

        def step(carry, w):
            new_r = [ar[j] * carry[j] - ai[j] * carry[j + 4] + w[j] for j in range(4)]
            new_i = [ar[j] * carry[j + 4] + ai[j] * carry[j] + w[j + 4] for j in range(4)]
            return tuple(new_r + new_i)

        def pass1(tt, carry):
            return step(carry, [w_ref[j, rows(tt), :] for j in range(8)])

        ends = lax.fori_loop(0, ns, pass1, zero)
        sub = lax.broadcasted_iota(jnp.int32, (8, LANE), 0)
        init = list(zero)
        order = range(7, 0, -1) if reverse else range(0, 7)
        for s in order:
            nxt = s - 1 if reverse else s + 1
            cand_r = [gr[j] * init[j] - gi[j] * init[j + 4] + ends[j] for j in range(4)]
            cand_i = [gr[j] * init[j + 4] + gi[j] * init[j] + ends[j + 4] for j in range(4)]
            cand = cand_r + cand_i
            shift = 7 if reverse else 1
            init = [jnp.where(sub == nxt, pltpu.roll(cand[j], shift, axis=0), init[j]) for j in range(8)]

        def pass2(tt, state):
            carry, acc = state
            r = rows(tt)
            for j in range(8):
                o_ref[j, r, :] = carry[j]
            if with_da:
                sp = [s_ref[j, r, :] for j in range(8)]
                acc_r = [acc[j] + carry[j] * sp[j] + carry[j + 4] * sp[j + 4] for j in range(4)]
                acc_i = [acc[j + 4] + carry[j + 4] * sp[j] - carry[j] * sp[j + 4] for j in range(4)]
                acc = tuple(acc_r + acc_i)
            return step(carry, [w_ref[j, r, :] for j in range(8)]), acc

        _, acc = lax.fori_loop(0, ns, pass2, (tuple(init), zero))
        if with_da:
            for j in range(8):
                da_ref[j] = acc[j]

    big = pl.BlockSpec((None, 8, nc, LANE), lambda q: (q, 0, 0, 0))
    small = pl.BlockSpec((None, 8, 8, LANE), lambda q: (q, 0, 0, 0))
    in_specs = [big, small, small] + ([big] if with_da else [])
    ops = [w4, a_t, aseg_t] + ([sprev4] if with_da else [])
    out_specs = (big, small) if with_da else big
    big_s = jax.ShapeDtypeStruct((nq, 8, nc, LANE), F32)
    out_shape = (big_s, jax.ShapeDtypeStruct((nq, 8, 8, LANE), F32)) if with_da else big_s
    return pl.pallas_call(
        body, name=name, grid=(nq,), in_specs=in_specs, out_specs=out_specs, out_shape=out_shape,
        compiler_params=_cparams(("parallel",)),
    )(*ops)


def _ssm_y(name, x8, sprev4, m_mat, cm_mat):
    nc = x8.shape[0]
    nq = m_mat.shape[0]

    def body(*refs):
        xq = _cat_tiles(refs[:8])
        s_ref, m_ref, cm_ref, o_ref = refs[8:12]
        sq = jnp.concatenate([s_ref[t] for t in range(8)], axis=1)
        o_ref[...] = _bdot(xq, m_ref[...], _DIMS["nn"]) + _bdot(sq, cm_ref[...], _DIMS["nn"])

    col = pl.BlockSpec((None, 1024, LANE), lambda q, j: (q, 0, j))
    return pl.pallas_call(
        body, name=name, grid=(nq, 8),
        in_specs=_x_tile_specs(nc, nq) + [pl.BlockSpec((None, 8, nc, LANE), lambda q, j: (q, 0, 0, 0)), col, col],
        out_specs=pl.BlockSpec((nc, LANE), lambda q, j: (0, j * nq + q)),
        out_shape=jax.ShapeDtypeStruct((nc, 8 * SSM_WIDTH), F32),
        compiler_params=_cparams(("parallel", "arbitrary")),
    )(*([x8] * 8), sprev4, m_mat, cm_mat)


def _ssm_ds(name, dz8, sprev4, cm_mat):
    nc = dz8.shape[0]
    nq = cm_mat.shape[0]

    def body(*refs):
        dyq = _cat_tiles(refs[:8]).astype(BF16)
        s_ref, cm_ref, ds_ref, dcm_ref = refs[8:12]
        ds_ref[...] = _bdot(dyq, cm_ref[...], _DIMS["nt"])
        dcm_ref[...] = _bdot(s_ref[...], dyq, _DIMS["tn"])

    tile = pl.BlockSpec((None, None, nc, LANE), lambda q, t: (q, t, 0, 0))
    rowblk = pl.BlockSpec((None, LANE, 1024), lambda q, t: (q, t, 0))
    return pl.pallas_call(
        body, name=name, grid=(nq, 8),
        in_specs=_x_tile_specs(nc, nq) + [tile, rowblk],
        out_specs=(tile, rowblk),
        out_shape=(jax.ShapeDtypeStruct((nq, 8, nc, LANE), F32), jax.ShapeDtypeStruct((nq, 1024, 1024), F32)),
        compiler_params=_cparams(("parallel", "arbitrary")),
    )(*([dz8] * 8), sprev4, cm_mat)


def _ssm_dx(name, dz8, g4, x8, m_mat, bw_mat, d8):
    nc = dz8.shape[0]
    nq = m_mat.shape[0]

    def body(*refs):
        dyq = _cat_tiles(refs[:8]).astype(BF16)
        g_ref, x_ref, m_ref, bw_ref, d_ref, dzi_ref, dx_ref, dm_ref, dbw_ref = refs[8:17]
        gq = jnp.concatenate([g_ref[t] for t in range(8)], axis=1).astype(BF16)
        dx = _bdot(dyq, m_ref[...], _DIMS["nt"]) + _bdot(gq, bw_ref[...], _DIMS["nt"])
        dx_ref[...] = (dx + d_ref[...] * dzi_ref[...]).astype(dx_ref.dtype)
        xi = x_ref[...]
        dm_ref[...] = _bdot(xi, dyq, _DIMS["tn"])
        dbw_ref[...] = _bdot(xi, gq, _DIMS["tn"])

    xtile = pl.BlockSpec((nc, LANE), lambda q, i: (0, i * nq + q))
    rowblk = pl.BlockSpec((None, LANE, 1024), lambda q, i: (q, i, 0))
    return pl.pallas_call(
        body, name=name, grid=(nq, 8),
        in_specs=_x_tile_specs(nc, nq) + [pl.BlockSpec((None, 8, nc, LANE), lambda q, i: (q, 0, 0, 0)), xtile, rowblk, rowblk,
                                          pl.BlockSpec((1, LANE), lambda q, i: (0, q)), xtile],
        out_specs=(xtile, rowblk, rowblk),
        out_shape=(jax.ShapeDtypeStruct((nc, 8 * SSM_WIDTH), BF16), jax.ShapeDtypeStruct((nq, 1024, 1024), F32),
                   jax.ShapeDtypeStruct((nq, 1024, 1024), F32)),
        compiler_params=_cparams(("parallel", "arbitrary")),
    )(*([dz8] * 8), g4, x8, m_mat, bw_mat, d8, dz8)


CUM_BLK = 256


def _split3(x):
    hi = x.astype(BF16)
    r1 = x - hi.astype(F32)
    mid = r1.astype(BF16)
    lo = (r1 - mid.astype(F32)).astype(BF16)
    return hi, mid, lo


def _tri_dot(x, tri):
    hi, mid, lo = _split3(x)
    d = _DIMS["nn"]
    return _bdot(hi, tri, d) + _bdot(mid, tri, d) + _bdot(lo, tri, d)


def _tri(n, lower):
    r = lax.broadcasted_iota(jnp.int32, (n, n), 0)
    c = lax.broadcasted_iota(jnp.int32, (n, n), 1)
    return jnp.where((r >= c) if lower else (r <= c), 1.0, 0.0).astype(BF16)


def _fox_cum(name, fproj, bcol):
    seq = fproj.shape[0]
    blk = min(CUM_BLK, seq)

    def body(f_ref, b_ref, o_ref, carry_ref):
        i = pl.program_id(0)

        @pl.when(i == 0)
        def _():
            carry_ref[...] = jnp.zeros_like(carry_ref)

        z = f_ref[...].T + b_ref[...]
        logf = jnp.minimum(z, 0.0) - jnp.log(1.0 + jnp.exp(-jnp.abs(z)))
        carry = carry_ref[...]
        cum = _tri_dot(logf, _tri(blk, lower=False)) + jnp.tile(carry, (1, blk // LANE))
        o_ref[...] = cum[0:8, :]
        carry_ref[...] = carry + jnp.sum(logf, axis=1, keepdims=True)

    return pl.pallas_call(
        body, name=name, grid=(seq // blk,),
        in_specs=[pl.BlockSpec((blk, LANE), lambda i: (i, 0)), pl.BlockSpec((LANE, 1), lambda i: (0, 0))],
        out_specs=pl.BlockSpec((8, blk), lambda i: (0, i)),
        out_shape=jax.ShapeDtypeStruct((8, seq), F32),
        scratch_shapes=[pltpu.VMEM((LANE, LANE), F32)],
        compiler_params=_cparams(("arbitrary",)),
    )(fproj, bcol)


def _fox_cum_bwd(name, dcs, fproj, bcol):
    seq = fproj.shape[0]
    blk = min(CUM_BLK, seq)
    n = seq // blk

    def body(dc_ref, f_ref, b_ref, df_ref, db_ref, carry_ref, acc_ref):
        i = pl.program_id(0)

        @pl.when(i == 0)
        def _():
            carry_ref[...] = jnp.zeros_like(carry_ref)
            acc_ref[...] = jnp.zeros_like(acc_ref)

        r = lax.broadcasted_iota(jnp.int32, (LANE, FOX_WIDTH), 0)
        c = lax.broadcasted_iota(jnp.int32, (LANE, FOX_WIDTH), 1)
        want = (r >> 1) * LANE + jnp.where((r & 1) == 0, FOX_HEAD_DIM, 0)
        sel = jnp.where(jnp.logical_and(r < FOX_HEADS, c == want), 1.0, 0.0).astype(BF16)
        hi, mid, lo = _split3(dc_ref[...])
        nt = _DIMS["nt"]
        dc = _bdot(sel, hi, nt) + _bdot(sel, mid, nt) + _bdot(sel, lo, nt)
        carry = carry_ref[...]
        dlogf = _tri_dot(dc, _tri(blk, lower=True)) + jnp.tile(carry, (1, blk // LANE))
        carry_ref[...] = carry + jnp.sum(dc, axis=1, keepdims=True)
        z = f_ref[...].T + b_ref[...]
        dft = dlogf / (1.0 + jnp.exp(z))
        df_ref[...] = dft.T.astype(df_ref.dtype)
        acc_ref[...] += jnp.sum(dft, axis=1, keepdims=True)

        @pl.when(i == n - 1)
        def _():
            db_ref[...] = acc_ref[...]

    return pl.pallas_call(
        body, name=name, grid=(n,),
        in_specs=[pl.BlockSpec((blk, FOX_WIDTH), lambda i: (n - 1 - i, 0)), pl.BlockSpec((blk, LANE), lambda i: (n - 1 - i, 0)),
                  pl.BlockSpec((LANE, 1), lambda i: (0, 0))],
        out_specs=(pl.BlockSpec((blk, LANE), lambda i: (n - 1 - i, 0)), pl.BlockSpec((LANE, LANE), lambda i: (0, 0))),
        out_shape=(jax.ShapeDtypeStruct((seq, LANE), BF16), jax.ShapeDtypeStruct((LANE, LANE), F32)),
        scratch_shapes=[pltpu.VMEM((LANE, LANE), F32), pltpu.VMEM((LANE, LANE), F32)],
        compiler_params=_cparams(("arbitrary",)),
    )(dcs, fproj, bcol)


FOX_BLK = 512
FOX_SCALE = FOX_HEAD_DIM ** -0.5


def _fox_head_mask(shape, hh):
    lane = lax.broadcasted_iota(jnp.int32, shape, 1)
    return (lane < FOX_HEAD_DIM) if hh == 0 else (lane >= FOX_HEAD_DIM)


def _fox_bias(cum_ref, hh, q0, k0, blk):
    c0 = jnp.max(cum_ref[hh:hh + 1, pl.ds(q0, LANE)], axis=1, keepdims=True)
    return c0 - cum_ref[hh:hh + 1, pl.ds(k0, blk)]


def _fox_fwd(name, qkv, cum_t):
    seq = qkv.shape[0]
    blk = min(FOX_BLK, seq)
    nb = seq // blk
    npair = FOX_HEADS // 2

    def body(q_ref, k_ref, v_ref, cum_ref, o_ref, lse_ref):
        iq = pl.program_id(1)
        q0 = pl.multiple_of(iq * blk, blk)
        qv = q_ref[...]
        row = lax.broadcasted_iota(jnp.int32, (blk, blk), 0)
        col = lax.broadcasted_iota(jnp.int32, (blk, blk), 1)
        qhs = [jnp.where(_fox_head_mask(qv.shape, hh), qv, jnp.zeros_like(qv)) * FOX_SCALE for hh in range(2)]

        def block(kb, states, masked):
            k0 = pl.multiple_of(kb * blk, blk)
            kv = k_ref[pl.ds(k0, blk), :]
            vv = v_ref[pl.ds(k0, blk), :]
            new = []
            for hh in range(2):
                m, acc = states[hh]
                s = _bdot(qhs[hh], kv, _DIMS["nt"]) + _fox_bias(cum_ref, hh, q0, k0, blk)
                if masked:
                    s = jnp.where(row >= col, s, -jnp.inf)
                m_new = jnp.maximum(m, jnp.max(s, axis=1, keepdims=True))
                p = jnp.exp(s - m_new)
                vh = jnp.where(_fox_head_mask(vv.shape, hh), vv, jnp.ones_like(vv))
                acc = jnp.exp(m - m_new) * acc + _bdot(p, vh, _DIMS["nn"])
                new.append((m_new, acc))
            return tuple(new)

        init = (jnp.full((blk, 1), -jnp.inf, F32), jnp.zeros((blk, LANE), F32))
        states = lax.fori_loop(0, iq, lambda kb, st: block(kb, st, False), (init, init))
        states = block(iq, states, True)
        outs = []
        for hh in range(2):
            m, acc = states[hh]
            other = pltpu.roll(acc, FOX_HEAD_DIM, axis=1)
            outs.append(acc / other)
            lse_ref[hh] = m + jnp.log(jnp.where(_fox_head_mask(acc.shape, hh), other, acc))
        o_ref[...] = jnp.where(_fox_head_mask(outs[0].shape, 0), outs[0], outs[1]).astype(o_ref.dtype)

    return pl.pallas_call(
        body, name=name, grid=(npair, nb),
        in_specs=[pl.BlockSpec((blk, LANE), lambda p, i: (i, p)),
                  pl.BlockSpec((seq, LANE), lambda p, i: (0, npair + p)),
                  pl.BlockSpec((seq, LANE), lambda p, i: (0, 2 * npair + p)),
                  pl.BlockSpec((None, 2, seq), lambda p, i: (p, 0, 0))],
        out_specs=(pl.BlockSpec((blk, LANE), lambda p, i: (i, p)),
                   pl.BlockSpec((2, blk, LANE), lambda p, i: (p, i, 0))),
        out_shape=(jax.ShapeDtypeStruct((seq, FOX_WIDTH), BF16), jax.ShapeDtypeStruct((FOX_HEADS, seq, LANE), F32)),
        compiler_params=_cparams(("parallel", "arbitrary")),
    )(qkv, qkv, qkv, cum_t)


def _fox_bwd(name, qkv, cum_t, att, datt, lse):
    seq = qkv.shape[0]
    blk = min(FOX_BLK, seq)
    nb = seq // blk
    npair = FOX_HEADS // 2

    def body(q_ref, k_ref, v_ref, cum_ref, o_ref, do_ref, lse_ref, dq_ref, dk_ref, dv_ref, dcs_ref):
        iq = pl.program_id(1)
        q0 = pl.multiple_of(iq * blk, blk)

        @pl.when(iq == 0)
        def _():
            dk_ref[...] = jnp.zeros_like(dk_ref)
            dv_ref[...] = jnp.zeros_like(dv_ref)
            dcs_ref[...] = jnp.zeros_like(dcs_ref)

        qv = q_ref[...]
        dov = do_ref[...].astype(F32)
        ov = o_ref[...].astype(F32)
        row = lax.broadcasted_iota(jnp.int32, (blk, blk), 0)
        col = lax.broadcasted_iota(jnp.int32, (blk, blk), 1)
        low = _fox_head_mask((blk, LANE), 0)
        qhs, qones, dohbs, deltas, lses = [], [], [], [], []
        for hh in range(2):
            hm = _fox_head_mask(qv.shape, hh)
            qh = jnp.where(hm, qv, jnp.zeros_like(qv)) * FOX_SCALE
            qhs.append(qh)
            qones.append(jnp.where(hm, qh, jnp.ones_like(qh)))
            doh = jnp.where(hm, dov, 0.0)
            dohbs.append(doh.astype(BF16))
            deltas.append(jnp.sum(doh * ov, axis=1, keepdims=True))
            lses.append(jnp.tile(lse_ref[hh], (1, blk // LANE)))

        def block(kb, dqs, masked):
            k0 = pl.multiple_of(kb * blk, blk)
            kv = k_ref[pl.ds(k0, blk), :]
            vv = v_ref[pl.ds(k0, blk), :]
            new, dks, dvs = [], [], []
            for hh in range(2):
                s = _bdot(qhs[hh], kv, _DIMS["nt"]) + _fox_bias(cum_ref, hh, q0, k0, blk)
                p = jnp.exp(s - lses[hh])
                if masked:
                    p = jnp.where(row >= col, p, 0.0)
                dp = _bdot(dohbs[hh], vv, _DIMS["nt"])
                dsb = (p * (dp - deltas[hh])).astype(BF16)
                dks.append(_bdot(dsb, qones[hh], _DIMS["tn"]))
                dvs.append(_bdot(p, dohbs[hh], _DIMS["tn"]))
                kones = jnp.where(_fox_head_mask(kv.shape, hh), kv, jnp.ones_like(kv))
                new.append(dqs[hh] + _bdot(dsb, kones, _DIMS["nn"]))
            dk_ref[pl.ds(k0, blk), :] += jnp.where(low, dks[0], dks[1])
            dv_ref[pl.ds(k0, blk), :] += dvs[0] + dvs[1]
            dcs_ref[pl.ds(k0, blk), :] -= jnp.where(low, dks[1], dks[0])
            return tuple(new)

        init = jnp.zeros((blk, LANE), F32)
        dqs = lax.fori_loop(0, iq, lambda kb, a: block(kb, a, False), (init, init))
        dqs = block(iq, dqs, True)
        dcs_ref[pl.ds(q0, blk), :] += jnp.where(low, dqs[1], dqs[0])
        dq_ref[...] = (jnp.where(low, dqs[0], dqs[1]) * FOX_SCALE).astype(dq_ref.dtype)

    qblk = pl.BlockSpec((blk, LANE), lambda p, i: (i, p))
    full = pl.BlockSpec((seq, LANE), lambda p, i: (0, p))
    return pl.pallas_call(
        body, name=name, grid=(npair, nb),
        in_specs=[qblk,
                  pl.BlockSpec((seq, LANE), lambda p, i: (0, npair + p)),
                  pl.BlockSpec((seq, LANE), lambda p, i: (0, 2 * npair + p)),
                  pl.BlockSpec((None, 2, seq), lambda p, i: (p, 0, 0)),
                  qblk, qblk,
                  pl.BlockSpec((2, blk, LANE), lambda p, i: (p, i, 0))],
        out_specs=(qblk, full, full, full),
        out_shape=(jax.ShapeDtypeStruct((seq, FOX_WIDTH), BF16), jax.ShapeDtypeStruct((seq, FOX_WIDTH), F32),
                   jax.ShapeDtypeStruct((seq, FOX_WIDTH), F32), jax.ShapeDtypeStruct((seq, FOX_WIDTH), F32)),
        compiler_params=_cparams(("arbitrary", "arbitrary")),
    )(qkv, qkv, qkv, cum_t, att, datt, lse)


MEM_SCALE = MEM_HEAD_DIM ** -0.5


def _mem_probs(qh, kh):
    s = _bdot(qh, kh, _DIMS["nt"]) * MEM_SCALE
    p = jnp.exp(s - jnp.max(s, axis=1, keepdims=True))
    return p / jnp.sum(p, axis=1, keepdims=True)


def _mem_fwd(name, q2, kv, *, tr=512):
    seq = q2.shape[0]
    mlen = kv.shape[0]
    tr = min(tr, seq)

    def body(q_ref, kv_ref, o_ref):
        for h in range(MEM_HEADS):
            sl = slice(h * MEM_HEAD_DIM, (h + 1) * MEM_HEAD_DIM)
            sv = slice(MEM_WIDTH + h * MEM_HEAD_DIM, MEM_WIDTH + (h + 1) * MEM_HEAD_DIM)
            p = _mem_probs(q_ref[:, sl], kv_ref[:, sl])
            o_ref[:, sl] = _bdot(p, kv_ref[:, sv], _DIMS["nn"]).astype(o_ref.dtype)

    return pl.pallas_call(
        body, name=name, grid=(seq // tr,),
        in_specs=[pl.BlockSpec((tr, MEM_WIDTH), lambda i: (i, 0)), pl.BlockSpec((mlen, 2 * MEM_WIDTH), lambda i: (0, 0))],
        out_specs=pl.BlockSpec((tr, MEM_WIDTH), lambda i: (i, 0)),
        out_shape=jax.ShapeDtypeStruct((seq, MEM_WIDTH), BF16),
        compiler_params=_cparams(("parallel",)),
    )(q2, kv)


def _mem_bwd(name, q2, kv, do2, *, tr=512):
    seq = q2.shape[0]
    mlen = kv.shape[0]
    tr = min(tr, seq)

    def body(q_ref, kv_ref, do_ref, dq_ref, dkv_ref):
        i = pl.program_id(0)

        @pl.when(i == 0)
        def _():
            dkv_ref[...] = jnp.zeros_like(dkv_ref)

        for h in range(MEM_HEADS):
            sl = slice(h * MEM_HEAD_DIM, (h + 1) * MEM_HEAD_DIM)
            sv = slice(MEM_WIDTH + h * MEM_HEAD_DIM, MEM_WIDTH + (h + 1) * MEM_HEAD_DIM)
            qh = q_ref[:, sl]
            kh = kv_ref[:, sl]
            doh = do_ref[:, sl].astype(BF16)
            p = _mem_probs(qh, kh)
            dp = _bdot(doh, kv_ref[:, sv], _DIMS["nt"])
            ds = (p * (dp - jnp.sum(p * dp, axis=1, keepdims=True)) * MEM_SCALE).astype(BF16)
            dq_ref[:, sl] = _bdot(ds, kh, _DIMS["nn"]).astype(dq_ref.dtype)
            dkv_ref[:, sl] += _bdot(ds, qh, _DIMS["tn"])
            dkv_ref[:, sv] += _bdot(p, doh, _DIMS["tn"])

    row = pl.BlockSpec((tr, MEM_WIDTH), lambda i: (i, 0))
    kvs = pl.BlockSpec((mlen, 2 * MEM_WIDTH), lambda i: (0, 0))
    return pl.pallas_call(
        body, name=name, grid=(seq // tr,), in_specs=[row, kvs, row], out_specs=(row, kvs),
        out_shape=(jax.ShapeDtypeStruct((seq, MEM_WIDTH), BF16), jax.ShapeDtypeStruct((mlen, 2 * MEM_WIDTH), F32)),
        compiler_params=_cparams(("arbitrary",)),
    )(q2, kv, do2)


_HBM = pl.BlockSpec(memory_space=pl.ANY)
_HBM_ONLY = pl.BlockSpec(memory_space=pltpu.HBM)
_MESH = pl.DeviceIdType.MESH


def _mesh_place():
    x, y, c = lax.axis_index("x"), lax.axis_index("y"), lax.axis_index("c")
    other_chips = [(1 - x, y), (x, 1 - y), (1 - x, 1 - y)]
    return x, y, c, other_chips


def _gather_all(name, arrays):
    n = len(arrays)

    def body(*refs):
        ins, outs = refs[:n], refs[n:2 * n]
        send_sems, recv_sems, local_sems = refs[2 * n:]
        x, y, c, chips = _mesh_place()
        me, sibling = (x, y, c), (x, y, 1 - c)

        def slot(a, place):
            px, py, pc = place
            return outs[a].at[4 * px + 2 * py + pc]

        def copy(a, k, block, to, src=None):
            return pltpu.make_async_remote_copy(
                src_ref=slot(a, block) if src is None else src, dst_ref=slot(a, block),
                send_sem=send_sems.at[a, k], recv_sem=recv_sems.at[a, k], device_id=to, device_id_type=_MESH)

        mine = [pltpu.make_async_copy(ins[a], slot(a, me), local_sems.at[a]) for a in range(n)]
        for cp in mine:
            cp.start()
        first = []
        for a in range(n):
            first.append(copy(a, 0, me, sibling, src=ins[a]))
            first += [copy(a, 1 + j, me, (*chip, c), src=ins[a]) for j, chip in enumerate(chips)]
        for cp in first:
            cp.start()
        passed = []
        for j, chip in enumerate(chips):
            for a in range(n):
                copy(a, 1 + j, (*chip, c), me).wait_recv()
                fwd = copy(a, 4 + j, (*chip, c), sibling)
                fwd.start()
                passed.append(fwd)
        for a in range(n):
            copy(a, 0, sibling, me).wait_recv()
            for j, chip in enumerate(chips):
                copy(a, 4 + j, (*chip, 1 - c), me).wait_recv()
        for cp in first + passed:
            cp.wait_send()
        for cp in mine:
            cp.wait()

    out_shape = tuple(jax.ShapeDtypeStruct((N_DEV,) + arr.shape, arr.dtype) for arr in arrays)
    return pl.pallas_call(
        body, name=name, in_specs=[_HBM] * n, out_specs=tuple([_HBM] * n), out_shape=out_shape,
        scratch_shapes=[pltpu.SemaphoreType.DMA((n, N_DEV - 1)), pltpu.SemaphoreType.DMA((n, N_DEV - 1)),
                        pltpu.SemaphoreType.DMA((n,))],
    )(*arrays)


_SEM = pl.BlockSpec(memory_space=pltpu.SEMAPHORE)
_DATAFLOW = pltpu.SideEffectType.DATAFLOW_SIDE_EFFECTING


def _device_index():
    return (4 * lax.axis_index("x") + 2 * lax.axis_index("y") + lax.axis_index("c")).astype(jnp.int32).reshape(1)


def _place_own(name, pieces, *, stacked_src, after=None):
    n = len(pieces)
    n_in = n + (after is not None)

    def body(me_ref, *refs):
        for a in range(n):
            refs[n_in + a][...] = refs[a][...]

    def spec(shape):
        return pl.BlockSpec((None,) + tuple(shape), lambda i, me_ref: (me_ref[0],) + (0,) * len(shape))

    shapes = [p.shape[1:] if stacked_src else p.shape for p in pieces]
    if stacked_src:
        in_specs = [spec(s) for s in shapes]
    else:
        in_specs = [pl.BlockSpec(tuple(s), lambda i, me_ref, nd=len(s): (0,) * nd) for s in shapes]
    operands = list(pieces)
    if after is not None:
        in_specs.append(_HBM)
        operands.append(after)
    return pl.pallas_call(
        body, name=name,
        grid_spec=pltpu.PrefetchScalarGridSpec(num_scalar_prefetch=1, grid=(1,), in_specs=in_specs,
                                               out_specs=tuple(spec(s) for s in shapes)),
        out_shape=tuple(jax.ShapeDtypeStruct((N_DEV,) + tuple(s), p.dtype) for s, p in zip(shapes, pieces)),
        compiler_params=_cparams(("arbitrary",)),
    )(_device_index(), *operands)


def _peer_places():
    x, y, c = lax.axis_index("x"), lax.axis_index("y"), lax.axis_index("c")
    peers = []
    for k in range(N_DEV - 1):
        flip = k + 1
        px = 1 - x if flip & 4 else x
        py = 1 - y if flip & 2 else y
        pc = 1 - c if flip & 1 else c
        peers.append((px, py, pc, 4 * px + 2 * py + pc))
    return 4 * x + 2 * y + c, peers


def _direct_copy(srcs, lands, send_sems, recv_sems, a, k, me, peer, scatter):
    px, py, pc, pidx = peer
    return pltpu.make_async_remote_copy(
        src_ref=srcs[a].at[pidx] if scatter else srcs[a], dst_ref=lands[a].at[me],
        send_sem=send_sems.at[a * (N_DEV - 1) + k], recv_sem=recv_sems.at[a * (N_DEV - 1) + k],
        device_id=(px, py, pc), device_id_type=_MESH)


def _send_start(name, srcs, lands, *, scatter):
    n = len(srcs)

    def body(*refs):
        src_refs, land_refs = refs[:n], refs[n:2 * n]
        send_sems, recv_sems = refs[2 * n], refs[2 * n + 1]
        token = refs[-1]
        me, peers = _peer_places()
        for k, peer in enumerate(peers):
            for a in range(n):
                _direct_copy(src_refs, land_refs, send_sems, recv_sems, a, k, me, peer, scatter).start()
        token[...] = jnp.zeros_like(token)

    hbm_shapes = [pltpu.HBM(t.shape, t.dtype) for t in list(srcs) + list(lands)]
    outs = pl.pallas_call(
        body, name=name,
        out_shape=(pltpu.SemaphoreType.DMA((n * (N_DEV - 1),)), pltpu.SemaphoreType.DMA((n * (N_DEV - 1),)), *hbm_shapes,
                   jax.ShapeDtypeStruct((8, LANE), F32)),
        in_specs=[_HBM_ONLY] * (2 * n),
        out_specs=(_SEM, _SEM, *([_HBM_ONLY] * (2 * n)), pl.BlockSpec(memory_space=pltpu.VMEM)),
        input_output_aliases={i: 2 + i for i in range(2 * n)},
        compiler_params=pltpu.CompilerParams(has_side_effects=_DATAFLOW),
    )(*[pltpu.with_memory_space_constraint(t, pltpu.HBM) for t in list(srcs) + list(lands)])
    return outs[0], outs[1], outs[2:2 + n], outs[2 + n:2 + 2 * n], outs[-1]


def _send_wait(name, send_sems, recv_sems, srcs, lands, after, *, scatter):
    n = len(srcs)
    afters = list(after) if isinstance(after, (tuple, list)) else [after]

    def body(*refs):
        src_refs, land_refs = refs[:n], refs[n:2 * n]
        send_sems, recv_sems = refs[2 * n], refs[2 * n + 1]
        me, peers = _peer_places()
        for k, peer in enumerate(peers):
            for a in range(n):
                cp = _direct_copy(src_refs, land_refs, send_sems, recv_sems, a, k, me, peer, scatter)
                cp.wait_send()
                cp.wait_recv()

    hbm_shapes = [pltpu.HBM(t.shape, t.dtype) for t in list(srcs) + list(lands)]
    outs = pl.pallas_call(
        body, name=name, out_shape=tuple(hbm_shapes),
        in_specs=[_HBM_ONLY] * (2 * n) + [_SEM, _SEM] + [_HBM] * len(afters),
        out_specs=tuple([_HBM_ONLY] * (2 * n)),
        input_output_aliases={i: i for i in range(2 * n)},
        compiler_params=pltpu.CompilerParams(has_side_effects=_DATAFLOW),
    )(*srcs, *lands, send_sems, recv_sems, *afters)
    return outs[n:]


def _unstack_cols(name, stacked):
    n, rows, cols = stacked.shape

    def body(i_ref, o_ref):
        o_ref[...] = i_ref[...]

    return pl.pallas_call(
        body, name=name, grid=(n,), in_specs=[pl.BlockSpec((None, rows, cols), lambda k: (k, 0, 0))],
        out_specs=pl.BlockSpec((rows, cols), lambda k: (0, k)),
        out_shape=jax.ShapeDtypeStruct((rows, n * cols), stacked.dtype),
        compiler_params=_cparams(("parallel",)),
    )(stacked)


def _restack_cols(name, mat):
    rows, width = mat.shape
    cols = width // N_DEV

    def body(i_ref, o_ref):
        o_ref[...] = i_ref[...]

    return pl.pallas_call(
        body, name=name, grid=(N_DEV,), in_specs=[pl.BlockSpec((rows, cols), lambda k: (0, k))],
        out_specs=pl.BlockSpec((None, rows, cols), lambda k: (k, 0, 0)),
        out_shape=jax.ShapeDtypeStruct((N_DEV, rows, cols), mat.dtype),
        compiler_params=_cparams(("parallel",)),
    )(mat)


def _remap_pieces(runs):
    plan = {}
    for du, dc, su, sc, ln in runs:
        while ln > 0:
            lane = dc % LANE
            take = min(ln, LANE - lane)
            plan.setdefault((du, dc // LANE), []).append((su, sc, take, lane))
            dc, sc, ln = dc + take, sc + take, ln - take
    return plan


def _remap(name, srcs, src_units, runs, *, out_units, out_cols, out_dtype, tr=256):
    rows = srcs[0].shape[-2]
    tr = min(tr, rows)
    plan = _remap_pieces(runs)
    n_src = len(srcs)
    stacked_out = out_units is not None
    n_tiles = out_cols // LANE

    def body(*refs):
        o_ref = refs[n_src]

        def src_tile(unit, t):
            ai, lead = src_units[unit]
            ref = refs[ai]
            sl = slice(t * LANE, (t + 1) * LANE)
            return (ref[:, sl] if lead is None else ref[lead, :, sl]).astype(F32)

        lane = lax.broadcasted_iota(jnp.int32, (tr, LANE), 1)
        for du in range(out_units if stacked_out else 1):
            for t in range(n_tiles):
                acc = jnp.zeros((tr, LANE), F32)
                for su, sc, ln, dl in plan.get((du if stacked_out else None, t), []):
                    st, so = sc // LANE, sc % LANE
                    first = src_tile(su, st)
                    if so == dl and so + ln <= LANE:
                        piece = first
                    else:
                        second = src_tile(su, st + 1) if so + ln > LANE else first
                        both = jnp.concatenate([first, second], axis=1)
                        piece = pltpu.roll(both, (dl - so) % (2 * LANE), axis=1)[:, 0:LANE]
                    acc = piece if (dl == 0 and ln == LANE) else jnp.where(
                        jnp.logical_and(lane >= dl, lane < dl + ln), piece, acc)
                if stacked_out:
                    o_ref[du, :, t * LANE:(t + 1) * LANE] = acc.astype(o_ref.dtype)
                else:
                    o_ref[:, t * LANE:(t + 1) * LANE] = acc.astype(o_ref.dtype)

    in_specs = []
    for arr in srcs:
        if arr.ndim == 2:
            in_specs.append(pl.BlockSpec((tr, arr.shape[1]), lambda i: (i, 0)))
        else:
            in_specs.append(pl.BlockSpec((arr.shape[0], tr, arr.shape[2]), lambda i: (0, i, 0)))
    if stacked_out:
        out_spec = pl.BlockSpec((out_units, tr, out_cols), lambda i: (0, i, 0))
        out_shape = jax.ShapeDtypeStruct((out_units, rows, out_cols), out_dtype)
    else:
        out_spec = pl.BlockSpec((tr, out_cols), lambda i: (i, 0))
        out_shape = jax.ShapeDtypeStruct((rows, out_cols), out_dtype)
    return pl.pallas_call(
        body, name=name, grid=(rows // tr,), in_specs=in_specs, out_specs=out_spec, out_shape=out_shape,
        compiler_params=_cparams(("parallel",)),
    )(*srcs)


def _proj_col(c):
    if c < PROJ_GATE0:
        return c
    if c < PROJ_GATE0 + FOX_HEADS:
        return PROJ_F0 + (c - PROJ_GATE0)
    return c - FOX_HEADS


def _win_runs():
    cuts = sorted(set([0, PROJ_GATE0, PROJ_GATE0 + FOX_HEADS, IN_WIDTH] + [SHARD_IN * k for k in range(N_DEV + 1)]))
    return [(lo // SHARD_IN, lo % SHARD_IN, _proj_col(lo), hi - lo) for lo, hi in zip(cuts[:-1], cuts[1:])]


def _assemble_win(name, stacked):
    runs = [(None, pc, k, sc, ln) for k, sc, pc, ln in _win_runs()]
    return _remap(name, [stacked], [(0, k) for k in range(N_DEV)], runs,
                  out_units=None, out_cols=PROJ_WIDTH, out_dtype=BF16)


def _disassemble_dwin(name, dw):
    runs = [(k, sc, 0, pc, ln) for k, sc, pc, ln in _win_runs()]
    return _remap(name, [dw], [(0, None)], runs, out_units=N_DEV, out_cols=SHARD_IN_PAD, out_dtype=BF16)


def _concat_cols(name, parts, *, tr=512):
    rows = parts[0].shape[0]
    tr = min(tr, rows)
    widths = [p.shape[1] for p in parts]
    total = sum(widths)

    def body(*refs):
        o_ref = refs[len(parts)]
        lo = 0
        for r, w in zip(refs[:len(parts)], widths):
            o_ref[:, lo:lo + w] = r[...].astype(o_ref.dtype)
            lo += w

    return pl.pallas_call(
        body, name=name, grid=(rows // tr,),
        in_specs=[pl.BlockSpec((tr, w), lambda i: (i, 0)) for w in widths],
        out_specs=pl.BlockSpec((tr, total), lambda i: (i, 0)),
        out_shape=jax.ShapeDtypeStruct((rows, total), BF16),
        compiler_params=_cparams(("parallel",)),
    )(*parts)


FFN_BLK = FFN_HIDDEN // 2


def _ffn_col(c):
    half, r = divmod(c, FFN_HIDDEN)
    blk, r = divmod(r, FFN_BLK)
    return blk * 2 * FFN_BLK + half * FFN_BLK + r


def _assemble_wffn(name, stacked):
    runs = [(None, _ffn_col(SHARD_FFN * k), k, 0, SHARD_FFN) for k in range(N_DEV)]
    return _remap(name, [stacked], [(0, k) for k in range(N_DEV)], runs,
                  out_units=None, out_cols=2 * FFN_HIDDEN, out_dtype=BF16)


def _disassemble_dwffn(name, dw):
    runs = [(k, 0, 0, _ffn_col(SHARD_FFN * k), SHARD_FFN) for k in range(N_DEV)]
    return _remap(name, [dw], [(0, None)], runs, out_units=N_DEV, out_cols=SHARD_FFN_PAD, out_dtype=BF16)


def _ffn_in_swiglu(name, xn, w, *, tm=512):
    rows, k = xn.shape
    tm = min(tm, rows)
    nblk = FFN_HIDDEN // FFN_BLK

    def body(x_ref, w_ref, f_ref, g_ref):
        f = _bdot(x_ref[...], w_ref[...], _DIMS["nn"])
        f_ref[...] = f.astype(f_ref.dtype)
        fa = f[:, 0:FFN_BLK]
        g_ref[...] = (fa * _sigmoid(fa) * f[:, FFN_BLK:2 * FFN_BLK]).astype(g_ref.dtype)

    return pl.pallas_call(
        body, name=name, grid=(nblk, rows // tm),
        in_specs=[pl.BlockSpec((tm, k), lambda j, i: (i, 0)), pl.BlockSpec((k, 2 * FFN_BLK), lambda j, i: (0, j))],
        out_specs=(pl.BlockSpec((tm, 2 * FFN_BLK), lambda j, i: (i, j)), pl.BlockSpec((tm, FFN_BLK), lambda j, i: (i, j))),
        out_shape=(jax.ShapeDtypeStruct((rows, 2 * FFN_HIDDEN), BF16), jax.ShapeDtypeStruct((rows, FFN_HIDDEN), BF16)),
        compiler_params=_cparams(("parallel", "arbitrary")),
    )(xn, w)


def _d_ffn_out_swiglu(name, dh, w_out, f, *, tm=512):
    rows, d = dh.shape
    tm = min(tm, rows)
    nblk = FFN_HIDDEN // FFN_BLK

    def body(dh_ref, w_ref, f_ref, df_ref):
        dg = _bdot(dh_ref[...], w_ref[...], _DIMS["nt"])
        fa = f_ref[:, 0:FFN_BLK].astype(F32)
        fb = f_ref[:, FFN_BLK:2 * FFN_BLK].astype(F32)
        s = _sigmoid(fa)
        df_ref[:, 0:FFN_BLK] = (dg * fb * s * (1.0 + fa * (1.0 - s))).astype(df_ref.dtype)
        df_ref[:, FFN_BLK:2 * FFN_BLK] = (dg * fa * s).astype(df_ref.dtype)

    wide = pl.BlockSpec((tm, 2 * FFN_BLK), lambda j, i: (i, j))
    return pl.pallas_call(
        body, name=name, grid=(nblk, rows // tm),
        in_specs=[pl.BlockSpec((tm, d), lambda j, i: (i, 0)), pl.BlockSpec((FFN_BLK, d), lambda j, i: (j, 0)), wide],
        out_specs=wide, out_shape=jax.ShapeDtypeStruct((rows, 2 * FFN_HIDDEN), BF16),
        compiler_params=_cparams(("parallel", "arbitrary")),
    )(dh, w_out, f)


def _adamw(name, parts, w, m, v, *, tr=128):
    rows, cols = w.shape
    n_parts = parts.shape[0]
    tr = min(tr, rows)
    assert rows % tr == 0, (name, rows, tr)
    c1 = 1.0 - ADAM_B1 ** ADAM_STEP
    c2 = 1.0 - ADAM_B2 ** ADAM_STEP

    def body(p_ref, w_ref, m_ref, v_ref, g_ref, d_ref, nm_ref, nv_ref):
        g = p_ref[0].astype(F32)
        for s in range(1, n_parts):
            g = g + p_ref[s].astype(F32)
        m_new = ADAM_B1 * m_ref[...] + (1.0 - ADAM_B1) * g
        v_new = ADAM_B2 * v_ref[...] + (1.0 - ADAM_B2) * (g * g)
        upd = (m_new / c1) / (jnp.sqrt(v_new / c2) + ADAM_EPS) + ADAM_WD * w_ref[...]
        g_ref[...] = g
        d_ref[...] = -ADAM_LR * upd
        nm_ref[...] = m_new
        nv_ref[...] = v_new

    row = pl.BlockSpec((tr, cols), lambda i: (i, 0))
    out = jax.ShapeDtypeStruct((rows, cols), F32)
    return pl.pallas_call(
        body, name=name, grid=(rows // tr,),
        in_specs=[pl.BlockSpec((n_parts, tr, cols), lambda i: (0, i, 0)), row, row, row],
        out_specs=(row, row, row, row), out_shape=(out, out, out, out),
        compiler_params=_cparams(("parallel",)),
    )(parts, w, m, v)


_WEIGHTS = ("norm_mix", "w_in", "b_forget", "lam_re", "lam_im", "log_dt", "b_re", "b_im", "c_re", "c_im",
            "d_skip", "w_glu", "w_fox_o", "w_mix_out", "norm_mem_q", "norm_mem_kv", "w_mem_q", "w_mem_kv",
            "w_mem_o", "norm_ffn", "w_ffn_in", "w_ffn_out", "norm_final")
_SHARDED = ("w_in", "w_glu", "w_fox_o", "w_mix_out", "w_mem_q", "w_mem_kv", "w_mem_o", "w_ffn_in", "w_ffn_out")
_SMALL = tuple(n for n in _WEIGHTS if n not in _SHARDED)
_PACK_COLS = 1024


def _pack(arrays):
    flat = jnp.concatenate([a.reshape(-1).astype(F32) for a in arrays])
    rows = -(-flat.shape[0] // _PACK_COLS)
    return jnp.pad(flat, (0, rows * _PACK_COLS - flat.shape[0])).reshape(rows, _PACK_COLS)


def _unpack(buf, like):
    flat = buf.reshape(-1)
    out, pos = [], 0
    for a in like:
        out.append(flat[pos:pos + a.size].reshape(a.shape))
        pos += a.size
    return out


def _mm(name, a, b, mode, m, n, k, out_dtype, tm=1024, tn=512, tk=1024, **kw):
    return _matmul(name, a, b, mode, m, n, k, out_dtype=out_dtype, tm=tm, tn=tn, tk=tk, **kw)


def kernel(x, mem, norm_mix, w_in, b_forget, lam_re, lam_im, log_dt, b_re, b_im, c_re, c_im, d_skip, w_glu, w_fox_o, w_mix_out, norm_mem_q, norm_mem_kv, w_mem_q, w_mem_kv, w_mem_o, norm_ffn, w_ffn_in, w_ffn_out, norm_final, loss_target, m_norm_mix, m_w_in, m_b_forget, m_lam_re, m_lam_im, m_log_dt, m_b_re, m_b_im, m_c_re, m_c_im, m_d_skip, m_w_glu, m_w_fox_o, m_w_mix_out, m_norm_mem_q, m_norm_mem_kv, m_w_mem_q, m_w_mem_kv, m_w_mem_o, m_norm_ffn, m_w_ffn_in, m_w_ffn_out, m_norm_final, v_norm_mix, v_w_in, v_b_forget, v_lam_re, v_lam_im, v_log_dt, v_b_re, v_b_im, v_c_re, v_c_im, v_d_skip, v_w_glu, v_w_fox_o, v_w_mix_out, v_norm_mem_q, v_norm_mem_kv, v_w_mem_q, v_w_mem_kv, v_w_mem_o, v_norm_ffn, v_w_ffn_in, v_w_ffn_out, v_norm_final):
    given = dict(locals())
    weights = {n: given[n] for n in _WEIGHTS}
    mom_m = {n: given["m_" + n] for n in _WEIGHTS}
    mom_v = {n: given["v_" + n] for n in _WEIGHTS}
    seq = x.shape[1]
    nc = seq // SSM_CHUNK
    d = D_MODEL
    xs, mems, tgt = x[0], mem[0], loss_target[0]

    def padcols(a, width):
        return jnp.pad(a, ((0, 0), (0, width - a.shape[1])))

    shards = [padcols(w_in[0].astype(BF16), SHARD_IN_PAD), w_glu[0].astype(BF16), w_fox_o[0].astype(BF16),
              w_mix_out[0].astype(BF16), w_mem_q[0].astype(BF16), w_mem_kv[0].astype(BF16),
              w_mem_o[0].astype(BF16), padcols(w_ffn_in[0].astype(BF16), SHARD_FFN_PAD), w_ffn_out[0].astype(BF16)]
    first = shards[:1]
    wsend, wrecv, first_thru, first_lands, wtoken = _send_start(
        "gather_w_in_start", first, _place_own("place_w_in_shard", first, stacked_src=False), scatter=False)
    rest = shards[1:]
    gsend, grecv, rest_thru, lands, gtoken = _send_start(
        "gather_rest_start", rest, _place_own("place_weight_shards", rest, stacked_src=False, after=wtoken),
        scatter=False)

    u = _rms_fwd("rms_mix", xs, norm_mix, after=gtoken)
    ssm_params = tuple(p[0] + wtoken[0, 0] for p in (lam_re, lam_im, log_dt, b_re, b_im, c_re, c_im))
    (m_c, bw_c, cm_c, a8, aseg), mats_vjp = jax.vjp(lambda *p: _ssm_mats(*p, nc), *ssm_params)
    m_b = _bd_expand("ssm_expand_m", _BD_M, m_c)
    bw_b = _bd_expand("ssm_expand_bw", _BD_BW, bw_c)
    cm_b = _bd_expand("ssm_expand_cm", _BD_CM, cm_c)
    win = _assemble_win("assemble_w_in", _send_wait(
        "gather_w_in_wait", wsend, wrecv, first_thru, first_lands, (u, m_b, bw_b, cm_b), scatter=False)[0])
    ussm = _mm("proj_ssm", u, win, "nn", seq, SSM_WIDTH, d, F32)
    qkv = _mm("proj_qkv", u, win, "nn", seq, 3 * FOX_WIDTH, d, BF16, tn=512, b_off=(0, SSM_WIDTH))
    gates = _mm("proj_gates", u, win, "nn", seq, 2 * d, d, BF16, tn=1024, b_off=(0, PROJ_GATE0))
    fproj = _mm("proj_forget", u, win, "nn", seq, LANE, d, F32, tn=LANE, b_off=(0, PROJ_F0))

    u8 = ussm.reshape(nc, SSM_CHUNK * SSM_WIDTH)
    d8 = jnp.tile(d_skip, (1, SSM_CHUNK))
    w4 = _ssm_w("ssm_w", u8, bw_b)
    sp4 = _ssm_scan("ssm_scan", w4, a8, aseg, reverse=False)
    y8 = _ssm_y("ssm_y", u8, sp4, m_b, cm_b)
    act = _ssm_post_fwd("ssm_act", y8, u8, d8).reshape(seq, SSM_WIDTH)

    bcol = jnp.pad(b_forget[0], (0, LANE - FOX_HEADS)).reshape(LANE, 1)
    cum_t = _fox_cum("fox_cum", fproj, bcol).reshape(FOX_HEADS // 2, 2, seq)
    att, lse = _fox_fwd("fox_fwd", qkv, cum_t)

    gathered = _send_wait("gather_rest_wait", gsend, grecv, rest_thru, lands, att, scatter=False)
    wglu = _unstack_cols("unstack_w_glu", gathered[0])
    wfoxo = _unstack_cols("unstack_w_fox_o", gathered[1])
    wmix = gathered[2].reshape(d, d)
    wmq = gathered[3].reshape(d, MEM_WIDTH)
    wmkv = gathered[4].reshape(d, 2 * MEM_WIDTH)
    wmo = _unstack_cols("unstack_w_mem_o", gathered[5])
    wffn_in = _assemble_wffn("assemble_w_ffn_in", gathered[6])
    wffn_out = gathered[7].reshape(FFN_HIDDEN, d)

    glu = _mm("glu", act, wglu, "nn", seq, 2 * d, SSM_WIDTH, BF16, tn=1024)
    out_b = _mm("fox_out", att, wfoxo, "nn", seq, d, FOX_WIDTH, BF16, tn=1024)

    mixin = _mix_fwd("mix", glu, gates, out_b)
    h1 = _mm("mix_out", mixin, wmix, "nn", seq, d, d, F32, tn=1024, add=xs)

    n1 = _rms_fwd("rms_mem_q", h1, norm_mem_q)
    q2 = _mm("mem_q", n1, wmq, "nn", seq, MEM_WIDTH, d, BF16)
    mn = _rms_fwd("rms_mem_kv", mems, norm_mem_kv)
    mlen = mems.shape[0]
    kv = _mm("mem_kv", mn, wmkv, "nn", mlen, 2 * MEM_WIDTH, d, BF16)
    o2 = _mem_fwd("mem_attn", q2, kv)
    h2 = _mm("mem_out", o2, wmo, "nn", seq, d, MEM_WIDTH, F32, tn=1024, add=h1)

    n2 = _rms_fwd("rms_ffn", h2, norm_ffn)
    f, g_act = _ffn_in_swiglu("ffn_in_swiglu", n2, wffn_in)
    loss_part, dh3, dg_final = _matmul_final_loss("ffn_out_final_loss", g_act, wffn_out, h2, tgt,
                                                  norm_final.reshape(1, d))

    df = _d_ffn_out_swiglu("d_ffn_out_swiglu", dh3, wffn_out, f)
    dwffn_out = _mm("d_ffn_out_w", g_act, dh3, "tn", FFN_HIDDEN, d, seq, BF16, tm=1408, tn=1024)
    dh2, dg_ffn = _matmul_rms_bwd("d_ffn_in_x_rms", df, wffn_in, 2 * FFN_HIDDEN, h2, norm_ffn, dh3, tm=1024, tk=1408)
    dwffn_in = _mm("d_ffn_in_w", n2, df, "tn", d, 2 * FFN_HIDDEN, seq, BF16, tn=1408)

    do2 = _mm("d_mem_out_x", dh2, wmo, "nt", seq, MEM_WIDTH, d, F32)
    dwmo = _restack_cols("restack_d_w_mem_o", _mm("d_mem_out_w", o2, dh2, "tn", MEM_WIDTH, d, seq, BF16, tn=1024))
    dq2, dkv = _mem_bwd("d_mem_attn", q2, kv, do2)
    dwmq = _mm("d_mem_q_w", n1, dq2, "tn", d, MEM_WIDTH, seq, BF16)
    dwmkv = _mm("d_mem_kv_w", mn, dkv, "tn", d, 2 * MEM_WIDTH, mlen, BF16, tn=1024)
    dmn = _mm("d_mem_kv_x", dkv, wmkv, "nt", mlen, d, 2 * MEM_WIDTH, F32)
    dg_memkv = _rms_gain_grad("d_rms_mem_kv", dmn, mems)

    early = [dwmq.reshape(N_DEV, d // N_DEV, MEM_WIDTH), dwmkv.reshape(N_DEV, d // N_DEV, 2 * MEM_WIDTH), dwmo,
             _disassemble_dwffn("split_d_w_ffn_in", dwffn_in), dwffn_out.reshape(N_DEV, FFN_HIDDEN // N_DEV, d)]
    ssend, srecv, early_thru, early_lands, stoken = _send_start(
        "scatter_early_start", early, _place_own("place_early_grads", early, stacked_src=True), scatter=True)
    dh1, dg_memq = _matmul_rms_bwd("d_mem_q_x_rms", dq2, wmq, MEM_WIDTH, h1, norm_mem_q, dh2, tm=1024, after=stoken)

    dwmix = _mm("d_mix_out_w", mixin, dh1, "tn", d, d, seq, BF16, tn=1024)
    dglu, dgates, dout_b = _mix_bwd("d_mix_out_x_mix", dh1, wmix, glu, gates, out_b)
    datt = _mm("d_fox_out_x", dout_b, wfoxo, "nt", seq, FOX_WIDTH, d, F32)
    dwfoxo = _restack_cols("restack_d_w_fox_o", _mm("d_fox_out_w", att, dout_b, "tn", FOX_WIDTH, d, seq, BF16, tn=1024))
    dact = _mm("d_glu_x", dglu, wglu, "nt", seq, SSM_WIDTH, 2 * d, F32, tk=2 * d)
    dwglu = _restack_cols("restack_d_w_glu", _mm("d_glu_w", act, dglu, "tn", SSM_WIDTH, 2 * d, seq, BF16, tn=2 * d))

    mid = [dwglu, dwfoxo, dwmix.reshape(N_DEV, d // N_DEV, d)]
    msend, mrecv, mid_thru, mid_lands, mtoken = _send_start(
        "scatter_mid_start", mid, _place_own("place_mid_grads", mid, stacked_src=True), scatter=True)

    dz8, dg_dskip = _ssm_post_bwd("d_ssm_act", dact.reshape(nc, SSM_CHUNK * SSM_WIDTH), y8, u8, d8, after=mtoken)
    ds4, dcm = _ssm_ds("d_ssm_y_state", dz8, sp4, cm_b)
    g4, da8 = _ssm_scan("d_ssm_scan", ds4, a8, aseg, reverse=True, sprev4=sp4)
    dx8, dm, dbw = _ssm_dx("d_ssm_x", dz8, g4, u8, m_b, bw_b, d8)
    dussm = dx8.reshape(seq, SSM_WIDTH)
    g_ssm = mats_vjp((_bd_reduce("ssm_reduce_dm", _BD_M, dm), _bd_reduce("ssm_reduce_dbw", _BD_BW, dbw),
                      _bd_reduce("ssm_reduce_dcm", _BD_CM, dcm), da8, jnp.zeros_like(aseg)))

    dq, dk, dv, dcs = _fox_bwd("d_fox", qkv, cum_t, att, datt, lse)
    dfproj, dbf = _fox_cum_bwd("d_fox_cum", dcs, fproj, bcol)
    dg_bforget = dbf[0:FOX_HEADS, 0].reshape(1, FOX_HEADS)

    dproj = _concat_cols("d_proj_concat", (dussm, dq, dk, dv, dgates, dfproj))
    dwin = _mm("d_proj_w", u, dproj, "tn", d, PROJ_WIDTH, seq, BF16, tn=1408)
    late = [_disassemble_dwin("split_d_w_in", dwin)]
    lsend, lrecv, late_thru, late_lands, ltoken = _send_start(
        "scatter_late_start", late, _place_own("place_late_grads", late, stacked_src=True), scatter=True)
    dx, dg_mix = _matmul_rms_bwd("d_proj_x_rms", dproj, win, PROJ_WIDTH, xs, norm_mix, dh1, tm=1024, tk=1408,
                                 after=ltoken)

    early_parts = _send_wait("scatter_early_wait", ssend, srecv, early_thru, early_lands, dx, scatter=True)
    mid_parts = _send_wait("scatter_mid_wait", msend, mrecv, mid_thru, mid_lands, dx, scatter=True)
    received = dict(zip(("w_glu", "w_fox_o", "w_mix_out"), mid_parts))
    received.update(zip(("w_mem_q", "w_mem_kv", "w_mem_o", "w_ffn_in", "w_ffn_out"), early_parts))

    small_grads = dict(zip(
        _SMALL, (dg_mix, dg_bforget, g_ssm[0][None], g_ssm[1][None], g_ssm[2][None], g_ssm[3][None], g_ssm[4][None],
                 g_ssm[5][None], g_ssm[6][None], dg_dskip, dg_memq, dg_memkv, dg_ffn, dg_final.reshape(d))))
    small_like = [weights[n] for n in _SMALL]
    small_all = _gather_all("gather_small_grads", [_pack([small_grads[n] for n in _SMALL])])[0]
    pk = [_pack([src[n] for n in _SMALL]) for src in (weights, mom_m, mom_v)]
    small_out = _adamw("adamw_small", small_all, pk[0], pk[1], pk[2], tr=small_all.shape[1])
    results = [dict(zip(_SMALL, _unpack(buf, small_like))) for buf in small_out]
    tiles = {"w_in": 128, "w_glu": 128, "w_fox_o": 128, "w_mix_out": 128, "w_mem_q": 128, "w_mem_kv": 128,
             "w_mem_o": 128, "w_ffn_in": 128, "w_ffn_out": 176}
    pads = {"w_in": SHARD_IN_PAD, "w_ffn_in": SHARD_FFN_PAD}
    outs = small_out
    for name in _SHARDED[1:] + _SHARDED[:1]:
        if name == "w_in":
            received[name] = _send_wait("scatter_late_wait", lsend, lrecv, late_thru, late_lands, outs[0],
                                        scatter=True)[0]
        parts = received[name]
        w2, m2, v2 = weights[name][0], mom_m[name][0], mom_v[name][0]
        cols = w2.shape[1]
        if name in pads:
            w2, m2, v2 = (padcols(t, pads[name]) for t in (w2, m2, v2))
        outs = _adamw("adamw_" + name, parts, w2, m2, v2, tr=tiles[name])
        for res, o in zip(results, outs):
            res[name] = o[:, :cols][None]

    loss = lax.psum(loss_part[0, 0], ("x", "y", "c"))
    out = [loss, dx[None]]
    for res in results:
        out.extend(res[n] for n in _WEIGHTS)
    return tuple(out)
```

```python
import math

import jax
import jax.numpy as jnp
import numpy as np
from jax import lax
from jax.experimental import pallas as pl
from jax.experimental.pallas import tpu as pltpu

F32 = jnp.float32
BF16 = jnp.bfloat16

N_DEV = 8
LANE = 128
VMEM_LIMIT = 56 * 1024 * 1024

D_MODEL = 1024
SSM_GROUP = 16
SSM_GROUPS = 32
SSM_WIDTH = 512
SSM_STATE = 64
SSM_CHUNK = 8
FOX_HEADS = 8
FOX_HEAD_DIM = 64
FOX_WIDTH = 512
MEM_HEADS = 4
MEM_HEAD_DIM = 128
MEM_WIDTH = 512
FFN_HIDDEN = 2816
RMS_EPS = 1e-6
IN_WIDTH = 4104
SHARD_IN = IN_WIDTH // N_DEV
SHARD_IN_PAD = 640
SHARD_FFN = 2 * FFN_HIDDEN // N_DEV
SHARD_FFN_PAD = 768
PROJ_GATE0 = 2048
PROJ_F0 = 4096
PROJ_WIDTH = 4224

ADAM_LR = 0.001
ADAM_B1 = 0.9
ADAM_B2 = 0.999
ADAM_EPS = 1e-08
ADAM_WD = 0.01
ADAM_STEP = 10


def _cparams(sem=None):
    return pltpu.CompilerParams(dimension_semantics=sem, vmem_limit_bytes=VMEM_LIMIT)


def _sigmoid(x):
    return 1.0 / (1.0 + jnp.exp(-x))


def _bdot(a, b, dims):
    return lax.dot_general(a.astype(BF16), b.astype(BF16), ((dims[0], dims[1]), ((), ())),
                           preferred_element_type=F32)


_DIMS = {"nn": ((1,), (0,)), "nt": ((1,), (1,)), "tn": ((0,), (0,))}


def _matmul(name, a, b, mode, m, n, k, *, out_dtype, tm, tn, tk, a_off=(0, 0), b_off=(0, 0), add=None):
    tm, tn, tk = min(tm, m), min(tn, n), min(tk, k)
    assert m % tm == 0 and n % tn == 0 and k % tk == 0, (name, m, n, k, tm, tn, tk)
    nk = k // tk
    grid = (m // tm, n // tn, nk)

    def blk(off, t):
        assert off % t == 0, (name, off, t)
        return off // t

    if mode in ("nn", "nt"):
        ar, ac = blk(a_off[0], tm), blk(a_off[1], tk)
        a_spec = pl.BlockSpec((tm, tk), lambda i, j, kk: (i + ar, kk + ac))
    else:
        ar, ac = blk(a_off[0], tk), blk(a_off[1], tm)
        a_spec = pl.BlockSpec((tk, tm), lambda i, j, kk: (kk + ar, i + ac))

    if mode in ("nn", "tn"):
        br, bc = blk(b_off[0], tk), blk(b_off[1], tn)
        b_spec = pl.BlockSpec((tk, tn), lambda i, j, kk: (kk + br, j + bc))
    else:
        br, bc = blk(b_off[0], tn), blk(b_off[1], tk)
        b_spec = pl.BlockSpec((tn, tk), lambda i, j, kk: (j + br, kk + bc))
    o_spec = pl.BlockSpec((tm, tn), lambda i, j, kk: (i, j))
    out_shape = jax.ShapeDtypeStruct((m, n), out_dtype)

    in_specs = [a_spec, b_spec]
    operands = [a, b]
    if add is not None:
        in_specs.append(pl.BlockSpec((tm, tn), lambda i, j, kk: (i, j)))
        operands.append(add)
    dims = _DIMS[mode]
    has_add = add is not None

    def body(*refs):
        a_ref, b_ref = refs[0], refs[1]
        add_ref = refs[2] if has_add else None
        o_ref = refs[3] if has_add else refs[2]
        acc_ref = refs[-1] if nk > 1 else None
        prod = _bdot(a_ref[...], b_ref[...], dims)

        def finish(total):
            if has_add:
                total = total + add_ref[...].astype(F32)
            o_ref[...] = total.astype(o_ref.dtype)

        if nk == 1:
            finish(prod)
        else:
            kk = pl.program_id(2)

            @pl.when(kk == 0)
            def _():
                acc_ref[...] = prod

            @pl.when(jnp.logical_and(kk > 0, kk < nk - 1))
            def _():
                acc_ref[...] += prod

            @pl.when(kk == nk - 1)
            def _():
                finish(acc_ref[...] + prod)

    scratch = [pltpu.VMEM((tm, tn), F32)] if nk > 1 else []
    return pl.pallas_call(
        body, name=name, grid=grid, in_specs=in_specs, out_specs=o_spec, out_shape=out_shape,
        scratch_shapes=scratch,
        compiler_params=_cparams(("parallel", "parallel", "arbitrary")),
    )(*operands)


def _rms_fwd(name, x, gain, *, tr=512, after=None):
    r, d = x.shape
    tr = min(tr, r)

    def body(x_ref, g_ref, *rest):
        o_ref = rest[-1]
        xv = x_ref[...]
        rstd = lax.rsqrt(jnp.mean(xv * xv, axis=-1, keepdims=True) + RMS_EPS)
        o_ref[...] = (xv * rstd * g_ref[...]).astype(o_ref.dtype)

    in_specs = [pl.BlockSpec((tr, d), lambda i: (i, 0)), pl.BlockSpec((1, d), lambda i: (0, 0))]
    ops = [x, gain]
    if after is not None:
        in_specs.append(pl.BlockSpec(after.shape, lambda i: (0, 0)))
        ops.append(after)
    return pl.pallas_call(
        body, name=name, grid=(r // tr,), in_specs=in_specs,
        out_specs=pl.BlockSpec((tr, d), lambda i: (i, 0)),
        out_shape=jax.ShapeDtypeStruct((r, d), BF16),
        compiler_params=_cparams(("parallel",)),
    )(*ops)


def _rms_gain_grad(name, dy, x, *, tr=512):
    r, d = x.shape
    tr = min(tr, r)
    n = r // tr

    def body(dy_ref, x_ref, dg_ref, acc_ref):
        i = pl.program_id(0)
        xv = x_ref[...]
        xh = xv * lax.rsqrt(jnp.mean(xv * xv, axis=-1, keepdims=True) + RMS_EPS)
        part = (dy_ref[...].astype(F32) * xh).reshape(tr // 8, 8, d).sum(axis=0)

        @pl.when(i == 0)
        def _():
            acc_ref[...] = part

        @pl.when(i > 0)
        def _():
            acc_ref[...] += part

        @pl.when(i == n - 1)
        def _():
            dg_ref[...] = jnp.sum(acc_ref[...], axis=0, keepdims=True)

    row = pl.BlockSpec((tr, d), lambda i: (i, 0))
    return pl.pallas_call(
        body, name=name, grid=(n,), in_specs=[row, row],
        out_specs=pl.BlockSpec((1, d), lambda i: (0, 0)),
        out_shape=jax.ShapeDtypeStruct((1, d), F32),
        scratch_shapes=[pltpu.VMEM((8, d), F32)],
        compiler_params=_cparams(("arbitrary",)),
    )(dy, x)


def _matmul_rms_bwd(name, a, b, k, x, gain, res, *, tm=512, tk=1024, after=None):
    m, d = x.shape
    tm, tk = min(tm, m), min(tk, k)
    assert m % tm == 0 and k % tk == 0, (name, m, k, tm, tk)
    ni, nk = m // tm, k // tk

    def body(a_ref, b_ref, x_ref, g_ref, res_ref, *rest):
        dx_ref, dg_ref, acc_ref, accg_ref = rest[-4:]
        i, kk = pl.program_id(0), pl.program_id(1)
        prod = _bdot(a_ref[...], b_ref[...], _DIMS["nt"])

        @pl.when(kk == 0)
        def _():
            acc_ref[...] = prod

        @pl.when(kk > 0)
        def _():
            acc_ref[...] += prod

        @pl.when(kk == nk - 1)
        def _():
            dyv = acc_ref[...]
            xv = x_ref[...]
            rstd = lax.rsqrt(jnp.mean(xv * xv, axis=-1, keepdims=True) + RMS_EPS)
            xh = xv * rstd
            dxh = dyv * g_ref[...]
            dx_ref[...] = rstd * (dxh - xh * jnp.mean(dxh * xh, axis=-1, keepdims=True)) + res_ref[...]
            part = (dyv * xh).reshape(tm // 8, 8, d).sum(axis=0)

            @pl.when(i == 0)
            def _():
                accg_ref[...] = part

            @pl.when(i > 0)
            def _():
                accg_ref[...] += part

            @pl.when(i == ni - 1)
            def _():
                dg_ref[...] = jnp.sum(accg_ref[...], axis=0, keepdims=True)

    row = pl.BlockSpec((tm, d), lambda i, kk: (i, 0))
    one = pl.BlockSpec((1, d), lambda i, kk: (0, 0))
    in_specs = [pl.BlockSpec((tm, tk), lambda i, kk: (i, kk)), pl.BlockSpec((d, tk), lambda i, kk: (0, kk)), row, one, row]
    ops = [a, b, x, gain, res]
    if after is not None:
        in_specs.append(pl.BlockSpec(after.shape, lambda i, kk: (0, 0)))
        ops.append(after)
    return pl.pallas_call(
        body, name=name, grid=(ni, nk), in_specs=in_specs, out_specs=(row, one),
        out_shape=(jax.ShapeDtypeStruct((m, d), F32), jax.ShapeDtypeStruct((1, d), F32)),
        scratch_shapes=[pltpu.VMEM((tm, d), F32), pltpu.VMEM((8, d), F32)],
        compiler_params=_cparams(("arbitrary", "arbitrary")),
    )(*ops)


def _matmul_final_loss(name, a, b, res, target, gain, *, tr=512):
    r, d = res.shape
    k = a.shape[1]
    tr = min(tr, r)
    n = r // tr

    def body(a_ref, b_ref, res_ref, t_ref, g_ref, loss_ref, dh_ref, dg_ref, accl_ref, accg_ref):
        i = pl.program_id(0)
        xv = _bdot(a_ref[...], b_ref[...], _DIMS["nn"]) + res_ref[...]
        rstd = lax.rsqrt(jnp.mean(xv * xv, axis=-1, keepdims=True) + RMS_EPS)
        xh = xv * rstd
        e = xh * g_ref[...] - t_ref[...]
        dyv = e * (1.0 / d)
        dxh = dyv * g_ref[...]
        dh_ref[...] = rstd * (dxh - xh * jnp.mean(dxh * xh, axis=-1, keepdims=True))
        lpart = (e * e).reshape(tr // 8, 8, d).sum(axis=0)
        gpart = (dyv * xh).reshape(tr // 8, 8, d).sum(axis=0)

        @pl.when(i == 0)
        def _():
            accl_ref[...] = lpart
            accg_ref[...] = gpart

        @pl.when(i > 0)
        def _():
            accl_ref[...] += lpart
            accg_ref[...] += gpart

        @pl.when(i == n - 1)
        def _():
            tot = jnp.sum(jnp.sum(accl_ref[...], axis=0, keepdims=True), axis=1, keepdims=True)
            loss_ref[...] = jnp.broadcast_to(tot * (0.5 / d), (1, LANE))
            dg_ref[...] = jnp.sum(accg_ref[...], axis=0, keepdims=True)

    row = pl.BlockSpec((tr, d), lambda i: (i, 0))
    one = pl.BlockSpec((1, d), lambda i: (0, 0))
    return pl.pallas_call(
        body, name=name, grid=(n,),
        in_specs=[pl.BlockSpec((tr, k), lambda i: (i, 0)), pl.BlockSpec((k, d), lambda i: (0, 0)), row, row, one],
        out_specs=(pl.BlockSpec((1, LANE), lambda i: (0, 0)), row, one),
        out_shape=(jax.ShapeDtypeStruct((1, LANE), F32), jax.ShapeDtypeStruct((r, d), F32),
                   jax.ShapeDtypeStruct((1, d), F32)),
        scratch_shapes=[pltpu.VMEM((8, d), F32), pltpu.VMEM((8, d), F32)],
        compiler_params=_cparams(("arbitrary",)),
    )(a, b, res, target, gain)


_GELU_C = math.sqrt(2.0 / math.pi)


def _gelu_parts(z):
    inner = _GELU_C * (z + 0.044715 * z * z * z)
    t = jnp.tanh(inner)
    val = 0.5 * z * (1.0 + t)
    dinner = _GELU_C * (1.0 + 3.0 * 0.044715 * z * z)
    grad = 0.5 * (1.0 + t) + 0.5 * z * (1.0 - t * t) * dinner
    return val, grad


def _ssm_post_fwd(name, y8, u8, d8, *, tr=256):
    r, c = y8.shape
    tr = min(tr, r)

    def body(y_ref, u_ref, d_ref, o_ref):
        z = y_ref[...] + d_ref[...] * u_ref[...]
        o_ref[...] = _gelu_parts(z)[0].astype(o_ref.dtype)

    row = pl.BlockSpec((tr, c), lambda i: (i, 0))
    return pl.pallas_call(
        body, name=name, grid=(r // tr,), in_specs=[row, row, pl.BlockSpec((1, c), lambda i: (0, 0))],
        out_specs=row, out_shape=jax.ShapeDtypeStruct((r, c), BF16),
        compiler_params=_cparams(("parallel",)),
    )(y8, u8, d8)


def _ssm_post_bwd(name, dact8, y8, u8, d8, *, tr=256, after=None):
    r, c = y8.shape
    tr = min(tr, r)
    n = r // tr

    def body(*refs):
        da_ref, y_ref, u_ref, d_ref = refs[:4]
        dz_ref, dd_ref, acc_ref = refs[-3:]
        i = pl.program_id(0)
        uv = u_ref[...]
        z = y_ref[...] + d_ref[...] * uv
        dz = da_ref[...].astype(F32) * _gelu_parts(z)[1]
        dz_ref[...] = dz
        part = (dz * uv).reshape(tr // 8, 8, c).sum(axis=0)

        @pl.when(i == 0)
        def _():
            acc_ref[...] = part

        @pl.when(i > 0)
        def _():
            acc_ref[...] += part

        @pl.when(i == n - 1)
        def _():
            tot = jnp.sum(acc_ref[...], axis=0, keepdims=True)
            out = tot[:, 0:SSM_WIDTH]
            for j in range(1, c // SSM_WIDTH):
                out = out + tot[:, j * SSM_WIDTH:(j + 1) * SSM_WIDTH]
            dd_ref[...] = out

    row = pl.BlockSpec((tr, c), lambda i: (i, 0))
    in_specs = [row, row, row, pl.BlockSpec((1, c), lambda i: (0, 0))]
    ops = [dact8, y8, u8, d8]
    if after is not None:
        in_specs.append(pl.BlockSpec(memory_space=pl.ANY))
        ops.append(after)
    return pl.pallas_call(
        body, name=name, grid=(n,), in_specs=in_specs,
        out_specs=(row, pl.BlockSpec((1, SSM_WIDTH), lambda i: (0, 0))),
        out_shape=(jax.ShapeDtypeStruct((r, c), F32), jax.ShapeDtypeStruct((1, SSM_WIDTH), F32)),
        scratch_shapes=[pltpu.VMEM((8, c), F32)],
        compiler_params=_cparams(("arbitrary",)),
    )(*ops)


def _mix_fwd(name, glu, gates, out_b, w_mix, res, *, tr=512):
    r = glu.shape[0]
    d = D_MODEL
    tr = min(tr, r)

    def body(glu_ref, gate_ref, ob_ref, w_ref, res_ref, o_ref, h_ref):
        out_a = glu_ref[:, 0:d].astype(F32) * _sigmoid(glu_ref[:, d:2 * d].astype(F32))
        mix = (_sigmoid(gate_ref[:, 0:d].astype(F32)) * out_a
               + _sigmoid(gate_ref[:, d:2 * d].astype(F32)) * ob_ref[...].astype(F32))
        o_ref[...] = mix.astype(o_ref.dtype)
        h_ref[...] = _bdot(o_ref[...], w_ref[...], _DIMS["nn"]) + res_ref[...]

    wide = pl.BlockSpec((tr, 2 * d), lambda i: (i, 0))
    row = pl.BlockSpec((tr, d), lambda i: (i, 0))
    return pl.pallas_call(
        body, name=name, grid=(r // tr,),
        in_specs=[wide, wide, row, pl.BlockSpec((d, d), lambda i: (0, 0)), row], out_specs=(row, row),
        out_shape=(jax.ShapeDtypeStruct((r, d), BF16), jax.ShapeDtypeStruct((r, d), F32)),
        compiler_params=_cparams(("parallel",)),
    )(glu, gates, out_b, w_mix, res)


def _mix_bwd(name, dh, w_mix, glu, gates, out_b, *, tr=512):
    r = glu.shape[0]
    d = D_MODEL
    tr = min(tr, r)

    def body(dh_ref, w_ref, glu_ref, gate_ref, ob_ref, dglu_ref, dgate_ref, dob_ref):
        dm = _bdot(dh_ref[...], w_ref[...], _DIMS["nt"])
        glu_a = glu_ref[:, 0:d].astype(F32)
        sb = _sigmoid(glu_ref[:, d:2 * d].astype(F32))
        ga = _sigmoid(gate_ref[:, 0:d].astype(F32))
        gb = _sigmoid(gate_ref[:, d:2 * d].astype(F32))
        out_a = glu_a * sb
        dout_a = dm * ga
        dglu_ref[:, 0:d] = (dout_a * sb).astype(dglu_ref.dtype)
        dglu_ref[:, d:2 * d] = (dout_a * glu_a * sb * (1.0 - sb)).astype(dglu_ref.dtype)
        dgate_ref[:, 0:d] = (dm * out_a * ga * (1.0 - ga)).astype(dgate_ref.dtype)
        dgate_ref[:, d:2 * d] = (dm * ob_ref[...].astype(F32) * gb * (1.0 - gb)).astype(dgate_ref.dtype)
        dob_ref[...] = (dm * gb).astype(dob_ref.dtype)

    wide = pl.BlockSpec((tr, 2 * d), lambda i: (i, 0))
    row = pl.BlockSpec((tr, d), lambda i: (i, 0))
    return pl.pallas_call(
        body, name=name, grid=(r // tr,),
        in_specs=[row, pl.BlockSpec((d, d), lambda i: (0, 0)), wide, wide, row], out_specs=(wide, wide, row),
        out_shape=(jax.ShapeDtypeStruct((r, 2 * d), BF16), jax.ShapeDtypeStruct((r, 2 * d), BF16),
                   jax.ShapeDtypeStruct((r, d), BF16)),
        compiler_params=_cparams(("parallel",)),
    )(dh, w_mix, glu, gates, out_b)


def _ssm_mats(lam_re, lam_im, log_dt, b_re, b_im, c_re, c_im, nc):
    hp = lax.Precision.HIGHEST
    t = SSM_CHUNK
    nq = SSM_GROUPS // 8
    lam = lax.complex(lam_re, lam_im)
    z = lam * jnp.exp(log_dt)[:, None]
    ks = jnp.arange(t + 1, dtype=F32)
    apow = jnp.exp(ks[:, None, None] * z[None])
    bbar = ((apow[1] - 1.0) / lam)[..., None] * lax.complex(b_re, b_im)
    c = lax.complex(c_re, c_im)

    ca = c[None] * apow[:, :, None, :]
    kmat = jnp.einsum("kgnp,gpm->kgnm", ca, bbar, precision=hp).real
    ii = np.arange(t)
    lag = ii[None, :] - ii[:, None]
    kt = kmat[np.clip(lag, 0, t)] * jnp.asarray(lag >= 0, F32)[:, :, None, None, None]
    kt = kt.reshape(t, t, nq, 8, SSM_GROUP, SSM_GROUP)
    m_c = kt.transpose(2, 0, 3, 5, 1, 4).reshape(nq, 1024, LANE)

    arev = jnp.exp((float(t - 1) - ks[:t])[:, None, None] * z[None])
    w = arev[:, :, :, None] * bbar[None]
    wr = jnp.stack([w.real, w.imag]).reshape(2, t, nq, 8, SSM_STATE, SSM_GROUP)
    bw_c = wr.transpose(2, 1, 3, 5, 0, 4).reshape(nq, 1024, LANE)

    ca1 = ca[1:]
    cr = jnp.stack([ca1.real, -ca1.imag]).reshape(2, t, nq, 8, SSM_GROUP, SSM_STATE)
    cm_c = cr.transpose(2, 0, 3, 5, 1, 4).reshape(nq, 1024, LANE)

    def tiles(v):
        vq = jnp.concatenate([v.real.reshape(nq, 512), v.imag.reshape(nq, 512)], axis=1)
        return jnp.broadcast_to(vq.reshape(nq, 8, 1, LANE), (nq, 8, 8, LANE))

    return m_c, bw_c, cm_c, tiles(apow[t]), tiles(jnp.exp(float(nc) * z))


_BD_M = (LANE, SSM_GROUP)
_BD_BW = (LANE, SSM_STATE)
_BD_CM = (512, SSM_GROUP)


def _bd_perm(cn):
    rr = lax.broadcasted_iota(jnp.int32, (1024, 1024), 0)
    cc = lax.broadcasted_iota(jnp.int32, (1024, 1024), 1)
    sh = cn.bit_length() - 1
    src = ((rr >> 7) << sh) + (((rr & (LANE - 1)) >> sh) << (3 + sh)) + (rr & (cn - 1))
    return jnp.where(src == cc, 1.0, 0.0).astype(BF16)


def _bd_rowgroup(span):
    r = lax.broadcasted_iota(jnp.int32, (1024, LANE), 0)
    return (r & (span - 1)) >> ((span // 8).bit_length() - 1)


def _bd_expand(name, kind, compact):
    span, cn = kind
    nq = compact.shape[0]

    def body(c_ref, o_ref, perm_scr):
        @pl.when(pl.program_id(0) == 0)
        def _():
            perm_scr[...] = _bd_perm(cn)

        x = c_ref[...]
        grp = _bd_rowgroup(span)
        xcat = jnp.concatenate([jnp.where(grp == h, x, 0.0) for h in range(8)], axis=1)
        o_ref[...] = _bdot(xcat, perm_scr[...], _DIMS["nn"]).astype(o_ref.dtype)

    return pl.pallas_call(
        body, name=name, grid=(nq,), in_specs=[pl.BlockSpec((None, 1024, LANE), lambda q: (q, 0, 0))],
        out_specs=pl.BlockSpec((None, 1024, 1024), lambda q: (q, 0, 0)),
        out_shape=jax.ShapeDtypeStruct((nq, 1024, 1024), BF16),
        scratch_shapes=[pltpu.VMEM((1024, 1024), BF16)],
        compiler_params=_cparams(("arbitrary",)),
    )(compact)


def _bd_reduce(name, kind, dbig):
    span, cn = kind
    nq = dbig.shape[0]

    def body(g_ref, o_ref, perm_scr):
        @pl.when(pl.program_id(0) == 0)
        def _():
            perm_scr[...] = _bd_perm(cn)

        back = _bdot(g_ref[...], perm_scr[...], _DIMS["nt"])
        grp = _bd_rowgroup(span)
        out = jnp.zeros((1024, LANE), F32)
        for h in range(8):
            out = jnp.where(grp == h, back[:, h * LANE:(h + 1) * LANE], out)
        o_ref[...] = out

    return pl.pallas_call(
        body, name=name, grid=(nq,), in_specs=[pl.BlockSpec((None, 1024, 1024), lambda q: (q, 0, 0))],
        out_specs=pl.BlockSpec((None, 1024, LANE), lambda q: (q, 0, 0)),
        out_shape=jax.ShapeDtypeStruct((nq, 1024, LANE), F32),
        scratch_shapes=[pltpu.VMEM((1024, 1024), BF16)],
        compiler_params=_cparams(("arbitrary",)),
    )(dbig)


def _x_tile_specs(nc, nq):
    return [pl.BlockSpec((nc, LANE), lambda q, t, i=i: (0, i * nq + q)) for i in range(SSM_CHUNK)]


def _cat_tiles(refs):
    return jnp.concatenate([r[...] for r in refs], axis=1)


def _ssm_w(name, x8, bw):
    nc = x8.shape[0]
    nq = bw.shape[0]

    def body(*refs):
        xq = _cat_tiles(refs[:8])
        refs[9][...] = _bdot(xq, refs[8][...], _DIMS["nn"])

    return pl.pallas_call(
        body, name=name, grid=(nq, 8),
        in_specs=_x_tile_specs(nc, nq) + [pl.BlockSpec((None, 1024, LANE), lambda q, t: (q, 0, t))],
        out_specs=pl.BlockSpec((None, None, nc, LANE), lambda q, t: (q, t, 0, 0)),
        out_shape=jax.ShapeDtypeStruct((nq, 8, nc, LANE), F32),
        compiler_params=_cparams(("parallel", "arbitrary")),
    )(*([x8] * 8), bw)


def _ssm_scan(name, w4, a_t, aseg_t, *, reverse, sprev4=None):
    nq, _, nc, _ = w4.shape
    ns = nc // 8
    with_da = sprev4 is not None

    def body(*refs):
        w_ref, a_ref, aseg_ref = refs[:3]
        s_ref = refs[3] if with_da else None
        o_ref = refs[4] if with_da else refs[3]
        da_ref = refs[5] if with_da else None
        sgn = -1.0 if reverse else 1.0
        ar = [a_ref[j] for j in range(4)]
        ai = [sgn * a_ref[j + 4] for j in range(4)]
        gr = [aseg_ref[j] for j in range(4)]
        gi = [sgn * aseg_ref[j + 4] for j in range(4)]
        zero = tuple(jnp.zeros((8, LANE), F32) for _ in range(8))

        def rows(tt):
            return pl.ds((ns - 1 - tt) if reverse else tt, 8, stride=ns)

        def step(carry, w):
            new_r = [ar[j] * carry[j] - ai[j] * carry[j + 4] + w[j] for j in range(4)]
            new_i = [ar[j] * carry[j + 4] + ai[j] * carry[j] + w[j + 4] for j in range(4)]
            return tuple(new_r + new_i)

        def pass1(tt, carry):
            return step(carry, [w_ref[j, rows(tt), :] for j in range(8)])

        ends = lax.fori_loop(0, ns, pass1, zero)
        sub = lax.broadcasted_iota(jnp.int32, (8, LANE), 0)
        init = list(zero)
        order = range(7, 0, -1) if reverse else range(0, 7)
        for s in order:
            nxt = s - 1 if reverse else s + 1
            cand_r = [gr[j] * init[j] - gi[j] * init[j + 4] + ends[j] for j in range(4)]
            cand_i = [gr[j] * init[j + 4] + gi[j] * init[j] + ends[j + 4] for j in range(4)]
            cand = cand_r + cand_i
            shift = 7 if reverse else 1
            init = [jnp.where(sub == nxt, pltpu.roll(cand[j], shift, axis=0), init[j]) for j in range(8)]

        def pass2(tt, state):
            carry, acc = state
            r = rows(tt)
            for j in range(8):
                o_ref[j, r, :] = carry[j]
            if with_da:
                sp = [s_ref[j, r, :] for j in range(8)]
                acc_r = [acc[j] + carry[j] * sp[j] + carry[j + 4] * sp[j + 4] for j in range(4)]
                acc_i = [acc[j + 4] + carry[j + 4] * sp[j] - carry[j] * sp[j + 4] for j in range(4)]
                acc = tuple(acc_r + acc_i)
            return step(carry, [w_ref[j, r, :] for j in range(8)]), acc

        _, acc = lax.fori_loop(0, ns, pass2, (tuple(init), zero))
        if with_da:
            for j in range(8):
                da_ref[j] = acc[j]

    big = pl.BlockSpec((None, 8, nc, LANE), lambda q: (q, 0, 0, 0))
    small = pl.BlockSpec((None, 8, 8, LANE), lambda q: (q, 0, 0, 0))
    in_specs = [big, small, small] + ([big] if with_da else [])
    ops = [w4, a_t, aseg_t] + ([sprev4] if with_da else [])
    out_specs = (big, small) if with_da else big
    big_s = jax.ShapeDtypeStruct((nq, 8, nc, LANE), F32)
    out_shape = (big_s, jax.ShapeDtypeStruct((nq, 8, 8, LANE), F32)) if with_da else big_s
    return pl.pallas_call(
        body, name=name, grid=(nq,), in_specs=in_specs, out_specs=out_specs, out_shape=out_shape,
        compiler_params=_cparams(("parallel",)),
    )(*ops)


def _ssm_y(name, x8, sprev4, m_mat, cm_mat):
    nc = x8.shape[0]
    nq = m_mat.shape[0]

    def body(*refs):
        xq = _cat_tiles(refs[:8])
        s_ref, m_ref, cm_ref, o_ref = refs[8:12]
        sq = jnp.concatenate([s_ref[t] for t in range(8)], axis=1)
        o_ref[...] = _bdot(xq, m_ref[...], _DIMS["nn"]) + _bdot(sq, cm_ref[...], _DIMS["nn"])

    col = pl.BlockSpec((None, 1024, LANE), lambda q, j: (q, 0, j))
    return pl.pallas_call(
        body, name=name, grid=(nq, 8),
        in_specs=_x_tile_specs(nc, nq) + [pl.BlockSpec((None, 8, nc, LANE), lambda q, j: (q, 0, 0, 0)), col, col],
        out_specs=pl.BlockSpec((nc, LANE), lambda q, j: (0, j * nq + q)),
        out_shape=jax.ShapeDtypeStruct((nc, 8 * SSM_WIDTH), F32),
        compiler_params=_cparams(("parallel", "arbitrary")),
    )(*([x8] * 8), sprev4, m_mat, cm_mat)


def _ssm_ds(name, dz8, sprev4, cm_mat):
    nc = dz8.shape[0]
    nq = cm_mat.shape[0]

    def body(*refs):
        dyq = _cat_tiles(refs[:8]).astype(BF16)
        s_ref, cm_ref, ds_ref, dcm_ref = refs[8:12]
        ds_ref[...] = _bdot(dyq, cm_ref[...], _DIMS["nt"])
        dcm_ref[...] = _bdot(s_ref[...], dyq, _DIMS["tn"])

    tile = pl.BlockSpec((None, None, nc, LANE), lambda q, t: (q, t, 0, 0))
    rowblk = pl.BlockSpec((None, LANE, 1024), lambda q, t: (q, t, 0))
    return pl.pallas_call(
        body, name=name, grid=(nq, 8),
        in_specs=_x_tile_specs(nc, nq) + [tile, rowblk],
        out_specs=(tile, rowblk),
        out_shape=(jax.ShapeDtypeStruct((nq, 8, nc, LANE), F32), jax.ShapeDtypeStruct((nq, 1024, 1024), F32)),
        compiler_params=_cparams(("parallel", "arbitrary")),
    )(*([dz8] * 8), sprev4, cm_mat)


def _ssm_dx(name, dz8, g4, x8, m_mat, bw_mat, d8):
    nc = dz8.shape[0]
    nq = m_mat.shape[0]

    def body(*refs):
        dyq = _cat_tiles(refs[:8]).astype(BF16)
        g_ref, x_ref, m_ref, bw_ref, d_ref, dzi_ref, dx_ref, dm_ref, dbw_ref = refs[8:17]
        gq = jnp.concatenate([g_ref[t] for t in range(8)], axis=1).astype(BF16)
        dx = _bdot(dyq, m_ref[...], _DIMS["nt"]) + _bdot(gq, bw_ref[...], _DIMS["nt"])
        dx_ref[...] = (dx + d_ref[...] * dzi_ref[...]).astype(dx_ref.dtype)
        xi = x_ref[...]
        dm_ref[...] = _bdot(xi, dyq, _DIMS["tn"])
        dbw_ref[...] = _bdot(xi, gq, _DIMS["tn"])

    xtile = pl.BlockSpec((nc, LANE), lambda q, i: (0, i * nq + q))
    rowblk = pl.BlockSpec((None, LANE, 1024), lambda q, i: (q, i, 0))
    return pl.pallas_call(
        body, name=name, grid=(nq, 8),
        in_specs=_x_tile_specs(nc, nq) + [pl.BlockSpec((None, 8, nc, LANE), lambda q, i: (q, 0, 0, 0)), xtile, rowblk, rowblk,
                                          pl.BlockSpec((1, LANE), lambda q, i: (0, q)), xtile],
        out_specs=(xtile, rowblk, rowblk),
        out_shape=(jax.ShapeDtypeStruct((nc, 8 * SSM_WIDTH), BF16), jax.ShapeDtypeStruct((nq, 1024, 1024), F32),
                   jax.ShapeDtypeStruct((nq, 1024, 1024), F32)),
        compiler_params=_cparams(("parallel", "arbitrary")),
    )(*([dz8] * 8), g4, x8, m_mat, bw_mat, d8, dz8)


CUM_BLK = 256


def _split3(x):
    hi = x.astype(BF16)
    r1 = x - hi.astype(F32)
    mid = r1.astype(BF16)
    lo = (r1 - mid.astype(F32)).astype(BF16)
    return hi, mid, lo


def _tri_dot(x, tri):
    hi, mid, lo = _split3(x)
    d = _DIMS["nn"]
    return _bdot(hi, tri, d) + _bdot(mid, tri, d) + _bdot(lo, tri, d)


def _tri(n, lower):
    r = lax.broadcasted_iota(jnp.int32, (n, n), 0)
    c = lax.broadcasted_iota(jnp.int32, (n, n), 1)
    return jnp.where((r >= c) if lower else (r <= c), 1.0, 0.0).astype(BF16)


def _fox_cum(name, fproj, bcol):
    seq = fproj.shape[0]
    blk = min(CUM_BLK, seq)

    def body(f_ref, b_ref, o_ref, carry_ref):
        i = pl.program_id(0)

        @pl.when(i == 0)
        def _():
            carry_ref[...] = jnp.zeros_like(carry_ref)

        z = f_ref[...].T + b_ref[...]
        logf = jnp.minimum(z, 0.0) - jnp.log(1.0 + jnp.exp(-jnp.abs(z)))
        carry = carry_ref[...]
        cum = _tri_dot(logf, _tri(blk, lower=False)) + jnp.tile(carry, (1, blk // LANE))
        o_ref[...] = cum[0:8, :]
        carry_ref[...] = carry + jnp.sum(logf, axis=1, keepdims=True)

    return pl.pallas_call(
        body, name=name, grid=(seq // blk,),
        in_specs=[pl.BlockSpec((blk, LANE), lambda i: (i, 0)), pl.BlockSpec((LANE, 1), lambda i: (0, 0))],
        out_specs=pl.BlockSpec((8, blk), lambda i: (0, i)),
        out_shape=jax.ShapeDtypeStruct((8, seq), F32),
        scratch_shapes=[pltpu.VMEM((LANE, LANE), F32)],
        compiler_params=_cparams(("arbitrary",)),
    )(fproj, bcol)


def _fox_cum_bwd(name, dcs, fproj, bcol):
    seq = fproj.shape[0]
    blk = min(CUM_BLK, seq)
    n = seq // blk

    def body(dc_ref, f_ref, b_ref, df_ref, db_ref, carry_ref, acc_ref):
        i = pl.program_id(0)

        @pl.when(i == 0)
        def _():
            carry_ref[...] = jnp.zeros_like(carry_ref)
            acc_ref[...] = jnp.zeros_like(acc_ref)

        r = lax.broadcasted_iota(jnp.int32, (LANE, FOX_WIDTH), 0)
        c = lax.broadcasted_iota(jnp.int32, (LANE, FOX_WIDTH), 1)
        want = (r >> 1) * LANE + jnp.where((r & 1) == 0, FOX_HEAD_DIM, 0)
        sel = jnp.where(jnp.logical_and(r < FOX_HEADS, c == want), 1.0, 0.0).astype(BF16)
        hi, mid, lo = _split3(dc_ref[...])
        nt = _DIMS["nt"]
        dc = _bdot(sel, hi, nt) + _bdot(sel, mid, nt) + _bdot(sel, lo, nt)
        carry = carry_ref[...]
        dlogf = _tri_dot(dc, _tri(blk, lower=True)) + jnp.tile(carry, (1, blk // LANE))
        carry_ref[...] = carry + jnp.sum(dc, axis=1, keepdims=True)
        z = f_ref[...].T + b_ref[...]
        dft = dlogf / (1.0 + jnp.exp(z))
        df_ref[...] = dft.T.astype(df_ref.dtype)
        acc_ref[...] += jnp.sum(dft, axis=1, keepdims=True)

        @pl.when(i == n - 1)
        def _():
            db_ref[...] = acc_ref[...]

    return pl.pallas_call(
        body, name=name, grid=(n,),
        in_specs=[pl.BlockSpec((blk, FOX_WIDTH), lambda i: (n - 1 - i, 0)), pl.BlockSpec((blk, LANE), lambda i: (n - 1 - i, 0)),
                  pl.BlockSpec((LANE, 1), lambda i: (0, 0))],
        out_specs=(pl.BlockSpec((blk, LANE), lambda i: (n - 1 - i, 0)), pl.BlockSpec((LANE, LANE), lambda i: (0, 0))),
        out_shape=(jax.ShapeDtypeStruct((seq, LANE), BF16), jax.ShapeDtypeStruct((LANE, LANE), F32)),
        scratch_shapes=[pltpu.VMEM((LANE, LANE), F32), pltpu.VMEM((LANE, LANE), F32)],
        compiler_params=_cparams(("arbitrary",)),
    )(dcs, fproj, bcol)


FOX_BLK = 512
FOX_SCALE = FOX_HEAD_DIM ** -0.5


def _fox_head_mask(shape, hh):
    lane = lax.broadcasted_iota(jnp.int32, shape, 1)
    return (lane < FOX_HEAD_DIM) if hh == 0 else (lane >= FOX_HEAD_DIM)


def _fox_bias(cum_ref, hh, q0, k0, blk):
    c0 = jnp.max(cum_ref[hh:hh + 1, pl.ds(q0, LANE)], axis=1, keepdims=True)
    return c0 - cum_ref[hh:hh + 1, pl.ds(k0, blk)]


def _fox_fwd(name, qkv, cum_t):
    seq = qkv.shape[0]
    blk = min(FOX_BLK, seq)
    nb = seq // blk
    npair = FOX_HEADS // 2

    def body(q_ref, k_ref, v_ref, cum_ref, o_ref, lse_ref):
        iq = pl.program_id(1)
        q0 = pl.multiple_of(iq * blk, blk)
        qv = q_ref[...]
        row = lax.broadcasted_iota(jnp.int32, (blk, blk), 0)
        col = lax.broadcasted_iota(jnp.int32, (blk, blk), 1)
        qhs = [jnp.where(_fox_head_mask(qv.shape, hh), qv, jnp.zeros_like(qv)) * FOX_SCALE for hh in range(2)]

        def block(kb, states, masked):
            k0 = pl.multiple_of(kb * blk, blk)
            kv = k_ref[pl.ds(k0, blk), :]
            vv = v_ref[pl.ds(k0, blk), :]
            new = []
            for hh in range(2):
                m, acc = states[hh]
                s = _bdot(qhs[hh], kv, _DIMS["nt"]) + _fox_bias(cum_ref, hh, q0, k0, blk)
                if masked:
                    s = jnp.where(row >= col, s, -jnp.inf)
                m_new = jnp.maximum(m, jnp.max(s, axis=1, keepdims=True))
                p = jnp.exp(s - m_new)
                vh = jnp.where(_fox_head_mask(vv.shape, hh), vv, jnp.ones_like(vv))
                acc = jnp.exp(m - m_new) * acc + _bdot(p, vh, _DIMS["nn"])
                new.append((m_new, acc))
            return tuple(new)

        init = (jnp.full((blk, 1), -jnp.inf, F32), jnp.zeros((blk, LANE), F32))
        states = lax.fori_loop(0, iq, lambda kb, st: block(kb, st, False), (init, init))
        states = block(iq, states, True)
        outs = []
        for hh in range(2):
            m, acc = states[hh]
            other = pltpu.roll(acc, FOX_HEAD_DIM, axis=1)
            outs.append(acc / other)
            lse_ref[hh] = m + jnp.log(jnp.where(_fox_head_mask(acc.shape, hh), other, acc))
        o_ref[...] = jnp.where(_fox_head_mask(outs[0].shape, 0), outs[0], outs[1]).astype(o_ref.dtype)

    return pl.pallas_call(
        body, name=name, grid=(npair, nb),
        in_specs=[pl.BlockSpec((blk, LANE), lambda p, i: (i, p)),
                  pl.BlockSpec((seq, LANE), lambda p, i: (0, npair + p)),
                  pl.BlockSpec((seq, LANE), lambda p, i: (0, 2 * npair + p)),
                  pl.BlockSpec((None, 2, seq), lambda p, i: (p, 0, 0))],
        out_specs=(pl.BlockSpec((blk, LANE), lambda p, i: (i, p)),
                   pl.BlockSpec((2, blk, LANE), lambda p, i: (p, i, 0))),
        out_shape=(jax.ShapeDtypeStruct((seq, FOX_WIDTH), BF16), jax.ShapeDtypeStruct((FOX_HEADS, seq, LANE), F32)),
        compiler_params=_cparams(("parallel", "arbitrary")),
    )(qkv, qkv, qkv, cum_t)


def _fox_bwd(name, qkv, cum_t, att, datt, lse):
    seq = qkv.shape[0]
    blk = min(FOX_BLK, seq)
    nb = seq // blk
    npair = FOX_HEADS // 2

    def body(q_ref, k_ref, v_ref, cum_ref, o_ref, do_ref, lse_ref, dq_ref, dk_ref, dv_ref, dcs_ref):
        iq = pl.program_id(1)
        q0 = pl.multiple_of(iq * blk, blk)

        @pl.when(iq == 0)
        def _():
            dk_ref[...] = jnp.zeros_like(dk_ref)
            dv_ref[...] = jnp.zeros_like(dv_ref)
            dcs_ref[...] = jnp.zeros_like(dcs_ref)

        qv = q_ref[...]
        dov = do_ref[...].astype(F32)
        ov = o_ref[...].astype(F32)
        row = lax.broadcasted_iota(jnp.int32, (blk, blk), 0)
        col = lax.broadcasted_iota(jnp.int32, (blk, blk), 1)
        low = _fox_head_mask((blk, LANE), 0)
        qhs, qones, dohbs, deltas, lses = [], [], [], [], []
        for hh in range(2):
            hm = _fox_head_mask(qv.shape, hh)
            qh = jnp.where(hm, qv, jnp.zeros_like(qv)) * FOX_SCALE
            qhs.append(qh)
            qones.append(jnp.where(hm, qh, jnp.ones_like(qh)))
            doh = jnp.where(hm, dov, 0.0)
            dohbs.append(doh.astype(BF16))
            deltas.append(jnp.sum(doh * ov, axis=1, keepdims=True))
            lses.append(jnp.tile(lse_ref[hh], (1, blk // LANE)))

        def block(kb, dqs, masked):
            k0 = pl.multiple_of(kb * blk, blk)
            kv = k_ref[pl.ds(k0, blk), :]
            vv = v_ref[pl.ds(k0, blk), :]
            new, dks, dvs = [], [], []
            for hh in range(2):
                s = _bdot(qhs[hh], kv, _DIMS["nt"]) + _fox_bias(cum_ref, hh, q0, k0, blk)
                p = jnp.exp(s - lses[hh])
                if masked:
                    p = jnp.where(row >= col, p, 0.0)
                dp = _bdot(dohbs[hh], vv, _DIMS["nt"])
                dsb = (p * (dp - deltas[hh])).astype(BF16)
                dks.append(_bdot(dsb, qones[hh], _DIMS["tn"]))
                dvs.append(_bdot(p, dohbs[hh], _DIMS["tn"]))
                kones = jnp.where(_fox_head_mask(kv.shape, hh), kv, jnp.ones_like(kv))
                new.append(dqs[hh] + _bdot(dsb, kones, _DIMS["nn"]))
            dk_ref[pl.ds(k0, blk), :] += jnp.where(low, dks[0], dks[1])
            dv_ref[pl.ds(k0, blk), :] += dvs[0] + dvs[1]
            dcs_ref[pl.ds(k0, blk), :] -= jnp.where(low, dks[1], dks[0])
            return tuple(new)

        init = jnp.zeros((blk, LANE), F32)
        dqs = lax.fori_loop(0, iq, lambda kb, a: block(kb, a, False), (init, init))
        dqs = block(iq, dqs, True)
        dcs_ref[pl.ds(q0, blk), :] += jnp.where(low, dqs[1], dqs[0])
        dq_ref[...] = (jnp.where(low, dqs[0], dqs[1]) * FOX_SCALE).astype(dq_ref.dtype)

    qblk = pl.BlockSpec((blk, LANE), lambda p, i: (i, p))
    full = pl.BlockSpec((seq, LANE), lambda p, i: (0, p))
    return pl.pallas_call(
        body, name=name, grid=(npair, nb),
        in_specs=[qblk,
                  pl.BlockSpec((seq, LANE), lambda p, i: (0, npair + p)),
                  pl.BlockSpec((seq, LANE), lambda p, i: (0, 2 * npair + p)),
                  pl.BlockSpec((None, 2, seq), lambda p, i: (p, 0, 0)),
                  qblk, qblk,
                  pl.BlockSpec((2, blk, LANE), lambda p, i: (p, i, 0))],
        out_specs=(qblk, full, full, full),
        out_shape=(jax.ShapeDtypeStruct((seq, FOX_WIDTH), BF16), jax.ShapeDtypeStruct((seq, FOX_WIDTH), F32),
                   jax.ShapeDtypeStruct((seq, FOX_WIDTH), F32), jax.ShapeDtypeStruct((seq, FOX_WIDTH), F32)),
        compiler_params=_cparams(("arbitrary", "arbitrary")),
    )(qkv, qkv, qkv, cum_t, att, datt, lse)


MEM_SCALE = MEM_HEAD_DIM ** -0.5


def _mem_probs(qh, kh):
    s = _bdot(qh, kh, _DIMS["nt"]) * MEM_SCALE
    p = jnp.exp(s - jnp.max(s, axis=1, keepdims=True))
    return p / jnp.sum(p, axis=1, keepdims=True)


def _mem_fwd(name, q2, kv, *, tr=512):
    seq = q2.shape[0]
    mlen = kv.shape[0]
    tr = min(tr, seq)

    def body(q_ref, kv_ref, o_ref):
        for h in range(MEM_HEADS):
            sl = slice(h * MEM_HEAD_DIM, (h + 1) * MEM_HEAD_DIM)
            sv = slice(MEM_WIDTH + h * MEM_HEAD_DIM, MEM_WIDTH + (h + 1) * MEM_HEAD_DIM)
            p = _mem_probs(q_ref[:, sl], kv_ref[:, sl])
            o_ref[:, sl] = _bdot(p, kv_ref[:, sv], _DIMS["nn"]).astype(o_ref.dtype)

    return pl.pallas_call(
        body, name=name, grid=(seq // tr,),
        in_specs=[pl.BlockSpec((tr, MEM_WIDTH), lambda i: (i, 0)), pl.BlockSpec((mlen, 2 * MEM_WIDTH), lambda i: (0, 0))],
        out_specs=pl.BlockSpec((tr, MEM_WIDTH), lambda i: (i, 0)),
        out_shape=jax.ShapeDtypeStruct((seq, MEM_WIDTH), BF16),
        compiler_params=_cparams(("parallel",)),
    )(q2, kv)


def _mem_bwd(name, q2, kv, do2, *, tr=512):
    seq = q2.shape[0]
    mlen = kv.shape[0]
    tr = min(tr, seq)

    def body(q_ref, kv_ref, do_ref, dq_ref, dkv_ref):
        i = pl.program_id(0)

        @pl.when(i == 0)
        def _():
            dkv_ref[...] = jnp.zeros_like(dkv_ref)

        for h in range(MEM_HEADS):
            sl = slice(h * MEM_HEAD_DIM, (h + 1) * MEM_HEAD_DIM)
            sv = slice(MEM_WIDTH + h * MEM_HEAD_DIM, MEM_WIDTH + (h + 1) * MEM_HEAD_DIM)
            qh = q_ref[:, sl]
            kh = kv_ref[:, sl]
            doh = do_ref[:, sl].astype(BF16)
            p = _mem_probs(qh, kh)
            dp = _bdot(doh, kv_ref[:, sv], _DIMS["nt"])
            ds = (p * (dp - jnp.sum(p * dp, axis=1, keepdims=True)) * MEM_SCALE).astype(BF16)
            dq_ref[:, sl] = _bdot(ds, kh, _DIMS["nn"]).astype(dq_ref.dtype)
            dkv_ref[:, sl] += _bdot(ds, qh, _DIMS["tn"])
            dkv_ref[:, sv] += _bdot(p, doh, _DIMS["tn"])

    row = pl.BlockSpec((tr, MEM_WIDTH), lambda i: (i, 0))
    kvs = pl.BlockSpec((mlen, 2 * MEM_WIDTH), lambda i: (0, 0))
    return pl.pallas_call(
        body, name=name, grid=(seq // tr,), in_specs=[row, kvs, row], out_specs=(row, kvs),
        out_shape=(jax.ShapeDtypeStruct((seq, MEM_WIDTH), BF16), jax.ShapeDtypeStruct((mlen, 2 * MEM_WIDTH), F32)),
        compiler_params=_cparams(("arbitrary",)),
    )(q2, kv, do2)


_HBM = pl.BlockSpec(memory_space=pl.ANY)
_HBM_ONLY = pl.BlockSpec(memory_space=pltpu.HBM)
_MESH = pl.DeviceIdType.MESH


def _mesh_place():
    x, y, c = lax.axis_index("x"), lax.axis_index("y"), lax.axis_index("c")
    other_chips = [(1 - x, y), (x, 1 - y), (1 - x, 1 - y)]
    return x, y, c, other_chips


def _gather_all(name, arrays):
    n = len(arrays)

    def body(*refs):
        ins, outs = refs[:n], refs[n:2 * n]
        send_sems, recv_sems, local_sems = refs[2 * n:]
        x, y, c, chips = _mesh_place()
        me, sibling = (x, y, c), (x, y, 1 - c)

        def slot(a, place):
            px, py, pc = place
            return outs[a].at[4 * px + 2 * py + pc]

        def copy(a, k, block, to, src=None):
            return pltpu.make_async_remote_copy(
                src_ref=slot(a, block) if src is None else src, dst_ref=slot(a, block),
                send_sem=send_sems.at[a, k], recv_sem=recv_sems.at[a, k], device_id=to, device_id_type=_MESH)

        mine = [pltpu.make_async_copy(ins[a], slot(a, me), local_sems.at[a]) for a in range(n)]
        for cp in mine:
            cp.start()
        first = []
        for a in range(n):
            first.append(copy(a, 0, me, sibling, src=ins[a]))
            first += [copy(a, 1 + j, me, (*chip, c), src=ins[a]) for j, chip in enumerate(chips)]
        for cp in first:
            cp.start()
        passed = []
        for j, chip in enumerate(chips):
            for a in range(n):
                copy(a, 1 + j, (*chip, c), me).wait_recv()
                fwd = copy(a, 4 + j, (*chip, c), sibling)
                fwd.start()
                passed.append(fwd)
        for a in range(n):
            copy(a, 0, sibling, me).wait_recv()
            for j, chip in enumerate(chips):
                copy(a, 4 + j, (*chip, 1 - c), me).wait_recv()
        for cp in first + passed:
            cp.wait_send()
        for cp in mine:
            cp.wait()

    out_shape = tuple(jax.ShapeDtypeStruct((N_DEV,) + arr.shape, arr.dtype) for arr in arrays)
    return pl.pallas_call(
        body, name=name, in_specs=[_HBM] * n, out_specs=tuple([_HBM] * n), out_shape=out_shape,
        scratch_shapes=[pltpu.SemaphoreType.DMA((n, N_DEV - 1)), pltpu.SemaphoreType.DMA((n, N_DEV - 1)),
                        pltpu.SemaphoreType.DMA((n,))],
    )(*arrays)


_SEM = pl.BlockSpec(memory_space=pltpu.SEMAPHORE)
_DATAFLOW = pltpu.SideEffectType.DATAFLOW_SIDE_EFFECTING


def _device_index():
    return (4 * lax.axis_index("x") + 2 * lax.axis_index("y") + lax.axis_index("c")).astype(jnp.int32).reshape(1)


def _place_own(name, pieces, *, stacked_src, after=None):
    n = len(pieces)
    n_in = n + (after is not None)

    def body(me_ref, *refs):
        for a in range(n):
            refs[n_in + a][...] = refs[a][...]

    def spec(shape):
        return pl.BlockSpec((None,) + tuple(shape), lambda i, me_ref: (me_ref[0],) + (0,) * len(shape))

    shapes = [p.shape[1:] if stacked_src else p.shape for p in pieces]
    if stacked_src:
        in_specs = [spec(s) for s in shapes]
    else:
        in_specs = [pl.BlockSpec(tuple(s), lambda i, me_ref, nd=len(s): (0,) * nd) for s in shapes]
    operands = list(pieces)
    if after is not None:
        in_specs.append(_HBM)
        operands.append(after)
    return pl.pallas_call(
        body, name=name,
        grid_spec=pltpu.PrefetchScalarGridSpec(num_scalar_prefetch=1, grid=(1,), in_specs=in_specs,
                                               out_specs=tuple(spec(s) for s in shapes)),
        out_shape=tuple(jax.ShapeDtypeStruct((N_DEV,) + tuple(s), p.dtype) for s, p in zip(shapes, pieces)),
        compiler_params=_cparams(("arbitrary",)),
    )(_device_index(), *operands)


def _peer_places():
    x, y, c = lax.axis_index("x"), lax.axis_index("y"), lax.axis_index("c")
    peers = []
    for k in range(N_DEV - 1):
        flip = k + 1
        px = 1 - x if flip & 4 else x
        py = 1 - y if flip & 2 else y
        pc = 1 - c if flip & 1 else c
        peers.append((px, py, pc, 4 * px + 2 * py + pc))
    return 4 * x + 2 * y + c, peers


def _direct_copy(srcs, lands, send_sems, recv_sems, a, k, me, peer, scatter):
    px, py, pc, pidx = peer
    return pltpu.make_async_remote_copy(
        src_ref=srcs[a].at[pidx] if scatter else srcs[a], dst_ref=lands[a].at[me],
        send_sem=send_sems.at[a * (N_DEV - 1) + k], recv_sem=recv_sems.at[a * (N_DEV - 1) + k],
        device_id=(px, py, pc), device_id_type=_MESH)


def _send_start(name, srcs, lands, *, scatter):
    n = len(srcs)

    def body(*refs):
        src_refs, land_refs = refs[:n], refs[n:2 * n]
        send_sems, recv_sems = refs[2 * n], refs[2 * n + 1]
        token = refs[-1]
        me, peers = _peer_places()
        for k, peer in enumerate(peers):
            for a in range(n):
                _direct_copy(src_refs, land_refs, send_sems, recv_sems, a, k, me, peer, scatter).start()
        token[...] = jnp.zeros_like(token)

    hbm_shapes = [pltpu.HBM(t.shape, t.dtype) for t in list(srcs) + list(lands)]
    outs = pl.pallas_call(
        body, name=name,
        out_shape=(pltpu.SemaphoreType.DMA((n * (N_DEV - 1),)), pltpu.SemaphoreType.DMA((n * (N_DEV - 1),)), *hbm_shapes,
                   jax.ShapeDtypeStruct((8, LANE), F32)),
        in_specs=[_HBM_ONLY] * (2 * n),
        out_specs=(_SEM, _SEM, *([_HBM_ONLY] * (2 * n)), pl.BlockSpec(memory_space=pltpu.VMEM)),
        input_output_aliases={i: 2 + i for i in range(2 * n)},
        compiler_params=pltpu.CompilerParams(has_side_effects=_DATAFLOW),
    )(*[pltpu.with_memory_space_constraint(t, pltpu.HBM) for t in list(srcs) + list(lands)])
    return outs[0], outs[1], outs[2:2 + n], outs[2 + n:2 + 2 * n], outs[-1]


def _send_wait(name, send_sems, recv_sems, srcs, lands, after, *, scatter):
    n = len(srcs)
    afters = list(after) if isinstance(after, (tuple, list)) else [after]

    def body(*refs):
        src_refs, land_refs = refs[:n], refs[n:2 * n]
        send_sems, recv_sems = refs[2 * n], refs[2 * n + 1]
        me, peers = _peer_places()
        for k, peer in enumerate(peers):
            for a in range(n):
                cp = _direct_copy(src_refs, land_refs, send_sems, recv_sems, a, k, me, peer, scatter)
                cp.wait_send()
                cp.wait_recv()

    hbm_shapes = [pltpu.HBM(t.shape, t.dtype) for t in list(srcs) + list(lands)]
    outs = pl.pallas_call(
        body, name=name, out_shape=tuple(hbm_shapes),
        in_specs=[_HBM_ONLY] * (2 * n) + [_SEM, _SEM] + [_HBM] * len(afters),
        out_specs=tuple([_HBM_ONLY] * (2 * n)),
        input_output_aliases={i: i for i in range(2 * n)},
        compiler_params=pltpu.CompilerParams(has_side_effects=_DATAFLOW),
    )(*srcs, *lands, send_sems, recv_sems, *afters)
    return outs[n:]


def _unstack_cols(name, stacked):
    n, rows, cols = stacked.shape

    def body(i_ref, o_ref):
        o_ref[...] = i_ref[...]

    return pl.pallas_call(
        body, name=name, grid=(n,), in_specs=[pl.BlockSpec((None, rows, cols), lambda k: (k, 0, 0))],
        out_specs=pl.BlockSpec((rows, cols), lambda k: (0, k)),
        out_shape=jax.ShapeDtypeStruct((rows, n * cols), stacked.dtype),
        compiler_params=_cparams(("parallel",)),
    )(stacked)


def _restack_cols(name, mat):
    rows, width = mat.shape
    cols = width // N_DEV

    def body(i_ref, o_ref):
        o_ref[...] = i_ref[...]

    return pl.pallas_call(
        body, name=name, grid=(N_DEV,), in_specs=[pl.BlockSpec((rows, cols), lambda k: (0, k))],
        out_specs=pl.BlockSpec((None, rows, cols), lambda k: (k, 0, 0)),
        out_shape=jax.ShapeDtypeStruct((N_DEV, rows, cols), mat.dtype),
        compiler_params=_cparams(("parallel",)),
    )(mat)


def _remap_pieces(runs):
    plan = {}
    for du, dc, su, sc, ln in runs:
        while ln > 0:
            lane = dc % LANE
            take = min(ln, LANE - lane)
            plan.setdefault((du, dc // LANE), []).append((su, sc, take, lane))
            dc, sc, ln = dc + take, sc + take, ln - take
    return plan


def _remap(name, srcs, src_units, runs, *, out_units, out_cols, out_dtype, tr=256):
    rows = srcs[0].shape[-2]
    tr = min(tr, rows)
    plan = _remap_pieces(runs)
    n_src = len(srcs)
    stacked_out = out_units is not None
    n_tiles = out_cols // LANE

    def body(*refs):
        o_ref = refs[n_src]

        def src_tile(unit, t):
            ai, lead = src_units[unit]
            ref = refs[ai]
            sl = slice(t * LANE, (t + 1) * LANE)
            return (ref[:, sl] if lead is None else ref[lead, :, sl]).astype(F32)

        lane = lax.broadcasted_iota(jnp.int32, (tr, LANE), 1)
        for du in range(out_units if stacked_out else 1):
            for t in range(n_tiles):
                acc = jnp.zeros((tr, LANE), F32)
                for su, sc, ln, dl in plan.get((du if stacked_out else None, t), []):
                    st, so = sc // LANE, sc % LANE
                    first = src_tile(su, st)
                    if so == dl and so + ln <= LANE:
                        piece = first
                    else:
                        second = src_tile(su, st + 1) if so + ln > LANE else first
                        both = jnp.concatenate([first, second], axis=1)
                        piece = pltpu.roll(both, (dl - so) % (2 * LANE), axis=1)[:, 0:LANE]
                    acc = piece if (dl == 0 and ln == LANE) else jnp.where(
                        jnp.logical_and(lane >= dl, lane < dl + ln), piece, acc)
                if stacked_out:
                    o_ref[du, :, t * LANE:(t + 1) * LANE] = acc.astype(o_ref.dtype)
                else:
                    o_ref[:, t * LANE:(t + 1) * LANE] = acc.astype(o_ref.dtype)

    in_specs = []
    for arr in srcs:
        if arr.ndim == 2:
            in_specs.append(pl.BlockSpec((tr, arr.shape[1]), lambda i: (i, 0)))
        else:
            in_specs.append(pl.BlockSpec((arr.shape[0], tr, arr.shape[2]), lambda i: (0, i, 0)))
    if stacked_out:
        out_spec = pl.BlockSpec((out_units, tr, out_cols), lambda i: (0, i, 0))
        out_shape = jax.ShapeDtypeStruct((out_units, rows, out_cols), out_dtype)
    else:
        out_spec = pl.BlockSpec((tr, out_cols), lambda i: (i, 0))
        out_shape = jax.ShapeDtypeStruct((rows, out_cols), out_dtype)
    return pl.pallas_call(
        body, name=name, grid=(rows // tr,), in_specs=in_specs, out_specs=out_spec, out_shape=out_shape,
        compiler_params=_cparams(("parallel",)),
    )(*srcs)


def _proj_col(c):
    if c < PROJ_GATE0:
        return c
    if c < PROJ_GATE0 + FOX_HEADS:
        return PROJ_F0 + (c - PROJ_GATE0)
    return c - FOX_HEADS


def _win_runs():
    cuts = sorted(set([0, PROJ_GATE0, PROJ_GATE0 + FOX_HEADS, IN_WIDTH] + [SHARD_IN * k for k in range(N_DEV + 1)]))
    return [(lo // SHARD_IN, lo % SHARD_IN, _proj_col(lo), hi - lo) for lo, hi in zip(cuts[:-1], cuts[1:])]


def _assemble_win(name, stacked):
    runs = [(None, pc, k, sc, ln) for k, sc, pc, ln in _win_runs()]
    return _remap(name, [stacked], [(0, k) for k in range(N_DEV)], runs,
                  out_units=None, out_cols=PROJ_WIDTH, out_dtype=BF16)


def _disassemble_dwin(name, dw):
    runs = [(k, sc, 0, pc, ln) for k, sc, pc, ln in _win_runs()]
    return _remap(name, [dw], [(0, None)], runs, out_units=N_DEV, out_cols=SHARD_IN_PAD, out_dtype=BF16)


def _concat_cols(name, parts, *, tr=512):
    rows = parts[0].shape[0]
    tr = min(tr, rows)
    widths = [p.shape[1] for p in parts]
    total = sum(widths)

    def body(*refs):
        o_ref = refs[len(parts)]
        lo = 0
        for r, w in zip(refs[:len(parts)], widths):
            o_ref[:, lo:lo + w] = r[...].astype(o_ref.dtype)
            lo += w

    return pl.pallas_call(
        body, name=name, grid=(rows // tr,),
        in_specs=[pl.BlockSpec((tr, w), lambda i: (i, 0)) for w in widths],
        out_specs=pl.BlockSpec((tr, total), lambda i: (i, 0)),
        out_shape=jax.ShapeDtypeStruct((rows, total), BF16),
        compiler_params=_cparams(("parallel",)),
    )(*parts)


FFN_BLK = FFN_HIDDEN // 2


def _ffn_col(c):
    half, r = divmod(c, FFN_HIDDEN)
    blk, r = divmod(r, FFN_BLK)
    return blk * 2 * FFN_BLK + half * FFN_BLK + r


def _assemble_wffn(name, stacked):
    runs = [(None, _ffn_col(SHARD_FFN * k), k, 0, SHARD_FFN) for k in range(N_DEV)]
    return _remap(name, [stacked], [(0, k) for k in range(N_DEV)], runs,
                  out_units=None, out_cols=2 * FFN_HIDDEN, out_dtype=BF16)


def _disassemble_dwffn(name, dw):
    runs = [(k, 0, 0, _ffn_col(SHARD_FFN * k), SHARD_FFN) for k in range(N_DEV)]
    return _remap(name, [dw], [(0, None)], runs, out_units=N_DEV, out_cols=SHARD_FFN_PAD, out_dtype=BF16)


def _ffn_in_swiglu(name, xn, w, *, tm=512):
    rows, k = xn.shape
    tm = min(tm, rows)
    nblk = FFN_HIDDEN // FFN_BLK

    def body(x_ref, w_ref, f_ref, g_ref):
        f = _bdot(x_ref[...], w_ref[...], _DIMS["nn"])
        f_ref[...] = f.astype(f_ref.dtype)
        fa = f[:, 0:FFN_BLK]
        g_ref[...] = (fa * _sigmoid(fa) * f[:, FFN_BLK:2 * FFN_BLK]).astype(g_ref.dtype)

    return pl.pallas_call(
        body, name=name, grid=(nblk, rows // tm),
        in_specs=[pl.BlockSpec((tm, k), lambda j, i: (i, 0)), pl.BlockSpec((k, 2 * FFN_BLK), lambda j, i: (0, j))],
        out_specs=(pl.BlockSpec((tm, 2 * FFN_BLK), lambda j, i: (i, j)), pl.BlockSpec((tm, FFN_BLK), lambda j, i: (i, j))),
        out_shape=(jax.ShapeDtypeStruct((rows, 2 * FFN_HIDDEN), BF16), jax.ShapeDtypeStruct((rows, FFN_HIDDEN), BF16)),
        compiler_params=_cparams(("parallel", "arbitrary")),
    )(xn, w)


def _d_ffn_out_swiglu(name, dh, w_out, f, *, tm=512):
    rows, d = dh.shape
    tm = min(tm, rows)
    nblk = FFN_HIDDEN // FFN_BLK

    def body(dh_ref, w_ref, f_ref, df_ref):
        dg = _bdot(dh_ref[...], w_ref[...], _DIMS["nt"])
        fa = f_ref[:, 0:FFN_BLK].astype(F32)
        fb = f_ref[:, FFN_BLK:2 * FFN_BLK].astype(F32)
        s = _sigmoid(fa)
        df_ref[:, 0:FFN_BLK] = (dg * fb * s * (1.0 + fa * (1.0 - s))).astype(df_ref.dtype)
        df_ref[:, FFN_BLK:2 * FFN_BLK] = (dg * fa * s).astype(df_ref.dtype)

    wide = pl.BlockSpec((tm, 2 * FFN_BLK), lambda j, i: (i, j))
    return pl.pallas_call(
        body, name=name, grid=(nblk, rows // tm),
        in_specs=[pl.BlockSpec((tm, d), lambda j, i: (i, 0)), pl.BlockSpec((FFN_BLK, d), lambda j, i: (j, 0)), wide],
        out_specs=wide, out_shape=jax.ShapeDtypeStruct((rows, 2 * FFN_HIDDEN), BF16),
        compiler_params=_cparams(("parallel", "arbitrary")),
    )(dh, w_out, f)


def _adamw(name, parts, w, m, v, *, tr=128):
    rows, cols = w.shape
    n_parts = parts.shape[0]
    tr = min(tr, rows)
    assert rows % tr == 0, (name, rows, tr)
    c1 = 1.0 - ADAM_B1 ** ADAM_STEP
    c2 = 1.0 - ADAM_B2 ** ADAM_STEP

    def body(p_ref, w_ref, m_ref, v_ref, g_ref, d_ref, nm_ref, nv_ref):
        g = p_ref[0].astype(F32)
        for s in range(1, n_parts):
            g = g + p_ref[s].astype(F32)
        m_new = ADAM_B1 * m_ref[...] + (1.0 - ADAM_B1) * g
        v_new = ADAM_B2 * v_ref[...] + (1.0 - ADAM_B2) * (g * g)
        upd = (m_new / c1) / (jnp.sqrt(v_new / c2) + ADAM_EPS) + ADAM_WD * w_ref[...]
        g_ref[...] = g
        d_ref[...] = -ADAM_LR * upd
        nm_ref[...] = m_new
        nv_ref[...] = v_new

    row = pl.BlockSpec((tr, cols), lambda i: (i, 0))
    out = jax.ShapeDtypeStruct((rows, cols), F32)
    return pl.pallas_call(
        body, name=name, grid=(rows // tr,),
        in_specs=[pl.BlockSpec((n_parts, tr, cols), lambda i: (0, i, 0)), row, row, row],
        out_specs=(row, row, row, row), out_shape=(out, out, out, out),
        compiler_params=_cparams(("parallel",)),
    )(parts, w, m, v)


_WEIGHTS = ("norm_mix", "w_in", "b_forget", "lam_re", "lam_im", "log_dt", "b_re", "b_im", "c_re", "c_im",
            "d_skip", "w_glu", "w_fox_o", "w_mix_out", "norm_mem_q", "norm_mem_kv", "w_mem_q", "w_mem_kv",
            "w_mem_o", "norm_ffn", "w_ffn_in", "w_ffn_out", "norm_final")
_SHARDED = ("w_in", "w_glu", "w_fox_o", "w_mix_out", "w_mem_q", "w_mem_kv", "w_mem_o", "w_ffn_in", "w_ffn_out")
_SMALL = tuple(n for n in _WEIGHTS if n not in _SHARDED)
_PACK_COLS = 1024


def _pack(arrays):
    flat = jnp.concatenate([a.reshape(-1).astype(F32) for a in arrays])
    rows = -(-flat.shape[0] // _PACK_COLS)
    return jnp.pad(flat, (0, rows * _PACK_COLS - flat.shape[0])).reshape(rows, _PACK_COLS)


def _unpack(buf, like):
    flat = buf.reshape(-1)
    out, pos = [], 0
    for a in like:
        out.append(flat[pos:pos + a.size].reshape(a.shape))
        pos += a.size
    return out


def _mm(name, a, b, mode, m, n, k, out_dtype, tm=1024, tn=512, tk=1024, **kw):
    return _matmul(name, a, b, mode, m, n, k, out_dtype=out_dtype, tm=tm, tn=tn, tk=tk, **kw)


def kernel(x, mem, norm_mix, w_in, b_forget, lam_re, lam_im, log_dt, b_re, b_im, c_re, c_im, d_skip, w_glu, w_fox_o, w_mix_out, norm_mem_q, norm_mem_kv, w_mem_q, w_mem_kv, w_mem_o, norm_ffn, w_ffn_in, w_ffn_out, norm_final, loss_target, m_norm_mix, m_w_in, m_b_forget, m_lam_re, m_lam_im, m_log_dt, m_b_re, m_b_im, m_c_re, m_c_im, m_d_skip, m_w_glu, m_w_fox_o, m_w_mix_out, m_norm_mem_q, m_norm_mem_kv, m_w_mem_q, m_w_mem_kv, m_w_mem_o, m_norm_ffn, m_w_ffn_in, m_w_ffn_out, m_norm_final, v_norm_mix, v_w_in, v_b_forget, v_lam_re, v_lam_im, v_log_dt, v_b_re, v_b_im, v_c_re, v_c_im, v_d_skip, v_w_glu, v_w_fox_o, v_w_mix_out, v_norm_mem_q, v_norm_mem_kv, v_w_mem_q, v_w_mem_kv, v_w_mem_o, v_norm_ffn, v_w_ffn_in, v_w_ffn_out, v_norm_final):
    given = dict(locals())
    weights = {n: given[n] for n in _WEIGHTS}
    mom_m = {n: given["m_" + n] for n in _WEIGHTS}
    mom_v = {n: given["v_" + n] for n in _WEIGHTS}
    seq = x.shape[1]
    nc = seq // SSM_CHUNK
    d = D_MODEL
    xs, mems, tgt = x[0], mem[0], loss_target[0]

    def padcols(a, width):
        return jnp.pad(a, ((0, 0), (0, width - a.shape[1])))

    shards = [padcols(w_in[0].astype(BF16), SHARD_IN_PAD), w_glu[0].astype(BF16), w_fox_o[0].astype(BF16),
              w_mix_out[0].astype(BF16), w_mem_q[0].astype(BF16), w_mem_kv[0].astype(BF16),
              w_mem_o[0].astype(BF16), padcols(w_ffn_in[0].astype(BF16), SHARD_FFN_PAD), w_ffn_out[0].astype(BF16)]
    first = shards[:1]
    wsend, wrecv, first_thru, first_lands, wtoken = _send_start(
        "gather_w_in_start", first, _place_own("place_w_in_shard", first, stacked_src=False), scatter=False)
    rest = shards[1:]
    gsend, grecv, rest_thru, lands, gtoken = _send_start(
        "gather_rest_start", rest, _place_own("place_weight_shards", rest, stacked_src=False, after=wtoken),
        scatter=False)

    u = _rms_fwd("rms_mix", xs, norm_mix, after=gtoken)
    ssm_params = tuple(p[0] + wtoken[0, 0] for p in (lam_re, lam_im, log_dt, b_re, b_im, c_re, c_im))
    (m_c, bw_c, cm_c, a8, aseg), mats_vjp = jax.vjp(lambda *p: _ssm_mats(*p, nc), *ssm_params)
    m_b = _bd_expand("ssm_expand_m", _BD_M, m_c)
    bw_b = _bd_expand("ssm_expand_bw", _BD_BW, bw_c)
    cm_b = _bd_expand("ssm_expand_cm", _BD_CM, cm_c)
    win = _assemble_win("assemble_w_in", _send_wait(
        "gather_w_in_wait", wsend, wrecv, first_thru, first_lands, (u, m_b, bw_b, cm_b), scatter=False)[0])
    ussm = _mm("proj_ssm", u, win, "nn", seq, SSM_WIDTH, d, F32)
    qkv = _mm("proj_qkv", u, win, "nn", seq, 3 * FOX_WIDTH, d, BF16, tn=512, b_off=(0, SSM_WIDTH))
    gates = _mm("proj_gates", u, win, "nn", seq, 2 * d, d, BF16, tn=1024, b_off=(0, PROJ_GATE0))
    fproj = _mm("proj_forget", u, win, "nn", seq, LANE, d, F32, tn=LANE, b_off=(0, PROJ_F0))

    u8 = ussm.reshape(nc, SSM_CHUNK * SSM_WIDTH)
    d8 = jnp.tile(d_skip, (1, SSM_CHUNK))
    w4 = _ssm_w("ssm_w", u8, bw_b)
    sp4 = _ssm_scan("ssm_scan", w4, a8, aseg, reverse=False)
    y8 = _ssm_y("ssm_y", u8, sp4, m_b, cm_b)
    act = _ssm_post_fwd("ssm_act", y8, u8, d8).reshape(seq, SSM_WIDTH)

    bcol = jnp.pad(b_forget[0], (0, LANE - FOX_HEADS)).reshape(LANE, 1)
    cum_t = _fox_cum("fox_cum", fproj, bcol).reshape(FOX_HEADS // 2, 2, seq)
    att, lse = _fox_fwd("fox_fwd", qkv, cum_t)

    gathered = _send_wait("gather_rest_wait", gsend, grecv, rest_thru, lands, att, scatter=False)
    wglu = _unstack_cols("unstack_w_glu", gathered[0])
    wfoxo = _unstack_cols("unstack_w_fox_o", gathered[1])
    wmix = gathered[2].reshape(d, d)
    wmq = gathered[3].reshape(d, MEM_WIDTH)
    wmkv = gathered[4].reshape(d, 2 * MEM_WIDTH)
    wmo = _unstack_cols("unstack_w_mem_o", gathered[5])
    wffn_in = _assemble_wffn("assemble_w_ffn_in", gathered[6])
    wffn_out = gathered[7].reshape(FFN_HIDDEN, d)

    glu = _mm("glu", act, wglu, "nn", seq, 2 * d, SSM_WIDTH, BF16, tn=1024)
    out_b = _mm("fox_out", att, wfoxo, "nn", seq, d, FOX_WIDTH, BF16, tn=1024)

    mixin, h1 = _mix_fwd("mix_mix_out", glu, gates, out_b, wmix, xs)

    n1 = _rms_fwd("rms_mem_q", h1, norm_mem_q)
    q2 = _mm("mem_q", n1, wmq, "nn", seq, MEM_WIDTH, d, BF16)
    mn = _rms_fwd("rms_mem_kv", mems, norm_mem_kv)
    mlen = mems.shape[0]
    kv = _mm("mem_kv", mn, wmkv, "nn", mlen, 2 * MEM_WIDTH, d, BF16)
    o2 = _mem_fwd("mem_attn", q2, kv)
    h2 = _mm("mem_out", o2, wmo, "nn", seq, d, MEM_WIDTH, F32, tn=1024, add=h1)

    n2 = _rms_fwd("rms_ffn", h2, norm_ffn)
    f, g_act = _ffn_in_swiglu("ffn_in_swiglu", n2, wffn_in)
    loss_part, dh3, dg_final = _matmul_final_loss("ffn_out_final_loss", g_act, wffn_out, h2, tgt,
                                                  norm_final.reshape(1, d))

    df = _d_ffn_out_swiglu("d_ffn_out_swiglu", dh3, wffn_out, f)
    dwffn_out = _mm("d_ffn_out_w", g_act, dh3, "tn", FFN_HIDDEN, d, seq, BF16, tm=1408, tn=1024)
    dh2, dg_ffn = _matmul_rms_bwd("d_ffn_in_x_rms", df, wffn_in, 2 * FFN_HIDDEN, h2, norm_ffn, dh3, tm=1024, tk=1408)
    dwffn_in = _mm("d_ffn_in_w", n2, df, "tn", d, 2 * FFN_HIDDEN, seq, BF16, tn=1408)

    do2 = _mm("d_mem_out_x", dh2, wmo, "nt", seq, MEM_WIDTH, d, F32)
    dwmo = _restack_cols("restack_d_w_mem_o", _mm("d_mem_out_w", o2, dh2, "tn", MEM_WIDTH, d, seq, BF16, tn=1024))
    dq2, dkv = _mem_bwd("d_mem_attn", q2, kv, do2)
    dwmq = _mm("d_mem_q_w", n1, dq2, "tn", d, MEM_WIDTH, seq, BF16)
    dwmkv = _mm("d_mem_kv_w", mn, dkv, "tn", d, 2 * MEM_WIDTH, mlen, BF16, tn=1024)
    dmn = _mm("d_mem_kv_x", dkv, wmkv, "nt", mlen, d, 2 * MEM_WIDTH, F32)
    dg_memkv = _rms_gain_grad("d_rms_mem_kv", dmn, mems)

    early = [dwmq.reshape(N_DEV, d // N_DEV, MEM_WIDTH), dwmkv.reshape(N_DEV, d // N_DEV, 2 * MEM_WIDTH), dwmo,
             _disassemble_dwffn("split_d_w_ffn_in", dwffn_in), dwffn_out.reshape(N_DEV, FFN_HIDDEN // N_DEV, d)]
    ssend, srecv, early_thru, early_lands, stoken = _send_start(
        "scatter_early_start", early, _place_own("place_early_grads", early, stacked_src=True), scatter=True)
    dh1, dg_memq = _matmul_rms_bwd("d_mem_q_x_rms", dq2, wmq, MEM_WIDTH, h1, norm_mem_q, dh2, tm=1024, after=stoken)

    dwmix = _mm("d_mix_out_w", mixin, dh1, "tn", d, d, seq, BF16, tn=1024)
    dglu, dgates, dout_b = _mix_bwd("d_mix_out_x_mix", dh1, wmix, glu, gates, out_b)
    datt = _mm("d_fox_out_x", dout_b, wfoxo, "nt", seq, FOX_WIDTH, d, F32)
    dwfoxo = _restack_cols("restack_d_w_fox_o", _mm("d_fox_out_w", att, dout_b, "tn", FOX_WIDTH, d, seq, BF16, tn=1024))
    dact = _mm("d_glu_x", dglu, wglu, "nt", seq, SSM_WIDTH, 2 * d, F32, tk=2 * d)
    dwglu = _restack_cols("restack_d_w_glu", _mm("d_glu_w", act, dglu, "tn", SSM_WIDTH, 2 * d, seq, BF16, tn=2 * d))

    mid = [dwglu, dwfoxo, dwmix.reshape(N_DEV, d // N_DEV, d)]
    msend, mrecv, mid_thru, mid_lands, mtoken = _send_start(
        "scatter_mid_start", mid, _place_own("place_mid_grads", mid, stacked_src=True), scatter=True)

    dz8, dg_dskip = _ssm_post_bwd("d_ssm_act", dact.reshape(nc, SSM_CHUNK * SSM_WIDTH), y8, u8, d8, after=mtoken)
    ds4, dcm = _ssm_ds("d_ssm_y_state", dz8, sp4, cm_b)
    g4, da8 = _ssm_scan("d_ssm_scan", ds4, a8, aseg, reverse=True, sprev4=sp4)
    dx8, dm, dbw = _ssm_dx("d_ssm_x", dz8, g4, u8, m_b, bw_b, d8)
    dussm = dx8.reshape(seq, SSM_WIDTH)
    g_ssm = mats_vjp((_bd_reduce("ssm_reduce_dm", _BD_M, dm), _bd_reduce("ssm_reduce_dbw", _BD_BW, dbw),
                      _bd_reduce("ssm_reduce_dcm", _BD_CM, dcm), da8, jnp.zeros_like(aseg)))

    dq, dk, dv, dcs = _fox_bwd("d_fox", qkv, cum_t, att, datt, lse)
    dfproj, dbf = _fox_cum_bwd("d_fox_cum", dcs, fproj, bcol)
    dg_bforget = dbf[0:FOX_HEADS, 0].reshape(1, FOX_HEADS)

    dproj = _concat_cols("d_proj_concat", (dussm, dq, dk, dv, dgates, dfproj))
    dwin = _mm("d_proj_w", u, dproj, "tn", d, PROJ_WIDTH, seq, BF16, tn=1408)
    late = [_disassemble_dwin("split_d_w_in", dwin)]
    lsend, lrecv, late_thru, late_lands, ltoken = _send_start(
        "scatter_late_start", late, _place_own("place_late_grads", late, stacked_src=True), scatter=True)
    dx, dg_mix = _matmul_rms_bwd("d_proj_x_rms", dproj, win, PROJ_WIDTH, xs, norm_mix, dh1, tm=1024, tk=1408,
                                 after=ltoken)

    early_parts = _send_wait("scatter_early_wait", ssend, srecv, early_thru, early_lands, dx, scatter=True)
    mid_parts = _send_wait("scatter_mid_wait", msend, mrecv, mid_thru, mid_lands, dx, scatter=True)
    received = dict(zip(("w_glu", "w_fox_o", "w_mix_out"), mid_parts))
    received.update(zip(("w_mem_q", "w_mem_kv", "w_mem_o", "w_ffn_in", "w_ffn_out"), early_parts))

    small_grads = dict(zip(
        _SMALL, (dg_mix, dg_bforget, g_ssm[0][None], g_ssm[1][None], g_ssm[2][None], g_ssm[3][None], g_ssm[4][None],
                 g_ssm[5][None], g_ssm[6][None], dg_dskip, dg_memq, dg_memkv, dg_ffn, dg_final.reshape(d))))
    small_like = [weights[n] for n in _SMALL]
    small_all = _gather_all("gather_small_grads", [_pack([small_grads[n] for n in _SMALL])])[0]
    pk = [_pack([src[n] for n in _SMALL]) for src in (weights, mom_m, mom_v)]
    small_out = _adamw("adamw_small", small_all, pk[0], pk[1], pk[2], tr=small_all.shape[1])
    results = [dict(zip(_SMALL, _unpack(buf, small_like))) for buf in small_out]
    tiles = {"w_in": 128, "w_glu": 128, "w_fox_o": 128, "w_mix_out": 128, "w_mem_q": 128, "w_mem_kv": 128,
             "w_mem_o": 128, "w_ffn_in": 128, "w_ffn_out": 176}
    pads = {"w_in": SHARD_IN_PAD, "w_ffn_in": SHARD_FFN_PAD}
    outs = small_out
    for name in _SHARDED[1:] + _SHARDED[:1]:
        if name == "w_in":
            received[name] = _send_wait("scatter_late_wait", lsend, lrecv, late_thru, late_lands, outs[0],
                                        scatter=True)[0]
        parts = received[name]
        w2, m2, v2 = weights[name][0], mom_m[name][0], mom_v[name][0]
        cols = w2.shape[1]
        if name in pads:
            w2, m2, v2 = (padcols(t, pads[name]) for t in (w2, m2, v2))
        outs = _adamw("adamw_" + name, parts, w2, m2, v2, tr=tiles[name])
        for res, o in zip(results, outs):
            res[name] = o[:, :cols][None]

    loss = lax.psum(loss_part[0, 0], ("x", "y", "c"))
    out = [loss, dx[None]]
    for res in results:
        out.extend(res[n] for n in _WEIGHTS)
    return tuple(out)
```

```python
import math

import jax
import jax.numpy as jnp
import numpy as np
from jax import lax
from jax.experimental import pallas as pl
from jax.experimental.pallas import tpu as pltpu

F32 = jnp.float32
BF16 = jnp.bfloat16

N_DEV = 8
LANE = 128
VMEM_LIMIT = 56 * 1024 * 1024

D_MODEL = 1024
SSM_GROUP = 16
SSM_GROUPS = 32
SSM_WIDTH = 512
SSM_STATE = 64
SSM_CHUNK = 8
FOX_HEADS = 8
FOX_HEAD_DIM = 64
FOX_WIDTH = 512
MEM_HEADS = 4
MEM_HEAD_DIM = 128
MEM_WIDTH = 512
FFN_HIDDEN = 2816
RMS_EPS = 1e-6
IN_WIDTH = 4104
SHARD_IN = IN_WIDTH // N_DEV
SHARD_IN_PAD = 640
SHARD_FFN = 2 * FFN_HIDDEN // N_DEV
SHARD_FFN_PAD = 768
PROJ_GATE0 = 2048
PROJ_F0 = 4096
PROJ_WIDTH = 4224

ADAM_LR = 0.001
ADAM_B1 = 0.9
ADAM_B2 = 0.999
ADAM_EPS = 1e-08
ADAM_WD = 0.01
ADAM_STEP = 10


def _cparams(sem=None):
    return pltpu.CompilerParams(dimension_semantics=sem, vmem_limit_bytes=VMEM_LIMIT)


def _sigmoid(x):
    return 1.0 / (1.0 + jnp.exp(-x))


def _bdot(a, b, dims):
    return lax.dot_general(a.astype(BF16), b.astype(BF16), ((dims[0], dims[1]), ((), ())),
                           preferred_element_type=F32)


_DIMS = {"nn": ((1,), (0,)), "nt": ((1,), (1,)), "tn": ((0,), (0,))}


def _matmul(name, a, b, mode, m, n, k, *, out_dtype, tm, tn, tk, a_off=(0, 0), b_off=(0, 0), add=None):
    tm, tn, tk = min(tm, m), min(tn, n), min(tk, k)
    assert m % tm == 0 and n % tn == 0 and k % tk == 0, (name, m, n, k, tm, tn, tk)
    nk = k // tk
    grid = (m // tm, n // tn, nk)

    def blk(off, t):
        assert off % t == 0, (name, off, t)
        return off // t

    if mode in ("nn", "nt"):
        ar, ac = blk(a_off[0], tm), blk(a_off[1], tk)
        a_spec = pl.BlockSpec((tm, tk), lambda i, j, kk: (i + ar, kk + ac))
    else:
        ar, ac = blk(a_off[0], tk), blk(a_off[1], tm)
        a_spec = pl.BlockSpec((tk, tm), lambda i, j, kk: (kk + ar, i + ac))

    if mode in ("nn", "tn"):
        br, bc = blk(b_off[0], tk), blk(b_off[1], tn)
        b_spec = pl.BlockSpec((tk, tn), lambda i, j, kk: (kk + br, j + bc))
    else:
        br, bc = blk(b_off[0], tn), blk(b_off[1], tk)
        b_spec = pl.BlockSpec((tn, tk), lambda i, j, kk: (j + br, kk + bc))
    o_spec = pl.BlockSpec((tm, tn), lambda i, j, kk: (i, j))
    out_shape = jax.ShapeDtypeStruct((m, n), out_dtype)

    in_specs = [a_spec, b_spec]
    operands = [a, b]
    if add is not None:
        in_specs.append(pl.BlockSpec((tm, tn), lambda i, j, kk: (i, j)))
        operands.append(add)
    dims = _DIMS[mode]
    has_add = add is not None

    def body(*refs):
        a_ref, b_ref = refs[0], refs[1]
        add_ref = refs[2] if has_add else None
        o_ref = refs[3] if has_add else refs[2]
        acc_ref = refs[-1] if nk > 1 else None
        prod = _bdot(a_ref[...], b_ref[...], dims)

        def finish(total):
            if has_add:
                total = total + add_ref[...].astype(F32)
            o_ref[...] = total.astype(o_ref.dtype)

        if nk == 1:
            finish(prod)
        else:
            kk = pl.program_id(2)

            @pl.when(kk == 0)
            def _():
                acc_ref[...] = prod

            @pl.when(jnp.logical_and(kk > 0, kk < nk - 1))
            def _():
                acc_ref[...] += prod

            @pl.when(kk == nk - 1)
            def _():
                finish(acc_ref[...] + prod)

    scratch = [pltpu.VMEM((tm, tn), F32)] if nk > 1 else []
    return pl.pallas_call(
        body, name=name, grid=grid, in_specs=in_specs, out_specs=o_spec, out_shape=out_shape,
        scratch_shapes=scratch,
        compiler_params=_cparams(("parallel", "parallel", "arbitrary")),
    )(*operands)


def _rms_fwd(name, x, gain, *, tr=512, after=None):
    r, d = x.shape
    tr = min(tr, r)

    def body(x_ref, g_ref, *rest):
        o_ref = rest[-1]
        xv = x_ref[...]
        rstd = lax.rsqrt(jnp.mean(xv * xv, axis=-1, keepdims=True) + RMS_EPS)
        o_ref[...] = (xv * rstd * g_ref[...]).astype(o_ref.dtype)

    in_specs = [pl.BlockSpec((tr, d), lambda i: (i, 0)), pl.BlockSpec((1, d), lambda i: (0, 0))]
    ops = [x, gain]
    if after is not None:
        in_specs.append(pl.BlockSpec(after.shape, lambda i: (0, 0)))
        ops.append(after)
    return pl.pallas_call(
        body, name=name, grid=(r // tr,), in_specs=in_specs,
        out_specs=pl.BlockSpec((tr, d), lambda i: (i, 0)),
        out_shape=jax.ShapeDtypeStruct((r, d), BF16),
        compiler_params=_cparams(("parallel",)),
    )(*ops)


def _rms_gain_grad(name, dy, x, *, tr=512):
    r, d = x.shape
    tr = min(tr, r)
    n = r // tr

    def body(dy_ref, x_ref, dg_ref, acc_ref):
        i = pl.program_id(0)
        xv = x_ref[...]
        xh = xv * lax.rsqrt(jnp.mean(xv * xv, axis=-1, keepdims=True) + RMS_EPS)
        part = (dy_ref[...].astype(F32) * xh).reshape(tr // 8, 8, d).sum(axis=0)

        @pl.when(i == 0)
        def _():
            acc_ref[...] = part

        @pl.when(i > 0)
        def _():
            acc_ref[...] += part

        @pl.when(i == n - 1)
        def _():
            dg_ref[...] = jnp.sum(acc_ref[...], axis=0, keepdims=True)

    row = pl.BlockSpec((tr, d), lambda i: (i, 0))
    return pl.pallas_call(
        body, name=name, grid=(n,), in_specs=[row, row],
        out_specs=pl.BlockSpec((1, d), lambda i: (0, 0)),
        out_shape=jax.ShapeDtypeStruct((1, d), F32),
        scratch_shapes=[pltpu.VMEM((8, d), F32)],
        compiler_params=_cparams(("arbitrary",)),
    )(dy, x)


def _matmul_rms_bwd(name, a, b, k, x, gain, res, *, tm=512, tk=1024, after=None):
    m, d = x.shape
    tm, tk = min(tm, m), min(tk, k)
    assert m % tm == 0 and k % tk == 0, (name, m, k, tm, tk)
    ni, nk = m // tm, k // tk

    def body(a_ref, b_ref, x_ref, g_ref, res_ref, *rest):
        dx_ref, dg_ref, acc_ref, accg_ref = rest[-4:]
        i, kk = pl.program_id(0), pl.program_id(1)
        prod = _bdot(a_ref[...], b_ref[...], _DIMS["nt"])

        @pl.when(kk == 0)
        def _():
            acc_ref[...] = prod

        @pl.when(kk > 0)
        def _():
            acc_ref[...] += prod

        @pl.when(kk == nk - 1)
        def _():
            dyv = acc_ref[...]
            xv = x_ref[...]
            rstd = lax.rsqrt(jnp.mean(xv * xv, axis=-1, keepdims=True) + RMS_EPS)
            xh = xv * rstd
            dxh = dyv * g_ref[...]
            dx_ref[...] = rstd * (dxh - xh * jnp.mean(dxh * xh, axis=-1, keepdims=True)) + res_ref[...]
            part = (dyv * xh).reshape(tm // 8, 8, d).sum(axis=0)

            @pl.when(i == 0)
            def _():
                accg_ref[...] = part

            @pl.when(i > 0)
            def _():
                accg_ref[...] += part

            @pl.when(i == ni - 1)
            def _():
                dg_ref[...] = jnp.sum(accg_ref[...], axis=0, keepdims=True)

    row = pl.BlockSpec((tm, d), lambda i, kk: (i, 0))
    one = pl.BlockSpec((1, d), lambda i, kk: (0, 0))
    in_specs = [pl.BlockSpec((tm, tk), lambda i, kk: (i, kk)), pl.BlockSpec((d, tk), lambda i, kk: (0, kk)), row, one, row]
    ops = [a, b, x, gain, res]
    if after is not None:
        in_specs.append(pl.BlockSpec(after.shape, lambda i, kk: (0, 0)))
        ops.append(after)
    return pl.pallas_call(
        body, name=name, grid=(ni, nk), in_specs=in_specs, out_specs=(row, one),
        out_shape=(jax.ShapeDtypeStruct((m, d), F32), jax.ShapeDtypeStruct((1, d), F32)),
        scratch_shapes=[pltpu.VMEM((tm, d), F32), pltpu.VMEM((8, d), F32)],
        compiler_params=_cparams(("arbitrary", "arbitrary")),
    )(*ops)


def _matmul_final_loss(name, a, b, res, target, gain, *, tr=512):
    r, d = res.shape
    k = a.shape[1]
    tr = min(tr, r)
    n = r // tr

    def body(a_ref, b_ref, res_ref, t_ref, g_ref, loss_ref, dh_ref, dg_ref, accl_ref, accg_ref):
        i = pl.program_id(0)
        xv = _bdot(a_ref[...], b_ref[...], _DIMS["nn"]) + res_ref[...]
        rstd = lax.rsqrt(jnp.mean(xv * xv, axis=-1, keepdims=True) + RMS_EPS)
        xh = xv * rstd
        e = xh * g_ref[...] - t_ref[...]
        dyv = e * (1.0 / d)
        dxh = dyv * g_ref[...]
        dh_ref[...] = rstd * (dxh - xh * jnp.mean(dxh * xh, axis=-1, keepdims=True))
        lpart = (e * e).reshape(tr // 8, 8, d).sum(axis=0)
        gpart = (dyv * xh).reshape(tr // 8, 8, d).sum(axis=0)

        @pl.when(i == 0)
        def _():
            accl_ref[...] = lpart
            accg_ref[...] = gpart

        @pl.when(i > 0)
        def _():
            accl_ref[...] += lpart
            accg_ref[...] += gpart

        @pl.when(i == n - 1)
        def _():
            tot = jnp.sum(jnp.sum(accl_ref[...], axis=0, keepdims=True), axis=1, keepdims=True)
            loss_ref[...] = jnp.broadcast_to(tot * (0.5 / d), (1, LANE))
            dg_ref[...] = jnp.sum(accg_ref[...], axis=0, keepdims=True)

    row = pl.BlockSpec((tr, d), lambda i: (i, 0))
    one = pl.BlockSpec((1, d), lambda i: (0, 0))
    return pl.pallas_call(
        body, name=name, grid=(n,),
        in_specs=[pl.BlockSpec((tr, k), lambda i: (i, 0)), pl.BlockSpec((k, d), lambda i: (0, 0)), row, row, one],
        out_specs=(pl.BlockSpec((1, LANE), lambda i: (0, 0)), row, one),
        out_shape=(jax.ShapeDtypeStruct((1, LANE), F32), jax.ShapeDtypeStruct((r, d), F32),
                   jax.ShapeDtypeStruct((1, d), F32)),
        scratch_shapes=[pltpu.VMEM((8, d), F32), pltpu.VMEM((8, d), F32)],
        compiler_params=_cparams(("arbitrary",)),
    )(a, b, res, target, gain)


_GELU_C = math.sqrt(2.0 / math.pi)


def _gelu_parts(z):
    inner = _GELU_C * (z + 0.044715 * z * z * z)
    t = jnp.tanh(inner)
    val = 0.5 * z * (1.0 + t)
    dinner = _GELU_C * (1.0 + 3.0 * 0.044715 * z * z)
    grad = 0.5 * (1.0 + t) + 0.5 * z * (1.0 - t * t) * dinner
    return val, grad


def _ssm_post_fwd(name, y8, u8, d8, *, tr=256):
    r, c = y8.shape
    tr = min(tr, r)

    def body(y_ref, u_ref, d_ref, o_ref):
        z = y_ref[...] + d_ref[...] * u_ref[...]
        o_ref[...] = _gelu_parts(z)[0].astype(o_ref.dtype)

    row = pl.BlockSpec((tr, c), lambda i: (i, 0))
    return pl.pallas_call(
        body, name=name, grid=(r // tr,), in_specs=[row, row, pl.BlockSpec((1, c), lambda i: (0, 0))],
        out_specs=row, out_shape=jax.ShapeDtypeStruct((r, c), BF16),
        compiler_params=_cparams(("parallel",)),
    )(y8, u8, d8)


def _ssm_post_bwd(name, dact8, y8, u8, d8, *, tr=256, after=None):
    r, c = y8.shape
    tr = min(tr, r)
    n = r // tr

    def body(*refs):
        da_ref, y_ref, u_ref, d_ref = refs[:4]
        dz_ref, dd_ref, acc_ref = refs[-3:]
        i = pl.program_id(0)
        uv = u_ref[...]
        z = y_ref[...] + d_ref[...] * uv
        dz = da_ref[...].astype(F32) * _gelu_parts(z)[1]
        dz_ref[...] = dz
        part = (dz * uv).reshape(tr // 8, 8, c).sum(axis=0)

        @pl.when(i == 0)
        def _():
            acc_ref[...] = part

        @pl.when(i > 0)
        def _():
            acc_ref[...] += part

        @pl.when(i == n - 1)
        def _():
            tot = jnp.sum(acc_ref[...], axis=0, keepdims=True)
            out = tot[:, 0:SSM_WIDTH]
            for j in range(1, c // SSM_WIDTH):
                out = out + tot[:, j * SSM_WIDTH:(j + 1) * SSM_WIDTH]
            dd_ref[...] = out

    row = pl.BlockSpec((tr, c), lambda i: (i, 0))
    in_specs = [row, row, row, pl.BlockSpec((1, c), lambda i: (0, 0))]
    ops = [dact8, y8, u8, d8]
    if after is not None:
        in_specs.append(pl.BlockSpec(memory_space=pl.ANY))
        ops.append(after)
    return pl.pallas_call(
        body, name=name, grid=(n,), in_specs=in_specs,
        out_specs=(row, pl.BlockSpec((1, SSM_WIDTH), lambda i: (0, 0))),
        out_shape=(jax.ShapeDtypeStruct((r, c), F32), jax.ShapeDtypeStruct((1, SSM_WIDTH), F32)),
        scratch_shapes=[pltpu.VMEM((8, c), F32)],
        compiler_params=_cparams(("arbitrary",)),
    )(*ops)


def _mix_fwd(name, glu, gates, out_b, w_mix, res, *, tr=512):
    r = glu.shape[0]
    d = D_MODEL
    tr = min(tr, r)

    def body(glu_ref, gate_ref, ob_ref, w_ref, res_ref, o_ref, h_ref):
        out_a = glu_ref[:, 0:d].astype(F32) * _sigmoid(glu_ref[:, d:2 * d].astype(F32))
        mix = (_sigmoid(gate_ref[:, 0:d].astype(F32)) * out_a
               + _sigmoid(gate_ref[:, d:2 * d].astype(F32)) * ob_ref[...].astype(F32))
        o_ref[...] = mix.astype(o_ref.dtype)
        h_ref[...] = _bdot(o_ref[...], w_ref[...], _DIMS["nn"]) + res_ref[...]

    wide = pl.BlockSpec((tr, 2 * d), lambda i: (i, 0))
    row = pl.BlockSpec((tr, d), lambda i: (i, 0))
    return pl.pallas_call(
        body, name=name, grid=(r // tr,),
        in_specs=[wide, wide, row, pl.BlockSpec((d, d), lambda i: (0, 0)), row], out_specs=(row, row),
        out_shape=(jax.ShapeDtypeStruct((r, d), BF16), jax.ShapeDtypeStruct((r, d), F32)),
        compiler_params=_cparams(("parallel",)),
    )(glu, gates, out_b, w_mix, res)


def _mix_bwd(name, dh, w_mix, glu, gates, out_b, *, tr=512):
    r = glu.shape[0]
    d = D_MODEL
    tr = min(tr, r)

    def body(dh_ref, w_ref, glu_ref, gate_ref, ob_ref, dglu_ref, dgate_ref, dob_ref):
        dm = _bdot(dh_ref[...], w_ref[...], _DIMS["nt"])
        glu_a = glu_ref[:, 0:d].astype(F32)
        sb = _sigmoid(glu_ref[:, d:2 * d].astype(F32))
        ga = _sigmoid(gate_ref[:, 0:d].astype(F32))
        gb = _sigmoid(gate_ref[:, d:2 * d].astype(F32))
        out_a = glu_a * sb
        dout_a = dm * ga
        dglu_ref[:, 0:d] = (dout_a * sb).astype(dglu_ref.dtype)
        dglu_ref[:, d:2 * d] = (dout_a * glu_a * sb * (1.0 - sb)).astype(dglu_ref.dtype)
        dgate_ref[:, 0:d] = (dm * out_a * ga * (1.0 - ga)).astype(dgate_ref.dtype)
        dgate_ref[:, d:2 * d] = (dm * ob_ref[...].astype(F32) * gb * (1.0 - gb)).astype(dgate_ref.dtype)
        dob_ref[...] = (dm * gb).astype(dob_ref.dtype)

    wide = pl.BlockSpec((tr, 2 * d), lambda i: (i, 0))
    row = pl.BlockSpec((tr, d), lambda i: (i, 0))
    return pl.pallas_call(
        body, name=name, grid=(r // tr,),
        in_specs=[row, pl.BlockSpec((d, d), lambda i: (0, 0)), wide, wide, row], out_specs=(wide, wide, row),
        out_shape=(jax.ShapeDtypeStruct((r, 2 * d), BF16), jax.ShapeDtypeStruct((r, 2 * d), BF16),
                   jax.ShapeDtypeStruct((r, d), BF16)),
        compiler_params=_cparams(("parallel",)),
    )(dh, w_mix, glu, gates, out_b)


def _ssm_mats(lam_re, lam_im, log_dt, b_re, b_im, c_re, c_im, nc):
    hp = lax.Precision.HIGHEST
    t = SSM_CHUNK
    nq = SSM_GROUPS // 8
    lam = lax.complex(lam_re, lam_im)
    z = lam * jnp.exp(log_dt)[:, None]
    ks = jnp.arange(t + 1, dtype=F32)
    apow = jnp.exp(ks[:, None, None] * z[None])
    bbar = ((apow[1] - 1.0) / lam)[..., None] * lax.complex(b_re, b_im)
    c = lax.complex(c_re, c_im)

    ca = c[None] * apow[:, :, None, :]
    kmat = jnp.einsum("kgnp,gpm->kgnm", ca, bbar, precision=hp).real
    ii = np.arange(t)
    lag = ii[None, :] - ii[:, None]
    kt = kmat[np.clip(lag, 0, t)] * jnp.asarray(lag >= 0, F32)[:, :, None, None, None]
    kt = kt.reshape(t, t, nq, 8, SSM_GROUP, SSM_GROUP)
    m_c = kt.transpose(2, 0, 3, 5, 1, 4).reshape(nq, 1024, LANE)

    arev = jnp.exp((float(t - 1) - ks[:t])[:, None, None] * z[None])
    w = arev[:, :, :, None] * bbar[None]
    wr = jnp.stack([w.real, w.imag]).reshape(2, t, nq, 8, SSM_STATE, SSM_GROUP)
    bw_c = wr.transpose(2, 1, 3, 5, 0, 4).reshape(nq, 1024, LANE)

    ca1 = ca[1:]
    cr = jnp.stack([ca1.real, -ca1.imag]).reshape(2, t, nq, 8, SSM_GROUP, SSM_STATE)
    cm_c = cr.transpose(2, 0, 3, 5, 1, 4).reshape(nq, 1024, LANE)

    def tiles(v):
        vq = jnp.concatenate([v.real.reshape(nq, 512), v.imag.reshape(nq, 512)], axis=1)
        return jnp.broadcast_to(vq.reshape(nq, 8, 1, LANE), (nq, 8, 8, LANE))

    return m_c, bw_c, cm_c, tiles(apow[t]), tiles(jnp.exp(float(nc) * z))


_BD_M = (LANE, SSM_GROUP)
_BD_BW = (LANE, SSM_STATE)
_BD_CM = (512, SSM_GROUP)


def _bd_perm(cn):
    rr = lax.broadcasted_iota(jnp.int32, (1024, 1024), 0)
    cc = lax.broadcasted_iota(jnp.int32, (1024, 1024), 1)
    sh = cn.bit_length() - 1
    src = ((rr >> 7) << sh) + (((rr & (LANE - 1)) >> sh) << (3 + sh)) + (rr & (cn - 1))
    return jnp.where(src == cc, 1.0, 0.0).astype(BF16)


def _bd_rowgroup(span):
    r = lax.broadcasted_iota(jnp.int32, (1024, LANE), 0)
    return (r & (span - 1)) >> ((span // 8).bit_length() - 1)


def _bd_expand(name, kind, compact):
    span, cn = kind
    nq = compact.shape[0]

    def body(c_ref, o_ref, perm_scr):
        @pl.when(pl.program_id(0) == 0)
        def _():
            perm_scr[...] = _bd_perm(cn)

        x = c_ref[...]
        grp = _bd_rowgroup(span)
        xcat = jnp.concatenate([jnp.where(grp == h, x, 0.0) for h in range(8)], axis=1)
        o_ref[...] = _bdot(xcat, perm_scr[...], _DIMS["nn"]).astype(o_ref.dtype)

    return pl.pallas_call(
        body, name=name, grid=(nq,), in_specs=[pl.BlockSpec((None, 1024, LANE), lambda q: (q, 0, 0))],
        out_specs=pl.BlockSpec((None, 1024, 1024), lambda q: (q, 0, 0)),
        out_shape=jax.ShapeDtypeStruct((nq, 1024, 1024), BF16),
        scratch_shapes=[pltpu.VMEM((1024, 1024), BF16)],
        compiler_params=_cparams(("arbitrary",)),
    )(compact)


def _bd_reduce(name, kind, dbig):
    span, cn = kind
    nq = dbig.shape[0]

    def body(g_ref, o_ref, perm_scr):
        @pl.when(pl.program_id(0) == 0)
        def _():
            perm_scr[...] = _bd_perm(cn)

        back = _bdot(g_ref[...], perm_scr[...], _DIMS["nt"])
        grp = _bd_rowgroup(span)
        out = jnp.zeros((1024, LANE), F32)
        for h in range(8):
            out = jnp.where(grp == h, back[:, h * LANE:(h + 1) * LANE], out)
        o_ref[...] = out

    return pl.pallas_call(
        body, name=name, grid=(nq,), in_specs=[pl.BlockSpec((None, 1024, 1024), lambda q: (q, 0, 0))],
        out_specs=pl.BlockSpec((None, 1024, LANE), lambda q: (q, 0, 0)),
        out_shape=jax.ShapeDtypeStruct((nq, 1024, LANE), F32),
        scratch_shapes=[pltpu.VMEM((1024, 1024), BF16)],
        compiler_params=_cparams(("arbitrary",)),
    )(dbig)


def _x_tile_specs(nc, nq):
    return [pl.BlockSpec((nc, LANE), lambda q, t, i=i: (0, i * nq + q)) for i in range(SSM_CHUNK)]


def _cat_tiles(refs):
    return jnp.concatenate([r[...] for r in refs], axis=1)


def _ssm_w(name, x8, bw):
    nc = x8.shape[0]
    nq = bw.shape[0]

    def body(*refs):
        xq = _cat_tiles(refs[:8])
        refs[9][...] = _bdot(xq, refs[8][...], _DIMS["nn"])

    return pl.pallas_call(
        body, name=name, grid=(nq, 8),
        in_specs=_x_tile_specs(nc, nq) + [pl.BlockSpec((None, 1024, LANE), lambda q, t: (q, 0, t))],
        out_specs=pl.BlockSpec((None, None, nc, LANE), lambda q, t: (q, t, 0, 0)),
        out_shape=jax.ShapeDtypeStruct((nq, 8, nc, LANE), F32),
        compiler_params=_cparams(("parallel", "arbitrary")),
    )(*([x8] * 8), bw)


def _ssm_scan(name, w4, a_t, aseg_t, *, reverse, sprev4=None):
    nq, _, nc, _ = w4.shape
    ns = nc // 8
    with_da = sprev4 is not None

    def body(*refs):
        w_ref, a_ref, aseg_ref = refs[:3]
        s_ref = refs[3] if with_da else None
        o_ref = refs[4] if with_da else refs[3]
        da_ref = refs[5] if with_da else None
        sgn = -1.0 if reverse else 1.0
        ar = [a_ref[j] for j in range(4)]
        ai = [sgn * a_ref[j + 4] for j in range(4)]
        gr = [aseg_ref[j] for j in range(4)]
        gi = [sgn * aseg_ref[j + 4] for j in range(4)]
        zero = tuple(jnp.zeros((8, LANE), F32) for _ in range(8))

        def rows(tt):
            return pl.ds((ns - 1 - tt) if reverse else tt, 8, stride=ns)

        def step(carry, w):
            new_r = [ar[j] * carry[j] - ai[j] * carry[j + 4] + w[j] for j in range(4)]
            new_i = [ar[j] * carry[j + 4] + ai[j] * carry[j] + w[j + 4] for j in range(4)]
            return tuple(new_r + new_i)

        def pass1(tt, carry):
            return step(carry, [w_ref[j, rows(tt), :] for j in range(8)])

        ends = lax.fori_loop(0, ns, pass1, zero)
        sub = lax.broadcasted_iota(jnp.int32, (8, LANE), 0)
        init = list(zero)
        order = range(7, 0, -1) if reverse else range(0, 7)
        for s in order:
            nxt = s - 1 if reverse else s + 1
            cand_r = [gr[j] * init[j] - gi[j] * init[j + 4] + ends[j] for j in range(4)]
            cand_i = [gr[j] * init[j + 4] + gi[j] * init[j] + ends[j + 4] for j in range(4)]
            cand = cand_r + cand_i
            shift = 7 if reverse else 1
            init = [jnp.where(sub == nxt, pltpu.roll(cand[j], shift, axis=0), init[j]) for j in range(8)]

        def pass2(tt, state):
            carry, acc = state
            r = rows(tt)
            for j in range(8):
                o_ref[j, r, :] = carry[j]
            if with_da:
                sp = [s_ref[j, r, :] for j in range(8)]
                acc_r = [acc[j] + carry[j] * sp[j] + carry[j + 4] * sp[j + 4] for j in range(4)]
                acc_i = [acc[j + 4] + carry[j + 4] * sp[j] - carry[j] * sp[j + 4] for j in range(4)]
                acc = tuple(acc_r + acc_i)
            return step(carry, [w_ref[j, r, :] for j in range(8)]), acc

        _, acc = lax.fori_loop(0, ns, pass2, (tuple(init), zero))
        if with_da:
            for j in range(8):
                da_ref[j] = acc[j]

    big = pl.BlockSpec((None, 8, nc, LANE), lambda q: (q, 0, 0, 0))
    small = pl.BlockSpec((None, 8, 8, LANE), lambda q: (q, 0, 0, 0))
    in_specs = [big, small, small] + ([big] if with_da else [])
    ops = [w4, a_t, aseg_t] + ([sprev4] if with_da else [])
    out_specs = (big, small) if with_da else big
    big_s = jax.ShapeDtypeStruct((nq, 8, nc, LANE), F32)
    out_shape = (big_s, jax.ShapeDtypeStruct((nq, 8, 8, LANE), F32)) if with_da else big_s
    return pl.pallas_call(
        body, name=name, grid=(nq,), in_specs=in_specs, out_specs=out_specs, out_shape=out_shape,
        compiler_params=_cparams(("parallel",)),
    )(*ops)


def _ssm_y(name, x8, sprev4, m_mat, cm_mat):
    nc = x8.shape[0]
    nq = m_mat.shape[0]

    def body(*refs):
        xq = _cat_tiles(refs[:8])
        s_ref, m_ref, cm_ref, o_ref = refs[8:12]
        sq = jnp.concatenate([s_ref[t] for t in range(8)], axis=1)
        o_ref[...] = _bdot(xq, m_ref[...], _DIMS["nn"]) + _bdot(sq, cm_ref[...], _DIMS["nn"])

    col = pl.BlockSpec((None, 1024, LANE), lambda q, j: (q, 0, j))
    return pl.pallas_call(
        body, name=name, grid=(nq, 8),
        in_specs=_x_tile_specs(nc, nq) + [pl.BlockSpec((None, 8, nc, LANE), lambda q, j: (q, 0, 0, 0)), col, col],
        out_specs=pl.BlockSpec((nc, LANE), lambda q, j: (0, j * nq + q)),
        out_shape=jax.ShapeDtypeStruct((nc, 8 * SSM_WIDTH), F32),
        compiler_params=_cparams(("parallel", "arbitrary")),
    )(*([x8] * 8), sprev4, m_mat, cm_mat)


def _ssm_ds(name, dz8, sprev4, cm_mat):
    nc = dz8.shape[0]
    nq = cm_mat.shape[0]

    def body(*refs):
        dyq = _cat_tiles(refs[:8]).astype(BF16)
        s_ref, cm_ref, ds_ref, dcm_ref = refs[8:12]
        ds_ref[...] = _bdot(dyq, cm_ref[...], _DIMS["nt"])
        dcm_ref[...] = _bdot(s_ref[...], dyq, _DIMS["tn"])

    tile = pl.BlockSpec((None, None, nc, LANE), lambda q, t: (q, t, 0, 0))
    rowblk = pl.BlockSpec((None, LANE, 1024), lambda q, t: (q, t, 0))
    return pl.pallas_call(
        body, name=name, grid=(nq, 8),
        in_specs=_x_tile_specs(nc, nq) + [tile, rowblk],
        out_specs=(tile, rowblk),
        out_shape=(jax.ShapeDtypeStruct((nq, 8, nc, LANE), F32), jax.ShapeDtypeStruct((nq, 1024, 1024), F32)),
        compiler_params=_cparams(("parallel", "arbitrary")),
    )(*([dz8] * 8), sprev4, cm_mat)


def _ssm_dx(name, dz8, g4, x8, m_mat, bw_mat, d8):
    nc = dz8.shape[0]
    nq = m_mat.shape[0]

    def body(*refs):
        dyq = _cat_tiles(refs[:8]).astype(BF16)
        g_ref, x_ref, m_ref, bw_ref, d_ref, dzi_ref, dx_ref, dm_ref, dbw_ref = refs[8:17]
        gq = jnp.concatenate([g_ref[t] for t in range(8)], axis=1).astype(BF16)
        dx = _bdot(dyq, m_ref[...], _DIMS["nt"]) + _bdot(gq, bw_ref[...], _DIMS["nt"])
        dx_ref[...] = (dx + d_ref[...] * dzi_ref[...]).astype(dx_ref.dtype)
        xi = x_ref[...]
        dm_ref[...] = _bdot(xi, dyq, _DIMS["tn"])
        dbw_ref[...] = _bdot(xi, gq, _DIMS["tn"])

    xtile = pl.BlockSpec((nc, LANE), lambda q, i: (0, i * nq + q))
    rowblk = pl.BlockSpec((None, LANE, 1024), lambda q, i: (q, i, 0))
    return pl.pallas_call(
        body, name=name, grid=(nq, 8),
        in_specs=_x_tile_specs(nc, nq) + [pl.BlockSpec((None, 8, nc, LANE), lambda q, i: (q, 0, 0, 0)), xtile, rowblk, rowblk,
                                          pl.BlockSpec((1, LANE), lambda q, i: (0, q)), xtile],
        out_specs=(xtile, rowblk, rowblk),
        out_shape=(jax.ShapeDtypeStruct((nc, 8 * SSM_WIDTH), BF16), jax.ShapeDtypeStruct((nq, 1024, 1024), F32),
                   jax.ShapeDtypeStruct((nq, 1024, 1024), F32)),
        compiler_params=_cparams(("parallel", "arbitrary")),
    )(*([dz8] * 8), g4, x8, m_mat, bw_mat, d8, dz8)


CUM_BLK = 256


def _split3(x):
    hi = x.astype(BF16)
    r1 = x - hi.astype(F32)
    mid = r1.astype(BF16)
    lo = (r1 - mid.astype(F32)).astype(BF16)
    return hi, mid, lo


def _tri_dot(x, tri):
    hi, mid, lo = _split3(x)
    d = _DIMS["nn"]
    return _bdot(hi, tri, d) + _bdot(mid, tri, d) + _bdot(lo, tri, d)


def _tri(n, lower):
    r = lax.broadcasted_iota(jnp.int32, (n, n), 0)
    c = lax.broadcasted_iota(jnp.int32, (n, n), 1)
    return jnp.where((r >= c) if lower else (r <= c), 1.0, 0.0).astype(BF16)


def _fox_cum(name, fproj, bcol):
    seq = fproj.shape[0]
    blk = min(CUM_BLK, seq)

    def body(f_ref, b_ref, o_ref, carry_ref):
        i = pl.program_id(0)

        @pl.when(i == 0)
        def _():
            carry_ref[...] = jnp.zeros_like(carry_ref)

        z = f_ref[...].T + b_ref[...]
        logf = jnp.minimum(z, 0.0) - jnp.log(1.0 + jnp.exp(-jnp.abs(z)))
        carry = carry_ref[...]
        cum = _tri_dot(logf, _tri(blk, lower=False)) + jnp.tile(carry, (1, blk // LANE))
        o_ref[...] = cum[0:8, :]
        carry_ref[...] = carry + jnp.sum(logf, axis=1, keepdims=True)

    return pl.pallas_call(
        body, name=name, grid=(seq // blk,),
        in_specs=[pl.BlockSpec((blk, LANE), lambda i: (i, 0)), pl.BlockSpec((LANE, 1), lambda i: (0, 0))],
        out_specs=pl.BlockSpec((8, blk), lambda i: (0, i)),
        out_shape=jax.ShapeDtypeStruct((8, seq), F32),
        scratch_shapes=[pltpu.VMEM((LANE, LANE), F32)],
        compiler_params=_cparams(("arbitrary",)),
    )(fproj, bcol)


def _fox_cum_bwd(name, dcs, fproj, bcol):
    seq = fproj.shape[0]
    blk = min(CUM_BLK, seq)
    n = seq // blk

    def body(dc_ref, f_ref, b_ref, df_ref, db_ref, carry_ref, acc_ref):
        i = pl.program_id(0)

        @pl.when(i == 0)
        def _():
            carry_ref[...] = jnp.zeros_like(carry_ref)
            acc_ref[...] = jnp.zeros_like(acc_ref)

        r = lax.broadcasted_iota(jnp.int32, (LANE, FOX_WIDTH), 0)
        c = lax.broadcasted_iota(jnp.int32, (LANE, FOX_WIDTH), 1)
        want = (r >> 1) * LANE + jnp.where((r & 1) == 0, FOX_HEAD_DIM, 0)
        sel = jnp.where(jnp.logical_and(r < FOX_HEADS, c == want), 1.0, 0.0).astype(BF16)
        hi, mid, lo = _split3(dc_ref[...])
        nt = _DIMS["nt"]
        dc = _bdot(sel, hi, nt) + _bdot(sel, mid, nt) + _bdot(sel, lo, nt)
        carry = carry_ref[...]
        dlogf = _tri_dot(dc, _tri(blk, lower=True)) + jnp.tile(carry, (1, blk // LANE))
        carry_ref[...] = carry + jnp.sum(dc, axis=1, keepdims=True)
        z = f_ref[...].T + b_ref[...]
        dft = dlogf / (1.0 + jnp.exp(z))
        df_ref[...] = dft.T.astype(df_ref.dtype)
        acc_ref[...] += jnp.sum(dft, axis=1, keepdims=True)

        @pl.when(i == n - 1)
        def _():
            db_ref[...] = acc_ref[...]

    return pl.pallas_call(
        body, name=name, grid=(n,),
        in_specs=[pl.BlockSpec((blk, FOX_WIDTH), lambda i: (n - 1 - i, 0)), pl.BlockSpec((blk, LANE), lambda i: (n - 1 - i, 0)),
                  pl.BlockSpec((LANE, 1), lambda i: (0, 0))],
        out_specs=(pl.BlockSpec((blk, LANE), lambda i: (n - 1 - i, 0)), pl.BlockSpec((LANE, LANE), lambda i: (0, 0))),
        out_shape=(jax.ShapeDtypeStruct((seq, LANE), BF16), jax.ShapeDtypeStruct((LANE, LANE), F32)),
        scratch_shapes=[pltpu.VMEM((LANE, LANE), F32), pltpu.VMEM((LANE, LANE), F32)],
        compiler_params=_cparams(("arbitrary",)),
    )(dcs, fproj, bcol)


FOX_BLK = 512
FOX_SCALE = FOX_HEAD_DIM ** -0.5


def _fox_head_mask(shape, hh):
    lane = lax.broadcasted_iota(jnp.int32, shape, 1)
    return (lane < FOX_HEAD_DIM) if hh == 0 else (lane >= FOX_HEAD_DIM)


def _fox_bias(cum_ref, hh, q0, k0, blk):
    c0 = jnp.max(cum_ref[hh:hh + 1, pl.ds(q0, LANE)], axis=1, keepdims=True)
    return c0 - cum_ref[hh:hh + 1, pl.ds(k0, blk)]


def _fox_fwd(name, qkv, cum_t):
    seq = qkv.shape[0]
    blk = min(FOX_BLK, seq)
    nb = seq // blk
    npair = FOX_HEADS // 2

    def body(q_ref, k_ref, v_ref, cum_ref, o_ref, lse_ref):
        iq = pl.program_id(1)
        q0 = pl.multiple_of(iq * blk, blk)
        qv = q_ref[...]
        row = lax.broadcasted_iota(jnp.int32, (blk, blk), 0)
        col = lax.broadcasted_iota(jnp.int32, (blk, blk), 1)
        qhs = [jnp.where(_fox_head_mask(qv.shape, hh), qv, jnp.zeros_like(qv)) * FOX_SCALE for hh in range(2)]

        def block(kb, states, masked):
            k0 = pl.multiple_of(kb * blk, blk)
            kv = k_ref[pl.ds(k0, blk), :]
            vv = v_ref[pl.ds(k0, blk), :]
            new = []
            for hh in range(2):
                m, acc = states[hh]
                s = _bdot(qhs[hh], kv, _DIMS["nt"]) + _fox_bias(cum_ref, hh, q0, k0, blk)
                if masked:
                    s = jnp.where(row >= col, s, -jnp.inf)
                m_new = jnp.maximum(m, jnp.max(s, axis=1, keepdims=True))
                p = jnp.exp(s - m_new)
                vh = jnp.where(_fox_head_mask(vv.shape, hh), vv, jnp.ones_like(vv))
                acc = jnp.exp(m - m_new) * acc + _bdot(p, vh, _DIMS["nn"])
                new.append((m_new, acc))
            return tuple(new)

        init = (jnp.full((blk, 1), -jnp.inf, F32), jnp.zeros((blk, LANE), F32))
        states = lax.fori_loop(0, iq, lambda kb, st: block(kb, st, False), (init, init))
        states = block(iq, states, True)
        outs = []
        for hh in range(2):
            m, acc = states[hh]
            other = pltpu.roll(acc, FOX_HEAD_DIM, axis=1)
            outs.append(acc / other)
            lse_ref[hh] = m + jnp.log(jnp.where(_fox_head_mask(acc.shape, hh), other, acc))
        o_ref[...] = jnp.where(_fox_head_mask(outs[0].shape, 0), outs[0], outs[1]).astype(o_ref.dtype)

    return pl.pallas_call(
        body, name=name, grid=(npair, nb),
        in_specs=[pl.BlockSpec((blk, LANE), lambda p, i: (i, p)),
                  pl.BlockSpec((seq, LANE), lambda p, i: (0, npair + p)),
                  pl.BlockSpec((seq, LANE), lambda p, i: (0, 2 * npair + p)),
                  pl.BlockSpec((None, 2, seq), lambda p, i: (p, 0, 0))],
        out_specs=(pl.BlockSpec((blk, LANE), lambda p, i: (i, p)),
                   pl.BlockSpec((2, blk, LANE), lambda p, i: (p, i, 0))),
        out_shape=(jax.ShapeDtypeStruct((seq, FOX_WIDTH), BF16), jax.ShapeDtypeStruct((FOX_HEADS, seq, LANE), F32)),
        compiler_params=_cparams(("parallel", "arbitrary")),
    )(qkv, qkv, qkv, cum_t)


def _fox_bwd(name, qkv, cum_t, att, datt, lse):
    seq = qkv.shape[0]
    blk = min(FOX_BLK, seq)
    nb = seq // blk
    npair = FOX_HEADS // 2

    def body(q_ref, k_ref, v_ref, cum_ref, o_ref, do_ref, lse_ref, dq_ref, dk_ref, dv_ref, dcs_ref):
        iq = pl.program_id(1)
        q0 = pl.multiple_of(iq * blk, blk)

        @pl.when(iq == 0)
        def _():
            dk_ref[...] = jnp.zeros_like(dk_ref)
            dv_ref[...] = jnp.zeros_like(dv_ref)
            dcs_ref[...] = jnp.zeros_like(dcs_ref)

        qv = q_ref[...]
        dov = do_ref[...].astype(F32)
        ov = o_ref[...].astype(F32)
        row = lax.broadcasted_iota(jnp.int32, (blk, blk), 0)
        col = lax.broadcasted_iota(jnp.int32, (blk, blk), 1)
        low = _fox_head_mask((blk, LANE), 0)
        qhs, qones, dohbs, deltas, lses = [], [], [], [], []
        for hh in range(2):
            hm = _fox_head_mask(qv.shape, hh)
            qh = jnp.where(hm, qv, jnp.zeros_like(qv)) * FOX_SCALE
            qhs.append(qh)
            qones.append(jnp.where(hm, qh, jnp.ones_like(qh)))
            doh = jnp.where(hm, dov, 0.0)
            dohbs.append(doh.astype(BF16))
            deltas.append(jnp.sum(doh * ov, axis=1, keepdims=True))
            lses.append(jnp.tile(lse_ref[hh], (1, blk // LANE)))

        def block(kb, dqs, masked):
            k0 = pl.multiple_of(kb * blk, blk)
            kv = k_ref[pl.ds(k0, blk), :]
            vv = v_ref[pl.ds(k0, blk), :]
            new, dks, dvs = [], [], []
            for hh in range(2):
                s = _bdot(qhs[hh], kv, _DIMS["nt"]) + _fox_bias(cum_ref, hh, q0, k0, blk)
                p = jnp.exp(s - lses[hh])
                if masked:
                    p = jnp.where(row >= col, p, 0.0)
                dp = _bdot(dohbs[hh], vv, _DIMS["nt"])
                dsb = (p * (dp - deltas[hh])).astype(BF16)
                dks.append(_bdot(dsb, qones[hh], _DIMS["tn"]))
                dvs.append(_bdot(p, dohbs[hh], _DIMS["tn"]))
                kones = jnp.where(_fox_head_mask(kv.shape, hh), kv, jnp.ones_like(kv))
                new.append(dqs[hh] + _bdot(dsb, kones, _DIMS["nn"]))
            dk_ref[pl.ds(k0, blk), :] += jnp.where(low, dks[0], dks[1])
            dv_ref[pl.ds(k0, blk), :] += dvs[0] + dvs[1]
            dcs_ref[pl.ds(k0, blk), :] -= jnp.where(low, dks[1], dks[0])
            return tuple(new)

        init = jnp.zeros((blk, LANE), F32)
        dqs = lax.fori_loop(0, iq, lambda kb, a: block(kb, a, False), (init, init))
        dqs = block(iq, dqs, True)
        dcs_ref[pl.ds(q0, blk), :] += jnp.where(low, dqs[1], dqs[0])
        dq_ref[...] = (jnp.where(low, dqs[0], dqs[1]) * FOX_SCALE).astype(dq_ref.dtype)

    qblk = pl.BlockSpec((blk, LANE), lambda p, i: (i, p))
    full = pl.BlockSpec((seq, LANE), lambda p, i: (0, p))
    return pl.pallas_call(
        body, name=name, grid=(npair, nb),
        in_specs=[qblk,
                  pl.BlockSpec((seq, LANE), lambda p, i: (0, npair + p)),
                  pl.BlockSpec((seq, LANE), lambda p, i: (0, 2 * npair + p)),
                  pl.BlockSpec((None, 2, seq), lambda p, i: (p, 0, 0)),
                  qblk, qblk,
                  pl.BlockSpec((2, blk, LANE), lambda p, i: (p, i, 0))],
        out_specs=(qblk, full, full, full),
        out_shape=(jax.ShapeDtypeStruct((seq, FOX_WIDTH), BF16), jax.ShapeDtypeStruct((seq, FOX_WIDTH), F32),
                   jax.ShapeDtypeStruct((seq, FOX_WIDTH), F32), jax.ShapeDtypeStruct((seq, FOX_WIDTH), F32)),
        compiler_params=_cparams(("arbitrary", "arbitrary")),
    )(qkv, qkv, qkv, cum_t, att, datt, lse)


MEM_SCALE = MEM_HEAD_DIM ** -0.5


def _mem_probs(qh, kh):
    s = _bdot(qh, kh, _DIMS["nt"]) * MEM_SCALE
    p = jnp.exp(s - jnp.max(s, axis=1, keepdims=True))
    return p / jnp.sum(p, axis=1, keepdims=True)


def _mem_fwd(name, q2, kv, *, tr=512):
    seq = q2.shape[0]
    mlen = kv.shape[0]
    tr = min(tr, seq)

    def body(q_ref, kv_ref, o_ref):
        for h in range(MEM_HEADS):
            sl = slice(h * MEM_HEAD_DIM, (h + 1) * MEM_HEAD_DIM)
            sv = slice(MEM_WIDTH + h * MEM_HEAD_DIM, MEM_WIDTH + (h + 1) * MEM_HEAD_DIM)
            p = _mem_probs(q_ref[:, sl], kv_ref[:, sl])
            o_ref[:, sl] = _bdot(p, kv_ref[:, sv], _DIMS["nn"]).astype(o_ref.dtype)

    return pl.pallas_call(
        body, name=name, grid=(seq // tr,),
        in_specs=[pl.BlockSpec((tr, MEM_WIDTH), lambda i: (i, 0)), pl.BlockSpec((mlen, 2 * MEM_WIDTH), lambda i: (0, 0))],
        out_specs=pl.BlockSpec((tr, MEM_WIDTH), lambda i: (i, 0)),
        out_shape=jax.ShapeDtypeStruct((seq, MEM_WIDTH), BF16),
        compiler_params=_cparams(("parallel",)),
    )(q2, kv)


def _mem_bwd(name, q2, kv, do2, *, tr=512):
    seq = q2.shape[0]
    mlen = kv.shape[0]
    tr = min(tr, seq)

    def body(q_ref, kv_ref, do_ref, dq_ref, dkv_ref):
        i = pl.program_id(0)

        @pl.when(i == 0)
        def _():
            dkv_ref[...] = jnp.zeros_like(dkv_ref)

        for h in range(MEM_HEADS):
            sl = slice(h * MEM_HEAD_DIM, (h + 1) * MEM_HEAD_DIM)
            sv = slice(MEM_WIDTH + h * MEM_HEAD_DIM, MEM_WIDTH + (h + 1) * MEM_HEAD_DIM)
            qh = q_ref[:, sl]
            kh = kv_ref[:, sl]
            doh = do_ref[:, sl].astype(BF16)
            p = _mem_probs(qh, kh)
            dp = _bdot(doh, kv_ref[:, sv], _DIMS["nt"])
            ds = (p * (dp - jnp.sum(p * dp, axis=1, keepdims=True)) * MEM_SCALE).astype(BF16)
            dq_ref[:, sl] = _bdot(ds, kh, _DIMS["nn"]).astype(dq_ref.dtype)
            dkv_ref[:, sl] += _bdot(ds, qh, _DIMS["tn"])
            dkv_ref[:, sv] += _bdot(p, doh, _DIMS["tn"])

    row = pl.BlockSpec((tr, MEM_WIDTH), lambda i: (i, 0))
    kvs = pl.BlockSpec((mlen, 2 * MEM_WIDTH), lambda i: (0, 0))
    return pl.pallas_call(
        body, name=name, grid=(seq // tr,), in_specs=[row, kvs, row], out_specs=(row, kvs),
        out_shape=(jax.ShapeDtypeStruct((seq, MEM_WIDTH), BF16), jax.ShapeDtypeStruct((mlen, 2 * MEM_WIDTH), F32)),
        compiler_params=_cparams(("arbitrary",)),
    )(q2, kv, do2)


_HBM = pl.BlockSpec(memory_space=pl.ANY)
_HBM_ONLY = pl.BlockSpec(memory_space=pltpu.HBM)
_MESH = pl.DeviceIdType.MESH


def _mesh_place():
    x, y, c = lax.axis_index("x"), lax.axis_index("y"), lax.axis_index("c")
    other_chips = [(1 - x, y), (x, 1 - y), (1 - x, 1 - y)]
    return x, y, c, other_chips


def _gather_all(name, arrays):
    n = len(arrays)

    def body(*refs):
        ins, outs = refs[:n], refs[n:2 * n]
        send_sems, recv_sems, local_sems = refs[2 * n:]
        x, y, c, chips = _mesh_place()
        me, sibling = (x, y, c), (x, y, 1 - c)

        def slot(a, place):
            px, py, pc = place
            return outs[a].at[4 * px + 2 * py + pc]

        def copy(a, k, block, to, src=None):
            return pltpu.make_async_remote_copy(
                src_ref=slot(a, block) if src is None else src, dst_ref=slot(a, block),
                send_sem=send_sems.at[a, k], recv_sem=recv_sems.at[a, k], device_id=to, device_id_type=_MESH)

        mine = [pltpu.make_async_copy(ins[a], slot(a, me), local_sems.at[a]) for a in range(n)]
        for cp in mine:
            cp.start()
        first = []
        for a in range(n):
            first.append(copy(a, 0, me, sibling, src=ins[a]))
            first += [copy(a, 1 + j, me, (*chip, c), src=ins[a]) for j, chip in enumerate(chips)]
        for cp in first:
            cp.start()
        passed = []
        for j, chip in enumerate(chips):
            for a in range(n):
                copy(a, 1 + j, (*chip, c), me).wait_recv()
                fwd = copy(a, 4 + j, (*chip, c), sibling)
                fwd.start()
                passed.append(fwd)
        for a in range(n):
            copy(a, 0, sibling, me).wait_recv()
            for j, chip in enumerate(chips):
                copy(a, 4 + j, (*chip, 1 - c), me).wait_recv()
        for cp in first + passed:
            cp.wait_send()
        for cp in mine:
            cp.wait()

    out_shape = tuple(jax.ShapeDtypeStruct((N_DEV,) + arr.shape, arr.dtype) for arr in arrays)
    return pl.pallas_call(
        body, name=name, in_specs=[_HBM] * n, out_specs=tuple([_HBM] * n), out_shape=out_shape,
        scratch_shapes=[pltpu.SemaphoreType.DMA((n, N_DEV - 1)), pltpu.SemaphoreType.DMA((n, N_DEV - 1)),
                        pltpu.SemaphoreType.DMA((n,))],
    )(*arrays)


_SEM = pl.BlockSpec(memory_space=pltpu.SEMAPHORE)
_DATAFLOW = pltpu.SideEffectType.DATAFLOW_SIDE_EFFECTING


def _device_index():
    return (4 * lax.axis_index("x") + 2 * lax.axis_index("y") + lax.axis_index("c")).astype(jnp.int32).reshape(1)


def _place_own(name, pieces, *, stacked_src, after=None):
    n = len(pieces)
    n_in = n + (after is not None)

    def body(me_ref, *refs):
        for a in range(n):
            refs[n_in + a][...] = refs[a][...]

    def spec(shape):
        return pl.BlockSpec((None,) + tuple(shape), lambda i, me_ref: (me_ref[0],) + (0,) * len(shape))

    shapes = [p.shape[1:] if stacked_src else p.shape for p in pieces]
    if stacked_src:
        in_specs = [spec(s) for s in shapes]
    else:
        in_specs = [pl.BlockSpec(tuple(s), lambda i, me_ref, nd=len(s): (0,) * nd) for s in shapes]
    operands = list(pieces)
    if after is not None:
        in_specs.append(_HBM)
        operands.append(after)
    return pl.pallas_call(
        body, name=name,
        grid_spec=pltpu.PrefetchScalarGridSpec(num_scalar_prefetch=1, grid=(1,), in_specs=in_specs,
                                               out_specs=tuple(spec(s) for s in shapes)),
        out_shape=tuple(jax.ShapeDtypeStruct((N_DEV,) + tuple(s), p.dtype) for s, p in zip(shapes, pieces)),
        compiler_params=_cparams(("arbitrary",)),
    )(_device_index(), *operands)


def _peer_places():
    x, y, c = lax.axis_index("x"), lax.axis_index("y"), lax.axis_index("c")
    peers = []
    for k in range(N_DEV - 1):
        flip = k + 1
        px = 1 - x if flip & 4 else x
        py = 1 - y if flip & 2 else y
        pc = 1 - c if flip & 1 else c
        peers.append((px, py, pc, 4 * px + 2 * py + pc))
    return 4 * x + 2 * y + c, peers


def _direct_copy(srcs, lands, send_sems, recv_sems, a, k, me, peer, scatter):
    px, py, pc, pidx = peer
    return pltpu.make_async_remote_copy(
        src_ref=srcs[a].at[pidx] if scatter else srcs[a], dst_ref=lands[a].at[me],
        send_sem=send_sems.at[a * (N_DEV - 1) + k], recv_sem=recv_sems.at[a * (N_DEV - 1) + k],
        device_id=(px, py, pc), device_id_type=_MESH)


def _send_start(name, srcs, lands, *, scatter):
    n = len(srcs)

    def body(*refs):
        src_refs, land_refs = refs[:n], refs[n:2 * n]
        send_sems, recv_sems = refs[2 * n], refs[2 * n + 1]
        token = refs[-1]
        me, peers = _peer_places()
        for k, peer in enumerate(peers):
            for a in range(n):
                _direct_copy(src_refs, land_refs, send_sems, recv_sems, a, k, me, peer, scatter).start()
        token[...] = jnp.zeros_like(token)

    hbm_shapes = [pltpu.HBM(t.shape, t.dtype) for t in list(srcs) + list(lands)]
    outs = pl.pallas_call(
        body, name=name,
        out_shape=(pltpu.SemaphoreType.DMA((n * (N_DEV - 1),)), pltpu.SemaphoreType.DMA((n * (N_DEV - 1),)), *hbm_shapes,
                   jax.ShapeDtypeStruct((8, LANE), F32)),
        in_specs=[_HBM_ONLY] * (2 * n),
        out_specs=(_SEM, _SEM, *([_HBM_ONLY] * (2 * n)), pl.BlockSpec(memory_space=pltpu.VMEM)),
        input_output_aliases={i: 2 + i for i in range(2 * n)},
        compiler_params=pltpu.CompilerParams(has_side_effects=_DATAFLOW),
    )(*[pltpu.with_memory_space_constraint(t, pltpu.HBM) for t in list(srcs) + list(lands)])
    return outs[0], outs[1], outs[2:2 + n], outs[2 + n:2 + 2 * n], outs[-1]


def _send_wait(name, send_sems, recv_sems, srcs, lands, after, *, scatter):
    n = len(srcs)
    afters = list(after) if isinstance(after, (tuple, list)) else [after]

    def body(*refs):
        src_refs, land_refs = refs[:n], refs[n:2 * n]
        send_sems, recv_sems = refs[2 * n], refs[2 * n + 1]
        me, peers = _peer_places()
        for k, peer in enumerate(peers):
            for a in range(n):
                cp = _direct_copy(src_refs, land_refs, send_sems, recv_sems, a, k, me, peer, scatter)
                cp.wait_send()
                cp.wait_recv()

    hbm_shapes = [pltpu.HBM(t.shape, t.dtype) for t in list(srcs) + list(lands)]
    outs = pl.pallas_call(
        body, name=name, out_shape=tuple(hbm_shapes),
        in_specs=[_HBM_ONLY] * (2 * n) + [_SEM, _SEM] + [_HBM] * len(afters),
        out_specs=tuple([_HBM_ONLY] * (2 * n)),
        input_output_aliases={i: i for i in range(2 * n)},
        compiler_params=pltpu.CompilerParams(has_side_effects=_DATAFLOW),
    )(*srcs, *lands, send_sems, recv_sems, *afters)
    return outs[n:]


def _unstack_cols(name, stacked):
    n, rows, cols = stacked.shape

    def body(i_ref, o_ref):
        o_ref[...] = i_ref[...]

    return pl.pallas_call(
        body, name=name, grid=(n,), in_specs=[pl.BlockSpec((None, rows, cols), lambda k: (k, 0, 0))],
        out_specs=pl.BlockSpec((rows, cols), lambda k: (0, k)),
        out_shape=jax.ShapeDtypeStruct((rows, n * cols), stacked.dtype),
        compiler_params=_cparams(("parallel",)),
    )(stacked)


def _restack_cols(name, mat):
    rows, width = mat.shape
    cols = width // N_DEV

    def body(i_ref, o_ref):
        o_ref[...] = i_ref[...]

    return pl.pallas_call(
        body, name=name, grid=(N_DEV,), in_specs=[pl.BlockSpec((rows, cols), lambda k: (0, k))],
        out_specs=pl.BlockSpec((None, rows, cols), lambda k: (k, 0, 0)),
        out_shape=jax.ShapeDtypeStruct((N_DEV, rows, cols), mat.dtype),
        compiler_params=_cparams(("parallel",)),
    )(mat)


def _remap_pieces(runs):
    plan = {}
    for du, dc, su, sc, ln in runs:
        while ln > 0:
            lane = dc % LANE
            take = min(ln, LANE - lane)
            plan.setdefault((du, dc // LANE), []).append((su, sc, take, lane))
            dc, sc, ln = dc + take, sc + take, ln - take
    return plan


def _remap(name, srcs, src_units, runs, *, out_units, out_cols, out_dtype, tr=256):
    rows = srcs[0].shape[-2]
    tr = min(tr, rows)
    plan = _remap_pieces(runs)
    n_src = len(srcs)
    stacked_out = out_units is not None
    n_tiles = out_cols // LANE

    def body(*refs):
        o_ref = refs[n_src]

        def src_tile(unit, t):
            ai, lead = src_units[unit]
            ref = refs[ai]
            sl = slice(t * LANE, (t + 1) * LANE)
            return (ref[:, sl] if lead is None else ref[lead, :, sl]).astype(F32)

        lane = lax.broadcasted_iota(jnp.int32, (tr, LANE), 1)
        for du in range(out_units if stacked_out else 1):
            for t in range(n_tiles):
                acc = jnp.zeros((tr, LANE), F32)
                for su, sc, ln, dl in plan.get((du if stacked_out else None, t), []):
                    st, so = sc // LANE, sc % LANE
                    first = src_tile(su, st)
                    if so == dl and so + ln <= LANE:
                        piece = first
                    else:
                        second = src_tile(su, st + 1) if so + ln > LANE else first
                        both = jnp.concatenate([first, second], axis=1)
                        piece = pltpu.roll(both, (dl - so) % (2 * LANE), axis=1)[:, 0:LANE]
                    acc = piece if (dl == 0 and ln == LANE) else jnp.where(
                        jnp.logical_and(lane >= dl, lane < dl + ln), piece, acc)
                if stacked_out:
                    o_ref[du, :, t * LANE:(t + 1) * LANE] = acc.astype(o_ref.dtype)
                else:
                    o_ref[:, t * LANE:(t + 1) * LANE] = acc.astype(o_ref.dtype)

    in_specs = []
    for arr in srcs:
        if arr.ndim == 2:
            in_specs.append(pl.BlockSpec((tr, arr.shape[1]), lambda i: (i, 0)))
        else:
            in_specs.append(pl.BlockSpec((arr.shape[0], tr, arr.shape[2]), lambda i: (0, i, 0)))
    if stacked_out:
        out_spec = pl.BlockSpec((out_units, tr, out_cols), lambda i: (0, i, 0))
        out_shape = jax.ShapeDtypeStruct((out_units, rows, out_cols), out_dtype)
    else:
        out_spec = pl.BlockSpec((tr, out_cols), lambda i: (i, 0))
        out_shape = jax.ShapeDtypeStruct((rows, out_cols), out_dtype)
    return pl.pallas_call(
        body, name=name, grid=(rows // tr,), in_specs=in_specs, out_specs=out_spec, out_shape=out_shape,
        compiler_params=_cparams(("parallel",)),
    )(*srcs)


def _proj_col(c):
    if c < PROJ_GATE0:
        return c
    if c < PROJ_GATE0 + FOX_HEADS:
        return PROJ_F0 + (c - PROJ_GATE0)
    return c - FOX_HEADS


def _win_runs():
    cuts = sorted(set([0, PROJ_GATE0, PROJ_GATE0 + FOX_HEADS, IN_WIDTH] + [SHARD_IN * k for k in range(N_DEV + 1)]))
    return [(lo // SHARD_IN, lo % SHARD_IN, _proj_col(lo), hi - lo) for lo, hi in zip(cuts[:-1], cuts[1:])]


def _assemble_win(name, stacked):
    runs = [(None, pc, k, sc, ln) for k, sc, pc, ln in _win_runs()]
    return _remap(name, [stacked], [(0, k) for k in range(N_DEV)], runs,
                  out_units=None, out_cols=PROJ_WIDTH, out_dtype=BF16)


def _disassemble_dwin(name, dw):
    runs = [(k, sc, 0, pc, ln) for k, sc, pc, ln in _win_runs()]
    return _remap(name, [dw], [(0, None)], runs, out_units=N_DEV, out_cols=SHARD_IN_PAD, out_dtype=BF16)


def _concat_cols(name, parts, *, tr=512):
    rows = parts[0].shape[0]
    tr = min(tr, rows)
    widths = [p.shape[1] for p in parts]
    total = sum(widths)

    def body(*refs):
        o_ref = refs[len(parts)]
        lo = 0
        for r, w in zip(refs[:len(parts)], widths):
            o_ref[:, lo:lo + w] = r[...].astype(o_ref.dtype)
            lo += w

    return pl.pallas_call(
        body, name=name, grid=(rows // tr,),
        in_specs=[pl.BlockSpec((tr, w), lambda i: (i, 0)) for w in widths],
        out_specs=pl.BlockSpec((tr, total), lambda i: (i, 0)),
        out_shape=jax.ShapeDtypeStruct((rows, total), BF16),
        compiler_params=_cparams(("parallel",)),
    )(*parts)


FFN_BLK = FFN_HIDDEN // 2


def _ffn_col(c):
    half, r = divmod(c, FFN_HIDDEN)
    blk, r = divmod(r, FFN_BLK)
    return blk * 2 * FFN_BLK + half * FFN_BLK + r


def _assemble_wffn(name, stacked):
    runs = [(None, _ffn_col(SHARD_FFN * k), k, 0, SHARD_FFN) for k in range(N_DEV)]
    return _remap(name, [stacked], [(0, k) for k in range(N_DEV)], runs,
                  out_units=None, out_cols=2 * FFN_HIDDEN, out_dtype=BF16)


def _disassemble_dwffn(name, dw):
    runs = [(k, 0, 0, _ffn_col(SHARD_FFN * k), SHARD_FFN) for k in range(N_DEV)]
    return _remap(name, [dw], [(0, None)], runs, out_units=N_DEV, out_cols=SHARD_FFN_PAD, out_dtype=BF16)


def _ffn_in_swiglu(name, xn, w, *, tm=512):
    rows, k = xn.shape
    tm = min(tm, rows)
    nblk = FFN_HIDDEN // FFN_BLK

    def body(x_ref, w_ref, f_ref, g_ref):
        f = _bdot(x_ref[...], w_ref[...], _DIMS["nn"])
        f_ref[...] = f.astype(f_ref.dtype)
        fa = f[:, 0:FFN_BLK]
        g_ref[...] = (fa * _sigmoid(fa) * f[:, FFN_BLK:2 * FFN_BLK]).astype(g_ref.dtype)

    return pl.pallas_call(
        body, name=name, grid=(nblk, rows // tm),
        in_specs=[pl.BlockSpec((tm, k), lambda j, i: (i, 0)), pl.BlockSpec((k, 2 * FFN_BLK), lambda j, i: (0, j))],
        out_specs=(pl.BlockSpec((tm, 2 * FFN_BLK), lambda j, i: (i, j)), pl.BlockSpec((tm, FFN_BLK), lambda j, i: (i, j))),
        out_shape=(jax.ShapeDtypeStruct((rows, 2 * FFN_HIDDEN), BF16), jax.ShapeDtypeStruct((rows, FFN_HIDDEN), BF16)),
        compiler_params=_cparams(("parallel", "arbitrary")),
    )(xn, w)


def _d_ffn_out_swiglu(name, dh, w_out, f, *, tm=512):
    rows, d = dh.shape
    tm = min(tm, rows)
    nblk = FFN_HIDDEN // FFN_BLK

    def body(dh_ref, w_ref, f_ref, df_ref):
        dg = _bdot(dh_ref[...], w_ref[...], _DIMS["nt"])
        fa = f_ref[:, 0:FFN_BLK].astype(F32)
        fb = f_ref[:, FFN_BLK:2 * FFN_BLK].astype(F32)
        s = _sigmoid(fa)
        df_ref[:, 0:FFN_BLK] = (dg * fb * s * (1.0 + fa * (1.0 - s))).astype(df_ref.dtype)
        df_ref[:, FFN_BLK:2 * FFN_BLK] = (dg * fa * s).astype(df_ref.dtype)

    wide = pl.BlockSpec((tm, 2 * FFN_BLK), lambda j, i: (i, j))
    return pl.pallas_call(
        body, name=name, grid=(nblk, rows // tm),
        in_specs=[pl.BlockSpec((tm, d), lambda j, i: (i, 0)), pl.BlockSpec((FFN_BLK, d), lambda j, i: (j, 0)), wide],
        out_specs=wide, out_shape=jax.ShapeDtypeStruct((rows, 2 * FFN_HIDDEN), BF16),
        compiler_params=_cparams(("parallel", "arbitrary")),
    )(dh, w_out, f)


def _adamw(name, parts, w, m, v, *, tr=128):
    rows, cols = w.shape
    n_parts = parts.shape[0]
    tr = min(tr, rows)
    assert rows % tr == 0, (name, rows, tr)
    c1 = 1.0 - ADAM_B1 ** ADAM_STEP
    c2 = 1.0 - ADAM_B2 ** ADAM_STEP

    def body(p_ref, w_ref, m_ref, v_ref, g_ref, d_ref, nm_ref, nv_ref):
        g = p_ref[0].astype(F32)
        for s in range(1, n_parts):
            g = g + p_ref[s].astype(F32)
        m_new = ADAM_B1 * m_ref[...] + (1.0 - ADAM_B1) * g
        v_new = ADAM_B2 * v_ref[...] + (1.0 - ADAM_B2) * (g * g)
        upd = (m_new / c1) / (jnp.sqrt(v_new / c2) + ADAM_EPS) + ADAM_WD * w_ref[...]
        g_ref[...] = g
        d_ref[...] = -ADAM_LR * upd
        nm_ref[...] = m_new
        nv_ref[...] = v_new

    row = pl.BlockSpec((tr, cols), lambda i: (i, 0))
    out = jax.ShapeDtypeStruct((rows, cols), F32)
    return pl.pallas_call(
        body, name=name, grid=(rows // tr,),
        in_specs=[pl.BlockSpec((n_parts, tr, cols), lambda i: (0, i, 0)), row, row, row],
        out_specs=(row, row, row, row), out_shape=(out, out, out, out),
        compiler_params=_cparams(("parallel",)),
    )(parts, w, m, v)


_WEIGHTS = ("norm_mix", "w_in", "b_forget", "lam_re", "lam_im", "log_dt", "b_re", "b_im", "c_re", "c_im",
            "d_skip", "w_glu", "w_fox_o", "w_mix_out", "norm_mem_q", "norm_mem_kv", "w_mem_q", "w_mem_kv",
            "w_mem_o", "norm_ffn", "w_ffn_in", "w_ffn_out", "norm_final")
_SHARDED = ("w_in", "w_glu", "w_fox_o", "w_mix_out", "w_mem_q", "w_mem_kv", "w_mem_o", "w_ffn_in", "w_ffn_out")
_SMALL = tuple(n for n in _WEIGHTS if n not in _SHARDED)
_PACK_COLS = 1024


def _pack(arrays):
    flat = jnp.concatenate([a.reshape(-1).astype(F32) for a in arrays])
    rows = -(-flat.shape[0] // _PACK_COLS)
    return jnp.pad(flat, (0, rows * _PACK_COLS - flat.shape[0])).reshape(rows, _PACK_COLS)


def _unpack(buf, like):
    flat = buf.reshape(-1)
    out, pos = [], 0
    for a in like:
        out.append(flat[pos:pos + a.size].reshape(a.shape))
        pos += a.size
    return out


def _proj_fwd(name, u, win, *, tm=512):
    seq, d = u.shape
    cuts = (0, SSM_WIDTH, PROJ_GATE0, PROJ_F0, PROJ_WIDTH)
    dtypes = (F32, BF16, BF16, F32)

    def body(u_ref, w_ref, *o_refs):
        uv = u_ref[...]
        for o_ref, lo, hi in zip(o_refs, cuts[:-1], cuts[1:]):
            o_ref[...] = _bdot(uv, w_ref[:, lo:hi], _DIMS["nn"]).astype(o_ref.dtype)

    widths = [hi - lo for lo, hi in zip(cuts[:-1], cuts[1:])]
    return pl.pallas_call(
        body, name=name, grid=(seq // tm,),
        in_specs=[pl.BlockSpec((tm, d), lambda i: (i, 0)), pl.BlockSpec((d, PROJ_WIDTH), lambda i: (0, 0))],
        out_specs=tuple(pl.BlockSpec((tm, w), lambda i: (i, 0)) for w in widths),
        out_shape=tuple(jax.ShapeDtypeStruct((seq, w), t) for w, t in zip(widths, dtypes)),
        compiler_params=_cparams(("parallel",)),
    )(u, win)


def _mm(name, a, b, mode, m, n, k, out_dtype, tm=1024, tn=512, tk=1024, **kw):
    return _matmul(name, a, b, mode, m, n, k, out_dtype=out_dtype, tm=tm, tn=tn, tk=tk, **kw)


def kernel(x, mem, norm_mix, w_in, b_forget, lam_re, lam_im, log_dt, b_re, b_im, c_re, c_im, d_skip, w_glu, w_fox_o, w_mix_out, norm_mem_q, norm_mem_kv, w_mem_q, w_mem_kv, w_mem_o, norm_ffn, w_ffn_in, w_ffn_out, norm_final, loss_target, m_norm_mix, m_w_in, m_b_forget, m_lam_re, m_lam_im, m_log_dt, m_b_re, m_b_im, m_c_re, m_c_im, m_d_skip, m_w_glu, m_w_fox_o, m_w_mix_out, m_norm_mem_q, m_norm_mem_kv, m_w_mem_q, m_w_mem_kv, m_w_mem_o, m_norm_ffn, m_w_ffn_in, m_w_ffn_out, m_norm_final, v_norm_mix, v_w_in, v_b_forget, v_lam_re, v_lam_im, v_log_dt, v_b_re, v_b_im, v_c_re, v_c_im, v_d_skip, v_w_glu, v_w_fox_o, v_w_mix_out, v_norm_mem_q, v_norm_mem_kv, v_w_mem_q, v_w_mem_kv, v_w_mem_o, v_norm_ffn, v_w_ffn_in, v_w_ffn_out, v_norm_final):
    given = dict(locals())
    weights = {n: given[n] for n in _WEIGHTS}
    mom_m = {n: given["m_" + n] for n in _WEIGHTS}
    mom_v = {n: given["v_" + n] for n in _WEIGHTS}
    seq = x.shape[1]
    nc = seq // SSM_CHUNK
    d = D_MODEL
    xs, mems, tgt = x[0], mem[0], loss_target[0]

    def padcols(a, width):
        return jnp.pad(a, ((0, 0), (0, width - a.shape[1])))

    shards = [padcols(w_in[0].astype(BF16), SHARD_IN_PAD), w_glu[0].astype(BF16), w_fox_o[0].astype(BF16),
              w_mix_out[0].astype(BF16), w_mem_q[0].astype(BF16), w_mem_kv[0].astype(BF16),
              w_mem_o[0].astype(BF16), padcols(w_ffn_in[0].astype(BF16), SHARD_FFN_PAD), w_ffn_out[0].astype(BF16)]
    first = shards[:1]
    wsend, wrecv, first_thru, first_lands, wtoken = _send_start(
        "gather_w_in_start", first, _place_own("place_w_in_shard", first, stacked_src=False), scatter=False)
    rest = shards[1:]
    gsend, grecv, rest_thru, lands, gtoken = _send_start(
        "gather_rest_start", rest, _place_own("place_weight_shards", rest, stacked_src=False, after=wtoken),
        scatter=False)

    u = _rms_fwd("rms_mix", xs, norm_mix, after=gtoken)
    ssm_params = tuple(p[0] + wtoken[0, 0] for p in (lam_re, lam_im, log_dt, b_re, b_im, c_re, c_im))
    (m_c, bw_c, cm_c, a8, aseg), mats_vjp = jax.vjp(lambda *p: _ssm_mats(*p, nc), *ssm_params)
    m_b = _bd_expand("ssm_expand_m", _BD_M, m_c)
    bw_b = _bd_expand("ssm_expand_bw", _BD_BW, bw_c)
    cm_b = _bd_expand("ssm_expand_cm", _BD_CM, cm_c)
    win = _assemble_win("assemble_w_in", _send_wait(
        "gather_w_in_wait", wsend, wrecv, first_thru, first_lands, (u, m_b, bw_b, cm_b), scatter=False)[0])
    ussm, qkv, gates, fproj = _proj_fwd("proj", u, win)

    u8 = ussm.reshape(nc, SSM_CHUNK * SSM_WIDTH)
    d8 = jnp.tile(d_skip, (1, SSM_CHUNK))
    w4 = _ssm_w("ssm_w", u8, bw_b)
    sp4 = _ssm_scan("ssm_scan", w4, a8, aseg, reverse=False)
    y8 = _ssm_y("ssm_y", u8, sp4, m_b, cm_b)
    act = _ssm_post_fwd("ssm_act", y8, u8, d8).reshape(seq, SSM_WIDTH)

    bcol = jnp.pad(b_forget[0], (0, LANE - FOX_HEADS)).reshape(LANE, 1)
    cum_t = _fox_cum("fox_cum", fproj, bcol).reshape(FOX_HEADS // 2, 2, seq)
    att, lse = _fox_fwd("fox_fwd", qkv, cum_t)

    gathered = _send_wait("gather_rest_wait", gsend, grecv, rest_thru, lands, att, scatter=False)
    wglu = _unstack_cols("unstack_w_glu", gathered[0])
    wfoxo = _unstack_cols("unstack_w_fox_o", gathered[1])
    wmix = gathered[2].reshape(d, d)
    wmq = gathered[3].reshape(d, MEM_WIDTH)
    wmkv = gathered[4].reshape(d, 2 * MEM_WIDTH)
    wmo = _unstack_cols("unstack_w_mem_o", gathered[5])
    wffn_in = _assemble_wffn("assemble_w_ffn_in", gathered[6])
    wffn_out = gathered[7].reshape(FFN_HIDDEN, d)

    glu = _mm("glu", act, wglu, "nn", seq, 2 * d, SSM_WIDTH, BF16, tn=1024)
    out_b = _mm("fox_out", att, wfoxo, "nn", seq, d, FOX_WIDTH, BF16, tn=1024)

    mixin, h1 = _mix_fwd("mix_mix_out", glu, gates, out_b, wmix, xs)

    n1 = _rms_fwd("rms_mem_q", h1, norm_mem_q)
    q2 = _mm("mem_q", n1, wmq, "nn", seq, MEM_WIDTH, d, BF16)
    mn = _rms_fwd("rms_mem_kv", mems, norm_mem_kv)
    mlen = mems.shape[0]
    kv = _mm("mem_kv", mn, wmkv, "nn", mlen, 2 * MEM_WIDTH, d, BF16)
    o2 = _mem_fwd("mem_attn", q2, kv)
    h2 = _mm("mem_out", o2, wmo, "nn", seq, d, MEM_WIDTH, F32, tn=1024, add=h1)

    n2 = _rms_fwd("rms_ffn", h2, norm_ffn)
    f, g_act = _ffn_in_swiglu("ffn_in_swiglu", n2, wffn_in)
    loss_part, dh3, dg_final = _matmul_final_loss("ffn_out_final_loss", g_act, wffn_out, h2, tgt,
                                                  norm_final.reshape(1, d))

    df = _d_ffn_out_swiglu("d_ffn_out_swiglu", dh3, wffn_out, f)
    dwffn_out = _mm("d_ffn_out_w", g_act, dh3, "tn", FFN_HIDDEN, d, seq, BF16, tm=1408, tn=1024)
    dh2, dg_ffn = _matmul_rms_bwd("d_ffn_in_x_rms", df, wffn_in, 2 * FFN_HIDDEN, h2, norm_ffn, dh3, tm=1024, tk=1408)
    dwffn_in = _mm("d_ffn_in_w", n2, df, "tn", d, 2 * FFN_HIDDEN, seq, BF16, tn=1408)

    do2 = _mm("d_mem_out_x", dh2, wmo, "nt", seq, MEM_WIDTH, d, F32)
    dwmo = _restack_cols("restack_d_w_mem_o", _mm("d_mem_out_w", o2, dh2, "tn", MEM_WIDTH, d, seq, BF16, tn=1024))
    dq2, dkv = _mem_bwd("d_mem_attn", q2, kv, do2)
    dwmq = _mm("d_mem_q_w", n1, dq2, "tn", d, MEM_WIDTH, seq, BF16)
    dwmkv = _mm("d_mem_kv_w", mn, dkv, "tn", d, 2 * MEM_WIDTH, mlen, BF16, tn=1024)
    dmn = _mm("d_mem_kv_x", dkv, wmkv, "nt", mlen, d, 2 * MEM_WIDTH, F32)
    dg_memkv = _rms_gain_grad("d_rms_mem_kv", dmn, mems)

    early = [dwmq.reshape(N_DEV, d // N_DEV, MEM_WIDTH), dwmkv.reshape(N_DEV, d // N_DEV, 2 * MEM_WIDTH), dwmo,
             _disassemble_dwffn("split_d_w_ffn_in", dwffn_in), dwffn_out.reshape(N_DEV, FFN_HIDDEN // N_DEV, d)]
    ssend, srecv, early_thru, early_lands, stoken = _send_start(
        "scatter_early_start", early, _place_own("place_early_grads", early, stacked_src=True), scatter=True)
    dh1, dg_memq = _matmul_rms_bwd("d_mem_q_x_rms", dq2, wmq, MEM_WIDTH, h1, norm_mem_q, dh2, tm=1024, after=stoken)

    dwmix = _mm("d_mix_out_w", mixin, dh1, "tn", d, d, seq, BF16, tn=1024)
    dglu, dgates, dout_b = _mix_bwd("d_mix_out_x_mix", dh1, wmix, glu, gates, out_b)
    datt = _mm("d_fox_out_x", dout_b, wfoxo, "nt", seq, FOX_WIDTH, d, F32)
    dwfoxo = _restack_cols("restack_d_w_fox_o", _mm("d_fox_out_w", att, dout_b, "tn", FOX_WIDTH, d, seq, BF16, tn=1024))
    dact = _mm("d_glu_x", dglu, wglu, "nt", seq, SSM_WIDTH, 2 * d, F32, tk=2 * d)
    dwglu = _restack_cols("restack_d_w_glu", _mm("d_glu_w", act, dglu, "tn", SSM_WIDTH, 2 * d, seq, BF16, tn=2 * d))

    mid = [dwglu, dwfoxo, dwmix.reshape(N_DEV, d // N_DEV, d)]
    msend, mrecv, mid_thru, mid_lands, mtoken = _send_start(
        "scatter_mid_start", mid, _place_own("place_mid_grads", mid, stacked_src=True), scatter=True)

    dz8, dg_dskip = _ssm_post_bwd("d_ssm_act", dact.reshape(nc, SSM_CHUNK * SSM_WIDTH), y8, u8, d8, after=mtoken)
    ds4, dcm = _ssm_ds("d_ssm_y_state", dz8, sp4, cm_b)
    g4, da8 = _ssm_scan("d_ssm_scan", ds4, a8, aseg, reverse=True, sprev4=sp4)
    dx8, dm, dbw = _ssm_dx("d_ssm_x", dz8, g4, u8, m_b, bw_b, d8)
    dussm = dx8.reshape(seq, SSM_WIDTH)
    g_ssm = mats_vjp((_bd_reduce("ssm_reduce_dm", _BD_M, dm), _bd_reduce("ssm_reduce_dbw", _BD_BW, dbw),
                      _bd_reduce("ssm_reduce_dcm", _BD_CM, dcm), da8, jnp.zeros_like(aseg)))

    dq, dk, dv, dcs = _fox_bwd("d_fox", qkv, cum_t, att, datt, lse)
    dfproj, dbf = _fox_cum_bwd("d_fox_cum", dcs, fproj, bcol)
    dg_bforget = dbf[0:FOX_HEADS, 0].reshape(1, FOX_HEADS)

    dproj = _concat_cols("d_proj_concat", (dussm, dq, dk, dv, dgates, dfproj))
    dwin = _mm("d_proj_w", u, dproj, "tn", d, PROJ_WIDTH, seq, BF16, tn=1408)
    late = [_disassemble_dwin("split_d_w_in", dwin)]
    lsend, lrecv, late_thru, late_lands, ltoken = _send_start(
        "scatter_late_start", late, _place_own("place_late_grads", late, stacked_src=True), scatter=True)
    dx, dg_mix = _matmul_rms_bwd("d_proj_x_rms", dproj, win, PROJ_WIDTH, xs, norm_mix, dh1, tm=1024, tk=1408,
                                 after=ltoken)

    early_parts = _send_wait("scatter_early_wait", ssend, srecv, early_thru, early_lands, dx, scatter=True)
    mid_parts = _send_wait("scatter_mid_wait", msend, mrecv, mid_thru, mid_lands, dx, scatter=True)
    received = dict(zip(("w_glu", "w_fox_o", "w_mix_out"), mid_parts))
    received.update(zip(("w_mem_q", "w_mem_kv", "w_mem_o", "w_ffn_in", "w_ffn_out"), early_parts))

    small_grads = dict(zip(
        _SMALL, (dg_mix, dg_bforget, g_ssm[0][None], g_ssm[1][None], g_ssm[2][None], g_ssm[3][None], g_ssm[4][None],
                 g_ssm[5][None], g_ssm[6][None], dg_dskip, dg_memq, dg_memkv, dg_ffn, dg_final.reshape(d))))
    small_like = [weights[n] for n in _SMALL]
    small_all = _gather_all("gather_small_grads", [_pack([small_grads[n] for n in _SMALL])])[0]
    pk = [_pack([src[n] for n in _SMALL]) for src in (weights, mom_m, mom_v)]
    small_out = _adamw("adamw_small", small_all, pk[0], pk[1], pk[2], tr=small_all.shape[1])
    results = [dict(zip(_SMALL, _unpack(buf, small_like))) for buf in small_out]
    tiles = {"w_in": 128, "w_glu": 128, "w_fox_o": 128, "w_mix_out": 128, "w_mem_q": 128, "w_mem_kv": 128,
             "w_mem_o": 128, "w_ffn_in": 128, "w_ffn_out": 176}
    pads = {"w_in": SHARD_IN_PAD, "w_ffn_in": SHARD_FFN_PAD}
    outs = small_out
    for name in _SHARDED[1:] + _SHARDED[:1]:
        if name == "w_in":
            received[name] = _send_wait("scatter_late_wait", lsend, lrecv, late_thru, late_lands, outs[0],
                                        scatter=True)[0]
        parts = received[name]
        w2, m2, v2 = weights[name][0], mom_m[name][0], mom_v[name][0]
        cols = w2.shape[1]
        if name in pads:
            w2, m2, v2 = (padcols(t, pads[name]) for t in (w2, m2, v2))
        outs = _adamw("adamw_" + name, parts, w2, m2, v2, tr=tiles[name])
        for res, o in zip(results, outs):
            res[name] = o[:, :cols][None]

    loss = lax.psum(loss_part[0, 0], ("x", "y", "c"))
    out = [loss, dx[None]]
    for res in results:
        out.extend(res[n] for n in _WEIGHTS)
    return tuple(out)
```

```python
import math

import jax
import jax.numpy as jnp
import numpy as np
from jax import lax
from jax.experimental import pallas as pl
from jax.experimental.pallas import tpu as pltpu

F32 = jnp.float32
BF16 = jnp.bfloat16

N_DEV = 8
LANE = 128
VMEM_LIMIT = 56 * 1024 * 1024

D_MODEL = 1024
SSM_GROUP = 16
SSM_GROUPS = 32
SSM_WIDTH = 512
SSM_STATE = 64
SSM_CHUNK = 8
FOX_HEADS = 8
FOX_HEAD_DIM = 64
FOX_WIDTH = 512
MEM_HEADS = 4
MEM_HEAD_DIM = 128
MEM_WIDTH = 512
FFN_HIDDEN = 2816
RMS_EPS = 1e-6
IN_WIDTH = 4104
SHARD_IN = IN_WIDTH // N_DEV
SHARD_IN_PAD = 640
SHARD_FFN = 2 * FFN_HIDDEN // N_DEV
SHARD_FFN_PAD = 768
PROJ_GATE0 = 2048
PROJ_F0 = 4096
PROJ_WIDTH = 4224

ADAM_LR = 0.001
ADAM_B1 = 0.9
ADAM_B2 = 0.999
ADAM_EPS = 1e-08
ADAM_WD = 0.01
ADAM_STEP = 10


def _cparams(sem=None):
    return pltpu.CompilerParams(dimension_semantics=sem, vmem_limit_bytes=VMEM_LIMIT)


def _sigmoid(x):
    return 1.0 / (1.0 + jnp.exp(-x))


def _bdot(a, b, dims):
    return lax.dot_general(a.astype(BF16), b.astype(BF16), ((dims[0], dims[1]), ((), ())),
                           preferred_element_type=F32)


_DIMS = {"nn": ((1,), (0,)), "nt": ((1,), (1,)), "tn": ((0,), (0,))}


def _matmul(name, a, b, mode, m, n, k, *, out_dtype, tm, tn, tk, a_off=(0, 0), b_off=(0, 0), add=None):
    tm, tn, tk = min(tm, m), min(tn, n), min(tk, k)
    assert m % tm == 0 and n % tn == 0 and k % tk == 0, (name, m, n, k, tm, tn, tk)
    nk = k // tk
    grid = (m // tm, n // tn, nk)

    def blk(off, t):
        assert off % t == 0, (name, off, t)
        return off // t

    if mode in ("nn", "nt"):
        ar, ac = blk(a_off[0], tm), blk(a_off[1], tk)
        a_spec = pl.BlockSpec((tm, tk), lambda i, j, kk: (i + ar, kk + ac))
    else:
        ar, ac = blk(a_off[0], tk), blk(a_off[1], tm)
        a_spec = pl.BlockSpec((tk, tm), lambda i, j, kk: (kk + ar, i + ac))

    if mode in ("nn", "tn"):
        br, bc = blk(b_off[0], tk), blk(b_off[1], tn)
        b_spec = pl.BlockSpec((tk, tn), lambda i, j, kk: (kk + br, j + bc))
    else:
        br, bc = blk(b_off[0], tn), blk(b_off[1], tk)
        b_spec = pl.BlockSpec((tn, tk), lambda i, j, kk: (j + br, kk + bc))
    o_spec = pl.BlockSpec((tm, tn), lambda i, j, kk: (i, j))
    out_shape = jax.ShapeDtypeStruct((m, n), out_dtype)

    in_specs = [a_spec, b_spec]
    operands = [a, b]
    if add is not None:
        in_specs.append(pl.BlockSpec((tm, tn), lambda i, j, kk: (i, j)))
        operands.append(add)
    dims = _DIMS[mode]
    has_add = add is not None

    def body(*refs):
        a_ref, b_ref = refs[0], refs[1]
        add_ref = refs[2] if has_add else None
        o_ref = refs[3] if has_add else refs[2]
        acc_ref = refs[-1] if nk > 1 else None
        prod = _bdot(a_ref[...], b_ref[...], dims)

        def finish(total):
            if has_add:
                total = total + add_ref[...].astype(F32)
            o_ref[...] = total.astype(o_ref.dtype)

        if nk == 1:
            finish(prod)
        else:
            kk = pl.program_id(2)

            @pl.when(kk == 0)
            def _():
                acc_ref[...] = prod

            @pl.when(jnp.logical_and(kk > 0, kk < nk - 1))
            def _():
                acc_ref[...] += prod

            @pl.when(kk == nk - 1)
            def _():
                finish(acc_ref[...] + prod)

    scratch = [pltpu.VMEM((tm, tn), F32)] if nk > 1 else []
    return pl.pallas_call(
        body, name=name, grid=grid, in_specs=in_specs, out_specs=o_spec, out_shape=out_shape,
        scratch_shapes=scratch,
        compiler_params=_cparams(("parallel", "parallel", "arbitrary")),
    )(*operands)


def _rms_fwd(name, x, gain, *, tr=512, after=None):
    r, d = x.shape
    tr = min(tr, r)

    def body(x_ref, g_ref, *rest):
        o_ref = rest[-1]
        xv = x_ref[...]
        rstd = lax.rsqrt(jnp.mean(xv * xv, axis=-1, keepdims=True) + RMS_EPS)
        o_ref[...] = (xv * rstd * g_ref[...]).astype(o_ref.dtype)

    in_specs = [pl.BlockSpec((tr, d), lambda i: (i, 0)), pl.BlockSpec((1, d), lambda i: (0, 0))]
    ops = [x, gain]
    if after is not None:
        in_specs.append(pl.BlockSpec(after.shape, lambda i: (0, 0)))
        ops.append(after)
    return pl.pallas_call(
        body, name=name, grid=(r // tr,), in_specs=in_specs,
        out_specs=pl.BlockSpec((tr, d), lambda i: (i, 0)),
        out_shape=jax.ShapeDtypeStruct((r, d), BF16),
        compiler_params=_cparams(("parallel",)),
    )(*ops)


def _rms_gain_grad(name, dy, x, *, tr=512):
    r, d = x.shape
    tr = min(tr, r)
    n = r // tr

    def body(dy_ref, x_ref, dg_ref, acc_ref):
        i = pl.program_id(0)
        xv = x_ref[...]
        xh = xv * lax.rsqrt(jnp.mean(xv * xv, axis=-1, keepdims=True) + RMS_EPS)
        part = (dy_ref[...].astype(F32) * xh).reshape(tr // 8, 8, d).sum(axis=0)

        @pl.when(i == 0)
        def _():
            acc_ref[...] = part

        @pl.when(i > 0)
        def _():
            acc_ref[...] += part

        @pl.when(i == n - 1)
        def _():
            dg_ref[...] = jnp.sum(acc_ref[...], axis=0, keepdims=True)

    row = pl.BlockSpec((tr, d), lambda i: (i, 0))
    return pl.pallas_call(
        body, name=name, grid=(n,), in_specs=[row, row],
        out_specs=pl.BlockSpec((1, d), lambda i: (0, 0)),
        out_shape=jax.ShapeDtypeStruct((1, d), F32),
        scratch_shapes=[pltpu.VMEM((8, d), F32)],
        compiler_params=_cparams(("arbitrary",)),
    )(dy, x)


def _matmul_rms_bwd(name, a, b, k, x, gain, res, *, tm=512, tk=1024, after=None):
    m, d = x.shape
    tm, tk = min(tm, m), min(tk, k)
    assert m % tm == 0 and k % tk == 0, (name, m, k, tm, tk)
    ni, nk = m // tm, k // tk

    def body(a_ref, b_ref, x_ref, g_ref, res_ref, *rest):
        dx_ref, dg_ref, acc_ref, accg_ref = rest[-4:]
        i, kk = pl.program_id(0), pl.program_id(1)
        prod = _bdot(a_ref[...], b_ref[...], _DIMS["nt"])

        @pl.when(kk == 0)
        def _():
            acc_ref[...] = prod

        @pl.when(kk > 0)
        def _():
            acc_ref[...] += prod

        @pl.when(kk == nk - 1)
        def _():
            dyv = acc_ref[...]
            xv = x_ref[...]
            rstd = lax.rsqrt(jnp.mean(xv * xv, axis=-1, keepdims=True) + RMS_EPS)
            xh = xv * rstd
            dxh = dyv * g_ref[...]
            dx_ref[...] = rstd * (dxh - xh * jnp.mean(dxh * xh, axis=-1, keepdims=True)) + res_ref[...]
            part = (dyv * xh).reshape(tm // 8, 8, d).sum(axis=0)

            @pl.when(i == 0)
            def _():
                accg_ref[...] = part

            @pl.when(i > 0)
            def _():
                accg_ref[...] += part

            @pl.when(i == ni - 1)
            def _():
                dg_ref[...] = jnp.sum(accg_ref[...], axis=0, keepdims=True)

    row = pl.BlockSpec((tm, d), lambda i, kk: (i, 0))
    one = pl.BlockSpec((1, d), lambda i, kk: (0, 0))
    in_specs = [pl.BlockSpec((tm, tk), lambda i, kk: (i, kk)), pl.BlockSpec((d, tk), lambda i, kk: (0, kk)), row, one, row]
    ops = [a, b, x, gain, res]
    if after is not None:
        in_specs.append(pl.BlockSpec(after.shape, lambda i, kk: (0, 0)))
        ops.append(after)
    return pl.pallas_call(
        body, name=name, grid=(ni, nk), in_specs=in_specs, out_specs=(row, one),
        out_shape=(jax.ShapeDtypeStruct((m, d), F32), jax.ShapeDtypeStruct((1, d), F32)),
        scratch_shapes=[pltpu.VMEM((tm, d), F32), pltpu.VMEM((8, d), F32)],
        compiler_params=_cparams(("arbitrary", "arbitrary")),
    )(*ops)


def _matmul_final_loss(name, a, b, res, target, gain, *, tr=512):
    r, d = res.shape
    k = a.shape[1]
    tr = min(tr, r)
    n = r // tr

    def body(a_ref, b_ref, res_ref, t_ref, g_ref, loss_ref, dh_ref, dg_ref, accl_ref, accg_ref):
        i = pl.program_id(0)
        xv = _bdot(a_ref[...], b_ref[...], _DIMS["nn"]) + res_ref[...]
        rstd = lax.rsqrt(jnp.mean(xv * xv, axis=-1, keepdims=True) + RMS_EPS)
        xh = xv * rstd
        e = xh * g_ref[...] - t_ref[...]
        dyv = e * (1.0 / d)
        dxh = dyv * g_ref[...]
        dh_ref[...] = rstd * (dxh - xh * jnp.mean(dxh * xh, axis=-1, keepdims=True))
        lpart = (e * e).reshape(tr // 8, 8, d).sum(axis=0)
        gpart = (dyv * xh).reshape(tr // 8, 8, d).sum(axis=0)

        @pl.when(i == 0)
        def _():
            accl_ref[...] = lpart
            accg_ref[...] = gpart

        @pl.when(i > 0)
        def _():
            accl_ref[...] += lpart
            accg_ref[...] += gpart

        @pl.when(i == n - 1)
        def _():
            tot = jnp.sum(jnp.sum(accl_ref[...], axis=0, keepdims=True), axis=1, keepdims=True)
            loss_ref[...] = jnp.broadcast_to(tot * (0.5 / d), (1, LANE))
            dg_ref[...] = jnp.sum(accg_ref[...], axis=0, keepdims=True)

    row = pl.BlockSpec((tr, d), lambda i: (i, 0))
    one = pl.BlockSpec((1, d), lambda i: (0, 0))
    return pl.pallas_call(
        body, name=name, grid=(n,),
        in_specs=[pl.BlockSpec((tr, k), lambda i: (i, 0)), pl.BlockSpec((k, d), lambda i: (0, 0)), row, row, one],
        out_specs=(pl.BlockSpec((1, LANE), lambda i: (0, 0)), row, one),
        out_shape=(jax.ShapeDtypeStruct((1, LANE), F32), jax.ShapeDtypeStruct((r, d), F32),
                   jax.ShapeDtypeStruct((1, d), F32)),
        scratch_shapes=[pltpu.VMEM((8, d), F32), pltpu.VMEM((8, d), F32)],
        compiler_params=_cparams(("arbitrary",)),
    )(a, b, res, target, gain)


_GELU_C = math.sqrt(2.0 / math.pi)


def _gelu_parts(z):
    inner = _GELU_C * (z + 0.044715 * z * z * z)
    t = jnp.tanh(inner)
    val = 0.5 * z * (1.0 + t)
    dinner = _GELU_C * (1.0 + 3.0 * 0.044715 * z * z)
    grad = 0.5 * (1.0 + t) + 0.5 * z * (1.0 - t * t) * dinner
    return val, grad


def _ssm_post_fwd(name, y8, u8, d8, *, tr=256):
    r, c = y8.shape
    tr = min(tr, r)

    def body(y_ref, u_ref, d_ref, o_ref):
        z = y_ref[...] + d_ref[...] * u_ref[...]
        o_ref[...] = _gelu_parts(z)[0].astype(o_ref.dtype)

    row = pl.BlockSpec((tr, c), lambda i: (i, 0))
    return pl.pallas_call(
        body, name=name, grid=(r // tr,), in_specs=[row, row, pl.BlockSpec((1, c), lambda i: (0, 0))],
        out_specs=row, out_shape=jax.ShapeDtypeStruct((r, c), BF16),
        compiler_params=_cparams(("parallel",)),
    )(y8, u8, d8)


def _ssm_post_bwd(name, dact8, y8, u8, d8, *, tr=256, after=None):
    r, c = y8.shape
    tr = min(tr, r)
    n = r // tr

    def body(*refs):
        da_ref, y_ref, u_ref, d_ref = refs[:4]
        dz_ref, dd_ref, acc_ref = refs[-3:]
        i = pl.program_id(0)
        uv = u_ref[...]
        z = y_ref[...] + d_ref[...] * uv
        dz = da_ref[...].astype(F32) * _gelu_parts(z)[1]
        dz_ref[...] = dz
        part = (dz * uv).reshape(tr // 8, 8, c).sum(axis=0)

        @pl.when(i == 0)
        def _():
            acc_ref[...] = part

        @pl.when(i > 0)
        def _():
            acc_ref[...] += part

        @pl.when(i == n - 1)
        def _():
            tot = jnp.sum(acc_ref[...], axis=0, keepdims=True)
            out = tot[:, 0:SSM_WIDTH]
            for j in range(1, c // SSM_WIDTH):
                out = out + tot[:, j * SSM_WIDTH:(j + 1) * SSM_WIDTH]
            dd_ref[...] = out

    row = pl.BlockSpec((tr, c), lambda i: (i, 0))
    in_specs = [row, row, row, pl.BlockSpec((1, c), lambda i: (0, 0))]
    ops = [dact8, y8, u8, d8]
    if after is not None:
        in_specs.append(pl.BlockSpec(memory_space=pl.ANY))
        ops.append(after)
    return pl.pallas_call(
        body, name=name, grid=(n,), in_specs=in_specs,
        out_specs=(row, pl.BlockSpec((1, SSM_WIDTH), lambda i: (0, 0))),
        out_shape=(jax.ShapeDtypeStruct((r, c), F32), jax.ShapeDtypeStruct((1, SSM_WIDTH), F32)),
        scratch_shapes=[pltpu.VMEM((8, c), F32)],
        compiler_params=_cparams(("arbitrary",)),
    )(*ops)


def _mix_fwd(name, glu, gates, out_b, w_mix, res, *, tr=512):
    r = glu.shape[0]
    d = D_MODEL
    tr = min(tr, r)

    def body(glu_ref, gate_ref, ob_ref, w_ref, res_ref, o_ref, h_ref):
        out_a = glu_ref[:, 0:d].astype(F32) * _sigmoid(glu_ref[:, d:2 * d].astype(F32))
        mix = (_sigmoid(gate_ref[:, 0:d].astype(F32)) * out_a
               + _sigmoid(gate_ref[:, d:2 * d].astype(F32)) * ob_ref[...].astype(F32))
        o_ref[...] = mix.astype(o_ref.dtype)
        h_ref[...] = _bdot(o_ref[...], w_ref[...], _DIMS["nn"]) + res_ref[...]

    wide = pl.BlockSpec((tr, 2 * d), lambda i: (i, 0))
    row = pl.BlockSpec((tr, d), lambda i: (i, 0))
    return pl.pallas_call(
        body, name=name, grid=(r // tr,),
        in_specs=[wide, wide, row, pl.BlockSpec((d, d), lambda i: (0, 0)), row], out_specs=(row, row),
        out_shape=(jax.ShapeDtypeStruct((r, d), BF16), jax.ShapeDtypeStruct((r, d), F32)),
        compiler_params=_cparams(("parallel",)),
    )(glu, gates, out_b, w_mix, res)


def _mix_bwd(name, dh, w_mix, glu, gates, out_b, *, tr=512):
    r = glu.shape[0]
    d = D_MODEL
    tr = min(tr, r)

    def body(dh_ref, w_ref, glu_ref, gate_ref, ob_ref, dglu_ref, dgate_ref, dob_ref):
        dm = _bdot(dh_ref[...], w_ref[...], _DIMS["nt"])
        glu_a = glu_ref[:, 0:d].astype(F32)
        sb = _sigmoid(glu_ref[:, d:2 * d].astype(F32))
        ga = _sigmoid(gate_ref[:, 0:d].astype(F32))
        gb = _sigmoid(gate_ref[:, d:2 * d].astype(F32))
        out_a = glu_a * sb
        dout_a = dm * ga
        dglu_ref[:, 0:d] = (dout_a * sb).astype(dglu_ref.dtype)
        dglu_ref[:, d:2 * d] = (dout_a * glu_a * sb * (1.0 - sb)).astype(dglu_ref.dtype)
        dgate_ref[:, 0:d] = (dm * out_a * ga * (1.0 - ga)).astype(dgate_ref.dtype)
        dgate_ref[:, d:2 * d] = (dm * ob_ref[...].astype(F32) * gb * (1.0 - gb)).astype(dgate_ref.dtype)
        dob_ref[...] = (dm * gb).astype(dob_ref.dtype)

    wide = pl.BlockSpec((tr, 2 * d), lambda i: (i, 0))
    row = pl.BlockSpec((tr, d), lambda i: (i, 0))
    return pl.pallas_call(
        body, name=name, grid=(r // tr,),
        in_specs=[row, pl.BlockSpec((d, d), lambda i: (0, 0)), wide, wide, row], out_specs=(wide, wide, row),
        out_shape=(jax.ShapeDtypeStruct((r, 2 * d), BF16), jax.ShapeDtypeStruct((r, 2 * d), BF16),
                   jax.ShapeDtypeStruct((r, d), BF16)),
        compiler_params=_cparams(("parallel",)),
    )(dh, w_mix, glu, gates, out_b)


def _ssm_mats(lam_re, lam_im, log_dt, b_re, b_im, c_re, c_im, nc):
    hp = lax.Precision.HIGHEST
    t = SSM_CHUNK
    nq = SSM_GROUPS // 8
    lam = lax.complex(lam_re, lam_im)
    z = lam * jnp.exp(log_dt)[:, None]
    ks = jnp.arange(t + 1, dtype=F32)
    apow = jnp.exp(ks[:, None, None] * z[None])
    bbar = ((apow[1] - 1.0) / lam)[..., None] * lax.complex(b_re, b_im)
    c = lax.complex(c_re, c_im)

    ca = c[None] * apow[:, :, None, :]
    kmat = jnp.einsum("kgnp,gpm->kgnm", ca, bbar, precision=hp).real
    ii = np.arange(t)
    lag = ii[None, :] - ii[:, None]
    kt = kmat[np.clip(lag, 0, t)] * jnp.asarray(lag >= 0, F32)[:, :, None, None, None]
    kt = kt.reshape(t, t, nq, 8, SSM_GROUP, SSM_GROUP)
    m_c = kt.transpose(2, 0, 3, 5, 1, 4).reshape(nq, 1024, LANE)

    arev = jnp.exp((float(t - 1) - ks[:t])[:, None, None] * z[None])
    w = arev[:, :, :, None] * bbar[None]
    wr = jnp.stack([w.real, w.imag]).reshape(2, t, nq, 8, SSM_STATE, SSM_GROUP)
    bw_c = wr.transpose(2, 1, 3, 5, 0, 4).reshape(nq, 1024, LANE)

    ca1 = ca[1:]
    cr = jnp.stack([ca1.real, -ca1.imag]).reshape(2, t, nq, 8, SSM_GROUP, SSM_STATE)
    cm_c = cr.transpose(2, 0, 3, 5, 1, 4).reshape(nq, 1024, LANE)

    def tiles(v):
        vq = jnp.concatenate([v.real.reshape(nq, 512), v.imag.reshape(nq, 512)], axis=1)
        return jnp.broadcast_to(vq.reshape(nq, 8, 1, LANE), (nq, 8, 8, LANE))

    return m_c, bw_c, cm_c, tiles(apow[t]), tiles(jnp.exp(float(nc) * z))


_BD_M = (LANE, SSM_GROUP)
_BD_BW = (LANE, SSM_STATE)
_BD_CM = (512, SSM_GROUP)


def _bd_perm(cn):
    rr = lax.broadcasted_iota(jnp.int32, (1024, 1024), 0)
    cc = lax.broadcasted_iota(jnp.int32, (1024, 1024), 1)
    sh = cn.bit_length() - 1
    src = ((rr >> 7) << sh) + (((rr & (LANE - 1)) >> sh) << (3 + sh)) + (rr & (cn - 1))
    return jnp.where(src == cc, 1.0, 0.0).astype(BF16)


def _bd_rowgroup(span):
    r = lax.broadcasted_iota(jnp.int32, (1024, LANE), 0)
    return (r & (span - 1)) >> ((span // 8).bit_length() - 1)


def _bd_expand(name, kind, compact):
    span, cn = kind
    nq = compact.shape[0]

    def body(c_ref, o_ref, perm_scr):
        @pl.when(pl.program_id(0) == 0)
        def _():
            perm_scr[...] = _bd_perm(cn)

        x = c_ref[...]
        grp = _bd_rowgroup(span)
        xcat = jnp.concatenate([jnp.where(grp == h, x, 0.0) for h in range(8)], axis=1)
        o_ref[...] = _bdot(xcat, perm_scr[...], _DIMS["nn"]).astype(o_ref.dtype)

    return pl.pallas_call(
        body, name=name, grid=(nq,), in_specs=[pl.BlockSpec((None, 1024, LANE), lambda q: (q, 0, 0))],
        out_specs=pl.BlockSpec((None, 1024, 1024), lambda q: (q, 0, 0)),
        out_shape=jax.ShapeDtypeStruct((nq, 1024, 1024), BF16),
        scratch_shapes=[pltpu.VMEM((1024, 1024), BF16)],
        compiler_params=_cparams(("arbitrary",)),
    )(compact)


def _bd_reduce(name, kind, dbig):
    span, cn = kind
    nq = dbig.shape[0]

    def body(g_ref, o_ref, perm_scr):
        @pl.when(pl.program_id(0) == 0)
        def _():
            perm_scr[...] = _bd_perm(cn)

        back = _bdot(g_ref[...], perm_scr[...], _DIMS["nt"])
        grp = _bd_rowgroup(span)
        out = jnp.zeros((1024, LANE), F32)
        for h in range(8):
            out = jnp.where(grp == h, back[:, h * LANE:(h + 1) * LANE], out)
        o_ref[...] = out

    return pl.pallas_call(
        body, name=name, grid=(nq,), in_specs=[pl.BlockSpec((None, 1024, 1024), lambda q: (q, 0, 0))],
        out_specs=pl.BlockSpec((None, 1024, LANE), lambda q: (q, 0, 0)),
        out_shape=jax.ShapeDtypeStruct((nq, 1024, LANE), F32),
        scratch_shapes=[pltpu.VMEM((1024, 1024), BF16)],
        compiler_params=_cparams(("arbitrary",)),
    )(dbig)


def _x_tile_specs(nc, nq):
    return [pl.BlockSpec((nc, LANE), lambda q, t, i=i: (0, i * nq + q)) for i in range(SSM_CHUNK)]


def _cat_tiles(refs):
    return jnp.concatenate([r[...] for r in refs], axis=1)


def _ssm_w(name, x8, bw):
    nc = x8.shape[0]
    nq = bw.shape[0]

    def body(*refs):
        xq = _cat_tiles(refs[:8])
        refs[9][...] = _bdot(xq, refs[8][...], _DIMS["nn"])

    return pl.pallas_call(
        body, name=name, grid=(nq, 8),
        in_specs=_x_tile_specs(nc, nq) + [pl.BlockSpec((None, 1024, LANE), lambda q, t: (q, 0, t))],
        out_specs=pl.BlockSpec((None, None, nc, LANE), lambda q, t: (q, t, 0, 0)),
        out_shape=jax.ShapeDtypeStruct((nq, 8, nc, LANE), F32),
        compiler_params=_cparams(("parallel", "arbitrary")),
    )(*([x8] * 8), bw)


def _ssm_scan(name, w4, a_t, aseg_t, *, reverse, sprev4=None):
    nq, _, nc, _ = w4.shape
    ns = nc // 8
    with_da = sprev4 is not None

    def body(*refs):
        w_ref, a_ref, aseg_ref = refs[:3]
        s_ref = refs[3] if with_da else None
        o_ref = refs[4] if with_da else refs[3]
        da_ref = refs[5] if with_da else None
        sgn = -1.0 if reverse else 1.0
        ar = [a_ref[j] for j in range(4)]
        ai = [sgn * a_ref[j + 4] for j in range(4)]
        gr = [aseg_ref[j] for j in range(4)]
        gi = [sgn * aseg_ref[j + 4] for j in range(4)]
        zero = tuple(jnp.zeros((8, LANE), F32) for _ in range(8))

        def rows(tt):
            return pl.ds((ns - 1 - tt) if reverse else tt, 8, stride=ns)

        def step(carry, w):
            new_r = [ar[j] * carry[j] - ai[j] * carry[j + 4] + w[j] for j in range(4)]
            new_i = [ar[j] * carry[j + 4] + ai[j] * carry[j] + w[j + 4] for j in range(4)]
            return tuple(new_r + new_i)

        def pass1(tt, carry):
            return step(carry, [w_ref[j, rows(tt), :] for j in range(8)])

        ends = lax.fori_loop(0, ns, pass1, zero)
        sub = lax.broadcasted_iota(jnp.int32, (8, LANE), 0)
        init = list(zero)
        order = range(7, 0, -1) if reverse else range(0, 7)
        for s in order:
            nxt = s - 1 if reverse else s + 1
            cand_r = [gr[j] * init[j] - gi[j] * init[j + 4] + ends[j] for j in range(4)]
            cand_i = [gr[j] * init[j + 4] + gi[j] * init[j] + ends[j + 4] for j in range(4)]
            cand = cand_r + cand_i
            shift = 7 if reverse else 1
            init = [jnp.where(sub == nxt, pltpu.roll(cand[j], shift, axis=0), init[j]) for j in range(8)]

        def pass2(tt, state):
            carry, acc = state
            r = rows(tt)
            for j in range(8):
                o_ref[j, r, :] = carry[j]
            if with_da:
                sp = [s_ref[j, r, :] for j in range(8)]
                acc_r = [acc[j] + carry[j] * sp[j] + carry[j + 4] * sp[j + 4] for j in range(4)]
                acc_i = [acc[j + 4] + carry[j + 4] * sp[j] - carry[j] * sp[j + 4] for j in range(4)]
                acc = tuple(acc_r + acc_i)
            return step(carry, [w_ref[j, r, :] for j in range(8)]), acc

        _, acc = lax.fori_loop(0, ns, pass2, (tuple(init), zero))
        if with_da:
            for j in range(8):
                da_ref[j] = acc[j]

    big = pl.BlockSpec((None, 8, nc, LANE), lambda q: (q, 0, 0, 0))
    small = pl.BlockSpec((None, 8, 8, LANE), lambda q: (q, 0, 0, 0))
    in_specs = [big, small, small] + ([big] if with_da else [])
    ops = [w4, a_t, aseg_t] + ([sprev4] if with_da else [])
    out_specs = (big, small) if with_da else big
    big_s = jax.ShapeDtypeStruct((nq, 8, nc, LANE), F32)
    out_shape = (big_s, jax.ShapeDtypeStruct((nq, 8, 8, LANE), F32)) if with_da else big_s
    return pl.pallas_call(
        body, name=name, grid=(nq,), in_specs=in_specs, out_specs=out_specs, out_shape=out_shape,
        compiler_params=_cparams(("parallel",)),
    )(*ops)


def _ssm_y(name, x8, sprev4, m_mat, cm_mat):
    nc = x8.shape[0]
    nq = m_mat.shape[0]

    def body(*refs):
        xq = _cat_tiles(refs[:8])
        s_ref, m_ref, cm_ref, o_ref = refs[8:12]
        sq = jnp.concatenate([s_ref[t] for t in range(8)], axis=1)
        o_ref[...] = _bdot(xq, m_ref[...], _DIMS["nn"]) + _bdot(sq, cm_ref[...], _DIMS["nn"])

    col = pl.BlockSpec((None, 1024, LANE), lambda q, j: (q, 0, j))
    return pl.pallas_call(
        body, name=name, grid=(nq, 8),
        in_specs=_x_tile_specs(nc, nq) + [pl.BlockSpec((None, 8, nc, LANE), lambda q, j: (q, 0, 0, 0)), col, col],
        out_specs=pl.BlockSpec((nc, LANE), lambda q, j: (0, j * nq + q)),
        out_shape=jax.ShapeDtypeStruct((nc, 8 * SSM_WIDTH), F32),
        compiler_params=_cparams(("parallel", "arbitrary")),
    )(*([x8] * 8), sprev4, m_mat, cm_mat)


def _ssm_ds(name, dz8, sprev4, cm_mat):
    nc = dz8.shape[0]
    nq = cm_mat.shape[0]

    def body(*refs):
        dyq = _cat_tiles(refs[:8]).astype(BF16)
        s_ref, cm_ref, ds_ref, dcm_ref = refs[8:12]
        ds_ref[...] = _bdot(dyq, cm_ref[...], _DIMS["nt"])
        dcm_ref[...] = _bdot(s_ref[...], dyq, _DIMS["tn"])

    tile = pl.BlockSpec((None, None, nc, LANE), lambda q, t: (q, t, 0, 0))
    rowblk = pl.BlockSpec((None, LANE, 1024), lambda q, t: (q, t, 0))
    return pl.pallas_call(
        body, name=name, grid=(nq, 8),
        in_specs=_x_tile_specs(nc, nq) + [tile, rowblk],
        out_specs=(tile, rowblk),
        out_shape=(jax.ShapeDtypeStruct((nq, 8, nc, LANE), F32), jax.ShapeDtypeStruct((nq, 1024, 1024), F32)),
        compiler_params=_cparams(("parallel", "arbitrary")),
    )(*([dz8] * 8), sprev4, cm_mat)


def _ssm_dx(name, dz8, g4, x8, m_mat, bw_mat, d8):
    nc = dz8.shape[0]
    nq = m_mat.shape[0]

    def body(*refs):
        dyq = _cat_tiles(refs[:8]).astype(BF16)
        g_ref, x_ref, m_ref, bw_ref, d_ref, dzi_ref, dx_ref, dm_ref, dbw_ref = refs[8:17]
        gq = jnp.concatenate([g_ref[t] for t in range(8)], axis=1).astype(BF16)
        dx = _bdot(dyq, m_ref[...], _DIMS["nt"]) + _bdot(gq, bw_ref[...], _DIMS["nt"])
        dx_ref[...] = (dx + d_ref[...] * dzi_ref[...]).astype(dx_ref.dtype)
        xi = x_ref[...]
        dm_ref[...] = _bdot(xi, dyq, _DIMS["tn"])
        dbw_ref[...] = _bdot(xi, gq, _DIMS["tn"])

    xtile = pl.BlockSpec((nc, LANE), lambda q, i: (0, i * nq + q))
    rowblk = pl.BlockSpec((None, LANE, 1024), lambda q, i: (q, i, 0))
    return pl.pallas_call(
        body, name=name, grid=(nq, 8),
        in_specs=_x_tile_specs(nc, nq) + [pl.BlockSpec((None, 8, nc, LANE), lambda q, i: (q, 0, 0, 0)), xtile, rowblk, rowblk,
                                          pl.BlockSpec((1, LANE), lambda q, i: (0, q)), xtile],
        out_specs=(xtile, rowblk, rowblk),
        out_shape=(jax.ShapeDtypeStruct((nc, 8 * SSM_WIDTH), BF16), jax.ShapeDtypeStruct((nq, 1024, 1024), F32),
                   jax.ShapeDtypeStruct((nq, 1024, 1024), F32)),
        compiler_params=_cparams(("parallel", "arbitrary")),
    )(*([dz8] * 8), g4, x8, m_mat, bw_mat, d8, dz8)


CUM_BLK = 256


def _split3(x):
    hi = x.astype(BF16)
    r1 = x - hi.astype(F32)
    mid = r1.astype(BF16)
    lo = (r1 - mid.astype(F32)).astype(BF16)
    return hi, mid, lo


def _tri_dot(x, tri):
    hi, mid, lo = _split3(x)
    d = _DIMS["nn"]
    return _bdot(hi, tri, d) + _bdot(mid, tri, d) + _bdot(lo, tri, d)


def _tri(n, lower):
    r = lax.broadcasted_iota(jnp.int32, (n, n), 0)
    c = lax.broadcasted_iota(jnp.int32, (n, n), 1)
    return jnp.where((r >= c) if lower else (r <= c), 1.0, 0.0).astype(BF16)


def _fox_cum(name, fproj, bcol):
    seq = fproj.shape[0]
    blk = min(CUM_BLK, seq)

    def body(f_ref, b_ref, o_ref, carry_ref):
        i = pl.program_id(0)

        @pl.when(i == 0)
        def _():
            carry_ref[...] = jnp.zeros_like(carry_ref)

        z = f_ref[...].T + b_ref[...]
        logf = jnp.minimum(z, 0.0) - jnp.log(1.0 + jnp.exp(-jnp.abs(z)))
        carry = carry_ref[...]
        cum = _tri_dot(logf, _tri(blk, lower=False)) + jnp.tile(carry, (1, blk // LANE))
        o_ref[...] = cum[0:8, :]
        carry_ref[...] = carry + jnp.sum(logf, axis=1, keepdims=True)

    return pl.pallas_call(
        body, name=name, grid=(seq // blk,),
        in_specs=[pl.BlockSpec((blk, LANE), lambda i: (i, 0)), pl.BlockSpec((LANE, 1), lambda i: (0, 0))],
        out_specs=pl.BlockSpec((8, blk), lambda i: (0, i)),
        out_shape=jax.ShapeDtypeStruct((8, seq), F32),
        scratch_shapes=[pltpu.VMEM((LANE, LANE), F32)],
        compiler_params=_cparams(("arbitrary",)),
    )(fproj, bcol)


def _fox_cum_bwd(name, dcs, fproj, bcol):
    seq = fproj.shape[0]
    blk = min(CUM_BLK, seq)
    n = seq // blk

    def body(dc_ref, f_ref, b_ref, df_ref, db_ref, carry_ref, acc_ref):
        i = pl.program_id(0)

        @pl.when(i == 0)
        def _():
            carry_ref[...] = jnp.zeros_like(carry_ref)
            acc_ref[...] = jnp.zeros_like(acc_ref)

        r = lax.broadcasted_iota(jnp.int32, (LANE, FOX_WIDTH), 0)
        c = lax.broadcasted_iota(jnp.int32, (LANE, FOX_WIDTH), 1)
        want = (r >> 1) * LANE + jnp.where((r & 1) == 0, FOX_HEAD_DIM, 0)
        sel = jnp.where(jnp.logical_and(r < FOX_HEADS, c == want), 1.0, 0.0).astype(BF16)
        hi, mid, lo = _split3(dc_ref[...])
        nt = _DIMS["nt"]
        dc = _bdot(sel, hi, nt) + _bdot(sel, mid, nt) + _bdot(sel, lo, nt)
        carry = carry_ref[...]
        dlogf = _tri_dot(dc, _tri(blk, lower=True)) + jnp.tile(carry, (1, blk // LANE))
        carry_ref[...] = carry + jnp.sum(dc, axis=1, keepdims=True)
        z = f_ref[...].T + b_ref[...]
        dft = dlogf / (1.0 + jnp.exp(z))
        df_ref[...] = dft.T.astype(df_ref.dtype)
        acc_ref[...] += jnp.sum(dft, axis=1, keepdims=True)

        @pl.when(i == n - 1)
        def _():
            db_ref[...] = acc_ref[...]

    return pl.pallas_call(
        body, name=name, grid=(n,),
        in_specs=[pl.BlockSpec((blk, FOX_WIDTH), lambda i: (n - 1 - i, 0)), pl.BlockSpec((blk, LANE), lambda i: (n - 1 - i, 0)),
                  pl.BlockSpec((LANE, 1), lambda i: (0, 0))],
        out_specs=(pl.BlockSpec((blk, LANE), lambda i: (n - 1 - i, 0)), pl.BlockSpec((LANE, LANE), lambda i: (0, 0))),
        out_shape=(jax.ShapeDtypeStruct((seq, LANE), BF16), jax.ShapeDtypeStruct((LANE, LANE), F32)),
        scratch_shapes=[pltpu.VMEM((LANE, LANE), F32), pltpu.VMEM((LANE, LANE), F32)],
        compiler_params=_cparams(("arbitrary",)),
    )(dcs, fproj, bcol)


FOX_BLK = 512
FOX_SCALE = FOX_HEAD_DIM ** -0.5


def _fox_head_mask(shape, hh):
    lane = lax.broadcasted_iota(jnp.int32, shape, 1)
    return (lane < FOX_HEAD_DIM) if hh == 0 else (lane >= FOX_HEAD_DIM)


def _fox_bias(cum_ref, hh, q0, k0, blk):
    c0 = jnp.max(cum_ref[hh:hh + 1, pl.ds(q0, LANE)], axis=1, keepdims=True)
    return c0 - cum_ref[hh:hh + 1, pl.ds(k0, blk)]


def _fox_fwd(name, qkv, cum_t):
    seq = qkv.shape[0]
    blk = min(FOX_BLK, seq)
    nb = seq // blk
    npair = FOX_HEADS // 2

    def body(q_ref, k_ref, v_ref, cum_ref, o_ref, lse_ref):
        iq = pl.program_id(1)
        q0 = pl.multiple_of(iq * blk, blk)
        qv = q_ref[...]
        row = lax.broadcasted_iota(jnp.int32, (blk, blk), 0)
        col = lax.broadcasted_iota(jnp.int32, (blk, blk), 1)
        qhs = [jnp.where(_fox_head_mask(qv.shape, hh), qv, jnp.zeros_like(qv)) * FOX_SCALE for hh in range(2)]

        def block(kb, states, masked):
            k0 = pl.multiple_of(kb * blk, blk)
            kv = k_ref[pl.ds(k0, blk), :]
            vv = v_ref[pl.ds(k0, blk), :]
            new = []
            for hh in range(2):
                m, acc = states[hh]
                s = _bdot(qhs[hh], kv, _DIMS["nt"]) + _fox_bias(cum_ref, hh, q0, k0, blk)
                if masked:
                    s = jnp.where(row >= col, s, -jnp.inf)
                m_new = jnp.maximum(m, jnp.max(s, axis=1, keepdims=True))
                p = jnp.exp(s - m_new)
                vh = jnp.where(_fox_head_mask(vv.shape, hh), vv, jnp.ones_like(vv))
                acc = jnp.exp(m - m_new) * acc + _bdot(p, vh, _DIMS["nn"])
                new.append((m_new, acc))
            return tuple(new)

        init = (jnp.full((blk, 1), -jnp.inf, F32), jnp.zeros((blk, LANE), F32))
        states = lax.fori_loop(0, iq, lambda kb, st: block(kb, st, False), (init, init))
        states = block(iq, states, True)
        outs = []
        for hh in range(2):
            m, acc = states[hh]
            other = pltpu.roll(acc, FOX_HEAD_DIM, axis=1)
            outs.append(acc / other)
            lse_ref[hh] = m + jnp.log(jnp.where(_fox_head_mask(acc.shape, hh), other, acc))
        o_ref[...] = jnp.where(_fox_head_mask(outs[0].shape, 0), outs[0], outs[1]).astype(o_ref.dtype)

    return pl.pallas_call(
        body, name=name, grid=(npair, nb),
        in_specs=[pl.BlockSpec((blk, LANE), lambda p, i: (i, p)),
                  pl.BlockSpec((seq, LANE), lambda p, i: (0, npair + p)),
                  pl.BlockSpec((seq, LANE), lambda p, i: (0, 2 * npair + p)),
                  pl.BlockSpec((None, 2, seq), lambda p, i: (p, 0, 0))],
        out_specs=(pl.BlockSpec((blk, LANE), lambda p, i: (i, p)),
                   pl.BlockSpec((2, blk, LANE), lambda p, i: (p, i, 0))),
        out_shape=(jax.ShapeDtypeStruct((seq, FOX_WIDTH), BF16), jax.ShapeDtypeStruct((FOX_HEADS, seq, LANE), F32)),
        compiler_params=_cparams(("parallel", "arbitrary")),
    )(qkv, qkv, qkv, cum_t)


def _fox_bwd(name, qkv, cum_t, att, datt, lse):
    seq = qkv.shape[0]
    blk = min(FOX_BLK, seq)
    nb = seq // blk
    npair = FOX_HEADS // 2

    def body(q_ref, k_ref, v_ref, cum_ref, o_ref, do_ref, lse_ref, dq_ref, dk_ref, dv_ref, dcs_ref):
        iq = pl.program_id(1)
        q0 = pl.multiple_of(iq * blk, blk)

        @pl.when(iq == 0)
        def _():
            dk_ref[...] = jnp.zeros_like(dk_ref)
            dv_ref[...] = jnp.zeros_like(dv_ref)
            dcs_ref[...] = jnp.zeros_like(dcs_ref)

        qv = q_ref[...]
        dov = do_ref[...].astype(F32)
        ov = o_ref[...].astype(F32)
        row = lax.broadcasted_iota(jnp.int32, (blk, blk), 0)
        col = lax.broadcasted_iota(jnp.int32, (blk, blk), 1)
        low = _fox_head_mask((blk, LANE), 0)
        qhs, qones, dohbs, deltas, lses = [], [], [], [], []
        for hh in range(2):
            hm = _fox_head_mask(qv.shape, hh)
            qh = jnp.where(hm, qv, jnp.zeros_like(qv)) * FOX_SCALE
            qhs.append(qh)
            qones.append(jnp.where(hm, qh, jnp.ones_like(qh)))
            doh = jnp.where(hm, dov, 0.0)
            dohbs.append(doh.astype(BF16))
            deltas.append(jnp.sum(doh * ov, axis=1, keepdims=True))
            lses.append(jnp.tile(lse_ref[hh], (1, blk // LANE)))

        def block(kb, dqs, masked):
            k0 = pl.multiple_of(kb * blk, blk)
            kv = k_ref[pl.ds(k0, blk), :]
            vv = v_ref[pl.ds(k0, blk), :]
            new, dks, dvs = [], [], []
            for hh in range(2):
                s = _bdot(qhs[hh], kv, _DIMS["nt"]) + _fox_bias(cum_ref, hh, q0, k0, blk)
                p = jnp.exp(s - lses[hh])
                if masked:
                    p = jnp.where(row >= col, p, 0.0)
                dp = _bdot(dohbs[hh], vv, _DIMS["nt"])
                dsb = (p * (dp - deltas[hh])).astype(BF16)
                dks.append(_bdot(dsb, qones[hh], _DIMS["tn"]))
                dvs.append(_bdot(p, dohbs[hh], _DIMS["tn"]))
                kones = jnp.where(_fox_head_mask(kv.shape, hh), kv, jnp.ones_like(kv))
                new.append(dqs[hh] + _bdot(dsb, kones, _DIMS["nn"]))
            dk_ref[pl.ds(k0, blk), :] += jnp.where(low, dks[0], dks[1])
            dv_ref[pl.ds(k0, blk), :] += dvs[0] + dvs[1]
            dcs_ref[pl.ds(k0, blk), :] -= jnp.where(low, dks[1], dks[0])
            return tuple(new)

        init = jnp.zeros((blk, LANE), F32)
        dqs = lax.fori_loop(0, iq, lambda kb, a: block(kb, a, False), (init, init))
        dqs = block(iq, dqs, True)
        dcs_ref[pl.ds(q0, blk), :] += jnp.where(low, dqs[1], dqs[0])
        dq_ref[...] = (jnp.where(low, dqs[0], dqs[1]) * FOX_SCALE).astype(dq_ref.dtype)

    qblk = pl.BlockSpec((blk, LANE), lambda p, i: (i, p))
    full = pl.BlockSpec((seq, LANE), lambda p, i: (0, p))
    return pl.pallas_call(
        body, name=name, grid=(npair, nb),
        in_specs=[qblk,
                  pl.BlockSpec((seq, LANE), lambda p, i: (0, npair + p)),
                  pl.BlockSpec((seq, LANE), lambda p, i: (0, 2 * npair + p)),
                  pl.BlockSpec((None, 2, seq), lambda p, i: (p, 0, 0)),
                  qblk, qblk,
                  pl.BlockSpec((2, blk, LANE), lambda p, i: (p, i, 0))],
        out_specs=(qblk, full, full, full),
        out_shape=(jax.ShapeDtypeStruct((seq, FOX_WIDTH), BF16), jax.ShapeDtypeStruct((seq, FOX_WIDTH), F32),
                   jax.ShapeDtypeStruct((seq, FOX_WIDTH), F32), jax.ShapeDtypeStruct((seq, FOX_WIDTH), F32)),
        compiler_params=_cparams(("arbitrary", "arbitrary")),
    )(qkv, qkv, qkv, cum_t, att, datt, lse)


MEM_SCALE = MEM_HEAD_DIM ** -0.5


def _mem_probs(qh, kh):
    s = _bdot(qh, kh, _DIMS["nt"]) * MEM_SCALE
    p = jnp.exp(s - jnp.max(s, axis=1, keepdims=True))
    return p / jnp.sum(p, axis=1, keepdims=True)


def _mem_fwd(name, q2, kv, *, tr=512):
    seq = q2.shape[0]
    mlen = kv.shape[0]
    tr = min(tr, seq)

    def body(q_ref, kv_ref, o_ref):
        for h in range(MEM_HEADS):
            sl = slice(h * MEM_HEAD_DIM, (h + 1) * MEM_HEAD_DIM)
            sv = slice(MEM_WIDTH + h * MEM_HEAD_DIM, MEM_WIDTH + (h + 1) * MEM_HEAD_DIM)
            p = _mem_probs(q_ref[:, sl], kv_ref[:, sl])
            o_ref[:, sl] = _bdot(p, kv_ref[:, sv], _DIMS["nn"]).astype(o_ref.dtype)

    return pl.pallas_call(
        body, name=name, grid=(seq // tr,),
        in_specs=[pl.BlockSpec((tr, MEM_WIDTH), lambda i: (i, 0)), pl.BlockSpec((mlen, 2 * MEM_WIDTH), lambda i: (0, 0))],
        out_specs=pl.BlockSpec((tr, MEM_WIDTH), lambda i: (i, 0)),
        out_shape=jax.ShapeDtypeStruct((seq, MEM_WIDTH), BF16),
        compiler_params=_cparams(("parallel",)),
    )(q2, kv)


def _mem_bwd(name, q2, kv, do2, *, tr=512):
    seq = q2.shape[0]
    mlen = kv.shape[0]
    tr = min(tr, seq)

    def body(q_ref, kv_ref, do_ref, dq_ref, dkv_ref):
        i = pl.program_id(0)

        @pl.when(i == 0)
        def _():
            dkv_ref[...] = jnp.zeros_like(dkv_ref)

        for h in range(MEM_HEADS):
            sl = slice(h * MEM_HEAD_DIM, (h + 1) * MEM_HEAD_DIM)
            sv = slice(MEM_WIDTH + h * MEM_HEAD_DIM, MEM_WIDTH + (h + 1) * MEM_HEAD_DIM)
            qh = q_ref[:, sl]
            kh = kv_ref[:, sl]
            doh = do_ref[:, sl].astype(BF16)
            p = _mem_probs(qh, kh)
            dp = _bdot(doh, kv_ref[:, sv], _DIMS["nt"])
            ds = (p * (dp - jnp.sum(p * dp, axis=1, keepdims=True)) * MEM_SCALE).astype(BF16)
            dq_ref[:, sl] = _bdot(ds, kh, _DIMS["nn"]).astype(dq_ref.dtype)
            dkv_ref[:, sl] += _bdot(ds, qh, _DIMS["tn"])
            dkv_ref[:, sv] += _bdot(p, doh, _DIMS["tn"])

    row = pl.BlockSpec((tr, MEM_WIDTH), lambda i: (i, 0))
    kvs = pl.BlockSpec((mlen, 2 * MEM_WIDTH), lambda i: (0, 0))
    return pl.pallas_call(
        body, name=name, grid=(seq // tr,), in_specs=[row, kvs, row], out_specs=(row, kvs),
        out_shape=(jax.ShapeDtypeStruct((seq, MEM_WIDTH), BF16), jax.ShapeDtypeStruct((mlen, 2 * MEM_WIDTH), F32)),
        compiler_params=_cparams(("arbitrary",)),
    )(q2, kv, do2)


_HBM = pl.BlockSpec(memory_space=pl.ANY)
_HBM_ONLY = pl.BlockSpec(memory_space=pltpu.HBM)
_MESH = pl.DeviceIdType.MESH


def _mesh_place():
    x, y, c = lax.axis_index("x"), lax.axis_index("y"), lax.axis_index("c")
    other_chips = [(1 - x, y), (x, 1 - y), (1 - x, 1 - y)]
    return x, y, c, other_chips


def _gather_all(name, arrays):
    n = len(arrays)

    def body(*refs):
        ins, outs = refs[:n], refs[n:2 * n]
        send_sems, recv_sems, local_sems = refs[2 * n:]
        x, y, c, chips = _mesh_place()
        me, sibling = (x, y, c), (x, y, 1 - c)

        def slot(a, place):
            px, py, pc = place
            return outs[a].at[4 * px + 2 * py + pc]

        def copy(a, k, block, to, src=None):
            return pltpu.make_async_remote_copy(
                src_ref=slot(a, block) if src is None else src, dst_ref=slot(a, block),
                send_sem=send_sems.at[a, k], recv_sem=recv_sems.at[a, k], device_id=to, device_id_type=_MESH)

        mine = [pltpu.make_async_copy(ins[a], slot(a, me), local_sems.at[a]) for a in range(n)]
        for cp in mine:
            cp.start()
        first = []
        for a in range(n):
            first.append(copy(a, 0, me, sibling, src=ins[a]))
            first += [copy(a, 1 + j, me, (*chip, c), src=ins[a]) for j, chip in enumerate(chips)]
        for cp in first:
            cp.start()
        passed = []
        for j, chip in enumerate(chips):
            for a in range(n):
                copy(a, 1 + j, (*chip, c), me).wait_recv()
                fwd = copy(a, 4 + j, (*chip, c), sibling)
                fwd.start()
                passed.append(fwd)
        for a in range(n):
            copy(a, 0, sibling, me).wait_recv()
            for j, chip in enumerate(chips):
                copy(a, 4 + j, (*chip, 1 - c), me).wait_recv()
        for cp in first + passed:
            cp.wait_send()
        for cp in mine:
            cp.wait()

    out_shape = tuple(jax.ShapeDtypeStruct((N_DEV,) + arr.shape, arr.dtype) for arr in arrays)
    return pl.pallas_call(
        body, name=name, in_specs=[_HBM] * n, out_specs=tuple([_HBM] * n), out_shape=out_shape,
        scratch_shapes=[pltpu.SemaphoreType.DMA((n, N_DEV - 1)), pltpu.SemaphoreType.DMA((n, N_DEV - 1)),
                        pltpu.SemaphoreType.DMA((n,))],
    )(*arrays)


_SEM = pl.BlockSpec(memory_space=pltpu.SEMAPHORE)
_DATAFLOW = pltpu.SideEffectType.DATAFLOW_SIDE_EFFECTING


def _device_index():
    return (4 * lax.axis_index("x") + 2 * lax.axis_index("y") + lax.axis_index("c")).astype(jnp.int32).reshape(1)


def _place_own(name, pieces, *, stacked_src, after=None):
    n = len(pieces)
    n_in = n + (after is not None)

    def body(me_ref, *refs):
        for a in range(n):
            refs[n_in + a][...] = refs[a][...]

    def spec(shape):
        return pl.BlockSpec((None,) + tuple(shape), lambda i, me_ref: (me_ref[0],) + (0,) * len(shape))

    shapes = [p.shape[1:] if stacked_src else p.shape for p in pieces]
    if stacked_src:
        in_specs = [spec(s) for s in shapes]
    else:
        in_specs = [pl.BlockSpec(tuple(s), lambda i, me_ref, nd=len(s): (0,) * nd) for s in shapes]
    operands = list(pieces)
    if after is not None:
        in_specs.append(_HBM)
        operands.append(after)
    return pl.pallas_call(
        body, name=name,
        grid_spec=pltpu.PrefetchScalarGridSpec(num_scalar_prefetch=1, grid=(1,), in_specs=in_specs,
                                               out_specs=tuple(spec(s) for s in shapes)),
        out_shape=tuple(jax.ShapeDtypeStruct((N_DEV,) + tuple(s), p.dtype) for s, p in zip(shapes, pieces)),
        compiler_params=_cparams(("arbitrary",)),
    )(_device_index(), *operands)


def _peer_places():
    x, y, c = lax.axis_index("x"), lax.axis_index("y"), lax.axis_index("c")
    peers = []
    for k in range(N_DEV - 1):
        flip = k + 1
        px = 1 - x if flip & 4 else x
        py = 1 - y if flip & 2 else y
        pc = 1 - c if flip & 1 else c
        peers.append((px, py, pc, 4 * px + 2 * py + pc))
    return 4 * x + 2 * y + c, peers


def _direct_copy(srcs, lands, send_sems, recv_sems, a, k, me, peer, scatter):
    px, py, pc, pidx = peer
    return pltpu.make_async_remote_copy(
        src_ref=srcs[a].at[pidx] if scatter else srcs[a], dst_ref=lands[a].at[me],
        send_sem=send_sems.at[a * (N_DEV - 1) + k], recv_sem=recv_sems.at[a * (N_DEV - 1) + k],
        device_id=(px, py, pc), device_id_type=_MESH)


def _send_start(name, srcs, lands, *, scatter):
    n = len(srcs)

    def body(*refs):
        src_refs, land_refs = refs[:n], refs[n:2 * n]
        send_sems, recv_sems = refs[2 * n], refs[2 * n + 1]
        token = refs[-1]
        me, peers = _peer_places()
        for k, peer in enumerate(peers):
            for a in range(n):
                _direct_copy(src_refs, land_refs, send_sems, recv_sems, a, k, me, peer, scatter).start()
        token[...] = jnp.zeros_like(token)

    hbm_shapes = [pltpu.HBM(t.shape, t.dtype) for t in list(srcs) + list(lands)]
    outs = pl.pallas_call(
        body, name=name,
        out_shape=(pltpu.SemaphoreType.DMA((n * (N_DEV - 1),)), pltpu.SemaphoreType.DMA((n * (N_DEV - 1),)), *hbm_shapes,
                   jax.ShapeDtypeStruct((8, LANE), F32)),
        in_specs=[_HBM_ONLY] * (2 * n),
        out_specs=(_SEM, _SEM, *([_HBM_ONLY] * (2 * n)), pl.BlockSpec(memory_space=pltpu.VMEM)),
        input_output_aliases={i: 2 + i for i in range(2 * n)},
        compiler_params=pltpu.CompilerParams(has_side_effects=_DATAFLOW),
    )(*[pltpu.with_memory_space_constraint(t, pltpu.HBM) for t in list(srcs) + list(lands)])
    return outs[0], outs[1], outs[2:2 + n], outs[2 + n:2 + 2 * n], outs[-1]


def _send_wait(name, send_sems, recv_sems, srcs, lands, after, *, scatter):
    n = len(srcs)
    afters = list(after) if isinstance(after, (tuple, list)) else [after]

    def body(*refs):
        src_refs, land_refs = refs[:n], refs[n:2 * n]
        send_sems, recv_sems = refs[2 * n], refs[2 * n + 1]
        me, peers = _peer_places()
        for k, peer in enumerate(peers):
            for a in range(n):
                cp = _direct_copy(src_refs, land_refs, send_sems, recv_sems, a, k, me, peer, scatter)
                cp.wait_send()
                cp.wait_recv()

    hbm_shapes = [pltpu.HBM(t.shape, t.dtype) for t in list(srcs) + list(lands)]
    outs = pl.pallas_call(
        body, name=name, out_shape=tuple(hbm_shapes),
        in_specs=[_HBM_ONLY] * (2 * n) + [_SEM, _SEM] + [_HBM] * len(afters),
        out_specs=tuple([_HBM_ONLY] * (2 * n)),
        input_output_aliases={i: i for i in range(2 * n)},
        compiler_params=pltpu.CompilerParams(has_side_effects=_DATAFLOW),
    )(*srcs, *lands, send_sems, recv_sems, *afters)
    return outs[n:]


def _unstack_cols(name, stacked):
    n, rows, cols = stacked.shape

    def body(i_ref, o_ref):
        o_ref[...] = i_ref[...]

    return pl.pallas_call(
        body, name=name, grid=(n,), in_specs=[pl.BlockSpec((None, rows, cols), lambda k: (k, 0, 0))],
        out_specs=pl.BlockSpec((rows, cols), lambda k: (0, k)),
        out_shape=jax.ShapeDtypeStruct((rows, n * cols), stacked.dtype),
        compiler_params=_cparams(("parallel",)),
    )(stacked)


def _restack_cols(name, mat):
    rows, width = mat.shape
    cols = width // N_DEV

    def body(i_ref, o_ref):
        o_ref[...] = i_ref[...]

    return pl.pallas_call(
        body, name=name, grid=(N_DEV,), in_specs=[pl.BlockSpec((rows, cols), lambda k: (0, k))],
        out_specs=pl.BlockSpec((None, rows, cols), lambda k: (k, 0, 0)),
        out_shape=jax.ShapeDtypeStruct((N_DEV, rows, cols), mat.dtype),
        compiler_params=_cparams(("parallel",)),
    )(mat)


def _remap_pieces(runs):
    plan = {}
    for du, dc, su, sc, ln in runs:
        while ln > 0:
            lane = dc % LANE
            take = min(ln, LANE - lane)
            plan.setdefault((du, dc // LANE), []).append((su, sc, take, lane))
            dc, sc, ln = dc + take, sc + take, ln - take
    return plan


def _remap(name, srcs, src_units, runs, *, out_units, out_cols, out_dtype, tr=256):
    rows = srcs[0].shape[-2]
    tr = min(tr, rows)
    plan = _remap_pieces(runs)
    n_src = len(srcs)
    stacked_out = out_units is not None
    n_tiles = out_cols // LANE

    def body(*refs):
        o_ref = refs[n_src]

        def src_tile(unit, t):
            ai, lead = src_units[unit]
            ref = refs[ai]
            sl = slice(t * LANE, (t + 1) * LANE)
            return (ref[:, sl] if lead is None else ref[lead, :, sl]).astype(F32)

        lane = lax.broadcasted_iota(jnp.int32, (tr, LANE), 1)
        for du in range(out_units if stacked_out else 1):
            for t in range(n_tiles):
                acc = jnp.zeros((tr, LANE), F32)
                for su, sc, ln, dl in plan.get((du if stacked_out else None, t), []):
                    st, so = sc // LANE, sc % LANE
                    first = src_tile(su, st)
                    if so == dl and so + ln <= LANE:
                        piece = first
                    else:
                        second = src_tile(su, st + 1) if so + ln > LANE else first
                        both = jnp.concatenate([first, second], axis=1)
                        piece = pltpu.roll(both, (dl - so) % (2 * LANE), axis=1)[:, 0:LANE]
                    acc = piece if (dl == 0 and ln == LANE) else jnp.where(
                        jnp.logical_and(lane >= dl, lane < dl + ln), piece, acc)
                if stacked_out:
                    o_ref[du, :, t * LANE:(t + 1) * LANE] = acc.astype(o_ref.dtype)
                else:
                    o_ref[:, t * LANE:(t + 1) * LANE] = acc.astype(o_ref.dtype)

    in_specs = []
    for arr in srcs:
        if arr.ndim == 2:
            in_specs.append(pl.BlockSpec((tr, arr.shape[1]), lambda i: (i, 0)))
        else:
            in_specs.append(pl.BlockSpec((arr.shape[0], tr, arr.shape[2]), lambda i: (0, i, 0)))
    if stacked_out:
        out_spec = pl.BlockSpec((out_units, tr, out_cols), lambda i: (0, i, 0))
        out_shape = jax.ShapeDtypeStruct((out_units, rows, out_cols), out_dtype)
    else:
        out_spec = pl.BlockSpec((tr, out_cols), lambda i: (i, 0))
        out_shape = jax.ShapeDtypeStruct((rows, out_cols), out_dtype)
    return pl.pallas_call(
        body, name=name, grid=(rows // tr,), in_specs=in_specs, out_specs=out_spec, out_shape=out_shape,
        compiler_params=_cparams(("parallel",)),
    )(*srcs)


def _proj_col(c):
    if c < PROJ_GATE0:
        return c
    if c < PROJ_GATE0 + FOX_HEADS:
        return PROJ_F0 + (c - PROJ_GATE0)
    return c - FOX_HEADS


def _win_runs():
    cuts = sorted(set([0, PROJ_GATE0, PROJ_GATE0 + FOX_HEADS, IN_WIDTH] + [SHARD_IN * k for k in range(N_DEV + 1)]))
    return [(lo // SHARD_IN, lo % SHARD_IN, _proj_col(lo), hi - lo) for lo, hi in zip(cuts[:-1], cuts[1:])]


def _assemble_win(name, stacked):
    runs = [(None, pc, k, sc, ln) for k, sc, pc, ln in _win_runs()]
    return _remap(name, [stacked], [(0, k) for k in range(N_DEV)], runs,
                  out_units=None, out_cols=PROJ_WIDTH, out_dtype=BF16)


def _disassemble_dwin(name, dw):
    runs = [(k, sc, 0, pc, ln) for k, sc, pc, ln in _win_runs()]
    return _remap(name, [dw], [(0, None)], runs, out_units=N_DEV, out_cols=SHARD_IN_PAD, out_dtype=BF16)


def _concat_cols(name, parts, *, tr=512):
    rows = parts[0].shape[0]
    tr = min(tr, rows)
    widths = [p.shape[1] for p in parts]
    total = sum(widths)

    def body(*refs):
        o_ref = refs[len(parts)]
        lo = 0
        for r, w in zip(refs[:len(parts)], widths):
            o_ref[:, lo:lo + w] = r[...].astype(o_ref.dtype)
            lo += w

    return pl.pallas_call(
        body, name=name, grid=(rows // tr,),
        in_specs=[pl.BlockSpec((tr, w), lambda i: (i, 0)) for w in widths],
        out_specs=pl.BlockSpec((tr, total), lambda i: (i, 0)),
        out_shape=jax.ShapeDtypeStruct((rows, total), BF16),
        compiler_params=_cparams(("parallel",)),
    )(*parts)


FFN_BLK = FFN_HIDDEN // 2


def _ffn_col(c):
    half, r = divmod(c, FFN_HIDDEN)
    blk, r = divmod(r, FFN_BLK)
    return blk * 2 * FFN_BLK + half * FFN_BLK + r


def _assemble_wffn(name, stacked):
    runs = [(None, _ffn_col(SHARD_FFN * k), k, 0, SHARD_FFN) for k in range(N_DEV)]
    return _remap(name, [stacked], [(0, k) for k in range(N_DEV)], runs,
                  out_units=None, out_cols=2 * FFN_HIDDEN, out_dtype=BF16)


def _disassemble_dwffn(name, dw):
    runs = [(k, 0, 0, _ffn_col(SHARD_FFN * k), SHARD_FFN) for k in range(N_DEV)]
    return _remap(name, [dw], [(0, None)], runs, out_units=N_DEV, out_cols=SHARD_FFN_PAD, out_dtype=BF16)


def _ffn_in_swiglu(name, xn, w, *, tm=512):
    rows, k = xn.shape
    tm = min(tm, rows)
    nblk = FFN_HIDDEN // FFN_BLK

    def body(x_ref, w_ref, f_ref, g_ref):
        f = _bdot(x_ref[...], w_ref[...], _DIMS["nn"])
        f_ref[...] = f.astype(f_ref.dtype)
        fa = f[:, 0:FFN_BLK]
        g_ref[...] = (fa * _sigmoid(fa) * f[:, FFN_BLK:2 * FFN_BLK]).astype(g_ref.dtype)

    return pl.pallas_call(
        body, name=name, grid=(nblk, rows // tm),
        in_specs=[pl.BlockSpec((tm, k), lambda j, i: (i, 0)), pl.BlockSpec((k, 2 * FFN_BLK), lambda j, i: (0, j))],
        out_specs=(pl.BlockSpec((tm, 2 * FFN_BLK), lambda j, i: (i, j)), pl.BlockSpec((tm, FFN_BLK), lambda j, i: (i, j))),
        out_shape=(jax.ShapeDtypeStruct((rows, 2 * FFN_HIDDEN), BF16), jax.ShapeDtypeStruct((rows, FFN_HIDDEN), BF16)),
        compiler_params=_cparams(("parallel", "arbitrary")),
    )(xn, w)


def _d_ffn_out_swiglu(name, dh, w_out, f, *, tm=512):
    rows, d = dh.shape
    tm = min(tm, rows)
    nblk = FFN_HIDDEN // FFN_BLK

    def body(dh_ref, w_ref, f_ref, df_ref):
        dg = _bdot(dh_ref[...], w_ref[...], _DIMS["nt"])
        fa = f_ref[:, 0:FFN_BLK].astype(F32)
        fb = f_ref[:, FFN_BLK:2 * FFN_BLK].astype(F32)
        s = _sigmoid(fa)
        df_ref[:, 0:FFN_BLK] = (dg * fb * s * (1.0 + fa * (1.0 - s))).astype(df_ref.dtype)
        df_ref[:, FFN_BLK:2 * FFN_BLK] = (dg * fa * s).astype(df_ref.dtype)

    wide = pl.BlockSpec((tm, 2 * FFN_BLK), lambda j, i: (i, j))
    return pl.pallas_call(
        body, name=name, grid=(nblk, rows // tm),
        in_specs=[pl.BlockSpec((tm, d), lambda j, i: (i, 0)), pl.BlockSpec((FFN_BLK, d), lambda j, i: (j, 0)), wide],
        out_specs=wide, out_shape=jax.ShapeDtypeStruct((rows, 2 * FFN_HIDDEN), BF16),
        compiler_params=_cparams(("parallel", "arbitrary")),
    )(dh, w_out, f)


def _adamw(name, parts, w, m, v, *, tr=128):
    rows, cols = w.shape
    n_parts = parts.shape[0]
    tr = min(tr, rows)
    assert rows % tr == 0, (name, rows, tr)
    c1 = 1.0 - ADAM_B1 ** ADAM_STEP
    c2 = 1.0 - ADAM_B2 ** ADAM_STEP

    def body(p_ref, w_ref, m_ref, v_ref, g_ref, d_ref, nm_ref, nv_ref):
        g = p_ref[0].astype(F32)
        for s in range(1, n_parts):
            g = g + p_ref[s].astype(F32)
        m_new = ADAM_B1 * m_ref[...] + (1.0 - ADAM_B1) * g
        v_new = ADAM_B2 * v_ref[...] + (1.0 - ADAM_B2) * (g * g)
        upd = (m_new / c1) / (jnp.sqrt(v_new / c2) + ADAM_EPS) + ADAM_WD * w_ref[...]
        g_ref[...] = g
        d_ref[...] = -ADAM_LR * upd
        nm_ref[...] = m_new
        nv_ref[...] = v_new

    row = pl.BlockSpec((tr, cols), lambda i: (i, 0))
    out = jax.ShapeDtypeStruct((rows, cols), F32)
    return pl.pallas_call(
        body, name=name, grid=(rows // tr,),
        in_specs=[pl.BlockSpec((n_parts, tr, cols), lambda i: (0, i, 0)), row, row, row],
        out_specs=(row, row, row, row), out_shape=(out, out, out, out),
        compiler_params=_cparams(("parallel",)),
    )(parts, w, m, v)


_WEIGHTS = ("norm_mix", "w_in", "b_forget", "lam_re", "lam_im", "log_dt", "b_re", "b_im", "c_re", "c_im",
            "d_skip", "w_glu", "w_fox_o", "w_mix_out", "norm_mem_q", "norm_mem_kv", "w_mem_q", "w_mem_kv",
            "w_mem_o", "norm_ffn", "w_ffn_in", "w_ffn_out", "norm_final")
_SHARDED = ("w_in", "w_glu", "w_fox_o", "w_mix_out", "w_mem_q", "w_mem_kv", "w_mem_o", "w_ffn_in", "w_ffn_out")
_SMALL = tuple(n for n in _WEIGHTS if n not in _SHARDED)
_PACK_COLS = 1024


def _pack(arrays):
    flat = jnp.concatenate([a.reshape(-1).astype(F32) for a in arrays])
    rows = -(-flat.shape[0] // _PACK_COLS)
    return jnp.pad(flat, (0, rows * _PACK_COLS - flat.shape[0])).reshape(rows, _PACK_COLS)


def _unpack(buf, like):
    flat = buf.reshape(-1)
    out, pos = [], 0
    for a in like:
        out.append(flat[pos:pos + a.size].reshape(a.shape))
        pos += a.size
    return out


def _proj_fwd(name, u, win, *, tm=512):
    seq, d = u.shape
    cuts = (0, SSM_WIDTH, PROJ_GATE0, PROJ_F0, PROJ_WIDTH)
    dtypes = (F32, BF16, BF16, F32)

    def body(u_ref, w_ref, *o_refs):
        uv = u_ref[...]
        for o_ref, lo, hi in zip(o_refs, cuts[:-1], cuts[1:]):
            o_ref[...] = _bdot(uv, w_ref[:, lo:hi], _DIMS["nn"]).astype(o_ref.dtype)

    widths = [hi - lo for lo, hi in zip(cuts[:-1], cuts[1:])]
    return pl.pallas_call(
        body, name=name, grid=(seq // tm,),
        in_specs=[pl.BlockSpec((tm, d), lambda i: (i, 0)), pl.BlockSpec((d, PROJ_WIDTH), lambda i: (0, 0))],
        out_specs=tuple(pl.BlockSpec((tm, w), lambda i: (i, 0)) for w in widths),
        out_shape=tuple(jax.ShapeDtypeStruct((seq, w), t) for w, t in zip(widths, dtypes)),
        compiler_params=_cparams(("parallel",)),
    )(u, win)


def _mm(name, a, b, mode, m, n, k, out_dtype, tm=1024, tn=512, tk=1024, **kw):
    return _matmul(name, a, b, mode, m, n, k, out_dtype=out_dtype, tm=tm, tn=tn, tk=tk, **kw)


def kernel(x, mem, norm_mix, w_in, b_forget, lam_re, lam_im, log_dt, b_re, b_im, c_re, c_im, d_skip, w_glu, w_fox_o, w_mix_out, norm_mem_q, norm_mem_kv, w_mem_q, w_mem_kv, w_mem_o, norm_ffn, w_ffn_in, w_ffn_out, norm_final, loss_target, m_norm_mix, m_w_in, m_b_forget, m_lam_re, m_lam_im, m_log_dt, m_b_re, m_b_im, m_c_re, m_c_im, m_d_skip, m_w_glu, m_w_fox_o, m_w_mix_out, m_norm_mem_q, m_norm_mem_kv, m_w_mem_q, m_w_mem_kv, m_w_mem_o, m_norm_ffn, m_w_ffn_in, m_w_ffn_out, m_norm_final, v_norm_mix, v_w_in, v_b_forget, v_lam_re, v_lam_im, v_log_dt, v_b_re, v_b_im, v_c_re, v_c_im, v_d_skip, v_w_glu, v_w_fox_o, v_w_mix_out, v_norm_mem_q, v_norm_mem_kv, v_w_mem_q, v_w_mem_kv, v_w_mem_o, v_norm_ffn, v_w_ffn_in, v_w_ffn_out, v_norm_final):
    given = dict(locals())
    weights = {n: given[n] for n in _WEIGHTS}
    mom_m = {n: given["m_" + n] for n in _WEIGHTS}
    mom_v = {n: given["v_" + n] for n in _WEIGHTS}
    seq = x.shape[1]
    nc = seq // SSM_CHUNK
    d = D_MODEL
    xs, mems, tgt = x[0], mem[0], loss_target[0]

    def padcols(a, width):
        return jnp.pad(a, ((0, 0), (0, width - a.shape[1])))

    shards = [padcols(w_in[0].astype(BF16), SHARD_IN_PAD), w_glu[0].astype(BF16), w_fox_o[0].astype(BF16),
              w_mix_out[0].astype(BF16), w_mem_q[0].astype(BF16), w_mem_kv[0].astype(BF16),
              w_mem_o[0].astype(BF16), padcols(w_ffn_in[0].astype(BF16), SHARD_FFN_PAD), w_ffn_out[0].astype(BF16)]
    first = shards[:1]
    wsend, wrecv, first_thru, first_lands, wtoken = _send_start(
        "gather_w_in_start", first, _place_own("place_w_in_shard", first, stacked_src=False), scatter=False)
    rest = shards[1:]
    gsend, grecv, rest_thru, lands, gtoken = _send_start(
        "gather_rest_start", rest, _place_own("place_weight_shards", rest, stacked_src=False, after=wtoken),
        scatter=False)

    u = _rms_fwd("rms_mix", xs, norm_mix, after=gtoken)
    ssm_params = tuple(p[0] + wtoken[0, 0] for p in (lam_re, lam_im, log_dt, b_re, b_im, c_re, c_im))
    (m_c, bw_c, cm_c, a8, aseg), mats_vjp = jax.vjp(lambda *p: _ssm_mats(*p, nc), *ssm_params)
    m_b = _bd_expand("ssm_expand_m", _BD_M, m_c)
    bw_b = _bd_expand("ssm_expand_bw", _BD_BW, bw_c)
    cm_b = _bd_expand("ssm_expand_cm", _BD_CM, cm_c)
    win = _assemble_win("assemble_w_in", _send_wait(
        "gather_w_in_wait", wsend, wrecv, first_thru, first_lands, (u, m_b, bw_b, cm_b), scatter=False)[0])
    ussm, qkv, gates, fproj = _proj_fwd("proj", u, win)

    u8 = ussm.reshape(nc, SSM_CHUNK * SSM_WIDTH)
    d8 = jnp.tile(d_skip, (1, SSM_CHUNK))
    w4 = _ssm_w("ssm_w", u8, bw_b)
    sp4 = _ssm_scan("ssm_scan", w4, a8, aseg, reverse=False)
    y8 = _ssm_y("ssm_y", u8, sp4, m_b, cm_b)
    act = _ssm_post_fwd("ssm_act", y8, u8, d8).reshape(seq, SSM_WIDTH)

    bcol = jnp.pad(b_forget[0], (0, LANE - FOX_HEADS)).reshape(LANE, 1)
    cum_t = _fox_cum("fox_cum", fproj, bcol).reshape(FOX_HEADS // 2, 2, seq)
    att, lse = _fox_fwd("fox_fwd", qkv, cum_t)

    gathered = _send_wait("gather_rest_wait", gsend, grecv, rest_thru, lands, att, scatter=False)
    wglu = _unstack_cols("unstack_w_glu", gathered[0])
    wfoxo = _unstack_cols("unstack_w_fox_o", gathered[1])
    wmix = gathered[2].reshape(d, d)
    wmq = gathered[3].reshape(d, MEM_WIDTH)
    wmkv = gathered[4].reshape(d, 2 * MEM_WIDTH)
    wmo = _unstack_cols("unstack_w_mem_o", gathered[5])
    wffn_in = _assemble_wffn("assemble_w_ffn_in", gathered[6])
    wffn_out = gathered[7].reshape(FFN_HIDDEN, d)

    glu = _mm("glu", act, wglu, "nn", seq, 2 * d, SSM_WIDTH, BF16, tn=1024)
    out_b = _mm("fox_out", att, wfoxo, "nn", seq, d, FOX_WIDTH, BF16, tn=1024)

    mixin, h1 = _mix_fwd("mix_mix_out", glu, gates, out_b, wmix, xs)

    n1 = _rms_fwd("rms_mem_q", h1, norm_mem_q)
    q2 = _mm("mem_q", n1, wmq, "nn", seq, MEM_WIDTH, d, BF16)
    mn = _rms_fwd("rms_mem_kv", mems, norm_mem_kv)
    mlen = mems.shape[0]
    kv = _mm("mem_kv", mn, wmkv, "nn", mlen, 2 * MEM_WIDTH, d, BF16)
    o2 = _mem_fwd("mem_attn", q2, kv)
    h2 = _mm("mem_out", o2, wmo, "nn", seq, d, MEM_WIDTH, F32, tn=1024, add=h1)

    n2 = _rms_fwd("rms_ffn", h2, norm_ffn)
    f, g_act = _ffn_in_swiglu("ffn_in_swiglu", n2, wffn_in)
    loss_part, dh3, dg_final = _matmul_final_loss("ffn_out_final_loss", g_act, wffn_out, h2, tgt,
                                                  norm_final.reshape(1, d))

    df = _d_ffn_out_swiglu("d_ffn_out_swiglu", dh3, wffn_out, f)
    dwffn_out = _mm("d_ffn_out_w", g_act, dh3, "tn", FFN_HIDDEN, d, seq, BF16, tm=1408, tn=1024)
    dh2, dg_ffn = _matmul_rms_bwd("d_ffn_in_x_rms", df, wffn_in, 2 * FFN_HIDDEN, h2, norm_ffn, dh3, tm=1024, tk=1408)
    dwffn_in = _mm("d_ffn_in_w", n2, df, "tn", d, 2 * FFN_HIDDEN, seq, BF16, tn=1408)

    do2 = _mm("d_mem_out_x", dh2, wmo, "nt", seq, MEM_WIDTH, d, F32)
    dwmo = _restack_cols("restack_d_w_mem_o", _mm("d_mem_out_w", o2, dh2, "tn", MEM_WIDTH, d, seq, BF16, tn=1024))
    dq2, dkv = _mem_bwd("d_mem_attn", q2, kv, do2)
    dwmq = _mm("d_mem_q_w", n1, dq2, "tn", d, MEM_WIDTH, seq, BF16)
    dwmkv = _mm("d_mem_kv_w", mn, dkv, "tn", d, 2 * MEM_WIDTH, mlen, BF16, tn=1024)
    dmn = _mm("d_mem_kv_x", dkv, wmkv, "nt", mlen, d, 2 * MEM_WIDTH, F32)
    dg_memkv = _rms_gain_grad("d_rms_mem_kv", dmn, mems)

    early = [dwmq.reshape(N_DEV, d // N_DEV, MEM_WIDTH), dwmkv.reshape(N_DEV, d // N_DEV, 2 * MEM_WIDTH), dwmo,
             _disassemble_dwffn("split_d_w_ffn_in", dwffn_in), dwffn_out.reshape(N_DEV, FFN_HIDDEN // N_DEV, d)]
    ssend, srecv, early_thru, early_lands, stoken = _send_start(
        "scatter_early_start", early, _place_own("place_early_grads", early, stacked_src=True), scatter=True)
    dh1, dg_memq = _matmul_rms_bwd("d_mem_q_x_rms", dq2, wmq, MEM_WIDTH, h1, norm_mem_q, dh2, tm=1024, after=stoken)

    dwmix = _mm("d_mix_out_w", mixin, dh1, "tn", d, d, seq, BF16, tn=1024)
    dglu, dgates, dout_b = _mix_bwd("d_mix_out_x_mix", dh1, wmix, glu, gates, out_b)
    datt = _mm("d_fox_out_x", dout_b, wfoxo, "nt", seq, FOX_WIDTH, d, F32)
    dwfoxo = _restack_cols("restack_d_w_fox_o", _mm("d_fox_out_w", att, dout_b, "tn", FOX_WIDTH, d, seq, BF16, tn=1024))
    dact = _mm("d_glu_x", dglu, wglu, "nt", seq, SSM_WIDTH, 2 * d, F32, tk=2 * d)
    dwglu = _restack_cols("restack_d_w_glu", _mm("d_glu_w", act, dglu, "tn", SSM_WIDTH, 2 * d, seq, BF16, tn=2 * d))

    mid = [dwglu, dwfoxo, dwmix.reshape(N_DEV, d // N_DEV, d)]
    msend, mrecv, mid_thru, mid_lands, mtoken = _send_start(
        "scatter_mid_start", mid, _place_own("place_mid_grads", mid, stacked_src=True), scatter=True)

    dz8, dg_dskip = _ssm_post_bwd("d_ssm_act", dact.reshape(nc, SSM_CHUNK * SSM_WIDTH), y8, u8, d8, after=mtoken)
    ds4, dcm = _ssm_ds("d_ssm_y_state", dz8, sp4, cm_b)
    g4, da8 = _ssm_scan("d_ssm_scan", ds4, a8, aseg, reverse=True, sprev4=sp4)
    dx8, dm, dbw = _ssm_dx("d_ssm_x", dz8, g4, u8, m_b, bw_b, d8)
    dussm = dx8.reshape(seq, SSM_WIDTH)
    g_ssm = mats_vjp((_bd_reduce("ssm_reduce_dm", _BD_M, dm), _bd_reduce("ssm_reduce_dbw", _BD_BW, dbw),
                      _bd_reduce("ssm_reduce_dcm", _BD_CM, dcm), da8, jnp.zeros_like(aseg)))

    dq, dk, dv, dcs = _fox_bwd("d_fox", qkv, cum_t, att, datt, lse)
    dfproj, dbf = _fox_cum_bwd("d_fox_cum", dcs, fproj, bcol)
    dg_bforget = dbf[0:FOX_HEADS, 0].reshape(1, FOX_HEADS)

    dproj = _concat_cols("d_proj_concat", (dussm, dq, dk, dv, dgates, dfproj))
    dwin = _mm("d_proj_w", u, dproj, "tn", d, PROJ_WIDTH, seq, BF16, tn=1408)
    late = [_disassemble_dwin("split_d_w_in", dwin)]
    lsend, lrecv, late_thru, late_lands, ltoken = _send_start(
        "scatter_late_start", late, _place_own("place_late_grads", late, stacked_src=True), scatter=True)
    dx, dg_mix = _matmul_rms_bwd("d_proj_x_rms", dproj, win, PROJ_WIDTH, xs, norm_mix, dh1, tm=512, tk=PROJ_WIDTH,
                                 after=ltoken)

    early_parts = _send_wait("scatter_early_wait", ssend, srecv, early_thru, early_lands, dx, scatter=True)
    mid_parts = _send_wait("scatter_mid_wait", msend, mrecv, mid_thru, mid_lands, dx, scatter=True)
    received = dict(zip(("w_glu", "w_fox_o", "w_mix_out"), mid_parts))
    received.update(zip(("w_mem_q", "w_mem_kv", "w_mem_o", "w_ffn_in", "w_ffn_out"), early_parts))

    small_grads = dict(zip(
        _SMALL, (dg_mix, dg_bforget, g_ssm[0][None], g_ssm[1][None], g_ssm[2][None], g_ssm[3][None], g_ssm[4][None],
                 g_ssm[5][None], g_ssm[6][None], dg_dskip, dg_memq, dg_memkv, dg_ffn, dg_final.reshape(d))))
    small_like = [weights[n] for n in _SMALL]
    small_all = _gather_all("gather_small_grads", [_pack([small_grads[n] for n in _SMALL])])[0]
    pk = [_pack([src[n] for n in _SMALL]) for src in (weights, mom_m, mom_v)]
    small_out = _adamw("adamw_small", small_all, pk[0], pk[1], pk[2], tr=small_all.shape[1])
    results = [dict(zip(_SMALL, _unpack(buf, small_like))) for buf in small_out]
    tiles = {"w_in": 128, "w_glu": 128, "w_fox_o": 128, "w_mix_out": 128, "w_mem_q": 128, "w_mem_kv": 128,
             "w_mem_o": 128, "w_ffn_in": 128, "w_ffn_out": 176}
    pads = {"w_in": SHARD_IN_PAD, "w_ffn_in": SHARD_FFN_PAD}
    outs = small_out
    for name in _SHARDED[1:] + _SHARDED[:1]:
        if name == "w_in":
            received[name] = _send_wait("scatter_late_wait", lsend, lrecv, late_thru, late_lands, outs[0],
                                        scatter=True)[0]
        parts = received[name]
        w2, m2, v2 = weights[name][0], mom_m[name][0], mom_v[name][0]
        cols = w2.shape[1]
        if name in pads:
            w2, m2, v2 = (padcols(t, pads[name]) for t in (w2, m2, v2))
        outs = _adamw("adamw_" + name, parts, w2, m2, v2, tr=tiles[name])
        for res, o in zip(results, outs):
            res[name] = o[:, :cols][None]

    loss = lax.psum(loss_part[0, 0], ("x", "y", "c"))
    out = [loss, dx[None]]
    for res in results:
        out.extend(res[n] for n in _WEIGHTS)
    return tuple(out)
```

```python
import math

import jax
import jax.numpy as jnp
import numpy as np
from jax import lax
from jax.experimental import pallas as pl
from jax.experimental.pallas import tpu as pltpu

F32 = jnp.float32
BF16 = jnp.bfloat16

N_DEV = 8
LANE = 128
VMEM_LIMIT = 56 * 1024 * 1024

D_MODEL = 1024
SSM_GROUP = 16
SSM_GROUPS = 32
SSM_WIDTH = 512
SSM_STATE = 64
SSM_CHUNK = 8
FOX_HEADS = 8
FOX_HEAD_DIM = 64
FOX_WIDTH = 512
MEM_HEADS = 4
MEM_HEAD_DIM = 128
MEM_WIDTH = 512
FFN_HIDDEN = 2816
RMS_EPS = 1e-6
IN_WIDTH = 4104
SHARD_IN = IN_WIDTH // N_DEV
SHARD_IN_PAD = 640
SHARD_FFN = 2 * FFN_HIDDEN // N_DEV
SHARD_FFN_PAD = 768
PROJ_GATE0 = 2048
PROJ_F0 = 4096
PROJ_WIDTH = 4224

ADAM_LR = 0.001
ADAM_B1 = 0.9
ADAM_B2 = 0.999
ADAM_EPS = 1e-08
ADAM_WD = 0.01
ADAM_STEP = 10


def _cparams(sem=None):
    return pltpu.CompilerParams(dimension_semantics=sem, vmem_limit_bytes=VMEM_LIMIT)


def _sigmoid(x):
    return 1.0 / (1.0 + jnp.exp(-x))


def _bdot(a, b, dims):
    return lax.dot_general(a.astype(BF16), b.astype(BF16), ((dims[0], dims[1]), ((), ())),
                           preferred_element_type=F32)


_DIMS = {"nn": ((1,), (0,)), "nt": ((1,), (1,)), "tn": ((0,), (0,))}


def _matmul(name, a, b, mode, m, n, k, *, out_dtype, tm, tn, tk, a_off=(0, 0), b_off=(0, 0), add=None):
    tm, tn, tk = min(tm, m), min(tn, n), min(tk, k)
    assert m % tm == 0 and n % tn == 0 and k % tk == 0, (name, m, n, k, tm, tn, tk)
    nk = k // tk
    grid = (m // tm, n // tn, nk)

    def blk(off, t):
        assert off % t == 0, (name, off, t)
        return off // t

    if mode in ("nn", "nt"):
        ar, ac = blk(a_off[0], tm), blk(a_off[1], tk)
        a_spec = pl.BlockSpec((tm, tk), lambda i, j, kk: (i + ar, kk + ac))
    else:
        ar, ac = blk(a_off[0], tk), blk(a_off[1], tm)
        a_spec = pl.BlockSpec((tk, tm), lambda i, j, kk: (kk + ar, i + ac))

    if mode in ("nn", "tn"):
        br, bc = blk(b_off[0], tk), blk(b_off[1], tn)
        b_spec = pl.BlockSpec((tk, tn), lambda i, j, kk: (kk + br, j + bc))
    else:
        br, bc = blk(b_off[0], tn), blk(b_off[1], tk)
        b_spec = pl.BlockSpec((tn, tk), lambda i, j, kk: (j + br, kk + bc))
    o_spec = pl.BlockSpec((tm, tn), lambda i, j, kk: (i, j))
    out_shape = jax.ShapeDtypeStruct((m, n), out_dtype)

    in_specs = [a_spec, b_spec]
    operands = [a, b]
    if add is not None:
        in_specs.append(pl.BlockSpec((tm, tn), lambda i, j, kk: (i, j)))
        operands.append(add)
    dims = _DIMS[mode]
    has_add = add is not None

    def body(*refs):
        a_ref, b_ref = refs[0], refs[1]
        add_ref = refs[2] if has_add else None
        o_ref = refs[3] if has_add else refs[2]
        acc_ref = refs[-1] if nk > 1 else None
        prod = _bdot(a_ref[...], b_ref[...], dims)

        def finish(total):
            if has_add:
                total = total + add_ref[...].astype(F32)
            o_ref[...] = total.astype(o_ref.dtype)

        if nk == 1:
            finish(prod)
        else:
            kk = pl.program_id(2)

            @pl.when(kk == 0)
            def _():
                acc_ref[...] = prod

            @pl.when(jnp.logical_and(kk > 0, kk < nk - 1))
            def _():
                acc_ref[...] += prod

            @pl.when(kk == nk - 1)
            def _():
                finish(acc_ref[...] + prod)

    scratch = [pltpu.VMEM((tm, tn), F32)] if nk > 1 else []
    return pl.pallas_call(
        body, name=name, grid=grid, in_specs=in_specs, out_specs=o_spec, out_shape=out_shape,
        scratch_shapes=scratch,
        compiler_params=_cparams(("parallel", "parallel", "arbitrary")),
    )(*operands)


def _rms_fwd(name, x, gain, *, tr=512, after=None):
    r, d = x.shape
    tr = min(tr, r)

    def body(x_ref, g_ref, *rest):
        o_ref = rest[-1]
        xv = x_ref[...]
        rstd = lax.rsqrt(jnp.mean(xv * xv, axis=-1, keepdims=True) + RMS_EPS)
        o_ref[...] = (xv * rstd * g_ref[...]).astype(o_ref.dtype)

    in_specs = [pl.BlockSpec((tr, d), lambda i: (i, 0)), pl.BlockSpec((1, d), lambda i: (0, 0))]
    ops = [x, gain]
    if after is not None:
        in_specs.append(pl.BlockSpec(after.shape, lambda i: (0, 0)))
        ops.append(after)
    return pl.pallas_call(
        body, name=name, grid=(r // tr,), in_specs=in_specs,
        out_specs=pl.BlockSpec((tr, d), lambda i: (i, 0)),
        out_shape=jax.ShapeDtypeStruct((r, d), BF16),
        compiler_params=_cparams(("parallel",)),
    )(*ops)


def _rms_gain_grad(name, dy, x, *, tr=512):
    r, d = x.shape
    tr = min(tr, r)
    n = r // tr

    def body(dy_ref, x_ref, dg_ref, acc_ref):
        i = pl.program_id(0)
        xv = x_ref[...]
        xh = xv * lax.rsqrt(jnp.mean(xv * xv, axis=-1, keepdims=True) + RMS_EPS)
        part = (dy_ref[...].astype(F32) * xh).reshape(tr // 8, 8, d).sum(axis=0)

        @pl.when(i == 0)
        def _():
            acc_ref[...] = part

        @pl.when(i > 0)
        def _():
            acc_ref[...] += part

        @pl.when(i == n - 1)
        def _():
            dg_ref[...] = jnp.sum(acc_ref[...], axis=0, keepdims=True)

    row = pl.BlockSpec((tr, d), lambda i: (i, 0))
    return pl.pallas_call(
        body, name=name, grid=(n,), in_specs=[row, row],
        out_specs=pl.BlockSpec((1, d), lambda i: (0, 0)),
        out_shape=jax.ShapeDtypeStruct((1, d), F32),
        scratch_shapes=[pltpu.VMEM((8, d), F32)],
        compiler_params=_cparams(("arbitrary",)),
    )(dy, x)


def _matmul_rms_bwd(name, a, b, k, x, gain, res, *, tm=512, tk=1024, after=None):
    m, d = x.shape
    tm, tk = min(tm, m), min(tk, k)
    assert m % tm == 0 and k % tk == 0, (name, m, k, tm, tk)
    ni, nk = m // tm, k // tk

    def body(a_ref, b_ref, x_ref, g_ref, res_ref, *rest):
        dx_ref, dg_ref, acc_ref, accg_ref = rest[-4:]
        i, kk = pl.program_id(0), pl.program_id(1)
        prod = _bdot(a_ref[...], b_ref[...], _DIMS["nt"])

        @pl.when(kk == 0)
        def _():
            acc_ref[...] = prod

        @pl.when(kk > 0)
        def _():
            acc_ref[...] += prod

        @pl.when(kk == nk - 1)
        def _():
            dyv = acc_ref[...]
            xv = x_ref[...]
            rstd = lax.rsqrt(jnp.mean(xv * xv, axis=-1, keepdims=True) + RMS_EPS)
            xh = xv * rstd
            dxh = dyv * g_ref[...]
            dx_ref[...] = rstd * (dxh - xh * jnp.mean(dxh * xh, axis=-1, keepdims=True)) + res_ref[...]
            part = (dyv * xh).reshape(tm // 8, 8, d).sum(axis=0)

            @pl.when(i == 0)
            def _():
                accg_ref[...] = part

            @pl.when(i > 0)
            def _():
                accg_ref[...] += part

            @pl.when(i == ni - 1)
            def _():
                dg_ref[...] = jnp.sum(accg_ref[...], axis=0, keepdims=True)

    row = pl.BlockSpec((tm, d), lambda i, kk: (i, 0))
    one = pl.BlockSpec((1, d), lambda i, kk: (0, 0))
    in_specs = [pl.BlockSpec((tm, tk), lambda i, kk: (i, kk)), pl.BlockSpec((d, tk), lambda i, kk: (0, kk)), row, one, row]
    ops = [a, b, x, gain, res]
    if after is not None:
        in_specs.append(pl.BlockSpec(after.shape, lambda i, kk: (0, 0)))
        ops.append(after)
    return pl.pallas_call(
        body, name=name, grid=(ni, nk), in_specs=in_specs, out_specs=(row, one),
        out_shape=(jax.ShapeDtypeStruct((m, d), F32), jax.ShapeDtypeStruct((1, d), F32)),
        scratch_shapes=[pltpu.VMEM((tm, d), F32), pltpu.VMEM((8, d), F32)],
        compiler_params=_cparams(("arbitrary", "arbitrary")),
    )(*ops)


def _matmul_final_loss(name, a, b, res, target, gain, *, tr=512):
    r, d = res.shape
    k = a.shape[1]
    tr = min(tr, r)
    n = r // tr

    def body(a_ref, b_ref, res_ref, t_ref, g_ref, loss_ref, dh_ref, dg_ref, accl_ref, accg_ref):
        i = pl.program_id(0)
        xv = _bdot(a_ref[...], b_ref[...], _DIMS["nn"]) + res_ref[...]
        rstd = lax.rsqrt(jnp.mean(xv * xv, axis=-1, keepdims=True) + RMS_EPS)
        xh = xv * rstd
        e = xh * g_ref[...] - t_ref[...]
        dyv = e * (1.0 / d)
        dxh = dyv * g_ref[...]
        dh_ref[...] = rstd * (dxh - xh * jnp.mean(dxh * xh, axis=-1, keepdims=True))
        lpart = (e * e).reshape(tr // 8, 8, d).sum(axis=0)
        gpart = (dyv * xh).reshape(tr // 8, 8, d).sum(axis=0)

        @pl.when(i == 0)
        def _():
            accl_ref[...] = lpart
            accg_ref[...] = gpart

        @pl.when(i > 0)
        def _():
            accl_ref[...] += lpart
            accg_ref[...] += gpart

        @pl.when(i == n - 1)
        def _():
            tot = jnp.sum(jnp.sum(accl_ref[...], axis=0, keepdims=True), axis=1, keepdims=True)
            loss_ref[...] = jnp.broadcast_to(tot * (0.5 / d), (1, LANE))
            dg_ref[...] = jnp.sum(accg_ref[...], axis=0, keepdims=True)

    row = pl.BlockSpec((tr, d), lambda i: (i, 0))
    one = pl.BlockSpec((1, d), lambda i: (0, 0))
    return pl.pallas_call(
        body, name=name, grid=(n,),
        in_specs=[pl.BlockSpec((tr, k), lambda i: (i, 0)), pl.BlockSpec((k, d), lambda i: (0, 0)), row, row, one],
        out_specs=(pl.BlockSpec((1, LANE), lambda i: (0, 0)), row, one),
        out_shape=(jax.ShapeDtypeStruct((1, LANE), F32), jax.ShapeDtypeStruct((r, d), F32),
                   jax.ShapeDtypeStruct((1, d), F32)),
        scratch_shapes=[pltpu.VMEM((8, d), F32), pltpu.VMEM((8, d), F32)],
        compiler_params=_cparams(("arbitrary",)),
    )(a, b, res, target, gain)


_GELU_C = math.sqrt(2.0 / math.pi)


def _gelu_parts(z):
    inner = _GELU_C * (z + 0.044715 * z * z * z)
    t = jnp.tanh(inner)
    val = 0.5 * z * (1.0 + t)
    dinner = _GELU_C * (1.0 + 3.0 * 0.044715 * z * z)
    grad = 0.5 * (1.0 + t) + 0.5 * z * (1.0 - t * t) * dinner
    return val, grad


def _ssm_post_fwd(name, y8, u8, d8, *, tr=256):
    r, c = y8.shape
    tr = min(tr, r)

    def body(y_ref, u_ref, d_ref, o_ref):
        z = y_ref[...] + d_ref[...] * u_ref[...]
        o_ref[...] = _gelu_parts(z)[0].astype(o_ref.dtype)

    row = pl.BlockSpec((tr, c), lambda i: (i, 0))
    return pl.pallas_call(
        body, name=name, grid=(r // tr,), in_specs=[row, row, pl.BlockSpec((1, c), lambda i: (0, 0))],
        out_specs=row, out_shape=jax.ShapeDtypeStruct((r, c), BF16),
        compiler_params=_cparams(("parallel",)),
    )(y8, u8, d8)


def _ssm_post_bwd(name, dact8, y8, u8, d8, *, tr=256, after=None):
    r, c = y8.shape
    tr = min(tr, r)
    n = r // tr

    def body(*refs):
        da_ref, y_ref, u_ref, d_ref = refs[:4]
        dz_ref, dd_ref, acc_ref = refs[-3:]
        i = pl.program_id(0)
        uv = u_ref[...]
        z = y_ref[...] + d_ref[...] * uv
        dz = da_ref[...].astype(F32) * _gelu_parts(z)[1]
        dz_ref[...] = dz
        part = (dz * uv).reshape(tr // 8, 8, c).sum(axis=0)

        @pl.when(i == 0)
        def _():
            acc_ref[...] = part

        @pl.when(i > 0)
        def _():
            acc_ref[...] += part

        @pl.when(i == n - 1)
        def _():
            tot = jnp.sum(acc_ref[...], axis=0, keepdims=True)
            out = tot[:, 0:SSM_WIDTH]
            for j in range(1, c // SSM_WIDTH):
                out = out + tot[:, j * SSM_WIDTH:(j + 1) * SSM_WIDTH]
            dd_ref[...] = out

    row = pl.BlockSpec((tr, c), lambda i: (i, 0))
    in_specs = [row, row, row, pl.BlockSpec((1, c), lambda i: (0, 0))]
    ops = [dact8, y8, u8, d8]
    if after is not None:
        in_specs.append(pl.BlockSpec(memory_space=pl.ANY))
        ops.append(after)
    return pl.pallas_call(
        body, name=name, grid=(n,), in_specs=in_specs,
        out_specs=(row, pl.BlockSpec((1, SSM_WIDTH), lambda i: (0, 0))),
        out_shape=(jax.ShapeDtypeStruct((r, c), F32), jax.ShapeDtypeStruct((1, SSM_WIDTH), F32)),
        scratch_shapes=[pltpu.VMEM((8, c), F32)],
        compiler_params=_cparams(("arbitrary",)),
    )(*ops)


def _mix_fwd(name, glu, gates, out_b, w_mix, res, *, tr=512):
    r = glu.shape[0]
    d = D_MODEL
    tr = min(tr, r)

    def body(glu_ref, gate_ref, ob_ref, w_ref, res_ref, o_ref, h_ref):
        out_a = glu_ref[:, 0:d].astype(F32) * _sigmoid(glu_ref[:, d:2 * d].astype(F32))
        mix = (_sigmoid(gate_ref[:, 0:d].astype(F32)) * out_a
               + _sigmoid(gate_ref[:, d:2 * d].astype(F32)) * ob_ref[...].astype(F32))
        o_ref[...] = mix.astype(o_ref.dtype)
        h_ref[...] = _bdot(o_ref[...], w_ref[...], _DIMS["nn"]) + res_ref[...]

    wide = pl.BlockSpec((tr, 2 * d), lambda i: (i, 0))
    row = pl.BlockSpec((tr, d), lambda i: (i, 0))
    return pl.pallas_call(
        body, name=name, grid=(r // tr,),
        in_specs=[wide, wide, row, pl.BlockSpec((d, d), lambda i: (0, 0)), row], out_specs=(row, row),
        out_shape=(jax.ShapeDtypeStruct((r, d), BF16), jax.ShapeDtypeStruct((r, d), F32)),
        compiler_params=_cparams(("parallel",)),
    )(glu, gates, out_b, w_mix, res)


def _mix_bwd(name, dh, w_mix, glu, gates, out_b, *, tr=512):
    r = glu.shape[0]
    d = D_MODEL
    tr = min(tr, r)

    def body(dh_ref, w_ref, glu_ref, gate_ref, ob_ref, dglu_ref, dgate_ref, dob_ref):
        dm = _bdot(dh_ref[...], w_ref[...], _DIMS["nt"])
        glu_a = glu_ref[:, 0:d].astype(F32)
        sb = _sigmoid(glu_ref[:, d:2 * d].astype(F32))
        ga = _sigmoid(gate_ref[:, 0:d].astype(F32))
        gb = _sigmoid(gate_ref[:, d:2 * d].astype(F32))
        out_a = glu_a * sb
        dout_a = dm * ga
        dglu_ref[:, 0:d] = (dout_a * sb).astype(dglu_ref.dtype)
        dglu_ref[:, d:2 * d] = (dout_a * glu_a * sb * (1.0 - sb)).astype(dglu_ref.dtype)
        dgate_ref[:, 0:d] = (dm * out_a * ga * (1.0 - ga)).astype(dgate_ref.dtype)
        dgate_ref[:, d:2 * d] = (dm * ob_ref[...].astype(F32) * gb * (1.0 - gb)).astype(dgate_ref.dtype)
        dob_ref[...] = (dm * gb).astype(dob_ref.dtype)

    wide = pl.BlockSpec((tr, 2 * d), lambda i: (i, 0))
    row = pl.BlockSpec((tr, d), lambda i: (i, 0))
    return pl.pallas_call(
        body, name=name, grid=(r // tr,),
        in_specs=[row, pl.BlockSpec((d, d), lambda i: (0, 0)), wide, wide, row], out_specs=(wide, wide, row),
        out_shape=(jax.ShapeDtypeStruct((r, 2 * d), BF16), jax.ShapeDtypeStruct((r, 2 * d), BF16),
                   jax.ShapeDtypeStruct((r, d), BF16)),
        compiler_params=_cparams(("parallel",)),
    )(dh, w_mix, glu, gates, out_b)


def _ssm_mats(lam_re, lam_im, log_dt, b_re, b_im, c_re, c_im, nc):
    hp = lax.Precision.HIGHEST
    t = SSM_CHUNK
    nq = SSM_GROUPS // 8
    lam = lax.complex(lam_re, lam_im)
    z = lam * jnp.exp(log_dt)[:, None]
    ks = jnp.arange(t + 1, dtype=F32)
    apow = jnp.exp(ks[:, None, None] * z[None])
    bbar = ((apow[1] - 1.0) / lam)[..., None] * lax.complex(b_re, b_im)
    c = lax.complex(c_re, c_im)

    ca = c[None] * apow[:, :, None, :]
    kmat = jnp.einsum("kgnp,gpm->kgnm", ca, bbar, precision=hp).real
    ii = np.arange(t)
    lag = ii[None, :] - ii[:, None]
    kt = kmat[np.clip(lag, 0, t)] * jnp.asarray(lag >= 0, F32)[:, :, None, None, None]
    kt = kt.reshape(t, t, nq, 8, SSM_GROUP, SSM_GROUP)
    m_c = kt.transpose(2, 0, 3, 5, 1, 4).reshape(nq, 1024, LANE)

    arev = jnp.exp((float(t - 1) - ks[:t])[:, None, None] * z[None])
    w = arev[:, :, :, None] * bbar[None]
    wr = jnp.stack([w.real, w.imag]).reshape(2, t, nq, 8, SSM_STATE, SSM_GROUP)
    bw_c = wr.transpose(2, 1, 3, 5, 0, 4).reshape(nq, 1024, LANE)

    ca1 = ca[1:]
    cr = jnp.stack([ca1.real, -ca1.imag]).reshape(2, t, nq, 8, SSM_GROUP, SSM_STATE)
    cm_c = cr.transpose(2, 0, 3, 5, 1, 4).reshape(nq, 1024, LANE)

    def tiles(v):
        vq = jnp.concatenate([v.real.reshape(nq, 512), v.imag.reshape(nq, 512)], axis=1)
        return jnp.broadcast_to(vq.reshape(nq, 8, 1, LANE), (nq, 8, 8, LANE))

    return m_c, bw_c, cm_c, tiles(apow[t]), tiles(jnp.exp(float(nc) * z))


_BD_M = (LANE, SSM_GROUP)
_BD_BW = (LANE, SSM_STATE)
_BD_CM = (512, SSM_GROUP)


def _bd_perm(cn):
    rr = lax.broadcasted_iota(jnp.int32, (1024, 1024), 0)
    cc = lax.broadcasted_iota(jnp.int32, (1024, 1024), 1)
    sh = cn.bit_length() - 1
    src = ((rr >> 7) << sh) + (((rr & (LANE - 1)) >> sh) << (3 + sh)) + (rr & (cn - 1))
    return jnp.where(src == cc, 1.0, 0.0).astype(BF16)


def _bd_rowgroup(span):
    r = lax.broadcasted_iota(jnp.int32, (1024, LANE), 0)
    return (r & (span - 1)) >> ((span // 8).bit_length() - 1)


def _bd_expand(name, kind, compact):
    span, cn = kind
    nq = compact.shape[0]

    def body(c_ref, o_ref, perm_scr):
        @pl.when(pl.program_id(0) == 0)
        def _():
            perm_scr[...] = _bd_perm(cn)

        x = c_ref[...]
        grp = _bd_rowgroup(span)
        xcat = jnp.concatenate([jnp.where(grp == h, x, 0.0) for h in range(8)], axis=1)
        o_ref[...] = _bdot(xcat, perm_scr[...], _DIMS["nn"]).astype(o_ref.dtype)

    return pl.pallas_call(
        body, name=name, grid=(nq,), in_specs=[pl.BlockSpec((None, 1024, LANE), lambda q: (q, 0, 0))],
        out_specs=pl.BlockSpec((None, 1024, 1024), lambda q: (q, 0, 0)),
        out_shape=jax.ShapeDtypeStruct((nq, 1024, 1024), BF16),
        scratch_shapes=[pltpu.VMEM((1024, 1024), BF16)],
        compiler_params=_cparams(("arbitrary",)),
    )(compact)


def _bd_reduce(name, kind, dbig):
    span, cn = kind
    nq = dbig.shape[0]

    def body(g_ref, o_ref, perm_scr):
        @pl.when(pl.program_id(0) == 0)
        def _():
            perm_scr[...] = _bd_perm(cn)

        back = _bdot(g_ref[...], perm_scr[...], _DIMS["nt"])
        grp = _bd_rowgroup(span)
        out = jnp.zeros((1024, LANE), F32)
        for h in range(8):
            out = jnp.where(grp == h, back[:, h * LANE:(h + 1) * LANE], out)
        o_ref[...] = out

    return pl.pallas_call(
        body, name=name, grid=(nq,), in_specs=[pl.BlockSpec((None, 1024, 1024), lambda q: (q, 0, 0))],
        out_specs=pl.BlockSpec((None, 1024, LANE), lambda q: (q, 0, 0)),
        out_shape=jax.ShapeDtypeStruct((nq, 1024, LANE), F32),
        scratch_shapes=[pltpu.VMEM((1024, 1024), BF16)],
        compiler_params=_cparams(("arbitrary",)),
    )(dbig)


def _x_tile_specs(nc, nq):
    return [pl.BlockSpec((nc, LANE), lambda q, t, i=i: (0, i * nq + q)) for i in range(SSM_CHUNK)]


def _cat_tiles(refs):
    return jnp.concatenate([r[...] for r in refs], axis=1)


def _ssm_w(name, x8, bw):
    nc = x8.shape[0]
    nq = bw.shape[0]

    def body(*refs):
        xq = _cat_tiles(refs[:8])
        refs[9][...] = _bdot(xq, refs[8][...], _DIMS["nn"])

    return pl.pallas_call(
        body, name=name, grid=(nq, 8),
        in_specs=_x_tile_specs(nc, nq) + [pl.BlockSpec((None, 1024, LANE), lambda q, t: (q, 0, t))],
        out_specs=pl.BlockSpec((None, None, nc, LANE), lambda q, t: (q, t, 0, 0)),
        out_shape=jax.ShapeDtypeStruct((nq, 8, nc, LANE), F32),
        compiler_params=_cparams(("parallel", "arbitrary")),
    )(*([x8] * 8), bw)


def _ssm_scan(name, w4, a_t, aseg_t, *, reverse, sprev4=None):
    nq, _, nc, _ = w4.shape
    ns = nc // 8
    with_da = sprev4 is not None

    def body(*refs):
        w_ref, a_ref, aseg_ref = refs[:3]
        s_ref = refs[3] if with_da else None
        o_ref = refs[4] if with_da else refs[3]
        da_ref = refs[5] if with_da else None
        sgn = -1.0 if reverse else 1.0
        ar = [a_ref[j] for j in range(4)]
        ai = [sgn * a_ref[j + 4] for j in range(4)]
        gr = [aseg_ref[j] for j in range(4)]
        gi = [sgn * aseg_ref[j + 4] for j in range(4)]
        zero = tuple(jnp.zeros((8, LANE), F32) for _ in range(8))

        def rows(tt):
            return pl.ds((ns - 1 - tt) if reverse else tt, 8, stride=ns)

        def step(carry, w):
            new_r = [ar[j] * carry[j] - ai[j] * carry[j + 4] + w[j] for j in range(4)]
            new_i = [ar[j] * carry[j + 4] + ai[j] * carry[j] + w[j + 4] for j in range(4)]
            return tuple(new_r + new_i)

        def pass1(tt, carry):
            return step(carry, [w_ref[j, rows(tt), :] for j in range(8)])

        ends = lax.fori_loop(0, ns, pass1, zero)
        sub = lax.broadcasted_iota(jnp.int32, (8, LANE), 0)
        init = list(zero)
        order = range(7, 0, -1) if reverse else range(0, 7)
        for s in order:
            nxt = s - 1 if reverse else s + 1
            cand_r = [gr[j] * init[j] - gi[j] * init[j + 4] + ends[j] for j in range(4)]
            cand_i = [gr[j] * init[j + 4] + gi[j] * init[j] + ends[j + 4] for j in range(4)]
            cand = cand_r + cand_i
            shift = 7 if reverse else 1
            init = [jnp.where(sub == nxt, pltpu.roll(cand[j], shift, axis=0), init[j]) for j in range(8)]

        def pass2(tt, state):
            carry, acc = state
            r = rows(tt)
            for j in range(8):
                o_ref[j, r, :] = carry[j]
            if with_da:
                sp = [s_ref[j, r, :] for j in range(8)]
                acc_r = [acc[j] + carry[j] * sp[j] + carry[j + 4] * sp[j + 4] for j in range(4)]
                acc_i = [acc[j + 4] + carry[j + 4] * sp[j] - carry[j] * sp[j + 4] for j in range(4)]
                acc = tuple(acc_r + acc_i)
            return step(carry, [w_ref[j, r, :] for j in range(8)]), acc

        _, acc = lax.fori_loop(0, ns, pass2, (tuple(init), zero))
        if with_da:
            for j in range(8):
                da_ref[j] = acc[j]

    big = pl.BlockSpec((None, 8, nc, LANE), lambda q: (q, 0, 0, 0))
    small = pl.BlockSpec((None, 8, 8, LANE), lambda q: (q, 0, 0, 0))
    in_specs = [big, small, small] + ([big] if with_da else [])
    ops = [w4, a_t, aseg_t] + ([sprev4] if with_da else [])
    out_specs = (big, small) if with_da else big
    big_s = jax.ShapeDtypeStruct((nq, 8, nc, LANE), F32)
    out_shape = (big_s, jax.ShapeDtypeStruct((nq, 8, 8, LANE), F32)) if with_da else big_s
    return pl.pallas_call(
        body, name=name, grid=(nq,), in_specs=in_specs, out_specs=out_specs, out_shape=out_shape,
        compiler_params=_cparams(("parallel",)),
    )(*ops)


def _ssm_y(name, x8, sprev4, m_mat, cm_mat):
    nc = x8.shape[0]
    nq = m_mat.shape[0]

    def body(*refs):
        xq = _cat_tiles(refs[:8])
        s_ref, m_ref, cm_ref, o_ref = refs[8:12]
        sq = jnp.concatenate([s_ref[t] for t in range(8)], axis=1)
        o_ref[...] = _bdot(xq, m_ref[...], _DIMS["nn"]) + _bdot(sq, cm_ref[...], _DIMS["nn"])

    col = pl.BlockSpec((None, 1024, LANE), lambda q, j: (q, 0, j))
    return pl.pallas_call(
        body, name=name, grid=(nq, 8),
        in_specs=_x_tile_specs(nc, nq) + [pl.BlockSpec((None, 8, nc, LANE), lambda q, j: (q, 0, 0, 0)), col, col],
        out_specs=pl.BlockSpec((nc, LANE), lambda q, j: (0, j * nq + q)),
        out_shape=jax.ShapeDtypeStruct((nc, 8 * SSM_WIDTH), F32),
        compiler_params=_cparams(("parallel", "arbitrary")),
    )(*([x8] * 8), sprev4, m_mat, cm_mat)


def _ssm_ds(name, dz8, sprev4, cm_mat):
    nc = dz8.shape[0]
    nq = cm_mat.shape[0]

    def body(*refs):
        dyq = _cat_tiles(refs[:8]).astype(BF16)
        s_ref, cm_ref, ds_ref, dcm_ref = refs[8:12]
        ds_ref[...] = _bdot(dyq, cm_ref[...], _DIMS["nt"])
        dcm_ref[...] = _bdot(s_ref[...], dyq, _DIMS["tn"])

    tile = pl.BlockSpec((None, None, nc, LANE), lambda q, t: (q, t, 0, 0))
    rowblk = pl.BlockSpec((None, LANE, 1024), lambda q, t: (q, t, 0))
    return pl.pallas_call(
        body, name=name, grid=(nq, 8),
        in_specs=_x_tile_specs(nc, nq) + [tile, rowblk],
        out_specs=(tile, rowblk),
        out_shape=(jax.ShapeDtypeStruct((nq, 8, nc, LANE), F32), jax.ShapeDtypeStruct((nq, 1024, 1024), F32)),
        compiler_params=_cparams(("parallel", "arbitrary")),
    )(*([dz8] * 8), sprev4, cm_mat)


def _ssm_dx(name, dz8, g4, x8, m_mat, bw_mat, d8):
    nc = dz8.shape[0]
    nq = m_mat.shape[0]

    def body(*refs):
        dyq = _cat_tiles(refs[:8]).astype(BF16)
        g_ref, x_ref, m_ref, bw_ref, d_ref, dzi_ref, dx_ref, dm_ref, dbw_ref = refs[8:17]
        gq = jnp.concatenate([g_ref[t] for t in range(8)], axis=1).astype(BF16)
        dx = _bdot(dyq, m_ref[...], _DIMS["nt"]) + _bdot(gq, bw_ref[...], _DIMS["nt"])
        dx_ref[...] = (dx + d_ref[...] * dzi_ref[...]).astype(dx_ref.dtype)
        xi = x_ref[...]
        dm_ref[...] = _bdot(xi, dyq, _DIMS["tn"])
        dbw_ref[...] = _bdot(xi, gq, _DIMS["tn"])

    xtile = pl.BlockSpec((nc, LANE), lambda q, i: (0, i * nq + q))
    rowblk = pl.BlockSpec((None, LANE, 1024), lambda q, i: (q, i, 0))
    return pl.pallas_call(
        body, name=name, grid=(nq, 8),
        in_specs=_x_tile_specs(nc, nq) + [pl.BlockSpec((None, 8, nc, LANE), lambda q, i: (q, 0, 0, 0)), xtile, rowblk, rowblk,
                                          pl.BlockSpec((1, LANE), lambda q, i: (0, q)), xtile],
        out_specs=(xtile, rowblk, rowblk),
        out_shape=(jax.ShapeDtypeStruct((nc, 8 * SSM_WIDTH), BF16), jax.ShapeDtypeStruct((nq, 1024, 1024), F32),
                   jax.ShapeDtypeStruct((nq, 1024, 1024), F32)),
        compiler_params=_cparams(("parallel", "arbitrary")),
    )(*([dz8] * 8), g4, x8, m_mat, bw_mat, d8, dz8)


CUM_BLK = 256


def _split3(x):
    hi = x.astype(BF16)
    r1 = x - hi.astype(F32)
    mid = r1.astype(BF16)
    lo = (r1 - mid.astype(F32)).astype(BF16)
    return hi, mid, lo


def _tri_dot(x, tri):
    hi, mid, lo = _split3(x)
    d = _DIMS["nn"]
    return _bdot(hi, tri, d) + _bdot(mid, tri, d) + _bdot(lo, tri, d)


def _tri(n, lower):
    r = lax.broadcasted_iota(jnp.int32, (n, n), 0)
    c = lax.broadcasted_iota(jnp.int32, (n, n), 1)
    return jnp.where((r >= c) if lower else (r <= c), 1.0, 0.0).astype(BF16)


def _fox_cum(name, fproj, bcol):
    seq = fproj.shape[0]
    blk = min(CUM_BLK, seq)

    def body(f_ref, b_ref, o_ref, carry_ref):
        i = pl.program_id(0)

        @pl.when(i == 0)
        def _():
            carry_ref[...] = jnp.zeros_like(carry_ref)

        z = f_ref[...].T + b_ref[...]
        logf = jnp.minimum(z, 0.0) - jnp.log(1.0 + jnp.exp(-jnp.abs(z)))
        carry = carry_ref[...]
        cum = _tri_dot(logf, _tri(blk, lower=False)) + jnp.tile(carry, (1, blk // LANE))
        o_ref[...] = cum[0:8, :]
        carry_ref[...] = carry + jnp.sum(logf, axis=1, keepdims=True)

    return pl.pallas_call(
        body, name=name, grid=(seq // blk,),
        in_specs=[pl.BlockSpec((blk, LANE), lambda i: (i, 0)), pl.BlockSpec((LANE, 1), lambda i: (0, 0))],
        out_specs=pl.BlockSpec((8, blk), lambda i: (0, i)),
        out_shape=jax.ShapeDtypeStruct((8, seq), F32),
        scratch_shapes=[pltpu.VMEM((LANE, LANE), F32)],
        compiler_params=_cparams(("arbitrary",)),
    )(fproj, bcol)


def _fox_cum_bwd(name, dcs, fproj, bcol):
    seq = fproj.shape[0]
    blk = min(CUM_BLK, seq)
    n = seq // blk

    def body(dc_ref, f_ref, b_ref, df_ref, db_ref, carry_ref, acc_ref):
        i = pl.program_id(0)

        @pl.when(i == 0)
        def _():
            carry_ref[...] = jnp.zeros_like(carry_ref)
            acc_ref[...] = jnp.zeros_like(acc_ref)

        r = lax.broadcasted_iota(jnp.int32, (LANE, FOX_WIDTH), 0)
        c = lax.broadcasted_iota(jnp.int32, (LANE, FOX_WIDTH), 1)
        want = (r >> 1) * LANE + jnp.where((r & 1) == 0, FOX_HEAD_DIM, 0)
        sel = jnp.where(jnp.logical_and(r < FOX_HEADS, c == want), 1.0, 0.0).astype(BF16)
        hi, mid, lo = _split3(dc_ref[...])
        nt = _DIMS["nt"]
        dc = _bdot(sel, hi, nt) + _bdot(sel, mid, nt) + _bdot(sel, lo, nt)
        carry = carry_ref[...]
        dlogf = _tri_dot(dc, _tri(blk, lower=True)) + jnp.tile(carry, (1, blk // LANE))
        carry_ref[...] = carry + jnp.sum(dc, axis=1, keepdims=True)
        z = f_ref[...].T + b_ref[...]
        dft = dlogf / (1.0 + jnp.exp(z))
        df_ref[...] = dft.T.astype(df_ref.dtype)
        acc_ref[...] += jnp.sum(dft, axis=1, keepdims=True)

        @pl.when(i == n - 1)
        def _():
            db_ref[...] = acc_ref[...]

    return pl.pallas_call(
        body, name=name, grid=(n,),
        in_specs=[pl.BlockSpec((blk, FOX_WIDTH), lambda i: (n - 1 - i, 0)), pl.BlockSpec((blk, LANE), lambda i: (n - 1 - i, 0)),
                  pl.BlockSpec((LANE, 1), lambda i: (0, 0))],
        out_specs=(pl.BlockSpec((blk, LANE), lambda i: (n - 1 - i, 0)), pl.BlockSpec((LANE, LANE), lambda i: (0, 0))),
        out_shape=(jax.ShapeDtypeStruct((seq, LANE), BF16), jax.ShapeDtypeStruct((LANE, LANE), F32)),
        scratch_shapes=[pltpu.VMEM((LANE, LANE), F32), pltpu.VMEM((LANE, LANE), F32)],
        compiler_params=_cparams(("arbitrary",)),
    )(dcs, fproj, bcol)


FOX_BLK = 512
FOX_SCALE = FOX_HEAD_DIM ** -0.5


def _fox_head_mask(shape, hh):
    lane = lax.broadcasted_iota(jnp.int32, shape, 1)
    return (lane < FOX_HEAD_DIM) if hh == 0 else (lane >= FOX_HEAD_DIM)


def _fox_bias(cum_ref, hh, q0, k0, blk):
    c0 = jnp.max(cum_ref[hh:hh + 1, pl.ds(q0, LANE)], axis=1, keepdims=True)
    return c0 - cum_ref[hh:hh + 1, pl.ds(k0, blk)]


def _fox_fwd(name, qkv, cum_t):
    seq = qkv.shape[0]
    blk = min(FOX_BLK, seq)
    nb = seq // blk
    npair = FOX_HEADS // 2

    def body(q_ref, k_ref, v_ref, cum_ref, o_ref, lse_ref):
        iq = pl.program_id(1)
        q0 = pl.multiple_of(iq * blk, blk)
        qv = q_ref[...]
        row = lax.broadcasted_iota(jnp.int32, (blk, blk), 0)
        col = lax.broadcasted_iota(jnp.int32, (blk, blk), 1)
        qhs = [jnp.where(_fox_head_mask(qv.shape, hh), qv, jnp.zeros_like(qv)) * FOX_SCALE for hh in range(2)]

        def block(kb, states, masked):
            k0 = pl.multiple_of(kb * blk, blk)
            kv = k_ref[pl.ds(k0, blk), :]
            vv = v_ref[pl.ds(k0, blk), :]
            new = []
            for hh in range(2):
                m, acc = states[hh]
                s = _bdot(qhs[hh], kv, _DIMS["nt"]) + _fox_bias(cum_ref, hh, q0, k0, blk)
                if masked:
                    s = jnp.where(row >= col, s, -jnp.inf)
                m_new = jnp.maximum(m, jnp.max(s, axis=1, keepdims=True))
                p = jnp.exp(s - m_new)
                vh = jnp.where(_fox_head_mask(vv.shape, hh), vv, jnp.ones_like(vv))
                acc = jnp.exp(m - m_new) * acc + _bdot(p, vh, _DIMS["nn"])
                new.append((m_new, acc))
            return tuple(new)

        init = (jnp.full((blk, 1), -jnp.inf, F32), jnp.zeros((blk, LANE), F32))
        states = lax.fori_loop(0, iq, lambda kb, st: block(kb, st, False), (init, init))
        states = block(iq, states, True)
        outs = []
        for hh in range(2):
            m, acc = states[hh]
            other = pltpu.roll(acc, FOX_HEAD_DIM, axis=1)
            outs.append(acc / other)
            lse_ref[hh] = m + jnp.log(jnp.where(_fox_head_mask(acc.shape, hh), other, acc))
        o_ref[...] = jnp.where(_fox_head_mask(outs[0].shape, 0), outs[0], outs[1]).astype(o_ref.dtype)

    return pl.pallas_call(
        body, name=name, grid=(npair, nb),
        in_specs=[pl.BlockSpec((blk, LANE), lambda p, i: (i, p)),
                  pl.BlockSpec((seq, LANE), lambda p, i: (0, npair + p)),
                  pl.BlockSpec((seq, LANE), lambda p, i: (0, 2 * npair + p)),
                  pl.BlockSpec((None, 2, seq), lambda p, i: (p, 0, 0))],
        out_specs=(pl.BlockSpec((blk, LANE), lambda p, i: (i, p)),
                   pl.BlockSpec((2, blk, LANE), lambda p, i: (p, i, 0))),
        out_shape=(jax.ShapeDtypeStruct((seq, FOX_WIDTH), BF16), jax.ShapeDtypeStruct((FOX_HEADS, seq, LANE), F32)),
        compiler_params=_cparams(("parallel", "arbitrary")),
    )(qkv, qkv, qkv, cum_t)


def _fox_bwd(name, qkv, cum_t, att, datt, lse):
    seq = qkv.shape[0]
    blk = min(FOX_BLK, seq)
    nb = seq // blk
    npair = FOX_HEADS // 2

    def body(q_ref, k_ref, v_ref, cum_ref, o_ref, do_ref, lse_ref, dq_ref, dk_ref, dv_ref, dcs_ref):
        iq = pl.program_id(1)
        q0 = pl.multiple_of(iq * blk, blk)

        @pl.when(iq == 0)
        def _():
            dk_ref[...] = jnp.zeros_like(dk_ref)
            dv_ref[...] = jnp.zeros_like(dv_ref)
            dcs_ref[...] = jnp.zeros_like(dcs_ref)

        qv = q_ref[...]
        dov = do_ref[...].astype(F32)
        ov = o_ref[...].astype(F32)
        row = lax.broadcasted_iota(jnp.int32, (blk, blk), 0)
        col = lax.broadcasted_iota(jnp.int32, (blk, blk), 1)
        low = _fox_head_mask((blk, LANE), 0)
        qhs, qones, dohbs, deltas, lses = [], [], [], [], []
        for hh in range(2):
            hm = _fox_head_mask(qv.shape, hh)
            qh = jnp.where(hm, qv, jnp.zeros_like(qv)) * FOX_SCALE
            qhs.append(qh)
            qones.append(jnp.where(hm, qh, jnp.ones_like(qh)))
            doh = jnp.where(hm, dov, 0.0)
            dohbs.append(doh.astype(BF16))
            deltas.append(jnp.sum(doh * ov, axis=1, keepdims=True))
            lses.append(jnp.tile(lse_ref[hh], (1, blk // LANE)))

        def block(kb, dqs, masked):
            k0 = pl.multiple_of(kb * blk, blk)
            kv = k_ref[pl.ds(k0, blk), :]
            vv = v_ref[pl.ds(k0, blk), :]
            new, dks, dvs = [], [], []
            for hh in range(2):
                s = _bdot(qhs[hh], kv, _DIMS["nt"]) + _fox_bias(cum_ref, hh, q0, k0, blk)
                p = jnp.exp(s - lses[hh])
                if masked:
                    p = jnp.where(row >= col, p, 0.0)
                dp = _bdot(dohbs[hh], vv, _DIMS["nt"])
                dsb = (p * (dp - deltas[hh])).astype(BF16)
                dks.append(_bdot(dsb, qones[hh], _DIMS["tn"]))
                dvs.append(_bdot(p, dohbs[hh], _DIMS["tn"]))
                kones = jnp.where(_fox_head_mask(kv.shape, hh), kv, jnp.ones_like(kv))
                new.append(dqs[hh] + _bdot(dsb, kones, _DIMS["nn"]))
            dk_ref[pl.ds(k0, blk), :] += jnp.where(low, dks[0], dks[1])
            dv_ref[pl.ds(k0, blk), :] += dvs[0] + dvs[1]
            dcs_ref[pl.ds(k0, blk), :] -= jnp.where(low, dks[1], dks[0])
            return tuple(new)

        init = jnp.zeros((blk, LANE), F32)
        dqs = lax.fori_loop(0, iq, lambda kb, a: block(kb, a, False), (init, init))
        dqs = block(iq, dqs, True)
        dcs_ref[pl.ds(q0, blk), :] += jnp.where(low, dqs[1], dqs[0])
        dq_ref[...] = (jnp.where(low, dqs[0], dqs[1]) * FOX_SCALE).astype(dq_ref.dtype)

    qblk = pl.BlockSpec((blk, LANE), lambda p, i: (i, p))
    full = pl.BlockSpec((seq, LANE), lambda p, i: (0, p))
    return pl.pallas_call(
        body, name=name, grid=(npair, nb),
        in_specs=[qblk,
                  pl.BlockSpec((seq, LANE), lambda p, i: (0, npair + p)),
                  pl.BlockSpec((seq, LANE), lambda p, i: (0, 2 * npair + p)),
                  pl.BlockSpec((None, 2, seq), lambda p, i: (p, 0, 0)),
                  qblk, qblk,
                  pl.BlockSpec((2, blk, LANE), lambda p, i: (p, i, 0))],
        out_specs=(qblk, full, full, full),
        out_shape=(jax.ShapeDtypeStruct((seq, FOX_WIDTH), BF16), jax.ShapeDtypeStruct((seq, FOX_WIDTH), F32),
                   jax.ShapeDtypeStruct((seq, FOX_WIDTH), F32), jax.ShapeDtypeStruct((seq, FOX_WIDTH), F32)),
        compiler_params=_cparams(("arbitrary", "arbitrary")),
    )(qkv, qkv, qkv, cum_t, att, datt, lse)


MEM_SCALE = MEM_HEAD_DIM ** -0.5


def _mem_probs(qh, kh):
    s = _bdot(qh, kh, _DIMS["nt"]) * MEM_SCALE
    p = jnp.exp(s - jnp.max(s, axis=1, keepdims=True))
    return p / jnp.sum(p, axis=1, keepdims=True)


def _mem_fwd(name, q2, kv, *, tr=512):
    seq = q2.shape[0]
    mlen = kv.shape[0]
    tr = min(tr, seq)

    def body(q_ref, kv_ref, o_ref):
        for h in range(MEM_HEADS):
            sl = slice(h * MEM_HEAD_DIM, (h + 1) * MEM_HEAD_DIM)
            sv = slice(MEM_WIDTH + h * MEM_HEAD_DIM, MEM_WIDTH + (h + 1) * MEM_HEAD_DIM)
            p = _mem_probs(q_ref[:, sl], kv_ref[:, sl])
            o_ref[:, sl] = _bdot(p, kv_ref[:, sv], _DIMS["nn"]).astype(o_ref.dtype)

    return pl.pallas_call(
        body, name=name, grid=(seq // tr,),
        in_specs=[pl.BlockSpec((tr, MEM_WIDTH), lambda i: (i, 0)), pl.BlockSpec((mlen, 2 * MEM_WIDTH), lambda i: (0, 0))],
        out_specs=pl.BlockSpec((tr, MEM_WIDTH), lambda i: (i, 0)),
        out_shape=jax.ShapeDtypeStruct((seq, MEM_WIDTH), BF16),
        compiler_params=_cparams(("parallel",)),
    )(q2, kv)


def _mem_bwd(name, q2, kv, do2, *, tr=512):
    seq = q2.shape[0]
    mlen = kv.shape[0]
    tr = min(tr, seq)

    def body(q_ref, kv_ref, do_ref, dq_ref, dkv_ref):
        i = pl.program_id(0)

        @pl.when(i == 0)
        def _():
            dkv_ref[...] = jnp.zeros_like(dkv_ref)

        for h in range(MEM_HEADS):
            sl = slice(h * MEM_HEAD_DIM, (h + 1) * MEM_HEAD_DIM)
            sv = slice(MEM_WIDTH + h * MEM_HEAD_DIM, MEM_WIDTH + (h + 1) * MEM_HEAD_DIM)
            qh = q_ref[:, sl]
            kh = kv_ref[:, sl]
            doh = do_ref[:, sl].astype(BF16)
            p = _mem_probs(qh, kh)
            dp = _bdot(doh, kv_ref[:, sv], _DIMS["nt"])
            ds = (p * (dp - jnp.sum(p * dp, axis=1, keepdims=True)) * MEM_SCALE).astype(BF16)
            dq_ref[:, sl] = _bdot(ds, kh, _DIMS["nn"]).astype(dq_ref.dtype)
            dkv_ref[:, sl] += _bdot(ds, qh, _DIMS["tn"])
            dkv_ref[:, sv] += _bdot(p, doh, _DIMS["tn"])

    row = pl.BlockSpec((tr, MEM_WIDTH), lambda i: (i, 0))
    kvs = pl.BlockSpec((mlen, 2 * MEM_WIDTH), lambda i: (0, 0))
    return pl.pallas_call(
        body, name=name, grid=(seq // tr,), in_specs=[row, kvs, row], out_specs=(row, kvs),
        out_shape=(jax.ShapeDtypeStruct((seq, MEM_WIDTH), BF16), jax.ShapeDtypeStruct((mlen, 2 * MEM_WIDTH), F32)),
        compiler_params=_cparams(("arbitrary",)),
    )(q2, kv, do2)


_HBM = pl.BlockSpec(memory_space=pl.ANY)
_HBM_ONLY = pl.BlockSpec(memory_space=pltpu.HBM)
_MESH = pl.DeviceIdType.MESH


def _mesh_place():
    x, y, c = lax.axis_index("x"), lax.axis_index("y"), lax.axis_index("c")
    other_chips = [(1 - x, y), (x, 1 - y), (1 - x, 1 - y)]
    return x, y, c, other_chips


def _gather_all(name, arrays):
    n = len(arrays)

    def body(*refs):
        ins, outs = refs[:n], refs[n:2 * n]
        send_sems, recv_sems, local_sems = refs[2 * n:]
        x, y, c, chips = _mesh_place()
        me, sibling = (x, y, c), (x, y, 1 - c)

        def slot(a, place):
            px, py, pc = place
            return outs[a].at[4 * px + 2 * py + pc]

        def copy(a, k, block, to, src=None):
            return pltpu.make_async_remote_copy(
                src_ref=slot(a, block) if src is None else src, dst_ref=slot(a, block),
                send_sem=send_sems.at[a, k], recv_sem=recv_sems.at[a, k], device_id=to, device_id_type=_MESH)

        mine = [pltpu.make_async_copy(ins[a], slot(a, me), local_sems.at[a]) for a in range(n)]
        for cp in mine:
            cp.start()
        first = []
        for a in range(n):
            first.append(copy(a, 0, me, sibling, src=ins[a]))
            first += [copy(a, 1 + j, me, (*chip, c), src=ins[a]) for j, chip in enumerate(chips)]
        for cp in first:
            cp.start()
        passed = []
        for j, chip in enumerate(chips):
            for a in range(n):
                copy(a, 1 + j, (*chip, c), me).wait_recv()
                fwd = copy(a, 4 + j, (*chip, c), sibling)
                fwd.start()
                passed.append(fwd)
        for a in range(n):
            copy(a, 0, sibling, me).wait_recv()
            for j, chip in enumerate(chips):
                copy(a, 4 + j, (*chip, 1 - c), me).wait_recv()
        for cp in first + passed:
            cp.wait_send()
        for cp in mine:
            cp.wait()

    out_shape = tuple(jax.ShapeDtypeStruct((N_DEV,) + arr.shape, arr.dtype) for arr in arrays)
    return pl.pallas_call(
        body, name=name, in_specs=[_HBM] * n, out_specs=tuple([_HBM] * n), out_shape=out_shape,
        scratch_shapes=[pltpu.SemaphoreType.DMA((n, N_DEV - 1)), pltpu.SemaphoreType.DMA((n, N_DEV - 1)),
                        pltpu.SemaphoreType.DMA((n,))],
    )(*arrays)


_SEM = pl.BlockSpec(memory_space=pltpu.SEMAPHORE)
_DATAFLOW = pltpu.SideEffectType.DATAFLOW_SIDE_EFFECTING


def _device_index():
    return (4 * lax.axis_index("x") + 2 * lax.axis_index("y") + lax.axis_index("c")).astype(jnp.int32).reshape(1)


def _place_own(name, pieces, *, stacked_src, after=None):
    n = len(pieces)
    n_in = n + (after is not None)

    def body(me_ref, *refs):
        for a in range(n):
            refs[n_in + a][...] = refs[a][...]

    def spec(shape):
        return pl.BlockSpec((None,) + tuple(shape), lambda i, me_ref: (me_ref[0],) + (0,) * len(shape))

    shapes = [p.shape[1:] if stacked_src else p.shape for p in pieces]
    if stacked_src:
        in_specs = [spec(s) for s in shapes]
    else:
        in_specs = [pl.BlockSpec(tuple(s), lambda i, me_ref, nd=len(s): (0,) * nd) for s in shapes]
    operands = list(pieces)
    if after is not None:
        in_specs.append(_HBM)
        operands.append(after)
    return pl.pallas_call(
        body, name=name,
        grid_spec=pltpu.PrefetchScalarGridSpec(num_scalar_prefetch=1, grid=(1,), in_specs=in_specs,
                                               out_specs=tuple(spec(s) for s in shapes)),
        out_shape=tuple(jax.ShapeDtypeStruct((N_DEV,) + tuple(s), p.dtype) for s, p in zip(shapes, pieces)),
        compiler_params=_cparams(("arbitrary",)),
    )(_device_index(), *operands)


def _peer_places():
    x, y, c = lax.axis_index("x"), lax.axis_index("y"), lax.axis_index("c")
    peers = []
    for k in range(N_DEV - 1):
        flip = k + 1
        px = 1 - x if flip & 4 else x
        py = 1 - y if flip & 2 else y
        pc = 1 - c if flip & 1 else c
        peers.append((px, py, pc, 4 * px + 2 * py + pc))
    return 4 * x + 2 * y + c, peers


def _direct_copy(srcs, lands, send_sems, recv_sems, a, k, me, peer, scatter):
    px, py, pc, pidx = peer
    return pltpu.make_async_remote_copy(
        src_ref=srcs[a].at[pidx] if scatter else srcs[a], dst_ref=lands[a].at[me],
        send_sem=send_sems.at[a * (N_DEV - 1) + k], recv_sem=recv_sems.at[a * (N_DEV - 1) + k],
        device_id=(px, py, pc), device_id_type=_MESH)


def _send_start(name, srcs, lands, *, scatter):
    n = len(srcs)

    def body(*refs):
        src_refs, land_refs = refs[:n], refs[n:2 * n]
        send_sems, recv_sems = refs[2 * n], refs[2 * n + 1]
        token = refs[-1]
        me, peers = _peer_places()
        for k, peer in enumerate(peers):
            for a in range(n):
                _direct_copy(src_refs, land_refs, send_sems, recv_sems, a, k, me, peer, scatter).start()
        token[...] = jnp.zeros_like(token)

    hbm_shapes = [pltpu.HBM(t.shape, t.dtype) for t in list(srcs) + list(lands)]
    outs = pl.pallas_call(
        body, name=name,
        out_shape=(pltpu.SemaphoreType.DMA((n * (N_DEV - 1),)), pltpu.SemaphoreType.DMA((n * (N_DEV - 1),)), *hbm_shapes,
                   jax.ShapeDtypeStruct((8, LANE), F32)),
        in_specs=[_HBM_ONLY] * (2 * n),
        out_specs=(_SEM, _SEM, *([_HBM_ONLY] * (2 * n)), pl.BlockSpec(memory_space=pltpu.VMEM)),
        input_output_aliases={i: 2 + i for i in range(2 * n)},
        compiler_params=pltpu.CompilerParams(has_side_effects=_DATAFLOW),
    )(*[pltpu.with_memory_space_constraint(t, pltpu.HBM) for t in list(srcs) + list(lands)])
    return outs[0], outs[1], outs[2:2 + n], outs[2 + n:2 + 2 * n], outs[-1]


def _send_wait(name, send_sems, recv_sems, srcs, lands, after, *, scatter):
    n = len(srcs)
    afters = list(after) if isinstance(after, (tuple, list)) else [after]

    def body(*refs):
        src_refs, land_refs = refs[:n], refs[n:2 * n]
        send_sems, recv_sems = refs[2 * n], refs[2 * n + 1]
        me, peers = _peer_places()
        for k, peer in enumerate(peers):
            for a in range(n):
                cp = _direct_copy(src_refs, land_refs, send_sems, recv_sems, a, k, me, peer, scatter)
                cp.wait_send()
                cp.wait_recv()

    hbm_shapes = [pltpu.HBM(t.shape, t.dtype) for t in list(srcs) + list(lands)]
    outs = pl.pallas_call(
        body, name=name, out_shape=tuple(hbm_shapes),
        in_specs=[_HBM_ONLY] * (2 * n) + [_SEM, _SEM] + [_HBM] * len(afters),
        out_specs=tuple([_HBM_ONLY] * (2 * n)),
        input_output_aliases={i: i for i in range(2 * n)},
        compiler_params=pltpu.CompilerParams(has_side_effects=_DATAFLOW),
    )(*srcs, *lands, send_sems, recv_sems, *afters)
    return outs[n:]


def _unstack_cols(name, stacked):
    n, rows, cols = stacked.shape

    def body(i_ref, o_ref):
        o_ref[...] = i_ref[...]

    return pl.pallas_call(
        body, name=name, grid=(n,), in_specs=[pl.BlockSpec((None, rows, cols), lambda k: (k, 0, 0))],
        out_specs=pl.BlockSpec((rows, cols), lambda k: (0, k)),
        out_shape=jax.ShapeDtypeStruct((rows, n * cols), stacked.dtype),
        compiler_params=_cparams(("parallel",)),
    )(stacked)


def _restack_cols(name, mat):
    rows, width = mat.shape
    cols = width // N_DEV

    def body(i_ref, o_ref):
        o_ref[...] = i_ref[...]

    return pl.pallas_call(
        body, name=name, grid=(N_DEV,), in_specs=[pl.BlockSpec((rows, cols), lambda k: (0, k))],
        out_specs=pl.BlockSpec((None, rows, cols), lambda k: (k, 0, 0)),
        out_shape=jax.ShapeDtypeStruct((N_DEV, rows, cols), mat.dtype),
        compiler_params=_cparams(("parallel",)),
    )(mat)


def _remap_pieces(runs):
    plan = {}
    for du, dc, su, sc, ln in runs:
        while ln > 0:
            lane = dc % LANE
            take = min(ln, LANE - lane)
            plan.setdefault((du, dc // LANE), []).append((su, sc, take, lane))
            dc, sc, ln = dc + take, sc + take, ln - take
    return plan


def _remap(name, srcs, src_units, runs, *, out_units, out_cols, out_dtype, tr=256):
    rows = srcs[0].shape[-2]
    tr = min(tr, rows)
    plan = _remap_pieces(runs)
    n_src = len(srcs)
    stacked_out = out_units is not None
    n_tiles = out_cols // LANE

    def body(*refs):
        o_ref = refs[n_src]

        def src_tile(unit, t):
            ai, lead = src_units[unit]
            ref = refs[ai]
            sl = slice(t * LANE, (t + 1) * LANE)
            return (ref[:, sl] if lead is None else ref[lead, :, sl]).astype(F32)

        lane = lax.broadcasted_iota(jnp.int32, (tr, LANE), 1)
        for du in range(out_units if stacked_out else 1):
            for t in range(n_tiles):
                acc = jnp.zeros((tr, LANE), F32)
                for su, sc, ln, dl in plan.get((du if stacked_out else None, t), []):
                    st, so = sc // LANE, sc % LANE
                    first = src_tile(su, st)
                    if so == dl and so + ln <= LANE:
                        piece = first
                    else:
                        second = src_tile(su, st + 1) if so + ln > LANE else first
                        both = jnp.concatenate([first, second], axis=1)
                        piece = pltpu.roll(both, (dl - so) % (2 * LANE), axis=1)[:, 0:LANE]
                    acc = piece if (dl == 0 and ln == LANE) else jnp.where(
                        jnp.logical_and(lane >= dl, lane < dl + ln), piece, acc)
                if stacked_out:
                    o_ref[du, :, t * LANE:(t + 1) * LANE] = acc.astype(o_ref.dtype)
                else:
                    o_ref[:, t * LANE:(t + 1) * LANE] = acc.astype(o_ref.dtype)

    in_specs = []
    for arr in srcs:
        if arr.ndim == 2:
            in_specs.append(pl.BlockSpec((tr, arr.shape[1]), lambda i: (i, 0)))
        else:
            in_specs.append(pl.BlockSpec((arr.shape[0], tr, arr.shape[2]), lambda i: (0, i, 0)))
    if stacked_out:
        out_spec = pl.BlockSpec((out_units, tr, out_cols), lambda i: (0, i, 0))
        out_shape = jax.ShapeDtypeStruct((out_units, rows, out_cols), out_dtype)
    else:
        out_spec = pl.BlockSpec((tr, out_cols), lambda i: (i, 0))
        out_shape = jax.ShapeDtypeStruct((rows, out_cols), out_dtype)
    return pl.pallas_call(
        body, name=name, grid=(rows // tr,), in_specs=in_specs, out_specs=out_spec, out_shape=out_shape,
        compiler_params=_cparams(("parallel",)),
    )(*srcs)


def _proj_col(c):
    if c < PROJ_GATE0:
        return c
    if c < PROJ_GATE0 + FOX_HEADS:
        return PROJ_F0 + (c - PROJ_GATE0)
    return c - FOX_HEADS


def _win_runs():
    cuts = sorted(set([0, PROJ_GATE0, PROJ_GATE0 + FOX_HEADS, IN_WIDTH] + [SHARD_IN * k for k in range(N_DEV + 1)]))
    return [(lo // SHARD_IN, lo % SHARD_IN, _proj_col(lo), hi - lo) for lo, hi in zip(cuts[:-1], cuts[1:])]


def _assemble_win(name, stacked):
    runs = [(None, pc, k, sc, ln) for k, sc, pc, ln in _win_runs()]
    return _remap(name, [stacked], [(0, k) for k in range(N_DEV)], runs,
                  out_units=None, out_cols=PROJ_WIDTH, out_dtype=BF16)


def _disassemble_dwin(name, dw):
    runs = [(k, sc, 0, pc, ln) for k, sc, pc, ln in _win_runs()]
    return _remap(name, [dw], [(0, None)], runs, out_units=N_DEV, out_cols=SHARD_IN_PAD, out_dtype=BF16)


def _concat_cols(name, parts, *, tr=512):
    rows = parts[0].shape[0]
    tr = min(tr, rows)
    widths = [p.shape[1] for p in parts]
    total = sum(widths)

    def body(*refs):
        o_ref = refs[len(parts)]
        lo = 0
        for r, w in zip(refs[:len(parts)], widths):
            o_ref[:, lo:lo + w] = r[...].astype(o_ref.dtype)
            lo += w

    return pl.pallas_call(
        body, name=name, grid=(rows // tr,),
        in_specs=[pl.BlockSpec((tr, w), lambda i: (i, 0)) for w in widths],
        out_specs=pl.BlockSpec((tr, total), lambda i: (i, 0)),
        out_shape=jax.ShapeDtypeStruct((rows, total), BF16),
        compiler_params=_cparams(("parallel",)),
    )(*parts)


FFN_BLK = FFN_HIDDEN // 2


def _ffn_col(c):
    half, r = divmod(c, FFN_HIDDEN)
    blk, r = divmod(r, FFN_BLK)
    return blk * 2 * FFN_BLK + half * FFN_BLK + r


def _assemble_wffn(name, stacked):
    runs = [(None, _ffn_col(SHARD_FFN * k), k, 0, SHARD_FFN) for k in range(N_DEV)]
    return _remap(name, [stacked], [(0, k) for k in range(N_DEV)], runs,
                  out_units=None, out_cols=2 * FFN_HIDDEN, out_dtype=BF16)


def _disassemble_dwffn(name, dw):
    runs = [(k, 0, 0, _ffn_col(SHARD_FFN * k), SHARD_FFN) for k in range(N_DEV)]
    return _remap(name, [dw], [(0, None)], runs, out_units=N_DEV, out_cols=SHARD_FFN_PAD, out_dtype=BF16)


def _ffn_in_swiglu(name, xn, w, *, tm=512):
    rows, k = xn.shape
    tm = min(tm, rows)
    nblk = FFN_HIDDEN // FFN_BLK

    def body(x_ref, w_ref, f_ref, g_ref):
        f = _bdot(x_ref[...], w_ref[...], _DIMS["nn"])
        f_ref[...] = f.astype(f_ref.dtype)
        fa = f[:, 0:FFN_BLK]
        g_ref[...] = (fa * _sigmoid(fa) * f[:, FFN_BLK:2 * FFN_BLK]).astype(g_ref.dtype)

    return pl.pallas_call(
        body, name=name, grid=(nblk, rows // tm),
        in_specs=[pl.BlockSpec((tm, k), lambda j, i: (i, 0)), pl.BlockSpec((k, 2 * FFN_BLK), lambda j, i: (0, j))],
        out_specs=(pl.BlockSpec((tm, 2 * FFN_BLK), lambda j, i: (i, j)), pl.BlockSpec((tm, FFN_BLK), lambda j, i: (i, j))),
        out_shape=(jax.ShapeDtypeStruct((rows, 2 * FFN_HIDDEN), BF16), jax.ShapeDtypeStruct((rows, FFN_HIDDEN), BF16)),
        compiler_params=_cparams(("parallel", "arbitrary")),
    )(xn, w)


def _d_ffn_out_swiglu(name, dh, w_out, f, *, tm=512):
    rows, d = dh.shape
    tm = min(tm, rows)
    nblk = FFN_HIDDEN // FFN_BLK

    def body(dh_ref, w_ref, f_ref, df_ref):
        dg = _bdot(dh_ref[...], w_ref[...], _DIMS["nt"])
        fa = f_ref[:, 0:FFN_BLK].astype(F32)
        fb = f_ref[:, FFN_BLK:2 * FFN_BLK].astype(F32)
        s = _sigmoid(fa)
        df_ref[:, 0:FFN_BLK] = (dg * fb * s * (1.0 + fa * (1.0 - s))).astype(df_ref.dtype)
        df_ref[:, FFN_BLK:2 * FFN_BLK] = (dg * fa * s).astype(df_ref.dtype)

    wide = pl.BlockSpec((tm, 2 * FFN_BLK), lambda j, i: (i, j))
    return pl.pallas_call(
        body, name=name, grid=(nblk, rows // tm),
        in_specs=[pl.BlockSpec((tm, d), lambda j, i: (i, 0)), pl.BlockSpec((FFN_BLK, d), lambda j, i: (j, 0)), wide],
        out_specs=wide, out_shape=jax.ShapeDtypeStruct((rows, 2 * FFN_HIDDEN), BF16),
        compiler_params=_cparams(("parallel", "arbitrary")),
    )(dh, w_out, f)


def _adamw(name, parts, w, m, v, *, tr=128):
    rows, cols = w.shape
    n_parts = parts.shape[0]
    tr = min(tr, rows)
    assert rows % tr == 0, (name, rows, tr)
    c1 = 1.0 - ADAM_B1 ** ADAM_STEP
    c2 = 1.0 - ADAM_B2 ** ADAM_STEP

    def body(p_ref, w_ref, m_ref, v_ref, g_ref, d_ref, nm_ref, nv_ref):
        g = p_ref[0].astype(F32)
        for s in range(1, n_parts):
            g = g + p_ref[s].astype(F32)
        m_new = ADAM_B1 * m_ref[...] + (1.0 - ADAM_B1) * g
        v_new = ADAM_B2 * v_ref[...] + (1.0 - ADAM_B2) * (g * g)
        upd = (m_new / c1) / (jnp.sqrt(v_new / c2) + ADAM_EPS) + ADAM_WD * w_ref[...]
        g_ref[...] = g
        d_ref[...] = -ADAM_LR * upd
        nm_ref[...] = m_new
        nv_ref[...] = v_new

    row = pl.BlockSpec((tr, cols), lambda i: (i, 0))
    out = jax.ShapeDtypeStruct((rows, cols), F32)
    return pl.pallas_call(
        body, name=name, grid=(rows // tr,),
        in_specs=[pl.BlockSpec((n_parts, tr, cols), lambda i: (0, i, 0)), row, row, row],
        out_specs=(row, row, row, row), out_shape=(out, out, out, out),
        compiler_params=_cparams(("parallel",)),
    )(parts, w, m, v)


_WEIGHTS = ("norm_mix", "w_in", "b_forget", "lam_re", "lam_im", "log_dt", "b_re", "b_im", "c_re", "c_im",
            "d_skip", "w_glu", "w_fox_o", "w_mix_out", "norm_mem_q", "norm_mem_kv", "w_mem_q", "w_mem_kv",
            "w_mem_o", "norm_ffn", "w_ffn_in", "w_ffn_out", "norm_final")
_SHARDED = ("w_in", "w_glu", "w_fox_o", "w_mix_out", "w_mem_q", "w_mem_kv", "w_mem_o", "w_ffn_in", "w_ffn_out")
_SMALL = tuple(n for n in _WEIGHTS if n not in _SHARDED)
_PACK_COLS = 1024


def _pack(arrays):
    flat = jnp.concatenate([a.reshape(-1).astype(F32) for a in arrays])
    rows = -(-flat.shape[0] // _PACK_COLS)
    return jnp.pad(flat, (0, rows * _PACK_COLS - flat.shape[0])).reshape(rows, _PACK_COLS)


def _unpack(buf, like):
    flat = buf.reshape(-1)
    out, pos = [], 0
    for a in like:
        out.append(flat[pos:pos + a.size].reshape(a.shape))
        pos += a.size
    return out


def _proj_fwd(name, u, win, *, tm=512):
    seq, d = u.shape
    cuts = (0, SSM_WIDTH, PROJ_GATE0, PROJ_F0, PROJ_WIDTH)
    dtypes = (F32, BF16, BF16, F32)

    def body(u_ref, w_ref, *o_refs):
        uv = u_ref[...]
        for o_ref, lo, hi in zip(o_refs, cuts[:-1], cuts[1:]):
            o_ref[...] = _bdot(uv, w_ref[:, lo:hi], _DIMS["nn"]).astype(o_ref.dtype)

    widths = [hi - lo for lo, hi in zip(cuts[:-1], cuts[1:])]
    return pl.pallas_call(
        body, name=name, grid=(seq // tm,),
        in_specs=[pl.BlockSpec((tm, d), lambda i: (i, 0)), pl.BlockSpec((d, PROJ_WIDTH), lambda i: (0, 0))],
        out_specs=tuple(pl.BlockSpec((tm, w), lambda i: (i, 0)) for w in widths),
        out_shape=tuple(jax.ShapeDtypeStruct((seq, w), t) for w, t in zip(widths, dtypes)),
        compiler_params=_cparams(("parallel",)),
    )(u, win)


def _mm(name, a, b, mode, m, n, k, out_dtype, tm=1024, tn=512, tk=1024, **kw):
    return _matmul(name, a, b, mode, m, n, k, out_dtype=out_dtype, tm=tm, tn=tn, tk=tk, **kw)


def kernel(x, mem, norm_mix, w_in, b_forget, lam_re, lam_im, log_dt, b_re, b_im, c_re, c_im, d_skip, w_glu, w_fox_o, w_mix_out, norm_mem_q, norm_mem_kv, w_mem_q, w_mem_kv, w_mem_o, norm_ffn, w_ffn_in, w_ffn_out, norm_final, loss_target, m_norm_mix, m_w_in, m_b_forget, m_lam_re, m_lam_im, m_log_dt, m_b_re, m_b_im, m_c_re, m_c_im, m_d_skip, m_w_glu, m_w_fox_o, m_w_mix_out, m_norm_mem_q, m_norm_mem_kv, m_w_mem_q, m_w_mem_kv, m_w_mem_o, m_norm_ffn, m_w_ffn_in, m_w_ffn_out, m_norm_final, v_norm_mix, v_w_in, v_b_forget, v_lam_re, v_lam_im, v_log_dt, v_b_re, v_b_im, v_c_re, v_c_im, v_d_skip, v_w_glu, v_w_fox_o, v_w_mix_out, v_norm_mem_q, v_norm_mem_kv, v_w_mem_q, v_w_mem_kv, v_w_mem_o, v_norm_ffn, v_w_ffn_in, v_w_ffn_out, v_norm_final):
    given = dict(locals())
    weights = {n: given[n] for n in _WEIGHTS}
    mom_m = {n: given["m_" + n] for n in _WEIGHTS}
    mom_v = {n: given["v_" + n] for n in _WEIGHTS}
    seq = x.shape[1]
    nc = seq // SSM_CHUNK
    d = D_MODEL
    xs, mems, tgt = x[0], mem[0], loss_target[0]

    def padcols(a, width):
        return jnp.pad(a, ((0, 0), (0, width - a.shape[1])))

    shards = [padcols(w_in[0].astype(BF16), SHARD_IN_PAD), w_glu[0].astype(BF16), w_fox_o[0].astype(BF16),
              w_mix_out[0].astype(BF16), w_mem_q[0].astype(BF16), w_mem_kv[0].astype(BF16),
              w_mem_o[0].astype(BF16), padcols(w_ffn_in[0].astype(BF16), SHARD_FFN_PAD), w_ffn_out[0].astype(BF16)]
    first = shards[:1]
    wsend, wrecv, first_thru, first_lands, wtoken = _send_start(
        "gather_w_in_start", first, _place_own("place_w_in_shard", first, stacked_src=False), scatter=False)
    rest = shards[1:]
    gsend, grecv, rest_thru, lands, gtoken = _send_start(
        "gather_rest_start", rest, _place_own("place_weight_shards", rest, stacked_src=False, after=wtoken),
        scatter=False)

    u = _rms_fwd("rms_mix", xs, norm_mix, after=gtoken)
    ssm_params = tuple(p[0] + wtoken[0, 0] for p in (lam_re, lam_im, log_dt, b_re, b_im, c_re, c_im))
    (m_c, bw_c, cm_c, a8, aseg), mats_vjp = jax.vjp(lambda *p: _ssm_mats(*p, nc), *ssm_params)
    m_b = _bd_expand("ssm_expand_m", _BD_M, m_c)
    bw_b = _bd_expand("ssm_expand_bw", _BD_BW, bw_c)
    cm_b = _bd_expand("ssm_expand_cm", _BD_CM, cm_c)
    win = _assemble_win("assemble_w_in", _send_wait(
        "gather_w_in_wait", wsend, wrecv, first_thru, first_lands, (u, m_b, bw_b, cm_b), scatter=False)[0])
    ussm, qkv, gates, fproj = _proj_fwd("proj", u, win)

    u8 = ussm.reshape(nc, SSM_CHUNK * SSM_WIDTH)
    d8 = jnp.tile(d_skip, (1, SSM_CHUNK))
    w4 = _ssm_w("ssm_w", u8, bw_b)
    sp4 = _ssm_scan("ssm_scan", w4, a8, aseg, reverse=False)
    y8 = _ssm_y("ssm_y", u8, sp4, m_b, cm_b)
    act = _ssm_post_fwd("ssm_act", y8, u8, d8).reshape(seq, SSM_WIDTH)

    bcol = jnp.pad(b_forget[0], (0, LANE - FOX_HEADS)).reshape(LANE, 1)
    cum_t = _fox_cum("fox_cum", fproj, bcol).reshape(FOX_HEADS // 2, 2, seq)
    att, lse = _fox_fwd("fox_fwd", qkv, cum_t)

    gathered = _send_wait("gather_rest_wait", gsend, grecv, rest_thru, lands, att, scatter=False)
    wglu = _unstack_cols("unstack_w_glu", gathered[0])
    wfoxo = _unstack_cols("unstack_w_fox_o", gathered[1])
    wmix = gathered[2].reshape(d, d)
    wmq = gathered[3].reshape(d, MEM_WIDTH)
    wmkv = gathered[4].reshape(d, 2 * MEM_WIDTH)
    wmo = _unstack_cols("unstack_w_mem_o", gathered[5])
    wffn_in = _assemble_wffn("assemble_w_ffn_in", gathered[6])
    wffn_out = gathered[7].reshape(FFN_HIDDEN, d)

    glu = _mm("glu", act, wglu, "nn", seq, 2 * d, SSM_WIDTH, BF16, tn=1024)
    out_b = _mm("fox_out", att, wfoxo, "nn", seq, d, FOX_WIDTH, BF16, tn=1024)

    mixin, h1 = _mix_fwd("mix_mix_out", glu, gates, out_b, wmix, xs)

    n1 = _rms_fwd("rms_mem_q", h1, norm_mem_q)
    q2 = _mm("mem_q", n1, wmq, "nn", seq, MEM_WIDTH, d, BF16)
    mn = _rms_fwd("rms_mem_kv", mems, norm_mem_kv)
    mlen = mems.shape[0]
    kv = _mm("mem_kv", mn, wmkv, "nn", mlen, 2 * MEM_WIDTH, d, BF16)
    o2 = _mem_fwd("mem_attn", q2, kv)
    h2 = _mm("mem_out", o2, wmo, "nn", seq, d, MEM_WIDTH, F32, tn=1024, add=h1)

    n2 = _rms_fwd("rms_ffn", h2, norm_ffn)
    f, g_act = _ffn_in_swiglu("ffn_in_swiglu", n2, wffn_in)
    loss_part, dh3, dg_final = _matmul_final_loss("ffn_out_final_loss", g_act, wffn_out, h2, tgt,
                                                  norm_final.reshape(1, d))

    df = _d_ffn_out_swiglu("d_ffn_out_swiglu", dh3, wffn_out, f)
    dwffn_out = _mm("d_ffn_out_w", g_act, dh3, "tn", FFN_HIDDEN, d, seq, BF16, tm=1408, tn=1024)
    dh2, dg_ffn = _matmul_rms_bwd("d_ffn_in_x_rms", df, wffn_in, 2 * FFN_HIDDEN, h2, norm_ffn, dh3, tm=512, tk=2 * FFN_HIDDEN)
    dwffn_in = _mm("d_ffn_in_w", n2, df, "tn", d, 2 * FFN_HIDDEN, seq, BF16, tn=1408)

    do2 = _mm("d_mem_out_x", dh2, wmo, "nt", seq, MEM_WIDTH, d, F32)
    dwmo = _restack_cols("restack_d_w_mem_o", _mm("d_mem_out_w", o2, dh2, "tn", MEM_WIDTH, d, seq, BF16, tn=1024))
    dq2, dkv = _mem_bwd("d_mem_attn", q2, kv, do2)
    dwmq = _mm("d_mem_q_w", n1, dq2, "tn", d, MEM_WIDTH, seq, BF16)
    dwmkv = _mm("d_mem_kv_w", mn, dkv, "tn", d, 2 * MEM_WIDTH, mlen, BF16, tn=1024)
    dmn = _mm("d_mem_kv_x", dkv, wmkv, "nt", mlen, d, 2 * MEM_WIDTH, F32)
    dg_memkv = _rms_gain_grad("d_rms_mem_kv", dmn, mems)

    early = [dwmq.reshape(N_DEV, d // N_DEV, MEM_WIDTH), dwmkv.reshape(N_DEV, d // N_DEV, 2 * MEM_WIDTH), dwmo,
             _disassemble_dwffn("split_d_w_ffn_in", dwffn_in), dwffn_out.reshape(N_DEV, FFN_HIDDEN // N_DEV, d)]
    ssend, srecv, early_thru, early_lands, stoken = _send_start(
        "scatter_early_start", early, _place_own("place_early_grads", early, stacked_src=True), scatter=True)
    dh1, dg_memq = _matmul_rms_bwd("d_mem_q_x_rms", dq2, wmq, MEM_WIDTH, h1, norm_mem_q, dh2, tm=1024, after=stoken)

    dwmix = _mm("d_mix_out_w", mixin, dh1, "tn", d, d, seq, BF16, tn=1024)
    dglu, dgates, dout_b = _mix_bwd("d_mix_out_x_mix", dh1, wmix, glu, gates, out_b)
    datt = _mm("d_fox_out_x", dout_b, wfoxo, "nt", seq, FOX_WIDTH, d, F32)
    dwfoxo = _restack_cols("restack_d_w_fox_o", _mm("d_fox_out_w", att, dout_b, "tn", FOX_WIDTH, d, seq, BF16, tn=1024))
    dact = _mm("d_glu_x", dglu, wglu, "nt", seq, SSM_WIDTH, 2 * d, F32, tk=2 * d)
    dwglu = _restack_cols("restack_d_w_glu", _mm("d_glu_w", act, dglu, "tn", SSM_WIDTH, 2 * d, seq, BF16, tn=2 * d))

    mid = [dwglu, dwfoxo, dwmix.reshape(N_DEV, d // N_DEV, d)]
    msend, mrecv, mid_thru, mid_lands, mtoken = _send_start(
        "scatter_mid_start", mid, _place_own("place_mid_grads", mid, stacked_src=True), scatter=True)

    dz8, dg_dskip = _ssm_post_bwd("d_ssm_act", dact.reshape(nc, SSM_CHUNK * SSM_WIDTH), y8, u8, d8, after=mtoken)
    ds4, dcm = _ssm_ds("d_ssm_y_state", dz8, sp4, cm_b)
    g4, da8 = _ssm_scan("d_ssm_scan", ds4, a8, aseg, reverse=True, sprev4=sp4)
    dx8, dm, dbw = _ssm_dx("d_ssm_x", dz8, g4, u8, m_b, bw_b, d8)
    dussm = dx8.reshape(seq, SSM_WIDTH)
    g_ssm = mats_vjp((_bd_reduce("ssm_reduce_dm", _BD_M, dm), _bd_reduce("ssm_reduce_dbw", _BD_BW, dbw),
                      _bd_reduce("ssm_reduce_dcm", _BD_CM, dcm), da8, jnp.zeros_like(aseg)))

    dq, dk, dv, dcs = _fox_bwd("d_fox", qkv, cum_t, att, datt, lse)
    dfproj, dbf = _fox_cum_bwd("d_fox_cum", dcs, fproj, bcol)
    dg_bforget = dbf[0:FOX_HEADS, 0].reshape(1, FOX_HEADS)

    dproj = _concat_cols("d_proj_concat", (dussm, dq, dk, dv, dgates, dfproj))
    dwin = _mm("d_proj_w", u, dproj, "tn", d, PROJ_WIDTH, seq, BF16, tn=1408)
    late = [_disassemble_dwin("split_d_w_in", dwin)]
    lsend, lrecv, late_thru, late_lands, ltoken = _send_start(
        "scatter_late_start", late, _place_own("place_late_grads", late, stacked_src=True), scatter=True)
    dx, dg_mix = _matmul_rms_bwd("d_proj_x_rms", dproj, win, PROJ_WIDTH, xs, norm_mix, dh1, tm=512, tk=PROJ_WIDTH,
                                 after=ltoken)

    early_parts = _send_wait("scatter_early_wait", ssend, srecv, early_thru, early_lands, dx, scatter=True)
    mid_parts = _send_wait("scatter_mid_wait", msend, mrecv, mid_thru, mid_lands, dx, scatter=True)
    received = dict(zip(("w_glu", "w_fox_o", "w_mix_out"), mid_parts))
    received.update(zip(("w_mem_q", "w_mem_kv", "w_mem_o", "w_ffn_in", "w_ffn_out"), early_parts))

    small_grads = dict(zip(
        _SMALL, (dg_mix, dg_bforget, g_ssm[0][None], g_ssm[1][None], g_ssm[2][None], g_ssm[3][None], g_ssm[4][None],
                 g_ssm[5][None], g_ssm[6][None], dg_dskip, dg_memq, dg_memkv, dg_ffn, dg_final.reshape(d))))
    small_like = [weights[n] for n in _SMALL]
    small_all = _gather_all("gather_small_grads", [_pack([small_grads[n] for n in _SMALL])])[0]
    pk = [_pack([src[n] for n in _SMALL]) for src in (weights, mom_m, mom_v)]
    small_out = _adamw("adamw_small", small_all, pk[0], pk[1], pk[2], tr=small_all.shape[1])
    results = [dict(zip(_SMALL, _unpack(buf, small_like))) for buf in small_out]
    tiles = {"w_in": 128, "w_glu": 128, "w_fox_o": 128, "w_mix_out": 128, "w_mem_q": 128, "w_mem_kv": 128,
             "w_mem_o": 128, "w_ffn_in": 128, "w_ffn_out": 176}
    pads = {"w_in": SHARD_IN_PAD, "w_ffn_in": SHARD_FFN_PAD}
    outs = small_out
    for name in _SHARDED[1:] + _SHARDED[:1]:
        if name == "w_in":
            received[name] = _send_wait("scatter_late_wait", lsend, lrecv, late_thru, late_lands, outs[0],
                                        scatter=True)[0]
        parts = received[name]
        w2, m2, v2 = weights[name][0], mom_m[name][0], mom_v[name][0]
        cols = w2.shape[1]
        if name in pads:
            w2, m2, v2 = (padcols(t, pads[name]) for t in (w2, m2, v2))
        outs = _adamw("adamw_" + name, parts, w2, m2, v2, tr=tiles[name])
        for res, o in zip(results, outs):
            res[name] = o[:, :cols][None]

    loss = lax.psum(loss_part[0, 0], ("x", "y", "c"))
    out = [loss, dx[None]]
    for res in results:
        out.extend(res[n] for n in _WEIGHTS)
    return tuple(out)
```
